```python
import jax, jax.numpy as jnp
from jax import lax
import numpy as np

D_MODEL = 1024
BATCH = 8
SEQ = 4096
DEPTH = 4

HEAD_DIM = 64
N_MEM = 256
MEM_HEADS = 4
CROSS_WIDTH = MEM_HEADS * HEAD_DIM
MIX_WIDTH = 12 * HEAD_DIM
ATTN_WIDTH = MIX_WIDTH + CROSS_WIDTH
EPS = 1e-6
NEG = -1e30
MLA_HEADS = 12
MLA_Q_RANK = 384
MLA_KV_RANK = 256
MLA_NOPE = 64
MLA_ROPE = 32
MLA_V = 64
MLA_QK = MLA_NOPE + MLA_ROPE
ROPE_THETA = 10000.0
Q_BLOCK = 128
MLA_IN = MLA_Q_RANK + MLA_KV_RANK + MLA_ROPE + CROSS_WIDTH
SWA_Q_HEADS = 12
SWA_KV_HEADS = 4
SWA_GROUP = SWA_Q_HEADS // SWA_KV_HEADS
WINDOW = 128
SWA_IN = (SWA_Q_HEADS + 2 * SWA_KV_HEADS) * HEAD_DIM + CROSS_WIDTH
D_FF = 4 * D_MODEL
N_MLA_LAYERS = (DEPTH + 1) // 2
N_SWA_LAYERS = DEPTH // 2

kernel_name = "hybrid_mla_swa_sink_memx_sqrelu"


def rmsnorm(x, g):
    xf = x.astype(jnp.float32)
    y = xf * lax.rsqrt(jnp.mean(xf * xf, axis=-1, keepdims=True) + EPS)
    return (y * g.astype(jnp.float32)).astype(x.dtype)


def rope(x, positions):
    r = x.shape[-1]
    half = r // 2
    inv = ROPE_THETA ** (-(jnp.arange(half, dtype=jnp.float32) * 2.0) / r)
    ang = positions.astype(jnp.float32)[..., None] * inv
    cos = jnp.cos(ang)[:, :, None, :]
    sin = jnp.sin(ang)[:, :, None, :]
    xf = x.astype(jnp.float32)
    x1, x2 = xf[..., :half], xf[..., half:]
    out = jnp.concatenate([x1 * cos - x2 * sin, x1 * sin + x2 * cos], axis=-1)
    return out.astype(x.dtype)


def alibi_slopes(n_heads):
    return 2.0 ** (-8.0 * (jnp.arange(n_heads, dtype=jnp.float32) + 1.0) / n_heads)


def causal_dense_attention(q, k, v):
    b, s, h, dk = q.shape
    dv = v.shape[-1]
    nb = s // Q_BLOCK
    scale = dk ** -0.5
    qb = q.reshape(b, nb, Q_BLOCK, h, dk).transpose(1, 0, 2, 3, 4)
    k_idx = jnp.arange(s)

    def one_block(args):
        q_blk, n = args
        t_idx = n * Q_BLOCK + jnp.arange(Q_BLOCK)
        sc = jnp.einsum('bqhd,bkhd->bhqk', q_blk, k,
                        preferred_element_type=jnp.float32) * scale
        mask = k_idx[None, :] <= t_idx[:, None]
        sc = jnp.where(mask[None, None], sc, NEG)
        p = jax.nn.softmax(sc, axis=-1).astype(v.dtype)
        return jnp.einsum('bhqk,bkhd->bqhd', p, v)

    out = lax.map(one_block, (qb, jnp.arange(nb)))
    return out.transpose(1, 0, 2, 3, 4).reshape(b, s, h, dv)


def mla_mixer(hn, positions, w_in, q_norm_g, kv_norm_g, w_uq, w_ukv):
    b, s, _ = hn.shape
    proj = hn @ w_in
    c_q = proj[..., :MLA_Q_RANK]
    c_kv = proj[..., MLA_Q_RANK:MLA_Q_RANK + MLA_KV_RANK]
    k_r = proj[..., MLA_Q_RANK + MLA_KV_RANK:MLA_Q_RANK + MLA_KV_RANK + MLA_ROPE]
    q_cross = proj[..., MLA_Q_RANK + MLA_KV_RANK + MLA_ROPE:]
    q = (rmsnorm(c_q, q_norm_g) @ w_uq).reshape(b, s, MLA_HEADS, MLA_QK)
    q = jnp.concatenate([q[..., :MLA_NOPE], rope(q[..., MLA_NOPE:], positions)], axis=-1)
    kv = (rmsnorm(c_kv, kv_norm_g) @ w_ukv).reshape(b, s, MLA_HEADS, MLA_NOPE + MLA_V)
    k_nope, v = kv[..., :MLA_NOPE], kv[..., MLA_NOPE:]
    k_rope = jnp.broadcast_to(rope(k_r[:, :, None, :], positions),
                              (b, s, MLA_HEADS, MLA_ROPE))
    k = jnp.concatenate([k_nope, k_rope], axis=-1)
    o = causal_dense_attention(q, k, v)
    return o.reshape(b, s, MLA_HEADS * MLA_V), q_cross


def _with_prev_block(a):
    pad = [(0, 0), (1, 0)] + [(0, 0)] * (a.ndim - 2)
    prev = jnp.pad(a[:, :-1], pad)
    return jnp.concatenate([prev, a], axis=2)


def swa_mixer(hn, positions, w_in, sinks):
    b, s, _ = hn.shape
    nb = s // WINDOW
    proj = hn @ w_in
    nq = SWA_Q_HEADS * HEAD_DIM
    nk = SWA_KV_HEADS * HEAD_DIM
    q = proj[..., :nq].reshape(b, nb, WINDOW, SWA_KV_HEADS, SWA_GROUP, HEAD_DIM)
    k = proj[..., nq:nq + nk].reshape(b, nb, WINDOW, SWA_KV_HEADS, HEAD_DIM)
    v = proj[..., nq + nk:nq + 2 * nk].reshape(b, nb, WINDOW, SWA_KV_HEADS, HEAD_DIM)
    q_cross = proj[..., nq + 2 * nk:]
    kk = _with_prev_block(k)
    vv = _with_prev_block(v)
    pos_q = positions.reshape(b, nb, WINDOW)
    pos_k = _with_prev_block(pos_q)
    sc = jnp.einsum('bnqhgd,bnkhd->bnhgqk', q, kk,
                    preferred_element_type=jnp.float32) * (HEAD_DIM ** -0.5)
    dist = (pos_q[..., :, None] - pos_k[..., None, :]).astype(jnp.float32)
    slopes = alibi_slopes(SWA_Q_HEADS).reshape(SWA_KV_HEADS, SWA_GROUP)
    sc = sc - slopes[None, None, :, :, None, None] * dist[:, :, None, None]
    qi = jnp.arange(WINDOW)[:, None]
    kj = jnp.arange(2 * WINDOW)[None, :]
    rel = WINDOW + qi - kj
    band = (rel >= 0) & (rel < WINDOW)
    valid = band[None] & ((jnp.arange(nb)[:, None, None] > 0) | (kj >= WINDOW)[None])
    sc = jnp.where(valid[None, :, None, None], sc, NEG)
    sink = sinks.astype(jnp.float32).reshape(SWA_KV_HEADS, SWA_GROUP)[None, None, :, :, None, None]
    sink = jnp.broadcast_to(sink, sc.shape[:-1] + (1,))
    p = jax.nn.softmax(jnp.concatenate([sc, sink], axis=-1), axis=-1)[..., :-1]
    o = jnp.einsum('bnhgqk,bnkhd->bnqhgd', p.astype(vv.dtype), vv)
    return o.reshape(b, s, SWA_Q_HEADS * HEAD_DIM), q_cross


def memory_cross_attention(q_cross, mem_n, w_mem_kv):
    b, s, _ = q_cross.shape
    kv = (mem_n @ w_mem_kv).reshape(b, N_MEM, 2, MEM_HEADS, HEAD_DIM)
    k, v = kv[:, :, 0], kv[:, :, 1]
    q = q_cross.reshape(b, s, MEM_HEADS, HEAD_DIM)
    sc = jnp.einsum('bshd,bmhd->bhsm', q, k,
                    preferred_element_type=jnp.float32) * (HEAD_DIM ** -0.5)
    p = jax.nn.softmax(sc, axis=-1).astype(v.dtype)
    return jnp.einsum('bhsm,bmhd->bshd', p, v).reshape(b, s, CROSS_WIDTH)


def squared_relu_mlp(h, w_up, w_down):
    a = jax.nn.relu(h @ w_up)
    return (a * a) @ w_down


def _fwd_setup_inputs(seed: int = 0) -> dict:
    key = jax.random.key(seed)
    ks = jax.random.split(key, 20)

    def w(k, shape, fan_in):
        return jax.random.normal(k, shape, jnp.float32) * (fan_in ** -0.5)

    def gain(k, shape):
        return 1.0 + 0.02 * jax.random.normal(k, shape, jnp.float32)

    x = jax.random.normal(ks[0], (BATCH, SEQ, D_MODEL), jnp.float32)
    mem = jax.random.normal(ks[1], (BATCH, N_MEM, D_MODEL), jnp.float32)
    offsets = jax.random.randint(ks[2], (BATCH, 1), 0, 1024, dtype=jnp.int32)
    positions = (offsets + jnp.arange(SEQ, dtype=jnp.int32)[None, :]).astype(jnp.int32)
    return {
        "x": x,
        "mem": mem,
        "positions": positions,
        "attn_norm_g": gain(ks[3], (DEPTH, D_MODEL)),
        "mlp_norm_g": gain(ks[4], (DEPTH, D_MODEL)),
        "mem_norm_g": gain(ks[5], (D_MODEL,)),
        "final_norm_g": gain(ks[6], (D_MODEL,)),
        "mla_w_in": w(ks[7], (N_MLA_LAYERS, D_MODEL, MLA_IN), D_MODEL),
        "mla_q_norm_g": gain(ks[8], (N_MLA_LAYERS, MLA_Q_RANK)),
        "mla_kv_norm_g": gain(ks[9], (N_MLA_LAYERS, MLA_KV_RANK)),
        "mla_w_uq": w(ks[10], (N_MLA_LAYERS, MLA_Q_RANK, MLA_HEADS * MLA_QK), MLA_Q_RANK),
        "mla_w_ukv": w(ks[11], (N_MLA_LAYERS, MLA_KV_RANK, MLA_HEADS * (MLA_NOPE + MLA_V)), MLA_KV_RANK),
        "swa_w_in": w(ks[12], (N_SWA_LAYERS, D_MODEL, SWA_IN), D_MODEL),
        "swa_sinks": 0.5 * jax.random.normal(ks[13], (N_SWA_LAYERS, SWA_Q_HEADS), jnp.float32),
        "w_mem_kv": w(ks[14], (DEPTH, D_MODEL, 2 * CROSS_WIDTH), D_MODEL),
        "w_o": w(ks[15], (DEPTH, ATTN_WIDTH, D_MODEL), ATTN_WIDTH),
        "mlp_w_up": w(ks[16], (DEPTH, D_MODEL, D_FF), D_MODEL),
        "mlp_w_down": w(ks[17], (DEPTH, D_FF, D_MODEL), D_FF),
    }


def _fwd_reference(x, mem, positions, attn_norm_g, mlp_norm_g, mem_norm_g, final_norm_g,
              mla_w_in, mla_q_norm_g, mla_kv_norm_g, mla_w_uq, mla_w_ukv,
              swa_w_in, swa_sinks, w_mem_kv, w_o, mlp_w_up, mlp_w_down):
    mem_n = rmsnorm(mem, mem_norm_g)
    for i in range(DEPTH):
        j = i // 2
        hn = rmsnorm(x, attn_norm_g[i])
        if i % 2 == 0:
            mix, q_cross = mla_mixer(hn, positions, mla_w_in[j], mla_q_norm_g[j],
                                     mla_kv_norm_g[j], mla_w_uq[j], mla_w_ukv[j])
        else:
            mix, q_cross = swa_mixer(hn, positions, swa_w_in[j], swa_sinks[j])
        cross = memory_cross_attention(q_cross, mem_n, w_mem_kv[i])
        x = x + jnp.concatenate([mix, cross], axis=-1) @ w_o[i]
        x = x + squared_relu_mlp(rmsnorm(x, mlp_norm_g[i]), mlp_w_up[i], mlp_w_down[i])
    return rmsnorm(x, final_norm_g)


import jax as _jax
import jax.numpy as _jnp

TWIN_FORMAT = 'train_step'
FWD_PARAMS = ['x', 'mem', 'positions', 'attn_norm_g', 'mlp_norm_g', 'mem_norm_g', 'final_norm_g', 'mla_w_in', 'mla_q_norm_g', 'mla_kv_norm_g', 'mla_w_uq', 'mla_w_ukv', 'swa_w_in', 'swa_sinks', 'w_mem_kv', 'w_o', 'mlp_w_up', 'mlp_w_down']
TWIN_WEIGHTS = ['attn_norm_g', 'mlp_norm_g', 'mem_norm_g', 'final_norm_g', 'mla_w_in', 'mla_q_norm_g', 'mla_kv_norm_g', 'mla_w_uq', 'mla_w_ukv', 'swa_w_in', 'swa_sinks', 'w_mem_kv', 'w_o', 'mlp_w_up', 'mlp_w_down']
TWIN_DIFF_INPUT = 'x'
TWIN_INPUTS = ['x', 'mem', 'positions', 'attn_norm_g', 'mlp_norm_g', 'mem_norm_g', 'final_norm_g', 'mla_w_in', 'mla_q_norm_g', 'mla_kv_norm_g', 'mla_w_uq', 'mla_w_ukv', 'swa_w_in', 'swa_sinks', 'w_mem_kv', 'w_o', 'mlp_w_up', 'mlp_w_down', 'loss_target', 'm_attn_norm_g', 'm_mlp_norm_g', 'm_mem_norm_g', 'm_final_norm_g', 'm_mla_w_in', 'm_mla_q_norm_g', 'm_mla_kv_norm_g', 'm_mla_w_uq', 'm_mla_w_ukv', 'm_swa_w_in', 'm_swa_sinks', 'm_w_mem_kv', 'm_w_o', 'm_mlp_w_up', 'm_mlp_w_down', 'v_attn_norm_g', 'v_mlp_norm_g', 'v_mem_norm_g', 'v_final_norm_g', 'v_mla_w_in', 'v_mla_q_norm_g', 'v_mla_kv_norm_g', 'v_mla_w_uq', 'v_mla_w_ukv', 'v_swa_w_in', 'v_swa_sinks', 'v_w_mem_kv', 'v_w_o', 'v_mlp_w_up', 'v_mlp_w_down']
TWIN_OUTPUTS = ['loss', 'grad_x', 'grad_attn_norm_g', 'grad_mlp_norm_g', 'grad_mem_norm_g', 'grad_final_norm_g', 'grad_mla_w_in', 'grad_mla_q_norm_g', 'grad_mla_kv_norm_g', 'grad_mla_w_uq', 'grad_mla_w_ukv', 'grad_swa_w_in', 'grad_swa_sinks', 'grad_w_mem_kv', 'grad_w_o', 'grad_mlp_w_up', 'grad_mlp_w_down', 'delta_attn_norm_g', 'delta_mlp_norm_g', 'delta_mem_norm_g', 'delta_final_norm_g', 'delta_mla_w_in', 'delta_mla_q_norm_g', 'delta_mla_kv_norm_g', 'delta_mla_w_uq', 'delta_mla_w_ukv', 'delta_swa_w_in', 'delta_swa_sinks', 'delta_w_mem_kv', 'delta_w_o', 'delta_mlp_w_up', 'delta_mlp_w_down', 'new_m_attn_norm_g', 'new_m_mlp_norm_g', 'new_m_mem_norm_g', 'new_m_final_norm_g', 'new_m_mla_w_in', 'new_m_mla_q_norm_g', 'new_m_mla_kv_norm_g', 'new_m_mla_w_uq', 'new_m_mla_w_ukv', 'new_m_swa_w_in', 'new_m_swa_sinks', 'new_m_w_mem_kv', 'new_m_w_o', 'new_m_mlp_w_up', 'new_m_mlp_w_down', 'new_v_attn_norm_g', 'new_v_mlp_norm_g', 'new_v_mem_norm_g', 'new_v_final_norm_g', 'new_v_mla_w_in', 'new_v_mla_q_norm_g', 'new_v_mla_kv_norm_g', 'new_v_mla_w_uq', 'new_v_mla_w_ukv', 'new_v_swa_w_in', 'new_v_swa_sinks', 'new_v_w_mem_kv', 'new_v_w_o', 'new_v_mlp_w_up', 'new_v_mlp_w_down']
TWIN_LEAF_KINDS = {'loss': 'loss', 'grad_x': 'grad_x', 'grad_attn_norm_g': 'grad_w', 'grad_mlp_norm_g': 'grad_w', 'grad_mem_norm_g': 'grad_w', 'grad_final_norm_g': 'grad_w', 'grad_mla_w_in': 'grad_w', 'grad_mla_q_norm_g': 'grad_w', 'grad_mla_kv_norm_g': 'grad_w', 'grad_mla_w_uq': 'grad_w', 'grad_mla_w_ukv': 'grad_w', 'grad_swa_w_in': 'grad_w', 'grad_swa_sinks': 'grad_w', 'grad_w_mem_kv': 'grad_w', 'grad_w_o': 'grad_w', 'grad_mlp_w_up': 'grad_w', 'grad_mlp_w_down': 'grad_w', 'delta_attn_norm_g': 'delta_w', 'delta_mlp_norm_g': 'delta_w', 'delta_mem_norm_g': 'delta_w', 'delta_final_norm_g': 'delta_w', 'delta_mla_w_in': 'delta_w', 'delta_mla_q_norm_g': 'delta_w', 'delta_mla_kv_norm_g': 'delta_w', 'delta_mla_w_uq': 'delta_w', 'delta_mla_w_ukv': 'delta_w', 'delta_swa_w_in': 'delta_w', 'delta_swa_sinks': 'delta_w', 'delta_w_mem_kv': 'delta_w', 'delta_w_o': 'delta_w', 'delta_mlp_w_up': 'delta_w', 'delta_mlp_w_down': 'delta_w', 'new_m_attn_norm_g': 'new_m', 'new_m_mlp_norm_g': 'new_m', 'new_m_mem_norm_g': 'new_m', 'new_m_final_norm_g': 'new_m', 'new_m_mla_w_in': 'new_m', 'new_m_mla_q_norm_g': 'new_m', 'new_m_mla_kv_norm_g': 'new_m', 'new_m_mla_w_uq': 'new_m', 'new_m_mla_w_ukv': 'new_m', 'new_m_swa_w_in': 'new_m', 'new_m_swa_sinks': 'new_m', 'new_m_w_mem_kv': 'new_m', 'new_m_w_o': 'new_m', 'new_m_mlp_w_up': 'new_m', 'new_m_mlp_w_down': 'new_m', 'new_v_attn_norm_g': 'new_v', 'new_v_mlp_norm_g': 'new_v', 'new_v_mem_norm_g': 'new_v', 'new_v_final_norm_g': 'new_v', 'new_v_mla_w_in': 'new_v', 'new_v_mla_q_norm_g': 'new_v', 'new_v_mla_kv_norm_g': 'new_v', 'new_v_mla_w_uq': 'new_v', 'new_v_mla_w_ukv': 'new_v', 'new_v_swa_w_in': 'new_v', 'new_v_swa_sinks': 'new_v', 'new_v_w_mem_kv': 'new_v', 'new_v_w_o': 'new_v', 'new_v_mlp_w_up': 'new_v', 'new_v_mlp_w_down': 'new_v'}


def _forward(args):
    return _fwd_reference(*[args[k] for k in FWD_PARAMS])


def _output_shape():
    def fwd():
        inp = _fwd_setup_inputs(0)
        return _fwd_reference(*[inp[k] for k in FWD_PARAMS])
    out = _jax.eval_shape(fwd)
    return out.shape, out.dtype

N_MICROBATCH = 1
ADAM_LR = 0.001
ADAM_B1 = 0.9
ADAM_B2 = 0.999
ADAM_EPS = 1e-08
ADAM_WD = 0.01
ADAM_STEP = 10
PER_EXAMPLE_BATCH_AXIS = {'x': 0, 'mem': 0, 'positions': 0, 'loss_target': 0}
SHARED_INPUTS = []
_WEIGHT_DTYPES = {'attn_norm_g': _jnp.float32, 'mlp_norm_g': _jnp.float32, 'mem_norm_g': _jnp.float32, 'final_norm_g': _jnp.float32, 'mla_w_in': _jnp.float32, 'mla_q_norm_g': _jnp.float32, 'mla_kv_norm_g': _jnp.float32, 'mla_w_uq': _jnp.float32, 'mla_w_ukv': _jnp.float32, 'swa_w_in': _jnp.float32, 'swa_sinks': _jnp.float32, 'w_mem_kv': _jnp.float32, 'w_o': _jnp.float32, 'mlp_w_up': _jnp.float32, 'mlp_w_down': _jnp.float32}
MOMENT_SCALE = {'attn_norm_g': 5.623953e-02, 'mlp_norm_g': 1.416274e-01, 'mem_norm_g': 2.367007e-02, 'final_norm_g': 3.341797e+01, 'mla_w_in': 5.956922e-02, 'mla_q_norm_g': 4.038464e-02, 'mla_kv_norm_g': 9.292503e-02, 'mla_w_uq': 2.355186e-02, 'mla_w_ukv': 3.901256e-02, 'swa_w_in': 6.054887e-02, 'swa_sinks': 8.047688e-02, 'w_mem_kv': 1.551461e-02, 'w_o': 6.062398e-02, 'mlp_w_up': 7.156789e-02, 'mlp_w_down': 1.660938e-01}


def _to_microbatches(a, axis):
    t = _jnp.moveaxis(a, axis, 0)
    t = t.reshape((N_MICROBATCH, t.shape[0] // N_MICROBATCH) + t.shape[1:])
    return _jnp.moveaxis(t, 1, axis + 1)


def setup_inputs(seed: int = 0) -> dict:
    inp = _fwd_setup_inputs(seed)
    key = _jax.random.fold_in(_jax.random.key(seed), 7919)
    shape, _ = _output_shape()
    out = dict(inp)
    out["loss_target"] = _jax.random.normal(_jax.random.fold_in(key, 0), shape, _jnp.float32)
    for i, name in enumerate(TWIN_WEIGHTS):
        w = inp[name].astype(_jnp.float32)
        if MOMENT_SCALE is None:
            s = _jnp.sqrt(_jnp.mean(_jnp.square(w)) + 1e-30)
        else:
            s = MOMENT_SCALE[name]
        km, kv = _jax.random.split(_jax.random.fold_in(key, i + 1))
        out[name] = w
        out["m_" + name] = s * _jax.random.normal(km, w.shape, _jnp.float32)
        out["v_" + name] = (s * s) * _jax.random.uniform(kv, w.shape, _jnp.float32, 0.5, 1.5)
    if N_MICROBATCH > 1:
        for name, axis in PER_EXAMPLE_BATCH_AXIS.items():
            out[name] = _to_microbatches(out[name], axis)
    return {'x': out['x'], 'mem': out['mem'], 'positions': out['positions'], 'attn_norm_g': out['attn_norm_g'], 'mlp_norm_g': out['mlp_norm_g'], 'mem_norm_g': out['mem_norm_g'], 'final_norm_g': out['final_norm_g'], 'mla_w_in': out['mla_w_in'], 'mla_q_norm_g': out['mla_q_norm_g'], 'mla_kv_norm_g': out['mla_kv_norm_g'], 'mla_w_uq': out['mla_w_uq'], 'mla_w_ukv': out['mla_w_ukv'], 'swa_w_in': out['swa_w_in'], 'swa_sinks': out['swa_sinks'], 'w_mem_kv': out['w_mem_kv'], 'w_o': out['w_o'], 'mlp_w_up': out['mlp_w_up'], 'mlp_w_down': out['mlp_w_down'], 'loss_target': out['loss_target'], 'm_attn_norm_g': out['m_attn_norm_g'], 'm_mlp_norm_g': out['m_mlp_norm_g'], 'm_mem_norm_g': out['m_mem_norm_g'], 'm_final_norm_g': out['m_final_norm_g'], 'm_mla_w_in': out['m_mla_w_in'], 'm_mla_q_norm_g': out['m_mla_q_norm_g'], 'm_mla_kv_norm_g': out['m_mla_kv_norm_g'], 'm_mla_w_uq': out['m_mla_w_uq'], 'm_mla_w_ukv': out['m_mla_w_ukv'], 'm_swa_w_in': out['m_swa_w_in'], 'm_swa_sinks': out['m_swa_sinks'], 'm_w_mem_kv': out['m_w_mem_kv'], 'm_w_o': out['m_w_o'], 'm_mlp_w_up': out['m_mlp_w_up'], 'm_mlp_w_down': out['m_mlp_w_down'], 'v_attn_norm_g': out['v_attn_norm_g'], 'v_mlp_norm_g': out['v_mlp_norm_g'], 'v_mem_norm_g': out['v_mem_norm_g'], 'v_final_norm_g': out['v_final_norm_g'], 'v_mla_w_in': out['v_mla_w_in'], 'v_mla_q_norm_g': out['v_mla_q_norm_g'], 'v_mla_kv_norm_g': out['v_mla_kv_norm_g'], 'v_mla_w_uq': out['v_mla_w_uq'], 'v_mla_w_ukv': out['v_mla_w_ukv'], 'v_swa_w_in': out['v_swa_w_in'], 'v_swa_sinks': out['v_swa_sinks'], 'v_w_mem_kv': out['v_w_mem_kv'], 'v_w_o': out['v_w_o'], 'v_mlp_w_up': out['v_mlp_w_up'], 'v_mlp_w_down': out['v_mlp_w_down']}


def _loss(weights, diff, rest, loss_target):
    with _jax.named_scope("forward"):
        args = {**rest, TWIN_DIFF_INPUT: diff, **{k: w.astype(_WEIGHT_DTYPES[k]) for k, w in weights.items()}}
        y = _forward(args)
    with _jax.named_scope("loss_head"):
        err = _jnp.square(y.astype(_jnp.float32) - loss_target)
        return 0.5 * _jnp.sum(_jnp.mean(err, axis=-1)) if err.ndim else 0.5 * err


def _adamw(w, g, m, v):
    m = ADAM_B1 * m + (1.0 - ADAM_B1) * g
    v = ADAM_B2 * v + (1.0 - ADAM_B2) * _jnp.square(g)
    m_hat = m / (1.0 - ADAM_B1 ** ADAM_STEP)
    v_hat = v / (1.0 - ADAM_B2 ** ADAM_STEP)
    delta = -ADAM_LR * (m_hat / (_jnp.sqrt(v_hat) + ADAM_EPS) + ADAM_WD * w)
    return delta, m, v


def reference(x, mem, positions, attn_norm_g, mlp_norm_g, mem_norm_g, final_norm_g, mla_w_in, mla_q_norm_g, mla_kv_norm_g, mla_w_uq, mla_w_ukv, swa_w_in, swa_sinks, w_mem_kv, w_o, mlp_w_up, mlp_w_down, loss_target, m_attn_norm_g, m_mlp_norm_g, m_mem_norm_g, m_final_norm_g, m_mla_w_in, m_mla_q_norm_g, m_mla_kv_norm_g, m_mla_w_uq, m_mla_w_ukv, m_swa_w_in, m_swa_sinks, m_w_mem_kv, m_w_o, m_mlp_w_up, m_mlp_w_down, v_attn_norm_g, v_mlp_norm_g, v_mem_norm_g, v_final_norm_g, v_mla_w_in, v_mla_q_norm_g, v_mla_kv_norm_g, v_mla_w_uq, v_mla_w_ukv, v_swa_w_in, v_swa_sinks, v_w_mem_kv, v_w_o, v_mlp_w_up, v_mlp_w_down):
    given = dict(x=x, mem=mem, positions=positions, attn_norm_g=attn_norm_g, mlp_norm_g=mlp_norm_g, mem_norm_g=mem_norm_g, final_norm_g=final_norm_g, mla_w_in=mla_w_in, mla_q_norm_g=mla_q_norm_g, mla_kv_norm_g=mla_kv_norm_g, mla_w_uq=mla_w_uq, mla_w_ukv=mla_w_ukv, swa_w_in=swa_w_in, swa_sinks=swa_sinks, w_mem_kv=w_mem_kv, w_o=w_o, mlp_w_up=mlp_w_up, mlp_w_down=mlp_w_down, loss_target=loss_target, m_attn_norm_g=m_attn_norm_g, m_mlp_norm_g=m_mlp_norm_g, m_mem_norm_g=m_mem_norm_g, m_final_norm_g=m_final_norm_g, m_mla_w_in=m_mla_w_in, m_mla_q_norm_g=m_mla_q_norm_g, m_mla_kv_norm_g=m_mla_kv_norm_g, m_mla_w_uq=m_mla_w_uq, m_mla_w_ukv=m_mla_w_ukv, m_swa_w_in=m_swa_w_in, m_swa_sinks=m_swa_sinks, m_w_mem_kv=m_w_mem_kv, m_w_o=m_w_o, m_mlp_w_up=m_mlp_w_up, m_mlp_w_down=m_mlp_w_down, v_attn_norm_g=v_attn_norm_g, v_mlp_norm_g=v_mlp_norm_g, v_mem_norm_g=v_mem_norm_g, v_final_norm_g=v_final_norm_g, v_mla_w_in=v_mla_w_in, v_mla_q_norm_g=v_mla_q_norm_g, v_mla_kv_norm_g=v_mla_kv_norm_g, v_mla_w_uq=v_mla_w_uq, v_mla_w_ukv=v_mla_w_ukv, v_swa_w_in=v_swa_w_in, v_swa_sinks=v_swa_sinks, v_w_mem_kv=v_w_mem_kv, v_w_o=v_w_o, v_mlp_w_up=v_mlp_w_up, v_mlp_w_down=v_mlp_w_down)
    weights = {n: given[n] for n in TWIN_WEIGHTS}
    shared = {n: given[n] for n in SHARED_INPUTS}
    per_example = {n: given[n] for n in ['x', 'mem', 'positions']}
    grad_fn = _jax.value_and_grad(_loss, argnums=(0, 1))

    def one_microbatch(ex, loss_target):
        ex = dict(ex)
        diff = ex.pop(TWIN_DIFF_INPUT)
        return grad_fn(weights, diff, {**shared, **ex}, loss_target)

    if N_MICROBATCH == 1:
        loss, (grad_w, grad_x) = one_microbatch(per_example, given["loss_target"])
    else:
        def body(carry, xs):
            loss_sum, grad_sum = carry
            l_k, (gw_k, gx_k) = one_microbatch(xs[0], xs[1])
            with _jax.named_scope("update"):
                return (loss_sum + l_k, _jax.tree.map(_jnp.add, grad_sum, gw_k)), gx_k

        init = (_jnp.zeros((), _jnp.float32), _jax.tree.map(_jnp.zeros_like, weights))
        (loss, grad_w), grad_x = _jax.lax.scan(body, init, (per_example, given["loss_target"]))
    with _jax.named_scope("update"):
        delta_w, new_m, new_v = {}, {}, {}
        for n in TWIN_WEIGHTS:
            delta_w[n], new_m[n], new_v[n] = _adamw(weights[n], grad_w[n], given["m_" + n], given["v_" + n])
    return (loss, grad_x, *[grad_w[n] for n in TWIN_WEIGHTS], *[delta_w[n] for n in TWIN_WEIGHTS],
            *[new_m[n] for n in TWIN_WEIGHTS], *[new_v[n] for n in TWIN_WEIGHTS])
```

```python
import functools

import jax
import jax.numpy as jnp
from jax import lax
from jax.experimental import pallas as pl
from jax.experimental.pallas import tpu as pltpu

F32 = jnp.float32
BF16 = jnp.bfloat16

D_MODEL = 1024
D_FF = 4096
N_MEM = 256
DEPTH = 4
SLOT = 128
HEAD_DIM = 64
MLA_HEADS = 12
MLA_QK = 96
MLA_Q_RANK = 384
MLA_KV_RANK = 256
SWA_HEADS = 12
SWA_KV_HEADS = 4
SWA_GROUP = 3
MEM_HEADS = 4
WINDOW = 128
EPS = 1e-6
NEG = -1e30
ROPE_THETA = 10000.0
N_DEV = 8

ADAM_LR = 0.001
ADAM_B1 = 0.9
ADAM_B2 = 0.999
ADAM_EPS = 1e-08
ADAM_WD = 0.01
ADAM_STEP = 10

TM = 512
TQ_MLA = 512
TQ_SWA = 256
TQ_CROSS = 512
TN_CAP = 1024
TK_CAP = 1024
PACK_LANES = 1024
ADAM_ROWS = 128
VMEM_LIMIT = 48 * 1024 * 1024

MESH_AXES = ("x", "y", "c")

MLA_PAD_IN = 384 + SLOT + 256 + MEM_HEADS * SLOT
MLA_QOFF = (384 + SLOT + 256) // SLOT
SWA_PAD_IN = (SWA_HEADS + 2 * SWA_KV_HEADS + MEM_HEADS) * SLOT
SWA_QOFF = SWA_HEADS + 2 * SWA_KV_HEADS

SHARDED = (
    ("mla_w_in", 1), ("mla_w_uq", 2), ("mla_w_ukv", 2), ("swa_w_in", 2),
    ("w_mem_kv", 1), ("w_o", 1), ("mlp_w_up", 2), ("mlp_w_down", 1),
)
REPLICATED = ("attn_norm_g", "mlp_norm_g", "mem_norm_g", "final_norm_g",
              "mla_q_norm_g", "mla_kv_norm_g", "swa_sinks")
WEIGHT_ORDER = ("attn_norm_g", "mlp_norm_g", "mem_norm_g", "final_norm_g", "mla_w_in",
                "mla_q_norm_g", "mla_kv_norm_g", "mla_w_uq", "mla_w_ukv", "swa_w_in",
                "swa_sinks", "w_mem_kv", "w_o", "mlp_w_up", "mlp_w_down")


def _cparams():
    return pltpu.CompilerParams(vmem_limit_bytes=VMEM_LIMIT)


def _tile(n, cap):
    best = None
    t = SLOT
    while t <= min(n, cap):
        if n % t == 0:
            best = t
        t += SLOT
    return n if best is None else best


_DIMS = {"nn": (((1,), (0,)), ((), ())), "nt": (((1,), (1,)), ((), ())), "tn": (((0,), (0,)), ((), ()))}


def _mm(a, b, mode, out_dtype, name, res=None, aux=None, epi=None):
    if mode == "nn":
        (m, k), (k2, n) = a.shape, b.shape
    elif mode == "nt":
        (m, k), (n, k2) = a.shape, b.shape
    else:
        (k, m), (k2, n) = a.shape, b.shape
    assert k == k2, (a.shape, b.shape, mode)
    tm, tn, tk = _tile(m, TM), _tile(n, TN_CAP), _tile(k, TK_CAP)
    nk = k // tk
    dims = _DIMS[mode]
    if mode == "tn":
        a_spec = pl.BlockSpec((tk, tm), lambda i, j, kk: (kk, i))
    else:
        a_spec = pl.BlockSpec((tm, tk), lambda i, j, kk: (i, kk))
    if mode == "nt":
        b_spec = pl.BlockSpec((tn, tk), lambda i, j, kk: (j, kk))
    else:
        b_spec = pl.BlockSpec((tk, tn), lambda i, j, kk: (kk, j))
    o_spec = pl.BlockSpec((tm, tn), lambda i, j, kk: (i, j))
    has_res, has_aux = res is not None, aux is not None

    def body(*refs):
        a_ref, b_ref = refs[0], refs[1]
        pos = 2
        res_ref = aux_ref = None
        if has_res:
            res_ref = refs[pos]
            pos += 1
        if has_aux:
            aux_ref = refs[pos]
            pos += 1
        outs = refs[pos:-1]
        acc = refs[-1]
        kk = pl.program_id(2)

        @pl.when(kk == 0)
        def _():
            acc[...] = jnp.zeros_like(acc)

        acc[...] += lax.dot_general(a_ref[...].astype(BF16), b_ref[...].astype(BF16), dims,
                                    preferred_element_type=F32)

        @pl.when(kk == nk - 1)
        def _():
            r = acc[...]
            if epi == "relu2":
                r = jnp.maximum(r, 0.0)
                outs[0][...] = r.astype(outs[0].dtype)
                outs[1][...] = (r * r).astype(outs[1].dtype)
            else:
                if epi == "mul2aux":
                    r = r * (2.0 * aux_ref[...].astype(F32))
                if has_res:
                    r = r + res_ref[...]
                outs[0][...] = r.astype(outs[0].dtype)

    in_specs = [a_spec, b_spec]
    args = [a, b]
    if has_res:
        in_specs.append(o_spec)
        args.append(res)
    if has_aux:
        in_specs.append(o_spec)
        args.append(aux)
    n_out = 2 if epi == "relu2" else 1
    out_shape = [jax.ShapeDtypeStruct((m, n), out_dtype)] * n_out
    out = pl.pallas_call(
        body, name=name, grid=(m // tm, n // tn, nk),
        in_specs=in_specs, out_specs=[o_spec] * n_out, out_shape=out_shape,
        scratch_shapes=[pltpu.VMEM((tm, tn), F32)], compiler_params=_cparams(),
    )(*args)
    return out if n_out == 2 else out[0]


def _rmsnorm_fwd(xarr, colblk, width, g, name):
    rows = xarr.shape[0]
    tm = min(TM, rows)

    def body(x_ref, g_ref, y_ref):
        x = x_ref[...].astype(F32)
        r = lax.rsqrt(jnp.mean(x * x, axis=1, keepdims=True) + EPS)
        y_ref[...] = (x * r * g_ref[...]).astype(y_ref.dtype)

    return pl.pallas_call(
        body, name=name, grid=(rows // tm,),
        in_specs=[pl.BlockSpec((tm, width), lambda i: (i, colblk)), pl.BlockSpec((1, width), lambda i: (0, 0))],
        out_specs=pl.BlockSpec((tm, width), lambda i: (i, 0)),
        out_shape=jax.ShapeDtypeStruct((rows, width), BF16), compiler_params=_cparams(),
    )(xarr, g.reshape(1, width))


def _rmsnorm_bwd(xarr, colblk, width, g, dy, dres, out_dtype, name):
    rows = xarr.shape[0]
    tm = min(TM, rows)
    has_res = dres is not None

    def body(*refs):
        x_ref, g_ref, dy_ref = refs[0], refs[1], refs[2]
        dres_ref = refs[3] if has_res else None
        dx_ref, dg_ref = refs[-2], refs[-1]
        x = x_ref[...].astype(F32)
        dyv = dy_ref[...].astype(F32)
        r = lax.rsqrt(jnp.mean(x * x, axis=1, keepdims=True) + EPS)
        xh = x * r
        dxh = dyv * g_ref[...]
        dx = r * (dxh - xh * jnp.mean(dxh * xh, axis=1, keepdims=True))
        if has_res:
            dx = dx + dres_ref[...]
        dx_ref[...] = dx.astype(dx_ref.dtype)

        @pl.when(pl.program_id(0) == 0)
        def _():
            dg_ref[...] = jnp.zeros_like(dg_ref)

        dg_ref[...] += jnp.sum(dyv * xh, axis=0, keepdims=True)

    row_spec = pl.BlockSpec((tm, width), lambda i: (i, 0))
    vec_spec = pl.BlockSpec((1, width), lambda i: (0, 0))
    in_specs = [pl.BlockSpec((tm, width), lambda i: (i, colblk)), vec_spec, row_spec]
    args = [xarr, g.reshape(1, width), dy]
    if has_res:
        in_specs.append(row_spec)
        args.append(dres)
    return pl.pallas_call(
        body, name=name, grid=(rows // tm,), in_specs=in_specs, out_specs=[row_spec, vec_spec],
        out_shape=[jax.ShapeDtypeStruct((rows, width), out_dtype), jax.ShapeDtypeStruct((1, width), F32)],
        compiler_params=_cparams(),
    )(*args)


def _loss_head(x, g, tgt):
    rows, width = x.shape
    tm = min(TM, rows)

    def body(x_ref, g_ref, t_ref, dx_ref, dg_ref, loss_ref):
        xv = x_ref[...]
        gv = g_ref[...]
        r = lax.rsqrt(jnp.mean(xv * xv, axis=1, keepdims=True) + EPS)
        xh = xv * r
        err = xh * gv - t_ref[...]
        part = 0.5 * jnp.sum(jnp.mean(err * err, axis=1, keepdims=True), axis=0, keepdims=True)
        dyv = err * (1.0 / width)
        dxh = dyv * gv
        dx_ref[...] = r * (dxh - xh * jnp.mean(dxh * xh, axis=1, keepdims=True))

        @pl.when(pl.program_id(0) == 0)
        def _():
            dg_ref[...] = jnp.zeros_like(dg_ref)
            loss_ref[...] = jnp.zeros_like(loss_ref)

        dg_ref[...] += jnp.sum(dyv * xh, axis=0, keepdims=True)
        loss_ref[...] += jnp.broadcast_to(part, loss_ref.shape)

    row_spec = pl.BlockSpec((tm, width), lambda i: (i, 0))
    vec_spec = pl.BlockSpec((1, width), lambda i: (0, 0))
    return pl.pallas_call(
        body, name="loss_head", grid=(rows // tm,), in_specs=[row_spec, vec_spec, row_spec],
        out_specs=[row_spec, vec_spec, pl.BlockSpec((1, SLOT), lambda i: (0, 0))],
        out_shape=[jax.ShapeDtypeStruct((rows, width), F32), jax.ShapeDtypeStruct((1, width), F32),
                   jax.ShapeDtypeStruct((1, SLOT), F32)],
        compiler_params=_cparams(),
    )(x, g.reshape(1, width), tgt)


def _lane_consts():
    half = 16
    inv = ROPE_THETA ** (-(jnp.arange(half, dtype=F32) * 2.0) / 32)
    lane = jnp.arange(SLOT)
    first = (lane >= 64) & (lane < 80)
    second = (lane >= 80) & (lane < 96)
    inv_lane = jnp.where(first | second, inv[(lane - 64) % half], 0.0)
    rows = [inv_lane, (lane < 64).astype(F32), first.astype(F32), second.astype(F32)]
    rows += [jnp.zeros((SLOT,), F32)] * 4
    return jnp.stack(rows).astype(F32)


def _rope_tables(pos_col, consts):
    rows = pos_col.shape[0]
    tm = min(TM, rows)

    def body(p_ref, k_ref, c_ref, s1_ref, s2_ref):
        ang = p_ref[...] * k_ref[0:1, :]
        cos, sin = jnp.cos(ang), jnp.sin(ang)
        first, second = k_ref[2:3, :], k_ref[3:4, :]
        c_ref[...] = k_ref[1:2, :] + (first + second) * cos
        s1_ref[...] = -first * sin
        s2_ref[...] = second * sin

    spec = pl.BlockSpec((tm, SLOT), lambda i: (i, 0))
    shp = jax.ShapeDtypeStruct((rows, SLOT), F32)
    return pl.pallas_call(
        body, name="rope_tables", grid=(rows // tm,),
        in_specs=[pl.BlockSpec((tm, 1), lambda i: (i, 0)), pl.BlockSpec((8, SLOT), lambda i: (0, 0))],
        out_specs=[spec, spec, spec], out_shape=[shp, shp, shp], compiler_params=_cparams(),
    )(pos_col, consts)


def _rot(xv, c, s1, s2):
    return xv * c + pltpu.roll(xv, SLOT - 16, 1) * s1 + pltpu.roll(xv, 16, 1) * s2


def _rot_t(dy, c, s1, s2):
    return dy * c + pltpu.roll(dy * s1, 16, 1) + pltpu.roll(dy * s2, SLOT - 16, 1)


def _mla_rope_fwd(qraw, kvraw, proj, tabs):
    rows = qraw.shape[0]
    tm = min(256, rows)
    hw = MLA_HEADS * SLOT

    def body(q_ref, kv_ref, kr_ref, c_ref, s1_ref, s2_ref, qo, ko, vo):
        c, s1, s2 = c_ref[...], s1_ref[...], s2_ref[...]
        kr = _rot(kr_ref[...], c, s1, s2)
        for h in range(MLA_HEADS):
            sl = slice(h * SLOT, (h + 1) * SLOT)
            qo[:, sl] = _rot(q_ref[:, sl], c, s1, s2).astype(BF16)
            ko[:, sl] = (kv_ref[:, sl] + kr).astype(BF16)
            vo[:, sl] = kv_ref[:, hw + h * SLOT:hw + (h + 1) * SLOT].astype(BF16)

    tab = pl.BlockSpec((tm, SLOT), lambda i: (i, 0))
    wide = pl.BlockSpec((tm, hw), lambda i: (i, 0))
    shp = jax.ShapeDtypeStruct((rows, hw), BF16)
    return pl.pallas_call(
        body, name="mla_rope_fwd", grid=(rows // tm,),
        in_specs=[wide, pl.BlockSpec((tm, 2 * hw), lambda i: (i, 0)), pl.BlockSpec((tm, SLOT), lambda i: (i, 3)),
                  tab, tab, tab],
        out_specs=[wide, wide, wide], out_shape=[shp, shp, shp], compiler_params=_cparams(),
    )(qraw, kvraw, proj, *tabs)


def _mla_rope_bwd(dq, dk, dv, tabs, consts):
    rows = dq.shape[0]
    tm = min(256, rows)
    hw = MLA_HEADS * SLOT

    def body(dq_ref, dk_ref, dv_ref, c_ref, s1_ref, s2_ref, k_ref, dqo, dkvo, dkro):
        c, s1, s2 = c_ref[...], s1_ref[...], s2_ref[...]
        ksum = jnp.zeros((tm, SLOT), F32)
        for h in range(MLA_HEADS):
            sl = slice(h * SLOT, (h + 1) * SLOT)
            dqo[:, sl] = _rot_t(dq_ref[:, sl], c, s1, s2).astype(BF16)
            dkh = dk_ref[:, sl]
            ksum = ksum + dkh
            dkvo[:, sl] = dkh.astype(BF16)
            dkvo[:, hw + h * SLOT:hw + (h + 1) * SLOT] = dv_ref[:, sl].astype(BF16)
        dkro[...] = _rot_t(ksum, c, s1, s2) * (k_ref[2:3, :] + k_ref[3:4, :])

    tab = pl.BlockSpec((tm, SLOT), lambda i: (i, 0))
    wide = pl.BlockSpec((tm, hw), lambda i: (i, 0))
    return pl.pallas_call(
        body, name="mla_rope_bwd", grid=(rows // tm,),
        in_specs=[wide, wide, wide, tab, tab, tab, pl.BlockSpec((8, SLOT), lambda i: (0, 0))],
        out_specs=[wide, pl.BlockSpec((tm, 2 * hw), lambda i: (i, 0)), tab],
        out_shape=[jax.ShapeDtypeStruct((rows, hw), BF16), jax.ShapeDtypeStruct((rows, 2 * hw), BF16),
                   jax.ShapeDtypeStruct((rows, SLOT), F32)],
        compiler_params=_cparams(),
    )(dq, dk, dv, *tabs, consts)


def _nt(a, b):
    return lax.dot_general(a, b, _DIMS["nt"], preferred_element_type=F32)


def _tn(a, b):
    return lax.dot_general(a, b, _DIMS["tn"], preferred_element_type=F32)


def _nn(a, b):
    return lax.dot_general(a, b, _DIMS["nn"], preferred_element_type=F32)


def _mla_attn_fwd(q, k, v):
    rows = q.shape[0]
    t = min(TQ_MLA, rows)
    nt = rows // t
    scale = MLA_QK ** -0.5

    def body(q_ref, k_ref, v_ref, o_ref, lse_ref, m_sc, l_sc, acc_sc):
        i, j = pl.program_id(1), pl.program_id(2)

        @pl.when(j == 0)
        def _():
            m_sc[...] = jnp.full_like(m_sc, NEG)
            l_sc[...] = jnp.zeros_like(l_sc)
            acc_sc[...] = jnp.zeros_like(acc_sc)

        @pl.when(j <= i)
        def _():
            s = _nt(q_ref[...], k_ref[...]) * scale
            row = i * t + lax.broadcasted_iota(jnp.int32, (t, t), 0)
            col = j * t + lax.broadcasted_iota(jnp.int32, (t, t), 1)
            s = jnp.where(col <= row, s, NEG)
            m_prev = m_sc[...]
            m_new = jnp.maximum(m_prev, jnp.max(s, axis=1, keepdims=True))
            p = jnp.exp(s - m_new)
            alpha = jnp.exp(m_prev - m_new)
            l_sc[...] = alpha * l_sc[...] + jnp.sum(p, axis=1, keepdims=True)
            acc_sc[...] = alpha * acc_sc[...] + _nn(p.astype(BF16), v_ref[...])
            m_sc[...] = m_new

        @pl.when(j == i)
        def _():
            l = l_sc[...]
            o_ref[...] = (acc_sc[...] / l).astype(o_ref.dtype)
            lse_ref[...] = jnp.broadcast_to(m_sc[...] + jnp.log(l), lse_ref.shape)

    q_spec = pl.BlockSpec((t, SLOT), lambda h, i, j: (i, h))
    kv_spec = pl.BlockSpec((t, SLOT), lambda h, i, j: (jnp.minimum(j, i), h))
    return pl.pallas_call(
        body, name="mla_attn_fwd", grid=(MLA_HEADS, nt, nt),
        in_specs=[q_spec, kv_spec, kv_spec], out_specs=[q_spec, q_spec],
        out_shape=[jax.ShapeDtypeStruct(q.shape, BF16), jax.ShapeDtypeStruct(q.shape, F32)],
        scratch_shapes=[pltpu.VMEM((t, 1), F32), pltpu.VMEM((t, 1), F32), pltpu.VMEM((t, SLOT), F32)],
        compiler_params=_cparams(),
    )(q, k, v)


def _mla_attn_bwd(q, k, v, o, do, lse):
    rows = q.shape[0]
    t = min(TQ_MLA, rows)
    nt = rows // t
    scale = MLA_QK ** -0.5

    def body(q_ref, k_ref, v_ref, o_ref, do_ref, lse_ref, dq_ref, dk_ref, dv_ref, dk_sc, dv_sc):
        j, i = pl.program_id(1), pl.program_id(2)

        @pl.when((j == 0) & (i == 0))
        def _():
            dq_ref[...] = jnp.zeros_like(dq_ref)

        @pl.when(i == 0)
        def _():
            dk_sc[...] = jnp.zeros_like(dk_sc)
            dv_sc[...] = jnp.zeros_like(dv_sc)

        @pl.when(i >= j)
        def _():
            qv, kv, dov = q_ref[...], k_ref[...], do_ref[...]
            s = _nt(qv, kv) * scale
            row = i * t + lax.broadcasted_iota(jnp.int32, (t, t), 0)
            col = j * t + lax.broadcasted_iota(jnp.int32, (t, t), 1)
            s = jnp.where(col <= row, s, NEG)
            p = jnp.exp(s - lse_ref[:, 0:1])
            delta = jnp.sum(dov.astype(F32) * o_ref[...].astype(F32), axis=1, keepdims=True)
            dp = _nt(dov, v_ref[...])
            ds = (p * (dp - delta) * scale).astype(BF16)
            dv_sc[...] += _tn(p.astype(BF16), dov)
            dk_sc[...] += _tn(ds, qv)
            r0 = pl.multiple_of(i * t, t)
            dq_ref[pl.ds(r0, t), :] += _nn(ds, kv)

        @pl.when(i == nt - 1)
        def _():
            dk_ref[...] = dk_sc[...]
            dv_ref[...] = dv_sc[...]

    q_spec = pl.BlockSpec((t, SLOT), lambda h, j, i: (jnp.maximum(i, j), h))
    kv_spec = pl.BlockSpec((t, SLOT), lambda h, j, i: (j, h))
    head_spec = pl.BlockSpec((rows, SLOT), lambda h, j, i: (0, h))
    shp = jax.ShapeDtypeStruct(q.shape, F32)
    return pl.pallas_call(
        body, name="mla_attn_bwd", grid=(MLA_HEADS, nt, nt),
        in_specs=[q_spec, kv_spec, kv_spec, q_spec, q_spec, q_spec],
        out_specs=[head_spec, kv_spec, kv_spec], out_shape=[shp, shp, shp],
        scratch_shapes=[pltpu.VMEM((t, SLOT), F32), pltpu.VMEM((t, SLOT), F32)],
        compiler_params=_cparams(),
    )(q, k, v, o, do, lse)


def _swa_scores(q, kp, kc, pq, pkp, pkc, slope, i, t):
    r = lax.broadcasted_iota(jnp.int32, (t, t), 0)
    c = lax.broadcasted_iota(jnp.int32, (t, t), 1)
    back_c = r - c
    back_p = back_c + t
    ok_c = (back_c >= 0) & (back_c < WINDOW)
    ok_p = back_p < jnp.where(i > 0, WINDOW, 0)
    s_p = _nt(q, kp) * (HEAD_DIM ** -0.5) - slope * (pq - pkp)
    s_c = _nt(q, kc) * (HEAD_DIM ** -0.5) - slope * (pq - pkc)
    return jnp.where(ok_p, s_p, NEG), jnp.where(ok_c, s_c, NEG)


def _swa_specs(t, head_of, tile_of, kvh_of):
    def prev(g):
        return jnp.maximum(tile_of(g) - 1, 0)
    q_spec = pl.BlockSpec((t, SLOT), lambda *g: (tile_of(g), head_of(g)))
    kp = pl.BlockSpec((t, SLOT), lambda *g: (prev(g), SWA_HEADS + kvh_of(g)))
    kc = pl.BlockSpec((t, SLOT), lambda *g: (tile_of(g), SWA_HEADS + kvh_of(g)))
    vp = pl.BlockSpec((t, SLOT), lambda *g: (prev(g), SWA_HEADS + SWA_KV_HEADS + kvh_of(g)))
    vc = pl.BlockSpec((t, SLOT), lambda *g: (tile_of(g), SWA_HEADS + SWA_KV_HEADS + kvh_of(g)))
    pcol = pl.BlockSpec((t, 1), lambda *g: (tile_of(g), 0))
    prow_p = pl.BlockSpec((1, t), lambda *g: (0, prev(g)))
    prow_c = pl.BlockSpec((1, t), lambda *g: (0, tile_of(g)))
    return q_spec, kp, kc, vp, vc, pcol, prow_p, prow_c


def _swa_attn_fwd(proj, pos_col, pos_row, slopes, sinks):
    rows = proj.shape[0]
    t = min(TQ_SWA, rows)
    hw = SWA_HEADS * SLOT

    def body(slope_ref, sink_ref, q_ref, kp_ref, kc_ref, vp_ref, vc_ref, pq_ref, pkp_ref, pkc_ref, o_ref, lse_ref):
        h, i = pl.program_id(0), pl.program_id(1)
        sink = sink_ref[h]
        s_p, s_c = _swa_scores(q_ref[...], kp_ref[...], kc_ref[...], pq_ref[...], pkp_ref[...], pkc_ref[...],
                               slope_ref[h], i, t)
        m = jnp.maximum(jnp.maximum(jnp.max(s_p, axis=1, keepdims=True), jnp.max(s_c, axis=1, keepdims=True)), sink)
        e_p, e_c = jnp.exp(s_p - m), jnp.exp(s_c - m)
        l = jnp.sum(e_p, axis=1, keepdims=True) + jnp.sum(e_c, axis=1, keepdims=True) + jnp.exp(sink - m)
        acc = _nn(e_p.astype(BF16), vp_ref[...]) + _nn(e_c.astype(BF16), vc_ref[...])
        o_ref[...] = (acc / l).astype(o_ref.dtype)
        lse_ref[...] = jnp.broadcast_to(m + jnp.log(l), lse_ref.shape)

    specs = _swa_specs(t, lambda g: g[0], lambda g: g[1], lambda g: g[0] // SWA_GROUP)
    smem = pl.BlockSpec(memory_space=pltpu.SMEM)
    out_spec = pl.BlockSpec((t, SLOT), lambda h, i: (i, h))
    return pl.pallas_call(
        body, name="swa_attn_fwd", grid=(SWA_HEADS, rows // t),
        in_specs=[smem, smem, specs[0], specs[1], specs[2], specs[3], specs[4], specs[5], specs[6], specs[7]],
        out_specs=[out_spec, out_spec],
        out_shape=[jax.ShapeDtypeStruct((rows, hw), BF16), jax.ShapeDtypeStruct((rows, hw), F32)],
        compiler_params=_cparams(),
    )(slopes, sinks, proj, proj, proj, proj, proj, pos_col, pos_row, pos_row)


def _swa_attn_bwd(proj, o, do, lse, pos_col, pos_row, slopes, sinks):
    rows = proj.shape[0]
    t = min(TQ_SWA, rows)
    nt = rows // t
    hw = SWA_HEADS * SLOT
    scale = HEAD_DIM ** -0.5

    def body(slope_ref, sink_ref, q_ref, kp_ref, kc_ref, vp_ref, vc_ref, pq_ref, pkp_ref, pkc_ref,
             o_ref, do_ref, lse_ref, dq_ref, dk_ref, dv_ref, dsink_ref):
        kvh, g, i = pl.program_id(0), pl.program_id(1), pl.program_id(2)
        h = kvh * SWA_GROUP + g
        sink = sink_ref[h]

        @pl.when((g == 0) & (i == 0))
        def _():
            dk_ref[...] = jnp.zeros_like(dk_ref)
            dv_ref[...] = jnp.zeros_like(dv_ref)

        @pl.when(i == 0)
        def _():
            dsink_ref[...] = jnp.zeros_like(dsink_ref)

        qv, dov = q_ref[...], do_ref[...]
        s_p, s_c = _swa_scores(qv, kp_ref[...], kc_ref[...], pq_ref[...], pkp_ref[...], pkc_ref[...],
                               slope_ref[h], i, t)
        lse = lse_ref[:, 0:1]
        p_p, p_c = jnp.exp(s_p - lse), jnp.exp(s_c - lse)
        delta = jnp.sum(dov.astype(F32) * o_ref[...].astype(F32), axis=1, keepdims=True)
        ds_p = (p_p * (_nt(dov, vp_ref[...]) - delta)).astype(BF16)
        ds_c = (p_c * (_nt(dov, vc_ref[...]) - delta)).astype(BF16)
        dq_ref[...] = (_nn(ds_p, kp_ref[...]) + _nn(ds_c, kc_ref[...])) * scale
        r_c = pl.multiple_of(i * t, t)
        dk_ref[pl.ds(r_c, t), :] += _tn(ds_c, qv) * scale
        dv_ref[pl.ds(r_c, t), :] += _tn(p_c.astype(BF16), dov)

        @pl.when(i > 0)
        def _():
            r_p = pl.multiple_of((i - 1) * t, t)
            dk_ref[pl.ds(r_p, t), :] += _tn(ds_p, qv) * scale
            dv_ref[pl.ds(r_p, t), :] += _tn(p_p.astype(BF16), dov)

        dsink = -jnp.sum(jnp.exp(sink - lse) * delta, axis=0, keepdims=True)
        dsink_ref[...] += jnp.broadcast_to(dsink, dsink_ref.shape)

    specs = _swa_specs(t, lambda g: g[0] * SWA_GROUP + g[1], lambda g: g[2], lambda g: g[0])
    smem = pl.BlockSpec(memory_space=pltpu.SMEM)
    qlike = pl.BlockSpec((t, SLOT), lambda kvh, g, i: (i, kvh * SWA_GROUP + g))
    kv_out = pl.BlockSpec((rows, SLOT), lambda kvh, g, i: (0, kvh))
    return pl.pallas_call(
        body, name="swa_attn_bwd", grid=(SWA_KV_HEADS, SWA_GROUP, nt),
        in_specs=[smem, smem, specs[0], specs[1], specs[2], specs[3], specs[4], specs[5], specs[6], specs[7],
                  qlike, qlike, qlike],
        out_specs=[qlike, kv_out, kv_out, pl.BlockSpec((8, SLOT), lambda kvh, g, i: (kvh * SWA_GROUP + g, 0))],
        out_shape=[jax.ShapeDtypeStruct((rows, hw), F32), jax.ShapeDtypeStruct((rows, SWA_KV_HEADS * SLOT), F32),
                   jax.ShapeDtypeStruct((rows, SWA_KV_HEADS * SLOT), F32),
                   jax.ShapeDtypeStruct((SWA_HEADS * 8, SLOT), F32)],
        compiler_params=_cparams(),
    )(slopes, sinks, proj, proj, proj, proj, proj, pos_col, pos_row, pos_row, o, do, lse)


def _cross_attn_fwd(proj, qoff, kvmem):
    rows = proj.shape[0]
    t = min(TQ_CROSS, rows)

    def body(q_ref, k_ref, v_ref, o_ref):
        s = _nt(q_ref[...].astype(BF16), k_ref[...]) * (HEAD_DIM ** -0.5)
        e = jnp.exp(s - jnp.max(s, axis=1, keepdims=True))
        p = e / jnp.sum(e, axis=1, keepdims=True)
        o_ref[...] = _nn(p.astype(BF16), v_ref[...]).astype(o_ref.dtype)

    return pl.pallas_call(
        body, name="cross_attn_fwd", grid=(rows // t, MEM_HEADS),
        in_specs=[pl.BlockSpec((t, SLOT), lambda i, h: (i, qoff + h)),
                  pl.BlockSpec((N_MEM, SLOT), lambda i, h: (0, h)),
                  pl.BlockSpec((N_MEM, SLOT), lambda i, h: (0, MEM_HEADS + h))],
        out_specs=pl.BlockSpec((t, SLOT), lambda i, h: (i, h)),
        out_shape=jax.ShapeDtypeStruct((rows, MEM_HEADS * SLOT), BF16), compiler_params=_cparams(),
    )(proj, kvmem, kvmem)


def _cross_attn_bwd(proj, qoff, kvmem, do):
    rows = proj.shape[0]
    t = min(TQ_CROSS, rows)
    scale = HEAD_DIM ** -0.5

    def body(q_ref, k_ref, v_ref, do_ref, dq_ref, dk_ref, dv_ref):
        @pl.when(pl.program_id(1) == 0)
        def _():
            dk_ref[...] = jnp.zeros_like(dk_ref)
            dv_ref[...] = jnp.zeros_like(dv_ref)

        qv, kv, dov = q_ref[...].astype(BF16), k_ref[...], do_ref[...]
        s = _nt(qv, kv) * scale
        e = jnp.exp(s - jnp.max(s, axis=1, keepdims=True))
        p = e / jnp.sum(e, axis=1, keepdims=True)
        dp = _nt(dov, v_ref[...])
        ds = (p * (dp - jnp.sum(p * dp, axis=1, keepdims=True))).astype(BF16)
        dq_ref[...] = _nn(ds, kv) * scale
        dk_ref[...] += _tn(ds, qv) * scale
        dv_ref[...] += _tn(p.astype(BF16), dov)

    mem_out = pl.BlockSpec((N_MEM, SLOT), lambda h, i: (0, h))
    return pl.pallas_call(
        body, name="cross_attn_bwd", grid=(MEM_HEADS, rows // t),
        in_specs=[pl.BlockSpec((t, SLOT), lambda h, i: (i, qoff + h)),
                  pl.BlockSpec((N_MEM, SLOT), lambda h, i: (0, h)),
                  pl.BlockSpec((N_MEM, SLOT), lambda h, i: (0, MEM_HEADS + h)),
                  pl.BlockSpec((t, SLOT), lambda h, i: (i, h))],
        out_specs=[pl.BlockSpec((t, SLOT), lambda h, i: (i, h)), mem_out, mem_out],
        out_shape=[jax.ShapeDtypeStruct((rows, MEM_HEADS * SLOT), F32),
                   jax.ShapeDtypeStruct((N_MEM, MEM_HEADS * SLOT), F32),
                   jax.ShapeDtypeStruct((N_MEM, MEM_HEADS * SLOT), F32)],
        compiler_params=_cparams(),
    )(proj, kvmem, kvmem, do)


def _place():
    return lax.axis_index("x"), lax.axis_index("y"), lax.axis_index("c")


def _flip(v, bit):
    return 1 - v if bit else v


def _all_gather(block, name):
    def body(x_ref, out_ref, send_sems, recv_sems, local_sem):
        x, y, c = _place()
        me, sibling = (x, y, c), (x, y, 1 - c)
        chips = [(1 - x, y), (x, 1 - y), (1 - x, 1 - y)]

        def slot(px, py, pc):
            return out_ref.at[4 * px + 2 * py + pc]

        def copy(k, blk, to, src=None):
            return pltpu.make_async_remote_copy(
                src_ref=slot(*blk) if src is None else src, dst_ref=slot(*blk),
                send_sem=send_sems.at[k], recv_sem=recv_sems.at[k],
                device_id=to, device_id_type=pl.DeviceIdType.MESH)

        mine = pltpu.make_async_copy(x_ref, slot(*me), local_sem)
        mine.start()
        first = [copy(0, me, sibling, src=x_ref)]
        first += [copy(1 + n, me, (*chip, c), src=x_ref) for n, chip in enumerate(chips)]
        for cp in first:
            cp.start()
        passed = [copy(4 + n, (*chip, c), sibling) for n, chip in enumerate(chips)]
        for n, chip in enumerate(chips):
            copy(1 + n, (*chip, c), me).wait_recv()
            passed[n].start()
        copy(0, sibling, me).wait_recv()
        for n, chip in enumerate(chips):
            copy(4 + n, (*chip, 1 - c), me).wait_recv()
        for cp in first + passed:
            cp.wait_send()
        mine.wait()

    any_spec = pl.BlockSpec(memory_space=pl.ANY)
    return pl.pallas_call(
        body, name=name, in_specs=[any_spec], out_specs=any_spec,
        out_shape=jax.ShapeDtypeStruct((N_DEV,) + block.shape, block.dtype),
        scratch_shapes=[pltpu.SemaphoreType.DMA((7,)), pltpu.SemaphoreType.DMA((7,)), pltpu.SemaphoreType.DMA],
    )(block)


def _all_to_all(blocks, name):
    def body(g_ref, out_ref, send_sems, recv_sems, local_sem):
        x, y, c = _place()
        my_idx = 4 * x + 2 * y + c
        mine = pltpu.make_async_copy(g_ref.at[my_idx], out_ref.at[my_idx], local_sem)
        mine.start()
        sends, arrivals = [], []
        for n in range(1, N_DEV):
            peer = (_flip(x, n & 4), _flip(y, n & 2), _flip(c, n & 1))
            peer_idx = 4 * peer[0] + 2 * peer[1] + peer[2]
            sends.append(pltpu.make_async_remote_copy(
                src_ref=g_ref.at[peer_idx], dst_ref=out_ref.at[my_idx],
                send_sem=send_sems.at[n - 1], recv_sem=recv_sems.at[n - 1],
                device_id=peer, device_id_type=pl.DeviceIdType.MESH))
            arrivals.append(pltpu.make_async_remote_copy(
                src_ref=g_ref.at[my_idx], dst_ref=out_ref.at[peer_idx],
                send_sem=send_sems.at[n - 1], recv_sem=recv_sems.at[n - 1],
                device_id=peer, device_id_type=pl.DeviceIdType.MESH))
        for cp in sends:
            cp.start()
        for cp in arrivals:
            cp.wait_recv()
        for cp in sends:
            cp.wait_send()
        mine.wait()

    any_spec = pl.BlockSpec(memory_space=pl.ANY)
    return pl.pallas_call(
        body, name=name, in_specs=[any_spec], out_specs=any_spec,
        out_shape=jax.ShapeDtypeStruct(blocks.shape, blocks.dtype),
        scratch_shapes=[pltpu.SemaphoreType.DMA((7,)), pltpu.SemaphoreType.DMA((7,)), pltpu.SemaphoreType.DMA],
    )(blocks)


def _adamw(parts, w, m, v, name):
    rows, cols = w.shape
    tr = min(ADAM_ROWS, rows)

    def body(p_ref, w_ref, m_ref, v_ref, g_out, d_out, m_out, v_out):
        g = p_ref[0].astype(F32)
        for s in range(1, N_DEV):
            g = g + p_ref[s].astype(F32)
        m2 = ADAM_B1 * m_ref[...] + (1.0 - ADAM_B1) * g
        v2 = ADAM_B2 * v_ref[...] + (1.0 - ADAM_B2) * (g * g)
        m_hat = m2 / (1.0 - ADAM_B1 ** ADAM_STEP)
        v_hat = v2 / (1.0 - ADAM_B2 ** ADAM_STEP)
        g_out[...] = g
        d_out[...] = -ADAM_LR * (m_hat / (jnp.sqrt(v_hat) + ADAM_EPS) + ADAM_WD * w_ref[...])
        m_out[...] = m2
        v_out[...] = v2

    spec = pl.BlockSpec((tr, cols), lambda i: (i, 0))
    shp = jax.ShapeDtypeStruct((rows, cols), F32)
    return pl.pallas_call(
        body, name=name, grid=(rows // tr,),
        in_specs=[pl.BlockSpec((N_DEV, tr, cols), lambda i: (0, i, 0)), spec, spec, spec],
        out_specs=[spec] * 4, out_shape=[shp] * 4, compiler_params=_cparams(),
    )(parts, w, m, v)


def _pack(arrays, lanes, row_mult, dtype):
    flat = jnp.concatenate([a.reshape(-1).astype(dtype) for a in arrays])
    unit = lanes * row_mult
    total = -(-flat.shape[0] // unit) * unit
    return jnp.pad(flat, (0, total - flat.shape[0])).reshape(total // lanes, lanes)


def _unpack(packed, shapes):
    flat = packed.reshape(-1)
    out, off = [], 0
    for shp in shapes:
        n = 1
        for d in shp:
            n *= d
        out.append(flat[off:off + n].reshape(shp))
        off += n
    return out


def _pack_blocks(arrays, lanes, row_mult, dtype):
    flat = jnp.concatenate([a.reshape(N_DEV, -1).astype(dtype) for a in arrays], axis=1)
    unit = lanes * row_mult
    total = -(-flat.shape[1] // unit) * unit
    return jnp.pad(flat, ((0, 0), (0, total - flat.shape[1]))).reshape(N_DEV, total // lanes, lanes)


def _pad_slots(w, axis):
    axis = axis % w.ndim
    n = w.shape[axis] // HEAD_DIM
    shp = w.shape[:axis] + (n, HEAD_DIM) + w.shape[axis + 1:]
    pad = [(0, 0)] * (w.ndim + 1)
    pad[axis + 1] = (0, SLOT - HEAD_DIM)
    return jnp.pad(w.reshape(shp), pad).reshape(w.shape[:axis] + (n * SLOT,) + w.shape[axis + 1:])


def _unpad_slots(w, axis, keep=HEAD_DIM):
    axis = axis % w.ndim
    n = w.shape[axis] // SLOT
    shp = w.shape[:axis] + (n, SLOT) + w.shape[axis + 1:]
    idx = [slice(None)] * (w.ndim + 1)
    idx[axis + 1] = slice(0, keep)
    return w.reshape(shp)[tuple(idx)].reshape(w.shape[:axis] + (n * keep,) + w.shape[axis + 1:])


def _mla_in_pad(w):
    z = functools.partial(jnp.zeros, dtype=w.dtype)
    rows = w.shape[0]
    return jnp.concatenate([w[:, :384], z((rows, 64)), w[:, 640:672], z((rows, 32)), w[:, 384:640],
                            _pad_slots(w[:, 672:], 1)], axis=1)


def _mla_in_unpad(d):
    return jnp.concatenate([d[:, :384], d[:, 512:768], d[:, 448:480], _unpad_slots(d[:, 768:], 1)], axis=1)


def _mla_uq_pad(w):
    return jnp.pad(w.reshape(w.shape[0], MLA_HEADS, MLA_QK), ((0, 0), (0, 0), (0, SLOT - MLA_QK))).reshape(
        w.shape[0], MLA_HEADS * SLOT)


def _mla_ukv_pad(w):
    w3 = w.reshape(w.shape[0], MLA_HEADS, 2 * HEAD_DIM)
    pad = ((0, 0), (0, 0), (0, SLOT - HEAD_DIM))
    k = jnp.pad(w3[:, :, :HEAD_DIM], pad).reshape(w.shape[0], -1)
    v = jnp.pad(w3[:, :, HEAD_DIM:], pad).reshape(w.shape[0], -1)
    return jnp.concatenate([k, v], axis=1)


def _mla_ukv_unpad(d):
    hw = MLA_HEADS * SLOT
    k = d[:, :hw].reshape(d.shape[0], MLA_HEADS, SLOT)[:, :, :HEAD_DIM]
    v = d[:, hw:].reshape(d.shape[0], MLA_HEADS, SLOT)[:, :, :HEAD_DIM]
    return jnp.concatenate([k, v], axis=2).reshape(d.shape[0], MLA_HEADS * 2 * HEAD_DIM)


def kernel(x, mem, positions, attn_norm_g, mlp_norm_g, mem_norm_g, final_norm_g, mla_w_in, mla_q_norm_g, mla_kv_norm_g, mla_w_uq, mla_w_ukv, swa_w_in, swa_sinks, w_mem_kv, w_o, mlp_w_up, mlp_w_down, loss_target, m_attn_norm_g, m_mlp_norm_g, m_mem_norm_g, m_final_norm_g, m_mla_w_in, m_mla_q_norm_g, m_mla_kv_norm_g, m_mla_w_uq, m_mla_w_ukv, m_swa_w_in, m_swa_sinks, m_w_mem_kv, m_w_o, m_mlp_w_up, m_mlp_w_down, v_attn_norm_g, v_mlp_norm_g, v_mem_norm_g, v_final_norm_g, v_mla_w_in, v_mla_q_norm_g, v_mla_kv_norm_g, v_mla_w_uq, v_mla_w_ukv, v_swa_w_in, v_swa_sinks, v_w_mem_kv, v_w_o, v_mlp_w_up, v_mlp_w_down):
    given = dict(locals())
    seq = x.shape[1]
    x0 = x.reshape(seq, D_MODEL)
    tgt = loss_target.reshape(seq, D_MODEL)
    mem0 = mem.reshape(N_MEM, D_MODEL)
    pos = positions.reshape(seq).astype(F32)
    pos_col, pos_row = pos.reshape(seq, 1), pos.reshape(1, seq)

    local_shapes = [given[n].shape for n, _ in SHARDED]
    w_all = _all_gather(_pack([given[n] for n, _ in SHARDED], PACK_LANES, ADAM_ROWS, BF16), "gather_weights")
    per_dev = [_unpack(w_all[d], local_shapes) for d in range(N_DEV)]
    full = {}
    for t, (n, axis) in enumerate(SHARDED):
        full[n] = jnp.concatenate([per_dev[d][t] for d in range(N_DEV)], axis=axis)

    consts = _lane_consts()
    tabs = _rope_tables(pos_col, consts)
    slopes = 2.0 ** (-8.0 * (jnp.arange(SWA_HEADS, dtype=F32) + 1.0) / SWA_HEADS)

    mem_n = _rmsnorm_fwd(mem0, 0, D_MODEL, mem_norm_g, "rmsnorm_fwd_mem")

    saved = []
    xc = x0
    for i in range(DEPTH):
        j = i // 2
        s = {"x_in": xc}
        hn = _rmsnorm_fwd(xc, 0, D_MODEL, attn_norm_g[i], "rmsnorm_fwd")
        if i % 2 == 0:
            w_in = _mla_in_pad(full["mla_w_in"][j])
            w_uq = _mla_uq_pad(full["mla_w_uq"][j])
            w_kv = _mla_ukv_pad(full["mla_w_ukv"][j])
            proj = _mm(hn, w_in, "nn", F32, "mm_mla_in")
            cqn = _rmsnorm_fwd(proj, 0, MLA_Q_RANK, mla_q_norm_g[j], "rmsnorm_fwd_q")
            ckvn = _rmsnorm_fwd(proj, 2, MLA_KV_RANK, mla_kv_norm_g[j], "rmsnorm_fwd_kv")
            qraw = _mm(cqn, w_uq, "nn", F32, "mm_mla_uq")
            kvraw = _mm(ckvn, w_kv, "nn", F32, "mm_mla_ukv")
            q, k, v = _mla_rope_fwd(qraw, kvraw, proj, tabs)
            o, lse = _mla_attn_fwd(q, k, v)
            qoff = MLA_QOFF
            s.update(w_uq=w_uq, w_kv=w_kv, cqn=cqn, ckvn=ckvn, q=q, k=k, v=v)
        else:
            w_in = _pad_slots(full["swa_w_in"][j], 1)
            proj = _mm(hn, w_in, "nn", BF16, "mm_swa_in")
            o, lse = _swa_attn_fwd(proj, pos_col, pos_row, slopes, swa_sinks[j])
            qoff = SWA_QOFF
        w_mem = _pad_slots(full["w_mem_kv"][i], 1)
        w_out = _pad_slots(full["w_o"][i], 0)
        w_o_mix, w_o_cross = w_out[:SWA_HEADS * SLOT], w_out[SWA_HEADS * SLOT:]
        kvmem = _mm(mem_n, w_mem, "nn", BF16, "mm_mem_kv")
        cross = _cross_attn_fwd(proj, qoff, kvmem)
        x1 = _mm(o, w_o_mix, "nn", F32, "mm_o_mix", res=xc)
        x1 = _mm(cross, w_o_cross, "nn", F32, "mm_o_cross", res=x1)
        hn2 = _rmsnorm_fwd(x1, 0, D_MODEL, mlp_norm_g[i], "rmsnorm_fwd")
        act, act2 = _mm(hn2, full["mlp_w_up"][i], "nn", BF16, "mm_mlp_up", epi="relu2")
        xc = _mm(act2, full["mlp_w_down"][i], "nn", F32, "mm_mlp_down", res=x1)
        s.update(hn=hn, w_in=w_in, proj=proj, o=o, lse=lse, qoff=qoff, w_mem=w_mem, w_o_mix=w_o_mix,
                 w_o_cross=w_o_cross, kvmem=kvmem, cross=cross, x1=x1, hn2=hn2, act=act, act2=act2)
        saved.append(s)

    dx, dg_final, loss_part = _loss_head(xc, final_norm_g, tgt)
    loss = lax.psum(loss_part[0, 0], MESH_AXES)

    grads = {n: [None] * DEPTH for n in ("w_mem_kv", "w_o", "mlp_w_up", "mlp_w_down", "attn_norm_g", "mlp_norm_g")}
    for n in ("mla_w_in", "mla_w_uq", "mla_w_ukv", "swa_w_in", "mla_q_norm_g", "mla_kv_norm_g", "swa_sinks"):
        grads[n] = [None] * 2
    dmem_n = None
    for i in reversed(range(DEPTH)):
        j = i // 2
        s = saved[i]
        du = _mm(dx, full["mlp_w_down"][i], "nt", BF16, "mm_mlp_down_dx", aux=s["act"], epi="mul2aux")
        grads["mlp_w_down"][i] = _mm(s["act2"], dx, "tn", F32, "mm_mlp_down_dw")
        dhn2 = _mm(du, full["mlp_w_up"][i], "nt", F32, "mm_mlp_up_dx")
        grads["mlp_w_up"][i] = _mm(s["hn2"], du, "tn", F32, "mm_mlp_up_dw")
        dx1, dg = _rmsnorm_bwd(s["x1"], 0, D_MODEL, mlp_norm_g[i], dhn2, dx, F32, "rmsnorm_bwd")
        grads["mlp_norm_g"][i] = dg[0]

        do = _mm(dx1, s["w_o_mix"], "nt", BF16, "mm_o_mix_dx")
        dcross = _mm(dx1, s["w_o_cross"], "nt", BF16, "mm_o_cross_dx")
        dw_o = jnp.concatenate([_mm(s["o"], dx1, "tn", F32, "mm_o_mix_dw"),
                                _mm(s["cross"], dx1, "tn", F32, "mm_o_cross_dw")], axis=0)
        grads["w_o"][i] = _unpad_slots(dw_o, 0)
        dqc, dkm, dvm = _cross_attn_bwd(s["proj"], s["qoff"], s["kvmem"], dcross)
        dkvmem = jnp.concatenate([dkm, dvm], axis=1).astype(BF16)
        grads["w_mem_kv"][i] = _unpad_slots(_mm(mem_n, dkvmem, "tn", F32, "mm_mem_kv_dw"), 1)
        dmem_n = _mm(dkvmem, s["w_mem"], "nt", F32, "mm_mem_kv_dx" if dmem_n is None else "mm_mem_kv_dx_acc",
                     res=dmem_n)

        if i % 2 == 0:
            dq, dk, dv = _mla_attn_bwd(s["q"], s["k"], s["v"], s["o"], do, s["lse"])
            dqraw, dkv, dkr = _mla_rope_bwd(dq, dk, dv, tabs, consts)
            dcqn = _mm(dqraw, s["w_uq"], "nt", F32, "mm_mla_uq_dx")
            grads["mla_w_uq"][j] = _unpad_slots(_mm(s["cqn"], dqraw, "tn", F32, "mm_mla_uq_dw"), 1, MLA_QK)
            dckvn = _mm(dkv, s["w_kv"], "nt", F32, "mm_mla_ukv_dx")
            grads["mla_w_ukv"][j] = _mla_ukv_unpad(_mm(s["ckvn"], dkv, "tn", F32, "mm_mla_ukv_dw"))
            dcq, dg = _rmsnorm_bwd(s["proj"], 0, MLA_Q_RANK, mla_q_norm_g[j], dcqn, None, BF16, "rmsnorm_bwd_q")
            grads["mla_q_norm_g"][j] = dg[0]
            dckv, dg = _rmsnorm_bwd(s["proj"], 2, MLA_KV_RANK, mla_kv_norm_g[j], dckvn, None, BF16, "rmsnorm_bwd_kv")
            grads["mla_kv_norm_g"][j] = dg[0]
            dproj = jnp.concatenate([dcq, dkr.astype(BF16), dckv, dqc.astype(BF16)], axis=1)
            dhn = _mm(dproj, s["w_in"], "nt", F32, "mm_mla_in_dx")
            grads["mla_w_in"][j] = _mla_in_unpad(_mm(s["hn"], dproj, "tn", F32, "mm_mla_in_dw"))
        else:
            dq, dk, dv, dsink = _swa_attn_bwd(s["proj"], s["o"], do, s["lse"], pos_col, pos_row, slopes, swa_sinks[j])
            grads["swa_sinks"][j] = dsink[::8, 0]
            dproj = jnp.concatenate([dq, dk, dv, dqc], axis=1).astype(BF16)
            dhn = _mm(dproj, s["w_in"], "nt", F32, "mm_swa_in_dx")
            grads["swa_w_in"][j] = _unpad_slots(_mm(s["hn"], dproj, "tn", F32, "mm_swa_in_dw"), 1)
        dx, dg = _rmsnorm_bwd(s["x_in"], 0, D_MODEL, attn_norm_g[i], dhn, dx1, F32, "rmsnorm_bwd")
        grads["attn_norm_g"][i] = dg[0]

    _, dg_mem = _rmsnorm_bwd(mem0, 0, D_MODEL, mem_norm_g, dmem_n, None, BF16, "rmsnorm_bwd_mem")
    grads = {n: jnp.stack(g) for n, g in grads.items()}
    grads["mem_norm_g"] = dg_mem[0]
    grads["final_norm_g"] = dg_final[0]

    blocks = []
    for n, axis in SHARDED:
        g = grads[n]
        lyr, r, c = g.shape
        if axis == 1:
            blocks.append(g.reshape(lyr, N_DEV, r // N_DEV, c).transpose(1, 0, 2, 3))
        else:
            blocks.append(g.reshape(lyr, r, N_DEV, c // N_DEV).transpose(2, 0, 1, 3))
    parts = _all_to_all(_pack_blocks(blocks, PACK_LANES, ADAM_ROWS, BF16), "exchange_grads")
    packed = [_pack([given[p + n] for n, _ in SHARDED], PACK_LANES, ADAM_ROWS, F32) for p in ("", "m_", "v_")]
    big = [_unpack(r, local_shapes) for r in _adamw(parts, *packed, "adamw")]

    rep_shapes = [given[n].shape for n in REPLICATED]
    rep_parts = _all_gather(_pack([grads[n] for n in REPLICATED], SLOT, 8, F32), "gather_gain_grads")
    rep_packed = [_pack([given[p + n] for n in REPLICATED], SLOT, 8, F32) for p in ("", "m_", "v_")]
    small = [_unpack(r, rep_shapes) for r in _adamw(rep_parts, *rep_packed, "adamw_gains")]

    result = {}
    for kind in range(4):
        for t, (n, _) in enumerate(SHARDED):
            result[(kind, n)] = big[kind][t]
        for t, n in enumerate(REPLICATED):
            result[(kind, n)] = small[kind][t]
    outs = [loss, dx.reshape(1, seq, D_MODEL)]
    for kind in range(4):
        outs += [result[(kind, n)] for n in WEIGHT_ORDER]
    return tuple(outs)
```

```python
import functools

import jax
import jax.numpy as jnp
from jax import lax
from jax.experimental import pallas as pl
from jax.experimental.pallas import tpu as pltpu

F32 = jnp.float32
BF16 = jnp.bfloat16

D_MODEL = 1024
D_FF = 4096
N_MEM = 256
DEPTH = 4
SLOT = 128
HEAD_DIM = 64
MLA_HEADS = 12
MLA_QK = 96
MLA_Q_RANK = 384
MLA_KV_RANK = 256
SWA_HEADS = 12
SWA_KV_HEADS = 4
SWA_GROUP = 3
MEM_HEADS = 4
WINDOW = 128
EPS = 1e-6
NEG = -1e30
ROPE_THETA = 10000.0
N_DEV = 8

ADAM_LR = 0.001
ADAM_B1 = 0.9
ADAM_B2 = 0.999
ADAM_EPS = 1e-08
ADAM_WD = 0.01
ADAM_STEP = 10

TM = 512
TQ_MLA = 512
MLA_PACK = 2
TQ_CROSS = 512
TN_CAP = 1024
TK_CAP = 1024
ADAM_ROWS = 128
VMEM_LIMIT = 48 * 1024 * 1024

MESH_AXES = ("x", "y", "c")

MLA_PAD_IN = 384 + SLOT + 256 + MEM_HEADS * SLOT
MLA_QOFF = (384 + SLOT + 256) // SLOT
SWA_PAD_IN = (SWA_HEADS + 2 * SWA_KV_HEADS + MEM_HEADS) * SLOT
SWA_QOFF = SWA_HEADS + 2 * SWA_KV_HEADS

SHARDED = (
    ("mla_w_in", 1), ("mla_w_uq", 2), ("mla_w_ukv", 2), ("swa_w_in", 2),
    ("w_mem_kv", 1), ("w_o", 1), ("mlp_w_up", 2), ("mlp_w_down", 1),
)
REPLICATED = ("attn_norm_g", "mlp_norm_g", "mem_norm_g", "final_norm_g",
              "mla_q_norm_g", "mla_kv_norm_g", "swa_sinks")
WEIGHT_ORDER = ("attn_norm_g", "mlp_norm_g", "mem_norm_g", "final_norm_g", "mla_w_in",
                "mla_q_norm_g", "mla_kv_norm_g", "mla_w_uq", "mla_w_ukv", "swa_w_in",
                "swa_sinks", "w_mem_kv", "w_o", "mlp_w_up", "mlp_w_down")


def _cparams():
    return pltpu.CompilerParams(vmem_limit_bytes=VMEM_LIMIT)


def _tile(n, cap):
    best = None
    t = SLOT
    while t <= min(n, cap):
        if n % t == 0:
            best = t
        t += SLOT
    return n if best is None else best


_DIMS = {"nn": (((1,), (0,)), ((), ())), "nt": (((1,), (1,)), ((), ())), "tn": (((0,), (0,)), ((), ()))}


def _mm(a, b, mode, out_dtype, name, res=None, aux=None, epi=None, b_blk=None, o_blk=None):
    if b_blk is not None:
        nb, br, bc = b.shape
        b_shape = (nb * br, bc) if b_blk == "rows" else (br, nb * bc)
    else:
        b_shape = b.shape
    if mode == "nn":
        (m, k), (k2, n) = a.shape, b_shape
    elif mode == "nt":
        (m, k), (n, k2) = a.shape, b_shape
    else:
        (k, m), (k2, n) = a.shape, b_shape
    assert k == k2, (a.shape, b_shape, mode)
    tm, tn, tk = _tile(m, TM), _tile(n, TN_CAP), _tile(k, TK_CAP)
    if b_blk is not None:
        first_is_k = mode != "nt"
        if (b_blk == "rows") == first_is_k:
            tk = br if b_blk == "rows" else bc
        else:
            tn = br if b_blk == "rows" else bc
    if o_blk == "rows":
        tm = m // N_DEV
    elif o_blk == "cols":
        tn = n // N_DEV
    nk = k // tk
    dims = _DIMS[mode]
    if mode == "tn":
        a_spec = pl.BlockSpec((tk, tm), lambda i, j, kk: (kk, i))
    else:
        a_spec = pl.BlockSpec((tm, tk), lambda i, j, kk: (i, kk))
    if b_blk is None:
        if mode == "nt":
            b_spec = pl.BlockSpec((tn, tk), lambda i, j, kk: (j, kk))
        else:
            b_spec = pl.BlockSpec((tk, tn), lambda i, j, kk: (kk, j))
    elif mode == "nt":
        if b_blk == "rows":
            b_spec = pl.BlockSpec((None, tn, tk), lambda i, j, kk: (j, 0, kk))
        else:
            b_spec = pl.BlockSpec((None, tn, tk), lambda i, j, kk: (kk, j, 0))
    else:
        if b_blk == "rows":
            b_spec = pl.BlockSpec((None, tk, tn), lambda i, j, kk: (kk, 0, j))
        else:
            b_spec = pl.BlockSpec((None, tk, tn), lambda i, j, kk: (j, kk, 0))
    if o_blk is None:
        o_spec = pl.BlockSpec((tm, tn), lambda i, j, kk: (i, j))
        o_shape = (m, n)
    elif o_blk == "rows":
        o_spec = pl.BlockSpec((None, tm, tn), lambda i, j, kk: (i, 0, j))
        o_shape = (N_DEV, tm, n)
    else:
        o_spec = pl.BlockSpec((None, tm, tn), lambda i, j, kk: (j, i, 0))
        o_shape = (N_DEV, m, tn)
    has_res, has_aux = res is not None, aux is not None
    assert o_blk is None or not (has_res or has_aux)

    def body(*refs):
        a_ref, b_ref = refs[0], refs[1]
        pos = 2
        res_ref = aux_ref = None
        if has_res:
            res_ref = refs[pos]
            pos += 1
        if has_aux:
            aux_ref = refs[pos]
            pos += 1
        outs = refs[pos:-1]
        acc = refs[-1]
        kk = pl.program_id(2)

        @pl.when(kk == 0)
        def _():
            acc[...] = jnp.zeros_like(acc)

        acc[...] += lax.dot_general(a_ref[...].astype(BF16), b_ref[...].astype(BF16), dims,
                                    preferred_element_type=F32)

        @pl.when(kk == nk - 1)
        def _():
            r = acc[...]
            if epi == "relu2":
                r = jnp.maximum(r, 0.0)
                outs[0][...] = r.astype(outs[0].dtype)
                outs[1][...] = (r * r).astype(outs[1].dtype)
            else:
                if epi == "mul2aux":
                    r = r * (2.0 * aux_ref[...].astype(F32))
                if has_res:
                    r = r + res_ref[...]
                outs[0][...] = r.astype(outs[0].dtype)

    in_specs = [a_spec, b_spec]
    args = [a, b]
    if has_res:
        in_specs.append(o_spec)
        args.append(res)
    if has_aux:
        in_specs.append(o_spec)
        args.append(aux)
    n_out = 2 if epi == "relu2" else 1
    out_shape = [jax.ShapeDtypeStruct(o_shape, out_dtype)] * n_out
    out = pl.pallas_call(
        body, name=name, grid=(m // tm, n // tn, nk),
        in_specs=in_specs, out_specs=[o_spec] * n_out, out_shape=out_shape,
        scratch_shapes=[pltpu.VMEM((tm, tn), F32)], compiler_params=_cparams(),
    )(*args)
    return out if n_out == 2 else out[0]


def _rmsnorm_fwd(xarr, colblk, width, g, name):
    rows = xarr.shape[0]
    tm = min(TM, rows)

    def body(x_ref, g_ref, y_ref):
        x = x_ref[...].astype(F32)
        r = lax.rsqrt(jnp.mean(x * x, axis=1, keepdims=True) + EPS)
        y_ref[...] = (x * r * g_ref[...]).astype(y_ref.dtype)

    return pl.pallas_call(
        body, name=name, grid=(rows // tm,),
        in_specs=[pl.BlockSpec((tm, width), lambda i: (i, colblk)), pl.BlockSpec((1, width), lambda i: (0, 0))],
        out_specs=pl.BlockSpec((tm, width), lambda i: (i, 0)),
        out_shape=jax.ShapeDtypeStruct((rows, width), BF16), compiler_params=_cparams(),
    )(xarr, g.reshape(1, width))


def _rmsnorm_bwd(xarr, colblk, width, g, dy, dres, out_dtype, name):
    rows = xarr.shape[0]
    tm = min(TM, rows)
    has_res = dres is not None

    def body(*refs):
        x_ref, g_ref, dy_ref = refs[0], refs[1], refs[2]
        dres_ref = refs[3] if has_res else None
        dx_ref, dg_ref = refs[-2], refs[-1]
        x = x_ref[...].astype(F32)
        dyv = dy_ref[...].astype(F32)
        r = lax.rsqrt(jnp.mean(x * x, axis=1, keepdims=True) + EPS)
        xh = x * r
        dxh = dyv * g_ref[...]
        dx = r * (dxh - xh * jnp.mean(dxh * xh, axis=1, keepdims=True))
        if has_res:
            dx = dx + dres_ref[...]
        dx_ref[...] = dx.astype(dx_ref.dtype)

        @pl.when(pl.program_id(0) == 0)
        def _():
            dg_ref[...] = jnp.zeros_like(dg_ref)

        dg_ref[...] += jnp.sum(dyv * xh, axis=0, keepdims=True)

    row_spec = pl.BlockSpec((tm, width), lambda i: (i, 0))
    vec_spec = pl.BlockSpec((1, width), lambda i: (0, 0))
    in_specs = [pl.BlockSpec((tm, width), lambda i: (i, colblk)), vec_spec, row_spec]
    args = [xarr, g.reshape(1, width), dy]
    if has_res:
        in_specs.append(row_spec)
        args.append(dres)
    return pl.pallas_call(
        body, name=name, grid=(rows // tm,), in_specs=in_specs, out_specs=[row_spec, vec_spec],
        out_shape=[jax.ShapeDtypeStruct((rows, width), out_dtype), jax.ShapeDtypeStruct((1, width), F32)],
        compiler_params=_cparams(),
    )(*args)


def _loss_head(x, g, tgt):
    rows, width = x.shape
    tm = min(TM, rows)

    def body(x_ref, g_ref, t_ref, dx_ref, dg_ref, loss_ref):
        xv = x_ref[...]
        gv = g_ref[...]
        r = lax.rsqrt(jnp.mean(xv * xv, axis=1, keepdims=True) + EPS)
        xh = xv * r
        err = xh * gv - t_ref[...]
        part = 0.5 * jnp.sum(jnp.mean(err * err, axis=1, keepdims=True), axis=0, keepdims=True)
        dyv = err * (1.0 / width)
        dxh = dyv * gv
        dx_ref[...] = r * (dxh - xh * jnp.mean(dxh * xh, axis=1, keepdims=True))

        @pl.when(pl.program_id(0) == 0)
        def _():
            dg_ref[...] = jnp.zeros_like(dg_ref)
            loss_ref[...] = jnp.zeros_like(loss_ref)

        dg_ref[...] += jnp.sum(dyv * xh, axis=0, keepdims=True)
        loss_ref[...] += jnp.broadcast_to(part, loss_ref.shape)

    row_spec = pl.BlockSpec((tm, width), lambda i: (i, 0))
    vec_spec = pl.BlockSpec((1, width), lambda i: (0, 0))
    return pl.pallas_call(
        body, name="loss_head", grid=(rows // tm,), in_specs=[row_spec, vec_spec, row_spec],
        out_specs=[row_spec, vec_spec, pl.BlockSpec((1, SLOT), lambda i: (0, 0))],
        out_shape=[jax.ShapeDtypeStruct((rows, width), F32), jax.ShapeDtypeStruct((1, width), F32),
                   jax.ShapeDtypeStruct((1, SLOT), F32)],
        compiler_params=_cparams(),
    )(x, g.reshape(1, width), tgt)


def _lane_consts():
    half = 16
    inv = ROPE_THETA ** (-(jnp.arange(half, dtype=F32) * 2.0) / 32)
    lane = jnp.arange(SLOT)
    first = (lane >= 64) & (lane < 80)
    second = (lane >= 80) & (lane < 96)
    inv_lane = jnp.where(first | second, inv[(lane - 64) % half], 0.0)
    rows = [inv_lane, (lane < 64).astype(F32), first.astype(F32), second.astype(F32)]
    rows += [jnp.zeros((SLOT,), F32)] * 4
    return jnp.stack(rows).astype(F32)


def _rope_tables(pos_col, consts):
    rows = pos_col.shape[0]
    tm = min(TM, rows)

    def body(p_ref, k_ref, c_ref, s1_ref, s2_ref):
        ang = p_ref[...] * k_ref[0:1, :]
        cos, sin = jnp.cos(ang), jnp.sin(ang)
        first, second = k_ref[2:3, :], k_ref[3:4, :]
        c_ref[...] = k_ref[1:2, :] + (first + second) * cos
        s1_ref[...] = -first * sin
        s2_ref[...] = second * sin

    spec = pl.BlockSpec((tm, SLOT), lambda i: (i, 0))
    shp = jax.ShapeDtypeStruct((rows, SLOT), F32)
    return pl.pallas_call(
        body, name="rope_tables", grid=(rows // tm,),
        in_specs=[pl.BlockSpec((tm, 1), lambda i: (i, 0)), pl.BlockSpec((8, SLOT), lambda i: (0, 0))],
        out_specs=[spec, spec, spec], out_shape=[shp, shp, shp], compiler_params=_cparams(),
    )(pos_col, consts)


def _rot(xv, c, s1, s2):
    return xv * c + pltpu.roll(xv, SLOT - 16, 1) * s1 + pltpu.roll(xv, 16, 1) * s2


def _rot_t(dy, c, s1, s2):
    return dy * c + pltpu.roll(dy * s1, 16, 1) + pltpu.roll(dy * s2, SLOT - 16, 1)


def _mla_rope_fwd(qraw, kvraw, proj, tabs):
    rows = qraw.shape[0]
    tm = min(256, rows)
    hw = MLA_HEADS * SLOT

    def body(q_ref, kv_ref, kr_ref, c_ref, s1_ref, s2_ref, qo, ko, vo):
        c, s1, s2 = c_ref[...], s1_ref[...], s2_ref[...]
        kr = _rot(kr_ref[...], c, s1, s2)
        for h in range(MLA_HEADS):
            sl = slice(h * SLOT, (h + 1) * SLOT)
            qo[:, sl] = _rot(q_ref[:, sl], c, s1, s2).astype(BF16)
            ko[:, sl] = (kv_ref[:, sl] + kr).astype(BF16)
            vo[:, sl] = kv_ref[:, hw + h * SLOT:hw + (h + 1) * SLOT].astype(BF16)

    tab = pl.BlockSpec((tm, SLOT), lambda i: (i, 0))
    wide = pl.BlockSpec((tm, hw), lambda i: (i, 0))
    shp = jax.ShapeDtypeStruct((rows, hw), BF16)
    return pl.pallas_call(
        body, name="mla_rope_fwd", grid=(rows // tm,),
        in_specs=[wide, pl.BlockSpec((tm, 2 * hw), lambda i: (i, 0)), pl.BlockSpec((tm, SLOT), lambda i: (i, 3)),
                  tab, tab, tab],
        out_specs=[wide, wide, wide], out_shape=[shp, shp, shp], compiler_params=_cparams(),
    )(qraw, kvraw, proj, *tabs)


def _mla_rope_bwd(dq, dk, dv, tabs, consts):
    rows = dq.shape[0]
    tm = min(256, rows)
    hw = MLA_HEADS * SLOT

    def body(dq_ref, dk_ref, dv_ref, c_ref, s1_ref, s2_ref, k_ref, dqo, dkvo, dkro):
        c, s1, s2 = c_ref[...], s1_ref[...], s2_ref[...]
        ksum = jnp.zeros((tm, SLOT), F32)
        for h in range(MLA_HEADS):
            sl = slice(h * SLOT, (h + 1) * SLOT)
            dqo[:, sl] = _rot_t(dq_ref[:, sl], c, s1, s2).astype(BF16)
            dkh = dk_ref[:, sl]
            ksum = ksum + dkh
            dkvo[:, sl] = dkh.astype(BF16)
            dkvo[:, hw + h * SLOT:hw + (h + 1) * SLOT] = dv_ref[:, sl].astype(BF16)
        dkro[...] = _rot_t(ksum, c, s1, s2) * (k_ref[2:3, :] + k_ref[3:4, :])

    tab = pl.BlockSpec((tm, SLOT), lambda i: (i, 0))
    wide = pl.BlockSpec((tm, hw), lambda i: (i, 0))
    return pl.pallas_call(
        body, name="mla_rope_bwd", grid=(rows // tm,),
        in_specs=[wide, wide, wide, tab, tab, tab, pl.BlockSpec((8, SLOT), lambda i: (0, 0))],
        out_specs=[wide, pl.BlockSpec((tm, 2 * hw), lambda i: (i, 0)), tab],
        out_shape=[jax.ShapeDtypeStruct((rows, hw), BF16), jax.ShapeDtypeStruct((rows, 2 * hw), BF16),
                   jax.ShapeDtypeStruct((rows, SLOT), F32)],
        compiler_params=_cparams(),
    )(dq, dk, dv, *tabs, consts)


def _nt(a, b):
    return lax.dot_general(a, b, _DIMS["nt"], preferred_element_type=F32)


def _tn(a, b):
    return lax.dot_general(a, b, _DIMS["tn"], preferred_element_type=F32)


def _nn(a, b):
    return lax.dot_general(a, b, _DIMS["nn"], preferred_element_type=F32)


def _causal(t):
    return lax.broadcasted_iota(jnp.int32, (t, t), 1) <= lax.broadcasted_iota(jnp.int32, (t, t), 0)


def _mla_attn_fwd(q, k, v):
    rows = q.shape[0]
    t = min(TQ_MLA, rows)
    nt = rows // t
    scale = MLA_QK ** -0.5
    wide = MLA_PACK * SLOT

    def body(q_ref, k_ref, v_ref, o_ref, lse_ref, m_sc, l_sc, acc_sc):
        i, j = pl.program_id(1), pl.program_id(2)

        @pl.when(j == 0)
        def _():
            m_sc[...] = jnp.full_like(m_sc, NEG)
            l_sc[...] = jnp.zeros_like(l_sc)
            acc_sc[...] = jnp.zeros_like(acc_sc)

        def step(diagonal):
            for hh in range(MLA_PACK):
                sl = slice(hh * SLOT, (hh + 1) * SLOT)
                s = _nt(q_ref[:, sl], k_ref[:, sl]) * scale
                if diagonal:
                    s = jnp.where(_causal(t), s, NEG)
                m_prev = m_sc[hh]
                m_new = jnp.maximum(m_prev, jnp.max(s, axis=1, keepdims=True))
                p = jnp.exp(s - m_new)
                alpha = jnp.exp(m_prev - m_new)
                l_new = alpha * l_sc[hh] + jnp.sum(p, axis=1, keepdims=True)
                acc = alpha * acc_sc[:, sl] + _nn(p.astype(BF16), v_ref[:, sl])
                if diagonal:
                    o_ref[:, sl] = (acc / l_new).astype(o_ref.dtype)
                    lse_ref[:, sl] = jnp.broadcast_to(m_new + jnp.log(l_new), (t, SLOT))
                else:
                    m_sc[hh] = m_new
                    l_sc[hh] = l_new
                    acc_sc[:, sl] = acc

        @pl.when(j < i)
        def _():
            step(False)

        @pl.when(j == i)
        def _():
            step(True)

    q_spec = pl.BlockSpec((t, wide), lambda h, i, j: (i, h))
    kv_spec = pl.BlockSpec((t, wide), lambda h, i, j: (jnp.minimum(j, i), h))
    return pl.pallas_call(
        body, name="mla_attn_fwd", grid=(MLA_HEADS // MLA_PACK, nt, nt),
        in_specs=[q_spec, kv_spec, kv_spec], out_specs=[q_spec, q_spec],
        out_shape=[jax.ShapeDtypeStruct(q.shape, BF16), jax.ShapeDtypeStruct(q.shape, F32)],
        scratch_shapes=[pltpu.VMEM((MLA_PACK, t, 1), F32), pltpu.VMEM((MLA_PACK, t, 1), F32),
                        pltpu.VMEM((t, wide), F32)],
        compiler_params=_cparams(),
    )(q, k, v)


def _mla_attn_bwd(q, k, v, o, do, lse):
    rows = q.shape[0]
    t = min(TQ_MLA, rows)
    nt = rows // t
    scale = MLA_QK ** -0.5
    wide = MLA_PACK * SLOT

    def body(q_ref, k_ref, v_ref, o_ref, do_ref, lse_ref, dq_ref, dk_ref, dv_ref, dk_sc, dv_sc):
        j, i = pl.program_id(1), pl.program_id(2)

        @pl.when((j == 0) & (i == 0))
        def _():
            dq_ref[...] = jnp.zeros_like(dq_ref)

        @pl.when(i == 0)
        def _():
            dk_sc[...] = jnp.zeros_like(dk_sc)
            dv_sc[...] = jnp.zeros_like(dv_sc)

        def step(diagonal):
            r0 = pl.multiple_of(i * t, t)
            for hh in range(MLA_PACK):
                sl = slice(hh * SLOT, (hh + 1) * SLOT)
                qv, kv, dov = q_ref[:, sl], k_ref[:, sl], do_ref[:, sl]
                s = _nt(qv, kv) * scale
                if diagonal:
                    s = jnp.where(_causal(t), s, NEG)
                p = jnp.exp(s - lse_ref[:, hh * SLOT:hh * SLOT + 1])
                delta = jnp.sum(dov.astype(F32) * o_ref[:, sl].astype(F32), axis=1, keepdims=True)
                dp = _nt(dov, v_ref[:, sl])
                ds = (p * (dp - delta) * scale).astype(BF16)
                dv_sc[:, sl] += _tn(p.astype(BF16), dov)
                dk_sc[:, sl] += _tn(ds, qv)
                dq_ref[pl.ds(r0, t), sl] += _nn(ds, kv)

        @pl.when(i > j)
        def _():
            step(False)

        @pl.when(i == j)
        def _():
            step(True)

        @pl.when(i == nt - 1)
        def _():
            dk_ref[...] = dk_sc[...]
            dv_ref[...] = dv_sc[...]

    q_spec = pl.BlockSpec((t, wide), lambda h, j, i: (jnp.maximum(i, j), h))
    kv_spec = pl.BlockSpec((t, wide), lambda h, j, i: (j, h))
    head_spec = pl.BlockSpec((rows, wide), lambda h, j, i: (0, h))
    shp = jax.ShapeDtypeStruct(q.shape, F32)
    return pl.pallas_call(
        body, name="mla_attn_bwd", grid=(MLA_HEADS // MLA_PACK, nt, nt),
        in_specs=[q_spec, kv_spec, kv_spec, q_spec, q_spec, q_spec],
        out_specs=[head_spec, kv_spec, kv_spec], out_shape=[shp, shp, shp],
        scratch_shapes=[pltpu.VMEM((t, wide), F32), pltpu.VMEM((t, wide), F32)],
        compiler_params=_cparams(),
    )(q, k, v, o, do, lse)


def _swa_specs(t):
    def prev(i):
        return jnp.maximum(i - 1, 0)
    q3 = pl.BlockSpec((t, SWA_GROUP * SLOT), lambda h, i: (i, h))
    kp = pl.BlockSpec((t, SLOT), lambda h, i: (prev(i), SWA_HEADS + h))
    kc = pl.BlockSpec((t, SLOT), lambda h, i: (i, SWA_HEADS + h))
    vp = pl.BlockSpec((t, SLOT), lambda h, i: (prev(i), SWA_HEADS + SWA_KV_HEADS + h))
    vc = pl.BlockSpec((t, SLOT), lambda h, i: (i, SWA_HEADS + SWA_KV_HEADS + h))
    pcol = pl.BlockSpec((t, 1), lambda h, i: (i, 0))
    prow_p = pl.BlockSpec((1, t), lambda h, i: (0, prev(i)))
    prow_c = pl.BlockSpec((1, t), lambda h, i: (0, i))
    return [q3, kp, kc, vp, vc, pcol, prow_p, prow_c]


def _stack(ref):
    return jnp.concatenate([ref[:, g * SLOT:(g + 1) * SLOT] for g in range(SWA_GROUP)], axis=0)


def _swa_logits(q3, kp, kc, pq, pkp, pkc, slope_ref, kvh, i, t):
    r = lax.broadcasted_iota(jnp.int32, (t, t), 0)
    c = lax.broadcasted_iota(jnp.int32, (t, t), 1)
    ok_c = c <= r
    ok_p = (c - r) > jnp.where(i > 0, 0, t)
    dist_p, dist_c = pq - pkp, pq - pkc
    s_p3 = _nt(q3, kp) * (HEAD_DIM ** -0.5)
    s_c3 = _nt(q3, kc) * (HEAD_DIM ** -0.5)
    out = []
    for g in range(SWA_GROUP):
        slope = slope_ref[kvh * SWA_GROUP + g]
        rows = slice(g * t, (g + 1) * t)
        out.append((jnp.where(ok_p, s_p3[rows] - slope * dist_p, NEG),
                    jnp.where(ok_c, s_c3[rows] - slope * dist_c, NEG)))
    return out


def _swa_attn_fwd(proj, pos_col, pos_row, slopes, sinks):
    rows = proj.shape[0]
    t = WINDOW
    hw = SWA_HEADS * SLOT

    def body(slope_ref, sink_ref, q_ref, kp_ref, kc_ref, vp_ref, vc_ref, pq_ref, pkp_ref, pkc_ref, o_ref, lse_ref):
        kvh, i = pl.program_id(0), pl.program_id(1)
        logits = _swa_logits(_stack(q_ref), kp_ref[...], kc_ref[...], pq_ref[...], pkp_ref[...], pkc_ref[...],
                             slope_ref, kvh, i, t)
        e_p, e_c, norm = [], [], []
        for g, (s_p, s_c) in enumerate(logits):
            sink = sink_ref[kvh * SWA_GROUP + g]
            m = jnp.maximum(jnp.maximum(jnp.max(s_p, axis=1, keepdims=True), jnp.max(s_c, axis=1, keepdims=True)),
                            sink)
            ep, ec = jnp.exp(s_p - m), jnp.exp(s_c - m)
            l = jnp.sum(ep, axis=1, keepdims=True) + jnp.sum(ec, axis=1, keepdims=True) + jnp.exp(sink - m)
            e_p.append(ep.astype(BF16))
            e_c.append(ec.astype(BF16))
            norm.append(l)
            lse_ref[:, g * SLOT:(g + 1) * SLOT] = jnp.broadcast_to(m + jnp.log(l), (t, SLOT))
        acc = _nn(jnp.concatenate(e_p, axis=0), vp_ref[...]) + _nn(jnp.concatenate(e_c, axis=0), vc_ref[...])
        for g in range(SWA_GROUP):
            o_ref[:, g * SLOT:(g + 1) * SLOT] = (acc[g * t:(g + 1) * t] / norm[g]).astype(o_ref.dtype)

    smem = pl.BlockSpec(memory_space=pltpu.SMEM)
    out_spec = pl.BlockSpec((t, SWA_GROUP * SLOT), lambda h, i: (i, h))
    return pl.pallas_call(
        body, name="swa_attn_fwd", grid=(SWA_KV_HEADS, rows // t),
        in_specs=[smem, smem] + _swa_specs(t), out_specs=[out_spec, out_spec],
        out_shape=[jax.ShapeDtypeStruct((rows, hw), BF16), jax.ShapeDtypeStruct((rows, hw), F32)],
        compiler_params=_cparams(),
    )(slopes, sinks, proj, proj, proj, proj, proj, pos_col, pos_row, pos_row)


def _swa_attn_bwd(proj, o, do, lse, pos_col, pos_row, slopes, sinks):
    rows = proj.shape[0]
    t = WINDOW
    hw = SWA_HEADS * SLOT
    scale = HEAD_DIM ** -0.5

    def body(slope_ref, sink_ref, q_ref, kp_ref, kc_ref, vp_ref, vc_ref, pq_ref, pkp_ref, pkc_ref,
             o_ref, do_ref, lse_ref, dq_ref, dk_ref, dv_ref, dsink_ref):
        kvh, i = pl.program_id(0), pl.program_id(1)

        @pl.when(i == 0)
        def _():
            dk_ref[...] = jnp.zeros_like(dk_ref)
            dv_ref[...] = jnp.zeros_like(dv_ref)
            dsink_ref[...] = jnp.zeros_like(dsink_ref)

        q3, do3 = _stack(q_ref), _stack(do_ref)
        logits = _swa_logits(q3, kp_ref[...], kc_ref[...], pq_ref[...], pkp_ref[...], pkc_ref[...],
                             slope_ref, kvh, i, t)
        dp_p3, dp_c3 = _nt(do3, vp_ref[...]), _nt(do3, vc_ref[...])
        p_p, p_c, ds_p, ds_c = [], [], [], []
        for g, (s_p, s_c) in enumerate(logits):
            sl = slice(g * SLOT, (g + 1) * SLOT)
            rws = slice(g * t, (g + 1) * t)
            lse_g = lse_ref[:, g * SLOT:g * SLOT + 1]
            pp, pc = jnp.exp(s_p - lse_g), jnp.exp(s_c - lse_g)
            delta = jnp.sum(do_ref[:, sl].astype(F32) * o_ref[:, sl].astype(F32), axis=1, keepdims=True)
            p_p.append(pp.astype(BF16))
            p_c.append(pc.astype(BF16))
            ds_p.append((pp * (dp_p3[rws] - delta)).astype(BF16))
            ds_c.append((pc * (dp_c3[rws] - delta)).astype(BF16))
            sink = sink_ref[kvh * SWA_GROUP + g]
            dsink = -jnp.sum(jnp.exp(sink - lse_g) * delta, axis=0, keepdims=True)
            dsink_ref[g * 8:(g + 1) * 8, :] += jnp.broadcast_to(dsink, (8, SLOT))
        p_p3, p_c3 = jnp.concatenate(p_p, axis=0), jnp.concatenate(p_c, axis=0)
        ds_p3, ds_c3 = jnp.concatenate(ds_p, axis=0), jnp.concatenate(ds_c, axis=0)
        dq3 = (_nn(ds_p3, kp_ref[...]) + _nn(ds_c3, kc_ref[...])) * scale
        for g in range(SWA_GROUP):
            dq_ref[:, g * SLOT:(g + 1) * SLOT] = dq3[g * t:(g + 1) * t]
        r_c = pl.multiple_of(i * t, t)
        dk_ref[pl.ds(r_c, t), :] += _tn(ds_c3, q3) * scale
        dv_ref[pl.ds(r_c, t), :] += _tn(p_c3, do3)

        @pl.when(i > 0)
        def _():
            r_p = pl.multiple_of((i - 1) * t, t)
            dk_ref[pl.ds(r_p, t), :] += _tn(ds_p3, q3) * scale
            dv_ref[pl.ds(r_p, t), :] += _tn(p_p3, do3)

    smem = pl.BlockSpec(memory_space=pltpu.SMEM)
    qlike = pl.BlockSpec((t, SWA_GROUP * SLOT), lambda h, i: (i, h))
    kv_out = pl.BlockSpec((rows, SLOT), lambda h, i: (0, h))
    return pl.pallas_call(
        body, name="swa_attn_bwd", grid=(SWA_KV_HEADS, rows // t),
        in_specs=[smem, smem] + _swa_specs(t) + [qlike, qlike, qlike],
        out_specs=[qlike, kv_out, kv_out, pl.BlockSpec((SWA_GROUP * 8, SLOT), lambda h, i: (h, 0))],
        out_shape=[jax.ShapeDtypeStruct((rows, hw), F32), jax.ShapeDtypeStruct((rows, SWA_KV_HEADS * SLOT), F32),
                   jax.ShapeDtypeStruct((rows, SWA_KV_HEADS * SLOT), F32),
                   jax.ShapeDtypeStruct((SWA_HEADS * 8, SLOT), F32)],
        compiler_params=_cparams(),
    )(slopes, sinks, proj, proj, proj, proj, proj, pos_col, pos_row, pos_row, o, do, lse)


def _cross_attn_fwd(proj, qoff, kvmem):
    rows = proj.shape[0]
    t = min(TQ_CROSS, rows)

    def body(q_ref, k_ref, v_ref, o_ref):
        s = _nt(q_ref[...].astype(BF16), k_ref[...]) * (HEAD_DIM ** -0.5)
        e = jnp.exp(s - jnp.max(s, axis=1, keepdims=True))
        p = e / jnp.sum(e, axis=1, keepdims=True)
        o_ref[...] = _nn(p.astype(BF16), v_ref[...]).astype(o_ref.dtype)

    return pl.pallas_call(
        body, name="cross_attn_fwd", grid=(rows // t, MEM_HEADS),
        in_specs=[pl.BlockSpec((t, SLOT), lambda i, h: (i, qoff + h)),
                  pl.BlockSpec((N_MEM, SLOT), lambda i, h: (0, h)),
                  pl.BlockSpec((N_MEM, SLOT), lambda i, h: (0, MEM_HEADS + h))],
        out_specs=pl.BlockSpec((t, SLOT), lambda i, h: (i, h)),
        out_shape=jax.ShapeDtypeStruct((rows, MEM_HEADS * SLOT), BF16), compiler_params=_cparams(),
    )(proj, kvmem, kvmem)


def _cross_attn_bwd(proj, qoff, kvmem, do):
    rows = proj.shape[0]
    t = min(TQ_CROSS, rows)
    scale = HEAD_DIM ** -0.5

    def body(q_ref, k_ref, v_ref, do_ref, dq_ref, dk_ref, dv_ref):
        @pl.when(pl.program_id(1) == 0)
        def _():
            dk_ref[...] = jnp.zeros_like(dk_ref)
            dv_ref[...] = jnp.zeros_like(dv_ref)

        qv, kv, dov = q_ref[...].astype(BF16), k_ref[...], do_ref[...]
        s = _nt(qv, kv) * scale
        e = jnp.exp(s - jnp.max(s, axis=1, keepdims=True))
        p = e / jnp.sum(e, axis=1, keepdims=True)
        dp = _nt(dov, v_ref[...])
        ds = (p * (dp - jnp.sum(p * dp, axis=1, keepdims=True))).astype(BF16)
        dq_ref[...] = _nn(ds, kv) * scale
        dk_ref[...] += _tn(ds, qv) * scale
        dv_ref[...] += _tn(p.astype(BF16), dov)

    mem_out = pl.BlockSpec((N_MEM, SLOT), lambda h, i: (0, h))
    return pl.pallas_call(
        body, name="cross_attn_bwd", grid=(MEM_HEADS, rows // t),
        in_specs=[pl.BlockSpec((t, SLOT), lambda h, i: (i, qoff + h)),
                  pl.BlockSpec((N_MEM, SLOT), lambda h, i: (0, h)),
                  pl.BlockSpec((N_MEM, SLOT), lambda h, i: (0, MEM_HEADS + h)),
                  pl.BlockSpec((t, SLOT), lambda h, i: (i, h))],
        out_specs=[pl.BlockSpec((t, SLOT), lambda h, i: (i, h)), mem_out, mem_out],
        out_shape=[jax.ShapeDtypeStruct((rows, MEM_HEADS * SLOT), F32),
                   jax.ShapeDtypeStruct((N_MEM, MEM_HEADS * SLOT), F32),
                   jax.ShapeDtypeStruct((N_MEM, MEM_HEADS * SLOT), F32)],
        compiler_params=_cparams(),
    )(proj, kvmem, kvmem, do)


def _place():
    return lax.axis_index("x"), lax.axis_index("y"), lax.axis_index("c")


def _flip(v, bit):
    return 1 - v if bit else v


def _all_gather(blocks, name):
    nb = len(blocks)

    def body(*refs):
        x_refs, out_refs = refs[:nb], refs[nb:2 * nb]
        send_sems, recv_sems, local_sems = refs[2 * nb:]
        x, y, c = _place()
        me, sibling = (x, y, c), (x, y, 1 - c)
        chips = [(1 - x, y), (x, 1 - y), (1 - x, 1 - y)]

        def copy(b, k, blk, to, from_input=False):
            slot = out_refs[b].at[4 * blk[0] + 2 * blk[1] + blk[2]]
            return pltpu.make_async_remote_copy(
                src_ref=x_refs[b] if from_input else slot, dst_ref=slot,
                send_sem=send_sems.at[b, k], recv_sem=recv_sems.at[b, k],
                device_id=to, device_id_type=pl.DeviceIdType.MESH)

        mine = [pltpu.make_async_copy(x_refs[b], out_refs[b].at[4 * x + 2 * y + c], local_sems.at[b])
                for b in range(nb)]
        for cp in mine:
            cp.start()
        first = []
        for b in range(nb):
            first.append(copy(b, 0, me, sibling, from_input=True))
            first += [copy(b, 1 + n, me, (*chip, c), from_input=True) for n, chip in enumerate(chips)]
        for cp in first:
            cp.start()
        passed = []
        for n, chip in enumerate(chips):
            for b in range(nb):
                copy(b, 1 + n, (*chip, c), me).wait_recv()
                passed.append(copy(b, 4 + n, (*chip, c), sibling))
                passed[-1].start()
        for b in range(nb):
            copy(b, 0, sibling, me).wait_recv()
            for n, chip in enumerate(chips):
                copy(b, 4 + n, (*chip, 1 - c), me).wait_recv()
        for cp in first + passed:
            cp.wait_send()
        for cp in mine:
            cp.wait()

    any_spec = pl.BlockSpec(memory_space=pl.ANY)
    return pl.pallas_call(
        body, name=name, in_specs=[any_spec] * nb, out_specs=[any_spec] * nb,
        out_shape=[jax.ShapeDtypeStruct((N_DEV,) + blk.shape, blk.dtype) for blk in blocks],
        scratch_shapes=[pltpu.SemaphoreType.DMA((nb, 7)), pltpu.SemaphoreType.DMA((nb, 7)),
                        pltpu.SemaphoreType.DMA((nb,))],
    )(*blocks)


def _all_to_all(arrays, name):
    na = len(arrays)

    def body(*refs):
        g_refs, out_refs = refs[:na], refs[na:2 * na]
        send_sems, recv_sems, local_sems = refs[2 * na:]
        x, y, c = _place()
        my_idx = 4 * x + 2 * y + c
        mine = [pltpu.make_async_copy(g_refs[a].at[my_idx], out_refs[a].at[my_idx], local_sems.at[a])
                for a in range(na)]
        for cp in mine:
            cp.start()
        sends, arrivals = [], []
        for n in range(1, N_DEV):
            peer = (_flip(x, n & 4), _flip(y, n & 2), _flip(c, n & 1))
            peer_idx = 4 * peer[0] + 2 * peer[1] + peer[2]
            for a in range(na):
                sends.append(pltpu.make_async_remote_copy(
                    src_ref=g_refs[a].at[peer_idx], dst_ref=out_refs[a].at[my_idx],
                    send_sem=send_sems.at[a, n - 1], recv_sem=recv_sems.at[a, n - 1],
                    device_id=peer, device_id_type=pl.DeviceIdType.MESH))
                arrivals.append(pltpu.make_async_remote_copy(
                    src_ref=g_refs[a].at[my_idx], dst_ref=out_refs[a].at[peer_idx],
                    send_sem=send_sems.at[a, n - 1], recv_sem=recv_sems.at[a, n - 1],
                    device_id=peer, device_id_type=pl.DeviceIdType.MESH))
        for cp in sends:
            cp.start()
        for cp in arrivals:
            cp.wait_recv()
        for cp in sends:
            cp.wait_send()
        for cp in mine:
            cp.wait()

    any_spec = pl.BlockSpec(memory_space=pl.ANY)
    return pl.pallas_call(
        body, name=name, in_specs=[any_spec] * na, out_specs=[any_spec] * na,
        out_shape=[jax.ShapeDtypeStruct(arr.shape, arr.dtype) for arr in arrays],
        scratch_shapes=[pltpu.SemaphoreType.DMA((na, 7)), pltpu.SemaphoreType.DMA((na, 7)),
                        pltpu.SemaphoreType.DMA((na,))],
    )(*arrays)


def _adamw(parts, w, m, v, name):
    lyr, rows, cols = w.shape
    tr = ADAM_ROWS if cols > 512 else 2 * ADAM_ROWS
    while rows % tr:
        tr //= 2
    tr = min(tr, rows)

    def body(p_ref, w_ref, m_ref, v_ref, g_out, d_out, m_out, v_out):
        g = p_ref[0].astype(F32)
        for s in range(1, N_DEV):
            g = g + p_ref[s].astype(F32)
        m2 = ADAM_B1 * m_ref[...] + (1.0 - ADAM_B1) * g
        v2 = ADAM_B2 * v_ref[...] + (1.0 - ADAM_B2) * (g * g)
        m_hat = m2 / (1.0 - ADAM_B1 ** ADAM_STEP)
        v_hat = v2 / (1.0 - ADAM_B2 ** ADAM_STEP)
        g_out[...] = g
        d_out[...] = -ADAM_LR * (m_hat / (jnp.sqrt(v_hat) + ADAM_EPS) + ADAM_WD * w_ref[...])
        m_out[...] = m2
        v_out[...] = v2

    spec = pl.BlockSpec((None, tr, cols), lambda l, i: (l, i, 0))
    shp = jax.ShapeDtypeStruct((lyr, rows, cols), F32)
    return pl.pallas_call(
        body, name=name, grid=(lyr, rows // tr),
        in_specs=[pl.BlockSpec((None, N_DEV, tr, cols), lambda l, i: (l, 0, i, 0)), spec, spec, spec],
        out_specs=[spec] * 4, out_shape=[shp] * 4, compiler_params=_cparams(),
    )(parts, w, m, v)


def _pack(arrays, lanes, row_mult, dtype):
    flat = jnp.concatenate([a.reshape(-1).astype(dtype) for a in arrays])
    unit = lanes * row_mult
    total = -(-flat.shape[0] // unit) * unit
    return jnp.pad(flat, (0, total - flat.shape[0])).reshape(total // lanes, lanes)


def _unpack(packed, shapes):
    flat = packed.reshape(-1)
    out, off = [], 0
    for shp in shapes:
        n = 1
        for d in shp:
            n *= d
        out.append(flat[off:off + n].reshape(shp))
        off += n
    return out


def _pad_slots(w, axis):
    axis = axis % w.ndim
    n = w.shape[axis] // HEAD_DIM
    shp = w.shape[:axis] + (n, HEAD_DIM) + w.shape[axis + 1:]
    pad = [(0, 0)] * (w.ndim + 1)
    pad[axis + 1] = (0, SLOT - HEAD_DIM)
    return jnp.pad(w.reshape(shp), pad).reshape(w.shape[:axis] + (n * SLOT,) + w.shape[axis + 1:])


def _unpad_slots(w, axis, keep=HEAD_DIM):
    axis = axis % w.ndim
    n = w.shape[axis] // SLOT
    shp = w.shape[:axis] + (n, SLOT) + w.shape[axis + 1:]
    idx = [slice(None)] * (w.ndim + 1)
    idx[axis + 1] = slice(0, keep)
    return w.reshape(shp)[tuple(idx)].reshape(w.shape[:axis] + (n * keep,) + w.shape[axis + 1:])


def _mla_in_pad(w):
    z = functools.partial(jnp.zeros, dtype=w.dtype)
    rows = w.shape[0]
    return jnp.concatenate([w[:, :384], z((rows, 64)), w[:, 640:672], z((rows, 32)), w[:, 384:640],
                            _pad_slots(w[:, 672:], 1)], axis=1)


def _mla_in_unpad(d):
    return jnp.concatenate([d[:, :384], d[:, 512:768], d[:, 448:480], _unpad_slots(d[:, 768:], 1)], axis=1)


def _mla_uq_pad(w):
    return jnp.pad(w.reshape(w.shape[0], MLA_HEADS, MLA_QK), ((0, 0), (0, 0), (0, SLOT - MLA_QK))).reshape(
        w.shape[0], MLA_HEADS * SLOT)


def _mla_ukv_pad(w):
    w3 = w.reshape(w.shape[0], MLA_HEADS, 2 * HEAD_DIM)
    pad = ((0, 0), (0, 0), (0, SLOT - HEAD_DIM))
    k = jnp.pad(w3[:, :, :HEAD_DIM], pad).reshape(w.shape[0], -1)
    v = jnp.pad(w3[:, :, HEAD_DIM:], pad).reshape(w.shape[0], -1)
    return jnp.concatenate([k, v], axis=1)


def _mla_ukv_unpad(d):
    hw = MLA_HEADS * SLOT
    k = d[:, :hw].reshape(d.shape[0], MLA_HEADS, SLOT)[:, :, :HEAD_DIM]
    v = d[:, hw:].reshape(d.shape[0], MLA_HEADS, SLOT)[:, :, :HEAD_DIM]
    return jnp.concatenate([k, v], axis=2).reshape(d.shape[0], MLA_HEADS * 2 * HEAD_DIM)


def _join(gathered, axis):
    nd, a, b = gathered.shape
    if axis == 1:
        return gathered.reshape(nd * a, b)
    return gathered.transpose(1, 0, 2).reshape(a, nd * b)


def _split(full, axis):
    r, c = full.shape
    if axis == 1:
        return full.reshape(N_DEV, r // N_DEV, c).astype(BF16)
    return full.reshape(r, N_DEV, c // N_DEV).transpose(1, 0, 2).astype(BF16)


def kernel(x, mem, positions, attn_norm_g, mlp_norm_g, mem_norm_g, final_norm_g, mla_w_in, mla_q_norm_g, mla_kv_norm_g, mla_w_uq, mla_w_ukv, swa_w_in, swa_sinks, w_mem_kv, w_o, mlp_w_up, mlp_w_down, loss_target, m_attn_norm_g, m_mlp_norm_g, m_mem_norm_g, m_final_norm_g, m_mla_w_in, m_mla_q_norm_g, m_mla_kv_norm_g, m_mla_w_uq, m_mla_w_ukv, m_swa_w_in, m_swa_sinks, m_w_mem_kv, m_w_o, m_mlp_w_up, m_mlp_w_down, v_attn_norm_g, v_mlp_norm_g, v_mem_norm_g, v_final_norm_g, v_mla_w_in, v_mla_q_norm_g, v_mla_kv_norm_g, v_mla_w_uq, v_mla_w_ukv, v_swa_w_in, v_swa_sinks, v_w_mem_kv, v_w_o, v_mlp_w_up, v_mlp_w_down):
    given = dict(locals())
    seq = x.shape[1]
    x0 = x.reshape(seq, D_MODEL)
    tgt = loss_target.reshape(seq, D_MODEL)
    mem0 = mem.reshape(N_MEM, D_MODEL)
    pos = positions.reshape(seq).astype(F32)
    pos_col, pos_row = pos.reshape(seq, 1), pos.reshape(1, seq)

    def layer_names(i):
        mixer = ("mla_w_in", "mla_w_uq", "mla_w_ukv") if i % 2 == 0 else ("swa_w_in",)
        return [(n, i // 2) for n in mixer] + [(n, i) for n in ("w_mem_kv", "w_o", "mlp_w_up", "mlp_w_down")]

    weights = []
    for i in range(DEPTH):
        names = layer_names(i)
        got = _all_gather([given[n][l].astype(BF16) for n, l in names],
                          "gather_weights_mla" if i % 2 == 0 else "gather_weights_swa")
        weights.append({n: g for (n, _), g in zip(names, got)})

    consts = _lane_consts()
    tabs = _rope_tables(pos_col, consts)
    slopes = 2.0 ** (-8.0 * (jnp.arange(SWA_HEADS, dtype=F32) + 1.0) / SWA_HEADS)

    mem_n = _rmsnorm_fwd(mem0, 0, D_MODEL, mem_norm_g, "rmsnorm_fwd_mem")

    saved = []
    xc = x0
    for i in range(DEPTH):
        j = i // 2
        wts = weights[i]
        s = {"x_in": xc}
        hn = _rmsnorm_fwd(xc, 0, D_MODEL, attn_norm_g[i], "rmsnorm_fwd")
        if i % 2 == 0:
            w_in = _mla_in_pad(_join(wts["mla_w_in"], 1))
            w_uq = _mla_uq_pad(_join(wts["mla_w_uq"], 2))
            w_kv = _mla_ukv_pad(_join(wts["mla_w_ukv"], 2))
            proj = _mm(hn, w_in, "nn", F32, "mm_mla_in")
            cqn = _rmsnorm_fwd(proj, 0, MLA_Q_RANK, mla_q_norm_g[j], "rmsnorm_fwd_q")
            ckvn = _rmsnorm_fwd(proj, 2, MLA_KV_RANK, mla_kv_norm_g[j], "rmsnorm_fwd_kv")
            qraw = _mm(cqn, w_uq, "nn", F32, "mm_mla_uq")
            kvraw = _mm(ckvn, w_kv, "nn", F32, "mm_mla_ukv")
            q, k, v = _mla_rope_fwd(qraw, kvraw, proj, tabs)
            o, lse = _mla_attn_fwd(q, k, v)
            qoff = MLA_QOFF
            s.update(w_uq=w_uq, w_kv=w_kv, cqn=cqn, ckvn=ckvn, q=q, k=k, v=v)
        else:
            w_in = _pad_slots(_join(wts["swa_w_in"], 2), 1)
            proj = _mm(hn, w_in, "nn", BF16, "mm_swa_in")
            o, lse = _swa_attn_fwd(proj, pos_col, pos_row, slopes, swa_sinks[j])
            qoff = SWA_QOFF
        w_mem = _pad_slots(_join(wts["w_mem_kv"], 1), 1)
        w_out = _pad_slots(_join(wts["w_o"], 1), 0)
        w_o_mix, w_o_cross = w_out[:SWA_HEADS * SLOT], w_out[SWA_HEADS * SLOT:]
        kvmem = _mm(mem_n, w_mem, "nn", BF16, "mm_mem_kv")
        cross = _cross_attn_fwd(proj, qoff, kvmem)
        x1 = _mm(o, w_o_mix, "nn", F32, "mm_o_mix", res=xc)
        x1 = _mm(cross, w_o_cross, "nn", F32, "mm_o_cross", res=x1)
        hn2 = _rmsnorm_fwd(x1, 0, D_MODEL, mlp_norm_g[i], "rmsnorm_fwd")
        act, act2 = _mm(hn2, wts["mlp_w_up"], "nn", BF16, "mm_mlp_up", epi="relu2", b_blk="cols")
        xc = _mm(act2, wts["mlp_w_down"], "nn", F32, "mm_mlp_down", res=x1, b_blk="rows")
        s.update(hn=hn, w_in=w_in, proj=proj, o=o, lse=lse, qoff=qoff, w_mem=w_mem, w_o_mix=w_o_mix,
                 w_o_cross=w_o_cross, kvmem=kvmem, cross=cross, x1=x1, hn2=hn2, act=act, act2=act2)
        saved.append(s)

    dx, dg_final, loss_part = _loss_head(xc, final_norm_g, tgt)
    loss = lax.psum(loss_part[0, 0], MESH_AXES)

    gains = {n: [None] * DEPTH for n in ("attn_norm_g", "mlp_norm_g")}
    for n in ("mla_q_norm_g", "mla_kv_norm_g", "swa_sinks"):
        gains[n] = [None] * 2
    received = {}
    dmem_n = None
    for i in reversed(range(DEPTH)):
        j = i // 2
        s = saved[i]
        wts = weights[i]
        out = {}
        du = _mm(dx, wts["mlp_w_down"], "nt", BF16, "mm_mlp_down_dx", aux=s["act"], epi="mul2aux", b_blk="rows")
        out["mlp_w_down"] = _mm(s["act2"], dx, "tn", BF16, "mm_mlp_down_dw", o_blk="rows")
        dhn2 = _mm(du, wts["mlp_w_up"], "nt", F32, "mm_mlp_up_dx", b_blk="cols")
        out["mlp_w_up"] = _mm(s["hn2"], du, "tn", BF16, "mm_mlp_up_dw", o_blk="cols")
        dx1, dg = _rmsnorm_bwd(s["x1"], 0, D_MODEL, mlp_norm_g[i], dhn2, dx, F32, "rmsnorm_bwd")
        gains["mlp_norm_g"][i] = dg[0]

        do = _mm(dx1, s["w_o_mix"], "nt", BF16, "mm_o_mix_dx")
        dcross = _mm(dx1, s["w_o_cross"], "nt", BF16, "mm_o_cross_dx")
        dw_o = jnp.concatenate([_mm(s["o"], dx1, "tn", F32, "mm_o_mix_dw"),
                                _mm(s["cross"], dx1, "tn", F32, "mm_o_cross_dw")], axis=0)
        out["w_o"] = _split(_unpad_slots(dw_o, 0), 1)
        dqc, dkm, dvm = _cross_attn_bwd(s["proj"], s["qoff"], s["kvmem"], dcross)
        dkvmem = jnp.concatenate([dkm, dvm], axis=1).astype(BF16)
        out["w_mem_kv"] = _split(_unpad_slots(_mm(mem_n, dkvmem, "tn", F32, "mm_mem_kv_dw"), 1), 1)
        dmem_n = _mm(dkvmem, s["w_mem"], "nt", F32, "mm_mem_kv_dx" if dmem_n is None else "mm_mem_kv_dx_acc",
                     res=dmem_n)

        if i % 2 == 0:
            dq, dk, dv = _mla_attn_bwd(s["q"], s["k"], s["v"], s["o"], do, s["lse"])
            dqraw, dkv, dkr = _mla_rope_bwd(dq, dk, dv, tabs, consts)
            dcqn = _mm(dqraw, s["w_uq"], "nt", F32, "mm_mla_uq_dx")
            out["mla_w_uq"] = _split(_unpad_slots(_mm(s["cqn"], dqraw, "tn", F32, "mm_mla_uq_dw"), 1, MLA_QK), 2)
            dckvn = _mm(dkv, s["w_kv"], "nt", F32, "mm_mla_ukv_dx")
            out["mla_w_ukv"] = _split(_mla_ukv_unpad(_mm(s["ckvn"], dkv, "tn", F32, "mm_mla_ukv_dw")), 2)
            dcq, dg = _rmsnorm_bwd(s["proj"], 0, MLA_Q_RANK, mla_q_norm_g[j], dcqn, None, BF16, "rmsnorm_bwd_q")
            gains["mla_q_norm_g"][j] = dg[0]
            dckv, dg = _rmsnorm_bwd(s["proj"], 2, MLA_KV_RANK, mla_kv_norm_g[j], dckvn, None, BF16, "rmsnorm_bwd_kv")
            gains["mla_kv_norm_g"][j] = dg[0]
            dproj = jnp.concatenate([dcq, dkr.astype(BF16), dckv, dqc.astype(BF16)], axis=1)
            dhn = _mm(dproj, s["w_in"], "nt", F32, "mm_mla_in_dx")
            out["mla_w_in"] = _split(_mla_in_unpad(_mm(s["hn"], dproj, "tn", F32, "mm_mla_in_dw")), 1)
        else:
            dq, dk, dv, dsink = _swa_attn_bwd(s["proj"], s["o"], do, s["lse"], pos_col, pos_row, slopes, swa_sinks[j])
            gains["swa_sinks"][j] = dsink[::8, 0]
            dproj = jnp.concatenate([dq, dk, dv, dqc], axis=1).astype(BF16)
            dhn = _mm(dproj, s["w_in"], "nt", F32, "mm_swa_in_dx")
            out["swa_w_in"] = _split(_unpad_slots(_mm(s["hn"], dproj, "tn", F32, "mm_swa_in_dw"), 1), 2)
        dx, dg = _rmsnorm_bwd(s["x_in"], 0, D_MODEL, attn_norm_g[i], dhn, dx1, F32, "rmsnorm_bwd")
        gains["attn_norm_g"][i] = dg[0]

        names = layer_names(i)
        got = _all_to_all([out[n] for n, _ in names], "exchange_grads_mla" if i % 2 == 0 else "exchange_grads_swa")
        for (n, l), r in zip(names, got):
            received[(n, l)] = r

    _, dg_mem = _rmsnorm_bwd(mem0, 0, D_MODEL, mem_norm_g, dmem_n, None, BF16, "rmsnorm_bwd_mem")
    gains = {n: jnp.stack(g) for n, g in gains.items()}
    gains["mem_norm_g"] = dg_mem[0]
    gains["final_norm_g"] = dg_final[0]

    result = {}
    for n, _ in SHARDED:
        parts = jnp.stack([received[(n, l)] for l in range(given[n].shape[0])])
        for kind, r in enumerate(_adamw(parts, given[n], given["m_" + n], given["v_" + n], "adamw_" + n)):
            result[(kind, n)] = r

    rep_shapes = [given[n].shape for n in REPLICATED]
    rep_parts = _all_gather([_pack([gains[n] for n in REPLICATED], SLOT, 8, F32)], "gather_gain_grads")[0]
    rep_packed = [_pack([given[p + n] for n in REPLICATED], SLOT, 8, F32)[None] for p in ("", "m_", "v_")]
    for kind, r in enumerate(_adamw(rep_parts[None], *rep_packed, "adamw_gains")):
        for n, part in zip(REPLICATED, _unpack(r[0], rep_shapes)):
            result[(kind, n)] = part

    outs = [loss, dx.reshape(1, seq, D_MODEL)]
    for kind in range(4):
        outs += [result[(kind, n)] for n in WEIGHT_ORDER]
    return tuple(outs)
```

```python
import functools

import jax
import jax.numpy as jnp
from jax import lax
from jax.experimental import pallas as pl
from jax.experimental.pallas import tpu as pltpu

F32 = jnp.float32
BF16 = jnp.bfloat16

D_MODEL = 1024
D_FF = 4096
N_MEM = 256
DEPTH = 4
SLOT = 128
HEAD_DIM = 64
MLA_HEADS = 12
MLA_QK = 96
MLA_Q_RANK = 384
MLA_KV_RANK = 256
SWA_HEADS = 12
SWA_KV_HEADS = 4
SWA_GROUP = 3
MEM_HEADS = 4
WINDOW = 128
EPS = 1e-6
NEG = -1e30
ROPE_THETA = 10000.0
N_DEV = 8

ADAM_LR = 0.001
ADAM_B1 = 0.9
ADAM_B2 = 0.999
ADAM_EPS = 1e-08
ADAM_WD = 0.01
ADAM_STEP = 10

TM = 512
TM_MM = 1024
TQ_MLA = 512
MLA_PACK = 2
TQ_CROSS = 512
TN_CAP = 1024
TK_CAP = 1024
ADAM_ROWS = 128
VMEM_LIMIT = 48 * 1024 * 1024

MESH_AXES = ("x", "y", "c")

MLA_PAD_IN = 384 + SLOT + 256 + MEM_HEADS * SLOT
MLA_QOFF = (384 + SLOT + 256) // SLOT
SWA_PAD_IN = (SWA_HEADS + 2 * SWA_KV_HEADS + MEM_HEADS) * SLOT
SWA_QOFF = SWA_HEADS + 2 * SWA_KV_HEADS

SHARDED = (
    ("mla_w_in", 1), ("mla_w_uq", 2), ("mla_w_ukv", 2), ("swa_w_in", 2),
    ("w_mem_kv", 1), ("w_o", 1), ("mlp_w_up", 2), ("mlp_w_down", 1),
)
REPLICATED = ("attn_norm_g", "mlp_norm_g", "mem_norm_g", "final_norm_g",
              "mla_q_norm_g", "mla_kv_norm_g", "swa_sinks")
WEIGHT_ORDER = ("attn_norm_g", "mlp_norm_g", "mem_norm_g", "final_norm_g", "mla_w_in",
                "mla_q_norm_g", "mla_kv_norm_g", "mla_w_uq", "mla_w_ukv", "swa_w_in",
                "swa_sinks", "w_mem_kv", "w_o", "mlp_w_up", "mlp_w_down")


def _cparams():
    return pltpu.CompilerParams(vmem_limit_bytes=VMEM_LIMIT)


def _tile(n, cap):
    best = None
    t = SLOT
    while t <= min(n, cap):
        if n % t == 0:
            best = t
        t += SLOT
    return n if best is None else best


_DIMS = {"nn": (((1,), (0,)), ((), ())), "nt": (((1,), (1,)), ((), ())), "tn": (((0,), (0,)), ((), ()))}


def _mm(a, b, mode, out_dtype, name, res=None, aux=None, epi=None, b_blk=None, o_blk=None, after=None):
    if b_blk is not None:
        nb, br, bc = b.shape
        b_shape = (nb * br, bc) if b_blk == "rows" else (br, nb * bc)
    else:
        b_shape = b.shape
    if mode == "nn":
        (m, k), (k2, n) = a.shape, b_shape
    elif mode == "nt":
        (m, k), (n, k2) = a.shape, b_shape
    else:
        (k, m), (k2, n) = a.shape, b_shape
    assert k == k2, (a.shape, b_shape, mode)
    tm, tn, tk = _tile(m, TM_MM), _tile(n, TN_CAP), _tile(k, TK_CAP)
    if b_blk is not None:
        first_is_k = mode != "nt"
        if (b_blk == "rows") == first_is_k:
            tk = br if b_blk == "rows" else bc
        else:
            tn = br if b_blk == "rows" else bc
    if o_blk == "rows":
        tm = m // N_DEV
    elif o_blk == "cols":
        tn = n // N_DEV
    nk = k // tk
    dims = _DIMS[mode]
    if mode == "tn":
        a_spec = pl.BlockSpec((tk, tm), lambda i, j, kk: (kk, i))
    else:
        a_spec = pl.BlockSpec((tm, tk), lambda i, j, kk: (i, kk))
    if b_blk is None:
        if mode == "nt":
            b_spec = pl.BlockSpec((tn, tk), lambda i, j, kk: (j, kk))
        else:
            b_spec = pl.BlockSpec((tk, tn), lambda i, j, kk: (kk, j))
    elif mode == "nt":
        if b_blk == "rows":
            b_spec = pl.BlockSpec((None, tn, tk), lambda i, j, kk: (j, 0, kk))
        else:
            b_spec = pl.BlockSpec((None, tn, tk), lambda i, j, kk: (kk, j, 0))
    else:
        if b_blk == "rows":
            b_spec = pl.BlockSpec((None, tk, tn), lambda i, j, kk: (kk, 0, j))
        else:
            b_spec = pl.BlockSpec((None, tk, tn), lambda i, j, kk: (j, kk, 0))
    if o_blk is None:
        o_spec = pl.BlockSpec((tm, tn), lambda i, j, kk: (i, j))
        o_shape = (m, n)
    elif o_blk == "rows":
        o_spec = pl.BlockSpec((None, tm, tn), lambda i, j, kk: (i, 0, j))
        o_shape = (N_DEV, tm, n)
    else:
        o_spec = pl.BlockSpec((None, tm, tn), lambda i, j, kk: (j, i, 0))
        o_shape = (N_DEV, m, tn)
    has_res, has_aux = res is not None, aux is not None
    assert o_blk is None or not (has_res or has_aux)

    def body(*refs):
        a_ref, b_ref = refs[0], refs[1]
        pos = 2
        res_ref = aux_ref = None
        if has_res:
            res_ref = refs[pos]
            pos += 1
        if has_aux:
            aux_ref = refs[pos]
            pos += 1
        if after is not None:
            pos += 1
        outs = refs[pos:-1]
        acc = refs[-1]
        kk = pl.program_id(2)

        @pl.when(kk == 0)
        def _():
            acc[...] = jnp.zeros_like(acc)

        acc[...] += lax.dot_general(a_ref[...].astype(BF16), b_ref[...].astype(BF16), dims,
                                    preferred_element_type=F32)

        @pl.when(kk == nk - 1)
        def _():
            r = acc[...]
            if epi == "relu2":
                r = jnp.maximum(r, 0.0)
                outs[0][...] = r.astype(outs[0].dtype)
                outs[1][...] = (r * r).astype(outs[1].dtype)
            else:
                if epi == "mul2aux":
                    r = r * (2.0 * aux_ref[...].astype(F32))
                if has_res:
                    r = r + res_ref[...]
                outs[0][...] = r.astype(outs[0].dtype)

    in_specs = [a_spec, b_spec]
    args = [a, b]
    if has_res:
        in_specs.append(o_spec)
        args.append(res)
    if has_aux:
        in_specs.append(o_spec)
        args.append(aux)
    if after is not None:
        in_specs.append(pl.BlockSpec(memory_space=pl.ANY))
        args.append(after)
    n_out = 2 if epi == "relu2" else 1
    out_shape = [jax.ShapeDtypeStruct(o_shape, out_dtype)] * n_out
    out = pl.pallas_call(
        body, name=name, grid=(m // tm, n // tn, nk),
        in_specs=in_specs, out_specs=[o_spec] * n_out, out_shape=out_shape,
        scratch_shapes=[pltpu.VMEM((tm, tn), F32)], compiler_params=_cparams(),
    )(*args)
    return out if n_out == 2 else out[0]


def _rmsnorm_fwd(xarr, colblk, width, g, name, after=None):
    rows = xarr.shape[0]
    tm = min(TM, rows)

    def body(x_ref, g_ref, *rest):
        y_ref = rest[-1]
        x = x_ref[...].astype(F32)
        r = lax.rsqrt(jnp.mean(x * x, axis=1, keepdims=True) + EPS)
        y_ref[...] = (x * r * g_ref[...]).astype(y_ref.dtype)

    in_specs = [pl.BlockSpec((tm, width), lambda i: (i, colblk)), pl.BlockSpec((1, width), lambda i: (0, 0))]
    args = [xarr, g.reshape(1, width)]
    if after is not None:
        in_specs.append(pl.BlockSpec(memory_space=pl.ANY))
        args.append(after)
    return pl.pallas_call(
        body, name=name, grid=(rows // tm,), in_specs=in_specs,
        out_specs=pl.BlockSpec((tm, width), lambda i: (i, 0)),
        out_shape=jax.ShapeDtypeStruct((rows, width), BF16), compiler_params=_cparams(),
    )(*args)


def _rmsnorm_bwd(xarr, colblk, width, g, dy, dres, out_dtype, name):
    rows = xarr.shape[0]
    tm = min(TM, rows)
    has_res = dres is not None

    def body(*refs):
        x_ref, g_ref, dy_ref = refs[0], refs[1], refs[2]
        dres_ref = refs[3] if has_res else None
        dx_ref, dg_ref = refs[-2], refs[-1]
        x = x_ref[...].astype(F32)
        dyv = dy_ref[...].astype(F32)
        r = lax.rsqrt(jnp.mean(x * x, axis=1, keepdims=True) + EPS)
        xh = x * r
        dxh = dyv * g_ref[...]
        dx = r * (dxh - xh * jnp.mean(dxh * xh, axis=1, keepdims=True))
        if has_res:
            dx = dx + dres_ref[...]
        dx_ref[...] = dx.astype(dx_ref.dtype)

        @pl.when(pl.program_id(0) == 0)
        def _():
            dg_ref[...] = jnp.zeros_like(dg_ref)

        dg_ref[...] += jnp.sum(dyv * xh, axis=0, keepdims=True)

    row_spec = pl.BlockSpec((tm, width), lambda i: (i, 0))
    vec_spec = pl.BlockSpec((1, width), lambda i: (0, 0))
    in_specs = [pl.BlockSpec((tm, width), lambda i: (i, colblk)), vec_spec, row_spec]
    args = [xarr, g.reshape(1, width), dy]
    if has_res:
        in_specs.append(row_spec)
        args.append(dres)
    return pl.pallas_call(
        body, name=name, grid=(rows // tm,), in_specs=in_specs, out_specs=[row_spec, vec_spec],
        out_shape=[jax.ShapeDtypeStruct((rows, width), out_dtype), jax.ShapeDtypeStruct((1, width), F32)],
        compiler_params=_cparams(),
    )(*args)


def _loss_head(x, g, tgt):
    rows, width = x.shape
    tm = min(TM, rows)

    def body(x_ref, g_ref, t_ref, dx_ref, dg_ref, loss_ref):
        xv = x_ref[...]
        gv = g_ref[...]
        r = lax.rsqrt(jnp.mean(xv * xv, axis=1, keepdims=True) + EPS)
        xh = xv * r
        err = xh * gv - t_ref[...]
        part = 0.5 * jnp.sum(jnp.mean(err * err, axis=1, keepdims=True), axis=0, keepdims=True)
        dyv = err * (1.0 / width)
        dxh = dyv * gv
        dx_ref[...] = r * (dxh - xh * jnp.mean(dxh * xh, axis=1, keepdims=True))

        @pl.when(pl.program_id(0) == 0)
        def _():
            dg_ref[...] = jnp.zeros_like(dg_ref)
            loss_ref[...] = jnp.zeros_like(loss_ref)

        dg_ref[...] += jnp.sum(dyv * xh, axis=0, keepdims=True)
        loss_ref[...] += jnp.broadcast_to(part, loss_ref.shape)

    row_spec = pl.BlockSpec((tm, width), lambda i: (i, 0))
    vec_spec = pl.BlockSpec((1, width), lambda i: (0, 0))
    return pl.pallas_call(
        body, name="loss_head", grid=(rows // tm,), in_specs=[row_spec, vec_spec, row_spec],
        out_specs=[row_spec, vec_spec, pl.BlockSpec((1, SLOT), lambda i: (0, 0))],
        out_shape=[jax.ShapeDtypeStruct((rows, width), F32), jax.ShapeDtypeStruct((1, width), F32),
                   jax.ShapeDtypeStruct((1, SLOT), F32)],
        compiler_params=_cparams(),
    )(x, g.reshape(1, width), tgt)


def _lane_consts():
    half = 16
    inv = ROPE_THETA ** (-(jnp.arange(half, dtype=F32) * 2.0) / 32)
    lane = jnp.arange(SLOT)
    first = (lane >= 64) & (lane < 80)
    second = (lane >= 80) & (lane < 96)
    inv_lane = jnp.where(first | second, inv[(lane - 64) % half], 0.0)
    rows = [inv_lane, (lane < 64).astype(F32), first.astype(F32), second.astype(F32)]
    rows += [jnp.zeros((SLOT,), F32)] * 4
    return jnp.stack(rows).astype(F32)


def _rope_tables(pos_col, consts):
    rows = pos_col.shape[0]
    tm = min(TM, rows)

    def body(p_ref, k_ref, c_ref, s1_ref, s2_ref):
        ang = p_ref[...] * k_ref[0:1, :]
        cos, sin = jnp.cos(ang), jnp.sin(ang)
        first, second = k_ref[2:3, :], k_ref[3:4, :]
        c_ref[...] = k_ref[1:2, :] + (first + second) * cos
        s1_ref[...] = -first * sin
        s2_ref[...] = second * sin

    spec = pl.BlockSpec((tm, SLOT), lambda i: (i, 0))
    shp = jax.ShapeDtypeStruct((rows, SLOT), F32)
    return pl.pallas_call(
        body, name="rope_tables", grid=(rows // tm,),
        in_specs=[pl.BlockSpec((tm, 1), lambda i: (i, 0)), pl.BlockSpec((8, SLOT), lambda i: (0, 0))],
        out_specs=[spec, spec, spec], out_shape=[shp, shp, shp], compiler_params=_cparams(),
    )(pos_col, consts)


def _rot(xv, c, s1, s2):
    return xv * c + pltpu.roll(xv, SLOT - 16, 1) * s1 + pltpu.roll(xv, 16, 1) * s2


def _rot_t(dy, c, s1, s2):
    return dy * c + pltpu.roll(dy * s1, 16, 1) + pltpu.roll(dy * s2, SLOT - 16, 1)


def _mla_rope_fwd(qraw, kvraw, proj, tabs):
    rows = qraw.shape[0]
    tm = min(256, rows)
    hw = MLA_HEADS * SLOT

    def body(q_ref, kv_ref, kr_ref, c_ref, s1_ref, s2_ref, qo, ko, vo):
        c, s1, s2 = c_ref[...], s1_ref[...], s2_ref[...]
        kr = _rot(kr_ref[...], c, s1, s2)
        for h in range(MLA_HEADS):
            sl = slice(h * SLOT, (h + 1) * SLOT)
            qo[:, sl] = _rot(q_ref[:, sl], c, s1, s2).astype(BF16)
            ko[:, sl] = (kv_ref[:, sl] + kr).astype(BF16)
            vo[:, sl] = kv_ref[:, hw + h * SLOT:hw + (h + 1) * SLOT].astype(BF16)

    tab = pl.BlockSpec((tm, SLOT), lambda i: (i, 0))
    wide = pl.BlockSpec((tm, hw), lambda i: (i, 0))
    shp = jax.ShapeDtypeStruct((rows, hw), BF16)
    return pl.pallas_call(
        body, name="mla_rope_fwd", grid=(rows // tm,),
        in_specs=[wide, pl.BlockSpec((tm, 2 * hw), lambda i: (i, 0)), pl.BlockSpec((tm, SLOT), lambda i: (i, 3)),
                  tab, tab, tab],
        out_specs=[wide, wide, wide], out_shape=[shp, shp, shp], compiler_params=_cparams(),
    )(qraw, kvraw, proj, *tabs)


def _mla_rope_bwd(dq, dk, dv, tabs, consts):
    rows = dq.shape[0]
    tm = min(256, rows)
    hw = MLA_HEADS * SLOT

    def body(dq_ref, dk_ref, dv_ref, c_ref, s1_ref, s2_ref, k_ref, dqo, dkvo, dkro):
        c, s1, s2 = c_ref[...], s1_ref[...], s2_ref[...]
        ksum = jnp.zeros((tm, SLOT), F32)
        for h in range(MLA_HEADS):
            sl = slice(h * SLOT, (h + 1) * SLOT)
            dqo[:, sl] = _rot_t(dq_ref[:, sl], c, s1, s2).astype(BF16)
            dkh = dk_ref[:, sl]
            ksum = ksum + dkh
            dkvo[:, sl] = dkh.astype(BF16)
            dkvo[:, hw + h * SLOT:hw + (h + 1) * SLOT] = dv_ref[:, sl].astype(BF16)
        dkro[...] = _rot_t(ksum, c, s1, s2) * (k_ref[2:3, :] + k_ref[3:4, :])

    tab = pl.BlockSpec((tm, SLOT), lambda i: (i, 0))
    wide = pl.BlockSpec((tm, hw), lambda i: (i, 0))
    return pl.pallas_call(
        body, name="mla_rope_bwd", grid=(rows // tm,),
        in_specs=[wide, wide, wide, tab, tab, tab, pl.BlockSpec((8, SLOT), lambda i: (0, 0))],
        out_specs=[wide, pl.BlockSpec((tm, 2 * hw), lambda i: (i, 0)), tab],
        out_shape=[jax.ShapeDtypeStruct((rows, hw), BF16), jax.ShapeDtypeStruct((rows, 2 * hw), BF16),
                   jax.ShapeDtypeStruct((rows, SLOT), F32)],
        compiler_params=_cparams(),
    )(dq, dk, dv, *tabs, consts)


def _nt(a, b):
    return lax.dot_general(a, b, _DIMS["nt"], preferred_element_type=F32)


def _tn(a, b):
    return lax.dot_general(a, b, _DIMS["tn"], preferred_element_type=F32)


def _nn(a, b):
    return lax.dot_general(a, b, _DIMS["nn"], preferred_element_type=F32)


def _causal(t):
    return lax.broadcasted_iota(jnp.int32, (t, t), 1) <= lax.broadcasted_iota(jnp.int32, (t, t), 0)


def _mla_attn_fwd(q, k, v):
    rows = q.shape[0]
    t = min(TQ_MLA, rows)
    nt = rows // t
    scale = MLA_QK ** -0.5
    wide = MLA_PACK * SLOT

    def body(q_ref, k_ref, v_ref, o_ref, lse_ref, m_sc, l_sc, acc_sc):
        i, j = pl.program_id(1), pl.program_id(2)

        @pl.when(j == 0)
        def _():
            m_sc[...] = jnp.full_like(m_sc, NEG)
            l_sc[...] = jnp.zeros_like(l_sc)
            acc_sc[...] = jnp.zeros_like(acc_sc)

        def step(diagonal):
            for hh in range(MLA_PACK):
                sl = slice(hh * SLOT, (hh + 1) * SLOT)
                s = _nt(q_ref[:, sl], k_ref[:, sl]) * scale
                if diagonal:
                    s = jnp.where(_causal(t), s, NEG)
                m_prev = m_sc[hh]
                m_new = jnp.maximum(m_prev, jnp.max(s, axis=1, keepdims=True))
                p = jnp.exp(s - m_new)
                alpha = jnp.exp(m_prev - m_new)
                l_new = alpha * l_sc[hh] + jnp.sum(p, axis=1, keepdims=True)
                acc = alpha * acc_sc[:, sl] + _nn(p.astype(BF16), v_ref[:, sl])
                if diagonal:
                    o_ref[:, sl] = (acc / l_new).astype(o_ref.dtype)
                    lse_ref[:, sl] = jnp.broadcast_to(m_new + jnp.log(l_new), (t, SLOT))
                else:
                    m_sc[hh] = m_new
                    l_sc[hh] = l_new
                    acc_sc[:, sl] = acc

        @pl.when(j < i)
        def _():
            step(False)

        @pl.when(j == i)
        def _():
            step(True)

    q_spec = pl.BlockSpec((t, wide), lambda h, i, j: (i, h))
    kv_spec = pl.BlockSpec((t, wide), lambda h, i, j: (jnp.minimum(j, i), h))
    return pl.pallas_call(
        body, name="mla_attn_fwd", grid=(MLA_HEADS // MLA_PACK, nt, nt),
        in_specs=[q_spec, kv_spec, kv_spec], out_specs=[q_spec, q_spec],
        out_shape=[jax.ShapeDtypeStruct(q.shape, BF16), jax.ShapeDtypeStruct(q.shape, F32)],
        scratch_shapes=[pltpu.VMEM((MLA_PACK, t, 1), F32), pltpu.VMEM((MLA_PACK, t, 1), F32),
                        pltpu.VMEM((t, wide), F32)],
        compiler_params=_cparams(),
    )(q, k, v)


def _mla_attn_bwd(q, k, v, o, do, lse):
    rows = q.shape[0]
    t = min(TQ_MLA, rows)
    nt = rows // t
    scale = MLA_QK ** -0.5
    wide = MLA_PACK * SLOT

    def body(q_ref, k_ref, v_ref, o_ref, do_ref, lse_ref, dq_ref, dk_ref, dv_ref, dk_sc, dv_sc):
        j, i = pl.program_id(1), pl.program_id(2)

        @pl.when((j == 0) & (i == 0))
        def _():
            dq_ref[...] = jnp.zeros_like(dq_ref)

        @pl.when(i == 0)
        def _():
            dk_sc[...] = jnp.zeros_like(dk_sc)
            dv_sc[...] = jnp.zeros_like(dv_sc)

        def step(diagonal):
            r0 = pl.multiple_of(i * t, t)
            for hh in range(MLA_PACK):
                sl = slice(hh * SLOT, (hh + 1) * SLOT)
                qv, kv, dov = q_ref[:, sl], k_ref[:, sl], do_ref[:, sl]
                s = _nt(qv, kv) * scale
                if diagonal:
                    s = jnp.where(_causal(t), s, NEG)
                p = jnp.exp(s - lse_ref[:, hh * SLOT:hh * SLOT + 1])
                delta = jnp.sum(dov.astype(F32) * o_ref[:, sl].astype(F32), axis=1, keepdims=True)
                dp = _nt(dov, v_ref[:, sl])
                ds = (p * (dp - delta) * scale).astype(BF16)
                dv_sc[:, sl] += _tn(p.astype(BF16), dov)
                dk_sc[:, sl] += _tn(ds, qv)
                dq_ref[pl.ds(r0, t), sl] += _nn(ds, kv)

        @pl.when(i > j)
        def _():
            step(False)

        @pl.when(i == j)
        def _():
            step(True)

        @pl.when(i == nt - 1)
        def _():
            dk_ref[...] = dk_sc[...]
            dv_ref[...] = dv_sc[...]

    q_spec = pl.BlockSpec((t, wide), lambda h, j, i: (jnp.maximum(i, j), h))
    kv_spec = pl.BlockSpec((t, wide), lambda h, j, i: (j, h))
    head_spec = pl.BlockSpec((rows, wide), lambda h, j, i: (0, h))
    shp = jax.ShapeDtypeStruct(q.shape, F32)
    return pl.pallas_call(
        body, name="mla_attn_bwd", grid=(MLA_HEADS // MLA_PACK, nt, nt),
        in_specs=[q_spec, kv_spec, kv_spec, q_spec, q_spec, q_spec],
        out_specs=[head_spec, kv_spec, kv_spec], out_shape=[shp, shp, shp],
        scratch_shapes=[pltpu.VMEM((t, wide), F32), pltpu.VMEM((t, wide), F32)],
        compiler_params=_cparams(),
    )(q, k, v, o, do, lse)


def _swa_specs(t):
    def prev(i):
        return jnp.maximum(i - 1, 0)
    q3 = pl.BlockSpec((t, SWA_GROUP * SLOT), lambda h, i: (i, h))
    kp = pl.BlockSpec((t, SLOT), lambda h, i: (prev(i), SWA_HEADS + h))
    kc = pl.BlockSpec((t, SLOT), lambda h, i: (i, SWA_HEADS + h))
    vp = pl.BlockSpec((t, SLOT), lambda h, i: (prev(i), SWA_HEADS + SWA_KV_HEADS + h))
    vc = pl.BlockSpec((t, SLOT), lambda h, i: (i, SWA_HEADS + SWA_KV_HEADS + h))
    pcol = pl.BlockSpec((t, 1), lambda h, i: (i, 0))
    prow_p = pl.BlockSpec((1, t), lambda h, i: (0, prev(i)))
    prow_c = pl.BlockSpec((1, t), lambda h, i: (0, i))
    return [q3, kp, kc, vp, vc, pcol, prow_p, prow_c]


def _stack(ref):
    return jnp.concatenate([ref[:, g * SLOT:(g + 1) * SLOT] for g in range(SWA_GROUP)], axis=0)


def _swa_logits(q3, kp, kc, pq, pkp, pkc, slope_ref, kvh, i, t):
    r = lax.broadcasted_iota(jnp.int32, (t, t), 0)
    c = lax.broadcasted_iota(jnp.int32, (t, t), 1)
    ok_c = c <= r
    ok_p = (c - r) > jnp.where(i > 0, 0, t)
    dist_p, dist_c = pq - pkp, pq - pkc
    s_p3 = _nt(q3, kp) * (HEAD_DIM ** -0.5)
    s_c3 = _nt(q3, kc) * (HEAD_DIM ** -0.5)
    out = []
    for g in range(SWA_GROUP):
        slope = slope_ref[kvh * SWA_GROUP + g]
        rows = slice(g * t, (g + 1) * t)
        out.append((jnp.where(ok_p, s_p3[rows] - slope * dist_p, NEG),
                    jnp.where(ok_c, s_c3[rows] - slope * dist_c, NEG)))
    return out


def _swa_attn_fwd(proj, pos_col, pos_row, slopes, sinks):
    rows = proj.shape[0]
    t = WINDOW
    hw = SWA_HEADS * SLOT

    def body(slope_ref, sink_ref, q_ref, kp_ref, kc_ref, vp_ref, vc_ref, pq_ref, pkp_ref, pkc_ref, o_ref, lse_ref):
        kvh, i = pl.program_id(0), pl.program_id(1)
        logits = _swa_logits(_stack(q_ref), kp_ref[...], kc_ref[...], pq_ref[...], pkp_ref[...], pkc_ref[...],
                             slope_ref, kvh, i, t)
        e_p, e_c, norm = [], [], []
        for g, (s_p, s_c) in enumerate(logits):
            sink = sink_ref[kvh * SWA_GROUP + g]
            m = jnp.maximum(jnp.maximum(jnp.max(s_p, axis=1, keepdims=True), jnp.max(s_c, axis=1, keepdims=True)),
                            sink)
            ep, ec = jnp.exp(s_p - m), jnp.exp(s_c - m)
            l = jnp.sum(ep, axis=1, keepdims=True) + jnp.sum(ec, axis=1, keepdims=True) + jnp.exp(sink - m)
            e_p.append(ep.astype(BF16))
            e_c.append(ec.astype(BF16))
            norm.append(l)
            lse_ref[:, g * SLOT:(g + 1) * SLOT] = jnp.broadcast_to(m + jnp.log(l), (t, SLOT))
        acc = _nn(jnp.concatenate(e_p, axis=0), vp_ref[...]) + _nn(jnp.concatenate(e_c, axis=0), vc_ref[...])
        for g in range(SWA_GROUP):
            o_ref[:, g * SLOT:(g + 1) * SLOT] = (acc[g * t:(g + 1) * t] / norm[g]).astype(o_ref.dtype)

    smem = pl.BlockSpec(memory_space=pltpu.SMEM)
    out_spec = pl.BlockSpec((t, SWA_GROUP * SLOT), lambda h, i: (i, h))
    return pl.pallas_call(
        body, name="swa_attn_fwd", grid=(SWA_KV_HEADS, rows // t),
        in_specs=[smem, smem] + _swa_specs(t), out_specs=[out_spec, out_spec],
        out_shape=[jax.ShapeDtypeStruct((rows, hw), BF16), jax.ShapeDtypeStruct((rows, hw), F32)],
        compiler_params=_cparams(),
    )(slopes, sinks, proj, proj, proj, proj, proj, pos_col, pos_row, pos_row)


def _swa_attn_bwd(proj, o, do, lse, pos_col, pos_row, slopes, sinks):
    rows = proj.shape[0]
    t = WINDOW
    hw = SWA_HEADS * SLOT
    scale = HEAD_DIM ** -0.5

    def body(slope_ref, sink_ref, q_ref, kp_ref, kc_ref, vp_ref, vc_ref, pq_ref, pkp_ref, pkc_ref,
             o_ref, do_ref, lse_ref, dq_ref, dk_ref, dv_ref, dsink_ref):
        kvh, i = pl.program_id(0), pl.program_id(1)

        @pl.when(i == 0)
        def _():
            dk_ref[...] = jnp.zeros_like(dk_ref)
            dv_ref[...] = jnp.zeros_like(dv_ref)
            dsink_ref[...] = jnp.zeros_like(dsink_ref)

        q3, do3 = _stack(q_ref), _stack(do_ref)
        logits = _swa_logits(q3, kp_ref[...], kc_ref[...], pq_ref[...], pkp_ref[...], pkc_ref[...],
                             slope_ref, kvh, i, t)
        dp_p3, dp_c3 = _nt(do3, vp_ref[...]), _nt(do3, vc_ref[...])
        p_p, p_c, ds_p, ds_c = [], [], [], []
        for g, (s_p, s_c) in enumerate(logits):
            sl = slice(g * SLOT, (g + 1) * SLOT)
            rws = slice(g * t, (g + 1) * t)
            lse_g = lse_ref[:, g * SLOT:g * SLOT + 1]
            pp, pc = jnp.exp(s_p - lse_g), jnp.exp(s_c - lse_g)
            delta = jnp.sum(do_ref[:, sl].astype(F32) * o_ref[:, sl].astype(F32), axis=1, keepdims=True)
            p_p.append(pp.astype(BF16))
            p_c.append(pc.astype(BF16))
            ds_p.append((pp * (dp_p3[rws] - delta)).astype(BF16))
            ds_c.append((pc * (dp_c3[rws] - delta)).astype(BF16))
            sink = sink_ref[kvh * SWA_GROUP + g]
            dsink = -jnp.sum(jnp.exp(sink - lse_g) * delta, axis=0, keepdims=True)
            dsink_ref[g * 8:(g + 1) * 8, :] += jnp.broadcast_to(dsink, (8, SLOT))
        p_p3, p_c3 = jnp.concatenate(p_p, axis=0), jnp.concatenate(p_c, axis=0)
        ds_p3, ds_c3 = jnp.concatenate(ds_p, axis=0), jnp.concatenate(ds_c, axis=0)
        dq3 = (_nn(ds_p3, kp_ref[...]) + _nn(ds_c3, kc_ref[...])) * scale
        for g in range(SWA_GROUP):
            dq_ref[:, g * SLOT:(g + 1) * SLOT] = dq3[g * t:(g + 1) * t]
        r_c = pl.multiple_of(i * t, t)
        dk_ref[pl.ds(r_c, t), :] += _tn(ds_c3, q3) * scale
        dv_ref[pl.ds(r_c, t), :] += _tn(p_c3, do3)

        @pl.when(i > 0)
        def _():
            r_p = pl.multiple_of((i - 1) * t, t)
            dk_ref[pl.ds(r_p, t), :] += _tn(ds_p3, q3) * scale
            dv_ref[pl.ds(r_p, t), :] += _tn(p_p3, do3)

    smem = pl.BlockSpec(memory_space=pltpu.SMEM)
    qlike = pl.BlockSpec((t, SWA_GROUP * SLOT), lambda h, i: (i, h))
    kv_out = pl.BlockSpec((rows, SLOT), lambda h, i: (0, h))
    return pl.pallas_call(
        body, name="swa_attn_bwd", grid=(SWA_KV_HEADS, rows // t),
        in_specs=[smem, smem] + _swa_specs(t) + [qlike, qlike, qlike],
        out_specs=[qlike, kv_out, kv_out, pl.BlockSpec((SWA_GROUP * 8, SLOT), lambda h, i: (h, 0))],
        out_shape=[jax.ShapeDtypeStruct((rows, hw), F32), jax.ShapeDtypeStruct((rows, SWA_KV_HEADS * SLOT), F32),
                   jax.ShapeDtypeStruct((rows, SWA_KV_HEADS * SLOT), F32),
                   jax.ShapeDtypeStruct((SWA_HEADS * 8, SLOT), F32)],
        compiler_params=_cparams(),
    )(slopes, sinks, proj, proj, proj, proj, proj, pos_col, pos_row, pos_row, o, do, lse)


def _cross_attn_fwd(proj, qoff, kvmem):
    rows = proj.shape[0]
    t = min(TQ_CROSS, rows)

    def body(q_ref, k_ref, v_ref, o_ref):
        s = _nt(q_ref[...].astype(BF16), k_ref[...]) * (HEAD_DIM ** -0.5)
        e = jnp.exp(s - jnp.max(s, axis=1, keepdims=True))
        p = e / jnp.sum(e, axis=1, keepdims=True)
        o_ref[...] = _nn(p.astype(BF16), v_ref[...]).astype(o_ref.dtype)

    return pl.pallas_call(
        body, name="cross_attn_fwd", grid=(rows // t, MEM_HEADS),
        in_specs=[pl.BlockSpec((t, SLOT), lambda i, h: (i, qoff + h)),
                  pl.BlockSpec((N_MEM, SLOT), lambda i, h: (0, h)),
                  pl.BlockSpec((N_MEM, SLOT), lambda i, h: (0, MEM_HEADS + h))],
        out_specs=pl.BlockSpec((t, SLOT), lambda i, h: (i, h)),
        out_shape=jax.ShapeDtypeStruct((rows, MEM_HEADS * SLOT), BF16), compiler_params=_cparams(),
    )(proj, kvmem, kvmem)


def _cross_attn_bwd(proj, qoff, kvmem, do):
    rows = proj.shape[0]
    t = min(TQ_CROSS, rows)
    scale = HEAD_DIM ** -0.5

    def body(q_ref, k_ref, v_ref, do_ref, dq_ref, dk_ref, dv_ref):
        @pl.when(pl.program_id(1) == 0)
        def _():
            dk_ref[...] = jnp.zeros_like(dk_ref)
            dv_ref[...] = jnp.zeros_like(dv_ref)

        qv, kv, dov = q_ref[...].astype(BF16), k_ref[...], do_ref[...]
        s = _nt(qv, kv) * scale
        e = jnp.exp(s - jnp.max(s, axis=1, keepdims=True))
        p = e / jnp.sum(e, axis=1, keepdims=True)
        dp = _nt(dov, v_ref[...])
        ds = (p * (dp - jnp.sum(p * dp, axis=1, keepdims=True))).astype(BF16)
        dq_ref[...] = _nn(ds, kv) * scale
        dk_ref[...] += _tn(ds, qv) * scale
        dv_ref[...] += _tn(p.astype(BF16), dov)

    mem_out = pl.BlockSpec((N_MEM, SLOT), lambda h, i: (0, h))
    return pl.pallas_call(
        body, name="cross_attn_bwd", grid=(MEM_HEADS, rows // t),
        in_specs=[pl.BlockSpec((t, SLOT), lambda h, i: (i, qoff + h)),
                  pl.BlockSpec((N_MEM, SLOT), lambda h, i: (0, h)),
                  pl.BlockSpec((N_MEM, SLOT), lambda h, i: (0, MEM_HEADS + h)),
                  pl.BlockSpec((t, SLOT), lambda h, i: (i, h))],
        out_specs=[pl.BlockSpec((t, SLOT), lambda h, i: (i, h)), mem_out, mem_out],
        out_shape=[jax.ShapeDtypeStruct((rows, MEM_HEADS * SLOT), F32),
                   jax.ShapeDtypeStruct((N_MEM, MEM_HEADS * SLOT), F32),
                   jax.ShapeDtypeStruct((N_MEM, MEM_HEADS * SLOT), F32)],
        compiler_params=_cparams(),
    )(proj, kvmem, kvmem, do)


def _place():
    return lax.axis_index("x"), lax.axis_index("y"), lax.axis_index("c")


def _flip(v, bit):
    return 1 - v if bit else v


def _all_gather(blocks, name):
    nb = len(blocks)

    def body(*refs):
        x_refs, out_refs = refs[:nb], refs[nb:2 * nb]
        send_sems, recv_sems, local_sems = refs[2 * nb:]
        x, y, c = _place()
        me, sibling = (x, y, c), (x, y, 1 - c)
        chips = [(1 - x, y), (x, 1 - y), (1 - x, 1 - y)]

        def copy(b, k, blk, to, from_input=False):
            slot = out_refs[b].at[4 * blk[0] + 2 * blk[1] + blk[2]]
            return pltpu.make_async_remote_copy(
                src_ref=x_refs[b] if from_input else slot, dst_ref=slot,
                send_sem=send_sems.at[b, k], recv_sem=recv_sems.at[b, k],
                device_id=to, device_id_type=pl.DeviceIdType.MESH)

        mine = [pltpu.make_async_copy(x_refs[b], out_refs[b].at[4 * x + 2 * y + c], local_sems.at[b])
                for b in range(nb)]
        for cp in mine:
            cp.start()
        first = []
        for b in range(nb):
            first.append(copy(b, 0, me, sibling, from_input=True))
            first += [copy(b, 1 + n, me, (*chip, c), from_input=True) for n, chip in enumerate(chips)]
        for cp in first:
            cp.start()
        passed = []
        for n, chip in enumerate(chips):
            for b in range(nb):
                copy(b, 1 + n, (*chip, c), me).wait_recv()
                passed.append(copy(b, 4 + n, (*chip, c), sibling))
                passed[-1].start()
        for b in range(nb):
            copy(b, 0, sibling, me).wait_recv()
            for n, chip in enumerate(chips):
                copy(b, 4 + n, (*chip, 1 - c), me).wait_recv()
        for cp in first + passed:
            cp.wait_send()
        for cp in mine:
            cp.wait()

    any_spec = pl.BlockSpec(memory_space=pl.ANY)
    return pl.pallas_call(
        body, name=name, in_specs=[any_spec] * nb, out_specs=[any_spec] * nb,
        out_shape=[jax.ShapeDtypeStruct((N_DEV,) + blk.shape, blk.dtype) for blk in blocks],
        scratch_shapes=[pltpu.SemaphoreType.DMA((nb, 7)), pltpu.SemaphoreType.DMA((nb, 7)),
                        pltpu.SemaphoreType.DMA((nb,))],
    )(*blocks)


def _peers(x, y, c):
    out = []
    for n in range(1, N_DEV):
        peer = (_flip(x, n & 4), _flip(y, n & 2), _flip(c, n & 1))
        out.append((n - 1, peer, 4 * peer[0] + 2 * peer[1] + peer[2]))
    return out


_HBM = pl.BlockSpec(memory_space=pltpu.HBM)
_SEM = pl.BlockSpec(memory_space=pltpu.SEMAPHORE)


def _exchange_start(srcs, scatter, name, after=None):
    ns = len(srcs)
    lands = [lax.empty(s.shape if scatter else (N_DEV,) + s.shape, s.dtype) for s in srcs]

    def body(*refs):
        src_refs, land_refs = refs[:ns], refs[ns:2 * ns]
        pos = 2 * ns + (1 if after is not None else 0)
        send_sems, recv_sems, token = refs[pos], refs[pos + 1], refs[-1]
        x, y, c = _place()
        my_idx = 4 * x + 2 * y + c
        for col, peer, peer_idx in _peers(x, y, c):
            for b in range(ns):
                pltpu.make_async_remote_copy(
                    src_ref=src_refs[b].at[peer_idx] if scatter else src_refs[b], dst_ref=land_refs[b].at[my_idx],
                    send_sem=send_sems.at[b * (N_DEV - 1) + col], recv_sem=recv_sems.at[b * (N_DEV - 1) + col],
                    device_id=peer, device_id_type=pl.DeviceIdType.MESH).start()
        token[...] = jnp.zeros_like(token)

    args = [pltpu.with_memory_space_constraint(a, pltpu.HBM) for a in list(srcs) + lands]
    in_specs = [_HBM] * (2 * ns)
    if after is not None:
        args.append(after)
        in_specs.append(pl.BlockSpec(memory_space=pl.ANY))
    out = pl.pallas_call(
        body, name=name, in_specs=in_specs,
        out_specs=[_SEM, _SEM] + [_HBM] * (2 * ns) + [pl.BlockSpec(memory_space=pltpu.VMEM)],
        out_shape=[pltpu.SemaphoreType.DMA((ns * (N_DEV - 1),)), pltpu.SemaphoreType.DMA((ns * (N_DEV - 1),))]
        + [pltpu.HBM(a.shape, a.dtype) for a in list(srcs) + lands] + [jax.ShapeDtypeStruct((8, SLOT), F32)],
        input_output_aliases={k: 2 + k for k in range(2 * ns)},
        compiler_params=pltpu.CompilerParams(has_side_effects=pltpu.SideEffectType.DATAFLOW_SIDE_EFFECTING),
    )(*args)
    return (out[0], out[1], out[2:2 + ns], out[2 + ns:2 + 2 * ns], scatter), out[-1]


def _exchange_wait(handle, after, name):
    send_sems, recv_sems, srcs, lands, scatter = handle
    ns = len(srcs)

    def body(*refs):
        src_refs, land_refs = refs[:ns], refs[ns:2 * ns]
        send_ref, recv_ref = refs[2 * ns], refs[2 * ns + 1]
        x, y, c = _place()
        for col, peer, peer_idx in _peers(x, y, c):
            for b in range(ns):
                copy = pltpu.make_async_remote_copy(
                    src_ref=src_refs[b].at[peer_idx] if scatter else src_refs[b], dst_ref=land_refs[b].at[peer_idx],
                    send_sem=send_ref.at[b * (N_DEV - 1) + col], recv_sem=recv_ref.at[b * (N_DEV - 1) + col],
                    device_id=peer, device_id_type=pl.DeviceIdType.MESH)
                copy.wait_send()
                copy.wait_recv()

    out = pl.pallas_call(
        body, name=name, in_specs=[_HBM] * (2 * ns) + [_SEM, _SEM, pl.BlockSpec(memory_space=pl.ANY)],
        out_specs=[_HBM] * (2 * ns),
        out_shape=[pltpu.HBM(a.shape, a.dtype) for a in list(srcs) + list(lands)],
        input_output_aliases={k: k for k in range(2 * ns)},
        compiler_params=pltpu.CompilerParams(has_side_effects=pltpu.SideEffectType.DATAFLOW_SIDE_EFFECTING),
    )(*srcs, *lands, send_sems, recv_sems, after)
    my_idx = 4 * lax.axis_index("x") + 2 * lax.axis_index("y") + lax.axis_index("c")
    landed = []
    for src, land in zip(out[:ns], out[ns:]):
        own = lax.dynamic_index_in_dim(src, my_idx, 0, keepdims=True) if scatter else src[None]
        landed.append(lax.dynamic_update_index_in_dim(land, own, my_idx, 0))
    return landed


def _adamw(parts, w, m, v, name):
    lyr, rows, cols = w.shape
    tr = ADAM_ROWS if cols > 512 else 2 * ADAM_ROWS
    while rows % tr:
        tr //= 2
    tr = min(tr, rows)

    def body(p_ref, w_ref, m_ref, v_ref, g_out, d_out, m_out, v_out):
        g = p_ref[0].astype(F32)
        for s in range(1, N_DEV):
            g = g + p_ref[s].astype(F32)
        m2 = ADAM_B1 * m_ref[...] + (1.0 - ADAM_B1) * g
        v2 = ADAM_B2 * v_ref[...] + (1.0 - ADAM_B2) * (g * g)
        m_hat = m2 / (1.0 - ADAM_B1 ** ADAM_STEP)
        v_hat = v2 / (1.0 - ADAM_B2 ** ADAM_STEP)
        g_out[...] = g
        d_out[...] = -ADAM_LR * (m_hat / (jnp.sqrt(v_hat) + ADAM_EPS) + ADAM_WD * w_ref[...])
        m_out[...] = m2
        v_out[...] = v2

    spec = pl.BlockSpec((None, tr, cols), lambda l, i: (l, i, 0))
    shp = jax.ShapeDtypeStruct((lyr, rows, cols), F32)
    return pl.pallas_call(
        body, name=name, grid=(lyr, rows // tr),
        in_specs=[pl.BlockSpec((None, N_DEV, tr, cols), lambda l, i: (l, 0, i, 0)), spec, spec, spec],
        out_specs=[spec] * 4, out_shape=[shp] * 4, compiler_params=_cparams(),
    )(parts, w, m, v)


def _pack(arrays, lanes, row_mult, dtype):
    flat = jnp.concatenate([a.reshape(-1).astype(dtype) for a in arrays])
    unit = lanes * row_mult
    total = -(-flat.shape[0] // unit) * unit
    return jnp.pad(flat, (0, total - flat.shape[0])).reshape(total // lanes, lanes)


def _unpack(packed, shapes):
    flat = packed.reshape(-1)
    out, off = [], 0
    for shp in shapes:
        n = 1
        for d in shp:
            n *= d
        out.append(flat[off:off + n].reshape(shp))
        off += n
    return out


def _pad_slots(w, axis):
    axis = axis % w.ndim
    n = w.shape[axis] // HEAD_DIM
    shp = w.shape[:axis] + (n, HEAD_DIM) + w.shape[axis + 1:]
    pad = [(0, 0)] * (w.ndim + 1)
    pad[axis + 1] = (0, SLOT - HEAD_DIM)
    return jnp.pad(w.reshape(shp), pad).reshape(w.shape[:axis] + (n * SLOT,) + w.shape[axis + 1:])


def _unpad_slots(w, axis, keep=HEAD_DIM):
    axis = axis % w.ndim
    n = w.shape[axis] // SLOT
    shp = w.shape[:axis] + (n, SLOT) + w.shape[axis + 1:]
    idx = [slice(None)] * (w.ndim + 1)
    idx[axis + 1] = slice(0, keep)
    return w.reshape(shp)[tuple(idx)].reshape(w.shape[:axis] + (n * keep,) + w.shape[axis + 1:])


def _mla_in_pad(w):
    z = functools.partial(jnp.zeros, dtype=w.dtype)
    rows = w.shape[0]
    return jnp.concatenate([w[:, :384], z((rows, 64)), w[:, 640:672], z((rows, 32)), w[:, 384:640],
                            _pad_slots(w[:, 672:], 1)], axis=1)


def _mla_in_unpad(d):
    return jnp.concatenate([d[:, :384], d[:, 512:768], d[:, 448:480], _unpad_slots(d[:, 768:], 1)], axis=1)


def _mla_uq_pad(w):
    return jnp.pad(w.reshape(w.shape[0], MLA_HEADS, MLA_QK), ((0, 0), (0, 0), (0, SLOT - MLA_QK))).reshape(
        w.shape[0], MLA_HEADS * SLOT)


def _mla_ukv_pad(w):
    w3 = w.reshape(w.shape[0], MLA_HEADS, 2 * HEAD_DIM)
    pad = ((0, 0), (0, 0), (0, SLOT - HEAD_DIM))
    k = jnp.pad(w3[:, :, :HEAD_DIM], pad).reshape(w.shape[0], -1)
    v = jnp.pad(w3[:, :, HEAD_DIM:], pad).reshape(w.shape[0], -1)
    return jnp.concatenate([k, v], axis=1)


def _mla_ukv_unpad(d):
    hw = MLA_HEADS * SLOT
    k = d[:, :hw].reshape(d.shape[0], MLA_HEADS, SLOT)[:, :, :HEAD_DIM]
    v = d[:, hw:].reshape(d.shape[0], MLA_HEADS, SLOT)[:, :, :HEAD_DIM]
    return jnp.concatenate([k, v], axis=2).reshape(d.shape[0], MLA_HEADS * 2 * HEAD_DIM)


def _join(gathered, axis):
    nd, a, b = gathered.shape
    if axis == 1:
        return gathered.reshape(nd * a, b)
    return gathered.transpose(1, 0, 2).reshape(a, nd * b)


def _split(full, axis):
    r, c = full.shape
    if axis == 1:
        return full.reshape(N_DEV, r // N_DEV, c).astype(BF16)
    return full.reshape(r, N_DEV, c // N_DEV).transpose(1, 0, 2).astype(BF16)


def kernel(x, mem, positions, attn_norm_g, mlp_norm_g, mem_norm_g, final_norm_g, mla_w_in, mla_q_norm_g, mla_kv_norm_g, mla_w_uq, mla_w_ukv, swa_w_in, swa_sinks, w_mem_kv, w_o, mlp_w_up, mlp_w_down, loss_target, m_attn_norm_g, m_mlp_norm_g, m_mem_norm_g, m_final_norm_g, m_mla_w_in, m_mla_q_norm_g, m_mla_kv_norm_g, m_mla_w_uq, m_mla_w_ukv, m_swa_w_in, m_swa_sinks, m_w_mem_kv, m_w_o, m_mlp_w_up, m_mlp_w_down, v_attn_norm_g, v_mlp_norm_g, v_mem_norm_g, v_final_norm_g, v_mla_w_in, v_mla_q_norm_g, v_mla_kv_norm_g, v_mla_w_uq, v_mla_w_ukv, v_swa_w_in, v_swa_sinks, v_w_mem_kv, v_w_o, v_mlp_w_up, v_mlp_w_down):
    given = dict(locals())
    seq = x.shape[1]
    x0 = x.reshape(seq, D_MODEL)
    tgt = loss_target.reshape(seq, D_MODEL)
    mem0 = mem.reshape(N_MEM, D_MODEL)
    pos = positions.reshape(seq).astype(F32)
    pos_col, pos_row = pos.reshape(seq, 1), pos.reshape(1, seq)

    def layer_names(i):
        mixer = ("mla_w_in", "mla_w_uq", "mla_w_ukv") if i % 2 == 0 else ("swa_w_in",)
        return [(n, i // 2) for n in mixer] + [(n, i) for n in ("w_mem_kv", "w_o", "mlp_w_up", "mlp_w_down")]

    def local_weights(i):
        return [given[n][l].astype(BF16) for n, l in layer_names(i)]

    weights = [dict(zip([n for n, _ in layer_names(0)], _all_gather(local_weights(0), "gather_weights_first")))]

    consts = _lane_consts()
    tabs = _rope_tables(pos_col, consts)
    slopes = 2.0 ** (-8.0 * (jnp.arange(SWA_HEADS, dtype=F32) + 1.0) / SWA_HEADS)

    mem_n = _rmsnorm_fwd(mem0, 0, D_MODEL, mem_norm_g, "rmsnorm_fwd_mem")

    saved = []
    xc = x0
    for i in range(DEPTH):
        j = i // 2
        wts = weights[i]
        s = {"x_in": xc}
        token = None
        if i + 1 < DEPTH:
            coming, token = _exchange_start(local_weights(i + 1), False, "gather_weights_start_%d" % (i + 1),
                                            after=wts["mlp_w_down"])
        hn = _rmsnorm_fwd(xc, 0, D_MODEL, attn_norm_g[i], "rmsnorm_fwd", after=token)
        if i % 2 == 0:
            w_in = _mla_in_pad(_join(wts["mla_w_in"], 1))
            w_uq = _mla_uq_pad(_join(wts["mla_w_uq"], 2))
            w_kv = _mla_ukv_pad(_join(wts["mla_w_ukv"], 2))
            proj = _mm(hn, w_in, "nn", F32, "mm_mla_in")
            cqn = _rmsnorm_fwd(proj, 0, MLA_Q_RANK, mla_q_norm_g[j], "rmsnorm_fwd_q")
            ckvn = _rmsnorm_fwd(proj, 2, MLA_KV_RANK, mla_kv_norm_g[j], "rmsnorm_fwd_kv")
            qraw = _mm(cqn, w_uq, "nn", F32, "mm_mla_uq")
            kvraw = _mm(ckvn, w_kv, "nn", F32, "mm_mla_ukv")
            q, k, v = _mla_rope_fwd(qraw, kvraw, proj, tabs)
            o, lse = _mla_attn_fwd(q, k, v)
            qoff = MLA_QOFF
            s.update(w_uq=w_uq, w_kv=w_kv, cqn=cqn, ckvn=ckvn, q=q, k=k, v=v)
        else:
            w_in = _pad_slots(_join(wts["swa_w_in"], 2), 1)
            proj = _mm(hn, w_in, "nn", BF16, "mm_swa_in")
            o, lse = _swa_attn_fwd(proj, pos_col, pos_row, slopes, swa_sinks[j])
            qoff = SWA_QOFF
        w_mem = _pad_slots(_join(wts["w_mem_kv"], 1), 1)
        w_out = _pad_slots(_join(wts["w_o"], 1), 0)
        w_o_mix, w_o_cross = w_out[:SWA_HEADS * SLOT], w_out[SWA_HEADS * SLOT:]
        kvmem = _mm(mem_n, w_mem, "nn", BF16, "mm_mem_kv")
        cross = _cross_attn_fwd(proj, qoff, kvmem)
        x1 = _mm(o, w_o_mix, "nn", F32, "mm_o_mix", res=xc)
        x1 = _mm(cross, w_o_cross, "nn", F32, "mm_o_cross", res=x1)
        hn2 = _rmsnorm_fwd(x1, 0, D_MODEL, mlp_norm_g[i], "rmsnorm_fwd")
        act, act2 = _mm(hn2, wts["mlp_w_up"], "nn", BF16, "mm_mlp_up", epi="relu2", b_blk="cols")
        xc = _mm(act2, wts["mlp_w_down"], "nn", F32, "mm_mlp_down", res=x1, b_blk="rows")
        s.update(hn=hn, w_in=w_in, proj=proj, o=o, lse=lse, qoff=qoff, w_mem=w_mem, w_o_mix=w_o_mix,
                 w_o_cross=w_o_cross, kvmem=kvmem, cross=cross, x1=x1, hn2=hn2, act=act, act2=act2)
        saved.append(s)
        if i + 1 < DEPTH:
            got = _exchange_wait(coming, xc, "gather_weights_wait_%d" % (i + 1))
            weights.append(dict(zip([n for n, _ in layer_names(i + 1)], got)))

    dx, dg_final, loss_part = _loss_head(xc, final_norm_g, tgt)
    loss = lax.psum(loss_part[0, 0], MESH_AXES)

    gains = {n: [None] * DEPTH for n in ("attn_norm_g", "mlp_norm_g")}
    for n in ("mla_q_norm_g", "mla_kv_norm_g", "swa_sinks"):
        gains[n] = [None] * 2
    leaving = {}
    token = None
    dmem_n = None
    for i in reversed(range(DEPTH)):
        j = i // 2
        s = saved[i]
        wts = weights[i]
        out = {}
        du = _mm(dx, wts["mlp_w_down"], "nt", BF16, "mm_mlp_down_dx", aux=s["act"], epi="mul2aux", b_blk="rows",
                 after=token)
        out["mlp_w_down"] = _mm(s["act2"], dx, "tn", BF16, "mm_mlp_down_dw", o_blk="rows")
        dhn2 = _mm(du, wts["mlp_w_up"], "nt", F32, "mm_mlp_up_dx", b_blk="cols")
        out["mlp_w_up"] = _mm(s["hn2"], du, "tn", BF16, "mm_mlp_up_dw", o_blk="cols")
        dx1, dg = _rmsnorm_bwd(s["x1"], 0, D_MODEL, mlp_norm_g[i], dhn2, dx, F32, "rmsnorm_bwd")
        gains["mlp_norm_g"][i] = dg[0]

        do = _mm(dx1, s["w_o_mix"], "nt", BF16, "mm_o_mix_dx")
        dcross = _mm(dx1, s["w_o_cross"], "nt", BF16, "mm_o_cross_dx")
        dw_o = jnp.concatenate([_mm(s["o"], dx1, "tn", F32, "mm_o_mix_dw"),
                                _mm(s["cross"], dx1, "tn", F32, "mm_o_cross_dw")], axis=0)
        out["w_o"] = _split(_unpad_slots(dw_o, 0), 1)
        dqc, dkm, dvm = _cross_attn_bwd(s["proj"], s["qoff"], s["kvmem"], dcross)
        dkvmem = jnp.concatenate([dkm, dvm], axis=1).astype(BF16)
        out["w_mem_kv"] = _split(_unpad_slots(_mm(mem_n, dkvmem, "tn", F32, "mm_mem_kv_dw"), 1), 1)
        dmem_n = _mm(dkvmem, s["w_mem"], "nt", F32, "mm_mem_kv_dx" if dmem_n is None else "mm_mem_kv_dx_acc",
                     res=dmem_n)

        if i % 2 == 0:
            dq, dk, dv = _mla_attn_bwd(s["q"], s["k"], s["v"], s["o"], do, s["lse"])
            dqraw, dkv, dkr = _mla_rope_bwd(dq, dk, dv, tabs, consts)
            dcqn = _mm(dqraw, s["w_uq"], "nt", F32, "mm_mla_uq_dx")
            out["mla_w_uq"] = _split(_unpad_slots(_mm(s["cqn"], dqraw, "tn", F32, "mm_mla_uq_dw"), 1, MLA_QK), 2)
            dckvn = _mm(dkv, s["w_kv"], "nt", F32, "mm_mla_ukv_dx")
            out["mla_w_ukv"] = _split(_mla_ukv_unpad(_mm(s["ckvn"], dkv, "tn", F32, "mm_mla_ukv_dw")), 2)
            dcq, dg = _rmsnorm_bwd(s["proj"], 0, MLA_Q_RANK, mla_q_norm_g[j], dcqn, None, BF16, "rmsnorm_bwd_q")
            gains["mla_q_norm_g"][j] = dg[0]
            dckv, dg = _rmsnorm_bwd(s["proj"], 2, MLA_KV_RANK, mla_kv_norm_g[j], dckvn, None, BF16, "rmsnorm_bwd_kv")
            gains["mla_kv_norm_g"][j] = dg[0]
            dproj = jnp.concatenate([dcq, dkr.astype(BF16), dckv, dqc.astype(BF16)], axis=1)
            dhn = _mm(dproj, s["w_in"], "nt", F32, "mm_mla_in_dx")
            out["mla_w_in"] = _split(_mla_in_unpad(_mm(s["hn"], dproj, "tn", F32, "mm_mla_in_dw")), 1)
        else:
            dq, dk, dv, dsink = _swa_attn_bwd(s["proj"], s["o"], do, s["lse"], pos_col, pos_row, slopes, swa_sinks[j])
            gains["swa_sinks"][j] = dsink[::8, 0]
            dproj = jnp.concatenate([dq, dk, dv, dqc], axis=1).astype(BF16)
            dhn = _mm(dproj, s["w_in"], "nt", F32, "mm_swa_in_dx")
            out["swa_w_in"] = _split(_unpad_slots(_mm(s["hn"], dproj, "tn", F32, "mm_swa_in_dw"), 1), 2)
        dx, dg = _rmsnorm_bwd(s["x_in"], 0, D_MODEL, attn_norm_g[i], dhn, dx1, F32, "rmsnorm_bwd")
        gains["attn_norm_g"][i] = dg[0]

        leaving[i], token = _exchange_start([out[n] for n, _ in layer_names(i)], True, "exchange_grads_start_%d" % i)

    _, dg_mem = _rmsnorm_bwd(mem0, 0, D_MODEL, mem_norm_g, dmem_n, None, BF16, "rmsnorm_bwd_mem")
    gains = {n: jnp.stack(g) for n, g in gains.items()}
    gains["mem_norm_g"] = dg_mem[0]
    gains["final_norm_g"] = dg_final[0]

    received = {}
    for i in reversed(range(DEPTH)):
        for (n, l), r in zip(layer_names(i), _exchange_wait(leaving[i], dx, "exchange_grads_wait_%d" % i)):
            received[(n, l)] = r

    result = {}
    for n, _ in SHARDED:
        parts = jnp.stack([received[(n, l)] for l in range(given[n].shape[0])])
        for kind, r in enumerate(_adamw(parts, given[n], given["m_" + n], given["v_" + n], "adamw_" + n)):
            result[(kind, n)] = r

    rep_shapes = [given[n].shape for n in REPLICATED]
    rep_parts = _all_gather([_pack([gains[n] for n in REPLICATED], SLOT, 8, F32)], "gather_gain_grads")[0]
    rep_packed = [_pack([given[p + n] for n in REPLICATED], SLOT, 8, F32)[None] for p in ("", "m_", "v_")]
    for kind, r in enumerate(_adamw(rep_parts[None], *rep_packed, "adamw_gains")):
        for n, part in zip(REPLICATED, _unpack(r[0], rep_shapes)):
            result[(kind, n)] = part

    outs = [loss, dx.reshape(1, seq, D_MODEL)]
    for kind in range(4):
        outs += [result[(kind, n)] for n in WEIGHT_ORDER]
    return tuple(outs)
```

```python
import functools

import jax
import jax.numpy as jnp
from jax import lax
from jax.experimental import pallas as pl
from jax.experimental.pallas import tpu as pltpu

F32 = jnp.float32
BF16 = jnp.bfloat16

D_MODEL = 1024
D_FF = 4096
N_MEM = 256
DEPTH = 4
SLOT = 128
HEAD_DIM = 64
MLA_HEADS = 12
MLA_QK = 96
MLA_Q_RANK = 384
MLA_KV_RANK = 256
SWA_HEADS = 12
SWA_KV_HEADS = 4
SWA_GROUP = 3
MEM_HEADS = 4
WINDOW = 128
EPS = 1e-6
NEG = -1e30
ROPE_THETA = 10000.0
N_DEV = 8

ADAM_LR = 0.001
ADAM_B1 = 0.9
ADAM_B2 = 0.999
ADAM_EPS = 1e-08
ADAM_WD = 0.01
ADAM_STEP = 10

TM = 512
TQ_MLA = 512
MLA_PACK = 2
TQ_CROSS = 512
MM_VMEM_BUDGET = 30 * 1024 * 1024
ADAM_ROWS = 128
VMEM_LIMIT = 48 * 1024 * 1024

MESH_AXES = ("x", "y", "c")

MLA_PAD_IN = 384 + SLOT + 256 + MEM_HEADS * SLOT
MLA_QOFF = (384 + SLOT + 256) // SLOT
SWA_PAD_IN = (SWA_HEADS + 2 * SWA_KV_HEADS + MEM_HEADS) * SLOT
SWA_QOFF = SWA_HEADS + 2 * SWA_KV_HEADS

SHARDED = (
    ("mla_w_in", 1), ("mla_w_uq", 2), ("mla_w_ukv", 2), ("swa_w_in", 2),
    ("w_mem_kv", 1), ("w_o", 1), ("mlp_w_up", 2), ("mlp_w_down", 1),
)
REPLICATED = ("attn_norm_g", "mlp_norm_g", "mem_norm_g", "final_norm_g",
              "mla_q_norm_g", "mla_kv_norm_g", "swa_sinks")
WEIGHT_ORDER = ("attn_norm_g", "mlp_norm_g", "mem_norm_g", "final_norm_g", "mla_w_in",
                "mla_q_norm_g", "mla_kv_norm_g", "mla_w_uq", "mla_w_ukv", "swa_w_in",
                "swa_sinks", "w_mem_kv", "w_o", "mlp_w_up", "mlp_w_down")


def _cparams():
    return pltpu.CompilerParams(vmem_limit_bytes=VMEM_LIMIT)


_DIMS = {"nn": (((1,), (0,)), ((), ())), "nt": (((1,), (1,)), ((), ())), "tn": (((0,), (0,)), ((), ()))}


def _mm_tiles(m, n, k, a_bytes, b_bytes, o_bytes, extra_bytes, tm_fixed, tn_fixed):
    best = None
    for tm in ([tm_fixed] if tm_fixed else [t for t in (1024, 512, 256, 128) if m % t == 0] or [m]):
        for tn in ([tn_fixed] if tn_fixed else [t for t in range(1024, 0, -SLOT) if n % t == 0] or [n]):
            need = 2 * (tm * k * a_bytes + k * tn * b_bytes + tm * tn * (o_bytes + extra_bytes))
            if need <= MM_VMEM_BUDGET and (best is None or tm * tn > best[0] * best[1]):
                best = (tm, tn)
    assert best is not None, (m, n, k)
    return best


def _mm(a, b, mode, out_dtype, name, res=None, aux=None, epi=None, b_blk=None, o_blk=None, after=None):
    if b_blk is not None:
        nb, br, bc = b.shape
        b_shape = (nb * br, bc) if b_blk == "rows" else (br, nb * bc)
    else:
        b_shape = b.shape
    if mode == "nn":
        (m, k), (k2, n) = a.shape, b_shape
    elif mode == "nt":
        (m, k), (n, k2) = a.shape, b_shape
    else:
        (k, m), (k2, n) = a.shape, b_shape
    assert k == k2, (a.shape, b_shape, mode)
    k_blocked = b_blk is not None and (b_blk == "rows") == (mode != "nt")
    assert not (k_blocked and mode == "nt")
    tn_fixed = None
    if b_blk is not None and not k_blocked:
        tn_fixed = br if b_blk == "rows" else bc
    if o_blk == "cols":
        tn_fixed = n // N_DEV
    tm_fixed = m // N_DEV if o_blk == "rows" else None
    has_res, has_aux = res is not None, aux is not None
    assert o_blk is None or not (has_res or has_aux)
    n_out = 2 if epi == "relu2" else 1
    tm, tn = _mm_tiles(m, n, k, a.dtype.itemsize, b.dtype.itemsize, n_out * jnp.dtype(out_dtype).itemsize,
                       (4 if has_res else 0) + (aux.dtype.itemsize if has_aux else 0), tm_fixed, tn_fixed)
    dims = _DIMS[mode]
    if mode == "tn":
        a_spec = pl.BlockSpec((k, tm), lambda i, j: (0, i))
    else:
        a_spec = pl.BlockSpec((tm, k), lambda i, j: (i, 0))
    if b_blk is None:
        if mode == "nt":
            b_spec = pl.BlockSpec((tn, k), lambda i, j: (j, 0))
        else:
            b_spec = pl.BlockSpec((k, tn), lambda i, j: (0, j))
    elif k_blocked:
        b_spec = pl.BlockSpec((N_DEV, br, tn), lambda i, j: (0, 0, j))
    elif mode == "nt":
        b_spec = pl.BlockSpec((None, tn, k), lambda i, j: (j, 0, 0))
    else:
        b_spec = pl.BlockSpec((None, k, tn), lambda i, j: (j, 0, 0))
    if o_blk is None:
        o_spec = pl.BlockSpec((tm, tn), lambda i, j: (i, j))
        o_shape = (m, n)
    elif o_blk == "rows":
        o_spec = pl.BlockSpec((None, tm, tn), lambda i, j: (i, 0, j))
        o_shape = (N_DEV, tm, n)
    else:
        o_spec = pl.BlockSpec((None, tm, tn), lambda i, j: (j, i, 0))
        o_shape = (N_DEV, m, tn)

    def body(*refs):
        a_ref, b_ref = refs[0], refs[1]
        pos = 2
        res_ref = aux_ref = None
        if has_res:
            res_ref = refs[pos]
            pos += 1
        if has_aux:
            aux_ref = refs[pos]
            pos += 1
        if after is not None:
            pos += 1
        outs = refs[pos:]
        bv = b_ref[...].reshape(k, tn) if k_blocked else b_ref[...]
        r = lax.dot_general(a_ref[...].astype(BF16), bv.astype(BF16), dims, preferred_element_type=F32)
        if epi == "relu2":
            r = jnp.maximum(r, 0.0)
            outs[0][...] = r.astype(outs[0].dtype)
            outs[1][...] = (r * r).astype(outs[1].dtype)
        else:
            if epi == "mul2aux":
                r = r * (2.0 * aux_ref[...].astype(F32))
            if has_res:
                r = r + res_ref[...]
            outs[0][...] = r.astype(outs[0].dtype)

    in_specs = [a_spec, b_spec]
    args = [a, b]
    if has_res:
        in_specs.append(o_spec)
        args.append(res)
    if has_aux:
        in_specs.append(o_spec)
        args.append(aux)
    if after is not None:
        in_specs.append(pl.BlockSpec(memory_space=pl.ANY))
        args.append(after)
    out_shape = [jax.ShapeDtypeStruct(o_shape, out_dtype)] * n_out
    out = pl.pallas_call(
        body, name=name, grid=(m // tm, n // tn),
        in_specs=in_specs, out_specs=[o_spec] * n_out, out_shape=out_shape, compiler_params=_cparams(),
    )(*args)
    return out if n_out == 2 else out[0]


def _rmsnorm_fwd(xarr, colblk, width, g, name, after=None):
    rows = xarr.shape[0]
    tm = min(TM, rows)

    def body(x_ref, g_ref, *rest):
        y_ref = rest[-1]
        x = x_ref[...].astype(F32)
        r = lax.rsqrt(jnp.mean(x * x, axis=1, keepdims=True) + EPS)
        y_ref[...] = (x * r * g_ref[...]).astype(y_ref.dtype)

    in_specs = [pl.BlockSpec((tm, width), lambda i: (i, colblk)), pl.BlockSpec((1, width), lambda i: (0, 0))]
    args = [xarr, g.reshape(1, width)]
    if after is not None:
        in_specs.append(pl.BlockSpec(memory_space=pl.ANY))
        args.append(after)
    return pl.pallas_call(
        body, name=name, grid=(rows // tm,), in_specs=in_specs,
        out_specs=pl.BlockSpec((tm, width), lambda i: (i, 0)),
        out_shape=jax.ShapeDtypeStruct((rows, width), BF16), compiler_params=_cparams(),
    )(*args)


def _rmsnorm_bwd(xarr, colblk, width, g, dy, dres, out_dtype, name):
    rows = xarr.shape[0]
    tm = min(TM, rows)
    has_res = dres is not None

    def body(*refs):
        x_ref, g_ref, dy_ref = refs[0], refs[1], refs[2]
        dres_ref = refs[3] if has_res else None
        dx_ref, dg_ref = refs[-2], refs[-1]
        x = x_ref[...].astype(F32)
        dyv = dy_ref[...].astype(F32)
        r = lax.rsqrt(jnp.mean(x * x, axis=1, keepdims=True) + EPS)
        xh = x * r
        dxh = dyv * g_ref[...]
        dx = r * (dxh - xh * jnp.mean(dxh * xh, axis=1, keepdims=True))
        if has_res:
            dx = dx + dres_ref[...]
        dx_ref[...] = dx.astype(dx_ref.dtype)

        @pl.when(pl.program_id(0) == 0)
        def _():
            dg_ref[...] = jnp.zeros_like(dg_ref)

        dg_ref[...] += jnp.sum(dyv * xh, axis=0, keepdims=True)

    row_spec = pl.BlockSpec((tm, width), lambda i: (i, 0))
    vec_spec = pl.BlockSpec((1, width), lambda i: (0, 0))
    in_specs = [pl.BlockSpec((tm, width), lambda i: (i, colblk)), vec_spec, row_spec]
    args = [xarr, g.reshape(1, width), dy]
    if has_res:
        in_specs.append(row_spec)
        args.append(dres)
    return pl.pallas_call(
        body, name=name, grid=(rows // tm,), in_specs=in_specs, out_specs=[row_spec, vec_spec],
        out_shape=[jax.ShapeDtypeStruct((rows, width), out_dtype), jax.ShapeDtypeStruct((1, width), F32)],
        compiler_params=_cparams(),
    )(*args)


def _loss_head(x, g, tgt):
    rows, width = x.shape
    tm = min(TM, rows)

    def body(x_ref, g_ref, t_ref, dx_ref, dg_ref, loss_ref):
        xv = x_ref[...]
        gv = g_ref[...]
        r = lax.rsqrt(jnp.mean(xv * xv, axis=1, keepdims=True) + EPS)
        xh = xv * r
        err = xh * gv - t_ref[...]
        part = 0.5 * jnp.sum(jnp.mean(err * err, axis=1, keepdims=True), axis=0, keepdims=True)
        dyv = err * (1.0 / width)
        dxh = dyv * gv
        dx_ref[...] = r * (dxh - xh * jnp.mean(dxh * xh, axis=1, keepdims=True))

        @pl.when(pl.program_id(0) == 0)
        def _():
            dg_ref[...] = jnp.zeros_like(dg_ref)
            loss_ref[...] = jnp.zeros_like(loss_ref)

        dg_ref[...] += jnp.sum(dyv * xh, axis=0, keepdims=True)
        loss_ref[...] += jnp.broadcast_to(part, loss_ref.shape)

    row_spec = pl.BlockSpec((tm, width), lambda i: (i, 0))
    vec_spec = pl.BlockSpec((1, width), lambda i: (0, 0))
    return pl.pallas_call(
        body, name="loss_head", grid=(rows // tm,), in_specs=[row_spec, vec_spec, row_spec],
        out_specs=[row_spec, vec_spec, pl.BlockSpec((1, SLOT), lambda i: (0, 0))],
        out_shape=[jax.ShapeDtypeStruct((rows, width), F32), jax.ShapeDtypeStruct((1, width), F32),
                   jax.ShapeDtypeStruct((1, SLOT), F32)],
        compiler_params=_cparams(),
    )(x, g.reshape(1, width), tgt)


def _lane_consts():
    half = 16
    inv = ROPE_THETA ** (-(jnp.arange(half, dtype=F32) * 2.0) / 32)
    lane = jnp.arange(SLOT)
    first = (lane >= 64) & (lane < 80)
    second = (lane >= 80) & (lane < 96)
    inv_lane = jnp.where(first | second, inv[(lane - 64) % half], 0.0)
    rows = [inv_lane, (lane < 64).astype(F32), first.astype(F32), second.astype(F32)]
    rows += [jnp.zeros((SLOT,), F32)] * 4
    return jnp.stack(rows).astype(F32)


def _rope_tables(pos_col, consts):
    rows = pos_col.shape[0]
    tm = min(TM, rows)

    def body(p_ref, k_ref, c_ref, s1_ref, s2_ref):
        ang = p_ref[...] * k_ref[0:1, :]
        cos, sin = jnp.cos(ang), jnp.sin(ang)
        first, second = k_ref[2:3, :], k_ref[3:4, :]
        c_ref[...] = k_ref[1:2, :] + (first + second) * cos
        s1_ref[...] = -first * sin
        s2_ref[...] = second * sin

    spec = pl.BlockSpec((tm, SLOT), lambda i: (i, 0))
    shp = jax.ShapeDtypeStruct((rows, SLOT), F32)
    return pl.pallas_call(
        body, name="rope_tables", grid=(rows // tm,),
        in_specs=[pl.BlockSpec((tm, 1), lambda i: (i, 0)), pl.BlockSpec((8, SLOT), lambda i: (0, 0))],
        out_specs=[spec, spec, spec], out_shape=[shp, shp, shp], compiler_params=_cparams(),
    )(pos_col, consts)


def _rot(xv, c, s1, s2):
    return xv * c + pltpu.roll(xv, SLOT - 16, 1) * s1 + pltpu.roll(xv, 16, 1) * s2


def _rot_t(dy, c, s1, s2):
    return dy * c + pltpu.roll(dy * s1, 16, 1) + pltpu.roll(dy * s2, SLOT - 16, 1)


def _mla_rope_fwd(qraw, kvraw, proj, tabs):
    rows = qraw.shape[0]
    tm = min(256, rows)
    hw = MLA_HEADS * SLOT

    def body(q_ref, kv_ref, kr_ref, c_ref, s1_ref, s2_ref, qo, ko, vo):
        c, s1, s2 = c_ref[...], s1_ref[...], s2_ref[...]
        kr = _rot(kr_ref[...], c, s1, s2)
        for h in range(MLA_HEADS):
            sl = slice(h * SLOT, (h + 1) * SLOT)
            qo[:, sl] = _rot(q_ref[:, sl], c, s1, s2).astype(BF16)
            ko[:, sl] = (kv_ref[:, sl] + kr).astype(BF16)
            vo[:, sl] = kv_ref[:, hw + h * SLOT:hw + (h + 1) * SLOT].astype(BF16)

    tab = pl.BlockSpec((tm, SLOT), lambda i: (i, 0))
    wide = pl.BlockSpec((tm, hw), lambda i: (i, 0))
    shp = jax.ShapeDtypeStruct((rows, hw), BF16)
    return pl.pallas_call(
        body, name="mla_rope_fwd", grid=(rows // tm,),
        in_specs=[wide, pl.BlockSpec((tm, 2 * hw), lambda i: (i, 0)), pl.BlockSpec((tm, SLOT), lambda i: (i, 3)),
                  tab, tab, tab],
        out_specs=[wide, wide, wide], out_shape=[shp, shp, shp], compiler_params=_cparams(),
    )(qraw, kvraw, proj, *tabs)


def _mla_rope_bwd(dq, dk, dv, tabs, consts):
    rows = dq.shape[0]
    tm = min(256, rows)
    hw = MLA_HEADS * SLOT

    def body(dq_ref, dk_ref, dv_ref, c_ref, s1_ref, s2_ref, k_ref, dqo, dkvo, dkro):
        c, s1, s2 = c_ref[...], s1_ref[...], s2_ref[...]
        ksum = jnp.zeros((tm, SLOT), F32)
        for h in range(MLA_HEADS):
            sl = slice(h * SLOT, (h + 1) * SLOT)
            dqo[:, sl] = _rot_t(dq_ref[:, sl], c, s1, s2).astype(BF16)
            dkh = dk_ref[:, sl]
            ksum = ksum + dkh
            dkvo[:, sl] = dkh.astype(BF16)
            dkvo[:, hw + h * SLOT:hw + (h + 1) * SLOT] = dv_ref[:, sl].astype(BF16)
        dkro[...] = _rot_t(ksum, c, s1, s2) * (k_ref[2:3, :] + k_ref[3:4, :])

    tab = pl.BlockSpec((tm, SLOT), lambda i: (i, 0))
    wide = pl.BlockSpec((tm, hw), lambda i: (i, 0))
    return pl.pallas_call(
        body, name="mla_rope_bwd", grid=(rows // tm,),
        in_specs=[wide, wide, wide, tab, tab, tab, pl.BlockSpec((8, SLOT), lambda i: (0, 0))],
        out_specs=[wide, pl.BlockSpec((tm, 2 * hw), lambda i: (i, 0)), tab],
        out_shape=[jax.ShapeDtypeStruct((rows, hw), BF16), jax.ShapeDtypeStruct((rows, 2 * hw), BF16),
                   jax.ShapeDtypeStruct((rows, SLOT), F32)],
        compiler_params=_cparams(),
    )(dq, dk, dv, *tabs, consts)


def _nt(a, b):
    return lax.dot_general(a, b, _DIMS["nt"], preferred_element_type=F32)


def _tn(a, b):
    return lax.dot_general(a, b, _DIMS["tn"], preferred_element_type=F32)


def _nn(a, b):
    return lax.dot_general(a, b, _DIMS["nn"], preferred_element_type=F32)


def _causal(t):
    return lax.broadcasted_iota(jnp.int32, (t, t), 1) <= lax.broadcasted_iota(jnp.int32, (t, t), 0)


def _mla_attn_fwd(q, k, v):
    rows = q.shape[0]
    t = min(TQ_MLA, rows)
    nt = rows // t
    scale = MLA_QK ** -0.5
    wide = MLA_PACK * SLOT

    def body(q_ref, k_ref, v_ref, o_ref, lse_ref, m_sc, l_sc, acc_sc):
        i, j = pl.program_id(1), pl.program_id(2)

        @pl.when(j == 0)
        def _():
            m_sc[...] = jnp.full_like(m_sc, NEG)
            l_sc[...] = jnp.zeros_like(l_sc)
            acc_sc[...] = jnp.zeros_like(acc_sc)

        def step(diagonal):
            for hh in range(MLA_PACK):
                sl = slice(hh * SLOT, (hh + 1) * SLOT)
                s = _nt(q_ref[:, sl], k_ref[:, sl]) * scale
                if diagonal:
                    s = jnp.where(_causal(t), s, NEG)
                m_prev = m_sc[hh]
                m_new = jnp.maximum(m_prev, jnp.max(s, axis=1, keepdims=True))
                p = jnp.exp(s - m_new)
                alpha = jnp.exp(m_prev - m_new)
                l_new = alpha * l_sc[hh] + jnp.sum(p, axis=1, keepdims=True)
                acc = alpha * acc_sc[:, sl] + _nn(p.astype(BF16), v_ref[:, sl])
                if diagonal:
                    o_ref[:, sl] = (acc / l_new).astype(o_ref.dtype)
                    lse_ref[:, sl] = jnp.broadcast_to(m_new + jnp.log(l_new), (t, SLOT))
                else:
                    m_sc[hh] = m_new
                    l_sc[hh] = l_new
                    acc_sc[:, sl] = acc

        @pl.when(j < i)
        def _():
            step(False)

        @pl.when(j == i)
        def _():
            step(True)

    q_spec = pl.BlockSpec((t, wide), lambda h, i, j: (i, h))
    kv_spec = pl.BlockSpec((t, wide), lambda h, i, j: (jnp.minimum(j, i), h))
    return pl.pallas_call(
        body, name="mla_attn_fwd", grid=(MLA_HEADS // MLA_PACK, nt, nt),
        in_specs=[q_spec, kv_spec, kv_spec], out_specs=[q_spec, q_spec],
        out_shape=[jax.ShapeDtypeStruct(q.shape, BF16), jax.ShapeDtypeStruct(q.shape, F32)],
        scratch_shapes=[pltpu.VMEM((MLA_PACK, t, 1), F32), pltpu.VMEM((MLA_PACK, t, 1), F32),
                        pltpu.VMEM((t, wide), F32)],
        compiler_params=_cparams(),
    )(q, k, v)


def _mla_attn_bwd(q, k, v, o, do, lse):
    rows = q.shape[0]
    t = min(TQ_MLA, rows)
    nt = rows // t
    scale = MLA_QK ** -0.5
    wide = MLA_PACK * SLOT

    def body(q_ref, k_ref, v_ref, o_ref, do_ref, lse_ref, dq_ref, dk_ref, dv_ref, dk_sc, dv_sc):
        j, i = pl.program_id(1), pl.program_id(2)

        @pl.when((j == 0) & (i == 0))
        def _():
            dq_ref[...] = jnp.zeros_like(dq_ref)

        @pl.when(i == 0)
        def _():
            dk_sc[...] = jnp.zeros_like(dk_sc)
            dv_sc[...] = jnp.zeros_like(dv_sc)

        def step(diagonal):
            r0 = pl.multiple_of(i * t, t)
            for hh in range(MLA_PACK):
                sl = slice(hh * SLOT, (hh + 1) * SLOT)
                qv, kv, dov = q_ref[:, sl], k_ref[:, sl], do_ref[:, sl]
                s = _nt(qv, kv) * scale
                if diagonal:
                    s = jnp.where(_causal(t), s, NEG)
                p = jnp.exp(s - lse_ref[:, hh * SLOT:hh * SLOT + 1])
                delta = jnp.sum(dov.astype(F32) * o_ref[:, sl].astype(F32), axis=1, keepdims=True)
                dp = _nt(dov, v_ref[:, sl])
                ds = (p * (dp - delta) * scale).astype(BF16)
                dv_sc[:, sl] += _tn(p.astype(BF16), dov)
                dk_sc[:, sl] += _tn(ds, qv)
                dq_ref[pl.ds(r0, t), sl] += _nn(ds, kv)

        @pl.when(i > j)
        def _():
            step(False)

        @pl.when(i == j)
        def _():
            step(True)

        @pl.when(i == nt - 1)
        def _():
            dk_ref[...] = dk_sc[...]
            dv_ref[...] = dv_sc[...]

    q_spec = pl.BlockSpec((t, wide), lambda h, j, i: (jnp.maximum(i, j), h))
    kv_spec = pl.BlockSpec((t, wide), lambda h, j, i: (j, h))
    head_spec = pl.BlockSpec((rows, wide), lambda h, j, i: (0, h))
    shp = jax.ShapeDtypeStruct(q.shape, F32)
    return pl.pallas_call(
        body, name="mla_attn_bwd", grid=(MLA_HEADS // MLA_PACK, nt, nt),
        in_specs=[q_spec, kv_spec, kv_spec, q_spec, q_spec, q_spec],
        out_specs=[head_spec, kv_spec, kv_spec], out_shape=[shp, shp, shp],
        scratch_shapes=[pltpu.VMEM((t, wide), F32), pltpu.VMEM((t, wide), F32)],
        compiler_params=_cparams(),
    )(q, k, v, o, do, lse)


def _swa_specs(t):
    def prev(i):
        return jnp.maximum(i - 1, 0)
    q3 = pl.BlockSpec((t, SWA_GROUP * SLOT), lambda h, i: (i, h))
    kp = pl.BlockSpec((t, SLOT), lambda h, i: (prev(i), SWA_HEADS + h))
    kc = pl.BlockSpec((t, SLOT), lambda h, i: (i, SWA_HEADS + h))
    vp = pl.BlockSpec((t, SLOT), lambda h, i: (prev(i), SWA_HEADS + SWA_KV_HEADS + h))
    vc = pl.BlockSpec((t, SLOT), lambda h, i: (i, SWA_HEADS + SWA_KV_HEADS + h))
    pcol = pl.BlockSpec((t, 1), lambda h, i: (i, 0))
    prow_p = pl.BlockSpec((1, t), lambda h, i: (0, prev(i)))
    prow_c = pl.BlockSpec((1, t), lambda h, i: (0, i))
    return [q3, kp, kc, vp, vc, pcol, prow_p, prow_c]


def _stack(ref):
    return jnp.concatenate([ref[:, g * SLOT:(g + 1) * SLOT] for g in range(SWA_GROUP)], axis=0)


def _swa_logits(q3, kp, kc, pq, pkp, pkc, slope_ref, kvh, i, t):
    r = lax.broadcasted_iota(jnp.int32, (t, t), 0)
    c = lax.broadcasted_iota(jnp.int32, (t, t), 1)
    ok_c = c <= r
    ok_p = (c - r) > jnp.where(i > 0, 0, t)
    dist_p, dist_c = pq - pkp, pq - pkc
    s_p3 = _nt(q3, kp) * (HEAD_DIM ** -0.5)
    s_c3 = _nt(q3, kc) * (HEAD_DIM ** -0.5)
    out = []
    for g in range(SWA_GROUP):
        slope = slope_ref[kvh * SWA_GROUP + g]
        rows = slice(g * t, (g + 1) * t)
        out.append((jnp.where(ok_p, s_p3[rows] - slope * dist_p, NEG),
                    jnp.where(ok_c, s_c3[rows] - slope * dist_c, NEG)))
    return out


def _swa_attn_fwd(proj, pos_col, pos_row, slopes, sinks):
    rows = proj.shape[0]
    t = WINDOW
    hw = SWA_HEADS * SLOT

    def body(slope_ref, sink_ref, q_ref, kp_ref, kc_ref, vp_ref, vc_ref, pq_ref, pkp_ref, pkc_ref, o_ref, lse_ref):
        kvh, i = pl.program_id(0), pl.program_id(1)
        logits = _swa_logits(_stack(q_ref), kp_ref[...], kc_ref[...], pq_ref[...], pkp_ref[...], pkc_ref[...],
                             slope_ref, kvh, i, t)
        e_p, e_c, norm = [], [], []
        for g, (s_p, s_c) in enumerate(logits):
            sink = sink_ref[kvh * SWA_GROUP + g]
            m = jnp.maximum(jnp.maximum(jnp.max(s_p, axis=1, keepdims=True), jnp.max(s_c, axis=1, keepdims=True)),
                            sink)
            ep, ec = jnp.exp(s_p - m), jnp.exp(s_c - m)
            l = jnp.sum(ep, axis=1, keepdims=True) + jnp.sum(ec, axis=1, keepdims=True) + jnp.exp(sink - m)
            e_p.append(ep.astype(BF16))
            e_c.append(ec.astype(BF16))
            norm.append(l)
            lse_ref[:, g * SLOT:(g + 1) * SLOT] = jnp.broadcast_to(m + jnp.log(l), (t, SLOT))
        acc = _nn(jnp.concatenate(e_p, axis=0), vp_ref[...]) + _nn(jnp.concatenate(e_c, axis=0), vc_ref[...])
        for g in range(SWA_GROUP):
            o_ref[:, g * SLOT:(g + 1) * SLOT] = (acc[g * t:(g + 1) * t] / norm[g]).astype(o_ref.dtype)

    smem = pl.BlockSpec(memory_space=pltpu.SMEM)
    out_spec = pl.BlockSpec((t, SWA_GROUP * SLOT), lambda h, i: (i, h))
    return pl.pallas_call(
        body, name="swa_attn_fwd", grid=(SWA_KV_HEADS, rows // t),
        in_specs=[smem, smem] + _swa_specs(t), out_specs=[out_spec, out_spec],
        out_shape=[jax.ShapeDtypeStruct((rows, hw), BF16), jax.ShapeDtypeStruct((rows, hw), F32)],
        compiler_params=_cparams(),
    )(slopes, sinks, proj, proj, proj, proj, proj, pos_col, pos_row, pos_row)


def _swa_attn_bwd(proj, o, do, lse, pos_col, pos_row, slopes, sinks):
    rows = proj.shape[0]
    t = WINDOW
    hw = SWA_HEADS * SLOT
    scale = HEAD_DIM ** -0.5

    def body(slope_ref, sink_ref, q_ref, kp_ref, kc_ref, vp_ref, vc_ref, pq_ref, pkp_ref, pkc_ref,
             o_ref, do_ref, lse_ref, dq_ref, dk_ref, dv_ref, dsink_ref):
        kvh, i = pl.program_id(0), pl.program_id(1)

        @pl.when(i == 0)
        def _():
            dk_ref[...] = jnp.zeros_like(dk_ref)
            dv_ref[...] = jnp.zeros_like(dv_ref)
            dsink_ref[...] = jnp.zeros_like(dsink_ref)

        q3, do3 = _stack(q_ref), _stack(do_ref)
        logits = _swa_logits(q3, kp_ref[...], kc_ref[...], pq_ref[...], pkp_ref[...], pkc_ref[...],
                             slope_ref, kvh, i, t)
        dp_p3, dp_c3 = _nt(do3, vp_ref[...]), _nt(do3, vc_ref[...])
        p_p, p_c, ds_p, ds_c = [], [], [], []
        for g, (s_p, s_c) in enumerate(logits):
            sl = slice(g * SLOT, (g + 1) * SLOT)
            rws = slice(g * t, (g + 1) * t)
            lse_g = lse_ref[:, g * SLOT:g * SLOT + 1]
            pp, pc = jnp.exp(s_p - lse_g), jnp.exp(s_c - lse_g)
            delta = jnp.sum(do_ref[:, sl].astype(F32) * o_ref[:, sl].astype(F32), axis=1, keepdims=True)
            p_p.append(pp.astype(BF16))
            p_c.append(pc.astype(BF16))
            ds_p.append((pp * (dp_p3[rws] - delta)).astype(BF16))
            ds_c.append((pc * (dp_c3[rws] - delta)).astype(BF16))
            sink = sink_ref[kvh * SWA_GROUP + g]
            dsink = -jnp.sum(jnp.exp(sink - lse_g) * delta, axis=0, keepdims=True)
            dsink_ref[g * 8:(g + 1) * 8, :] += jnp.broadcast_to(dsink, (8, SLOT))
        p_p3, p_c3 = jnp.concatenate(p_p, axis=0), jnp.concatenate(p_c, axis=0)
        ds_p3, ds_c3 = jnp.concatenate(ds_p, axis=0), jnp.concatenate(ds_c, axis=0)
        dq3 = (_nn(ds_p3, kp_ref[...]) + _nn(ds_c3, kc_ref[...])) * scale
        for g in range(SWA_GROUP):
            dq_ref[:, g * SLOT:(g + 1) * SLOT] = dq3[g * t:(g + 1) * t]
        r_c = pl.multiple_of(i * t, t)
        dk_ref[pl.ds(r_c, t), :] += _tn(ds_c3, q3) * scale
        dv_ref[pl.ds(r_c, t), :] += _tn(p_c3, do3)

        @pl.when(i > 0)
        def _():
            r_p = pl.multiple_of((i - 1) * t, t)
            dk_ref[pl.ds(r_p, t), :] += _tn(ds_p3, q3) * scale
            dv_ref[pl.ds(r_p, t), :] += _tn(p_p3, do3)

    smem = pl.BlockSpec(memory_space=pltpu.SMEM)
    qlike = pl.BlockSpec((t, SWA_GROUP * SLOT), lambda h, i: (i, h))
    kv_out = pl.BlockSpec((rows, SLOT), lambda h, i: (0, h))
    return pl.pallas_call(
        body, name="swa_attn_bwd", grid=(SWA_KV_HEADS, rows // t),
        in_specs=[smem, smem] + _swa_specs(t) + [qlike, qlike, qlike],
        out_specs=[qlike, kv_out, kv_out, pl.BlockSpec((SWA_GROUP * 8, SLOT), lambda h, i: (h, 0))],
        out_shape=[jax.ShapeDtypeStruct((rows, hw), F32), jax.ShapeDtypeStruct((rows, SWA_KV_HEADS * SLOT), F32),
                   jax.ShapeDtypeStruct((rows, SWA_KV_HEADS * SLOT), F32),
                   jax.ShapeDtypeStruct((SWA_HEADS * 8, SLOT), F32)],
        compiler_params=_cparams(),
    )(slopes, sinks, proj, proj, proj, proj, proj, pos_col, pos_row, pos_row, o, do, lse)


def _cross_attn_fwd(proj, qoff, kvmem):
    rows = proj.shape[0]
    t = min(TQ_CROSS, rows)

    def body(q_ref, k_ref, v_ref, o_ref):
        s = _nt(q_ref[...].astype(BF16), k_ref[...]) * (HEAD_DIM ** -0.5)
        e = jnp.exp(s - jnp.max(s, axis=1, keepdims=True))
        p = e / jnp.sum(e, axis=1, keepdims=True)
        o_ref[...] = _nn(p.astype(BF16), v_ref[...]).astype(o_ref.dtype)

    return pl.pallas_call(
        body, name="cross_attn_fwd", grid=(rows // t, MEM_HEADS),
        in_specs=[pl.BlockSpec((t, SLOT), lambda i, h: (i, qoff + h)),
                  pl.BlockSpec((N_MEM, SLOT), lambda i, h: (0, h)),
                  pl.BlockSpec((N_MEM, SLOT), lambda i, h: (0, MEM_HEADS + h))],
        out_specs=pl.BlockSpec((t, SLOT), lambda i, h: (i, h)),
        out_shape=jax.ShapeDtypeStruct((rows, MEM_HEADS * SLOT), BF16), compiler_params=_cparams(),
    )(proj, kvmem, kvmem)


def _cross_attn_bwd(proj, qoff, kvmem, do):
    rows = proj.shape[0]
    t = min(TQ_CROSS, rows)
    scale = HEAD_DIM ** -0.5

    def body(q_ref, k_ref, v_ref, do_ref, dq_ref, dk_ref, dv_ref):
        @pl.when(pl.program_id(1) == 0)
        def _():
            dk_ref[...] = jnp.zeros_like(dk_ref)
            dv_ref[...] = jnp.zeros_like(dv_ref)

        qv, kv, dov = q_ref[...].astype(BF16), k_ref[...], do_ref[...]
        s = _nt(qv, kv) * scale
        e = jnp.exp(s - jnp.max(s, axis=1, keepdims=True))
        p = e / jnp.sum(e, axis=1, keepdims=True)
        dp = _nt(dov, v_ref[...])
        ds = (p * (dp - jnp.sum(p * dp, axis=1, keepdims=True))).astype(BF16)
        dq_ref[...] = _nn(ds, kv) * scale
        dk_ref[...] += _tn(ds, qv) * scale
        dv_ref[...] += _tn(p.astype(BF16), dov)

    mem_out = pl.BlockSpec((N_MEM, SLOT), lambda h, i: (0, h))
    return pl.pallas_call(
        body, name="cross_attn_bwd", grid=(MEM_HEADS, rows // t),
        in_specs=[pl.BlockSpec((t, SLOT), lambda h, i: (i, qoff + h)),
                  pl.BlockSpec((N_MEM, SLOT), lambda h, i: (0, h)),
                  pl.BlockSpec((N_MEM, SLOT), lambda h, i: (0, MEM_HEADS + h)),
                  pl.BlockSpec((t, SLOT), lambda h, i: (i, h))],
        out_specs=[pl.BlockSpec((t, SLOT), lambda h, i: (i, h)), mem_out, mem_out],
        out_shape=[jax.ShapeDtypeStruct((rows, MEM_HEADS * SLOT), F32),
                   jax.ShapeDtypeStruct((N_MEM, MEM_HEADS * SLOT), F32),
                   jax.ShapeDtypeStruct((N_MEM, MEM_HEADS * SLOT), F32)],
        compiler_params=_cparams(),
    )(proj, kvmem, kvmem, do)


def _place():
    return lax.axis_index("x"), lax.axis_index("y"), lax.axis_index("c")


def _flip(v, bit):
    return 1 - v if bit else v


def _all_gather(blocks, name):
    nb = len(blocks)

    def body(*refs):
        x_refs, out_refs = refs[:nb], refs[nb:2 * nb]
        send_sems, recv_sems, local_sems = refs[2 * nb:]
        x, y, c = _place()
        me, sibling = (x, y, c), (x, y, 1 - c)
        chips = [(1 - x, y), (x, 1 - y), (1 - x, 1 - y)]

        def copy(b, k, blk, to, from_input=False):
            slot = out_refs[b].at[4 * blk[0] + 2 * blk[1] + blk[2]]
            return pltpu.make_async_remote_copy(
                src_ref=x_refs[b] if from_input else slot, dst_ref=slot,
                send_sem=send_sems.at[b, k], recv_sem=recv_sems.at[b, k],
                device_id=to, device_id_type=pl.DeviceIdType.MESH)

        mine = [pltpu.make_async_copy(x_refs[b], out_refs[b].at[4 * x + 2 * y + c], local_sems.at[b])
                for b in range(nb)]
        for cp in mine:
            cp.start()
        first = []
        for b in range(nb):
            first.append(copy(b, 0, me, sibling, from_input=True))
            first += [copy(b, 1 + n, me, (*chip, c), from_input=True) for n, chip in enumerate(chips)]
        for cp in first:
            cp.start()
        passed = []
        for n, chip in enumerate(chips):
            for b in range(nb):
                copy(b, 1 + n, (*chip, c), me).wait_recv()
                passed.append(copy(b, 4 + n, (*chip, c), sibling))
                passed[-1].start()
        for b in range(nb):
            copy(b, 0, sibling, me).wait_recv()
            for n, chip in enumerate(chips):
                copy(b, 4 + n, (*chip, 1 - c), me).wait_recv()
        for cp in first + passed:
            cp.wait_send()
        for cp in mine:
            cp.wait()

    any_spec = pl.BlockSpec(memory_space=pl.ANY)
    return pl.pallas_call(
        body, name=name, in_specs=[any_spec] * nb, out_specs=[any_spec] * nb,
        out_shape=[jax.ShapeDtypeStruct((N_DEV,) + blk.shape, blk.dtype) for blk in blocks],
        scratch_shapes=[pltpu.SemaphoreType.DMA((nb, 7)), pltpu.SemaphoreType.DMA((nb, 7)),
                        pltpu.SemaphoreType.DMA((nb,))],
    )(*blocks)


def _peers(x, y, c):
    out = []
    for n in range(1, N_DEV):
        peer = (_flip(x, n & 4), _flip(y, n & 2), _flip(c, n & 1))
        out.append((n - 1, peer, 4 * peer[0] + 2 * peer[1] + peer[2]))
    return out


_HBM = pl.BlockSpec(memory_space=pltpu.HBM)
_SEM = pl.BlockSpec(memory_space=pltpu.SEMAPHORE)


def _exchange_start(srcs, scatter, name, after=None):
    ns = len(srcs)
    lands = [lax.empty(s.shape if scatter else (N_DEV,) + s.shape, s.dtype) for s in srcs]

    def body(*refs):
        src_refs, land_refs = refs[:ns], refs[ns:2 * ns]
        pos = 2 * ns + (1 if after is not None else 0)
        send_sems, recv_sems, token = refs[pos], refs[pos + 1], refs[-1]
        x, y, c = _place()
        my_idx = 4 * x + 2 * y + c
        for col, peer, peer_idx in _peers(x, y, c):
            for b in range(ns):
                pltpu.make_async_remote_copy(
                    src_ref=src_refs[b].at[peer_idx] if scatter else src_refs[b], dst_ref=land_refs[b].at[my_idx],
                    send_sem=send_sems.at[b * (N_DEV - 1) + col], recv_sem=recv_sems.at[b * (N_DEV - 1) + col],
                    device_id=peer, device_id_type=pl.DeviceIdType.MESH).start()
        token[...] = jnp.zeros_like(token)

    args = [pltpu.with_memory_space_constraint(a, pltpu.HBM) for a in list(srcs) + lands]
    in_specs = [_HBM] * (2 * ns)
    if after is not None:
        args.append(after)
        in_specs.append(pl.BlockSpec(memory_space=pl.ANY))
    out = pl.pallas_call(
        body, name=name, in_specs=in_specs,
        out_specs=[_SEM, _SEM] + [_HBM] * (2 * ns) + [pl.BlockSpec(memory_space=pltpu.VMEM)],
        out_shape=[pltpu.SemaphoreType.DMA((ns * (N_DEV - 1),)), pltpu.SemaphoreType.DMA((ns * (N_DEV - 1),))]
        + [pltpu.HBM(a.shape, a.dtype) for a in list(srcs) + lands] + [jax.ShapeDtypeStruct((8, SLOT), F32)],
        input_output_aliases={k: 2 + k for k in range(2 * ns)},
        compiler_params=pltpu.CompilerParams(has_side_effects=pltpu.SideEffectType.DATAFLOW_SIDE_EFFECTING),
    )(*args)
    return (out[0], out[1], out[2:2 + ns], out[2 + ns:2 + 2 * ns], scatter), out[-1]


def _exchange_wait(handle, after, name):
    send_sems, recv_sems, srcs, lands, scatter = handle
    ns = len(srcs)

    def body(*refs):
        src_refs, land_refs = refs[:ns], refs[ns:2 * ns]
        send_ref, recv_ref = refs[2 * ns], refs[2 * ns + 1]
        x, y, c = _place()
        for col, peer, peer_idx in _peers(x, y, c):
            for b in range(ns):
                copy = pltpu.make_async_remote_copy(
                    src_ref=src_refs[b].at[peer_idx] if scatter else src_refs[b], dst_ref=land_refs[b].at[peer_idx],
                    send_sem=send_ref.at[b * (N_DEV - 1) + col], recv_sem=recv_ref.at[b * (N_DEV - 1) + col],
                    device_id=peer, device_id_type=pl.DeviceIdType.MESH)
                copy.wait_send()
                copy.wait_recv()

    out = pl.pallas_call(
        body, name=name, in_specs=[_HBM] * (2 * ns) + [_SEM, _SEM, pl.BlockSpec(memory_space=pl.ANY)],
        out_specs=[_HBM] * (2 * ns),
        out_shape=[pltpu.HBM(a.shape, a.dtype) for a in list(srcs) + list(lands)],
        input_output_aliases={k: k for k in range(2 * ns)},
        compiler_params=pltpu.CompilerParams(has_side_effects=pltpu.SideEffectType.DATAFLOW_SIDE_EFFECTING),
    )(*srcs, *lands, send_sems, recv_sems, after)
    my_idx = 4 * lax.axis_index("x") + 2 * lax.axis_index("y") + lax.axis_index("c")
    landed = []
    for src, land in zip(out[:ns], out[ns:]):
        own = lax.dynamic_index_in_dim(src, my_idx, 0, keepdims=True) if scatter else src[None]
        landed.append(lax.dynamic_update_index_in_dim(land, own, my_idx, 0))
    return landed


def _adamw(parts, w, m, v, name):
    lyr, rows, cols = w.shape
    tr = ADAM_ROWS if cols > 512 else 2 * ADAM_ROWS
    while rows % tr:
        tr //= 2
    tr = min(tr, rows)

    def body(p_ref, w_ref, m_ref, v_ref, g_out, d_out, m_out, v_out):
        g = p_ref[0].astype(F32)
        for s in range(1, N_DEV):
            g = g + p_ref[s].astype(F32)
        m2 = ADAM_B1 * m_ref[...] + (1.0 - ADAM_B1) * g
        v2 = ADAM_B2 * v_ref[...] + (1.0 - ADAM_B2) * (g * g)
        m_hat = m2 / (1.0 - ADAM_B1 ** ADAM_STEP)
        v_hat = v2 / (1.0 - ADAM_B2 ** ADAM_STEP)
        g_out[...] = g
        d_out[...] = -ADAM_LR * (m_hat / (jnp.sqrt(v_hat) + ADAM_EPS) + ADAM_WD * w_ref[...])
        m_out[...] = m2
        v_out[...] = v2

    spec = pl.BlockSpec((None, tr, cols), lambda l, i: (l, i, 0))
    shp = jax.ShapeDtypeStruct((lyr, rows, cols), F32)
    return pl.pallas_call(
        body, name=name, grid=(lyr, rows // tr),
        in_specs=[pl.BlockSpec((None, N_DEV, tr, cols), lambda l, i: (l, 0, i, 0)), spec, spec, spec],
        out_specs=[spec] * 4, out_shape=[shp] * 4, compiler_params=_cparams(),
    )(parts, w, m, v)


def _pack(arrays, lanes, row_mult, dtype):
    flat = jnp.concatenate([a.reshape(-1).astype(dtype) for a in arrays])
    unit = lanes * row_mult
    total = -(-flat.shape[0] // unit) * unit
    return jnp.pad(flat, (0, total - flat.shape[0])).reshape(total // lanes, lanes)


def _unpack(packed, shapes):
    flat = packed.reshape(-1)
    out, off = [], 0
    for shp in shapes:
        n = 1
        for d in shp:
            n *= d
        out.append(flat[off:off + n].reshape(shp))
        off += n
    return out


def _pad_slots(w, axis):
    axis = axis % w.ndim
    n = w.shape[axis] // HEAD_DIM
    shp = w.shape[:axis] + (n, HEAD_DIM) + w.shape[axis + 1:]
    pad = [(0, 0)] * (w.ndim + 1)
    pad[axis + 1] = (0, SLOT - HEAD_DIM)
    return jnp.pad(w.reshape(shp), pad).reshape(w.shape[:axis] + (n * SLOT,) + w.shape[axis + 1:])


def _unpad_slots(w, axis, keep=HEAD_DIM):
    axis = axis % w.ndim
    n = w.shape[axis] // SLOT
    shp = w.shape[:axis] + (n, SLOT) + w.shape[axis + 1:]
    idx = [slice(None)] * (w.ndim + 1)
    idx[axis + 1] = slice(0, keep)
    return w.reshape(shp)[tuple(idx)].reshape(w.shape[:axis] + (n * keep,) + w.shape[axis + 1:])


def _mla_in_pad(w):
    z = functools.partial(jnp.zeros, dtype=w.dtype)
    rows = w.shape[0]
    return jnp.concatenate([w[:, :384], z((rows, 64)), w[:, 640:672], z((rows, 32)), w[:, 384:640],
                            _pad_slots(w[:, 672:], 1)], axis=1)


def _mla_in_unpad(d):
    return jnp.concatenate([d[:, :384], d[:, 512:768], d[:, 448:480], _unpad_slots(d[:, 768:], 1)], axis=1)


def _mla_uq_pad(w):
    return jnp.pad(w.reshape(w.shape[0], MLA_HEADS, MLA_QK), ((0, 0), (0, 0), (0, SLOT - MLA_QK))).reshape(
        w.shape[0], MLA_HEADS * SLOT)


def _mla_ukv_pad(w):
    w3 = w.reshape(w.shape[0], MLA_HEADS, 2 * HEAD_DIM)
    pad = ((0, 0), (0, 0), (0, SLOT - HEAD_DIM))
    k = jnp.pad(w3[:, :, :HEAD_DIM], pad).reshape(w.shape[0], -1)
    v = jnp.pad(w3[:, :, HEAD_DIM:], pad).reshape(w.shape[0], -1)
    return jnp.concatenate([k, v], axis=1)


def _mla_ukv_unpad(d):
    hw = MLA_HEADS * SLOT
    k = d[:, :hw].reshape(d.shape[0], MLA_HEADS, SLOT)[:, :, :HEAD_DIM]
    v = d[:, hw:].reshape(d.shape[0], MLA_HEADS, SLOT)[:, :, :HEAD_DIM]
    return jnp.concatenate([k, v], axis=2).reshape(d.shape[0], MLA_HEADS * 2 * HEAD_DIM)


def _join(gathered, axis):
    nd, a, b = gathered.shape
    if axis == 1:
        return gathered.reshape(nd * a, b)
    return gathered.transpose(1, 0, 2).reshape(a, nd * b)


def _split(full, axis):
    r, c = full.shape
    if axis == 1:
        return full.reshape(N_DEV, r // N_DEV, c).astype(BF16)
    return full.reshape(r, N_DEV, c // N_DEV).transpose(1, 0, 2).astype(BF16)


def kernel(x, mem, positions, attn_norm_g, mlp_norm_g, mem_norm_g, final_norm_g, mla_w_in, mla_q_norm_g, mla_kv_norm_g, mla_w_uq, mla_w_ukv, swa_w_in, swa_sinks, w_mem_kv, w_o, mlp_w_up, mlp_w_down, loss_target, m_attn_norm_g, m_mlp_norm_g, m_mem_norm_g, m_final_norm_g, m_mla_w_in, m_mla_q_norm_g, m_mla_kv_norm_g, m_mla_w_uq, m_mla_w_ukv, m_swa_w_in, m_swa_sinks, m_w_mem_kv, m_w_o, m_mlp_w_up, m_mlp_w_down, v_attn_norm_g, v_mlp_norm_g, v_mem_norm_g, v_final_norm_g, v_mla_w_in, v_mla_q_norm_g, v_mla_kv_norm_g, v_mla_w_uq, v_mla_w_ukv, v_swa_w_in, v_swa_sinks, v_w_mem_kv, v_w_o, v_mlp_w_up, v_mlp_w_down):
    given = dict(locals())
    seq = x.shape[1]
    x0 = x.reshape(seq, D_MODEL)
    tgt = loss_target.reshape(seq, D_MODEL)
    mem0 = mem.reshape(N_MEM, D_MODEL)
    pos = positions.reshape(seq).astype(F32)
    pos_col, pos_row = pos.reshape(seq, 1), pos.reshape(1, seq)

    def layer_names(i):
        mixer = ("mla_w_in", "mla_w_uq", "mla_w_ukv") if i % 2 == 0 else ("swa_w_in",)
        return [(n, i // 2) for n in mixer] + [(n, i) for n in ("w_mem_kv", "w_o", "mlp_w_up", "mlp_w_down")]

    def local_weights(names):
        return [given[n][l].astype(BF16) for n, l in names]

    first_attn, first_mlp = layer_names(0)[:-2], layer_names(0)[-2:]
    weights = [dict(zip([n for n, _ in first_attn], _all_gather(local_weights(first_attn), "gather_weights_first")))]
    coming_mlp, _ = _exchange_start(local_weights(first_mlp), False, "gather_weights_start_0",
                                    after=weights[0]["w_o"])

    consts = _lane_consts()
    tabs = _rope_tables(pos_col, consts)
    slopes = 2.0 ** (-8.0 * (jnp.arange(SWA_HEADS, dtype=F32) + 1.0) / SWA_HEADS)

    mem_n = _rmsnorm_fwd(mem0, 0, D_MODEL, mem_norm_g, "rmsnorm_fwd_mem")

    saved = []
    xc = x0
    for i in range(DEPTH):
        j = i // 2
        wts = weights[i]
        s = {"x_in": xc}
        token = None
        if i + 1 < DEPTH:
            coming, token = _exchange_start(local_weights(layer_names(i + 1)), False,
                                            "gather_weights_start_%d" % (i + 1), after=wts["w_o"])
        hn = _rmsnorm_fwd(xc, 0, D_MODEL, attn_norm_g[i], "rmsnorm_fwd", after=token)
        if i % 2 == 0:
            w_in = _mla_in_pad(_join(wts["mla_w_in"], 1))
            w_uq = _mla_uq_pad(_join(wts["mla_w_uq"], 2))
            w_kv = _mla_ukv_pad(_join(wts["mla_w_ukv"], 2))
            proj = _mm(hn, w_in, "nn", F32, "mm_mla_in")
            cqn = _rmsnorm_fwd(proj, 0, MLA_Q_RANK, mla_q_norm_g[j], "rmsnorm_fwd_q")
            ckvn = _rmsnorm_fwd(proj, 2, MLA_KV_RANK, mla_kv_norm_g[j], "rmsnorm_fwd_kv")
            qraw = _mm(cqn, w_uq, "nn", F32, "mm_mla_uq")
            kvraw = _mm(ckvn, w_kv, "nn", F32, "mm_mla_ukv")
            q, k, v = _mla_rope_fwd(qraw, kvraw, proj, tabs)
            o, lse = _mla_attn_fwd(q, k, v)
            qoff = MLA_QOFF
            s.update(w_uq=w_uq, w_kv=w_kv, cqn=cqn, ckvn=ckvn, q=q, k=k, v=v)
        else:
            w_in = _pad_slots(_join(wts["swa_w_in"], 2), 1)
            proj = _mm(hn, w_in, "nn", BF16, "mm_swa_in")
            o, lse = _swa_attn_fwd(proj, pos_col, pos_row, slopes, swa_sinks[j])
            qoff = SWA_QOFF
        w_mem = _pad_slots(_join(wts["w_mem_kv"], 1), 1)
        w_out = _pad_slots(_join(wts["w_o"], 1), 0)
        w_o_mix, w_o_cross = w_out[:SWA_HEADS * SLOT], w_out[SWA_HEADS * SLOT:]
        kvmem = _mm(mem_n, w_mem, "nn", BF16, "mm_mem_kv")
        cross = _cross_attn_fwd(proj, qoff, kvmem)
        x1 = _mm(o, w_o_mix, "nn", F32, "mm_o_mix", res=xc)
        x1 = _mm(cross, w_o_cross, "nn", F32, "mm_o_cross", res=x1)
        hn2 = _rmsnorm_fwd(x1, 0, D_MODEL, mlp_norm_g[i], "rmsnorm_fwd")
        if i == 0:
            wts.update(zip([n for n, _ in first_mlp], _exchange_wait(coming_mlp, hn2, "gather_weights_wait_0")))
        act, act2 = _mm(hn2, wts["mlp_w_up"], "nn", BF16, "mm_mlp_up", epi="relu2", b_blk="cols")
        xc = _mm(act2, wts["mlp_w_down"], "nn", F32, "mm_mlp_down", res=x1, b_blk="rows")
        s.update(hn=hn, w_in=w_in, proj=proj, o=o, lse=lse, qoff=qoff, w_mem=w_mem, w_o_mix=w_o_mix,
                 w_o_cross=w_o_cross, kvmem=kvmem, cross=cross, x1=x1, hn2=hn2, act=act, act2=act2)
        saved.append(s)
        if i + 1 < DEPTH:
            got = _exchange_wait(coming, xc, "gather_weights_wait_%d" % (i + 1))
            weights.append(dict(zip([n for n, _ in layer_names(i + 1)], got)))

    dx, dg_final, loss_part = _loss_head(xc, final_norm_g, tgt)
    loss = lax.psum(loss_part[0, 0], MESH_AXES)

    gains = {n: [None] * DEPTH for n in ("attn_norm_g", "mlp_norm_g")}
    for n in ("mla_q_norm_g", "mla_kv_norm_g", "swa_sinks"):
        gains[n] = [None] * 2
    leaving = {}
    token = None
    dmem_n = None
    for i in reversed(range(DEPTH)):
        j = i // 2
        s = saved[i]
        wts = weights[i]
        out = {}
        du = _mm(dx, wts["mlp_w_down"], "nt", BF16, "mm_mlp_down_dx", aux=s["act"], epi="mul2aux", b_blk="rows",
                 after=token)
        out["mlp_w_down"] = _mm(s["act2"], dx, "tn", BF16, "mm_mlp_down_dw", o_blk="rows")
        dhn2 = _mm(du, wts["mlp_w_up"].transpose(0, 2, 1), "nn", F32, "mm_mlp_up_dx", b_blk="rows")
        out["mlp_w_up"] = _mm(s["hn2"], du, "tn", BF16, "mm_mlp_up_dw", o_blk="cols")
        dx1, dg = _rmsnorm_bwd(s["x1"], 0, D_MODEL, mlp_norm_g[i], dhn2, dx, F32, "rmsnorm_bwd")
        gains["mlp_norm_g"][i] = dg[0]
        leaving[(i, "mlp")], token = _exchange_start([out["mlp_w_up"], out["mlp_w_down"]], True,
                                                     "exchange_grads_mlp_start_%d" % i)

        do = _mm(dx1, s["w_o_mix"], "nt", BF16, "mm_o_mix_dx", after=token)
        dcross = _mm(dx1, s["w_o_cross"], "nt", BF16, "mm_o_cross_dx")
        dw_o = jnp.concatenate([_mm(s["o"], dx1, "tn", F32, "mm_o_mix_dw"),
                                _mm(s["cross"], dx1, "tn", F32, "mm_o_cross_dw")], axis=0)
        out["w_o"] = _split(_unpad_slots(dw_o, 0), 1)
        dqc, dkm, dvm = _cross_attn_bwd(s["proj"], s["qoff"], s["kvmem"], dcross)
        dkvmem = jnp.concatenate([dkm, dvm], axis=1).astype(BF16)
        out["w_mem_kv"] = _split(_unpad_slots(_mm(mem_n, dkvmem, "tn", F32, "mm_mem_kv_dw"), 1), 1)
        dmem_n = _mm(dkvmem, s["w_mem"], "nt", F32, "mm_mem_kv_dx" if dmem_n is None else "mm_mem_kv_dx_acc",
                     res=dmem_n)

        if i % 2 == 0:
            dq, dk, dv = _mla_attn_bwd(s["q"], s["k"], s["v"], s["o"], do, s["lse"])
            dqraw, dkv, dkr = _mla_rope_bwd(dq, dk, dv, tabs, consts)
            dcqn = _mm(dqraw, s["w_uq"], "nt", F32, "mm_mla_uq_dx")
            out["mla_w_uq"] = _split(_unpad_slots(_mm(s["cqn"], dqraw, "tn", F32, "mm_mla_uq_dw"), 1, MLA_QK), 2)
            dckvn = _mm(dkv, s["w_kv"], "nt", F32, "mm_mla_ukv_dx")
            out["mla_w_ukv"] = _split(_mla_ukv_unpad(_mm(s["ckvn"], dkv, "tn", F32, "mm_mla_ukv_dw")), 2)
            dcq, dg = _rmsnorm_bwd(s["proj"], 0, MLA_Q_RANK, mla_q_norm_g[j], dcqn, None, BF16, "rmsnorm_bwd_q")
            gains["mla_q_norm_g"][j] = dg[0]
            dckv, dg = _rmsnorm_bwd(s["proj"], 2, MLA_KV_RANK, mla_kv_norm_g[j], dckvn, None, BF16, "rmsnorm_bwd_kv")
            gains["mla_kv_norm_g"][j] = dg[0]
            dproj = jnp.concatenate([dcq, dkr.astype(BF16), dckv, dqc.astype(BF16)], axis=1)
            dhn = _mm(dproj, s["w_in"], "nt", F32, "mm_mla_in_dx")
            out["mla_w_in"] = _split(_mla_in_unpad(_mm(s["hn"], dproj, "tn", F32, "mm_mla_in_dw")), 1)
        else:
            dq, dk, dv, dsink = _swa_attn_bwd(s["proj"], s["o"], do, s["lse"], pos_col, pos_row, slopes, swa_sinks[j])
            gains["swa_sinks"][j] = dsink[::8, 0]
            dproj = jnp.concatenate([dq, dk, dv, dqc], axis=1).astype(BF16)
            dhn = _mm(dproj, s["w_in"], "nt", F32, "mm_swa_in_dx")
            out["swa_w_in"] = _split(_unpad_slots(_mm(s["hn"], dproj, "tn", F32, "mm_swa_in_dw"), 1), 2)
        dx, dg = _rmsnorm_bwd(s["x_in"], 0, D_MODEL, attn_norm_g[i], dhn, dx1, F32, "rmsnorm_bwd")
        gains["attn_norm_g"][i] = dg[0]

        leaving[(i, "attn")], token = _exchange_start([out[n] for n, _ in layer_names(i)[:-2]], True,
                                                      "exchange_grads_attn_start_%d" % i)

    _, dg_mem = _rmsnorm_bwd(mem0, 0, D_MODEL, mem_norm_g, dmem_n, None, BF16, "rmsnorm_bwd_mem")
    gains = {n: jnp.stack(g) for n, g in gains.items()}
    gains["mem_norm_g"] = dg_mem[0]
    gains["final_norm_g"] = dg_final[0]

    result = {}

    def adamw_of(names, received):
        for n in names:
            parts = jnp.stack([received[(n, l)] for l in range(given[n].shape[0])])
            for kind, r in enumerate(_adamw(parts, given[n], given["m_" + n], given["v_" + n], "adamw_" + n)):
                result[(kind, n)] = r

    received = {}
    for i in reversed(range(DEPTH)):
        got = _exchange_wait(leaving[(i, "mlp")], dx, "exchange_grads_mlp_wait_%d" % i)
        received[("mlp_w_up", i)], received[("mlp_w_down", i)] = got
    adamw_of(("mlp_w_up", "mlp_w_down"), received)
    for i in reversed(range(DEPTH)):
        got = _exchange_wait(leaving[(i, "attn")], result[(0, "mlp_w_down")], "exchange_grads_attn_wait_%d" % i)
        received.update(zip(layer_names(i)[:-2], got))
    adamw_of([n for n, _ in SHARDED if not n.startswith("mlp")], received)

    rep_shapes = [given[n].shape for n in REPLICATED]
    rep_parts = _all_gather([_pack([gains[n] for n in REPLICATED], SLOT, 8, F32)], "gather_gain_grads")[0]
    rep_packed = [_pack([given[p + n] for n in REPLICATED], SLOT, 8, F32)[None] for p in ("", "m_", "v_")]
    for kind, r in enumerate(_adamw(rep_parts[None], *rep_packed, "adamw_gains")):
        for n, part in zip(REPLICATED, _unpack(r[0], rep_shapes)):
            result[(kind, n)] = part

    outs = [loss, dx.reshape(1, seq, D_MODEL)]
    for kind in range(4):
        outs += [result[(kind, n)] for n in WEIGHT_ORDER]
    return tuple(outs)
```

```python
import functools

import jax
import jax.numpy as jnp
from jax import lax
from jax.experimental import pallas as pl
from jax.experimental.pallas import tpu as pltpu

F32 = jnp.float32
BF16 = jnp.bfloat16

D_MODEL = 1024
D_FF = 4096
N_MEM = 256
DEPTH = 4
SLOT = 128
HEAD_DIM = 64
MLA_HEADS = 12
MLA_QK = 96
MLA_Q_RANK = 384
MLA_KV_RANK = 256
SWA_HEADS = 12
SWA_KV_HEADS = 4
SWA_GROUP = 3
MEM_HEADS = 4
WINDOW = 128
EPS = 1e-6
NEG = -1e30
ROPE_THETA = 10000.0
N_DEV = 8

ADAM_LR = 0.001
ADAM_B1 = 0.9
ADAM_B2 = 0.999
ADAM_EPS = 1e-08
ADAM_WD = 0.01
ADAM_STEP = 10

TM = 512
TQ_MLA = 1024
MLA_PACK = 2
TQ_CROSS = 512
MM_VMEM_BUDGET = 30 * 1024 * 1024
ADAM_ROWS = 128
VMEM_LIMIT = 48 * 1024 * 1024

MESH_AXES = ("x", "y", "c")

MLA_PAD_IN = 384 + SLOT + 256 + MEM_HEADS * SLOT
MLA_QOFF = (384 + SLOT + 256) // SLOT
SWA_PAD_IN = (SWA_HEADS + 2 * SWA_KV_HEADS + MEM_HEADS) * SLOT
SWA_QOFF = SWA_HEADS + 2 * SWA_KV_HEADS

SHARDED = (
    ("mla_w_in", 1), ("mla_w_uq", 2), ("mla_w_ukv", 2), ("swa_w_in", 2),
    ("w_mem_kv", 1), ("w_o", 1), ("mlp_w_up", 2), ("mlp_w_down", 1),
)
REPLICATED = ("attn_norm_g", "mlp_norm_g", "mem_norm_g", "final_norm_g",
              "mla_q_norm_g", "mla_kv_norm_g", "swa_sinks")
WEIGHT_ORDER = ("attn_norm_g", "mlp_norm_g", "mem_norm_g", "final_norm_g", "mla_w_in",
                "mla_q_norm_g", "mla_kv_norm_g", "mla_w_uq", "mla_w_ukv", "swa_w_in",
                "swa_sinks", "w_mem_kv", "w_o", "mlp_w_up", "mlp_w_down")


def _cparams():
    return pltpu.CompilerParams(vmem_limit_bytes=VMEM_LIMIT)


_DIMS = {"nn": (((1,), (0,)), ((), ())), "nt": (((1,), (1,)), ((), ())), "tn": (((0,), (0,)), ((), ()))}


def _mm_tiles(m, n, k, a_bytes, b_bytes, o_bytes, extra_bytes, tm_fixed, tn_fixed):
    best = None
    for tm in ([tm_fixed] if tm_fixed else [t for t in (1024, 512, 256, 128) if m % t == 0] or [m]):
        for tn in ([tn_fixed] if tn_fixed else [t for t in range(1024, 0, -SLOT) if n % t == 0] or [n]):
            need = 2 * (tm * k * a_bytes + k * tn * b_bytes + tm * tn * (o_bytes + extra_bytes))
            if need <= MM_VMEM_BUDGET and (best is None or tm * tn > best[0] * best[1]):
                best = (tm, tn)
    assert best is not None, (m, n, k)
    return best


def _mm(a, b, mode, out_dtype, name, res=None, aux=None, epi=None, b_blk=None, o_blk=None, after=None):
    if b_blk is not None:
        nb, br, bc = b.shape
        b_shape = (nb * br, bc) if b_blk == "rows" else (br, nb * bc)
    else:
        b_shape = b.shape
    if mode == "nn":
        (m, k), (k2, n) = a.shape, b_shape
    elif mode == "nt":
        (m, k), (n, k2) = a.shape, b_shape
    else:
        (k, m), (k2, n) = a.shape, b_shape
    assert k == k2, (a.shape, b_shape, mode)
    k_blocked = b_blk is not None and (b_blk == "rows") == (mode != "nt")
    assert not (k_blocked and mode == "nt")
    tn_fixed = None
    if b_blk is not None and not k_blocked:
        tn_fixed = br if b_blk == "rows" else bc
    if o_blk == "cols":
        tn_fixed = n // N_DEV
    tm_fixed = m // N_DEV if o_blk == "rows" else None
    has_res, has_aux = res is not None, aux is not None
    assert o_blk is None or not (has_res or has_aux)
    n_out = 2 if epi == "relu2" else 1
    tm, tn = _mm_tiles(m, n, k, a.dtype.itemsize, b.dtype.itemsize, n_out * jnp.dtype(out_dtype).itemsize,
                       (4 if has_res else 0) + (aux.dtype.itemsize if has_aux else 0), tm_fixed, tn_fixed)
    dims = _DIMS[mode]
    if mode == "tn":
        a_spec = pl.BlockSpec((k, tm), lambda i, j: (0, i))
    else:
        a_spec = pl.BlockSpec((tm, k), lambda i, j: (i, 0))
    if b_blk is None:
        if mode == "nt":
            b_spec = pl.BlockSpec((tn, k), lambda i, j: (j, 0))
        else:
            b_spec = pl.BlockSpec((k, tn), lambda i, j: (0, j))
    elif k_blocked:
        b_spec = pl.BlockSpec((N_DEV, br, tn), lambda i, j: (0, 0, j))
    elif mode == "nt":
        b_spec = pl.BlockSpec((None, tn, k), lambda i, j: (j, 0, 0))
    else:
        b_spec = pl.BlockSpec((None, k, tn), lambda i, j: (j, 0, 0))
    if o_blk is None:
        o_spec = pl.BlockSpec((tm, tn), lambda i, j: (i, j))
        o_shape = (m, n)
    elif o_blk == "rows":
        o_spec = pl.BlockSpec((None, tm, tn), lambda i, j: (i, 0, j))
        o_shape = (N_DEV, tm, n)
    else:
        o_spec = pl.BlockSpec((None, tm, tn), lambda i, j: (j, i, 0))
        o_shape = (N_DEV, m, tn)

    def body(*refs):
        a_ref, b_ref = refs[0], refs[1]
        pos = 2
        res_ref = aux_ref = None
        if has_res:
            res_ref = refs[pos]
            pos += 1
        if has_aux:
            aux_ref = refs[pos]
            pos += 1
        if after is not None:
            pos += 1
        outs = refs[pos:]
        bv = b_ref[...].reshape(k, tn) if k_blocked else b_ref[...]
        r = lax.dot_general(a_ref[...].astype(BF16), bv.astype(BF16), dims, preferred_element_type=F32)
        if epi == "relu2":
            r = jnp.maximum(r, 0.0)
            outs[0][...] = r.astype(outs[0].dtype)
            outs[1][...] = (r * r).astype(outs[1].dtype)
        else:
            if epi == "mul2aux":
                r = r * (2.0 * aux_ref[...].astype(F32))
            if has_res:
                r = r + res_ref[...]
            outs[0][...] = r.astype(outs[0].dtype)

    in_specs = [a_spec, b_spec]
    args = [a, b]
    if has_res:
        in_specs.append(o_spec)
        args.append(res)
    if has_aux:
        in_specs.append(o_spec)
        args.append(aux)
    if after is not None:
        in_specs.append(pl.BlockSpec(memory_space=pl.ANY))
        args.append(after)
    out_shape = [jax.ShapeDtypeStruct(o_shape, out_dtype)] * n_out
    out = pl.pallas_call(
        body, name=name, grid=(m // tm, n // tn),
        in_specs=in_specs, out_specs=[o_spec] * n_out, out_shape=out_shape, compiler_params=_cparams(),
    )(*args)
    return out if n_out == 2 else out[0]


def _rmsnorm_fwd(xarr, colblk, width, g, name, after=None):
    rows = xarr.shape[0]
    tm = min(TM, rows)

    def body(x_ref, g_ref, *rest):
        y_ref = rest[-1]
        x = x_ref[...].astype(F32)
        r = lax.rsqrt(jnp.mean(x * x, axis=1, keepdims=True) + EPS)
        y_ref[...] = (x * r * g_ref[...]).astype(y_ref.dtype)

    in_specs = [pl.BlockSpec((tm, width), lambda i: (i, colblk)), pl.BlockSpec((1, width), lambda i: (0, 0))]
    args = [xarr, g.reshape(1, width)]
    if after is not None:
        in_specs.append(pl.BlockSpec(memory_space=pl.ANY))
        args.append(after)
    return pl.pallas_call(
        body, name=name, grid=(rows // tm,), in_specs=in_specs,
        out_specs=pl.BlockSpec((tm, width), lambda i: (i, 0)),
        out_shape=jax.ShapeDtypeStruct((rows, width), BF16), compiler_params=_cparams(),
    )(*args)


def _rmsnorm_bwd(xarr, colblk, width, g, dy, dres, out_dtype, name):
    rows = xarr.shape[0]
    tm = min(TM, rows)
    has_res = dres is not None

    def body(*refs):
        x_ref, g_ref, dy_ref = refs[0], refs[1], refs[2]
        dres_ref = refs[3] if has_res else None
        dx_ref, dg_ref = refs[-2], refs[-1]
        x = x_ref[...].astype(F32)
        dyv = dy_ref[...].astype(F32)
        r = lax.rsqrt(jnp.mean(x * x, axis=1, keepdims=True) + EPS)
        xh = x * r
        dxh = dyv * g_ref[...]
        dx = r * (dxh - xh * jnp.mean(dxh * xh, axis=1, keepdims=True))
        if has_res:
            dx = dx + dres_ref[...]
        dx_ref[...] = dx.astype(dx_ref.dtype)

        @pl.when(pl.program_id(0) == 0)
        def _():
            dg_ref[...] = jnp.zeros_like(dg_ref)

        dg_ref[...] += jnp.sum(dyv * xh, axis=0, keepdims=True)

    row_spec = pl.BlockSpec((tm, width), lambda i: (i, 0))
    vec_spec = pl.BlockSpec((1, width), lambda i: (0, 0))
    in_specs = [pl.BlockSpec((tm, width), lambda i: (i, colblk)), vec_spec, row_spec]
    args = [xarr, g.reshape(1, width), dy]
    if has_res:
        in_specs.append(row_spec)
        args.append(dres)
    return pl.pallas_call(
        body, name=name, grid=(rows // tm,), in_specs=in_specs, out_specs=[row_spec, vec_spec],
        out_shape=[jax.ShapeDtypeStruct((rows, width), out_dtype), jax.ShapeDtypeStruct((1, width), F32)],
        compiler_params=_cparams(),
    )(*args)


def _loss_head(x, g, tgt):
    rows, width = x.shape
    tm = min(TM, rows)

    def body(x_ref, g_ref, t_ref, dx_ref, dg_ref, loss_ref):
        xv = x_ref[...]
        gv = g_ref[...]
        r = lax.rsqrt(jnp.mean(xv * xv, axis=1, keepdims=True) + EPS)
        xh = xv * r
        err = xh * gv - t_ref[...]
        part = 0.5 * jnp.sum(jnp.mean(err * err, axis=1, keepdims=True), axis=0, keepdims=True)
        dyv = err * (1.0 / width)
        dxh = dyv * gv
        dx_ref[...] = r * (dxh - xh * jnp.mean(dxh * xh, axis=1, keepdims=True))

        @pl.when(pl.program_id(0) == 0)
        def _():
            dg_ref[...] = jnp.zeros_like(dg_ref)
            loss_ref[...] = jnp.zeros_like(loss_ref)

        dg_ref[...] += jnp.sum(dyv * xh, axis=0, keepdims=True)
        loss_ref[...] += jnp.broadcast_to(part, loss_ref.shape)

    row_spec = pl.BlockSpec((tm, width), lambda i: (i, 0))
    vec_spec = pl.BlockSpec((1, width), lambda i: (0, 0))
    return pl.pallas_call(
        body, name="loss_head", grid=(rows // tm,), in_specs=[row_spec, vec_spec, row_spec],
        out_specs=[row_spec, vec_spec, pl.BlockSpec((1, SLOT), lambda i: (0, 0))],
        out_shape=[jax.ShapeDtypeStruct((rows, width), F32), jax.ShapeDtypeStruct((1, width), F32),
                   jax.ShapeDtypeStruct((1, SLOT), F32)],
        compiler_params=_cparams(),
    )(x, g.reshape(1, width), tgt)


def _lane_consts():
    half = 16
    inv = ROPE_THETA ** (-(jnp.arange(half, dtype=F32) * 2.0) / 32)
    lane = jnp.arange(SLOT)
    first = (lane >= 64) & (lane < 80)
    second = (lane >= 80) & (lane < 96)
    inv_lane = jnp.where(first | second, inv[(lane - 64) % half], 0.0)
    rows = [inv_lane, (lane < 64).astype(F32), first.astype(F32), second.astype(F32)]
    rows += [jnp.zeros((SLOT,), F32)] * 4
    return jnp.stack(rows).astype(F32)


def _rope_tables(pos_col, consts):
    rows = pos_col.shape[0]
    tm = min(TM, rows)

    def body(p_ref, k_ref, c_ref, s1_ref, s2_ref):
        ang = p_ref[...] * k_ref[0:1, :]
        cos, sin = jnp.cos(ang), jnp.sin(ang)
        first, second = k_ref[2:3, :], k_ref[3:4, :]
        c_ref[...] = k_ref[1:2, :] + (first + second) * cos
        s1_ref[...] = -first * sin
        s2_ref[...] = second * sin

    spec = pl.BlockSpec((tm, SLOT), lambda i: (i, 0))
    shp = jax.ShapeDtypeStruct((rows, SLOT), F32)
    return pl.pallas_call(
        body, name="rope_tables", grid=(rows // tm,),
        in_specs=[pl.BlockSpec((tm, 1), lambda i: (i, 0)), pl.BlockSpec((8, SLOT), lambda i: (0, 0))],
        out_specs=[spec, spec, spec], out_shape=[shp, shp, shp], compiler_params=_cparams(),
    )(pos_col, consts)


def _rot(xv, c, s1, s2):
    return xv * c + pltpu.roll(xv, SLOT - 16, 1) * s1 + pltpu.roll(xv, 16, 1) * s2


def _rot_t(dy, c, s1, s2):
    return dy * c + pltpu.roll(dy * s1, 16, 1) + pltpu.roll(dy * s2, SLOT - 16, 1)


def _mla_rope_fwd(qraw, kvraw, proj, tabs):
    rows = qraw.shape[0]
    tm = min(256, rows)
    hw = MLA_HEADS * SLOT

    def body(q_ref, kv_ref, kr_ref, c_ref, s1_ref, s2_ref, qo, ko, vo):
        c, s1, s2 = c_ref[...], s1_ref[...], s2_ref[...]
        kr = _rot(kr_ref[...], c, s1, s2)
        for h in range(MLA_HEADS):
            sl = slice(h * SLOT, (h + 1) * SLOT)
            qo[:, sl] = _rot(q_ref[:, sl], c, s1, s2).astype(BF16)
            ko[:, sl] = (kv_ref[:, sl] + kr).astype(BF16)
            vo[:, sl] = kv_ref[:, hw + h * SLOT:hw + (h + 1) * SLOT].astype(BF16)

    tab = pl.BlockSpec((tm, SLOT), lambda i: (i, 0))
    wide = pl.BlockSpec((tm, hw), lambda i: (i, 0))
    shp = jax.ShapeDtypeStruct((rows, hw), BF16)
    return pl.pallas_call(
        body, name="mla_rope_fwd", grid=(rows // tm,),
        in_specs=[wide, pl.BlockSpec((tm, 2 * hw), lambda i: (i, 0)), pl.BlockSpec((tm, SLOT), lambda i: (i, 3)),
                  tab, tab, tab],
        out_specs=[wide, wide, wide], out_shape=[shp, shp, shp], compiler_params=_cparams(),
    )(qraw, kvraw, proj, *tabs)


def _mla_rope_bwd(dq, dk, dv, tabs, consts):
    rows = dq.shape[0]
    tm = min(256, rows)
    hw = MLA_HEADS * SLOT

    def body(dq_ref, dk_ref, dv_ref, c_ref, s1_ref, s2_ref, k_ref, dqo, dkvo, dkro):
        c, s1, s2 = c_ref[...], s1_ref[...], s2_ref[...]
        ksum = jnp.zeros((tm, SLOT), F32)
        for h in range(MLA_HEADS):
            sl = slice(h * SLOT, (h + 1) * SLOT)
            dqo[:, sl] = _rot_t(dq_ref[:, sl], c, s1, s2).astype(BF16)
            dkh = dk_ref[:, sl]
            ksum = ksum + dkh
            dkvo[:, sl] = dkh.astype(BF16)
            dkvo[:, hw + h * SLOT:hw + (h + 1) * SLOT] = dv_ref[:, sl].astype(BF16)
        dkro[...] = _rot_t(ksum, c, s1, s2) * (k_ref[2:3, :] + k_ref[3:4, :])

    tab = pl.BlockSpec((tm, SLOT), lambda i: (i, 0))
    wide = pl.BlockSpec((tm, hw), lambda i: (i, 0))
    return pl.pallas_call(
        body, name="mla_rope_bwd", grid=(rows // tm,),
        in_specs=[wide, wide, wide, tab, tab, tab, pl.BlockSpec((8, SLOT), lambda i: (0, 0))],
        out_specs=[wide, pl.BlockSpec((tm, 2 * hw), lambda i: (i, 0)), tab],
        out_shape=[jax.ShapeDtypeStruct((rows, hw), BF16), jax.ShapeDtypeStruct((rows, 2 * hw), BF16),
                   jax.ShapeDtypeStruct((rows, SLOT), F32)],
        compiler_params=_cparams(),
    )(dq, dk, dv, *tabs, consts)


def _nt(a, b):
    return lax.dot_general(a, b, _DIMS["nt"], preferred_element_type=F32)


def _tn(a, b):
    return lax.dot_general(a, b, _DIMS["tn"], preferred_element_type=F32)


def _nn(a, b):
    return lax.dot_general(a, b, _DIMS["nn"], preferred_element_type=F32)


def _causal(t):
    return lax.broadcasted_iota(jnp.int32, (t, t), 1) <= lax.broadcasted_iota(jnp.int32, (t, t), 0)


def _mla_attn_fwd(q, k, v):
    rows = q.shape[0]
    t = min(TQ_MLA, rows)
    nt = rows // t
    scale = MLA_QK ** -0.5
    wide = MLA_PACK * SLOT

    def body(q_ref, k_ref, v_ref, o_ref, lse_ref, m_sc, l_sc, acc_sc):
        i, j = pl.program_id(1), pl.program_id(2)

        @pl.when(j == 0)
        def _():
            m_sc[...] = jnp.full_like(m_sc, NEG)
            l_sc[...] = jnp.zeros_like(l_sc)
            acc_sc[...] = jnp.zeros_like(acc_sc)

        def step(diagonal):
            for hh in range(MLA_PACK):
                sl = slice(hh * SLOT, (hh + 1) * SLOT)
                s = _nt(q_ref[:, sl], k_ref[:, sl]) * scale
                if diagonal:
                    s = jnp.where(_causal(t), s, NEG)
                m_prev = m_sc[hh]
                m_new = jnp.maximum(m_prev, jnp.max(s, axis=1, keepdims=True))
                p = jnp.exp(s - m_new)
                alpha = jnp.exp(m_prev - m_new)
                l_new = alpha * l_sc[hh] + jnp.sum(p, axis=1, keepdims=True)
                acc = alpha * acc_sc[:, sl] + _nn(p.astype(BF16), v_ref[:, sl])
                if diagonal:
                    o_ref[:, sl] = (acc / l_new).astype(o_ref.dtype)
                    lse_ref[:, sl] = jnp.broadcast_to(m_new + jnp.log(l_new), (t, SLOT))
                else:
                    m_sc[hh] = m_new
                    l_sc[hh] = l_new
                    acc_sc[:, sl] = acc

        @pl.when(j < i)
        def _():
            step(False)

        @pl.when(j == i)
        def _():
            step(True)

    q_spec = pl.BlockSpec((t, wide), lambda h, i, j: (i, h))
    kv_spec = pl.BlockSpec((t, wide), lambda h, i, j: (jnp.minimum(j, i), h))
    return pl.pallas_call(
        body, name="mla_attn_fwd", grid=(MLA_HEADS // MLA_PACK, nt, nt),
        in_specs=[q_spec, kv_spec, kv_spec], out_specs=[q_spec, q_spec],
        out_shape=[jax.ShapeDtypeStruct(q.shape, BF16), jax.ShapeDtypeStruct(q.shape, F32)],
        scratch_shapes=[pltpu.VMEM((MLA_PACK, t, 1), F32), pltpu.VMEM((MLA_PACK, t, 1), F32),
                        pltpu.VMEM((t, wide), F32)],
        compiler_params=_cparams(),
    )(q, k, v)


def _mla_attn_bwd(q, k, v, o, do, lse, after):
    rows = q.shape[0]
    t = min(TQ_MLA, rows)
    nt = rows // t
    scale = MLA_QK ** -0.5
    wide = MLA_PACK * SLOT

    def body(q_ref, k_ref, v_ref, o_ref, do_ref, lse_ref, after_ref, dq_ref, dk_ref, dv_ref, dk_sc, dv_sc):
        j, i = pl.program_id(1), pl.program_id(2)

        @pl.when((j == 0) & (i == 0))
        def _():
            dq_ref[...] = jnp.zeros_like(dq_ref)

        @pl.when(i == 0)
        def _():
            dk_sc[...] = jnp.zeros_like(dk_sc)
            dv_sc[...] = jnp.zeros_like(dv_sc)

        def step(diagonal):
            r0 = pl.multiple_of(i * t, t)
            for hh in range(MLA_PACK):
                sl = slice(hh * SLOT, (hh + 1) * SLOT)
                qv, kv, dov = q_ref[:, sl], k_ref[:, sl], do_ref[:, sl]
                s = _nt(qv, kv) * scale
                if diagonal:
                    s = jnp.where(_causal(t), s, NEG)
                p = jnp.exp(s - lse_ref[:, hh * SLOT:hh * SLOT + 1])
                delta = jnp.sum(dov.astype(F32) * o_ref[:, sl].astype(F32), axis=1, keepdims=True)
                dp = _nt(dov, v_ref[:, sl])
                ds = (p * (dp - delta) * scale).astype(BF16)
                dv_sc[:, sl] += _tn(p.astype(BF16), dov)
                dk_sc[:, sl] += _tn(ds, qv)
                dq_ref[pl.ds(r0, t), sl] += _nn(ds, kv)

        @pl.when(i > j)
        def _():
            step(False)

        @pl.when(i == j)
        def _():
            step(True)

        @pl.when(i == nt - 1)
        def _():
            dk_ref[...] = dk_sc[...]
            dv_ref[...] = dv_sc[...]

    q_spec = pl.BlockSpec((t, wide), lambda h, j, i: (jnp.maximum(i, j), h))
    kv_spec = pl.BlockSpec((t, wide), lambda h, j, i: (j, h))
    head_spec = pl.BlockSpec((rows, wide), lambda h, j, i: (0, h))
    shp = jax.ShapeDtypeStruct(q.shape, F32)
    return pl.pallas_call(
        body, name="mla_attn_bwd", grid=(MLA_HEADS // MLA_PACK, nt, nt),
        in_specs=[q_spec, kv_spec, kv_spec, q_spec, q_spec, q_spec, pl.BlockSpec(memory_space=pl.ANY)],
        out_specs=[head_spec, kv_spec, kv_spec], out_shape=[shp, shp, shp],
        scratch_shapes=[pltpu.VMEM((t, wide), F32), pltpu.VMEM((t, wide), F32)],
        compiler_params=_cparams(),
    )(q, k, v, o, do, lse, after)


def _swa_specs(t):
    def prev(i):
        return jnp.maximum(i - 1, 0)
    q3 = pl.BlockSpec((t, SWA_GROUP * SLOT), lambda h, i: (i, h))
    kp = pl.BlockSpec((t, SLOT), lambda h, i: (prev(i), SWA_HEADS + h))
    kc = pl.BlockSpec((t, SLOT), lambda h, i: (i, SWA_HEADS + h))
    vp = pl.BlockSpec((t, SLOT), lambda h, i: (prev(i), SWA_HEADS + SWA_KV_HEADS + h))
    vc = pl.BlockSpec((t, SLOT), lambda h, i: (i, SWA_HEADS + SWA_KV_HEADS + h))
    pcol = pl.BlockSpec((t, 1), lambda h, i: (i, 0))
    prow_p = pl.BlockSpec((1, t), lambda h, i: (0, prev(i)))
    prow_c = pl.BlockSpec((1, t), lambda h, i: (0, i))
    return [q3, kp, kc, vp, vc, pcol, prow_p, prow_c]


def _stack(ref):
    return jnp.concatenate([ref[:, g * SLOT:(g + 1) * SLOT] for g in range(SWA_GROUP)], axis=0)


def _swa_logits(q3, kp, kc, pq, pkp, pkc, slope_ref, kvh, i, t):
    r = lax.broadcasted_iota(jnp.int32, (t, t), 0)
    c = lax.broadcasted_iota(jnp.int32, (t, t), 1)
    ok_c = c <= r
    ok_p = (c - r) > jnp.where(i > 0, 0, t)
    dist_p, dist_c = pq - pkp, pq - pkc
    s_p3 = _nt(q3, kp) * (HEAD_DIM ** -0.5)
    s_c3 = _nt(q3, kc) * (HEAD_DIM ** -0.5)
    out = []
    for g in range(SWA_GROUP):
        slope = slope_ref[kvh * SWA_GROUP + g]
        rows = slice(g * t, (g + 1) * t)
        out.append((jnp.where(ok_p, s_p3[rows] - slope * dist_p, NEG),
                    jnp.where(ok_c, s_c3[rows] - slope * dist_c, NEG)))
    return out


def _swa_attn_fwd(proj, pos_col, pos_row, slopes, sinks):
    rows = proj.shape[0]
    t = WINDOW
    hw = SWA_HEADS * SLOT

    def body(slope_ref, sink_ref, q_ref, kp_ref, kc_ref, vp_ref, vc_ref, pq_ref, pkp_ref, pkc_ref, o_ref, lse_ref):
        kvh, i = pl.program_id(0), pl.program_id(1)
        logits = _swa_logits(_stack(q_ref), kp_ref[...], kc_ref[...], pq_ref[...], pkp_ref[...], pkc_ref[...],
                             slope_ref, kvh, i, t)
        e_p, e_c, norm = [], [], []
        for g, (s_p, s_c) in enumerate(logits):
            sink = sink_ref[kvh * SWA_GROUP + g]
            m = jnp.maximum(jnp.maximum(jnp.max(s_p, axis=1, keepdims=True), jnp.max(s_c, axis=1, keepdims=True)),
                            sink)
            ep, ec = jnp.exp(s_p - m), jnp.exp(s_c - m)
            l = jnp.sum(ep, axis=1, keepdims=True) + jnp.sum(ec, axis=1, keepdims=True) + jnp.exp(sink - m)
            e_p.append(ep.astype(BF16))
            e_c.append(ec.astype(BF16))
            norm.append(l)
            lse_ref[:, g * SLOT:(g + 1) * SLOT] = jnp.broadcast_to(m + jnp.log(l), (t, SLOT))
        acc = _nn(jnp.concatenate(e_p, axis=0), vp_ref[...]) + _nn(jnp.concatenate(e_c, axis=0), vc_ref[...])
        for g in range(SWA_GROUP):
            o_ref[:, g * SLOT:(g + 1) * SLOT] = (acc[g * t:(g + 1) * t] / norm[g]).astype(o_ref.dtype)

    smem = pl.BlockSpec(memory_space=pltpu.SMEM)
    out_spec = pl.BlockSpec((t, SWA_GROUP * SLOT), lambda h, i: (i, h))
    return pl.pallas_call(
        body, name="swa_attn_fwd", grid=(SWA_KV_HEADS, rows // t),
        in_specs=[smem, smem] + _swa_specs(t), out_specs=[out_spec, out_spec],
        out_shape=[jax.ShapeDtypeStruct((rows, hw), BF16), jax.ShapeDtypeStruct((rows, hw), F32)],
        compiler_params=_cparams(),
    )(slopes, sinks, proj, proj, proj, proj, proj, pos_col, pos_row, pos_row)


def _swa_attn_bwd(proj, o, do, lse, pos_col, pos_row, slopes, sinks, after):
    rows = proj.shape[0]
    t = WINDOW
    hw = SWA_HEADS * SLOT
    scale = HEAD_DIM ** -0.5

    def body(slope_ref, sink_ref, q_ref, kp_ref, kc_ref, vp_ref, vc_ref, pq_ref, pkp_ref, pkc_ref,
             o_ref, do_ref, lse_ref, after_ref, dq_ref, dk_ref, dv_ref, dsink_ref):
        kvh, i = pl.program_id(0), pl.program_id(1)

        @pl.when(i == 0)
        def _():
            dk_ref[...] = jnp.zeros_like(dk_ref)
            dv_ref[...] = jnp.zeros_like(dv_ref)
            dsink_ref[...] = jnp.zeros_like(dsink_ref)

        q3, do3 = _stack(q_ref), _stack(do_ref)
        logits = _swa_logits(q3, kp_ref[...], kc_ref[...], pq_ref[...], pkp_ref[...], pkc_ref[...],
                             slope_ref, kvh, i, t)
        dp_p3, dp_c3 = _nt(do3, vp_ref[...]), _nt(do3, vc_ref[...])
        p_p, p_c, ds_p, ds_c = [], [], [], []
        for g, (s_p, s_c) in enumerate(logits):
            sl = slice(g * SLOT, (g + 1) * SLOT)
            rws = slice(g * t, (g + 1) * t)
            lse_g = lse_ref[:, g * SLOT:g * SLOT + 1]
            pp, pc = jnp.exp(s_p - lse_g), jnp.exp(s_c - lse_g)
            delta = jnp.sum(do_ref[:, sl].astype(F32) * o_ref[:, sl].astype(F32), axis=1, keepdims=True)
            p_p.append(pp.astype(BF16))
            p_c.append(pc.astype(BF16))
            ds_p.append((pp * (dp_p3[rws] - delta)).astype(BF16))
            ds_c.append((pc * (dp_c3[rws] - delta)).astype(BF16))
            sink = sink_ref[kvh * SWA_GROUP + g]
            dsink = -jnp.sum(jnp.exp(sink - lse_g) * delta, axis=0, keepdims=True)
            dsink_ref[g * 8:(g + 1) * 8, :] += jnp.broadcast_to(dsink, (8, SLOT))
        p_p3, p_c3 = jnp.concatenate(p_p, axis=0), jnp.concatenate(p_c, axis=0)
        ds_p3, ds_c3 = jnp.concatenate(ds_p, axis=0), jnp.concatenate(ds_c, axis=0)
        dq3 = (_nn(ds_p3, kp_ref[...]) + _nn(ds_c3, kc_ref[...])) * scale
        for g in range(SWA_GROUP):
            dq_ref[:, g * SLOT:(g + 1) * SLOT] = dq3[g * t:(g + 1) * t]
        r_c = pl.multiple_of(i * t, t)
        dk_ref[pl.ds(r_c, t), :] += _tn(ds_c3, q3) * scale
        dv_ref[pl.ds(r_c, t), :] += _tn(p_c3, do3)

        @pl.when(i > 0)
        def _():
            r_p = pl.multiple_of((i - 1) * t, t)
            dk_ref[pl.ds(r_p, t), :] += _tn(ds_p3, q3) * scale
            dv_ref[pl.ds(r_p, t), :] += _tn(p_p3, do3)

    smem = pl.BlockSpec(memory_space=pltpu.SMEM)
    qlike = pl.BlockSpec((t, SWA_GROUP * SLOT), lambda h, i: (i, h))
    kv_out = pl.BlockSpec((rows, SLOT), lambda h, i: (0, h))
    return pl.pallas_call(
        body, name="swa_attn_bwd", grid=(SWA_KV_HEADS, rows // t),
        in_specs=[smem, smem] + _swa_specs(t) + [qlike, qlike, qlike, pl.BlockSpec(memory_space=pl.ANY)],
        out_specs=[qlike, kv_out, kv_out, pl.BlockSpec((SWA_GROUP * 8, SLOT), lambda h, i: (h, 0))],
        out_shape=[jax.ShapeDtypeStruct((rows, hw), F32), jax.ShapeDtypeStruct((rows, SWA_KV_HEADS * SLOT), F32),
                   jax.ShapeDtypeStruct((rows, SWA_KV_HEADS * SLOT), F32),
                   jax.ShapeDtypeStruct((SWA_HEADS * 8, SLOT), F32)],
        compiler_params=_cparams(),
    )(slopes, sinks, proj, proj, proj, proj, proj, pos_col, pos_row, pos_row, o, do, lse, after)


def _cross_attn_fwd(proj, qoff, kvmem):
    rows = proj.shape[0]
    t = min(TQ_CROSS, rows)

    def body(q_ref, k_ref, v_ref, o_ref):
        s = _nt(q_ref[...].astype(BF16), k_ref[...]) * (HEAD_DIM ** -0.5)
        e = jnp.exp(s - jnp.max(s, axis=1, keepdims=True))
        p = e / jnp.sum(e, axis=1, keepdims=True)
        o_ref[...] = _nn(p.astype(BF16), v_ref[...]).astype(o_ref.dtype)

    return pl.pallas_call(
        body, name="cross_attn_fwd", grid=(rows // t, MEM_HEADS),
        in_specs=[pl.BlockSpec((t, SLOT), lambda i, h: (i, qoff + h)),
                  pl.BlockSpec((N_MEM, SLOT), lambda i, h: (0, h)),
                  pl.BlockSpec((N_MEM, SLOT), lambda i, h: (0, MEM_HEADS + h))],
        out_specs=pl.BlockSpec((t, SLOT), lambda i, h: (i, h)),
        out_shape=jax.ShapeDtypeStruct((rows, MEM_HEADS * SLOT), BF16), compiler_params=_cparams(),
    )(proj, kvmem, kvmem)


def _cross_attn_bwd(proj, qoff, kvmem, do):
    rows = proj.shape[0]
    t = min(TQ_CROSS, rows)
    scale = HEAD_DIM ** -0.5

    def body(q_ref, k_ref, v_ref, do_ref, dq_ref, dk_ref, dv_ref):
        @pl.when(pl.program_id(1) == 0)
        def _():
            dk_ref[...] = jnp.zeros_like(dk_ref)
            dv_ref[...] = jnp.zeros_like(dv_ref)

        qv, kv, dov = q_ref[...].astype(BF16), k_ref[...], do_ref[...]
        s = _nt(qv, kv) * scale
        e = jnp.exp(s - jnp.max(s, axis=1, keepdims=True))
        p = e / jnp.sum(e, axis=1, keepdims=True)
        dp = _nt(dov, v_ref[...])
        ds = (p * (dp - jnp.sum(p * dp, axis=1, keepdims=True))).astype(BF16)
        dq_ref[...] = _nn(ds, kv) * scale
        dk_ref[...] += _tn(ds, qv) * scale
        dv_ref[...] += _tn(p.astype(BF16), dov)

    mem_out = pl.BlockSpec((N_MEM, SLOT), lambda h, i: (0, h))
    return pl.pallas_call(
        body, name="cross_attn_bwd", grid=(MEM_HEADS, rows // t),
        in_specs=[pl.BlockSpec((t, SLOT), lambda h, i: (i, qoff + h)),
                  pl.BlockSpec((N_MEM, SLOT), lambda h, i: (0, h)),
                  pl.BlockSpec((N_MEM, SLOT), lambda h, i: (0, MEM_HEADS + h)),
                  pl.BlockSpec((t, SLOT), lambda h, i: (i, h))],
        out_specs=[pl.BlockSpec((t, SLOT), lambda h, i: (i, h)), mem_out, mem_out],
        out_shape=[jax.ShapeDtypeStruct((rows, MEM_HEADS * SLOT), F32),
                   jax.ShapeDtypeStruct((N_MEM, MEM_HEADS * SLOT), F32),
                   jax.ShapeDtypeStruct((N_MEM, MEM_HEADS * SLOT), F32)],
        compiler_params=_cparams(),
    )(proj, kvmem, kvmem, do)


def _place():
    return lax.axis_index("x"), lax.axis_index("y"), lax.axis_index("c")


def _flip(v, bit):
    return 1 - v if bit else v


def _all_gather(blocks, name):
    nb = len(blocks)

    def body(*refs):
        x_refs, out_refs = refs[:nb], refs[nb:2 * nb]
        send_sems, recv_sems, local_sems = refs[2 * nb:]
        x, y, c = _place()
        me, sibling = (x, y, c), (x, y, 1 - c)
        chips = [(1 - x, y), (x, 1 - y), (1 - x, 1 - y)]

        def copy(b, k, blk, to, from_input=False):
            slot = out_refs[b].at[4 * blk[0] + 2 * blk[1] + blk[2]]
            return pltpu.make_async_remote_copy(
                src_ref=x_refs[b] if from_input else slot, dst_ref=slot,
                send_sem=send_sems.at[b, k], recv_sem=recv_sems.at[b, k],
                device_id=to, device_id_type=pl.DeviceIdType.MESH)

        mine = [pltpu.make_async_copy(x_refs[b], out_refs[b].at[4 * x + 2 * y + c], local_sems.at[b])
                for b in range(nb)]
        for cp in mine:
            cp.start()
        first = []
        for b in range(nb):
            first.append(copy(b, 0, me, sibling, from_input=True))
            first += [copy(b, 1 + n, me, (*chip, c), from_input=True) for n, chip in enumerate(chips)]
        for cp in first:
            cp.start()
        passed = []
        for n, chip in enumerate(chips):
            for b in range(nb):
                copy(b, 1 + n, (*chip, c), me).wait_recv()
                passed.append(copy(b, 4 + n, (*chip, c), sibling))
                passed[-1].start()
        for b in range(nb):
            copy(b, 0, sibling, me).wait_recv()
            for n, chip in enumerate(chips):
                copy(b, 4 + n, (*chip, 1 - c), me).wait_recv()
        for cp in first + passed:
            cp.wait_send()
        for cp in mine:
            cp.wait()

    any_spec = pl.BlockSpec(memory_space=pl.ANY)
    return pl.pallas_call(
        body, name=name, in_specs=[any_spec] * nb, out_specs=[any_spec] * nb,
        out_shape=[jax.ShapeDtypeStruct((N_DEV,) + blk.shape, blk.dtype) for blk in blocks],
        scratch_shapes=[pltpu.SemaphoreType.DMA((nb, 7)), pltpu.SemaphoreType.DMA((nb, 7)),
                        pltpu.SemaphoreType.DMA((nb,))],
    )(*blocks)


def _peers(x, y, c):
    out = []
    for n in range(1, N_DEV):
        peer = (_flip(x, n & 4), _flip(y, n & 2), _flip(c, n & 1))
        out.append((n - 1, peer, 4 * peer[0] + 2 * peer[1] + peer[2]))
    return out


_HBM = pl.BlockSpec(memory_space=pltpu.HBM)
_SEM = pl.BlockSpec(memory_space=pltpu.SEMAPHORE)


def _exchange_start(srcs, scatter, name, after=None):
    ns = len(srcs)
    lands = [lax.empty(s.shape if scatter else (N_DEV,) + s.shape, s.dtype) for s in srcs]

    def body(*refs):
        src_refs, land_refs = refs[:ns], refs[ns:2 * ns]
        pos = 2 * ns + (1 if after is not None else 0)
        send_sems, recv_sems, token = refs[pos], refs[pos + 1], refs[-1]
        x, y, c = _place()
        my_idx = 4 * x + 2 * y + c
        for col, peer, peer_idx in _peers(x, y, c):
            for b in range(ns):
                pltpu.make_async_remote_copy(
                    src_ref=src_refs[b].at[peer_idx] if scatter else src_refs[b], dst_ref=land_refs[b].at[my_idx],
                    send_sem=send_sems.at[b * (N_DEV - 1) + col], recv_sem=recv_sems.at[b * (N_DEV - 1) + col],
                    device_id=peer, device_id_type=pl.DeviceIdType.MESH).start()
        token[...] = jnp.zeros_like(token)

    args = [pltpu.with_memory_space_constraint(a, pltpu.HBM) for a in list(srcs) + lands]
    in_specs = [_HBM] * (2 * ns)
    if after is not None:
        args.append(after)
        in_specs.append(pl.BlockSpec(memory_space=pl.ANY))
    out = pl.pallas_call(
        body, name=name, in_specs=in_specs,
        out_specs=[_SEM, _SEM] + [_HBM] * (2 * ns) + [pl.BlockSpec(memory_space=pltpu.VMEM)],
        out_shape=[pltpu.SemaphoreType.DMA((ns * (N_DEV - 1),)), pltpu.SemaphoreType.DMA((ns * (N_DEV - 1),))]
        + [pltpu.HBM(a.shape, a.dtype) for a in list(srcs) + lands] + [jax.ShapeDtypeStruct((8, SLOT), F32)],
        input_output_aliases={k: 2 + k for k in range(2 * ns)},
        compiler_params=pltpu.CompilerParams(has_side_effects=pltpu.SideEffectType.DATAFLOW_SIDE_EFFECTING),
    )(*args)
    return (out[0], out[1], out[2:2 + ns], out[2 + ns:2 + 2 * ns], scatter), out[-1]


def _exchange_wait(handle, after, name):
    send_sems, recv_sems, srcs, lands, scatter = handle
    ns = len(srcs)

    def body(*refs):
        src_refs, land_refs = refs[:ns], refs[ns:2 * ns]
        send_ref, recv_ref = refs[2 * ns], refs[2 * ns + 1]
        x, y, c = _place()
        for col, peer, peer_idx in _peers(x, y, c):
            for b in range(ns):
                copy = pltpu.make_async_remote_copy(
                    src_ref=src_refs[b].at[peer_idx] if scatter else src_refs[b], dst_ref=land_refs[b].at[peer_idx],
                    send_sem=send_ref.at[b * (N_DEV - 1) + col], recv_sem=recv_ref.at[b * (N_DEV - 1) + col],
                    device_id=peer, device_id_type=pl.DeviceIdType.MESH)
                copy.wait_send()
                copy.wait_recv()

    out = pl.pallas_call(
        body, name=name, in_specs=[_HBM] * (2 * ns) + [_SEM, _SEM, pl.BlockSpec(memory_space=pl.ANY)],
        out_specs=[_HBM] * (2 * ns),
        out_shape=[pltpu.HBM(a.shape, a.dtype) for a in list(srcs) + list(lands)],
        input_output_aliases={k: k for k in range(2 * ns)},
        compiler_params=pltpu.CompilerParams(has_side_effects=pltpu.SideEffectType.DATAFLOW_SIDE_EFFECTING),
    )(*srcs, *lands, send_sems, recv_sems, after)
    my_idx = 4 * lax.axis_index("x") + 2 * lax.axis_index("y") + lax.axis_index("c")
    landed = []
    for src, land in zip(out[:ns], out[ns:]):
        own = lax.dynamic_index_in_dim(src, my_idx, 0, keepdims=True) if scatter else src[None]
        landed.append(lax.dynamic_update_index_in_dim(land, own, my_idx, 0))
    return landed


def _adamw(parts, w, m, v, name):
    lyr, rows, cols = w.shape
    tr = ADAM_ROWS if cols > 512 else 2 * ADAM_ROWS
    while rows % tr:
        tr //= 2
    tr = min(tr, rows)

    def body(p_ref, w_ref, m_ref, v_ref, g_out, d_out, m_out, v_out):
        g = p_ref[0].astype(F32)
        for s in range(1, N_DEV):
            g = g + p_ref[s].astype(F32)
        m2 = ADAM_B1 * m_ref[...] + (1.0 - ADAM_B1) * g
        v2 = ADAM_B2 * v_ref[...] + (1.0 - ADAM_B2) * (g * g)
        m_hat = m2 / (1.0 - ADAM_B1 ** ADAM_STEP)
        v_hat = v2 / (1.0 - ADAM_B2 ** ADAM_STEP)
        g_out[...] = g
        d_out[...] = -ADAM_LR * (m_hat / (jnp.sqrt(v_hat) + ADAM_EPS) + ADAM_WD * w_ref[...])
        m_out[...] = m2
        v_out[...] = v2

    spec = pl.BlockSpec((None, tr, cols), lambda l, i: (l, i, 0))
    shp = jax.ShapeDtypeStruct((lyr, rows, cols), F32)
    return pl.pallas_call(
        body, name=name, grid=(lyr, rows // tr),
        in_specs=[pl.BlockSpec((None, N_DEV, tr, cols), lambda l, i: (l, 0, i, 0)), spec, spec, spec],
        out_specs=[spec] * 4, out_shape=[shp] * 4, compiler_params=_cparams(),
    )(parts, w, m, v)


def _pack(arrays, lanes, row_mult, dtype):
    flat = jnp.concatenate([a.reshape(-1).astype(dtype) for a in arrays])
    unit = lanes * row_mult
    total = -(-flat.shape[0] // unit) * unit
    return jnp.pad(flat, (0, total - flat.shape[0])).reshape(total // lanes, lanes)


def _unpack(packed, shapes):
    flat = packed.reshape(-1)
    out, off = [], 0
    for shp in shapes:
        n = 1
        for d in shp:
            n *= d
        out.append(flat[off:off + n].reshape(shp))
        off += n
    return out


def _pad_slots(w, axis):
    axis = axis % w.ndim
    n = w.shape[axis] // HEAD_DIM
    shp = w.shape[:axis] + (n, HEAD_DIM) + w.shape[axis + 1:]
    pad = [(0, 0)] * (w.ndim + 1)
    pad[axis + 1] = (0, SLOT - HEAD_DIM)
    return jnp.pad(w.reshape(shp), pad).reshape(w.shape[:axis] + (n * SLOT,) + w.shape[axis + 1:])


def _unpad_slots(w, axis, keep=HEAD_DIM):
    axis = axis % w.ndim
    n = w.shape[axis] // SLOT
    shp = w.shape[:axis] + (n, SLOT) + w.shape[axis + 1:]
    idx = [slice(None)] * (w.ndim + 1)
    idx[axis + 1] = slice(0, keep)
    return w.reshape(shp)[tuple(idx)].reshape(w.shape[:axis] + (n * keep,) + w.shape[axis + 1:])


def _mla_in_pad(w):
    z = functools.partial(jnp.zeros, dtype=w.dtype)
    rows = w.shape[0]
    return jnp.concatenate([w[:, :384], z((rows, 64)), w[:, 640:672], z((rows, 32)), w[:, 384:640],
                            _pad_slots(w[:, 672:], 1)], axis=1)


def _mla_in_unpad(d):
    return jnp.concatenate([d[:, :384], d[:, 512:768], d[:, 448:480], _unpad_slots(d[:, 768:], 1)], axis=1)


def _mla_uq_pad(w):
    return jnp.pad(w.reshape(w.shape[0], MLA_HEADS, MLA_QK), ((0, 0), (0, 0), (0, SLOT - MLA_QK))).reshape(
        w.shape[0], MLA_HEADS * SLOT)


def _mla_ukv_pad(w):
    w3 = w.reshape(w.shape[0], MLA_HEADS, 2 * HEAD_DIM)
    pad = ((0, 0), (0, 0), (0, SLOT - HEAD_DIM))
    k = jnp.pad(w3[:, :, :HEAD_DIM], pad).reshape(w.shape[0], -1)
    v = jnp.pad(w3[:, :, HEAD_DIM:], pad).reshape(w.shape[0], -1)
    return jnp.concatenate([k, v], axis=1)


def _mla_ukv_unpad(d):
    hw = MLA_HEADS * SLOT
    k = d[:, :hw].reshape(d.shape[0], MLA_HEADS, SLOT)[:, :, :HEAD_DIM]
    v = d[:, hw:].reshape(d.shape[0], MLA_HEADS, SLOT)[:, :, :HEAD_DIM]
    return jnp.concatenate([k, v], axis=2).reshape(d.shape[0], MLA_HEADS * 2 * HEAD_DIM)


def _join(gathered, axis):
    nd, a, b = gathered.shape
    if axis == 1:
        return gathered.reshape(nd * a, b)
    return gathered.transpose(1, 0, 2).reshape(a, nd * b)


def _split(full, axis):
    r, c = full.shape
    if axis == 1:
        return full.reshape(N_DEV, r // N_DEV, c).astype(BF16)
    return full.reshape(r, N_DEV, c // N_DEV).transpose(1, 0, 2).astype(BF16)


def kernel(x, mem, positions, attn_norm_g, mlp_norm_g, mem_norm_g, final_norm_g, mla_w_in, mla_q_norm_g, mla_kv_norm_g, mla_w_uq, mla_w_ukv, swa_w_in, swa_sinks, w_mem_kv, w_o, mlp_w_up, mlp_w_down, loss_target, m_attn_norm_g, m_mlp_norm_g, m_mem_norm_g, m_final_norm_g, m_mla_w_in, m_mla_q_norm_g, m_mla_kv_norm_g, m_mla_w_uq, m_mla_w_ukv, m_swa_w_in, m_swa_sinks, m_w_mem_kv, m_w_o, m_mlp_w_up, m_mlp_w_down, v_attn_norm_g, v_mlp_norm_g, v_mem_norm_g, v_final_norm_g, v_mla_w_in, v_mla_q_norm_g, v_mla_kv_norm_g, v_mla_w_uq, v_mla_w_ukv, v_swa_w_in, v_swa_sinks, v_w_mem_kv, v_w_o, v_mlp_w_up, v_mlp_w_down):
    given = dict(locals())
    seq = x.shape[1]
    x0 = x.reshape(seq, D_MODEL)
    tgt = loss_target.reshape(seq, D_MODEL)
    mem0 = mem.reshape(N_MEM, D_MODEL)
    pos = positions.reshape(seq).astype(F32)
    pos_col, pos_row = pos.reshape(seq, 1), pos.reshape(1, seq)

    def layer_names(i):
        mixer = ("mla_w_in", "mla_w_uq", "mla_w_ukv") if i % 2 == 0 else ("swa_w_in",)
        return [(n, i // 2) for n in mixer] + [(n, i) for n in ("w_mem_kv", "w_o", "mlp_w_up", "mlp_w_down")]

    def local_weights(names):
        return [given[n][l].astype(BF16) for n, l in names]

    first_attn, first_mlp = layer_names(0)[:-2], layer_names(0)[-2:]
    weights = [dict(zip([n for n, _ in first_attn], _all_gather(local_weights(first_attn), "gather_weights_first")))]
    coming_mlp, first_token = _exchange_start(local_weights(first_mlp), False, "gather_weights_start_0",
                                              after=weights[0]["w_o"])

    consts = _lane_consts()
    tabs = _rope_tables(pos_col, consts)
    slopes = 2.0 ** (-8.0 * (jnp.arange(SWA_HEADS, dtype=F32) + 1.0) / SWA_HEADS)

    mem_n = _rmsnorm_fwd(mem0, 0, D_MODEL, mem_norm_g, "rmsnorm_fwd_mem")

    saved = []
    xc = x0
    for i in range(DEPTH):
        j = i // 2
        wts = weights[i]
        s = {"x_in": xc}
        token = None
        if i + 1 < DEPTH:
            coming, token = _exchange_start(local_weights(layer_names(i + 1)), False,
                                            "gather_weights_start_%d" % (i + 1),
                                            after=first_token if i == 0 else wts["w_o"])
        hn = _rmsnorm_fwd(xc, 0, D_MODEL, attn_norm_g[i], "rmsnorm_fwd", after=token)
        if i % 2 == 0:
            w_in = _mla_in_pad(_join(wts["mla_w_in"], 1))
            w_uq = _mla_uq_pad(_join(wts["mla_w_uq"], 2))
            w_kv = _mla_ukv_pad(_join(wts["mla_w_ukv"], 2))
            proj = _mm(hn, w_in, "nn", F32, "mm_mla_in")
            cqn = _rmsnorm_fwd(proj, 0, MLA_Q_RANK, mla_q_norm_g[j], "rmsnorm_fwd_q")
            ckvn = _rmsnorm_fwd(proj, 2, MLA_KV_RANK, mla_kv_norm_g[j], "rmsnorm_fwd_kv")
            qraw = _mm(cqn, w_uq, "nn", F32, "mm_mla_uq")
            kvraw = _mm(ckvn, w_kv, "nn", F32, "mm_mla_ukv")
            q, k, v = _mla_rope_fwd(qraw, kvraw, proj, tabs)
            o, lse = _mla_attn_fwd(q, k, v)
            qoff = MLA_QOFF
            s.update(w_uq=w_uq, w_kv=w_kv, cqn=cqn, ckvn=ckvn, q=q, k=k, v=v)
        else:
            w_in = _pad_slots(_join(wts["swa_w_in"], 2), 1)
            proj = _mm(hn, w_in, "nn", BF16, "mm_swa_in")
            o, lse = _swa_attn_fwd(proj, pos_col, pos_row, slopes, swa_sinks[j])
            qoff = SWA_QOFF
        w_mem = _pad_slots(_join(wts["w_mem_kv"], 1), 1)
        w_out = _pad_slots(_join(wts["w_o"], 1), 0)
        w_o_mix, w_o_cross = w_out[:SWA_HEADS * SLOT], w_out[SWA_HEADS * SLOT:]
        kvmem = _mm(mem_n, w_mem, "nn", BF16, "mm_mem_kv")
        cross = _cross_attn_fwd(proj, qoff, kvmem)
        x1 = _mm(o, w_o_mix, "nn", F32, "mm_o_mix", res=xc)
        x1 = _mm(cross, w_o_cross, "nn", F32, "mm_o_cross", res=x1)
        hn2 = _rmsnorm_fwd(x1, 0, D_MODEL, mlp_norm_g[i], "rmsnorm_fwd")
        if i == 0:
            wts.update(zip([n for n, _ in first_mlp], _exchange_wait(coming_mlp, hn2, "gather_weights_wait_0")))
        act, act2 = _mm(hn2, wts["mlp_w_up"], "nn", BF16, "mm_mlp_up", epi="relu2", b_blk="cols")
        xc = _mm(act2, wts["mlp_w_down"], "nn", F32, "mm_mlp_down", res=x1, b_blk="rows")
        s.update(hn=hn, w_in=w_in, proj=proj, o=o, lse=lse, qoff=qoff, w_mem=w_mem, w_o_mix=w_o_mix,
                 w_o_cross=w_o_cross, kvmem=kvmem, cross=cross, x1=x1, hn2=hn2, act=act, act2=act2)
        saved.append(s)
        if i + 1 < DEPTH:
            got = _exchange_wait(coming, xc, "gather_weights_wait_%d" % (i + 1))
            weights.append(dict(zip([n for n, _ in layer_names(i + 1)], got)))

    dx, dg_final, loss_part = _loss_head(xc, final_norm_g, tgt)
    loss = lax.psum(loss_part[0, 0], MESH_AXES)

    gains = {n: [None] * DEPTH for n in ("attn_norm_g", "mlp_norm_g")}
    for n in ("mla_q_norm_g", "mla_kv_norm_g", "swa_sinks"):
        gains[n] = [None] * 2
    leaving = {}
    token = None
    dmem_n = None
    for i in reversed(range(DEPTH)):
        j = i // 2
        s = saved[i]
        wts = weights[i]
        out = {}
        du = _mm(dx, wts["mlp_w_down"], "nt", BF16, "mm_mlp_down_dx", aux=s["act"], epi="mul2aux", b_blk="rows",
                 after=token)
        out["mlp_w_down"] = _mm(s["act2"], dx, "tn", BF16, "mm_mlp_down_dw", o_blk="rows")
        dhn2 = _mm(du, wts["mlp_w_up"].transpose(0, 2, 1), "nn", F32, "mm_mlp_up_dx", b_blk="rows")
        out["mlp_w_up"] = _mm(s["hn2"], du, "tn", BF16, "mm_mlp_up_dw", o_blk="cols")
        dx1, dg = _rmsnorm_bwd(s["x1"], 0, D_MODEL, mlp_norm_g[i], dhn2, dx, F32, "rmsnorm_bwd")
        gains["mlp_norm_g"][i] = dg[0]

        do = _mm(dx1, s["w_o_mix"], "nt", BF16, "mm_o_mix_dx")
        dcross = _mm(dx1, s["w_o_cross"], "nt", BF16, "mm_o_cross_dx")
        dw_o = jnp.concatenate([_mm(s["o"], dx1, "tn", F32, "mm_o_mix_dw"),
                                _mm(s["cross"], dx1, "tn", F32, "mm_o_cross_dw")], axis=0)
        out["w_o"] = _split(_unpad_slots(dw_o, 0), 1)
        dqc, dkm, dvm = _cross_attn_bwd(s["proj"], s["qoff"], s["kvmem"], dcross)
        dkvmem = jnp.concatenate([dkm, dvm], axis=1).astype(BF16)
        out["w_mem_kv"] = _split(_unpad_slots(_mm(mem_n, dkvmem, "tn", F32, "mm_mem_kv_dw"), 1), 1)
        dmem_n = _mm(dkvmem, s["w_mem"], "nt", F32, "mm_mem_kv_dx" if dmem_n is None else "mm_mem_kv_dx_acc",
                     res=dmem_n)
        leaving[(i, "main")], token = _exchange_start([out[n] for n, _ in layer_names(i)[-4:]], True,
                                                      "exchange_grads_main_start_%d" % i)

        if i % 2 == 0:
            dq, dk, dv = _mla_attn_bwd(s["q"], s["k"], s["v"], s["o"], do, s["lse"], token)
            dqraw, dkv, dkr = _mla_rope_bwd(dq, dk, dv, tabs, consts)
            dcqn = _mm(dqraw, s["w_uq"], "nt", F32, "mm_mla_uq_dx")
            out["mla_w_uq"] = _split(_unpad_slots(_mm(s["cqn"], dqraw, "tn", F32, "mm_mla_uq_dw"), 1, MLA_QK), 2)
            dckvn = _mm(dkv, s["w_kv"], "nt", F32, "mm_mla_ukv_dx")
            out["mla_w_ukv"] = _split(_mla_ukv_unpad(_mm(s["ckvn"], dkv, "tn", F32, "mm_mla_ukv_dw")), 2)
            dcq, dg = _rmsnorm_bwd(s["proj"], 0, MLA_Q_RANK, mla_q_norm_g[j], dcqn, None, BF16, "rmsnorm_bwd_q")
            gains["mla_q_norm_g"][j] = dg[0]
            dckv, dg = _rmsnorm_bwd(s["proj"], 2, MLA_KV_RANK, mla_kv_norm_g[j], dckvn, None, BF16, "rmsnorm_bwd_kv")
            gains["mla_kv_norm_g"][j] = dg[0]
            dproj = jnp.concatenate([dcq, dkr.astype(BF16), dckv, dqc.astype(BF16)], axis=1)
            dhn = _mm(dproj, s["w_in"], "nt", F32, "mm_mla_in_dx")
            out["mla_w_in"] = _split(_mla_in_unpad(_mm(s["hn"], dproj, "tn", F32, "mm_mla_in_dw")), 1)
        else:
            dq, dk, dv, dsink = _swa_attn_bwd(s["proj"], s["o"], do, s["lse"], pos_col, pos_row, slopes, swa_sinks[j],
                                              token)
            gains["swa_sinks"][j] = dsink[::8, 0]
            dproj = jnp.concatenate([dq, dk, dv, dqc], axis=1).astype(BF16)
            dhn = _mm(dproj, s["w_in"], "nt", F32, "mm_swa_in_dx")
            out["swa_w_in"] = _split(_unpad_slots(_mm(s["hn"], dproj, "tn", F32, "mm_swa_in_dw"), 1), 2)
        dx, dg = _rmsnorm_bwd(s["x_in"], 0, D_MODEL, attn_norm_g[i], dhn, dx1, F32, "rmsnorm_bwd")
        gains["attn_norm_g"][i] = dg[0]

        leaving[(i, "mixer")], token = _exchange_start([out[n] for n, _ in layer_names(i)[:-4]], True,
                                                       "exchange_grads_mixer_start_%d" % i)

    _, dg_mem = _rmsnorm_bwd(mem0, 0, D_MODEL, mem_norm_g, dmem_n, None, BF16, "rmsnorm_bwd_mem")
    gains = {n: jnp.stack(g) for n, g in gains.items()}
    gains["mem_norm_g"] = dg_mem[0]
    gains["final_norm_g"] = dg_final[0]

    result = {}

    def adamw_of(names, received):
        for n in names:
            parts = jnp.stack([received[(n, l)] for l in range(given[n].shape[0])])
            for kind, r in enumerate(_adamw(parts, given[n], given["m_" + n], given["v_" + n], "adamw_" + n)):
                result[(kind, n)] = r

    received = {}
    for i in reversed(range(DEPTH)):
        got = _exchange_wait(leaving[(i, "main")], dx, "exchange_grads_main_wait_%d" % i)
        received.update(zip(layer_names(i)[-4:], got))
    adamw_of(("mlp_w_up", "mlp_w_down", "w_o", "w_mem_kv"), received)
    for i in reversed(range(DEPTH)):
        got = _exchange_wait(leaving[(i, "mixer")], result[(0, "w_mem_kv")], "exchange_grads_mixer_wait_%d" % i)
        received.update(zip(layer_names(i)[:-4], got))
    adamw_of(("mla_w_in", "mla_w_uq", "mla_w_ukv", "swa_w_in"), received)

    rep_shapes = [given[n].shape for n in REPLICATED]
    rep_parts = _all_gather([_pack([gains[n] for n in REPLICATED], SLOT, 8, F32)], "gather_gain_grads")[0]
    rep_packed = [_pack([given[p + n] for n in REPLICATED], SLOT, 8, F32)[None] for p in ("", "m_", "v_")]
    for kind, r in enumerate(_adamw(rep_parts[None], *rep_packed, "adamw_gains")):
        for n, part in zip(REPLICATED, _unpack(r[0], rep_shapes)):
            result[(kind, n)] = part

    outs = [loss, dx.reshape(1, seq, D_MODEL)]
    for kind in range(4):
        outs += [result[(kind, n)] for n in WEIGHT_ORDER]
    return tuple(outs)
```

```python
import functools

import jax
import jax.numpy as jnp
from jax import lax
from jax.experimental import pallas as pl
from jax.experimental.pallas import tpu as pltpu

F32 = jnp.float32
BF16 = jnp.bfloat16

D_MODEL = 1024
D_FF = 4096
N_MEM = 256
DEPTH = 4
SLOT = 128
HEAD_DIM = 64
MLA_HEADS = 12
MLA_QK = 96
MLA_Q_RANK = 384
MLA_KV_RANK = 256
SWA_HEADS = 12
SWA_KV_HEADS = 4
SWA_GROUP = 3
MEM_HEADS = 4
WINDOW = 128
EPS = 1e-6
NEG = -1e30
ROPE_THETA = 10000.0
N_DEV = 8

ADAM_LR = 0.001
ADAM_B1 = 0.9
ADAM_B2 = 0.999
ADAM_EPS = 1e-08
ADAM_WD = 0.01
ADAM_STEP = 10

TM = 512
TQ_MLA = 1024
MLA_PACK = 2
TQ_CROSS = 2048
MM_VMEM_BUDGET = 30 * 1024 * 1024
ADAM_ROWS = 128
VMEM_LIMIT = 48 * 1024 * 1024

MESH_AXES = ("x", "y", "c")

MLA_PAD_IN = 384 + SLOT + 256 + MEM_HEADS * SLOT
MLA_QOFF = (384 + SLOT + 256) // SLOT
SWA_PAD_IN = (SWA_HEADS + 2 * SWA_KV_HEADS + MEM_HEADS) * SLOT
SWA_QOFF = SWA_HEADS + 2 * SWA_KV_HEADS

SHARDED = (
    ("mla_w_in", 1), ("mla_w_uq", 2), ("mla_w_ukv", 2), ("swa_w_in", 2),
    ("w_mem_kv", 1), ("w_o", 1), ("mlp_w_up", 2), ("mlp_w_down", 1),
)
REPLICATED = ("attn_norm_g", "mlp_norm_g", "mem_norm_g", "final_norm_g",
              "mla_q_norm_g", "mla_kv_norm_g", "swa_sinks")
WEIGHT_ORDER = ("attn_norm_g", "mlp_norm_g", "mem_norm_g", "final_norm_g", "mla_w_in",
                "mla_q_norm_g", "mla_kv_norm_g", "mla_w_uq", "mla_w_ukv", "swa_w_in",
                "swa_sinks", "w_mem_kv", "w_o", "mlp_w_up", "mlp_w_down")


def _cparams():
    return pltpu.CompilerParams(vmem_limit_bytes=VMEM_LIMIT)


_DIMS = {"nn": (((1,), (0,)), ((), ())), "nt": (((1,), (1,)), ((), ())), "tn": (((0,), (0,)), ((), ()))}


def _mm_tiles(m, n, k, a_bytes, b_bytes, o_bytes, extra_bytes, tm_fixed, tn_fixed):
    best = None
    for tm in ([tm_fixed] if tm_fixed else [t for t in (1024, 512, 256, 128) if m % t == 0] or [m]):
        for tn in ([tn_fixed] if tn_fixed else [t for t in range(1024, 0, -SLOT) if n % t == 0] or [n]):
            need = 2 * (tm * k * a_bytes + k * tn * b_bytes + tm * tn * (o_bytes + extra_bytes))
            if need <= MM_VMEM_BUDGET and (best is None or tm * tn > best[0] * best[1]):
                best = (tm, tn)
    assert best is not None, (m, n, k)
    return best


def _mm(a, b, mode, out_dtype, name, res=None, aux=None, epi=None, b_blk=None, o_blk=None, after=None):
    if b_blk is not None:
        nb, br, bc = b.shape
        b_shape = (nb * br, bc) if b_blk == "rows" else (br, nb * bc)
    else:
        b_shape = b.shape
    if mode == "nn":
        (m, k), (k2, n) = a.shape, b_shape
    elif mode == "nt":
        (m, k), (n, k2) = a.shape, b_shape
    else:
        (k, m), (k2, n) = a.shape, b_shape
    assert k == k2, (a.shape, b_shape, mode)
    k_blocked = b_blk is not None and (b_blk == "rows") == (mode != "nt")
    assert not (k_blocked and mode == "nt")
    tn_fixed = None
    if b_blk is not None and not k_blocked:
        tn_fixed = br if b_blk == "rows" else bc
    if o_blk == "cols":
        tn_fixed = n // N_DEV
    tm_fixed = m // N_DEV if o_blk == "rows" else None
    has_res, has_aux = res is not None, aux is not None
    assert o_blk is None or not (has_res or has_aux)
    n_out = 2 if epi == "relu2" else 1
    tm, tn = _mm_tiles(m, n, k, a.dtype.itemsize, b.dtype.itemsize, n_out * jnp.dtype(out_dtype).itemsize,
                       (4 if has_res else 0) + (aux.dtype.itemsize if has_aux else 0), tm_fixed, tn_fixed)
    dims = _DIMS[mode]
    if mode == "tn":
        a_spec = pl.BlockSpec((k, tm), lambda i, j: (0, i))
    else:
        a_spec = pl.BlockSpec((tm, k), lambda i, j: (i, 0))
    if b_blk is None:
        if mode == "nt":
            b_spec = pl.BlockSpec((tn, k), lambda i, j: (j, 0))
        else:
            b_spec = pl.BlockSpec((k, tn), lambda i, j: (0, j))
    elif k_blocked:
        b_spec = pl.BlockSpec((N_DEV, br, tn), lambda i, j: (0, 0, j))
    elif mode == "nt":
        b_spec = pl.BlockSpec((None, tn, k), lambda i, j: (j, 0, 0))
    else:
        b_spec = pl.BlockSpec((None, k, tn), lambda i, j: (j, 0, 0))
    if o_blk is None:
        o_spec = pl.BlockSpec((tm, tn), lambda i, j: (i, j))
        o_shape = (m, n)
    elif o_blk == "rows":
        o_spec = pl.BlockSpec((None, tm, tn), lambda i, j: (i, 0, j))
        o_shape = (N_DEV, tm, n)
    else:
        o_spec = pl.BlockSpec((None, tm, tn), lambda i, j: (j, i, 0))
        o_shape = (N_DEV, m, tn)

    def body(*refs):
        a_ref, b_ref = refs[0], refs[1]
        pos = 2
        res_ref = aux_ref = None
        if has_res:
            res_ref = refs[pos]
            pos += 1
        if has_aux:
            aux_ref = refs[pos]
            pos += 1
        if after is not None:
            pos += 1
        outs = refs[pos:]
        bv = b_ref[...].reshape(k, tn) if k_blocked else b_ref[...]
        r = lax.dot_general(a_ref[...].astype(BF16), bv.astype(BF16), dims, preferred_element_type=F32)
        if epi == "relu2":
            r = jnp.maximum(r, 0.0)
            outs[0][...] = r.astype(outs[0].dtype)
            outs[1][...] = (r * r).astype(outs[1].dtype)
        else:
            if epi == "mul2aux":
                r = r * (2.0 * aux_ref[...].astype(F32))
            if has_res:
                r = r + res_ref[...]
            outs[0][...] = r.astype(outs[0].dtype)

    in_specs = [a_spec, b_spec]
    args = [a, b]
    if has_res:
        in_specs.append(o_spec)
        args.append(res)
    if has_aux:
        in_specs.append(o_spec)
        args.append(aux)
    if after is not None:
        in_specs.append(pl.BlockSpec(memory_space=pl.ANY))
        args.append(after)
    out_shape = [jax.ShapeDtypeStruct(o_shape, out_dtype)] * n_out
    out = pl.pallas_call(
        body, name=name, grid=(m // tm, n // tn),
        in_specs=in_specs, out_specs=[o_spec] * n_out, out_shape=out_shape, compiler_params=_cparams(),
    )(*args)
    return out if n_out == 2 else out[0]


def _rmsnorm_fwd(xarr, colblk, width, g, name, after=None):
    rows = xarr.shape[0]
    tm = min(TM, rows)

    def body(x_ref, g_ref, *rest):
        y_ref = rest[-1]
        x = x_ref[...].astype(F32)
        r = lax.rsqrt(jnp.mean(x * x, axis=1, keepdims=True) + EPS)
        y_ref[...] = (x * r * g_ref[...]).astype(y_ref.dtype)

    in_specs = [pl.BlockSpec((tm, width), lambda i: (i, colblk)), pl.BlockSpec((1, width), lambda i: (0, 0))]
    args = [xarr, g.reshape(1, width)]
    if after is not None:
        in_specs.append(pl.BlockSpec(memory_space=pl.ANY))
        args.append(after)
    return pl.pallas_call(
        body, name=name, grid=(rows // tm,), in_specs=in_specs,
        out_specs=pl.BlockSpec((tm, width), lambda i: (i, 0)),
        out_shape=jax.ShapeDtypeStruct((rows, width), BF16), compiler_params=_cparams(),
    )(*args)


def _rmsnorm_bwd(xarr, colblk, width, g, dy, dres, out_dtype, name, also_bf16=False):
    rows = xarr.shape[0]
    tm = min(TM, rows)
    has_res = dres is not None

    def body(*refs):
        x_ref, g_ref, dy_ref = refs[0], refs[1], refs[2]
        dres_ref = refs[3] if has_res else None
        dx_ref, dg_ref = refs[-3 if also_bf16 else -2], refs[-1]
        x = x_ref[...].astype(F32)
        dyv = dy_ref[...].astype(F32)
        r = lax.rsqrt(jnp.mean(x * x, axis=1, keepdims=True) + EPS)
        xh = x * r
        dxh = dyv * g_ref[...]
        dx = r * (dxh - xh * jnp.mean(dxh * xh, axis=1, keepdims=True))
        if has_res:
            dx = dx + dres_ref[...]
        dx_ref[...] = dx.astype(dx_ref.dtype)
        if also_bf16:
            refs[-2][...] = dx.astype(BF16)

        @pl.when(pl.program_id(0) == 0)
        def _():
            dg_ref[...] = jnp.zeros_like(dg_ref)

        dg_ref[...] += jnp.sum(dyv * xh, axis=0, keepdims=True)

    row_spec = pl.BlockSpec((tm, width), lambda i: (i, 0))
    vec_spec = pl.BlockSpec((1, width), lambda i: (0, 0))
    in_specs = [pl.BlockSpec((tm, width), lambda i: (i, colblk)), vec_spec, row_spec]
    args = [xarr, g.reshape(1, width), dy]
    if has_res:
        in_specs.append(row_spec)
        args.append(dres)
    dx_shapes = [jax.ShapeDtypeStruct((rows, width), out_dtype)]
    if also_bf16:
        dx_shapes.append(jax.ShapeDtypeStruct((rows, width), BF16))
    return pl.pallas_call(
        body, name=name, grid=(rows // tm,), in_specs=in_specs, out_specs=[row_spec] * len(dx_shapes) + [vec_spec],
        out_shape=dx_shapes + [jax.ShapeDtypeStruct((1, width), F32)], compiler_params=_cparams(),
    )(*args)


def _loss_head(x, g, tgt):
    rows, width = x.shape
    tm = min(TM, rows)

    def body(x_ref, g_ref, t_ref, dx_ref, dxb_ref, dg_ref, loss_ref):
        xv = x_ref[...]
        gv = g_ref[...]
        r = lax.rsqrt(jnp.mean(xv * xv, axis=1, keepdims=True) + EPS)
        xh = xv * r
        err = xh * gv - t_ref[...]
        part = 0.5 * jnp.sum(jnp.mean(err * err, axis=1, keepdims=True), axis=0, keepdims=True)
        dyv = err * (1.0 / width)
        dxh = dyv * gv
        dxv = r * (dxh - xh * jnp.mean(dxh * xh, axis=1, keepdims=True))
        dx_ref[...] = dxv
        dxb_ref[...] = dxv.astype(BF16)

        @pl.when(pl.program_id(0) == 0)
        def _():
            dg_ref[...] = jnp.zeros_like(dg_ref)
            loss_ref[...] = jnp.zeros_like(loss_ref)

        dg_ref[...] += jnp.sum(dyv * xh, axis=0, keepdims=True)
        loss_ref[...] += jnp.broadcast_to(part, loss_ref.shape)

    row_spec = pl.BlockSpec((tm, width), lambda i: (i, 0))
    vec_spec = pl.BlockSpec((1, width), lambda i: (0, 0))
    return pl.pallas_call(
        body, name="loss_head", grid=(rows // tm,), in_specs=[row_spec, vec_spec, row_spec],
        out_specs=[row_spec, row_spec, vec_spec, pl.BlockSpec((1, SLOT), lambda i: (0, 0))],
        out_shape=[jax.ShapeDtypeStruct((rows, width), F32), jax.ShapeDtypeStruct((rows, width), BF16),
                   jax.ShapeDtypeStruct((1, width), F32), jax.ShapeDtypeStruct((1, SLOT), F32)],
        compiler_params=_cparams(),
    )(x, g.reshape(1, width), tgt)


def _lane_consts():
    half = 16
    inv = ROPE_THETA ** (-(jnp.arange(half, dtype=F32) * 2.0) / 32)
    lane = jnp.arange(SLOT)
    first = (lane >= 64) & (lane < 80)
    second = (lane >= 80) & (lane < 96)
    inv_lane = jnp.where(first | second, inv[(lane - 64) % half], 0.0)
    rows = [inv_lane, (lane < 64).astype(F32), first.astype(F32), second.astype(F32)]
    rows += [jnp.zeros((SLOT,), F32)] * 4
    return jnp.stack(rows).astype(F32)


def _rope_tables(pos_col, consts):
    rows = pos_col.shape[0]
    tm = min(TM, rows)

    def body(p_ref, k_ref, c_ref, s1_ref, s2_ref):
        ang = p_ref[...] * k_ref[0:1, :]
        cos, sin = jnp.cos(ang), jnp.sin(ang)
        first, second = k_ref[2:3, :], k_ref[3:4, :]
        c_ref[...] = k_ref[1:2, :] + (first + second) * cos
        s1_ref[...] = -first * sin
        s2_ref[...] = second * sin

    spec = pl.BlockSpec((tm, SLOT), lambda i: (i, 0))
    shp = jax.ShapeDtypeStruct((rows, SLOT), F32)
    return pl.pallas_call(
        body, name="rope_tables", grid=(rows // tm,),
        in_specs=[pl.BlockSpec((tm, 1), lambda i: (i, 0)), pl.BlockSpec((8, SLOT), lambda i: (0, 0))],
        out_specs=[spec, spec, spec], out_shape=[shp, shp, shp], compiler_params=_cparams(),
    )(pos_col, consts)


def _rot(xv, c, s1, s2):
    return xv * c + pltpu.roll(xv, SLOT - 16, 1) * s1 + pltpu.roll(xv, 16, 1) * s2


def _rot_t(dy, c, s1, s2):
    return dy * c + pltpu.roll(dy * s1, 16, 1) + pltpu.roll(dy * s2, SLOT - 16, 1)


def _mla_rope_fwd(qraw, kvraw, proj, tabs):
    rows = qraw.shape[0]
    tm = min(256, rows)
    hw = MLA_HEADS * SLOT

    def body(q_ref, kv_ref, kr_ref, c_ref, s1_ref, s2_ref, qo, ko, vo):
        c, s1, s2 = c_ref[...], s1_ref[...], s2_ref[...]
        kr = _rot(kr_ref[...], c, s1, s2)
        for h in range(MLA_HEADS):
            sl = slice(h * SLOT, (h + 1) * SLOT)
            qo[:, sl] = _rot(q_ref[:, sl], c, s1, s2).astype(BF16)
            ko[:, sl] = (kv_ref[:, sl] + kr).astype(BF16)
            vo[:, sl] = kv_ref[:, hw + h * SLOT:hw + (h + 1) * SLOT].astype(BF16)

    tab = pl.BlockSpec((tm, SLOT), lambda i: (i, 0))
    wide = pl.BlockSpec((tm, hw), lambda i: (i, 0))
    shp = jax.ShapeDtypeStruct((rows, hw), BF16)
    return pl.pallas_call(
        body, name="mla_rope_fwd", grid=(rows // tm,),
        in_specs=[wide, pl.BlockSpec((tm, 2 * hw), lambda i: (i, 0)), pl.BlockSpec((tm, SLOT), lambda i: (i, 3)),
                  tab, tab, tab],
        out_specs=[wide, wide, wide], out_shape=[shp, shp, shp], compiler_params=_cparams(),
    )(qraw, kvraw, proj, *tabs)


def _mla_rope_bwd(dq, dk, dv, tabs, consts):
    rows = dq.shape[0]
    tm = min(256, rows)
    hw = MLA_HEADS * SLOT

    def body(dq_ref, dk_ref, dv_ref, c_ref, s1_ref, s2_ref, k_ref, dqo, dkvo, dkro):
        c, s1, s2 = c_ref[...], s1_ref[...], s2_ref[...]
        ksum = jnp.zeros((tm, SLOT), F32)
        for h in range(MLA_HEADS):
            sl = slice(h * SLOT, (h + 1) * SLOT)
            dqo[:, sl] = _rot_t(dq_ref[:, sl], c, s1, s2).astype(BF16)
            dkh = dk_ref[:, sl]
            ksum = ksum + dkh
            dkvo[:, sl] = dkh.astype(BF16)
            dkvo[:, hw + h * SLOT:hw + (h + 1) * SLOT] = dv_ref[:, sl].astype(BF16)
        dkro[...] = _rot_t(ksum, c, s1, s2) * (k_ref[2:3, :] + k_ref[3:4, :])

    tab = pl.BlockSpec((tm, SLOT), lambda i: (i, 0))
    wide = pl.BlockSpec((tm, hw), lambda i: (i, 0))
    return pl.pallas_call(
        body, name="mla_rope_bwd", grid=(rows // tm,),
        in_specs=[wide, wide, wide, tab, tab, tab, pl.BlockSpec((8, SLOT), lambda i: (0, 0))],
        out_specs=[wide, pl.BlockSpec((tm, 2 * hw), lambda i: (i, 0)), tab],
        out_shape=[jax.ShapeDtypeStruct((rows, hw), BF16), jax.ShapeDtypeStruct((rows, 2 * hw), BF16),
                   jax.ShapeDtypeStruct((rows, SLOT), F32)],
        compiler_params=_cparams(),
    )(dq, dk, dv, *tabs, consts)


def _nt(a, b):
    return lax.dot_general(a, b, _DIMS["nt"], preferred_element_type=F32)


def _tn(a, b):
    return lax.dot_general(a, b, _DIMS["tn"], preferred_element_type=F32)


def _nn(a, b):
    return lax.dot_general(a, b, _DIMS["nn"], preferred_element_type=F32)


def _causal(t):
    return lax.broadcasted_iota(jnp.int32, (t, t), 1) <= lax.broadcasted_iota(jnp.int32, (t, t), 0)


def _mla_attn_fwd(q, k, v):
    rows = q.shape[0]
    t = min(TQ_MLA, rows)
    nt = rows // t
    scale = MLA_QK ** -0.5
    wide = MLA_PACK * SLOT

    def body(q_ref, k_ref, v_ref, o_ref, lse_ref, m_sc, l_sc, acc_sc):
        i, j = pl.program_id(1), pl.program_id(2)

        @pl.when(j == 0)
        def _():
            m_sc[...] = jnp.full_like(m_sc, NEG)
            l_sc[...] = jnp.zeros_like(l_sc)
            acc_sc[...] = jnp.zeros_like(acc_sc)

        def step(diagonal):
            for hh in range(MLA_PACK):
                sl = slice(hh * SLOT, (hh + 1) * SLOT)
                s = _nt(q_ref[:, sl], k_ref[:, sl]) * scale
                if diagonal:
                    s = jnp.where(_causal(t), s, NEG)
                m_prev = m_sc[hh]
                m_new = jnp.maximum(m_prev, jnp.max(s, axis=1, keepdims=True))
                p = jnp.exp(s - m_new)
                alpha = jnp.exp(m_prev - m_new)
                l_new = alpha * l_sc[hh] + jnp.sum(p, axis=1, keepdims=True)
                acc = alpha * acc_sc[:, sl] + _nn(p.astype(BF16), v_ref[:, sl])
                if diagonal:
                    o_ref[:, sl] = (acc / l_new).astype(o_ref.dtype)
                    lse_ref[:, sl] = jnp.broadcast_to(m_new + jnp.log(l_new), (t, SLOT))
                else:
                    m_sc[hh] = m_new
                    l_sc[hh] = l_new
                    acc_sc[:, sl] = acc

        @pl.when(j < i)
        def _():
            step(False)

        @pl.when(j == i)
        def _():
            step(True)

    q_spec = pl.BlockSpec((t, wide), lambda h, i, j: (i, h))
    kv_spec = pl.BlockSpec((t, wide), lambda h, i, j: (jnp.minimum(j, i), h))
    return pl.pallas_call(
        body, name="mla_attn_fwd", grid=(MLA_HEADS // MLA_PACK, nt, nt),
        in_specs=[q_spec, kv_spec, kv_spec], out_specs=[q_spec, q_spec],
        out_shape=[jax.ShapeDtypeStruct(q.shape, BF16), jax.ShapeDtypeStruct(q.shape, F32)],
        scratch_shapes=[pltpu.VMEM((MLA_PACK, t, 1), F32), pltpu.VMEM((MLA_PACK, t, 1), F32),
                        pltpu.VMEM((t, wide), F32)],
        compiler_params=_cparams(),
    )(q, k, v)


def _mla_attn_bwd(q, k, v, o, do, lse, after):
    rows = q.shape[0]
    t = min(TQ_MLA, rows)
    nt = rows // t
    scale = MLA_QK ** -0.5
    wide = MLA_PACK * SLOT

    def body(q_ref, k_ref, v_ref, o_ref, do_ref, lse_ref, after_ref, dq_ref, dk_ref, dv_ref, dk_sc, dv_sc):
        j, i = pl.program_id(1), pl.program_id(2)

        @pl.when((j == 0) & (i == 0))
        def _():
            dq_ref[...] = jnp.zeros_like(dq_ref)

        @pl.when(i == 0)
        def _():
            dk_sc[...] = jnp.zeros_like(dk_sc)
            dv_sc[...] = jnp.zeros_like(dv_sc)

        def step(diagonal):
            r0 = pl.multiple_of(i * t, t)
            for hh in range(MLA_PACK):
                sl = slice(hh * SLOT, (hh + 1) * SLOT)
                qv, kv, dov = q_ref[:, sl], k_ref[:, sl], do_ref[:, sl]
                s = _nt(qv, kv) * scale
                if diagonal:
                    s = jnp.where(_causal(t), s, NEG)
                p = jnp.exp(s - lse_ref[:, hh * SLOT:hh * SLOT + 1])
                delta = jnp.sum(dov.astype(F32) * o_ref[:, sl].astype(F32), axis=1, keepdims=True)
                dp = _nt(dov, v_ref[:, sl])
                ds = (p * (dp - delta) * scale).astype(BF16)
                dv_sc[:, sl] += _tn(p.astype(BF16), dov)
                dk_sc[:, sl] += _tn(ds, qv)
                dq_ref[pl.ds(r0, t), sl] += _nn(ds, kv)

        @pl.when(i > j)
        def _():
            step(False)

        @pl.when(i == j)
        def _():
            step(True)

        @pl.when(i == nt - 1)
        def _():
            dk_ref[...] = dk_sc[...]
            dv_ref[...] = dv_sc[...]

    q_spec = pl.BlockSpec((t, wide), lambda h, j, i: (jnp.maximum(i, j), h))
    kv_spec = pl.BlockSpec((t, wide), lambda h, j, i: (j, h))
    head_spec = pl.BlockSpec((rows, wide), lambda h, j, i: (0, h))
    shp = jax.ShapeDtypeStruct(q.shape, F32)
    return pl.pallas_call(
        body, name="mla_attn_bwd", grid=(MLA_HEADS // MLA_PACK, nt, nt),
        in_specs=[q_spec, kv_spec, kv_spec, q_spec, q_spec, q_spec, pl.BlockSpec(memory_space=pl.ANY)],
        out_specs=[head_spec, kv_spec, kv_spec], out_shape=[shp, shp, shp],
        scratch_shapes=[pltpu.VMEM((t, wide), F32), pltpu.VMEM((t, wide), F32)],
        compiler_params=_cparams(),
    )(q, k, v, o, do, lse, after)


def _swa_specs(t):
    def prev(i):
        return jnp.maximum(i - 1, 0)
    q3 = pl.BlockSpec((t, SWA_GROUP * SLOT), lambda h, i: (i, h))
    kp = pl.BlockSpec((t, SLOT), lambda h, i: (prev(i), SWA_HEADS + h))
    kc = pl.BlockSpec((t, SLOT), lambda h, i: (i, SWA_HEADS + h))
    vp = pl.BlockSpec((t, SLOT), lambda h, i: (prev(i), SWA_HEADS + SWA_KV_HEADS + h))
    vc = pl.BlockSpec((t, SLOT), lambda h, i: (i, SWA_HEADS + SWA_KV_HEADS + h))
    pcol = pl.BlockSpec((t, 1), lambda h, i: (i, 0))
    prow_p = pl.BlockSpec((1, t), lambda h, i: (0, prev(i)))
    prow_c = pl.BlockSpec((1, t), lambda h, i: (0, i))
    return [q3, kp, kc, vp, vc, pcol, prow_p, prow_c]


def _stack(ref):
    return jnp.concatenate([ref[:, g * SLOT:(g + 1) * SLOT] for g in range(SWA_GROUP)], axis=0)


def _swa_logits(q3, kp, kc, pq, pkp, pkc, slope_ref, kvh, i, t):
    r = lax.broadcasted_iota(jnp.int32, (t, t), 0)
    c = lax.broadcasted_iota(jnp.int32, (t, t), 1)
    ok_c = c <= r
    ok_p = (c - r) > jnp.where(i > 0, 0, t)
    dist_p, dist_c = pq - pkp, pq - pkc
    s_p3 = _nt(q3, kp) * (HEAD_DIM ** -0.5)
    s_c3 = _nt(q3, kc) * (HEAD_DIM ** -0.5)
    out = []
    for g in range(SWA_GROUP):
        slope = slope_ref[kvh * SWA_GROUP + g]
        rows = slice(g * t, (g + 1) * t)
        out.append((jnp.where(ok_p, s_p3[rows] - slope * dist_p, NEG),
                    jnp.where(ok_c, s_c3[rows] - slope * dist_c, NEG)))
    return out


def _swa_attn_fwd(proj, pos_col, pos_row, slopes, sinks):
    rows = proj.shape[0]
    t = WINDOW
    hw = SWA_HEADS * SLOT

    def body(slope_ref, sink_ref, q_ref, kp_ref, kc_ref, vp_ref, vc_ref, pq_ref, pkp_ref, pkc_ref, o_ref, lse_ref):
        kvh, i = pl.program_id(0), pl.program_id(1)
        logits = _swa_logits(_stack(q_ref), kp_ref[...], kc_ref[...], pq_ref[...], pkp_ref[...], pkc_ref[...],
                             slope_ref, kvh, i, t)
        e_p, e_c, norm = [], [], []
        for g, (s_p, s_c) in enumerate(logits):
            sink = sink_ref[kvh * SWA_GROUP + g]
            m = jnp.maximum(jnp.maximum(jnp.max(s_p, axis=1, keepdims=True), jnp.max(s_c, axis=1, keepdims=True)),
                            sink)
            ep, ec = jnp.exp(s_p - m), jnp.exp(s_c - m)
            l = jnp.sum(ep, axis=1, keepdims=True) + jnp.sum(ec, axis=1, keepdims=True) + jnp.exp(sink - m)
            e_p.append(ep.astype(BF16))
            e_c.append(ec.astype(BF16))
            norm.append(l)
            lse_ref[:, g * SLOT:(g + 1) * SLOT] = jnp.broadcast_to(m + jnp.log(l), (t, SLOT))
        acc = _nn(jnp.concatenate(e_p, axis=0), vp_ref[...]) + _nn(jnp.concatenate(e_c, axis=0), vc_ref[...])
        for g in range(SWA_GROUP):
            o_ref[:, g * SLOT:(g + 1) * SLOT] = (acc[g * t:(g + 1) * t] / norm[g]).astype(o_ref.dtype)

    smem = pl.BlockSpec(memory_space=pltpu.SMEM)
    out_spec = pl.BlockSpec((t, SWA_GROUP * SLOT), lambda h, i: (i, h))
    return pl.pallas_call(
        body, name="swa_attn_fwd", grid=(SWA_KV_HEADS, rows // t),
        in_specs=[smem, smem] + _swa_specs(t), out_specs=[out_spec, out_spec],
        out_shape=[jax.ShapeDtypeStruct((rows, hw), BF16), jax.ShapeDtypeStruct((rows, hw), F32)],
        compiler_params=_cparams(),
    )(slopes, sinks, proj, proj, proj, proj, proj, pos_col, pos_row, pos_row)


def _swa_attn_bwd(proj, o, do, lse, pos_col, pos_row, slopes, sinks, after):
    rows = proj.shape[0]
    t = WINDOW
    hw = SWA_HEADS * SLOT
    scale = HEAD_DIM ** -0.5

    def body(slope_ref, sink_ref, q_ref, kp_ref, kc_ref, vp_ref, vc_ref, pq_ref, pkp_ref, pkc_ref,
             o_ref, do_ref, lse_ref, after_ref, dq_ref, dk_ref, dv_ref, dsink_ref):
        kvh, i = pl.program_id(0), pl.program_id(1)

        @pl.when(i == 0)
        def _():
            dk_ref[...] = jnp.zeros_like(dk_ref)
            dv_ref[...] = jnp.zeros_like(dv_ref)
            dsink_ref[...] = jnp.zeros_like(dsink_ref)

        q3, do3 = _stack(q_ref), _stack(do_ref)
        logits = _swa_logits(q3, kp_ref[...], kc_ref[...], pq_ref[...], pkp_ref[...], pkc_ref[...],
                             slope_ref, kvh, i, t)
        dp_p3, dp_c3 = _nt(do3, vp_ref[...]), _nt(do3, vc_ref[...])
        p_p, p_c, ds_p, ds_c = [], [], [], []
        for g, (s_p, s_c) in enumerate(logits):
            sl = slice(g * SLOT, (g + 1) * SLOT)
            rws = slice(g * t, (g + 1) * t)
            lse_g = lse_ref[:, g * SLOT:g * SLOT + 1]
            pp, pc = jnp.exp(s_p - lse_g), jnp.exp(s_c - lse_g)
            delta = jnp.sum(do_ref[:, sl].astype(F32) * o_ref[:, sl].astype(F32), axis=1, keepdims=True)
            p_p.append(pp.astype(BF16))
            p_c.append(pc.astype(BF16))
            ds_p.append((pp * (dp_p3[rws] - delta)).astype(BF16))
            ds_c.append((pc * (dp_c3[rws] - delta)).astype(BF16))
            sink = sink_ref[kvh * SWA_GROUP + g]
            dsink = -jnp.sum(jnp.exp(sink - lse_g) * delta, axis=0, keepdims=True)
            dsink_ref[g * 8:(g + 1) * 8, :] += jnp.broadcast_to(dsink, (8, SLOT))
        p_p3, p_c3 = jnp.concatenate(p_p, axis=0), jnp.concatenate(p_c, axis=0)
        ds_p3, ds_c3 = jnp.concatenate(ds_p, axis=0), jnp.concatenate(ds_c, axis=0)
        dq3 = (_nn(ds_p3, kp_ref[...]) + _nn(ds_c3, kc_ref[...])) * scale
        for g in range(SWA_GROUP):
            dq_ref[:, g * SLOT:(g + 1) * SLOT] = dq3[g * t:(g + 1) * t]
        r_c = pl.multiple_of(i * t, t)
        dk_ref[pl.ds(r_c, t), :] += _tn(ds_c3, q3) * scale
        dv_ref[pl.ds(r_c, t), :] += _tn(p_c3, do3)

        @pl.when(i > 0)
        def _():
            r_p = pl.multiple_of((i - 1) * t, t)
            dk_ref[pl.ds(r_p, t), :] += _tn(ds_p3, q3) * scale
            dv_ref[pl.ds(r_p, t), :] += _tn(p_p3, do3)

    smem = pl.BlockSpec(memory_space=pltpu.SMEM)
    qlike = pl.BlockSpec((t, SWA_GROUP * SLOT), lambda h, i: (i, h))
    kv_out = pl.BlockSpec((rows, SLOT), lambda h, i: (0, h))
    return pl.pallas_call(
        body, name="swa_attn_bwd", grid=(SWA_KV_HEADS, rows // t),
        in_specs=[smem, smem] + _swa_specs(t) + [qlike, qlike, qlike, pl.BlockSpec(memory_space=pl.ANY)],
        out_specs=[qlike, kv_out, kv_out, pl.BlockSpec((SWA_GROUP * 8, SLOT), lambda h, i: (h, 0))],
        out_shape=[jax.ShapeDtypeStruct((rows, hw), F32), jax.ShapeDtypeStruct((rows, SWA_KV_HEADS * SLOT), F32),
                   jax.ShapeDtypeStruct((rows, SWA_KV_HEADS * SLOT), F32),
                   jax.ShapeDtypeStruct((SWA_HEADS * 8, SLOT), F32)],
        compiler_params=_cparams(),
    )(slopes, sinks, proj, proj, proj, proj, proj, pos_col, pos_row, pos_row, o, do, lse, after)


def _cross_attn_fwd(proj, qoff, kvmem):
    rows = proj.shape[0]
    t = min(TQ_CROSS, rows)

    def body(q_ref, k_ref, v_ref, o_ref):
        s = _nt(q_ref[...].astype(BF16), k_ref[...]) * (HEAD_DIM ** -0.5)
        e = jnp.exp(s - jnp.max(s, axis=1, keepdims=True))
        p = e / jnp.sum(e, axis=1, keepdims=True)
        o_ref[...] = _nn(p.astype(BF16), v_ref[...]).astype(o_ref.dtype)

    return pl.pallas_call(
        body, name="cross_attn_fwd", grid=(rows // t, MEM_HEADS),
        in_specs=[pl.BlockSpec((t, SLOT), lambda i, h: (i, qoff + h)),
                  pl.BlockSpec((N_MEM, SLOT), lambda i, h: (0, h)),
                  pl.BlockSpec((N_MEM, SLOT), lambda i, h: (0, MEM_HEADS + h))],
        out_specs=pl.BlockSpec((t, SLOT), lambda i, h: (i, h)),
        out_shape=jax.ShapeDtypeStruct((rows, MEM_HEADS * SLOT), BF16), compiler_params=_cparams(),
    )(proj, kvmem, kvmem)


def _cross_attn_bwd(proj, qoff, kvmem, do):
    rows = proj.shape[0]
    t = min(TQ_CROSS, rows)
    scale = HEAD_DIM ** -0.5

    def body(q_ref, k_ref, v_ref, do_ref, dq_ref, dk_ref, dv_ref):
        @pl.when(pl.program_id(1) == 0)
        def _():
            dk_ref[...] = jnp.zeros_like(dk_ref)
            dv_ref[...] = jnp.zeros_like(dv_ref)

        qv, kv, dov = q_ref[...].astype(BF16), k_ref[...], do_ref[...]
        s = _nt(qv, kv) * scale
        e = jnp.exp(s - jnp.max(s, axis=1, keepdims=True))
        p = e / jnp.sum(e, axis=1, keepdims=True)
        dp = _nt(dov, v_ref[...])
        ds = (p * (dp - jnp.sum(p * dp, axis=1, keepdims=True))).astype(BF16)
        dq_ref[...] = _nn(ds, kv) * scale
        dk_ref[...] += _tn(ds, qv) * scale
        dv_ref[...] += _tn(p.astype(BF16), dov)

    mem_out = pl.BlockSpec((N_MEM, SLOT), lambda h, i: (0, h))
    return pl.pallas_call(
        body, name="cross_attn_bwd", grid=(MEM_HEADS, rows // t),
        in_specs=[pl.BlockSpec((t, SLOT), lambda h, i: (i, qoff + h)),
                  pl.BlockSpec((N_MEM, SLOT), lambda h, i: (0, h)),
                  pl.BlockSpec((N_MEM, SLOT), lambda h, i: (0, MEM_HEADS + h)),
                  pl.BlockSpec((t, SLOT), lambda h, i: (i, h))],
        out_specs=[pl.BlockSpec((t, SLOT), lambda h, i: (i, h)), mem_out, mem_out],
        out_shape=[jax.ShapeDtypeStruct((rows, MEM_HEADS * SLOT), F32),
                   jax.ShapeDtypeStruct((N_MEM, MEM_HEADS * SLOT), F32),
                   jax.ShapeDtypeStruct((N_MEM, MEM_HEADS * SLOT), F32)],
        compiler_params=_cparams(),
    )(proj, kvmem, kvmem, do)


def _place():
    return lax.axis_index("x"), lax.axis_index("y"), lax.axis_index("c")


def _flip(v, bit):
    return 1 - v if bit else v


def _all_gather(blocks, name):
    nb = len(blocks)

    def body(*refs):
        x_refs, out_refs = refs[:nb], refs[nb:2 * nb]
        send_sems, recv_sems, local_sems = refs[2 * nb:]
        x, y, c = _place()
        me, sibling = (x, y, c), (x, y, 1 - c)
        chips = [(1 - x, y), (x, 1 - y), (1 - x, 1 - y)]

        def copy(b, k, blk, to, from_input=False):
            slot = out_refs[b].at[4 * blk[0] + 2 * blk[1] + blk[2]]
            return pltpu.make_async_remote_copy(
                src_ref=x_refs[b] if from_input else slot, dst_ref=slot,
                send_sem=send_sems.at[b, k], recv_sem=recv_sems.at[b, k],
                device_id=to, device_id_type=pl.DeviceIdType.MESH)

        mine = [pltpu.make_async_copy(x_refs[b], out_refs[b].at[4 * x + 2 * y + c], local_sems.at[b])
                for b in range(nb)]
        for cp in mine:
            cp.start()
        first = []
        for b in range(nb):
            first.append(copy(b, 0, me, sibling, from_input=True))
            first += [copy(b, 1 + n, me, (*chip, c), from_input=True) for n, chip in enumerate(chips)]
        for cp in first:
            cp.start()
        passed = []
        for n, chip in enumerate(chips):
            for b in range(nb):
                copy(b, 1 + n, (*chip, c), me).wait_recv()
                passed.append(copy(b, 4 + n, (*chip, c), sibling))
                passed[-1].start()
        for b in range(nb):
            copy(b, 0, sibling, me).wait_recv()
            for n, chip in enumerate(chips):
                copy(b, 4 + n, (*chip, 1 - c), me).wait_recv()
        for cp in first + passed:
            cp.wait_send()
        for cp in mine:
            cp.wait()

    any_spec = pl.BlockSpec(memory_space=pl.ANY)
    return pl.pallas_call(
        body, name=name, in_specs=[any_spec] * nb, out_specs=[any_spec] * nb,
        out_shape=[jax.ShapeDtypeStruct((N_DEV,) + blk.shape, blk.dtype) for blk in blocks],
        scratch_shapes=[pltpu.SemaphoreType.DMA((nb, 7)), pltpu.SemaphoreType.DMA((nb, 7)),
                        pltpu.SemaphoreType.DMA((nb,))],
    )(*blocks)


def _peers(x, y, c):
    out = []
    for n in range(1, N_DEV):
        peer = (_flip(x, n & 4), _flip(y, n & 2), _flip(c, n & 1))
        out.append((n - 1, peer, 4 * peer[0] + 2 * peer[1] + peer[2]))
    return out


_HBM = pl.BlockSpec(memory_space=pltpu.HBM)
_SEM = pl.BlockSpec(memory_space=pltpu.SEMAPHORE)


def _exchange_start(srcs, scatter, name, after=None):
    ns = len(srcs)
    lands = [lax.empty(s.shape if scatter else (N_DEV,) + s.shape, s.dtype) for s in srcs]

    def body(*refs):
        src_refs, land_refs = refs[:ns], refs[ns:2 * ns]
        pos = 2 * ns + (1 if after is not None else 0)
        send_sems, recv_sems, token = refs[pos], refs[pos + 1], refs[-1]
        x, y, c = _place()
        my_idx = 4 * x + 2 * y + c
        for col, peer, peer_idx in _peers(x, y, c):
            for b in range(ns):
                pltpu.make_async_remote_copy(
                    src_ref=src_refs[b].at[peer_idx] if scatter else src_refs[b], dst_ref=land_refs[b].at[my_idx],
                    send_sem=send_sems.at[b * (N_DEV - 1) + col], recv_sem=recv_sems.at[b * (N_DEV - 1) + col],
                    device_id=peer, device_id_type=pl.DeviceIdType.MESH).start()
        token[...] = jnp.zeros_like(token)

    args = [pltpu.with_memory_space_constraint(a, pltpu.HBM) for a in list(srcs) + lands]
    in_specs = [_HBM] * (2 * ns)
    if after is not None:
        args.append(after)
        in_specs.append(pl.BlockSpec(memory_space=pl.ANY))
    out = pl.pallas_call(
        body, name=name, in_specs=in_specs,
        out_specs=[_SEM, _SEM] + [_HBM] * (2 * ns) + [pl.BlockSpec(memory_space=pltpu.VMEM)],
        out_shape=[pltpu.SemaphoreType.DMA((ns * (N_DEV - 1),)), pltpu.SemaphoreType.DMA((ns * (N_DEV - 1),))]
        + [pltpu.HBM(a.shape, a.dtype) for a in list(srcs) + lands] + [jax.ShapeDtypeStruct((8, SLOT), F32)],
        input_output_aliases={k: 2 + k for k in range(2 * ns)},
        compiler_params=pltpu.CompilerParams(has_side_effects=pltpu.SideEffectType.DATAFLOW_SIDE_EFFECTING),
    )(*args)
    return (out[0], out[1], out[2:2 + ns], out[2 + ns:2 + 2 * ns], scatter), out[-1]


def _exchange_wait(handle, after, name):
    send_sems, recv_sems, srcs, lands, scatter = handle
    ns = len(srcs)

    def body(*refs):
        src_refs, land_refs = refs[:ns], refs[ns:2 * ns]
        send_ref, recv_ref = refs[2 * ns], refs[2 * ns + 1]
        x, y, c = _place()
        for col, peer, peer_idx in _peers(x, y, c):
            for b in range(ns):
                copy = pltpu.make_async_remote_copy(
                    src_ref=src_refs[b].at[peer_idx] if scatter else src_refs[b], dst_ref=land_refs[b].at[peer_idx],
                    send_sem=send_ref.at[b * (N_DEV - 1) + col], recv_sem=recv_ref.at[b * (N_DEV - 1) + col],
                    device_id=peer, device_id_type=pl.DeviceIdType.MESH)
                copy.wait_send()
                copy.wait_recv()

    out = pl.pallas_call(
        body, name=name, in_specs=[_HBM] * (2 * ns) + [_SEM, _SEM, pl.BlockSpec(memory_space=pl.ANY)],
        out_specs=[_HBM] * (2 * ns),
        out_shape=[pltpu.HBM(a.shape, a.dtype) for a in list(srcs) + list(lands)],
        input_output_aliases={k: k for k in range(2 * ns)},
        compiler_params=pltpu.CompilerParams(has_side_effects=pltpu.SideEffectType.DATAFLOW_SIDE_EFFECTING),
    )(*srcs, *lands, send_sems, recv_sems, after)
    my_idx = 4 * lax.axis_index("x") + 2 * lax.axis_index("y") + lax.axis_index("c")
    landed = []
    for src, land in zip(out[:ns], out[ns:]):
        own = lax.dynamic_index_in_dim(src, my_idx, 0, keepdims=True) if scatter else src[None]
        landed.append(lax.dynamic_update_index_in_dim(land, own, my_idx, 0))
    return landed


def _adamw(parts, w, m, v, name):
    lyr, rows, cols = w.shape
    assert len(parts) == lyr
    tr = ADAM_ROWS if cols > 512 else 2 * ADAM_ROWS
    while rows % tr:
        tr //= 2
    tr = min(tr, rows)

    def body(*refs):
        p_refs = refs[:lyr]
        w_ref, m_ref, v_ref, g_out, d_out, m_out, v_out = refs[lyr:]
        for k in range(lyr):
            @pl.when(pl.program_id(0) == k)
            def _(p_ref=p_refs[k]):
                g = p_ref[0].astype(F32)
                for s in range(1, N_DEV):
                    g = g + p_ref[s].astype(F32)
                m2 = ADAM_B1 * m_ref[...] + (1.0 - ADAM_B1) * g
                v2 = ADAM_B2 * v_ref[...] + (1.0 - ADAM_B2) * (g * g)
                m_hat = m2 / (1.0 - ADAM_B1 ** ADAM_STEP)
                v_hat = v2 / (1.0 - ADAM_B2 ** ADAM_STEP)
                g_out[...] = g
                d_out[...] = -ADAM_LR * (m_hat / (jnp.sqrt(v_hat) + ADAM_EPS) + ADAM_WD * w_ref[...])
                m_out[...] = m2
                v_out[...] = v2

    def part_spec(k):
        return pl.BlockSpec((N_DEV, tr, cols), lambda l, i: (0, jnp.where(l == k, i, 0), 0))

    spec = pl.BlockSpec((None, tr, cols), lambda l, i: (l, i, 0))
    shp = jax.ShapeDtypeStruct((lyr, rows, cols), F32)
    return pl.pallas_call(
        body, name=name, grid=(lyr, rows // tr),
        in_specs=[part_spec(k) for k in range(lyr)] + [spec, spec, spec],
        out_specs=[spec] * 4, out_shape=[shp] * 4, compiler_params=_cparams(),
    )(*parts, w, m, v)


def _pack(arrays, lanes, row_mult, dtype):
    flat = jnp.concatenate([a.reshape(-1).astype(dtype) for a in arrays])
    unit = lanes * row_mult
    total = -(-flat.shape[0] // unit) * unit
    return jnp.pad(flat, (0, total - flat.shape[0])).reshape(total // lanes, lanes)


def _unpack(packed, shapes):
    flat = packed.reshape(-1)
    out, off = [], 0
    for shp in shapes:
        n = 1
        for d in shp:
            n *= d
        out.append(flat[off:off + n].reshape(shp))
        off += n
    return out


def _pad_slots(w, axis):
    axis = axis % w.ndim
    n = w.shape[axis] // HEAD_DIM
    shp = w.shape[:axis] + (n, HEAD_DIM) + w.shape[axis + 1:]
    pad = [(0, 0)] * (w.ndim + 1)
    pad[axis + 1] = (0, SLOT - HEAD_DIM)
    return jnp.pad(w.reshape(shp), pad).reshape(w.shape[:axis] + (n * SLOT,) + w.shape[axis + 1:])


def _unpad_slots(w, axis, keep=HEAD_DIM):
    axis = axis % w.ndim
    n = w.shape[axis] // SLOT
    shp = w.shape[:axis] + (n, SLOT) + w.shape[axis + 1:]
    idx = [slice(None)] * (w.ndim + 1)
    idx[axis + 1] = slice(0, keep)
    return w.reshape(shp)[tuple(idx)].reshape(w.shape[:axis] + (n * keep,) + w.shape[axis + 1:])


def _mla_in_pad(w):
    z = functools.partial(jnp.zeros, dtype=w.dtype)
    rows = w.shape[0]
    return jnp.concatenate([w[:, :384], z((rows, 64)), w[:, 640:672], z((rows, 32)), w[:, 384:640],
                            _pad_slots(w[:, 672:], 1)], axis=1)


def _mla_in_unpad(d):
    return jnp.concatenate([d[:, :384], d[:, 512:768], d[:, 448:480], _unpad_slots(d[:, 768:], 1)], axis=1)


def _mla_uq_pad(w):
    return jnp.pad(w.reshape(w.shape[0], MLA_HEADS, MLA_QK), ((0, 0), (0, 0), (0, SLOT - MLA_QK))).reshape(
        w.shape[0], MLA_HEADS * SLOT)


def _mla_ukv_pad(w):
    w3 = w.reshape(w.shape[0], MLA_HEADS, 2 * HEAD_DIM)
    pad = ((0, 0), (0, 0), (0, SLOT - HEAD_DIM))
    k = jnp.pad(w3[:, :, :HEAD_DIM], pad).reshape(w.shape[0], -1)
    v = jnp.pad(w3[:, :, HEAD_DIM:], pad).reshape(w.shape[0], -1)
    return jnp.concatenate([k, v], axis=1)


def _mla_ukv_unpad(d):
    hw = MLA_HEADS * SLOT
    k = d[:, :hw].reshape(d.shape[0], MLA_HEADS, SLOT)[:, :, :HEAD_DIM]
    v = d[:, hw:].reshape(d.shape[0], MLA_HEADS, SLOT)[:, :, :HEAD_DIM]
    return jnp.concatenate([k, v], axis=2).reshape(d.shape[0], MLA_HEADS * 2 * HEAD_DIM)


def _join(gathered, axis):
    nd, a, b = gathered.shape
    if axis == 1:
        return gathered.reshape(nd * a, b)
    return gathered.transpose(1, 0, 2).reshape(a, nd * b)


def _split(full, axis):
    r, c = full.shape
    if axis == 1:
        return full.reshape(N_DEV, r // N_DEV, c).astype(BF16)
    return full.reshape(r, N_DEV, c // N_DEV).transpose(1, 0, 2).astype(BF16)


def kernel(x, mem, positions, attn_norm_g, mlp_norm_g, mem_norm_g, final_norm_g, mla_w_in, mla_q_norm_g, mla_kv_norm_g, mla_w_uq, mla_w_ukv, swa_w_in, swa_sinks, w_mem_kv, w_o, mlp_w_up, mlp_w_down, loss_target, m_attn_norm_g, m_mlp_norm_g, m_mem_norm_g, m_final_norm_g, m_mla_w_in, m_mla_q_norm_g, m_mla_kv_norm_g, m_mla_w_uq, m_mla_w_ukv, m_swa_w_in, m_swa_sinks, m_w_mem_kv, m_w_o, m_mlp_w_up, m_mlp_w_down, v_attn_norm_g, v_mlp_norm_g, v_mem_norm_g, v_final_norm_g, v_mla_w_in, v_mla_q_norm_g, v_mla_kv_norm_g, v_mla_w_uq, v_mla_w_ukv, v_swa_w_in, v_swa_sinks, v_w_mem_kv, v_w_o, v_mlp_w_up, v_mlp_w_down):
    given = dict(locals())
    seq = x.shape[1]
    x0 = x.reshape(seq, D_MODEL)
    tgt = loss_target.reshape(seq, D_MODEL)
    mem0 = mem.reshape(N_MEM, D_MODEL)
    pos = positions.reshape(seq).astype(F32)
    pos_col, pos_row = pos.reshape(seq, 1), pos.reshape(1, seq)

    def layer_names(i):
        mixer = ("mla_w_in", "mla_w_uq", "mla_w_ukv") if i % 2 == 0 else ("swa_w_in",)
        return [(n, i // 2) for n in mixer] + [(n, i) for n in ("w_mem_kv", "w_o", "mlp_w_up", "mlp_w_down")]

    def local_weights(names):
        return [given[n][l].astype(BF16) for n, l in names]

    first_attn, first_mlp = layer_names(0)[:-2], layer_names(0)[-2:]
    weights = [dict(zip([n for n, _ in first_attn], _all_gather(local_weights(first_attn), "gather_weights_first")))]
    coming_mlp, first_token = _exchange_start(local_weights(first_mlp), False, "gather_weights_start_0",
                                              after=weights[0]["w_o"])

    consts = _lane_consts()
    tabs = _rope_tables(pos_col, consts)
    slopes = 2.0 ** (-8.0 * (jnp.arange(SWA_HEADS, dtype=F32) + 1.0) / SWA_HEADS)

    mem_n = _rmsnorm_fwd(mem0, 0, D_MODEL, mem_norm_g, "rmsnorm_fwd_mem")

    saved = []
    xc = x0
    for i in range(DEPTH):
        j = i // 2
        wts = weights[i]
        s = {"x_in": xc}
        token = None
        if i + 1 < DEPTH:
            coming, token = _exchange_start(local_weights(layer_names(i + 1)), False,
                                            "gather_weights_start_%d" % (i + 1),
                                            after=first_token if i == 0 else wts["w_o"])
        hn = _rmsnorm_fwd(xc, 0, D_MODEL, attn_norm_g[i], "rmsnorm_fwd", after=token)
        if i % 2 == 0:
            w_in = _mla_in_pad(_join(wts["mla_w_in"], 1))
            w_uq = _mla_uq_pad(_join(wts["mla_w_uq"], 2))
            w_kv = _mla_ukv_pad(_join(wts["mla_w_ukv"], 2))
            proj = _mm(hn, w_in, "nn", F32, "mm_mla_in")
            cqn = _rmsnorm_fwd(proj, 0, MLA_Q_RANK, mla_q_norm_g[j], "rmsnorm_fwd_q")
            ckvn = _rmsnorm_fwd(proj, 2, MLA_KV_RANK, mla_kv_norm_g[j], "rmsnorm_fwd_kv")
            qraw = _mm(cqn, w_uq, "nn", F32, "mm_mla_uq")
            kvraw = _mm(ckvn, w_kv, "nn", F32, "mm_mla_ukv")
            q, k, v = _mla_rope_fwd(qraw, kvraw, proj, tabs)
            o, lse = _mla_attn_fwd(q, k, v)
            qoff = MLA_QOFF
            s.update(w_uq=w_uq, w_kv=w_kv, cqn=cqn, ckvn=ckvn, q=q, k=k, v=v)
        else:
            w_in = _pad_slots(_join(wts["swa_w_in"], 2), 1)
            proj = _mm(hn, w_in, "nn", BF16, "mm_swa_in")
            o, lse = _swa_attn_fwd(proj, pos_col, pos_row, slopes, swa_sinks[j])
            qoff = SWA_QOFF
        w_mem = _pad_slots(_join(wts["w_mem_kv"], 1), 1)
        w_out = _pad_slots(_join(wts["w_o"], 1), 0)
        w_o_mix, w_o_cross = w_out[:SWA_HEADS * SLOT], w_out[SWA_HEADS * SLOT:]
        kvmem = _mm(mem_n, w_mem, "nn", BF16, "mm_mem_kv")
        cross = _cross_attn_fwd(proj, qoff, kvmem)
        x1 = _mm(o, w_o_mix, "nn", F32, "mm_o_mix", res=xc)
        x1 = _mm(cross, w_o_cross, "nn", F32, "mm_o_cross", res=x1)
        hn2 = _rmsnorm_fwd(x1, 0, D_MODEL, mlp_norm_g[i], "rmsnorm_fwd")
        if i == 0:
            wts.update(zip([n for n, _ in first_mlp], _exchange_wait(coming_mlp, hn2, "gather_weights_wait_0")))
        act, act2 = _mm(hn2, wts["mlp_w_up"], "nn", BF16, "mm_mlp_up", epi="relu2", b_blk="cols")
        xc = _mm(act2, wts["mlp_w_down"], "nn", F32, "mm_mlp_down", res=x1, b_blk="rows")
        s.update(hn=hn, w_in=w_in, proj=proj, o=o, lse=lse, qoff=qoff, w_mem=w_mem, w_o_mix=w_o_mix,
                 w_o_cross=w_o_cross, kvmem=kvmem, cross=cross, x1=x1, hn2=hn2, act=act, act2=act2)
        saved.append(s)
        if i + 1 < DEPTH:
            got = _exchange_wait(coming, xc, "gather_weights_wait_%d" % (i + 1))
            weights.append(dict(zip([n for n, _ in layer_names(i + 1)], got)))

    dx, dx_b, dg_final, loss_part = _loss_head(xc, final_norm_g, tgt)
    loss = lax.psum(loss_part[0, 0], MESH_AXES)

    gains = {n: [None] * DEPTH for n in ("attn_norm_g", "mlp_norm_g")}
    for n in ("mla_q_norm_g", "mla_kv_norm_g", "swa_sinks"):
        gains[n] = [None] * 2
    leaving = {}
    token = None
    dmem_n = None
    for i in reversed(range(DEPTH)):
        j = i // 2
        s = saved[i]
        wts = weights[i]
        out = {}
        du = _mm(dx_b, wts["mlp_w_down"], "nt", BF16, "mm_mlp_down_dx", aux=s["act"], epi="mul2aux", b_blk="rows",
                 after=token)
        out["mlp_w_down"] = _mm(s["act2"], dx_b, "tn", BF16, "mm_mlp_down_dw", o_blk="rows")
        dhn2 = _mm(du, wts["mlp_w_up"].transpose(0, 2, 1), "nn", F32, "mm_mlp_up_dx", b_blk="rows")
        out["mlp_w_up"] = _mm(s["hn2"], du, "tn", BF16, "mm_mlp_up_dw", o_blk="cols")
        dx1, dx1_b, dg = _rmsnorm_bwd(s["x1"], 0, D_MODEL, mlp_norm_g[i], dhn2, dx, F32, "rmsnorm_bwd", also_bf16=True)
        gains["mlp_norm_g"][i] = dg[0]

        do = _mm(dx1_b, s["w_o_mix"], "nt", BF16, "mm_o_mix_dx")
        dcross = _mm(dx1_b, s["w_o_cross"], "nt", BF16, "mm_o_cross_dx")
        dw_o = jnp.concatenate([_mm(s["o"], dx1_b, "tn", F32, "mm_o_mix_dw"),
                                _mm(s["cross"], dx1_b, "tn", F32, "mm_o_cross_dw")], axis=0)
        out["w_o"] = _split(_unpad_slots(dw_o, 0), 1)
        dqc, dkm, dvm = _cross_attn_bwd(s["proj"], s["qoff"], s["kvmem"], dcross)
        dkvmem = jnp.concatenate([dkm, dvm], axis=1).astype(BF16)
        out["w_mem_kv"] = _split(_unpad_slots(_mm(mem_n, dkvmem, "tn", F32, "mm_mem_kv_dw"), 1), 1)
        dmem_n = _mm(dkvmem, s["w_mem"], "nt", F32, "mm_mem_kv_dx" if dmem_n is None else "mm_mem_kv_dx_acc",
                     res=dmem_n)
        leaving[(i, "main")], token = _exchange_start([out[n] for n, _ in layer_names(i)[-4:]], True,
                                                      "exchange_grads_main_start_%d" % i)

        if i % 2 == 0:
            dq, dk, dv = _mla_attn_bwd(s["q"], s["k"], s["v"], s["o"], do, s["lse"], token)
            dqraw, dkv, dkr = _mla_rope_bwd(dq, dk, dv, tabs, consts)
            dcqn = _mm(dqraw, s["w_uq"], "nt", F32, "mm_mla_uq_dx")
            out["mla_w_uq"] = _split(_unpad_slots(_mm(s["cqn"], dqraw, "tn", F32, "mm_mla_uq_dw"), 1, MLA_QK), 2)
            dckvn = _mm(dkv, s["w_kv"], "nt", F32, "mm_mla_ukv_dx")
            out["mla_w_ukv"] = _split(_mla_ukv_unpad(_mm(s["ckvn"], dkv, "tn", F32, "mm_mla_ukv_dw")), 2)
            dcq, dg = _rmsnorm_bwd(s["proj"], 0, MLA_Q_RANK, mla_q_norm_g[j], dcqn, None, BF16, "rmsnorm_bwd_q")
            gains["mla_q_norm_g"][j] = dg[0]
            dckv, dg = _rmsnorm_bwd(s["proj"], 2, MLA_KV_RANK, mla_kv_norm_g[j], dckvn, None, BF16, "rmsnorm_bwd_kv")
            gains["mla_kv_norm_g"][j] = dg[0]
            dproj = jnp.concatenate([dcq, dkr.astype(BF16), dckv, dqc.astype(BF16)], axis=1)
            dhn = _mm(dproj, s["w_in"], "nt", F32, "mm_mla_in_dx")
            out["mla_w_in"] = _split(_mla_in_unpad(_mm(s["hn"], dproj, "tn", F32, "mm_mla_in_dw")), 1)
        else:
            dq, dk, dv, dsink = _swa_attn_bwd(s["proj"], s["o"], do, s["lse"], pos_col, pos_row, slopes, swa_sinks[j],
                                              token)
            gains["swa_sinks"][j] = dsink[::8, 0]
            dproj = jnp.concatenate([dq, dk, dv, dqc], axis=1).astype(BF16)
            dhn = _mm(dproj, s["w_in"], "nt", F32, "mm_swa_in_dx")
            out["swa_w_in"] = _split(_unpad_slots(_mm(s["hn"], dproj, "tn", F32, "mm_swa_in_dw"), 1), 2)
        dx, dx_b, dg = _rmsnorm_bwd(s["x_in"], 0, D_MODEL, attn_norm_g[i], dhn, dx1, F32, "rmsnorm_bwd", also_bf16=True)
        gains["attn_norm_g"][i] = dg[0]

        leaving[(i, "mixer")], token = _exchange_start([out[n] for n, _ in layer_names(i)[:-4]], True,
                                                       "exchange_grads_mixer_start_%d" % i)

    _, dg_mem = _rmsnorm_bwd(mem0, 0, D_MODEL, mem_norm_g, dmem_n, None, BF16, "rmsnorm_bwd_mem")
    gains = {n: jnp.stack(g) for n, g in gains.items()}
    gains["mem_norm_g"] = dg_mem[0]
    gains["final_norm_g"] = dg_final[0]

    result = {}

    def adamw_of(names, received):
        for n in names:
            parts = [received[(n, l)] for l in range(given[n].shape[0])]
            for kind, r in enumerate(_adamw(parts, given[n], given["m_" + n], given["v_" + n], "adamw_" + n)):
                result[(kind, n)] = r

    received = {}
    for i in reversed(range(DEPTH)):
        got = _exchange_wait(leaving[(i, "main")], dx, "exchange_grads_main_wait_%d" % i)
        received.update(zip(layer_names(i)[-4:], got))
    adamw_of(("mlp_w_up", "mlp_w_down", "w_o", "w_mem_kv"), received)
    for i in reversed(range(DEPTH)):
        got = _exchange_wait(leaving[(i, "mixer")], result[(0, "w_mem_kv")], "exchange_grads_mixer_wait_%d" % i)
        received.update(zip(layer_names(i)[:-4], got))
    adamw_of(("mla_w_in", "mla_w_uq", "mla_w_ukv", "swa_w_in"), received)

    rep_shapes = [given[n].shape for n in REPLICATED]
    rep_parts = _all_gather([_pack([gains[n] for n in REPLICATED], SLOT, 8, F32)], "gather_gain_grads")[0]
    rep_packed = [_pack([given[p + n] for n in REPLICATED], SLOT, 8, F32)[None] for p in ("", "m_", "v_")]
    for kind, r in enumerate(_adamw([rep_parts], *rep_packed, "adamw_gains")):
        for n, part in zip(REPLICATED, _unpack(r[0], rep_shapes)):
            result[(kind, n)] = part

    outs = [loss, dx.reshape(1, seq, D_MODEL)]
    for kind in range(4):
        outs += [result[(kind, n)] for n in WEIGHT_ORDER]
    return tuple(outs)
```

```python
import functools

import jax
import jax.numpy as jnp
from jax import lax
from jax.experimental import pallas as pl
from jax.experimental.pallas import tpu as pltpu

F32 = jnp.float32
BF16 = jnp.bfloat16

D_MODEL = 1024
D_FF = 4096
N_MEM = 256
DEPTH = 4
SLOT = 128
HEAD_DIM = 64
MLA_HEADS = 12
MLA_QK = 96
MLA_Q_RANK = 384
MLA_KV_RANK = 256
SWA_HEADS = 12
SWA_KV_HEADS = 4
SWA_GROUP = 3
MEM_HEADS = 4
WINDOW = 128
EPS = 1e-6
NEG = -1e30
ROPE_THETA = 10000.0
N_DEV = 8

ADAM_LR = 0.001
ADAM_B1 = 0.9
ADAM_B2 = 0.999
ADAM_EPS = 1e-08
ADAM_WD = 0.01
ADAM_STEP = 10

TM = 512
TQ_MLA = 1024
MLA_PACK = 2
SWA_PACK = 2
TQ_CROSS = 2048
MM_VMEM_BUDGET = 38 * 1024 * 1024
ADAM_ROWS = 128
VMEM_LIMIT = 56 * 1024 * 1024

MESH_AXES = ("x", "y", "c")

MLA_PAD_IN = 384 + SLOT + 256 + MEM_HEADS * SLOT
MLA_QOFF = (384 + SLOT + 256) // SLOT
SWA_PAD_IN = (SWA_HEADS + 2 * SWA_KV_HEADS + MEM_HEADS) * SLOT
SWA_QOFF = SWA_HEADS + 2 * SWA_KV_HEADS

SHARDED = (
    ("mla_w_in", 1), ("mla_w_uq", 2), ("mla_w_ukv", 2), ("swa_w_in", 2),
    ("w_mem_kv", 1), ("w_o", 1), ("mlp_w_up", 2), ("mlp_w_down", 1),
)
REPLICATED = ("attn_norm_g", "mlp_norm_g", "mem_norm_g", "final_norm_g",
              "mla_q_norm_g", "mla_kv_norm_g", "swa_sinks")
WEIGHT_ORDER = ("attn_norm_g", "mlp_norm_g", "mem_norm_g", "final_norm_g", "mla_w_in",
                "mla_q_norm_g", "mla_kv_norm_g", "mla_w_uq", "mla_w_ukv", "swa_w_in",
                "swa_sinks", "w_mem_kv", "w_o", "mlp_w_up", "mlp_w_down")


def _cparams():
    return pltpu.CompilerParams(vmem_limit_bytes=VMEM_LIMIT)


_DIMS = {"nn": (((1,), (0,)), ((), ())), "nt": (((1,), (1,)), ((), ())), "tn": (((0,), (0,)), ((), ()))}


def _mm_tiles(m, n, k, a_bytes, b_bytes, o_bytes, extra_bytes, tm_fixed, tn_fixed, lhs_t):
    best = None
    for tm in ([tm_fixed] if tm_fixed else [t for t in (4096, 2048, 1024, 512, 256, 128) if m % t == 0] or [m]):
        for tn in ([tn_fixed] if tn_fixed else [t for t in range(1024, 0, -SLOT) if n % t == 0] or [n]):
            need = 2 * (tm * k * a_bytes + k * tn * b_bytes + tm * tn * (o_bytes + extra_bytes))
            need += tm * tn * 4 + (tm * k * 2 if lhs_t else 0)
            if need <= MM_VMEM_BUDGET and (best is None or tm * tn > best[0] * best[1]):
                best = (tm, tn)
    assert best is not None, (m, n, k)
    return best


def _mm(a, b, mode, out_dtype, name, res=None, aux=None, epi=None, b_blk=None, o_blk=None, after=None):
    if b_blk is not None:
        nb, br, bc = b.shape
        b_shape = (nb * br, bc) if b_blk == "rows" else (br, nb * bc)
    else:
        b_shape = b.shape
    if mode == "nn":
        (m, k), (k2, n) = a.shape, b_shape
    elif mode == "nt":
        (m, k), (n, k2) = a.shape, b_shape
    else:
        (k, m), (k2, n) = a.shape, b_shape
    assert k == k2, (a.shape, b_shape, mode)
    k_blocked = b_blk is not None and (b_blk == "rows") == (mode != "nt")
    assert not (k_blocked and mode == "nt")
    tn_fixed = None
    if b_blk is not None and not k_blocked:
        tn_fixed = br if b_blk == "rows" else bc
    if o_blk in ("cols", "cols_t"):
        tn_fixed = n // N_DEV
    tm_fixed = m // N_DEV if o_blk == "rows" else None
    has_res, has_aux = res is not None, aux is not None
    assert o_blk is None or not (has_res or has_aux)
    n_out = 2 if epi == "relu2" else 1
    tm, tn = _mm_tiles(m, n, k, a.dtype.itemsize, b.dtype.itemsize, n_out * jnp.dtype(out_dtype).itemsize,
                       (4 if has_res else 0) + (aux.dtype.itemsize if has_aux else 0), tm_fixed, tn_fixed,
                       mode == "tn")
    keep_lhs_t = mode == "tn" and n // tn > 1
    dims = _DIMS["nn" if keep_lhs_t else mode]
    if mode == "tn":
        a_spec = pl.BlockSpec((k, tm), lambda i, j: (0, i))
    else:
        a_spec = pl.BlockSpec((tm, k), lambda i, j: (i, 0))
    if b_blk is None:
        if mode == "nt":
            b_spec = pl.BlockSpec((tn, k), lambda i, j: (j, 0))
        else:
            b_spec = pl.BlockSpec((k, tn), lambda i, j: (0, j))
    elif k_blocked:
        b_spec = pl.BlockSpec((N_DEV, br, tn), lambda i, j: (0, 0, j))
    elif mode == "nt":
        b_spec = pl.BlockSpec((None, tn, k), lambda i, j: (j, 0, 0))
    else:
        b_spec = pl.BlockSpec((None, k, tn), lambda i, j: (j, 0, 0))
    if o_blk is None:
        o_spec = pl.BlockSpec((tm, tn), lambda i, j: (i, j))
        o_shape = (m, n)
    elif o_blk == "rows":
        o_spec = pl.BlockSpec((None, tm, tn), lambda i, j: (i, 0, j))
        o_shape = (N_DEV, tm, n)
    elif o_blk == "cols":
        o_spec = pl.BlockSpec((None, tm, tn), lambda i, j: (j, i, 0))
        o_shape = (N_DEV, m, tn)
    else:
        o_spec = pl.BlockSpec((None, tn, tm), lambda i, j: (j, 0, i))
        o_shape = (N_DEV, tn, m)

    def body(*refs):
        a_ref, b_ref = refs[0], refs[1]
        pos = 2
        res_ref = aux_ref = None
        if has_res:
            res_ref = refs[pos]
            pos += 1
        if has_aux:
            aux_ref = refs[pos]
            pos += 1
        if after is not None:
            pos += 1
        outs = refs[pos:-1] if keep_lhs_t else refs[pos:]
        bv = b_ref[...].reshape(k, tn) if k_blocked else b_ref[...]
        if keep_lhs_t:
            lhs_t = refs[-1]

            @pl.when(pl.program_id(1) == 0)
            def _():
                lhs_t[...] = a_ref[...].astype(BF16).T

            av = lhs_t[...]
        else:
            av = a_ref[...].astype(BF16)
        r = lax.dot_general(av, bv.astype(BF16), dims, preferred_element_type=F32)
        if o_blk == "cols_t":
            r = r.T
        if epi == "relu2":
            r = jnp.maximum(r, 0.0)
            outs[0][...] = r.astype(outs[0].dtype)
            outs[1][...] = (r * r).astype(outs[1].dtype)
        else:
            if epi == "mul2aux":
                r = r * (2.0 * aux_ref[...].astype(F32))
            if has_res:
                r = r + res_ref[...]
            outs[0][...] = r.astype(outs[0].dtype)

    in_specs = [a_spec, b_spec]
    args = [a, b]
    if has_res:
        in_specs.append(o_spec)
        args.append(res)
    if has_aux:
        in_specs.append(o_spec)
        args.append(aux)
    if after is not None:
        in_specs.append(pl.BlockSpec(memory_space=pl.ANY))
        args.append(after)
    out_shape = [jax.ShapeDtypeStruct(o_shape, out_dtype)] * n_out
    out = pl.pallas_call(
        body, name=name, grid=(m // tm, n // tn),
        in_specs=in_specs, out_specs=[o_spec] * n_out, out_shape=out_shape,
        scratch_shapes=[pltpu.VMEM((tm, k), BF16)] if keep_lhs_t else [], compiler_params=_cparams(),
    )(*args)
    return out if n_out == 2 else out[0]


def _rmsnorm_fwd(xarr, colblk, width, g, name, after=None):
    rows = xarr.shape[0]
    tm = min(TM, rows)

    def body(x_ref, g_ref, *rest):
        y_ref = rest[-1]
        x = x_ref[...].astype(F32)
        r = lax.rsqrt(jnp.mean(x * x, axis=1, keepdims=True) + EPS)
        y_ref[...] = (x * r * g_ref[...]).astype(y_ref.dtype)

    in_specs = [pl.BlockSpec((tm, width), lambda i: (i, colblk)), pl.BlockSpec((1, width), lambda i: (0, 0))]
    args = [xarr, g.reshape(1, width)]
    if after is not None:
        in_specs.append(pl.BlockSpec(memory_space=pl.ANY))
        args.append(after)
    return pl.pallas_call(
        body, name=name, grid=(rows // tm,), in_specs=in_specs,
        out_specs=pl.BlockSpec((tm, width), lambda i: (i, 0)),
        out_shape=jax.ShapeDtypeStruct((rows, width), BF16), compiler_params=_cparams(),
    )(*args)


def _rmsnorm_bwd(xarr, colblk, width, g, dy, dres, out_dtype, name, also_bf16=False):
    rows = xarr.shape[0]
    tm = min(TM, rows)
    has_res = dres is not None

    def body(*refs):
        x_ref, g_ref, dy_ref = refs[0], refs[1], refs[2]
        dres_ref = refs[3] if has_res else None
        dx_ref, dg_ref = refs[-3 if also_bf16 else -2], refs[-1]
        x = x_ref[...].astype(F32)
        dyv = dy_ref[...].astype(F32)
        r = lax.rsqrt(jnp.mean(x * x, axis=1, keepdims=True) + EPS)
        xh = x * r
        dxh = dyv * g_ref[...]
        dx = r * (dxh - xh * jnp.mean(dxh * xh, axis=1, keepdims=True))
        if has_res:
            dx = dx + dres_ref[...]
        dx_ref[...] = dx.astype(dx_ref.dtype)
        if also_bf16:
            refs[-2][...] = dx.astype(BF16)

        @pl.when(pl.program_id(0) == 0)
        def _():
            dg_ref[...] = jnp.zeros_like(dg_ref)

        dg_ref[...] += jnp.sum(dyv * xh, axis=0, keepdims=True)

    row_spec = pl.BlockSpec((tm, width), lambda i: (i, 0))
    vec_spec = pl.BlockSpec((1, width), lambda i: (0, 0))
    in_specs = [pl.BlockSpec((tm, width), lambda i: (i, colblk)), vec_spec, row_spec]
    args = [xarr, g.reshape(1, width), dy]
    if has_res:
        in_specs.append(row_spec)
        args.append(dres)
    dx_shapes = [jax.ShapeDtypeStruct((rows, width), out_dtype)]
    if also_bf16:
        dx_shapes.append(jax.ShapeDtypeStruct((rows, width), BF16))
    return pl.pallas_call(
        body, name=name, grid=(rows // tm,), in_specs=in_specs, out_specs=[row_spec] * len(dx_shapes) + [vec_spec],
        out_shape=dx_shapes + [jax.ShapeDtypeStruct((1, width), F32)], compiler_params=_cparams(),
    )(*args)


def _loss_head(x, g, tgt):
    rows, width = x.shape
    tm = min(TM, rows)

    def body(x_ref, g_ref, t_ref, dx_ref, dxb_ref, dg_ref, loss_ref):
        xv = x_ref[...]
        gv = g_ref[...]
        r = lax.rsqrt(jnp.mean(xv * xv, axis=1, keepdims=True) + EPS)
        xh = xv * r
        err = xh * gv - t_ref[...]
        part = 0.5 * jnp.sum(jnp.mean(err * err, axis=1, keepdims=True), axis=0, keepdims=True)
        dyv = err * (1.0 / width)
        dxh = dyv * gv
        dxv = r * (dxh - xh * jnp.mean(dxh * xh, axis=1, keepdims=True))
        dx_ref[...] = dxv
        dxb_ref[...] = dxv.astype(BF16)

        @pl.when(pl.program_id(0) == 0)
        def _():
            dg_ref[...] = jnp.zeros_like(dg_ref)
            loss_ref[...] = jnp.zeros_like(loss_ref)

        dg_ref[...] += jnp.sum(dyv * xh, axis=0, keepdims=True)
        loss_ref[...] += jnp.broadcast_to(part, loss_ref.shape)

    row_spec = pl.BlockSpec((tm, width), lambda i: (i, 0))
    vec_spec = pl.BlockSpec((1, width), lambda i: (0, 0))
    return pl.pallas_call(
        body, name="loss_head", grid=(rows // tm,), in_specs=[row_spec, vec_spec, row_spec],
        out_specs=[row_spec, row_spec, vec_spec, pl.BlockSpec((1, SLOT), lambda i: (0, 0))],
        out_shape=[jax.ShapeDtypeStruct((rows, width), F32), jax.ShapeDtypeStruct((rows, width), BF16),
                   jax.ShapeDtypeStruct((1, width), F32), jax.ShapeDtypeStruct((1, SLOT), F32)],
        compiler_params=_cparams(),
    )(x, g.reshape(1, width), tgt)


def _lane_consts():
    half = 16
    inv = ROPE_THETA ** (-(jnp.arange(half, dtype=F32) * 2.0) / 32)
    lane = jnp.arange(SLOT)
    first = (lane >= 64) & (lane < 80)
    second = (lane >= 80) & (lane < 96)
    inv_lane = jnp.where(first | second, inv[(lane - 64) % half], 0.0)
    rows = [inv_lane, (lane < 64).astype(F32), first.astype(F32), second.astype(F32)]
    rows += [jnp.zeros((SLOT,), F32)] * 4
    return jnp.stack(rows).astype(F32)


def _rope_tables(pos_col, consts):
    rows = pos_col.shape[0]
    tm = min(TM, rows)

    def body(p_ref, k_ref, c_ref, s1_ref, s2_ref):
        ang = p_ref[...] * k_ref[0:1, :]
        cos, sin = jnp.cos(ang), jnp.sin(ang)
        first, second = k_ref[2:3, :], k_ref[3:4, :]
        c_ref[...] = k_ref[1:2, :] + (first + second) * cos
        s1_ref[...] = -first * sin
        s2_ref[...] = second * sin

    spec = pl.BlockSpec((tm, SLOT), lambda i: (i, 0))
    shp = jax.ShapeDtypeStruct((rows, SLOT), F32)
    return pl.pallas_call(
        body, name="rope_tables", grid=(rows // tm,),
        in_specs=[pl.BlockSpec((tm, 1), lambda i: (i, 0)), pl.BlockSpec((8, SLOT), lambda i: (0, 0))],
        out_specs=[spec, spec, spec], out_shape=[shp, shp, shp], compiler_params=_cparams(),
    )(pos_col, consts)


def _rot(xv, c, s1, s2):
    return xv * c + pltpu.roll(xv, SLOT - 16, 1) * s1 + pltpu.roll(xv, 16, 1) * s2


def _rot_t(dy, c, s1, s2):
    return dy * c + pltpu.roll(dy * s1, 16, 1) + pltpu.roll(dy * s2, SLOT - 16, 1)


def _mla_rope_fwd(qraw, kvraw, proj, tabs):
    rows = qraw.shape[0]
    tm = min(256, rows)
    hw = MLA_HEADS * SLOT

    def body(q_ref, kv_ref, kr_ref, c_ref, s1_ref, s2_ref, qo, ko, vo):
        c, s1, s2 = c_ref[...], s1_ref[...], s2_ref[...]
        kr = _rot(kr_ref[...], c, s1, s2)
        for h in range(MLA_HEADS):
            sl = slice(h * SLOT, (h + 1) * SLOT)
            qo[:, sl] = _rot(q_ref[:, sl], c, s1, s2).astype(BF16)
            ko[:, sl] = (kv_ref[:, sl] + kr).astype(BF16)
            vo[:, sl] = kv_ref[:, hw + h * SLOT:hw + (h + 1) * SLOT].astype(BF16)

    tab = pl.BlockSpec((tm, SLOT), lambda i: (i, 0))
    wide = pl.BlockSpec((tm, hw), lambda i: (i, 0))
    shp = jax.ShapeDtypeStruct((rows, hw), BF16)
    return pl.pallas_call(
        body, name="mla_rope_fwd", grid=(rows // tm,),
        in_specs=[wide, pl.BlockSpec((tm, 2 * hw), lambda i: (i, 0)), pl.BlockSpec((tm, SLOT), lambda i: (i, 3)),
                  tab, tab, tab],
        out_specs=[wide, wide, wide], out_shape=[shp, shp, shp], compiler_params=_cparams(),
    )(qraw, kvraw, proj, *tabs)


def _mla_rope_bwd(dq, dk, dv, tabs, consts):
    rows = dq.shape[0]
    tm = min(256, rows)
    hw = MLA_HEADS * SLOT

    def body(dq_ref, dk_ref, dv_ref, c_ref, s1_ref, s2_ref, k_ref, dqo, dkvo, dkro):
        c, s1, s2 = c_ref[...], s1_ref[...], s2_ref[...]
        ksum = jnp.zeros((tm, SLOT), F32)
        for h in range(MLA_HEADS):
            sl = slice(h * SLOT, (h + 1) * SLOT)
            dqo[:, sl] = _rot_t(dq_ref[:, sl], c, s1, s2).astype(BF16)
            dkh = dk_ref[:, sl]
            ksum = ksum + dkh
            dkvo[:, sl] = dkh.astype(BF16)
            dkvo[:, hw + h * SLOT:hw + (h + 1) * SLOT] = dv_ref[:, sl].astype(BF16)
        dkro[...] = _rot_t(ksum, c, s1, s2) * (k_ref[2:3, :] + k_ref[3:4, :])

    tab = pl.BlockSpec((tm, SLOT), lambda i: (i, 0))
    wide = pl.BlockSpec((tm, hw), lambda i: (i, 0))
    return pl.pallas_call(
        body, name="mla_rope_bwd", grid=(rows // tm,),
        in_specs=[wide, wide, wide, tab, tab, tab, pl.BlockSpec((8, SLOT), lambda i: (0, 0))],
        out_specs=[wide, pl.BlockSpec((tm, 2 * hw), lambda i: (i, 0)), tab],
        out_shape=[jax.ShapeDtypeStruct((rows, hw), BF16), jax.ShapeDtypeStruct((rows, 2 * hw), BF16),
                   jax.ShapeDtypeStruct((rows, SLOT), F32)],
        compiler_params=_cparams(),
    )(dq, dk, dv, *tabs, consts)


def _nt(a, b):
    return lax.dot_general(a, b, _DIMS["nt"], preferred_element_type=F32)


def _tn(a, b):
    return lax.dot_general(a, b, _DIMS["tn"], preferred_element_type=F32)


def _nn(a, b):
    return lax.dot_general(a, b, _DIMS["nn"], preferred_element_type=F32)


def _causal(t):
    return lax.broadcasted_iota(jnp.int32, (t, t), 1) <= lax.broadcasted_iota(jnp.int32, (t, t), 0)


def _mla_attn_fwd(q, k, v):
    rows = q.shape[0]
    t = min(TQ_MLA, rows)
    nt = rows // t
    scale = MLA_QK ** -0.5
    wide = MLA_PACK * SLOT

    def body(q_ref, k_ref, v_ref, o_ref, lse_ref, m_sc, l_sc, acc_sc):
        i, j = pl.program_id(1), pl.program_id(2)

        @pl.when(j == 0)
        def _():
            m_sc[...] = jnp.full_like(m_sc, NEG)
            l_sc[...] = jnp.zeros_like(l_sc)
            acc_sc[...] = jnp.zeros_like(acc_sc)

        def step(diagonal):
            for hh in range(MLA_PACK):
                sl = slice(hh * SLOT, (hh + 1) * SLOT)
                s = _nt(q_ref[:, sl], k_ref[:, sl]) * scale
                if diagonal:
                    s = jnp.where(_causal(t), s, NEG)
                m_prev = m_sc[hh]
                m_new = jnp.maximum(m_prev, jnp.max(s, axis=1, keepdims=True))
                p = jnp.exp(s - m_new)
                alpha = jnp.exp(m_prev - m_new)
                l_new = alpha * l_sc[hh] + jnp.sum(p, axis=1, keepdims=True)
                acc = alpha * acc_sc[:, sl] + _nn(p.astype(BF16), v_ref[:, sl])
                if diagonal:
                    o_ref[:, sl] = (acc / l_new).astype(o_ref.dtype)
                    lse_ref[:, sl] = jnp.broadcast_to(m_new + jnp.log(l_new), (t, SLOT))
                else:
                    m_sc[hh] = m_new
                    l_sc[hh] = l_new
                    acc_sc[:, sl] = acc

        @pl.when(j < i)
        def _():
            step(False)

        @pl.when(j == i)
        def _():
            step(True)

    q_spec = pl.BlockSpec((t, wide), lambda h, i, j: (i, h))
    kv_spec = pl.BlockSpec((t, wide), lambda h, i, j: (jnp.minimum(j, i), h))
    return pl.pallas_call(
        body, name="mla_attn_fwd", grid=(MLA_HEADS // MLA_PACK, nt, nt),
        in_specs=[q_spec, kv_spec, kv_spec], out_specs=[q_spec, q_spec],
        out_shape=[jax.ShapeDtypeStruct(q.shape, BF16), jax.ShapeDtypeStruct(q.shape, F32)],
        scratch_shapes=[pltpu.VMEM((MLA_PACK, t, 1), F32), pltpu.VMEM((MLA_PACK, t, 1), F32),
                        pltpu.VMEM((t, wide), F32)],
        compiler_params=_cparams(),
    )(q, k, v)


def _mla_attn_bwd(q, k, v, o, do, lse, after):
    rows = q.shape[0]
    t = min(TQ_MLA, rows)
    nt = rows // t
    scale = MLA_QK ** -0.5
    wide = MLA_PACK * SLOT

    def body(q_ref, k_ref, v_ref, o_ref, do_ref, lse_ref, after_ref, dq_ref, dk_ref, dv_ref, dk_sc, dv_sc):
        j, i = pl.program_id(1), pl.program_id(2)

        @pl.when((j == 0) & (i == 0))
        def _():
            dq_ref[...] = jnp.zeros_like(dq_ref)

        @pl.when(i == 0)
        def _():
            dk_sc[...] = jnp.zeros_like(dk_sc)
            dv_sc[...] = jnp.zeros_like(dv_sc)

        def step(diagonal):
            r0 = pl.multiple_of(i * t, t)
            for hh in range(MLA_PACK):
                sl = slice(hh * SLOT, (hh + 1) * SLOT)
                qv, kv, dov = q_ref[:, sl], k_ref[:, sl], do_ref[:, sl]
                s = _nt(qv, kv) * scale
                if diagonal:
                    s = jnp.where(_causal(t), s, NEG)
                p = jnp.exp(s - lse_ref[:, hh * SLOT:hh * SLOT + 1])
                delta = jnp.sum(dov.astype(F32) * o_ref[:, sl].astype(F32), axis=1, keepdims=True)
                dp = _nt(dov, v_ref[:, sl])
                ds = (p * (dp - delta) * scale).astype(BF16)
                dv_sc[:, sl] += _tn(p.astype(BF16), dov)
                dk_sc[:, sl] += _tn(ds, qv)
                dq_ref[pl.ds(r0, t), sl] += _nn(ds, kv)

        @pl.when(i > j)
        def _():
            step(False)

        @pl.when(i == j)
        def _():
            step(True)

        @pl.when(i == nt - 1)
        def _():
            dk_ref[...] = dk_sc[...]
            dv_ref[...] = dv_sc[...]

    q_spec = pl.BlockSpec((t, wide), lambda h, j, i: (jnp.maximum(i, j), h))
    kv_spec = pl.BlockSpec((t, wide), lambda h, j, i: (j, h))
    head_spec = pl.BlockSpec((rows, wide), lambda h, j, i: (0, h))
    shp = jax.ShapeDtypeStruct(q.shape, F32)
    return pl.pallas_call(
        body, name="mla_attn_bwd", grid=(MLA_HEADS // MLA_PACK, nt, nt),
        in_specs=[q_spec, kv_spec, kv_spec, q_spec, q_spec, q_spec, pl.BlockSpec(memory_space=pl.ANY)],
        out_specs=[head_spec, kv_spec, kv_spec], out_shape=[shp, shp, shp],
        scratch_shapes=[pltpu.VMEM((t, wide), F32), pltpu.VMEM((t, wide), F32)],
        compiler_params=_cparams(),
    )(q, k, v, o, do, lse, after)


def _swa_specs(t):
    def prev(i):
        return jnp.maximum(i - 1, 0)
    kw = SWA_PACK * SLOT
    k0, v0 = SWA_HEADS // SWA_PACK, (SWA_HEADS + SWA_KV_HEADS) // SWA_PACK
    q3 = pl.BlockSpec((t, SWA_PACK * SWA_GROUP * SLOT), lambda h, i: (i, h))
    kp = pl.BlockSpec((t, kw), lambda h, i: (prev(i), k0 + h))
    kc = pl.BlockSpec((t, kw), lambda h, i: (i, k0 + h))
    vp = pl.BlockSpec((t, kw), lambda h, i: (prev(i), v0 + h))
    vc = pl.BlockSpec((t, kw), lambda h, i: (i, v0 + h))
    pcol = pl.BlockSpec((t, 1), lambda h, i: (i, 0))
    prow_p = pl.BlockSpec((1, t), lambda h, i: (0, prev(i)))
    prow_c = pl.BlockSpec((1, t), lambda h, i: (0, i))
    return [q3, kp, kc, vp, vc, pcol, prow_p, prow_c]


def _stack(ref, first):
    return jnp.concatenate([ref[:, (first + g) * SLOT:(first + g + 1) * SLOT] for g in range(SWA_GROUP)], axis=0)


def _swa_logits(q3, kp, kc, pq, pkp, pkc, slope_ref, kvh, i, t):
    r = lax.broadcasted_iota(jnp.int32, (t, t), 0)
    c = lax.broadcasted_iota(jnp.int32, (t, t), 1)
    ok_c = c <= r
    ok_p = (c - r) > jnp.where(i > 0, 0, t)
    dist_p, dist_c = pq - pkp, pq - pkc
    s_p3 = _nt(q3, kp) * (HEAD_DIM ** -0.5)
    s_c3 = _nt(q3, kc) * (HEAD_DIM ** -0.5)
    out = []
    for g in range(SWA_GROUP):
        slope = slope_ref[kvh * SWA_GROUP + g]
        rows = slice(g * t, (g + 1) * t)
        out.append((jnp.where(ok_p, s_p3[rows] - slope * dist_p, NEG),
                    jnp.where(ok_c, s_c3[rows] - slope * dist_c, NEG)))
    return out


def _swa_attn_fwd(proj, pos_col, pos_row, slopes, sinks):
    rows = proj.shape[0]
    t = WINDOW
    hw = SWA_HEADS * SLOT

    def body(slope_ref, sink_ref, q_ref, kp_ref, kc_ref, vp_ref, vc_ref, pq_ref, pkp_ref, pkc_ref, o_ref, lse_ref):
        i = pl.program_id(1)
        for kv in range(SWA_PACK):
            kvh = pl.program_id(0) * SWA_PACK + kv
            ksl = slice(kv * SLOT, (kv + 1) * SLOT)
            logits = _swa_logits(_stack(q_ref, kv * SWA_GROUP), kp_ref[:, ksl], kc_ref[:, ksl], pq_ref[...],
                                 pkp_ref[...], pkc_ref[...], slope_ref, kvh, i, t)
            e_p, e_c, norm = [], [], []
            for g, (s_p, s_c) in enumerate(logits):
                sl = slice((kv * SWA_GROUP + g) * SLOT, (kv * SWA_GROUP + g + 1) * SLOT)
                sink = sink_ref[kvh * SWA_GROUP + g]
                m = jnp.maximum(jnp.maximum(jnp.max(s_p, axis=1, keepdims=True),
                                            jnp.max(s_c, axis=1, keepdims=True)), sink)
                ep, ec = jnp.exp(s_p - m), jnp.exp(s_c - m)
                l = jnp.sum(ep, axis=1, keepdims=True) + jnp.sum(ec, axis=1, keepdims=True) + jnp.exp(sink - m)
                e_p.append(ep.astype(BF16))
                e_c.append(ec.astype(BF16))
                norm.append(l)
                lse_ref[:, sl] = jnp.broadcast_to(m + jnp.log(l), (t, SLOT))
            acc = (_nn(jnp.concatenate(e_p, axis=0), vp_ref[:, ksl])
                   + _nn(jnp.concatenate(e_c, axis=0), vc_ref[:, ksl]))
            for g in range(SWA_GROUP):
                sl = slice((kv * SWA_GROUP + g) * SLOT, (kv * SWA_GROUP + g + 1) * SLOT)
                o_ref[:, sl] = (acc[g * t:(g + 1) * t] / norm[g]).astype(o_ref.dtype)

    smem = pl.BlockSpec(memory_space=pltpu.SMEM)
    out_spec = pl.BlockSpec((t, SWA_PACK * SWA_GROUP * SLOT), lambda h, i: (i, h))
    return pl.pallas_call(
        body, name="swa_attn_fwd", grid=(SWA_KV_HEADS // SWA_PACK, rows // t),
        in_specs=[smem, smem] + _swa_specs(t), out_specs=[out_spec, out_spec],
        out_shape=[jax.ShapeDtypeStruct((rows, hw), BF16), jax.ShapeDtypeStruct((rows, hw), F32)],
        compiler_params=_cparams(),
    )(slopes, sinks, proj, proj, proj, proj, proj, pos_col, pos_row, pos_row)


def _swa_attn_bwd(proj, o, do, lse, pos_col, pos_row, slopes, sinks, after):
    rows = proj.shape[0]
    t = WINDOW
    hw = SWA_HEADS * SLOT
    scale = HEAD_DIM ** -0.5

    def body(slope_ref, sink_ref, q_ref, kp_ref, kc_ref, vp_ref, vc_ref, pq_ref, pkp_ref, pkc_ref,
             o_ref, do_ref, lse_ref, after_ref, dq_ref, dk_ref, dv_ref, dsink_ref):
        i = pl.program_id(1)

        @pl.when(i == 0)
        def _():
            dk_ref[...] = jnp.zeros_like(dk_ref)
            dv_ref[...] = jnp.zeros_like(dv_ref)
            dsink_ref[...] = jnp.zeros_like(dsink_ref)

        r_c = pl.multiple_of(i * t, t)
        r_p = pl.multiple_of(jnp.maximum(i - 1, 0) * t, t)
        for kv in range(SWA_PACK):
            kvh = pl.program_id(0) * SWA_PACK + kv
            ksl = slice(kv * SLOT, (kv + 1) * SLOT)
            q3, do3 = _stack(q_ref, kv * SWA_GROUP), _stack(do_ref, kv * SWA_GROUP)
            logits = _swa_logits(q3, kp_ref[:, ksl], kc_ref[:, ksl], pq_ref[...], pkp_ref[...], pkc_ref[...],
                                 slope_ref, kvh, i, t)
            dp_p3, dp_c3 = _nt(do3, vp_ref[:, ksl]), _nt(do3, vc_ref[:, ksl])
            p_p, p_c, ds_p, ds_c = [], [], [], []
            for g, (s_p, s_c) in enumerate(logits):
                head = kv * SWA_GROUP + g
                sl = slice(head * SLOT, (head + 1) * SLOT)
                rws = slice(g * t, (g + 1) * t)
                lse_g = lse_ref[:, head * SLOT:head * SLOT + 1]
                pp, pc = jnp.exp(s_p - lse_g), jnp.exp(s_c - lse_g)
                delta = jnp.sum(do_ref[:, sl].astype(F32) * o_ref[:, sl].astype(F32), axis=1, keepdims=True)
                p_p.append(pp.astype(BF16))
                p_c.append(pc.astype(BF16))
                ds_p.append((pp * (dp_p3[rws] - delta)).astype(BF16))
                ds_c.append((pc * (dp_c3[rws] - delta)).astype(BF16))
                sink = sink_ref[kvh * SWA_GROUP + g]
                dsink = -jnp.sum(jnp.exp(sink - lse_g) * delta, axis=0, keepdims=True)
                dsink_ref[head * 8:(head + 1) * 8, :] += jnp.broadcast_to(dsink, (8, SLOT))
            p_p3, p_c3 = jnp.concatenate(p_p, axis=0), jnp.concatenate(p_c, axis=0)
            ds_p3, ds_c3 = jnp.concatenate(ds_p, axis=0), jnp.concatenate(ds_c, axis=0)
            dq3 = (_nn(ds_p3, kp_ref[:, ksl]) + _nn(ds_c3, kc_ref[:, ksl])) * scale
            for g in range(SWA_GROUP):
                head = kv * SWA_GROUP + g
                dq_ref[:, head * SLOT:(head + 1) * SLOT] = dq3[g * t:(g + 1) * t]
            dk_ref[pl.ds(r_c, t), ksl] += _tn(ds_c3, q3) * scale
            dv_ref[pl.ds(r_c, t), ksl] += _tn(p_c3, do3)
            dk_ref[pl.ds(r_p, t), ksl] += _tn(ds_p3, q3) * scale
            dv_ref[pl.ds(r_p, t), ksl] += _tn(p_p3, do3)

    smem = pl.BlockSpec(memory_space=pltpu.SMEM)
    qlike = pl.BlockSpec((t, SWA_PACK * SWA_GROUP * SLOT), lambda h, i: (i, h))
    kv_out = pl.BlockSpec((rows, SWA_PACK * SLOT), lambda h, i: (0, h))
    return pl.pallas_call(
        body, name="swa_attn_bwd", grid=(SWA_KV_HEADS // SWA_PACK, rows // t),
        in_specs=[smem, smem] + _swa_specs(t) + [qlike, qlike, qlike, pl.BlockSpec(memory_space=pl.ANY)],
        out_specs=[qlike, kv_out, kv_out,
                   pl.BlockSpec((SWA_PACK * SWA_GROUP * 8, SLOT), lambda h, i: (h, 0))],
        out_shape=[jax.ShapeDtypeStruct((rows, hw), F32), jax.ShapeDtypeStruct((rows, SWA_KV_HEADS * SLOT), F32),
                   jax.ShapeDtypeStruct((rows, SWA_KV_HEADS * SLOT), F32),
                   jax.ShapeDtypeStruct((SWA_HEADS * 8, SLOT), F32)],
        compiler_params=_cparams(),
    )(slopes, sinks, proj, proj, proj, proj, proj, pos_col, pos_row, pos_row, o, do, lse, after)


def _cross_attn_fwd(proj, qoff, kvmem):
    rows = proj.shape[0]
    t = min(TQ_CROSS, rows)

    def body(q_ref, k_ref, v_ref, o_ref):
        s = _nt(q_ref[...].astype(BF16), k_ref[...]) * (HEAD_DIM ** -0.5)
        e = jnp.exp(s - jnp.max(s, axis=1, keepdims=True))
        p = e / jnp.sum(e, axis=1, keepdims=True)
        o_ref[...] = _nn(p.astype(BF16), v_ref[...]).astype(o_ref.dtype)

    return pl.pallas_call(
        body, name="cross_attn_fwd", grid=(rows // t, MEM_HEADS),
        in_specs=[pl.BlockSpec((t, SLOT), lambda i, h: (i, qoff + h)),
                  pl.BlockSpec((N_MEM, SLOT), lambda i, h: (0, h)),
                  pl.BlockSpec((N_MEM, SLOT), lambda i, h: (0, MEM_HEADS + h))],
        out_specs=pl.BlockSpec((t, SLOT), lambda i, h: (i, h)),
        out_shape=jax.ShapeDtypeStruct((rows, MEM_HEADS * SLOT), BF16), compiler_params=_cparams(),
    )(proj, kvmem, kvmem)


def _cross_attn_bwd(proj, qoff, kvmem, do):
    rows = proj.shape[0]
    t = min(TQ_CROSS, rows)
    scale = HEAD_DIM ** -0.5

    def body(q_ref, k_ref, v_ref, do_ref, dq_ref, dk_ref, dv_ref):
        @pl.when(pl.program_id(1) == 0)
        def _():
            dk_ref[...] = jnp.zeros_like(dk_ref)
            dv_ref[...] = jnp.zeros_like(dv_ref)

        qv, kv, dov = q_ref[...].astype(BF16), k_ref[...], do_ref[...]
        s = _nt(qv, kv) * scale
        e = jnp.exp(s - jnp.max(s, axis=1, keepdims=True))
        p = e / jnp.sum(e, axis=1, keepdims=True)
        dp = _nt(dov, v_ref[...])
        ds = (p * (dp - jnp.sum(p * dp, axis=1, keepdims=True))).astype(BF16)
        dq_ref[...] = _nn(ds, kv) * scale
        dk_ref[...] += _tn(ds, qv) * scale
        dv_ref[...] += _tn(p.astype(BF16), dov)

    mem_out = pl.BlockSpec((N_MEM, SLOT), lambda h, i: (0, h))
    return pl.pallas_call(
        body, name="cross_attn_bwd", grid=(MEM_HEADS, rows // t),
        in_specs=[pl.BlockSpec((t, SLOT), lambda h, i: (i, qoff + h)),
                  pl.BlockSpec((N_MEM, SLOT), lambda h, i: (0, h)),
                  pl.BlockSpec((N_MEM, SLOT), lambda h, i: (0, MEM_HEADS + h)),
                  pl.BlockSpec((t, SLOT), lambda h, i: (i, h))],
        out_specs=[pl.BlockSpec((t, SLOT), lambda h, i: (i, h)), mem_out, mem_out],
        out_shape=[jax.ShapeDtypeStruct((rows, MEM_HEADS * SLOT), F32),
                   jax.ShapeDtypeStruct((N_MEM, MEM_HEADS * SLOT), F32),
                   jax.ShapeDtypeStruct((N_MEM, MEM_HEADS * SLOT), F32)],
        compiler_params=_cparams(),
    )(proj, kvmem, kvmem, do)


def _place():
    return lax.axis_index("x"), lax.axis_index("y"), lax.axis_index("c")


def _flip(v, bit):
    return 1 - v if bit else v


def _all_gather(blocks, name):
    nb = len(blocks)

    def body(*refs):
        x_refs, out_refs = refs[:nb], refs[nb:2 * nb]
        send_sems, recv_sems, local_sems = refs[2 * nb:]
        x, y, c = _place()
        me, sibling = (x, y, c), (x, y, 1 - c)
        chips = [(1 - x, y), (x, 1 - y), (1 - x, 1 - y)]

        def copy(b, k, blk, to, from_input=False):
            slot = out_refs[b].at[4 * blk[0] + 2 * blk[1] + blk[2]]
            return pltpu.make_async_remote_copy(
                src_ref=x_refs[b] if from_input else slot, dst_ref=slot,
                send_sem=send_sems.at[b, k], recv_sem=recv_sems.at[b, k],
                device_id=to, device_id_type=pl.DeviceIdType.MESH)

        mine = [pltpu.make_async_copy(x_refs[b], out_refs[b].at[4 * x + 2 * y + c], local_sems.at[b])
                for b in range(nb)]
        for cp in mine:
            cp.start()
        first = []
        for b in range(nb):
            first.append(copy(b, 0, me, sibling, from_input=True))
            first += [copy(b, 1 + n, me, (*chip, c), from_input=True) for n, chip in enumerate(chips)]
        for cp in first:
            cp.start()
        passed = []
        for n, chip in enumerate(chips):
            for b in range(nb):
                copy(b, 1 + n, (*chip, c), me).wait_recv()
                passed.append(copy(b, 4 + n, (*chip, c), sibling))
                passed[-1].start()
        for b in range(nb):
            copy(b, 0, sibling, me).wait_recv()
            for n, chip in enumerate(chips):
                copy(b, 4 + n, (*chip, 1 - c), me).wait_recv()
        for cp in first + passed:
            cp.wait_send()
        for cp in mine:
            cp.wait()

    any_spec = pl.BlockSpec(memory_space=pl.ANY)
    return pl.pallas_call(
        body, name=name, in_specs=[any_spec] * nb, out_specs=[any_spec] * nb,
        out_shape=[jax.ShapeDtypeStruct((N_DEV,) + blk.shape, blk.dtype) for blk in blocks],
        scratch_shapes=[pltpu.SemaphoreType.DMA((nb, 7)), pltpu.SemaphoreType.DMA((nb, 7)),
                        pltpu.SemaphoreType.DMA((nb,))],
    )(*blocks)


def _peers(x, y, c):
    out = []
    for n in range(1, N_DEV):
        peer = (_flip(x, n & 4), _flip(y, n & 2), _flip(c, n & 1))
        out.append((n - 1, peer, 4 * peer[0] + 2 * peer[1] + peer[2]))
    return out


_HBM = pl.BlockSpec(memory_space=pltpu.HBM)
_SEM = pl.BlockSpec(memory_space=pltpu.SEMAPHORE)


def _exchange_start(srcs, scatter, name, after=None):
    ns = len(srcs)
    lands = [lax.empty(s.shape if scatter else (N_DEV,) + s.shape, s.dtype) for s in srcs]

    def body(*refs):
        src_refs, land_refs = refs[:ns], refs[ns:2 * ns]
        pos = 2 * ns + (1 if after is not None else 0)
        send_sems, recv_sems, token = refs[pos], refs[pos + 1], refs[-1]
        x, y, c = _place()
        my_idx = 4 * x + 2 * y + c
        for col, peer, peer_idx in _peers(x, y, c):
            for b in range(ns):
                pltpu.make_async_remote_copy(
                    src_ref=src_refs[b].at[peer_idx] if scatter else src_refs[b], dst_ref=land_refs[b].at[my_idx],
                    send_sem=send_sems.at[b * (N_DEV - 1) + col], recv_sem=recv_sems.at[b * (N_DEV - 1) + col],
                    device_id=peer, device_id_type=pl.DeviceIdType.MESH).start()
        token[...] = jnp.zeros_like(token)

    args = [pltpu.with_memory_space_constraint(a, pltpu.HBM) for a in list(srcs) + lands]
    in_specs = [_HBM] * (2 * ns)
    if after is not None:
        args.append(after)
        in_specs.append(pl.BlockSpec(memory_space=pl.ANY))
    out = pl.pallas_call(
        body, name=name, in_specs=in_specs,
        out_specs=[_SEM, _SEM] + [_HBM] * (2 * ns) + [pl.BlockSpec(memory_space=pltpu.VMEM)],
        out_shape=[pltpu.SemaphoreType.DMA((ns * (N_DEV - 1),)), pltpu.SemaphoreType.DMA((ns * (N_DEV - 1),))]
        + [pltpu.HBM(a.shape, a.dtype) for a in list(srcs) + lands] + [jax.ShapeDtypeStruct((8, SLOT), F32)],
        input_output_aliases={k: 2 + k for k in range(2 * ns)},
        compiler_params=pltpu.CompilerParams(has_side_effects=pltpu.SideEffectType.DATAFLOW_SIDE_EFFECTING),
    )(*args)
    return (out[0], out[1], out[2:2 + ns], out[2 + ns:2 + 2 * ns], scatter), out[-1]


def _exchange_wait(handle, after, name):
    send_sems, recv_sems, srcs, lands, scatter = handle
    ns = len(srcs)

    def body(*refs):
        src_refs, land_refs = refs[:ns], refs[ns:2 * ns]
        send_ref, recv_ref = refs[2 * ns], refs[2 * ns + 1]
        x, y, c = _place()
        for col, peer, peer_idx in _peers(x, y, c):
            for b in range(ns):
                copy = pltpu.make_async_remote_copy(
                    src_ref=src_refs[b].at[peer_idx] if scatter else src_refs[b], dst_ref=land_refs[b].at[peer_idx],
                    send_sem=send_ref.at[b * (N_DEV - 1) + col], recv_sem=recv_ref.at[b * (N_DEV - 1) + col],
                    device_id=peer, device_id_type=pl.DeviceIdType.MESH)
                copy.wait_send()
                copy.wait_recv()

    out = pl.pallas_call(
        body, name=name, in_specs=[_HBM] * (2 * ns) + [_SEM, _SEM, pl.BlockSpec(memory_space=pl.ANY)],
        out_specs=[_HBM] * (2 * ns),
        out_shape=[pltpu.HBM(a.shape, a.dtype) for a in list(srcs) + list(lands)],
        input_output_aliases={k: k for k in range(2 * ns)},
        compiler_params=pltpu.CompilerParams(has_side_effects=pltpu.SideEffectType.DATAFLOW_SIDE_EFFECTING),
    )(*srcs, *lands, send_sems, recv_sems, after)
    my_idx = 4 * lax.axis_index("x") + 2 * lax.axis_index("y") + lax.axis_index("c")
    landed = []
    for src, land in zip(out[:ns], out[ns:]):
        own = lax.dynamic_index_in_dim(src, my_idx, 0, keepdims=True) if scatter else src[None]
        landed.append(lax.dynamic_update_index_in_dim(land, own, my_idx, 0))
    return landed


def _adamw(parts, w, m, v, name):
    lyr, rows, cols = w.shape
    assert len(parts) == lyr
    tr = ADAM_ROWS if cols > 512 else 2 * ADAM_ROWS
    while rows % tr:
        tr //= 2
    tr = min(tr, rows)

    def body(*refs):
        p_refs = refs[:lyr]
        w_ref, m_ref, v_ref, g_out, d_out, m_out, v_out = refs[lyr:]
        for k in range(lyr):
            @pl.when(pl.program_id(0) == k)
            def _(p_ref=p_refs[k]):
                g = p_ref[0].astype(F32)
                for s in range(1, N_DEV):
                    g = g + p_ref[s].astype(F32)
                m2 = ADAM_B1 * m_ref[...] + (1.0 - ADAM_B1) * g
                v2 = ADAM_B2 * v_ref[...] + (1.0 - ADAM_B2) * (g * g)
                m_hat = m2 / (1.0 - ADAM_B1 ** ADAM_STEP)
                v_hat = v2 / (1.0 - ADAM_B2 ** ADAM_STEP)
                g_out[...] = g
                d_out[...] = -ADAM_LR * (m_hat / (jnp.sqrt(v_hat) + ADAM_EPS) + ADAM_WD * w_ref[...])
                m_out[...] = m2
                v_out[...] = v2

    def part_spec(k):
        return pl.BlockSpec((N_DEV, tr, cols), lambda l, i: (0, jnp.where(l == k, i, 0), 0))

    spec = pl.BlockSpec((None, tr, cols), lambda l, i: (l, i, 0))
    shp = jax.ShapeDtypeStruct((lyr, rows, cols), F32)
    return pl.pallas_call(
        body, name=name, grid=(lyr, rows // tr),
        in_specs=[part_spec(k) for k in range(lyr)] + [spec, spec, spec],
        out_specs=[spec] * 4, out_shape=[shp] * 4, compiler_params=_cparams(),
    )(*parts, w, m, v)


def _pack(arrays, lanes, row_mult, dtype):
    flat = jnp.concatenate([a.reshape(-1).astype(dtype) for a in arrays])
    unit = lanes * row_mult
    total = -(-flat.shape[0] // unit) * unit
    return jnp.pad(flat, (0, total - flat.shape[0])).reshape(total // lanes, lanes)


def _unpack(packed, shapes):
    flat = packed.reshape(-1)
    out, off = [], 0
    for shp in shapes:
        n = 1
        for d in shp:
            n *= d
        out.append(flat[off:off + n].reshape(shp))
        off += n
    return out


def _pad_slots(w, axis):
    axis = axis % w.ndim
    n = w.shape[axis] // HEAD_DIM
    shp = w.shape[:axis] + (n, HEAD_DIM) + w.shape[axis + 1:]
    pad = [(0, 0)] * (w.ndim + 1)
    pad[axis + 1] = (0, SLOT - HEAD_DIM)
    return jnp.pad(w.reshape(shp), pad).reshape(w.shape[:axis] + (n * SLOT,) + w.shape[axis + 1:])


def _unpad_slots(w, axis, keep=HEAD_DIM):
    axis = axis % w.ndim
    n = w.shape[axis] // SLOT
    shp = w.shape[:axis] + (n, SLOT) + w.shape[axis + 1:]
    idx = [slice(None)] * (w.ndim + 1)
    idx[axis + 1] = slice(0, keep)
    return w.reshape(shp)[tuple(idx)].reshape(w.shape[:axis] + (n * keep,) + w.shape[axis + 1:])


def _mla_in_pad(w):
    z = functools.partial(jnp.zeros, dtype=w.dtype)
    rows = w.shape[0]
    return jnp.concatenate([w[:, :384], z((rows, 64)), w[:, 640:672], z((rows, 32)), w[:, 384:640],
                            _pad_slots(w[:, 672:], 1)], axis=1)


def _mla_in_unpad(d):
    return jnp.concatenate([d[:, :384], d[:, 512:768], d[:, 448:480], _unpad_slots(d[:, 768:], 1)], axis=1)


def _mla_uq_pad(w):
    return jnp.pad(w.reshape(w.shape[0], MLA_HEADS, MLA_QK), ((0, 0), (0, 0), (0, SLOT - MLA_QK))).reshape(
        w.shape[0], MLA_HEADS * SLOT)


def _mla_ukv_pad(w):
    w3 = w.reshape(w.shape[0], MLA_HEADS, 2 * HEAD_DIM)
    pad = ((0, 0), (0, 0), (0, SLOT - HEAD_DIM))
    k = jnp.pad(w3[:, :, :HEAD_DIM], pad).reshape(w.shape[0], -1)
    v = jnp.pad(w3[:, :, HEAD_DIM:], pad).reshape(w.shape[0], -1)
    return jnp.concatenate([k, v], axis=1)


def _mla_ukv_unpad(d):
    hw = MLA_HEADS * SLOT
    k = d[:, :hw].reshape(d.shape[0], MLA_HEADS, SLOT)[:, :, :HEAD_DIM]
    v = d[:, hw:].reshape(d.shape[0], MLA_HEADS, SLOT)[:, :, :HEAD_DIM]
    return jnp.concatenate([k, v], axis=2).reshape(d.shape[0], MLA_HEADS * 2 * HEAD_DIM)


def _join(gathered, axis):
    nd, a, b = gathered.shape
    if axis == 1:
        return gathered.reshape(nd * a, b)
    return gathered.transpose(1, 0, 2).reshape(a, nd * b)


def _split(full, axis):
    r, c = full.shape
    if axis == 1:
        return full.reshape(N_DEV, r // N_DEV, c).astype(BF16)
    return full.reshape(r, N_DEV, c // N_DEV).transpose(1, 0, 2).astype(BF16)


def kernel(x, mem, positions, attn_norm_g, mlp_norm_g, mem_norm_g, final_norm_g, mla_w_in, mla_q_norm_g, mla_kv_norm_g, mla_w_uq, mla_w_ukv, swa_w_in, swa_sinks, w_mem_kv, w_o, mlp_w_up, mlp_w_down, loss_target, m_attn_norm_g, m_mlp_norm_g, m_mem_norm_g, m_final_norm_g, m_mla_w_in, m_mla_q_norm_g, m_mla_kv_norm_g, m_mla_w_uq, m_mla_w_ukv, m_swa_w_in, m_swa_sinks, m_w_mem_kv, m_w_o, m_mlp_w_up, m_mlp_w_down, v_attn_norm_g, v_mlp_norm_g, v_mem_norm_g, v_final_norm_g, v_mla_w_in, v_mla_q_norm_g, v_mla_kv_norm_g, v_mla_w_uq, v_mla_w_ukv, v_swa_w_in, v_swa_sinks, v_w_mem_kv, v_w_o, v_mlp_w_up, v_mlp_w_down):
    given = dict(locals())
    seq = x.shape[1]
    x0 = x.reshape(seq, D_MODEL)
    tgt = loss_target.reshape(seq, D_MODEL)
    mem0 = mem.reshape(N_MEM, D_MODEL)
    pos = positions.reshape(seq).astype(F32)
    pos_col, pos_row = pos.reshape(seq, 1), pos.reshape(1, seq)

    def layer_names(i):
        mixer = ("mla_w_in", "mla_w_uq", "mla_w_ukv") if i % 2 == 0 else ("swa_w_in",)
        return [(n, i // 2) for n in mixer] + [(n, i) for n in ("w_mem_kv", "w_o", "mlp_w_up", "mlp_w_down")]

    def local_weights(names):
        return [given[n][l].astype(BF16) for n, l in names]

    first_attn, first_mlp = layer_names(0)[:-2], layer_names(0)[-2:]
    weights = [dict(zip([n for n, _ in first_attn], _all_gather(local_weights(first_attn), "gather_weights_first")))]
    coming_mlp, first_token = _exchange_start(local_weights(first_mlp), False, "gather_weights_start_0",
                                              after=weights[0]["w_o"])

    consts = _lane_consts()
    tabs = _rope_tables(pos_col, consts)
    slopes = 2.0 ** (-8.0 * (jnp.arange(SWA_HEADS, dtype=F32) + 1.0) / SWA_HEADS)

    mem_n = _rmsnorm_fwd(mem0, 0, D_MODEL, mem_norm_g, "rmsnorm_fwd_mem")

    saved = []
    xc = x0
    for i in range(DEPTH):
        j = i // 2
        wts = weights[i]
        s = {"x_in": xc}
        token = None
        if i + 1 < DEPTH:
            coming, token = _exchange_start(local_weights(layer_names(i + 1)), False,
                                            "gather_weights_start_%d" % (i + 1),
                                            after=first_token if i == 0 else wts["w_o"])
        hn = _rmsnorm_fwd(xc, 0, D_MODEL, attn_norm_g[i], "rmsnorm_fwd", after=token)
        if i % 2 == 0:
            w_in = _mla_in_pad(_join(wts["mla_w_in"], 1))
            w_uq = _mla_uq_pad(_join(wts["mla_w_uq"], 2))
            w_kv = _mla_ukv_pad(_join(wts["mla_w_ukv"], 2))
            proj = _mm(hn, w_in, "nn", F32, "mm_mla_in")
            cqn = _rmsnorm_fwd(proj, 0, MLA_Q_RANK, mla_q_norm_g[j], "rmsnorm_fwd_q")
            ckvn = _rmsnorm_fwd(proj, 2, MLA_KV_RANK, mla_kv_norm_g[j], "rmsnorm_fwd_kv")
            qraw = _mm(cqn, w_uq, "nn", F32, "mm_mla_uq")
            kvraw = _mm(ckvn, w_kv, "nn", F32, "mm_mla_ukv")
            q, k, v = _mla_rope_fwd(qraw, kvraw, proj, tabs)
            o, lse = _mla_attn_fwd(q, k, v)
            qoff = MLA_QOFF
            s.update(w_uq=w_uq, w_kv=w_kv, cqn=cqn, ckvn=ckvn, q=q, k=k, v=v)
        else:
            w_in = _pad_slots(_join(wts["swa_w_in"], 2), 1)
            proj = _mm(hn, w_in, "nn", BF16, "mm_swa_in")
            o, lse = _swa_attn_fwd(proj, pos_col, pos_row, slopes, swa_sinks[j])
            qoff = SWA_QOFF
        w_mem = _pad_slots(_join(wts["w_mem_kv"], 1), 1)
        w_out = _pad_slots(_join(wts["w_o"], 1), 0)
        w_o_mix, w_o_cross = w_out[:SWA_HEADS * SLOT], w_out[SWA_HEADS * SLOT:]
        kvmem = _mm(mem_n, w_mem, "nn", BF16, "mm_mem_kv")
        cross = _cross_attn_fwd(proj, qoff, kvmem)
        x1 = _mm(o, w_o_mix, "nn", F32, "mm_o_mix", res=xc)
        x1 = _mm(cross, w_o_cross, "nn", F32, "mm_o_cross", res=x1)
        hn2 = _rmsnorm_fwd(x1, 0, D_MODEL, mlp_norm_g[i], "rmsnorm_fwd")
        if i == 0:
            wts.update(zip([n for n, _ in first_mlp], _exchange_wait(coming_mlp, hn2, "gather_weights_wait_0")))
        act, act2 = _mm(hn2, wts["mlp_w_up"], "nn", BF16, "mm_mlp_up", epi="relu2", b_blk="cols")
        xc = _mm(act2, wts["mlp_w_down"], "nn", F32, "mm_mlp_down", res=x1, b_blk="rows")
        s.update(hn=hn, w_in=w_in, proj=proj, o=o, lse=lse, qoff=qoff, w_mem=w_mem, w_o_mix=w_o_mix,
                 w_o_cross=w_o_cross, kvmem=kvmem, cross=cross, x1=x1, hn2=hn2, act=act, act2=act2)
        saved.append(s)
        if i + 1 < DEPTH:
            got = _exchange_wait(coming, xc, "gather_weights_wait_%d" % (i + 1))
            weights.append(dict(zip([n for n, _ in layer_names(i + 1)], got)))

    dx, dx_b, dg_final, loss_part = _loss_head(xc, final_norm_g, tgt)
    loss = lax.psum(loss_part[0, 0], MESH_AXES)

    gains = {n: [None] * DEPTH for n in ("attn_norm_g", "mlp_norm_g")}
    for n in ("mla_q_norm_g", "mla_kv_norm_g", "swa_sinks"):
        gains[n] = [None] * 2
    leaving = {}
    token = None
    dmem_n = None
    for i in reversed(range(DEPTH)):
        j = i // 2
        s = saved[i]
        wts = weights[i]
        out = {}
        du = _mm(dx_b, wts["mlp_w_down"], "nt", BF16, "mm_mlp_down_dx", aux=s["act"], epi="mul2aux", b_blk="rows",
                 after=token)
        out["mlp_w_down"] = _mm(dx_b, s["act2"], "tn", BF16, "mm_mlp_down_dw", o_blk="cols_t")
        dhn2 = _mm(du, wts["mlp_w_up"].transpose(0, 2, 1), "nn", F32, "mm_mlp_up_dx", b_blk="rows")
        out["mlp_w_up"] = _mm(s["hn2"], du, "tn", BF16, "mm_mlp_up_dw", o_blk="cols")
        dx1, dx1_b, dg = _rmsnorm_bwd(s["x1"], 0, D_MODEL, mlp_norm_g[i], dhn2, dx, F32, "rmsnorm_bwd", also_bf16=True)
        gains["mlp_norm_g"][i] = dg[0]

        do = _mm(dx1_b, s["w_o_mix"], "nt", BF16, "mm_o_mix_dx")
        dcross = _mm(dx1_b, s["w_o_cross"], "nt", BF16, "mm_o_cross_dx")
        dw_o = jnp.concatenate([_mm(s["o"], dx1_b, "tn", F32, "mm_o_mix_dw"),
                                _mm(s["cross"], dx1_b, "tn", F32, "mm_o_cross_dw")], axis=0)
        out["w_o"] = _split(_unpad_slots(dw_o, 0), 1)
        dqc, dkm, dvm = _cross_attn_bwd(s["proj"], s["qoff"], s["kvmem"], dcross)
        dkvmem = jnp.concatenate([dkm, dvm], axis=1).astype(BF16)
        out["w_mem_kv"] = _split(_unpad_slots(_mm(mem_n, dkvmem, "tn", F32, "mm_mem_kv_dw"), 1), 1)
        dmem_n = _mm(dkvmem, s["w_mem"], "nt", F32, "mm_mem_kv_dx" if dmem_n is None else "mm_mem_kv_dx_acc",
                     res=dmem_n)
        leaving[(i, "main")], token = _exchange_start([out[n] for n, _ in layer_names(i)[-4:]], True,
                                                      "exchange_grads_main_start_%d" % i)

        if i % 2 == 0:
            dq, dk, dv = _mla_attn_bwd(s["q"], s["k"], s["v"], s["o"], do, s["lse"], token)
            dqraw, dkv, dkr = _mla_rope_bwd(dq, dk, dv, tabs, consts)
            dcqn = _mm(dqraw, s["w_uq"], "nt", F32, "mm_mla_uq_dx")
            out["mla_w_uq"] = _split(_unpad_slots(_mm(s["cqn"], dqraw, "tn", F32, "mm_mla_uq_dw"), 1, MLA_QK), 2)
            dckvn = _mm(dkv, s["w_kv"], "nt", F32, "mm_mla_ukv_dx")
            out["mla_w_ukv"] = _split(_mla_ukv_unpad(_mm(s["ckvn"], dkv, "tn", F32, "mm_mla_ukv_dw")), 2)
            dcq, dg = _rmsnorm_bwd(s["proj"], 0, MLA_Q_RANK, mla_q_norm_g[j], dcqn, None, BF16, "rmsnorm_bwd_q")
            gains["mla_q_norm_g"][j] = dg[0]
            dckv, dg = _rmsnorm_bwd(s["proj"], 2, MLA_KV_RANK, mla_kv_norm_g[j], dckvn, None, BF16, "rmsnorm_bwd_kv")
            gains["mla_kv_norm_g"][j] = dg[0]
            dproj = jnp.concatenate([dcq, dkr.astype(BF16), dckv, dqc.astype(BF16)], axis=1)
            dhn = _mm(dproj, s["w_in"], "nt", F32, "mm_mla_in_dx")
            out["mla_w_in"] = _split(_mla_in_unpad(_mm(s["hn"], dproj, "tn", F32, "mm_mla_in_dw")), 1)
        else:
            dq, dk, dv, dsink = _swa_attn_bwd(s["proj"], s["o"], do, s["lse"], pos_col, pos_row, slopes, swa_sinks[j],
                                              token)
            gains["swa_sinks"][j] = dsink[::8, 0]
            dproj = jnp.concatenate([dq, dk, dv, dqc], axis=1).astype(BF16)
            dhn = _mm(dproj, s["w_in"], "nt", F32, "mm_swa_in_dx")
            out["swa_w_in"] = _split(_unpad_slots(_mm(s["hn"], dproj, "tn", F32, "mm_swa_in_dw"), 1), 2)
        dx, dx_b, dg = _rmsnorm_bwd(s["x_in"], 0, D_MODEL, attn_norm_g[i], dhn, dx1, F32, "rmsnorm_bwd", also_bf16=True)
        gains["attn_norm_g"][i] = dg[0]

        leaving[(i, "mixer")], token = _exchange_start([out[n] for n, _ in layer_names(i)[:-4]], True,
                                                       "exchange_grads_mixer_start_%d" % i)

    _, dg_mem = _rmsnorm_bwd(mem0, 0, D_MODEL, mem_norm_g, dmem_n, None, BF16, "rmsnorm_bwd_mem")
    gains = {n: jnp.stack(g) for n, g in gains.items()}
    gains["mem_norm_g"] = dg_mem[0]
    gains["final_norm_g"] = dg_final[0]

    result = {}

    def adamw_of(names, received):
        for n in names:
            parts = [received[(n, l)] for l in range(given[n].shape[0])]
            for kind, r in enumerate(_adamw(parts, given[n], given["m_" + n], given["v_" + n], "adamw_" + n)):
                result[(kind, n)] = r

    received = {}
    for i in reversed(range(DEPTH)):
        got = _exchange_wait(leaving[(i, "main")], dx, "exchange_grads_main_wait_%d" % i)
        received.update(zip(layer_names(i)[-4:], got))
    adamw_of(("mlp_w_up", "mlp_w_down", "w_o", "w_mem_kv"), received)
    for i in reversed(range(DEPTH)):
        got = _exchange_wait(leaving[(i, "mixer")], result[(0, "w_mem_kv")], "exchange_grads_mixer_wait_%d" % i)
        received.update(zip(layer_names(i)[:-4], got))
    adamw_of(("mla_w_in", "mla_w_uq", "mla_w_ukv", "swa_w_in"), received)

    rep_shapes = [given[n].shape for n in REPLICATED]
    rep_parts = _all_gather([_pack([gains[n] for n in REPLICATED], SLOT, 8, F32)], "gather_gain_grads")[0]
    rep_packed = [_pack([given[p + n] for n in REPLICATED], SLOT, 8, F32)[None] for p in ("", "m_", "v_")]
    for kind, r in enumerate(_adamw([rep_parts], *rep_packed, "adamw_gains")):
        for n, part in zip(REPLICATED, _unpack(r[0], rep_shapes)):
            result[(kind, n)] = part

    outs = [loss, dx.reshape(1, seq, D_MODEL)]
    for kind in range(4):
        outs += [result[(kind, n)] for n in WEIGHT_ORDER]
    return tuple(outs)
```

```python
import functools

import jax
import jax.numpy as jnp
from jax import lax
from jax.experimental import pallas as pl
from jax.experimental.pallas import tpu as pltpu

F32 = jnp.float32
BF16 = jnp.bfloat16

D_MODEL = 1024
D_FF = 4096
N_MEM = 256
DEPTH = 4
SLOT = 128
HEAD_DIM = 64
MLA_HEADS = 12
MLA_QK = 96
MLA_Q_RANK = 384
MLA_KV_RANK = 256
SWA_HEADS = 12
SWA_KV_HEADS = 4
SWA_GROUP = 3
MEM_HEADS = 4
WINDOW = 128
EPS = 1e-6
NEG = -1e30
ROPE_THETA = 10000.0
N_DEV = 8

ADAM_LR = 0.001
ADAM_B1 = 0.9
ADAM_B2 = 0.999
ADAM_EPS = 1e-08
ADAM_WD = 0.01
ADAM_STEP = 10

TM = 512
TQ_MLA = 1024
MLA_PACK = 2
SWA_PACK = 2
TQ_CROSS = 2048
MM_VMEM_BUDGET = 38 * 1024 * 1024
ADAM_ROWS = 128
VMEM_LIMIT = 56 * 1024 * 1024

MESH_AXES = ("x", "y", "c")

MLA_PAD_IN = 384 + SLOT + 256 + MEM_HEADS * SLOT
MLA_QOFF = (384 + SLOT + 256) // SLOT
SWA_PAD_IN = (SWA_HEADS + 2 * SWA_KV_HEADS + MEM_HEADS) * SLOT
SWA_QOFF = SWA_HEADS + 2 * SWA_KV_HEADS

SHARDED = (
    ("mla_w_in", 1), ("mla_w_uq", 2), ("mla_w_ukv", 2), ("swa_w_in", 2),
    ("w_mem_kv", 1), ("w_o", 1), ("mlp_w_up", 2), ("mlp_w_down", 1),
)
REPLICATED = ("attn_norm_g", "mlp_norm_g", "mem_norm_g", "final_norm_g",
              "mla_q_norm_g", "mla_kv_norm_g", "swa_sinks")
WEIGHT_ORDER = ("attn_norm_g", "mlp_norm_g", "mem_norm_g", "final_norm_g", "mla_w_in",
                "mla_q_norm_g", "mla_kv_norm_g", "mla_w_uq", "mla_w_ukv", "swa_w_in",
                "swa_sinks", "w_mem_kv", "w_o", "mlp_w_up", "mlp_w_down")


def _cparams():
    return pltpu.CompilerParams(vmem_limit_bytes=VMEM_LIMIT)


_DIMS = {"nn": (((1,), (0,)), ((), ())), "nt": (((1,), (1,)), ((), ())), "tn": (((0,), (0,)), ((), ()))}


def _mm_tiles(m, n, k, a_bytes, b_bytes, o_bytes, extra_bytes, tm_fixed, tn_fixed):
    best = None
    for tm in ([tm_fixed] if tm_fixed else [t for t in (4096, 2048, 1024, 512, 256, 128) if m % t == 0] or [m]):
        for tn in ([tn_fixed] if tn_fixed else [t for t in range(1024, 0, -SLOT) if n % t == 0] or [n]):
            need = 2 * (tm * k * a_bytes + k * tn * b_bytes + tm * tn * (o_bytes + extra_bytes))
            need += tm * tn * 4
            if need <= MM_VMEM_BUDGET and (best is None or tm * tn > best[0] * best[1]):
                best = (tm, tn)
    assert best is not None, (m, n, k)
    return best


def _mm(a, b, mode, out_dtype, name, res=None, aux=None, epi=None, b_blk=None, o_blk=None, after=None, norm=None):
    if b_blk is not None:
        nb, br, bc = b.shape
        b_shape = (nb * br, bc) if b_blk == "rows" else (br, nb * bc)
    else:
        b_shape = b.shape
    if mode == "nn":
        (m, k), (k2, n) = a.shape, b_shape
    elif mode == "nt":
        (m, k), (n, k2) = a.shape, b_shape
    else:
        (k, m), (k2, n) = a.shape, b_shape
    assert k == k2, (a.shape, b_shape, mode)
    k_blocked = b_blk is not None and (b_blk == "rows") == (mode != "nt")
    assert not (k_blocked and mode == "nt")
    tn_fixed = None
    if b_blk is not None and not k_blocked:
        tn_fixed = br if b_blk == "rows" else bc
    if o_blk == "cols":
        tn_fixed = n // N_DEV
    tm_fixed = m // N_DEV if o_blk == "rows" else None
    has_res, has_aux, has_norm = res is not None, aux is not None, epi == "normbwd"
    assert o_blk is None or not (has_res or has_aux or has_norm)
    n_out = 2 if epi == "relu2" else 1
    if has_norm:
        tn_fixed = n
        o_bytes, extra_bytes = 4 + 2, 4 + 4
    else:
        o_bytes = n_out * jnp.dtype(out_dtype).itemsize
        extra_bytes = (4 if has_res else 0) + (aux.dtype.itemsize if has_aux else 0)
    tm, tn = _mm_tiles(m, n, k, a.dtype.itemsize, b.dtype.itemsize, o_bytes, extra_bytes, tm_fixed, tn_fixed)
    dims = _DIMS[mode]
    if mode == "tn":
        a_spec = pl.BlockSpec((k, tm), lambda i, j: (0, i))
    else:
        a_spec = pl.BlockSpec((tm, k), lambda i, j: (i, 0))
    if b_blk is None:
        if mode == "nt":
            b_spec = pl.BlockSpec((tn, k), lambda i, j: (j, 0))
        else:
            b_spec = pl.BlockSpec((k, tn), lambda i, j: (0, j))
    elif k_blocked:
        b_spec = pl.BlockSpec((N_DEV, br, tn), lambda i, j: (0, 0, j))
    elif mode == "nt":
        b_spec = pl.BlockSpec((None, tn, k), lambda i, j: (j, 0, 0))
    else:
        b_spec = pl.BlockSpec((None, k, tn), lambda i, j: (j, 0, 0))
    if o_blk is None:
        o_spec = pl.BlockSpec((tm, tn), lambda i, j: (i, j))
        o_shape = (m, n)
    elif o_blk == "rows":
        o_spec = pl.BlockSpec((None, tm, tn), lambda i, j: (i, 0, j))
        o_shape = (N_DEV, tm, n)
    else:
        o_spec = pl.BlockSpec((None, tm, tn), lambda i, j: (j, i, 0))
        o_shape = (N_DEV, m, tn)

    def body(*refs):
        a_ref, b_ref = refs[0], refs[1]
        pos = 2
        res_ref = aux_ref = None
        if has_res:
            res_ref = refs[pos]
            pos += 1
        if has_aux:
            aux_ref = refs[pos]
            pos += 1
        if has_norm:
            x_ref, g_ref, dres_ref = refs[pos:pos + 3]
            pos += 3
        if after is not None:
            pos += 1
        outs = refs[pos:]
        bv = b_ref[...].reshape(k, tn) if k_blocked else b_ref[...]
        r = lax.dot_general(a_ref[...].astype(BF16), bv.astype(BF16), dims, preferred_element_type=F32)
        if epi == "relu2":
            r = jnp.maximum(r, 0.0)
            outs[0][...] = r.astype(outs[0].dtype)
            outs[1][...] = (r * r).astype(outs[1].dtype)
        elif has_norm:
            xv = x_ref[...]
            rs = lax.rsqrt(jnp.mean(xv * xv, axis=1, keepdims=True) + EPS)
            xh = xv * rs
            dxh = r * g_ref[...]
            dx = rs * (dxh - xh * jnp.mean(dxh * xh, axis=1, keepdims=True)) + dres_ref[...]
            outs[0][...] = dx
            outs[1][...] = dx.astype(BF16)

            @pl.when(pl.program_id(0) == 0)
            def _():
                outs[2][...] = jnp.zeros_like(outs[2])

            outs[2][...] += jnp.sum(r * xh, axis=0, keepdims=True)
        else:
            if epi == "mul2aux":
                r = r * (2.0 * aux_ref[...].astype(F32))
            if has_res:
                r = r + res_ref[...]
            outs[0][...] = r.astype(outs[0].dtype)

    in_specs = [a_spec, b_spec]
    args = [a, b]
    if has_res:
        in_specs.append(o_spec)
        args.append(res)
    if has_aux:
        in_specs.append(o_spec)
        args.append(aux)
    vec_spec = pl.BlockSpec((1, n), lambda i, j: (0, 0))
    if has_norm:
        in_specs += [o_spec, vec_spec, o_spec]
        args += [norm[0], norm[1].reshape(1, n), norm[2]]
    if after is not None:
        in_specs.append(pl.BlockSpec(memory_space=pl.ANY))
        args.append(after)
    if has_norm:
        out_specs = [o_spec, o_spec, vec_spec]
        out_shape = [jax.ShapeDtypeStruct(o_shape, F32), jax.ShapeDtypeStruct(o_shape, BF16),
                     jax.ShapeDtypeStruct((1, n), F32)]
    else:
        out_specs = [o_spec] * n_out
        out_shape = [jax.ShapeDtypeStruct(o_shape, out_dtype)] * n_out
    out = pl.pallas_call(
        body, name=name, grid=(m // tm, n // tn),
        in_specs=in_specs, out_specs=out_specs, out_shape=out_shape, compiler_params=_cparams(),
    )(*args)
    return out if len(out) > 1 else out[0]


def _rmsnorm_fwd(xarr, colblk, width, g, name, after=None):
    rows = xarr.shape[0]
    tm = min(TM, rows)

    def body(x_ref, g_ref, *rest):
        y_ref = rest[-1]
        x = x_ref[...].astype(F32)
        r = lax.rsqrt(jnp.mean(x * x, axis=1, keepdims=True) + EPS)
        y_ref[...] = (x * r * g_ref[...]).astype(y_ref.dtype)

    in_specs = [pl.BlockSpec((tm, width), lambda i: (i, colblk)), pl.BlockSpec((1, width), lambda i: (0, 0))]
    args = [xarr, g.reshape(1, width)]
    if after is not None:
        in_specs.append(pl.BlockSpec(memory_space=pl.ANY))
        args.append(after)
    return pl.pallas_call(
        body, name=name, grid=(rows // tm,), in_specs=in_specs,
        out_specs=pl.BlockSpec((tm, width), lambda i: (i, 0)),
        out_shape=jax.ShapeDtypeStruct((rows, width), BF16), compiler_params=_cparams(),
    )(*args)


def _rmsnorm_bwd(xarr, colblk, width, g, dy, dres, out_dtype, name):
    rows = xarr.shape[0]
    tm = min(TM, rows)
    has_res = dres is not None

    def body(*refs):
        x_ref, g_ref, dy_ref = refs[0], refs[1], refs[2]
        dres_ref = refs[3] if has_res else None
        dx_ref, dg_ref = refs[-2], refs[-1]
        x = x_ref[...].astype(F32)
        dyv = dy_ref[...].astype(F32)
        r = lax.rsqrt(jnp.mean(x * x, axis=1, keepdims=True) + EPS)
        xh = x * r
        dxh = dyv * g_ref[...]
        dx = r * (dxh - xh * jnp.mean(dxh * xh, axis=1, keepdims=True))
        if has_res:
            dx = dx + dres_ref[...]
        dx_ref[...] = dx.astype(dx_ref.dtype)

        @pl.when(pl.program_id(0) == 0)
        def _():
            dg_ref[...] = jnp.zeros_like(dg_ref)

        dg_ref[...] += jnp.sum(dyv * xh, axis=0, keepdims=True)

    row_spec = pl.BlockSpec((tm, width), lambda i: (i, 0))
    vec_spec = pl.BlockSpec((1, width), lambda i: (0, 0))
    in_specs = [pl.BlockSpec((tm, width), lambda i: (i, colblk)), vec_spec, row_spec]
    args = [xarr, g.reshape(1, width), dy]
    if has_res:
        in_specs.append(row_spec)
        args.append(dres)
    return pl.pallas_call(
        body, name=name, grid=(rows // tm,), in_specs=in_specs, out_specs=[row_spec, vec_spec],
        out_shape=[jax.ShapeDtypeStruct((rows, width), out_dtype), jax.ShapeDtypeStruct((1, width), F32)],
        compiler_params=_cparams(),
    )(*args)


def _loss_head(x, g, tgt):
    rows, width = x.shape
    tm = min(TM, rows)

    def body(x_ref, g_ref, t_ref, dx_ref, dxb_ref, dg_ref, loss_ref):
        xv = x_ref[...]
        gv = g_ref[...]
        r = lax.rsqrt(jnp.mean(xv * xv, axis=1, keepdims=True) + EPS)
        xh = xv * r
        err = xh * gv - t_ref[...]
        part = 0.5 * jnp.sum(jnp.mean(err * err, axis=1, keepdims=True), axis=0, keepdims=True)
        dyv = err * (1.0 / width)
        dxh = dyv * gv
        dxv = r * (dxh - xh * jnp.mean(dxh * xh, axis=1, keepdims=True))
        dx_ref[...] = dxv
        dxb_ref[...] = dxv.astype(BF16)

        @pl.when(pl.program_id(0) == 0)
        def _():
            dg_ref[...] = jnp.zeros_like(dg_ref)
            loss_ref[...] = jnp.zeros_like(loss_ref)

        dg_ref[...] += jnp.sum(dyv * xh, axis=0, keepdims=True)
        loss_ref[...] += jnp.broadcast_to(part, loss_ref.shape)

    row_spec = pl.BlockSpec((tm, width), lambda i: (i, 0))
    vec_spec = pl.BlockSpec((1, width), lambda i: (0, 0))
    return pl.pallas_call(
        body, name="loss_head", grid=(rows // tm,), in_specs=[row_spec, vec_spec, row_spec],
        out_specs=[row_spec, row_spec, vec_spec, pl.BlockSpec((1, SLOT), lambda i: (0, 0))],
        out_shape=[jax.ShapeDtypeStruct((rows, width), F32), jax.ShapeDtypeStruct((rows, width), BF16),
                   jax.ShapeDtypeStruct((1, width), F32), jax.ShapeDtypeStruct((1, SLOT), F32)],
        compiler_params=_cparams(),
    )(x, g.reshape(1, width), tgt)


def _lane_consts():
    half = 16
    inv = ROPE_THETA ** (-(jnp.arange(half, dtype=F32) * 2.0) / 32)
    lane = jnp.arange(SLOT)
    first = (lane >= 64) & (lane < 80)
    second = (lane >= 80) & (lane < 96)
    inv_lane = jnp.where(first | second, inv[(lane - 64) % half], 0.0)
    rows = [inv_lane, (lane < 64).astype(F32), first.astype(F32), second.astype(F32)]
    rows += [jnp.zeros((SLOT,), F32)] * 4
    return jnp.stack(rows).astype(F32)


def _rope_tables(pos_col, consts):
    rows = pos_col.shape[0]
    tm = min(TM, rows)

    def body(p_ref, k_ref, c_ref, s1_ref, s2_ref):
        ang = p_ref[...] * k_ref[0:1, :]
        cos, sin = jnp.cos(ang), jnp.sin(ang)
        first, second = k_ref[2:3, :], k_ref[3:4, :]
        c_ref[...] = k_ref[1:2, :] + (first + second) * cos
        s1_ref[...] = -first * sin
        s2_ref[...] = second * sin

    spec = pl.BlockSpec((tm, SLOT), lambda i: (i, 0))
    shp = jax.ShapeDtypeStruct((rows, SLOT), F32)
    return pl.pallas_call(
        body, name="rope_tables", grid=(rows // tm,),
        in_specs=[pl.BlockSpec((tm, 1), lambda i: (i, 0)), pl.BlockSpec((8, SLOT), lambda i: (0, 0))],
        out_specs=[spec, spec, spec], out_shape=[shp, shp, shp], compiler_params=_cparams(),
    )(pos_col, consts)


def _rot(xv, c, s1, s2):
    return xv * c + pltpu.roll(xv, SLOT - 16, 1) * s1 + pltpu.roll(xv, 16, 1) * s2


def _rot_t(dy, c, s1, s2):
    return dy * c + pltpu.roll(dy * s1, 16, 1) + pltpu.roll(dy * s2, SLOT - 16, 1)


def _mla_rope_fwd(qraw, kvraw, proj, tabs):
    rows = qraw.shape[0]
    tm = min(256, rows)
    hw = MLA_HEADS * SLOT

    def body(q_ref, kv_ref, kr_ref, c_ref, s1_ref, s2_ref, qo, ko, vo):
        c, s1, s2 = c_ref[...], s1_ref[...], s2_ref[...]
        kr = _rot(kr_ref[...], c, s1, s2)
        for h in range(MLA_HEADS):
            sl = slice(h * SLOT, (h + 1) * SLOT)
            qo[:, sl] = _rot(q_ref[:, sl], c, s1, s2).astype(BF16)
            ko[:, sl] = (kv_ref[:, sl] + kr).astype(BF16)
            vo[:, sl] = kv_ref[:, hw + h * SLOT:hw + (h + 1) * SLOT].astype(BF16)

    tab = pl.BlockSpec((tm, SLOT), lambda i: (i, 0))
    wide = pl.BlockSpec((tm, hw), lambda i: (i, 0))
    shp = jax.ShapeDtypeStruct((rows, hw), BF16)
    return pl.pallas_call(
        body, name="mla_rope_fwd", grid=(rows // tm,),
        in_specs=[wide, pl.BlockSpec((tm, 2 * hw), lambda i: (i, 0)), pl.BlockSpec((tm, SLOT), lambda i: (i, 3)),
                  tab, tab, tab],
        out_specs=[wide, wide, wide], out_shape=[shp, shp, shp], compiler_params=_cparams(),
    )(qraw, kvraw, proj, *tabs)


def _mla_rope_bwd(dq, dk, dv, tabs, consts):
    rows = dq.shape[0]
    tm = min(256, rows)
    hw = MLA_HEADS * SLOT

    def body(dq_ref, dk_ref, dv_ref, c_ref, s1_ref, s2_ref, k_ref, dqo, dkvo, dkro):
        c, s1, s2 = c_ref[...], s1_ref[...], s2_ref[...]
        ksum = jnp.zeros((tm, SLOT), F32)
        for h in range(MLA_HEADS):
            sl = slice(h * SLOT, (h + 1) * SLOT)
            dqo[:, sl] = _rot_t(dq_ref[:, sl], c, s1, s2).astype(BF16)
            dkh = dk_ref[:, sl]
            ksum = ksum + dkh
            dkvo[:, sl] = dkh.astype(BF16)
            dkvo[:, hw + h * SLOT:hw + (h + 1) * SLOT] = dv_ref[:, sl].astype(BF16)
        dkro[...] = _rot_t(ksum, c, s1, s2) * (k_ref[2:3, :] + k_ref[3:4, :])

    tab = pl.BlockSpec((tm, SLOT), lambda i: (i, 0))
    wide = pl.BlockSpec((tm, hw), lambda i: (i, 0))
    return pl.pallas_call(
        body, name="mla_rope_bwd", grid=(rows // tm,),
        in_specs=[wide, wide, wide, tab, tab, tab, pl.BlockSpec((8, SLOT), lambda i: (0, 0))],
        out_specs=[wide, pl.BlockSpec((tm, 2 * hw), lambda i: (i, 0)), tab],
        out_shape=[jax.ShapeDtypeStruct((rows, hw), BF16), jax.ShapeDtypeStruct((rows, 2 * hw), BF16),
                   jax.ShapeDtypeStruct((rows, SLOT), F32)],
        compiler_params=_cparams(),
    )(dq, dk, dv, *tabs, consts)


def _nt(a, b):
    return lax.dot_general(a, b, _DIMS["nt"], preferred_element_type=F32)


def _tn(a, b):
    return lax.dot_general(a, b, _DIMS["tn"], preferred_element_type=F32)


def _nn(a, b):
    return lax.dot_general(a, b, _DIMS["nn"], preferred_element_type=F32)


def _causal(rows, keys):
    shp = (rows.stop - rows.start, keys.stop - keys.start)
    return (keys.start + lax.broadcasted_iota(jnp.int32, shp, 1)
            <= rows.start + lax.broadcasted_iota(jnp.int32, shp, 0))


def _mla_attn_fwd(q, k, v):
    rows = q.shape[0]
    t = min(TQ_MLA, rows)
    nt = rows // t
    scale = MLA_QK ** -0.5
    wide = MLA_PACK * SLOT

    def body(q_ref, k_ref, v_ref, o_ref, lse_ref, m_sc, l_sc, acc_sc):
        i, j = pl.program_id(1), pl.program_id(2)

        @pl.when(j == 0)
        def _():
            m_sc[...] = jnp.full_like(m_sc, NEG)
            l_sc[...] = jnp.zeros_like(l_sc)
            acc_sc[...] = jnp.zeros_like(acc_sc)

        def chunk(hh, rows, keys, masked):
            sl = slice(hh * SLOT, (hh + 1) * SLOT)
            s = _nt(q_ref[rows, sl], k_ref[keys, sl]) * scale
            if masked:
                s = jnp.where(_causal(rows, keys), s, NEG)
            m_prev = m_sc[hh, rows]
            m_new = jnp.maximum(m_prev, jnp.max(s, axis=1, keepdims=True))
            p = jnp.exp(s - m_new)
            alpha = jnp.exp(m_prev - m_new)
            l_sc[hh, rows] = alpha * l_sc[hh, rows] + jnp.sum(p, axis=1, keepdims=True)
            acc_sc[rows, sl] = alpha * acc_sc[rows, sl] + _nn(p.astype(BF16), v_ref[keys, sl])
            m_sc[hh, rows] = m_new

        @pl.when(j < i)
        def _():
            for hh in range(MLA_PACK):
                chunk(hh, slice(0, t), slice(0, t), False)

        @pl.when(j == i)
        def _():
            for hh in range(MLA_PACK):
                sl = slice(hh * SLOT, (hh + 1) * SLOT)
                chunk(hh, slice(0, t), slice(0, t // 2), True)
                chunk(hh, slice(t // 2, t), slice(t // 2, t), True)
                l = l_sc[hh]
                o_ref[:, sl] = (acc_sc[:, sl] / l).astype(o_ref.dtype)
                lse_ref[:, sl] = jnp.broadcast_to(m_sc[hh] + jnp.log(l), (t, SLOT))

    q_spec = pl.BlockSpec((t, wide), lambda h, i, j: (i, h))
    kv_spec = pl.BlockSpec((t, wide), lambda h, i, j: (jnp.minimum(j, i), h))
    return pl.pallas_call(
        body, name="mla_attn_fwd", grid=(MLA_HEADS // MLA_PACK, nt, nt),
        in_specs=[q_spec, kv_spec, kv_spec], out_specs=[q_spec, q_spec],
        out_shape=[jax.ShapeDtypeStruct(q.shape, BF16), jax.ShapeDtypeStruct(q.shape, F32)],
        scratch_shapes=[pltpu.VMEM((MLA_PACK, t, 1), F32), pltpu.VMEM((MLA_PACK, t, 1), F32),
                        pltpu.VMEM((t, wide), F32)],
        compiler_params=_cparams(),
    )(q, k, v)


def _mla_attn_bwd(q, k, v, o, do, lse, after):
    rows = q.shape[0]
    t = min(TQ_MLA, rows)
    nt = rows // t
    scale = MLA_QK ** -0.5
    wide = MLA_PACK * SLOT

    def body(q_ref, k_ref, v_ref, o_ref, do_ref, lse_ref, after_ref, dq_ref, dk_ref, dv_ref, dk_sc, dv_sc):
        j, i = pl.program_id(1), pl.program_id(2)

        @pl.when((j == 0) & (i == 0))
        def _():
            dq_ref[...] = jnp.zeros_like(dq_ref)

        @pl.when(i == 0)
        def _():
            dk_sc[...] = jnp.zeros_like(dk_sc)
            dv_sc[...] = jnp.zeros_like(dv_sc)

        def chunk(hh, rows, keys, masked):
            sl = slice(hh * SLOT, (hh + 1) * SLOT)
            n_rows = rows.stop - rows.start
            qv, kv, dov = q_ref[rows, sl], k_ref[keys, sl], do_ref[rows, sl]
            s = _nt(qv, kv) * scale
            if masked:
                s = jnp.where(_causal(rows, keys), s, NEG)
            p = jnp.exp(s - lse_ref[rows, hh * SLOT:hh * SLOT + 1])
            delta = jnp.sum(dov.astype(F32) * o_ref[rows, sl].astype(F32), axis=1, keepdims=True)
            dp = _nt(dov, v_ref[keys, sl])
            ds = (p * (dp - delta) * scale).astype(BF16)
            dv_sc[keys, sl] += _tn(p.astype(BF16), dov)
            dk_sc[keys, sl] += _tn(ds, qv)
            r0 = pl.multiple_of(i * t + rows.start, n_rows)
            dq_ref[pl.ds(r0, n_rows), sl] += _nn(ds, kv)

        @pl.when(i > j)
        def _():
            for hh in range(MLA_PACK):
                chunk(hh, slice(0, t), slice(0, t), False)

        @pl.when(i == j)
        def _():
            for hh in range(MLA_PACK):
                chunk(hh, slice(0, t), slice(0, t // 2), True)
                chunk(hh, slice(t // 2, t), slice(t // 2, t), True)

        @pl.when(i == nt - 1)
        def _():
            dk_ref[...] = dk_sc[...]
            dv_ref[...] = dv_sc[...]

    q_spec = pl.BlockSpec((t, wide), lambda h, j, i: (jnp.maximum(i, j), h))
    kv_spec = pl.BlockSpec((t, wide), lambda h, j, i: (j, h))
    head_spec = pl.BlockSpec((rows, wide), lambda h, j, i: (0, h))
    shp = jax.ShapeDtypeStruct(q.shape, F32)
    return pl.pallas_call(
        body, name="mla_attn_bwd", grid=(MLA_HEADS // MLA_PACK, nt, nt),
        in_specs=[q_spec, kv_spec, kv_spec, q_spec, q_spec, q_spec, pl.BlockSpec(memory_space=pl.ANY)],
        out_specs=[head_spec, kv_spec, kv_spec], out_shape=[shp, shp, shp],
        scratch_shapes=[pltpu.VMEM((t, wide), F32), pltpu.VMEM((t, wide), F32)],
        compiler_params=_cparams(),
    )(q, k, v, o, do, lse, after)


def _swa_specs(t):
    def prev(i):
        return jnp.maximum(i - 1, 0)
    kw = SWA_PACK * SLOT
    k0, v0 = SWA_HEADS // SWA_PACK, (SWA_HEADS + SWA_KV_HEADS) // SWA_PACK
    q3 = pl.BlockSpec((t, SWA_PACK * SWA_GROUP * SLOT), lambda h, i: (i, h))
    kp = pl.BlockSpec((t, kw), lambda h, i: (prev(i), k0 + h))
    kc = pl.BlockSpec((t, kw), lambda h, i: (i, k0 + h))
    vp = pl.BlockSpec((t, kw), lambda h, i: (prev(i), v0 + h))
    vc = pl.BlockSpec((t, kw), lambda h, i: (i, v0 + h))
    pcol = pl.BlockSpec((t, 1), lambda h, i: (i, 0))
    prow_p = pl.BlockSpec((1, t), lambda h, i: (0, prev(i)))
    prow_c = pl.BlockSpec((1, t), lambda h, i: (0, i))
    return [q3, kp, kc, vp, vc, pcol, prow_p, prow_c]


def _stack(ref, first):
    return jnp.concatenate([ref[:, (first + g) * SLOT:(first + g + 1) * SLOT] for g in range(SWA_GROUP)], axis=0)


def _swa_logits(q3, kp, kc, pq, pkp, pkc, slope_ref, kvh, i, t):
    r = lax.broadcasted_iota(jnp.int32, (t, t), 0)
    c = lax.broadcasted_iota(jnp.int32, (t, t), 1)
    ok_c = c <= r
    ok_p = (c - r) > jnp.where(i > 0, 0, t)
    dist_p, dist_c = pq - pkp, pq - pkc
    s_p3 = _nt(q3, kp) * (HEAD_DIM ** -0.5)
    s_c3 = _nt(q3, kc) * (HEAD_DIM ** -0.5)
    out = []
    for g in range(SWA_GROUP):
        slope = slope_ref[kvh * SWA_GROUP + g]
        rows = slice(g * t, (g + 1) * t)
        out.append((jnp.where(ok_p, s_p3[rows] - slope * dist_p, NEG),
                    jnp.where(ok_c, s_c3[rows] - slope * dist_c, NEG)))
    return out


def _swa_attn_fwd(proj, pos_col, pos_row, slopes, sinks):
    rows = proj.shape[0]
    t = WINDOW
    hw = SWA_HEADS * SLOT

    def body(slope_ref, sink_ref, q_ref, kp_ref, kc_ref, vp_ref, vc_ref, pq_ref, pkp_ref, pkc_ref, o_ref, lse_ref):
        i = pl.program_id(1)
        for kv in range(SWA_PACK):
            kvh = pl.program_id(0) * SWA_PACK + kv
            ksl = slice(kv * SLOT, (kv + 1) * SLOT)
            logits = _swa_logits(_stack(q_ref, kv * SWA_GROUP), kp_ref[:, ksl], kc_ref[:, ksl], pq_ref[...],
                                 pkp_ref[...], pkc_ref[...], slope_ref, kvh, i, t)
            e_p, e_c, norm = [], [], []
            for g, (s_p, s_c) in enumerate(logits):
                sl = slice((kv * SWA_GROUP + g) * SLOT, (kv * SWA_GROUP + g + 1) * SLOT)
                sink = sink_ref[kvh * SWA_GROUP + g]
                m = jnp.maximum(jnp.maximum(jnp.max(s_p, axis=1, keepdims=True),
                                            jnp.max(s_c, axis=1, keepdims=True)), sink)
                ep, ec = jnp.exp(s_p - m), jnp.exp(s_c - m)
                l = jnp.sum(ep, axis=1, keepdims=True) + jnp.sum(ec, axis=1, keepdims=True) + jnp.exp(sink - m)
                e_p.append(ep.astype(BF16))
                e_c.append(ec.astype(BF16))
                norm.append(l)
                lse_ref[:, sl] = jnp.broadcast_to(m + jnp.log(l), (t, SLOT))
            acc = (_nn(jnp.concatenate(e_p, axis=0), vp_ref[:, ksl])
                   + _nn(jnp.concatenate(e_c, axis=0), vc_ref[:, ksl]))
            for g in range(SWA_GROUP):
                sl = slice((kv * SWA_GROUP + g) * SLOT, (kv * SWA_GROUP + g + 1) * SLOT)
                o_ref[:, sl] = (acc[g * t:(g + 1) * t] / norm[g]).astype(o_ref.dtype)

    smem = pl.BlockSpec(memory_space=pltpu.SMEM)
    out_spec = pl.BlockSpec((t, SWA_PACK * SWA_GROUP * SLOT), lambda h, i: (i, h))
    return pl.pallas_call(
        body, name="swa_attn_fwd", grid=(SWA_KV_HEADS // SWA_PACK, rows // t),
        in_specs=[smem, smem] + _swa_specs(t), out_specs=[out_spec, out_spec],
        out_shape=[jax.ShapeDtypeStruct((rows, hw), BF16), jax.ShapeDtypeStruct((rows, hw), F32)],
        compiler_params=_cparams(),
    )(slopes, sinks, proj, proj, proj, proj, proj, pos_col, pos_row, pos_row)


def _swa_attn_bwd(proj, o, do, lse, pos_col, pos_row, slopes, sinks, after):
    rows = proj.shape[0]
    t = WINDOW
    hw = SWA_HEADS * SLOT
    scale = HEAD_DIM ** -0.5

    def body(slope_ref, sink_ref, q_ref, kp_ref, kc_ref, vp_ref, vc_ref, pq_ref, pkp_ref, pkc_ref,
             o_ref, do_ref, lse_ref, after_ref, dq_ref, dk_ref, dv_ref, dsink_ref):
        i = pl.program_id(1)

        @pl.when(i == 0)
        def _():
            dk_ref[...] = jnp.zeros_like(dk_ref)
            dv_ref[...] = jnp.zeros_like(dv_ref)
            dsink_ref[...] = jnp.zeros_like(dsink_ref)

        r_c = pl.multiple_of(i * t, t)
        r_p = pl.multiple_of(jnp.maximum(i - 1, 0) * t, t)
        for kv in range(SWA_PACK):
            kvh = pl.program_id(0) * SWA_PACK + kv
            ksl = slice(kv * SLOT, (kv + 1) * SLOT)
            q3, do3 = _stack(q_ref, kv * SWA_GROUP), _stack(do_ref, kv * SWA_GROUP)
            logits = _swa_logits(q3, kp_ref[:, ksl], kc_ref[:, ksl], pq_ref[...], pkp_ref[...], pkc_ref[...],
                                 slope_ref, kvh, i, t)
            dp_p3, dp_c3 = _nt(do3, vp_ref[:, ksl]), _nt(do3, vc_ref[:, ksl])
            p_p, p_c, ds_p, ds_c = [], [], [], []
            for g, (s_p, s_c) in enumerate(logits):
                head = kv * SWA_GROUP + g
                sl = slice(head * SLOT, (head + 1) * SLOT)
                rws = slice(g * t, (g + 1) * t)
                lse_g = lse_ref[:, head * SLOT:head * SLOT + 1]
                pp, pc = jnp.exp(s_p - lse_g), jnp.exp(s_c - lse_g)
                delta = jnp.sum(do_ref[:, sl].astype(F32) * o_ref[:, sl].astype(F32), axis=1, keepdims=True)
                p_p.append(pp.astype(BF16))
                p_c.append(pc.astype(BF16))
                ds_p.append((pp * (dp_p3[rws] - delta)).astype(BF16))
                ds_c.append((pc * (dp_c3[rws] - delta)).astype(BF16))
                sink = sink_ref[kvh * SWA_GROUP + g]
                dsink = -jnp.sum(jnp.exp(sink - lse_g) * delta, axis=0, keepdims=True)
                dsink_ref[head * 8:(head + 1) * 8, :] += jnp.broadcast_to(dsink, (8, SLOT))
            p_p3, p_c3 = jnp.concatenate(p_p, axis=0), jnp.concatenate(p_c, axis=0)
            ds_p3, ds_c3 = jnp.concatenate(ds_p, axis=0), jnp.concatenate(ds_c, axis=0)
            dq3 = (_nn(ds_p3, kp_ref[:, ksl]) + _nn(ds_c3, kc_ref[:, ksl])) * scale
            for g in range(SWA_GROUP):
                head = kv * SWA_GROUP + g
                dq_ref[:, head * SLOT:(head + 1) * SLOT] = dq3[g * t:(g + 1) * t]
            dk_ref[pl.ds(r_c, t), ksl] += _tn(ds_c3, q3) * scale
            dv_ref[pl.ds(r_c, t), ksl] += _tn(p_c3, do3)
            dk_ref[pl.ds(r_p, t), ksl] += _tn(ds_p3, q3) * scale
            dv_ref[pl.ds(r_p, t), ksl] += _tn(p_p3, do3)

    smem = pl.BlockSpec(memory_space=pltpu.SMEM)
    qlike = pl.BlockSpec((t, SWA_PACK * SWA_GROUP * SLOT), lambda h, i: (i, h))
    kv_out = pl.BlockSpec((rows, SWA_PACK * SLOT), lambda h, i: (0, h))
    return pl.pallas_call(
        body, name="swa_attn_bwd", grid=(SWA_KV_HEADS // SWA_PACK, rows // t),
        in_specs=[smem, smem] + _swa_specs(t) + [qlike, qlike, qlike, pl.BlockSpec(memory_space=pl.ANY)],
        out_specs=[qlike, kv_out, kv_out,
                   pl.BlockSpec((SWA_PACK * SWA_GROUP * 8, SLOT), lambda h, i: (h, 0))],
        out_shape=[jax.ShapeDtypeStruct((rows, hw), F32), jax.ShapeDtypeStruct((rows, SWA_KV_HEADS * SLOT), F32),
                   jax.ShapeDtypeStruct((rows, SWA_KV_HEADS * SLOT), F32),
                   jax.ShapeDtypeStruct((SWA_HEADS * 8, SLOT), F32)],
        compiler_params=_cparams(),
    )(slopes, sinks, proj, proj, proj, proj, proj, pos_col, pos_row, pos_row, o, do, lse, after)


def _cross_attn_fwd(proj, qoff, kvmem):
    rows = proj.shape[0]
    t = min(TQ_CROSS, rows)

    def body(q_ref, k_ref, v_ref, o_ref):
        s = _nt(q_ref[...].astype(BF16), k_ref[...]) * (HEAD_DIM ** -0.5)
        e = jnp.exp(s - jnp.max(s, axis=1, keepdims=True))
        p = e / jnp.sum(e, axis=1, keepdims=True)
        o_ref[...] = _nn(p.astype(BF16), v_ref[...]).astype(o_ref.dtype)

    return pl.pallas_call(
        body, name="cross_attn_fwd", grid=(rows // t, MEM_HEADS),
        in_specs=[pl.BlockSpec((t, SLOT), lambda i, h: (i, qoff + h)),
                  pl.BlockSpec((N_MEM, SLOT), lambda i, h: (0, h)),
                  pl.BlockSpec((N_MEM, SLOT), lambda i, h: (0, MEM_HEADS + h))],
        out_specs=pl.BlockSpec((t, SLOT), lambda i, h: (i, h)),
        out_shape=jax.ShapeDtypeStruct((rows, MEM_HEADS * SLOT), BF16), compiler_params=_cparams(),
    )(proj, kvmem, kvmem)


def _cross_attn_bwd(proj, qoff, kvmem, do):
    rows = proj.shape[0]
    t = min(TQ_CROSS, rows)
    scale = HEAD_DIM ** -0.5

    def body(q_ref, k_ref, v_ref, do_ref, dq_ref, dk_ref, dv_ref):
        @pl.when(pl.program_id(1) == 0)
        def _():
            dk_ref[...] = jnp.zeros_like(dk_ref)
            dv_ref[...] = jnp.zeros_like(dv_ref)

        qv, kv, dov = q_ref[...].astype(BF16), k_ref[...], do_ref[...]
        s = _nt(qv, kv) * scale
        e = jnp.exp(s - jnp.max(s, axis=1, keepdims=True))
        p = e / jnp.sum(e, axis=1, keepdims=True)
        dp = _nt(dov, v_ref[...])
        ds = (p * (dp - jnp.sum(p * dp, axis=1, keepdims=True))).astype(BF16)
        dq_ref[...] = _nn(ds, kv) * scale
        dk_ref[...] += _tn(ds, qv) * scale
        dv_ref[...] += _tn(p.astype(BF16), dov)

    mem_out = pl.BlockSpec((N_MEM, SLOT), lambda h, i: (0, h))
    return pl.pallas_call(
        body, name="cross_attn_bwd", grid=(MEM_HEADS, rows // t),
        in_specs=[pl.BlockSpec((t, SLOT), lambda h, i: (i, qoff + h)),
                  pl.BlockSpec((N_MEM, SLOT), lambda h, i: (0, h)),
                  pl.BlockSpec((N_MEM, SLOT), lambda h, i: (0, MEM_HEADS + h)),
                  pl.BlockSpec((t, SLOT), lambda h, i: (i, h))],
        out_specs=[pl.BlockSpec((t, SLOT), lambda h, i: (i, h)), mem_out, mem_out],
        out_shape=[jax.ShapeDtypeStruct((rows, MEM_HEADS * SLOT), F32),
                   jax.ShapeDtypeStruct((N_MEM, MEM_HEADS * SLOT), F32),
                   jax.ShapeDtypeStruct((N_MEM, MEM_HEADS * SLOT), F32)],
        compiler_params=_cparams(),
    )(proj, kvmem, kvmem, do)


def _place():
    return lax.axis_index("x"), lax.axis_index("y"), lax.axis_index("c")


def _flip(v, bit):
    return 1 - v if bit else v


def _all_gather(blocks, name):
    nb = len(blocks)

    def body(*refs):
        x_refs, out_refs = refs[:nb], refs[nb:2 * nb]
        send_sems, recv_sems, local_sems = refs[2 * nb:]
        x, y, c = _place()
        me, sibling = (x, y, c), (x, y, 1 - c)
        chips = [(1 - x, y), (x, 1 - y), (1 - x, 1 - y)]

        def copy(b, k, blk, to, from_input=False):
            slot = out_refs[b].at[4 * blk[0] + 2 * blk[1] + blk[2]]
            return pltpu.make_async_remote_copy(
                src_ref=x_refs[b] if from_input else slot, dst_ref=slot,
                send_sem=send_sems.at[b, k], recv_sem=recv_sems.at[b, k],
                device_id=to, device_id_type=pl.DeviceIdType.MESH)

        mine = [pltpu.make_async_copy(x_refs[b], out_refs[b].at[4 * x + 2 * y + c], local_sems.at[b])
                for b in range(nb)]
        for cp in mine:
            cp.start()
        first = []
        for b in range(nb):
            first.append(copy(b, 0, me, sibling, from_input=True))
            first += [copy(b, 1 + n, me, (*chip, c), from_input=True) for n, chip in enumerate(chips)]
        for cp in first:
            cp.start()
        passed = []
        for n, chip in enumerate(chips):
            for b in range(nb):
                copy(b, 1 + n, (*chip, c), me).wait_recv()
                passed.append(copy(b, 4 + n, (*chip, c), sibling))
                passed[-1].start()
        for b in range(nb):
            copy(b, 0, sibling, me).wait_recv()
            for n, chip in enumerate(chips):
                copy(b, 4 + n, (*chip, 1 - c), me).wait_recv()
        for cp in first + passed:
            cp.wait_send()
        for cp in mine:
            cp.wait()

    any_spec = pl.BlockSpec(memory_space=pl.ANY)
    return pl.pallas_call(
        body, name=name, in_specs=[any_spec] * nb, out_specs=[any_spec] * nb,
        out_shape=[jax.ShapeDtypeStruct((N_DEV,) + blk.shape, blk.dtype) for blk in blocks],
        scratch_shapes=[pltpu.SemaphoreType.DMA((nb, 7)), pltpu.SemaphoreType.DMA((nb, 7)),
                        pltpu.SemaphoreType.DMA((nb,))],
    )(*blocks)


def _peers(x, y, c):
    out = []
    for n in range(1, N_DEV):
        peer = (_flip(x, n & 4), _flip(y, n & 2), _flip(c, n & 1))
        out.append((n - 1, peer, 4 * peer[0] + 2 * peer[1] + peer[2]))
    return out


_HBM = pl.BlockSpec(memory_space=pltpu.HBM)
_SEM = pl.BlockSpec(memory_space=pltpu.SEMAPHORE)


def _exchange_start(srcs, scatter, name, after=None):
    ns = len(srcs)
    lands = [lax.empty(s.shape if scatter else (N_DEV,) + s.shape, s.dtype) for s in srcs]

    def body(*refs):
        src_refs, land_refs = refs[:ns], refs[ns:2 * ns]
        pos = 2 * ns + (1 if after is not None else 0)
        send_sems, recv_sems, token = refs[pos], refs[pos + 1], refs[-1]
        x, y, c = _place()
        my_idx = 4 * x + 2 * y + c
        for col, peer, peer_idx in _peers(x, y, c):
            for b in range(ns):
                pltpu.make_async_remote_copy(
                    src_ref=src_refs[b].at[peer_idx] if scatter else src_refs[b], dst_ref=land_refs[b].at[my_idx],
                    send_sem=send_sems.at[b * (N_DEV - 1) + col], recv_sem=recv_sems.at[b * (N_DEV - 1) + col],
                    device_id=peer, device_id_type=pl.DeviceIdType.MESH).start()
        token[...] = jnp.zeros_like(token)

    args = [pltpu.with_memory_space_constraint(a, pltpu.HBM) for a in list(srcs) + lands]
    in_specs = [_HBM] * (2 * ns)
    if after is not None:
        args.append(after)
        in_specs.append(pl.BlockSpec(memory_space=pl.ANY))
    out = pl.pallas_call(
        body, name=name, in_specs=in_specs,
        out_specs=[_SEM, _SEM] + [_HBM] * (2 * ns) + [pl.BlockSpec(memory_space=pltpu.VMEM)],
        out_shape=[pltpu.SemaphoreType.DMA((ns * (N_DEV - 1),)), pltpu.SemaphoreType.DMA((ns * (N_DEV - 1),))]
        + [pltpu.HBM(a.shape, a.dtype) for a in list(srcs) + lands] + [jax.ShapeDtypeStruct((8, SLOT), F32)],
        input_output_aliases={k: 2 + k for k in range(2 * ns)},
        compiler_params=pltpu.CompilerParams(has_side_effects=pltpu.SideEffectType.DATAFLOW_SIDE_EFFECTING),
    )(*args)
    return (out[0], out[1], out[2:2 + ns], out[2 + ns:2 + 2 * ns], scatter), out[-1]


def _exchange_wait(handle, after, name):
    send_sems, recv_sems, srcs, lands, scatter = handle
    ns = len(srcs)

    def body(*refs):
        src_refs, land_refs = refs[:ns], refs[ns:2 * ns]
        send_ref, recv_ref = refs[2 * ns], refs[2 * ns + 1]
        x, y, c = _place()
        for col, peer, peer_idx in _peers(x, y, c):
            for b in range(ns):
                copy = pltpu.make_async_remote_copy(
                    src_ref=src_refs[b].at[peer_idx] if scatter else src_refs[b], dst_ref=land_refs[b].at[peer_idx],
                    send_sem=send_ref.at[b * (N_DEV - 1) + col], recv_sem=recv_ref.at[b * (N_DEV - 1) + col],
                    device_id=peer, device_id_type=pl.DeviceIdType.MESH)
                copy.wait_send()
                copy.wait_recv()

    out = pl.pallas_call(
        body, name=name, in_specs=[_HBM] * (2 * ns) + [_SEM, _SEM, pl.BlockSpec(memory_space=pl.ANY)],
        out_specs=[_HBM] * (2 * ns),
        out_shape=[pltpu.HBM(a.shape, a.dtype) for a in list(srcs) + list(lands)],
        input_output_aliases={k: k for k in range(2 * ns)},
        compiler_params=pltpu.CompilerParams(has_side_effects=pltpu.SideEffectType.DATAFLOW_SIDE_EFFECTING),
    )(*srcs, *lands, send_sems, recv_sems, after)
    my_idx = 4 * lax.axis_index("x") + 2 * lax.axis_index("y") + lax.axis_index("c")
    landed = []
    for src, land in zip(out[:ns], out[ns:]):
        own = lax.dynamic_index_in_dim(src, my_idx, 0, keepdims=True) if scatter else src[None]
        landed.append(lax.dynamic_update_index_in_dim(land, own, my_idx, 0))
    return landed


def _adamw(parts, w, m, v, name):
    lyr, rows, cols = w.shape
    assert len(parts) == lyr
    tr = ADAM_ROWS if cols > 512 else 2 * ADAM_ROWS
    while rows % tr:
        tr //= 2
    tr = min(tr, rows)

    def body(*refs):
        p_refs = refs[:lyr]
        w_ref, m_ref, v_ref, g_out, d_out, m_out, v_out = refs[lyr:]
        for k in range(lyr):
            @pl.when(pl.program_id(0) == k)
            def _(p_ref=p_refs[k]):
                g = p_ref[0].astype(F32)
                for s in range(1, N_DEV):
                    g = g + p_ref[s].astype(F32)
                m2 = ADAM_B1 * m_ref[...] + (1.0 - ADAM_B1) * g
                v2 = ADAM_B2 * v_ref[...] + (1.0 - ADAM_B2) * (g * g)
                m_hat = m2 / (1.0 - ADAM_B1 ** ADAM_STEP)
                v_hat = v2 / (1.0 - ADAM_B2 ** ADAM_STEP)
                g_out[...] = g
                d_out[...] = -ADAM_LR * (m_hat / (jnp.sqrt(v_hat) + ADAM_EPS) + ADAM_WD * w_ref[...])
                m_out[...] = m2
                v_out[...] = v2

    def part_spec(k):
        return pl.BlockSpec((N_DEV, tr, cols), lambda l, i: (0, jnp.where(l == k, i, 0), 0))

    spec = pl.BlockSpec((None, tr, cols), lambda l, i: (l, i, 0))
    shp = jax.ShapeDtypeStruct((lyr, rows, cols), F32)
    return pl.pallas_call(
        body, name=name, grid=(lyr, rows // tr),
        in_specs=[part_spec(k) for k in range(lyr)] + [spec, spec, spec],
        out_specs=[spec] * 4, out_shape=[shp] * 4, compiler_params=_cparams(),
    )(*parts, w, m, v)


def _pack(arrays, lanes, row_mult, dtype):
    flat = jnp.concatenate([a.reshape(-1).astype(dtype) for a in arrays])
    unit = lanes * row_mult
    total = -(-flat.shape[0] // unit) * unit
    return jnp.pad(flat, (0, total - flat.shape[0])).reshape(total // lanes, lanes)


def _unpack(packed, shapes):
    flat = packed.reshape(-1)
    out, off = [], 0
    for shp in shapes:
        n = 1
        for d in shp:
            n *= d
        out.append(flat[off:off + n].reshape(shp))
        off += n
    return out


def _pad_slots(w, axis):
    axis = axis % w.ndim
    n = w.shape[axis] // HEAD_DIM
    shp = w.shape[:axis] + (n, HEAD_DIM) + w.shape[axis + 1:]
    pad = [(0, 0)] * (w.ndim + 1)
    pad[axis + 1] = (0, SLOT - HEAD_DIM)
    return jnp.pad(w.reshape(shp), pad).reshape(w.shape[:axis] + (n * SLOT,) + w.shape[axis + 1:])


def _unpad_slots(w, axis, keep=HEAD_DIM):
    axis = axis % w.ndim
    n = w.shape[axis] // SLOT
    shp = w.shape[:axis] + (n, SLOT) + w.shape[axis + 1:]
    idx = [slice(None)] * (w.ndim + 1)
    idx[axis + 1] = slice(0, keep)
    return w.reshape(shp)[tuple(idx)].reshape(w.shape[:axis] + (n * keep,) + w.shape[axis + 1:])


def _mla_in_pad(w):
    z = functools.partial(jnp.zeros, dtype=w.dtype)
    rows = w.shape[0]
    return jnp.concatenate([w[:, :384], z((rows, 64)), w[:, 640:672], z((rows, 32)), w[:, 384:640],
                            _pad_slots(w[:, 672:], 1)], axis=1)


def _mla_in_unpad(d):
    return jnp.concatenate([d[:, :384], d[:, 512:768], d[:, 448:480], _unpad_slots(d[:, 768:], 1)], axis=1)


def _mla_uq_pad(w):
    return jnp.pad(w.reshape(w.shape[0], MLA_HEADS, MLA_QK), ((0, 0), (0, 0), (0, SLOT - MLA_QK))).reshape(
        w.shape[0], MLA_HEADS * SLOT)


def _mla_ukv_pad(w):
    w3 = w.reshape(w.shape[0], MLA_HEADS, 2 * HEAD_DIM)
    pad = ((0, 0), (0, 0), (0, SLOT - HEAD_DIM))
    k = jnp.pad(w3[:, :, :HEAD_DIM], pad).reshape(w.shape[0], -1)
    v = jnp.pad(w3[:, :, HEAD_DIM:], pad).reshape(w.shape[0], -1)
    return jnp.concatenate([k, v], axis=1)


def _mla_ukv_unpad(d):
    hw = MLA_HEADS * SLOT
    k = d[:, :hw].reshape(d.shape[0], MLA_HEADS, SLOT)[:, :, :HEAD_DIM]
    v = d[:, hw:].reshape(d.shape[0], MLA_HEADS, SLOT)[:, :, :HEAD_DIM]
    return jnp.concatenate([k, v], axis=2).reshape(d.shape[0], MLA_HEADS * 2 * HEAD_DIM)


def _join(gathered, axis):
    nd, a, b = gathered.shape
    if axis == 1:
        return gathered.reshape(nd * a, b)
    return gathered.transpose(1, 0, 2).reshape(a, nd * b)


def _split(full, axis):
    r, c = full.shape
    if axis == 1:
        return full.reshape(N_DEV, r // N_DEV, c).astype(BF16)
    return full.reshape(r, N_DEV, c // N_DEV).transpose(1, 0, 2).astype(BF16)


def kernel(x, mem, positions, attn_norm_g, mlp_norm_g, mem_norm_g, final_norm_g, mla_w_in, mla_q_norm_g, mla_kv_norm_g, mla_w_uq, mla_w_ukv, swa_w_in, swa_sinks, w_mem_kv, w_o, mlp_w_up, mlp_w_down, loss_target, m_attn_norm_g, m_mlp_norm_g, m_mem_norm_g, m_final_norm_g, m_mla_w_in, m_mla_q_norm_g, m_mla_kv_norm_g, m_mla_w_uq, m_mla_w_ukv, m_swa_w_in, m_swa_sinks, m_w_mem_kv, m_w_o, m_mlp_w_up, m_mlp_w_down, v_attn_norm_g, v_mlp_norm_g, v_mem_norm_g, v_final_norm_g, v_mla_w_in, v_mla_q_norm_g, v_mla_kv_norm_g, v_mla_w_uq, v_mla_w_ukv, v_swa_w_in, v_swa_sinks, v_w_mem_kv, v_w_o, v_mlp_w_up, v_mlp_w_down):
    given = dict(locals())
    seq = x.shape[1]
    x0 = x.reshape(seq, D_MODEL)
    tgt = loss_target.reshape(seq, D_MODEL)
    mem0 = mem.reshape(N_MEM, D_MODEL)
    pos = positions.reshape(seq).astype(F32)
    pos_col, pos_row = pos.reshape(seq, 1), pos.reshape(1, seq)

    def layer_names(i):
        mixer = ("mla_w_in", "mla_w_uq", "mla_w_ukv") if i % 2 == 0 else ("swa_w_in",)
        return [(n, i // 2) for n in mixer] + [(n, i) for n in ("w_mem_kv", "w_o", "mlp_w_up", "mlp_w_down")]

    def local_weights(names):
        return [given[n][l].astype(BF16) for n, l in names]

    first_attn, first_mlp = layer_names(0)[:-2], layer_names(0)[-2:]
    weights = [dict(zip([n for n, _ in first_attn], _all_gather(local_weights(first_attn), "gather_weights_first")))]
    coming_mlp, first_token = _exchange_start(local_weights(first_mlp), False, "gather_weights_start_0",
                                              after=weights[0]["w_o"])

    consts = _lane_consts()
    tabs = _rope_tables(pos_col, consts)
    slopes = 2.0 ** (-8.0 * (jnp.arange(SWA_HEADS, dtype=F32) + 1.0) / SWA_HEADS)

    mem_n = _rmsnorm_fwd(mem0, 0, D_MODEL, mem_norm_g, "rmsnorm_fwd_mem")

    saved = []
    xc = x0
    for i in range(DEPTH):
        j = i // 2
        wts = weights[i]
        s = {"x_in": xc}
        token = None
        if i + 1 < DEPTH:
            coming, token = _exchange_start(local_weights(layer_names(i + 1)), False,
                                            "gather_weights_start_%d" % (i + 1),
                                            after=first_token if i == 0 else wts["w_o"])
        hn = _rmsnorm_fwd(xc, 0, D_MODEL, attn_norm_g[i], "rmsnorm_fwd", after=token)
        if i % 2 == 0:
            w_in = _mla_in_pad(_join(wts["mla_w_in"], 1))
            w_uq = _mla_uq_pad(_join(wts["mla_w_uq"], 2))
            w_kv = _mla_ukv_pad(_join(wts["mla_w_ukv"], 2))
            proj = _mm(hn, w_in, "nn", F32, "mm_mla_in")
            cqn = _rmsnorm_fwd(proj, 0, MLA_Q_RANK, mla_q_norm_g[j], "rmsnorm_fwd_q")
            ckvn = _rmsnorm_fwd(proj, 2, MLA_KV_RANK, mla_kv_norm_g[j], "rmsnorm_fwd_kv")
            qraw = _mm(cqn, w_uq, "nn", F32, "mm_mla_uq")
            kvraw = _mm(ckvn, w_kv, "nn", F32, "mm_mla_ukv")
            q, k, v = _mla_rope_fwd(qraw, kvraw, proj, tabs)
            o, lse = _mla_attn_fwd(q, k, v)
            qoff = MLA_QOFF
            s.update(w_uq=w_uq, w_kv=w_kv, cqn=cqn, ckvn=ckvn, q=q, k=k, v=v)
        else:
            w_in = _pad_slots(_join(wts["swa_w_in"], 2), 1)
            proj = _mm(hn, w_in, "nn", BF16, "mm_swa_in")
            o, lse = _swa_attn_fwd(proj, pos_col, pos_row, slopes, swa_sinks[j])
            qoff = SWA_QOFF
        w_mem = _pad_slots(_join(wts["w_mem_kv"], 1), 1)
        w_out = _pad_slots(_join(wts["w_o"], 1), 0)
        w_o_mix, w_o_cross = w_out[:SWA_HEADS * SLOT], w_out[SWA_HEADS * SLOT:]
        kvmem = _mm(mem_n, w_mem, "nn", BF16, "mm_mem_kv")
        cross = _cross_attn_fwd(proj, qoff, kvmem)
        x1 = _mm(o, w_o_mix, "nn", F32, "mm_o_mix", res=xc)
        x1 = _mm(cross, w_o_cross, "nn", F32, "mm_o_cross", res=x1)
        hn2 = _rmsnorm_fwd(x1, 0, D_MODEL, mlp_norm_g[i], "rmsnorm_fwd")
        if i == 0:
            wts.update(zip([n for n, _ in first_mlp], _exchange_wait(coming_mlp, hn2, "gather_weights_wait_0")))
        act, act2 = _mm(hn2, wts["mlp_w_up"], "nn", BF16, "mm_mlp_up", epi="relu2", b_blk="cols")
        xc = _mm(act2, wts["mlp_w_down"], "nn", F32, "mm_mlp_down", res=x1, b_blk="rows")
        s.update(hn=hn, w_in=w_in, proj=proj, o=o, lse=lse, qoff=qoff, w_mem=w_mem, w_o_mix=w_o_mix,
                 w_o_cross=w_o_cross, kvmem=kvmem, cross=cross, x1=x1, hn2=hn2, act=act, act2=act2)
        saved.append(s)
        if i + 1 < DEPTH:
            got = _exchange_wait(coming, xc, "gather_weights_wait_%d" % (i + 1))
            weights.append(dict(zip([n for n, _ in layer_names(i + 1)], got)))

    dx, dx_b, dg_final, loss_part = _loss_head(xc, final_norm_g, tgt)
    loss = lax.psum(loss_part[0, 0], MESH_AXES)

    gains = {n: [None] * DEPTH for n in ("attn_norm_g", "mlp_norm_g")}
    for n in ("mla_q_norm_g", "mla_kv_norm_g", "swa_sinks"):
        gains[n] = [None] * 2
    leaving = {}
    token = None
    dmem_n = None
    for i in reversed(range(DEPTH)):
        j = i // 2
        s = saved[i]
        wts = weights[i]
        out = {}
        du = _mm(dx_b, wts["mlp_w_down"], "nt", BF16, "mm_mlp_down_dx", aux=s["act"], epi="mul2aux", b_blk="rows",
                 after=token)
        out["mlp_w_down"] = _mm(s["act2"], dx_b, "tn", BF16, "mm_mlp_down_dw", o_blk="rows")
        out["mlp_w_up"] = _mm(s["hn2"], du, "tn", BF16, "mm_mlp_up_dw", o_blk="cols")
        dx1, dx1_b, dg = _mm(du, wts["mlp_w_up"].transpose(0, 2, 1), "nn", F32, "mm_mlp_up_dx", b_blk="rows",
                             epi="normbwd", norm=(s["x1"], mlp_norm_g[i], dx))
        gains["mlp_norm_g"][i] = dg[0]

        do = _mm(dx1_b, s["w_o_mix"], "nt", BF16, "mm_o_mix_dx")
        dcross = _mm(dx1_b, s["w_o_cross"], "nt", BF16, "mm_o_cross_dx")
        dw_o = jnp.concatenate([_mm(s["o"], dx1_b, "tn", F32, "mm_o_mix_dw"),
                                _mm(s["cross"], dx1_b, "tn", F32, "mm_o_cross_dw")], axis=0)
        out["w_o"] = _split(_unpad_slots(dw_o, 0), 1)
        dqc, dkm, dvm = _cross_attn_bwd(s["proj"], s["qoff"], s["kvmem"], dcross)
        dkvmem = jnp.concatenate([dkm, dvm], axis=1).astype(BF16)
        out["w_mem_kv"] = _split(_unpad_slots(_mm(mem_n, dkvmem, "tn", F32, "mm_mem_kv_dw"), 1), 1)
        dmem_n = _mm(dkvmem, s["w_mem"], "nt", F32, "mm_mem_kv_dx" if dmem_n is None else "mm_mem_kv_dx_acc",
                     res=dmem_n)
        leaving[(i, "main")], token = _exchange_start([out[n] for n, _ in layer_names(i)[-4:]], True,
                                                      "exchange_grads_main_start_%d" % i)

        if i % 2 == 0:
            dq, dk, dv = _mla_attn_bwd(s["q"], s["k"], s["v"], s["o"], do, s["lse"], token)
            dqraw, dkv, dkr = _mla_rope_bwd(dq, dk, dv, tabs, consts)
            dcqn = _mm(dqraw, s["w_uq"], "nt", F32, "mm_mla_uq_dx")
            out["mla_w_uq"] = _split(_unpad_slots(_mm(s["cqn"], dqraw, "tn", F32, "mm_mla_uq_dw"), 1, MLA_QK), 2)
            dckvn = _mm(dkv, s["w_kv"], "nt", F32, "mm_mla_ukv_dx")
            out["mla_w_ukv"] = _split(_mla_ukv_unpad(_mm(s["ckvn"], dkv, "tn", F32, "mm_mla_ukv_dw")), 2)
            dcq, dg = _rmsnorm_bwd(s["proj"], 0, MLA_Q_RANK, mla_q_norm_g[j], dcqn, None, BF16, "rmsnorm_bwd_q")
            gains["mla_q_norm_g"][j] = dg[0]
            dckv, dg = _rmsnorm_bwd(s["proj"], 2, MLA_KV_RANK, mla_kv_norm_g[j], dckvn, None, BF16, "rmsnorm_bwd_kv")
            gains["mla_kv_norm_g"][j] = dg[0]
            dproj = jnp.concatenate([dcq, dkr.astype(BF16), dckv, dqc.astype(BF16)], axis=1)
            in_dx = "mm_mla_in_dx"
            out["mla_w_in"] = _split(_mla_in_unpad(_mm(s["hn"], dproj, "tn", F32, "mm_mla_in_dw")), 1)
        else:
            dq, dk, dv, dsink = _swa_attn_bwd(s["proj"], s["o"], do, s["lse"], pos_col, pos_row, slopes, swa_sinks[j],
                                              token)
            gains["swa_sinks"][j] = dsink[::8, 0]
            dproj = jnp.concatenate([dq, dk, dv, dqc], axis=1).astype(BF16)
            in_dx = "mm_swa_in_dx"
            out["swa_w_in"] = _split(_unpad_slots(_mm(s["hn"], dproj, "tn", F32, "mm_swa_in_dw"), 1), 2)
        dx, dx_b, dg = _mm(dproj, s["w_in"], "nt", F32, in_dx, epi="normbwd", norm=(s["x_in"], attn_norm_g[i], dx1))
        gains["attn_norm_g"][i] = dg[0]

        leaving[(i, "mixer")], token = _exchange_start([out[n] for n, _ in layer_names(i)[:-4]], True,
                                                       "exchange_grads_mixer_start_%d" % i)

    _, dg_mem = _rmsnorm_bwd(mem0, 0, D_MODEL, mem_norm_g, dmem_n, None, BF16, "rmsnorm_bwd_mem")
    gains = {n: jnp.stack(g) for n, g in gains.items()}
    gains["mem_norm_g"] = dg_mem[0]
    gains["final_norm_g"] = dg_final[0]

    result = {}

    def adamw_of(names, received):
        for n in names:
            parts = [received[(n, l)] for l in range(given[n].shape[0])]
            for kind, r in enumerate(_adamw(parts, given[n], given["m_" + n], given["v_" + n], "adamw_" + n)):
                result[(kind, n)] = r

    received = {}
    for i in reversed(range(DEPTH)):
        got = _exchange_wait(leaving[(i, "main")], dx, "exchange_grads_main_wait_%d" % i)
        received.update(zip(layer_names(i)[-4:], got))
    adamw_of(("mlp_w_up", "mlp_w_down", "w_o", "w_mem_kv"), received)
    for i in reversed(range(DEPTH)):
        got = _exchange_wait(leaving[(i, "mixer")], result[(0, "w_mem_kv")], "exchange_grads_mixer_wait_%d" % i)
        received.update(zip(layer_names(i)[:-4], got))
    adamw_of(("mla_w_in", "mla_w_uq", "mla_w_ukv", "swa_w_in"), received)

    rep_shapes = [given[n].shape for n in REPLICATED]
    rep_parts = _all_gather([_pack([gains[n] for n in REPLICATED], SLOT, 8, F32)], "gather_gain_grads")[0]
    rep_packed = [_pack([given[p + n] for n in REPLICATED], SLOT, 8, F32)[None] for p in ("", "m_", "v_")]
    for kind, r in enumerate(_adamw([rep_parts], *rep_packed, "adamw_gains")):
        for n, part in zip(REPLICATED, _unpack(r[0], rep_shapes)):
            result[(kind, n)] = part

    outs = [loss, dx.reshape(1, seq, D_MODEL)]
    for kind in range(4):
        outs += [result[(kind, n)] for n in WEIGHT_ORDER]
    return tuple(outs)
```

```python
import functools

import jax
import jax.numpy as jnp
from jax import lax
from jax.experimental import pallas as pl
from jax.experimental.pallas import tpu as pltpu

F32 = jnp.float32
BF16 = jnp.bfloat16

D_MODEL = 1024
D_FF = 4096
N_MEM = 256
DEPTH = 4
SLOT = 128
HEAD_DIM = 64
MLA_HEADS = 12
MLA_QK = 96
MLA_Q_RANK = 384
MLA_KV_RANK = 256
SWA_HEADS = 12
SWA_KV_HEADS = 4
SWA_GROUP = 3
MEM_HEADS = 4
WINDOW = 128
EPS = 1e-6
NEG = -1e30
ROPE_THETA = 10000.0
N_DEV = 8

ADAM_LR = 0.001
ADAM_B1 = 0.9
ADAM_B2 = 0.999
ADAM_EPS = 1e-08
ADAM_WD = 0.01
ADAM_STEP = 10

TM = 512
TQ_MLA = 1024
MLA_PACK = 2
SWA_PACK = 2
TQ_CROSS = 2048
MM_VMEM_BUDGET = 38 * 1024 * 1024
ADAM_ROWS = 128
VMEM_LIMIT = 56 * 1024 * 1024

MESH_AXES = ("x", "y", "c")

MLA_PAD_IN = 384 + SLOT + 256 + MEM_HEADS * SLOT
MLA_QOFF = (384 + SLOT + 256) // SLOT
SWA_PAD_IN = (SWA_HEADS + 2 * SWA_KV_HEADS + MEM_HEADS) * SLOT
SWA_QOFF = SWA_HEADS + 2 * SWA_KV_HEADS

SHARDED = (
    ("mla_w_in", 1), ("mla_w_uq", 2), ("mla_w_ukv", 2), ("swa_w_in", 2),
    ("w_mem_kv", 1), ("w_o", 1), ("mlp_w_up", 2), ("mlp_w_down", 1),
)
REPLICATED = ("attn_norm_g", "mlp_norm_g", "mem_norm_g", "final_norm_g",
              "mla_q_norm_g", "mla_kv_norm_g", "swa_sinks")
WEIGHT_ORDER = ("attn_norm_g", "mlp_norm_g", "mem_norm_g", "final_norm_g", "mla_w_in",
                "mla_q_norm_g", "mla_kv_norm_g", "mla_w_uq", "mla_w_ukv", "swa_w_in",
                "swa_sinks", "w_mem_kv", "w_o", "mlp_w_up", "mlp_w_down")


def _cparams():
    return pltpu.CompilerParams(vmem_limit_bytes=VMEM_LIMIT)


_DIMS = {"nn": (((1,), (0,)), ((), ())), "nt": (((1,), (1,)), ((), ())), "tn": (((0,), (0,)), ((), ()))}


def _mm_tiles(m, n, k, a_bytes, b_bytes, o_bytes, extra_bytes, tm_fixed, tn_fixed):
    best = None
    for tm in ([tm_fixed] if tm_fixed else [t for t in (4096, 2048, 1024, 512, 256, 128) if m % t == 0] or [m]):
        for tn in ([tn_fixed] if tn_fixed else [t for t in range(1024, 0, -SLOT) if n % t == 0] or [n]):
            need = 2 * (tm * k * a_bytes + k * tn * b_bytes + tm * tn * (o_bytes + extra_bytes))
            need += tm * tn * 4
            if need <= MM_VMEM_BUDGET and (best is None or tm * tn > best[0] * best[1]):
                best = (tm, tn)
    assert best is not None, (m, n, k)
    return best


def _mm(a, b, mode, out_dtype, name, res=None, aux=None, epi=None, b_blk=None, o_blk=None, after=None, norm=None):
    if b_blk is not None:
        nb, br, bc = b.shape
        b_shape = (nb * br, bc) if b_blk == "rows" else (br, nb * bc)
    else:
        b_shape = b.shape
    if mode == "nn":
        (m, k), (k2, n) = a.shape, b_shape
    elif mode == "nt":
        (m, k), (n, k2) = a.shape, b_shape
    else:
        (k, m), (k2, n) = a.shape, b_shape
    assert k == k2, (a.shape, b_shape, mode)
    k_blocked = b_blk is not None and (b_blk == "rows") == (mode != "nt")
    assert not (k_blocked and mode == "nt")
    tn_fixed = None
    if b_blk is not None and not k_blocked:
        tn_fixed = br if b_blk == "rows" else bc
    if o_blk == "cols":
        tn_fixed = n // N_DEV
    tm_fixed = m // N_DEV if o_blk == "rows" else None
    has_res, has_aux, has_norm = res is not None, aux is not None, epi == "normbwd"
    assert o_blk is None or not (has_res or has_aux or has_norm)
    n_out = 2 if epi == "relu2" else 1
    if has_norm:
        tn_fixed = n
        o_bytes, extra_bytes = 4 + 2, 4 + 4
    else:
        o_bytes = n_out * jnp.dtype(out_dtype).itemsize
        extra_bytes = (4 if has_res else 0) + (aux.dtype.itemsize if has_aux else 0)
    tm, tn = _mm_tiles(m, n, k, a.dtype.itemsize, b.dtype.itemsize, o_bytes, extra_bytes, tm_fixed, tn_fixed)
    dims = _DIMS[mode]
    if mode == "tn":
        a_spec = pl.BlockSpec((k, tm), lambda i, j: (0, i))
    else:
        a_spec = pl.BlockSpec((tm, k), lambda i, j: (i, 0))
    if b_blk is None:
        if mode == "nt":
            b_spec = pl.BlockSpec((tn, k), lambda i, j: (j, 0))
        else:
            b_spec = pl.BlockSpec((k, tn), lambda i, j: (0, j))
    elif k_blocked:
        b_spec = pl.BlockSpec((N_DEV, br, tn), lambda i, j: (0, 0, j))
    elif mode == "nt":
        b_spec = pl.BlockSpec((None, tn, k), lambda i, j: (j, 0, 0))
    else:
        b_spec = pl.BlockSpec((None, k, tn), lambda i, j: (j, 0, 0))
    if o_blk is None:
        o_spec = pl.BlockSpec((tm, tn), lambda i, j: (i, j))
        o_shape = (m, n)
    elif o_blk == "rows":
        o_spec = pl.BlockSpec((None, tm, tn), lambda i, j: (i, 0, j))
        o_shape = (N_DEV, tm, n)
    else:
        o_spec = pl.BlockSpec((None, tm, tn), lambda i, j: (j, i, 0))
        o_shape = (N_DEV, m, tn)

    def body(*refs):
        a_ref, b_ref = refs[0], refs[1]
        pos = 2
        res_ref = aux_ref = None
        if has_res:
            res_ref = refs[pos]
            pos += 1
        if has_aux:
            aux_ref = refs[pos]
            pos += 1
        if has_norm:
            x_ref, g_ref, dres_ref = refs[pos:pos + 3]
            pos += 3
        if after is not None:
            pos += 1
        outs = refs[pos:]
        bv = b_ref[...].reshape(k, tn) if k_blocked else b_ref[...]
        r = lax.dot_general(a_ref[...].astype(BF16), bv.astype(BF16), dims, preferred_element_type=F32)
        if epi == "relu2":
            r = jnp.maximum(r, 0.0)
            outs[0][...] = r.astype(outs[0].dtype)
            outs[1][...] = (r * r).astype(outs[1].dtype)
        elif has_norm:
            xv = x_ref[...]
            rs = lax.rsqrt(jnp.mean(xv * xv, axis=1, keepdims=True) + EPS)
            xh = xv * rs
            dxh = r * g_ref[...]
            dx = rs * (dxh - xh * jnp.mean(dxh * xh, axis=1, keepdims=True)) + dres_ref[...]
            outs[0][...] = dx
            outs[1][...] = dx.astype(BF16)

            @pl.when(pl.program_id(0) == 0)
            def _():
                outs[2][...] = jnp.zeros_like(outs[2])

            outs[2][...] += jnp.sum(r * xh, axis=0, keepdims=True)
        else:
            if epi == "mul2aux":
                r = r * (2.0 * aux_ref[...].astype(F32))
            if has_res:
                r = r + res_ref[...]
            outs[0][...] = r.astype(outs[0].dtype)

    in_specs = [a_spec, b_spec]
    args = [a, b]
    if has_res:
        in_specs.append(o_spec)
        args.append(res)
    if has_aux:
        in_specs.append(o_spec)
        args.append(aux)
    vec_spec = pl.BlockSpec((1, n), lambda i, j: (0, 0))
    if has_norm:
        in_specs += [o_spec, vec_spec, o_spec]
        args += [norm[0], norm[1].reshape(1, n), norm[2]]
    if after is not None:
        in_specs.append(pl.BlockSpec(memory_space=pl.ANY))
        args.append(after)
    if has_norm:
        out_specs = [o_spec, o_spec, vec_spec]
        out_shape = [jax.ShapeDtypeStruct(o_shape, F32), jax.ShapeDtypeStruct(o_shape, BF16),
                     jax.ShapeDtypeStruct((1, n), F32)]
    else:
        out_specs = [o_spec] * n_out
        out_shape = [jax.ShapeDtypeStruct(o_shape, out_dtype)] * n_out
    out = pl.pallas_call(
        body, name=name, grid=(m // tm, n // tn),
        in_specs=in_specs, out_specs=out_specs, out_shape=out_shape, compiler_params=_cparams(),
    )(*args)
    return out if len(out) > 1 else out[0]


def _rmsnorm_fwd(xarr, colblk, width, g, name, after=None):
    rows = xarr.shape[0]
    tm = min(TM, rows)

    def body(x_ref, g_ref, *rest):
        y_ref = rest[-1]
        x = x_ref[...].astype(F32)
        r = lax.rsqrt(jnp.mean(x * x, axis=1, keepdims=True) + EPS)
        y_ref[...] = (x * r * g_ref[...]).astype(y_ref.dtype)

    in_specs = [pl.BlockSpec((tm, width), lambda i: (i, colblk)), pl.BlockSpec((1, width), lambda i: (0, 0))]
    args = [xarr, g.reshape(1, width)]
    if after is not None:
        in_specs.append(pl.BlockSpec(memory_space=pl.ANY))
        args.append(after)
    return pl.pallas_call(
        body, name=name, grid=(rows // tm,), in_specs=in_specs,
        out_specs=pl.BlockSpec((tm, width), lambda i: (i, 0)),
        out_shape=jax.ShapeDtypeStruct((rows, width), BF16), compiler_params=_cparams(),
    )(*args)


def _rmsnorm_bwd(xarr, colblk, width, g, dy, dres, out_dtype, name):
    rows = xarr.shape[0]
    tm = min(TM, rows)
    has_res = dres is not None

    def body(*refs):
        x_ref, g_ref, dy_ref = refs[0], refs[1], refs[2]
        dres_ref = refs[3] if has_res else None
        dx_ref, dg_ref = refs[-2], refs[-1]
        x = x_ref[...].astype(F32)
        dyv = dy_ref[...].astype(F32)
        r = lax.rsqrt(jnp.mean(x * x, axis=1, keepdims=True) + EPS)
        xh = x * r
        dxh = dyv * g_ref[...]
        dx = r * (dxh - xh * jnp.mean(dxh * xh, axis=1, keepdims=True))
        if has_res:
            dx = dx + dres_ref[...]
        dx_ref[...] = dx.astype(dx_ref.dtype)

        @pl.when(pl.program_id(0) == 0)
        def _():
            dg_ref[...] = jnp.zeros_like(dg_ref)

        dg_ref[...] += jnp.sum(dyv * xh, axis=0, keepdims=True)

    row_spec = pl.BlockSpec((tm, width), lambda i: (i, 0))
    vec_spec = pl.BlockSpec((1, width), lambda i: (0, 0))
    in_specs = [pl.BlockSpec((tm, width), lambda i: (i, colblk)), vec_spec, row_spec]
    args = [xarr, g.reshape(1, width), dy]
    if has_res:
        in_specs.append(row_spec)
        args.append(dres)
    return pl.pallas_call(
        body, name=name, grid=(rows // tm,), in_specs=in_specs, out_specs=[row_spec, vec_spec],
        out_shape=[jax.ShapeDtypeStruct((rows, width), out_dtype), jax.ShapeDtypeStruct((1, width), F32)],
        compiler_params=_cparams(),
    )(*args)


def _loss_head(x, g, tgt):
    rows, width = x.shape
    tm = min(TM, rows)

    def body(x_ref, g_ref, t_ref, dx_ref, dxb_ref, dg_ref, loss_ref):
        xv = x_ref[...]
        gv = g_ref[...]
        r = lax.rsqrt(jnp.mean(xv * xv, axis=1, keepdims=True) + EPS)
        xh = xv * r
        err = xh * gv - t_ref[...]
        part = 0.5 * jnp.sum(jnp.mean(err * err, axis=1, keepdims=True), axis=0, keepdims=True)
        dyv = err * (1.0 / width)
        dxh = dyv * gv
        dxv = r * (dxh - xh * jnp.mean(dxh * xh, axis=1, keepdims=True))
        dx_ref[...] = dxv
        dxb_ref[...] = dxv.astype(BF16)

        @pl.when(pl.program_id(0) == 0)
        def _():
            dg_ref[...] = jnp.zeros_like(dg_ref)
            loss_ref[...] = jnp.zeros_like(loss_ref)

        dg_ref[...] += jnp.sum(dyv * xh, axis=0, keepdims=True)
        loss_ref[...] += jnp.broadcast_to(part, loss_ref.shape)

    row_spec = pl.BlockSpec((tm, width), lambda i: (i, 0))
    vec_spec = pl.BlockSpec((1, width), lambda i: (0, 0))
    return pl.pallas_call(
        body, name="loss_head", grid=(rows // tm,), in_specs=[row_spec, vec_spec, row_spec],
        out_specs=[row_spec, row_spec, vec_spec, pl.BlockSpec((1, SLOT), lambda i: (0, 0))],
        out_shape=[jax.ShapeDtypeStruct((rows, width), F32), jax.ShapeDtypeStruct((rows, width), BF16),
                   jax.ShapeDtypeStruct((1, width), F32), jax.ShapeDtypeStruct((1, SLOT), F32)],
        compiler_params=_cparams(),
    )(x, g.reshape(1, width), tgt)


def _lane_consts():
    half = 16
    inv = ROPE_THETA ** (-(jnp.arange(half, dtype=F32) * 2.0) / 32)
    lane = jnp.arange(SLOT)
    first = (lane >= 64) & (lane < 80)
    second = (lane >= 80) & (lane < 96)
    inv_lane = jnp.where(first | second, inv[(lane - 64) % half], 0.0)
    rows = [inv_lane, (lane < 64).astype(F32), first.astype(F32), second.astype(F32)]
    rows += [jnp.zeros((SLOT,), F32)] * 4
    return jnp.stack(rows).astype(F32)


def _rope_tables(pos_col, consts):
    rows = pos_col.shape[0]
    tm = min(TM, rows)

    def body(p_ref, k_ref, c_ref, s1_ref, s2_ref):
        ang = p_ref[...] * k_ref[0:1, :]
        cos, sin = jnp.cos(ang), jnp.sin(ang)
        first, second = k_ref[2:3, :], k_ref[3:4, :]
        c_ref[...] = k_ref[1:2, :] + (first + second) * cos
        s1_ref[...] = -first * sin
        s2_ref[...] = second * sin

    spec = pl.BlockSpec((tm, SLOT), lambda i: (i, 0))
    shp = jax.ShapeDtypeStruct((rows, SLOT), F32)
    return pl.pallas_call(
        body, name="rope_tables", grid=(rows // tm,),
        in_specs=[pl.BlockSpec((tm, 1), lambda i: (i, 0)), pl.BlockSpec((8, SLOT), lambda i: (0, 0))],
        out_specs=[spec, spec, spec], out_shape=[shp, shp, shp], compiler_params=_cparams(),
    )(pos_col, consts)


def _rot(xv, c, s1, s2):
    return xv * c + pltpu.roll(xv, SLOT - 16, 1) * s1 + pltpu.roll(xv, 16, 1) * s2


def _rot_t(dy, c, s1, s2):
    return dy * c + pltpu.roll(dy * s1, 16, 1) + pltpu.roll(dy * s2, SLOT - 16, 1)


def _mla_rope_fwd(qraw, kvraw, proj, tabs):
    rows = qraw.shape[0]
    tm = min(256, rows)
    hw = MLA_HEADS * SLOT

    def body(q_ref, kv_ref, kr_ref, c_ref, s1_ref, s2_ref, qo, ko, vo):
        c, s1, s2 = c_ref[...], s1_ref[...], s2_ref[...]
        kr = _rot(kr_ref[...], c, s1, s2)
        for h in range(MLA_HEADS):
            sl = slice(h * SLOT, (h + 1) * SLOT)
            qo[:, sl] = _rot(q_ref[:, sl], c, s1, s2).astype(BF16)
            ko[:, sl] = (kv_ref[:, sl] + kr).astype(BF16)
            vo[:, sl] = kv_ref[:, hw + h * SLOT:hw + (h + 1) * SLOT].astype(BF16)

    tab = pl.BlockSpec((tm, SLOT), lambda i: (i, 0))
    wide = pl.BlockSpec((tm, hw), lambda i: (i, 0))
    shp = jax.ShapeDtypeStruct((rows, hw), BF16)
    return pl.pallas_call(
        body, name="mla_rope_fwd", grid=(rows // tm,),
        in_specs=[wide, pl.BlockSpec((tm, 2 * hw), lambda i: (i, 0)), pl.BlockSpec((tm, SLOT), lambda i: (i, 3)),
                  tab, tab, tab],
        out_specs=[wide, wide, wide], out_shape=[shp, shp, shp], compiler_params=_cparams(),
    )(qraw, kvraw, proj, *tabs)


def _mla_rope_bwd(dq, dk, dv, tabs, consts):
    rows = dq.shape[0]
    tm = min(256, rows)
    hw = MLA_HEADS * SLOT

    def body(dq_ref, dk_ref, dv_ref, c_ref, s1_ref, s2_ref, k_ref, dqo, dkvo, dkro):
        c, s1, s2 = c_ref[...], s1_ref[...], s2_ref[...]
        ksum = jnp.zeros((tm, SLOT), F32)
        for h in range(MLA_HEADS):
            sl = slice(h * SLOT, (h + 1) * SLOT)
            dqo[:, sl] = _rot_t(dq_ref[:, sl], c, s1, s2).astype(BF16)
            dkh = dk_ref[:, sl]
            ksum = ksum + dkh
            dkvo[:, sl] = dkh.astype(BF16)
            dkvo[:, hw + h * SLOT:hw + (h + 1) * SLOT] = dv_ref[:, sl].astype(BF16)
        dkro[...] = _rot_t(ksum, c, s1, s2) * (k_ref[2:3, :] + k_ref[3:4, :])

    tab = pl.BlockSpec((tm, SLOT), lambda i: (i, 0))
    wide = pl.BlockSpec((tm, hw), lambda i: (i, 0))
    return pl.pallas_call(
        body, name="mla_rope_bwd", grid=(rows // tm,),
        in_specs=[wide, wide, wide, tab, tab, tab, pl.BlockSpec((8, SLOT), lambda i: (0, 0))],
        out_specs=[wide, pl.BlockSpec((tm, 2 * hw), lambda i: (i, 0)), tab],
        out_shape=[jax.ShapeDtypeStruct((rows, hw), BF16), jax.ShapeDtypeStruct((rows, 2 * hw), BF16),
                   jax.ShapeDtypeStruct((rows, SLOT), F32)],
        compiler_params=_cparams(),
    )(dq, dk, dv, *tabs, consts)


def _nt(a, b):
    return lax.dot_general(a, b, _DIMS["nt"], preferred_element_type=F32)


def _tn(a, b):
    return lax.dot_general(a, b, _DIMS["tn"], preferred_element_type=F32)


def _nn(a, b):
    return lax.dot_general(a, b, _DIMS["nn"], preferred_element_type=F32)


def _causal(rows, keys):
    shp = (rows.stop - rows.start, keys.stop - keys.start)
    return (keys.start + lax.broadcasted_iota(jnp.int32, shp, 1)
            <= rows.start + lax.broadcasted_iota(jnp.int32, shp, 0))


def _mla_attn_fwd(q, k, v):
    rows = q.shape[0]
    t = min(TQ_MLA, rows)
    nt = rows // t
    scale = MLA_QK ** -0.5
    wide = MLA_PACK * SLOT

    def body(q_ref, k_ref, v_ref, o_ref, lse_ref, m_sc, l_sc, acc_sc):
        i, j = pl.program_id(1), pl.program_id(2)

        @pl.when(j == 0)
        def _():
            m_sc[...] = jnp.full_like(m_sc, NEG)
            l_sc[...] = jnp.zeros_like(l_sc)
            acc_sc[...] = jnp.zeros_like(acc_sc)

        def step(diagonal):
            for hh in range(MLA_PACK):
                sl = slice(hh * SLOT, (hh + 1) * SLOT)
                s = _nt(q_ref[:, sl], k_ref[:, sl]) * scale
                if diagonal:
                    s = jnp.where(_causal(slice(0, t), slice(0, t)), s, NEG)
                m_prev = m_sc[hh]
                m_new = jnp.maximum(m_prev, jnp.max(s, axis=1, keepdims=True))
                p = jnp.exp(s - m_new)
                alpha = jnp.exp(m_prev - m_new)
                l_new = alpha * l_sc[hh] + jnp.sum(p, axis=1, keepdims=True)
                acc = alpha * acc_sc[:, sl] + _nn(p.astype(BF16), v_ref[:, sl])
                if diagonal:
                    o_ref[:, sl] = (acc / l_new).astype(o_ref.dtype)
                    lse_ref[:, sl] = jnp.broadcast_to(m_new + jnp.log(l_new), (t, SLOT))
                else:
                    m_sc[hh] = m_new
                    l_sc[hh] = l_new
                    acc_sc[:, sl] = acc

        @pl.when(j < i)
        def _():
            step(False)

        @pl.when(j == i)
        def _():
            step(True)

    q_spec = pl.BlockSpec((t, wide), lambda h, i, j: (i, h))
    kv_spec = pl.BlockSpec((t, wide), lambda h, i, j: (jnp.minimum(j, i), h))
    return pl.pallas_call(
        body, name="mla_attn_fwd", grid=(MLA_HEADS // MLA_PACK, nt, nt),
        in_specs=[q_spec, kv_spec, kv_spec], out_specs=[q_spec, q_spec],
        out_shape=[jax.ShapeDtypeStruct(q.shape, BF16), jax.ShapeDtypeStruct(q.shape, F32)],
        scratch_shapes=[pltpu.VMEM((MLA_PACK, t, 1), F32), pltpu.VMEM((MLA_PACK, t, 1), F32),
                        pltpu.VMEM((t, wide), F32)],
        compiler_params=_cparams(),
    )(q, k, v)


def _mla_attn_bwd(q, k, v, o, do, lse, after):
    rows = q.shape[0]
    t = min(TQ_MLA, rows)
    nt = rows // t
    scale = MLA_QK ** -0.5
    wide = MLA_PACK * SLOT

    def body(q_ref, k_ref, v_ref, o_ref, do_ref, lse_ref, after_ref, dq_ref, dk_ref, dv_ref, dk_sc, dv_sc):
        j, i = pl.program_id(1), pl.program_id(2)

        @pl.when((j == 0) & (i == 0))
        def _():
            dq_ref[...] = jnp.zeros_like(dq_ref)

        @pl.when(i == 0)
        def _():
            dk_sc[...] = jnp.zeros_like(dk_sc)
            dv_sc[...] = jnp.zeros_like(dv_sc)

        def chunk(hh, rows, keys, masked):
            sl = slice(hh * SLOT, (hh + 1) * SLOT)
            n_rows = rows.stop - rows.start
            qv, kv, dov = q_ref[rows, sl], k_ref[keys, sl], do_ref[rows, sl]
            s = _nt(qv, kv) * scale
            if masked:
                s = jnp.where(_causal(rows, keys), s, NEG)
            p = jnp.exp(s - lse_ref[rows, hh * SLOT:hh * SLOT + 1])
            delta = jnp.sum(dov.astype(F32) * o_ref[rows, sl].astype(F32), axis=1, keepdims=True)
            dp = _nt(dov, v_ref[keys, sl])
            ds = (p * (dp - delta) * scale).astype(BF16)
            dv_sc[keys, sl] += _tn(p.astype(BF16), dov)
            dk_sc[keys, sl] += _tn(ds, qv)
            r0 = pl.multiple_of(i * t + rows.start, n_rows)
            dq_ref[pl.ds(r0, n_rows), sl] += _nn(ds, kv)

        @pl.when(i > j)
        def _():
            for hh in range(MLA_PACK):
                chunk(hh, slice(0, t), slice(0, t), False)

        @pl.when(i == j)
        def _():
            for hh in range(MLA_PACK):
                chunk(hh, slice(0, t), slice(0, t // 2), True)
                chunk(hh, slice(t // 2, t), slice(t // 2, t), True)

        @pl.when(i == nt - 1)
        def _():
            dk_ref[...] = dk_sc[...]
            dv_ref[...] = dv_sc[...]

    q_spec = pl.BlockSpec((t, wide), lambda h, j, i: (jnp.maximum(i, j), h))
    kv_spec = pl.BlockSpec((t, wide), lambda h, j, i: (j, h))
    head_spec = pl.BlockSpec((rows, wide), lambda h, j, i: (0, h))
    shp = jax.ShapeDtypeStruct(q.shape, F32)
    return pl.pallas_call(
        body, name="mla_attn_bwd", grid=(MLA_HEADS // MLA_PACK, nt, nt),
        in_specs=[q_spec, kv_spec, kv_spec, q_spec, q_spec, q_spec, pl.BlockSpec(memory_space=pl.ANY)],
        out_specs=[head_spec, kv_spec, kv_spec], out_shape=[shp, shp, shp],
        scratch_shapes=[pltpu.VMEM((t, wide), F32), pltpu.VMEM((t, wide), F32)],
        compiler_params=_cparams(),
    )(q, k, v, o, do, lse, after)


def _swa_specs(t):
    def prev(i):
        return jnp.maximum(i - 1, 0)
    kw = SWA_PACK * SLOT
    k0, v0 = SWA_HEADS // SWA_PACK, (SWA_HEADS + SWA_KV_HEADS) // SWA_PACK
    q3 = pl.BlockSpec((t, SWA_PACK * SWA_GROUP * SLOT), lambda h, i: (i, h))
    kp = pl.BlockSpec((t, kw), lambda h, i: (prev(i), k0 + h))
    kc = pl.BlockSpec((t, kw), lambda h, i: (i, k0 + h))
    vp = pl.BlockSpec((t, kw), lambda h, i: (prev(i), v0 + h))
    vc = pl.BlockSpec((t, kw), lambda h, i: (i, v0 + h))
    pcol = pl.BlockSpec((t, 1), lambda h, i: (i, 0))
    prow_p = pl.BlockSpec((1, t), lambda h, i: (0, prev(i)))
    prow_c = pl.BlockSpec((1, t), lambda h, i: (0, i))
    return [q3, kp, kc, vp, vc, pcol, prow_p, prow_c]


def _stack(ref, first):
    return jnp.concatenate([ref[:, (first + g) * SLOT:(first + g + 1) * SLOT] for g in range(SWA_GROUP)], axis=0)


def _swa_logits(q3, kp, kc, pq, pkp, pkc, slope_ref, kvh, i, t):
    r = lax.broadcasted_iota(jnp.int32, (t, t), 0)
    c = lax.broadcasted_iota(jnp.int32, (t, t), 1)
    ok_c = c <= r
    ok_p = (c - r) > jnp.where(i > 0, 0, t)
    dist_p, dist_c = pq - pkp, pq - pkc
    s_p3 = _nt(q3, kp) * (HEAD_DIM ** -0.5)
    s_c3 = _nt(q3, kc) * (HEAD_DIM ** -0.5)
    out = []
    for g in range(SWA_GROUP):
        slope = slope_ref[kvh * SWA_GROUP + g]
        rows = slice(g * t, (g + 1) * t)
        out.append((jnp.where(ok_p, s_p3[rows] - slope * dist_p, NEG),
                    jnp.where(ok_c, s_c3[rows] - slope * dist_c, NEG)))
    return out


def _swa_attn_fwd(proj, pos_col, pos_row, slopes, sinks):
    rows = proj.shape[0]
    t = WINDOW
    hw = SWA_HEADS * SLOT

    def body(slope_ref, sink_ref, q_ref, kp_ref, kc_ref, vp_ref, vc_ref, pq_ref, pkp_ref, pkc_ref, o_ref, lse_ref):
        i = pl.program_id(1)
        for kv in range(SWA_PACK):
            kvh = pl.program_id(0) * SWA_PACK + kv
            ksl = slice(kv * SLOT, (kv + 1) * SLOT)
            logits = _swa_logits(_stack(q_ref, kv * SWA_GROUP), kp_ref[:, ksl], kc_ref[:, ksl], pq_ref[...],
                                 pkp_ref[...], pkc_ref[...], slope_ref, kvh, i, t)
            e_p, e_c, norm = [], [], []
            for g, (s_p, s_c) in enumerate(logits):
                sl = slice((kv * SWA_GROUP + g) * SLOT, (kv * SWA_GROUP + g + 1) * SLOT)
                sink = sink_ref[kvh * SWA_GROUP + g]
                m = jnp.maximum(jnp.maximum(jnp.max(s_p, axis=1, keepdims=True),
                                            jnp.max(s_c, axis=1, keepdims=True)), sink)
                ep, ec = jnp.exp(s_p - m), jnp.exp(s_c - m)
                l = jnp.sum(ep, axis=1, keepdims=True) + jnp.sum(ec, axis=1, keepdims=True) + jnp.exp(sink - m)
                e_p.append(ep.astype(BF16))
                e_c.append(ec.astype(BF16))
                norm.append(l)
                lse_ref[:, sl] = jnp.broadcast_to(m + jnp.log(l), (t, SLOT))
            acc = (_nn(jnp.concatenate(e_p, axis=0), vp_ref[:, ksl])
                   + _nn(jnp.concatenate(e_c, axis=0), vc_ref[:, ksl]))
            for g in range(SWA_GROUP):
                sl = slice((kv * SWA_GROUP + g) * SLOT, (kv * SWA_GROUP + g + 1) * SLOT)
                o_ref[:, sl] = (acc[g * t:(g + 1) * t] / norm[g]).astype(o_ref.dtype)

    smem = pl.BlockSpec(memory_space=pltpu.SMEM)
    out_spec = pl.BlockSpec((t, SWA_PACK * SWA_GROUP * SLOT), lambda h, i: (i, h))
    return pl.pallas_call(
        body, name="swa_attn_fwd", grid=(SWA_KV_HEADS // SWA_PACK, rows // t),
        in_specs=[smem, smem] + _swa_specs(t), out_specs=[out_spec, out_spec],
        out_shape=[jax.ShapeDtypeStruct((rows, hw), BF16), jax.ShapeDtypeStruct((rows, hw), F32)],
        compiler_params=_cparams(),
    )(slopes, sinks, proj, proj, proj, proj, proj, pos_col, pos_row, pos_row)


def _swa_attn_bwd(proj, o, do, lse, pos_col, pos_row, slopes, sinks, after):
    rows = proj.shape[0]
    t = WINDOW
    hw = SWA_HEADS * SLOT
    scale = HEAD_DIM ** -0.5

    def body(slope_ref, sink_ref, q_ref, kp_ref, kc_ref, vp_ref, vc_ref, pq_ref, pkp_ref, pkc_ref,
             o_ref, do_ref, lse_ref, after_ref, dq_ref, dk_ref, dv_ref, dsink_ref):
        i = pl.program_id(1)

        @pl.when(i == 0)
        def _():
            dk_ref[...] = jnp.zeros_like(dk_ref)
            dv_ref[...] = jnp.zeros_like(dv_ref)
            dsink_ref[...] = jnp.zeros_like(dsink_ref)

        r_c = pl.multiple_of(i * t, t)
        r_p = pl.multiple_of(jnp.maximum(i - 1, 0) * t, t)
        for kv in range(SWA_PACK):
            kvh = pl.program_id(0) * SWA_PACK + kv
            ksl = slice(kv * SLOT, (kv + 1) * SLOT)
            q3, do3 = _stack(q_ref, kv * SWA_GROUP), _stack(do_ref, kv * SWA_GROUP)
            logits = _swa_logits(q3, kp_ref[:, ksl], kc_ref[:, ksl], pq_ref[...], pkp_ref[...], pkc_ref[...],
                                 slope_ref, kvh, i, t)
            dp_p3, dp_c3 = _nt(do3, vp_ref[:, ksl]), _nt(do3, vc_ref[:, ksl])
            p_p, p_c, ds_p, ds_c = [], [], [], []
            for g, (s_p, s_c) in enumerate(logits):
                head = kv * SWA_GROUP + g
                sl = slice(head * SLOT, (head + 1) * SLOT)
                rws = slice(g * t, (g + 1) * t)
                lse_g = lse_ref[:, head * SLOT:head * SLOT + 1]
                pp, pc = jnp.exp(s_p - lse_g), jnp.exp(s_c - lse_g)
                delta = jnp.sum(do_ref[:, sl].astype(F32) * o_ref[:, sl].astype(F32), axis=1, keepdims=True)
                p_p.append(pp.astype(BF16))
                p_c.append(pc.astype(BF16))
                ds_p.append((pp * (dp_p3[rws] - delta)).astype(BF16))
                ds_c.append((pc * (dp_c3[rws] - delta)).astype(BF16))
                sink = sink_ref[kvh * SWA_GROUP + g]
                dsink = -jnp.sum(jnp.exp(sink - lse_g) * delta, axis=0, keepdims=True)
                dsink_ref[head * 8:(head + 1) * 8, :] += jnp.broadcast_to(dsink, (8, SLOT))
            p_p3, p_c3 = jnp.concatenate(p_p, axis=0), jnp.concatenate(p_c, axis=0)
            ds_p3, ds_c3 = jnp.concatenate(ds_p, axis=0), jnp.concatenate(ds_c, axis=0)
            dq3 = (_nn(ds_p3, kp_ref[:, ksl]) + _nn(ds_c3, kc_ref[:, ksl])) * scale
            for g in range(SWA_GROUP):
                head = kv * SWA_GROUP + g
                dq_ref[:, head * SLOT:(head + 1) * SLOT] = dq3[g * t:(g + 1) * t]
            dk_ref[pl.ds(r_c, t), ksl] += _tn(ds_c3, q3) * scale
            dv_ref[pl.ds(r_c, t), ksl] += _tn(p_c3, do3)
            dk_ref[pl.ds(r_p, t), ksl] += _tn(ds_p3, q3) * scale
            dv_ref[pl.ds(r_p, t), ksl] += _tn(p_p3, do3)

    smem = pl.BlockSpec(memory_space=pltpu.SMEM)
    qlike = pl.BlockSpec((t, SWA_PACK * SWA_GROUP * SLOT), lambda h, i: (i, h))
    kv_out = pl.BlockSpec((rows, SWA_PACK * SLOT), lambda h, i: (0, h))
    return pl.pallas_call(
        body, name="swa_attn_bwd", grid=(SWA_KV_HEADS // SWA_PACK, rows // t),
        in_specs=[smem, smem] + _swa_specs(t) + [qlike, qlike, qlike, pl.BlockSpec(memory_space=pl.ANY)],
        out_specs=[qlike, kv_out, kv_out,
                   pl.BlockSpec((SWA_PACK * SWA_GROUP * 8, SLOT), lambda h, i: (h, 0))],
        out_shape=[jax.ShapeDtypeStruct((rows, hw), F32), jax.ShapeDtypeStruct((rows, SWA_KV_HEADS * SLOT), F32),
                   jax.ShapeDtypeStruct((rows, SWA_KV_HEADS * SLOT), F32),
                   jax.ShapeDtypeStruct((SWA_HEADS * 8, SLOT), F32)],
        compiler_params=_cparams(),
    )(slopes, sinks, proj, proj, proj, proj, proj, pos_col, pos_row, pos_row, o, do, lse, after)


def _cross_attn_fwd(proj, qoff, kvmem):
    rows = proj.shape[0]
    t = min(TQ_CROSS, rows)

    def body(q_ref, k_ref, v_ref, o_ref):
        s = _nt(q_ref[...].astype(BF16), k_ref[...]) * (HEAD_DIM ** -0.5)
        e = jnp.exp(s - jnp.max(s, axis=1, keepdims=True))
        p = e / jnp.sum(e, axis=1, keepdims=True)
        o_ref[...] = _nn(p.astype(BF16), v_ref[...]).astype(o_ref.dtype)

    return pl.pallas_call(
        body, name="cross_attn_fwd", grid=(rows // t, MEM_HEADS),
        in_specs=[pl.BlockSpec((t, SLOT), lambda i, h: (i, qoff + h)),
                  pl.BlockSpec((N_MEM, SLOT), lambda i, h: (0, h)),
                  pl.BlockSpec((N_MEM, SLOT), lambda i, h: (0, MEM_HEADS + h))],
        out_specs=pl.BlockSpec((t, SLOT), lambda i, h: (i, h)),
        out_shape=jax.ShapeDtypeStruct((rows, MEM_HEADS * SLOT), BF16), compiler_params=_cparams(),
    )(proj, kvmem, kvmem)


def _cross_attn_bwd(proj, qoff, kvmem, do):
    rows = proj.shape[0]
    t = min(TQ_CROSS, rows)
    scale = HEAD_DIM ** -0.5

    def body(q_ref, k_ref, v_ref, do_ref, dq_ref, dk_ref, dv_ref):
        @pl.when(pl.program_id(1) == 0)
        def _():
            dk_ref[...] = jnp.zeros_like(dk_ref)
            dv_ref[...] = jnp.zeros_like(dv_ref)

        qv, kv, dov = q_ref[...].astype(BF16), k_ref[...], do_ref[...]
        s = _nt(qv, kv) * scale
        e = jnp.exp(s - jnp.max(s, axis=1, keepdims=True))
        p = e / jnp.sum(e, axis=1, keepdims=True)
        dp = _nt(dov, v_ref[...])
        ds = (p * (dp - jnp.sum(p * dp, axis=1, keepdims=True))).astype(BF16)
        dq_ref[...] = _nn(ds, kv) * scale
        dk_ref[...] += _tn(ds, qv) * scale
        dv_ref[...] += _tn(p.astype(BF16), dov)

    mem_out = pl.BlockSpec((N_MEM, SLOT), lambda h, i: (0, h))
    return pl.pallas_call(
        body, name="cross_attn_bwd", grid=(MEM_HEADS, rows // t),
        in_specs=[pl.BlockSpec((t, SLOT), lambda h, i: (i, qoff + h)),
                  pl.BlockSpec((N_MEM, SLOT), lambda h, i: (0, h)),
                  pl.BlockSpec((N_MEM, SLOT), lambda h, i: (0, MEM_HEADS + h)),
                  pl.BlockSpec((t, SLOT), lambda h, i: (i, h))],
        out_specs=[pl.BlockSpec((t, SLOT), lambda h, i: (i, h)), mem_out, mem_out],
        out_shape=[jax.ShapeDtypeStruct((rows, MEM_HEADS * SLOT), F32),
                   jax.ShapeDtypeStruct((N_MEM, MEM_HEADS * SLOT), F32),
                   jax.ShapeDtypeStruct((N_MEM, MEM_HEADS * SLOT), F32)],
        compiler_params=_cparams(),
    )(proj, kvmem, kvmem, do)


def _place():
    return lax.axis_index("x"), lax.axis_index("y"), lax.axis_index("c")


def _flip(v, bit):
    return 1 - v if bit else v


def _all_gather(blocks, name):
    nb = len(blocks)

    def body(*refs):
        x_refs, out_refs = refs[:nb], refs[nb:2 * nb]
        send_sems, recv_sems, local_sems = refs[2 * nb:]
        x, y, c = _place()
        me, sibling = (x, y, c), (x, y, 1 - c)
        chips = [(1 - x, y), (x, 1 - y), (1 - x, 1 - y)]

        def copy(b, k, blk, to, from_input=False):
            slot = out_refs[b].at[4 * blk[0] + 2 * blk[1] + blk[2]]
            return pltpu.make_async_remote_copy(
                src_ref=x_refs[b] if from_input else slot, dst_ref=slot,
                send_sem=send_sems.at[b, k], recv_sem=recv_sems.at[b, k],
                device_id=to, device_id_type=pl.DeviceIdType.MESH)

        mine = [pltpu.make_async_copy(x_refs[b], out_refs[b].at[4 * x + 2 * y + c], local_sems.at[b])
                for b in range(nb)]
        for cp in mine:
            cp.start()
        first = []
        for b in range(nb):
            first.append(copy(b, 0, me, sibling, from_input=True))
            first += [copy(b, 1 + n, me, (*chip, c), from_input=True) for n, chip in enumerate(chips)]
        for cp in first:
            cp.start()
        passed = []
        for n, chip in enumerate(chips):
            for b in range(nb):
                copy(b, 1 + n, (*chip, c), me).wait_recv()
                passed.append(copy(b, 4 + n, (*chip, c), sibling))
                passed[-1].start()
        for b in range(nb):
            copy(b, 0, sibling, me).wait_recv()
            for n, chip in enumerate(chips):
                copy(b, 4 + n, (*chip, 1 - c), me).wait_recv()
        for cp in first + passed:
            cp.wait_send()
        for cp in mine:
            cp.wait()

    any_spec = pl.BlockSpec(memory_space=pl.ANY)
    return pl.pallas_call(
        body, name=name, in_specs=[any_spec] * nb, out_specs=[any_spec] * nb,
        out_shape=[jax.ShapeDtypeStruct((N_DEV,) + blk.shape, blk.dtype) for blk in blocks],
        scratch_shapes=[pltpu.SemaphoreType.DMA((nb, 7)), pltpu.SemaphoreType.DMA((nb, 7)),
                        pltpu.SemaphoreType.DMA((nb,))],
    )(*blocks)


def _peers(x, y, c):
    out = []
    for n in range(1, N_DEV):
        peer = (_flip(x, n & 4), _flip(y, n & 2), _flip(c, n & 1))
        out.append((n - 1, peer, 4 * peer[0] + 2 * peer[1] + peer[2]))
    return out


_HBM = pl.BlockSpec(memory_space=pltpu.HBM)
_SEM = pl.BlockSpec(memory_space=pltpu.SEMAPHORE)


def _exchange_start(srcs, scatter, name, after=None):
    ns = len(srcs)
    lands = [lax.empty(s.shape if scatter else (N_DEV,) + s.shape, s.dtype) for s in srcs]

    def body(*refs):
        src_refs, land_refs = refs[:ns], refs[ns:2 * ns]
        pos = 2 * ns + (1 if after is not None else 0)
        send_sems, recv_sems, token = refs[pos], refs[pos + 1], refs[-1]
        x, y, c = _place()
        my_idx = 4 * x + 2 * y + c
        for col, peer, peer_idx in _peers(x, y, c):
            for b in range(ns):
                pltpu.make_async_remote_copy(
                    src_ref=src_refs[b].at[peer_idx] if scatter else src_refs[b], dst_ref=land_refs[b].at[my_idx],
                    send_sem=send_sems.at[b * (N_DEV - 1) + col], recv_sem=recv_sems.at[b * (N_DEV - 1) + col],
                    device_id=peer, device_id_type=pl.DeviceIdType.MESH).start()
        token[...] = jnp.zeros_like(token)

    args = [pltpu.with_memory_space_constraint(a, pltpu.HBM) for a in list(srcs) + lands]
    in_specs = [_HBM] * (2 * ns)
    if after is not None:
        args.append(after)
        in_specs.append(pl.BlockSpec(memory_space=pl.ANY))
    out = pl.pallas_call(
        body, name=name, in_specs=in_specs,
        out_specs=[_SEM, _SEM] + [_HBM] * (2 * ns) + [pl.BlockSpec(memory_space=pltpu.VMEM)],
        out_shape=[pltpu.SemaphoreType.DMA((ns * (N_DEV - 1),)), pltpu.SemaphoreType.DMA((ns * (N_DEV - 1),))]
        + [pltpu.HBM(a.shape, a.dtype) for a in list(srcs) + lands] + [jax.ShapeDtypeStruct((8, SLOT), F32)],
        input_output_aliases={k: 2 + k for k in range(2 * ns)},
        compiler_params=pltpu.CompilerParams(has_side_effects=pltpu.SideEffectType.DATAFLOW_SIDE_EFFECTING),
    )(*args)
    return (out[0], out[1], out[2:2 + ns], out[2 + ns:2 + 2 * ns], scatter), out[-1]


def _exchange_wait(handle, after, name):
    send_sems, recv_sems, srcs, lands, scatter = handle
    ns = len(srcs)

    def body(*refs):
        src_refs, land_refs = refs[:ns], refs[ns:2 * ns]
        send_ref, recv_ref, own_sems = refs[2 * ns], refs[2 * ns + 1], refs[-1]
        x, y, c = _place()
        my_idx = 4 * x + 2 * y + c
        own = [pltpu.make_async_copy(src_refs[b].at[my_idx] if scatter else src_refs[b], land_refs[b].at[my_idx],
                                     own_sems.at[b]) for b in range(ns)]
        for cp in own:
            cp.start()
        for col, peer, peer_idx in _peers(x, y, c):
            for b in range(ns):
                copy = pltpu.make_async_remote_copy(
                    src_ref=src_refs[b].at[peer_idx] if scatter else src_refs[b], dst_ref=land_refs[b].at[peer_idx],
                    send_sem=send_ref.at[b * (N_DEV - 1) + col], recv_sem=recv_ref.at[b * (N_DEV - 1) + col],
                    device_id=peer, device_id_type=pl.DeviceIdType.MESH)
                copy.wait_send()
                copy.wait_recv()
        for cp in own:
            cp.wait()

    out = pl.pallas_call(
        body, name=name, in_specs=[_HBM] * (2 * ns) + [_SEM, _SEM, pl.BlockSpec(memory_space=pl.ANY)],
        out_specs=[_HBM] * (2 * ns),
        out_shape=[pltpu.HBM(a.shape, a.dtype) for a in list(srcs) + list(lands)],
        input_output_aliases={k: k for k in range(2 * ns)},
        scratch_shapes=[pltpu.SemaphoreType.DMA((ns,))],
        compiler_params=pltpu.CompilerParams(has_side_effects=pltpu.SideEffectType.DATAFLOW_SIDE_EFFECTING),
    )(*srcs, *lands, send_sems, recv_sems, after)
    return list(out[ns:])


def _adamw(parts, w, m, v, name):
    lyr, rows, cols = w.shape
    assert len(parts) == lyr
    tr = ADAM_ROWS if cols > 512 else 2 * ADAM_ROWS
    while rows % tr:
        tr //= 2
    tr = min(tr, rows)

    def body(*refs):
        p_refs = refs[:lyr]
        w_ref, m_ref, v_ref, g_out, d_out, m_out, v_out = refs[lyr:]
        for k in range(lyr):
            @pl.when(pl.program_id(0) == k)
            def _(p_ref=p_refs[k]):
                g = p_ref[0].astype(F32)
                for s in range(1, N_DEV):
                    g = g + p_ref[s].astype(F32)
                m2 = ADAM_B1 * m_ref[...] + (1.0 - ADAM_B1) * g
                v2 = ADAM_B2 * v_ref[...] + (1.0 - ADAM_B2) * (g * g)
                m_hat = m2 / (1.0 - ADAM_B1 ** ADAM_STEP)
                v_hat = v2 / (1.0 - ADAM_B2 ** ADAM_STEP)
                g_out[...] = g
                d_out[...] = -ADAM_LR * (m_hat / (jnp.sqrt(v_hat) + ADAM_EPS) + ADAM_WD * w_ref[...])
                m_out[...] = m2
                v_out[...] = v2

    def part_spec(k):
        return pl.BlockSpec((N_DEV, tr, cols), lambda l, i: (0, jnp.where(l == k, i, 0), 0))

    spec = pl.BlockSpec((None, tr, cols), lambda l, i: (l, i, 0))
    shp = jax.ShapeDtypeStruct((lyr, rows, cols), F32)
    return pl.pallas_call(
        body, name=name, grid=(lyr, rows // tr),
        in_specs=[part_spec(k) for k in range(lyr)] + [spec, spec, spec],
        out_specs=[spec] * 4, out_shape=[shp] * 4, compiler_params=_cparams(),
    )(*parts, w, m, v)


def _pack(arrays, lanes, row_mult, dtype):
    flat = jnp.concatenate([a.reshape(-1).astype(dtype) for a in arrays])
    unit = lanes * row_mult
    total = -(-flat.shape[0] // unit) * unit
    return jnp.pad(flat, (0, total - flat.shape[0])).reshape(total // lanes, lanes)


def _unpack(packed, shapes):
    flat = packed.reshape(-1)
    out, off = [], 0
    for shp in shapes:
        n = 1
        for d in shp:
            n *= d
        out.append(flat[off:off + n].reshape(shp))
        off += n
    return out


def _pad_slots(w, axis):
    axis = axis % w.ndim
    n = w.shape[axis] // HEAD_DIM
    shp = w.shape[:axis] + (n, HEAD_DIM) + w.shape[axis + 1:]
    pad = [(0, 0)] * (w.ndim + 1)
    pad[axis + 1] = (0, SLOT - HEAD_DIM)
    return jnp.pad(w.reshape(shp), pad).reshape(w.shape[:axis] + (n * SLOT,) + w.shape[axis + 1:])


def _unpad_slots(w, axis, keep=HEAD_DIM):
    axis = axis % w.ndim
    n = w.shape[axis] // SLOT
    shp = w.shape[:axis] + (n, SLOT) + w.shape[axis + 1:]
    idx = [slice(None)] * (w.ndim + 1)
    idx[axis + 1] = slice(0, keep)
    return w.reshape(shp)[tuple(idx)].reshape(w.shape[:axis] + (n * keep,) + w.shape[axis + 1:])


def _mla_in_pad(w):
    z = functools.partial(jnp.zeros, dtype=w.dtype)
    rows = w.shape[0]
    return jnp.concatenate([w[:, :384], z((rows, 64)), w[:, 640:672], z((rows, 32)), w[:, 384:640],
                            _pad_slots(w[:, 672:], 1)], axis=1)


def _mla_in_unpad(d):
    return jnp.concatenate([d[:, :384], d[:, 512:768], d[:, 448:480], _unpad_slots(d[:, 768:], 1)], axis=1)


def _mla_uq_pad(w):
    return jnp.pad(w.reshape(w.shape[0], MLA_HEADS, MLA_QK), ((0, 0), (0, 0), (0, SLOT - MLA_QK))).reshape(
        w.shape[0], MLA_HEADS * SLOT)


def _mla_ukv_pad(w):
    w3 = w.reshape(w.shape[0], MLA_HEADS, 2 * HEAD_DIM)
    pad = ((0, 0), (0, 0), (0, SLOT - HEAD_DIM))
    k = jnp.pad(w3[:, :, :HEAD_DIM], pad).reshape(w.shape[0], -1)
    v = jnp.pad(w3[:, :, HEAD_DIM:], pad).reshape(w.shape[0], -1)
    return jnp.concatenate([k, v], axis=1)


def _mla_ukv_unpad(d):
    hw = MLA_HEADS * SLOT
    k = d[:, :hw].reshape(d.shape[0], MLA_HEADS, SLOT)[:, :, :HEAD_DIM]
    v = d[:, hw:].reshape(d.shape[0], MLA_HEADS, SLOT)[:, :, :HEAD_DIM]
    return jnp.concatenate([k, v], axis=2).reshape(d.shape[0], MLA_HEADS * 2 * HEAD_DIM)


def _join(gathered, axis):
    nd, a, b = gathered.shape
    if axis == 1:
        return gathered.reshape(nd * a, b)
    return gathered.transpose(1, 0, 2).reshape(a, nd * b)


def _split(full, axis):
    r, c = full.shape
    if axis == 1:
        return full.reshape(N_DEV, r // N_DEV, c).astype(BF16)
    return full.reshape(r, N_DEV, c // N_DEV).transpose(1, 0, 2).astype(BF16)


def kernel(x, mem, positions, attn_norm_g, mlp_norm_g, mem_norm_g, final_norm_g, mla_w_in, mla_q_norm_g, mla_kv_norm_g, mla_w_uq, mla_w_ukv, swa_w_in, swa_sinks, w_mem_kv, w_o, mlp_w_up, mlp_w_down, loss_target, m_attn_norm_g, m_mlp_norm_g, m_mem_norm_g, m_final_norm_g, m_mla_w_in, m_mla_q_norm_g, m_mla_kv_norm_g, m_mla_w_uq, m_mla_w_ukv, m_swa_w_in, m_swa_sinks, m_w_mem_kv, m_w_o, m_mlp_w_up, m_mlp_w_down, v_attn_norm_g, v_mlp_norm_g, v_mem_norm_g, v_final_norm_g, v_mla_w_in, v_mla_q_norm_g, v_mla_kv_norm_g, v_mla_w_uq, v_mla_w_ukv, v_swa_w_in, v_swa_sinks, v_w_mem_kv, v_w_o, v_mlp_w_up, v_mlp_w_down):
    given = dict(locals())
    seq = x.shape[1]
    x0 = x.reshape(seq, D_MODEL)
    tgt = loss_target.reshape(seq, D_MODEL)
    mem0 = mem.reshape(N_MEM, D_MODEL)
    pos = positions.reshape(seq).astype(F32)
    pos_col, pos_row = pos.reshape(seq, 1), pos.reshape(1, seq)

    def layer_names(i):
        mixer = ("mla_w_in", "mla_w_uq", "mla_w_ukv") if i % 2 == 0 else ("swa_w_in",)
        return [(n, i // 2) for n in mixer] + [(n, i) for n in ("w_mem_kv", "w_o", "mlp_w_up", "mlp_w_down")]

    def local_weights(names):
        return [given[n][l].astype(BF16) for n, l in names]

    first_attn, first_mlp = layer_names(0)[:-2], layer_names(0)[-2:]
    weights = [dict(zip([n for n, _ in first_attn], _all_gather(local_weights(first_attn), "gather_weights_first")))]
    coming_mlp, first_token = _exchange_start(local_weights(first_mlp), False, "gather_weights_start_0",
                                              after=weights[0]["w_o"])

    consts = _lane_consts()
    tabs = _rope_tables(pos_col, consts)
    slopes = 2.0 ** (-8.0 * (jnp.arange(SWA_HEADS, dtype=F32) + 1.0) / SWA_HEADS)

    mem_n = _rmsnorm_fwd(mem0, 0, D_MODEL, mem_norm_g, "rmsnorm_fwd_mem")

    saved = []
    xc = x0
    for i in range(DEPTH):
        j = i // 2
        wts = weights[i]
        s = {"x_in": xc}
        token = None
        if i + 1 < DEPTH:
            coming, token = _exchange_start(local_weights(layer_names(i + 1)), False,
                                            "gather_weights_start_%d" % (i + 1),
                                            after=first_token if i == 0 else wts["w_o"])
        hn = _rmsnorm_fwd(xc, 0, D_MODEL, attn_norm_g[i], "rmsnorm_fwd", after=token)
        if i % 2 == 0:
            w_in = _mla_in_pad(_join(wts["mla_w_in"], 1))
            w_uq = _mla_uq_pad(_join(wts["mla_w_uq"], 2))
            w_kv = _mla_ukv_pad(_join(wts["mla_w_ukv"], 2))
            proj = _mm(hn, w_in, "nn", F32, "mm_mla_in")
            cqn = _rmsnorm_fwd(proj, 0, MLA_Q_RANK, mla_q_norm_g[j], "rmsnorm_fwd_q")
            ckvn = _rmsnorm_fwd(proj, 2, MLA_KV_RANK, mla_kv_norm_g[j], "rmsnorm_fwd_kv")
            qraw = _mm(cqn, w_uq, "nn", F32, "mm_mla_uq")
            kvraw = _mm(ckvn, w_kv, "nn", F32, "mm_mla_ukv")
            q, k, v = _mla_rope_fwd(qraw, kvraw, proj, tabs)
            o, lse = _mla_attn_fwd(q, k, v)
            qoff = MLA_QOFF
            s.update(w_uq=w_uq, w_kv=w_kv, cqn=cqn, ckvn=ckvn, q=q, k=k, v=v)
        else:
            w_in = _pad_slots(_join(wts["swa_w_in"], 2), 1)
            proj = _mm(hn, w_in, "nn", BF16, "mm_swa_in")
            o, lse = _swa_attn_fwd(proj, pos_col, pos_row, slopes, swa_sinks[j])
            qoff = SWA_QOFF
        w_mem = _pad_slots(_join(wts["w_mem_kv"], 1), 1)
        w_out = _pad_slots(_join(wts["w_o"], 1), 0)
        w_o_mix, w_o_cross = w_out[:SWA_HEADS * SLOT], w_out[SWA_HEADS * SLOT:]
        kvmem = _mm(mem_n, w_mem, "nn", BF16, "mm_mem_kv")
        cross = _cross_attn_fwd(proj, qoff, kvmem)
        x1 = _mm(o, w_o_mix, "nn", F32, "mm_o_mix", res=xc)
        x1 = _mm(cross, w_o_cross, "nn", F32, "mm_o_cross", res=x1)
        hn2 = _rmsnorm_fwd(x1, 0, D_MODEL, mlp_norm_g[i], "rmsnorm_fwd")
        if i == 0:
            wts.update(zip([n for n, _ in first_mlp], _exchange_wait(coming_mlp, hn2, "gather_weights_wait_0")))
        act, act2 = _mm(hn2, wts["mlp_w_up"], "nn", BF16, "mm_mlp_up", epi="relu2", b_blk="cols")
        xc = _mm(act2, wts["mlp_w_down"], "nn", F32, "mm_mlp_down", res=x1, b_blk="rows")
        s.update(hn=hn, w_in=w_in, proj=proj, o=o, lse=lse, qoff=qoff, w_mem=w_mem, w_o_mix=w_o_mix,
                 w_o_cross=w_o_cross, kvmem=kvmem, cross=cross, x1=x1, hn2=hn2, act=act, act2=act2)
        saved.append(s)
        if i + 1 < DEPTH:
            got = _exchange_wait(coming, xc, "gather_weights_wait_%d" % (i + 1))
            weights.append(dict(zip([n for n, _ in layer_names(i + 1)], got)))

    dx, dx_b, dg_final, loss_part = _loss_head(xc, final_norm_g, tgt)
    loss = lax.psum(loss_part[0, 0], MESH_AXES)

    gains = {n: [None] * DEPTH for n in ("attn_norm_g", "mlp_norm_g")}
    for n in ("mla_q_norm_g", "mla_kv_norm_g", "swa_sinks"):
        gains[n] = [None] * 2
    leaving = {}
    token = None
    dmem_n = None
    for i in reversed(range(DEPTH)):
        j = i // 2
        s = saved[i]
        wts = weights[i]
        out = {}
        du = _mm(dx_b, wts["mlp_w_down"], "nt", BF16, "mm_mlp_down_dx", aux=s["act"], epi="mul2aux", b_blk="rows",
                 after=token)
        out["mlp_w_down"] = _mm(s["act2"], dx_b, "tn", BF16, "mm_mlp_down_dw", o_blk="rows")
        out["mlp_w_up"] = _mm(s["hn2"], du, "tn", BF16, "mm_mlp_up_dw", o_blk="cols")
        dx1, dx1_b, dg = _mm(du, wts["mlp_w_up"].transpose(0, 2, 1), "nn", F32, "mm_mlp_up_dx", b_blk="rows",
                             epi="normbwd", norm=(s["x1"], mlp_norm_g[i], dx))
        gains["mlp_norm_g"][i] = dg[0]

        do = _mm(dx1_b, s["w_o_mix"], "nt", BF16, "mm_o_mix_dx")
        dcross = _mm(dx1_b, s["w_o_cross"], "nt", BF16, "mm_o_cross_dx")
        dw_o = jnp.concatenate([_mm(s["o"], dx1_b, "tn", F32, "mm_o_mix_dw"),
                                _mm(s["cross"], dx1_b, "tn", F32, "mm_o_cross_dw")], axis=0)
        out["w_o"] = _split(_unpad_slots(dw_o, 0), 1)
        dqc, dkm, dvm = _cross_attn_bwd(s["proj"], s["qoff"], s["kvmem"], dcross)
        dkvmem = jnp.concatenate([dkm, dvm], axis=1).astype(BF16)
        out["w_mem_kv"] = _split(_unpad_slots(_mm(mem_n, dkvmem, "tn", F32, "mm_mem_kv_dw"), 1), 1)
        dmem_n = _mm(dkvmem, s["w_mem"], "nt", F32, "mm_mem_kv_dx" if dmem_n is None else "mm_mem_kv_dx_acc",
                     res=dmem_n)
        leaving[(i, "main")], token = _exchange_start([out[n] for n, _ in layer_names(i)[-4:]], True,
                                                      "exchange_grads_main_start_%d" % i)

        if i % 2 == 0:
            dq, dk, dv = _mla_attn_bwd(s["q"], s["k"], s["v"], s["o"], do, s["lse"], token)
            dqraw, dkv, dkr = _mla_rope_bwd(dq, dk, dv, tabs, consts)
            dcqn = _mm(dqraw, s["w_uq"], "nt", F32, "mm_mla_uq_dx")
            out["mla_w_uq"] = _split(_unpad_slots(_mm(s["cqn"], dqraw, "tn", F32, "mm_mla_uq_dw"), 1, MLA_QK), 2)
            dckvn = _mm(dkv, s["w_kv"], "nt", F32, "mm_mla_ukv_dx")
            out["mla_w_ukv"] = _split(_mla_ukv_unpad(_mm(s["ckvn"], dkv, "tn", F32, "mm_mla_ukv_dw")), 2)
            dcq, dg = _rmsnorm_bwd(s["proj"], 0, MLA_Q_RANK, mla_q_norm_g[j], dcqn, None, BF16, "rmsnorm_bwd_q")
            gains["mla_q_norm_g"][j] = dg[0]
            dckv, dg = _rmsnorm_bwd(s["proj"], 2, MLA_KV_RANK, mla_kv_norm_g[j], dckvn, None, BF16, "rmsnorm_bwd_kv")
            gains["mla_kv_norm_g"][j] = dg[0]
            dproj = jnp.concatenate([dcq, dkr.astype(BF16), dckv, dqc.astype(BF16)], axis=1)
            in_dx = "mm_mla_in_dx"
            out["mla_w_in"] = _split(_mla_in_unpad(_mm(s["hn"], dproj, "tn", F32, "mm_mla_in_dw")), 1)
        else:
            dq, dk, dv, dsink = _swa_attn_bwd(s["proj"], s["o"], do, s["lse"], pos_col, pos_row, slopes, swa_sinks[j],
                                              token)
            gains["swa_sinks"][j] = dsink[::8, 0]
            dproj = jnp.concatenate([dq, dk, dv, dqc], axis=1).astype(BF16)
            in_dx = "mm_swa_in_dx"
            out["swa_w_in"] = _split(_unpad_slots(_mm(s["hn"], dproj, "tn", F32, "mm_swa_in_dw"), 1), 2)
        dx, dx_b, dg = _mm(dproj, s["w_in"], "nt", F32, in_dx, epi="normbwd", norm=(s["x_in"], attn_norm_g[i], dx1))
        gains["attn_norm_g"][i] = dg[0]

        leaving[(i, "mixer")], token = _exchange_start([out[n] for n, _ in layer_names(i)[:-4]], True,
                                                       "exchange_grads_mixer_start_%d" % i)

    _, dg_mem = _rmsnorm_bwd(mem0, 0, D_MODEL, mem_norm_g, dmem_n, None, BF16, "rmsnorm_bwd_mem")
    gains = {n: jnp.stack(g) for n, g in gains.items()}
    gains["mem_norm_g"] = dg_mem[0]
    gains["final_norm_g"] = dg_final[0]

    result = {}

    def adamw_of(names, received):
        for n in names:
            parts = [received[(n, l)] for l in range(given[n].shape[0])]
            for kind, r in enumerate(_adamw(parts, given[n], given["m_" + n], given["v_" + n], "adamw_" + n)):
                result[(kind, n)] = r

    received = {}
    for i in reversed(range(DEPTH)):
        got = _exchange_wait(leaving[(i, "main")], dx, "exchange_grads_main_wait_%d" % i)
        received.update(zip(layer_names(i)[-4:], got))
    adamw_of(("mlp_w_up", "mlp_w_down", "w_o", "w_mem_kv"), received)
    for i in reversed(range(DEPTH)):
        got = _exchange_wait(leaving[(i, "mixer")], result[(0, "w_mem_kv")], "exchange_grads_mixer_wait_%d" % i)
        received.update(zip(layer_names(i)[:-4], got))
    adamw_of(("mla_w_in", "mla_w_uq", "mla_w_ukv", "swa_w_in"), received)

    rep_shapes = [given[n].shape for n in REPLICATED]
    rep_parts = _all_gather([_pack([gains[n] for n in REPLICATED], SLOT, 8, F32)], "gather_gain_grads")[0]
    rep_packed = [_pack([given[p + n] for n in REPLICATED], SLOT, 8, F32)[None] for p in ("", "m_", "v_")]
    for kind, r in enumerate(_adamw([rep_parts], *rep_packed, "adamw_gains")):
        for n, part in zip(REPLICATED, _unpack(r[0], rep_shapes)):
            result[(kind, n)] = part

    outs = [loss, dx.reshape(1, seq, D_MODEL)]
    for kind in range(4):
        outs += [result[(kind, n)] for n in WEIGHT_ORDER]
    return tuple(outs)
```

```python
import functools

import jax
import jax.numpy as jnp
from jax import lax
from jax.experimental import pallas as pl
from jax.experimental.pallas import tpu as pltpu

F32 = jnp.float32
BF16 = jnp.bfloat16

D_MODEL = 1024
D_FF = 4096
N_MEM = 256
DEPTH = 4
SLOT = 128
HEAD_DIM = 64
MLA_HEADS = 12
MLA_QK = 96
MLA_Q_RANK = 384
MLA_KV_RANK = 256
SWA_HEADS = 12
SWA_KV_HEADS = 4
SWA_GROUP = 3
MEM_HEADS = 4
WINDOW = 128
EPS = 1e-6
NEG = -1e30
ROPE_THETA = 10000.0
N_DEV = 8

ADAM_LR = 0.001
ADAM_B1 = 0.9
ADAM_B2 = 0.999
ADAM_EPS = 1e-08
ADAM_WD = 0.01
ADAM_STEP = 10

TM = 512
TQ_MLA = 1024
MLA_PACK = 2
SWA_PACK = 2
TQ_CROSS = 2048
MM_VMEM_BUDGET = 38 * 1024 * 1024
ADAM_ROWS = 128
VMEM_LIMIT = 56 * 1024 * 1024

MESH_AXES = ("x", "y", "c")

LOG2_E = 1.4426950408889634
MLA_SCALE = MLA_QK ** -0.5
MLA_Q_SCALE = MLA_SCALE * LOG2_E

MLA_PAD_IN = 384 + SLOT + 256 + MEM_HEADS * SLOT
MLA_QOFF = (384 + SLOT + 256) // SLOT
SWA_PAD_IN = (SWA_HEADS + 2 * SWA_KV_HEADS + MEM_HEADS) * SLOT
SWA_QOFF = SWA_HEADS + 2 * SWA_KV_HEADS

SHARDED = (
    ("mla_w_in", 1), ("mla_w_uq", 2), ("mla_w_ukv", 2), ("swa_w_in", 2),
    ("w_mem_kv", 1), ("w_o", 1), ("mlp_w_up", 2), ("mlp_w_down", 1),
)
REPLICATED = ("attn_norm_g", "mlp_norm_g", "mem_norm_g", "final_norm_g",
              "mla_q_norm_g", "mla_kv_norm_g", "swa_sinks")
WEIGHT_ORDER = ("attn_norm_g", "mlp_norm_g", "mem_norm_g", "final_norm_g", "mla_w_in",
                "mla_q_norm_g", "mla_kv_norm_g", "mla_w_uq", "mla_w_ukv", "swa_w_in",
                "swa_sinks", "w_mem_kv", "w_o", "mlp_w_up", "mlp_w_down")


def _cparams():
    return pltpu.CompilerParams(vmem_limit_bytes=VMEM_LIMIT)


_DIMS = {"nn": (((1,), (0,)), ((), ())), "nt": (((1,), (1,)), ((), ())), "tn": (((0,), (0,)), ((), ()))}


def _mm_tiles(m, n, k, a_bytes, b_bytes, o_bytes, extra_bytes, tm_fixed, tn_fixed):
    best = None
    for tm in ([tm_fixed] if tm_fixed else [t for t in (4096, 2048, 1024, 512, 256, 128) if m % t == 0] or [m]):
        for tn in ([tn_fixed] if tn_fixed else [t for t in range(1024, 0, -SLOT) if n % t == 0] or [n]):
            need = 2 * (tm * k * a_bytes + k * tn * b_bytes + tm * tn * (o_bytes + extra_bytes))
            need += tm * tn * 4
            if need <= MM_VMEM_BUDGET and (best is None or tm * tn > best[0] * best[1]):
                best = (tm, tn)
    assert best is not None, (m, n, k)
    return best


def _mm(a, b, mode, out_dtype, name, res=None, aux=None, epi=None, b_blk=None, o_blk=None, after=None, norm=None):
    if b_blk is not None:
        nb, br, bc = b.shape
        b_shape = (nb * br, bc) if b_blk == "rows" else (br, nb * bc)
    else:
        b_shape = b.shape
    if mode == "nn":
        (m, k), (k2, n) = a.shape, b_shape
    elif mode == "nt":
        (m, k), (n, k2) = a.shape, b_shape
    else:
        (k, m), (k2, n) = a.shape, b_shape
    assert k == k2, (a.shape, b_shape, mode)
    k_blocked = b_blk is not None and (b_blk == "rows") == (mode != "nt")
    assert not (k_blocked and mode == "nt")
    tn_fixed = None
    if b_blk is not None and not k_blocked:
        tn_fixed = br if b_blk == "rows" else bc
    if o_blk == "cols":
        tn_fixed = n // N_DEV
    tm_fixed = m // N_DEV if o_blk == "rows" else None
    has_res, has_aux, has_norm = res is not None, aux is not None, epi == "normbwd"
    assert o_blk is None or not (has_res or has_aux or has_norm)
    n_out = 2 if epi == "relu2" else 1
    if has_norm:
        tn_fixed = n
        o_bytes, extra_bytes = 4 + 2, 4 + 4
    else:
        o_bytes = n_out * jnp.dtype(out_dtype).itemsize
        extra_bytes = (4 if has_res else 0) + (aux.dtype.itemsize if has_aux else 0)
    tm, tn = _mm_tiles(m, n, k, a.dtype.itemsize, b.dtype.itemsize, o_bytes, extra_bytes, tm_fixed, tn_fixed)
    dims = _DIMS[mode]
    if mode == "tn":
        a_spec = pl.BlockSpec((k, tm), lambda i, j: (0, i))
    else:
        a_spec = pl.BlockSpec((tm, k), lambda i, j: (i, 0))
    if b_blk is None:
        if mode == "nt":
            b_spec = pl.BlockSpec((tn, k), lambda i, j: (j, 0))
        else:
            b_spec = pl.BlockSpec((k, tn), lambda i, j: (0, j))
    elif k_blocked:
        b_spec = pl.BlockSpec((N_DEV, br, tn), lambda i, j: (0, 0, j))
    elif mode == "nt":
        b_spec = pl.BlockSpec((None, tn, k), lambda i, j: (j, 0, 0))
    else:
        b_spec = pl.BlockSpec((None, k, tn), lambda i, j: (j, 0, 0))
    if o_blk is None:
        o_spec = pl.BlockSpec((tm, tn), lambda i, j: (i, j))
        o_shape = (m, n)
    elif o_blk == "rows":
        o_spec = pl.BlockSpec((None, tm, tn), lambda i, j: (i, 0, j))
        o_shape = (N_DEV, tm, n)
    else:
        o_spec = pl.BlockSpec((None, tm, tn), lambda i, j: (j, i, 0))
        o_shape = (N_DEV, m, tn)

    def body(*refs):
        a_ref, b_ref = refs[0], refs[1]
        pos = 2
        res_ref = aux_ref = None
        if has_res:
            res_ref = refs[pos]
            pos += 1
        if has_aux:
            aux_ref = refs[pos]
            pos += 1
        if has_norm:
            x_ref, g_ref, dres_ref = refs[pos:pos + 3]
            pos += 3
        if after is not None:
            pos += 1
        outs = refs[pos:]
        bv = b_ref[...].reshape(k, tn) if k_blocked else b_ref[...]
        r = lax.dot_general(a_ref[...].astype(BF16), bv.astype(BF16), dims, preferred_element_type=F32)
        if epi == "relu2":
            r = jnp.maximum(r, 0.0)
            outs[0][...] = r.astype(outs[0].dtype)
            outs[1][...] = (r * r).astype(outs[1].dtype)
        elif has_norm:
            xv = x_ref[...]
            rs = lax.rsqrt(jnp.mean(xv * xv, axis=1, keepdims=True) + EPS)
            xh = xv * rs
            dxh = r * g_ref[...]
            dx = rs * (dxh - xh * jnp.mean(dxh * xh, axis=1, keepdims=True)) + dres_ref[...]
            outs[0][...] = dx
            outs[1][...] = dx.astype(BF16)

            @pl.when(pl.program_id(0) == 0)
            def _():
                outs[2][...] = jnp.zeros_like(outs[2])

            outs[2][...] += jnp.sum(r * xh, axis=0, keepdims=True)
        else:
            if epi == "mul2aux":
                r = r * (2.0 * aux_ref[...].astype(F32))
            if has_res:
                r = r + res_ref[...]
            outs[0][...] = r.astype(outs[0].dtype)

    in_specs = [a_spec, b_spec]
    args = [a, b]
    if has_res:
        in_specs.append(o_spec)
        args.append(res)
    if has_aux:
        in_specs.append(o_spec)
        args.append(aux)
    vec_spec = pl.BlockSpec((1, n), lambda i, j: (0, 0))
    if has_norm:
        in_specs += [o_spec, vec_spec, o_spec]
        args += [norm[0], norm[1].reshape(1, n), norm[2]]
    if after is not None:
        in_specs.append(pl.BlockSpec(memory_space=pl.ANY))
        args.append(after)
    if has_norm:
        out_specs = [o_spec, o_spec, vec_spec]
        out_shape = [jax.ShapeDtypeStruct(o_shape, F32), jax.ShapeDtypeStruct(o_shape, BF16),
                     jax.ShapeDtypeStruct((1, n), F32)]
    else:
        out_specs = [o_spec] * n_out
        out_shape = [jax.ShapeDtypeStruct(o_shape, out_dtype)] * n_out
    out = pl.pallas_call(
        body, name=name, grid=(m // tm, n // tn),
        in_specs=in_specs, out_specs=out_specs, out_shape=out_shape, compiler_params=_cparams(),
    )(*args)
    return out if len(out) > 1 else out[0]


def _rmsnorm_fwd(xarr, colblk, width, g, name, after=None):
    rows = xarr.shape[0]
    tm = min(TM, rows)

    def body(x_ref, g_ref, *rest):
        y_ref = rest[-1]
        x = x_ref[...].astype(F32)
        r = lax.rsqrt(jnp.mean(x * x, axis=1, keepdims=True) + EPS)
        y_ref[...] = (x * r * g_ref[...]).astype(y_ref.dtype)

    in_specs = [pl.BlockSpec((tm, width), lambda i: (i, colblk)), pl.BlockSpec((1, width), lambda i: (0, 0))]
    args = [xarr, g.reshape(1, width)]
    if after is not None:
        in_specs.append(pl.BlockSpec(memory_space=pl.ANY))
        args.append(after)
    return pl.pallas_call(
        body, name=name, grid=(rows // tm,), in_specs=in_specs,
        out_specs=pl.BlockSpec((tm, width), lambda i: (i, 0)),
        out_shape=jax.ShapeDtypeStruct((rows, width), BF16), compiler_params=_cparams(),
    )(*args)


def _rmsnorm_bwd(xarr, colblk, width, g, dy, dres, out_dtype, name):
    rows = xarr.shape[0]
    tm = min(TM, rows)
    has_res = dres is not None

    def body(*refs):
        x_ref, g_ref, dy_ref = refs[0], refs[1], refs[2]
        dres_ref = refs[3] if has_res else None
        dx_ref, dg_ref = refs[-2], refs[-1]
        x = x_ref[...].astype(F32)
        dyv = dy_ref[...].astype(F32)
        r = lax.rsqrt(jnp.mean(x * x, axis=1, keepdims=True) + EPS)
        xh = x * r
        dxh = dyv * g_ref[...]
        dx = r * (dxh - xh * jnp.mean(dxh * xh, axis=1, keepdims=True))
        if has_res:
            dx = dx + dres_ref[...]
        dx_ref[...] = dx.astype(dx_ref.dtype)

        @pl.when(pl.program_id(0) == 0)
        def _():
            dg_ref[...] = jnp.zeros_like(dg_ref)

        dg_ref[...] += jnp.sum(dyv * xh, axis=0, keepdims=True)

    row_spec = pl.BlockSpec((tm, width), lambda i: (i, 0))
    vec_spec = pl.BlockSpec((1, width), lambda i: (0, 0))
    in_specs = [pl.BlockSpec((tm, width), lambda i: (i, colblk)), vec_spec, row_spec]
    args = [xarr, g.reshape(1, width), dy]
    if has_res:
        in_specs.append(row_spec)
        args.append(dres)
    return pl.pallas_call(
        body, name=name, grid=(rows // tm,), in_specs=in_specs, out_specs=[row_spec, vec_spec],
        out_shape=[jax.ShapeDtypeStruct((rows, width), out_dtype), jax.ShapeDtypeStruct((1, width), F32)],
        compiler_params=_cparams(),
    )(*args)


def _loss_head(x, g, tgt):
    rows, width = x.shape
    tm = min(TM, rows)

    def body(x_ref, g_ref, t_ref, dx_ref, dxb_ref, dg_ref, loss_ref):
        xv = x_ref[...]
        gv = g_ref[...]
        r = lax.rsqrt(jnp.mean(xv * xv, axis=1, keepdims=True) + EPS)
        xh = xv * r
        err = xh * gv - t_ref[...]
        part = 0.5 * jnp.sum(jnp.mean(err * err, axis=1, keepdims=True), axis=0, keepdims=True)
        dyv = err * (1.0 / width)
        dxh = dyv * gv
        dxv = r * (dxh - xh * jnp.mean(dxh * xh, axis=1, keepdims=True))
        dx_ref[...] = dxv
        dxb_ref[...] = dxv.astype(BF16)

        @pl.when(pl.program_id(0) == 0)
        def _():
            dg_ref[...] = jnp.zeros_like(dg_ref)
            loss_ref[...] = jnp.zeros_like(loss_ref)

        dg_ref[...] += jnp.sum(dyv * xh, axis=0, keepdims=True)
        loss_ref[...] += jnp.broadcast_to(part, loss_ref.shape)

    row_spec = pl.BlockSpec((tm, width), lambda i: (i, 0))
    vec_spec = pl.BlockSpec((1, width), lambda i: (0, 0))
    return pl.pallas_call(
        body, name="loss_head", grid=(rows // tm,), in_specs=[row_spec, vec_spec, row_spec],
        out_specs=[row_spec, row_spec, vec_spec, pl.BlockSpec((1, SLOT), lambda i: (0, 0))],
        out_shape=[jax.ShapeDtypeStruct((rows, width), F32), jax.ShapeDtypeStruct((rows, width), BF16),
                   jax.ShapeDtypeStruct((1, width), F32), jax.ShapeDtypeStruct((1, SLOT), F32)],
        compiler_params=_cparams(),
    )(x, g.reshape(1, width), tgt)


def _lane_consts():
    half = 16
    inv = ROPE_THETA ** (-(jnp.arange(half, dtype=F32) * 2.0) / 32)
    lane = jnp.arange(SLOT)
    first = (lane >= 64) & (lane < 80)
    second = (lane >= 80) & (lane < 96)
    inv_lane = jnp.where(first | second, inv[(lane - 64) % half], 0.0)
    rows = [inv_lane, (lane < 64).astype(F32), first.astype(F32), second.astype(F32)]
    rows += [jnp.zeros((SLOT,), F32)] * 4
    return jnp.stack(rows).astype(F32)


def _rope_tables(pos_col, consts):
    rows = pos_col.shape[0]
    tm = min(TM, rows)

    def body(p_ref, k_ref, c_ref, s1_ref, s2_ref):
        ang = p_ref[...] * k_ref[0:1, :]
        cos, sin = jnp.cos(ang), jnp.sin(ang)
        first, second = k_ref[2:3, :], k_ref[3:4, :]
        c_ref[...] = k_ref[1:2, :] + (first + second) * cos
        s1_ref[...] = -first * sin
        s2_ref[...] = second * sin

    spec = pl.BlockSpec((tm, SLOT), lambda i: (i, 0))
    shp = jax.ShapeDtypeStruct((rows, SLOT), F32)
    return pl.pallas_call(
        body, name="rope_tables", grid=(rows // tm,),
        in_specs=[pl.BlockSpec((tm, 1), lambda i: (i, 0)), pl.BlockSpec((8, SLOT), lambda i: (0, 0))],
        out_specs=[spec, spec, spec], out_shape=[shp, shp, shp], compiler_params=_cparams(),
    )(pos_col, consts)


def _rot(xv, c, s1, s2):
    return xv * c + pltpu.roll(xv, SLOT - 16, 1) * s1 + pltpu.roll(xv, 16, 1) * s2


def _rot_t(dy, c, s1, s2):
    return dy * c + pltpu.roll(dy * s1, 16, 1) + pltpu.roll(dy * s2, SLOT - 16, 1)


def _mla_rope_fwd(qraw, kvraw, proj, tabs):
    rows = qraw.shape[0]
    tm = min(256, rows)
    hw = MLA_HEADS * SLOT

    def body(q_ref, kv_ref, kr_ref, c_ref, s1_ref, s2_ref, qo, ko, vo):
        c, s1, s2 = c_ref[...], s1_ref[...], s2_ref[...]
        kr = _rot(kr_ref[...], c, s1, s2)
        for h in range(MLA_HEADS):
            sl = slice(h * SLOT, (h + 1) * SLOT)
            qo[:, sl] = (_rot(q_ref[:, sl], c, s1, s2) * MLA_Q_SCALE).astype(BF16)
            ko[:, sl] = (kv_ref[:, sl] + kr).astype(BF16)
            vo[:, sl] = kv_ref[:, hw + h * SLOT:hw + (h + 1) * SLOT].astype(BF16)

    tab = pl.BlockSpec((tm, SLOT), lambda i: (i, 0))
    wide = pl.BlockSpec((tm, hw), lambda i: (i, 0))
    shp = jax.ShapeDtypeStruct((rows, hw), BF16)
    return pl.pallas_call(
        body, name="mla_rope_fwd", grid=(rows // tm,),
        in_specs=[wide, pl.BlockSpec((tm, 2 * hw), lambda i: (i, 0)), pl.BlockSpec((tm, SLOT), lambda i: (i, 3)),
                  tab, tab, tab],
        out_specs=[wide, wide, wide], out_shape=[shp, shp, shp], compiler_params=_cparams(),
    )(qraw, kvraw, proj, *tabs)


def _mla_rope_bwd(dq, dk, dv, tabs, consts):
    rows = dq.shape[0]
    tm = min(256, rows)
    hw = MLA_HEADS * SLOT

    def body(dq_ref, dk_ref, dv_ref, c_ref, s1_ref, s2_ref, k_ref, dqo, dkvo, dkro):
        c, s1, s2 = c_ref[...], s1_ref[...], s2_ref[...]
        ksum = jnp.zeros((tm, SLOT), F32)
        for h in range(MLA_HEADS):
            sl = slice(h * SLOT, (h + 1) * SLOT)
            dqo[:, sl] = _rot_t(dq_ref[:, sl], c, s1, s2).astype(BF16)
            dkh = dk_ref[:, sl]
            ksum = ksum + dkh
            dkvo[:, sl] = dkh.astype(BF16)
            dkvo[:, hw + h * SLOT:hw + (h + 1) * SLOT] = dv_ref[:, sl].astype(BF16)
        dkro[...] = _rot_t(ksum, c, s1, s2) * (k_ref[2:3, :] + k_ref[3:4, :])

    tab = pl.BlockSpec((tm, SLOT), lambda i: (i, 0))
    wide = pl.BlockSpec((tm, hw), lambda i: (i, 0))
    return pl.pallas_call(
        body, name="mla_rope_bwd", grid=(rows // tm,),
        in_specs=[wide, wide, wide, tab, tab, tab, pl.BlockSpec((8, SLOT), lambda i: (0, 0))],
        out_specs=[wide, pl.BlockSpec((tm, 2 * hw), lambda i: (i, 0)), tab],
        out_shape=[jax.ShapeDtypeStruct((rows, hw), BF16), jax.ShapeDtypeStruct((rows, 2 * hw), BF16),
                   jax.ShapeDtypeStruct((rows, SLOT), F32)],
        compiler_params=_cparams(),
    )(dq, dk, dv, *tabs, consts)


def _nt(a, b):
    return lax.dot_general(a, b, _DIMS["nt"], preferred_element_type=F32)


def _tn(a, b):
    return lax.dot_general(a, b, _DIMS["tn"], preferred_element_type=F32)


def _nn(a, b):
    return lax.dot_general(a, b, _DIMS["nn"], preferred_element_type=F32)


def _causal(rows, keys):
    shp = (rows.stop - rows.start, keys.stop - keys.start)
    return (keys.start + lax.broadcasted_iota(jnp.int32, shp, 1)
            <= rows.start + lax.broadcasted_iota(jnp.int32, shp, 0))


def _mla_attn_fwd(q, k, v):
    rows = q.shape[0]
    t = min(TQ_MLA, rows)
    nt = rows // t
    wide = MLA_PACK * SLOT

    def body(q_ref, k_ref, v_ref, o_ref, lse_ref, m_sc, l_sc, acc_sc):
        i, j = pl.program_id(1), pl.program_id(2)

        @pl.when(j == 0)
        def _():
            m_sc[...] = jnp.full_like(m_sc, NEG)
            l_sc[...] = jnp.zeros_like(l_sc)
            acc_sc[...] = jnp.zeros_like(acc_sc)

        def step(diagonal):
            for hh in range(MLA_PACK):
                sl = slice(hh * SLOT, (hh + 1) * SLOT)
                s = _nt(q_ref[:, sl], k_ref[:, sl])
                if diagonal:
                    s = jnp.where(_causal(slice(0, t), slice(0, t)), s, NEG)
                m_prev = m_sc[hh]
                m_new = jnp.maximum(m_prev, jnp.max(s, axis=1, keepdims=True))
                p = jnp.exp2(s - m_new)
                alpha = jnp.exp2(m_prev - m_new)
                l_new = alpha * l_sc[hh] + jnp.sum(p, axis=1, keepdims=True)
                acc = alpha * acc_sc[:, sl] + _nn(p.astype(BF16), v_ref[:, sl])
                if diagonal:
                    o_ref[:, sl] = (acc / l_new).astype(o_ref.dtype)
                    lse_ref[:, sl] = jnp.broadcast_to(m_new + jnp.log(l_new) * LOG2_E, (t, SLOT))
                else:
                    m_sc[hh] = m_new
                    l_sc[hh] = l_new
                    acc_sc[:, sl] = acc

        @pl.when(j < i)
        def _():
            step(False)

        @pl.when(j == i)
        def _():
            step(True)

    q_spec = pl.BlockSpec((t, wide), lambda h, i, j: (i, h))
    kv_spec = pl.BlockSpec((t, wide), lambda h, i, j: (jnp.minimum(j, i), h))
    return pl.pallas_call(
        body, name="mla_attn_fwd", grid=(MLA_HEADS // MLA_PACK, nt, nt),
        in_specs=[q_spec, kv_spec, kv_spec], out_specs=[q_spec, q_spec],
        out_shape=[jax.ShapeDtypeStruct(q.shape, BF16), jax.ShapeDtypeStruct(q.shape, F32)],
        scratch_shapes=[pltpu.VMEM((MLA_PACK, t, 1), F32), pltpu.VMEM((MLA_PACK, t, 1), F32),
                        pltpu.VMEM((t, wide), F32)],
        compiler_params=_cparams(),
    )(q, k, v)


def _mla_attn_bwd(q, k, v, o, do, lse, after):
    rows = q.shape[0]
    t = min(TQ_MLA, rows)
    nt = rows // t
    wide = MLA_PACK * SLOT

    def body(q_ref, k_ref, v_ref, o_ref, do_ref, lse_ref, after_ref, dq_ref, dk_ref, dv_ref, dk_sc, dv_sc):
        j, i = pl.program_id(1), pl.program_id(2)

        @pl.when((j == 0) & (i == 0))
        def _():
            dq_ref[...] = jnp.zeros_like(dq_ref)

        @pl.when(i == 0)
        def _():
            dk_sc[...] = jnp.zeros_like(dk_sc)
            dv_sc[...] = jnp.zeros_like(dv_sc)

        def chunk(hh, rows, keys, masked):
            sl = slice(hh * SLOT, (hh + 1) * SLOT)
            n_rows = rows.stop - rows.start
            qv, kv, dov = q_ref[rows, sl], k_ref[keys, sl], do_ref[rows, sl]
            s = _nt(qv, kv)
            if masked:
                s = jnp.where(_causal(rows, keys), s, NEG)
            p = jnp.exp2(s - lse_ref[rows, hh * SLOT:hh * SLOT + 1])
            delta = jnp.sum(dov.astype(F32) * o_ref[rows, sl].astype(F32), axis=1, keepdims=True)
            dp = _nt(dov, v_ref[keys, sl])
            ds = (p * (dp - delta)).astype(BF16)
            dv_sc[keys, sl] += _tn(p.astype(BF16), dov)
            dk_sc[keys, sl] += _tn(ds, qv)
            r0 = pl.multiple_of(i * t + rows.start, n_rows)
            dq_ref[pl.ds(r0, n_rows), sl] += _nn(ds, kv) * MLA_SCALE

        @pl.when(i > j)
        def _():
            for hh in range(MLA_PACK):
                chunk(hh, slice(0, t), slice(0, t), False)

        @pl.when(i == j)
        def _():
            for hh in range(MLA_PACK):
                chunk(hh, slice(0, t), slice(0, t // 2), True)
                chunk(hh, slice(t // 2, t), slice(t // 2, t), True)

        @pl.when(i == nt - 1)
        def _():
            dk_ref[...] = dk_sc[...] * (1.0 / LOG2_E)
            dv_ref[...] = dv_sc[...]

    q_spec = pl.BlockSpec((t, wide), lambda h, j, i: (jnp.maximum(i, j), h))
    kv_spec = pl.BlockSpec((t, wide), lambda h, j, i: (j, h))
    head_spec = pl.BlockSpec((rows, wide), lambda h, j, i: (0, h))
    shp = jax.ShapeDtypeStruct(q.shape, F32)
    return pl.pallas_call(
        body, name="mla_attn_bwd", grid=(MLA_HEADS // MLA_PACK, nt, nt),
        in_specs=[q_spec, kv_spec, kv_spec, q_spec, q_spec, q_spec, pl.BlockSpec(memory_space=pl.ANY)],
        out_specs=[head_spec, kv_spec, kv_spec], out_shape=[shp, shp, shp],
        scratch_shapes=[pltpu.VMEM((t, wide), F32), pltpu.VMEM((t, wide), F32)],
        compiler_params=_cparams(),
    )(q, k, v, o, do, lse, after)


def _swa_specs(t):
    def prev(i):
        return jnp.maximum(i - 1, 0)
    kw = SWA_PACK * SLOT
    k0, v0 = SWA_HEADS // SWA_PACK, (SWA_HEADS + SWA_KV_HEADS) // SWA_PACK
    q3 = pl.BlockSpec((t, SWA_PACK * SWA_GROUP * SLOT), lambda h, i: (i, h))
    kp = pl.BlockSpec((t, kw), lambda h, i: (prev(i), k0 + h))
    kc = pl.BlockSpec((t, kw), lambda h, i: (i, k0 + h))
    vp = pl.BlockSpec((t, kw), lambda h, i: (prev(i), v0 + h))
    vc = pl.BlockSpec((t, kw), lambda h, i: (i, v0 + h))
    pcol = pl.BlockSpec((t, 1), lambda h, i: (i, 0))
    prow_p = pl.BlockSpec((1, t), lambda h, i: (0, prev(i)))
    prow_c = pl.BlockSpec((1, t), lambda h, i: (0, i))
    return [q3, kp, kc, vp, vc, pcol, prow_p, prow_c]


def _stack(ref, first):
    return jnp.concatenate([ref[:, (first + g) * SLOT:(first + g + 1) * SLOT] for g in range(SWA_GROUP)], axis=0)


def _swa_logits(q3, kp, kc, pq, pkp, pkc, slope_ref, kvh, i, t):
    r = lax.broadcasted_iota(jnp.int32, (t, t), 0)
    c = lax.broadcasted_iota(jnp.int32, (t, t), 1)
    ok_c = c <= r
    ok_p = (c - r) > jnp.where(i > 0, 0, t)
    dist_p, dist_c = pq - pkp, pq - pkc
    s_p3 = _nt(q3, kp) * (HEAD_DIM ** -0.5)
    s_c3 = _nt(q3, kc) * (HEAD_DIM ** -0.5)
    out = []
    for g in range(SWA_GROUP):
        slope = slope_ref[kvh * SWA_GROUP + g]
        rows = slice(g * t, (g + 1) * t)
        out.append((jnp.where(ok_p, s_p3[rows] - slope * dist_p, NEG),
                    jnp.where(ok_c, s_c3[rows] - slope * dist_c, NEG)))
    return out


def _swa_attn_fwd(proj, pos_col, pos_row, slopes, sinks):
    rows = proj.shape[0]
    t = WINDOW
    hw = SWA_HEADS * SLOT

    def body(slope_ref, sink_ref, q_ref, kp_ref, kc_ref, vp_ref, vc_ref, pq_ref, pkp_ref, pkc_ref, o_ref, lse_ref):
        i = pl.program_id(1)
        for kv in range(SWA_PACK):
            kvh = pl.program_id(0) * SWA_PACK + kv
            ksl = slice(kv * SLOT, (kv + 1) * SLOT)
            logits = _swa_logits(_stack(q_ref, kv * SWA_GROUP), kp_ref[:, ksl], kc_ref[:, ksl], pq_ref[...],
                                 pkp_ref[...], pkc_ref[...], slope_ref, kvh, i, t)
            e_p, e_c, norm = [], [], []
            for g, (s_p, s_c) in enumerate(logits):
                sl = slice((kv * SWA_GROUP + g) * SLOT, (kv * SWA_GROUP + g + 1) * SLOT)
                sink = sink_ref[kvh * SWA_GROUP + g]
                m = jnp.maximum(jnp.maximum(jnp.max(s_p, axis=1, keepdims=True),
                                            jnp.max(s_c, axis=1, keepdims=True)), sink)
                ep, ec = jnp.exp(s_p - m), jnp.exp(s_c - m)
                l = jnp.sum(ep, axis=1, keepdims=True) + jnp.sum(ec, axis=1, keepdims=True) + jnp.exp(sink - m)
                e_p.append(ep.astype(BF16))
                e_c.append(ec.astype(BF16))
                norm.append(l)
                lse_ref[:, sl] = jnp.broadcast_to(m + jnp.log(l), (t, SLOT))
            acc = (_nn(jnp.concatenate(e_p, axis=0), vp_ref[:, ksl])
                   + _nn(jnp.concatenate(e_c, axis=0), vc_ref[:, ksl]))
            for g in range(SWA_GROUP):
                sl = slice((kv * SWA_GROUP + g) * SLOT, (kv * SWA_GROUP + g + 1) * SLOT)
                o_ref[:, sl] = (acc[g * t:(g + 1) * t] / norm[g]).astype(o_ref.dtype)

    smem = pl.BlockSpec(memory_space=pltpu.SMEM)
    out_spec = pl.BlockSpec((t, SWA_PACK * SWA_GROUP * SLOT), lambda h, i: (i, h))
    return pl.pallas_call(
        body, name="swa_attn_fwd", grid=(SWA_KV_HEADS // SWA_PACK, rows // t),
        in_specs=[smem, smem] + _swa_specs(t), out_specs=[out_spec, out_spec],
        out_shape=[jax.ShapeDtypeStruct((rows, hw), BF16), jax.ShapeDtypeStruct((rows, hw), F32)],
        compiler_params=_cparams(),
    )(slopes, sinks, proj, proj, proj, proj, proj, pos_col, pos_row, pos_row)


def _swa_attn_bwd(proj, o, do, lse, pos_col, pos_row, slopes, sinks, after):
    rows = proj.shape[0]
    t = WINDOW
    hw = SWA_HEADS * SLOT
    scale = HEAD_DIM ** -0.5

    def body(slope_ref, sink_ref, q_ref, kp_ref, kc_ref, vp_ref, vc_ref, pq_ref, pkp_ref, pkc_ref,
             o_ref, do_ref, lse_ref, after_ref, dq_ref, dk_ref, dv_ref, dsink_ref):
        i = pl.program_id(1)

        @pl.when(i == 0)
        def _():
            dk_ref[...] = jnp.zeros_like(dk_ref)
            dv_ref[...] = jnp.zeros_like(dv_ref)
            dsink_ref[...] = jnp.zeros_like(dsink_ref)

        r_c = pl.multiple_of(i * t, t)
        r_p = pl.multiple_of(jnp.maximum(i - 1, 0) * t, t)
        for kv in range(SWA_PACK):
            kvh = pl.program_id(0) * SWA_PACK + kv
            ksl = slice(kv * SLOT, (kv + 1) * SLOT)
            q3, do3 = _stack(q_ref, kv * SWA_GROUP), _stack(do_ref, kv * SWA_GROUP)
            logits = _swa_logits(q3, kp_ref[:, ksl], kc_ref[:, ksl], pq_ref[...], pkp_ref[...], pkc_ref[...],
                                 slope_ref, kvh, i, t)
            dp_p3, dp_c3 = _nt(do3, vp_ref[:, ksl]), _nt(do3, vc_ref[:, ksl])
            p_p, p_c, ds_p, ds_c = [], [], [], []
            for g, (s_p, s_c) in enumerate(logits):
                head = kv * SWA_GROUP + g
                sl = slice(head * SLOT, (head + 1) * SLOT)
                rws = slice(g * t, (g + 1) * t)
                lse_g = lse_ref[:, head * SLOT:head * SLOT + 1]
                pp, pc = jnp.exp(s_p - lse_g), jnp.exp(s_c - lse_g)
                delta = jnp.sum(do_ref[:, sl].astype(F32) * o_ref[:, sl].astype(F32), axis=1, keepdims=True)
                p_p.append(pp.astype(BF16))
                p_c.append(pc.astype(BF16))
                ds_p.append((pp * (dp_p3[rws] - delta)).astype(BF16))
                ds_c.append((pc * (dp_c3[rws] - delta)).astype(BF16))
                sink = sink_ref[kvh * SWA_GROUP + g]
                dsink = -jnp.sum(jnp.exp(sink - lse_g) * delta, axis=0, keepdims=True)
                dsink_ref[head * 8:(head + 1) * 8, :] += jnp.broadcast_to(dsink, (8, SLOT))
            p_p3, p_c3 = jnp.concatenate(p_p, axis=0), jnp.concatenate(p_c, axis=0)
            ds_p3, ds_c3 = jnp.concatenate(ds_p, axis=0), jnp.concatenate(ds_c, axis=0)
            dq3 = (_nn(ds_p3, kp_ref[:, ksl]) + _nn(ds_c3, kc_ref[:, ksl])) * scale
            for g in range(SWA_GROUP):
                head = kv * SWA_GROUP + g
                dq_ref[:, head * SLOT:(head + 1) * SLOT] = dq3[g * t:(g + 1) * t]
            dk_ref[pl.ds(r_c, t), ksl] += _tn(ds_c3, q3) * scale
            dv_ref[pl.ds(r_c, t), ksl] += _tn(p_c3, do3)
            dk_ref[pl.ds(r_p, t), ksl] += _tn(ds_p3, q3) * scale
            dv_ref[pl.ds(r_p, t), ksl] += _tn(p_p3, do3)

    smem = pl.BlockSpec(memory_space=pltpu.SMEM)
    qlike = pl.BlockSpec((t, SWA_PACK * SWA_GROUP * SLOT), lambda h, i: (i, h))
    kv_out = pl.BlockSpec((rows, SWA_PACK * SLOT), lambda h, i: (0, h))
    return pl.pallas_call(
        body, name="swa_attn_bwd", grid=(SWA_KV_HEADS // SWA_PACK, rows // t),
        in_specs=[smem, smem] + _swa_specs(t) + [qlike, qlike, qlike, pl.BlockSpec(memory_space=pl.ANY)],
        out_specs=[qlike, kv_out, kv_out,
                   pl.BlockSpec((SWA_PACK * SWA_GROUP * 8, SLOT), lambda h, i: (h, 0))],
        out_shape=[jax.ShapeDtypeStruct((rows, hw), F32), jax.ShapeDtypeStruct((rows, SWA_KV_HEADS * SLOT), F32),
                   jax.ShapeDtypeStruct((rows, SWA_KV_HEADS * SLOT), F32),
                   jax.ShapeDtypeStruct((SWA_HEADS * 8, SLOT), F32)],
        compiler_params=_cparams(),
    )(slopes, sinks, proj, proj, proj, proj, proj, pos_col, pos_row, pos_row, o, do, lse, after)


def _cross_attn_fwd(proj, qoff, kvmem):
    rows = proj.shape[0]
    t = min(TQ_CROSS, rows)

    def body(q_ref, k_ref, v_ref, o_ref):
        s = _nt(q_ref[...].astype(BF16), k_ref[...]) * (HEAD_DIM ** -0.5)
        e = jnp.exp(s - jnp.max(s, axis=1, keepdims=True))
        p = e / jnp.sum(e, axis=1, keepdims=True)
        o_ref[...] = _nn(p.astype(BF16), v_ref[...]).astype(o_ref.dtype)

    return pl.pallas_call(
        body, name="cross_attn_fwd", grid=(rows // t, MEM_HEADS),
        in_specs=[pl.BlockSpec((t, SLOT), lambda i, h: (i, qoff + h)),
                  pl.BlockSpec((N_MEM, SLOT), lambda i, h: (0, h)),
                  pl.BlockSpec((N_MEM, SLOT), lambda i, h: (0, MEM_HEADS + h))],
        out_specs=pl.BlockSpec((t, SLOT), lambda i, h: (i, h)),
        out_shape=jax.ShapeDtypeStruct((rows, MEM_HEADS * SLOT), BF16), compiler_params=_cparams(),
    )(proj, kvmem, kvmem)


def _cross_attn_bwd(proj, qoff, kvmem, do):
    rows = proj.shape[0]
    t = min(TQ_CROSS, rows)
    scale = HEAD_DIM ** -0.5

    def body(q_ref, k_ref, v_ref, do_ref, dq_ref, dk_ref, dv_ref):
        @pl.when(pl.program_id(1) == 0)
        def _():
            dk_ref[...] = jnp.zeros_like(dk_ref)
            dv_ref[...] = jnp.zeros_like(dv_ref)

        qv, kv, dov = q_ref[...].astype(BF16), k_ref[...], do_ref[...]
        s = _nt(qv, kv) * scale
        e = jnp.exp(s - jnp.max(s, axis=1, keepdims=True))
        p = e / jnp.sum(e, axis=1, keepdims=True)
        dp = _nt(dov, v_ref[...])
        ds = (p * (dp - jnp.sum(p * dp, axis=1, keepdims=True))).astype(BF16)
        dq_ref[...] = _nn(ds, kv) * scale
        dk_ref[...] += _tn(ds, qv) * scale
        dv_ref[...] += _tn(p.astype(BF16), dov)

    mem_out = pl.BlockSpec((N_MEM, SLOT), lambda h, i: (0, h))
    return pl.pallas_call(
        body, name="cross_attn_bwd", grid=(MEM_HEADS, rows // t),
        in_specs=[pl.BlockSpec((t, SLOT), lambda h, i: (i, qoff + h)),
                  pl.BlockSpec((N_MEM, SLOT), lambda h, i: (0, h)),
                  pl.BlockSpec((N_MEM, SLOT), lambda h, i: (0, MEM_HEADS + h)),
                  pl.BlockSpec((t, SLOT), lambda h, i: (i, h))],
        out_specs=[pl.BlockSpec((t, SLOT), lambda h, i: (i, h)), mem_out, mem_out],
        out_shape=[jax.ShapeDtypeStruct((rows, MEM_HEADS * SLOT), F32),
                   jax.ShapeDtypeStruct((N_MEM, MEM_HEADS * SLOT), F32),
                   jax.ShapeDtypeStruct((N_MEM, MEM_HEADS * SLOT), F32)],
        compiler_params=_cparams(),
    )(proj, kvmem, kvmem, do)


def _place():
    return lax.axis_index("x"), lax.axis_index("y"), lax.axis_index("c")


def _flip(v, bit):
    return 1 - v if bit else v


def _all_gather(blocks, name):
    nb = len(blocks)

    def body(*refs):
        x_refs, out_refs = refs[:nb], refs[nb:2 * nb]
        send_sems, recv_sems, local_sems = refs[2 * nb:]
        x, y, c = _place()
        me, sibling = (x, y, c), (x, y, 1 - c)
        chips = [(1 - x, y), (x, 1 - y), (1 - x, 1 - y)]

        def copy(b, k, blk, to, from_input=False):
            slot = out_refs[b].at[4 * blk[0] + 2 * blk[1] + blk[2]]
            return pltpu.make_async_remote_copy(
                src_ref=x_refs[b] if from_input else slot, dst_ref=slot,
                send_sem=send_sems.at[b, k], recv_sem=recv_sems.at[b, k],
                device_id=to, device_id_type=pl.DeviceIdType.MESH)

        mine = [pltpu.make_async_copy(x_refs[b], out_refs[b].at[4 * x + 2 * y + c], local_sems.at[b])
                for b in range(nb)]
        for cp in mine:
            cp.start()
        first = []
        for b in range(nb):
            first.append(copy(b, 0, me, sibling, from_input=True))
            first += [copy(b, 1 + n, me, (*chip, c), from_input=True) for n, chip in enumerate(chips)]
        for cp in first:
            cp.start()
        passed = []
        for n, chip in enumerate(chips):
            for b in range(nb):
                copy(b, 1 + n, (*chip, c), me).wait_recv()
                passed.append(copy(b, 4 + n, (*chip, c), sibling))
                passed[-1].start()
        for b in range(nb):
            copy(b, 0, sibling, me).wait_recv()
            for n, chip in enumerate(chips):
                copy(b, 4 + n, (*chip, 1 - c), me).wait_recv()
        for cp in first + passed:
            cp.wait_send()
        for cp in mine:
            cp.wait()

    any_spec = pl.BlockSpec(memory_space=pl.ANY)
    return pl.pallas_call(
        body, name=name, in_specs=[any_spec] * nb, out_specs=[any_spec] * nb,
        out_shape=[jax.ShapeDtypeStruct((N_DEV,) + blk.shape, blk.dtype) for blk in blocks],
        scratch_shapes=[pltpu.SemaphoreType.DMA((nb, 7)), pltpu.SemaphoreType.DMA((nb, 7)),
                        pltpu.SemaphoreType.DMA((nb,))],
    )(*blocks)


def _peers(x, y, c):
    out = []
    for n in range(1, N_DEV):
        peer = (_flip(x, n & 4), _flip(y, n & 2), _flip(c, n & 1))
        out.append((n - 1, peer, 4 * peer[0] + 2 * peer[1] + peer[2]))
    return out


_HBM = pl.BlockSpec(memory_space=pltpu.HBM)
_SEM = pl.BlockSpec(memory_space=pltpu.SEMAPHORE)


def _exchange_start(srcs, scatter, name, after=None):
    ns = len(srcs)
    lands = [lax.empty(s.shape if scatter else (N_DEV,) + s.shape, s.dtype) for s in srcs]

    def body(*refs):
        src_refs, land_refs = refs[:ns], refs[ns:2 * ns]
        pos = 2 * ns + (1 if after is not None else 0)
        send_sems, recv_sems, token = refs[pos], refs[pos + 1], refs[-1]
        x, y, c = _place()
        my_idx = 4 * x + 2 * y + c
        for col, peer, peer_idx in _peers(x, y, c):
            for b in range(ns):
                pltpu.make_async_remote_copy(
                    src_ref=src_refs[b].at[peer_idx] if scatter else src_refs[b], dst_ref=land_refs[b].at[my_idx],
                    send_sem=send_sems.at[b * (N_DEV - 1) + col], recv_sem=recv_sems.at[b * (N_DEV - 1) + col],
                    device_id=peer, device_id_type=pl.DeviceIdType.MESH).start()
        token[...] = jnp.zeros_like(token)

    args = [pltpu.with_memory_space_constraint(a, pltpu.HBM) for a in list(srcs) + lands]
    in_specs = [_HBM] * (2 * ns)
    if after is not None:
        args.append(after)
        in_specs.append(pl.BlockSpec(memory_space=pl.ANY))
    out = pl.pallas_call(
        body, name=name, in_specs=in_specs,
        out_specs=[_SEM, _SEM] + [_HBM] * (2 * ns) + [pl.BlockSpec(memory_space=pltpu.VMEM)],
        out_shape=[pltpu.SemaphoreType.DMA((ns * (N_DEV - 1),)), pltpu.SemaphoreType.DMA((ns * (N_DEV - 1),))]
        + [pltpu.HBM(a.shape, a.dtype) for a in list(srcs) + lands] + [jax.ShapeDtypeStruct((8, SLOT), F32)],
        input_output_aliases={k: 2 + k for k in range(2 * ns)},
        compiler_params=pltpu.CompilerParams(has_side_effects=pltpu.SideEffectType.DATAFLOW_SIDE_EFFECTING),
    )(*args)
    return (out[0], out[1], out[2:2 + ns], out[2 + ns:2 + 2 * ns], scatter), out[-1]


def _exchange_wait(handle, after, name):
    send_sems, recv_sems, srcs, lands, scatter = handle
    ns = len(srcs)

    def body(*refs):
        src_refs, land_refs = refs[:ns], refs[ns:2 * ns]
        send_ref, recv_ref = refs[2 * ns], refs[2 * ns + 1]
        x, y, c = _place()
        for col, peer, peer_idx in _peers(x, y, c):
            for b in range(ns):
                copy = pltpu.make_async_remote_copy(
                    src_ref=src_refs[b].at[peer_idx] if scatter else src_refs[b], dst_ref=land_refs[b].at[peer_idx],
                    send_sem=send_ref.at[b * (N_DEV - 1) + col], recv_sem=recv_ref.at[b * (N_DEV - 1) + col],
                    device_id=peer, device_id_type=pl.DeviceIdType.MESH)
                copy.wait_send()
                copy.wait_recv()

    out = pl.pallas_call(
        body, name=name, in_specs=[_HBM] * (2 * ns) + [_SEM, _SEM, pl.BlockSpec(memory_space=pl.ANY)],
        out_specs=[_HBM] * (2 * ns),
        out_shape=[pltpu.HBM(a.shape, a.dtype) for a in list(srcs) + list(lands)],
        input_output_aliases={k: k for k in range(2 * ns)},
        compiler_params=pltpu.CompilerParams(has_side_effects=pltpu.SideEffectType.DATAFLOW_SIDE_EFFECTING),
    )(*srcs, *lands, send_sems, recv_sems, after)
    my_idx = 4 * lax.axis_index("x") + 2 * lax.axis_index("y") + lax.axis_index("c")
    landed = []
    for src, land in zip(out[:ns], out[ns:]):
        own = lax.dynamic_index_in_dim(src, my_idx, 0, keepdims=True) if scatter else src[None]
        landed.append(lax.dynamic_update_index_in_dim(land, own, my_idx, 0))
    return landed


def _adamw(parts, w, m, v, name):
    lyr, rows, cols = w.shape
    assert len(parts) == lyr
    tr = ADAM_ROWS if cols > 512 else 2 * ADAM_ROWS
    while rows % tr:
        tr //= 2
    tr = min(tr, rows)

    def body(*refs):
        p_refs = refs[:lyr]
        w_ref, m_ref, v_ref, g_out, d_out, m_out, v_out = refs[lyr:]
        for k in range(lyr):
            @pl.when(pl.program_id(0) == k)
            def _(p_ref=p_refs[k]):
                g = p_ref[0].astype(F32)
                for s in range(1, N_DEV):
                    g = g + p_ref[s].astype(F32)
                m2 = ADAM_B1 * m_ref[...] + (1.0 - ADAM_B1) * g
                v2 = ADAM_B2 * v_ref[...] + (1.0 - ADAM_B2) * (g * g)
                m_hat = m2 / (1.0 - ADAM_B1 ** ADAM_STEP)
                v_hat = v2 / (1.0 - ADAM_B2 ** ADAM_STEP)
                g_out[...] = g
                d_out[...] = -ADAM_LR * (m_hat / (jnp.sqrt(v_hat) + ADAM_EPS) + ADAM_WD * w_ref[...])
                m_out[...] = m2
                v_out[...] = v2

    def part_spec(k):
        return pl.BlockSpec((N_DEV, tr, cols), lambda l, i: (0, jnp.where(l == k, i, 0), 0))

    spec = pl.BlockSpec((None, tr, cols), lambda l, i: (l, i, 0))
    shp = jax.ShapeDtypeStruct((lyr, rows, cols), F32)
    return pl.pallas_call(
        body, name=name, grid=(lyr, rows // tr),
        in_specs=[part_spec(k) for k in range(lyr)] + [spec, spec, spec],
        out_specs=[spec] * 4, out_shape=[shp] * 4, compiler_params=_cparams(),
    )(*parts, w, m, v)


def _pack(arrays, lanes, row_mult, dtype):
    flat = jnp.concatenate([a.reshape(-1).astype(dtype) for a in arrays])
    unit = lanes * row_mult
    total = -(-flat.shape[0] // unit) * unit
    return jnp.pad(flat, (0, total - flat.shape[0])).reshape(total // lanes, lanes)


def _unpack(packed, shapes):
    flat = packed.reshape(-1)
    out, off = [], 0
    for shp in shapes:
        n = 1
        for d in shp:
            n *= d
        out.append(flat[off:off + n].reshape(shp))
        off += n
    return out


def _pad_slots(w, axis):
    axis = axis % w.ndim
    n = w.shape[axis] // HEAD_DIM
    shp = w.shape[:axis] + (n, HEAD_DIM) + w.shape[axis + 1:]
    pad = [(0, 0)] * (w.ndim + 1)
    pad[axis + 1] = (0, SLOT - HEAD_DIM)
    return jnp.pad(w.reshape(shp), pad).reshape(w.shape[:axis] + (n * SLOT,) + w.shape[axis + 1:])


def _unpad_slots(w, axis, keep=HEAD_DIM):
    axis = axis % w.ndim
    n = w.shape[axis] // SLOT
    shp = w.shape[:axis] + (n, SLOT) + w.shape[axis + 1:]
    idx = [slice(None)] * (w.ndim + 1)
    idx[axis + 1] = slice(0, keep)
    return w.reshape(shp)[tuple(idx)].reshape(w.shape[:axis] + (n * keep,) + w.shape[axis + 1:])


def _mla_in_pad(w):
    z = functools.partial(jnp.zeros, dtype=w.dtype)
    rows = w.shape[0]
    return jnp.concatenate([w[:, :384], z((rows, 64)), w[:, 640:672], z((rows, 32)), w[:, 384:640],
                            _pad_slots(w[:, 672:], 1)], axis=1)


def _mla_in_unpad(d):
    return jnp.concatenate([d[:, :384], d[:, 512:768], d[:, 448:480], _unpad_slots(d[:, 768:], 1)], axis=1)


def _mla_uq_pad(w):
    return jnp.pad(w.reshape(w.shape[0], MLA_HEADS, MLA_QK), ((0, 0), (0, 0), (0, SLOT - MLA_QK))).reshape(
        w.shape[0], MLA_HEADS * SLOT)


def _mla_ukv_pad(w):
    w3 = w.reshape(w.shape[0], MLA_HEADS, 2 * HEAD_DIM)
    pad = ((0, 0), (0, 0), (0, SLOT - HEAD_DIM))
    k = jnp.pad(w3[:, :, :HEAD_DIM], pad).reshape(w.shape[0], -1)
    v = jnp.pad(w3[:, :, HEAD_DIM:], pad).reshape(w.shape[0], -1)
    return jnp.concatenate([k, v], axis=1)


def _mla_ukv_unpad(d):
    hw = MLA_HEADS * SLOT
    k = d[:, :hw].reshape(d.shape[0], MLA_HEADS, SLOT)[:, :, :HEAD_DIM]
    v = d[:, hw:].reshape(d.shape[0], MLA_HEADS, SLOT)[:, :, :HEAD_DIM]
    return jnp.concatenate([k, v], axis=2).reshape(d.shape[0], MLA_HEADS * 2 * HEAD_DIM)


def _join(gathered, axis):
    nd, a, b = gathered.shape
    if axis == 1:
        return gathered.reshape(nd * a, b)
    return gathered.transpose(1, 0, 2).reshape(a, nd * b)


def _split(full, axis):
    r, c = full.shape
    if axis == 1:
        return full.reshape(N_DEV, r // N_DEV, c).astype(BF16)
    return full.reshape(r, N_DEV, c // N_DEV).transpose(1, 0, 2).astype(BF16)


def kernel(x, mem, positions, attn_norm_g, mlp_norm_g, mem_norm_g, final_norm_g, mla_w_in, mla_q_norm_g, mla_kv_norm_g, mla_w_uq, mla_w_ukv, swa_w_in, swa_sinks, w_mem_kv, w_o, mlp_w_up, mlp_w_down, loss_target, m_attn_norm_g, m_mlp_norm_g, m_mem_norm_g, m_final_norm_g, m_mla_w_in, m_mla_q_norm_g, m_mla_kv_norm_g, m_mla_w_uq, m_mla_w_ukv, m_swa_w_in, m_swa_sinks, m_w_mem_kv, m_w_o, m_mlp_w_up, m_mlp_w_down, v_attn_norm_g, v_mlp_norm_g, v_mem_norm_g, v_final_norm_g, v_mla_w_in, v_mla_q_norm_g, v_mla_kv_norm_g, v_mla_w_uq, v_mla_w_ukv, v_swa_w_in, v_swa_sinks, v_w_mem_kv, v_w_o, v_mlp_w_up, v_mlp_w_down):
    given = dict(locals())
    seq = x.shape[1]
    x0 = x.reshape(seq, D_MODEL)
    tgt = loss_target.reshape(seq, D_MODEL)
    mem0 = mem.reshape(N_MEM, D_MODEL)
    pos = positions.reshape(seq).astype(F32)
    pos_col, pos_row = pos.reshape(seq, 1), pos.reshape(1, seq)

    def layer_names(i):
        mixer = ("mla_w_in", "mla_w_uq", "mla_w_ukv") if i % 2 == 0 else ("swa_w_in",)
        return [(n, i // 2) for n in mixer] + [(n, i) for n in ("w_mem_kv", "w_o", "mlp_w_up", "mlp_w_down")]

    def local_weights(names):
        return [given[n][l].astype(BF16) for n, l in names]

    first_attn, first_mlp = layer_names(0)[:-2], layer_names(0)[-2:]
    weights = [dict(zip([n for n, _ in first_attn], _all_gather(local_weights(first_attn), "gather_weights_first")))]
    coming_mlp, first_token = _exchange_start(local_weights(first_mlp), False, "gather_weights_start_0",
                                              after=weights[0]["w_o"])

    consts = _lane_consts()
    tabs = _rope_tables(pos_col, consts)
    slopes = 2.0 ** (-8.0 * (jnp.arange(SWA_HEADS, dtype=F32) + 1.0) / SWA_HEADS)

    mem_n = _rmsnorm_fwd(mem0, 0, D_MODEL, mem_norm_g, "rmsnorm_fwd_mem")

    saved = []
    xc = x0
    for i in range(DEPTH):
        j = i // 2
        wts = weights[i]
        s = {"x_in": xc}
        token = None
        if i + 1 < DEPTH:
            coming, token = _exchange_start(local_weights(layer_names(i + 1)), False,
                                            "gather_weights_start_%d" % (i + 1),
                                            after=first_token if i == 0 else wts["w_o"])
        hn = _rmsnorm_fwd(xc, 0, D_MODEL, attn_norm_g[i], "rmsnorm_fwd", after=token)
        if i % 2 == 0:
            w_in = _mla_in_pad(_join(wts["mla_w_in"], 1))
            w_uq = _mla_uq_pad(_join(wts["mla_w_uq"], 2))
            w_kv = _mla_ukv_pad(_join(wts["mla_w_ukv"], 2))
            proj = _mm(hn, w_in, "nn", F32, "mm_mla_in")
            cqn = _rmsnorm_fwd(proj, 0, MLA_Q_RANK, mla_q_norm_g[j], "rmsnorm_fwd_q")
            ckvn = _rmsnorm_fwd(proj, 2, MLA_KV_RANK, mla_kv_norm_g[j], "rmsnorm_fwd_kv")
            qraw = _mm(cqn, w_uq, "nn", F32, "mm_mla_uq")
            kvraw = _mm(ckvn, w_kv, "nn", F32, "mm_mla_ukv")
            q, k, v = _mla_rope_fwd(qraw, kvraw, proj, tabs)
            o, lse = _mla_attn_fwd(q, k, v)
            qoff = MLA_QOFF
            s.update(w_uq=w_uq, w_kv=w_kv, cqn=cqn, ckvn=ckvn, q=q, k=k, v=v)
        else:
            w_in = _pad_slots(_join(wts["swa_w_in"], 2), 1)
            proj = _mm(hn, w_in, "nn", BF16, "mm_swa_in")
            o, lse = _swa_attn_fwd(proj, pos_col, pos_row, slopes, swa_sinks[j])
            qoff = SWA_QOFF
        w_mem = _pad_slots(_join(wts["w_mem_kv"], 1), 1)
        w_out = _pad_slots(_join(wts["w_o"], 1), 0)
        w_o_mix, w_o_cross = w_out[:SWA_HEADS * SLOT], w_out[SWA_HEADS * SLOT:]
        kvmem = _mm(mem_n, w_mem, "nn", BF16, "mm_mem_kv")
        cross = _cross_attn_fwd(proj, qoff, kvmem)
        x1 = _mm(o, w_o_mix, "nn", F32, "mm_o_mix", res=xc)
        x1 = _mm(cross, w_o_cross, "nn", F32, "mm_o_cross", res=x1)
        hn2 = _rmsnorm_fwd(x1, 0, D_MODEL, mlp_norm_g[i], "rmsnorm_fwd")
        if i == 0:
            wts.update(zip([n for n, _ in first_mlp], _exchange_wait(coming_mlp, hn2, "gather_weights_wait_0")))
        act, act2 = _mm(hn2, wts["mlp_w_up"], "nn", BF16, "mm_mlp_up", epi="relu2", b_blk="cols")
        xc = _mm(act2, wts["mlp_w_down"], "nn", F32, "mm_mlp_down", res=x1, b_blk="rows")
        s.update(hn=hn, w_in=w_in, proj=proj, o=o, lse=lse, qoff=qoff, w_mem=w_mem, w_o_mix=w_o_mix,
                 w_o_cross=w_o_cross, kvmem=kvmem, cross=cross, x1=x1, hn2=hn2, act=act, act2=act2)
        saved.append(s)
        if i + 1 < DEPTH:
            got = _exchange_wait(coming, xc, "gather_weights_wait_%d" % (i + 1))
            weights.append(dict(zip([n for n, _ in layer_names(i + 1)], got)))

    dx, dx_b, dg_final, loss_part = _loss_head(xc, final_norm_g, tgt)
    loss = lax.psum(loss_part[0, 0], MESH_AXES)

    gains = {n: [None] * DEPTH for n in ("attn_norm_g", "mlp_norm_g")}
    for n in ("mla_q_norm_g", "mla_kv_norm_g", "swa_sinks"):
        gains[n] = [None] * 2
    leaving = {}
    token = None
    dmem_n = None
    for i in reversed(range(DEPTH)):
        j = i // 2
        s = saved[i]
        wts = weights[i]
        out = {}
        du = _mm(dx_b, wts["mlp_w_down"], "nt", BF16, "mm_mlp_down_dx", aux=s["act"], epi="mul2aux", b_blk="rows",
                 after=token)
        out["mlp_w_down"] = _mm(s["act2"], dx_b, "tn", BF16, "mm_mlp_down_dw", o_blk="rows")
        out["mlp_w_up"] = _mm(s["hn2"], du, "tn", BF16, "mm_mlp_up_dw", o_blk="cols")
        dx1, dx1_b, dg = _mm(du, wts["mlp_w_up"].transpose(0, 2, 1), "nn", F32, "mm_mlp_up_dx", b_blk="rows",
                             epi="normbwd", norm=(s["x1"], mlp_norm_g[i], dx))
        gains["mlp_norm_g"][i] = dg[0]

        do = _mm(dx1_b, s["w_o_mix"], "nt", BF16, "mm_o_mix_dx")
        dcross = _mm(dx1_b, s["w_o_cross"], "nt", BF16, "mm_o_cross_dx")
        dw_o = jnp.concatenate([_mm(s["o"], dx1_b, "tn", F32, "mm_o_mix_dw"),
                                _mm(s["cross"], dx1_b, "tn", F32, "mm_o_cross_dw")], axis=0)
        out["w_o"] = _split(_unpad_slots(dw_o, 0), 1)
        dqc, dkm, dvm = _cross_attn_bwd(s["proj"], s["qoff"], s["kvmem"], dcross)
        dkvmem = jnp.concatenate([dkm, dvm], axis=1).astype(BF16)
        out["w_mem_kv"] = _split(_unpad_slots(_mm(mem_n, dkvmem, "tn", F32, "mm_mem_kv_dw"), 1), 1)
        dmem_n = _mm(dkvmem, s["w_mem"], "nt", F32, "mm_mem_kv_dx" if dmem_n is None else "mm_mem_kv_dx_acc",
                     res=dmem_n)
        leaving[(i, "main")], token = _exchange_start([out[n] for n, _ in layer_names(i)[-4:]], True,
                                                      "exchange_grads_main_start_%d" % i)

        if i % 2 == 0:
            dq, dk, dv = _mla_attn_bwd(s["q"], s["k"], s["v"], s["o"], do, s["lse"], token)
            dqraw, dkv, dkr = _mla_rope_bwd(dq, dk, dv, tabs, consts)
            dcqn = _mm(dqraw, s["w_uq"], "nt", F32, "mm_mla_uq_dx")
            out["mla_w_uq"] = _split(_unpad_slots(_mm(s["cqn"], dqraw, "tn", F32, "mm_mla_uq_dw"), 1, MLA_QK), 2)
            dckvn = _mm(dkv, s["w_kv"], "nt", F32, "mm_mla_ukv_dx")
            out["mla_w_ukv"] = _split(_mla_ukv_unpad(_mm(s["ckvn"], dkv, "tn", F32, "mm_mla_ukv_dw")), 2)
            dcq, dg = _rmsnorm_bwd(s["proj"], 0, MLA_Q_RANK, mla_q_norm_g[j], dcqn, None, BF16, "rmsnorm_bwd_q")
            gains["mla_q_norm_g"][j] = dg[0]
            dckv, dg = _rmsnorm_bwd(s["proj"], 2, MLA_KV_RANK, mla_kv_norm_g[j], dckvn, None, BF16, "rmsnorm_bwd_kv")
            gains["mla_kv_norm_g"][j] = dg[0]
            dproj = jnp.concatenate([dcq, dkr.astype(BF16), dckv, dqc.astype(BF16)], axis=1)
            in_dx = "mm_mla_in_dx"
            out["mla_w_in"] = _split(_mla_in_unpad(_mm(s["hn"], dproj, "tn", F32, "mm_mla_in_dw")), 1)
        else:
            dq, dk, dv, dsink = _swa_attn_bwd(s["proj"], s["o"], do, s["lse"], pos_col, pos_row, slopes, swa_sinks[j],
                                              token)
            gains["swa_sinks"][j] = dsink[::8, 0]
            dproj = jnp.concatenate([dq, dk, dv, dqc], axis=1).astype(BF16)
            in_dx = "mm_swa_in_dx"
            out["swa_w_in"] = _split(_unpad_slots(_mm(s["hn"], dproj, "tn", F32, "mm_swa_in_dw"), 1), 2)
        dx, dx_b, dg = _mm(dproj, s["w_in"], "nt", F32, in_dx, epi="normbwd", norm=(s["x_in"], attn_norm_g[i], dx1))
        gains["attn_norm_g"][i] = dg[0]

        leaving[(i, "mixer")], token = _exchange_start([out[n] for n, _ in layer_names(i)[:-4]], True,
                                                       "exchange_grads_mixer_start_%d" % i)

    _, dg_mem = _rmsnorm_bwd(mem0, 0, D_MODEL, mem_norm_g, dmem_n, None, BF16, "rmsnorm_bwd_mem")
    gains = {n: jnp.stack(g) for n, g in gains.items()}
    gains["mem_norm_g"] = dg_mem[0]
    gains["final_norm_g"] = dg_final[0]

    result = {}

    def adamw_of(names, received):
        for n in names:
            parts = [received[(n, l)] for l in range(given[n].shape[0])]
            for kind, r in enumerate(_adamw(parts, given[n], given["m_" + n], given["v_" + n], "adamw_" + n)):
                result[(kind, n)] = r

    received = {}
    for i in reversed(range(DEPTH)):
        got = _exchange_wait(leaving[(i, "main")], dx, "exchange_grads_main_wait_%d" % i)
        received.update(zip(layer_names(i)[-4:], got))
    adamw_of(("mlp_w_up", "mlp_w_down", "w_o", "w_mem_kv"), received)
    for i in reversed(range(DEPTH)):
        got = _exchange_wait(leaving[(i, "mixer")], result[(0, "w_mem_kv")], "exchange_grads_mixer_wait_%d" % i)
        received.update(zip(layer_names(i)[:-4], got))
    adamw_of(("mla_w_in", "mla_w_uq", "mla_w_ukv", "swa_w_in"), received)

    rep_shapes = [given[n].shape for n in REPLICATED]
    rep_parts = _all_gather([_pack([gains[n] for n in REPLICATED], SLOT, 8, F32)], "gather_gain_grads")[0]
    rep_packed = [_pack([given[p + n] for n in REPLICATED], SLOT, 8, F32)[None] for p in ("", "m_", "v_")]
    for kind, r in enumerate(_adamw([rep_parts], *rep_packed, "adamw_gains")):
        for n, part in zip(REPLICATED, _unpack(r[0], rep_shapes)):
            result[(kind, n)] = part

    outs = [loss, dx.reshape(1, seq, D_MODEL)]
    for kind in range(4):
        outs += [result[(kind, n)] for n in WEIGHT_ORDER]
    return tuple(outs)
```

```python
import functools

import jax
import jax.numpy as jnp
from jax import lax
from jax.experimental import pallas as pl
from jax.experimental.pallas import tpu as pltpu

F32 = jnp.float32
BF16 = jnp.bfloat16

D_MODEL = 1024
D_FF = 4096
N_MEM = 256
DEPTH = 4
SLOT = 128
HEAD_DIM = 64
MLA_HEADS = 12
MLA_QK = 96
MLA_Q_RANK = 384
MLA_KV_RANK = 256
SWA_HEADS = 12
SWA_KV_HEADS = 4
SWA_GROUP = 3
MEM_HEADS = 4
WINDOW = 128
EPS = 1e-6
NEG = -1e30
ROPE_THETA = 10000.0
N_DEV = 8

ADAM_LR = 0.001
ADAM_B1 = 0.9
ADAM_B2 = 0.999
ADAM_EPS = 1e-08
ADAM_WD = 0.01
ADAM_STEP = 10

TM = 512
TQ_MLA = 1024
MLA_PACK = 2
SWA_PACK = 4
TQ_CROSS = 2048
MM_VMEM_BUDGET = 38 * 1024 * 1024
ADAM_ROWS = 128
VMEM_LIMIT = 56 * 1024 * 1024

MESH_AXES = ("x", "y", "c")

LOG2_E = 1.4426950408889634
MLA_SCALE = MLA_QK ** -0.5
MLA_Q_SCALE = MLA_SCALE * LOG2_E

MLA_PAD_IN = 384 + SLOT + 256 + MEM_HEADS * SLOT
MLA_QOFF = (384 + SLOT + 256) // SLOT
SWA_PAD_IN = (SWA_HEADS + 2 * SWA_KV_HEADS + MEM_HEADS) * SLOT
SWA_QOFF = SWA_HEADS + 2 * SWA_KV_HEADS

SHARDED = (
    ("mla_w_in", 1), ("mla_w_uq", 2), ("mla_w_ukv", 2), ("swa_w_in", 2),
    ("w_mem_kv", 1), ("w_o", 1), ("mlp_w_up", 2), ("mlp_w_down", 1),
)
REPLICATED = ("attn_norm_g", "mlp_norm_g", "mem_norm_g", "final_norm_g",
              "mla_q_norm_g", "mla_kv_norm_g", "swa_sinks")
WEIGHT_ORDER = ("attn_norm_g", "mlp_norm_g", "mem_norm_g", "final_norm_g", "mla_w_in",
                "mla_q_norm_g", "mla_kv_norm_g", "mla_w_uq", "mla_w_ukv", "swa_w_in",
                "swa_sinks", "w_mem_kv", "w_o", "mlp_w_up", "mlp_w_down")


def _cparams():
    return pltpu.CompilerParams(vmem_limit_bytes=VMEM_LIMIT)


_DIMS = {"nn": (((1,), (0,)), ((), ())), "nt": (((1,), (1,)), ((), ())), "tn": (((0,), (0,)), ((), ()))}


def _mm_tiles(m, n, k, a_bytes, b_bytes, o_bytes, extra_bytes, tm_fixed, tn_fixed):
    best = None
    for tm in ([tm_fixed] if tm_fixed else [t for t in (4096, 2048, 1024, 512, 256, 128) if m % t == 0] or [m]):
        for tn in ([tn_fixed] if tn_fixed else [t for t in range(1024, 0, -SLOT) if n % t == 0] or [n]):
            need = 2 * (tm * k * a_bytes + k * tn * b_bytes + tm * tn * (o_bytes + extra_bytes))
            need += tm * tn * 4
            if need <= MM_VMEM_BUDGET and (best is None or tm * tn > best[0] * best[1]):
                best = (tm, tn)
    assert best is not None, (m, n, k)
    return best


def _mm(a, b, mode, out_dtype, name, res=None, aux=None, epi=None, b_blk=None, o_blk=None, after=None, norm=None):
    if b_blk is not None:
        nb, br, bc = b.shape
        b_shape = (nb * br, bc) if b_blk == "rows" else (br, nb * bc)
    else:
        b_shape = b.shape
    if mode == "nn":
        (m, k), (k2, n) = a.shape, b_shape
    elif mode == "nt":
        (m, k), (n, k2) = a.shape, b_shape
    else:
        (k, m), (k2, n) = a.shape, b_shape
    assert k == k2, (a.shape, b_shape, mode)
    k_blocked = b_blk is not None and (b_blk == "rows") == (mode != "nt")
    tn_fixed = None
    if b_blk is not None and not k_blocked:
        tn_fixed = br if b_blk == "rows" else bc
    if o_blk == "cols":
        tn_fixed = n // N_DEV
    tm_fixed = m // N_DEV if o_blk == "rows" else None
    has_res, has_aux, has_norm = res is not None, aux is not None, epi == "normbwd"
    assert o_blk is None or not (has_res or has_aux or has_norm)
    n_out = 2 if epi == "relu2" else 1
    if has_norm:
        tn_fixed = n
        o_bytes, extra_bytes = 4 + 2, 4 + 4
    else:
        o_bytes = n_out * jnp.dtype(out_dtype).itemsize
        extra_bytes = (4 if has_res else 0) + (aux.dtype.itemsize if has_aux else 0)
    tm, tn = _mm_tiles(m, n, k, a.dtype.itemsize, b.dtype.itemsize, o_bytes, extra_bytes, tm_fixed, tn_fixed)
    dims = _DIMS[mode]
    if mode == "tn":
        a_spec = pl.BlockSpec((k, tm), lambda i, j: (0, i))
    else:
        a_spec = pl.BlockSpec((tm, k), lambda i, j: (i, 0))
    if b_blk is None:
        if mode == "nt":
            b_spec = pl.BlockSpec((tn, k), lambda i, j: (j, 0))
        else:
            b_spec = pl.BlockSpec((k, tn), lambda i, j: (0, j))
    elif k_blocked and mode == "nt":
        b_spec = pl.BlockSpec((N_DEV, tn, bc), lambda i, j: (0, j, 0))
    elif k_blocked:
        b_spec = pl.BlockSpec((N_DEV, br, tn), lambda i, j: (0, 0, j))
    elif mode == "nt":
        b_spec = pl.BlockSpec((None, tn, k), lambda i, j: (j, 0, 0))
    else:
        b_spec = pl.BlockSpec((None, k, tn), lambda i, j: (j, 0, 0))
    if o_blk is None:
        o_spec = pl.BlockSpec((tm, tn), lambda i, j: (i, j))
        o_shape = (m, n)
    elif o_blk == "rows":
        o_spec = pl.BlockSpec((None, tm, tn), lambda i, j: (i, 0, j))
        o_shape = (N_DEV, tm, n)
    else:
        o_spec = pl.BlockSpec((None, tm, tn), lambda i, j: (j, i, 0))
        o_shape = (N_DEV, m, tn)

    def body(*refs):
        a_ref, b_ref = refs[0], refs[1]
        pos = 2
        res_ref = aux_ref = None
        if has_res:
            res_ref = refs[pos]
            pos += 1
        if has_aux:
            aux_ref = refs[pos]
            pos += 1
        if has_norm:
            x_ref, g_ref, dres_ref = refs[pos:pos + 3]
            pos += 3
        if after is not None:
            pos += 1
        outs = refs[pos:]
        if k_blocked and mode == "nt":
            r = None
            for d in range(N_DEV):
                part = lax.dot_general(a_ref[:, d * bc:(d + 1) * bc].astype(BF16), b_ref[d].astype(BF16), dims,
                                       preferred_element_type=F32)
                r = part if r is None else r + part
        else:
            bv = b_ref[...].reshape(k, tn) if k_blocked else b_ref[...]
            r = lax.dot_general(a_ref[...].astype(BF16), bv.astype(BF16), dims, preferred_element_type=F32)
        if epi == "relu2":
            r = jnp.maximum(r, 0.0)
            outs[0][...] = r.astype(outs[0].dtype)
            outs[1][...] = (r * r).astype(outs[1].dtype)
        elif has_norm:
            xv = x_ref[...]
            rs = lax.rsqrt(jnp.mean(xv * xv, axis=1, keepdims=True) + EPS)
            xh = xv * rs
            dxh = r * g_ref[...]
            dx = rs * (dxh - xh * jnp.mean(dxh * xh, axis=1, keepdims=True)) + dres_ref[...]
            outs[0][...] = dx
            outs[1][...] = dx.astype(BF16)

            @pl.when(pl.program_id(0) == 0)
            def _():
                outs[2][...] = jnp.zeros_like(outs[2])

            outs[2][...] += jnp.sum(r * xh, axis=0, keepdims=True)
        else:
            if epi == "mul2aux":
                r = r * (2.0 * aux_ref[...].astype(F32))
            if has_res:
                r = r + res_ref[...]
            outs[0][...] = r.astype(outs[0].dtype)

    in_specs = [a_spec, b_spec]
    args = [a, b]
    if has_res:
        in_specs.append(o_spec)
        args.append(res)
    if has_aux:
        in_specs.append(o_spec)
        args.append(aux)
    vec_spec = pl.BlockSpec((1, n), lambda i, j: (0, 0))
    if has_norm:
        in_specs += [o_spec, vec_spec, o_spec]
        args += [norm[0], norm[1].reshape(1, n), norm[2]]
    if after is not None:
        in_specs.append(pl.BlockSpec(memory_space=pl.ANY))
        args.append(after)
    if has_norm:
        out_specs = [o_spec, o_spec, vec_spec]
        out_shape = [jax.ShapeDtypeStruct(o_shape, F32), jax.ShapeDtypeStruct(o_shape, BF16),
                     jax.ShapeDtypeStruct((1, n), F32)]
    else:
        out_specs = [o_spec] * n_out
        out_shape = [jax.ShapeDtypeStruct(o_shape, out_dtype)] * n_out
    out = pl.pallas_call(
        body, name=name, grid=(m // tm, n // tn),
        in_specs=in_specs, out_specs=out_specs, out_shape=out_shape, compiler_params=_cparams(),
    )(*args)
    return out if len(out) > 1 else out[0]


def _rmsnorm_fwd(xarr, colblk, width, g, name, after=None):
    rows = xarr.shape[0]
    tm = min(TM, rows)

    def body(x_ref, g_ref, *rest):
        y_ref = rest[-1]
        x = x_ref[...].astype(F32)
        r = lax.rsqrt(jnp.mean(x * x, axis=1, keepdims=True) + EPS)
        y_ref[...] = (x * r * g_ref[...]).astype(y_ref.dtype)

    in_specs = [pl.BlockSpec((tm, width), lambda i: (i, colblk)), pl.BlockSpec((1, width), lambda i: (0, 0))]
    args = [xarr, g.reshape(1, width)]
    if after is not None:
        in_specs.append(pl.BlockSpec(memory_space=pl.ANY))
        args.append(after)
    return pl.pallas_call(
        body, name=name, grid=(rows // tm,), in_specs=in_specs,
        out_specs=pl.BlockSpec((tm, width), lambda i: (i, 0)),
        out_shape=jax.ShapeDtypeStruct((rows, width), BF16), compiler_params=_cparams(),
    )(*args)


def _rmsnorm_bwd(xarr, colblk, width, g, dy, dres, out_dtype, name):
    rows = xarr.shape[0]
    tm = min(TM, rows)
    has_res = dres is not None

    def body(*refs):
        x_ref, g_ref, dy_ref = refs[0], refs[1], refs[2]
        dres_ref = refs[3] if has_res else None
        dx_ref, dg_ref = refs[-2], refs[-1]
        x = x_ref[...].astype(F32)
        dyv = dy_ref[...].astype(F32)
        r = lax.rsqrt(jnp.mean(x * x, axis=1, keepdims=True) + EPS)
        xh = x * r
        dxh = dyv * g_ref[...]
        dx = r * (dxh - xh * jnp.mean(dxh * xh, axis=1, keepdims=True))
        if has_res:
            dx = dx + dres_ref[...]
        dx_ref[...] = dx.astype(dx_ref.dtype)

        @pl.when(pl.program_id(0) == 0)
        def _():
            dg_ref[...] = jnp.zeros_like(dg_ref)

        dg_ref[...] += jnp.sum(dyv * xh, axis=0, keepdims=True)

    row_spec = pl.BlockSpec((tm, width), lambda i: (i, 0))
    vec_spec = pl.BlockSpec((1, width), lambda i: (0, 0))
    in_specs = [pl.BlockSpec((tm, width), lambda i: (i, colblk)), vec_spec, row_spec]
    args = [xarr, g.reshape(1, width), dy]
    if has_res:
        in_specs.append(row_spec)
        args.append(dres)
    return pl.pallas_call(
        body, name=name, grid=(rows // tm,), in_specs=in_specs, out_specs=[row_spec, vec_spec],
        out_shape=[jax.ShapeDtypeStruct((rows, width), out_dtype), jax.ShapeDtypeStruct((1, width), F32)],
        compiler_params=_cparams(),
    )(*args)


def _loss_head(x, g, tgt):
    rows, width = x.shape
    tm = min(TM, rows)

    def body(x_ref, g_ref, t_ref, dx_ref, dxb_ref, dg_ref, loss_ref):
        xv = x_ref[...]
        gv = g_ref[...]
        r = lax.rsqrt(jnp.mean(xv * xv, axis=1, keepdims=True) + EPS)
        xh = xv * r
        err = xh * gv - t_ref[...]
        part = 0.5 * jnp.sum(jnp.mean(err * err, axis=1, keepdims=True), axis=0, keepdims=True)
        dyv = err * (1.0 / width)
        dxh = dyv * gv
        dxv = r * (dxh - xh * jnp.mean(dxh * xh, axis=1, keepdims=True))
        dx_ref[...] = dxv
        dxb_ref[...] = dxv.astype(BF16)

        @pl.when(pl.program_id(0) == 0)
        def _():
            dg_ref[...] = jnp.zeros_like(dg_ref)
            loss_ref[...] = jnp.zeros_like(loss_ref)

        dg_ref[...] += jnp.sum(dyv * xh, axis=0, keepdims=True)
        loss_ref[...] += jnp.broadcast_to(part, loss_ref.shape)

    row_spec = pl.BlockSpec((tm, width), lambda i: (i, 0))
    vec_spec = pl.BlockSpec((1, width), lambda i: (0, 0))
    return pl.pallas_call(
        body, name="loss_head", grid=(rows // tm,), in_specs=[row_spec, vec_spec, row_spec],
        out_specs=[row_spec, row_spec, vec_spec, pl.BlockSpec((1, SLOT), lambda i: (0, 0))],
        out_shape=[jax.ShapeDtypeStruct((rows, width), F32), jax.ShapeDtypeStruct((rows, width), BF16),
                   jax.ShapeDtypeStruct((1, width), F32), jax.ShapeDtypeStruct((1, SLOT), F32)],
        compiler_params=_cparams(),
    )(x, g.reshape(1, width), tgt)


def _lane_consts():
    half = 16
    inv = ROPE_THETA ** (-(jnp.arange(half, dtype=F32) * 2.0) / 32)
    lane = jnp.arange(SLOT)
    first = (lane >= 64) & (lane < 80)
    second = (lane >= 80) & (lane < 96)
    inv_lane = jnp.where(first | second, inv[(lane - 64) % half], 0.0)
    rows = [inv_lane, (lane < 64).astype(F32), first.astype(F32), second.astype(F32)]
    rows += [jnp.zeros((SLOT,), F32)] * 4
    return jnp.stack(rows).astype(F32)


def _rope_tables(pos_col, consts):
    rows = pos_col.shape[0]
    tm = min(TM, rows)

    def body(p_ref, k_ref, c_ref, s1_ref, s2_ref):
        ang = p_ref[...] * k_ref[0:1, :]
        cos, sin = jnp.cos(ang), jnp.sin(ang)
        first, second = k_ref[2:3, :], k_ref[3:4, :]
        c_ref[...] = k_ref[1:2, :] + (first + second) * cos
        s1_ref[...] = -first * sin
        s2_ref[...] = second * sin

    spec = pl.BlockSpec((tm, SLOT), lambda i: (i, 0))
    shp = jax.ShapeDtypeStruct((rows, SLOT), F32)
    return pl.pallas_call(
        body, name="rope_tables", grid=(rows // tm,),
        in_specs=[pl.BlockSpec((tm, 1), lambda i: (i, 0)), pl.BlockSpec((8, SLOT), lambda i: (0, 0))],
        out_specs=[spec, spec, spec], out_shape=[shp, shp, shp], compiler_params=_cparams(),
    )(pos_col, consts)


def _rot(xv, c, s1, s2):
    return xv * c + pltpu.roll(xv, SLOT - 16, 1) * s1 + pltpu.roll(xv, 16, 1) * s2


def _rot_t(dy, c, s1, s2):
    return dy * c + pltpu.roll(dy * s1, 16, 1) + pltpu.roll(dy * s2, SLOT - 16, 1)


def _mla_rope_fwd(qraw, kvraw, proj, tabs):
    rows = qraw.shape[0]
    tm = min(256, rows)
    hw = MLA_HEADS * SLOT

    def body(q_ref, kv_ref, kr_ref, c_ref, s1_ref, s2_ref, qo, ko, vo):
        c, s1, s2 = c_ref[...], s1_ref[...], s2_ref[...]
        kr = _rot(kr_ref[...], c, s1, s2)
        for h in range(MLA_HEADS):
            sl = slice(h * SLOT, (h + 1) * SLOT)
            qo[:, sl] = (_rot(q_ref[:, sl], c, s1, s2) * MLA_Q_SCALE).astype(BF16)
            ko[:, sl] = (kv_ref[:, sl] + kr).astype(BF16)
            vo[:, sl] = kv_ref[:, hw + h * SLOT:hw + (h + 1) * SLOT].astype(BF16)

    tab = pl.BlockSpec((tm, SLOT), lambda i: (i, 0))
    wide = pl.BlockSpec((tm, hw), lambda i: (i, 0))
    shp = jax.ShapeDtypeStruct((rows, hw), BF16)
    return pl.pallas_call(
        body, name="mla_rope_fwd", grid=(rows // tm,),
        in_specs=[wide, pl.BlockSpec((tm, 2 * hw), lambda i: (i, 0)), pl.BlockSpec((tm, SLOT), lambda i: (i, 3)),
                  tab, tab, tab],
        out_specs=[wide, wide, wide], out_shape=[shp, shp, shp], compiler_params=_cparams(),
    )(qraw, kvraw, proj, *tabs)


def _mla_rope_bwd(dq, dk, dv, tabs, consts):
    rows = dq.shape[0]
    tm = min(256, rows)
    hw = MLA_HEADS * SLOT

    def body(dq_ref, dk_ref, dv_ref, c_ref, s1_ref, s2_ref, k_ref, dqo, dkvo, dkro):
        c, s1, s2 = c_ref[...], s1_ref[...], s2_ref[...]
        ksum = jnp.zeros((tm, SLOT), F32)
        for h in range(MLA_HEADS):
            sl = slice(h * SLOT, (h + 1) * SLOT)
            dqo[:, sl] = _rot_t(dq_ref[:, sl], c, s1, s2).astype(BF16)
            dkh = dk_ref[:, sl]
            ksum = ksum + dkh
            dkvo[:, sl] = dkh.astype(BF16)
            dkvo[:, hw + h * SLOT:hw + (h + 1) * SLOT] = dv_ref[:, sl].astype(BF16)
        dkro[...] = _rot_t(ksum, c, s1, s2) * (k_ref[2:3, :] + k_ref[3:4, :])

    tab = pl.BlockSpec((tm, SLOT), lambda i: (i, 0))
    wide = pl.BlockSpec((tm, hw), lambda i: (i, 0))
    return pl.pallas_call(
        body, name="mla_rope_bwd", grid=(rows // tm,),
        in_specs=[wide, wide, wide, tab, tab, tab, pl.BlockSpec((8, SLOT), lambda i: (0, 0))],
        out_specs=[wide, pl.BlockSpec((tm, 2 * hw), lambda i: (i, 0)), tab],
        out_shape=[jax.ShapeDtypeStruct((rows, hw), BF16), jax.ShapeDtypeStruct((rows, 2 * hw), BF16),
                   jax.ShapeDtypeStruct((rows, SLOT), F32)],
        compiler_params=_cparams(),
    )(dq, dk, dv, *tabs, consts)


def _nt(a, b):
    return lax.dot_general(a, b, _DIMS["nt"], preferred_element_type=F32)


def _tn(a, b):
    return lax.dot_general(a, b, _DIMS["tn"], preferred_element_type=F32)


def _nn(a, b):
    return lax.dot_general(a, b, _DIMS["nn"], preferred_element_type=F32)


def _causal(rows, keys):
    shp = (rows.stop - rows.start, keys.stop - keys.start)
    return (keys.start + lax.broadcasted_iota(jnp.int32, shp, 1)
            <= rows.start + lax.broadcasted_iota(jnp.int32, shp, 0))


def _mla_attn_fwd(q, k, v):
    rows = q.shape[0]
    t = min(TQ_MLA, rows)
    nt = rows // t
    wide = MLA_PACK * SLOT

    def body(q_ref, k_ref, v_ref, o_ref, lse_ref, m_sc, l_sc, acc_sc):
        i, j = pl.program_id(1), pl.program_id(2)

        @pl.when(j == 0)
        def _():
            m_sc[...] = jnp.full_like(m_sc, NEG)
            l_sc[...] = jnp.zeros_like(l_sc)
            acc_sc[...] = jnp.zeros_like(acc_sc)

        def step(diagonal):
            for hh in range(MLA_PACK):
                sl = slice(hh * SLOT, (hh + 1) * SLOT)
                s = _nt(q_ref[:, sl], k_ref[:, sl])
                if diagonal:
                    s = jnp.where(_causal(slice(0, t), slice(0, t)), s, NEG)
                m_prev = m_sc[hh]
                m_new = jnp.maximum(m_prev, jnp.max(s, axis=1, keepdims=True))
                p = jnp.exp2(s - m_new)
                alpha = jnp.exp2(m_prev - m_new)
                l_new = alpha * l_sc[hh] + jnp.sum(p, axis=1, keepdims=True)
                acc = alpha * acc_sc[:, sl] + _nn(p.astype(BF16), v_ref[:, sl])
                if diagonal:
                    o_ref[:, sl] = (acc / l_new).astype(o_ref.dtype)
                    lse_ref[:, sl] = jnp.broadcast_to(m_new + jnp.log(l_new) * LOG2_E, (t, SLOT))
                else:
                    m_sc[hh] = m_new
                    l_sc[hh] = l_new
                    acc_sc[:, sl] = acc

        @pl.when(j < i)
        def _():
            step(False)

        @pl.when(j == i)
        def _():
            step(True)

    q_spec = pl.BlockSpec((t, wide), lambda h, i, j: (i, h))
    kv_spec = pl.BlockSpec((t, wide), lambda h, i, j: (jnp.minimum(j, i), h))
    return pl.pallas_call(
        body, name="mla_attn_fwd", grid=(MLA_HEADS // MLA_PACK, nt, nt),
        in_specs=[q_spec, kv_spec, kv_spec], out_specs=[q_spec, q_spec],
        out_shape=[jax.ShapeDtypeStruct(q.shape, BF16), jax.ShapeDtypeStruct(q.shape, F32)],
        scratch_shapes=[pltpu.VMEM((MLA_PACK, t, 1), F32), pltpu.VMEM((MLA_PACK, t, 1), F32),
                        pltpu.VMEM((t, wide), F32)],
        compiler_params=_cparams(),
    )(q, k, v)


def _mla_attn_bwd(q, k, v, o, do, lse, after):
    rows = q.shape[0]
    t = min(TQ_MLA, rows)
    nt = rows // t
    wide = MLA_PACK * SLOT

    def body(q_ref, k_ref, v_ref, o_ref, do_ref, lse_ref, after_ref, dq_ref, dk_ref, dv_ref, dk_sc, dv_sc):
        j, i = pl.program_id(1), pl.program_id(2)

        @pl.when((j == 0) & (i == 0))
        def _():
            dq_ref[...] = jnp.zeros_like(dq_ref)

        @pl.when(i == 0)
        def _():
            dk_sc[...] = jnp.zeros_like(dk_sc)
            dv_sc[...] = jnp.zeros_like(dv_sc)

        def chunk(hh, rows, keys, masked):
            sl = slice(hh * SLOT, (hh + 1) * SLOT)
            n_rows = rows.stop - rows.start
            qv, kv, dov = q_ref[rows, sl], k_ref[keys, sl], do_ref[rows, sl]
            s = _nt(qv, kv)
            if masked:
                s = jnp.where(_causal(rows, keys), s, NEG)
            p = jnp.exp2(s - lse_ref[rows, hh * SLOT:hh * SLOT + 1])
            delta = jnp.sum(dov.astype(F32) * o_ref[rows, sl].astype(F32), axis=1, keepdims=True)
            dp = _nt(dov, v_ref[keys, sl])
            ds = (p * (dp - delta)).astype(BF16)
            dv_sc[keys, sl] += _tn(p.astype(BF16), dov)
            dk_sc[keys, sl] += _tn(ds, qv)
            r0 = pl.multiple_of(i * t + rows.start, n_rows)
            dq_ref[pl.ds(r0, n_rows), sl] += _nn(ds, kv) * MLA_SCALE

        @pl.when(i > j)
        def _():
            for hh in range(MLA_PACK):
                chunk(hh, slice(0, t), slice(0, t), False)

        @pl.when(i == j)
        def _():
            for hh in range(MLA_PACK):
                chunk(hh, slice(0, t), slice(0, t // 2), True)
                chunk(hh, slice(t // 2, t), slice(t // 2, t), True)

        @pl.when(i == nt - 1)
        def _():
            dk_ref[...] = dk_sc[...] * (1.0 / LOG2_E)
            dv_ref[...] = dv_sc[...]

    q_spec = pl.BlockSpec((t, wide), lambda h, j, i: (jnp.maximum(i, j), h))
    kv_spec = pl.BlockSpec((t, wide), lambda h, j, i: (j, h))
    head_spec = pl.BlockSpec((rows, wide), lambda h, j, i: (0, h))
    shp = jax.ShapeDtypeStruct(q.shape, F32)
    return pl.pallas_call(
        body, name="mla_attn_bwd", grid=(MLA_HEADS // MLA_PACK, nt, nt),
        in_specs=[q_spec, kv_spec, kv_spec, q_spec, q_spec, q_spec, pl.BlockSpec(memory_space=pl.ANY)],
        out_specs=[head_spec, kv_spec, kv_spec], out_shape=[shp, shp, shp],
        scratch_shapes=[pltpu.VMEM((t, wide), F32), pltpu.VMEM((t, wide), F32)],
        compiler_params=_cparams(),
    )(q, k, v, o, do, lse, after)


def _swa_specs(t):
    def prev(i):
        return jnp.maximum(i - 1, 0)
    kw = SWA_PACK * SLOT
    k0, v0 = SWA_HEADS // SWA_PACK, (SWA_HEADS + SWA_KV_HEADS) // SWA_PACK
    q3 = pl.BlockSpec((t, SWA_PACK * SWA_GROUP * SLOT), lambda h, i: (i, h))
    kp = pl.BlockSpec((t, kw), lambda h, i: (prev(i), k0 + h))
    kc = pl.BlockSpec((t, kw), lambda h, i: (i, k0 + h))
    vp = pl.BlockSpec((t, kw), lambda h, i: (prev(i), v0 + h))
    vc = pl.BlockSpec((t, kw), lambda h, i: (i, v0 + h))
    pcol = pl.BlockSpec((t, 1), lambda h, i: (i, 0))
    prow_p = pl.BlockSpec((1, t), lambda h, i: (0, prev(i)))
    prow_c = pl.BlockSpec((1, t), lambda h, i: (0, i))
    return [q3, kp, kc, vp, vc, pcol, prow_p, prow_c]


def _stack(ref, first):
    return jnp.concatenate([ref[:, (first + g) * SLOT:(first + g + 1) * SLOT] for g in range(SWA_GROUP)], axis=0)


def _swa_logits(q3, kp, kc, pq, pkp, pkc, slope_ref, kvh, i, t):
    r = lax.broadcasted_iota(jnp.int32, (t, t), 0)
    c = lax.broadcasted_iota(jnp.int32, (t, t), 1)
    ok_c = c <= r
    ok_p = (c - r) > jnp.where(i > 0, 0, t)
    dist_p, dist_c = pq - pkp, pq - pkc
    s_p3 = _nt(q3, kp) * (HEAD_DIM ** -0.5)
    s_c3 = _nt(q3, kc) * (HEAD_DIM ** -0.5)
    out = []
    for g in range(SWA_GROUP):
        slope = slope_ref[kvh * SWA_GROUP + g]
        rows = slice(g * t, (g + 1) * t)
        out.append((jnp.where(ok_p, s_p3[rows] - slope * dist_p, NEG),
                    jnp.where(ok_c, s_c3[rows] - slope * dist_c, NEG)))
    return out


def _swa_attn_fwd(proj, pos_col, pos_row, slopes, sinks):
    rows = proj.shape[0]
    t = WINDOW
    hw = SWA_HEADS * SLOT

    def body(slope_ref, sink_ref, q_ref, kp_ref, kc_ref, vp_ref, vc_ref, pq_ref, pkp_ref, pkc_ref, o_ref, lse_ref):
        i = pl.program_id(1)
        for kv in range(SWA_PACK):
            kvh = pl.program_id(0) * SWA_PACK + kv
            ksl = slice(kv * SLOT, (kv + 1) * SLOT)
            logits = _swa_logits(_stack(q_ref, kv * SWA_GROUP), kp_ref[:, ksl], kc_ref[:, ksl], pq_ref[...],
                                 pkp_ref[...], pkc_ref[...], slope_ref, kvh, i, t)
            e_p, e_c, norm = [], [], []
            for g, (s_p, s_c) in enumerate(logits):
                sl = slice((kv * SWA_GROUP + g) * SLOT, (kv * SWA_GROUP + g + 1) * SLOT)
                sink = sink_ref[kvh * SWA_GROUP + g]
                m = jnp.maximum(jnp.maximum(jnp.max(s_p, axis=1, keepdims=True),
                                            jnp.max(s_c, axis=1, keepdims=True)), sink)
                ep, ec = jnp.exp(s_p - m), jnp.exp(s_c - m)
                l = jnp.sum(ep, axis=1, keepdims=True) + jnp.sum(ec, axis=1, keepdims=True) + jnp.exp(sink - m)
                e_p.append(ep.astype(BF16))
                e_c.append(ec.astype(BF16))
                norm.append(l)
                lse_ref[:, sl] = jnp.broadcast_to(m + jnp.log(l), (t, SLOT))
            acc = (_nn(jnp.concatenate(e_p, axis=0), vp_ref[:, ksl])
                   + _nn(jnp.concatenate(e_c, axis=0), vc_ref[:, ksl]))
            for g in range(SWA_GROUP):
                sl = slice((kv * SWA_GROUP + g) * SLOT, (kv * SWA_GROUP + g + 1) * SLOT)
                o_ref[:, sl] = (acc[g * t:(g + 1) * t] / norm[g]).astype(o_ref.dtype)

    smem = pl.BlockSpec(memory_space=pltpu.SMEM)
    out_spec = pl.BlockSpec((t, SWA_PACK * SWA_GROUP * SLOT), lambda h, i: (i, h))
    return pl.pallas_call(
        body, name="swa_attn_fwd", grid=(SWA_KV_HEADS // SWA_PACK, rows // t),
        in_specs=[smem, smem] + _swa_specs(t), out_specs=[out_spec, out_spec],
        out_shape=[jax.ShapeDtypeStruct((rows, hw), BF16), jax.ShapeDtypeStruct((rows, hw), F32)],
        compiler_params=_cparams(),
    )(slopes, sinks, proj, proj, proj, proj, proj, pos_col, pos_row, pos_row)


def _swa_attn_bwd(proj, o, do, lse, pos_col, pos_row, slopes, sinks, after):
    rows = proj.shape[0]
    t = WINDOW
    hw = SWA_HEADS * SLOT
    scale = HEAD_DIM ** -0.5

    def body(slope_ref, sink_ref, q_ref, kp_ref, kc_ref, vp_ref, vc_ref, pq_ref, pkp_ref, pkc_ref,
             o_ref, do_ref, lse_ref, after_ref, dq_ref, dk_ref, dv_ref, dsink_ref):
        i = pl.program_id(1)

        @pl.when(i == 0)
        def _():
            dk_ref[...] = jnp.zeros_like(dk_ref)
            dv_ref[...] = jnp.zeros_like(dv_ref)
            dsink_ref[...] = jnp.zeros_like(dsink_ref)

        r_c = pl.multiple_of(i * t, t)
        r_p = pl.multiple_of(jnp.maximum(i - 1, 0) * t, t)
        for kv in range(SWA_PACK):
            kvh = pl.program_id(0) * SWA_PACK + kv
            ksl = slice(kv * SLOT, (kv + 1) * SLOT)
            q3, do3 = _stack(q_ref, kv * SWA_GROUP), _stack(do_ref, kv * SWA_GROUP)
            logits = _swa_logits(q3, kp_ref[:, ksl], kc_ref[:, ksl], pq_ref[...], pkp_ref[...], pkc_ref[...],
                                 slope_ref, kvh, i, t)
            dp_p3, dp_c3 = _nt(do3, vp_ref[:, ksl]), _nt(do3, vc_ref[:, ksl])
            p_p, p_c, ds_p, ds_c = [], [], [], []
            for g, (s_p, s_c) in enumerate(logits):
                head = kv * SWA_GROUP + g
                sl = slice(head * SLOT, (head + 1) * SLOT)
                rws = slice(g * t, (g + 1) * t)
                lse_g = lse_ref[:, head * SLOT:head * SLOT + 1]
                pp, pc = jnp.exp(s_p - lse_g), jnp.exp(s_c - lse_g)
                delta = jnp.sum(do_ref[:, sl].astype(F32) * o_ref[:, sl].astype(F32), axis=1, keepdims=True)
                p_p.append(pp.astype(BF16))
                p_c.append(pc.astype(BF16))
                ds_p.append((pp * (dp_p3[rws] - delta)).astype(BF16))
                ds_c.append((pc * (dp_c3[rws] - delta)).astype(BF16))
                sink = sink_ref[kvh * SWA_GROUP + g]
                dsink = -jnp.sum(jnp.exp(sink - lse_g) * delta, axis=0, keepdims=True)
                dsink_ref[head * 8:(head + 1) * 8, :] += jnp.broadcast_to(dsink, (8, SLOT))
            p_p3, p_c3 = jnp.concatenate(p_p, axis=0), jnp.concatenate(p_c, axis=0)
            ds_p3, ds_c3 = jnp.concatenate(ds_p, axis=0), jnp.concatenate(ds_c, axis=0)
            dq3 = (_nn(ds_p3, kp_ref[:, ksl]) + _nn(ds_c3, kc_ref[:, ksl])) * scale
            for g in range(SWA_GROUP):
                head = kv * SWA_GROUP + g
                dq_ref[:, head * SLOT:(head + 1) * SLOT] = dq3[g * t:(g + 1) * t]
            dk_ref[pl.ds(r_c, t), ksl] += _tn(ds_c3, q3) * scale
            dv_ref[pl.ds(r_c, t), ksl] += _tn(p_c3, do3)
            dk_ref[pl.ds(r_p, t), ksl] += _tn(ds_p3, q3) * scale
            dv_ref[pl.ds(r_p, t), ksl] += _tn(p_p3, do3)

    smem = pl.BlockSpec(memory_space=pltpu.SMEM)
    qlike = pl.BlockSpec((t, SWA_PACK * SWA_GROUP * SLOT), lambda h, i: (i, h))
    kv_out = pl.BlockSpec((rows, SWA_PACK * SLOT), lambda h, i: (0, h))
    return pl.pallas_call(
        body, name="swa_attn_bwd", grid=(SWA_KV_HEADS // SWA_PACK, rows // t),
        in_specs=[smem, smem] + _swa_specs(t) + [qlike, qlike, qlike, pl.BlockSpec(memory_space=pl.ANY)],
        out_specs=[qlike, kv_out, kv_out,
                   pl.BlockSpec((SWA_PACK * SWA_GROUP * 8, SLOT), lambda h, i: (h, 0))],
        out_shape=[jax.ShapeDtypeStruct((rows, hw), F32), jax.ShapeDtypeStruct((rows, SWA_KV_HEADS * SLOT), F32),
                   jax.ShapeDtypeStruct((rows, SWA_KV_HEADS * SLOT), F32),
                   jax.ShapeDtypeStruct((SWA_HEADS * 8, SLOT), F32)],
        compiler_params=_cparams(),
    )(slopes, sinks, proj, proj, proj, proj, proj, pos_col, pos_row, pos_row, o, do, lse, after)


def _cross_attn_fwd(proj, qoff, kvmem):
    rows = proj.shape[0]
    t = min(TQ_CROSS, rows)

    def body(q_ref, k_ref, v_ref, o_ref):
        s = _nt(q_ref[...].astype(BF16), k_ref[...]) * (HEAD_DIM ** -0.5)
        e = jnp.exp(s - jnp.max(s, axis=1, keepdims=True))
        p = e / jnp.sum(e, axis=1, keepdims=True)
        o_ref[...] = _nn(p.astype(BF16), v_ref[...]).astype(o_ref.dtype)

    return pl.pallas_call(
        body, name="cross_attn_fwd", grid=(rows // t, MEM_HEADS),
        in_specs=[pl.BlockSpec((t, SLOT), lambda i, h: (i, qoff + h)),
                  pl.BlockSpec((N_MEM, SLOT), lambda i, h: (0, h)),
                  pl.BlockSpec((N_MEM, SLOT), lambda i, h: (0, MEM_HEADS + h))],
        out_specs=pl.BlockSpec((t, SLOT), lambda i, h: (i, h)),
        out_shape=jax.ShapeDtypeStruct((rows, MEM_HEADS * SLOT), BF16), compiler_params=_cparams(),
    )(proj, kvmem, kvmem)


def _cross_attn_bwd(proj, qoff, kvmem, do):
    rows = proj.shape[0]
    t = min(TQ_CROSS, rows)
    scale = HEAD_DIM ** -0.5

    def body(q_ref, k_ref, v_ref, do_ref, dq_ref, dk_ref, dv_ref):
        @pl.when(pl.program_id(1) == 0)
        def _():
            dk_ref[...] = jnp.zeros_like(dk_ref)
            dv_ref[...] = jnp.zeros_like(dv_ref)

        qv, kv, dov = q_ref[...].astype(BF16), k_ref[...], do_ref[...]
        s = _nt(qv, kv) * scale
        e = jnp.exp(s - jnp.max(s, axis=1, keepdims=True))
        p = e / jnp.sum(e, axis=1, keepdims=True)
        dp = _nt(dov, v_ref[...])
        ds = (p * (dp - jnp.sum(p * dp, axis=1, keepdims=True))).astype(BF16)
        dq_ref[...] = _nn(ds, kv) * scale
        dk_ref[...] += _tn(ds, qv) * scale
        dv_ref[...] += _tn(p.astype(BF16), dov)

    mem_out = pl.BlockSpec((N_MEM, SLOT), lambda h, i: (0, h))
    return pl.pallas_call(
        body, name="cross_attn_bwd", grid=(MEM_HEADS, rows // t),
        in_specs=[pl.BlockSpec((t, SLOT), lambda h, i: (i, qoff + h)),
                  pl.BlockSpec((N_MEM, SLOT), lambda h, i: (0, h)),
                  pl.BlockSpec((N_MEM, SLOT), lambda h, i: (0, MEM_HEADS + h)),
                  pl.BlockSpec((t, SLOT), lambda h, i: (i, h))],
        out_specs=[pl.BlockSpec((t, SLOT), lambda h, i: (i, h)), mem_out, mem_out],
        out_shape=[jax.ShapeDtypeStruct((rows, MEM_HEADS * SLOT), F32),
                   jax.ShapeDtypeStruct((N_MEM, MEM_HEADS * SLOT), F32),
                   jax.ShapeDtypeStruct((N_MEM, MEM_HEADS * SLOT), F32)],
        compiler_params=_cparams(),
    )(proj, kvmem, kvmem, do)


def _place():
    return lax.axis_index("x"), lax.axis_index("y"), lax.axis_index("c")


def _flip(v, bit):
    return 1 - v if bit else v


def _all_gather(blocks, name):
    nb = len(blocks)

    def body(*refs):
        x_refs, out_refs = refs[:nb], refs[nb:2 * nb]
        send_sems, recv_sems, local_sems = refs[2 * nb:]
        x, y, c = _place()
        me, sibling = (x, y, c), (x, y, 1 - c)
        chips = [(1 - x, y), (x, 1 - y), (1 - x, 1 - y)]

        def copy(b, k, blk, to, from_input=False):
            slot = out_refs[b].at[4 * blk[0] + 2 * blk[1] + blk[2]]
            return pltpu.make_async_remote_copy(
                src_ref=x_refs[b] if from_input else slot, dst_ref=slot,
                send_sem=send_sems.at[b, k], recv_sem=recv_sems.at[b, k],
                device_id=to, device_id_type=pl.DeviceIdType.MESH)

        mine = [pltpu.make_async_copy(x_refs[b], out_refs[b].at[4 * x + 2 * y + c], local_sems.at[b])
                for b in range(nb)]
        for cp in mine:
            cp.start()
        first = []
        for b in range(nb):
            first.append(copy(b, 0, me, sibling, from_input=True))
            first += [copy(b, 1 + n, me, (*chip, c), from_input=True) for n, chip in enumerate(chips)]
        for cp in first:
            cp.start()
        passed = []
        for n, chip in enumerate(chips):
            for b in range(nb):
                copy(b, 1 + n, (*chip, c), me).wait_recv()
                passed.append(copy(b, 4 + n, (*chip, c), sibling))
                passed[-1].start()
        for b in range(nb):
            copy(b, 0, sibling, me).wait_recv()
            for n, chip in enumerate(chips):
                copy(b, 4 + n, (*chip, 1 - c), me).wait_recv()
        for cp in first + passed:
            cp.wait_send()
        for cp in mine:
            cp.wait()

    any_spec = pl.BlockSpec(memory_space=pl.ANY)
    return pl.pallas_call(
        body, name=name, in_specs=[any_spec] * nb, out_specs=[any_spec] * nb,
        out_shape=[jax.ShapeDtypeStruct((N_DEV,) + blk.shape, blk.dtype) for blk in blocks],
        scratch_shapes=[pltpu.SemaphoreType.DMA((nb, 7)), pltpu.SemaphoreType.DMA((nb, 7)),
                        pltpu.SemaphoreType.DMA((nb,))],
    )(*blocks)


def _peers(x, y, c):
    out = []
    for n in range(1, N_DEV):
        peer = (_flip(x, n & 4), _flip(y, n & 2), _flip(c, n & 1))
        out.append((n - 1, peer, 4 * peer[0] + 2 * peer[1] + peer[2]))
    return out


_HBM = pl.BlockSpec(memory_space=pltpu.HBM)
_SEM = pl.BlockSpec(memory_space=pltpu.SEMAPHORE)


def _exchange_start(srcs, scatter, name, after=None):
    ns = len(srcs)
    lands = [lax.empty(s.shape if scatter else (N_DEV,) + s.shape, s.dtype) for s in srcs]

    def body(*refs):
        src_refs, land_refs = refs[:ns], refs[ns:2 * ns]
        pos = 2 * ns + (1 if after is not None else 0)
        send_sems, recv_sems, token = refs[pos], refs[pos + 1], refs[-1]
        x, y, c = _place()
        my_idx = 4 * x + 2 * y + c
        for col, peer, peer_idx in _peers(x, y, c):
            for b in range(ns):
                pltpu.make_async_remote_copy(
                    src_ref=src_refs[b].at[peer_idx] if scatter else src_refs[b], dst_ref=land_refs[b].at[my_idx],
                    send_sem=send_sems.at[b * (N_DEV - 1) + col], recv_sem=recv_sems.at[b * (N_DEV - 1) + col],
                    device_id=peer, device_id_type=pl.DeviceIdType.MESH).start()
        token[...] = jnp.zeros_like(token)

    args = [pltpu.with_memory_space_constraint(a, pltpu.HBM) for a in list(srcs) + lands]
    in_specs = [_HBM] * (2 * ns)
    if after is not None:
        args.append(after)
        in_specs.append(pl.BlockSpec(memory_space=pl.ANY))
    out = pl.pallas_call(
        body, name=name, in_specs=in_specs,
        out_specs=[_SEM, _SEM] + [_HBM] * (2 * ns) + [pl.BlockSpec(memory_space=pltpu.VMEM)],
        out_shape=[pltpu.SemaphoreType.DMA((ns * (N_DEV - 1),)), pltpu.SemaphoreType.DMA((ns * (N_DEV - 1),))]
        + [pltpu.HBM(a.shape, a.dtype) for a in list(srcs) + lands] + [jax.ShapeDtypeStruct((8, SLOT), F32)],
        input_output_aliases={k: 2 + k for k in range(2 * ns)},
        compiler_params=pltpu.CompilerParams(has_side_effects=pltpu.SideEffectType.DATAFLOW_SIDE_EFFECTING),
    )(*args)
    return (out[0], out[1], out[2:2 + ns], out[2 + ns:2 + 2 * ns], scatter), out[-1]


def _exchange_wait(handle, after, name):
    send_sems, recv_sems, srcs, lands, scatter = handle
    ns = len(srcs)

    def body(*refs):
        src_refs, land_refs = refs[:ns], refs[ns:2 * ns]
        send_ref, recv_ref = refs[2 * ns], refs[2 * ns + 1]
        x, y, c = _place()
        for col, peer, peer_idx in _peers(x, y, c):
            for b in range(ns):
                copy = pltpu.make_async_remote_copy(
                    src_ref=src_refs[b].at[peer_idx] if scatter else src_refs[b], dst_ref=land_refs[b].at[peer_idx],
                    send_sem=send_ref.at[b * (N_DEV - 1) + col], recv_sem=recv_ref.at[b * (N_DEV - 1) + col],
                    device_id=peer, device_id_type=pl.DeviceIdType.MESH)
                copy.wait_send()
                copy.wait_recv()

    out = pl.pallas_call(
        body, name=name, in_specs=[_HBM] * (2 * ns) + [_SEM, _SEM, pl.BlockSpec(memory_space=pl.ANY)],
        out_specs=[_HBM] * (2 * ns),
        out_shape=[pltpu.HBM(a.shape, a.dtype) for a in list(srcs) + list(lands)],
        input_output_aliases={k: k for k in range(2 * ns)},
        compiler_params=pltpu.CompilerParams(has_side_effects=pltpu.SideEffectType.DATAFLOW_SIDE_EFFECTING),
    )(*srcs, *lands, send_sems, recv_sems, after)
    my_idx = 4 * lax.axis_index("x") + 2 * lax.axis_index("y") + lax.axis_index("c")
    landed = []
    for src, land in zip(out[:ns], out[ns:]):
        own = lax.dynamic_index_in_dim(src, my_idx, 0, keepdims=True) if scatter else src[None]
        landed.append(lax.dynamic_update_index_in_dim(land, own, my_idx, 0))
    return landed


def _adamw(parts, w, m, v, name):
    lyr, rows, cols = w.shape
    assert len(parts) == lyr
    tr = ADAM_ROWS if cols > 512 else 2 * ADAM_ROWS
    while rows % tr:
        tr //= 2
    tr = min(tr, rows)

    def body(*refs):
        p_refs = refs[:lyr]
        w_ref, m_ref, v_ref, g_out, d_out, m_out, v_out = refs[lyr:]
        for k in range(lyr):
            @pl.when(pl.program_id(0) == k)
            def _(p_ref=p_refs[k]):
                g = p_ref[0].astype(F32)
                for s in range(1, N_DEV):
                    g = g + p_ref[s].astype(F32)
                m2 = ADAM_B1 * m_ref[...] + (1.0 - ADAM_B1) * g
                v2 = ADAM_B2 * v_ref[...] + (1.0 - ADAM_B2) * (g * g)
                m_hat = m2 / (1.0 - ADAM_B1 ** ADAM_STEP)
                v_hat = v2 / (1.0 - ADAM_B2 ** ADAM_STEP)
                g_out[...] = g
                d_out[...] = -ADAM_LR * (m_hat / (jnp.sqrt(v_hat) + ADAM_EPS) + ADAM_WD * w_ref[...])
                m_out[...] = m2
                v_out[...] = v2

    def part_spec(k):
        return pl.BlockSpec((N_DEV, tr, cols), lambda l, i: (0, jnp.where(l == k, i, 0), 0))

    spec = pl.BlockSpec((None, tr, cols), lambda l, i: (l, i, 0))
    shp = jax.ShapeDtypeStruct((lyr, rows, cols), F32)
    return pl.pallas_call(
        body, name=name, grid=(lyr, rows // tr),
        in_specs=[part_spec(k) for k in range(lyr)] + [spec, spec, spec],
        out_specs=[spec] * 4, out_shape=[shp] * 4, compiler_params=_cparams(),
    )(*parts, w, m, v)


def _pack(arrays, lanes, row_mult, dtype):
    flat = jnp.concatenate([a.reshape(-1).astype(dtype) for a in arrays])
    unit = lanes * row_mult
    total = -(-flat.shape[0] // unit) * unit
    return jnp.pad(flat, (0, total - flat.shape[0])).reshape(total // lanes, lanes)


def _unpack(packed, shapes):
    flat = packed.reshape(-1)
    out, off = [], 0
    for shp in shapes:
        n = 1
        for d in shp:
            n *= d
        out.append(flat[off:off + n].reshape(shp))
        off += n
    return out


def _pad_slots(w, axis):
    axis = axis % w.ndim
    n = w.shape[axis] // HEAD_DIM
    shp = w.shape[:axis] + (n, HEAD_DIM) + w.shape[axis + 1:]
    pad = [(0, 0)] * (w.ndim + 1)
    pad[axis + 1] = (0, SLOT - HEAD_DIM)
    return jnp.pad(w.reshape(shp), pad).reshape(w.shape[:axis] + (n * SLOT,) + w.shape[axis + 1:])


def _unpad_slots(w, axis, keep=HEAD_DIM):
    axis = axis % w.ndim
    n = w.shape[axis] // SLOT
    shp = w.shape[:axis] + (n, SLOT) + w.shape[axis + 1:]
    idx = [slice(None)] * (w.ndim + 1)
    idx[axis + 1] = slice(0, keep)
    return w.reshape(shp)[tuple(idx)].reshape(w.shape[:axis] + (n * keep,) + w.shape[axis + 1:])


def _mla_in_pad(w):
    z = functools.partial(jnp.zeros, dtype=w.dtype)
    rows = w.shape[0]
    return jnp.concatenate([w[:, :384], z((rows, 64)), w[:, 640:672], z((rows, 32)), w[:, 384:640],
                            _pad_slots(w[:, 672:], 1)], axis=1)


def _mla_in_unpad(d):
    return jnp.concatenate([d[:, :384], d[:, 512:768], d[:, 448:480], _unpad_slots(d[:, 768:], 1)], axis=1)


def _mla_uq_pad(w):
    return jnp.pad(w.reshape(w.shape[0], MLA_HEADS, MLA_QK), ((0, 0), (0, 0), (0, SLOT - MLA_QK))).reshape(
        w.shape[0], MLA_HEADS * SLOT)


def _mla_ukv_pad(w):
    w3 = w.reshape(w.shape[0], MLA_HEADS, 2 * HEAD_DIM)
    pad = ((0, 0), (0, 0), (0, SLOT - HEAD_DIM))
    k = jnp.pad(w3[:, :, :HEAD_DIM], pad).reshape(w.shape[0], -1)
    v = jnp.pad(w3[:, :, HEAD_DIM:], pad).reshape(w.shape[0], -1)
    return jnp.concatenate([k, v], axis=1)


def _mla_ukv_unpad(d):
    hw = MLA_HEADS * SLOT
    k = d[:, :hw].reshape(d.shape[0], MLA_HEADS, SLOT)[:, :, :HEAD_DIM]
    v = d[:, hw:].reshape(d.shape[0], MLA_HEADS, SLOT)[:, :, :HEAD_DIM]
    return jnp.concatenate([k, v], axis=2).reshape(d.shape[0], MLA_HEADS * 2 * HEAD_DIM)


def _join(gathered, axis):
    nd, a, b = gathered.shape
    if axis == 1:
        return gathered.reshape(nd * a, b)
    return gathered.transpose(1, 0, 2).reshape(a, nd * b)


def _split(full, axis):
    r, c = full.shape
    if axis == 1:
        return full.reshape(N_DEV, r // N_DEV, c).astype(BF16)
    return full.reshape(r, N_DEV, c // N_DEV).transpose(1, 0, 2).astype(BF16)


def kernel(x, mem, positions, attn_norm_g, mlp_norm_g, mem_norm_g, final_norm_g, mla_w_in, mla_q_norm_g, mla_kv_norm_g, mla_w_uq, mla_w_ukv, swa_w_in, swa_sinks, w_mem_kv, w_o, mlp_w_up, mlp_w_down, loss_target, m_attn_norm_g, m_mlp_norm_g, m_mem_norm_g, m_final_norm_g, m_mla_w_in, m_mla_q_norm_g, m_mla_kv_norm_g, m_mla_w_uq, m_mla_w_ukv, m_swa_w_in, m_swa_sinks, m_w_mem_kv, m_w_o, m_mlp_w_up, m_mlp_w_down, v_attn_norm_g, v_mlp_norm_g, v_mem_norm_g, v_final_norm_g, v_mla_w_in, v_mla_q_norm_g, v_mla_kv_norm_g, v_mla_w_uq, v_mla_w_ukv, v_swa_w_in, v_swa_sinks, v_w_mem_kv, v_w_o, v_mlp_w_up, v_mlp_w_down):
    given = dict(locals())
    seq = x.shape[1]
    x0 = x.reshape(seq, D_MODEL)
    tgt = loss_target.reshape(seq, D_MODEL)
    mem0 = mem.reshape(N_MEM, D_MODEL)
    pos = positions.reshape(seq).astype(F32)
    pos_col, pos_row = pos.reshape(seq, 1), pos.reshape(1, seq)

    def layer_names(i):
        mixer = ("mla_w_in", "mla_w_uq", "mla_w_ukv") if i % 2 == 0 else ("swa_w_in",)
        return [(n, i // 2) for n in mixer] + [(n, i) for n in ("w_mem_kv", "w_o", "mlp_w_up", "mlp_w_down")]

    def local_weights(names):
        return [given[n][l].astype(BF16) for n, l in names]

    first_attn, first_mlp = layer_names(0)[:-2], layer_names(0)[-2:]
    weights = [dict(zip([n for n, _ in first_attn], _all_gather(local_weights(first_attn), "gather_weights_first")))]
    coming_mlp, first_token = _exchange_start(local_weights(first_mlp), False, "gather_weights_start_0",
                                              after=weights[0]["w_o"])

    consts = _lane_consts()
    tabs = _rope_tables(pos_col, consts)
    slopes = 2.0 ** (-8.0 * (jnp.arange(SWA_HEADS, dtype=F32) + 1.0) / SWA_HEADS)

    mem_n = _rmsnorm_fwd(mem0, 0, D_MODEL, mem_norm_g, "rmsnorm_fwd_mem")

    saved = []
    xc = x0
    for i in range(DEPTH):
        j = i // 2
        wts = weights[i]
        s = {"x_in": xc}
        token = None
        if i + 1 < DEPTH:
            coming, token = _exchange_start(local_weights(layer_names(i + 1)), False,
                                            "gather_weights_start_%d" % (i + 1),
                                            after=first_token if i == 0 else wts["w_o"])
        hn = _rmsnorm_fwd(xc, 0, D_MODEL, attn_norm_g[i], "rmsnorm_fwd", after=token)
        if i % 2 == 0:
            w_in = _mla_in_pad(_join(wts["mla_w_in"], 1))
            w_uq = _mla_uq_pad(_join(wts["mla_w_uq"], 2))
            w_kv = _mla_ukv_pad(_join(wts["mla_w_ukv"], 2))
            proj = _mm(hn, w_in, "nn", F32, "mm_mla_in")
            cqn = _rmsnorm_fwd(proj, 0, MLA_Q_RANK, mla_q_norm_g[j], "rmsnorm_fwd_q")
            ckvn = _rmsnorm_fwd(proj, 2, MLA_KV_RANK, mla_kv_norm_g[j], "rmsnorm_fwd_kv")
            qraw = _mm(cqn, w_uq, "nn", F32, "mm_mla_uq")
            kvraw = _mm(ckvn, w_kv, "nn", F32, "mm_mla_ukv")
            q, k, v = _mla_rope_fwd(qraw, kvraw, proj, tabs)
            o, lse = _mla_attn_fwd(q, k, v)
            qoff = MLA_QOFF
            s.update(w_uq=w_uq, w_kv=w_kv, cqn=cqn, ckvn=ckvn, q=q, k=k, v=v)
        else:
            w_in = _pad_slots(_join(wts["swa_w_in"], 2), 1)
            proj = _mm(hn, w_in, "nn", BF16, "mm_swa_in")
            o, lse = _swa_attn_fwd(proj, pos_col, pos_row, slopes, swa_sinks[j])
            qoff = SWA_QOFF
        w_mem = _pad_slots(_join(wts["w_mem_kv"], 1), 1)
        w_out = _pad_slots(_join(wts["w_o"], 1), 0)
        w_o_mix, w_o_cross = w_out[:SWA_HEADS * SLOT], w_out[SWA_HEADS * SLOT:]
        kvmem = _mm(mem_n, w_mem, "nn", BF16, "mm_mem_kv")
        cross = _cross_attn_fwd(proj, qoff, kvmem)
        x1 = _mm(o, w_o_mix, "nn", F32, "mm_o_mix", res=xc)
        x1 = _mm(cross, w_o_cross, "nn", F32, "mm_o_cross", res=x1)
        hn2 = _rmsnorm_fwd(x1, 0, D_MODEL, mlp_norm_g[i], "rmsnorm_fwd")
        if i == 0:
            wts.update(zip([n for n, _ in first_mlp], _exchange_wait(coming_mlp, hn2, "gather_weights_wait_0")))
        act, act2 = _mm(hn2, wts["mlp_w_up"], "nn", BF16, "mm_mlp_up", epi="relu2", b_blk="cols")
        xc = _mm(act2, wts["mlp_w_down"], "nn", F32, "mm_mlp_down", res=x1, b_blk="rows")
        s.update(hn=hn, w_in=w_in, proj=proj, o=o, lse=lse, qoff=qoff, w_mem=w_mem, w_o_mix=w_o_mix,
                 w_o_cross=w_o_cross, kvmem=kvmem, cross=cross, x1=x1, hn2=hn2, act=act, act2=act2)
        saved.append(s)
        if i + 1 < DEPTH:
            got = _exchange_wait(coming, xc, "gather_weights_wait_%d" % (i + 1))
            weights.append(dict(zip([n for n, _ in layer_names(i + 1)], got)))

    dx, dx_b, dg_final, loss_part = _loss_head(xc, final_norm_g, tgt)
    loss = lax.psum(loss_part[0, 0], MESH_AXES)

    gains = {n: [None] * DEPTH for n in ("attn_norm_g", "mlp_norm_g")}
    for n in ("mla_q_norm_g", "mla_kv_norm_g", "swa_sinks"):
        gains[n] = [None] * 2
    leaving = {}
    token = None
    dmem_n = None
    for i in reversed(range(DEPTH)):
        j = i // 2
        s = saved[i]
        wts = weights[i]
        out = {}
        du = _mm(dx_b, wts["mlp_w_down"], "nt", BF16, "mm_mlp_down_dx", aux=s["act"], epi="mul2aux", b_blk="rows",
                 after=token)
        out["mlp_w_down"] = _mm(s["act2"], dx_b, "tn", BF16, "mm_mlp_down_dw", o_blk="rows")
        out["mlp_w_up"] = _mm(s["hn2"], du, "tn", BF16, "mm_mlp_up_dw", o_blk="cols")
        dx1, dx1_b, dg = _mm(du, wts["mlp_w_up"], "nt", F32, "mm_mlp_up_dx", b_blk="cols",
                             epi="normbwd", norm=(s["x1"], mlp_norm_g[i], dx))
        gains["mlp_norm_g"][i] = dg[0]

        do = _mm(dx1_b, s["w_o_mix"], "nt", BF16, "mm_o_mix_dx")
        dcross = _mm(dx1_b, s["w_o_cross"], "nt", BF16, "mm_o_cross_dx")
        dw_o = jnp.concatenate([_mm(s["o"], dx1_b, "tn", F32, "mm_o_mix_dw"),
                                _mm(s["cross"], dx1_b, "tn", F32, "mm_o_cross_dw")], axis=0)
        out["w_o"] = _split(_unpad_slots(dw_o, 0), 1)
        dqc, dkm, dvm = _cross_attn_bwd(s["proj"], s["qoff"], s["kvmem"], dcross)
        dkvmem = jnp.concatenate([dkm, dvm], axis=1).astype(BF16)
        out["w_mem_kv"] = _split(_unpad_slots(_mm(mem_n, dkvmem, "tn", F32, "mm_mem_kv_dw"), 1), 1)
        dmem_n = _mm(dkvmem, s["w_mem"], "nt", F32, "mm_mem_kv_dx" if dmem_n is None else "mm_mem_kv_dx_acc",
                     res=dmem_n)
        leaving[(i, "main")], token = _exchange_start([out[n] for n, _ in layer_names(i)[-4:]], True,
                                                      "exchange_grads_main_start_%d" % i)

        if i % 2 == 0:
            dq, dk, dv = _mla_attn_bwd(s["q"], s["k"], s["v"], s["o"], do, s["lse"], token)
            dqraw, dkv, dkr = _mla_rope_bwd(dq, dk, dv, tabs, consts)
            dcqn = _mm(dqraw, s["w_uq"], "nt", F32, "mm_mla_uq_dx")
            out["mla_w_uq"] = _split(_unpad_slots(_mm(s["cqn"], dqraw, "tn", F32, "mm_mla_uq_dw"), 1, MLA_QK), 2)
            dckvn = _mm(dkv, s["w_kv"], "nt", F32, "mm_mla_ukv_dx")
            out["mla_w_ukv"] = _split(_mla_ukv_unpad(_mm(s["ckvn"], dkv, "tn", F32, "mm_mla_ukv_dw")), 2)
            dcq, dg = _rmsnorm_bwd(s["proj"], 0, MLA_Q_RANK, mla_q_norm_g[j], dcqn, None, BF16, "rmsnorm_bwd_q")
            gains["mla_q_norm_g"][j] = dg[0]
            dckv, dg = _rmsnorm_bwd(s["proj"], 2, MLA_KV_RANK, mla_kv_norm_g[j], dckvn, None, BF16, "rmsnorm_bwd_kv")
            gains["mla_kv_norm_g"][j] = dg[0]
            dproj = jnp.concatenate([dcq, dkr.astype(BF16), dckv, dqc.astype(BF16)], axis=1)
            in_dx = "mm_mla_in_dx"
            out["mla_w_in"] = _split(_mla_in_unpad(_mm(s["hn"], dproj, "tn", F32, "mm_mla_in_dw")), 1)
        else:
            dq, dk, dv, dsink = _swa_attn_bwd(s["proj"], s["o"], do, s["lse"], pos_col, pos_row, slopes, swa_sinks[j],
                                              token)
            gains["swa_sinks"][j] = dsink[::8, 0]
            dproj = jnp.concatenate([dq, dk, dv, dqc], axis=1).astype(BF16)
            in_dx = "mm_swa_in_dx"
            out["swa_w_in"] = _split(_unpad_slots(_mm(s["hn"], dproj, "tn", F32, "mm_swa_in_dw"), 1), 2)
        dx, dx_b, dg = _mm(dproj, s["w_in"], "nt", F32, in_dx, epi="normbwd", norm=(s["x_in"], attn_norm_g[i], dx1))
        gains["attn_norm_g"][i] = dg[0]

        leaving[(i, "mixer")], token = _exchange_start([out[n] for n, _ in layer_names(i)[:-4]], True,
                                                       "exchange_grads_mixer_start_%d" % i)

    _, dg_mem = _rmsnorm_bwd(mem0, 0, D_MODEL, mem_norm_g, dmem_n, None, BF16, "rmsnorm_bwd_mem")
    gains = {n: jnp.stack(g) for n, g in gains.items()}
    gains["mem_norm_g"] = dg_mem[0]
    gains["final_norm_g"] = dg_final[0]

    result = {}

    def adamw_of(names, received):
        for n in names:
            parts = [received[(n, l)] for l in range(given[n].shape[0])]
            for kind, r in enumerate(_adamw(parts, given[n], given["m_" + n], given["v_" + n], "adamw_" + n)):
                result[(kind, n)] = r

    received = {}
    for i in reversed(range(DEPTH)):
        got = _exchange_wait(leaving[(i, "main")], dx, "exchange_grads_main_wait_%d" % i)
        received.update(zip(layer_names(i)[-4:], got))
    adamw_of(("mlp_w_up", "mlp_w_down", "w_o", "w_mem_kv"), received)
    for i in reversed(range(DEPTH)):
        got = _exchange_wait(leaving[(i, "mixer")], result[(0, "w_mem_kv")], "exchange_grads_mixer_wait_%d" % i)
        received.update(zip(layer_names(i)[:-4], got))
    adamw_of(("mla_w_in", "mla_w_uq", "mla_w_ukv", "swa_w_in"), received)

    rep_shapes = [given[n].shape for n in REPLICATED]
    rep_parts = _all_gather([_pack([gains[n] for n in REPLICATED], SLOT, 8, F32)], "gather_gain_grads")[0]
    rep_packed = [_pack([given[p + n] for n in REPLICATED], SLOT, 8, F32)[None] for p in ("", "m_", "v_")]
    for kind, r in enumerate(_adamw([rep_parts], *rep_packed, "adamw_gains")):
        for n, part in zip(REPLICATED, _unpack(r[0], rep_shapes)):
            result[(kind, n)] = part

    outs = [loss, dx.reshape(1, seq, D_MODEL)]
    for kind in range(4):
        outs += [result[(kind, n)] for n in WEIGHT_ORDER]
    return tuple(outs)
```

```python
import functools

import jax
import jax.numpy as jnp
from jax import lax
from jax.experimental import pallas as pl
from jax.experimental.pallas import tpu as pltpu

F32 = jnp.float32
BF16 = jnp.bfloat16

D_MODEL = 1024
D_FF = 4096
N_MEM = 256
DEPTH = 4
SLOT = 128
HEAD_DIM = 64
MLA_HEADS = 12
MLA_QK = 96
MLA_Q_RANK = 384
MLA_KV_RANK = 256
SWA_HEADS = 12
SWA_KV_HEADS = 4
SWA_GROUP = 3
MEM_HEADS = 4
WINDOW = 128
EPS = 1e-6
NEG = -1e30
ROPE_THETA = 10000.0
N_DEV = 8

ADAM_LR = 0.001
ADAM_B1 = 0.9
ADAM_B2 = 0.999
ADAM_EPS = 1e-08
ADAM_WD = 0.01
ADAM_STEP = 10

TM = 512
TQ_MLA = 1024
MLA_PACK = 2
SWA_PACK = 4
TQ_CROSS = 2048
MM_VMEM_BUDGET = 38 * 1024 * 1024
ADAM_ROWS = 128
VMEM_LIMIT = 56 * 1024 * 1024

MESH_AXES = ("x", "y", "c")

LOG2_E = 1.4426950408889634
MLA_SCALE = MLA_QK ** -0.5
MLA_Q_SCALE = MLA_SCALE * LOG2_E

MLA_PAD_IN = 384 + SLOT + 256 + MEM_HEADS * SLOT
MLA_QOFF = (384 + SLOT + 256) // SLOT
SWA_PAD_IN = (SWA_HEADS + 2 * SWA_KV_HEADS + MEM_HEADS) * SLOT
SWA_QOFF = SWA_HEADS + 2 * SWA_KV_HEADS

SHARDED = (
    ("mla_w_in", 1), ("mla_w_uq", 2), ("mla_w_ukv", 2), ("swa_w_in", 2),
    ("w_mem_kv", 1), ("w_o", 1), ("mlp_w_up", 2), ("mlp_w_down", 1),
)
REPLICATED = ("attn_norm_g", "mlp_norm_g", "mem_norm_g", "final_norm_g",
              "mla_q_norm_g", "mla_kv_norm_g", "swa_sinks")
WEIGHT_ORDER = ("attn_norm_g", "mlp_norm_g", "mem_norm_g", "final_norm_g", "mla_w_in",
                "mla_q_norm_g", "mla_kv_norm_g", "mla_w_uq", "mla_w_ukv", "swa_w_in",
                "swa_sinks", "w_mem_kv", "w_o", "mlp_w_up", "mlp_w_down")


def _cparams():
    return pltpu.CompilerParams(vmem_limit_bytes=VMEM_LIMIT)


_DIMS = {"nn": (((1,), (0,)), ((), ())), "nt": (((1,), (1,)), ((), ())), "tn": (((0,), (0,)), ((), ()))}


def _mm_tiles(m, n, k, a_bytes, b_bytes, o_bytes, extra_bytes, tm_fixed, tn_fixed):
    best = None
    for tm in ([tm_fixed] if tm_fixed else [t for t in (4096, 2048, 1024, 512, 256, 128) if m % t == 0] or [m]):
        for tn in ([tn_fixed] if tn_fixed else [t for t in range(1024, 0, -SLOT) if n % t == 0] or [n]):
            need = (1 if tm == m else 2) * tm * k * a_bytes + (1 if tn == n else 2) * k * tn * b_bytes
            need += 2 * tm * tn * (o_bytes + extra_bytes) + tm * tn * 4
            if need <= MM_VMEM_BUDGET and (best is None or tm * tn > best[0] * best[1]):
                best = (tm, tn)
    assert best is not None, (m, n, k)
    return best


def _mm(a, b, mode, out_dtype, name, res=None, aux=None, epi=None, b_blk=None, o_blk=None, after=None, norm=None):
    if b_blk is not None:
        nb, br, bc = b.shape
        b_shape = (nb * br, bc) if b_blk == "rows" else (br, nb * bc)
    else:
        b_shape = b.shape
    if mode == "nn":
        (m, k), (k2, n) = a.shape, b_shape
    elif mode == "nt":
        (m, k), (n, k2) = a.shape, b_shape
    else:
        (k, m), (k2, n) = a.shape, b_shape
    assert k == k2, (a.shape, b_shape, mode)
    k_blocked = b_blk is not None and (b_blk == "rows") == (mode != "nt")
    tn_fixed = None
    if b_blk is not None and not k_blocked:
        tn_fixed = br if b_blk == "rows" else bc
    if o_blk == "cols":
        tn_fixed = n // N_DEV
    tm_fixed = m // N_DEV if o_blk == "rows" else None
    has_res, has_aux, has_norm = res is not None, aux is not None, epi == "normbwd"
    assert o_blk is None or not (has_res or has_aux or has_norm)
    n_out = 2 if epi == "relu2" else 1
    if has_norm:
        tn_fixed = n
        o_bytes, extra_bytes = 4 + 2, 4 + 4
    else:
        o_bytes = n_out * jnp.dtype(out_dtype).itemsize
        extra_bytes = (4 if has_res else 0) + (aux.dtype.itemsize if has_aux else 0)
    tm, tn = _mm_tiles(m, n, k, a.dtype.itemsize, b.dtype.itemsize, o_bytes, extra_bytes, tm_fixed, tn_fixed)
    dims = _DIMS[mode]
    a_mode = pl.Buffered(1) if tm == m else None
    b_mode = pl.Buffered(1) if tn == n else None
    if mode == "tn":
        a_spec = pl.BlockSpec((k, tm), lambda i, j: (0, i), pipeline_mode=a_mode)
    else:
        a_spec = pl.BlockSpec((tm, k), lambda i, j: (i, 0), pipeline_mode=a_mode)
    if b_blk is None:
        if mode == "nt":
            b_spec = pl.BlockSpec((tn, k), lambda i, j: (j, 0), pipeline_mode=b_mode)
        else:
            b_spec = pl.BlockSpec((k, tn), lambda i, j: (0, j), pipeline_mode=b_mode)
    elif k_blocked and mode == "nt":
        b_spec = pl.BlockSpec((N_DEV, tn, bc), lambda i, j: (0, j, 0), pipeline_mode=b_mode)
    elif k_blocked:
        b_spec = pl.BlockSpec((N_DEV, br, tn), lambda i, j: (0, 0, j), pipeline_mode=b_mode)
    elif mode == "nt":
        b_spec = pl.BlockSpec((None, tn, k), lambda i, j: (j, 0, 0))
    else:
        b_spec = pl.BlockSpec((None, k, tn), lambda i, j: (j, 0, 0))
    if o_blk is None:
        o_spec = pl.BlockSpec((tm, tn), lambda i, j: (i, j))
        o_shape = (m, n)
    elif o_blk == "rows":
        o_spec = pl.BlockSpec((None, tm, tn), lambda i, j: (i, 0, j))
        o_shape = (N_DEV, tm, n)
    else:
        o_spec = pl.BlockSpec((None, tm, tn), lambda i, j: (j, i, 0))
        o_shape = (N_DEV, m, tn)

    def body(*refs):
        a_ref, b_ref = refs[0], refs[1]
        pos = 2
        res_ref = aux_ref = None
        if has_res:
            res_ref = refs[pos]
            pos += 1
        if has_aux:
            aux_ref = refs[pos]
            pos += 1
        if has_norm:
            x_ref, g_ref, dres_ref = refs[pos:pos + 3]
            pos += 3
        if after is not None:
            pos += 1
        outs = refs[pos:]
        if k_blocked and mode == "nt":
            r = None
            for d in range(N_DEV):
                part = lax.dot_general(a_ref[:, d * bc:(d + 1) * bc].astype(BF16), b_ref[d].astype(BF16), dims,
                                       preferred_element_type=F32)
                r = part if r is None else r + part
        else:
            bv = b_ref[...].reshape(k, tn) if k_blocked else b_ref[...]
            r = lax.dot_general(a_ref[...].astype(BF16), bv.astype(BF16), dims, preferred_element_type=F32)
        if epi == "relu2":
            r = jnp.maximum(r, 0.0)
            outs[0][...] = r.astype(outs[0].dtype)
            outs[1][...] = (r * r).astype(outs[1].dtype)
        elif has_norm:
            xv = x_ref[...]
            rs = lax.rsqrt(jnp.mean(xv * xv, axis=1, keepdims=True) + EPS)
            xh = xv * rs
            dxh = r * g_ref[...]
            dx = rs * (dxh - xh * jnp.mean(dxh * xh, axis=1, keepdims=True)) + dres_ref[...]
            outs[0][...] = dx
            outs[1][...] = dx.astype(BF16)

            @pl.when(pl.program_id(0) == 0)
            def _():
                outs[2][...] = jnp.zeros_like(outs[2])

            outs[2][...] += jnp.sum(r * xh, axis=0, keepdims=True)
        else:
            if epi == "mul2aux":
                r = r * (2.0 * aux_ref[...].astype(F32))
            if has_res:
                r = r + res_ref[...]
            outs[0][...] = r.astype(outs[0].dtype)

    in_specs = [a_spec, b_spec]
    args = [a, b]
    if has_res:
        in_specs.append(o_spec)
        args.append(res)
    if has_aux:
        in_specs.append(o_spec)
        args.append(aux)
    vec_spec = pl.BlockSpec((1, n), lambda i, j: (0, 0))
    if has_norm:
        in_specs += [o_spec, vec_spec, o_spec]
        args += [norm[0], norm[1].reshape(1, n), norm[2]]
    if after is not None:
        in_specs.append(pl.BlockSpec(memory_space=pl.ANY))
        args.append(after)
    if has_norm:
        out_specs = [o_spec, o_spec, vec_spec]
        out_shape = [jax.ShapeDtypeStruct(o_shape, F32), jax.ShapeDtypeStruct(o_shape, BF16),
                     jax.ShapeDtypeStruct((1, n), F32)]
    else:
        out_specs = [o_spec] * n_out
        out_shape = [jax.ShapeDtypeStruct(o_shape, out_dtype)] * n_out
    out = pl.pallas_call(
        body, name=name, grid=(m // tm, n // tn),
        in_specs=in_specs, out_specs=out_specs, out_shape=out_shape, compiler_params=_cparams(),
    )(*args)
    return out if len(out) > 1 else out[0]


def _rmsnorm_fwd(xarr, colblk, width, g, name, after=None):
    rows = xarr.shape[0]
    tm = min(TM, rows)

    def body(x_ref, g_ref, *rest):
        y_ref = rest[-1]
        x = x_ref[...].astype(F32)
        r = lax.rsqrt(jnp.mean(x * x, axis=1, keepdims=True) + EPS)
        y_ref[...] = (x * r * g_ref[...]).astype(y_ref.dtype)

    in_specs = [pl.BlockSpec((tm, width), lambda i: (i, colblk)), pl.BlockSpec((1, width), lambda i: (0, 0))]
    args = [xarr, g.reshape(1, width)]
    if after is not None:
        in_specs.append(pl.BlockSpec(memory_space=pl.ANY))
        args.append(after)
    return pl.pallas_call(
        body, name=name, grid=(rows // tm,), in_specs=in_specs,
        out_specs=pl.BlockSpec((tm, width), lambda i: (i, 0)),
        out_shape=jax.ShapeDtypeStruct((rows, width), BF16), compiler_params=_cparams(),
    )(*args)


def _rmsnorm_bwd(xarr, colblk, width, g, dy, dres, out_dtype, name):
    rows = xarr.shape[0]
    tm = min(TM, rows)
    has_res = dres is not None

    def body(*refs):
        x_ref, g_ref, dy_ref = refs[0], refs[1], refs[2]
        dres_ref = refs[3] if has_res else None
        dx_ref, dg_ref = refs[-2], refs[-1]
        x = x_ref[...].astype(F32)
        dyv = dy_ref[...].astype(F32)
        r = lax.rsqrt(jnp.mean(x * x, axis=1, keepdims=True) + EPS)
        xh = x * r
        dxh = dyv * g_ref[...]
        dx = r * (dxh - xh * jnp.mean(dxh * xh, axis=1, keepdims=True))
        if has_res:
            dx = dx + dres_ref[...]
        dx_ref[...] = dx.astype(dx_ref.dtype)

        @pl.when(pl.program_id(0) == 0)
        def _():
            dg_ref[...] = jnp.zeros_like(dg_ref)

        dg_ref[...] += jnp.sum(dyv * xh, axis=0, keepdims=True)

    row_spec = pl.BlockSpec((tm, width), lambda i: (i, 0))
    vec_spec = pl.BlockSpec((1, width), lambda i: (0, 0))
    in_specs = [pl.BlockSpec((tm, width), lambda i: (i, colblk)), vec_spec, row_spec]
    args = [xarr, g.reshape(1, width), dy]
    if has_res:
        in_specs.append(row_spec)
        args.append(dres)
    return pl.pallas_call(
        body, name=name, grid=(rows // tm,), in_specs=in_specs, out_specs=[row_spec, vec_spec],
        out_shape=[jax.ShapeDtypeStruct((rows, width), out_dtype), jax.ShapeDtypeStruct((1, width), F32)],
        compiler_params=_cparams(),
    )(*args)


def _loss_head(x, g, tgt):
    rows, width = x.shape
    tm = min(TM, rows)

    def body(x_ref, g_ref, t_ref, dx_ref, dxb_ref, dg_ref, loss_ref):
        xv = x_ref[...]
        gv = g_ref[...]
        r = lax.rsqrt(jnp.mean(xv * xv, axis=1, keepdims=True) + EPS)
        xh = xv * r
        err = xh * gv - t_ref[...]
        part = 0.5 * jnp.sum(jnp.mean(err * err, axis=1, keepdims=True), axis=0, keepdims=True)
        dyv = err * (1.0 / width)
        dxh = dyv * gv
        dxv = r * (dxh - xh * jnp.mean(dxh * xh, axis=1, keepdims=True))
        dx_ref[...] = dxv
        dxb_ref[...] = dxv.astype(BF16)

        @pl.when(pl.program_id(0) == 0)
        def _():
            dg_ref[...] = jnp.zeros_like(dg_ref)
            loss_ref[...] = jnp.zeros_like(loss_ref)

        dg_ref[...] += jnp.sum(dyv * xh, axis=0, keepdims=True)
        loss_ref[...] += jnp.broadcast_to(part, loss_ref.shape)

    row_spec = pl.BlockSpec((tm, width), lambda i: (i, 0))
    vec_spec = pl.BlockSpec((1, width), lambda i: (0, 0))
    return pl.pallas_call(
        body, name="loss_head", grid=(rows // tm,), in_specs=[row_spec, vec_spec, row_spec],
        out_specs=[row_spec, row_spec, vec_spec, pl.BlockSpec((1, SLOT), lambda i: (0, 0))],
        out_shape=[jax.ShapeDtypeStruct((rows, width), F32), jax.ShapeDtypeStruct((rows, width), BF16),
                   jax.ShapeDtypeStruct((1, width), F32), jax.ShapeDtypeStruct((1, SLOT), F32)],
        compiler_params=_cparams(),
    )(x, g.reshape(1, width), tgt)


def _lane_consts():
    half = 16
    inv = ROPE_THETA ** (-(jnp.arange(half, dtype=F32) * 2.0) / 32)
    lane = jnp.arange(SLOT)
    first = (lane >= 64) & (lane < 80)
    second = (lane >= 80) & (lane < 96)
    inv_lane = jnp.where(first | second, inv[(lane - 64) % half], 0.0)
    rows = [inv_lane, (lane < 64).astype(F32), first.astype(F32), second.astype(F32)]
    rows += [jnp.zeros((SLOT,), F32)] * 4
    return jnp.stack(rows).astype(F32)


def _rope_tables(pos_col, consts):
    rows = pos_col.shape[0]
    tm = min(TM, rows)

    def body(p_ref, k_ref, c_ref, s1_ref, s2_ref):
        ang = p_ref[...] * k_ref[0:1, :]
        cos, sin = jnp.cos(ang), jnp.sin(ang)
        first, second = k_ref[2:3, :], k_ref[3:4, :]
        c_ref[...] = k_ref[1:2, :] + (first + second) * cos
        s1_ref[...] = -first * sin
        s2_ref[...] = second * sin

    spec = pl.BlockSpec((tm, SLOT), lambda i: (i, 0))
    shp = jax.ShapeDtypeStruct((rows, SLOT), F32)
    return pl.pallas_call(
        body, name="rope_tables", grid=(rows // tm,),
        in_specs=[pl.BlockSpec((tm, 1), lambda i: (i, 0)), pl.BlockSpec((8, SLOT), lambda i: (0, 0))],
        out_specs=[spec, spec, spec], out_shape=[shp, shp, shp], compiler_params=_cparams(),
    )(pos_col, consts)


def _rot(xv, c, s1, s2):
    return xv * c + pltpu.roll(xv, SLOT - 16, 1) * s1 + pltpu.roll(xv, 16, 1) * s2


def _rot_t(dy, c, s1, s2):
    return dy * c + pltpu.roll(dy * s1, 16, 1) + pltpu.roll(dy * s2, SLOT - 16, 1)


def _mla_rope_fwd(qraw, kvraw, proj, tabs):
    rows = qraw.shape[0]
    tm = min(256, rows)
    hw = MLA_HEADS * SLOT

    def body(q_ref, kv_ref, kr_ref, c_ref, s1_ref, s2_ref, qo, ko, vo):
        c, s1, s2 = c_ref[...], s1_ref[...], s2_ref[...]
        kr = _rot(kr_ref[...], c, s1, s2)
        for h in range(MLA_HEADS):
            sl = slice(h * SLOT, (h + 1) * SLOT)
            qo[:, sl] = (_rot(q_ref[:, sl], c, s1, s2) * MLA_Q_SCALE).astype(BF16)
            ko[:, sl] = (kv_ref[:, sl] + kr).astype(BF16)
            vo[:, sl] = kv_ref[:, hw + h * SLOT:hw + (h + 1) * SLOT].astype(BF16)

    tab = pl.BlockSpec((tm, SLOT), lambda i: (i, 0))
    wide = pl.BlockSpec((tm, hw), lambda i: (i, 0))
    shp = jax.ShapeDtypeStruct((rows, hw), BF16)
    return pl.pallas_call(
        body, name="mla_rope_fwd", grid=(rows // tm,),
        in_specs=[wide, pl.BlockSpec((tm, 2 * hw), lambda i: (i, 0)), pl.BlockSpec((tm, SLOT), lambda i: (i, 3)),
                  tab, tab, tab],
        out_specs=[wide, wide, wide], out_shape=[shp, shp, shp], compiler_params=_cparams(),
    )(qraw, kvraw, proj, *tabs)


def _mla_rope_bwd(dq, dk, dv, tabs, consts):
    rows = dq.shape[0]
    tm = min(256, rows)
    hw = MLA_HEADS * SLOT

    def body(dq_ref, dk_ref, dv_ref, c_ref, s1_ref, s2_ref, k_ref, dqo, dkvo, dkro):
        c, s1, s2 = c_ref[...], s1_ref[...], s2_ref[...]
        ksum = jnp.zeros((tm, SLOT), F32)
        for h in range(MLA_HEADS):
            sl = slice(h * SLOT, (h + 1) * SLOT)
            dqo[:, sl] = _rot_t(dq_ref[:, sl], c, s1, s2).astype(BF16)
            dkh = dk_ref[:, sl]
            ksum = ksum + dkh
            dkvo[:, sl] = dkh.astype(BF16)
            dkvo[:, hw + h * SLOT:hw + (h + 1) * SLOT] = dv_ref[:, sl].astype(BF16)
        dkro[...] = _rot_t(ksum, c, s1, s2) * (k_ref[2:3, :] + k_ref[3:4, :])

    tab = pl.BlockSpec((tm, SLOT), lambda i: (i, 0))
    wide = pl.BlockSpec((tm, hw), lambda i: (i, 0))
    return pl.pallas_call(
        body, name="mla_rope_bwd", grid=(rows // tm,),
        in_specs=[wide, wide, wide, tab, tab, tab, pl.BlockSpec((8, SLOT), lambda i: (0, 0))],
        out_specs=[wide, pl.BlockSpec((tm, 2 * hw), lambda i: (i, 0)), tab],
        out_shape=[jax.ShapeDtypeStruct((rows, hw), BF16), jax.ShapeDtypeStruct((rows, 2 * hw), BF16),
                   jax.ShapeDtypeStruct((rows, SLOT), F32)],
        compiler_params=_cparams(),
    )(dq, dk, dv, *tabs, consts)


def _nt(a, b):
    return lax.dot_general(a, b, _DIMS["nt"], preferred_element_type=F32)


def _tn(a, b):
    return lax.dot_general(a, b, _DIMS["tn"], preferred_element_type=F32)


def _nn(a, b):
    return lax.dot_general(a, b, _DIMS["nn"], preferred_element_type=F32)


def _causal(rows, keys):
    shp = (rows.stop - rows.start, keys.stop - keys.start)
    return (keys.start + lax.broadcasted_iota(jnp.int32, shp, 1)
            <= rows.start + lax.broadcasted_iota(jnp.int32, shp, 0))


def _mla_attn_fwd(q, k, v):
    rows = q.shape[0]
    t = min(TQ_MLA, rows)
    nt = rows // t
    wide = MLA_PACK * SLOT

    def body(q_ref, k_ref, v_ref, o_ref, lse_ref, m_sc, l_sc, acc_sc):
        i, j = pl.program_id(1), pl.program_id(2)

        @pl.when(j == 0)
        def _():
            m_sc[...] = jnp.full_like(m_sc, NEG)
            l_sc[...] = jnp.zeros_like(l_sc)
            acc_sc[...] = jnp.zeros_like(acc_sc)

        def step(diagonal):
            for hh in range(MLA_PACK):
                sl = slice(hh * SLOT, (hh + 1) * SLOT)
                s = _nt(q_ref[:, sl], k_ref[:, sl])
                if diagonal:
                    s = jnp.where(_causal(slice(0, t), slice(0, t)), s, NEG)
                m_prev = m_sc[hh]
                m_new = jnp.maximum(m_prev, jnp.max(s, axis=1, keepdims=True))
                p = jnp.exp2(s - m_new)
                alpha = jnp.exp2(m_prev - m_new)
                l_new = alpha * l_sc[hh] + jnp.sum(p, axis=1, keepdims=True)
                acc = alpha * acc_sc[:, sl] + _nn(p.astype(BF16), v_ref[:, sl])
                if diagonal:
                    o_ref[:, sl] = (acc / l_new).astype(o_ref.dtype)
                    lse_ref[:, sl] = jnp.broadcast_to(m_new + jnp.log(l_new) * LOG2_E, (t, SLOT))
                else:
                    m_sc[hh] = m_new
                    l_sc[hh] = l_new
                    acc_sc[:, sl] = acc

        @pl.when(j < i)
        def _():
            step(False)

        @pl.when(j == i)
        def _():
            step(True)

    q_spec = pl.BlockSpec((t, wide), lambda h, i, j: (i, h))
    kv_spec = pl.BlockSpec((t, wide), lambda h, i, j: (jnp.minimum(j, i), h))
    return pl.pallas_call(
        body, name="mla_attn_fwd", grid=(MLA_HEADS // MLA_PACK, nt, nt),
        in_specs=[q_spec, kv_spec, kv_spec], out_specs=[q_spec, q_spec],
        out_shape=[jax.ShapeDtypeStruct(q.shape, BF16), jax.ShapeDtypeStruct(q.shape, F32)],
        scratch_shapes=[pltpu.VMEM((MLA_PACK, t, 1), F32), pltpu.VMEM((MLA_PACK, t, 1), F32),
                        pltpu.VMEM((t, wide), F32)],
        compiler_params=_cparams(),
    )(q, k, v)


def _mla_attn_bwd(q, k, v, o, do, lse, after):
    rows = q.shape[0]
    t = min(TQ_MLA, rows)
    nt = rows // t
    wide = MLA_PACK * SLOT

    def body(q_ref, k_ref, v_ref, o_ref, do_ref, lse_ref, after_ref, dq_ref, dk_ref, dv_ref, dk_sc, dv_sc):
        j, i = pl.program_id(1), pl.program_id(2)

        @pl.when((j == 0) & (i == 0))
        def _():
            dq_ref[...] = jnp.zeros_like(dq_ref)

        @pl.when(i == 0)
        def _():
            dk_sc[...] = jnp.zeros_like(dk_sc)
            dv_sc[...] = jnp.zeros_like(dv_sc)

        def chunk(hh, rows, keys, masked):
            sl = slice(hh * SLOT, (hh + 1) * SLOT)
            n_rows = rows.stop - rows.start
            qv, kv, dov = q_ref[rows, sl], k_ref[keys, sl], do_ref[rows, sl]
            s = _nt(qv, kv)
            if masked:
                s = jnp.where(_causal(rows, keys), s, NEG)
            p = jnp.exp2(s - lse_ref[rows, hh * SLOT:hh * SLOT + 1])
            delta = jnp.sum(dov.astype(F32) * o_ref[rows, sl].astype(F32), axis=1, keepdims=True)
            dp = _nt(dov, v_ref[keys, sl])
            ds = (p * (dp - delta)).astype(BF16)
            dv_sc[keys, sl] += _tn(p.astype(BF16), dov)
            dk_sc[keys, sl] += _tn(ds, qv)
            r0 = pl.multiple_of(i * t + rows.start, n_rows)
            dq_ref[pl.ds(r0, n_rows), sl] += _nn(ds, kv) * MLA_SCALE

        @pl.when(i > j)
        def _():
            for hh in range(MLA_PACK):
                chunk(hh, slice(0, t), slice(0, t), False)

        @pl.when(i == j)
        def _():
            for hh in range(MLA_PACK):
                chunk(hh, slice(0, t), slice(0, t // 2), True)
                chunk(hh, slice(t // 2, t), slice(t // 2, t), True)

        @pl.when(i == nt - 1)
        def _():
            dk_ref[...] = dk_sc[...] * (1.0 / LOG2_E)
            dv_ref[...] = dv_sc[...]

    q_spec = pl.BlockSpec((t, wide), lambda h, j, i: (jnp.maximum(i, j), h))
    kv_spec = pl.BlockSpec((t, wide), lambda h, j, i: (j, h))
    head_spec = pl.BlockSpec((rows, wide), lambda h, j, i: (0, h))
    shp = jax.ShapeDtypeStruct(q.shape, F32)
    return pl.pallas_call(
        body, name="mla_attn_bwd", grid=(MLA_HEADS // MLA_PACK, nt, nt),
        in_specs=[q_spec, kv_spec, kv_spec, q_spec, q_spec, q_spec, pl.BlockSpec(memory_space=pl.ANY)],
        out_specs=[head_spec, kv_spec, kv_spec], out_shape=[shp, shp, shp],
        scratch_shapes=[pltpu.VMEM((t, wide), F32), pltpu.VMEM((t, wide), F32)],
        compiler_params=_cparams(),
    )(q, k, v, o, do, lse, after)


def _swa_specs(t):
    def prev(i):
        return jnp.maximum(i - 1, 0)
    kw = SWA_PACK * SLOT
    k0, v0 = SWA_HEADS // SWA_PACK, (SWA_HEADS + SWA_KV_HEADS) // SWA_PACK
    q3 = pl.BlockSpec((t, SWA_PACK * SWA_GROUP * SLOT), lambda h, i: (i, h))
    kp = pl.BlockSpec((t, kw), lambda h, i: (prev(i), k0 + h))
    kc = pl.BlockSpec((t, kw), lambda h, i: (i, k0 + h))
    vp = pl.BlockSpec((t, kw), lambda h, i: (prev(i), v0 + h))
    vc = pl.BlockSpec((t, kw), lambda h, i: (i, v0 + h))
    pcol = pl.BlockSpec((t, 1), lambda h, i: (i, 0))
    prow_p = pl.BlockSpec((1, t), lambda h, i: (0, prev(i)))
    prow_c = pl.BlockSpec((1, t), lambda h, i: (0, i))
    return [q3, kp, kc, vp, vc, pcol, prow_p, prow_c]


def _stack(ref, first):
    return jnp.concatenate([ref[:, (first + g) * SLOT:(first + g + 1) * SLOT] for g in range(SWA_GROUP)], axis=0)


def _swa_logits(q3, kp, kc, pq, pkp, pkc, slope_ref, kvh, i, t):
    r = lax.broadcasted_iota(jnp.int32, (t, t), 0)
    c = lax.broadcasted_iota(jnp.int32, (t, t), 1)
    ok_c = c <= r
    ok_p = (c - r) > jnp.where(i > 0, 0, t)
    dist_p, dist_c = pq - pkp, pq - pkc
    s_p3 = _nt(q3, kp) * (HEAD_DIM ** -0.5)
    s_c3 = _nt(q3, kc) * (HEAD_DIM ** -0.5)
    out = []
    for g in range(SWA_GROUP):
        slope = slope_ref[kvh * SWA_GROUP + g]
        rows = slice(g * t, (g + 1) * t)
        out.append((jnp.where(ok_p, s_p3[rows] - slope * dist_p, NEG),
                    jnp.where(ok_c, s_c3[rows] - slope * dist_c, NEG)))
    return out


def _swa_attn_fwd(proj, pos_col, pos_row, slopes, sinks):
    rows = proj.shape[0]
    t = WINDOW
    hw = SWA_HEADS * SLOT

    def body(slope_ref, sink_ref, q_ref, kp_ref, kc_ref, vp_ref, vc_ref, pq_ref, pkp_ref, pkc_ref, o_ref, lse_ref):
        i = pl.program_id(1)
        for kv in range(SWA_PACK):
            kvh = pl.program_id(0) * SWA_PACK + kv
            ksl = slice(kv * SLOT, (kv + 1) * SLOT)
            logits = _swa_logits(_stack(q_ref, kv * SWA_GROUP), kp_ref[:, ksl], kc_ref[:, ksl], pq_ref[...],
                                 pkp_ref[...], pkc_ref[...], slope_ref, kvh, i, t)
            e_p, e_c, norm = [], [], []
            for g, (s_p, s_c) in enumerate(logits):
                sl = slice((kv * SWA_GROUP + g) * SLOT, (kv * SWA_GROUP + g + 1) * SLOT)
                sink = sink_ref[kvh * SWA_GROUP + g]
                m = jnp.maximum(jnp.maximum(jnp.max(s_p, axis=1, keepdims=True),
                                            jnp.max(s_c, axis=1, keepdims=True)), sink)
                ep, ec = jnp.exp(s_p - m), jnp.exp(s_c - m)
                l = jnp.sum(ep, axis=1, keepdims=True) + jnp.sum(ec, axis=1, keepdims=True) + jnp.exp(sink - m)
                e_p.append(ep.astype(BF16))
                e_c.append(ec.astype(BF16))
                norm.append(l)
                lse_ref[:, sl] = jnp.broadcast_to(m + jnp.log(l), (t, SLOT))
            acc = (_nn(jnp.concatenate(e_p, axis=0), vp_ref[:, ksl])
                   + _nn(jnp.concatenate(e_c, axis=0), vc_ref[:, ksl]))
            for g in range(SWA_GROUP):
                sl = slice((kv * SWA_GROUP + g) * SLOT, (kv * SWA_GROUP + g + 1) * SLOT)
                o_ref[:, sl] = (acc[g * t:(g + 1) * t] / norm[g]).astype(o_ref.dtype)

    smem = pl.BlockSpec(memory_space=pltpu.SMEM)
    out_spec = pl.BlockSpec((t, SWA_PACK * SWA_GROUP * SLOT), lambda h, i: (i, h))
    return pl.pallas_call(
        body, name="swa_attn_fwd", grid=(SWA_KV_HEADS // SWA_PACK, rows // t),
        in_specs=[smem, smem] + _swa_specs(t), out_specs=[out_spec, out_spec],
        out_shape=[jax.ShapeDtypeStruct((rows, hw), BF16), jax.ShapeDtypeStruct((rows, hw), F32)],
        compiler_params=_cparams(),
    )(slopes, sinks, proj, proj, proj, proj, proj, pos_col, pos_row, pos_row)


def _swa_attn_bwd(proj, o, do, lse, pos_col, pos_row, slopes, sinks, after):
    rows = proj.shape[0]
    t = WINDOW
    hw = SWA_HEADS * SLOT
    scale = HEAD_DIM ** -0.5

    def body(slope_ref, sink_ref, q_ref, kp_ref, kc_ref, vp_ref, vc_ref, pq_ref, pkp_ref, pkc_ref,
             o_ref, do_ref, lse_ref, after_ref, dq_ref, dk_ref, dv_ref, dsink_ref):
        i = pl.program_id(1)

        @pl.when(i == 0)
        def _():
            dk_ref[...] = jnp.zeros_like(dk_ref)
            dv_ref[...] = jnp.zeros_like(dv_ref)
            dsink_ref[...] = jnp.zeros_like(dsink_ref)

        r_c = pl.multiple_of(i * t, t)
        r_p = pl.multiple_of(jnp.maximum(i - 1, 0) * t, t)
        for kv in range(SWA_PACK):
            kvh = pl.program_id(0) * SWA_PACK + kv
            ksl = slice(kv * SLOT, (kv + 1) * SLOT)
            q3, do3 = _stack(q_ref, kv * SWA_GROUP), _stack(do_ref, kv * SWA_GROUP)
            logits = _swa_logits(q3, kp_ref[:, ksl], kc_ref[:, ksl], pq_ref[...], pkp_ref[...], pkc_ref[...],
                                 slope_ref, kvh, i, t)
            dp_p3, dp_c3 = _nt(do3, vp_ref[:, ksl]), _nt(do3, vc_ref[:, ksl])
            p_p, p_c, ds_p, ds_c = [], [], [], []
            for g, (s_p, s_c) in enumerate(logits):
                head = kv * SWA_GROUP + g
                sl = slice(head * SLOT, (head + 1) * SLOT)
                rws = slice(g * t, (g + 1) * t)
                lse_g = lse_ref[:, head * SLOT:head * SLOT + 1]
                pp, pc = jnp.exp(s_p - lse_g), jnp.exp(s_c - lse_g)
                delta = jnp.sum(do_ref[:, sl].astype(F32) * o_ref[:, sl].astype(F32), axis=1, keepdims=True)
                p_p.append(pp.astype(BF16))
                p_c.append(pc.astype(BF16))
                ds_p.append((pp * (dp_p3[rws] - delta)).astype(BF16))
                ds_c.append((pc * (dp_c3[rws] - delta)).astype(BF16))
                sink = sink_ref[kvh * SWA_GROUP + g]
                dsink = -jnp.sum(jnp.exp(sink - lse_g) * delta, axis=0, keepdims=True)
                dsink_ref[head * 8:(head + 1) * 8, :] += jnp.broadcast_to(dsink, (8, SLOT))
            p_p3, p_c3 = jnp.concatenate(p_p, axis=0), jnp.concatenate(p_c, axis=0)
            ds_p3, ds_c3 = jnp.concatenate(ds_p, axis=0), jnp.concatenate(ds_c, axis=0)
            dq3 = (_nn(ds_p3, kp_ref[:, ksl]) + _nn(ds_c3, kc_ref[:, ksl])) * scale
            for g in range(SWA_GROUP):
                head = kv * SWA_GROUP + g
                dq_ref[:, head * SLOT:(head + 1) * SLOT] = dq3[g * t:(g + 1) * t]
            dk_ref[pl.ds(r_c, t), ksl] += _tn(ds_c3, q3) * scale
            dv_ref[pl.ds(r_c, t), ksl] += _tn(p_c3, do3)
            dk_ref[pl.ds(r_p, t), ksl] += _tn(ds_p3, q3) * scale
            dv_ref[pl.ds(r_p, t), ksl] += _tn(p_p3, do3)

    smem = pl.BlockSpec(memory_space=pltpu.SMEM)
    qlike = pl.BlockSpec((t, SWA_PACK * SWA_GROUP * SLOT), lambda h, i: (i, h))
    kv_out = pl.BlockSpec((rows, SWA_PACK * SLOT), lambda h, i: (0, h))
    return pl.pallas_call(
        body, name="swa_attn_bwd", grid=(SWA_KV_HEADS // SWA_PACK, rows // t),
        in_specs=[smem, smem] + _swa_specs(t) + [qlike, qlike, qlike, pl.BlockSpec(memory_space=pl.ANY)],
        out_specs=[qlike, kv_out, kv_out,
                   pl.BlockSpec((SWA_PACK * SWA_GROUP * 8, SLOT), lambda h, i: (h, 0))],
        out_shape=[jax.ShapeDtypeStruct((rows, hw), F32), jax.ShapeDtypeStruct((rows, SWA_KV_HEADS * SLOT), F32),
                   jax.ShapeDtypeStruct((rows, SWA_KV_HEADS * SLOT), F32),
                   jax.ShapeDtypeStruct((SWA_HEADS * 8, SLOT), F32)],
        compiler_params=_cparams(),
    )(slopes, sinks, proj, proj, proj, proj, proj, pos_col, pos_row, pos_row, o, do, lse, after)


def _cross_attn_fwd(proj, qoff, kvmem):
    rows = proj.shape[0]
    t = min(TQ_CROSS, rows)

    def body(q_ref, k_ref, v_ref, o_ref):
        s = _nt(q_ref[...].astype(BF16), k_ref[...]) * (HEAD_DIM ** -0.5)
        e = jnp.exp(s - jnp.max(s, axis=1, keepdims=True))
        p = e / jnp.sum(e, axis=1, keepdims=True)
        o_ref[...] = _nn(p.astype(BF16), v_ref[...]).astype(o_ref.dtype)

    return pl.pallas_call(
        body, name="cross_attn_fwd", grid=(rows // t, MEM_HEADS),
        in_specs=[pl.BlockSpec((t, SLOT), lambda i, h: (i, qoff + h)),
                  pl.BlockSpec((N_MEM, SLOT), lambda i, h: (0, h)),
                  pl.BlockSpec((N_MEM, SLOT), lambda i, h: (0, MEM_HEADS + h))],
        out_specs=pl.BlockSpec((t, SLOT), lambda i, h: (i, h)),
        out_shape=jax.ShapeDtypeStruct((rows, MEM_HEADS * SLOT), BF16), compiler_params=_cparams(),
    )(proj, kvmem, kvmem)


def _cross_attn_bwd(proj, qoff, kvmem, do):
    rows = proj.shape[0]
    t = min(TQ_CROSS, rows)
    scale = HEAD_DIM ** -0.5

    def body(q_ref, k_ref, v_ref, do_ref, dq_ref, dk_ref, dv_ref):
        @pl.when(pl.program_id(1) == 0)
        def _():
            dk_ref[...] = jnp.zeros_like(dk_ref)
            dv_ref[...] = jnp.zeros_like(dv_ref)

        qv, kv, dov = q_ref[...].astype(BF16), k_ref[...], do_ref[...]
        s = _nt(qv, kv) * scale
        e = jnp.exp(s - jnp.max(s, axis=1, keepdims=True))
        p = e / jnp.sum(e, axis=1, keepdims=True)
        dp = _nt(dov, v_ref[...])
        ds = (p * (dp - jnp.sum(p * dp, axis=1, keepdims=True))).astype(BF16)
        dq_ref[...] = _nn(ds, kv) * scale
        dk_ref[...] += _tn(ds, qv) * scale
        dv_ref[...] += _tn(p.astype(BF16), dov)

    mem_out = pl.BlockSpec((N_MEM, SLOT), lambda h, i: (0, h))
    return pl.pallas_call(
        body, name="cross_attn_bwd", grid=(MEM_HEADS, rows // t),
        in_specs=[pl.BlockSpec((t, SLOT), lambda h, i: (i, qoff + h)),
                  pl.BlockSpec((N_MEM, SLOT), lambda h, i: (0, h)),
                  pl.BlockSpec((N_MEM, SLOT), lambda h, i: (0, MEM_HEADS + h)),
                  pl.BlockSpec((t, SLOT), lambda h, i: (i, h))],
        out_specs=[pl.BlockSpec((t, SLOT), lambda h, i: (i, h)), mem_out, mem_out],
        out_shape=[jax.ShapeDtypeStruct((rows, MEM_HEADS * SLOT), F32),
                   jax.ShapeDtypeStruct((N_MEM, MEM_HEADS * SLOT), F32),
                   jax.ShapeDtypeStruct((N_MEM, MEM_HEADS * SLOT), F32)],
        compiler_params=_cparams(),
    )(proj, kvmem, kvmem, do)


def _place():
    return lax.axis_index("x"), lax.axis_index("y"), lax.axis_index("c")


def _flip(v, bit):
    return 1 - v if bit else v


def _all_gather(blocks, name):
    nb = len(blocks)

    def body(*refs):
        x_refs, out_refs = refs[:nb], refs[nb:2 * nb]
        send_sems, recv_sems, local_sems = refs[2 * nb:]
        x, y, c = _place()
        me, sibling = (x, y, c), (x, y, 1 - c)
        chips = [(1 - x, y), (x, 1 - y), (1 - x, 1 - y)]

        def copy(b, k, blk, to, from_input=False):
            slot = out_refs[b].at[4 * blk[0] + 2 * blk[1] + blk[2]]
            return pltpu.make_async_remote_copy(
                src_ref=x_refs[b] if from_input else slot, dst_ref=slot,
                send_sem=send_sems.at[b, k], recv_sem=recv_sems.at[b, k],
                device_id=to, device_id_type=pl.DeviceIdType.MESH)

        mine = [pltpu.make_async_copy(x_refs[b], out_refs[b].at[4 * x + 2 * y + c], local_sems.at[b])
                for b in range(nb)]
        for cp in mine:
            cp.start()
        first = []
        for b in range(nb):
            first.append(copy(b, 0, me, sibling, from_input=True))
            first += [copy(b, 1 + n, me, (*chip, c), from_input=True) for n, chip in enumerate(chips)]
        for cp in first:
            cp.start()
        passed = []
        for n, chip in enumerate(chips):
            for b in range(nb):
                copy(b, 1 + n, (*chip, c), me).wait_recv()
                passed.append(copy(b, 4 + n, (*chip, c), sibling))
                passed[-1].start()
        for b in range(nb):
            copy(b, 0, sibling, me).wait_recv()
            for n, chip in enumerate(chips):
                copy(b, 4 + n, (*chip, 1 - c), me).wait_recv()
        for cp in first + passed:
            cp.wait_send()
        for cp in mine:
            cp.wait()

    any_spec = pl.BlockSpec(memory_space=pl.ANY)
    return pl.pallas_call(
        body, name=name, in_specs=[any_spec] * nb, out_specs=[any_spec] * nb,
        out_shape=[jax.ShapeDtypeStruct((N_DEV,) + blk.shape, blk.dtype) for blk in blocks],
        scratch_shapes=[pltpu.SemaphoreType.DMA((nb, 7)), pltpu.SemaphoreType.DMA((nb, 7)),
                        pltpu.SemaphoreType.DMA((nb,))],
    )(*blocks)


def _peers(x, y, c):
    out = []
    for n in range(1, N_DEV):
        peer = (_flip(x, n & 4), _flip(y, n & 2), _flip(c, n & 1))
        out.append((n - 1, peer, 4 * peer[0] + 2 * peer[1] + peer[2]))
    return out


_HBM = pl.BlockSpec(memory_space=pltpu.HBM)
_SEM = pl.BlockSpec(memory_space=pltpu.SEMAPHORE)


def _exchange_start(srcs, scatter, name, after=None):
    ns = len(srcs)
    lands = [lax.empty(s.shape if scatter else (N_DEV,) + s.shape, s.dtype) for s in srcs]

    def body(*refs):
        src_refs, land_refs = refs[:ns], refs[ns:2 * ns]
        pos = 2 * ns + (1 if after is not None else 0)
        send_sems, recv_sems, token = refs[pos], refs[pos + 1], refs[-1]
        x, y, c = _place()
        my_idx = 4 * x + 2 * y + c
        for col, peer, peer_idx in _peers(x, y, c):
            for b in range(ns):
                pltpu.make_async_remote_copy(
                    src_ref=src_refs[b].at[peer_idx] if scatter else src_refs[b], dst_ref=land_refs[b].at[my_idx],
                    send_sem=send_sems.at[b * (N_DEV - 1) + col], recv_sem=recv_sems.at[b * (N_DEV - 1) + col],
                    device_id=peer, device_id_type=pl.DeviceIdType.MESH).start()
        token[...] = jnp.zeros_like(token)

    args = [pltpu.with_memory_space_constraint(a, pltpu.HBM) for a in list(srcs) + lands]
    in_specs = [_HBM] * (2 * ns)
    if after is not None:
        args.append(after)
        in_specs.append(pl.BlockSpec(memory_space=pl.ANY))
    out = pl.pallas_call(
        body, name=name, in_specs=in_specs,
        out_specs=[_SEM, _SEM] + [_HBM] * (2 * ns) + [pl.BlockSpec(memory_space=pltpu.VMEM)],
        out_shape=[pltpu.SemaphoreType.DMA((ns * (N_DEV - 1),)), pltpu.SemaphoreType.DMA((ns * (N_DEV - 1),))]
        + [pltpu.HBM(a.shape, a.dtype) for a in list(srcs) + lands] + [jax.ShapeDtypeStruct((8, SLOT), F32)],
        input_output_aliases={k: 2 + k for k in range(2 * ns)},
        compiler_params=pltpu.CompilerParams(has_side_effects=pltpu.SideEffectType.DATAFLOW_SIDE_EFFECTING),
    )(*args)
    return (out[0], out[1], out[2:2 + ns], out[2 + ns:2 + 2 * ns], scatter), out[-1]


def _exchange_wait(handle, after, name):
    send_sems, recv_sems, srcs, lands, scatter = handle
    ns = len(srcs)

    def body(*refs):
        src_refs, land_refs = refs[:ns], refs[ns:2 * ns]
        send_ref, recv_ref = refs[2 * ns], refs[2 * ns + 1]
        x, y, c = _place()
        for col, peer, peer_idx in _peers(x, y, c):
            for b in range(ns):
                copy = pltpu.make_async_remote_copy(
                    src_ref=src_refs[b].at[peer_idx] if scatter else src_refs[b], dst_ref=land_refs[b].at[peer_idx],
                    send_sem=send_ref.at[b * (N_DEV - 1) + col], recv_sem=recv_ref.at[b * (N_DEV - 1) + col],
                    device_id=peer, device_id_type=pl.DeviceIdType.MESH)
                copy.wait_send()
                copy.wait_recv()

    out = pl.pallas_call(
        body, name=name, in_specs=[_HBM] * (2 * ns) + [_SEM, _SEM, pl.BlockSpec(memory_space=pl.ANY)],
        out_specs=[_HBM] * (2 * ns),
        out_shape=[pltpu.HBM(a.shape, a.dtype) for a in list(srcs) + list(lands)],
        input_output_aliases={k: k for k in range(2 * ns)},
        compiler_params=pltpu.CompilerParams(has_side_effects=pltpu.SideEffectType.DATAFLOW_SIDE_EFFECTING),
    )(*srcs, *lands, send_sems, recv_sems, after)
    my_idx = 4 * lax.axis_index("x") + 2 * lax.axis_index("y") + lax.axis_index("c")
    landed = []
    for src, land in zip(out[:ns], out[ns:]):
        own = lax.dynamic_index_in_dim(src, my_idx, 0, keepdims=True) if scatter else src[None]
        landed.append(lax.dynamic_update_index_in_dim(land, own, my_idx, 0))
    return landed


def _adamw(parts, w, m, v, name):
    lyr, rows, cols = w.shape
    assert len(parts) == lyr
    tr = ADAM_ROWS if cols > 512 else 2 * ADAM_ROWS
    while rows % tr:
        tr //= 2
    tr = min(tr, rows)

    def body(*refs):
        p_refs = refs[:lyr]
        w_ref, m_ref, v_ref, g_out, d_out, m_out, v_out = refs[lyr:]
        for k in range(lyr):
            @pl.when(pl.program_id(0) == k)
            def _(p_ref=p_refs[k]):
                g = p_ref[0].astype(F32)
                for s in range(1, N_DEV):
                    g = g + p_ref[s].astype(F32)
                m2 = ADAM_B1 * m_ref[...] + (1.0 - ADAM_B1) * g
                v2 = ADAM_B2 * v_ref[...] + (1.0 - ADAM_B2) * (g * g)
                m_hat = m2 / (1.0 - ADAM_B1 ** ADAM_STEP)
                v_hat = v2 / (1.0 - ADAM_B2 ** ADAM_STEP)
                g_out[...] = g
                d_out[...] = -ADAM_LR * (m_hat / (jnp.sqrt(v_hat) + ADAM_EPS) + ADAM_WD * w_ref[...])
                m_out[...] = m2
                v_out[...] = v2

    def part_spec(k):
        return pl.BlockSpec((N_DEV, tr, cols), lambda l, i: (0, jnp.where(l == k, i, 0), 0))

    spec = pl.BlockSpec((None, tr, cols), lambda l, i: (l, i, 0))
    shp = jax.ShapeDtypeStruct((lyr, rows, cols), F32)
    return pl.pallas_call(
        body, name=name, grid=(lyr, rows // tr),
        in_specs=[part_spec(k) for k in range(lyr)] + [spec, spec, spec],
        out_specs=[spec] * 4, out_shape=[shp] * 4, compiler_params=_cparams(),
    )(*parts, w, m, v)


def _pack(arrays, lanes, row_mult, dtype):
    flat = jnp.concatenate([a.reshape(-1).astype(dtype) for a in arrays])
    unit = lanes * row_mult
    total = -(-flat.shape[0] // unit) * unit
    return jnp.pad(flat, (0, total - flat.shape[0])).reshape(total // lanes, lanes)


def _unpack(packed, shapes):
    flat = packed.reshape(-1)
    out, off = [], 0
    for shp in shapes:
        n = 1
        for d in shp:
            n *= d
        out.append(flat[off:off + n].reshape(shp))
        off += n
    return out


def _pad_slots(w, axis):
    axis = axis % w.ndim
    n = w.shape[axis] // HEAD_DIM
    shp = w.shape[:axis] + (n, HEAD_DIM) + w.shape[axis + 1:]
    pad = [(0, 0)] * (w.ndim + 1)
    pad[axis + 1] = (0, SLOT - HEAD_DIM)
    return jnp.pad(w.reshape(shp), pad).reshape(w.shape[:axis] + (n * SLOT,) + w.shape[axis + 1:])


def _unpad_slots(w, axis, keep=HEAD_DIM):
    axis = axis % w.ndim
    n = w.shape[axis] // SLOT
    shp = w.shape[:axis] + (n, SLOT) + w.shape[axis + 1:]
    idx = [slice(None)] * (w.ndim + 1)
    idx[axis + 1] = slice(0, keep)
    return w.reshape(shp)[tuple(idx)].reshape(w.shape[:axis] + (n * keep,) + w.shape[axis + 1:])


def _mla_in_pad(w):
    z = functools.partial(jnp.zeros, dtype=w.dtype)
    rows = w.shape[0]
    return jnp.concatenate([w[:, :384], z((rows, 64)), w[:, 640:672], z((rows, 32)), w[:, 384:640],
                            _pad_slots(w[:, 672:], 1)], axis=1)


def _mla_in_unpad(d):
    return jnp.concatenate([d[:, :384], d[:, 512:768], d[:, 448:480], _unpad_slots(d[:, 768:], 1)], axis=1)


def _mla_uq_pad(w):
    return jnp.pad(w.reshape(w.shape[0], MLA_HEADS, MLA_QK), ((0, 0), (0, 0), (0, SLOT - MLA_QK))).reshape(
        w.shape[0], MLA_HEADS * SLOT)


def _mla_ukv_pad(w):
    w3 = w.reshape(w.shape[0], MLA_HEADS, 2 * HEAD_DIM)
    pad = ((0, 0), (0, 0), (0, SLOT - HEAD_DIM))
    k = jnp.pad(w3[:, :, :HEAD_DIM], pad).reshape(w.shape[0], -1)
    v = jnp.pad(w3[:, :, HEAD_DIM:], pad).reshape(w.shape[0], -1)
    return jnp.concatenate([k, v], axis=1)


def _mla_ukv_unpad(d):
    hw = MLA_HEADS * SLOT
    k = d[:, :hw].reshape(d.shape[0], MLA_HEADS, SLOT)[:, :, :HEAD_DIM]
    v = d[:, hw:].reshape(d.shape[0], MLA_HEADS, SLOT)[:, :, :HEAD_DIM]
    return jnp.concatenate([k, v], axis=2).reshape(d.shape[0], MLA_HEADS * 2 * HEAD_DIM)


def _join(gathered, axis):
    nd, a, b = gathered.shape
    if axis == 1:
        return gathered.reshape(nd * a, b)
    return gathered.transpose(1, 0, 2).reshape(a, nd * b)


def _split(full, axis):
    r, c = full.shape
    if axis == 1:
        return full.reshape(N_DEV, r // N_DEV, c).astype(BF16)
    return full.reshape(r, N_DEV, c // N_DEV).transpose(1, 0, 2).astype(BF16)


def kernel(x, mem, positions, attn_norm_g, mlp_norm_g, mem_norm_g, final_norm_g, mla_w_in, mla_q_norm_g, mla_kv_norm_g, mla_w_uq, mla_w_ukv, swa_w_in, swa_sinks, w_mem_kv, w_o, mlp_w_up, mlp_w_down, loss_target, m_attn_norm_g, m_mlp_norm_g, m_mem_norm_g, m_final_norm_g, m_mla_w_in, m_mla_q_norm_g, m_mla_kv_norm_g, m_mla_w_uq, m_mla_w_ukv, m_swa_w_in, m_swa_sinks, m_w_mem_kv, m_w_o, m_mlp_w_up, m_mlp_w_down, v_attn_norm_g, v_mlp_norm_g, v_mem_norm_g, v_final_norm_g, v_mla_w_in, v_mla_q_norm_g, v_mla_kv_norm_g, v_mla_w_uq, v_mla_w_ukv, v_swa_w_in, v_swa_sinks, v_w_mem_kv, v_w_o, v_mlp_w_up, v_mlp_w_down):
    given = dict(locals())
    seq = x.shape[1]
    x0 = x.reshape(seq, D_MODEL)
    tgt = loss_target.reshape(seq, D_MODEL)
    mem0 = mem.reshape(N_MEM, D_MODEL)
    pos = positions.reshape(seq).astype(F32)
    pos_col, pos_row = pos.reshape(seq, 1), pos.reshape(1, seq)

    def layer_names(i):
        mixer = ("mla_w_in", "mla_w_uq", "mla_w_ukv") if i % 2 == 0 else ("swa_w_in",)
        return [(n, i // 2) for n in mixer] + [(n, i) for n in ("w_mem_kv", "w_o", "mlp_w_up", "mlp_w_down")]

    def local_weights(names):
        return [given[n][l].astype(BF16) for n, l in names]

    first_attn, first_mlp = layer_names(0)[:-2], layer_names(0)[-2:]
    weights = [dict(zip([n for n, _ in first_attn], _all_gather(local_weights(first_attn), "gather_weights_first")))]
    coming_mlp, first_token = _exchange_start(local_weights(first_mlp), False, "gather_weights_start_0",
                                              after=weights[0]["w_o"])

    consts = _lane_consts()
    tabs = _rope_tables(pos_col, consts)
    slopes = 2.0 ** (-8.0 * (jnp.arange(SWA_HEADS, dtype=F32) + 1.0) / SWA_HEADS)

    mem_n = _rmsnorm_fwd(mem0, 0, D_MODEL, mem_norm_g, "rmsnorm_fwd_mem")

    saved = []
    xc = x0
    for i in range(DEPTH):
        j = i // 2
        wts = weights[i]
        s = {"x_in": xc}
        token = None
        if i + 1 < DEPTH:
            coming, token = _exchange_start(local_weights(layer_names(i + 1)), False,
                                            "gather_weights_start_%d" % (i + 1),
                                            after=first_token if i == 0 else wts["w_o"])
        hn = _rmsnorm_fwd(xc, 0, D_MODEL, attn_norm_g[i], "rmsnorm_fwd", after=token)
        if i % 2 == 0:
            w_in = _mla_in_pad(_join(wts["mla_w_in"], 1))
            w_uq = _mla_uq_pad(_join(wts["mla_w_uq"], 2))
            w_kv = _mla_ukv_pad(_join(wts["mla_w_ukv"], 2))
            proj = _mm(hn, w_in, "nn", F32, "mm_mla_in")
            cqn = _rmsnorm_fwd(proj, 0, MLA_Q_RANK, mla_q_norm_g[j], "rmsnorm_fwd_q")
            ckvn = _rmsnorm_fwd(proj, 2, MLA_KV_RANK, mla_kv_norm_g[j], "rmsnorm_fwd_kv")
            qraw = _mm(cqn, w_uq, "nn", F32, "mm_mla_uq")
            kvraw = _mm(ckvn, w_kv, "nn", F32, "mm_mla_ukv")
            q, k, v = _mla_rope_fwd(qraw, kvraw, proj, tabs)
            o, lse = _mla_attn_fwd(q, k, v)
            qoff = MLA_QOFF
            s.update(w_uq=w_uq, w_kv=w_kv, cqn=cqn, ckvn=ckvn, q=q, k=k, v=v)
        else:
            w_in = _pad_slots(_join(wts["swa_w_in"], 2), 1)
            proj = _mm(hn, w_in, "nn", BF16, "mm_swa_in")
            o, lse = _swa_attn_fwd(proj, pos_col, pos_row, slopes, swa_sinks[j])
            qoff = SWA_QOFF
        w_mem = _pad_slots(_join(wts["w_mem_kv"], 1), 1)
        w_out = _pad_slots(_join(wts["w_o"], 1), 0)
        w_o_mix, w_o_cross = w_out[:SWA_HEADS * SLOT], w_out[SWA_HEADS * SLOT:]
        kvmem = _mm(mem_n, w_mem, "nn", BF16, "mm_mem_kv")
        cross = _cross_attn_fwd(proj, qoff, kvmem)
        x1 = _mm(o, w_o_mix, "nn", F32, "mm_o_mix", res=xc)
        x1 = _mm(cross, w_o_cross, "nn", F32, "mm_o_cross", res=x1)
        hn2 = _rmsnorm_fwd(x1, 0, D_MODEL, mlp_norm_g[i], "rmsnorm_fwd")
        if i == 0:
            wts.update(zip([n for n, _ in first_mlp], _exchange_wait(coming_mlp, hn2, "gather_weights_wait_0")))
        act, act2 = _mm(hn2, wts["mlp_w_up"], "nn", BF16, "mm_mlp_up", epi="relu2", b_blk="cols")
        xc = _mm(act2, wts["mlp_w_down"], "nn", F32, "mm_mlp_down", res=x1, b_blk="rows")
        s.update(hn=hn, w_in=w_in, proj=proj, o=o, lse=lse, qoff=qoff, w_mem=w_mem, w_o_mix=w_o_mix,
                 w_o_cross=w_o_cross, kvmem=kvmem, cross=cross, x1=x1, hn2=hn2, act=act, act2=act2)
        saved.append(s)
        if i + 1 < DEPTH:
            got = _exchange_wait(coming, xc, "gather_weights_wait_%d" % (i + 1))
            weights.append(dict(zip([n for n, _ in layer_names(i + 1)], got)))

    dx, dx_b, dg_final, loss_part = _loss_head(xc, final_norm_g, tgt)
    loss = lax.psum(loss_part[0, 0], MESH_AXES)

    gains = {n: [None] * DEPTH for n in ("attn_norm_g", "mlp_norm_g")}
    for n in ("mla_q_norm_g", "mla_kv_norm_g", "swa_sinks"):
        gains[n] = [None] * 2
    leaving = {}
    token = None
    dmem_n = None
    for i in reversed(range(DEPTH)):
        j = i // 2
        s = saved[i]
        wts = weights[i]
        out = {}
        du = _mm(dx_b, wts["mlp_w_down"], "nt", BF16, "mm_mlp_down_dx", aux=s["act"], epi="mul2aux", b_blk="rows",
                 after=token)
        out["mlp_w_down"] = _mm(s["act2"], dx_b, "tn", BF16, "mm_mlp_down_dw", o_blk="rows")
        out["mlp_w_up"] = _mm(s["hn2"], du, "tn", BF16, "mm_mlp_up_dw", o_blk="cols")
        dx1, dx1_b, dg = _mm(du, wts["mlp_w_up"], "nt", F32, "mm_mlp_up_dx", b_blk="cols",
                             epi="normbwd", norm=(s["x1"], mlp_norm_g[i], dx))
        gains["mlp_norm_g"][i] = dg[0]

        do = _mm(dx1_b, s["w_o_mix"], "nt", BF16, "mm_o_mix_dx")
        dcross = _mm(dx1_b, s["w_o_cross"], "nt", BF16, "mm_o_cross_dx")
        dw_o = jnp.concatenate([_mm(s["o"], dx1_b, "tn", F32, "mm_o_mix_dw"),
                                _mm(s["cross"], dx1_b, "tn", F32, "mm_o_cross_dw")], axis=0)
        out["w_o"] = _split(_unpad_slots(dw_o, 0), 1)
        dqc, dkm, dvm = _cross_attn_bwd(s["proj"], s["qoff"], s["kvmem"], dcross)
        dkvmem = jnp.concatenate([dkm, dvm], axis=1).astype(BF16)
        out["w_mem_kv"] = _split(_unpad_slots(_mm(mem_n, dkvmem, "tn", F32, "mm_mem_kv_dw"), 1), 1)
        dmem_n = _mm(dkvmem, s["w_mem"], "nt", F32, "mm_mem_kv_dx" if dmem_n is None else "mm_mem_kv_dx_acc",
                     res=dmem_n)
        leaving[(i, "main")], token = _exchange_start([out[n] for n, _ in layer_names(i)[-4:]], True,
                                                      "exchange_grads_main_start_%d" % i)

        if i % 2 == 0:
            dq, dk, dv = _mla_attn_bwd(s["q"], s["k"], s["v"], s["o"], do, s["lse"], token)
            dqraw, dkv, dkr = _mla_rope_bwd(dq, dk, dv, tabs, consts)
            dcqn = _mm(dqraw, s["w_uq"], "nt", F32, "mm_mla_uq_dx")
            out["mla_w_uq"] = _split(_unpad_slots(_mm(s["cqn"], dqraw, "tn", F32, "mm_mla_uq_dw"), 1, MLA_QK), 2)
            dckvn = _mm(dkv, s["w_kv"], "nt", F32, "mm_mla_ukv_dx")
            out["mla_w_ukv"] = _split(_mla_ukv_unpad(_mm(s["ckvn"], dkv, "tn", F32, "mm_mla_ukv_dw")), 2)
            dcq, dg = _rmsnorm_bwd(s["proj"], 0, MLA_Q_RANK, mla_q_norm_g[j], dcqn, None, BF16, "rmsnorm_bwd_q")
            gains["mla_q_norm_g"][j] = dg[0]
            dckv, dg = _rmsnorm_bwd(s["proj"], 2, MLA_KV_RANK, mla_kv_norm_g[j], dckvn, None, BF16, "rmsnorm_bwd_kv")
            gains["mla_kv_norm_g"][j] = dg[0]
            dproj = jnp.concatenate([dcq, dkr.astype(BF16), dckv, dqc.astype(BF16)], axis=1)
            in_dx = "mm_mla_in_dx"
            out["mla_w_in"] = _split(_mla_in_unpad(_mm(s["hn"], dproj, "tn", F32, "mm_mla_in_dw")), 1)
        else:
            dq, dk, dv, dsink = _swa_attn_bwd(s["proj"], s["o"], do, s["lse"], pos_col, pos_row, slopes, swa_sinks[j],
                                              token)
            gains["swa_sinks"][j] = dsink[::8, 0]
            dproj = jnp.concatenate([dq, dk, dv, dqc], axis=1).astype(BF16)
            in_dx = "mm_swa_in_dx"
            out["swa_w_in"] = _split(_unpad_slots(_mm(s["hn"], dproj, "tn", F32, "mm_swa_in_dw"), 1), 2)
        dx, dx_b, dg = _mm(dproj, s["w_in"], "nt", F32, in_dx, epi="normbwd", norm=(s["x_in"], attn_norm_g[i], dx1))
        gains["attn_norm_g"][i] = dg[0]

        leaving[(i, "mixer")], token = _exchange_start([out[n] for n, _ in layer_names(i)[:-4]], True,
                                                       "exchange_grads_mixer_start_%d" % i)

    _, dg_mem = _rmsnorm_bwd(mem0, 0, D_MODEL, mem_norm_g, dmem_n, None, BF16, "rmsnorm_bwd_mem")
    gains = {n: jnp.stack(g) for n, g in gains.items()}
    gains["mem_norm_g"] = dg_mem[0]
    gains["final_norm_g"] = dg_final[0]

    result = {}

    def adamw_of(names, received):
        for n in names:
            parts = [received[(n, l)] for l in range(given[n].shape[0])]
            for kind, r in enumerate(_adamw(parts, given[n], given["m_" + n], given["v_" + n], "adamw_" + n)):
                result[(kind, n)] = r

    received = {}
    for i in reversed(range(DEPTH)):
        got = _exchange_wait(leaving[(i, "main")], dx, "exchange_grads_main_wait_%d" % i)
        received.update(zip(layer_names(i)[-4:], got))
    adamw_of(("mlp_w_up", "mlp_w_down", "w_o", "w_mem_kv"), received)
    for i in reversed(range(DEPTH)):
        got = _exchange_wait(leaving[(i, "mixer")], result[(0, "w_mem_kv")], "exchange_grads_mixer_wait_%d" % i)
        received.update(zip(layer_names(i)[:-4], got))
    adamw_of(("mla_w_in", "mla_w_uq", "mla_w_ukv", "swa_w_in"), received)

    rep_shapes = [given[n].shape for n in REPLICATED]
    rep_parts = _all_gather([_pack([gains[n] for n in REPLICATED], SLOT, 8, F32)], "gather_gain_grads")[0]
    rep_packed = [_pack([given[p + n] for n in REPLICATED], SLOT, 8, F32)[None] for p in ("", "m_", "v_")]
    for kind, r in enumerate(_adamw([rep_parts], *rep_packed, "adamw_gains")):
        for n, part in zip(REPLICATED, _unpack(r[0], rep_shapes)):
            result[(kind, n)] = part

    outs = [loss, dx.reshape(1, seq, D_MODEL)]
    for kind in range(4):
        outs += [result[(kind, n)] for n in WEIGHT_ORDER]
    return tuple(outs)
```

```python
import functools

import jax
import jax.numpy as jnp
from jax import lax
from jax.experimental import pallas as pl
from jax.experimental.pallas import tpu as pltpu

F32 = jnp.float32
BF16 = jnp.bfloat16

D_MODEL = 1024
D_FF = 4096
N_MEM = 256
DEPTH = 4
SLOT = 128
HEAD_DIM = 64
MLA_HEADS = 12
MLA_QK = 96
MLA_Q_RANK = 384
MLA_KV_RANK = 256
SWA_HEADS = 12
SWA_KV_HEADS = 4
SWA_GROUP = 3
MEM_HEADS = 4
WINDOW = 128
EPS = 1e-6
NEG = -1e30
ROPE_THETA = 10000.0
N_DEV = 8

ADAM_LR = 0.001
ADAM_B1 = 0.9
ADAM_B2 = 0.999
ADAM_EPS = 1e-08
ADAM_WD = 0.01
ADAM_STEP = 10

TM = 512
TQ_MLA = 1024
MLA_PACK = 2
SWA_PACK = 4
TQ_CROSS = 2048
MM_VMEM_BUDGET = 38 * 1024 * 1024
ADAM_ROWS = 128
VMEM_LIMIT = 56 * 1024 * 1024

MESH_AXES = ("x", "y", "c")

LOG2_E = 1.4426950408889634
MLA_SCALE = MLA_QK ** -0.5
MLA_Q_SCALE = MLA_SCALE * LOG2_E

MLA_PAD_IN = 384 + SLOT + 256 + MEM_HEADS * SLOT
MLA_QOFF = (384 + SLOT + 256) // SLOT
SWA_PAD_IN = (SWA_HEADS + 2 * SWA_KV_HEADS + MEM_HEADS) * SLOT
SWA_QOFF = SWA_HEADS + 2 * SWA_KV_HEADS

SHARDED = (
    ("mla_w_in", 1), ("mla_w_uq", 2), ("mla_w_ukv", 2), ("swa_w_in", 2),
    ("w_mem_kv", 1), ("w_o", 1), ("mlp_w_up", 2), ("mlp_w_down", 1),
)
REPLICATED = ("attn_norm_g", "mlp_norm_g", "mem_norm_g", "final_norm_g",
              "mla_q_norm_g", "mla_kv_norm_g", "swa_sinks")
WEIGHT_ORDER = ("attn_norm_g", "mlp_norm_g", "mem_norm_g", "final_norm_g", "mla_w_in",
                "mla_q_norm_g", "mla_kv_norm_g", "mla_w_uq", "mla_w_ukv", "swa_w_in",
                "swa_sinks", "w_mem_kv", "w_o", "mlp_w_up", "mlp_w_down")


def _cparams():
    return pltpu.CompilerParams(vmem_limit_bytes=VMEM_LIMIT)


_DIMS = {"nn": (((1,), (0,)), ((), ())), "nt": (((1,), (1,)), ((), ())), "tn": (((0,), (0,)), ((), ()))}


def _compact(x):
    pairs = [x[:, 2 * j * SLOT:(2 * j + 1) * SLOT] + pltpu.roll(x[:, (2 * j + 1) * SLOT:(2 * j + 2) * SLOT], HEAD_DIM, 1)
             for j in range(x.shape[1] // (2 * SLOT))]
    return pairs[0] if len(pairs) == 1 else jnp.concatenate(pairs, axis=1)


def _expand(x):
    low = lax.broadcasted_iota(jnp.int32, (x.shape[0], SLOT), 1) < HEAD_DIM
    slots = []
    for j in range(x.shape[1] // SLOT):
        pair = x[:, j * SLOT:(j + 1) * SLOT]
        slots += [jnp.where(low, pair, 0.0), pltpu.roll(jnp.where(low, 0.0, pair), HEAD_DIM, 1)]
    return jnp.concatenate(slots, axis=1)


def _mm_tiles(m, n, k, a_bytes, b_bytes, o_bytes, extra_bytes, tm_fixed, tn_fixed):
    best = None
    for tm in ([tm_fixed] if tm_fixed else [t for t in range(4096, 0, -SLOT) if m % t == 0] or [m]):
        for tn in ([tn_fixed] if tn_fixed else [t for t in range(1024, 0, -SLOT) if n % t == 0] or [n]):
            need = 2 * (tm * k * a_bytes + k * tn * b_bytes + tm * tn * (o_bytes + extra_bytes))
            need += tm * tn * 4
            if need <= MM_VMEM_BUDGET and (best is None or tm * tn > best[0] * best[1]):
                best = (tm, tn)
    assert best is not None, (m, n, k)
    return best


def _mm(a, b, mode, out_dtype, name, res=None, aux=None, epi=None, b_blk=None, o_blk=None, after=None, norm=None,
        pairs=""):
    if b_blk is not None:
        nb, br, bc = b.shape
        b_shape = (nb * br, bc) if b_blk == "rows" else (br, nb * bc)
    else:
        b_shape = b.shape
    assert not pairs or (b_blk is None and o_blk is None and not ("b" in pairs and mode == "nt"))
    a_shape = (a.shape[0], a.shape[1] // 2) if "a" in pairs else a.shape
    if "b" in pairs:
        b_shape = (b_shape[0], b_shape[1] // 2)
    if mode == "nn":
        (m, k), (k2, n) = a_shape, b_shape
    elif mode == "nt":
        (m, k), (n, k2) = a_shape, b_shape
    else:
        (k, m), (k2, n) = a_shape, b_shape
    assert k == k2, (a.shape, b_shape, mode)
    k_blocked = b_blk is not None and (b_blk == "rows") == (mode != "nt")
    tn_fixed = None
    if b_blk is not None and not k_blocked:
        tn_fixed = br if b_blk == "rows" else bc
    if o_blk == "cols":
        tn_fixed = n // N_DEV
    tm_fixed = m // N_DEV if o_blk == "rows" else None
    has_res, has_aux, has_norm = res is not None, aux is not None, epi == "normbwd"
    assert o_blk is None or not (has_res or has_aux or has_norm)
    n_out = 2 if epi == "relu2" else 1
    if has_norm:
        tn_fixed = n
        o_bytes, extra_bytes = 4 + 2, 4 + 4
    else:
        o_bytes = n_out * jnp.dtype(out_dtype).itemsize
        extra_bytes = (4 if has_res else 0) + (aux.dtype.itemsize if has_aux else 0)
    pa, pb, po = (2 if "a" in pairs else 1), (2 if "b" in pairs else 1), (2 if "o" in pairs else 1)
    tm, tn = _mm_tiles(m, n, k, a.dtype.itemsize * (3 if pa == 2 else 1), b.dtype.itemsize * (3 if pb == 2 else 1),
                       o_bytes * po, extra_bytes, tm_fixed, tn_fixed)
    dims = _DIMS[mode]
    if mode == "tn":
        a_spec = pl.BlockSpec((k, pa * tm), lambda i, j: (0, i))
    else:
        a_spec = pl.BlockSpec((tm, pa * k), lambda i, j: (i, 0))
    if b_blk is None:
        if mode == "nt":
            b_spec = pl.BlockSpec((tn, k), lambda i, j: (j, 0))
        else:
            b_spec = pl.BlockSpec((k, pb * tn), lambda i, j: (0, j))
    elif k_blocked and mode == "nt":
        b_spec = pl.BlockSpec((N_DEV, tn, bc), lambda i, j: (0, j, 0))
    elif k_blocked:
        b_spec = pl.BlockSpec((N_DEV, br, tn), lambda i, j: (0, 0, j))
    elif mode == "nt":
        b_spec = pl.BlockSpec((None, tn, k), lambda i, j: (j, 0, 0))
    else:
        b_spec = pl.BlockSpec((None, k, tn), lambda i, j: (j, 0, 0))
    if o_blk is None:
        o_spec = pl.BlockSpec((tm, po * tn), lambda i, j: (i, j))
        o_shape = (m, po * n)
    elif o_blk == "rows":
        o_spec = pl.BlockSpec((None, tm, tn), lambda i, j: (i, 0, j))
        o_shape = (N_DEV, tm, n)
    else:
        o_spec = pl.BlockSpec((None, tm, tn), lambda i, j: (j, i, 0))
        o_shape = (N_DEV, m, tn)

    def body(*refs):
        a_ref, b_ref = refs[0], refs[1]
        pos = 2
        res_ref = aux_ref = None
        if has_res:
            res_ref = refs[pos]
            pos += 1
        if has_aux:
            aux_ref = refs[pos]
            pos += 1
        if has_norm:
            x_ref, g_ref, dres_ref = refs[pos:pos + 3]
            pos += 3
        if after is not None:
            pos += 1
        outs = refs[pos:]
        if k_blocked and mode == "nt":
            r = None
            for d in range(N_DEV):
                part = lax.dot_general(a_ref[:, d * bc:(d + 1) * bc].astype(BF16), b_ref[d].astype(BF16), dims,
                                       preferred_element_type=F32)
                r = part if r is None else r + part
        else:
            bv = b_ref[...].reshape(k, tn) if k_blocked else b_ref[...]
            av = _compact(a_ref[...].astype(F32)) if pa == 2 else a_ref[...]
            bv = _compact(bv.astype(F32)) if pb == 2 else bv
            r = lax.dot_general(av.astype(BF16), bv.astype(BF16), dims, preferred_element_type=F32)
        if po == 2:
            r = _expand(r)
        if epi == "relu2":
            r = jnp.maximum(r, 0.0)
            outs[0][...] = r.astype(outs[0].dtype)
            outs[1][...] = (r * r).astype(outs[1].dtype)
        elif has_norm:
            xv = x_ref[...]
            rs = lax.rsqrt(jnp.mean(xv * xv, axis=1, keepdims=True) + EPS)
            xh = xv * rs
            dxh = r * g_ref[...]
            dx = rs * (dxh - xh * jnp.mean(dxh * xh, axis=1, keepdims=True)) + dres_ref[...]
            outs[0][...] = dx
            outs[1][...] = dx.astype(BF16)

            @pl.when(pl.program_id(0) == 0)
            def _():
                outs[2][...] = jnp.zeros_like(outs[2])

            outs[2][...] += jnp.sum(r * xh, axis=0, keepdims=True)
        else:
            if epi == "mul2aux":
                r = r * (2.0 * aux_ref[...].astype(F32))
            if has_res:
                r = r + res_ref[...]
            outs[0][...] = r.astype(outs[0].dtype)

    in_specs = [a_spec, b_spec]
    args = [a, b]
    if has_res:
        in_specs.append(o_spec)
        args.append(res)
    if has_aux:
        in_specs.append(o_spec)
        args.append(aux)
    vec_spec = pl.BlockSpec((1, n), lambda i, j: (0, 0))
    if has_norm:
        in_specs += [o_spec, vec_spec, o_spec]
        args += [norm[0], norm[1].reshape(1, n), norm[2]]
    if after is not None:
        in_specs.append(pl.BlockSpec(memory_space=pl.ANY))
        args.append(after)
    if has_norm:
        out_specs = [o_spec, o_spec, vec_spec]
        out_shape = [jax.ShapeDtypeStruct(o_shape, F32), jax.ShapeDtypeStruct(o_shape, BF16),
                     jax.ShapeDtypeStruct((1, n), F32)]
    else:
        out_specs = [o_spec] * n_out
        out_shape = [jax.ShapeDtypeStruct(o_shape, out_dtype)] * n_out
    out = pl.pallas_call(
        body, name=name, grid=(m // tm, n // tn),
        in_specs=in_specs, out_specs=out_specs, out_shape=out_shape, compiler_params=_cparams(),
    )(*args)
    return out if len(out) > 1 else out[0]


def _rmsnorm_fwd(xarr, colblk, width, g, name, after=None):
    rows = xarr.shape[0]
    tm = min(TM, rows)

    def body(x_ref, g_ref, *rest):
        y_ref = rest[-1]
        x = x_ref[...].astype(F32)
        r = lax.rsqrt(jnp.mean(x * x, axis=1, keepdims=True) + EPS)
        y_ref[...] = (x * r * g_ref[...]).astype(y_ref.dtype)

    in_specs = [pl.BlockSpec((tm, width), lambda i: (i, colblk)), pl.BlockSpec((1, width), lambda i: (0, 0))]
    args = [xarr, g.reshape(1, width)]
    if after is not None:
        in_specs.append(pl.BlockSpec(memory_space=pl.ANY))
        args.append(after)
    return pl.pallas_call(
        body, name=name, grid=(rows // tm,), in_specs=in_specs,
        out_specs=pl.BlockSpec((tm, width), lambda i: (i, 0)),
        out_shape=jax.ShapeDtypeStruct((rows, width), BF16), compiler_params=_cparams(),
    )(*args)


def _rmsnorm_bwd(xarr, colblk, width, g, dy, dres, out_dtype, name):
    rows = xarr.shape[0]
    tm = min(TM, rows)
    has_res = dres is not None

    def body(*refs):
        x_ref, g_ref, dy_ref = refs[0], refs[1], refs[2]
        dres_ref = refs[3] if has_res else None
        dx_ref, dg_ref = refs[-2], refs[-1]
        x = x_ref[...].astype(F32)
        dyv = dy_ref[...].astype(F32)
        r = lax.rsqrt(jnp.mean(x * x, axis=1, keepdims=True) + EPS)
        xh = x * r
        dxh = dyv * g_ref[...]
        dx = r * (dxh - xh * jnp.mean(dxh * xh, axis=1, keepdims=True))
        if has_res:
            dx = dx + dres_ref[...]
        dx_ref[...] = dx.astype(dx_ref.dtype)

        @pl.when(pl.program_id(0) == 0)
        def _():
            dg_ref[...] = jnp.zeros_like(dg_ref)

        dg_ref[...] += jnp.sum(dyv * xh, axis=0, keepdims=True)

    row_spec = pl.BlockSpec((tm, width), lambda i: (i, 0))
    vec_spec = pl.BlockSpec((1, width), lambda i: (0, 0))
    in_specs = [pl.BlockSpec((tm, width), lambda i: (i, colblk)), vec_spec, row_spec]
    args = [xarr, g.reshape(1, width), dy]
    if has_res:
        in_specs.append(row_spec)
        args.append(dres)
    return pl.pallas_call(
        body, name=name, grid=(rows // tm,), in_specs=in_specs, out_specs=[row_spec, vec_spec],
        out_shape=[jax.ShapeDtypeStruct((rows, width), out_dtype), jax.ShapeDtypeStruct((1, width), F32)],
        compiler_params=_cparams(),
    )(*args)


def _loss_head(x, g, tgt):
    rows, width = x.shape
    tm = min(TM, rows)

    def body(x_ref, g_ref, t_ref, dx_ref, dxb_ref, dg_ref, loss_ref):
        xv = x_ref[...]
        gv = g_ref[...]
        r = lax.rsqrt(jnp.mean(xv * xv, axis=1, keepdims=True) + EPS)
        xh = xv * r
        err = xh * gv - t_ref[...]
        part = 0.5 * jnp.sum(jnp.mean(err * err, axis=1, keepdims=True), axis=0, keepdims=True)
        dyv = err * (1.0 / width)
        dxh = dyv * gv
        dxv = r * (dxh - xh * jnp.mean(dxh * xh, axis=1, keepdims=True))
        dx_ref[...] = dxv
        dxb_ref[...] = dxv.astype(BF16)

        @pl.when(pl.program_id(0) == 0)
        def _():
            dg_ref[...] = jnp.zeros_like(dg_ref)
            loss_ref[...] = jnp.zeros_like(loss_ref)

        dg_ref[...] += jnp.sum(dyv * xh, axis=0, keepdims=True)
        loss_ref[...] += jnp.broadcast_to(part, loss_ref.shape)

    row_spec = pl.BlockSpec((tm, width), lambda i: (i, 0))
    vec_spec = pl.BlockSpec((1, width), lambda i: (0, 0))
    return pl.pallas_call(
        body, name="loss_head", grid=(rows // tm,), in_specs=[row_spec, vec_spec, row_spec],
        out_specs=[row_spec, row_spec, vec_spec, pl.BlockSpec((1, SLOT), lambda i: (0, 0))],
        out_shape=[jax.ShapeDtypeStruct((rows, width), F32), jax.ShapeDtypeStruct((rows, width), BF16),
                   jax.ShapeDtypeStruct((1, width), F32), jax.ShapeDtypeStruct((1, SLOT), F32)],
        compiler_params=_cparams(),
    )(x, g.reshape(1, width), tgt)


def _lane_consts():
    half = 16
    inv = ROPE_THETA ** (-(jnp.arange(half, dtype=F32) * 2.0) / 32)
    lane = jnp.arange(SLOT)
    first = (lane >= 64) & (lane < 80)
    second = (lane >= 80) & (lane < 96)
    inv_lane = jnp.where(first | second, inv[(lane - 64) % half], 0.0)
    rows = [inv_lane, (lane < 64).astype(F32), first.astype(F32), second.astype(F32)]
    rows += [jnp.zeros((SLOT,), F32)] * 4
    return jnp.stack(rows).astype(F32)


def _rope_tables(pos_col, consts):
    rows = pos_col.shape[0]
    tm = min(TM, rows)

    def body(p_ref, k_ref, c_ref, s1_ref, s2_ref):
        ang = p_ref[...] * k_ref[0:1, :]
        cos, sin = jnp.cos(ang), jnp.sin(ang)
        first, second = k_ref[2:3, :], k_ref[3:4, :]
        c_ref[...] = k_ref[1:2, :] + (first + second) * cos
        s1_ref[...] = -first * sin
        s2_ref[...] = second * sin

    spec = pl.BlockSpec((tm, SLOT), lambda i: (i, 0))
    shp = jax.ShapeDtypeStruct((rows, SLOT), F32)
    return pl.pallas_call(
        body, name="rope_tables", grid=(rows // tm,),
        in_specs=[pl.BlockSpec((tm, 1), lambda i: (i, 0)), pl.BlockSpec((8, SLOT), lambda i: (0, 0))],
        out_specs=[spec, spec, spec], out_shape=[shp, shp, shp], compiler_params=_cparams(),
    )(pos_col, consts)


def _rot(xv, c, s1, s2):
    return xv * c + pltpu.roll(xv, SLOT - 16, 1) * s1 + pltpu.roll(xv, 16, 1) * s2


def _rot_t(dy, c, s1, s2):
    return dy * c + pltpu.roll(dy * s1, 16, 1) + pltpu.roll(dy * s2, SLOT - 16, 1)


def _mla_rope_fwd(qraw, kvraw, proj, tabs):
    rows = qraw.shape[0]
    tm = min(256, rows)
    hw = MLA_HEADS * SLOT

    def body(q_ref, kv_ref, kr_ref, c_ref, s1_ref, s2_ref, qo, ko, vo):
        c, s1, s2 = c_ref[...], s1_ref[...], s2_ref[...]
        kr = _rot(kr_ref[...], c, s1, s2)
        for h in range(MLA_HEADS):
            sl = slice(h * SLOT, (h + 1) * SLOT)
            qo[:, sl] = (_rot(q_ref[:, sl], c, s1, s2) * MLA_Q_SCALE).astype(BF16)
            ko[:, sl] = (kv_ref[:, sl] + kr).astype(BF16)
            vo[:, sl] = kv_ref[:, hw + h * SLOT:hw + (h + 1) * SLOT].astype(BF16)

    tab = pl.BlockSpec((tm, SLOT), lambda i: (i, 0))
    wide = pl.BlockSpec((tm, hw), lambda i: (i, 0))
    shp = jax.ShapeDtypeStruct((rows, hw), BF16)
    return pl.pallas_call(
        body, name="mla_rope_fwd", grid=(rows // tm,),
        in_specs=[wide, pl.BlockSpec((tm, 2 * hw), lambda i: (i, 0)), pl.BlockSpec((tm, SLOT), lambda i: (i, 3)),
                  tab, tab, tab],
        out_specs=[wide, wide, wide], out_shape=[shp, shp, shp], compiler_params=_cparams(),
    )(qraw, kvraw, proj, *tabs)


def _mla_rope_bwd(dq, dk, dv, tabs, consts):
    rows = dq.shape[0]
    tm = min(256, rows)
    hw = MLA_HEADS * SLOT

    def body(dq_ref, dk_ref, dv_ref, c_ref, s1_ref, s2_ref, k_ref, dqo, dkvo, dkro):
        c, s1, s2 = c_ref[...], s1_ref[...], s2_ref[...]
        ksum = jnp.zeros((tm, SLOT), F32)
        for h in range(MLA_HEADS):
            sl = slice(h * SLOT, (h + 1) * SLOT)
            dqo[:, sl] = _rot_t(dq_ref[:, sl], c, s1, s2).astype(BF16)
            dkh = dk_ref[:, sl]
            ksum = ksum + dkh
            dkvo[:, sl] = dkh.astype(BF16)
            dkvo[:, hw + h * SLOT:hw + (h + 1) * SLOT] = dv_ref[:, sl].astype(BF16)
        dkro[...] = _rot_t(ksum, c, s1, s2) * (k_ref[2:3, :] + k_ref[3:4, :])

    tab = pl.BlockSpec((tm, SLOT), lambda i: (i, 0))
    wide = pl.BlockSpec((tm, hw), lambda i: (i, 0))
    return pl.pallas_call(
        body, name="mla_rope_bwd", grid=(rows // tm,),
        in_specs=[wide, wide, wide, tab, tab, tab, pl.BlockSpec((8, SLOT), lambda i: (0, 0))],
        out_specs=[wide, pl.BlockSpec((tm, 2 * hw), lambda i: (i, 0)), tab],
        out_shape=[jax.ShapeDtypeStruct((rows, hw), BF16), jax.ShapeDtypeStruct((rows, 2 * hw), BF16),
                   jax.ShapeDtypeStruct((rows, SLOT), F32)],
        compiler_params=_cparams(),
    )(dq, dk, dv, *tabs, consts)


def _nt(a, b):
    return lax.dot_general(a, b, _DIMS["nt"], preferred_element_type=F32)


def _tn(a, b):
    return lax.dot_general(a, b, _DIMS["tn"], preferred_element_type=F32)


def _nn(a, b):
    return lax.dot_general(a, b, _DIMS["nn"], preferred_element_type=F32)


def _causal(rows, keys):
    shp = (rows.stop - rows.start, keys.stop - keys.start)
    return (keys.start + lax.broadcasted_iota(jnp.int32, shp, 1)
            <= rows.start + lax.broadcasted_iota(jnp.int32, shp, 0))


def _mla_attn_fwd(q, k, v):
    rows = q.shape[0]
    t = min(TQ_MLA, rows)
    nt = rows // t
    wide = MLA_PACK * SLOT

    def body(q_ref, k_ref, v_ref, o_ref, lse_ref, m_sc, l_sc, acc_sc):
        i, j = pl.program_id(1), pl.program_id(2)

        @pl.when(j == 0)
        def _():
            m_sc[...] = jnp.full_like(m_sc, NEG)
            l_sc[...] = jnp.zeros_like(l_sc)
            acc_sc[...] = jnp.zeros_like(acc_sc)

        def step(diagonal):
            for hh in range(MLA_PACK):
                sl = slice(hh * SLOT, (hh + 1) * SLOT)
                s = _nt(q_ref[:, sl], k_ref[:, sl])
                if diagonal:
                    s = jnp.where(_causal(slice(0, t), slice(0, t)), s, NEG)
                m_prev = m_sc[hh]
                m_new = jnp.maximum(m_prev, jnp.max(s, axis=1, keepdims=True))
                p = jnp.exp2(s - m_new)
                alpha = jnp.exp2(m_prev - m_new)
                l_new = alpha * l_sc[hh] + jnp.sum(p, axis=1, keepdims=True)
                acc = alpha * acc_sc[:, sl] + _nn(p.astype(BF16), v_ref[:, sl])
                if diagonal:
                    o_ref[:, sl] = (acc / l_new).astype(o_ref.dtype)
                    lse_ref[:, sl] = jnp.broadcast_to(m_new + jnp.log(l_new) * LOG2_E, (t, SLOT))
                else:
                    m_sc[hh] = m_new
                    l_sc[hh] = l_new
                    acc_sc[:, sl] = acc

        @pl.when(j < i)
        def _():
            step(False)

        @pl.when(j == i)
        def _():
            step(True)

    q_spec = pl.BlockSpec((t, wide), lambda h, i, j: (i, h))
    kv_spec = pl.BlockSpec((t, wide), lambda h, i, j: (jnp.minimum(j, i), h))
    return pl.pallas_call(
        body, name="mla_attn_fwd", grid=(MLA_HEADS // MLA_PACK, nt, nt),
        in_specs=[q_spec, kv_spec, kv_spec], out_specs=[q_spec, q_spec],
        out_shape=[jax.ShapeDtypeStruct(q.shape, BF16), jax.ShapeDtypeStruct(q.shape, F32)],
        scratch_shapes=[pltpu.VMEM((MLA_PACK, t, 1), F32), pltpu.VMEM((MLA_PACK, t, 1), F32),
                        pltpu.VMEM((t, wide), F32)],
        compiler_params=_cparams(),
    )(q, k, v)


def _mla_attn_bwd(q, k, v, o, do, lse, after):
    rows = q.shape[0]
    t = min(TQ_MLA, rows)
    nt = rows // t
    wide = MLA_PACK * SLOT

    def body(q_ref, k_ref, v_ref, o_ref, do_ref, lse_ref, after_ref, dq_ref, dk_ref, dv_ref, dk_sc, dv_sc):
        j, i = pl.program_id(1), pl.program_id(2)

        @pl.when((j == 0) & (i == 0))
        def _():
            dq_ref[...] = jnp.zeros_like(dq_ref)

        @pl.when(i == 0)
        def _():
            dk_sc[...] = jnp.zeros_like(dk_sc)
            dv_sc[...] = jnp.zeros_like(dv_sc)

        def chunk(hh, rows, keys, masked):
            sl = slice(hh * SLOT, (hh + 1) * SLOT)
            n_rows = rows.stop - rows.start
            qv, kv, dov = q_ref[rows, sl], k_ref[keys, sl], do_ref[rows, sl]
            s = _nt(qv, kv)
            if masked:
                s = jnp.where(_causal(rows, keys), s, NEG)
            p = jnp.exp2(s - lse_ref[rows, hh * SLOT:hh * SLOT + 1])
            delta = jnp.sum(dov.astype(F32) * o_ref[rows, sl].astype(F32), axis=1, keepdims=True)
            dp = _nt(dov, v_ref[keys, sl])
            ds = (p * (dp - delta)).astype(BF16)
            dv_sc[keys, sl] += _tn(p.astype(BF16), dov)
            dk_sc[keys, sl] += _tn(ds, qv)
            r0 = pl.multiple_of(i * t + rows.start, n_rows)
            dq_ref[pl.ds(r0, n_rows), sl] += _nn(ds, kv) * MLA_SCALE

        @pl.when(i > j)
        def _():
            for hh in range(MLA_PACK):
                chunk(hh, slice(0, t), slice(0, t), False)

        @pl.when(i == j)
        def _():
            for hh in range(MLA_PACK):
                chunk(hh, slice(0, t), slice(0, t // 2), True)
                chunk(hh, slice(t // 2, t), slice(t // 2, t), True)

        @pl.when(i == nt - 1)
        def _():
            dk_ref[...] = dk_sc[...] * (1.0 / LOG2_E)
            dv_ref[...] = dv_sc[...]

    q_spec = pl.BlockSpec((t, wide), lambda h, j, i: (jnp.maximum(i, j), h))
    kv_spec = pl.BlockSpec((t, wide), lambda h, j, i: (j, h))
    head_spec = pl.BlockSpec((rows, wide), lambda h, j, i: (0, h))
    shp = jax.ShapeDtypeStruct(q.shape, F32)
    return pl.pallas_call(
        body, name="mla_attn_bwd", grid=(MLA_HEADS // MLA_PACK, nt, nt),
        in_specs=[q_spec, kv_spec, kv_spec, q_spec, q_spec, q_spec, pl.BlockSpec(memory_space=pl.ANY)],
        out_specs=[head_spec, kv_spec, kv_spec], out_shape=[shp, shp, shp],
        scratch_shapes=[pltpu.VMEM((t, wide), F32), pltpu.VMEM((t, wide), F32)],
        compiler_params=_cparams(),
    )(q, k, v, o, do, lse, after)


def _swa_specs(t):
    def prev(i):
        return jnp.maximum(i - 1, 0)
    kw = SWA_PACK * SLOT
    k0, v0 = SWA_HEADS // SWA_PACK, (SWA_HEADS + SWA_KV_HEADS) // SWA_PACK
    q3 = pl.BlockSpec((t, SWA_PACK * SWA_GROUP * SLOT), lambda h, i: (i, h))
    kp = pl.BlockSpec((t, kw), lambda h, i: (prev(i), k0 + h))
    kc = pl.BlockSpec((t, kw), lambda h, i: (i, k0 + h))
    vp = pl.BlockSpec((t, kw), lambda h, i: (prev(i), v0 + h))
    vc = pl.BlockSpec((t, kw), lambda h, i: (i, v0 + h))
    pcol = pl.BlockSpec((t, 1), lambda h, i: (i, 0))
    prow_p = pl.BlockSpec((1, t), lambda h, i: (0, prev(i)))
    prow_c = pl.BlockSpec((1, t), lambda h, i: (0, i))
    return [q3, kp, kc, vp, vc, pcol, prow_p, prow_c]


def _stack(ref, first):
    return jnp.concatenate([ref[:, (first + g) * SLOT:(first + g + 1) * SLOT] for g in range(SWA_GROUP)], axis=0)


def _swa_logits(q3, kp, kc, pq, pkp, pkc, slope_ref, kvh, i, t):
    r = lax.broadcasted_iota(jnp.int32, (t, t), 0)
    c = lax.broadcasted_iota(jnp.int32, (t, t), 1)
    ok_c = c <= r
    ok_p = (c - r) > jnp.where(i > 0, 0, t)
    dist_p, dist_c = pq - pkp, pq - pkc
    s_p3 = _nt(q3, kp) * (HEAD_DIM ** -0.5)
    s_c3 = _nt(q3, kc) * (HEAD_DIM ** -0.5)
    out = []
    for g in range(SWA_GROUP):
        slope = slope_ref[kvh * SWA_GROUP + g]
        rows = slice(g * t, (g + 1) * t)
        out.append((jnp.where(ok_p, s_p3[rows] - slope * dist_p, NEG),
                    jnp.where(ok_c, s_c3[rows] - slope * dist_c, NEG)))
    return out


def _swa_attn_fwd(proj, pos_col, pos_row, slopes, sinks):
    rows = proj.shape[0]
    t = WINDOW
    hw = SWA_HEADS * SLOT

    def body(slope_ref, sink_ref, q_ref, kp_ref, kc_ref, vp_ref, vc_ref, pq_ref, pkp_ref, pkc_ref, o_ref, lse_ref):
        i = pl.program_id(1)
        for kv in range(SWA_PACK):
            kvh = pl.program_id(0) * SWA_PACK + kv
            ksl = slice(kv * SLOT, (kv + 1) * SLOT)
            logits = _swa_logits(_stack(q_ref, kv * SWA_GROUP), kp_ref[:, ksl], kc_ref[:, ksl], pq_ref[...],
                                 pkp_ref[...], pkc_ref[...], slope_ref, kvh, i, t)
            e_p, e_c, norm = [], [], []
            for g, (s_p, s_c) in enumerate(logits):
                sl = slice((kv * SWA_GROUP + g) * SLOT, (kv * SWA_GROUP + g + 1) * SLOT)
                sink = sink_ref[kvh * SWA_GROUP + g]
                m = jnp.maximum(jnp.maximum(jnp.max(s_p, axis=1, keepdims=True),
                                            jnp.max(s_c, axis=1, keepdims=True)), sink)
                ep, ec = jnp.exp(s_p - m), jnp.exp(s_c - m)
                l = jnp.sum(ep, axis=1, keepdims=True) + jnp.sum(ec, axis=1, keepdims=True) + jnp.exp(sink - m)
                e_p.append(ep.astype(BF16))
                e_c.append(ec.astype(BF16))
                norm.append(l)
                lse_ref[:, sl] = jnp.broadcast_to(m + jnp.log(l), (t, SLOT))
            acc = (_nn(jnp.concatenate(e_p, axis=0), vp_ref[:, ksl])
                   + _nn(jnp.concatenate(e_c, axis=0), vc_ref[:, ksl]))
            for g in range(SWA_GROUP):
                sl = slice((kv * SWA_GROUP + g) * SLOT, (kv * SWA_GROUP + g + 1) * SLOT)
                o_ref[:, sl] = (acc[g * t:(g + 1) * t] / norm[g]).astype(o_ref.dtype)

    smem = pl.BlockSpec(memory_space=pltpu.SMEM)
    out_spec = pl.BlockSpec((t, SWA_PACK * SWA_GROUP * SLOT), lambda h, i: (i, h))
    return pl.pallas_call(
        body, name="swa_attn_fwd", grid=(SWA_KV_HEADS // SWA_PACK, rows // t),
        in_specs=[smem, smem] + _swa_specs(t), out_specs=[out_spec, out_spec],
        out_shape=[jax.ShapeDtypeStruct((rows, hw), BF16), jax.ShapeDtypeStruct((rows, hw), F32)],
        compiler_params=_cparams(),
    )(slopes, sinks, proj, proj, proj, proj, proj, pos_col, pos_row, pos_row)


def _swa_attn_bwd(proj, o, do, lse, pos_col, pos_row, slopes, sinks, after):
    rows = proj.shape[0]
    t = WINDOW
    hw = SWA_HEADS * SLOT
    scale = HEAD_DIM ** -0.5

    def body(slope_ref, sink_ref, q_ref, kp_ref, kc_ref, vp_ref, vc_ref, pq_ref, pkp_ref, pkc_ref,
             o_ref, do_ref, lse_ref, after_ref, dq_ref, dk_ref, dv_ref, dsink_ref):
        i = pl.program_id(1)

        @pl.when(i == 0)
        def _():
            dk_ref[...] = jnp.zeros_like(dk_ref)
            dv_ref[...] = jnp.zeros_like(dv_ref)
            dsink_ref[...] = jnp.zeros_like(dsink_ref)

        r_c = pl.multiple_of(i * t, t)
        r_p = pl.multiple_of(jnp.maximum(i - 1, 0) * t, t)
        for kv in range(SWA_PACK):
            kvh = pl.program_id(0) * SWA_PACK + kv
            ksl = slice(kv * SLOT, (kv + 1) * SLOT)
            q3, do3 = _stack(q_ref, kv * SWA_GROUP), _stack(do_ref, kv * SWA_GROUP)
            logits = _swa_logits(q3, kp_ref[:, ksl], kc_ref[:, ksl], pq_ref[...], pkp_ref[...], pkc_ref[...],
                                 slope_ref, kvh, i, t)
            dp_p3, dp_c3 = _nt(do3, vp_ref[:, ksl]), _nt(do3, vc_ref[:, ksl])
            p_p, p_c, ds_p, ds_c = [], [], [], []
            for g, (s_p, s_c) in enumerate(logits):
                head = kv * SWA_GROUP + g
                sl = slice(head * SLOT, (head + 1) * SLOT)
                rws = slice(g * t, (g + 1) * t)
                lse_g = lse_ref[:, head * SLOT:head * SLOT + 1]
                pp, pc = jnp.exp(s_p - lse_g), jnp.exp(s_c - lse_g)
                delta = jnp.sum(do_ref[:, sl].astype(F32) * o_ref[:, sl].astype(F32), axis=1, keepdims=True)
                p_p.append(pp.astype(BF16))
                p_c.append(pc.astype(BF16))
                ds_p.append((pp * (dp_p3[rws] - delta)).astype(BF16))
                ds_c.append((pc * (dp_c3[rws] - delta)).astype(BF16))
                sink = sink_ref[kvh * SWA_GROUP + g]
                dsink = -jnp.sum(jnp.exp(sink - lse_g) * delta, axis=0, keepdims=True)
                dsink_ref[head * 8:(head + 1) * 8, :] += jnp.broadcast_to(dsink, (8, SLOT))
            p_p3, p_c3 = jnp.concatenate(p_p, axis=0), jnp.concatenate(p_c, axis=0)
            ds_p3, ds_c3 = jnp.concatenate(ds_p, axis=0), jnp.concatenate(ds_c, axis=0)
            dq3 = (_nn(ds_p3, kp_ref[:, ksl]) + _nn(ds_c3, kc_ref[:, ksl])) * scale
            for g in range(SWA_GROUP):
                head = kv * SWA_GROUP + g
                dq_ref[:, head * SLOT:(head + 1) * SLOT] = dq3[g * t:(g + 1) * t]
            dk_ref[pl.ds(r_c, t), ksl] += _tn(ds_c3, q3) * scale
            dv_ref[pl.ds(r_c, t), ksl] += _tn(p_c3, do3)
            dk_ref[pl.ds(r_p, t), ksl] += _tn(ds_p3, q3) * scale
            dv_ref[pl.ds(r_p, t), ksl] += _tn(p_p3, do3)

    smem = pl.BlockSpec(memory_space=pltpu.SMEM)
    qlike = pl.BlockSpec((t, SWA_PACK * SWA_GROUP * SLOT), lambda h, i: (i, h))
    kv_out = pl.BlockSpec((rows, SWA_PACK * SLOT), lambda h, i: (0, h))
    return pl.pallas_call(
        body, name="swa_attn_bwd", grid=(SWA_KV_HEADS // SWA_PACK, rows // t),
        in_specs=[smem, smem] + _swa_specs(t) + [qlike, qlike, qlike, pl.BlockSpec(memory_space=pl.ANY)],
        out_specs=[qlike, kv_out, kv_out,
                   pl.BlockSpec((SWA_PACK * SWA_GROUP * 8, SLOT), lambda h, i: (h, 0))],
        out_shape=[jax.ShapeDtypeStruct((rows, hw), F32), jax.ShapeDtypeStruct((rows, SWA_KV_HEADS * SLOT), F32),
                   jax.ShapeDtypeStruct((rows, SWA_KV_HEADS * SLOT), F32),
                   jax.ShapeDtypeStruct((SWA_HEADS * 8, SLOT), F32)],
        compiler_params=_cparams(),
    )(slopes, sinks, proj, proj, proj, proj, proj, pos_col, pos_row, pos_row, o, do, lse, after)


def _cross_attn_fwd(proj, qoff, kvmem):
    rows = proj.shape[0]
    t = min(TQ_CROSS, rows)

    def body(q_ref, k_ref, v_ref, o_ref):
        s = _nt(q_ref[...].astype(BF16), k_ref[...]) * (HEAD_DIM ** -0.5)
        e = jnp.exp(s - jnp.max(s, axis=1, keepdims=True))
        p = e / jnp.sum(e, axis=1, keepdims=True)
        o_ref[...] = _nn(p.astype(BF16), v_ref[...]).astype(o_ref.dtype)

    return pl.pallas_call(
        body, name="cross_attn_fwd", grid=(rows // t, MEM_HEADS),
        in_specs=[pl.BlockSpec((t, SLOT), lambda i, h: (i, qoff + h)),
                  pl.BlockSpec((N_MEM, SLOT), lambda i, h: (0, h)),
                  pl.BlockSpec((N_MEM, SLOT), lambda i, h: (0, MEM_HEADS + h))],
        out_specs=pl.BlockSpec((t, SLOT), lambda i, h: (i, h)),
        out_shape=jax.ShapeDtypeStruct((rows, MEM_HEADS * SLOT), BF16), compiler_params=_cparams(),
    )(proj, kvmem, kvmem)


def _cross_attn_bwd(proj, qoff, kvmem, do):
    rows = proj.shape[0]
    t = min(TQ_CROSS, rows)
    scale = HEAD_DIM ** -0.5

    def body(q_ref, k_ref, v_ref, do_ref, dq_ref, dk_ref, dv_ref):
        @pl.when(pl.program_id(1) == 0)
        def _():
            dk_ref[...] = jnp.zeros_like(dk_ref)
            dv_ref[...] = jnp.zeros_like(dv_ref)

        qv, kv, dov = q_ref[...].astype(BF16), k_ref[...], do_ref[...]
        s = _nt(qv, kv) * scale
        e = jnp.exp(s - jnp.max(s, axis=1, keepdims=True))
        p = e / jnp.sum(e, axis=1, keepdims=True)
        dp = _nt(dov, v_ref[...])
        ds = (p * (dp - jnp.sum(p * dp, axis=1, keepdims=True))).astype(BF16)
        dq_ref[...] = _nn(ds, kv) * scale
        dk_ref[...] += _tn(ds, qv) * scale
        dv_ref[...] += _tn(p.astype(BF16), dov)

    mem_out = pl.BlockSpec((N_MEM, SLOT), lambda h, i: (0, h))
    return pl.pallas_call(
        body, name="cross_attn_bwd", grid=(MEM_HEADS, rows // t),
        in_specs=[pl.BlockSpec((t, SLOT), lambda h, i: (i, qoff + h)),
                  pl.BlockSpec((N_MEM, SLOT), lambda h, i: (0, h)),
                  pl.BlockSpec((N_MEM, SLOT), lambda h, i: (0, MEM_HEADS + h)),
                  pl.BlockSpec((t, SLOT), lambda h, i: (i, h))],
        out_specs=[pl.BlockSpec((t, SLOT), lambda h, i: (i, h)), mem_out, mem_out],
        out_shape=[jax.ShapeDtypeStruct((rows, MEM_HEADS * SLOT), F32),
                   jax.ShapeDtypeStruct((N_MEM, MEM_HEADS * SLOT), F32),
                   jax.ShapeDtypeStruct((N_MEM, MEM_HEADS * SLOT), F32)],
        compiler_params=_cparams(),
    )(proj, kvmem, kvmem, do)


def _place():
    return lax.axis_index("x"), lax.axis_index("y"), lax.axis_index("c")


def _flip(v, bit):
    return 1 - v if bit else v


def _all_gather(blocks, name):
    nb = len(blocks)

    def body(*refs):
        x_refs, out_refs = refs[:nb], refs[nb:2 * nb]
        send_sems, recv_sems, local_sems = refs[2 * nb:]
        x, y, c = _place()
        me, sibling = (x, y, c), (x, y, 1 - c)
        chips = [(1 - x, y), (x, 1 - y), (1 - x, 1 - y)]

        def copy(b, k, blk, to, from_input=False):
            slot = out_refs[b].at[4 * blk[0] + 2 * blk[1] + blk[2]]
            return pltpu.make_async_remote_copy(
                src_ref=x_refs[b] if from_input else slot, dst_ref=slot,
                send_sem=send_sems.at[b, k], recv_sem=recv_sems.at[b, k],
                device_id=to, device_id_type=pl.DeviceIdType.MESH)

        mine = [pltpu.make_async_copy(x_refs[b], out_refs[b].at[4 * x + 2 * y + c], local_sems.at[b])
                for b in range(nb)]
        for cp in mine:
            cp.start()
        first = []
        for b in range(nb):
            first.append(copy(b, 0, me, sibling, from_input=True))
            first += [copy(b, 1 + n, me, (*chip, c), from_input=True) for n, chip in enumerate(chips)]
        for cp in first:
            cp.start()
        passed = []
        for n, chip in enumerate(chips):
            for b in range(nb):
                copy(b, 1 + n, (*chip, c), me).wait_recv()
                passed.append(copy(b, 4 + n, (*chip, c), sibling))
                passed[-1].start()
        for b in range(nb):
            copy(b, 0, sibling, me).wait_recv()
            for n, chip in enumerate(chips):
                copy(b, 4 + n, (*chip, 1 - c), me).wait_recv()
        for cp in first + passed:
            cp.wait_send()
        for cp in mine:
            cp.wait()

    any_spec = pl.BlockSpec(memory_space=pl.ANY)
    return pl.pallas_call(
        body, name=name, in_specs=[any_spec] * nb, out_specs=[any_spec] * nb,
        out_shape=[jax.ShapeDtypeStruct((N_DEV,) + blk.shape, blk.dtype) for blk in blocks],
        scratch_shapes=[pltpu.SemaphoreType.DMA((nb, 7)), pltpu.SemaphoreType.DMA((nb, 7)),
                        pltpu.SemaphoreType.DMA((nb,))],
    )(*blocks)


def _peers(x, y, c):
    out = []
    for n in range(1, N_DEV):
        peer = (_flip(x, n & 4), _flip(y, n & 2), _flip(c, n & 1))
        out.append((n - 1, peer, 4 * peer[0] + 2 * peer[1] + peer[2]))
    return out


_HBM = pl.BlockSpec(memory_space=pltpu.HBM)
_SEM = pl.BlockSpec(memory_space=pltpu.SEMAPHORE)


def _exchange_start(srcs, scatter, name, after=None):
    ns = len(srcs)
    lands = [lax.empty(s.shape if scatter else (N_DEV,) + s.shape, s.dtype) for s in srcs]

    def body(*refs):
        src_refs, land_refs = refs[:ns], refs[ns:2 * ns]
        pos = 2 * ns + (1 if after is not None else 0)
        send_sems, recv_sems, token = refs[pos], refs[pos + 1], refs[-1]
        x, y, c = _place()
        my_idx = 4 * x + 2 * y + c
        for col, peer, peer_idx in _peers(x, y, c):
            for b in range(ns):
                pltpu.make_async_remote_copy(
                    src_ref=src_refs[b].at[peer_idx] if scatter else src_refs[b], dst_ref=land_refs[b].at[my_idx],
                    send_sem=send_sems.at[b * (N_DEV - 1) + col], recv_sem=recv_sems.at[b * (N_DEV - 1) + col],
                    device_id=peer, device_id_type=pl.DeviceIdType.MESH).start()
        token[...] = jnp.zeros_like(token)

    args = [pltpu.with_memory_space_constraint(a, pltpu.HBM) for a in list(srcs) + lands]
    in_specs = [_HBM] * (2 * ns)
    if after is not None:
        args.append(after)
        in_specs.append(pl.BlockSpec(memory_space=pl.ANY))
    out = pl.pallas_call(
        body, name=name, in_specs=in_specs,
        out_specs=[_SEM, _SEM] + [_HBM] * (2 * ns) + [pl.BlockSpec(memory_space=pltpu.VMEM)],
        out_shape=[pltpu.SemaphoreType.DMA((ns * (N_DEV - 1),)), pltpu.SemaphoreType.DMA((ns * (N_DEV - 1),))]
        + [pltpu.HBM(a.shape, a.dtype) for a in list(srcs) + lands] + [jax.ShapeDtypeStruct((8, SLOT), F32)],
        input_output_aliases={k: 2 + k for k in range(2 * ns)},
        compiler_params=pltpu.CompilerParams(has_side_effects=pltpu.SideEffectType.DATAFLOW_SIDE_EFFECTING),
    )(*args)
    return (out[0], out[1], out[2:2 + ns], out[2 + ns:2 + 2 * ns], scatter), out[-1]


def _exchange_wait(handle, after, name):
    send_sems, recv_sems, srcs, lands, scatter = handle
    ns = len(srcs)

    def body(*refs):
        src_refs, land_refs = refs[:ns], refs[ns:2 * ns]
        send_ref, recv_ref = refs[2 * ns], refs[2 * ns + 1]
        x, y, c = _place()
        for col, peer, peer_idx in _peers(x, y, c):
            for b in range(ns):
                copy = pltpu.make_async_remote_copy(
                    src_ref=src_refs[b].at[peer_idx] if scatter else src_refs[b], dst_ref=land_refs[b].at[peer_idx],
                    send_sem=send_ref.at[b * (N_DEV - 1) + col], recv_sem=recv_ref.at[b * (N_DEV - 1) + col],
                    device_id=peer, device_id_type=pl.DeviceIdType.MESH)
                copy.wait_send()
                copy.wait_recv()

    out = pl.pallas_call(
        body, name=name, in_specs=[_HBM] * (2 * ns) + [_SEM, _SEM, pl.BlockSpec(memory_space=pl.ANY)],
        out_specs=[_HBM] * (2 * ns),
        out_shape=[pltpu.HBM(a.shape, a.dtype) for a in list(srcs) + list(lands)],
        input_output_aliases={k: k for k in range(2 * ns)},
        compiler_params=pltpu.CompilerParams(has_side_effects=pltpu.SideEffectType.DATAFLOW_SIDE_EFFECTING),
    )(*srcs, *lands, send_sems, recv_sems, after)
    my_idx = 4 * lax.axis_index("x") + 2 * lax.axis_index("y") + lax.axis_index("c")
    landed = []
    for src, land in zip(out[:ns], out[ns:]):
        own = lax.dynamic_index_in_dim(src, my_idx, 0, keepdims=True) if scatter else src[None]
        landed.append(lax.dynamic_update_index_in_dim(land, own, my_idx, 0))
    return landed


def _adamw(parts, w, m, v, name):
    lyr, rows, cols = w.shape
    assert len(parts) == lyr
    tr = ADAM_ROWS if cols > 512 else 2 * ADAM_ROWS
    while rows % tr:
        tr //= 2
    tr = min(tr, rows)

    def body(*refs):
        p_refs = refs[:lyr]
        w_ref, m_ref, v_ref, g_out, d_out, m_out, v_out = refs[lyr:]
        for k in range(lyr):
            @pl.when(pl.program_id(0) == k)
            def _(p_ref=p_refs[k]):
                g = p_ref[0].astype(F32)
                for s in range(1, N_DEV):
                    g = g + p_ref[s].astype(F32)
                m2 = ADAM_B1 * m_ref[...] + (1.0 - ADAM_B1) * g
                v2 = ADAM_B2 * v_ref[...] + (1.0 - ADAM_B2) * (g * g)
                m_hat = m2 / (1.0 - ADAM_B1 ** ADAM_STEP)
                v_hat = v2 / (1.0 - ADAM_B2 ** ADAM_STEP)
                g_out[...] = g
                d_out[...] = -ADAM_LR * (m_hat / (jnp.sqrt(v_hat) + ADAM_EPS) + ADAM_WD * w_ref[...])
                m_out[...] = m2
                v_out[...] = v2

    def part_spec(k):
        return pl.BlockSpec((N_DEV, tr, cols), lambda l, i: (0, jnp.where(l == k, i, 0), 0))

    spec = pl.BlockSpec((None, tr, cols), lambda l, i: (l, i, 0))
    shp = jax.ShapeDtypeStruct((lyr, rows, cols), F32)
    return pl.pallas_call(
        body, name=name, grid=(lyr, rows // tr),
        in_specs=[part_spec(k) for k in range(lyr)] + [spec, spec, spec],
        out_specs=[spec] * 4, out_shape=[shp] * 4, compiler_params=_cparams(),
    )(*parts, w, m, v)


def _pack(arrays, lanes, row_mult, dtype):
    flat = jnp.concatenate([a.reshape(-1).astype(dtype) for a in arrays])
    unit = lanes * row_mult
    total = -(-flat.shape[0] // unit) * unit
    return jnp.pad(flat, (0, total - flat.shape[0])).reshape(total // lanes, lanes)


def _unpack(packed, shapes):
    flat = packed.reshape(-1)
    out, off = [], 0
    for shp in shapes:
        n = 1
        for d in shp:
            n *= d
        out.append(flat[off:off + n].reshape(shp))
        off += n
    return out


def _pad_slots(w, axis):
    axis = axis % w.ndim
    n = w.shape[axis] // HEAD_DIM
    shp = w.shape[:axis] + (n, HEAD_DIM) + w.shape[axis + 1:]
    pad = [(0, 0)] * (w.ndim + 1)
    pad[axis + 1] = (0, SLOT - HEAD_DIM)
    return jnp.pad(w.reshape(shp), pad).reshape(w.shape[:axis] + (n * SLOT,) + w.shape[axis + 1:])


def _unpad_slots(w, axis, keep=HEAD_DIM):
    axis = axis % w.ndim
    n = w.shape[axis] // SLOT
    shp = w.shape[:axis] + (n, SLOT) + w.shape[axis + 1:]
    idx = [slice(None)] * (w.ndim + 1)
    idx[axis + 1] = slice(0, keep)
    return w.reshape(shp)[tuple(idx)].reshape(w.shape[:axis] + (n * keep,) + w.shape[axis + 1:])


def _mla_in_pad(w):
    z = functools.partial(jnp.zeros, dtype=w.dtype)
    rows = w.shape[0]
    return jnp.concatenate([w[:, :384], z((rows, 64)), w[:, 640:672], z((rows, 32)), w[:, 384:640],
                            _pad_slots(w[:, 672:], 1)], axis=1)


def _mla_in_unpad(d):
    return jnp.concatenate([d[:, :384], d[:, 512:768], d[:, 448:480], _unpad_slots(d[:, 768:], 1)], axis=1)


def _mla_uq_pad(w):
    return jnp.pad(w.reshape(w.shape[0], MLA_HEADS, MLA_QK), ((0, 0), (0, 0), (0, SLOT - MLA_QK))).reshape(
        w.shape[0], MLA_HEADS * SLOT)


def _mla_ukv_pad(w):
    w3 = w.reshape(w.shape[0], MLA_HEADS, 2 * HEAD_DIM)
    pad = ((0, 0), (0, 0), (0, SLOT - HEAD_DIM))
    k = jnp.pad(w3[:, :, :HEAD_DIM], pad).reshape(w.shape[0], -1)
    v = jnp.pad(w3[:, :, HEAD_DIM:], pad).reshape(w.shape[0], -1)
    return jnp.concatenate([k, v], axis=1)


def _mla_ukv_unpad(d):
    hw = MLA_HEADS * SLOT
    k = d[:, :hw].reshape(d.shape[0], MLA_HEADS, SLOT)[:, :, :HEAD_DIM]
    v = d[:, hw:].reshape(d.shape[0], MLA_HEADS, SLOT)[:, :, :HEAD_DIM]
    return jnp.concatenate([k, v], axis=2).reshape(d.shape[0], MLA_HEADS * 2 * HEAD_DIM)


def _join(gathered, axis):
    nd, a, b = gathered.shape
    if axis == 1:
        return gathered.reshape(nd * a, b)
    return gathered.transpose(1, 0, 2).reshape(a, nd * b)


def _split(full, axis):
    r, c = full.shape
    if axis == 1:
        return full.reshape(N_DEV, r // N_DEV, c).astype(BF16)
    return full.reshape(r, N_DEV, c // N_DEV).transpose(1, 0, 2).astype(BF16)


def kernel(x, mem, positions, attn_norm_g, mlp_norm_g, mem_norm_g, final_norm_g, mla_w_in, mla_q_norm_g, mla_kv_norm_g, mla_w_uq, mla_w_ukv, swa_w_in, swa_sinks, w_mem_kv, w_o, mlp_w_up, mlp_w_down, loss_target, m_attn_norm_g, m_mlp_norm_g, m_mem_norm_g, m_final_norm_g, m_mla_w_in, m_mla_q_norm_g, m_mla_kv_norm_g, m_mla_w_uq, m_mla_w_ukv, m_swa_w_in, m_swa_sinks, m_w_mem_kv, m_w_o, m_mlp_w_up, m_mlp_w_down, v_attn_norm_g, v_mlp_norm_g, v_mem_norm_g, v_final_norm_g, v_mla_w_in, v_mla_q_norm_g, v_mla_kv_norm_g, v_mla_w_uq, v_mla_w_ukv, v_swa_w_in, v_swa_sinks, v_w_mem_kv, v_w_o, v_mlp_w_up, v_mlp_w_down):
    given = dict(locals())
    seq = x.shape[1]
    x0 = x.reshape(seq, D_MODEL)
    tgt = loss_target.reshape(seq, D_MODEL)
    mem0 = mem.reshape(N_MEM, D_MODEL)
    pos = positions.reshape(seq).astype(F32)
    pos_col, pos_row = pos.reshape(seq, 1), pos.reshape(1, seq)

    def layer_names(i):
        mixer = ("mla_w_in", "mla_w_uq", "mla_w_ukv") if i % 2 == 0 else ("swa_w_in",)
        return [(n, i // 2) for n in mixer] + [(n, i) for n in ("w_mem_kv", "w_o", "mlp_w_up", "mlp_w_down")]

    def local_weights(names):
        return [given[n][l].astype(BF16) for n, l in names]

    first_attn, first_mlp = layer_names(0)[:-2], layer_names(0)[-2:]
    weights = [dict(zip([n for n, _ in first_attn], _all_gather(local_weights(first_attn), "gather_weights_first")))]
    coming_mlp, first_token = _exchange_start(local_weights(first_mlp), False, "gather_weights_start_0",
                                              after=weights[0]["w_o"])

    consts = _lane_consts()
    tabs = _rope_tables(pos_col, consts)
    slopes = 2.0 ** (-8.0 * (jnp.arange(SWA_HEADS, dtype=F32) + 1.0) / SWA_HEADS)

    mem_n = _rmsnorm_fwd(mem0, 0, D_MODEL, mem_norm_g, "rmsnorm_fwd_mem")

    saved = []
    xc = x0
    for i in range(DEPTH):
        j = i // 2
        wts = weights[i]
        s = {"x_in": xc}
        token = None
        if i + 1 < DEPTH:
            coming, token = _exchange_start(local_weights(layer_names(i + 1)), False,
                                            "gather_weights_start_%d" % (i + 1),
                                            after=first_token if i == 0 else wts["w_o"])
        hn = _rmsnorm_fwd(xc, 0, D_MODEL, attn_norm_g[i], "rmsnorm_fwd", after=token)
        if i % 2 == 0:
            w_in = _mla_in_pad(_join(wts["mla_w_in"], 1))
            w_uq = _mla_uq_pad(_join(wts["mla_w_uq"], 2))
            w_kv = _mla_ukv_pad(_join(wts["mla_w_ukv"], 2))
            proj = _mm(hn, w_in, "nn", F32, "mm_mla_in")
            cqn = _rmsnorm_fwd(proj, 0, MLA_Q_RANK, mla_q_norm_g[j], "rmsnorm_fwd_q")
            ckvn = _rmsnorm_fwd(proj, 2, MLA_KV_RANK, mla_kv_norm_g[j], "rmsnorm_fwd_kv")
            qraw = _mm(cqn, w_uq, "nn", F32, "mm_mla_uq")
            kvraw = _mm(ckvn, w_kv, "nn", F32, "mm_mla_ukv")
            q, k, v = _mla_rope_fwd(qraw, kvraw, proj, tabs)
            o, lse = _mla_attn_fwd(q, k, v)
            qoff = MLA_QOFF
            s.update(w_uq=w_uq, w_kv=w_kv, cqn=cqn, ckvn=ckvn, q=q, k=k, v=v)
        else:
            w_in = _join(wts["swa_w_in"], 2)
            proj = _mm(hn, w_in, "nn", BF16, "mm_swa_in", pairs="o")
            o, lse = _swa_attn_fwd(proj, pos_col, pos_row, slopes, swa_sinks[j])
            qoff = SWA_QOFF
        w_mem = _pad_slots(_join(wts["w_mem_kv"], 1), 1)
        w_out = _join(wts["w_o"], 1)
        w_o_mix, w_o_cross = w_out[:SWA_HEADS * HEAD_DIM], w_out[SWA_HEADS * HEAD_DIM:]
        kvmem = _mm(mem_n, w_mem, "nn", BF16, "mm_mem_kv")
        cross = _cross_attn_fwd(proj, qoff, kvmem)
        x1 = _mm(o, w_o_mix, "nn", F32, "mm_o_mix", res=xc, pairs="a")
        x1 = _mm(cross, w_o_cross, "nn", F32, "mm_o_cross", res=x1, pairs="a")
        hn2 = _rmsnorm_fwd(x1, 0, D_MODEL, mlp_norm_g[i], "rmsnorm_fwd")
        if i == 0:
            wts.update(zip([n for n, _ in first_mlp], _exchange_wait(coming_mlp, hn2, "gather_weights_wait_0")))
        act, act2 = _mm(hn2, wts["mlp_w_up"], "nn", BF16, "mm_mlp_up", epi="relu2", b_blk="cols")
        xc = _mm(act2, wts["mlp_w_down"], "nn", F32, "mm_mlp_down", res=x1, b_blk="rows")
        s.update(hn=hn, w_in=w_in, proj=proj, o=o, lse=lse, qoff=qoff, w_mem=w_mem, w_o_mix=w_o_mix,
                 w_o_cross=w_o_cross, kvmem=kvmem, cross=cross, x1=x1, hn2=hn2, act=act, act2=act2)
        saved.append(s)
        if i + 1 < DEPTH:
            got = _exchange_wait(coming, xc, "gather_weights_wait_%d" % (i + 1))
            weights.append(dict(zip([n for n, _ in layer_names(i + 1)], got)))

    dx, dx_b, dg_final, loss_part = _loss_head(xc, final_norm_g, tgt)
    loss = lax.psum(loss_part[0, 0], MESH_AXES)

    gains = {n: [None] * DEPTH for n in ("attn_norm_g", "mlp_norm_g")}
    for n in ("mla_q_norm_g", "mla_kv_norm_g", "swa_sinks"):
        gains[n] = [None] * 2
    leaving = {}
    token = None
    dmem_n = None
    for i in reversed(range(DEPTH)):
        j = i // 2
        s = saved[i]
        wts = weights[i]
        out = {}
        du = _mm(dx_b, wts["mlp_w_down"], "nt", BF16, "mm_mlp_down_dx", aux=s["act"], epi="mul2aux", b_blk="rows",
                 after=token)
        out["mlp_w_down"] = _mm(s["act2"], dx_b, "tn", BF16, "mm_mlp_down_dw", o_blk="rows")
        out["mlp_w_up"] = _mm(s["hn2"], du, "tn", BF16, "mm_mlp_up_dw", o_blk="cols")
        dx1, dx1_b, dg = _mm(du, wts["mlp_w_up"], "nt", F32, "mm_mlp_up_dx", b_blk="cols",
                             epi="normbwd", norm=(s["x1"], mlp_norm_g[i], dx))
        gains["mlp_norm_g"][i] = dg[0]

        do = _mm(dx1_b, s["w_o_mix"], "nt", BF16, "mm_o_mix_dx", pairs="o")
        dcross = _mm(dx1_b, s["w_o_cross"], "nt", BF16, "mm_o_cross_dx", pairs="o")
        dw_o = jnp.concatenate([_mm(s["o"], dx1_b, "tn", F32, "mm_o_mix_dw", pairs="a"),
                                _mm(s["cross"], dx1_b, "tn", F32, "mm_o_cross_dw", pairs="a")], axis=0)
        out["w_o"] = _split(dw_o, 1)
        dqc, dkm, dvm = _cross_attn_bwd(s["proj"], s["qoff"], s["kvmem"], dcross)
        dkvmem = jnp.concatenate([dkm, dvm], axis=1).astype(BF16)
        out["w_mem_kv"] = _split(_unpad_slots(_mm(mem_n, dkvmem, "tn", F32, "mm_mem_kv_dw"), 1), 1)
        dmem_n = _mm(dkvmem, s["w_mem"], "nt", F32, "mm_mem_kv_dx" if dmem_n is None else "mm_mem_kv_dx_acc",
                     res=dmem_n)
        leaving[(i, "main")], token = _exchange_start([out[n] for n, _ in layer_names(i)[-4:]], True,
                                                      "exchange_grads_main_start_%d" % i)

        if i % 2 == 0:
            dq, dk, dv = _mla_attn_bwd(s["q"], s["k"], s["v"], s["o"], do, s["lse"], token)
            dqraw, dkv, dkr = _mla_rope_bwd(dq, dk, dv, tabs, consts)
            dcqn = _mm(dqraw, s["w_uq"], "nt", F32, "mm_mla_uq_dx")
            out["mla_w_uq"] = _split(_unpad_slots(_mm(s["cqn"], dqraw, "tn", F32, "mm_mla_uq_dw"), 1, MLA_QK), 2)
            dckvn = _mm(dkv, s["w_kv"], "nt", F32, "mm_mla_ukv_dx")
            out["mla_w_ukv"] = _split(_mla_ukv_unpad(_mm(s["ckvn"], dkv, "tn", F32, "mm_mla_ukv_dw")), 2)
            dcq, dg = _rmsnorm_bwd(s["proj"], 0, MLA_Q_RANK, mla_q_norm_g[j], dcqn, None, BF16, "rmsnorm_bwd_q")
            gains["mla_q_norm_g"][j] = dg[0]
            dckv, dg = _rmsnorm_bwd(s["proj"], 2, MLA_KV_RANK, mla_kv_norm_g[j], dckvn, None, BF16, "rmsnorm_bwd_kv")
            gains["mla_kv_norm_g"][j] = dg[0]
            dproj = jnp.concatenate([dcq, dkr.astype(BF16), dckv, dqc.astype(BF16)], axis=1)
            in_dx = "mm_mla_in_dx"
            out["mla_w_in"] = _split(_mla_in_unpad(_mm(s["hn"], dproj, "tn", F32, "mm_mla_in_dw")), 1)
        else:
            dq, dk, dv, dsink = _swa_attn_bwd(s["proj"], s["o"], do, s["lse"], pos_col, pos_row, slopes, swa_sinks[j],
                                              token)
            gains["swa_sinks"][j] = dsink[::8, 0]
            dproj = jnp.concatenate([dq, dk, dv, dqc], axis=1).astype(BF16)
            in_dx = "mm_swa_in_dx"
            out["swa_w_in"] = _split(_mm(s["hn"], dproj, "tn", F32, "mm_swa_in_dw", pairs="b"), 2)
        dx, dx_b, dg = _mm(dproj, s["w_in"], "nt", F32, in_dx, epi="normbwd", norm=(s["x_in"], attn_norm_g[i], dx1),
                           pairs="" if i % 2 == 0 else "a")
        gains["attn_norm_g"][i] = dg[0]

        leaving[(i, "mixer")], token = _exchange_start([out[n] for n, _ in layer_names(i)[:-4]], True,
                                                       "exchange_grads_mixer_start_%d" % i)

    _, dg_mem = _rmsnorm_bwd(mem0, 0, D_MODEL, mem_norm_g, dmem_n, None, BF16, "rmsnorm_bwd_mem")
    gains = {n: jnp.stack(g) for n, g in gains.items()}
    gains["mem_norm_g"] = dg_mem[0]
    gains["final_norm_g"] = dg_final[0]

    result = {}

    def adamw_of(names, received):
        for n in names:
            parts = [received[(n, l)] for l in range(given[n].shape[0])]
            for kind, r in enumerate(_adamw(parts, given[n], given["m_" + n], given["v_" + n], "adamw_" + n)):
                result[(kind, n)] = r

    received = {}
    for i in reversed(range(DEPTH)):
        got = _exchange_wait(leaving[(i, "main")], dx, "exchange_grads_main_wait_%d" % i)
        received.update(zip(layer_names(i)[-4:], got))
    adamw_of(("mlp_w_up", "mlp_w_down", "w_o", "w_mem_kv"), received)
    for i in reversed(range(DEPTH)):
        got = _exchange_wait(leaving[(i, "mixer")], result[(0, "w_mem_kv")], "exchange_grads_mixer_wait_%d" % i)
        received.update(zip(layer_names(i)[:-4], got))
    adamw_of(("mla_w_in", "mla_w_uq", "mla_w_ukv", "swa_w_in"), received)

    rep_shapes = [given[n].shape for n in REPLICATED]
    rep_parts = _all_gather([_pack([gains[n] for n in REPLICATED], SLOT, 8, F32)], "gather_gain_grads")[0]
    rep_packed = [_pack([given[p + n] for n in REPLICATED], SLOT, 8, F32)[None] for p in ("", "m_", "v_")]
    for kind, r in enumerate(_adamw([rep_parts], *rep_packed, "adamw_gains")):
        for n, part in zip(REPLICATED, _unpack(r[0], rep_shapes)):
            result[(kind, n)] = part

    outs = [loss, dx.reshape(1, seq, D_MODEL)]
    for kind in range(4):
        outs += [result[(kind, n)] for n in WEIGHT_ORDER]
    return tuple(outs)
```

```python
import functools

import jax
import jax.numpy as jnp
from jax import lax
from jax.experimental import pallas as pl
from jax.experimental.pallas import tpu as pltpu

F32 = jnp.float32
BF16 = jnp.bfloat16

D_MODEL = 1024
D_FF = 4096
N_MEM = 256
DEPTH = 4
SLOT = 128
HEAD_DIM = 64
MLA_HEADS = 12
MLA_QK = 96
MLA_Q_RANK = 384
MLA_KV_RANK = 256
SWA_HEADS = 12
SWA_KV_HEADS = 4
SWA_GROUP = 3
MEM_HEADS = 4
WINDOW = 128
EPS = 1e-6
NEG = -1e30
ROPE_THETA = 10000.0
N_DEV = 8

ADAM_LR = 0.001
ADAM_B1 = 0.9
ADAM_B2 = 0.999
ADAM_EPS = 1e-08
ADAM_WD = 0.01
ADAM_STEP = 10

TM = 512
TQ_MLA = 1024
MLA_PACK = 2
SWA_PACK = 4
TQ_CROSS = 2048
MM_VMEM_BUDGET = 38 * 1024 * 1024
ADAM_ROWS = 128
VMEM_LIMIT = 56 * 1024 * 1024

MESH_AXES = ("x", "y", "c")

LOG2_E = 1.4426950408889634
MLA_SCALE = MLA_QK ** -0.5
MLA_Q_SCALE = MLA_SCALE * LOG2_E

MLA_PAD_IN = 384 + SLOT + 256 + MEM_HEADS * SLOT
MLA_QOFF = (384 + SLOT + 256) // SLOT
SWA_PAD_IN = (SWA_HEADS + 2 * SWA_KV_HEADS + MEM_HEADS) * SLOT
SWA_QOFF = SWA_HEADS + 2 * SWA_KV_HEADS

SHARDED = (
    ("mla_w_in", 1), ("mla_w_uq", 2), ("mla_w_ukv", 2), ("swa_w_in", 2),
    ("w_mem_kv", 1), ("w_o", 1), ("mlp_w_up", 2), ("mlp_w_down", 1),
)
REPLICATED = ("attn_norm_g", "mlp_norm_g", "mem_norm_g", "final_norm_g",
              "mla_q_norm_g", "mla_kv_norm_g", "swa_sinks")
WEIGHT_ORDER = ("attn_norm_g", "mlp_norm_g", "mem_norm_g", "final_norm_g", "mla_w_in",
                "mla_q_norm_g", "mla_kv_norm_g", "mla_w_uq", "mla_w_ukv", "swa_w_in",
                "swa_sinks", "w_mem_kv", "w_o", "mlp_w_up", "mlp_w_down")


def _cparams():
    return pltpu.CompilerParams(vmem_limit_bytes=VMEM_LIMIT)


_DIMS = {"nn": (((1,), (0,)), ((), ())), "nt": (((1,), (1,)), ((), ())), "tn": (((0,), (0,)), ((), ()))}


def _compact(x):
    pairs = [x[:, 2 * j * SLOT:(2 * j + 1) * SLOT] + pltpu.roll(x[:, (2 * j + 1) * SLOT:(2 * j + 2) * SLOT], HEAD_DIM, 1)
             for j in range(x.shape[1] // (2 * SLOT))]
    return pairs[0] if len(pairs) == 1 else jnp.concatenate(pairs, axis=1)


def _expand(x):
    low = lax.broadcasted_iota(jnp.int32, (x.shape[0], SLOT), 1) < HEAD_DIM
    slots = []
    for j in range(x.shape[1] // SLOT):
        pair = x[:, j * SLOT:(j + 1) * SLOT]
        slots += [jnp.where(low, pair, 0.0), pltpu.roll(jnp.where(low, 0.0, pair), HEAD_DIM, 1)]
    return jnp.concatenate(slots, axis=1)


def _mm_tiles(m, n, k, a_bytes, b_bytes, o_bytes, extra_bytes, tm_fixed, tn_fixed):
    best = None
    for tm in ([tm_fixed] if tm_fixed else [t for t in range(4096, 0, -SLOT) if m % t == 0] or [m]):
        for tn in ([tn_fixed] if tn_fixed else [t for t in range(1024, 0, -SLOT) if n % t == 0] or [n]):
            need = 2 * (tm * k * a_bytes + k * tn * b_bytes + tm * tn * (o_bytes + extra_bytes))
            need += tm * tn * 4
            if need <= MM_VMEM_BUDGET and (best is None or tm * tn > best[0] * best[1]):
                best = (tm, tn)
    assert best is not None, (m, n, k)
    return best


def _mm(a, b, mode, out_dtype, name, res=None, aux=None, epi=None, b_blk=None, o_blk=None, after=None, norm=None,
        pairs=""):
    if b_blk is not None:
        nb, br, bc = b.shape
        b_shape = (nb * br, bc) if b_blk == "rows" else (br, nb * bc)
    else:
        b_shape = b.shape
    assert not pairs or (b_blk is None and o_blk is None and not ("b" in pairs and mode == "nt"))
    a_shape = (a.shape[0], a.shape[1] // 2) if "a" in pairs else a.shape
    if "b" in pairs:
        b_shape = (b_shape[0], b_shape[1] // 2)
    if mode == "nn":
        (m, k), (k2, n) = a_shape, b_shape
    elif mode == "nt":
        (m, k), (n, k2) = a_shape, b_shape
    else:
        (k, m), (k2, n) = a_shape, b_shape
    assert k == k2, (a.shape, b_shape, mode)
    k_blocked = b_blk is not None and (b_blk == "rows") == (mode != "nt")
    tn_fixed = None
    if b_blk is not None and not k_blocked:
        tn_fixed = br if b_blk == "rows" else bc
    if o_blk == "cols":
        tn_fixed = n // N_DEV
    tm_fixed = m // N_DEV if o_blk == "rows" else None
    has_res, has_aux, has_norm = res is not None, aux is not None, epi == "normbwd"
    assert o_blk is None or not (has_res or has_aux or has_norm)
    n_out = 2 if epi == "relu2" else 1
    if has_norm:
        tn_fixed = n
        o_bytes, extra_bytes = 4 + 2, 4 + 4
    else:
        o_bytes = n_out * jnp.dtype(out_dtype).itemsize
        extra_bytes = (4 if has_res else 0) + (aux.dtype.itemsize if has_aux else 0)
    pa, pb, po = (2 if "a" in pairs else 1), (2 if "b" in pairs else 1), (2 if "o" in pairs else 1)
    tm, tn = _mm_tiles(m, n, k, a.dtype.itemsize * (3 if pa == 2 else 1), b.dtype.itemsize * (3 if pb == 2 else 1),
                       o_bytes * po, extra_bytes, tm_fixed, tn_fixed)
    dims = _DIMS[mode]
    if mode == "tn":
        a_spec = pl.BlockSpec((k, pa * tm), lambda i, j: (0, i))
    else:
        a_spec = pl.BlockSpec((tm, pa * k), lambda i, j: (i, 0))
    if b_blk is None:
        if mode == "nt":
            b_spec = pl.BlockSpec((tn, k), lambda i, j: (j, 0))
        else:
            b_spec = pl.BlockSpec((k, pb * tn), lambda i, j: (0, j))
    elif k_blocked and mode == "nt":
        b_spec = pl.BlockSpec((N_DEV, tn, bc), lambda i, j: (0, j, 0))
    elif k_blocked:
        b_spec = pl.BlockSpec((N_DEV, br, tn), lambda i, j: (0, 0, j))
    elif mode == "nt":
        b_spec = pl.BlockSpec((None, tn, k), lambda i, j: (j, 0, 0))
    else:
        b_spec = pl.BlockSpec((None, k, tn), lambda i, j: (j, 0, 0))
    if o_blk is None:
        o_spec = pl.BlockSpec((tm, po * tn), lambda i, j: (i, j))
        o_shape = (m, po * n)
    elif o_blk == "rows":
        o_spec = pl.BlockSpec((None, tm, tn), lambda i, j: (i, 0, j))
        o_shape = (N_DEV, tm, n)
    else:
        o_spec = pl.BlockSpec((None, tm, tn), lambda i, j: (j, i, 0))
        o_shape = (N_DEV, m, tn)

    def body(*refs):
        a_ref, b_ref = refs[0], refs[1]
        pos = 2
        res_ref = aux_ref = None
        if has_res:
            res_ref = refs[pos]
            pos += 1
        if has_aux:
            aux_ref = refs[pos]
            pos += 1
        if has_norm:
            x_ref, g_ref, dres_ref = refs[pos:pos + 3]
            pos += 3
        if after is not None:
            pos += 1
        outs = refs[pos:]
        if k_blocked and mode == "nt":
            r = None
            for d in range(N_DEV):
                part = lax.dot_general(a_ref[:, d * bc:(d + 1) * bc].astype(BF16), b_ref[d].astype(BF16), dims,
                                       preferred_element_type=F32)
                r = part if r is None else r + part
        else:
            bv = b_ref[...].reshape(k, tn) if k_blocked else b_ref[...]
            av = _compact(a_ref[...].astype(F32)) if pa == 2 else a_ref[...]
            bv = _compact(bv.astype(F32)) if pb == 2 else bv
            r = lax.dot_general(av.astype(BF16), bv.astype(BF16), dims, preferred_element_type=F32)
        if po == 2:
            r = _expand(r)
        if epi == "relu2":
            r = jnp.maximum(r, 0.0)
            outs[0][...] = r.astype(outs[0].dtype)
            outs[1][...] = (r * r).astype(outs[1].dtype)
        elif has_norm:
            xv = x_ref[...]
            rs = lax.rsqrt(jnp.mean(xv * xv, axis=1, keepdims=True) + EPS)
            xh = xv * rs
            dxh = r * g_ref[...]
            dx = rs * (dxh - xh * jnp.mean(dxh * xh, axis=1, keepdims=True)) + dres_ref[...]
            outs[0][...] = dx
            outs[1][...] = dx.astype(BF16)

            @pl.when(pl.program_id(0) == 0)
            def _():
                outs[2][...] = jnp.zeros_like(outs[2])

            outs[2][...] += jnp.sum(r * xh, axis=0, keepdims=True)
        else:
            if epi == "mul2aux":
                r = r * (2.0 * aux_ref[...].astype(F32))
            if has_res:
                r = r + res_ref[...]
            outs[0][...] = r.astype(outs[0].dtype)

    in_specs = [a_spec, b_spec]
    args = [a, b]
    if has_res:
        in_specs.append(o_spec)
        args.append(res)
    if has_aux:
        in_specs.append(o_spec)
        args.append(aux)
    vec_spec = pl.BlockSpec((1, n), lambda i, j: (0, 0))
    if has_norm:
        in_specs += [o_spec, vec_spec, o_spec]
        args += [norm[0], norm[1].reshape(1, n), norm[2]]
    if after is not None:
        in_specs.append(pl.BlockSpec(memory_space=pl.ANY))
        args.append(after)
    if has_norm:
        out_specs = [o_spec, o_spec, vec_spec]
        out_shape = [jax.ShapeDtypeStruct(o_shape, F32), jax.ShapeDtypeStruct(o_shape, BF16),
                     jax.ShapeDtypeStruct((1, n), F32)]
    else:
        out_specs = [o_spec] * n_out
        out_shape = [jax.ShapeDtypeStruct(o_shape, out_dtype)] * n_out
    out = pl.pallas_call(
        body, name=name, grid=(m // tm, n // tn),
        in_specs=in_specs, out_specs=out_specs, out_shape=out_shape, compiler_params=_cparams(),
    )(*args)
    return out if len(out) > 1 else out[0]


def _rmsnorm_fwd(xarr, colblk, width, g, name, after=None):
    rows = xarr.shape[0]
    tm = min(TM, rows)

    def body(x_ref, g_ref, *rest):
        y_ref = rest[-1]
        x = x_ref[...].astype(F32)
        r = lax.rsqrt(jnp.mean(x * x, axis=1, keepdims=True) + EPS)
        y_ref[...] = (x * r * g_ref[...]).astype(y_ref.dtype)

    in_specs = [pl.BlockSpec((tm, width), lambda i: (i, colblk)), pl.BlockSpec((1, width), lambda i: (0, 0))]
    args = [xarr, g.reshape(1, width)]
    if after is not None:
        in_specs.append(pl.BlockSpec(memory_space=pl.ANY))
        args.append(after)
    return pl.pallas_call(
        body, name=name, grid=(rows // tm,), in_specs=in_specs,
        out_specs=pl.BlockSpec((tm, width), lambda i: (i, 0)),
        out_shape=jax.ShapeDtypeStruct((rows, width), BF16), compiler_params=_cparams(),
    )(*args)


def _rmsnorm_bwd(xarr, colblk, width, g, dy, dres, out_dtype, name):
    rows = xarr.shape[0]
    tm = min(TM, rows)
    has_res = dres is not None

    def body(*refs):
        x_ref, g_ref, dy_ref = refs[0], refs[1], refs[2]
        dres_ref = refs[3] if has_res else None
        dx_ref, dg_ref = refs[-2], refs[-1]
        x = x_ref[...].astype(F32)
        dyv = dy_ref[...].astype(F32)
        r = lax.rsqrt(jnp.mean(x * x, axis=1, keepdims=True) + EPS)
        xh = x * r
        dxh = dyv * g_ref[...]
        dx = r * (dxh - xh * jnp.mean(dxh * xh, axis=1, keepdims=True))
        if has_res:
            dx = dx + dres_ref[...]
        dx_ref[...] = dx.astype(dx_ref.dtype)

        @pl.when(pl.program_id(0) == 0)
        def _():
            dg_ref[...] = jnp.zeros_like(dg_ref)

        dg_ref[...] += jnp.sum(dyv * xh, axis=0, keepdims=True)

    row_spec = pl.BlockSpec((tm, width), lambda i: (i, 0))
    vec_spec = pl.BlockSpec((1, width), lambda i: (0, 0))
    in_specs = [pl.BlockSpec((tm, width), lambda i: (i, colblk)), vec_spec, row_spec]
    args = [xarr, g.reshape(1, width), dy]
    if has_res:
        in_specs.append(row_spec)
        args.append(dres)
    return pl.pallas_call(
        body, name=name, grid=(rows // tm,), in_specs=in_specs, out_specs=[row_spec, vec_spec],
        out_shape=[jax.ShapeDtypeStruct((rows, width), out_dtype), jax.ShapeDtypeStruct((1, width), F32)],
        compiler_params=_cparams(),
    )(*args)


def _loss_head(x, g, tgt):
    rows, width = x.shape
    tm = min(TM, rows)

    def body(x_ref, g_ref, t_ref, dx_ref, dxb_ref, dg_ref, loss_ref):
        xv = x_ref[...]
        gv = g_ref[...]
        r = lax.rsqrt(jnp.mean(xv * xv, axis=1, keepdims=True) + EPS)
        xh = xv * r
        err = xh * gv - t_ref[...]
        part = 0.5 * jnp.sum(jnp.mean(err * err, axis=1, keepdims=True), axis=0, keepdims=True)
        dyv = err * (1.0 / width)
        dxh = dyv * gv
        dxv = r * (dxh - xh * jnp.mean(dxh * xh, axis=1, keepdims=True))
        dx_ref[...] = dxv
        dxb_ref[...] = dxv.astype(BF16)

        @pl.when(pl.program_id(0) == 0)
        def _():
            dg_ref[...] = jnp.zeros_like(dg_ref)
            loss_ref[...] = jnp.zeros_like(loss_ref)

        dg_ref[...] += jnp.sum(dyv * xh, axis=0, keepdims=True)
        loss_ref[...] += jnp.broadcast_to(part, loss_ref.shape)

    row_spec = pl.BlockSpec((tm, width), lambda i: (i, 0))
    vec_spec = pl.BlockSpec((1, width), lambda i: (0, 0))
    return pl.pallas_call(
        body, name="loss_head", grid=(rows // tm,), in_specs=[row_spec, vec_spec, row_spec],
        out_specs=[row_spec, row_spec, vec_spec, pl.BlockSpec((1, SLOT), lambda i: (0, 0))],
        out_shape=[jax.ShapeDtypeStruct((rows, width), F32), jax.ShapeDtypeStruct((rows, width), BF16),
                   jax.ShapeDtypeStruct((1, width), F32), jax.ShapeDtypeStruct((1, SLOT), F32)],
        compiler_params=_cparams(),
    )(x, g.reshape(1, width), tgt)


def _lane_consts():
    half = 16
    inv = ROPE_THETA ** (-(jnp.arange(half, dtype=F32) * 2.0) / 32)
    lane = jnp.arange(SLOT)
    first = (lane >= 64) & (lane < 80)
    second = (lane >= 80) & (lane < 96)
    inv_lane = jnp.where(first | second, inv[(lane - 64) % half], 0.0)
    rows = [inv_lane, (lane < 64).astype(F32), first.astype(F32), second.astype(F32)]
    rows += [jnp.zeros((SLOT,), F32)] * 4
    return jnp.stack(rows).astype(F32)


def _rope_tables(pos_col, consts):
    rows = pos_col.shape[0]
    tm = min(TM, rows)

    def body(p_ref, k_ref, c_ref, s1_ref, s2_ref):
        ang = p_ref[...] * k_ref[0:1, :]
        cos, sin = jnp.cos(ang), jnp.sin(ang)
        first, second = k_ref[2:3, :], k_ref[3:4, :]
        c_ref[...] = k_ref[1:2, :] + (first + second) * cos
        s1_ref[...] = -first * sin
        s2_ref[...] = second * sin

    spec = pl.BlockSpec((tm, SLOT), lambda i: (i, 0))
    shp = jax.ShapeDtypeStruct((rows, SLOT), F32)
    return pl.pallas_call(
        body, name="rope_tables", grid=(rows // tm,),
        in_specs=[pl.BlockSpec((tm, 1), lambda i: (i, 0)), pl.BlockSpec((8, SLOT), lambda i: (0, 0))],
        out_specs=[spec, spec, spec], out_shape=[shp, shp, shp], compiler_params=_cparams(),
    )(pos_col, consts)


def _rot(xv, c, s1, s2):
    return xv * c + pltpu.roll(xv, SLOT - 16, 1) * s1 + pltpu.roll(xv, 16, 1) * s2


def _rot_t(dy, c, s1, s2):
    return dy * c + pltpu.roll(dy * s1, 16, 1) + pltpu.roll(dy * s2, SLOT - 16, 1)


def _mla_rope_fwd(qraw, kvraw, proj, tabs):
    rows = qraw.shape[0]
    tm = min(256, rows)
    hw = MLA_HEADS * SLOT

    def body(q_ref, kv_ref, kr_ref, c_ref, s1_ref, s2_ref, qo, ko, vo):
        c, s1, s2 = c_ref[...], s1_ref[...], s2_ref[...]
        kr = _rot(kr_ref[...], c, s1, s2)
        low = lax.broadcasted_iota(jnp.int32, (tm, SLOT), 1) < HEAD_DIM
        for h in range(MLA_HEADS):
            sl = slice(h * SLOT, (h + 1) * SLOT)
            qo[:, sl] = (_rot(q_ref[:, sl], c, s1, s2) * MLA_Q_SCALE).astype(BF16)
            kvh = kv_ref[:, sl]
            ko[:, sl] = (jnp.where(low, kvh, 0.0) + kr).astype(BF16)
            vo[:, sl] = pltpu.roll(jnp.where(low, 0.0, kvh), HEAD_DIM, 1).astype(BF16)

    tab = pl.BlockSpec((tm, SLOT), lambda i: (i, 0))
    wide = pl.BlockSpec((tm, hw), lambda i: (i, 0))
    shp = jax.ShapeDtypeStruct((rows, hw), BF16)
    return pl.pallas_call(
        body, name="mla_rope_fwd", grid=(rows // tm,),
        in_specs=[wide, wide, pl.BlockSpec((tm, SLOT), lambda i: (i, 3)),
                  tab, tab, tab],
        out_specs=[wide, wide, wide], out_shape=[shp, shp, shp], compiler_params=_cparams(),
    )(qraw, kvraw, proj, *tabs)


def _mla_rope_bwd(dq, dk, dv, tabs, consts):
    rows = dq.shape[0]
    tm = min(256, rows)
    hw = MLA_HEADS * SLOT

    def body(dq_ref, dk_ref, dv_ref, c_ref, s1_ref, s2_ref, k_ref, dqo, dkvo, dkro):
        c, s1, s2 = c_ref[...], s1_ref[...], s2_ref[...]
        ksum = jnp.zeros((tm, SLOT), F32)
        low = lax.broadcasted_iota(jnp.int32, (tm, SLOT), 1) < HEAD_DIM
        for h in range(MLA_HEADS):
            sl = slice(h * SLOT, (h + 1) * SLOT)
            dqo[:, sl] = _rot_t(dq_ref[:, sl], c, s1, s2).astype(BF16)
            dkh = dk_ref[:, sl]
            ksum = ksum + dkh
            dvh = pltpu.roll(jnp.where(low, dv_ref[:, sl], 0.0), HEAD_DIM, 1)
            dkvo[:, sl] = (jnp.where(low, dkh, 0.0) + dvh).astype(BF16)
        dkro[...] = _rot_t(ksum, c, s1, s2) * (k_ref[2:3, :] + k_ref[3:4, :])

    tab = pl.BlockSpec((tm, SLOT), lambda i: (i, 0))
    wide = pl.BlockSpec((tm, hw), lambda i: (i, 0))
    return pl.pallas_call(
        body, name="mla_rope_bwd", grid=(rows // tm,),
        in_specs=[wide, wide, wide, tab, tab, tab, pl.BlockSpec((8, SLOT), lambda i: (0, 0))],
        out_specs=[wide, wide, tab],
        out_shape=[jax.ShapeDtypeStruct((rows, hw), BF16), jax.ShapeDtypeStruct((rows, hw), BF16),
                   jax.ShapeDtypeStruct((rows, SLOT), F32)],
        compiler_params=_cparams(),
    )(dq, dk, dv, *tabs, consts)


def _nt(a, b):
    return lax.dot_general(a, b, _DIMS["nt"], preferred_element_type=F32)


def _tn(a, b):
    return lax.dot_general(a, b, _DIMS["tn"], preferred_element_type=F32)


def _nn(a, b):
    return lax.dot_general(a, b, _DIMS["nn"], preferred_element_type=F32)


def _causal(rows, keys):
    shp = (rows.stop - rows.start, keys.stop - keys.start)
    return (keys.start + lax.broadcasted_iota(jnp.int32, shp, 1)
            <= rows.start + lax.broadcasted_iota(jnp.int32, shp, 0))


def _mla_attn_fwd(q, k, v):
    rows = q.shape[0]
    t = min(TQ_MLA, rows)
    nt = rows // t
    wide = MLA_PACK * SLOT

    def body(q_ref, k_ref, v_ref, o_ref, lse_ref, m_sc, l_sc, acc_sc):
        i, j = pl.program_id(1), pl.program_id(2)

        @pl.when(j == 0)
        def _():
            m_sc[...] = jnp.full_like(m_sc, NEG)
            l_sc[...] = jnp.zeros_like(l_sc)
            acc_sc[...] = jnp.zeros_like(acc_sc)

        def step(diagonal):
            for hh in range(MLA_PACK):
                sl = slice(hh * SLOT, (hh + 1) * SLOT)
                s = _nt(q_ref[:, sl], k_ref[:, sl])
                if diagonal:
                    s = jnp.where(_causal(slice(0, t), slice(0, t)), s, NEG)
                m_prev = m_sc[hh]
                m_new = jnp.maximum(m_prev, jnp.max(s, axis=1, keepdims=True))
                p = jnp.exp2(s - m_new)
                alpha = jnp.exp2(m_prev - m_new)
                l_new = alpha * l_sc[hh] + jnp.sum(p, axis=1, keepdims=True)
                acc = alpha * acc_sc[:, sl] + _nn(p.astype(BF16), v_ref[:, sl])
                if diagonal:
                    o_ref[:, sl] = (acc / l_new).astype(o_ref.dtype)
                    lse_ref[:, sl] = jnp.broadcast_to(m_new + jnp.log(l_new) * LOG2_E, (t, SLOT))
                else:
                    m_sc[hh] = m_new
                    l_sc[hh] = l_new
                    acc_sc[:, sl] = acc

        @pl.when(j < i)
        def _():
            step(False)

        @pl.when(j == i)
        def _():
            step(True)

    q_spec = pl.BlockSpec((t, wide), lambda h, i, j: (i, h))
    kv_spec = pl.BlockSpec((t, wide), lambda h, i, j: (jnp.minimum(j, i), h))
    return pl.pallas_call(
        body, name="mla_attn_fwd", grid=(MLA_HEADS // MLA_PACK, nt, nt),
        in_specs=[q_spec, kv_spec, kv_spec], out_specs=[q_spec, q_spec],
        out_shape=[jax.ShapeDtypeStruct(q.shape, BF16), jax.ShapeDtypeStruct(q.shape, F32)],
        scratch_shapes=[pltpu.VMEM((MLA_PACK, t, 1), F32), pltpu.VMEM((MLA_PACK, t, 1), F32),
                        pltpu.VMEM((t, wide), F32)],
        compiler_params=_cparams(),
    )(q, k, v)


def _mla_attn_bwd(q, k, v, o, do, lse, after):
    rows = q.shape[0]
    t = min(TQ_MLA, rows)
    nt = rows // t
    wide = MLA_PACK * SLOT

    def body(q_ref, k_ref, v_ref, o_ref, do_ref, lse_ref, after_ref, dq_ref, dk_ref, dv_ref, dk_sc, dv_sc):
        j, i = pl.program_id(1), pl.program_id(2)

        @pl.when((j == 0) & (i == 0))
        def _():
            dq_ref[...] = jnp.zeros_like(dq_ref)

        @pl.when(i == 0)
        def _():
            dk_sc[...] = jnp.zeros_like(dk_sc)
            dv_sc[...] = jnp.zeros_like(dv_sc)

        def chunk(hh, rows, keys, masked):
            sl = slice(hh * SLOT, (hh + 1) * SLOT)
            n_rows = rows.stop - rows.start
            qv, kv, dov = q_ref[rows, sl], k_ref[keys, sl], do_ref[rows, sl]
            s = _nt(qv, kv)
            if masked:
                s = jnp.where(_causal(rows, keys), s, NEG)
            p = jnp.exp2(s - lse_ref[rows, hh * SLOT:hh * SLOT + 1])
            delta = jnp.sum(dov.astype(F32) * o_ref[rows, sl].astype(F32), axis=1, keepdims=True)
            dp = _nt(dov, v_ref[keys, sl])
            ds = (p * (dp - delta)).astype(BF16)
            dv_sc[keys, sl] += _tn(p.astype(BF16), dov)
            dk_sc[keys, sl] += _tn(ds, qv)
            r0 = pl.multiple_of(i * t + rows.start, n_rows)
            dq_ref[pl.ds(r0, n_rows), sl] += _nn(ds, kv) * MLA_SCALE

        @pl.when(i > j)
        def _():
            for hh in range(MLA_PACK):
                chunk(hh, slice(0, t), slice(0, t), False)

        @pl.when(i == j)
        def _():
            for hh in range(MLA_PACK):
                chunk(hh, slice(0, t), slice(0, t // 2), True)
                chunk(hh, slice(t // 2, t), slice(t // 2, t), True)

        @pl.when(i == nt - 1)
        def _():
            dk_ref[...] = dk_sc[...] * (1.0 / LOG2_E)
            dv_ref[...] = dv_sc[...]

    q_spec = pl.BlockSpec((t, wide), lambda h, j, i: (jnp.maximum(i, j), h))
    kv_spec = pl.BlockSpec((t, wide), lambda h, j, i: (j, h))
    head_spec = pl.BlockSpec((rows, wide), lambda h, j, i: (0, h))
    shp = jax.ShapeDtypeStruct(q.shape, F32)
    return pl.pallas_call(
        body, name="mla_attn_bwd", grid=(MLA_HEADS // MLA_PACK, nt, nt),
        in_specs=[q_spec, kv_spec, kv_spec, q_spec, q_spec, q_spec, pl.BlockSpec(memory_space=pl.ANY)],
        out_specs=[head_spec, kv_spec, kv_spec], out_shape=[shp, shp, shp],
        scratch_shapes=[pltpu.VMEM((t, wide), F32), pltpu.VMEM((t, wide), F32)],
        compiler_params=_cparams(),
    )(q, k, v, o, do, lse, after)


def _swa_specs(t):
    def prev(i):
        return jnp.maximum(i - 1, 0)
    kw = SWA_PACK * SLOT
    k0, v0 = SWA_HEADS // SWA_PACK, (SWA_HEADS + SWA_KV_HEADS) // SWA_PACK
    q3 = pl.BlockSpec((t, SWA_PACK * SWA_GROUP * SLOT), lambda h, i: (i, h))
    kp = pl.BlockSpec((t, kw), lambda h, i: (prev(i), k0 + h))
    kc = pl.BlockSpec((t, kw), lambda h, i: (i, k0 + h))
    vp = pl.BlockSpec((t, kw), lambda h, i: (prev(i), v0 + h))
    vc = pl.BlockSpec((t, kw), lambda h, i: (i, v0 + h))
    pcol = pl.BlockSpec((t, 1), lambda h, i: (i, 0))
    prow_p = pl.BlockSpec((1, t), lambda h, i: (0, prev(i)))
    prow_c = pl.BlockSpec((1, t), lambda h, i: (0, i))
    return [q3, kp, kc, vp, vc, pcol, prow_p, prow_c]


def _stack(ref, first):
    return jnp.concatenate([ref[:, (first + g) * SLOT:(first + g + 1) * SLOT] for g in range(SWA_GROUP)], axis=0)


def _swa_logits(q3, kp, kc, pq, pkp, pkc, slope_ref, kvh, i, t):
    r = lax.broadcasted_iota(jnp.int32, (t, t), 0)
    c = lax.broadcasted_iota(jnp.int32, (t, t), 1)
    ok_c = c <= r
    ok_p = (c - r) > jnp.where(i > 0, 0, t)
    dist_p, dist_c = pq - pkp, pq - pkc
    s_p3 = _nt(q3, kp) * (HEAD_DIM ** -0.5)
    s_c3 = _nt(q3, kc) * (HEAD_DIM ** -0.5)
    out = []
    for g in range(SWA_GROUP):
        slope = slope_ref[kvh * SWA_GROUP + g]
        rows = slice(g * t, (g + 1) * t)
        out.append((jnp.where(ok_p, s_p3[rows] - slope * dist_p, NEG),
                    jnp.where(ok_c, s_c3[rows] - slope * dist_c, NEG)))
    return out


def _swa_attn_fwd(proj, pos_col, pos_row, slopes, sinks):
    rows = proj.shape[0]
    t = WINDOW
    hw = SWA_HEADS * SLOT

    def body(slope_ref, sink_ref, q_ref, kp_ref, kc_ref, vp_ref, vc_ref, pq_ref, pkp_ref, pkc_ref, o_ref, lse_ref):
        i = pl.program_id(1)
        for kv in range(SWA_PACK):
            kvh = pl.program_id(0) * SWA_PACK + kv
            ksl = slice(kv * SLOT, (kv + 1) * SLOT)
            logits = _swa_logits(_stack(q_ref, kv * SWA_GROUP), kp_ref[:, ksl], kc_ref[:, ksl], pq_ref[...],
                                 pkp_ref[...], pkc_ref[...], slope_ref, kvh, i, t)
            e_p, e_c, norm = [], [], []
            for g, (s_p, s_c) in enumerate(logits):
                sl = slice((kv * SWA_GROUP + g) * SLOT, (kv * SWA_GROUP + g + 1) * SLOT)
                sink = sink_ref[kvh * SWA_GROUP + g]
                m = jnp.maximum(jnp.maximum(jnp.max(s_p, axis=1, keepdims=True),
                                            jnp.max(s_c, axis=1, keepdims=True)), sink)
                ep, ec = jnp.exp(s_p - m), jnp.exp(s_c - m)
                l = jnp.sum(ep, axis=1, keepdims=True) + jnp.sum(ec, axis=1, keepdims=True) + jnp.exp(sink - m)
                e_p.append(ep.astype(BF16))
                e_c.append(ec.astype(BF16))
                norm.append(l)
                lse_ref[:, sl] = jnp.broadcast_to(m + jnp.log(l), (t, SLOT))
            acc = (_nn(jnp.concatenate(e_p, axis=0), vp_ref[:, ksl])
                   + _nn(jnp.concatenate(e_c, axis=0), vc_ref[:, ksl]))
            for g in range(SWA_GROUP):
                sl = slice((kv * SWA_GROUP + g) * SLOT, (kv * SWA_GROUP + g + 1) * SLOT)
                o_ref[:, sl] = (acc[g * t:(g + 1) * t] / norm[g]).astype(o_ref.dtype)

    smem = pl.BlockSpec(memory_space=pltpu.SMEM)
    out_spec = pl.BlockSpec((t, SWA_PACK * SWA_GROUP * SLOT), lambda h, i: (i, h))
    return pl.pallas_call(
        body, name="swa_attn_fwd", grid=(SWA_KV_HEADS // SWA_PACK, rows // t),
        in_specs=[smem, smem] + _swa_specs(t), out_specs=[out_spec, out_spec],
        out_shape=[jax.ShapeDtypeStruct((rows, hw), BF16), jax.ShapeDtypeStruct((rows, hw), F32)],
        compiler_params=_cparams(),
    )(slopes, sinks, proj, proj, proj, proj, proj, pos_col, pos_row, pos_row)


def _swa_attn_bwd(proj, o, do, lse, pos_col, pos_row, slopes, sinks, after):
    rows = proj.shape[0]
    t = WINDOW
    hw = SWA_HEADS * SLOT
    scale = HEAD_DIM ** -0.5

    def body(slope_ref, sink_ref, q_ref, kp_ref, kc_ref, vp_ref, vc_ref, pq_ref, pkp_ref, pkc_ref,
             o_ref, do_ref, lse_ref, after_ref, dq_ref, dk_ref, dv_ref, dsink_ref):
        i = pl.program_id(1)

        @pl.when(i == 0)
        def _():
            dk_ref[...] = jnp.zeros_like(dk_ref)
            dv_ref[...] = jnp.zeros_like(dv_ref)
            dsink_ref[...] = jnp.zeros_like(dsink_ref)

        r_c = pl.multiple_of(i * t, t)
        r_p = pl.multiple_of(jnp.maximum(i - 1, 0) * t, t)
        for kv in range(SWA_PACK):
            kvh = pl.program_id(0) * SWA_PACK + kv
            ksl = slice(kv * SLOT, (kv + 1) * SLOT)
            q3, do3 = _stack(q_ref, kv * SWA_GROUP), _stack(do_ref, kv * SWA_GROUP)
            logits = _swa_logits(q3, kp_ref[:, ksl], kc_ref[:, ksl], pq_ref[...], pkp_ref[...], pkc_ref[...],
                                 slope_ref, kvh, i, t)
            dp_p3, dp_c3 = _nt(do3, vp_ref[:, ksl]), _nt(do3, vc_ref[:, ksl])
            p_p, p_c, ds_p, ds_c = [], [], [], []
            for g, (s_p, s_c) in enumerate(logits):
                head = kv * SWA_GROUP + g
                sl = slice(head * SLOT, (head + 1) * SLOT)
                rws = slice(g * t, (g + 1) * t)
                lse_g = lse_ref[:, head * SLOT:head * SLOT + 1]
                pp, pc = jnp.exp(s_p - lse_g), jnp.exp(s_c - lse_g)
                delta = jnp.sum(do_ref[:, sl].astype(F32) * o_ref[:, sl].astype(F32), axis=1, keepdims=True)
                p_p.append(pp.astype(BF16))
                p_c.append(pc.astype(BF16))
                ds_p.append((pp * (dp_p3[rws] - delta)).astype(BF16))
                ds_c.append((pc * (dp_c3[rws] - delta)).astype(BF16))
                sink = sink_ref[kvh * SWA_GROUP + g]
                dsink = -jnp.sum(jnp.exp(sink - lse_g) * delta, axis=0, keepdims=True)
                dsink_ref[head * 8:(head + 1) * 8, :] += jnp.broadcast_to(dsink, (8, SLOT))
            p_p3, p_c3 = jnp.concatenate(p_p, axis=0), jnp.concatenate(p_c, axis=0)
            ds_p3, ds_c3 = jnp.concatenate(ds_p, axis=0), jnp.concatenate(ds_c, axis=0)
            dq3 = (_nn(ds_p3, kp_ref[:, ksl]) + _nn(ds_c3, kc_ref[:, ksl])) * scale
            for g in range(SWA_GROUP):
                head = kv * SWA_GROUP + g
                dq_ref[:, head * SLOT:(head + 1) * SLOT] = dq3[g * t:(g + 1) * t]
            dk_ref[pl.ds(r_c, t), ksl] += _tn(ds_c3, q3) * scale
            dv_ref[pl.ds(r_c, t), ksl] += _tn(p_c3, do3)
            dk_ref[pl.ds(r_p, t), ksl] += _tn(ds_p3, q3) * scale
            dv_ref[pl.ds(r_p, t), ksl] += _tn(p_p3, do3)

    smem = pl.BlockSpec(memory_space=pltpu.SMEM)
    qlike = pl.BlockSpec((t, SWA_PACK * SWA_GROUP * SLOT), lambda h, i: (i, h))
    kv_out = pl.BlockSpec((rows, SWA_PACK * SLOT), lambda h, i: (0, h))
    return pl.pallas_call(
        body, name="swa_attn_bwd", grid=(SWA_KV_HEADS // SWA_PACK, rows // t),
        in_specs=[smem, smem] + _swa_specs(t) + [qlike, qlike, qlike, pl.BlockSpec(memory_space=pl.ANY)],
        out_specs=[qlike, kv_out, kv_out,
                   pl.BlockSpec((SWA_PACK * SWA_GROUP * 8, SLOT), lambda h, i: (h, 0))],
        out_shape=[jax.ShapeDtypeStruct((rows, hw), F32), jax.ShapeDtypeStruct((rows, SWA_KV_HEADS * SLOT), F32),
                   jax.ShapeDtypeStruct((rows, SWA_KV_HEADS * SLOT), F32),
                   jax.ShapeDtypeStruct((SWA_HEADS * 8, SLOT), F32)],
        compiler_params=_cparams(),
    )(slopes, sinks, proj, proj, proj, proj, proj, pos_col, pos_row, pos_row, o, do, lse, after)


def _cross_attn_fwd(proj, qoff, kvmem):
    rows = proj.shape[0]
    t = min(TQ_CROSS, rows)

    def body(q_ref, k_ref, v_ref, o_ref):
        s = _nt(q_ref[...].astype(BF16), k_ref[...]) * (HEAD_DIM ** -0.5)
        e = jnp.exp(s - jnp.max(s, axis=1, keepdims=True))
        p = e / jnp.sum(e, axis=1, keepdims=True)
        o_ref[...] = _nn(p.astype(BF16), v_ref[...]).astype(o_ref.dtype)

    return pl.pallas_call(
        body, name="cross_attn_fwd", grid=(rows // t, MEM_HEADS),
        in_specs=[pl.BlockSpec((t, SLOT), lambda i, h: (i, qoff + h)),
                  pl.BlockSpec((N_MEM, SLOT), lambda i, h: (0, h)),
                  pl.BlockSpec((N_MEM, SLOT), lambda i, h: (0, MEM_HEADS + h))],
        out_specs=pl.BlockSpec((t, SLOT), lambda i, h: (i, h)),
        out_shape=jax.ShapeDtypeStruct((rows, MEM_HEADS * SLOT), BF16), compiler_params=_cparams(),
    )(proj, kvmem, kvmem)


def _cross_attn_bwd(proj, qoff, kvmem, do):
    rows = proj.shape[0]
    t = min(TQ_CROSS, rows)
    scale = HEAD_DIM ** -0.5

    def body(q_ref, k_ref, v_ref, do_ref, dq_ref, dk_ref, dv_ref):
        @pl.when(pl.program_id(1) == 0)
        def _():
            dk_ref[...] = jnp.zeros_like(dk_ref)
            dv_ref[...] = jnp.zeros_like(dv_ref)

        qv, kv, dov = q_ref[...].astype(BF16), k_ref[...], do_ref[...]
        s = _nt(qv, kv) * scale
        e = jnp.exp(s - jnp.max(s, axis=1, keepdims=True))
        p = e / jnp.sum(e, axis=1, keepdims=True)
        dp = _nt(dov, v_ref[...])
        ds = (p * (dp - jnp.sum(p * dp, axis=1, keepdims=True))).astype(BF16)
        dq_ref[...] = _nn(ds, kv) * scale
        dk_ref[...] += _tn(ds, qv) * scale
        dv_ref[...] += _tn(p.astype(BF16), dov)

    mem_out = pl.BlockSpec((N_MEM, SLOT), lambda h, i: (0, h))
    return pl.pallas_call(
        body, name="cross_attn_bwd", grid=(MEM_HEADS, rows // t),
        in_specs=[pl.BlockSpec((t, SLOT), lambda h, i: (i, qoff + h)),
                  pl.BlockSpec((N_MEM, SLOT), lambda h, i: (0, h)),
                  pl.BlockSpec((N_MEM, SLOT), lambda h, i: (0, MEM_HEADS + h)),
                  pl.BlockSpec((t, SLOT), lambda h, i: (i, h))],
        out_specs=[pl.BlockSpec((t, SLOT), lambda h, i: (i, h)), mem_out, mem_out],
        out_shape=[jax.ShapeDtypeStruct((rows, MEM_HEADS * SLOT), F32),
                   jax.ShapeDtypeStruct((N_MEM, MEM_HEADS * SLOT), F32),
                   jax.ShapeDtypeStruct((N_MEM, MEM_HEADS * SLOT), F32)],
        compiler_params=_cparams(),
    )(proj, kvmem, kvmem, do)


def _place():
    return lax.axis_index("x"), lax.axis_index("y"), lax.axis_index("c")


def _flip(v, bit):
    return 1 - v if bit else v


def _all_gather(blocks, name):
    nb = len(blocks)

    def body(*refs):
        x_refs, out_refs = refs[:nb], refs[nb:2 * nb]
        send_sems, recv_sems, local_sems = refs[2 * nb:]
        x, y, c = _place()
        me, sibling = (x, y, c), (x, y, 1 - c)
        chips = [(1 - x, y), (x, 1 - y), (1 - x, 1 - y)]

        def copy(b, k, blk, to, from_input=False):
            slot = out_refs[b].at[4 * blk[0] + 2 * blk[1] + blk[2]]
            return pltpu.make_async_remote_copy(
                src_ref=x_refs[b] if from_input else slot, dst_ref=slot,
                send_sem=send_sems.at[b, k], recv_sem=recv_sems.at[b, k],
                device_id=to, device_id_type=pl.DeviceIdType.MESH)

        mine = [pltpu.make_async_copy(x_refs[b], out_refs[b].at[4 * x + 2 * y + c], local_sems.at[b])
                for b in range(nb)]
        for cp in mine:
            cp.start()
        first = []
        for b in range(nb):
            first.append(copy(b, 0, me, sibling, from_input=True))
            first += [copy(b, 1 + n, me, (*chip, c), from_input=True) for n, chip in enumerate(chips)]
        for cp in first:
            cp.start()
        passed = []
        for n, chip in enumerate(chips):
            for b in range(nb):
                copy(b, 1 + n, (*chip, c), me).wait_recv()
                passed.append(copy(b, 4 + n, (*chip, c), sibling))
                passed[-1].start()
        for b in range(nb):
            copy(b, 0, sibling, me).wait_recv()
            for n, chip in enumerate(chips):
                copy(b, 4 + n, (*chip, 1 - c), me).wait_recv()
        for cp in first + passed:
            cp.wait_send()
        for cp in mine:
            cp.wait()

    any_spec = pl.BlockSpec(memory_space=pl.ANY)
    return pl.pallas_call(
        body, name=name, in_specs=[any_spec] * nb, out_specs=[any_spec] * nb,
        out_shape=[jax.ShapeDtypeStruct((N_DEV,) + blk.shape, blk.dtype) for blk in blocks],
        scratch_shapes=[pltpu.SemaphoreType.DMA((nb, 7)), pltpu.SemaphoreType.DMA((nb, 7)),
                        pltpu.SemaphoreType.DMA((nb,))],
    )(*blocks)


def _peers(x, y, c):
    out = []
    for n in range(1, N_DEV):
        peer = (_flip(x, n & 4), _flip(y, n & 2), _flip(c, n & 1))
        out.append((n - 1, peer, 4 * peer[0] + 2 * peer[1] + peer[2]))
    return out


_HBM = pl.BlockSpec(memory_space=pltpu.HBM)
_SEM = pl.BlockSpec(memory_space=pltpu.SEMAPHORE)


def _exchange_start(srcs, scatter, name, after=None):
    ns = len(srcs)
    lands = [lax.empty(s.shape if scatter else (N_DEV,) + s.shape, s.dtype) for s in srcs]

    def body(*refs):
        src_refs, land_refs = refs[:ns], refs[ns:2 * ns]
        pos = 2 * ns + (1 if after is not None else 0)
        send_sems, recv_sems, token = refs[pos], refs[pos + 1], refs[-1]
        x, y, c = _place()
        my_idx = 4 * x + 2 * y + c
        for col, peer, peer_idx in _peers(x, y, c):
            for b in range(ns):
                pltpu.make_async_remote_copy(
                    src_ref=src_refs[b].at[peer_idx] if scatter else src_refs[b], dst_ref=land_refs[b].at[my_idx],
                    send_sem=send_sems.at[b * (N_DEV - 1) + col], recv_sem=recv_sems.at[b * (N_DEV - 1) + col],
                    device_id=peer, device_id_type=pl.DeviceIdType.MESH).start()
        token[...] = jnp.zeros_like(token)

    args = [pltpu.with_memory_space_constraint(a, pltpu.HBM) for a in list(srcs) + lands]
    in_specs = [_HBM] * (2 * ns)
    if after is not None:
        args.append(after)
        in_specs.append(pl.BlockSpec(memory_space=pl.ANY))
    out = pl.pallas_call(
        body, name=name, in_specs=in_specs,
        out_specs=[_SEM, _SEM] + [_HBM] * (2 * ns) + [pl.BlockSpec(memory_space=pltpu.VMEM)],
        out_shape=[pltpu.SemaphoreType.DMA((ns * (N_DEV - 1),)), pltpu.SemaphoreType.DMA((ns * (N_DEV - 1),))]
        + [pltpu.HBM(a.shape, a.dtype) for a in list(srcs) + lands] + [jax.ShapeDtypeStruct((8, SLOT), F32)],
        input_output_aliases={k: 2 + k for k in range(2 * ns)},
        compiler_params=pltpu.CompilerParams(has_side_effects=pltpu.SideEffectType.DATAFLOW_SIDE_EFFECTING),
    )(*args)
    return (out[0], out[1], out[2:2 + ns], out[2 + ns:2 + 2 * ns], scatter), out[-1]


def _exchange_wait(handle, after, name):
    send_sems, recv_sems, srcs, lands, scatter = handle
    ns = len(srcs)

    def body(*refs):
        src_refs, land_refs = refs[:ns], refs[ns:2 * ns]
        send_ref, recv_ref = refs[2 * ns], refs[2 * ns + 1]
        x, y, c = _place()
        for col, peer, peer_idx in _peers(x, y, c):
            for b in range(ns):
                copy = pltpu.make_async_remote_copy(
                    src_ref=src_refs[b].at[peer_idx] if scatter else src_refs[b], dst_ref=land_refs[b].at[peer_idx],
                    send_sem=send_ref.at[b * (N_DEV - 1) + col], recv_sem=recv_ref.at[b * (N_DEV - 1) + col],
                    device_id=peer, device_id_type=pl.DeviceIdType.MESH)
                copy.wait_send()
                copy.wait_recv()

    out = pl.pallas_call(
        body, name=name, in_specs=[_HBM] * (2 * ns) + [_SEM, _SEM, pl.BlockSpec(memory_space=pl.ANY)],
        out_specs=[_HBM] * (2 * ns),
        out_shape=[pltpu.HBM(a.shape, a.dtype) for a in list(srcs) + list(lands)],
        input_output_aliases={k: k for k in range(2 * ns)},
        compiler_params=pltpu.CompilerParams(has_side_effects=pltpu.SideEffectType.DATAFLOW_SIDE_EFFECTING),
    )(*srcs, *lands, send_sems, recv_sems, after)
    my_idx = 4 * lax.axis_index("x") + 2 * lax.axis_index("y") + lax.axis_index("c")
    landed = []
    for src, land in zip(out[:ns], out[ns:]):
        own = lax.dynamic_index_in_dim(src, my_idx, 0, keepdims=True) if scatter else src[None]
        landed.append(lax.dynamic_update_index_in_dim(land, own, my_idx, 0))
    return landed


def _adamw(parts, w, m, v, name):
    lyr, rows, cols = w.shape
    assert len(parts) == lyr
    tr = ADAM_ROWS if cols > 512 else 2 * ADAM_ROWS
    while rows % tr:
        tr //= 2
    tr = min(tr, rows)

    def body(*refs):
        p_refs = refs[:lyr]
        w_ref, m_ref, v_ref, g_out, d_out, m_out, v_out = refs[lyr:]
        for k in range(lyr):
            @pl.when(pl.program_id(0) == k)
            def _(p_ref=p_refs[k]):
                g = p_ref[0].astype(F32)
                for s in range(1, N_DEV):
                    g = g + p_ref[s].astype(F32)
                m2 = ADAM_B1 * m_ref[...] + (1.0 - ADAM_B1) * g
                v2 = ADAM_B2 * v_ref[...] + (1.0 - ADAM_B2) * (g * g)
                m_hat = m2 / (1.0 - ADAM_B1 ** ADAM_STEP)
                v_hat = v2 / (1.0 - ADAM_B2 ** ADAM_STEP)
                g_out[...] = g
                d_out[...] = -ADAM_LR * (m_hat / (jnp.sqrt(v_hat) + ADAM_EPS) + ADAM_WD * w_ref[...])
                m_out[...] = m2
                v_out[...] = v2

    def part_spec(k):
        return pl.BlockSpec((N_DEV, tr, cols), lambda l, i: (0, jnp.where(l == k, i, 0), 0))

    spec = pl.BlockSpec((None, tr, cols), lambda l, i: (l, i, 0))
    shp = jax.ShapeDtypeStruct((lyr, rows, cols), F32)
    return pl.pallas_call(
        body, name=name, grid=(lyr, rows // tr),
        in_specs=[part_spec(k) for k in range(lyr)] + [spec, spec, spec],
        out_specs=[spec] * 4, out_shape=[shp] * 4, compiler_params=_cparams(),
    )(*parts, w, m, v)


def _pack(arrays, lanes, row_mult, dtype):
    flat = jnp.concatenate([a.reshape(-1).astype(dtype) for a in arrays])
    unit = lanes * row_mult
    total = -(-flat.shape[0] // unit) * unit
    return jnp.pad(flat, (0, total - flat.shape[0])).reshape(total // lanes, lanes)


def _unpack(packed, shapes):
    flat = packed.reshape(-1)
    out, off = [], 0
    for shp in shapes:
        n = 1
        for d in shp:
            n *= d
        out.append(flat[off:off + n].reshape(shp))
        off += n
    return out


def _pad_slots(w, axis):
    axis = axis % w.ndim
    n = w.shape[axis] // HEAD_DIM
    shp = w.shape[:axis] + (n, HEAD_DIM) + w.shape[axis + 1:]
    pad = [(0, 0)] * (w.ndim + 1)
    pad[axis + 1] = (0, SLOT - HEAD_DIM)
    return jnp.pad(w.reshape(shp), pad).reshape(w.shape[:axis] + (n * SLOT,) + w.shape[axis + 1:])


def _unpad_slots(w, axis, keep=HEAD_DIM):
    axis = axis % w.ndim
    n = w.shape[axis] // SLOT
    shp = w.shape[:axis] + (n, SLOT) + w.shape[axis + 1:]
    idx = [slice(None)] * (w.ndim + 1)
    idx[axis + 1] = slice(0, keep)
    return w.reshape(shp)[tuple(idx)].reshape(w.shape[:axis] + (n * keep,) + w.shape[axis + 1:])


def _mla_in_pad(w):
    z = functools.partial(jnp.zeros, dtype=w.dtype)
    rows = w.shape[0]
    return jnp.concatenate([w[:, :384], z((rows, 64)), w[:, 640:672], z((rows, 32)), w[:, 384:640],
                            _pad_slots(w[:, 672:], 1)], axis=1)


def _mla_in_unpad(d):
    return jnp.concatenate([d[:, :384], d[:, 512:768], d[:, 448:480], _unpad_slots(d[:, 768:], 1)], axis=1)


def _mla_uq_pad(w):
    return jnp.pad(w.reshape(w.shape[0], MLA_HEADS, MLA_QK), ((0, 0), (0, 0), (0, SLOT - MLA_QK))).reshape(
        w.shape[0], MLA_HEADS * SLOT)


def _join(gathered, axis):
    nd, a, b = gathered.shape
    if axis == 1:
        return gathered.reshape(nd * a, b)
    return gathered.transpose(1, 0, 2).reshape(a, nd * b)


def _split(full, axis):
    r, c = full.shape
    if axis == 1:
        return full.reshape(N_DEV, r // N_DEV, c).astype(BF16)
    return full.reshape(r, N_DEV, c // N_DEV).transpose(1, 0, 2).astype(BF16)


def kernel(x, mem, positions, attn_norm_g, mlp_norm_g, mem_norm_g, final_norm_g, mla_w_in, mla_q_norm_g, mla_kv_norm_g, mla_w_uq, mla_w_ukv, swa_w_in, swa_sinks, w_mem_kv, w_o, mlp_w_up, mlp_w_down, loss_target, m_attn_norm_g, m_mlp_norm_g, m_mem_norm_g, m_final_norm_g, m_mla_w_in, m_mla_q_norm_g, m_mla_kv_norm_g, m_mla_w_uq, m_mla_w_ukv, m_swa_w_in, m_swa_sinks, m_w_mem_kv, m_w_o, m_mlp_w_up, m_mlp_w_down, v_attn_norm_g, v_mlp_norm_g, v_mem_norm_g, v_final_norm_g, v_mla_w_in, v_mla_q_norm_g, v_mla_kv_norm_g, v_mla_w_uq, v_mla_w_ukv, v_swa_w_in, v_swa_sinks, v_w_mem_kv, v_w_o, v_mlp_w_up, v_mlp_w_down):
    given = dict(locals())
    seq = x.shape[1]
    x0 = x.reshape(seq, D_MODEL)
    tgt = loss_target.reshape(seq, D_MODEL)
    mem0 = mem.reshape(N_MEM, D_MODEL)
    pos = positions.reshape(seq).astype(F32)
    pos_col, pos_row = pos.reshape(seq, 1), pos.reshape(1, seq)

    def layer_names(i):
        mixer = ("mla_w_in", "mla_w_uq", "mla_w_ukv") if i % 2 == 0 else ("swa_w_in",)
        return [(n, i // 2) for n in mixer] + [(n, i) for n in ("w_mem_kv", "w_o", "mlp_w_up", "mlp_w_down")]

    def local_weights(names):
        return [given[n][l].astype(BF16) for n, l in names]

    first_attn, first_mlp = layer_names(0)[:-2], layer_names(0)[-2:]
    weights = [dict(zip([n for n, _ in first_attn], _all_gather(local_weights(first_attn), "gather_weights_first")))]
    coming_mlp, first_token = _exchange_start(local_weights(first_mlp), False, "gather_weights_start_0",
                                              after=weights[0]["w_o"])

    consts = _lane_consts()
    tabs = _rope_tables(pos_col, consts)
    slopes = 2.0 ** (-8.0 * (jnp.arange(SWA_HEADS, dtype=F32) + 1.0) / SWA_HEADS)

    mem_n = _rmsnorm_fwd(mem0, 0, D_MODEL, mem_norm_g, "rmsnorm_fwd_mem")

    saved = []
    xc = x0
    for i in range(DEPTH):
        j = i // 2
        wts = weights[i]
        s = {"x_in": xc}
        token = None
        if i + 1 < DEPTH:
            coming, token = _exchange_start(local_weights(layer_names(i + 1)), False,
                                            "gather_weights_start_%d" % (i + 1),
                                            after=first_token if i == 0 else wts["w_o"])
        hn = _rmsnorm_fwd(xc, 0, D_MODEL, attn_norm_g[i], "rmsnorm_fwd", after=token)
        if i % 2 == 0:
            w_in = _mla_in_pad(_join(wts["mla_w_in"], 1))
            w_uq = _mla_uq_pad(_join(wts["mla_w_uq"], 2))
            w_kv = _join(wts["mla_w_ukv"], 2)
            proj = _mm(hn, w_in, "nn", F32, "mm_mla_in")
            cqn = _rmsnorm_fwd(proj, 0, MLA_Q_RANK, mla_q_norm_g[j], "rmsnorm_fwd_q")
            ckvn = _rmsnorm_fwd(proj, 2, MLA_KV_RANK, mla_kv_norm_g[j], "rmsnorm_fwd_kv")
            qraw = _mm(cqn, w_uq, "nn", F32, "mm_mla_uq")
            kvraw = _mm(ckvn, w_kv, "nn", F32, "mm_mla_ukv")
            q, k, v = _mla_rope_fwd(qraw, kvraw, proj, tabs)
            o, lse = _mla_attn_fwd(q, k, v)
            qoff = MLA_QOFF
            s.update(w_uq=w_uq, w_kv=w_kv, cqn=cqn, ckvn=ckvn, q=q, k=k, v=v)
        else:
            w_in = _join(wts["swa_w_in"], 2)
            proj = _mm(hn, w_in, "nn", BF16, "mm_swa_in", pairs="o")
            o, lse = _swa_attn_fwd(proj, pos_col, pos_row, slopes, swa_sinks[j])
            qoff = SWA_QOFF
        w_mem = _pad_slots(_join(wts["w_mem_kv"], 1), 1)
        w_out = _join(wts["w_o"], 1)
        w_o_mix, w_o_cross = w_out[:SWA_HEADS * HEAD_DIM], w_out[SWA_HEADS * HEAD_DIM:]
        kvmem = _mm(mem_n, w_mem, "nn", BF16, "mm_mem_kv")
        cross = _cross_attn_fwd(proj, qoff, kvmem)
        x1 = _mm(o, w_o_mix, "nn", F32, "mm_o_mix", res=xc, pairs="a")
        x1 = _mm(cross, w_o_cross, "nn", F32, "mm_o_cross", res=x1, pairs="a")
        hn2 = _rmsnorm_fwd(x1, 0, D_MODEL, mlp_norm_g[i], "rmsnorm_fwd")
        if i == 0:
            wts.update(zip([n for n, _ in first_mlp], _exchange_wait(coming_mlp, hn2, "gather_weights_wait_0")))
        act, act2 = _mm(hn2, wts["mlp_w_up"], "nn", BF16, "mm_mlp_up", epi="relu2", b_blk="cols")
        xc = _mm(act2, wts["mlp_w_down"], "nn", F32, "mm_mlp_down", res=x1, b_blk="rows")
        s.update(hn=hn, w_in=w_in, proj=proj, o=o, lse=lse, qoff=qoff, w_mem=w_mem, w_o_mix=w_o_mix,
                 w_o_cross=w_o_cross, kvmem=kvmem, cross=cross, x1=x1, hn2=hn2, act=act, act2=act2)
        saved.append(s)
        if i + 1 < DEPTH:
            got = _exchange_wait(coming, xc, "gather_weights_wait_%d" % (i + 1))
            weights.append(dict(zip([n for n, _ in layer_names(i + 1)], got)))

    dx, dx_b, dg_final, loss_part = _loss_head(xc, final_norm_g, tgt)
    loss = lax.psum(loss_part[0, 0], MESH_AXES)

    gains = {n: [None] * DEPTH for n in ("attn_norm_g", "mlp_norm_g")}
    for n in ("mla_q_norm_g", "mla_kv_norm_g", "swa_sinks"):
        gains[n] = [None] * 2
    leaving = {}
    token = None
    dmem_n = None
    for i in reversed(range(DEPTH)):
        j = i // 2
        s = saved[i]
        wts = weights[i]
        out = {}
        du = _mm(dx_b, wts["mlp_w_down"], "nt", BF16, "mm_mlp_down_dx", aux=s["act"], epi="mul2aux", b_blk="rows",
                 after=token)
        out["mlp_w_down"] = _mm(s["act2"], dx_b, "tn", BF16, "mm_mlp_down_dw", o_blk="rows")
        out["mlp_w_up"] = _mm(s["hn2"], du, "tn", BF16, "mm_mlp_up_dw", o_blk="cols")
        dx1, dx1_b, dg = _mm(du, wts["mlp_w_up"], "nt", F32, "mm_mlp_up_dx", b_blk="cols",
                             epi="normbwd", norm=(s["x1"], mlp_norm_g[i], dx))
        gains["mlp_norm_g"][i] = dg[0]

        do = _mm(dx1_b, s["w_o_mix"], "nt", BF16, "mm_o_mix_dx", pairs="o")
        dcross = _mm(dx1_b, s["w_o_cross"], "nt", BF16, "mm_o_cross_dx", pairs="o")
        dw_o = jnp.concatenate([_mm(s["o"], dx1_b, "tn", F32, "mm_o_mix_dw", pairs="a"),
                                _mm(s["cross"], dx1_b, "tn", F32, "mm_o_cross_dw", pairs="a")], axis=0)
        out["w_o"] = _split(dw_o, 1)
        dqc, dkm, dvm = _cross_attn_bwd(s["proj"], s["qoff"], s["kvmem"], dcross)
        dkvmem = jnp.concatenate([dkm, dvm], axis=1).astype(BF16)
        out["w_mem_kv"] = _split(_unpad_slots(_mm(mem_n, dkvmem, "tn", F32, "mm_mem_kv_dw"), 1), 1)
        dmem_n = _mm(dkvmem, s["w_mem"], "nt", F32, "mm_mem_kv_dx" if dmem_n is None else "mm_mem_kv_dx_acc",
                     res=dmem_n)
        leaving[(i, "main")], token = _exchange_start([out[n] for n, _ in layer_names(i)[-4:]], True,
                                                      "exchange_grads_main_start_%d" % i)

        if i % 2 == 0:
            dq, dk, dv = _mla_attn_bwd(s["q"], s["k"], s["v"], s["o"], do, s["lse"], token)
            dqraw, dkv, dkr = _mla_rope_bwd(dq, dk, dv, tabs, consts)
            dcqn = _mm(dqraw, s["w_uq"], "nt", F32, "mm_mla_uq_dx")
            out["mla_w_uq"] = _split(_unpad_slots(_mm(s["cqn"], dqraw, "tn", F32, "mm_mla_uq_dw"), 1, MLA_QK), 2)
            dckvn = _mm(dkv, s["w_kv"], "nt", F32, "mm_mla_ukv_dx")
            out["mla_w_ukv"] = _split(_mm(s["ckvn"], dkv, "tn", F32, "mm_mla_ukv_dw"), 2)
            dcq, dg = _rmsnorm_bwd(s["proj"], 0, MLA_Q_RANK, mla_q_norm_g[j], dcqn, None, BF16, "rmsnorm_bwd_q")
            gains["mla_q_norm_g"][j] = dg[0]
            dckv, dg = _rmsnorm_bwd(s["proj"], 2, MLA_KV_RANK, mla_kv_norm_g[j], dckvn, None, BF16, "rmsnorm_bwd_kv")
            gains["mla_kv_norm_g"][j] = dg[0]
            dproj = jnp.concatenate([dcq, dkr.astype(BF16), dckv, dqc.astype(BF16)], axis=1)
            in_dx = "mm_mla_in_dx"
            out["mla_w_in"] = _split(_mla_in_unpad(_mm(s["hn"], dproj, "tn", F32, "mm_mla_in_dw")), 1)
        else:
            dq, dk, dv, dsink = _swa_attn_bwd(s["proj"], s["o"], do, s["lse"], pos_col, pos_row, slopes, swa_sinks[j],
                                              token)
            gains["swa_sinks"][j] = dsink[::8, 0]
            dproj = jnp.concatenate([dq, dk, dv, dqc], axis=1).astype(BF16)
            in_dx = "mm_swa_in_dx"
            out["swa_w_in"] = _split(_mm(s["hn"], dproj, "tn", F32, "mm_swa_in_dw", pairs="b"), 2)
        dx, dx_b, dg = _mm(dproj, s["w_in"], "nt", F32, in_dx, epi="normbwd", norm=(s["x_in"], attn_norm_g[i], dx1),
                           pairs="" if i % 2 == 0 else "a")
        gains["attn_norm_g"][i] = dg[0]

        leaving[(i, "mixer")], token = _exchange_start([out[n] for n, _ in layer_names(i)[:-4]], True,
                                                       "exchange_grads_mixer_start_%d" % i)

    _, dg_mem = _rmsnorm_bwd(mem0, 0, D_MODEL, mem_norm_g, dmem_n, None, BF16, "rmsnorm_bwd_mem")
    gains = {n: jnp.stack(g) for n, g in gains.items()}
    gains["mem_norm_g"] = dg_mem[0]
    gains["final_norm_g"] = dg_final[0]

    result = {}

    def adamw_of(names, received):
        for n in names:
            parts = [received[(n, l)] for l in range(given[n].shape[0])]
            for kind, r in enumerate(_adamw(parts, given[n], given["m_" + n], given["v_" + n], "adamw_" + n)):
                result[(kind, n)] = r

    received = {}
    for i in reversed(range(DEPTH)):
        got = _exchange_wait(leaving[(i, "main")], dx, "exchange_grads_main_wait_%d" % i)
        received.update(zip(layer_names(i)[-4:], got))
    adamw_of(("mlp_w_up", "mlp_w_down", "w_o", "w_mem_kv"), received)
    for i in reversed(range(DEPTH)):
        got = _exchange_wait(leaving[(i, "mixer")], result[(0, "w_mem_kv")], "exchange_grads_mixer_wait_%d" % i)
        received.update(zip(layer_names(i)[:-4], got))
    adamw_of(("mla_w_in", "mla_w_uq", "mla_w_ukv", "swa_w_in"), received)

    rep_shapes = [given[n].shape for n in REPLICATED]
    rep_parts = _all_gather([_pack([gains[n] for n in REPLICATED], SLOT, 8, F32)], "gather_gain_grads")[0]
    rep_packed = [_pack([given[p + n] for n in REPLICATED], SLOT, 8, F32)[None] for p in ("", "m_", "v_")]
    for kind, r in enumerate(_adamw([rep_parts], *rep_packed, "adamw_gains")):
        for n, part in zip(REPLICATED, _unpack(r[0], rep_shapes)):
            result[(kind, n)] = part

    outs = [loss, dx.reshape(1, seq, D_MODEL)]
    for kind in range(4):
        outs += [result[(kind, n)] for n in WEIGHT_ORDER]
    return tuple(outs)
```

```python
import functools

import jax
import jax.numpy as jnp
from jax import lax
from jax.experimental import pallas as pl
from jax.experimental.pallas import tpu as pltpu

F32 = jnp.float32
BF16 = jnp.bfloat16

D_MODEL = 1024
D_FF = 4096
N_MEM = 256
DEPTH = 4
SLOT = 128
HEAD_DIM = 64
MLA_HEADS = 12
MLA_QK = 96
MLA_Q_RANK = 384
MLA_KV_RANK = 256
SWA_HEADS = 12
SWA_KV_HEADS = 4
SWA_GROUP = 3
MEM_HEADS = 4
WINDOW = 128
EPS = 1e-6
NEG = -1e30
ROPE_THETA = 10000.0
N_DEV = 8

ADAM_LR = 0.001
ADAM_B1 = 0.9
ADAM_B2 = 0.999
ADAM_EPS = 1e-08
ADAM_WD = 0.01
ADAM_STEP = 10

TM = 512
TQ_MLA = 1024
MLA_PACK = 2
SWA_PACK = 4
TQ_CROSS = 2048
MM_VMEM_BUDGET = 38 * 1024 * 1024
ADAM_ROWS = 128
VMEM_LIMIT = 56 * 1024 * 1024

MESH_AXES = ("x", "y", "c")

LOG2_E = 1.4426950408889634
MLA_SCALE = MLA_QK ** -0.5
MLA_Q_SCALE = MLA_SCALE * LOG2_E

MLA_PAD_IN = 384 + SLOT + 256 + MEM_HEADS * SLOT
MLA_QOFF = (384 + SLOT + 256) // SLOT
SWA_PAD_IN = (SWA_HEADS + 2 * SWA_KV_HEADS + MEM_HEADS) * SLOT
SWA_QOFF = SWA_HEADS + 2 * SWA_KV_HEADS

SHARDED = (
    ("mla_w_in", 1), ("mla_w_uq", 2), ("mla_w_ukv", 2), ("swa_w_in", 2),
    ("w_mem_kv", 1), ("w_o", 1), ("mlp_w_up", 2), ("mlp_w_down", 1),
)
REPLICATED = ("attn_norm_g", "mlp_norm_g", "mem_norm_g", "final_norm_g",
              "mla_q_norm_g", "mla_kv_norm_g", "swa_sinks")
WEIGHT_ORDER = ("attn_norm_g", "mlp_norm_g", "mem_norm_g", "final_norm_g", "mla_w_in",
                "mla_q_norm_g", "mla_kv_norm_g", "mla_w_uq", "mla_w_ukv", "swa_w_in",
                "swa_sinks", "w_mem_kv", "w_o", "mlp_w_up", "mlp_w_down")


def _cparams():
    return pltpu.CompilerParams(vmem_limit_bytes=VMEM_LIMIT)


_DIMS = {"nn": (((1,), (0,)), ((), ())), "nt": (((1,), (1,)), ((), ())), "tn": (((0,), (0,)), ((), ()))}


def _compact(x):
    pairs = [x[:, 2 * j * SLOT:(2 * j + 1) * SLOT] + pltpu.roll(x[:, (2 * j + 1) * SLOT:(2 * j + 2) * SLOT], HEAD_DIM, 1)
             for j in range(x.shape[1] // (2 * SLOT))]
    return pairs[0] if len(pairs) == 1 else jnp.concatenate(pairs, axis=1)


def _expand(x):
    low = lax.broadcasted_iota(jnp.int32, (x.shape[0], SLOT), 1) < HEAD_DIM
    slots = []
    for j in range(x.shape[1] // SLOT):
        pair = x[:, j * SLOT:(j + 1) * SLOT]
        slots += [jnp.where(low, pair, 0.0), pltpu.roll(jnp.where(low, 0.0, pair), HEAD_DIM, 1)]
    return jnp.concatenate(slots, axis=1)


def _mm_tiles(m, n, k, a_bytes, b_bytes, o_bytes, extra_bytes, tm_fixed, tn_fixed):
    best = None
    for tm in ([tm_fixed] if tm_fixed else [t for t in range(4096, 0, -SLOT) if m % t == 0] or [m]):
        for tn in ([tn_fixed] if tn_fixed else [t for t in range(1024, 0, -SLOT) if n % t == 0] or [n]):
            need = 2 * (tm * k * a_bytes + k * tn * b_bytes + tm * tn * (o_bytes + extra_bytes))
            need += tm * tn * 4
            if need <= MM_VMEM_BUDGET and (best is None or tm * tn > best[0] * best[1]):
                best = (tm, tn)
    assert best is not None, (m, n, k)
    return best


def _mm(a, b, mode, out_dtype, name, res=None, aux=None, epi=None, b_blk=None, o_blk=None, after=None, norm=None,
        pairs=""):
    if b_blk is not None:
        nb, br, bc = b.shape
        b_shape = (nb * br, bc) if b_blk == "rows" else (br, nb * bc)
    else:
        b_shape = b.shape
    assert not pairs or (b_blk is None and o_blk is None and not ("b" in pairs and mode == "nt"))
    a_shape = (a.shape[0], a.shape[1] // 2) if "a" in pairs else a.shape
    if "b" in pairs:
        b_shape = (b_shape[0], b_shape[1] // 2)
    if mode == "nn":
        (m, k), (k2, n) = a_shape, b_shape
    elif mode == "nt":
        (m, k), (n, k2) = a_shape, b_shape
    else:
        (k, m), (k2, n) = a_shape, b_shape
    assert k == k2, (a.shape, b_shape, mode)
    k_blocked = b_blk is not None and (b_blk == "rows") == (mode != "nt")
    tn_fixed = None
    if b_blk is not None and not k_blocked:
        tn_fixed = br if b_blk == "rows" else bc
    if o_blk == "cols":
        tn_fixed = n // N_DEV
    tm_fixed = m // N_DEV if o_blk == "rows" else None
    has_res, has_aux, has_norm = res is not None, aux is not None, epi == "normbwd"
    assert o_blk is None or not (has_res or has_aux or has_norm)
    n_out = 2 if epi == "relu2" else 1
    if has_norm:
        tn_fixed = n
        o_bytes, extra_bytes = 4 + 2, 4 + 4
    else:
        o_bytes = n_out * jnp.dtype(out_dtype).itemsize
        extra_bytes = (4 if has_res else 0) + (aux.dtype.itemsize if has_aux else 0)
    pa, pb, po = (2 if "a" in pairs else 1), (2 if "b" in pairs else 1), (2 if "o" in pairs else 1)
    tm, tn = _mm_tiles(m, n, k, a.dtype.itemsize * (3 if pa == 2 else 1), b.dtype.itemsize * (3 if pb == 2 else 1),
                       o_bytes * po, extra_bytes, tm_fixed, tn_fixed)
    dims = _DIMS[mode]
    if mode == "tn":
        a_spec = pl.BlockSpec((k, pa * tm), lambda i, j: (0, i))
    else:
        a_spec = pl.BlockSpec((tm, pa * k), lambda i, j: (i, 0))
    if b_blk is None:
        if mode == "nt":
            b_spec = pl.BlockSpec((tn, k), lambda i, j: (j, 0))
        else:
            b_spec = pl.BlockSpec((k, pb * tn), lambda i, j: (0, j))
    elif k_blocked and mode == "nt":
        b_spec = pl.BlockSpec((N_DEV, tn, bc), lambda i, j: (0, j, 0))
    elif k_blocked:
        b_spec = pl.BlockSpec((N_DEV, br, tn), lambda i, j: (0, 0, j))
    elif mode == "nt":
        b_spec = pl.BlockSpec((None, tn, k), lambda i, j: (j, 0, 0))
    else:
        b_spec = pl.BlockSpec((None, k, tn), lambda i, j: (j, 0, 0))
    if o_blk is None:
        o_spec = pl.BlockSpec((tm, po * tn), lambda i, j: (i, j))
        o_shape = (m, po * n)
    elif o_blk == "rows":
        o_spec = pl.BlockSpec((None, tm, tn), lambda i, j: (i, 0, j))
        o_shape = (N_DEV, tm, n)
    else:
        o_spec = pl.BlockSpec((None, tm, tn), lambda i, j: (j, i, 0))
        o_shape = (N_DEV, m, tn)

    def body(*refs):
        a_ref, b_ref = refs[0], refs[1]
        pos = 2
        res_ref = aux_ref = None
        if has_res:
            res_ref = refs[pos]
            pos += 1
        if has_aux:
            aux_ref = refs[pos]
            pos += 1
        if has_norm:
            x_ref, g_ref, dres_ref = refs[pos:pos + 3]
            pos += 3
        if after is not None:
            pos += 1
        outs = refs[pos:]
        if k_blocked and mode == "nt":
            r = None
            for d in range(N_DEV):
                part = lax.dot_general(a_ref[:, d * bc:(d + 1) * bc].astype(BF16), b_ref[d].astype(BF16), dims,
                                       preferred_element_type=F32)
                r = part if r is None else r + part
        else:
            bv = b_ref[...].reshape(k, tn) if k_blocked else b_ref[...]
            av = _compact(a_ref[...].astype(F32)) if pa == 2 else a_ref[...]
            bv = _compact(bv.astype(F32)) if pb == 2 else bv
            r = lax.dot_general(av.astype(BF16), bv.astype(BF16), dims, preferred_element_type=F32)
        if po == 2:
            r = _expand(r)
        if epi == "relu2":
            r = jnp.maximum(r, 0.0)
            outs[0][...] = r.astype(outs[0].dtype)
            outs[1][...] = (r * r).astype(outs[1].dtype)
        elif has_norm:
            xv = x_ref[...]
            rs = lax.rsqrt(jnp.mean(xv * xv, axis=1, keepdims=True) + EPS)
            xh = xv * rs
            dxh = r * g_ref[...]
            dx = rs * (dxh - xh * jnp.mean(dxh * xh, axis=1, keepdims=True)) + dres_ref[...]
            outs[0][...] = dx
            outs[1][...] = dx.astype(BF16)

            @pl.when(pl.program_id(0) == 0)
            def _():
                outs[2][...] = jnp.zeros_like(outs[2])

            outs[2][...] += jnp.sum(r * xh, axis=0, keepdims=True)
        else:
            if epi == "mul2aux":
                r = r * (2.0 * aux_ref[...].astype(F32))
            if has_res:
                r = r + res_ref[...]
            outs[0][...] = r.astype(outs[0].dtype)

    in_specs = [a_spec, b_spec]
    args = [a, b]
    if has_res:
        in_specs.append(o_spec)
        args.append(res)
    if has_aux:
        in_specs.append(o_spec)
        args.append(aux)
    vec_spec = pl.BlockSpec((1, n), lambda i, j: (0, 0))
    if has_norm:
        in_specs += [o_spec, vec_spec, o_spec]
        args += [norm[0], norm[1].reshape(1, n), norm[2]]
    if after is not None:
        in_specs.append(pl.BlockSpec(memory_space=pl.ANY))
        args.append(after)
    if has_norm:
        out_specs = [o_spec, o_spec, vec_spec]
        out_shape = [jax.ShapeDtypeStruct(o_shape, F32), jax.ShapeDtypeStruct(o_shape, BF16),
                     jax.ShapeDtypeStruct((1, n), F32)]
    else:
        out_specs = [o_spec] * n_out
        out_shape = [jax.ShapeDtypeStruct(o_shape, out_dtype)] * n_out
    out = pl.pallas_call(
        body, name=name, grid=(m // tm, n // tn),
        in_specs=in_specs, out_specs=out_specs, out_shape=out_shape, compiler_params=_cparams(),
    )(*args)
    return out if len(out) > 1 else out[0]


def _rmsnorm_fwd(xarr, colblk, width, g, name, after=None):
    rows = xarr.shape[0]
    tm = min(TM, rows)

    def body(x_ref, g_ref, *rest):
        y_ref = rest[-1]
        x = x_ref[...].astype(F32)
        r = lax.rsqrt(jnp.mean(x * x, axis=1, keepdims=True) + EPS)
        y_ref[...] = (x * r * g_ref[...]).astype(y_ref.dtype)

    in_specs = [pl.BlockSpec((tm, width), lambda i: (i, colblk)), pl.BlockSpec((1, width), lambda i: (0, 0))]
    args = [xarr, g.reshape(1, width)]
    if after is not None:
        in_specs.append(pl.BlockSpec(memory_space=pl.ANY))
        args.append(after)
    return pl.pallas_call(
        body, name=name, grid=(rows // tm,), in_specs=in_specs,
        out_specs=pl.BlockSpec((tm, width), lambda i: (i, 0)),
        out_shape=jax.ShapeDtypeStruct((rows, width), BF16), compiler_params=_cparams(),
    )(*args)


def _rmsnorm_bwd(xarr, colblk, width, g, dy, dres, out_dtype, name):
    rows = xarr.shape[0]
    tm = min(TM, rows)
    has_res = dres is not None

    def body(*refs):
        x_ref, g_ref, dy_ref = refs[0], refs[1], refs[2]
        dres_ref = refs[3] if has_res else None
        dx_ref, dg_ref = refs[-2], refs[-1]
        x = x_ref[...].astype(F32)
        dyv = dy_ref[...].astype(F32)
        r = lax.rsqrt(jnp.mean(x * x, axis=1, keepdims=True) + EPS)
        xh = x * r
        dxh = dyv * g_ref[...]
        dx = r * (dxh - xh * jnp.mean(dxh * xh, axis=1, keepdims=True))
        if has_res:
            dx = dx + dres_ref[...]
        dx_ref[...] = dx.astype(dx_ref.dtype)

        @pl.when(pl.program_id(0) == 0)
        def _():
            dg_ref[...] = jnp.zeros_like(dg_ref)

        dg_ref[...] += jnp.sum(dyv * xh, axis=0, keepdims=True)

    row_spec = pl.BlockSpec((tm, width), lambda i: (i, 0))
    vec_spec = pl.BlockSpec((1, width), lambda i: (0, 0))
    in_specs = [pl.BlockSpec((tm, width), lambda i: (i, colblk)), vec_spec, row_spec]
    args = [xarr, g.reshape(1, width), dy]
    if has_res:
        in_specs.append(row_spec)
        args.append(dres)
    return pl.pallas_call(
        body, name=name, grid=(rows // tm,), in_specs=in_specs, out_specs=[row_spec, vec_spec],
        out_shape=[jax.ShapeDtypeStruct((rows, width), out_dtype), jax.ShapeDtypeStruct((1, width), F32)],
        compiler_params=_cparams(),
    )(*args)


def _loss_head(x, g, tgt):
    rows, width = x.shape
    tm = min(TM, rows)

    def body(x_ref, g_ref, t_ref, dx_ref, dxb_ref, dg_ref, loss_ref):
        xv = x_ref[...]
        gv = g_ref[...]
        r = lax.rsqrt(jnp.mean(xv * xv, axis=1, keepdims=True) + EPS)
        xh = xv * r
        err = xh * gv - t_ref[...]
        part = 0.5 * jnp.sum(jnp.mean(err * err, axis=1, keepdims=True), axis=0, keepdims=True)
        dyv = err * (1.0 / width)
        dxh = dyv * gv
        dxv = r * (dxh - xh * jnp.mean(dxh * xh, axis=1, keepdims=True))
        dx_ref[...] = dxv
        dxb_ref[...] = dxv.astype(BF16)

        @pl.when(pl.program_id(0) == 0)
        def _():
            dg_ref[...] = jnp.zeros_like(dg_ref)
            loss_ref[...] = jnp.zeros_like(loss_ref)

        dg_ref[...] += jnp.sum(dyv * xh, axis=0, keepdims=True)
        loss_ref[...] += jnp.broadcast_to(part, loss_ref.shape)

    row_spec = pl.BlockSpec((tm, width), lambda i: (i, 0))
    vec_spec = pl.BlockSpec((1, width), lambda i: (0, 0))
    return pl.pallas_call(
        body, name="loss_head", grid=(rows // tm,), in_specs=[row_spec, vec_spec, row_spec],
        out_specs=[row_spec, row_spec, vec_spec, pl.BlockSpec((1, SLOT), lambda i: (0, 0))],
        out_shape=[jax.ShapeDtypeStruct((rows, width), F32), jax.ShapeDtypeStruct((rows, width), BF16),
                   jax.ShapeDtypeStruct((1, width), F32), jax.ShapeDtypeStruct((1, SLOT), F32)],
        compiler_params=_cparams(),
    )(x, g.reshape(1, width), tgt)


def _lane_consts():
    half = 16
    inv = ROPE_THETA ** (-(jnp.arange(half, dtype=F32) * 2.0) / 32)
    lane = jnp.arange(SLOT)
    first = (lane >= 64) & (lane < 80)
    second = (lane >= 80) & (lane < 96)
    inv_lane = jnp.where(first | second, inv[(lane - 64) % half], 0.0)
    rows = [inv_lane, (lane < 64).astype(F32), first.astype(F32), second.astype(F32)]
    rows += [jnp.zeros((SLOT,), F32)] * 4
    return jnp.stack(rows).astype(F32)


def _rope_tables(pos_col, consts):
    rows = pos_col.shape[0]
    tm = min(TM, rows)

    def body(p_ref, k_ref, c_ref, s1_ref, s2_ref):
        ang = p_ref[...] * k_ref[0:1, :]
        cos, sin = jnp.cos(ang), jnp.sin(ang)
        first, second = k_ref[2:3, :], k_ref[3:4, :]
        c_ref[...] = k_ref[1:2, :] + (first + second) * cos
        s1_ref[...] = -first * sin
        s2_ref[...] = second * sin

    spec = pl.BlockSpec((tm, SLOT), lambda i: (i, 0))
    shp = jax.ShapeDtypeStruct((rows, SLOT), F32)
    return pl.pallas_call(
        body, name="rope_tables", grid=(rows // tm,),
        in_specs=[pl.BlockSpec((tm, 1), lambda i: (i, 0)), pl.BlockSpec((8, SLOT), lambda i: (0, 0))],
        out_specs=[spec, spec, spec], out_shape=[shp, shp, shp], compiler_params=_cparams(),
    )(pos_col, consts)


def _rot(xv, c, s1, s2):
    return xv * c + pltpu.roll(xv, SLOT - 16, 1) * s1 + pltpu.roll(xv, 16, 1) * s2


def _rot_t(dy, c, s1, s2):
    return dy * c + pltpu.roll(dy * s1, 16, 1) + pltpu.roll(dy * s2, SLOT - 16, 1)


def _mla_rope_fwd(qraw, kvraw, proj, tabs):
    rows = qraw.shape[0]
    tm = min(256, rows)
    hw = MLA_HEADS * SLOT

    def body(q_ref, kv_ref, kr_ref, c_ref, s1_ref, s2_ref, qo, ko, vo):
        c, s1, s2 = c_ref[...], s1_ref[...], s2_ref[...]
        kr = _rot(kr_ref[...], c, s1, s2)
        low = lax.broadcasted_iota(jnp.int32, (tm, SLOT), 1) < HEAD_DIM
        for h in range(MLA_HEADS):
            sl = slice(h * SLOT, (h + 1) * SLOT)
            qo[:, sl] = (_rot(q_ref[:, sl], c, s1, s2) * MLA_Q_SCALE).astype(BF16)
            kvh = kv_ref[:, sl]
            ko[:, sl] = (jnp.where(low, kvh, 0.0) + kr).astype(BF16)
            vo[:, sl] = pltpu.roll(jnp.where(low, 0.0, kvh), HEAD_DIM, 1).astype(BF16)

    tab = pl.BlockSpec((tm, SLOT), lambda i: (i, 0))
    wide = pl.BlockSpec((tm, hw), lambda i: (i, 0))
    shp = jax.ShapeDtypeStruct((rows, hw), BF16)
    return pl.pallas_call(
        body, name="mla_rope_fwd", grid=(rows // tm,),
        in_specs=[wide, wide, pl.BlockSpec((tm, SLOT), lambda i: (i, 3)),
                  tab, tab, tab],
        out_specs=[wide, wide, wide], out_shape=[shp, shp, shp], compiler_params=_cparams(),
    )(qraw, kvraw, proj, *tabs)


def _mla_rope_bwd(dq, dk, dv, tabs, consts):
    rows = dq.shape[0]
    tm = min(256, rows)
    hw = MLA_HEADS * SLOT

    def body(dq_ref, dk_ref, dv_ref, c_ref, s1_ref, s2_ref, k_ref, dqo, dkvo, dkro):
        c, s1, s2 = c_ref[...], s1_ref[...], s2_ref[...]
        ksum = jnp.zeros((tm, SLOT), F32)
        low = lax.broadcasted_iota(jnp.int32, (tm, SLOT), 1) < HEAD_DIM
        for h in range(MLA_HEADS):
            sl = slice(h * SLOT, (h + 1) * SLOT)
            dqo[:, sl] = _rot_t(dq_ref[:, sl], c, s1, s2).astype(BF16)
            dkh = dk_ref[:, sl]
            ksum = ksum + dkh
            dvh = pltpu.roll(jnp.where(low, dv_ref[:, sl], 0.0), HEAD_DIM, 1)
            dkvo[:, sl] = (jnp.where(low, dkh, 0.0) + dvh).astype(BF16)
        dkro[...] = _rot_t(ksum, c, s1, s2) * (k_ref[2:3, :] + k_ref[3:4, :])

    tab = pl.BlockSpec((tm, SLOT), lambda i: (i, 0))
    wide = pl.BlockSpec((tm, hw), lambda i: (i, 0))
    return pl.pallas_call(
        body, name="mla_rope_bwd", grid=(rows // tm,),
        in_specs=[wide, wide, wide, tab, tab, tab, pl.BlockSpec((8, SLOT), lambda i: (0, 0))],
        out_specs=[wide, wide, tab],
        out_shape=[jax.ShapeDtypeStruct((rows, hw), BF16), jax.ShapeDtypeStruct((rows, hw), BF16),
                   jax.ShapeDtypeStruct((rows, SLOT), F32)],
        compiler_params=_cparams(),
    )(dq, dk, dv, *tabs, consts)


def _nt(a, b):
    return lax.dot_general(a, b, _DIMS["nt"], preferred_element_type=F32)


def _tn(a, b):
    return lax.dot_general(a, b, _DIMS["tn"], preferred_element_type=F32)


def _nn(a, b):
    return lax.dot_general(a, b, _DIMS["nn"], preferred_element_type=F32)


def _causal(rows, keys):
    shp = (rows.stop - rows.start, keys.stop - keys.start)
    return (keys.start + lax.broadcasted_iota(jnp.int32, shp, 1)
            <= rows.start + lax.broadcasted_iota(jnp.int32, shp, 0))


def _mla_attn_fwd(q, k, v):
    rows = q.shape[0]
    t = min(TQ_MLA, rows)
    nt = rows // t
    wide = MLA_PACK * SLOT

    def body(q_ref, k_ref, v_ref, o_ref, lse_ref, m_sc, l_sc, acc_sc):
        i, j = pl.program_id(1), pl.program_id(2)

        @pl.when(j == 0)
        def _():
            m_sc[...] = jnp.full_like(m_sc, NEG)
            l_sc[...] = jnp.zeros_like(l_sc)
            acc_sc[...] = jnp.zeros_like(acc_sc)

        def step(diagonal):
            for hh in range(MLA_PACK):
                sl = slice(hh * SLOT, (hh + 1) * SLOT)
                s = _nt(k_ref[:, sl], q_ref[:, sl])
                if diagonal:
                    key = lax.broadcasted_iota(jnp.int32, (t, t), 0)
                    s = jnp.where(key <= lax.broadcasted_iota(jnp.int32, (t, t), 1), s, NEG)
                m_prev = m_sc[hh]
                m_new = jnp.maximum(m_prev, jnp.max(s, axis=0, keepdims=True))
                p = jnp.exp2(s - m_new)
                alpha = jnp.exp2(m_prev - m_new)
                l_new = alpha * l_sc[hh] + jnp.sum(p, axis=0, keepdims=True)
                acc = alpha * acc_sc[hh] + _tn(v_ref[:, sl], p.astype(BF16))
                if diagonal:
                    o_ref[:, sl] = (acc / l_new).T.astype(o_ref.dtype)
                    lse_ref[:, sl] = jnp.broadcast_to(m_new + jnp.log(l_new) * LOG2_E, (SLOT, t)).T
                else:
                    m_sc[hh] = m_new
                    l_sc[hh] = l_new
                    acc_sc[hh] = acc

        @pl.when(j < i)
        def _():
            step(False)

        @pl.when(j == i)
        def _():
            step(True)

    q_spec = pl.BlockSpec((t, wide), lambda h, i, j: (i, h))
    kv_spec = pl.BlockSpec((t, wide), lambda h, i, j: (jnp.minimum(j, i), h))
    return pl.pallas_call(
        body, name="mla_attn_fwd", grid=(MLA_HEADS // MLA_PACK, nt, nt),
        in_specs=[q_spec, kv_spec, kv_spec], out_specs=[q_spec, q_spec],
        out_shape=[jax.ShapeDtypeStruct(q.shape, BF16), jax.ShapeDtypeStruct(q.shape, F32)],
        scratch_shapes=[pltpu.VMEM((MLA_PACK, 1, t), F32), pltpu.VMEM((MLA_PACK, 1, t), F32),
                        pltpu.VMEM((MLA_PACK, SLOT, t), F32)],
        compiler_params=_cparams(),
    )(q, k, v)


def _mla_attn_bwd(q, k, v, o, do, lse, after):
    rows = q.shape[0]
    t = min(TQ_MLA, rows)
    nt = rows // t
    wide = MLA_PACK * SLOT

    def body(q_ref, k_ref, v_ref, o_ref, do_ref, lse_ref, after_ref, dq_ref, dk_ref, dv_ref, dk_sc, dv_sc):
        j, i = pl.program_id(1), pl.program_id(2)

        @pl.when((j == 0) & (i == 0))
        def _():
            dq_ref[...] = jnp.zeros_like(dq_ref)

        @pl.when(i == 0)
        def _():
            dk_sc[...] = jnp.zeros_like(dk_sc)
            dv_sc[...] = jnp.zeros_like(dv_sc)

        def chunk(hh, rows, keys, masked):
            sl = slice(hh * SLOT, (hh + 1) * SLOT)
            n_rows = rows.stop - rows.start
            qv, kv, dov = q_ref[rows, sl], k_ref[keys, sl], do_ref[rows, sl]
            s = _nt(qv, kv)
            if masked:
                s = jnp.where(_causal(rows, keys), s, NEG)
            p = jnp.exp2(s - lse_ref[rows, hh * SLOT:hh * SLOT + 1])
            delta = jnp.sum(dov.astype(F32) * o_ref[rows, sl].astype(F32), axis=1, keepdims=True)
            dp = _nt(dov, v_ref[keys, sl])
            ds = (p * (dp - delta)).astype(BF16)
            dv_sc[keys, sl] += _tn(p.astype(BF16), dov)
            dk_sc[keys, sl] += _tn(ds, qv)
            r0 = pl.multiple_of(i * t + rows.start, n_rows)
            dq_ref[pl.ds(r0, n_rows), sl] += _nn(ds, kv) * MLA_SCALE

        @pl.when(i > j)
        def _():
            for hh in range(MLA_PACK):
                chunk(hh, slice(0, t), slice(0, t), False)

        @pl.when(i == j)
        def _():
            for hh in range(MLA_PACK):
                chunk(hh, slice(0, t), slice(0, t // 2), True)
                chunk(hh, slice(t // 2, t), slice(t // 2, t), True)

        @pl.when(i == nt - 1)
        def _():
            dk_ref[...] = dk_sc[...] * (1.0 / LOG2_E)
            dv_ref[...] = dv_sc[...]

    q_spec = pl.BlockSpec((t, wide), lambda h, j, i: (jnp.maximum(i, j), h))
    kv_spec = pl.BlockSpec((t, wide), lambda h, j, i: (j, h))
    head_spec = pl.BlockSpec((rows, wide), lambda h, j, i: (0, h))
    shp = jax.ShapeDtypeStruct(q.shape, F32)
    return pl.pallas_call(
        body, name="mla_attn_bwd", grid=(MLA_HEADS // MLA_PACK, nt, nt),
        in_specs=[q_spec, kv_spec, kv_spec, q_spec, q_spec, q_spec, pl.BlockSpec(memory_space=pl.ANY)],
        out_specs=[head_spec, kv_spec, kv_spec], out_shape=[shp, shp, shp],
        scratch_shapes=[pltpu.VMEM((t, wide), F32), pltpu.VMEM((t, wide), F32)],
        compiler_params=_cparams(),
    )(q, k, v, o, do, lse, after)


def _swa_specs(t):
    def prev(i):
        return jnp.maximum(i - 1, 0)
    kw = SWA_PACK * SLOT
    k0, v0 = SWA_HEADS // SWA_PACK, (SWA_HEADS + SWA_KV_HEADS) // SWA_PACK
    q3 = pl.BlockSpec((t, SWA_PACK * SWA_GROUP * SLOT), lambda h, i: (i, h))
    kp = pl.BlockSpec((t, kw), lambda h, i: (prev(i), k0 + h))
    kc = pl.BlockSpec((t, kw), lambda h, i: (i, k0 + h))
    vp = pl.BlockSpec((t, kw), lambda h, i: (prev(i), v0 + h))
    vc = pl.BlockSpec((t, kw), lambda h, i: (i, v0 + h))
    pcol = pl.BlockSpec((t, 1), lambda h, i: (i, 0))
    prow_p = pl.BlockSpec((1, t), lambda h, i: (0, prev(i)))
    prow_c = pl.BlockSpec((1, t), lambda h, i: (0, i))
    return [q3, kp, kc, vp, vc, pcol, prow_p, prow_c]


def _stack(ref, first):
    return jnp.concatenate([ref[:, (first + g) * SLOT:(first + g + 1) * SLOT] for g in range(SWA_GROUP)], axis=0)


def _swa_logits(q3, kp, kc, pq, pkp, pkc, slope_ref, kvh, i, t):
    r = lax.broadcasted_iota(jnp.int32, (t, t), 0)
    c = lax.broadcasted_iota(jnp.int32, (t, t), 1)
    ok_c = c <= r
    ok_p = (c - r) > jnp.where(i > 0, 0, t)
    dist_p, dist_c = pq - pkp, pq - pkc
    s_p3 = _nt(q3, kp) * (HEAD_DIM ** -0.5)
    s_c3 = _nt(q3, kc) * (HEAD_DIM ** -0.5)
    out = []
    for g in range(SWA_GROUP):
        slope = slope_ref[kvh * SWA_GROUP + g]
        rows = slice(g * t, (g + 1) * t)
        out.append((jnp.where(ok_p, s_p3[rows] - slope * dist_p, NEG),
                    jnp.where(ok_c, s_c3[rows] - slope * dist_c, NEG)))
    return out


def _swa_attn_fwd(proj, pos_col, pos_row, slopes, sinks):
    rows = proj.shape[0]
    t = WINDOW
    hw = SWA_HEADS * SLOT

    def body(slope_ref, sink_ref, q_ref, kp_ref, kc_ref, vp_ref, vc_ref, pq_ref, pkp_ref, pkc_ref, o_ref, lse_ref):
        i = pl.program_id(1)
        for kv in range(SWA_PACK):
            kvh = pl.program_id(0) * SWA_PACK + kv
            ksl = slice(kv * SLOT, (kv + 1) * SLOT)
            logits = _swa_logits(_stack(q_ref, kv * SWA_GROUP), kp_ref[:, ksl], kc_ref[:, ksl], pq_ref[...],
                                 pkp_ref[...], pkc_ref[...], slope_ref, kvh, i, t)
            e_p, e_c, norm = [], [], []
            for g, (s_p, s_c) in enumerate(logits):
                sl = slice((kv * SWA_GROUP + g) * SLOT, (kv * SWA_GROUP + g + 1) * SLOT)
                sink = sink_ref[kvh * SWA_GROUP + g]
                m = jnp.maximum(jnp.maximum(jnp.max(s_p, axis=1, keepdims=True),
                                            jnp.max(s_c, axis=1, keepdims=True)), sink)
                ep, ec = jnp.exp(s_p - m), jnp.exp(s_c - m)
                l = jnp.sum(ep, axis=1, keepdims=True) + jnp.sum(ec, axis=1, keepdims=True) + jnp.exp(sink - m)
                e_p.append(ep.astype(BF16))
                e_c.append(ec.astype(BF16))
                norm.append(l)
                lse_ref[:, sl] = jnp.broadcast_to(m + jnp.log(l), (t, SLOT))
            acc = (_nn(jnp.concatenate(e_p, axis=0), vp_ref[:, ksl])
                   + _nn(jnp.concatenate(e_c, axis=0), vc_ref[:, ksl]))
            for g in range(SWA_GROUP):
                sl = slice((kv * SWA_GROUP + g) * SLOT, (kv * SWA_GROUP + g + 1) * SLOT)
                o_ref[:, sl] = (acc[g * t:(g + 1) * t] / norm[g]).astype(o_ref.dtype)

    smem = pl.BlockSpec(memory_space=pltpu.SMEM)
    out_spec = pl.BlockSpec((t, SWA_PACK * SWA_GROUP * SLOT), lambda h, i: (i, h))
    return pl.pallas_call(
        body, name="swa_attn_fwd", grid=(SWA_KV_HEADS // SWA_PACK, rows // t),
        in_specs=[smem, smem] + _swa_specs(t), out_specs=[out_spec, out_spec],
        out_shape=[jax.ShapeDtypeStruct((rows, hw), BF16), jax.ShapeDtypeStruct((rows, hw), F32)],
        compiler_params=_cparams(),
    )(slopes, sinks, proj, proj, proj, proj, proj, pos_col, pos_row, pos_row)


def _swa_attn_bwd(proj, o, do, lse, pos_col, pos_row, slopes, sinks, after):
    rows = proj.shape[0]
    t = WINDOW
    hw = SWA_HEADS * SLOT
    scale = HEAD_DIM ** -0.5

    def body(slope_ref, sink_ref, q_ref, kp_ref, kc_ref, vp_ref, vc_ref, pq_ref, pkp_ref, pkc_ref,
             o_ref, do_ref, lse_ref, after_ref, dq_ref, dk_ref, dv_ref, dsink_ref):
        i = pl.program_id(1)

        @pl.when(i == 0)
        def _():
            dk_ref[...] = jnp.zeros_like(dk_ref)
            dv_ref[...] = jnp.zeros_like(dv_ref)
            dsink_ref[...] = jnp.zeros_like(dsink_ref)

        r_c = pl.multiple_of(i * t, t)
        r_p = pl.multiple_of(jnp.maximum(i - 1, 0) * t, t)
        for kv in range(SWA_PACK):
            kvh = pl.program_id(0) * SWA_PACK + kv
            ksl = slice(kv * SLOT, (kv + 1) * SLOT)
            q3, do3 = _stack(q_ref, kv * SWA_GROUP), _stack(do_ref, kv * SWA_GROUP)
            logits = _swa_logits(q3, kp_ref[:, ksl], kc_ref[:, ksl], pq_ref[...], pkp_ref[...], pkc_ref[...],
                                 slope_ref, kvh, i, t)
            dp_p3, dp_c3 = _nt(do3, vp_ref[:, ksl]), _nt(do3, vc_ref[:, ksl])
            p_p, p_c, ds_p, ds_c = [], [], [], []
            for g, (s_p, s_c) in enumerate(logits):
                head = kv * SWA_GROUP + g
                sl = slice(head * SLOT, (head + 1) * SLOT)
                rws = slice(g * t, (g + 1) * t)
                lse_g = lse_ref[:, head * SLOT:head * SLOT + 1]
                pp, pc = jnp.exp(s_p - lse_g), jnp.exp(s_c - lse_g)
                delta = jnp.sum(do_ref[:, sl].astype(F32) * o_ref[:, sl].astype(F32), axis=1, keepdims=True)
                p_p.append(pp.astype(BF16))
                p_c.append(pc.astype(BF16))
                ds_p.append((pp * (dp_p3[rws] - delta)).astype(BF16))
                ds_c.append((pc * (dp_c3[rws] - delta)).astype(BF16))
                sink = sink_ref[kvh * SWA_GROUP + g]
                dsink = -jnp.sum(jnp.exp(sink - lse_g) * delta, axis=0, keepdims=True)
                dsink_ref[head * 8:(head + 1) * 8, :] += jnp.broadcast_to(dsink, (8, SLOT))
            p_p3, p_c3 = jnp.concatenate(p_p, axis=0), jnp.concatenate(p_c, axis=0)
            ds_p3, ds_c3 = jnp.concatenate(ds_p, axis=0), jnp.concatenate(ds_c, axis=0)
            dq3 = (_nn(ds_p3, kp_ref[:, ksl]) + _nn(ds_c3, kc_ref[:, ksl])) * scale
            for g in range(SWA_GROUP):
                head = kv * SWA_GROUP + g
                dq_ref[:, head * SLOT:(head + 1) * SLOT] = dq3[g * t:(g + 1) * t]
            dk_ref[pl.ds(r_c, t), ksl] += _tn(ds_c3, q3) * scale
            dv_ref[pl.ds(r_c, t), ksl] += _tn(p_c3, do3)
            dk_ref[pl.ds(r_p, t), ksl] += _tn(ds_p3, q3) * scale
            dv_ref[pl.ds(r_p, t), ksl] += _tn(p_p3, do3)

    smem = pl.BlockSpec(memory_space=pltpu.SMEM)
    qlike = pl.BlockSpec((t, SWA_PACK * SWA_GROUP * SLOT), lambda h, i: (i, h))
    kv_out = pl.BlockSpec((rows, SWA_PACK * SLOT), lambda h, i: (0, h))
    return pl.pallas_call(
        body, name="swa_attn_bwd", grid=(SWA_KV_HEADS // SWA_PACK, rows // t),
        in_specs=[smem, smem] + _swa_specs(t) + [qlike, qlike, qlike, pl.BlockSpec(memory_space=pl.ANY)],
        out_specs=[qlike, kv_out, kv_out,
                   pl.BlockSpec((SWA_PACK * SWA_GROUP * 8, SLOT), lambda h, i: (h, 0))],
        out_shape=[jax.ShapeDtypeStruct((rows, hw), F32), jax.ShapeDtypeStruct((rows, SWA_KV_HEADS * SLOT), F32),
                   jax.ShapeDtypeStruct((rows, SWA_KV_HEADS * SLOT), F32),
                   jax.ShapeDtypeStruct((SWA_HEADS * 8, SLOT), F32)],
        compiler_params=_cparams(),
    )(slopes, sinks, proj, proj, proj, proj, proj, pos_col, pos_row, pos_row, o, do, lse, after)


def _cross_attn_fwd(proj, qoff, kvmem):
    rows = proj.shape[0]
    t = min(TQ_CROSS, rows)

    def body(q_ref, k_ref, v_ref, o_ref):
        s = _nt(q_ref[...].astype(BF16), k_ref[...]) * (HEAD_DIM ** -0.5)
        e = jnp.exp(s - jnp.max(s, axis=1, keepdims=True))
        p = e / jnp.sum(e, axis=1, keepdims=True)
        o_ref[...] = _nn(p.astype(BF16), v_ref[...]).astype(o_ref.dtype)

    return pl.pallas_call(
        body, name="cross_attn_fwd", grid=(rows // t, MEM_HEADS),
        in_specs=[pl.BlockSpec((t, SLOT), lambda i, h: (i, qoff + h)),
                  pl.BlockSpec((N_MEM, SLOT), lambda i, h: (0, h)),
                  pl.BlockSpec((N_MEM, SLOT), lambda i, h: (0, MEM_HEADS + h))],
        out_specs=pl.BlockSpec((t, SLOT), lambda i, h: (i, h)),
        out_shape=jax.ShapeDtypeStruct((rows, MEM_HEADS * SLOT), BF16), compiler_params=_cparams(),
    )(proj, kvmem, kvmem)


def _cross_attn_bwd(proj, qoff, kvmem, do):
    rows = proj.shape[0]
    t = min(TQ_CROSS, rows)
    scale = HEAD_DIM ** -0.5

    def body(q_ref, k_ref, v_ref, do_ref, dq_ref, dk_ref, dv_ref):
        @pl.when(pl.program_id(1) == 0)
        def _():
            dk_ref[...] = jnp.zeros_like(dk_ref)
            dv_ref[...] = jnp.zeros_like(dv_ref)

        qv, kv, dov = q_ref[...].astype(BF16), k_ref[...], do_ref[...]
        s = _nt(qv, kv) * scale
        e = jnp.exp(s - jnp.max(s, axis=1, keepdims=True))
        p = e / jnp.sum(e, axis=1, keepdims=True)
        dp = _nt(dov, v_ref[...])
        ds = (p * (dp - jnp.sum(p * dp, axis=1, keepdims=True))).astype(BF16)
        dq_ref[...] = _nn(ds, kv) * scale
        dk_ref[...] += _tn(ds, qv) * scale
        dv_ref[...] += _tn(p.astype(BF16), dov)

    mem_out = pl.BlockSpec((N_MEM, SLOT), lambda h, i: (0, h))
    return pl.pallas_call(
        body, name="cross_attn_bwd", grid=(MEM_HEADS, rows // t),
        in_specs=[pl.BlockSpec((t, SLOT), lambda h, i: (i, qoff + h)),
                  pl.BlockSpec((N_MEM, SLOT), lambda h, i: (0, h)),
                  pl.BlockSpec((N_MEM, SLOT), lambda h, i: (0, MEM_HEADS + h)),
                  pl.BlockSpec((t, SLOT), lambda h, i: (i, h))],
        out_specs=[pl.BlockSpec((t, SLOT), lambda h, i: (i, h)), mem_out, mem_out],
        out_shape=[jax.ShapeDtypeStruct((rows, MEM_HEADS * SLOT), F32),
                   jax.ShapeDtypeStruct((N_MEM, MEM_HEADS * SLOT), F32),
                   jax.ShapeDtypeStruct((N_MEM, MEM_HEADS * SLOT), F32)],
        compiler_params=_cparams(),
    )(proj, kvmem, kvmem, do)


def _place():
    return lax.axis_index("x"), lax.axis_index("y"), lax.axis_index("c")


def _flip(v, bit):
    return 1 - v if bit else v


def _all_gather(blocks, name):
    nb = len(blocks)

    def body(*refs):
        x_refs, out_refs = refs[:nb], refs[nb:2 * nb]
        send_sems, recv_sems, local_sems = refs[2 * nb:]
        x, y, c = _place()
        me, sibling = (x, y, c), (x, y, 1 - c)
        chips = [(1 - x, y), (x, 1 - y), (1 - x, 1 - y)]

        def copy(b, k, blk, to, from_input=False):
            slot = out_refs[b].at[4 * blk[0] + 2 * blk[1] + blk[2]]
            return pltpu.make_async_remote_copy(
                src_ref=x_refs[b] if from_input else slot, dst_ref=slot,
                send_sem=send_sems.at[b, k], recv_sem=recv_sems.at[b, k],
                device_id=to, device_id_type=pl.DeviceIdType.MESH)

        mine = [pltpu.make_async_copy(x_refs[b], out_refs[b].at[4 * x + 2 * y + c], local_sems.at[b])
                for b in range(nb)]
        for cp in mine:
            cp.start()
        first = []
        for b in range(nb):
            first.append(copy(b, 0, me, sibling, from_input=True))
            first += [copy(b, 1 + n, me, (*chip, c), from_input=True) for n, chip in enumerate(chips)]
        for cp in first:
            cp.start()
        passed = []
        for n, chip in enumerate(chips):
            for b in range(nb):
                copy(b, 1 + n, (*chip, c), me).wait_recv()
                passed.append(copy(b, 4 + n, (*chip, c), sibling))
                passed[-1].start()
        for b in range(nb):
            copy(b, 0, sibling, me).wait_recv()
            for n, chip in enumerate(chips):
                copy(b, 4 + n, (*chip, 1 - c), me).wait_recv()
        for cp in first + passed:
            cp.wait_send()
        for cp in mine:
            cp.wait()

    any_spec = pl.BlockSpec(memory_space=pl.ANY)
    return pl.pallas_call(
        body, name=name, in_specs=[any_spec] * nb, out_specs=[any_spec] * nb,
        out_shape=[jax.ShapeDtypeStruct((N_DEV,) + blk.shape, blk.dtype) for blk in blocks],
        scratch_shapes=[pltpu.SemaphoreType.DMA((nb, 7)), pltpu.SemaphoreType.DMA((nb, 7)),
                        pltpu.SemaphoreType.DMA((nb,))],
    )(*blocks)


def _peers(x, y, c):
    out = []
    for n in range(1, N_DEV):
        peer = (_flip(x, n & 4), _flip(y, n & 2), _flip(c, n & 1))
        out.append((n - 1, peer, 4 * peer[0] + 2 * peer[1] + peer[2]))
    return out


_HBM = pl.BlockSpec(memory_space=pltpu.HBM)
_SEM = pl.BlockSpec(memory_space=pltpu.SEMAPHORE)


def _exchange_start(srcs, scatter, name, after=None):
    ns = len(srcs)
    lands = [lax.empty(s.shape if scatter else (N_DEV,) + s.shape, s.dtype) for s in srcs]

    def body(*refs):
        src_refs, land_refs = refs[:ns], refs[ns:2 * ns]
        pos = 2 * ns + (1 if after is not None else 0)
        send_sems, recv_sems, token = refs[pos], refs[pos + 1], refs[-1]
        x, y, c = _place()
        my_idx = 4 * x + 2 * y + c
        for col, peer, peer_idx in _peers(x, y, c):
            for b in range(ns):
                pltpu.make_async_remote_copy(
                    src_ref=src_refs[b].at[peer_idx] if scatter else src_refs[b], dst_ref=land_refs[b].at[my_idx],
                    send_sem=send_sems.at[b * (N_DEV - 1) + col], recv_sem=recv_sems.at[b * (N_DEV - 1) + col],
                    device_id=peer, device_id_type=pl.DeviceIdType.MESH).start()
        token[...] = jnp.zeros_like(token)

    args = [pltpu.with_memory_space_constraint(a, pltpu.HBM) for a in list(srcs) + lands]
    in_specs = [_HBM] * (2 * ns)
    if after is not None:
        args.append(after)
        in_specs.append(pl.BlockSpec(memory_space=pl.ANY))
    out = pl.pallas_call(
        body, name=name, in_specs=in_specs,
        out_specs=[_SEM, _SEM] + [_HBM] * (2 * ns) + [pl.BlockSpec(memory_space=pltpu.VMEM)],
        out_shape=[pltpu.SemaphoreType.DMA((ns * (N_DEV - 1),)), pltpu.SemaphoreType.DMA((ns * (N_DEV - 1),))]
        + [pltpu.HBM(a.shape, a.dtype) for a in list(srcs) + lands] + [jax.ShapeDtypeStruct((8, SLOT), F32)],
        input_output_aliases={k: 2 + k for k in range(2 * ns)},
        compiler_params=pltpu.CompilerParams(has_side_effects=pltpu.SideEffectType.DATAFLOW_SIDE_EFFECTING),
    )(*args)
    return (out[0], out[1], out[2:2 + ns], out[2 + ns:2 + 2 * ns], scatter), out[-1]


def _exchange_wait(handle, after, name):
    send_sems, recv_sems, srcs, lands, scatter = handle
    ns = len(srcs)

    def body(*refs):
        src_refs, land_refs = refs[:ns], refs[ns:2 * ns]
        send_ref, recv_ref = refs[2 * ns], refs[2 * ns + 1]
        x, y, c = _place()
        for col, peer, peer_idx in _peers(x, y, c):
            for b in range(ns):
                copy = pltpu.make_async_remote_copy(
                    src_ref=src_refs[b].at[peer_idx] if scatter else src_refs[b], dst_ref=land_refs[b].at[peer_idx],
                    send_sem=send_ref.at[b * (N_DEV - 1) + col], recv_sem=recv_ref.at[b * (N_DEV - 1) + col],
                    device_id=peer, device_id_type=pl.DeviceIdType.MESH)
                copy.wait_send()
                copy.wait_recv()

    out = pl.pallas_call(
        body, name=name, in_specs=[_HBM] * (2 * ns) + [_SEM, _SEM, pl.BlockSpec(memory_space=pl.ANY)],
        out_specs=[_HBM] * (2 * ns),
        out_shape=[pltpu.HBM(a.shape, a.dtype) for a in list(srcs) + list(lands)],
        input_output_aliases={k: k for k in range(2 * ns)},
        compiler_params=pltpu.CompilerParams(has_side_effects=pltpu.SideEffectType.DATAFLOW_SIDE_EFFECTING),
    )(*srcs, *lands, send_sems, recv_sems, after)
    my_idx = 4 * lax.axis_index("x") + 2 * lax.axis_index("y") + lax.axis_index("c")
    landed = []
    for src, land in zip(out[:ns], out[ns:]):
        own = lax.dynamic_index_in_dim(src, my_idx, 0, keepdims=True) if scatter else src[None]
        landed.append(lax.dynamic_update_index_in_dim(land, own, my_idx, 0))
    return landed


def _adamw(parts, w, m, v, name):
    lyr, rows, cols = w.shape
    assert len(parts) == lyr
    tr = ADAM_ROWS if cols > 512 else 2 * ADAM_ROWS
    while rows % tr:
        tr //= 2
    tr = min(tr, rows)

    def body(*refs):
        p_refs = refs[:lyr]
        w_ref, m_ref, v_ref, g_out, d_out, m_out, v_out = refs[lyr:]
        for k in range(lyr):
            @pl.when(pl.program_id(0) == k)
            def _(p_ref=p_refs[k]):
                g = p_ref[0].astype(F32)
                for s in range(1, N_DEV):
                    g = g + p_ref[s].astype(F32)
                m2 = ADAM_B1 * m_ref[...] + (1.0 - ADAM_B1) * g
                v2 = ADAM_B2 * v_ref[...] + (1.0 - ADAM_B2) * (g * g)
                m_hat = m2 / (1.0 - ADAM_B1 ** ADAM_STEP)
                v_hat = v2 / (1.0 - ADAM_B2 ** ADAM_STEP)
                g_out[...] = g
                d_out[...] = -ADAM_LR * (m_hat / (jnp.sqrt(v_hat) + ADAM_EPS) + ADAM_WD * w_ref[...])
                m_out[...] = m2
                v_out[...] = v2

    def part_spec(k):
        return pl.BlockSpec((N_DEV, tr, cols), lambda l, i: (0, jnp.where(l == k, i, 0), 0))

    spec = pl.BlockSpec((None, tr, cols), lambda l, i: (l, i, 0))
    shp = jax.ShapeDtypeStruct((lyr, rows, cols), F32)
    return pl.pallas_call(
        body, name=name, grid=(lyr, rows // tr),
        in_specs=[part_spec(k) for k in range(lyr)] + [spec, spec, spec],
        out_specs=[spec] * 4, out_shape=[shp] * 4, compiler_params=_cparams(),
    )(*parts, w, m, v)


def _pack(arrays, lanes, row_mult, dtype):
    flat = jnp.concatenate([a.reshape(-1).astype(dtype) for a in arrays])
    unit = lanes * row_mult
    total = -(-flat.shape[0] // unit) * unit
    return jnp.pad(flat, (0, total - flat.shape[0])).reshape(total // lanes, lanes)


def _unpack(packed, shapes):
    flat = packed.reshape(-1)
    out, off = [], 0
    for shp in shapes:
        n = 1
        for d in shp:
            n *= d
        out.append(flat[off:off + n].reshape(shp))
        off += n
    return out


def _pad_slots(w, axis):
    axis = axis % w.ndim
    n = w.shape[axis] // HEAD_DIM
    shp = w.shape[:axis] + (n, HEAD_DIM) + w.shape[axis + 1:]
    pad = [(0, 0)] * (w.ndim + 1)
    pad[axis + 1] = (0, SLOT - HEAD_DIM)
    return jnp.pad(w.reshape(shp), pad).reshape(w.shape[:axis] + (n * SLOT,) + w.shape[axis + 1:])


def _unpad_slots(w, axis, keep=HEAD_DIM):
    axis = axis % w.ndim
    n = w.shape[axis] // SLOT
    shp = w.shape[:axis] + (n, SLOT) + w.shape[axis + 1:]
    idx = [slice(None)] * (w.ndim + 1)
    idx[axis + 1] = slice(0, keep)
    return w.reshape(shp)[tuple(idx)].reshape(w.shape[:axis] + (n * keep,) + w.shape[axis + 1:])


def _mla_in_pad(w):
    z = functools.partial(jnp.zeros, dtype=w.dtype)
    rows = w.shape[0]
    return jnp.concatenate([w[:, :384], z((rows, 64)), w[:, 640:672], z((rows, 32)), w[:, 384:640],
                            _pad_slots(w[:, 672:], 1)], axis=1)


def _mla_in_unpad(d):
    return jnp.concatenate([d[:, :384], d[:, 512:768], d[:, 448:480], _unpad_slots(d[:, 768:], 1)], axis=1)


def _mla_uq_pad(w):
    return jnp.pad(w.reshape(w.shape[0], MLA_HEADS, MLA_QK), ((0, 0), (0, 0), (0, SLOT - MLA_QK))).reshape(
        w.shape[0], MLA_HEADS * SLOT)


def _join(gathered, axis):
    nd, a, b = gathered.shape
    if axis == 1:
        return gathered.reshape(nd * a, b)
    return gathered.transpose(1, 0, 2).reshape(a, nd * b)


def _split(full, axis):
    r, c = full.shape
    if axis == 1:
        return full.reshape(N_DEV, r // N_DEV, c).astype(BF16)
    return full.reshape(r, N_DEV, c // N_DEV).transpose(1, 0, 2).astype(BF16)


def kernel(x, mem, positions, attn_norm_g, mlp_norm_g, mem_norm_g, final_norm_g, mla_w_in, mla_q_norm_g, mla_kv_norm_g, mla_w_uq, mla_w_ukv, swa_w_in, swa_sinks, w_mem_kv, w_o, mlp_w_up, mlp_w_down, loss_target, m_attn_norm_g, m_mlp_norm_g, m_mem_norm_g, m_final_norm_g, m_mla_w_in, m_mla_q_norm_g, m_mla_kv_norm_g, m_mla_w_uq, m_mla_w_ukv, m_swa_w_in, m_swa_sinks, m_w_mem_kv, m_w_o, m_mlp_w_up, m_mlp_w_down, v_attn_norm_g, v_mlp_norm_g, v_mem_norm_g, v_final_norm_g, v_mla_w_in, v_mla_q_norm_g, v_mla_kv_norm_g, v_mla_w_uq, v_mla_w_ukv, v_swa_w_in, v_swa_sinks, v_w_mem_kv, v_w_o, v_mlp_w_up, v_mlp_w_down):
    given = dict(locals())
    seq = x.shape[1]
    x0 = x.reshape(seq, D_MODEL)
    tgt = loss_target.reshape(seq, D_MODEL)
    mem0 = mem.reshape(N_MEM, D_MODEL)
    pos = positions.reshape(seq).astype(F32)
    pos_col, pos_row = pos.reshape(seq, 1), pos.reshape(1, seq)

    def layer_names(i):
        mixer = ("mla_w_in", "mla_w_uq", "mla_w_ukv") if i % 2 == 0 else ("swa_w_in",)
        return [(n, i // 2) for n in mixer] + [(n, i) for n in ("w_mem_kv", "w_o", "mlp_w_up", "mlp_w_down")]

    def local_weights(names):
        return [given[n][l].astype(BF16) for n, l in names]

    first_attn, first_mlp = layer_names(0)[:-2], layer_names(0)[-2:]
    weights = [dict(zip([n for n, _ in first_attn], _all_gather(local_weights(first_attn), "gather_weights_first")))]
    coming_mlp, first_token = _exchange_start(local_weights(first_mlp), False, "gather_weights_start_0",
                                              after=weights[0]["w_o"])

    consts = _lane_consts()
    tabs = _rope_tables(pos_col, consts)
    slopes = 2.0 ** (-8.0 * (jnp.arange(SWA_HEADS, dtype=F32) + 1.0) / SWA_HEADS)

    mem_n = _rmsnorm_fwd(mem0, 0, D_MODEL, mem_norm_g, "rmsnorm_fwd_mem")

    saved = []
    xc = x0
    for i in range(DEPTH):
        j = i // 2
        wts = weights[i]
        s = {"x_in": xc}
        token = None
        if i + 1 < DEPTH:
            coming, token = _exchange_start(local_weights(layer_names(i + 1)), False,
                                            "gather_weights_start_%d" % (i + 1),
                                            after=first_token if i == 0 else wts["w_o"])
        hn = _rmsnorm_fwd(xc, 0, D_MODEL, attn_norm_g[i], "rmsnorm_fwd", after=token)
        if i % 2 == 0:
            w_in = _mla_in_pad(_join(wts["mla_w_in"], 1))
            w_uq = _mla_uq_pad(_join(wts["mla_w_uq"], 2))
            w_kv = _join(wts["mla_w_ukv"], 2)
            proj = _mm(hn, w_in, "nn", F32, "mm_mla_in")
            cqn = _rmsnorm_fwd(proj, 0, MLA_Q_RANK, mla_q_norm_g[j], "rmsnorm_fwd_q")
            ckvn = _rmsnorm_fwd(proj, 2, MLA_KV_RANK, mla_kv_norm_g[j], "rmsnorm_fwd_kv")
            qraw = _mm(cqn, w_uq, "nn", F32, "mm_mla_uq")
            kvraw = _mm(ckvn, w_kv, "nn", F32, "mm_mla_ukv")
            q, k, v = _mla_rope_fwd(qraw, kvraw, proj, tabs)
            o, lse = _mla_attn_fwd(q, k, v)
            qoff = MLA_QOFF
            s.update(w_uq=w_uq, w_kv=w_kv, cqn=cqn, ckvn=ckvn, q=q, k=k, v=v)
        else:
            w_in = _join(wts["swa_w_in"], 2)
            proj = _mm(hn, w_in, "nn", BF16, "mm_swa_in", pairs="o")
            o, lse = _swa_attn_fwd(proj, pos_col, pos_row, slopes, swa_sinks[j])
            qoff = SWA_QOFF
        w_mem = _pad_slots(_join(wts["w_mem_kv"], 1), 1)
        w_out = _join(wts["w_o"], 1)
        w_o_mix, w_o_cross = w_out[:SWA_HEADS * HEAD_DIM], w_out[SWA_HEADS * HEAD_DIM:]
        kvmem = _mm(mem_n, w_mem, "nn", BF16, "mm_mem_kv")
        cross = _cross_attn_fwd(proj, qoff, kvmem)
        x1 = _mm(o, w_o_mix, "nn", F32, "mm_o_mix", res=xc, pairs="a")
        x1 = _mm(cross, w_o_cross, "nn", F32, "mm_o_cross", res=x1, pairs="a")
        hn2 = _rmsnorm_fwd(x1, 0, D_MODEL, mlp_norm_g[i], "rmsnorm_fwd")
        if i == 0:
            wts.update(zip([n for n, _ in first_mlp], _exchange_wait(coming_mlp, hn2, "gather_weights_wait_0")))
        act, act2 = _mm(hn2, wts["mlp_w_up"], "nn", BF16, "mm_mlp_up", epi="relu2", b_blk="cols")
        xc = _mm(act2, wts["mlp_w_down"], "nn", F32, "mm_mlp_down", res=x1, b_blk="rows")
        s.update(hn=hn, w_in=w_in, proj=proj, o=o, lse=lse, qoff=qoff, w_mem=w_mem, w_o_mix=w_o_mix,
                 w_o_cross=w_o_cross, kvmem=kvmem, cross=cross, x1=x1, hn2=hn2, act=act, act2=act2)
        saved.append(s)
        if i + 1 < DEPTH:
            got = _exchange_wait(coming, xc, "gather_weights_wait_%d" % (i + 1))
            weights.append(dict(zip([n for n, _ in layer_names(i + 1)], got)))

    dx, dx_b, dg_final, loss_part = _loss_head(xc, final_norm_g, tgt)
    loss = lax.psum(loss_part[0, 0], MESH_AXES)

    gains = {n: [None] * DEPTH for n in ("attn_norm_g", "mlp_norm_g")}
    for n in ("mla_q_norm_g", "mla_kv_norm_g", "swa_sinks"):
        gains[n] = [None] * 2
    leaving = {}
    token = None
    dmem_n = None
    for i in reversed(range(DEPTH)):
        j = i // 2
        s = saved[i]
        wts = weights[i]
        out = {}
        du = _mm(dx_b, wts["mlp_w_down"], "nt", BF16, "mm_mlp_down_dx", aux=s["act"], epi="mul2aux", b_blk="rows",
                 after=token)
        out["mlp_w_down"] = _mm(s["act2"], dx_b, "tn", BF16, "mm_mlp_down_dw", o_blk="rows")
        out["mlp_w_up"] = _mm(s["hn2"], du, "tn", BF16, "mm_mlp_up_dw", o_blk="cols")
        dx1, dx1_b, dg = _mm(du, wts["mlp_w_up"], "nt", F32, "mm_mlp_up_dx", b_blk="cols",
                             epi="normbwd", norm=(s["x1"], mlp_norm_g[i], dx))
        gains["mlp_norm_g"][i] = dg[0]

        do = _mm(dx1_b, s["w_o_mix"], "nt", BF16, "mm_o_mix_dx", pairs="o")
        dcross = _mm(dx1_b, s["w_o_cross"], "nt", BF16, "mm_o_cross_dx", pairs="o")
        dw_o = jnp.concatenate([_mm(s["o"], dx1_b, "tn", F32, "mm_o_mix_dw", pairs="a"),
                                _mm(s["cross"], dx1_b, "tn", F32, "mm_o_cross_dw", pairs="a")], axis=0)
        out["w_o"] = _split(dw_o, 1)
        dqc, dkm, dvm = _cross_attn_bwd(s["proj"], s["qoff"], s["kvmem"], dcross)
        dkvmem = jnp.concatenate([dkm, dvm], axis=1).astype(BF16)
        out["w_mem_kv"] = _split(_unpad_slots(_mm(mem_n, dkvmem, "tn", F32, "mm_mem_kv_dw"), 1), 1)
        dmem_n = _mm(dkvmem, s["w_mem"], "nt", F32, "mm_mem_kv_dx" if dmem_n is None else "mm_mem_kv_dx_acc",
                     res=dmem_n)
        leaving[(i, "main")], token = _exchange_start([out[n] for n, _ in layer_names(i)[-4:]], True,
                                                      "exchange_grads_main_start_%d" % i)

        if i % 2 == 0:
            dq, dk, dv = _mla_attn_bwd(s["q"], s["k"], s["v"], s["o"], do, s["lse"], token)
            dqraw, dkv, dkr = _mla_rope_bwd(dq, dk, dv, tabs, consts)
            dcqn = _mm(dqraw, s["w_uq"], "nt", F32, "mm_mla_uq_dx")
            out["mla_w_uq"] = _split(_unpad_slots(_mm(s["cqn"], dqraw, "tn", F32, "mm_mla_uq_dw"), 1, MLA_QK), 2)
            dckvn = _mm(dkv, s["w_kv"], "nt", F32, "mm_mla_ukv_dx")
            out["mla_w_ukv"] = _split(_mm(s["ckvn"], dkv, "tn", F32, "mm_mla_ukv_dw"), 2)
            dcq, dg = _rmsnorm_bwd(s["proj"], 0, MLA_Q_RANK, mla_q_norm_g[j], dcqn, None, BF16, "rmsnorm_bwd_q")
            gains["mla_q_norm_g"][j] = dg[0]
            dckv, dg = _rmsnorm_bwd(s["proj"], 2, MLA_KV_RANK, mla_kv_norm_g[j], dckvn, None, BF16, "rmsnorm_bwd_kv")
            gains["mla_kv_norm_g"][j] = dg[0]
            dproj = jnp.concatenate([dcq, dkr.astype(BF16), dckv, dqc.astype(BF16)], axis=1)
            in_dx = "mm_mla_in_dx"
            out["mla_w_in"] = _split(_mla_in_unpad(_mm(s["hn"], dproj, "tn", F32, "mm_mla_in_dw")), 1)
        else:
            dq, dk, dv, dsink = _swa_attn_bwd(s["proj"], s["o"], do, s["lse"], pos_col, pos_row, slopes, swa_sinks[j],
                                              token)
            gains["swa_sinks"][j] = dsink[::8, 0]
            dproj = jnp.concatenate([dq, dk, dv, dqc], axis=1).astype(BF16)
            in_dx = "mm_swa_in_dx"
            out["swa_w_in"] = _split(_mm(s["hn"], dproj, "tn", F32, "mm_swa_in_dw", pairs="b"), 2)
        dx, dx_b, dg = _mm(dproj, s["w_in"], "nt", F32, in_dx, epi="normbwd", norm=(s["x_in"], attn_norm_g[i], dx1),
                           pairs="" if i % 2 == 0 else "a")
        gains["attn_norm_g"][i] = dg[0]

        leaving[(i, "mixer")], token = _exchange_start([out[n] for n, _ in layer_names(i)[:-4]], True,
                                                       "exchange_grads_mixer_start_%d" % i)

    _, dg_mem = _rmsnorm_bwd(mem0, 0, D_MODEL, mem_norm_g, dmem_n, None, BF16, "rmsnorm_bwd_mem")
    gains = {n: jnp.stack(g) for n, g in gains.items()}
    gains["mem_norm_g"] = dg_mem[0]
    gains["final_norm_g"] = dg_final[0]

    result = {}

    def adamw_of(names, received):
        for n in names:
            parts = [received[(n, l)] for l in range(given[n].shape[0])]
            for kind, r in enumerate(_adamw(parts, given[n], given["m_" + n], given["v_" + n], "adamw_" + n)):
                result[(kind, n)] = r

    received = {}
    for i in reversed(range(DEPTH)):
        got = _exchange_wait(leaving[(i, "main")], dx, "exchange_grads_main_wait_%d" % i)
        received.update(zip(layer_names(i)[-4:], got))
    adamw_of(("mlp_w_up", "mlp_w_down", "w_o", "w_mem_kv"), received)
    for i in reversed(range(DEPTH)):
        got = _exchange_wait(leaving[(i, "mixer")], result[(0, "w_mem_kv")], "exchange_grads_mixer_wait_%d" % i)
        received.update(zip(layer_names(i)[:-4], got))
    adamw_of(("mla_w_in", "mla_w_uq", "mla_w_ukv", "swa_w_in"), received)

    rep_shapes = [given[n].shape for n in REPLICATED]
    rep_parts = _all_gather([_pack([gains[n] for n in REPLICATED], SLOT, 8, F32)], "gather_gain_grads")[0]
    rep_packed = [_pack([given[p + n] for n in REPLICATED], SLOT, 8, F32)[None] for p in ("", "m_", "v_")]
    for kind, r in enumerate(_adamw([rep_parts], *rep_packed, "adamw_gains")):
        for n, part in zip(REPLICATED, _unpack(r[0], rep_shapes)):
            result[(kind, n)] = part

    outs = [loss, dx.reshape(1, seq, D_MODEL)]
    for kind in range(4):
        outs += [result[(kind, n)] for n in WEIGHT_ORDER]
    return tuple(outs)
```

```python
import functools

import jax
import jax.numpy as jnp
from jax import lax
from jax.experimental import pallas as pl
from jax.experimental.pallas import tpu as pltpu

F32 = jnp.float32
BF16 = jnp.bfloat16

D_MODEL = 1024
D_FF = 4096
N_MEM = 256
DEPTH = 4
SLOT = 128
HEAD_DIM = 64
MLA_HEADS = 12
MLA_QK = 96
MLA_Q_RANK = 384
MLA_KV_RANK = 256
SWA_HEADS = 12
SWA_KV_HEADS = 4
SWA_GROUP = 3
MEM_HEADS = 4
WINDOW = 128
EPS = 1e-6
NEG = -1e30
ROPE_THETA = 10000.0
N_DEV = 8

ADAM_LR = 0.001
ADAM_B1 = 0.9
ADAM_B2 = 0.999
ADAM_EPS = 1e-08
ADAM_WD = 0.01
ADAM_STEP = 10

TM = 512
TQ_MLA = 1024
MLA_PACK = 2
SWA_PACK = 4
TQ_CROSS = 2048
MM_VMEM_BUDGET = 38 * 1024 * 1024
ADAM_ROWS = 128
VMEM_LIMIT = 56 * 1024 * 1024

MESH_AXES = ("x", "y", "c")

LOG2_E = 1.4426950408889634
MLA_SCALE = MLA_QK ** -0.5
MLA_Q_SCALE = MLA_SCALE * LOG2_E

MLA_PAD_IN = 384 + SLOT + 256 + MEM_HEADS * SLOT
MLA_QOFF = (384 + SLOT + 256) // SLOT
SWA_PAD_IN = (SWA_HEADS + 2 * SWA_KV_HEADS + MEM_HEADS) * SLOT
SWA_QOFF = SWA_HEADS + 2 * SWA_KV_HEADS

SHARDED = (
    ("mla_w_in", 1), ("mla_w_uq", 2), ("mla_w_ukv", 2), ("swa_w_in", 2),
    ("w_mem_kv", 1), ("w_o", 1), ("mlp_w_up", 2), ("mlp_w_down", 1),
)
REPLICATED = ("attn_norm_g", "mlp_norm_g", "mem_norm_g", "final_norm_g",
              "mla_q_norm_g", "mla_kv_norm_g", "swa_sinks")
WEIGHT_ORDER = ("attn_norm_g", "mlp_norm_g", "mem_norm_g", "final_norm_g", "mla_w_in",
                "mla_q_norm_g", "mla_kv_norm_g", "mla_w_uq", "mla_w_ukv", "swa_w_in",
                "swa_sinks", "w_mem_kv", "w_o", "mlp_w_up", "mlp_w_down")


def _cparams():
    return pltpu.CompilerParams(vmem_limit_bytes=VMEM_LIMIT)


_DIMS = {"nn": (((1,), (0,)), ((), ())), "nt": (((1,), (1,)), ((), ())), "tn": (((0,), (0,)), ((), ()))}


def _compact(x):
    pairs = [x[:, 2 * j * SLOT:(2 * j + 1) * SLOT] + pltpu.roll(x[:, (2 * j + 1) * SLOT:(2 * j + 2) * SLOT], HEAD_DIM, 1)
             for j in range(x.shape[1] // (2 * SLOT))]
    return pairs[0] if len(pairs) == 1 else jnp.concatenate(pairs, axis=1)


def _expand(x):
    low = lax.broadcasted_iota(jnp.int32, (x.shape[0], SLOT), 1) < HEAD_DIM
    slots = []
    for j in range(x.shape[1] // SLOT):
        pair = x[:, j * SLOT:(j + 1) * SLOT]
        slots += [jnp.where(low, pair, 0.0), pltpu.roll(jnp.where(low, 0.0, pair), HEAD_DIM, 1)]
    return jnp.concatenate(slots, axis=1)


def _mm_tiles(m, n, k, a_bytes, b_bytes, o_bytes, extra_bytes, tm_fixed, tn_fixed):
    best = None
    for tm in ([tm_fixed] if tm_fixed else [t for t in range(4096, 0, -SLOT) if m % t == 0] or [m]):
        for tn in ([tn_fixed] if tn_fixed else [t for t in range(1024, 0, -SLOT) if n % t == 0] or [n]):
            need = 2 * (tm * k * a_bytes + k * tn * b_bytes + tm * tn * (o_bytes + extra_bytes))
            need += tm * tn * 4
            if need <= MM_VMEM_BUDGET and (best is None or tm * tn > best[0] * best[1]):
                best = (tm, tn)
    assert best is not None, (m, n, k)
    return best


def _mm(a, b, mode, out_dtype, name, res=None, aux=None, epi=None, b_blk=None, o_blk=None, after=None, norm=None,
        pairs=""):
    if b_blk is not None:
        nb, br, bc = b.shape
        b_shape = (nb * br, bc) if b_blk == "rows" else (br, nb * bc)
    else:
        b_shape = b.shape
    assert not pairs or (b_blk is None and o_blk is None and not ("b" in pairs and mode == "nt"))
    a_shape = (a.shape[0], a.shape[1] // 2) if "a" in pairs else a.shape
    if "b" in pairs:
        b_shape = (b_shape[0], b_shape[1] // 2)
    if mode == "nn":
        (m, k), (k2, n) = a_shape, b_shape
    elif mode == "nt":
        (m, k), (n, k2) = a_shape, b_shape
    else:
        (k, m), (k2, n) = a_shape, b_shape
    assert k == k2, (a.shape, b_shape, mode)
    k_blocked = b_blk is not None and (b_blk == "rows") == (mode != "nt")
    tn_fixed = None
    if b_blk is not None and not k_blocked:
        tn_fixed = br if b_blk == "rows" else bc
    if o_blk == "cols":
        tn_fixed = n // N_DEV
    tm_fixed = m // N_DEV if o_blk == "rows" else None
    has_res, has_aux, has_norm = res is not None, aux is not None, epi == "normbwd"
    assert o_blk is None or not (has_res or has_aux or has_norm)
    n_out = 2 if epi == "relu2" else 1
    if has_norm:
        tn_fixed = n
        o_bytes, extra_bytes = 4 + 2, 4 + 4
    else:
        o_bytes = n_out * jnp.dtype(out_dtype).itemsize
        extra_bytes = (4 if has_res else 0) + (aux.dtype.itemsize if has_aux else 0)
    pa, pb, po = (2 if "a" in pairs else 1), (2 if "b" in pairs else 1), (2 if "o" in pairs else 1)
    tm, tn = _mm_tiles(m, n, k, a.dtype.itemsize * (3 if pa == 2 else 1), b.dtype.itemsize * (3 if pb == 2 else 1),
                       o_bytes * po, extra_bytes, tm_fixed, tn_fixed)
    dims = _DIMS[mode]
    if mode == "tn":
        a_spec = pl.BlockSpec((k, pa * tm), lambda i, j: (0, i))
    else:
        a_spec = pl.BlockSpec((tm, pa * k), lambda i, j: (i, 0))
    if b_blk is None:
        if mode == "nt":
            b_spec = pl.BlockSpec((tn, k), lambda i, j: (j, 0))
        else:
            b_spec = pl.BlockSpec((k, pb * tn), lambda i, j: (0, j))
    elif k_blocked and mode == "nt":
        b_spec = pl.BlockSpec((N_DEV, tn, bc), lambda i, j: (0, j, 0))
    elif k_blocked:
        b_spec = pl.BlockSpec((N_DEV, br, tn), lambda i, j: (0, 0, j))
    elif mode == "nt":
        b_spec = pl.BlockSpec((None, tn, k), lambda i, j: (j, 0, 0))
    else:
        b_spec = pl.BlockSpec((None, k, tn), lambda i, j: (j, 0, 0))
    if o_blk is None:
        o_spec = pl.BlockSpec((tm, po * tn), lambda i, j: (i, j))
        o_shape = (m, po * n)
    elif o_blk == "rows":
        o_spec = pl.BlockSpec((None, tm, tn), lambda i, j: (i, 0, j))
        o_shape = (N_DEV, tm, n)
    else:
        o_spec = pl.BlockSpec((None, tm, tn), lambda i, j: (j, i, 0))
        o_shape = (N_DEV, m, tn)

    def body(*refs):
        a_ref, b_ref = refs[0], refs[1]
        pos = 2
        res_ref = aux_ref = None
        if has_res:
            res_ref = refs[pos]
            pos += 1
        if has_aux:
            aux_ref = refs[pos]
            pos += 1
        if has_norm:
            x_ref, g_ref, dres_ref = refs[pos:pos + 3]
            pos += 3
        if after is not None:
            pos += 1
        outs = refs[pos:]
        if k_blocked and mode == "nt":
            r = None
            for d in range(N_DEV):
                part = lax.dot_general(a_ref[:, d * bc:(d + 1) * bc].astype(BF16), b_ref[d].astype(BF16), dims,
                                       preferred_element_type=F32)
                r = part if r is None else r + part
        else:
            bv = b_ref[...].reshape(k, tn) if k_blocked else b_ref[...]
            av = _compact(a_ref[...].astype(F32)) if pa == 2 else a_ref[...]
            bv = _compact(bv.astype(F32)) if pb == 2 else bv
            r = lax.dot_general(av.astype(BF16), bv.astype(BF16), dims, preferred_element_type=F32)
        if po == 2:
            r = _expand(r)
        if epi == "relu2":
            r = jnp.maximum(r, 0.0)
            outs[0][...] = r.astype(outs[0].dtype)
            outs[1][...] = (r * r).astype(outs[1].dtype)
        elif has_norm:
            xv = x_ref[...]
            rs = lax.rsqrt(jnp.mean(xv * xv, axis=1, keepdims=True) + EPS)
            xh = xv * rs
            dxh = r * g_ref[...]
            dx = rs * (dxh - xh * jnp.mean(dxh * xh, axis=1, keepdims=True)) + dres_ref[...]
            outs[0][...] = dx
            outs[1][...] = dx.astype(BF16)

            @pl.when(pl.program_id(0) == 0)
            def _():
                outs[2][...] = jnp.zeros_like(outs[2])

            outs[2][...] += jnp.sum(r * xh, axis=0, keepdims=True)
        else:
            if epi == "mul2aux":
                r = r * (2.0 * aux_ref[...].astype(F32))
            if has_res:
                r = r + res_ref[...]
            outs[0][...] = r.astype(outs[0].dtype)

    in_specs = [a_spec, b_spec]
    args = [a, b]
    if has_res:
        in_specs.append(o_spec)
        args.append(res)
    if has_aux:
        in_specs.append(o_spec)
        args.append(aux)
    vec_spec = pl.BlockSpec((1, n), lambda i, j: (0, 0))
    if has_norm:
        in_specs += [o_spec, vec_spec, o_spec]
        args += [norm[0], norm[1].reshape(1, n), norm[2]]
    if after is not None:
        in_specs.append(pl.BlockSpec(memory_space=pl.ANY))
        args.append(after)
    if has_norm:
        out_specs = [o_spec, o_spec, vec_spec]
        out_shape = [jax.ShapeDtypeStruct(o_shape, F32), jax.ShapeDtypeStruct(o_shape, BF16),
                     jax.ShapeDtypeStruct((1, n), F32)]
    else:
        out_specs = [o_spec] * n_out
        out_shape = [jax.ShapeDtypeStruct(o_shape, out_dtype)] * n_out
    out = pl.pallas_call(
        body, name=name, grid=(m // tm, n // tn),
        in_specs=in_specs, out_specs=out_specs, out_shape=out_shape, compiler_params=_cparams(),
    )(*args)
    return out if len(out) > 1 else out[0]


def _rmsnorm_fwd(xarr, colblk, width, g, name, after=None):
    rows = xarr.shape[0]
    tm = min(TM, rows)

    def body(x_ref, g_ref, *rest):
        y_ref = rest[-1]
        x = x_ref[...].astype(F32)
        r = lax.rsqrt(jnp.mean(x * x, axis=1, keepdims=True) + EPS)
        y_ref[...] = (x * r * g_ref[...]).astype(y_ref.dtype)

    in_specs = [pl.BlockSpec((tm, width), lambda i: (i, colblk)), pl.BlockSpec((1, width), lambda i: (0, 0))]
    args = [xarr, g.reshape(1, width)]
    if after is not None:
        in_specs.append(pl.BlockSpec(memory_space=pl.ANY))
        args.append(after)
    return pl.pallas_call(
        body, name=name, grid=(rows // tm,), in_specs=in_specs,
        out_specs=pl.BlockSpec((tm, width), lambda i: (i, 0)),
        out_shape=jax.ShapeDtypeStruct((rows, width), BF16), compiler_params=_cparams(),
    )(*args)


def _rmsnorm_bwd(xarr, colblk, width, g, dy, dres, out_dtype, name):
    rows = xarr.shape[0]
    tm = min(TM, rows)
    has_res = dres is not None

    def body(*refs):
        x_ref, g_ref, dy_ref = refs[0], refs[1], refs[2]
        dres_ref = refs[3] if has_res else None
        dx_ref, dg_ref = refs[-2], refs[-1]
        x = x_ref[...].astype(F32)
        dyv = dy_ref[...].astype(F32)
        r = lax.rsqrt(jnp.mean(x * x, axis=1, keepdims=True) + EPS)
        xh = x * r
        dxh = dyv * g_ref[...]
        dx = r * (dxh - xh * jnp.mean(dxh * xh, axis=1, keepdims=True))
        if has_res:
            dx = dx + dres_ref[...]
        dx_ref[...] = dx.astype(dx_ref.dtype)

        @pl.when(pl.program_id(0) == 0)
        def _():
            dg_ref[...] = jnp.zeros_like(dg_ref)

        dg_ref[...] += jnp.sum(dyv * xh, axis=0, keepdims=True)

    row_spec = pl.BlockSpec((tm, width), lambda i: (i, 0))
    vec_spec = pl.BlockSpec((1, width), lambda i: (0, 0))
    in_specs = [pl.BlockSpec((tm, width), lambda i: (i, colblk)), vec_spec, row_spec]
    args = [xarr, g.reshape(1, width), dy]
    if has_res:
        in_specs.append(row_spec)
        args.append(dres)
    return pl.pallas_call(
        body, name=name, grid=(rows // tm,), in_specs=in_specs, out_specs=[row_spec, vec_spec],
        out_shape=[jax.ShapeDtypeStruct((rows, width), out_dtype), jax.ShapeDtypeStruct((1, width), F32)],
        compiler_params=_cparams(),
    )(*args)


def _loss_head(x, g, tgt):
    rows, width = x.shape
    tm = min(TM, rows)

    def body(x_ref, g_ref, t_ref, dx_ref, dxb_ref, dg_ref, loss_ref):
        xv = x_ref[...]
        gv = g_ref[...]
        r = lax.rsqrt(jnp.mean(xv * xv, axis=1, keepdims=True) + EPS)
        xh = xv * r
        err = xh * gv - t_ref[...]
        part = 0.5 * jnp.sum(jnp.mean(err * err, axis=1, keepdims=True), axis=0, keepdims=True)
        dyv = err * (1.0 / width)
        dxh = dyv * gv
        dxv = r * (dxh - xh * jnp.mean(dxh * xh, axis=1, keepdims=True))
        dx_ref[...] = dxv
        dxb_ref[...] = dxv.astype(BF16)

        @pl.when(pl.program_id(0) == 0)
        def _():
            dg_ref[...] = jnp.zeros_like(dg_ref)
            loss_ref[...] = jnp.zeros_like(loss_ref)

        dg_ref[...] += jnp.sum(dyv * xh, axis=0, keepdims=True)
        loss_ref[...] += jnp.broadcast_to(part, loss_ref.shape)

    row_spec = pl.BlockSpec((tm, width), lambda i: (i, 0))
    vec_spec = pl.BlockSpec((1, width), lambda i: (0, 0))
    return pl.pallas_call(
        body, name="loss_head", grid=(rows // tm,), in_specs=[row_spec, vec_spec, row_spec],
        out_specs=[row_spec, row_spec, vec_spec, pl.BlockSpec((1, SLOT), lambda i: (0, 0))],
        out_shape=[jax.ShapeDtypeStruct((rows, width), F32), jax.ShapeDtypeStruct((rows, width), BF16),
                   jax.ShapeDtypeStruct((1, width), F32), jax.ShapeDtypeStruct((1, SLOT), F32)],
        compiler_params=_cparams(),
    )(x, g.reshape(1, width), tgt)


def _lane_consts():
    half = 16
    inv = ROPE_THETA ** (-(jnp.arange(half, dtype=F32) * 2.0) / 32)
    lane = jnp.arange(SLOT)
    first = (lane >= 64) & (lane < 80)
    second = (lane >= 80) & (lane < 96)
    inv_lane = jnp.where(first | second, inv[(lane - 64) % half], 0.0)
    rows = [inv_lane, (lane < 64).astype(F32), first.astype(F32), second.astype(F32)]
    rows += [jnp.zeros((SLOT,), F32)] * 4
    return jnp.stack(rows).astype(F32)


def _rope_tables(pos_col, consts):
    rows = pos_col.shape[0]
    tm = min(TM, rows)

    def body(p_ref, k_ref, c_ref, s1_ref, s2_ref):
        ang = p_ref[...] * k_ref[0:1, :]
        cos, sin = jnp.cos(ang), jnp.sin(ang)
        first, second = k_ref[2:3, :], k_ref[3:4, :]
        c_ref[...] = k_ref[1:2, :] + (first + second) * cos
        s1_ref[...] = -first * sin
        s2_ref[...] = second * sin

    spec = pl.BlockSpec((tm, SLOT), lambda i: (i, 0))
    shp = jax.ShapeDtypeStruct((rows, SLOT), F32)
    return pl.pallas_call(
        body, name="rope_tables", grid=(rows // tm,),
        in_specs=[pl.BlockSpec((tm, 1), lambda i: (i, 0)), pl.BlockSpec((8, SLOT), lambda i: (0, 0))],
        out_specs=[spec, spec, spec], out_shape=[shp, shp, shp], compiler_params=_cparams(),
    )(pos_col, consts)


def _rot(xv, c, s1, s2):
    return xv * c + pltpu.roll(xv, SLOT - 16, 1) * s1 + pltpu.roll(xv, 16, 1) * s2


def _rot_t(dy, c, s1, s2):
    return dy * c + pltpu.roll(dy * s1, 16, 1) + pltpu.roll(dy * s2, SLOT - 16, 1)


def _mla_rope_fwd(qraw, kvraw, proj, tabs):
    rows = qraw.shape[0]
    tm = min(256, rows)
    hw = MLA_HEADS * SLOT

    def body(q_ref, kv_ref, kr_ref, c_ref, s1_ref, s2_ref, qo, ko, vo):
        c, s1, s2 = c_ref[...], s1_ref[...], s2_ref[...]
        kr = _rot(kr_ref[...], c, s1, s2)
        low = lax.broadcasted_iota(jnp.int32, (tm, SLOT), 1) < HEAD_DIM
        for h in range(MLA_HEADS):
            sl = slice(h * SLOT, (h + 1) * SLOT)
            qo[:, sl] = (_rot(q_ref[:, sl], c, s1, s2) * MLA_Q_SCALE).astype(BF16)
            kvh = kv_ref[:, sl]
            ko[:, sl] = (jnp.where(low, kvh, 0.0) + kr).astype(BF16)
            vo[:, sl] = pltpu.roll(jnp.where(low, 0.0, kvh), HEAD_DIM, 1).astype(BF16)

    tab = pl.BlockSpec((tm, SLOT), lambda i: (i, 0))
    wide = pl.BlockSpec((tm, hw), lambda i: (i, 0))
    shp = jax.ShapeDtypeStruct((rows, hw), BF16)
    return pl.pallas_call(
        body, name="mla_rope_fwd", grid=(rows // tm,),
        in_specs=[wide, wide, pl.BlockSpec((tm, SLOT), lambda i: (i, 3)),
                  tab, tab, tab],
        out_specs=[wide, wide, wide], out_shape=[shp, shp, shp], compiler_params=_cparams(),
    )(qraw, kvraw, proj, *tabs)


def _mla_rope_bwd(dq, dk, dv, tabs, consts):
    rows = dq.shape[0]
    tm = min(256, rows)
    hw = MLA_HEADS * SLOT

    def body(dq_ref, dk_ref, dv_ref, c_ref, s1_ref, s2_ref, k_ref, dqo, dkvo, dkro):
        c, s1, s2 = c_ref[...], s1_ref[...], s2_ref[...]
        ksum = jnp.zeros((tm, SLOT), F32)
        low = lax.broadcasted_iota(jnp.int32, (tm, SLOT), 1) < HEAD_DIM
        for h in range(MLA_HEADS):
            sl = slice(h * SLOT, (h + 1) * SLOT)
            dqo[:, sl] = _rot_t(dq_ref[:, sl], c, s1, s2).astype(BF16)
            dkh = dk_ref[:, sl]
            ksum = ksum + dkh
            dvh = pltpu.roll(jnp.where(low, dv_ref[:, sl], 0.0), HEAD_DIM, 1)
            dkvo[:, sl] = (jnp.where(low, dkh, 0.0) + dvh).astype(BF16)
        dkro[...] = _rot_t(ksum, c, s1, s2) * (k_ref[2:3, :] + k_ref[3:4, :])

    tab = pl.BlockSpec((tm, SLOT), lambda i: (i, 0))
    wide = pl.BlockSpec((tm, hw), lambda i: (i, 0))
    return pl.pallas_call(
        body, name="mla_rope_bwd", grid=(rows // tm,),
        in_specs=[wide, wide, wide, tab, tab, tab, pl.BlockSpec((8, SLOT), lambda i: (0, 0))],
        out_specs=[wide, wide, tab],
        out_shape=[jax.ShapeDtypeStruct((rows, hw), BF16), jax.ShapeDtypeStruct((rows, hw), BF16),
                   jax.ShapeDtypeStruct((rows, SLOT), F32)],
        compiler_params=_cparams(),
    )(dq, dk, dv, *tabs, consts)


def _nt(a, b):
    return lax.dot_general(a, b, _DIMS["nt"], preferred_element_type=F32)


def _tn(a, b):
    return lax.dot_general(a, b, _DIMS["tn"], preferred_element_type=F32)


def _nn(a, b):
    return lax.dot_general(a, b, _DIMS["nn"], preferred_element_type=F32)


def _mla_attn_fwd(q, k, v):
    rows = q.shape[0]
    t = min(TQ_MLA, rows)
    nt = rows // t
    wide = MLA_PACK * SLOT

    def body(q_ref, k_ref, v_ref, o_ref, lse_ref, m_sc, l_sc, acc_sc):
        i, j = pl.program_id(1), pl.program_id(2)

        @pl.when(j == 0)
        def _():
            m_sc[...] = jnp.full_like(m_sc, NEG)
            l_sc[...] = jnp.zeros_like(l_sc)
            acc_sc[...] = jnp.zeros_like(acc_sc)

        def step(diagonal):
            for hh in range(MLA_PACK):
                sl = slice(hh * SLOT, (hh + 1) * SLOT)
                s = _nt(k_ref[:, sl], q_ref[:, sl])
                if diagonal:
                    key = lax.broadcasted_iota(jnp.int32, (t, t), 0)
                    s = jnp.where(key <= lax.broadcasted_iota(jnp.int32, (t, t), 1), s, NEG)
                m_prev = m_sc[hh]
                m_new = jnp.maximum(m_prev, jnp.max(s, axis=0, keepdims=True))
                p = jnp.exp2(s - m_new)
                alpha = jnp.exp2(m_prev - m_new)
                l_new = alpha * l_sc[hh] + jnp.sum(p, axis=0, keepdims=True)
                acc = alpha * acc_sc[hh] + _tn(v_ref[:, sl], p.astype(BF16))
                if diagonal:
                    o_ref[:, sl] = (acc / l_new).T.astype(o_ref.dtype)
                    lse_ref[hh:hh + 1, :] = m_new + jnp.log(l_new) * LOG2_E
                else:
                    m_sc[hh] = m_new
                    l_sc[hh] = l_new
                    acc_sc[hh] = acc

        @pl.when(j < i)
        def _():
            step(False)

        @pl.when(j == i)
        def _():
            lse_ref[...] = jnp.zeros_like(lse_ref)
            step(True)

    q_spec = pl.BlockSpec((t, wide), lambda h, i, j: (i, h))
    kv_spec = pl.BlockSpec((t, wide), lambda h, i, j: (jnp.minimum(j, i), h))
    return pl.pallas_call(
        body, name="mla_attn_fwd", grid=(MLA_HEADS // MLA_PACK, nt, nt),
        in_specs=[q_spec, kv_spec, kv_spec],
        out_specs=[q_spec, pl.BlockSpec((None, 8, t), lambda h, i, j: (h, 0, i))],
        out_shape=[jax.ShapeDtypeStruct(q.shape, BF16),
                   jax.ShapeDtypeStruct((MLA_HEADS // MLA_PACK, 8, rows), F32)],
        scratch_shapes=[pltpu.VMEM((MLA_PACK, 1, t), F32), pltpu.VMEM((MLA_PACK, 1, t), F32),
                        pltpu.VMEM((MLA_PACK, SLOT, t), F32)],
        compiler_params=_cparams(),
    )(q, k, v)


def _mla_delta(o, do):
    rows = o.shape[0]
    t = min(TQ_MLA, rows)
    wide = MLA_PACK * SLOT

    def body(o_ref, do_ref, d_ref):
        d_ref[...] = jnp.zeros_like(d_ref)
        for hh in range(MLA_PACK):
            sl = slice(hh * SLOT, (hh + 1) * SLOT)
            d = jnp.sum(do_ref[:, sl].astype(F32) * o_ref[:, sl].astype(F32), axis=1, keepdims=True)
            d_ref[hh:hh + 1, :] = jnp.broadcast_to(d, (t, SLOT)).T[0:1, :]

    spec = pl.BlockSpec((t, wide), lambda h, i: (i, h))
    return pl.pallas_call(
        body, name="mla_delta", grid=(MLA_HEADS // MLA_PACK, rows // t), in_specs=[spec, spec],
        out_specs=pl.BlockSpec((None, 8, t), lambda h, i: (h, 0, i)),
        out_shape=jax.ShapeDtypeStruct((MLA_HEADS // MLA_PACK, 8, rows), F32), compiler_params=_cparams(),
    )(o, do)


def _mla_attn_bwd(q, k, v, do, lse, delta, after):
    rows = q.shape[0]
    t = min(TQ_MLA, rows)
    nt = rows // t
    wide = MLA_PACK * SLOT

    def body(q_ref, k_ref, v_ref, do_ref, lse_ref, delta_ref, after_ref, dq_ref, dk_ref, dv_ref, dk_sc, dv_sc):
        j, i = pl.program_id(1), pl.program_id(2)

        @pl.when((j == 0) & (i == 0))
        def _():
            dq_ref[...] = jnp.zeros_like(dq_ref)

        @pl.when(i == 0)
        def _():
            dk_sc[...] = jnp.zeros_like(dk_sc)
            dv_sc[...] = jnp.zeros_like(dv_sc)

        def chunk(hh, rows, keys, masked):
            sl = slice(hh * SLOT, (hh + 1) * SLOT)
            n_rows = rows.stop - rows.start
            qv, kv, dov = q_ref[rows, sl], k_ref[keys, sl], do_ref[rows, sl]
            s = _nt(kv, qv)
            if masked:
                shp = (keys.stop - keys.start, n_rows)
                s = jnp.where(keys.start + lax.broadcasted_iota(jnp.int32, shp, 0)
                              <= rows.start + lax.broadcasted_iota(jnp.int32, shp, 1), s, NEG)
            p = jnp.exp2(s - lse_ref[hh:hh + 1, rows])
            dp = _nt(v_ref[keys, sl], dov)
            ds = (p * (dp - delta_ref[hh:hh + 1, rows])).astype(BF16)
            dv_sc[keys, sl] += _nn(p.astype(BF16), dov)
            dk_sc[keys, sl] += _nn(ds, qv)
            r0 = pl.multiple_of(i * t + rows.start, n_rows)
            dq_ref[pl.ds(r0, n_rows), sl] += _tn(ds, kv) * MLA_SCALE

        @pl.when(i > j)
        def _():
            for hh in range(MLA_PACK):
                chunk(hh, slice(0, t), slice(0, t), False)

        @pl.when(i == j)
        def _():
            for hh in range(MLA_PACK):
                chunk(hh, slice(0, t), slice(0, t // 2), True)
                chunk(hh, slice(t // 2, t), slice(t // 2, t), True)

        @pl.when(i == nt - 1)
        def _():
            dk_ref[...] = dk_sc[...] * (1.0 / LOG2_E)
            dv_ref[...] = dv_sc[...]

    q_spec = pl.BlockSpec((t, wide), lambda h, j, i: (jnp.maximum(i, j), h))
    kv_spec = pl.BlockSpec((t, wide), lambda h, j, i: (j, h))
    row_spec = pl.BlockSpec((None, 8, t), lambda h, j, i: (h, 0, jnp.maximum(i, j)))
    head_spec = pl.BlockSpec((rows, wide), lambda h, j, i: (0, h))
    shp = jax.ShapeDtypeStruct(q.shape, F32)
    return pl.pallas_call(
        body, name="mla_attn_bwd", grid=(MLA_HEADS // MLA_PACK, nt, nt),
        in_specs=[q_spec, kv_spec, kv_spec, q_spec, row_spec, row_spec, pl.BlockSpec(memory_space=pl.ANY)],
        out_specs=[head_spec, kv_spec, kv_spec], out_shape=[shp, shp, shp],
        scratch_shapes=[pltpu.VMEM((t, wide), F32), pltpu.VMEM((t, wide), F32)],
        compiler_params=_cparams(),
    )(q, k, v, do, lse, delta, after)


def _swa_specs(t):
    def prev(i):
        return jnp.maximum(i - 1, 0)
    kw = SWA_PACK * SLOT
    k0, v0 = SWA_HEADS // SWA_PACK, (SWA_HEADS + SWA_KV_HEADS) // SWA_PACK
    q3 = pl.BlockSpec((t, SWA_PACK * SWA_GROUP * SLOT), lambda h, i: (i, h))
    kp = pl.BlockSpec((t, kw), lambda h, i: (prev(i), k0 + h))
    kc = pl.BlockSpec((t, kw), lambda h, i: (i, k0 + h))
    vp = pl.BlockSpec((t, kw), lambda h, i: (prev(i), v0 + h))
    vc = pl.BlockSpec((t, kw), lambda h, i: (i, v0 + h))
    pcol = pl.BlockSpec((t, 1), lambda h, i: (i, 0))
    prow_p = pl.BlockSpec((1, t), lambda h, i: (0, prev(i)))
    prow_c = pl.BlockSpec((1, t), lambda h, i: (0, i))
    return [q3, kp, kc, vp, vc, pcol, prow_p, prow_c]


def _stack(ref, first):
    return jnp.concatenate([ref[:, (first + g) * SLOT:(first + g + 1) * SLOT] for g in range(SWA_GROUP)], axis=0)


def _swa_logits(q3, kp, kc, pq, pkp, pkc, slope_ref, kvh, i, t):
    r = lax.broadcasted_iota(jnp.int32, (t, t), 0)
    c = lax.broadcasted_iota(jnp.int32, (t, t), 1)
    ok_c = c <= r
    ok_p = (c - r) > jnp.where(i > 0, 0, t)
    dist_p, dist_c = pq - pkp, pq - pkc
    s_p3 = _nt(q3, kp) * (HEAD_DIM ** -0.5)
    s_c3 = _nt(q3, kc) * (HEAD_DIM ** -0.5)
    out = []
    for g in range(SWA_GROUP):
        slope = slope_ref[kvh * SWA_GROUP + g]
        rows = slice(g * t, (g + 1) * t)
        out.append((jnp.where(ok_p, s_p3[rows] - slope * dist_p, NEG),
                    jnp.where(ok_c, s_c3[rows] - slope * dist_c, NEG)))
    return out


def _swa_attn_fwd(proj, pos_col, pos_row, slopes, sinks):
    rows = proj.shape[0]
    t = WINDOW
    hw = SWA_HEADS * SLOT

    def body(slope_ref, sink_ref, q_ref, kp_ref, kc_ref, vp_ref, vc_ref, pq_ref, pkp_ref, pkc_ref, o_ref, lse_ref):
        i = pl.program_id(1)
        for kv in range(SWA_PACK):
            kvh = pl.program_id(0) * SWA_PACK + kv
            ksl = slice(kv * SLOT, (kv + 1) * SLOT)
            logits = _swa_logits(_stack(q_ref, kv * SWA_GROUP), kp_ref[:, ksl], kc_ref[:, ksl], pq_ref[...],
                                 pkp_ref[...], pkc_ref[...], slope_ref, kvh, i, t)
            e_p, e_c, norm = [], [], []
            for g, (s_p, s_c) in enumerate(logits):
                sl = slice((kv * SWA_GROUP + g) * SLOT, (kv * SWA_GROUP + g + 1) * SLOT)
                sink = sink_ref[kvh * SWA_GROUP + g]
                m = jnp.maximum(jnp.maximum(jnp.max(s_p, axis=1, keepdims=True),
                                            jnp.max(s_c, axis=1, keepdims=True)), sink)
                ep, ec = jnp.exp(s_p - m), jnp.exp(s_c - m)
                l = jnp.sum(ep, axis=1, keepdims=True) + jnp.sum(ec, axis=1, keepdims=True) + jnp.exp(sink - m)
                e_p.append(ep.astype(BF16))
                e_c.append(ec.astype(BF16))
                norm.append(l)
                lse_ref[:, sl] = jnp.broadcast_to(m + jnp.log(l), (t, SLOT))
            acc = (_nn(jnp.concatenate(e_p, axis=0), vp_ref[:, ksl])
                   + _nn(jnp.concatenate(e_c, axis=0), vc_ref[:, ksl]))
            for g in range(SWA_GROUP):
                sl = slice((kv * SWA_GROUP + g) * SLOT, (kv * SWA_GROUP + g + 1) * SLOT)
                o_ref[:, sl] = (acc[g * t:(g + 1) * t] / norm[g]).astype(o_ref.dtype)

    smem = pl.BlockSpec(memory_space=pltpu.SMEM)
    out_spec = pl.BlockSpec((t, SWA_PACK * SWA_GROUP * SLOT), lambda h, i: (i, h))
    return pl.pallas_call(
        body, name="swa_attn_fwd", grid=(SWA_KV_HEADS // SWA_PACK, rows // t),
        in_specs=[smem, smem] + _swa_specs(t), out_specs=[out_spec, out_spec],
        out_shape=[jax.ShapeDtypeStruct((rows, hw), BF16), jax.ShapeDtypeStruct((rows, hw), F32)],
        compiler_params=_cparams(),
    )(slopes, sinks, proj, proj, proj, proj, proj, pos_col, pos_row, pos_row)


def _swa_attn_bwd(proj, o, do, lse, pos_col, pos_row, slopes, sinks, after):
    rows = proj.shape[0]
    t = WINDOW
    hw = SWA_HEADS * SLOT
    scale = HEAD_DIM ** -0.5

    def body(slope_ref, sink_ref, q_ref, kp_ref, kc_ref, vp_ref, vc_ref, pq_ref, pkp_ref, pkc_ref,
             o_ref, do_ref, lse_ref, after_ref, dq_ref, dk_ref, dv_ref, dsink_ref):
        i = pl.program_id(1)

        @pl.when(i == 0)
        def _():
            dk_ref[...] = jnp.zeros_like(dk_ref)
            dv_ref[...] = jnp.zeros_like(dv_ref)
            dsink_ref[...] = jnp.zeros_like(dsink_ref)

        r_c = pl.multiple_of(i * t, t)
        r_p = pl.multiple_of(jnp.maximum(i - 1, 0) * t, t)
        for kv in range(SWA_PACK):
            kvh = pl.program_id(0) * SWA_PACK + kv
            ksl = slice(kv * SLOT, (kv + 1) * SLOT)
            q3, do3 = _stack(q_ref, kv * SWA_GROUP), _stack(do_ref, kv * SWA_GROUP)
            logits = _swa_logits(q3, kp_ref[:, ksl], kc_ref[:, ksl], pq_ref[...], pkp_ref[...], pkc_ref[...],
                                 slope_ref, kvh, i, t)
            dp_p3, dp_c3 = _nt(do3, vp_ref[:, ksl]), _nt(do3, vc_ref[:, ksl])
            p_p, p_c, ds_p, ds_c = [], [], [], []
            for g, (s_p, s_c) in enumerate(logits):
                head = kv * SWA_GROUP + g
                sl = slice(head * SLOT, (head + 1) * SLOT)
                rws = slice(g * t, (g + 1) * t)
                lse_g = lse_ref[:, head * SLOT:head * SLOT + 1]
                pp, pc = jnp.exp(s_p - lse_g), jnp.exp(s_c - lse_g)
                delta = jnp.sum(do_ref[:, sl].astype(F32) * o_ref[:, sl].astype(F32), axis=1, keepdims=True)
                p_p.append(pp.astype(BF16))
                p_c.append(pc.astype(BF16))
                ds_p.append((pp * (dp_p3[rws] - delta)).astype(BF16))
                ds_c.append((pc * (dp_c3[rws] - delta)).astype(BF16))
                sink = sink_ref[kvh * SWA_GROUP + g]
                dsink = -jnp.sum(jnp.exp(sink - lse_g) * delta, axis=0, keepdims=True)
                dsink_ref[head * 8:(head + 1) * 8, :] += jnp.broadcast_to(dsink, (8, SLOT))
            p_p3, p_c3 = jnp.concatenate(p_p, axis=0), jnp.concatenate(p_c, axis=0)
            ds_p3, ds_c3 = jnp.concatenate(ds_p, axis=0), jnp.concatenate(ds_c, axis=0)
            dq3 = (_nn(ds_p3, kp_ref[:, ksl]) + _nn(ds_c3, kc_ref[:, ksl])) * scale
            for g in range(SWA_GROUP):
                head = kv * SWA_GROUP + g
                dq_ref[:, head * SLOT:(head + 1) * SLOT] = dq3[g * t:(g + 1) * t]
            dk_ref[pl.ds(r_c, t), ksl] += _tn(ds_c3, q3) * scale
            dv_ref[pl.ds(r_c, t), ksl] += _tn(p_c3, do3)
            dk_ref[pl.ds(r_p, t), ksl] += _tn(ds_p3, q3) * scale
            dv_ref[pl.ds(r_p, t), ksl] += _tn(p_p3, do3)

    smem = pl.BlockSpec(memory_space=pltpu.SMEM)
    qlike = pl.BlockSpec((t, SWA_PACK * SWA_GROUP * SLOT), lambda h, i: (i, h))
    kv_out = pl.BlockSpec((rows, SWA_PACK * SLOT), lambda h, i: (0, h))
    return pl.pallas_call(
        body, name="swa_attn_bwd", grid=(SWA_KV_HEADS // SWA_PACK, rows // t),
        in_specs=[smem, smem] + _swa_specs(t) + [qlike, qlike, qlike, pl.BlockSpec(memory_space=pl.ANY)],
        out_specs=[qlike, kv_out, kv_out,
                   pl.BlockSpec((SWA_PACK * SWA_GROUP * 8, SLOT), lambda h, i: (h, 0))],
        out_shape=[jax.ShapeDtypeStruct((rows, hw), F32), jax.ShapeDtypeStruct((rows, SWA_KV_HEADS * SLOT), F32),
                   jax.ShapeDtypeStruct((rows, SWA_KV_HEADS * SLOT), F32),
                   jax.ShapeDtypeStruct((SWA_HEADS * 8, SLOT), F32)],
        compiler_params=_cparams(),
    )(slopes, sinks, proj, proj, proj, proj, proj, pos_col, pos_row, pos_row, o, do, lse, after)


def _cross_attn_fwd(proj, qoff, kvmem):
    rows = proj.shape[0]
    t = min(TQ_CROSS, rows)

    def body(q_ref, k_ref, v_ref, o_ref):
        s = _nt(q_ref[...].astype(BF16), k_ref[...]) * (HEAD_DIM ** -0.5)
        e = jnp.exp(s - jnp.max(s, axis=1, keepdims=True))
        p = e / jnp.sum(e, axis=1, keepdims=True)
        o_ref[...] = _nn(p.astype(BF16), v_ref[...]).astype(o_ref.dtype)

    return pl.pallas_call(
        body, name="cross_attn_fwd", grid=(rows // t, MEM_HEADS),
        in_specs=[pl.BlockSpec((t, SLOT), lambda i, h: (i, qoff + h)),
                  pl.BlockSpec((N_MEM, SLOT), lambda i, h: (0, h)),
                  pl.BlockSpec((N_MEM, SLOT), lambda i, h: (0, MEM_HEADS + h))],
        out_specs=pl.BlockSpec((t, SLOT), lambda i, h: (i, h)),
        out_shape=jax.ShapeDtypeStruct((rows, MEM_HEADS * SLOT), BF16), compiler_params=_cparams(),
    )(proj, kvmem, kvmem)


def _cross_attn_bwd(proj, qoff, kvmem, do):
    rows = proj.shape[0]
    t = min(TQ_CROSS, rows)
    scale = HEAD_DIM ** -0.5

    def body(q_ref, k_ref, v_ref, do_ref, dq_ref, dk_ref, dv_ref):
        @pl.when(pl.program_id(1) == 0)
        def _():
            dk_ref[...] = jnp.zeros_like(dk_ref)
            dv_ref[...] = jnp.zeros_like(dv_ref)

        qv, kv, dov = q_ref[...].astype(BF16), k_ref[...], do_ref[...]
        s = _nt(qv, kv) * scale
        e = jnp.exp(s - jnp.max(s, axis=1, keepdims=True))
        p = e / jnp.sum(e, axis=1, keepdims=True)
        dp = _nt(dov, v_ref[...])
        ds = (p * (dp - jnp.sum(p * dp, axis=1, keepdims=True))).astype(BF16)
        dq_ref[...] = _nn(ds, kv) * scale
        dk_ref[...] += _tn(ds, qv) * scale
        dv_ref[...] += _tn(p.astype(BF16), dov)

    mem_out = pl.BlockSpec((N_MEM, SLOT), lambda h, i: (0, h))
    return pl.pallas_call(
        body, name="cross_attn_bwd", grid=(MEM_HEADS, rows // t),
        in_specs=[pl.BlockSpec((t, SLOT), lambda h, i: (i, qoff + h)),
                  pl.BlockSpec((N_MEM, SLOT), lambda h, i: (0, h)),
                  pl.BlockSpec((N_MEM, SLOT), lambda h, i: (0, MEM_HEADS + h)),
                  pl.BlockSpec((t, SLOT), lambda h, i: (i, h))],
        out_specs=[pl.BlockSpec((t, SLOT), lambda h, i: (i, h)), mem_out, mem_out],
        out_shape=[jax.ShapeDtypeStruct((rows, MEM_HEADS * SLOT), F32),
                   jax.ShapeDtypeStruct((N_MEM, MEM_HEADS * SLOT), F32),
                   jax.ShapeDtypeStruct((N_MEM, MEM_HEADS * SLOT), F32)],
        compiler_params=_cparams(),
    )(proj, kvmem, kvmem, do)


def _place():
    return lax.axis_index("x"), lax.axis_index("y"), lax.axis_index("c")


def _flip(v, bit):
    return 1 - v if bit else v


def _all_gather(blocks, name):
    nb = len(blocks)

    def body(*refs):
        x_refs, out_refs = refs[:nb], refs[nb:2 * nb]
        send_sems, recv_sems, local_sems = refs[2 * nb:]
        x, y, c = _place()
        me, sibling = (x, y, c), (x, y, 1 - c)
        chips = [(1 - x, y), (x, 1 - y), (1 - x, 1 - y)]

        def copy(b, k, blk, to, from_input=False):
            slot = out_refs[b].at[4 * blk[0] + 2 * blk[1] + blk[2]]
            return pltpu.make_async_remote_copy(
                src_ref=x_refs[b] if from_input else slot, dst_ref=slot,
                send_sem=send_sems.at[b, k], recv_sem=recv_sems.at[b, k],
                device_id=to, device_id_type=pl.DeviceIdType.MESH)

        mine = [pltpu.make_async_copy(x_refs[b], out_refs[b].at[4 * x + 2 * y + c], local_sems.at[b])
                for b in range(nb)]
        for cp in mine:
            cp.start()
        first = []
        for b in range(nb):
            first.append(copy(b, 0, me, sibling, from_input=True))
            first += [copy(b, 1 + n, me, (*chip, c), from_input=True) for n, chip in enumerate(chips)]
        for cp in first:
            cp.start()
        passed = []
        for n, chip in enumerate(chips):
            for b in range(nb):
                copy(b, 1 + n, (*chip, c), me).wait_recv()
                passed.append(copy(b, 4 + n, (*chip, c), sibling))
                passed[-1].start()
        for b in range(nb):
            copy(b, 0, sibling, me).wait_recv()
            for n, chip in enumerate(chips):
                copy(b, 4 + n, (*chip, 1 - c), me).wait_recv()
        for cp in first + passed:
            cp.wait_send()
        for cp in mine:
            cp.wait()

    any_spec = pl.BlockSpec(memory_space=pl.ANY)
    return pl.pallas_call(
        body, name=name, in_specs=[any_spec] * nb, out_specs=[any_spec] * nb,
        out_shape=[jax.ShapeDtypeStruct((N_DEV,) + blk.shape, blk.dtype) for blk in blocks],
        scratch_shapes=[pltpu.SemaphoreType.DMA((nb, 7)), pltpu.SemaphoreType.DMA((nb, 7)),
                        pltpu.SemaphoreType.DMA((nb,))],
    )(*blocks)


def _peers(x, y, c):
    out = []
    for n in range(1, N_DEV):
        peer = (_flip(x, n & 4), _flip(y, n & 2), _flip(c, n & 1))
        out.append((n - 1, peer, 4 * peer[0] + 2 * peer[1] + peer[2]))
    return out


_HBM = pl.BlockSpec(memory_space=pltpu.HBM)
_SEM = pl.BlockSpec(memory_space=pltpu.SEMAPHORE)


def _exchange_start(srcs, scatter, name, after=None):
    ns = len(srcs)
    lands = [lax.empty(s.shape if scatter else (N_DEV,) + s.shape, s.dtype) for s in srcs]

    def body(*refs):
        src_refs, land_refs = refs[:ns], refs[ns:2 * ns]
        pos = 2 * ns + (1 if after is not None else 0)
        send_sems, recv_sems, token = refs[pos], refs[pos + 1], refs[-1]
        x, y, c = _place()
        my_idx = 4 * x + 2 * y + c
        for col, peer, peer_idx in _peers(x, y, c):
            for b in range(ns):
                pltpu.make_async_remote_copy(
                    src_ref=src_refs[b].at[peer_idx] if scatter else src_refs[b], dst_ref=land_refs[b].at[my_idx],
                    send_sem=send_sems.at[b * (N_DEV - 1) + col], recv_sem=recv_sems.at[b * (N_DEV - 1) + col],
                    device_id=peer, device_id_type=pl.DeviceIdType.MESH).start()
        token[...] = jnp.zeros_like(token)

    args = [pltpu.with_memory_space_constraint(a, pltpu.HBM) for a in list(srcs) + lands]
    in_specs = [_HBM] * (2 * ns)
    if after is not None:
        args.append(after)
        in_specs.append(pl.BlockSpec(memory_space=pl.ANY))
    out = pl.pallas_call(
        body, name=name, in_specs=in_specs,
        out_specs=[_SEM, _SEM] + [_HBM] * (2 * ns) + [pl.BlockSpec(memory_space=pltpu.VMEM)],
        out_shape=[pltpu.SemaphoreType.DMA((ns * (N_DEV - 1),)), pltpu.SemaphoreType.DMA((ns * (N_DEV - 1),))]
        + [pltpu.HBM(a.shape, a.dtype) for a in list(srcs) + lands] + [jax.ShapeDtypeStruct((8, SLOT), F32)],
        input_output_aliases={k: 2 + k for k in range(2 * ns)},
        compiler_params=pltpu.CompilerParams(has_side_effects=pltpu.SideEffectType.DATAFLOW_SIDE_EFFECTING),
    )(*args)
    return (out[0], out[1], out[2:2 + ns], out[2 + ns:2 + 2 * ns], scatter), out[-1]


def _exchange_wait(handle, after, name):
    send_sems, recv_sems, srcs, lands, scatter = handle
    ns = len(srcs)

    def body(*refs):
        src_refs, land_refs = refs[:ns], refs[ns:2 * ns]
        send_ref, recv_ref = refs[2 * ns], refs[2 * ns + 1]
        x, y, c = _place()
        for col, peer, peer_idx in _peers(x, y, c):
            for b in range(ns):
                copy = pltpu.make_async_remote_copy(
                    src_ref=src_refs[b].at[peer_idx] if scatter else src_refs[b], dst_ref=land_refs[b].at[peer_idx],
                    send_sem=send_ref.at[b * (N_DEV - 1) + col], recv_sem=recv_ref.at[b * (N_DEV - 1) + col],
                    device_id=peer, device_id_type=pl.DeviceIdType.MESH)
                copy.wait_send()
                copy.wait_recv()

    out = pl.pallas_call(
        body, name=name, in_specs=[_HBM] * (2 * ns) + [_SEM, _SEM, pl.BlockSpec(memory_space=pl.ANY)],
        out_specs=[_HBM] * (2 * ns),
        out_shape=[pltpu.HBM(a.shape, a.dtype) for a in list(srcs) + list(lands)],
        input_output_aliases={k: k for k in range(2 * ns)},
        compiler_params=pltpu.CompilerParams(has_side_effects=pltpu.SideEffectType.DATAFLOW_SIDE_EFFECTING),
    )(*srcs, *lands, send_sems, recv_sems, after)
    my_idx = 4 * lax.axis_index("x") + 2 * lax.axis_index("y") + lax.axis_index("c")
    landed = []
    for src, land in zip(out[:ns], out[ns:]):
        own = lax.dynamic_index_in_dim(src, my_idx, 0, keepdims=True) if scatter else src[None]
        landed.append(lax.dynamic_update_index_in_dim(land, own, my_idx, 0))
    return landed


def _adamw(parts, w, m, v, name):
    lyr, rows, cols = w.shape
    assert len(parts) == lyr
    tr = ADAM_ROWS if cols > 512 else 2 * ADAM_ROWS
    while rows % tr:
        tr //= 2
    tr = min(tr, rows)

    def body(*refs):
        p_refs = refs[:lyr]
        w_ref, m_ref, v_ref, g_out, d_out, m_out, v_out = refs[lyr:]
        for k in range(lyr):
            @pl.when(pl.program_id(0) == k)
            def _(p_ref=p_refs[k]):
                g = p_ref[0].astype(F32)
                for s in range(1, N_DEV):
                    g = g + p_ref[s].astype(F32)
                m2 = ADAM_B1 * m_ref[...] + (1.0 - ADAM_B1) * g
                v2 = ADAM_B2 * v_ref[...] + (1.0 - ADAM_B2) * (g * g)
                m_hat = m2 / (1.0 - ADAM_B1 ** ADAM_STEP)
                v_hat = v2 / (1.0 - ADAM_B2 ** ADAM_STEP)
                g_out[...] = g
                d_out[...] = -ADAM_LR * (m_hat / (jnp.sqrt(v_hat) + ADAM_EPS) + ADAM_WD * w_ref[...])
                m_out[...] = m2
                v_out[...] = v2

    def part_spec(k):
        return pl.BlockSpec((N_DEV, tr, cols), lambda l, i: (0, jnp.where(l == k, i, 0), 0))

    spec = pl.BlockSpec((None, tr, cols), lambda l, i: (l, i, 0))
    shp = jax.ShapeDtypeStruct((lyr, rows, cols), F32)
    return pl.pallas_call(
        body, name=name, grid=(lyr, rows // tr),
        in_specs=[part_spec(k) for k in range(lyr)] + [spec, spec, spec],
        out_specs=[spec] * 4, out_shape=[shp] * 4, compiler_params=_cparams(),
    )(*parts, w, m, v)


def _pack(arrays, lanes, row_mult, dtype):
    flat = jnp.concatenate([a.reshape(-1).astype(dtype) for a in arrays])
    unit = lanes * row_mult
    total = -(-flat.shape[0] // unit) * unit
    return jnp.pad(flat, (0, total - flat.shape[0])).reshape(total // lanes, lanes)


def _unpack(packed, shapes):
    flat = packed.reshape(-1)
    out, off = [], 0
    for shp in shapes:
        n = 1
        for d in shp:
            n *= d
        out.append(flat[off:off + n].reshape(shp))
        off += n
    return out


def _pad_slots(w, axis):
    axis = axis % w.ndim
    n = w.shape[axis] // HEAD_DIM
    shp = w.shape[:axis] + (n, HEAD_DIM) + w.shape[axis + 1:]
    pad = [(0, 0)] * (w.ndim + 1)
    pad[axis + 1] = (0, SLOT - HEAD_DIM)
    return jnp.pad(w.reshape(shp), pad).reshape(w.shape[:axis] + (n * SLOT,) + w.shape[axis + 1:])


def _unpad_slots(w, axis, keep=HEAD_DIM):
    axis = axis % w.ndim
    n = w.shape[axis] // SLOT
    shp = w.shape[:axis] + (n, SLOT) + w.shape[axis + 1:]
    idx = [slice(None)] * (w.ndim + 1)
    idx[axis + 1] = slice(0, keep)
    return w.reshape(shp)[tuple(idx)].reshape(w.shape[:axis] + (n * keep,) + w.shape[axis + 1:])


def _mla_in_pad(w):
    z = functools.partial(jnp.zeros, dtype=w.dtype)
    rows = w.shape[0]
    return jnp.concatenate([w[:, :384], z((rows, 64)), w[:, 640:672], z((rows, 32)), w[:, 384:640],
                            _pad_slots(w[:, 672:], 1)], axis=1)


def _mla_in_unpad(d):
    return jnp.concatenate([d[:, :384], d[:, 512:768], d[:, 448:480], _unpad_slots(d[:, 768:], 1)], axis=1)


def _mla_uq_pad(w):
    return jnp.pad(w.reshape(w.shape[0], MLA_HEADS, MLA_QK), ((0, 0), (0, 0), (0, SLOT - MLA_QK))).reshape(
        w.shape[0], MLA_HEADS * SLOT)


def _join(gathered, axis):
    nd, a, b = gathered.shape
    if axis == 1:
        return gathered.reshape(nd * a, b)
    return gathered.transpose(1, 0, 2).reshape(a, nd * b)


def _split(full, axis):
    r, c = full.shape
    if axis == 1:
        return full.reshape(N_DEV, r // N_DEV, c).astype(BF16)
    return full.reshape(r, N_DEV, c // N_DEV).transpose(1, 0, 2).astype(BF16)


def kernel(x, mem, positions, attn_norm_g, mlp_norm_g, mem_norm_g, final_norm_g, mla_w_in, mla_q_norm_g, mla_kv_norm_g, mla_w_uq, mla_w_ukv, swa_w_in, swa_sinks, w_mem_kv, w_o, mlp_w_up, mlp_w_down, loss_target, m_attn_norm_g, m_mlp_norm_g, m_mem_norm_g, m_final_norm_g, m_mla_w_in, m_mla_q_norm_g, m_mla_kv_norm_g, m_mla_w_uq, m_mla_w_ukv, m_swa_w_in, m_swa_sinks, m_w_mem_kv, m_w_o, m_mlp_w_up, m_mlp_w_down, v_attn_norm_g, v_mlp_norm_g, v_mem_norm_g, v_final_norm_g, v_mla_w_in, v_mla_q_norm_g, v_mla_kv_norm_g, v_mla_w_uq, v_mla_w_ukv, v_swa_w_in, v_swa_sinks, v_w_mem_kv, v_w_o, v_mlp_w_up, v_mlp_w_down):
    given = dict(locals())
    seq = x.shape[1]
    x0 = x.reshape(seq, D_MODEL)
    tgt = loss_target.reshape(seq, D_MODEL)
    mem0 = mem.reshape(N_MEM, D_MODEL)
    pos = positions.reshape(seq).astype(F32)
    pos_col, pos_row = pos.reshape(seq, 1), pos.reshape(1, seq)

    def layer_names(i):
        mixer = ("mla_w_in", "mla_w_uq", "mla_w_ukv") if i % 2 == 0 else ("swa_w_in",)
        return [(n, i // 2) for n in mixer] + [(n, i) for n in ("w_mem_kv", "w_o", "mlp_w_up", "mlp_w_down")]

    def local_weights(names):
        return [given[n][l].astype(BF16) for n, l in names]

    first_attn, first_mlp = layer_names(0)[:-2], layer_names(0)[-2:]
    weights = [dict(zip([n for n, _ in first_attn], _all_gather(local_weights(first_attn), "gather_weights_first")))]
    coming_mlp, first_token = _exchange_start(local_weights(first_mlp), False, "gather_weights_start_0",
                                              after=weights[0]["w_o"])

    consts = _lane_consts()
    tabs = _rope_tables(pos_col, consts)
    slopes = 2.0 ** (-8.0 * (jnp.arange(SWA_HEADS, dtype=F32) + 1.0) / SWA_HEADS)

    mem_n = _rmsnorm_fwd(mem0, 0, D_MODEL, mem_norm_g, "rmsnorm_fwd_mem")

    saved = []
    xc = x0
    for i in range(DEPTH):
        j = i // 2
        wts = weights[i]
        s = {"x_in": xc}
        token = None
        if i + 1 < DEPTH:
            coming, token = _exchange_start(local_weights(layer_names(i + 1)), False,
                                            "gather_weights_start_%d" % (i + 1),
                                            after=first_token if i == 0 else wts["w_o"])
        hn = _rmsnorm_fwd(xc, 0, D_MODEL, attn_norm_g[i], "rmsnorm_fwd", after=token)
        if i % 2 == 0:
            w_in = _mla_in_pad(_join(wts["mla_w_in"], 1))
            w_uq = _mla_uq_pad(_join(wts["mla_w_uq"], 2))
            w_kv = _join(wts["mla_w_ukv"], 2)
            proj = _mm(hn, w_in, "nn", F32, "mm_mla_in")
            cqn = _rmsnorm_fwd(proj, 0, MLA_Q_RANK, mla_q_norm_g[j], "rmsnorm_fwd_q")
            ckvn = _rmsnorm_fwd(proj, 2, MLA_KV_RANK, mla_kv_norm_g[j], "rmsnorm_fwd_kv")
            qraw = _mm(cqn, w_uq, "nn", F32, "mm_mla_uq")
            kvraw = _mm(ckvn, w_kv, "nn", F32, "mm_mla_ukv")
            q, k, v = _mla_rope_fwd(qraw, kvraw, proj, tabs)
            o, lse = _mla_attn_fwd(q, k, v)
            qoff = MLA_QOFF
            s.update(w_uq=w_uq, w_kv=w_kv, cqn=cqn, ckvn=ckvn, q=q, k=k, v=v)
        else:
            w_in = _join(wts["swa_w_in"], 2)
            proj = _mm(hn, w_in, "nn", BF16, "mm_swa_in", pairs="o")
            o, lse = _swa_attn_fwd(proj, pos_col, pos_row, slopes, swa_sinks[j])
            qoff = SWA_QOFF
        w_mem = _pad_slots(_join(wts["w_mem_kv"], 1), 1)
        w_out = _join(wts["w_o"], 1)
        w_o_mix, w_o_cross = w_out[:SWA_HEADS * HEAD_DIM], w_out[SWA_HEADS * HEAD_DIM:]
        kvmem = _mm(mem_n, w_mem, "nn", BF16, "mm_mem_kv")
        cross = _cross_attn_fwd(proj, qoff, kvmem)
        x1 = _mm(o, w_o_mix, "nn", F32, "mm_o_mix", res=xc, pairs="a")
        x1 = _mm(cross, w_o_cross, "nn", F32, "mm_o_cross", res=x1, pairs="a")
        hn2 = _rmsnorm_fwd(x1, 0, D_MODEL, mlp_norm_g[i], "rmsnorm_fwd")
        if i == 0:
            wts.update(zip([n for n, _ in first_mlp], _exchange_wait(coming_mlp, hn2, "gather_weights_wait_0")))
        act, act2 = _mm(hn2, wts["mlp_w_up"], "nn", BF16, "mm_mlp_up", epi="relu2", b_blk="cols")
        xc = _mm(act2, wts["mlp_w_down"], "nn", F32, "mm_mlp_down", res=x1, b_blk="rows")
        s.update(hn=hn, w_in=w_in, proj=proj, o=o, lse=lse, qoff=qoff, w_mem=w_mem, w_o_mix=w_o_mix,
                 w_o_cross=w_o_cross, kvmem=kvmem, cross=cross, x1=x1, hn2=hn2, act=act, act2=act2)
        saved.append(s)
        if i + 1 < DEPTH:
            got = _exchange_wait(coming, xc, "gather_weights_wait_%d" % (i + 1))
            weights.append(dict(zip([n for n, _ in layer_names(i + 1)], got)))

    dx, dx_b, dg_final, loss_part = _loss_head(xc, final_norm_g, tgt)
    loss = lax.psum(loss_part[0, 0], MESH_AXES)

    gains = {n: [None] * DEPTH for n in ("attn_norm_g", "mlp_norm_g")}
    for n in ("mla_q_norm_g", "mla_kv_norm_g", "swa_sinks"):
        gains[n] = [None] * 2
    leaving = {}
    token = None
    dmem_n = None
    for i in reversed(range(DEPTH)):
        j = i // 2
        s = saved[i]
        wts = weights[i]
        out = {}
        du = _mm(dx_b, wts["mlp_w_down"], "nt", BF16, "mm_mlp_down_dx", aux=s["act"], epi="mul2aux", b_blk="rows",
                 after=token)
        out["mlp_w_down"] = _mm(s["act2"], dx_b, "tn", BF16, "mm_mlp_down_dw", o_blk="rows")
        out["mlp_w_up"] = _mm(s["hn2"], du, "tn", BF16, "mm_mlp_up_dw", o_blk="cols")
        dx1, dx1_b, dg = _mm(du, wts["mlp_w_up"], "nt", F32, "mm_mlp_up_dx", b_blk="cols",
                             epi="normbwd", norm=(s["x1"], mlp_norm_g[i], dx))
        gains["mlp_norm_g"][i] = dg[0]

        do = _mm(dx1_b, s["w_o_mix"], "nt", BF16, "mm_o_mix_dx", pairs="o")
        dcross = _mm(dx1_b, s["w_o_cross"], "nt", BF16, "mm_o_cross_dx", pairs="o")
        dw_o = jnp.concatenate([_mm(s["o"], dx1_b, "tn", F32, "mm_o_mix_dw", pairs="a"),
                                _mm(s["cross"], dx1_b, "tn", F32, "mm_o_cross_dw", pairs="a")], axis=0)
        out["w_o"] = _split(dw_o, 1)
        dqc, dkm, dvm = _cross_attn_bwd(s["proj"], s["qoff"], s["kvmem"], dcross)
        dkvmem = jnp.concatenate([dkm, dvm], axis=1).astype(BF16)
        out["w_mem_kv"] = _split(_unpad_slots(_mm(mem_n, dkvmem, "tn", F32, "mm_mem_kv_dw"), 1), 1)
        dmem_n = _mm(dkvmem, s["w_mem"], "nt", F32, "mm_mem_kv_dx" if dmem_n is None else "mm_mem_kv_dx_acc",
                     res=dmem_n)
        leaving[(i, "main")], token = _exchange_start([out[n] for n, _ in layer_names(i)[-4:]], True,
                                                      "exchange_grads_main_start_%d" % i)

        if i % 2 == 0:
            dq, dk, dv = _mla_attn_bwd(s["q"], s["k"], s["v"], do, s["lse"], _mla_delta(s["o"], do), token)
            dqraw, dkv, dkr = _mla_rope_bwd(dq, dk, dv, tabs, consts)
            dcqn = _mm(dqraw, s["w_uq"], "nt", F32, "mm_mla_uq_dx")
            out["mla_w_uq"] = _split(_unpad_slots(_mm(s["cqn"], dqraw, "tn", F32, "mm_mla_uq_dw"), 1, MLA_QK), 2)
            dckvn = _mm(dkv, s["w_kv"], "nt", F32, "mm_mla_ukv_dx")
            out["mla_w_ukv"] = _split(_mm(s["ckvn"], dkv, "tn", F32, "mm_mla_ukv_dw"), 2)
            dcq, dg = _rmsnorm_bwd(s["proj"], 0, MLA_Q_RANK, mla_q_norm_g[j], dcqn, None, BF16, "rmsnorm_bwd_q")
            gains["mla_q_norm_g"][j] = dg[0]
            dckv, dg = _rmsnorm_bwd(s["proj"], 2, MLA_KV_RANK, mla_kv_norm_g[j], dckvn, None, BF16, "rmsnorm_bwd_kv")
            gains["mla_kv_norm_g"][j] = dg[0]
            dproj = jnp.concatenate([dcq, dkr.astype(BF16), dckv, dqc.astype(BF16)], axis=1)
            in_dx = "mm_mla_in_dx"
            out["mla_w_in"] = _split(_mla_in_unpad(_mm(s["hn"], dproj, "tn", F32, "mm_mla_in_dw")), 1)
        else:
            dq, dk, dv, dsink = _swa_attn_bwd(s["proj"], s["o"], do, s["lse"], pos_col, pos_row, slopes, swa_sinks[j],
                                              token)
            gains["swa_sinks"][j] = dsink[::8, 0]
            dproj = jnp.concatenate([dq, dk, dv, dqc], axis=1).astype(BF16)
            in_dx = "mm_swa_in_dx"
            out["swa_w_in"] = _split(_mm(s["hn"], dproj, "tn", F32, "mm_swa_in_dw", pairs="b"), 2)
        dx, dx_b, dg = _mm(dproj, s["w_in"], "nt", F32, in_dx, epi="normbwd", norm=(s["x_in"], attn_norm_g[i], dx1),
                           pairs="" if i % 2 == 0 else "a")
        gains["attn_norm_g"][i] = dg[0]

        leaving[(i, "mixer")], token = _exchange_start([out[n] for n, _ in layer_names(i)[:-4]], True,
                                                       "exchange_grads_mixer_start_%d" % i)

    _, dg_mem = _rmsnorm_bwd(mem0, 0, D_MODEL, mem_norm_g, dmem_n, None, BF16, "rmsnorm_bwd_mem")
    gains = {n: jnp.stack(g) for n, g in gains.items()}
    gains["mem_norm_g"] = dg_mem[0]
    gains["final_norm_g"] = dg_final[0]

    result = {}

    def adamw_of(names, received):
        for n in names:
            parts = [received[(n, l)] for l in range(given[n].shape[0])]
            for kind, r in enumerate(_adamw(parts, given[n], given["m_" + n], given["v_" + n], "adamw_" + n)):
                result[(kind, n)] = r

    received = {}
    for i in reversed(range(DEPTH)):
        got = _exchange_wait(leaving[(i, "main")], dx, "exchange_grads_main_wait_%d" % i)
        received.update(zip(layer_names(i)[-4:], got))
    adamw_of(("mlp_w_up", "mlp_w_down", "w_o", "w_mem_kv"), received)
    for i in reversed(range(DEPTH)):
        got = _exchange_wait(leaving[(i, "mixer")], result[(0, "w_mem_kv")], "exchange_grads_mixer_wait_%d" % i)
        received.update(zip(layer_names(i)[:-4], got))
    adamw_of(("mla_w_in", "mla_w_uq", "mla_w_ukv", "swa_w_in"), received)

    rep_shapes = [given[n].shape for n in REPLICATED]
    rep_parts = _all_gather([_pack([gains[n] for n in REPLICATED], SLOT, 8, F32)], "gather_gain_grads")[0]
    rep_packed = [_pack([given[p + n] for n in REPLICATED], SLOT, 8, F32)[None] for p in ("", "m_", "v_")]
    for kind, r in enumerate(_adamw([rep_parts], *rep_packed, "adamw_gains")):
        for n, part in zip(REPLICATED, _unpack(r[0], rep_shapes)):
            result[(kind, n)] = part

    outs = [loss, dx.reshape(1, seq, D_MODEL)]
    for kind in range(4):
        outs += [result[(kind, n)] for n in WEIGHT_ORDER]
    return tuple(outs)
```

```python
import functools

import jax
import jax.numpy as jnp
from jax import lax
from jax.experimental import pallas as pl
from jax.experimental.pallas import tpu as pltpu

F32 = jnp.float32
BF16 = jnp.bfloat16

D_MODEL = 1024
D_FF = 4096
N_MEM = 256
DEPTH = 4
SLOT = 128
HEAD_DIM = 64
MLA_HEADS = 12
MLA_QK = 96
MLA_Q_RANK = 384
MLA_KV_RANK = 256
SWA_HEADS = 12
SWA_KV_HEADS = 4
SWA_GROUP = 3
MEM_HEADS = 4
WINDOW = 128
EPS = 1e-6
NEG = -1e30
ROPE_THETA = 10000.0
N_DEV = 8

ADAM_LR = 0.001
ADAM_B1 = 0.9
ADAM_B2 = 0.999
ADAM_EPS = 1e-08
ADAM_WD = 0.01
ADAM_STEP = 10

TM = 512
TQ_MLA = 1024
MLA_PACK = 2
SWA_PACK = 4
TQ_CROSS = 2048
MM_VMEM_BUDGET = 38 * 1024 * 1024
ADAM_ROWS = 128
VMEM_LIMIT = 56 * 1024 * 1024

MESH_AXES = ("x", "y", "c")

LOG2_E = 1.4426950408889634
MLA_SCALE = MLA_QK ** -0.5
MLA_Q_SCALE = MLA_SCALE * LOG2_E

MLA_PAD_IN = 384 + SLOT + 256 + MEM_HEADS * SLOT
MLA_QOFF = (384 + SLOT + 256) // SLOT
SWA_PAD_IN = (SWA_HEADS + 2 * SWA_KV_HEADS + MEM_HEADS) * SLOT
SWA_QOFF = SWA_HEADS + 2 * SWA_KV_HEADS

SHARDED = (
    ("mla_w_in", 1), ("mla_w_uq", 2), ("mla_w_ukv", 2), ("swa_w_in", 2),
    ("w_mem_kv", 1), ("w_o", 1), ("mlp_w_up", 2), ("mlp_w_down", 1),
)
REPLICATED = ("attn_norm_g", "mlp_norm_g", "mem_norm_g", "final_norm_g",
              "mla_q_norm_g", "mla_kv_norm_g", "swa_sinks")
WEIGHT_ORDER = ("attn_norm_g", "mlp_norm_g", "mem_norm_g", "final_norm_g", "mla_w_in",
                "mla_q_norm_g", "mla_kv_norm_g", "mla_w_uq", "mla_w_ukv", "swa_w_in",
                "swa_sinks", "w_mem_kv", "w_o", "mlp_w_up", "mlp_w_down")


def _cparams():
    return pltpu.CompilerParams(vmem_limit_bytes=VMEM_LIMIT)


_DIMS = {"nn": (((1,), (0,)), ((), ())), "nt": (((1,), (1,)), ((), ())), "tn": (((0,), (0,)), ((), ()))}


def _compact(x):
    pairs = [x[:, 2 * j * SLOT:(2 * j + 1) * SLOT] + pltpu.roll(x[:, (2 * j + 1) * SLOT:(2 * j + 2) * SLOT], HEAD_DIM, 1)
             for j in range(x.shape[1] // (2 * SLOT))]
    return pairs[0] if len(pairs) == 1 else jnp.concatenate(pairs, axis=1)


def _expand(x):
    low = lax.broadcasted_iota(jnp.int32, (x.shape[0], SLOT), 1) < HEAD_DIM
    slots = []
    for j in range(x.shape[1] // SLOT):
        pair = x[:, j * SLOT:(j + 1) * SLOT]
        slots += [jnp.where(low, pair, 0.0), pltpu.roll(jnp.where(low, 0.0, pair), HEAD_DIM, 1)]
    return jnp.concatenate(slots, axis=1)


def _mm_tiles(m, n, k, a_bytes, b_bytes, o_bytes, extra_bytes, tm_fixed, tn_fixed):
    best = None
    for tm in ([tm_fixed] if tm_fixed else [t for t in range(4096, 0, -SLOT) if m % t == 0] or [m]):
        for tn in ([tn_fixed] if tn_fixed else [t for t in range(1024, 0, -SLOT) if n % t == 0] or [n]):
            need = 2 * (tm * k * a_bytes + k * tn * b_bytes + tm * tn * (o_bytes + extra_bytes))
            need += tm * tn * 4
            if need <= MM_VMEM_BUDGET and (best is None or tm * tn > best[0] * best[1]):
                best = (tm, tn)
    assert best is not None, (m, n, k)
    return best


def _mm(a, b, mode, out_dtype, name, res=None, aux=None, epi=None, b_blk=None, o_blk=None, after=None, norm=None,
        pairs=""):
    if b_blk is not None:
        nb, br, bc = b.shape
        b_shape = (nb * br, bc) if b_blk == "rows" else (br, nb * bc)
    else:
        b_shape = b.shape
    assert not pairs or (b_blk is None and o_blk is None and not ("b" in pairs and mode == "nt"))
    a_shape = (a.shape[0], a.shape[1] // 2) if "a" in pairs else a.shape
    if "b" in pairs:
        b_shape = (b_shape[0], b_shape[1] // 2)
    if mode == "nn":
        (m, k), (k2, n) = a_shape, b_shape
    elif mode == "nt":
        (m, k), (n, k2) = a_shape, b_shape
    else:
        (k, m), (k2, n) = a_shape, b_shape
    assert k == k2, (a.shape, b_shape, mode)
    k_blocked = b_blk is not None and (b_blk == "rows") == (mode != "nt")
    tn_fixed = None
    if b_blk is not None and not k_blocked:
        tn_fixed = br if b_blk == "rows" else bc
    if o_blk == "cols":
        tn_fixed = n // N_DEV
    tm_fixed = m // N_DEV if o_blk == "rows" else None
    has_res, has_aux, has_norm = res is not None, aux is not None, epi == "normbwd"
    assert o_blk is None or not (has_res or has_aux or has_norm)
    n_out = 2 if epi == "relu2" else 1
    if has_norm:
        tn_fixed = n
        o_bytes, extra_bytes = 4 + 2, 4 + 4
    else:
        o_bytes = n_out * jnp.dtype(out_dtype).itemsize
        extra_bytes = (4 if has_res else 0) + (aux.dtype.itemsize if has_aux else 0)
    pa, pb, po = (2 if "a" in pairs else 1), (2 if "b" in pairs else 1), (2 if "o" in pairs else 1)
    tm, tn = _mm_tiles(m, n, k, a.dtype.itemsize * (3 if pa == 2 else 1), b.dtype.itemsize * (3 if pb == 2 else 1),
                       o_bytes * po, extra_bytes, tm_fixed, tn_fixed)
    dims = _DIMS[mode]
    if mode == "tn":
        a_spec = pl.BlockSpec((k, pa * tm), lambda i, j: (0, i))
    else:
        a_spec = pl.BlockSpec((tm, pa * k), lambda i, j: (i, 0))
    if b_blk is None:
        if mode == "nt":
            b_spec = pl.BlockSpec((tn, k), lambda i, j: (j, 0))
        else:
            b_spec = pl.BlockSpec((k, pb * tn), lambda i, j: (0, j))
    elif k_blocked and mode == "nt":
        b_spec = pl.BlockSpec((N_DEV, tn, bc), lambda i, j: (0, j, 0))
    elif k_blocked:
        b_spec = pl.BlockSpec((N_DEV, br, tn), lambda i, j: (0, 0, j))
    elif mode == "nt":
        b_spec = pl.BlockSpec((None, tn, k), lambda i, j: (j, 0, 0))
    else:
        b_spec = pl.BlockSpec((None, k, tn), lambda i, j: (j, 0, 0))
    if o_blk is None:
        o_spec = pl.BlockSpec((tm, po * tn), lambda i, j: (i, j))
        o_shape = (m, po * n)
    elif o_blk == "rows":
        o_spec = pl.BlockSpec((None, tm, tn), lambda i, j: (i, 0, j))
        o_shape = (N_DEV, tm, n)
    else:
        o_spec = pl.BlockSpec((None, tm, tn), lambda i, j: (j, i, 0))
        o_shape = (N_DEV, m, tn)

    def body(*refs):
        a_ref, b_ref = refs[0], refs[1]
        pos = 2
        res_ref = aux_ref = None
        if has_res:
            res_ref = refs[pos]
            pos += 1
        if has_aux:
            aux_ref = refs[pos]
            pos += 1
        if has_norm:
            x_ref, g_ref, dres_ref = refs[pos:pos + 3]
            pos += 3
        if after is not None:
            pos += 1
        outs = refs[pos:]
        if k_blocked and mode == "nt":
            r = None
            for d in range(N_DEV):
                part = lax.dot_general(a_ref[:, d * bc:(d + 1) * bc].astype(BF16), b_ref[d].astype(BF16), dims,
                                       preferred_element_type=F32)
                r = part if r is None else r + part
        else:
            bv = b_ref[...].reshape(k, tn) if k_blocked else b_ref[...]
            av = _compact(a_ref[...].astype(F32)) if pa == 2 else a_ref[...]
            bv = _compact(bv.astype(F32)) if pb == 2 else bv
            r = lax.dot_general(av.astype(BF16), bv.astype(BF16), dims, preferred_element_type=F32)
        if po == 2:
            r = _expand(r)
        if epi == "relu2":
            r = jnp.maximum(r, 0.0)
            outs[0][...] = r.astype(outs[0].dtype)
            outs[1][...] = (r * r).astype(outs[1].dtype)
        elif has_norm:
            xv = x_ref[...]
            rs = lax.rsqrt(jnp.mean(xv * xv, axis=1, keepdims=True) + EPS)
            xh = xv * rs
            dxh = r * g_ref[...]
            dx = rs * (dxh - xh * jnp.mean(dxh * xh, axis=1, keepdims=True)) + dres_ref[...]
            outs[0][...] = dx
            outs[1][...] = dx.astype(BF16)

            @pl.when(pl.program_id(0) == 0)
            def _():
                outs[2][...] = jnp.zeros_like(outs[2])

            outs[2][...] += jnp.sum(r * xh, axis=0, keepdims=True)
        else:
            if epi == "mul2aux":
                r = r * (2.0 * aux_ref[...].astype(F32))
            if has_res:
                r = r + res_ref[...]
            outs[0][...] = r.astype(outs[0].dtype)

    in_specs = [a_spec, b_spec]
    args = [a, b]
    if has_res:
        in_specs.append(o_spec)
        args.append(res)
    if has_aux:
        in_specs.append(o_spec)
        args.append(aux)
    vec_spec = pl.BlockSpec((1, n), lambda i, j: (0, 0))
    if has_norm:
        in_specs += [o_spec, vec_spec, o_spec]
        args += [norm[0], norm[1].reshape(1, n), norm[2]]
    if after is not None:
        in_specs.append(pl.BlockSpec(memory_space=pl.ANY))
        args.append(after)
    if has_norm:
        out_specs = [o_spec, o_spec, vec_spec]
        out_shape = [jax.ShapeDtypeStruct(o_shape, F32), jax.ShapeDtypeStruct(o_shape, BF16),
                     jax.ShapeDtypeStruct((1, n), F32)]
    else:
        out_specs = [o_spec] * n_out
        out_shape = [jax.ShapeDtypeStruct(o_shape, out_dtype)] * n_out
    out = pl.pallas_call(
        body, name=name, grid=(m // tm, n // tn),
        in_specs=in_specs, out_specs=out_specs, out_shape=out_shape, compiler_params=_cparams(),
    )(*args)
    return out if len(out) > 1 else out[0]


def _rmsnorm_fwd(xarr, colblk, width, g, name, after=None):
    rows = xarr.shape[0]
    tm = min(TM, rows)

    def body(x_ref, g_ref, *rest):
        y_ref = rest[-1]
        x = x_ref[...].astype(F32)
        r = lax.rsqrt(jnp.mean(x * x, axis=1, keepdims=True) + EPS)
        y_ref[...] = (x * r * g_ref[...]).astype(y_ref.dtype)

    in_specs = [pl.BlockSpec((tm, width), lambda i: (i, colblk)), pl.BlockSpec((1, width), lambda i: (0, 0))]
    args = [xarr, g.reshape(1, width)]
    if after is not None:
        in_specs.append(pl.BlockSpec(memory_space=pl.ANY))
        args.append(after)
    return pl.pallas_call(
        body, name=name, grid=(rows // tm,), in_specs=in_specs,
        out_specs=pl.BlockSpec((tm, width), lambda i: (i, 0)),
        out_shape=jax.ShapeDtypeStruct((rows, width), BF16), compiler_params=_cparams(),
    )(*args)


def _rmsnorm_bwd(xarr, colblk, width, g, dy, dres, out_dtype, name):
    rows = xarr.shape[0]
    tm = min(TM, rows)
    has_res = dres is not None

    def body(*refs):
        x_ref, g_ref, dy_ref = refs[0], refs[1], refs[2]
        dres_ref = refs[3] if has_res else None
        dx_ref, dg_ref = refs[-2], refs[-1]
        x = x_ref[...].astype(F32)
        dyv = dy_ref[...].astype(F32)
        r = lax.rsqrt(jnp.mean(x * x, axis=1, keepdims=True) + EPS)
        xh = x * r
        dxh = dyv * g_ref[...]
        dx = r * (dxh - xh * jnp.mean(dxh * xh, axis=1, keepdims=True))
        if has_res:
            dx = dx + dres_ref[...]
        dx_ref[...] = dx.astype(dx_ref.dtype)

        @pl.when(pl.program_id(0) == 0)
        def _():
            dg_ref[...] = jnp.zeros_like(dg_ref)

        dg_ref[...] += jnp.sum(dyv * xh, axis=0, keepdims=True)

    row_spec = pl.BlockSpec((tm, width), lambda i: (i, 0))
    vec_spec = pl.BlockSpec((1, width), lambda i: (0, 0))
    in_specs = [pl.BlockSpec((tm, width), lambda i: (i, colblk)), vec_spec, row_spec]
    args = [xarr, g.reshape(1, width), dy]
    if has_res:
        in_specs.append(row_spec)
        args.append(dres)
    return pl.pallas_call(
        body, name=name, grid=(rows // tm,), in_specs=in_specs, out_specs=[row_spec, vec_spec],
        out_shape=[jax.ShapeDtypeStruct((rows, width), out_dtype), jax.ShapeDtypeStruct((1, width), F32)],
        compiler_params=_cparams(),
    )(*args)


def _loss_head(x, g, tgt):
    rows, width = x.shape
    tm = min(TM, rows)

    def body(x_ref, g_ref, t_ref, dx_ref, dxb_ref, dg_ref, loss_ref):
        xv = x_ref[...]
        gv = g_ref[...]
        r = lax.rsqrt(jnp.mean(xv * xv, axis=1, keepdims=True) + EPS)
        xh = xv * r
        err = xh * gv - t_ref[...]
        part = 0.5 * jnp.sum(jnp.mean(err * err, axis=1, keepdims=True), axis=0, keepdims=True)
        dyv = err * (1.0 / width)
        dxh = dyv * gv
        dxv = r * (dxh - xh * jnp.mean(dxh * xh, axis=1, keepdims=True))
        dx_ref[...] = dxv
        dxb_ref[...] = dxv.astype(BF16)

        @pl.when(pl.program_id(0) == 0)
        def _():
            dg_ref[...] = jnp.zeros_like(dg_ref)
            loss_ref[...] = jnp.zeros_like(loss_ref)

        dg_ref[...] += jnp.sum(dyv * xh, axis=0, keepdims=True)
        loss_ref[...] += jnp.broadcast_to(part, loss_ref.shape)

    row_spec = pl.BlockSpec((tm, width), lambda i: (i, 0))
    vec_spec = pl.BlockSpec((1, width), lambda i: (0, 0))
    return pl.pallas_call(
        body, name="loss_head", grid=(rows // tm,), in_specs=[row_spec, vec_spec, row_spec],
        out_specs=[row_spec, row_spec, vec_spec, pl.BlockSpec((1, SLOT), lambda i: (0, 0))],
        out_shape=[jax.ShapeDtypeStruct((rows, width), F32), jax.ShapeDtypeStruct((rows, width), BF16),
                   jax.ShapeDtypeStruct((1, width), F32), jax.ShapeDtypeStruct((1, SLOT), F32)],
        compiler_params=_cparams(),
    )(x, g.reshape(1, width), tgt)


def _lane_consts():
    half = 16
    inv = ROPE_THETA ** (-(jnp.arange(half, dtype=F32) * 2.0) / 32)
    lane = jnp.arange(SLOT)
    first = (lane >= 64) & (lane < 80)
    second = (lane >= 80) & (lane < 96)
    inv_lane = jnp.where(first | second, inv[(lane - 64) % half], 0.0)
    rows = [inv_lane, (lane < 64).astype(F32), first.astype(F32), second.astype(F32)]
    rows += [jnp.zeros((SLOT,), F32)] * 4
    return jnp.stack(rows).astype(F32)


def _rope_tables(pos_col, consts):
    rows = pos_col.shape[0]
    tm = min(TM, rows)

    def body(p_ref, k_ref, c_ref, s1_ref, s2_ref):
        ang = p_ref[...] * k_ref[0:1, :]
        cos, sin = jnp.cos(ang), jnp.sin(ang)
        first, second = k_ref[2:3, :], k_ref[3:4, :]
        c_ref[...] = k_ref[1:2, :] + (first + second) * cos
        s1_ref[...] = -first * sin
        s2_ref[...] = second * sin

    spec = pl.BlockSpec((tm, SLOT), lambda i: (i, 0))
    shp = jax.ShapeDtypeStruct((rows, SLOT), F32)
    return pl.pallas_call(
        body, name="rope_tables", grid=(rows // tm,),
        in_specs=[pl.BlockSpec((tm, 1), lambda i: (i, 0)), pl.BlockSpec((8, SLOT), lambda i: (0, 0))],
        out_specs=[spec, spec, spec], out_shape=[shp, shp, shp], compiler_params=_cparams(),
    )(pos_col, consts)


def _rot(xv, c, s1, s2):
    return xv * c + pltpu.roll(xv, SLOT - 16, 1) * s1 + pltpu.roll(xv, 16, 1) * s2


def _rot_t(dy, c, s1, s2):
    return dy * c + pltpu.roll(dy * s1, 16, 1) + pltpu.roll(dy * s2, SLOT - 16, 1)


def _mla_rope_fwd(qraw, kvraw, proj, tabs):
    rows = qraw.shape[0]
    tm = min(256, rows)
    hw = MLA_HEADS * SLOT

    def body(q_ref, kv_ref, kr_ref, c_ref, s1_ref, s2_ref, qo, ko, vo):
        c, s1, s2 = c_ref[...], s1_ref[...], s2_ref[...]
        kr = _rot(kr_ref[...], c, s1, s2)
        low = lax.broadcasted_iota(jnp.int32, (tm, SLOT), 1) < HEAD_DIM
        for h in range(MLA_HEADS):
            sl = slice(h * SLOT, (h + 1) * SLOT)
            qo[:, sl] = (_rot(q_ref[:, sl], c, s1, s2) * MLA_Q_SCALE).astype(BF16)
            kvh = kv_ref[:, sl]
            ko[:, sl] = (jnp.where(low, kvh, 0.0) + kr).astype(BF16)
            vo[:, sl] = pltpu.roll(jnp.where(low, 0.0, kvh), HEAD_DIM, 1).astype(BF16)

    tab = pl.BlockSpec((tm, SLOT), lambda i: (i, 0))
    wide = pl.BlockSpec((tm, hw), lambda i: (i, 0))
    shp = jax.ShapeDtypeStruct((rows, hw), BF16)
    return pl.pallas_call(
        body, name="mla_rope_fwd", grid=(rows // tm,),
        in_specs=[wide, wide, pl.BlockSpec((tm, SLOT), lambda i: (i, 3)),
                  tab, tab, tab],
        out_specs=[wide, wide, wide], out_shape=[shp, shp, shp], compiler_params=_cparams(),
    )(qraw, kvraw, proj, *tabs)


def _mla_rope_bwd(dq, dk, dv, tabs, consts):
    rows = dq.shape[0]
    tm = min(256, rows)
    hw = MLA_HEADS * SLOT

    def body(dq_ref, dk_ref, dv_ref, c_ref, s1_ref, s2_ref, k_ref, dqo, dkvo, dkro):
        c, s1, s2 = c_ref[...], s1_ref[...], s2_ref[...]
        ksum = jnp.zeros((tm, SLOT), F32)
        low = lax.broadcasted_iota(jnp.int32, (tm, SLOT), 1) < HEAD_DIM
        for h in range(MLA_HEADS):
            sl = slice(h * SLOT, (h + 1) * SLOT)
            dqo[:, sl] = _rot_t(dq_ref[:, sl], c, s1, s2).astype(BF16)
            dkh = dk_ref[:, sl]
            ksum = ksum + dkh
            dvh = pltpu.roll(jnp.where(low, dv_ref[:, sl], 0.0), HEAD_DIM, 1)
            dkvo[:, sl] = (jnp.where(low, dkh, 0.0) + dvh).astype(BF16)
        dkro[...] = _rot_t(ksum, c, s1, s2) * (k_ref[2:3, :] + k_ref[3:4, :])

    tab = pl.BlockSpec((tm, SLOT), lambda i: (i, 0))
    wide = pl.BlockSpec((tm, hw), lambda i: (i, 0))
    return pl.pallas_call(
        body, name="mla_rope_bwd", grid=(rows // tm,),
        in_specs=[wide, wide, wide, tab, tab, tab, pl.BlockSpec((8, SLOT), lambda i: (0, 0))],
        out_specs=[wide, wide, tab],
        out_shape=[jax.ShapeDtypeStruct((rows, hw), BF16), jax.ShapeDtypeStruct((rows, hw), BF16),
                   jax.ShapeDtypeStruct((rows, SLOT), F32)],
        compiler_params=_cparams(),
    )(dq, dk, dv, *tabs, consts)


def _nt(a, b):
    return lax.dot_general(a, b, _DIMS["nt"], preferred_element_type=F32)


def _tn(a, b):
    return lax.dot_general(a, b, _DIMS["tn"], preferred_element_type=F32)


def _nn(a, b):
    return lax.dot_general(a, b, _DIMS["nn"], preferred_element_type=F32)


def _mla_attn_fwd(q, k, v):
    rows = q.shape[0]
    t = min(TQ_MLA, rows)
    nt = rows // t
    wide = MLA_PACK * SLOT

    def body(q_ref, k_ref, v_ref, o_ref, lse_ref, m_sc, l_sc, acc_sc):
        i, j = pl.program_id(1), pl.program_id(2)

        @pl.when(j == 0)
        def _():
            m_sc[...] = jnp.full_like(m_sc, NEG)
            l_sc[...] = jnp.zeros_like(l_sc)
            acc_sc[...] = jnp.zeros_like(acc_sc)

        def step(diagonal):
            for hh in range(MLA_PACK):
                sl = slice(hh * SLOT, (hh + 1) * SLOT)
                s = _nt(k_ref[:, sl], q_ref[:, sl])
                if diagonal:
                    key = lax.broadcasted_iota(jnp.int32, (t, t), 0)
                    s = jnp.where(key <= lax.broadcasted_iota(jnp.int32, (t, t), 1), s, NEG)
                m_prev = m_sc[hh]
                m_new = jnp.maximum(m_prev, jnp.max(s, axis=0, keepdims=True))
                p = jnp.exp2(s - m_new)
                alpha = jnp.exp2(m_prev - m_new)
                l_new = alpha * l_sc[hh] + jnp.sum(p, axis=0, keepdims=True)
                acc = alpha * acc_sc[hh] + _tn(v_ref[:, sl], p.astype(BF16))
                if diagonal:
                    o_ref[:, sl] = (acc / l_new).T.astype(o_ref.dtype)
                    lse_ref[hh:hh + 1, :] = m_new + jnp.log(l_new) * LOG2_E
                else:
                    m_sc[hh] = m_new
                    l_sc[hh] = l_new
                    acc_sc[hh] = acc

        @pl.when(j < i)
        def _():
            step(False)

        @pl.when(j == i)
        def _():
            lse_ref[...] = jnp.zeros_like(lse_ref)
            step(True)

    q_spec = pl.BlockSpec((t, wide), lambda h, i, j: (i, h))
    kv_spec = pl.BlockSpec((t, wide), lambda h, i, j: (jnp.minimum(j, i), h))
    return pl.pallas_call(
        body, name="mla_attn_fwd", grid=(MLA_HEADS // MLA_PACK, nt, nt),
        in_specs=[q_spec, kv_spec, kv_spec],
        out_specs=[q_spec, pl.BlockSpec((None, 8, t), lambda h, i, j: (h, 0, i))],
        out_shape=[jax.ShapeDtypeStruct(q.shape, BF16),
                   jax.ShapeDtypeStruct((MLA_HEADS // MLA_PACK, 8, rows), F32)],
        scratch_shapes=[pltpu.VMEM((MLA_PACK, 1, t), F32), pltpu.VMEM((MLA_PACK, 1, t), F32),
                        pltpu.VMEM((MLA_PACK, SLOT, t), F32)],
        compiler_params=_cparams(),
    )(q, k, v)


def _mla_delta(o, do):
    rows = o.shape[0]
    t = min(TQ_MLA, rows)
    wide = MLA_PACK * SLOT

    def body(o_ref, do_ref, d_ref):
        d_ref[...] = jnp.zeros_like(d_ref)
        ones = jnp.ones((8, SLOT), BF16)
        for hh in range(MLA_PACK):
            sl = slice(hh * SLOT, (hh + 1) * SLOT)
            prod = do_ref[:, sl].astype(F32) * o_ref[:, sl].astype(F32)
            high = prod.astype(BF16)
            low = (prod - high.astype(F32)).astype(BF16)
            d_ref[hh:hh + 1, :] = (_nt(ones, high) + _nt(ones, low))[0:1, :]

    spec = pl.BlockSpec((t, wide), lambda h, i: (i, h))
    return pl.pallas_call(
        body, name="mla_delta", grid=(MLA_HEADS // MLA_PACK, rows // t), in_specs=[spec, spec],
        out_specs=pl.BlockSpec((None, 8, t), lambda h, i: (h, 0, i)),
        out_shape=jax.ShapeDtypeStruct((MLA_HEADS // MLA_PACK, 8, rows), F32), compiler_params=_cparams(),
    )(o, do)


def _mla_attn_bwd(q, k, v, do, lse, delta, after):
    rows = q.shape[0]
    t = min(TQ_MLA, rows)
    nt = rows // t
    wide = MLA_PACK * SLOT

    def body(q_ref, k_ref, v_ref, do_ref, lse_ref, delta_ref, after_ref, dq_ref, dk_ref, dv_ref, dk_sc, dv_sc):
        j, i = pl.program_id(1), pl.program_id(2)

        @pl.when((j == 0) & (i == 0))
        def _():
            dq_ref[...] = jnp.zeros_like(dq_ref)

        @pl.when(i == 0)
        def _():
            dk_sc[...] = jnp.zeros_like(dk_sc)
            dv_sc[...] = jnp.zeros_like(dv_sc)

        def chunk(hh, rows, keys, masked):
            sl = slice(hh * SLOT, (hh + 1) * SLOT)
            n_rows = rows.stop - rows.start
            qv, kv, dov = q_ref[rows, sl], k_ref[keys, sl], do_ref[rows, sl]
            s = _nt(kv, qv)
            if masked:
                shp = (keys.stop - keys.start, n_rows)
                s = jnp.where(keys.start + lax.broadcasted_iota(jnp.int32, shp, 0)
                              <= rows.start + lax.broadcasted_iota(jnp.int32, shp, 1), s, NEG)
            p = jnp.exp2(s - lse_ref[hh:hh + 1, rows])
            dp = _nt(v_ref[keys, sl], dov)
            ds = (p * (dp - delta_ref[hh:hh + 1, rows])).astype(BF16)
            dv_sc[keys, sl] += _nn(p.astype(BF16), dov)
            dk_sc[keys, sl] += _nn(ds, qv)
            r0 = pl.multiple_of(i * t + rows.start, n_rows)
            dq_ref[pl.ds(r0, n_rows), sl] += _tn(ds, kv) * MLA_SCALE

        @pl.when(i > j)
        def _():
            for hh in range(MLA_PACK):
                chunk(hh, slice(0, t), slice(0, t), False)

        @pl.when(i == j)
        def _():
            for hh in range(MLA_PACK):
                chunk(hh, slice(0, t), slice(0, t // 2), True)
                chunk(hh, slice(t // 2, t), slice(t // 2, t), True)

        @pl.when(i == nt - 1)
        def _():
            dk_ref[...] = dk_sc[...] * (1.0 / LOG2_E)
            dv_ref[...] = dv_sc[...]

    q_spec = pl.BlockSpec((t, wide), lambda h, j, i: (jnp.maximum(i, j), h))
    kv_spec = pl.BlockSpec((t, wide), lambda h, j, i: (j, h))
    row_spec = pl.BlockSpec((None, 8, t), lambda h, j, i: (h, 0, jnp.maximum(i, j)))
    head_spec = pl.BlockSpec((rows, wide), lambda h, j, i: (0, h))
    shp = jax.ShapeDtypeStruct(q.shape, F32)
    return pl.pallas_call(
        body, name="mla_attn_bwd", grid=(MLA_HEADS // MLA_PACK, nt, nt),
        in_specs=[q_spec, kv_spec, kv_spec, q_spec, row_spec, row_spec, pl.BlockSpec(memory_space=pl.ANY)],
        out_specs=[head_spec, kv_spec, kv_spec], out_shape=[shp, shp, shp],
        scratch_shapes=[pltpu.VMEM((t, wide), F32), pltpu.VMEM((t, wide), F32)],
        compiler_params=_cparams(),
    )(q, k, v, do, lse, delta, after)


def _swa_specs(t):
    def prev(i):
        return jnp.maximum(i - 1, 0)
    kw = SWA_PACK * SLOT
    k0, v0 = SWA_HEADS // SWA_PACK, (SWA_HEADS + SWA_KV_HEADS) // SWA_PACK
    q3 = pl.BlockSpec((t, SWA_PACK * SWA_GROUP * SLOT), lambda h, i: (i, h))
    kp = pl.BlockSpec((t, kw), lambda h, i: (prev(i), k0 + h))
    kc = pl.BlockSpec((t, kw), lambda h, i: (i, k0 + h))
    vp = pl.BlockSpec((t, kw), lambda h, i: (prev(i), v0 + h))
    vc = pl.BlockSpec((t, kw), lambda h, i: (i, v0 + h))
    pcol = pl.BlockSpec((t, 1), lambda h, i: (i, 0))
    prow_p = pl.BlockSpec((1, t), lambda h, i: (0, prev(i)))
    prow_c = pl.BlockSpec((1, t), lambda h, i: (0, i))
    return [q3, kp, kc, vp, vc, pcol, prow_p, prow_c]


def _stack(ref, first):
    return jnp.concatenate([ref[:, (first + g) * SLOT:(first + g + 1) * SLOT] for g in range(SWA_GROUP)], axis=0)


def _swa_logits(q3, kp, kc, pq, pkp, pkc, slope_ref, kvh, i, t):
    r = lax.broadcasted_iota(jnp.int32, (t, t), 0)
    c = lax.broadcasted_iota(jnp.int32, (t, t), 1)
    ok_c = c <= r
    ok_p = (c - r) > jnp.where(i > 0, 0, t)
    dist_p, dist_c = pq - pkp, pq - pkc
    s_p3 = _nt(q3, kp) * (HEAD_DIM ** -0.5)
    s_c3 = _nt(q3, kc) * (HEAD_DIM ** -0.5)
    out = []
    for g in range(SWA_GROUP):
        slope = slope_ref[kvh * SWA_GROUP + g]
        rows = slice(g * t, (g + 1) * t)
        out.append((jnp.where(ok_p, s_p3[rows] - slope * dist_p, NEG),
                    jnp.where(ok_c, s_c3[rows] - slope * dist_c, NEG)))
    return out


def _swa_attn_fwd(proj, pos_col, pos_row, slopes, sinks):
    rows = proj.shape[0]
    t = WINDOW
    hw = SWA_HEADS * SLOT

    def body(slope_ref, sink_ref, q_ref, kp_ref, kc_ref, vp_ref, vc_ref, pq_ref, pkp_ref, pkc_ref, o_ref, lse_ref):
        i = pl.program_id(1)
        for kv in range(SWA_PACK):
            kvh = pl.program_id(0) * SWA_PACK + kv
            ksl = slice(kv * SLOT, (kv + 1) * SLOT)
            logits = _swa_logits(_stack(q_ref, kv * SWA_GROUP), kp_ref[:, ksl], kc_ref[:, ksl], pq_ref[...],
                                 pkp_ref[...], pkc_ref[...], slope_ref, kvh, i, t)
            e_p, e_c, norm = [], [], []
            for g, (s_p, s_c) in enumerate(logits):
                sl = slice((kv * SWA_GROUP + g) * SLOT, (kv * SWA_GROUP + g + 1) * SLOT)
                sink = sink_ref[kvh * SWA_GROUP + g]
                m = jnp.maximum(jnp.maximum(jnp.max(s_p, axis=1, keepdims=True),
                                            jnp.max(s_c, axis=1, keepdims=True)), sink)
                ep, ec = jnp.exp(s_p - m), jnp.exp(s_c - m)
                l = jnp.sum(ep, axis=1, keepdims=True) + jnp.sum(ec, axis=1, keepdims=True) + jnp.exp(sink - m)
                e_p.append(ep.astype(BF16))
                e_c.append(ec.astype(BF16))
                norm.append(l)
                lse_ref[:, sl] = jnp.broadcast_to(m + jnp.log(l), (t, SLOT))
            acc = (_nn(jnp.concatenate(e_p, axis=0), vp_ref[:, ksl])
                   + _nn(jnp.concatenate(e_c, axis=0), vc_ref[:, ksl]))
            for g in range(SWA_GROUP):
                sl = slice((kv * SWA_GROUP + g) * SLOT, (kv * SWA_GROUP + g + 1) * SLOT)
                o_ref[:, sl] = (acc[g * t:(g + 1) * t] / norm[g]).astype(o_ref.dtype)

    smem = pl.BlockSpec(memory_space=pltpu.SMEM)
    out_spec = pl.BlockSpec((t, SWA_PACK * SWA_GROUP * SLOT), lambda h, i: (i, h))
    return pl.pallas_call(
        body, name="swa_attn_fwd", grid=(SWA_KV_HEADS // SWA_PACK, rows // t),
        in_specs=[smem, smem] + _swa_specs(t), out_specs=[out_spec, out_spec],
        out_shape=[jax.ShapeDtypeStruct((rows, hw), BF16), jax.ShapeDtypeStruct((rows, hw), F32)],
        compiler_params=_cparams(),
    )(slopes, sinks, proj, proj, proj, proj, proj, pos_col, pos_row, pos_row)


def _swa_attn_bwd(proj, o, do, lse, pos_col, pos_row, slopes, sinks, after):
    rows = proj.shape[0]
    t = WINDOW
    hw = SWA_HEADS * SLOT
    scale = HEAD_DIM ** -0.5

    def body(slope_ref, sink_ref, q_ref, kp_ref, kc_ref, vp_ref, vc_ref, pq_ref, pkp_ref, pkc_ref,
             o_ref, do_ref, lse_ref, after_ref, dq_ref, dk_ref, dv_ref, dsink_ref):
        i = pl.program_id(1)

        @pl.when(i == 0)
        def _():
            dk_ref[...] = jnp.zeros_like(dk_ref)
            dv_ref[...] = jnp.zeros_like(dv_ref)
            dsink_ref[...] = jnp.zeros_like(dsink_ref)

        r_c = pl.multiple_of(i * t, t)
        r_p = pl.multiple_of(jnp.maximum(i - 1, 0) * t, t)
        for kv in range(SWA_PACK):
            kvh = pl.program_id(0) * SWA_PACK + kv
            ksl = slice(kv * SLOT, (kv + 1) * SLOT)
            q3, do3 = _stack(q_ref, kv * SWA_GROUP), _stack(do_ref, kv * SWA_GROUP)
            logits = _swa_logits(q3, kp_ref[:, ksl], kc_ref[:, ksl], pq_ref[...], pkp_ref[...], pkc_ref[...],
                                 slope_ref, kvh, i, t)
            dp_p3, dp_c3 = _nt(do3, vp_ref[:, ksl]), _nt(do3, vc_ref[:, ksl])
            p_p, p_c, ds_p, ds_c = [], [], [], []
            for g, (s_p, s_c) in enumerate(logits):
                head = kv * SWA_GROUP + g
                sl = slice(head * SLOT, (head + 1) * SLOT)
                rws = slice(g * t, (g + 1) * t)
                lse_g = lse_ref[:, head * SLOT:head * SLOT + 1]
                pp, pc = jnp.exp(s_p - lse_g), jnp.exp(s_c - lse_g)
                delta = jnp.sum(do_ref[:, sl].astype(F32) * o_ref[:, sl].astype(F32), axis=1, keepdims=True)
                p_p.append(pp.astype(BF16))
                p_c.append(pc.astype(BF16))
                ds_p.append((pp * (dp_p3[rws] - delta)).astype(BF16))
                ds_c.append((pc * (dp_c3[rws] - delta)).astype(BF16))
                sink = sink_ref[kvh * SWA_GROUP + g]
                dsink = -jnp.sum(jnp.exp(sink - lse_g) * delta, axis=0, keepdims=True)
                dsink_ref[head * 8:(head + 1) * 8, :] += jnp.broadcast_to(dsink, (8, SLOT))
            p_p3, p_c3 = jnp.concatenate(p_p, axis=0), jnp.concatenate(p_c, axis=0)
            ds_p3, ds_c3 = jnp.concatenate(ds_p, axis=0), jnp.concatenate(ds_c, axis=0)
            dq3 = (_nn(ds_p3, kp_ref[:, ksl]) + _nn(ds_c3, kc_ref[:, ksl])) * scale
            for g in range(SWA_GROUP):
                head = kv * SWA_GROUP + g
                dq_ref[:, head * SLOT:(head + 1) * SLOT] = dq3[g * t:(g + 1) * t]
            dk_ref[pl.ds(r_c, t), ksl] += _tn(ds_c3, q3) * scale
            dv_ref[pl.ds(r_c, t), ksl] += _tn(p_c3, do3)
            dk_ref[pl.ds(r_p, t), ksl] += _tn(ds_p3, q3) * scale
            dv_ref[pl.ds(r_p, t), ksl] += _tn(p_p3, do3)

    smem = pl.BlockSpec(memory_space=pltpu.SMEM)
    qlike = pl.BlockSpec((t, SWA_PACK * SWA_GROUP * SLOT), lambda h, i: (i, h))
    kv_out = pl.BlockSpec((rows, SWA_PACK * SLOT), lambda h, i: (0, h))
    return pl.pallas_call(
        body, name="swa_attn_bwd", grid=(SWA_KV_HEADS // SWA_PACK, rows // t),
        in_specs=[smem, smem] + _swa_specs(t) + [qlike, qlike, qlike, pl.BlockSpec(memory_space=pl.ANY)],
        out_specs=[qlike, kv_out, kv_out,
                   pl.BlockSpec((SWA_PACK * SWA_GROUP * 8, SLOT), lambda h, i: (h, 0))],
        out_shape=[jax.ShapeDtypeStruct((rows, hw), F32), jax.ShapeDtypeStruct((rows, SWA_KV_HEADS * SLOT), F32),
                   jax.ShapeDtypeStruct((rows, SWA_KV_HEADS * SLOT), F32),
                   jax.ShapeDtypeStruct((SWA_HEADS * 8, SLOT), F32)],
        compiler_params=_cparams(),
    )(slopes, sinks, proj, proj, proj, proj, proj, pos_col, pos_row, pos_row, o, do, lse, after)


def _cross_attn_fwd(proj, qoff, kvmem):
    rows = proj.shape[0]
    t = min(TQ_CROSS, rows)

    def body(q_ref, k_ref, v_ref, o_ref):
        s = _nt(k_ref[...], q_ref[...].astype(BF16)) * (HEAD_DIM ** -0.5)
        e = jnp.exp(s - jnp.max(s, axis=0, keepdims=True))
        p = e / jnp.sum(e, axis=0, keepdims=True)
        o_ref[...] = _tn(v_ref[...], p.astype(BF16)).T.astype(o_ref.dtype)

    return pl.pallas_call(
        body, name="cross_attn_fwd", grid=(rows // t, MEM_HEADS),
        in_specs=[pl.BlockSpec((t, SLOT), lambda i, h: (i, qoff + h)),
                  pl.BlockSpec((N_MEM, SLOT), lambda i, h: (0, h)),
                  pl.BlockSpec((N_MEM, SLOT), lambda i, h: (0, MEM_HEADS + h))],
        out_specs=pl.BlockSpec((t, SLOT), lambda i, h: (i, h)),
        out_shape=jax.ShapeDtypeStruct((rows, MEM_HEADS * SLOT), BF16), compiler_params=_cparams(),
    )(proj, kvmem, kvmem)


def _cross_attn_bwd(proj, qoff, kvmem, do):
    rows = proj.shape[0]
    t = min(TQ_CROSS, rows)
    scale = HEAD_DIM ** -0.5

    def body(q_ref, k_ref, v_ref, do_ref, dq_ref, dk_ref, dv_ref):
        @pl.when(pl.program_id(1) == 0)
        def _():
            dk_ref[...] = jnp.zeros_like(dk_ref)
            dv_ref[...] = jnp.zeros_like(dv_ref)

        qv, kv, dov = q_ref[...].astype(BF16), k_ref[...], do_ref[...]
        s = _nt(kv, qv) * scale
        e = jnp.exp(s - jnp.max(s, axis=0, keepdims=True))
        p = e / jnp.sum(e, axis=0, keepdims=True)
        dp = _nt(v_ref[...], dov)
        ds = (p * (dp - jnp.sum(p * dp, axis=0, keepdims=True))).astype(BF16)
        dq_ref[...] = _tn(ds, kv) * scale
        dk_ref[...] += _nn(ds, qv) * scale
        dv_ref[...] += _nn(p.astype(BF16), dov)

    mem_out = pl.BlockSpec((N_MEM, SLOT), lambda h, i: (0, h))
    return pl.pallas_call(
        body, name="cross_attn_bwd", grid=(MEM_HEADS, rows // t),
        in_specs=[pl.BlockSpec((t, SLOT), lambda h, i: (i, qoff + h)),
                  pl.BlockSpec((N_MEM, SLOT), lambda h, i: (0, h)),
                  pl.BlockSpec((N_MEM, SLOT), lambda h, i: (0, MEM_HEADS + h)),
                  pl.BlockSpec((t, SLOT), lambda h, i: (i, h))],
        out_specs=[pl.BlockSpec((t, SLOT), lambda h, i: (i, h)), mem_out, mem_out],
        out_shape=[jax.ShapeDtypeStruct((rows, MEM_HEADS * SLOT), F32),
                   jax.ShapeDtypeStruct((N_MEM, MEM_HEADS * SLOT), F32),
                   jax.ShapeDtypeStruct((N_MEM, MEM_HEADS * SLOT), F32)],
        compiler_params=_cparams(),
    )(proj, kvmem, kvmem, do)


def _place():
    return lax.axis_index("x"), lax.axis_index("y"), lax.axis_index("c")


def _flip(v, bit):
    return 1 - v if bit else v


def _all_gather(blocks, name):
    nb = len(blocks)

    def body(*refs):
        x_refs, out_refs = refs[:nb], refs[nb:2 * nb]
        send_sems, recv_sems, local_sems = refs[2 * nb:]
        x, y, c = _place()
        me, sibling = (x, y, c), (x, y, 1 - c)
        chips = [(1 - x, y), (x, 1 - y), (1 - x, 1 - y)]

        def copy(b, k, blk, to, from_input=False):
            slot = out_refs[b].at[4 * blk[0] + 2 * blk[1] + blk[2]]
            return pltpu.make_async_remote_copy(
                src_ref=x_refs[b] if from_input else slot, dst_ref=slot,
                send_sem=send_sems.at[b, k], recv_sem=recv_sems.at[b, k],
                device_id=to, device_id_type=pl.DeviceIdType.MESH)

        mine = [pltpu.make_async_copy(x_refs[b], out_refs[b].at[4 * x + 2 * y + c], local_sems.at[b])
                for b in range(nb)]
        for cp in mine:
            cp.start()
        first = []
        for b in range(nb):
            first.append(copy(b, 0, me, sibling, from_input=True))
            first += [copy(b, 1 + n, me, (*chip, c), from_input=True) for n, chip in enumerate(chips)]
        for cp in first:
            cp.start()
        passed = []
        for n, chip in enumerate(chips):
            for b in range(nb):
                copy(b, 1 + n, (*chip, c), me).wait_recv()
                passed.append(copy(b, 4 + n, (*chip, c), sibling))
                passed[-1].start()
        for b in range(nb):
            copy(b, 0, sibling, me).wait_recv()
            for n, chip in enumerate(chips):
                copy(b, 4 + n, (*chip, 1 - c), me).wait_recv()
        for cp in first + passed:
            cp.wait_send()
        for cp in mine:
            cp.wait()

    any_spec = pl.BlockSpec(memory_space=pl.ANY)
    return pl.pallas_call(
        body, name=name, in_specs=[any_spec] * nb, out_specs=[any_spec] * nb,
        out_shape=[jax.ShapeDtypeStruct((N_DEV,) + blk.shape, blk.dtype) for blk in blocks],
        scratch_shapes=[pltpu.SemaphoreType.DMA((nb, 7)), pltpu.SemaphoreType.DMA((nb, 7)),
                        pltpu.SemaphoreType.DMA((nb,))],
    )(*blocks)


def _peers(x, y, c):
    out = []
    for n in range(1, N_DEV):
        peer = (_flip(x, n & 4), _flip(y, n & 2), _flip(c, n & 1))
        out.append((n - 1, peer, 4 * peer[0] + 2 * peer[1] + peer[2]))
    return out


_HBM = pl.BlockSpec(memory_space=pltpu.HBM)
_SEM = pl.BlockSpec(memory_space=pltpu.SEMAPHORE)


def _exchange_start(srcs, scatter, name, after=None):
    ns = len(srcs)
    lands = [lax.empty(s.shape if scatter else (N_DEV,) + s.shape, s.dtype) for s in srcs]

    def body(*refs):
        src_refs, land_refs = refs[:ns], refs[ns:2 * ns]
        pos = 2 * ns + (1 if after is not None else 0)
        send_sems, recv_sems, token = refs[pos], refs[pos + 1], refs[-1]
        x, y, c = _place()
        my_idx = 4 * x + 2 * y + c
        for col, peer, peer_idx in _peers(x, y, c):
            for b in range(ns):
                pltpu.make_async_remote_copy(
                    src_ref=src_refs[b].at[peer_idx] if scatter else src_refs[b], dst_ref=land_refs[b].at[my_idx],
                    send_sem=send_sems.at[b * (N_DEV - 1) + col], recv_sem=recv_sems.at[b * (N_DEV - 1) + col],
                    device_id=peer, device_id_type=pl.DeviceIdType.MESH).start()
        token[...] = jnp.zeros_like(token)

    args = [pltpu.with_memory_space_constraint(a, pltpu.HBM) for a in list(srcs) + lands]
    in_specs = [_HBM] * (2 * ns)
    if after is not None:
        args.append(after)
        in_specs.append(pl.BlockSpec(memory_space=pl.ANY))
    out = pl.pallas_call(
        body, name=name, in_specs=in_specs,
        out_specs=[_SEM, _SEM] + [_HBM] * (2 * ns) + [pl.BlockSpec(memory_space=pltpu.VMEM)],
        out_shape=[pltpu.SemaphoreType.DMA((ns * (N_DEV - 1),)), pltpu.SemaphoreType.DMA((ns * (N_DEV - 1),))]
        + [pltpu.HBM(a.shape, a.dtype) for a in list(srcs) + lands] + [jax.ShapeDtypeStruct((8, SLOT), F32)],
        input_output_aliases={k: 2 + k for k in range(2 * ns)},
        compiler_params=pltpu.CompilerParams(has_side_effects=pltpu.SideEffectType.DATAFLOW_SIDE_EFFECTING),
    )(*args)
    return (out[0], out[1], out[2:2 + ns], out[2 + ns:2 + 2 * ns], scatter), out[-1]


def _exchange_wait(handle, after, name):
    send_sems, recv_sems, srcs, lands, scatter = handle
    ns = len(srcs)

    def body(*refs):
        src_refs, land_refs = refs[:ns], refs[ns:2 * ns]
        send_ref, recv_ref = refs[2 * ns], refs[2 * ns + 1]
        x, y, c = _place()
        for col, peer, peer_idx in _peers(x, y, c):
            for b in range(ns):
                copy = pltpu.make_async_remote_copy(
                    src_ref=src_refs[b].at[peer_idx] if scatter else src_refs[b], dst_ref=land_refs[b].at[peer_idx],
                    send_sem=send_ref.at[b * (N_DEV - 1) + col], recv_sem=recv_ref.at[b * (N_DEV - 1) + col],
                    device_id=peer, device_id_type=pl.DeviceIdType.MESH)
                copy.wait_send()
                copy.wait_recv()

    out = pl.pallas_call(
        body, name=name, in_specs=[_HBM] * (2 * ns) + [_SEM, _SEM, pl.BlockSpec(memory_space=pl.ANY)],
        out_specs=[_HBM] * (2 * ns),
        out_shape=[pltpu.HBM(a.shape, a.dtype) for a in list(srcs) + list(lands)],
        input_output_aliases={k: k for k in range(2 * ns)},
        compiler_params=pltpu.CompilerParams(has_side_effects=pltpu.SideEffectType.DATAFLOW_SIDE_EFFECTING),
    )(*srcs, *lands, send_sems, recv_sems, after)
    my_idx = 4 * lax.axis_index("x") + 2 * lax.axis_index("y") + lax.axis_index("c")
    landed = []
    for src, land in zip(out[:ns], out[ns:]):
        own = lax.dynamic_index_in_dim(src, my_idx, 0, keepdims=True) if scatter else src[None]
        landed.append(lax.dynamic_update_index_in_dim(land, own, my_idx, 0))
    return landed


def _adamw(parts, w, m, v, name):
    lyr, rows, cols = w.shape
    assert len(parts) == lyr
    tr = ADAM_ROWS if cols > 512 else 2 * ADAM_ROWS
    while rows % tr:
        tr //= 2
    tr = min(tr, rows)

    def body(*refs):
        p_refs = refs[:lyr]
        w_ref, m_ref, v_ref, g_out, d_out, m_out, v_out = refs[lyr:]
        for k in range(lyr):
            @pl.when(pl.program_id(0) == k)
            def _(p_ref=p_refs[k]):
                g = p_ref[0].astype(F32)
                for s in range(1, N_DEV):
                    g = g + p_ref[s].astype(F32)
                m2 = ADAM_B1 * m_ref[...] + (1.0 - ADAM_B1) * g
                v2 = ADAM_B2 * v_ref[...] + (1.0 - ADAM_B2) * (g * g)
                m_hat = m2 / (1.0 - ADAM_B1 ** ADAM_STEP)
                v_hat = v2 / (1.0 - ADAM_B2 ** ADAM_STEP)
                g_out[...] = g
                d_out[...] = -ADAM_LR * (m_hat / (jnp.sqrt(v_hat) + ADAM_EPS) + ADAM_WD * w_ref[...])
                m_out[...] = m2
                v_out[...] = v2

    def part_spec(k):
        return pl.BlockSpec((N_DEV, tr, cols), lambda l, i: (0, jnp.where(l == k, i, 0), 0))

    spec = pl.BlockSpec((None, tr, cols), lambda l, i: (l, i, 0))
    shp = jax.ShapeDtypeStruct((lyr, rows, cols), F32)
    return pl.pallas_call(
        body, name=name, grid=(lyr, rows // tr),
        in_specs=[part_spec(k) for k in range(lyr)] + [spec, spec, spec],
        out_specs=[spec] * 4, out_shape=[shp] * 4, compiler_params=_cparams(),
    )(*parts, w, m, v)


def _pack(arrays, lanes, row_mult, dtype):
    flat = jnp.concatenate([a.reshape(-1).astype(dtype) for a in arrays])
    unit = lanes * row_mult
    total = -(-flat.shape[0] // unit) * unit
    return jnp.pad(flat, (0, total - flat.shape[0])).reshape(total // lanes, lanes)


def _unpack(packed, shapes):
    flat = packed.reshape(-1)
    out, off = [], 0
    for shp in shapes:
        n = 1
        for d in shp:
            n *= d
        out.append(flat[off:off + n].reshape(shp))
        off += n
    return out


def _pad_slots(w, axis):
    axis = axis % w.ndim
    n = w.shape[axis] // HEAD_DIM
    shp = w.shape[:axis] + (n, HEAD_DIM) + w.shape[axis + 1:]
    pad = [(0, 0)] * (w.ndim + 1)
    pad[axis + 1] = (0, SLOT - HEAD_DIM)
    return jnp.pad(w.reshape(shp), pad).reshape(w.shape[:axis] + (n * SLOT,) + w.shape[axis + 1:])


def _unpad_slots(w, axis, keep=HEAD_DIM):
    axis = axis % w.ndim
    n = w.shape[axis] // SLOT
    shp = w.shape[:axis] + (n, SLOT) + w.shape[axis + 1:]
    idx = [slice(None)] * (w.ndim + 1)
    idx[axis + 1] = slice(0, keep)
    return w.reshape(shp)[tuple(idx)].reshape(w.shape[:axis] + (n * keep,) + w.shape[axis + 1:])


def _mla_in_pad(w):
    z = functools.partial(jnp.zeros, dtype=w.dtype)
    rows = w.shape[0]
    return jnp.concatenate([w[:, :384], z((rows, 64)), w[:, 640:672], z((rows, 32)), w[:, 384:640],
                            _pad_slots(w[:, 672:], 1)], axis=1)


def _mla_in_unpad(d):
    return jnp.concatenate([d[:, :384], d[:, 512:768], d[:, 448:480], _unpad_slots(d[:, 768:], 1)], axis=1)


def _mla_uq_pad(w):
    return jnp.pad(w.reshape(w.shape[0], MLA_HEADS, MLA_QK), ((0, 0), (0, 0), (0, SLOT - MLA_QK))).reshape(
        w.shape[0], MLA_HEADS * SLOT)


def _join(gathered, axis):
    nd, a, b = gathered.shape
    if axis == 1:
        return gathered.reshape(nd * a, b)
    return gathered.transpose(1, 0, 2).reshape(a, nd * b)


def _split(full, axis):
    r, c = full.shape
    if axis == 1:
        return full.reshape(N_DEV, r // N_DEV, c).astype(BF16)
    return full.reshape(r, N_DEV, c // N_DEV).transpose(1, 0, 2).astype(BF16)


def kernel(x, mem, positions, attn_norm_g, mlp_norm_g, mem_norm_g, final_norm_g, mla_w_in, mla_q_norm_g, mla_kv_norm_g, mla_w_uq, mla_w_ukv, swa_w_in, swa_sinks, w_mem_kv, w_o, mlp_w_up, mlp_w_down, loss_target, m_attn_norm_g, m_mlp_norm_g, m_mem_norm_g, m_final_norm_g, m_mla_w_in, m_mla_q_norm_g, m_mla_kv_norm_g, m_mla_w_uq, m_mla_w_ukv, m_swa_w_in, m_swa_sinks, m_w_mem_kv, m_w_o, m_mlp_w_up, m_mlp_w_down, v_attn_norm_g, v_mlp_norm_g, v_mem_norm_g, v_final_norm_g, v_mla_w_in, v_mla_q_norm_g, v_mla_kv_norm_g, v_mla_w_uq, v_mla_w_ukv, v_swa_w_in, v_swa_sinks, v_w_mem_kv, v_w_o, v_mlp_w_up, v_mlp_w_down):
    given = dict(locals())
    seq = x.shape[1]
    x0 = x.reshape(seq, D_MODEL)
    tgt = loss_target.reshape(seq, D_MODEL)
    mem0 = mem.reshape(N_MEM, D_MODEL)
    pos = positions.reshape(seq).astype(F32)
    pos_col, pos_row = pos.reshape(seq, 1), pos.reshape(1, seq)

    def layer_names(i):
        mixer = ("mla_w_in", "mla_w_uq", "mla_w_ukv") if i % 2 == 0 else ("swa_w_in",)
        return [(n, i // 2) for n in mixer] + [(n, i) for n in ("w_mem_kv", "w_o", "mlp_w_up", "mlp_w_down")]

    def local_weights(names):
        return [given[n][l].astype(BF16) for n, l in names]

    first_attn, first_mlp = layer_names(0)[:-2], layer_names(0)[-2:]
    weights = [dict(zip([n for n, _ in first_attn], _all_gather(local_weights(first_attn), "gather_weights_first")))]
    coming_mlp, first_token = _exchange_start(local_weights(first_mlp), False, "gather_weights_start_0",
                                              after=weights[0]["w_o"])

    consts = _lane_consts()
    tabs = _rope_tables(pos_col, consts)
    slopes = 2.0 ** (-8.0 * (jnp.arange(SWA_HEADS, dtype=F32) + 1.0) / SWA_HEADS)

    mem_n = _rmsnorm_fwd(mem0, 0, D_MODEL, mem_norm_g, "rmsnorm_fwd_mem")

    saved = []
    xc = x0
    for i in range(DEPTH):
        j = i // 2
        wts = weights[i]
        s = {"x_in": xc}
        token = None
        if i + 1 < DEPTH:
            coming, token = _exchange_start(local_weights(layer_names(i + 1)), False,
                                            "gather_weights_start_%d" % (i + 1),
                                            after=first_token if i == 0 else wts["w_o"])
        hn = _rmsnorm_fwd(xc, 0, D_MODEL, attn_norm_g[i], "rmsnorm_fwd", after=token)
        if i % 2 == 0:
            w_in = _mla_in_pad(_join(wts["mla_w_in"], 1))
            w_uq = _mla_uq_pad(_join(wts["mla_w_uq"], 2))
            w_kv = _join(wts["mla_w_ukv"], 2)
            proj = _mm(hn, w_in, "nn", F32, "mm_mla_in")
            cqn = _rmsnorm_fwd(proj, 0, MLA_Q_RANK, mla_q_norm_g[j], "rmsnorm_fwd_q")
            ckvn = _rmsnorm_fwd(proj, 2, MLA_KV_RANK, mla_kv_norm_g[j], "rmsnorm_fwd_kv")
            qraw = _mm(cqn, w_uq, "nn", F32, "mm_mla_uq")
            kvraw = _mm(ckvn, w_kv, "nn", F32, "mm_mla_ukv")
            q, k, v = _mla_rope_fwd(qraw, kvraw, proj, tabs)
            o, lse = _mla_attn_fwd(q, k, v)
            qoff = MLA_QOFF
            s.update(w_uq=w_uq, w_kv=w_kv, cqn=cqn, ckvn=ckvn, q=q, k=k, v=v)
        else:
            w_in = _join(wts["swa_w_in"], 2)
            proj = _mm(hn, w_in, "nn", BF16, "mm_swa_in", pairs="o")
            o, lse = _swa_attn_fwd(proj, pos_col, pos_row, slopes, swa_sinks[j])
            qoff = SWA_QOFF
        w_mem = _pad_slots(_join(wts["w_mem_kv"], 1), 1)
        w_out = _join(wts["w_o"], 1)
        w_o_mix, w_o_cross = w_out[:SWA_HEADS * HEAD_DIM], w_out[SWA_HEADS * HEAD_DIM:]
        kvmem = _mm(mem_n, w_mem, "nn", BF16, "mm_mem_kv")
        cross = _cross_attn_fwd(proj, qoff, kvmem)
        x1 = _mm(o, w_o_mix, "nn", F32, "mm_o_mix", res=xc, pairs="a")
        x1 = _mm(cross, w_o_cross, "nn", F32, "mm_o_cross", res=x1, pairs="a")
        hn2 = _rmsnorm_fwd(x1, 0, D_MODEL, mlp_norm_g[i], "rmsnorm_fwd")
        if i == 0:
            wts.update(zip([n for n, _ in first_mlp], _exchange_wait(coming_mlp, hn2, "gather_weights_wait_0")))
        act, act2 = _mm(hn2, wts["mlp_w_up"], "nn", BF16, "mm_mlp_up", epi="relu2", b_blk="cols")
        xc = _mm(act2, wts["mlp_w_down"], "nn", F32, "mm_mlp_down", res=x1, b_blk="rows")
        s.update(hn=hn, w_in=w_in, proj=proj, o=o, lse=lse, qoff=qoff, w_mem=w_mem, w_o_mix=w_o_mix,
                 w_o_cross=w_o_cross, kvmem=kvmem, cross=cross, x1=x1, hn2=hn2, act=act, act2=act2)
        saved.append(s)
        if i + 1 < DEPTH:
            got = _exchange_wait(coming, xc, "gather_weights_wait_%d" % (i + 1))
            weights.append(dict(zip([n for n, _ in layer_names(i + 1)], got)))

    dx, dx_b, dg_final, loss_part = _loss_head(xc, final_norm_g, tgt)
    loss = lax.psum(loss_part[0, 0], MESH_AXES)

    gains = {n: [None] * DEPTH for n in ("attn_norm_g", "mlp_norm_g")}
    for n in ("mla_q_norm_g", "mla_kv_norm_g", "swa_sinks"):
        gains[n] = [None] * 2
    leaving = {}
    token = None
    dmem_n = None
    for i in reversed(range(DEPTH)):
        j = i // 2
        s = saved[i]
        wts = weights[i]
        out = {}
        du = _mm(dx_b, wts["mlp_w_down"], "nt", BF16, "mm_mlp_down_dx", aux=s["act"], epi="mul2aux", b_blk="rows",
                 after=token)
        out["mlp_w_down"] = _mm(s["act2"], dx_b, "tn", BF16, "mm_mlp_down_dw", o_blk="rows")
        out["mlp_w_up"] = _mm(s["hn2"], du, "tn", BF16, "mm_mlp_up_dw", o_blk="cols")
        dx1, dx1_b, dg = _mm(du, wts["mlp_w_up"], "nt", F32, "mm_mlp_up_dx", b_blk="cols",
                             epi="normbwd", norm=(s["x1"], mlp_norm_g[i], dx))
        gains["mlp_norm_g"][i] = dg[0]

        do = _mm(dx1_b, s["w_o_mix"], "nt", BF16, "mm_o_mix_dx", pairs="o")
        dcross = _mm(dx1_b, s["w_o_cross"], "nt", BF16, "mm_o_cross_dx", pairs="o")
        dw_o = jnp.concatenate([_mm(s["o"], dx1_b, "tn", F32, "mm_o_mix_dw", pairs="a"),
                                _mm(s["cross"], dx1_b, "tn", F32, "mm_o_cross_dw", pairs="a")], axis=0)
        out["w_o"] = _split(dw_o, 1)
        dqc, dkm, dvm = _cross_attn_bwd(s["proj"], s["qoff"], s["kvmem"], dcross)
        dkvmem = jnp.concatenate([dkm, dvm], axis=1).astype(BF16)
        out["w_mem_kv"] = _split(_unpad_slots(_mm(mem_n, dkvmem, "tn", F32, "mm_mem_kv_dw"), 1), 1)
        dmem_n = _mm(dkvmem, s["w_mem"], "nt", F32, "mm_mem_kv_dx" if dmem_n is None else "mm_mem_kv_dx_acc",
                     res=dmem_n)
        leaving[(i, "main")], token = _exchange_start([out[n] for n, _ in layer_names(i)[-4:]], True,
                                                      "exchange_grads_main_start_%d" % i)

        if i % 2 == 0:
            dq, dk, dv = _mla_attn_bwd(s["q"], s["k"], s["v"], do, s["lse"], _mla_delta(s["o"], do), token)
            dqraw, dkv, dkr = _mla_rope_bwd(dq, dk, dv, tabs, consts)
            dcqn = _mm(dqraw, s["w_uq"], "nt", F32, "mm_mla_uq_dx")
            out["mla_w_uq"] = _split(_unpad_slots(_mm(s["cqn"], dqraw, "tn", F32, "mm_mla_uq_dw"), 1, MLA_QK), 2)
            dckvn = _mm(dkv, s["w_kv"], "nt", F32, "mm_mla_ukv_dx")
            out["mla_w_ukv"] = _split(_mm(s["ckvn"], dkv, "tn", F32, "mm_mla_ukv_dw"), 2)
            dcq, dg = _rmsnorm_bwd(s["proj"], 0, MLA_Q_RANK, mla_q_norm_g[j], dcqn, None, BF16, "rmsnorm_bwd_q")
            gains["mla_q_norm_g"][j] = dg[0]
            dckv, dg = _rmsnorm_bwd(s["proj"], 2, MLA_KV_RANK, mla_kv_norm_g[j], dckvn, None, BF16, "rmsnorm_bwd_kv")
            gains["mla_kv_norm_g"][j] = dg[0]
            dproj = jnp.concatenate([dcq, dkr.astype(BF16), dckv, dqc.astype(BF16)], axis=1)
            in_dx = "mm_mla_in_dx"
            out["mla_w_in"] = _split(_mla_in_unpad(_mm(s["hn"], dproj, "tn", F32, "mm_mla_in_dw")), 1)
        else:
            dq, dk, dv, dsink = _swa_attn_bwd(s["proj"], s["o"], do, s["lse"], pos_col, pos_row, slopes, swa_sinks[j],
                                              token)
            gains["swa_sinks"][j] = dsink[::8, 0]
            dproj = jnp.concatenate([dq, dk, dv, dqc], axis=1).astype(BF16)
            in_dx = "mm_swa_in_dx"
            out["swa_w_in"] = _split(_mm(s["hn"], dproj, "tn", F32, "mm_swa_in_dw", pairs="b"), 2)
        dx, dx_b, dg = _mm(dproj, s["w_in"], "nt", F32, in_dx, epi="normbwd", norm=(s["x_in"], attn_norm_g[i], dx1),
                           pairs="" if i % 2 == 0 else "a")
        gains["attn_norm_g"][i] = dg[0]

        leaving[(i, "mixer")], token = _exchange_start([out[n] for n, _ in layer_names(i)[:-4]], True,
                                                       "exchange_grads_mixer_start_%d" % i)

    _, dg_mem = _rmsnorm_bwd(mem0, 0, D_MODEL, mem_norm_g, dmem_n, None, BF16, "rmsnorm_bwd_mem")
    gains = {n: jnp.stack(g) for n, g in gains.items()}
    gains["mem_norm_g"] = dg_mem[0]
    gains["final_norm_g"] = dg_final[0]

    result = {}

    def adamw_of(names, received):
        for n in names:
            parts = [received[(n, l)] for l in range(given[n].shape[0])]
            for kind, r in enumerate(_adamw(parts, given[n], given["m_" + n], given["v_" + n], "adamw_" + n)):
                result[(kind, n)] = r

    received = {}
    for i in reversed(range(DEPTH)):
        got = _exchange_wait(leaving[(i, "main")], dx, "exchange_grads_main_wait_%d" % i)
        received.update(zip(layer_names(i)[-4:], got))
    adamw_of(("mlp_w_up", "mlp_w_down", "w_o", "w_mem_kv"), received)
    for i in reversed(range(DEPTH)):
        got = _exchange_wait(leaving[(i, "mixer")], result[(0, "w_mem_kv")], "exchange_grads_mixer_wait_%d" % i)
        received.update(zip(layer_names(i)[:-4], got))
    adamw_of(("mla_w_in", "mla_w_uq", "mla_w_ukv", "swa_w_in"), received)

    rep_shapes = [given[n].shape for n in REPLICATED]
    rep_parts = _all_gather([_pack([gains[n] for n in REPLICATED], SLOT, 8, F32)], "gather_gain_grads")[0]
    rep_packed = [_pack([given[p + n] for n in REPLICATED], SLOT, 8, F32)[None] for p in ("", "m_", "v_")]
    for kind, r in enumerate(_adamw([rep_parts], *rep_packed, "adamw_gains")):
        for n, part in zip(REPLICATED, _unpack(r[0], rep_shapes)):
            result[(kind, n)] = part

    outs = [loss, dx.reshape(1, seq, D_MODEL)]
    for kind in range(4):
        outs += [result[(kind, n)] for n in WEIGHT_ORDER]
    return tuple(outs)
```

```python
import functools

import jax
import jax.numpy as jnp
from jax import lax
from jax.experimental import pallas as pl
from jax.experimental.pallas import tpu as pltpu

F32 = jnp.float32
BF16 = jnp.bfloat16

D_MODEL = 1024
N_MEM = 256
DEPTH = 4
SLOT = 128
HEAD_DIM = 64
MLA_HEADS = 12
MLA_QK = 96
MLA_Q_RANK = 384
MLA_KV_RANK = 256
SWA_HEADS = 12
SWA_KV_HEADS = 4
SWA_GROUP = 3
MEM_HEADS = 4
WINDOW = 128
EPS = 1e-6
NEG = -1e30
ROPE_THETA = 10000.0
N_DEV = 8

ADAM_LR = 0.001
ADAM_B1 = 0.9
ADAM_B2 = 0.999
ADAM_EPS = 1e-08
ADAM_WD = 0.01
ADAM_STEP = 10

TM = 512
TQ_MLA = 1024
MLA_PACK = 2
SWA_PACK = 4
TQ_CROSS = 2048
MM_VMEM_BUDGET = 38 * 1024 * 1024
ADAM_ROWS = 128
VMEM_LIMIT = 56 * 1024 * 1024

MESH_AXES = ("x", "y", "c")

LOG2_E = 1.4426950408889634
MLA_SCALE = MLA_QK ** -0.5
MLA_Q_SCALE = MLA_SCALE * LOG2_E

MLA_QOFF = (MLA_Q_RANK + SLOT + MLA_KV_RANK) // SLOT
SWA_QOFF = SWA_HEADS + 2 * SWA_KV_HEADS

SHARDED = (
    ("mla_w_in", 1), ("mla_w_uq", 2), ("mla_w_ukv", 2), ("swa_w_in", 2),
    ("w_mem_kv", 1), ("w_o", 1), ("mlp_w_up", 2), ("mlp_w_down", 1),
)
REPLICATED = ("attn_norm_g", "mlp_norm_g", "mem_norm_g", "final_norm_g",
              "mla_q_norm_g", "mla_kv_norm_g", "swa_sinks")
WEIGHT_ORDER = ("attn_norm_g", "mlp_norm_g", "mem_norm_g", "final_norm_g", "mla_w_in",
                "mla_q_norm_g", "mla_kv_norm_g", "mla_w_uq", "mla_w_ukv", "swa_w_in",
                "swa_sinks", "w_mem_kv", "w_o", "mlp_w_up", "mlp_w_down")


def _cparams():
    return pltpu.CompilerParams(vmem_limit_bytes=VMEM_LIMIT)


_DIMS = {"nn": (((1,), (0,)), ((), ())), "nt": (((1,), (1,)), ((), ())), "tn": (((0,), (0,)), ((), ()))}


def _compact(x):
    pairs = [x[:, 2 * j * SLOT:(2 * j + 1) * SLOT] + pltpu.roll(x[:, (2 * j + 1) * SLOT:(2 * j + 2) * SLOT], HEAD_DIM, 1)
             for j in range(x.shape[1] // (2 * SLOT))]
    return pairs[0] if len(pairs) == 1 else jnp.concatenate(pairs, axis=1)


def _expand(x):
    low = lax.broadcasted_iota(jnp.int32, (x.shape[0], SLOT), 1) < HEAD_DIM
    slots = []
    for j in range(x.shape[1] // SLOT):
        pair = x[:, j * SLOT:(j + 1) * SLOT]
        slots += [jnp.where(low, pair, 0.0), pltpu.roll(jnp.where(low, 0.0, pair), HEAD_DIM, 1)]
    return jnp.concatenate(slots, axis=1)


def _mm_tiles(m, n, k, a_bytes, b_bytes, o_bytes, extra_bytes, tm_fixed, tn_fixed):
    best = None
    for tm in ([tm_fixed] if tm_fixed else [t for t in range(4096, 0, -SLOT) if m % t == 0] or [m]):
        for tn in ([tn_fixed] if tn_fixed else [t for t in range(1024, 0, -SLOT) if n % t == 0] or [n]):
            need = 2 * (tm * k * a_bytes + k * tn * b_bytes + tm * tn * (o_bytes + extra_bytes))
            need += tm * tn * 4
            if need <= MM_VMEM_BUDGET and (best is None or tm * tn > best[0] * best[1]):
                best = (tm, tn)
    assert best is not None, (m, n, k)
    return best


def _mm(a, b, mode, out_dtype, name, res=None, aux=None, epi=None, b_blk=None, o_blk=None, after=None, norm=None,
        pairs=""):
    if b_blk is not None:
        nb, br, bc = b.shape
        b_shape = (nb * br, bc) if b_blk == "rows" else (br, nb * bc)
    else:
        b_shape = b.shape
    assert not pairs or (b_blk is None and o_blk is None and not ("b" in pairs and mode == "nt"))
    a_shape = (a.shape[0], a.shape[1] // 2) if "a" in pairs else a.shape
    if "b" in pairs:
        b_shape = (b_shape[0], b_shape[1] // 2)
    if mode == "nn":
        (m, k), (k2, n) = a_shape, b_shape
    elif mode == "nt":
        (m, k), (n, k2) = a_shape, b_shape
    else:
        (k, m), (k2, n) = a_shape, b_shape
    assert k == k2, (a.shape, b_shape, mode)
    k_blocked = b_blk is not None and (b_blk == "rows") == (mode != "nt")
    tn_fixed = None
    if b_blk is not None and not k_blocked:
        tn_fixed = br if b_blk == "rows" else bc
    if o_blk == "cols":
        tn_fixed = n // N_DEV
    tm_fixed = m // N_DEV if o_blk == "rows" else None
    has_res, has_aux, has_norm = res is not None, aux is not None, epi == "normbwd"
    assert o_blk is None or not (has_res or has_aux or has_norm)
    n_out = 2 if epi == "relu2" else 1
    if has_norm:
        tn_fixed = n
        o_bytes, extra_bytes = 4 + 2, 4 + 4
    else:
        o_bytes = n_out * jnp.dtype(out_dtype).itemsize
        extra_bytes = (4 if has_res else 0) + (aux.dtype.itemsize if has_aux else 0)
    pa, pb, po = (2 if "a" in pairs else 1), (2 if "b" in pairs else 1), (2 if "o" in pairs else 1)
    tm, tn = _mm_tiles(m, n, k, a.dtype.itemsize * (3 if pa == 2 else 1), b.dtype.itemsize * (3 if pb == 2 else 1),
                       o_bytes * po, extra_bytes, tm_fixed, tn_fixed)
    dims = _DIMS[mode]
    if mode == "tn":
        a_spec = pl.BlockSpec((k, pa * tm), lambda i, j: (0, i))
    else:
        a_spec = pl.BlockSpec((tm, pa * k), lambda i, j: (i, 0))
    if b_blk is None:
        if mode == "nt":
            b_spec = pl.BlockSpec((tn, k), lambda i, j: (j, 0))
        else:
            b_spec = pl.BlockSpec((k, pb * tn), lambda i, j: (0, j))
    elif k_blocked and mode == "nt":
        b_spec = pl.BlockSpec((N_DEV, tn, bc), lambda i, j: (0, j, 0))
    elif k_blocked:
        b_spec = pl.BlockSpec((N_DEV, br, tn), lambda i, j: (0, 0, j))
    elif mode == "nt":
        b_spec = pl.BlockSpec((None, tn, k), lambda i, j: (j, 0, 0))
    else:
        b_spec = pl.BlockSpec((None, k, tn), lambda i, j: (j, 0, 0))
    if o_blk is None:
        o_spec = pl.BlockSpec((tm, po * tn), lambda i, j: (i, j))
        o_shape = (m, po * n)
    elif o_blk == "rows":
        o_spec = pl.BlockSpec((None, tm, tn), lambda i, j: (i, 0, j))
        o_shape = (N_DEV, tm, n)
    else:
        o_spec = pl.BlockSpec((None, tm, tn), lambda i, j: (j, i, 0))
        o_shape = (N_DEV, m, tn)

    def body(*refs):
        a_ref, b_ref = refs[0], refs[1]
        pos = 2
        res_ref = aux_ref = None
        if has_res:
            res_ref = refs[pos]
            pos += 1
        if has_aux:
            aux_ref = refs[pos]
            pos += 1
        if has_norm:
            x_ref, g_ref, dres_ref = refs[pos:pos + 3]
            pos += 3
        if after is not None:
            pos += 1
        outs = refs[pos:]
        if k_blocked and mode == "nt":
            r = None
            for d in range(N_DEV):
                part = lax.dot_general(a_ref[:, d * bc:(d + 1) * bc].astype(BF16), b_ref[d].astype(BF16), dims,
                                       preferred_element_type=F32)
                r = part if r is None else r + part
        else:
            bv = b_ref[...].reshape(k, tn) if k_blocked else b_ref[...]
            av = _compact(a_ref[...].astype(F32)) if pa == 2 else a_ref[...]
            bv = _compact(bv.astype(F32)) if pb == 2 else bv
            r = lax.dot_general(av.astype(BF16), bv.astype(BF16), dims, preferred_element_type=F32)
        if po == 2:
            r = _expand(r)
        if epi == "relu2":
            r = jnp.maximum(r, 0.0)
            outs[0][...] = r.astype(outs[0].dtype)
            outs[1][...] = (r * r).astype(outs[1].dtype)
        elif has_norm:
            xv = x_ref[...]
            rs = lax.rsqrt(jnp.mean(xv * xv, axis=1, keepdims=True) + EPS)
            xh = xv * rs
            dxh = r * g_ref[...]
            dx = rs * (dxh - xh * jnp.mean(dxh * xh, axis=1, keepdims=True)) + dres_ref[...]
            outs[0][...] = dx
            outs[1][...] = dx.astype(BF16)

            @pl.when(pl.program_id(0) == 0)
            def _():
                outs[2][...] = jnp.zeros_like(outs[2])

            outs[2][...] += jnp.sum(r * xh, axis=0, keepdims=True)
        else:
            if epi == "mul2aux":
                r = r * (2.0 * aux_ref[...].astype(F32))
            if has_res:
                r = r + res_ref[...]
            outs[0][...] = r.astype(outs[0].dtype)

    in_specs = [a_spec, b_spec]
    args = [a, b]
    if has_res:
        in_specs.append(o_spec)
        args.append(res)
    if has_aux:
        in_specs.append(o_spec)
        args.append(aux)
    vec_spec = pl.BlockSpec((1, n), lambda i, j: (0, 0))
    if has_norm:
        in_specs += [o_spec, vec_spec, o_spec]
        args += [norm[0], norm[1].reshape(1, n), norm[2]]
    if after is not None:
        in_specs.append(pl.BlockSpec(memory_space=pl.ANY))
        args.append(after)
    if has_norm:
        out_specs = [o_spec, o_spec, vec_spec]
        out_shape = [jax.ShapeDtypeStruct(o_shape, F32), jax.ShapeDtypeStruct(o_shape, BF16),
                     jax.ShapeDtypeStruct((1, n), F32)]
    else:
        out_specs = [o_spec] * n_out
        out_shape = [jax.ShapeDtypeStruct(o_shape, out_dtype)] * n_out
    out = pl.pallas_call(
        body, name=name, grid=(m // tm, n // tn),
        in_specs=in_specs, out_specs=out_specs, out_shape=out_shape, compiler_params=_cparams(),
    )(*args)
    return out if len(out) > 1 else out[0]


def _rmsnorm_fwd(xarr, colblk, width, g, name, after=None):
    rows = xarr.shape[0]
    tm = min(TM, rows)

    def body(x_ref, g_ref, *rest):
        y_ref = rest[-1]
        x = x_ref[...].astype(F32)
        r = lax.rsqrt(jnp.mean(x * x, axis=1, keepdims=True) + EPS)
        y_ref[...] = (x * r * g_ref[...]).astype(y_ref.dtype)

    in_specs = [pl.BlockSpec((tm, width), lambda i: (i, colblk)), pl.BlockSpec((1, width), lambda i: (0, 0))]
    args = [xarr, g.reshape(1, width)]
    if after is not None:
        in_specs.append(pl.BlockSpec(memory_space=pl.ANY))
        args.append(after)
    return pl.pallas_call(
        body, name=name, grid=(rows // tm,), in_specs=in_specs,
        out_specs=pl.BlockSpec((tm, width), lambda i: (i, 0)),
        out_shape=jax.ShapeDtypeStruct((rows, width), BF16), compiler_params=_cparams(),
    )(*args)


def _rmsnorm_bwd(xarr, colblk, width, g, dy, dres, out_dtype, name):
    rows = xarr.shape[0]
    tm = min(TM, rows)
    has_res = dres is not None

    def body(*refs):
        x_ref, g_ref, dy_ref = refs[0], refs[1], refs[2]
        dres_ref = refs[3] if has_res else None
        dx_ref, dg_ref = refs[-2], refs[-1]
        x = x_ref[...].astype(F32)
        dyv = dy_ref[...].astype(F32)
        r = lax.rsqrt(jnp.mean(x * x, axis=1, keepdims=True) + EPS)
        xh = x * r
        dxh = dyv * g_ref[...]
        dx = r * (dxh - xh * jnp.mean(dxh * xh, axis=1, keepdims=True))
        if has_res:
            dx = dx + dres_ref[...]
        dx_ref[...] = dx.astype(dx_ref.dtype)

        @pl.when(pl.program_id(0) == 0)
        def _():
            dg_ref[...] = jnp.zeros_like(dg_ref)

        dg_ref[...] += jnp.sum(dyv * xh, axis=0, keepdims=True)

    row_spec = pl.BlockSpec((tm, width), lambda i: (i, 0))
    vec_spec = pl.BlockSpec((1, width), lambda i: (0, 0))
    in_specs = [pl.BlockSpec((tm, width), lambda i: (i, colblk)), vec_spec, row_spec]
    args = [xarr, g.reshape(1, width), dy]
    if has_res:
        in_specs.append(row_spec)
        args.append(dres)
    return pl.pallas_call(
        body, name=name, grid=(rows // tm,), in_specs=in_specs, out_specs=[row_spec, vec_spec],
        out_shape=[jax.ShapeDtypeStruct((rows, width), out_dtype), jax.ShapeDtypeStruct((1, width), F32)],
        compiler_params=_cparams(),
    )(*args)


def _loss_head(x, g, tgt):
    rows, width = x.shape
    tm = min(TM, rows)

    def body(x_ref, g_ref, t_ref, dx_ref, dxb_ref, dg_ref, loss_ref):
        xv = x_ref[...]
        gv = g_ref[...]
        r = lax.rsqrt(jnp.mean(xv * xv, axis=1, keepdims=True) + EPS)
        xh = xv * r
        err = xh * gv - t_ref[...]
        part = 0.5 * jnp.sum(jnp.mean(err * err, axis=1, keepdims=True), axis=0, keepdims=True)
        dyv = err * (1.0 / width)
        dxh = dyv * gv
        dxv = r * (dxh - xh * jnp.mean(dxh * xh, axis=1, keepdims=True))
        dx_ref[...] = dxv
        dxb_ref[...] = dxv.astype(BF16)

        @pl.when(pl.program_id(0) == 0)
        def _():
            dg_ref[...] = jnp.zeros_like(dg_ref)
            loss_ref[...] = jnp.zeros_like(loss_ref)

        dg_ref[...] += jnp.sum(dyv * xh, axis=0, keepdims=True)
        loss_ref[...] += jnp.broadcast_to(part, loss_ref.shape)

    row_spec = pl.BlockSpec((tm, width), lambda i: (i, 0))
    vec_spec = pl.BlockSpec((1, width), lambda i: (0, 0))
    return pl.pallas_call(
        body, name="loss_head", grid=(rows // tm,), in_specs=[row_spec, vec_spec, row_spec],
        out_specs=[row_spec, row_spec, vec_spec, pl.BlockSpec((1, SLOT), lambda i: (0, 0))],
        out_shape=[jax.ShapeDtypeStruct((rows, width), F32), jax.ShapeDtypeStruct((rows, width), BF16),
                   jax.ShapeDtypeStruct((1, width), F32), jax.ShapeDtypeStruct((1, SLOT), F32)],
        compiler_params=_cparams(),
    )(x, g.reshape(1, width), tgt)


def _lane_consts():
    half = 16
    inv = ROPE_THETA ** (-(jnp.arange(half, dtype=F32) * 2.0) / 32)
    lane = jnp.arange(SLOT)
    first = (lane >= 64) & (lane < 80)
    second = (lane >= 80) & (lane < 96)
    inv_lane = jnp.where(first | second, inv[(lane - 64) % half], 0.0)
    rows = [inv_lane, (lane < 64).astype(F32), first.astype(F32), second.astype(F32)]
    rows += [jnp.zeros((SLOT,), F32)] * 4
    return jnp.stack(rows).astype(F32)


def _rope_tables(pos_col, consts):
    rows = pos_col.shape[0]
    tm = min(TM, rows)

    def body(p_ref, k_ref, c_ref, s1_ref, s2_ref):
        ang = p_ref[...] * k_ref[0:1, :]
        cos, sin = jnp.cos(ang), jnp.sin(ang)
        first, second = k_ref[2:3, :], k_ref[3:4, :]
        c_ref[...] = k_ref[1:2, :] + (first + second) * cos
        s1_ref[...] = -first * sin
        s2_ref[...] = second * sin

    spec = pl.BlockSpec((tm, SLOT), lambda i: (i, 0))
    shp = jax.ShapeDtypeStruct((rows, SLOT), F32)
    return pl.pallas_call(
        body, name="rope_tables", grid=(rows // tm,),
        in_specs=[pl.BlockSpec((tm, 1), lambda i: (i, 0)), pl.BlockSpec((8, SLOT), lambda i: (0, 0))],
        out_specs=[spec, spec, spec], out_shape=[shp, shp, shp], compiler_params=_cparams(),
    )(pos_col, consts)


def _rot(xv, c, s1, s2):
    return xv * c + pltpu.roll(xv, SLOT - 16, 1) * s1 + pltpu.roll(xv, 16, 1) * s2


def _rot_t(dy, c, s1, s2):
    return dy * c + pltpu.roll(dy * s1, 16, 1) + pltpu.roll(dy * s2, SLOT - 16, 1)


def _mla_rope_fwd(qraw, kvraw, proj, tabs):
    rows = qraw.shape[0]
    tm = min(256, rows)
    hw = MLA_HEADS * SLOT

    def body(q_ref, kv_ref, kr_ref, c_ref, s1_ref, s2_ref, qo, ko, vo):
        c, s1, s2 = c_ref[...], s1_ref[...], s2_ref[...]
        kr = _rot(kr_ref[...], c, s1, s2)
        low = lax.broadcasted_iota(jnp.int32, (tm, SLOT), 1) < HEAD_DIM
        for h in range(MLA_HEADS):
            sl = slice(h * SLOT, (h + 1) * SLOT)
            qo[:, sl] = (_rot(q_ref[:, sl], c, s1, s2) * MLA_Q_SCALE).astype(BF16)
            kvh = kv_ref[:, sl]
            ko[:, sl] = (jnp.where(low, kvh, 0.0) + kr).astype(BF16)
            vo[:, sl] = pltpu.roll(jnp.where(low, 0.0, kvh), HEAD_DIM, 1).astype(BF16)

    tab = pl.BlockSpec((tm, SLOT), lambda i: (i, 0))
    wide = pl.BlockSpec((tm, hw), lambda i: (i, 0))
    shp = jax.ShapeDtypeStruct((rows, hw), BF16)
    return pl.pallas_call(
        body, name="mla_rope_fwd", grid=(rows // tm,),
        in_specs=[wide, wide, pl.BlockSpec((tm, SLOT), lambda i: (i, 3)),
                  tab, tab, tab],
        out_specs=[wide, wide, wide], out_shape=[shp, shp, shp], compiler_params=_cparams(),
    )(qraw, kvraw, proj, *tabs)


def _mla_rope_bwd(dq, dk, dv, tabs, consts):
    rows = dq.shape[0]
    tm = min(256, rows)
    hw = MLA_HEADS * SLOT

    def body(dq_ref, dk_ref, dv_ref, c_ref, s1_ref, s2_ref, k_ref, dqo, dkvo, dkro):
        c, s1, s2 = c_ref[...], s1_ref[...], s2_ref[...]
        ksum = jnp.zeros((tm, SLOT), F32)
        low = lax.broadcasted_iota(jnp.int32, (tm, SLOT), 1) < HEAD_DIM
        for h in range(MLA_HEADS):
            sl = slice(h * SLOT, (h + 1) * SLOT)
            dqo[:, sl] = _rot_t(dq_ref[:, sl], c, s1, s2).astype(BF16)
            dkh = dk_ref[:, sl]
            ksum = ksum + dkh
            dvh = pltpu.roll(jnp.where(low, dv_ref[:, sl], 0.0), HEAD_DIM, 1)
            dkvo[:, sl] = (jnp.where(low, dkh, 0.0) + dvh).astype(BF16)
        dkro[...] = _rot_t(ksum, c, s1, s2) * (k_ref[2:3, :] + k_ref[3:4, :])

    tab = pl.BlockSpec((tm, SLOT), lambda i: (i, 0))
    wide = pl.BlockSpec((tm, hw), lambda i: (i, 0))
    return pl.pallas_call(
        body, name="mla_rope_bwd", grid=(rows // tm,),
        in_specs=[wide, wide, wide, tab, tab, tab, pl.BlockSpec((8, SLOT), lambda i: (0, 0))],
        out_specs=[wide, wide, tab],
        out_shape=[jax.ShapeDtypeStruct((rows, hw), BF16), jax.ShapeDtypeStruct((rows, hw), BF16),
                   jax.ShapeDtypeStruct((rows, SLOT), F32)],
        compiler_params=_cparams(),
    )(dq, dk, dv, *tabs, consts)


def _nt(a, b):
    return lax.dot_general(a, b, _DIMS["nt"], preferred_element_type=F32)


def _tn(a, b):
    return lax.dot_general(a, b, _DIMS["tn"], preferred_element_type=F32)


def _nn(a, b):
    return lax.dot_general(a, b, _DIMS["nn"], preferred_element_type=F32)


def _mla_attn_fwd(q, k, v):
    rows = q.shape[0]
    t = min(TQ_MLA, rows)
    nt = rows // t
    wide = MLA_PACK * SLOT

    def body(q_ref, k_ref, v_ref, o_ref, lse_ref, m_sc, l_sc, acc_sc):
        i, j = pl.program_id(1), pl.program_id(2)

        @pl.when(j == 0)
        def _():
            m_sc[...] = jnp.full_like(m_sc, NEG)
            l_sc[...] = jnp.zeros_like(l_sc)
            acc_sc[...] = jnp.zeros_like(acc_sc)

        def step(diagonal):
            for hh in range(MLA_PACK):
                sl = slice(hh * SLOT, (hh + 1) * SLOT)
                s = _nt(k_ref[:, sl], q_ref[:, sl])
                if diagonal:
                    key = lax.broadcasted_iota(jnp.int32, (t, t), 0)
                    s = jnp.where(key <= lax.broadcasted_iota(jnp.int32, (t, t), 1), s, NEG)
                m_prev = m_sc[hh]
                m_new = jnp.maximum(m_prev, jnp.max(s, axis=0, keepdims=True))
                p = jnp.exp2(s - m_new)
                alpha = jnp.exp2(m_prev - m_new)
                l_new = alpha * l_sc[hh] + jnp.sum(p, axis=0, keepdims=True)
                acc = alpha * acc_sc[hh] + _tn(v_ref[:, sl], p.astype(BF16))
                if diagonal:
                    o_ref[:, sl] = (acc / l_new).T.astype(o_ref.dtype)
                    lse_ref[hh:hh + 1, :] = m_new + jnp.log(l_new) * LOG2_E
                else:
                    m_sc[hh] = m_new
                    l_sc[hh] = l_new
                    acc_sc[hh] = acc

        @pl.when(j < i)
        def _():
            step(False)

        @pl.when(j == i)
        def _():
            lse_ref[...] = jnp.zeros_like(lse_ref)
            step(True)

    q_spec = pl.BlockSpec((t, wide), lambda h, i, j: (i, h))
    kv_spec = pl.BlockSpec((t, wide), lambda h, i, j: (jnp.minimum(j, i), h))
    return pl.pallas_call(
        body, name="mla_attn_fwd", grid=(MLA_HEADS // MLA_PACK, nt, nt),
        in_specs=[q_spec, kv_spec, kv_spec],
        out_specs=[q_spec, pl.BlockSpec((None, 8, t), lambda h, i, j: (h, 0, i))],
        out_shape=[jax.ShapeDtypeStruct(q.shape, BF16),
                   jax.ShapeDtypeStruct((MLA_HEADS // MLA_PACK, 8, rows), F32)],
        scratch_shapes=[pltpu.VMEM((MLA_PACK, 1, t), F32), pltpu.VMEM((MLA_PACK, 1, t), F32),
                        pltpu.VMEM((MLA_PACK, SLOT, t), F32)],
        compiler_params=_cparams(),
    )(q, k, v)


def _mla_delta(o, do):
    rows = o.shape[0]
    t = rows
    wide = MLA_PACK * SLOT

    def body(o_ref, do_ref, d_ref):
        d_ref[...] = jnp.zeros_like(d_ref)
        ones = jnp.ones((8, SLOT), BF16)
        for hh in range(MLA_PACK):
            sl = slice(hh * SLOT, (hh + 1) * SLOT)
            prod = do_ref[:, sl].astype(F32) * o_ref[:, sl].astype(F32)
            high = prod.astype(BF16)
            low = (prod - high.astype(F32)).astype(BF16)
            d_ref[hh:hh + 1, :] = (_nt(ones, high) + _nt(ones, low))[0:1, :]

    spec = pl.BlockSpec((t, wide), lambda h, i: (i, h))
    return pl.pallas_call(
        body, name="mla_delta", grid=(MLA_HEADS // MLA_PACK, rows // t), in_specs=[spec, spec],
        out_specs=pl.BlockSpec((None, 8, t), lambda h, i: (h, 0, i)),
        out_shape=jax.ShapeDtypeStruct((MLA_HEADS // MLA_PACK, 8, rows), F32), compiler_params=_cparams(),
    )(o, do)


def _mla_attn_bwd(q, k, v, do, lse, delta, after):
    rows = q.shape[0]
    t = min(TQ_MLA, rows)
    nt = rows // t
    wide = MLA_PACK * SLOT

    def body(q_ref, k_ref, v_ref, do_ref, lse_ref, delta_ref, after_ref, dq_ref, dk_ref, dv_ref, dk_sc, dv_sc):
        j, i = pl.program_id(1), pl.program_id(2)

        @pl.when((j == 0) & (i == 0))
        def _():
            dq_ref[...] = jnp.zeros_like(dq_ref)

        @pl.when(i == 0)
        def _():
            dk_sc[...] = jnp.zeros_like(dk_sc)
            dv_sc[...] = jnp.zeros_like(dv_sc)

        def chunk(hh, rows, keys, masked):
            sl = slice(hh * SLOT, (hh + 1) * SLOT)
            n_rows = rows.stop - rows.start
            qv, kv, dov = q_ref[rows, sl], k_ref[keys, sl], do_ref[rows, sl]
            s = _nt(kv, qv)
            if masked:
                shp = (keys.stop - keys.start, n_rows)
                s = jnp.where(keys.start + lax.broadcasted_iota(jnp.int32, shp, 0)
                              <= rows.start + lax.broadcasted_iota(jnp.int32, shp, 1), s, NEG)
            p = jnp.exp2(s - lse_ref[hh:hh + 1, rows])
            dp = _nt(v_ref[keys, sl], dov)
            ds = (p * (dp - delta_ref[hh:hh + 1, rows])).astype(BF16)
            dv_sc[keys, sl] += _nn(p.astype(BF16), dov)
            dk_sc[keys, sl] += _nn(ds, qv)
            r0 = pl.multiple_of(i * t + rows.start, n_rows)
            dq_ref[pl.ds(r0, n_rows), sl] += _tn(ds, kv) * MLA_SCALE

        @pl.when(i > j)
        def _():
            for hh in range(MLA_PACK):
                chunk(hh, slice(0, t), slice(0, t), False)

        @pl.when(i == j)
        def _():
            for hh in range(MLA_PACK):
                chunk(hh, slice(0, t), slice(0, t // 2), True)
                chunk(hh, slice(t // 2, t), slice(t // 2, t), True)

        @pl.when(i == nt - 1)
        def _():
            dk_ref[...] = dk_sc[...] * (1.0 / LOG2_E)
            dv_ref[...] = dv_sc[...]

    q_spec = pl.BlockSpec((t, wide), lambda h, j, i: (jnp.maximum(i, j), h))
    kv_spec = pl.BlockSpec((t, wide), lambda h, j, i: (j, h))
    row_spec = pl.BlockSpec((None, 8, t), lambda h, j, i: (h, 0, jnp.maximum(i, j)))
    head_spec = pl.BlockSpec((rows, wide), lambda h, j, i: (0, h))
    shp = jax.ShapeDtypeStruct(q.shape, F32)
    return pl.pallas_call(
        body, name="mla_attn_bwd", grid=(MLA_HEADS // MLA_PACK, nt, nt),
        in_specs=[q_spec, kv_spec, kv_spec, q_spec, row_spec, row_spec, pl.BlockSpec(memory_space=pl.ANY)],
        out_specs=[head_spec, kv_spec, kv_spec], out_shape=[shp, shp, shp],
        scratch_shapes=[pltpu.VMEM((t, wide), F32), pltpu.VMEM((t, wide), F32)],
        compiler_params=_cparams(),
    )(q, k, v, do, lse, delta, after)


def _swa_specs(t):
    def prev(i):
        return jnp.maximum(i - 1, 0)
    kw = SWA_PACK * SLOT
    k0, v0 = SWA_HEADS // SWA_PACK, (SWA_HEADS + SWA_KV_HEADS) // SWA_PACK
    q3 = pl.BlockSpec((t, SWA_PACK * SWA_GROUP * SLOT), lambda h, i: (i, h))
    kp = pl.BlockSpec((t, kw), lambda h, i: (prev(i), k0 + h))
    kc = pl.BlockSpec((t, kw), lambda h, i: (i, k0 + h))
    vp = pl.BlockSpec((t, kw), lambda h, i: (prev(i), v0 + h))
    vc = pl.BlockSpec((t, kw), lambda h, i: (i, v0 + h))
    pcol = pl.BlockSpec((t, 1), lambda h, i: (i, 0))
    prow_p = pl.BlockSpec((1, t), lambda h, i: (0, prev(i)))
    prow_c = pl.BlockSpec((1, t), lambda h, i: (0, i))
    return [q3, kp, kc, vp, vc, pcol, prow_p, prow_c]


def _stack(ref, first):
    return jnp.concatenate([ref[:, (first + g) * SLOT:(first + g + 1) * SLOT] for g in range(SWA_GROUP)], axis=0)


def _swa_logits(q3, kp, kc, pq, pkp, pkc, slope_ref, kvh, i, t):
    r = lax.broadcasted_iota(jnp.int32, (t, t), 0)
    c = lax.broadcasted_iota(jnp.int32, (t, t), 1)
    ok_c = c <= r
    ok_p = (c - r) > jnp.where(i > 0, 0, t)
    dist_p, dist_c = pq - pkp, pq - pkc
    s_p3 = _nt(q3, kp) * (HEAD_DIM ** -0.5)
    s_c3 = _nt(q3, kc) * (HEAD_DIM ** -0.5)
    out = []
    for g in range(SWA_GROUP):
        slope = slope_ref[kvh * SWA_GROUP + g]
        rows = slice(g * t, (g + 1) * t)
        out.append((jnp.where(ok_p, s_p3[rows] - slope * dist_p, NEG),
                    jnp.where(ok_c, s_c3[rows] - slope * dist_c, NEG)))
    return out


def _swa_attn_fwd(proj, pos_col, pos_row, slopes, sinks):
    rows = proj.shape[0]
    t = WINDOW
    hw = SWA_HEADS * SLOT

    def body(slope_ref, sink_ref, q_ref, kp_ref, kc_ref, vp_ref, vc_ref, pq_ref, pkp_ref, pkc_ref, o_ref, lse_ref):
        i = pl.program_id(1)
        for kv in range(SWA_PACK):
            kvh = pl.program_id(0) * SWA_PACK + kv
            ksl = slice(kv * SLOT, (kv + 1) * SLOT)
            logits = _swa_logits(_stack(q_ref, kv * SWA_GROUP), kp_ref[:, ksl], kc_ref[:, ksl], pq_ref[...],
                                 pkp_ref[...], pkc_ref[...], slope_ref, kvh, i, t)
            e_p, e_c, norm = [], [], []
            for g, (s_p, s_c) in enumerate(logits):
                sl = slice((kv * SWA_GROUP + g) * SLOT, (kv * SWA_GROUP + g + 1) * SLOT)
                sink = sink_ref[kvh * SWA_GROUP + g]
                m = jnp.maximum(jnp.maximum(jnp.max(s_p, axis=1, keepdims=True),
                                            jnp.max(s_c, axis=1, keepdims=True)), sink)
                ep, ec = jnp.exp(s_p - m), jnp.exp(s_c - m)
                l = jnp.sum(ep, axis=1, keepdims=True) + jnp.sum(ec, axis=1, keepdims=True) + jnp.exp(sink - m)
                e_p.append(ep.astype(BF16))
                e_c.append(ec.astype(BF16))
                norm.append(l)
                lse_ref[:, sl] = jnp.broadcast_to(m + jnp.log(l), (t, SLOT))
            acc = (_nn(jnp.concatenate(e_p, axis=0), vp_ref[:, ksl])
                   + _nn(jnp.concatenate(e_c, axis=0), vc_ref[:, ksl]))
            for g in range(SWA_GROUP):
                sl = slice((kv * SWA_GROUP + g) * SLOT, (kv * SWA_GROUP + g + 1) * SLOT)
                o_ref[:, sl] = (acc[g * t:(g + 1) * t] / norm[g]).astype(o_ref.dtype)

    smem = pl.BlockSpec(memory_space=pltpu.SMEM)
    out_spec = pl.BlockSpec((t, SWA_PACK * SWA_GROUP * SLOT), lambda h, i: (i, h))
    return pl.pallas_call(
        body, name="swa_attn_fwd", grid=(SWA_KV_HEADS // SWA_PACK, rows // t),
        in_specs=[smem, smem] + _swa_specs(t), out_specs=[out_spec, out_spec],
        out_shape=[jax.ShapeDtypeStruct((rows, hw), BF16), jax.ShapeDtypeStruct((rows, hw), F32)],
        compiler_params=_cparams(),
    )(slopes, sinks, proj, proj, proj, proj, proj, pos_col, pos_row, pos_row)


def _swa_attn_bwd(proj, o, do, lse, pos_col, pos_row, slopes, sinks, after):
    rows = proj.shape[0]
    t = WINDOW
    hw = SWA_HEADS * SLOT
    scale = HEAD_DIM ** -0.5

    def body(slope_ref, sink_ref, q_ref, kp_ref, kc_ref, vp_ref, vc_ref, pq_ref, pkp_ref, pkc_ref,
             o_ref, do_ref, lse_ref, after_ref, dq_ref, dk_ref, dv_ref, dsink_ref):
        i = pl.program_id(1)

        @pl.when(i == 0)
        def _():
            dk_ref[...] = jnp.zeros_like(dk_ref)
            dv_ref[...] = jnp.zeros_like(dv_ref)
            dsink_ref[...] = jnp.zeros_like(dsink_ref)

        r_c = pl.multiple_of(i * t, t)
        r_p = pl.multiple_of(jnp.maximum(i - 1, 0) * t, t)
        for kv in range(SWA_PACK):
            kvh = pl.program_id(0) * SWA_PACK + kv
            ksl = slice(kv * SLOT, (kv + 1) * SLOT)
            q3, do3 = _stack(q_ref, kv * SWA_GROUP), _stack(do_ref, kv * SWA_GROUP)
            logits = _swa_logits(q3, kp_ref[:, ksl], kc_ref[:, ksl], pq_ref[...], pkp_ref[...], pkc_ref[...],
                                 slope_ref, kvh, i, t)
            dp_p3, dp_c3 = _nt(do3, vp_ref[:, ksl]), _nt(do3, vc_ref[:, ksl])
            p_p, p_c, ds_p, ds_c = [], [], [], []
            for g, (s_p, s_c) in enumerate(logits):
                head = kv * SWA_GROUP + g
                sl = slice(head * SLOT, (head + 1) * SLOT)
                rws = slice(g * t, (g + 1) * t)
                lse_g = lse_ref[:, head * SLOT:head * SLOT + 1]
                pp, pc = jnp.exp(s_p - lse_g), jnp.exp(s_c - lse_g)
                delta = jnp.sum(do_ref[:, sl].astype(F32) * o_ref[:, sl].astype(F32), axis=1, keepdims=True)
                p_p.append(pp.astype(BF16))
                p_c.append(pc.astype(BF16))
                ds_p.append((pp * (dp_p3[rws] - delta)).astype(BF16))
                ds_c.append((pc * (dp_c3[rws] - delta)).astype(BF16))
                sink = sink_ref[kvh * SWA_GROUP + g]
                dsink = -jnp.sum(jnp.exp(sink - lse_g) * delta, axis=0, keepdims=True)
                dsink_ref[head * 8:(head + 1) * 8, :] += jnp.broadcast_to(dsink, (8, SLOT))
            p_p3, p_c3 = jnp.concatenate(p_p, axis=0), jnp.concatenate(p_c, axis=0)
            ds_p3, ds_c3 = jnp.concatenate(ds_p, axis=0), jnp.concatenate(ds_c, axis=0)
            dq3 = (_nn(ds_p3, kp_ref[:, ksl]) + _nn(ds_c3, kc_ref[:, ksl])) * scale
            for g in range(SWA_GROUP):
                head = kv * SWA_GROUP + g
                dq_ref[:, head * SLOT:(head + 1) * SLOT] = dq3[g * t:(g + 1) * t]
            dk_ref[pl.ds(r_c, t), ksl] += _tn(ds_c3, q3) * scale
            dv_ref[pl.ds(r_c, t), ksl] += _tn(p_c3, do3)
            dk_ref[pl.ds(r_p, t), ksl] += _tn(ds_p3, q3) * scale
            dv_ref[pl.ds(r_p, t), ksl] += _tn(p_p3, do3)

    smem = pl.BlockSpec(memory_space=pltpu.SMEM)
    qlike = pl.BlockSpec((t, SWA_PACK * SWA_GROUP * SLOT), lambda h, i: (i, h))
    kv_out = pl.BlockSpec((rows, SWA_PACK * SLOT), lambda h, i: (0, h))
    return pl.pallas_call(
        body, name="swa_attn_bwd", grid=(SWA_KV_HEADS // SWA_PACK, rows // t),
        in_specs=[smem, smem] + _swa_specs(t) + [qlike, qlike, qlike, pl.BlockSpec(memory_space=pl.ANY)],
        out_specs=[qlike, kv_out, kv_out,
                   pl.BlockSpec((SWA_PACK * SWA_GROUP * 8, SLOT), lambda h, i: (h, 0))],
        out_shape=[jax.ShapeDtypeStruct((rows, hw), F32), jax.ShapeDtypeStruct((rows, SWA_KV_HEADS * SLOT), F32),
                   jax.ShapeDtypeStruct((rows, SWA_KV_HEADS * SLOT), F32),
                   jax.ShapeDtypeStruct((SWA_HEADS * 8, SLOT), F32)],
        compiler_params=_cparams(),
    )(slopes, sinks, proj, proj, proj, proj, proj, pos_col, pos_row, pos_row, o, do, lse, after)


def _cross_attn_fwd(proj, qoff, kvmem):
    rows = proj.shape[0]
    t = min(TQ_CROSS, rows)

    def body(q_ref, k_ref, v_ref, o_ref):
        s = _nt(k_ref[...], q_ref[...].astype(BF16)) * (HEAD_DIM ** -0.5)
        e = jnp.exp(s - jnp.max(s, axis=0, keepdims=True))
        p = e / jnp.sum(e, axis=0, keepdims=True)
        o_ref[...] = _tn(v_ref[...], p.astype(BF16)).T.astype(o_ref.dtype)

    return pl.pallas_call(
        body, name="cross_attn_fwd", grid=(rows // t, MEM_HEADS),
        in_specs=[pl.BlockSpec((t, SLOT), lambda i, h: (i, qoff + h)),
                  pl.BlockSpec((N_MEM, SLOT), lambda i, h: (0, h)),
                  pl.BlockSpec((N_MEM, SLOT), lambda i, h: (0, MEM_HEADS + h))],
        out_specs=pl.BlockSpec((t, SLOT), lambda i, h: (i, h)),
        out_shape=jax.ShapeDtypeStruct((rows, MEM_HEADS * SLOT), BF16), compiler_params=_cparams(),
    )(proj, kvmem, kvmem)


def _cross_attn_bwd(proj, qoff, kvmem, do):
    rows = proj.shape[0]
    t = min(TQ_CROSS, rows)
    scale = HEAD_DIM ** -0.5

    def body(q_ref, k_ref, v_ref, do_ref, dq_ref, dk_ref, dv_ref):
        @pl.when(pl.program_id(1) == 0)
        def _():
            dk_ref[...] = jnp.zeros_like(dk_ref)
            dv_ref[...] = jnp.zeros_like(dv_ref)

        qv, kv, dov = q_ref[...].astype(BF16), k_ref[...], do_ref[...]
        s = _nt(kv, qv) * scale
        e = jnp.exp(s - jnp.max(s, axis=0, keepdims=True))
        p = e / jnp.sum(e, axis=0, keepdims=True)
        dp = _nt(v_ref[...], dov)
        ds = (p * (dp - jnp.sum(p * dp, axis=0, keepdims=True))).astype(BF16)
        dq_ref[...] = _tn(ds, kv) * scale
        dk_ref[...] += _nn(ds, qv) * scale
        dv_ref[...] += _nn(p.astype(BF16), dov)

    mem_out = pl.BlockSpec((N_MEM, SLOT), lambda h, i: (0, h))
    return pl.pallas_call(
        body, name="cross_attn_bwd", grid=(MEM_HEADS, rows // t),
        in_specs=[pl.BlockSpec((t, SLOT), lambda h, i: (i, qoff + h)),
                  pl.BlockSpec((N_MEM, SLOT), lambda h, i: (0, h)),
                  pl.BlockSpec((N_MEM, SLOT), lambda h, i: (0, MEM_HEADS + h)),
                  pl.BlockSpec((t, SLOT), lambda h, i: (i, h))],
        out_specs=[pl.BlockSpec((t, SLOT), lambda h, i: (i, h)), mem_out, mem_out],
        out_shape=[jax.ShapeDtypeStruct((rows, MEM_HEADS * SLOT), F32),
                   jax.ShapeDtypeStruct((N_MEM, MEM_HEADS * SLOT), F32),
                   jax.ShapeDtypeStruct((N_MEM, MEM_HEADS * SLOT), F32)],
        compiler_params=_cparams(),
    )(proj, kvmem, kvmem, do)


def _place():
    return lax.axis_index("x"), lax.axis_index("y"), lax.axis_index("c")


def _flip(v, bit):
    return 1 - v if bit else v


def _all_gather(blocks, name):
    nb = len(blocks)

    def body(*refs):
        x_refs, out_refs = refs[:nb], refs[nb:2 * nb]
        send_sems, recv_sems, local_sems = refs[2 * nb:]
        x, y, c = _place()
        me, sibling = (x, y, c), (x, y, 1 - c)
        chips = [(1 - x, y), (x, 1 - y), (1 - x, 1 - y)]

        def copy(b, k, blk, to, from_input=False):
            slot = out_refs[b].at[4 * blk[0] + 2 * blk[1] + blk[2]]
            return pltpu.make_async_remote_copy(
                src_ref=x_refs[b] if from_input else slot, dst_ref=slot,
                send_sem=send_sems.at[b, k], recv_sem=recv_sems.at[b, k],
                device_id=to, device_id_type=pl.DeviceIdType.MESH)

        mine = [pltpu.make_async_copy(x_refs[b], out_refs[b].at[4 * x + 2 * y + c], local_sems.at[b])
                for b in range(nb)]
        for cp in mine:
            cp.start()
        first = []
        for b in range(nb):
            first.append(copy(b, 0, me, sibling, from_input=True))
            first += [copy(b, 1 + n, me, (*chip, c), from_input=True) for n, chip in enumerate(chips)]
        for cp in first:
            cp.start()
        passed = []
        for n, chip in enumerate(chips):
            for b in range(nb):
                copy(b, 1 + n, (*chip, c), me).wait_recv()
                passed.append(copy(b, 4 + n, (*chip, c), sibling))
                passed[-1].start()
        for b in range(nb):
            copy(b, 0, sibling, me).wait_recv()
            for n, chip in enumerate(chips):
                copy(b, 4 + n, (*chip, 1 - c), me).wait_recv()
        for cp in first + passed:
            cp.wait_send()
        for cp in mine:
            cp.wait()

    any_spec = pl.BlockSpec(memory_space=pl.ANY)
    return pl.pallas_call(
        body, name=name, in_specs=[any_spec] * nb, out_specs=[any_spec] * nb,
        out_shape=[jax.ShapeDtypeStruct((N_DEV,) + blk.shape, blk.dtype) for blk in blocks],
        scratch_shapes=[pltpu.SemaphoreType.DMA((nb, 7)), pltpu.SemaphoreType.DMA((nb, 7)),
                        pltpu.SemaphoreType.DMA((nb,))],
    )(*blocks)


def _peers(x, y, c):
    out = []
    for n in range(1, N_DEV):
        peer = (_flip(x, n & 4), _flip(y, n & 2), _flip(c, n & 1))
        out.append((n - 1, peer, 4 * peer[0] + 2 * peer[1] + peer[2]))
    return out


_HBM = pl.BlockSpec(memory_space=pltpu.HBM)
_SEM = pl.BlockSpec(memory_space=pltpu.SEMAPHORE)


def _exchange_start(srcs, scatter, name, after=None):
    ns = len(srcs)
    lands = [lax.empty(s.shape if scatter else (N_DEV,) + s.shape, s.dtype) for s in srcs]

    def body(*refs):
        src_refs, land_refs = refs[:ns], refs[ns:2 * ns]
        pos = 2 * ns + (1 if after is not None else 0)
        send_sems, recv_sems, token = refs[pos], refs[pos + 1], refs[-1]
        x, y, c = _place()
        my_idx = 4 * x + 2 * y + c
        for col, peer, peer_idx in _peers(x, y, c):
            for b in range(ns):
                pltpu.make_async_remote_copy(
                    src_ref=src_refs[b].at[peer_idx] if scatter else src_refs[b], dst_ref=land_refs[b].at[my_idx],
                    send_sem=send_sems.at[b * (N_DEV - 1) + col], recv_sem=recv_sems.at[b * (N_DEV - 1) + col],
                    device_id=peer, device_id_type=pl.DeviceIdType.MESH).start()
        token[...] = jnp.zeros_like(token)

    args = [pltpu.with_memory_space_constraint(a, pltpu.HBM) for a in list(srcs) + lands]
    in_specs = [_HBM] * (2 * ns)
    if after is not None:
        args.append(after)
        in_specs.append(pl.BlockSpec(memory_space=pl.ANY))
    out = pl.pallas_call(
        body, name=name, in_specs=in_specs,
        out_specs=[_SEM, _SEM] + [_HBM] * (2 * ns) + [pl.BlockSpec(memory_space=pltpu.VMEM)],
        out_shape=[pltpu.SemaphoreType.DMA((ns * (N_DEV - 1),)), pltpu.SemaphoreType.DMA((ns * (N_DEV - 1),))]
        + [pltpu.HBM(a.shape, a.dtype) for a in list(srcs) + lands] + [jax.ShapeDtypeStruct((8, SLOT), F32)],
        input_output_aliases={k: 2 + k for k in range(2 * ns)},
        compiler_params=pltpu.CompilerParams(has_side_effects=pltpu.SideEffectType.DATAFLOW_SIDE_EFFECTING),
    )(*args)
    return (out[0], out[1], out[2:2 + ns], out[2 + ns:2 + 2 * ns], scatter), out[-1]


def _exchange_wait(handle, after, name):
    send_sems, recv_sems, srcs, lands, scatter = handle
    ns = len(srcs)

    def body(*refs):
        src_refs, land_refs = refs[:ns], refs[ns:2 * ns]
        send_ref, recv_ref = refs[2 * ns], refs[2 * ns + 1]
        x, y, c = _place()
        for col, peer, peer_idx in _peers(x, y, c):
            for b in range(ns):
                copy = pltpu.make_async_remote_copy(
                    src_ref=src_refs[b].at[peer_idx] if scatter else src_refs[b], dst_ref=land_refs[b].at[peer_idx],
                    send_sem=send_ref.at[b * (N_DEV - 1) + col], recv_sem=recv_ref.at[b * (N_DEV - 1) + col],
                    device_id=peer, device_id_type=pl.DeviceIdType.MESH)
                copy.wait_send()
                copy.wait_recv()

    out = pl.pallas_call(
        body, name=name, in_specs=[_HBM] * (2 * ns) + [_SEM, _SEM, pl.BlockSpec(memory_space=pl.ANY)],
        out_specs=[_HBM] * (2 * ns),
        out_shape=[pltpu.HBM(a.shape, a.dtype) for a in list(srcs) + list(lands)],
        input_output_aliases={k: k for k in range(2 * ns)},
        compiler_params=pltpu.CompilerParams(has_side_effects=pltpu.SideEffectType.DATAFLOW_SIDE_EFFECTING),
    )(*srcs, *lands, send_sems, recv_sems, after)
    my_idx = 4 * lax.axis_index("x") + 2 * lax.axis_index("y") + lax.axis_index("c")
    landed = []
    for src, land in zip(out[:ns], out[ns:]):
        own = lax.dynamic_index_in_dim(src, my_idx, 0, keepdims=True) if scatter else src[None]
        landed.append(lax.dynamic_update_index_in_dim(land, own, my_idx, 0))
    return landed


def _adamw(parts, w, m, v, name):
    lyr, rows, cols = w.shape
    assert len(parts) == lyr
    tr = ADAM_ROWS if cols > 512 else 2 * ADAM_ROWS
    while rows % tr:
        tr //= 2
    tr = min(tr, rows)

    def body(*refs):
        p_refs = refs[:lyr]
        w_ref, m_ref, v_ref, g_out, d_out, m_out, v_out = refs[lyr:]
        for k in range(lyr):
            @pl.when(pl.program_id(0) == k)
            def _(p_ref=p_refs[k]):
                g = p_ref[0].astype(F32)
                for s in range(1, N_DEV):
                    g = g + p_ref[s].astype(F32)
                m2 = ADAM_B1 * m_ref[...] + (1.0 - ADAM_B1) * g
                v2 = ADAM_B2 * v_ref[...] + (1.0 - ADAM_B2) * (g * g)
                m_hat = m2 / (1.0 - ADAM_B1 ** ADAM_STEP)
                v_hat = v2 / (1.0 - ADAM_B2 ** ADAM_STEP)
                g_out[...] = g
                d_out[...] = -ADAM_LR * (m_hat / (jnp.sqrt(v_hat) + ADAM_EPS) + ADAM_WD * w_ref[...])
                m_out[...] = m2
                v_out[...] = v2

    def part_spec(k):
        return pl.BlockSpec((N_DEV, tr, cols), lambda l, i: (0, jnp.where(l == k, i, 0), 0))

    spec = pl.BlockSpec((None, tr, cols), lambda l, i: (l, i, 0))
    shp = jax.ShapeDtypeStruct((lyr, rows, cols), F32)
    return pl.pallas_call(
        body, name=name, grid=(lyr, rows // tr),
        in_specs=[part_spec(k) for k in range(lyr)] + [spec, spec, spec],
        out_specs=[spec] * 4, out_shape=[shp] * 4, compiler_params=_cparams(),
    )(*parts, w, m, v)


def _pack(arrays, lanes, row_mult, dtype):
    flat = jnp.concatenate([a.reshape(-1).astype(dtype) for a in arrays])
    unit = lanes * row_mult
    total = -(-flat.shape[0] // unit) * unit
    return jnp.pad(flat, (0, total - flat.shape[0])).reshape(total // lanes, lanes)


def _unpack(packed, shapes):
    flat = packed.reshape(-1)
    out, off = [], 0
    for shp in shapes:
        n = 1
        for d in shp:
            n *= d
        out.append(flat[off:off + n].reshape(shp))
        off += n
    return out


def _pad_slots(w, axis):
    axis = axis % w.ndim
    n = w.shape[axis] // HEAD_DIM
    shp = w.shape[:axis] + (n, HEAD_DIM) + w.shape[axis + 1:]
    pad = [(0, 0)] * (w.ndim + 1)
    pad[axis + 1] = (0, SLOT - HEAD_DIM)
    return jnp.pad(w.reshape(shp), pad).reshape(w.shape[:axis] + (n * SLOT,) + w.shape[axis + 1:])


def _unpad_slots(w, axis, keep=HEAD_DIM):
    axis = axis % w.ndim
    n = w.shape[axis] // SLOT
    shp = w.shape[:axis] + (n, SLOT) + w.shape[axis + 1:]
    idx = [slice(None)] * (w.ndim + 1)
    idx[axis + 1] = slice(0, keep)
    return w.reshape(shp)[tuple(idx)].reshape(w.shape[:axis] + (n * keep,) + w.shape[axis + 1:])


def _mla_in_pad(w):
    z = functools.partial(jnp.zeros, dtype=w.dtype)
    rows = w.shape[0]
    return jnp.concatenate([w[:, :384], z((rows, 64)), w[:, 640:672], z((rows, 32)), w[:, 384:640],
                            _pad_slots(w[:, 672:], 1)], axis=1)


def _mla_in_unpad(d):
    return jnp.concatenate([d[:, :384], d[:, 512:768], d[:, 448:480], _unpad_slots(d[:, 768:], 1)], axis=1)


def _mla_uq_pad(w):
    return jnp.pad(w.reshape(w.shape[0], MLA_HEADS, MLA_QK), ((0, 0), (0, 0), (0, SLOT - MLA_QK))).reshape(
        w.shape[0], MLA_HEADS * SLOT)


def _join(gathered, axis):
    nd, a, b = gathered.shape
    if axis == 1:
        return gathered.reshape(nd * a, b)
    return gathered.transpose(1, 0, 2).reshape(a, nd * b)


def _split(full, axis):
    r, c = full.shape
    if axis == 1:
        return full.reshape(N_DEV, r // N_DEV, c).astype(BF16)
    return full.reshape(r, N_DEV, c // N_DEV).transpose(1, 0, 2).astype(BF16)


def kernel(x, mem, positions, attn_norm_g, mlp_norm_g, mem_norm_g, final_norm_g, mla_w_in, mla_q_norm_g, mla_kv_norm_g, mla_w_uq, mla_w_ukv, swa_w_in, swa_sinks, w_mem_kv, w_o, mlp_w_up, mlp_w_down, loss_target, m_attn_norm_g, m_mlp_norm_g, m_mem_norm_g, m_final_norm_g, m_mla_w_in, m_mla_q_norm_g, m_mla_kv_norm_g, m_mla_w_uq, m_mla_w_ukv, m_swa_w_in, m_swa_sinks, m_w_mem_kv, m_w_o, m_mlp_w_up, m_mlp_w_down, v_attn_norm_g, v_mlp_norm_g, v_mem_norm_g, v_final_norm_g, v_mla_w_in, v_mla_q_norm_g, v_mla_kv_norm_g, v_mla_w_uq, v_mla_w_ukv, v_swa_w_in, v_swa_sinks, v_w_mem_kv, v_w_o, v_mlp_w_up, v_mlp_w_down):
    given = dict(locals())
    seq = x.shape[1]
    x0 = x.reshape(seq, D_MODEL)
    tgt = loss_target.reshape(seq, D_MODEL)
    mem0 = mem.reshape(N_MEM, D_MODEL)
    pos = positions.reshape(seq).astype(F32)
    pos_col, pos_row = pos.reshape(seq, 1), pos.reshape(1, seq)

    def layer_names(i):
        mixer = ("mla_w_in", "mla_w_uq", "mla_w_ukv") if i % 2 == 0 else ("swa_w_in",)
        return [(n, i // 2) for n in mixer] + [(n, i) for n in ("w_mem_kv", "w_o", "mlp_w_up", "mlp_w_down")]

    def local_weights(names):
        return [given[n][l].astype(BF16) for n, l in names]

    first_attn, first_mlp = layer_names(0)[:-2], layer_names(0)[-2:]
    weights = [dict(zip([n for n, _ in first_attn], _all_gather(local_weights(first_attn), "gather_weights_first")))]
    coming_mlp, first_token = _exchange_start(local_weights(first_mlp), False, "gather_weights_start_0",
                                              after=weights[0]["w_o"])

    consts = _lane_consts()
    tabs = _rope_tables(pos_col, consts)
    slopes = 2.0 ** (-8.0 * (jnp.arange(SWA_HEADS, dtype=F32) + 1.0) / SWA_HEADS)

    mem_n = _rmsnorm_fwd(mem0, 0, D_MODEL, mem_norm_g, "rmsnorm_fwd_mem")

    saved = []
    xc = x0
    for i in range(DEPTH):
        j = i // 2
        wts = weights[i]
        s = {"x_in": xc}
        token = None
        if i + 1 < DEPTH:
            coming, token = _exchange_start(local_weights(layer_names(i + 1)), False,
                                            "gather_weights_start_%d" % (i + 1),
                                            after=first_token if i == 0 else wts["w_o"])
        hn = _rmsnorm_fwd(xc, 0, D_MODEL, attn_norm_g[i], "rmsnorm_fwd", after=token)
        if i % 2 == 0:
            w_in = _mla_in_pad(_join(wts["mla_w_in"], 1))
            w_uq = _mla_uq_pad(_join(wts["mla_w_uq"], 2))
            w_kv = _join(wts["mla_w_ukv"], 2)
            proj = _mm(hn, w_in, "nn", F32, "mm_mla_in")
            cqn = _rmsnorm_fwd(proj, 0, MLA_Q_RANK, mla_q_norm_g[j], "rmsnorm_fwd_q")
            ckvn = _rmsnorm_fwd(proj, 2, MLA_KV_RANK, mla_kv_norm_g[j], "rmsnorm_fwd_kv")
            qraw = _mm(cqn, w_uq, "nn", F32, "mm_mla_uq")
            kvraw = _mm(ckvn, w_kv, "nn", F32, "mm_mla_ukv")
            q, k, v = _mla_rope_fwd(qraw, kvraw, proj, tabs)
            o, lse = _mla_attn_fwd(q, k, v)
            qoff = MLA_QOFF
            s.update(w_uq=w_uq, w_kv=w_kv, cqn=cqn, ckvn=ckvn, q=q, k=k, v=v)
        else:
            w_in = _join(wts["swa_w_in"], 2)
            proj = _mm(hn, w_in, "nn", BF16, "mm_swa_in", pairs="o")
            o, lse = _swa_attn_fwd(proj, pos_col, pos_row, slopes, swa_sinks[j])
            qoff = SWA_QOFF
        w_mem = _pad_slots(_join(wts["w_mem_kv"], 1), 1)
        w_out = _join(wts["w_o"], 1)
        w_o_mix, w_o_cross = w_out[:SWA_HEADS * HEAD_DIM], w_out[SWA_HEADS * HEAD_DIM:]
        kvmem = _mm(mem_n, w_mem, "nn", BF16, "mm_mem_kv")
        cross = _cross_attn_fwd(proj, qoff, kvmem)
        x1 = _mm(o, w_o_mix, "nn", F32, "mm_o_mix", res=xc, pairs="a")
        x1 = _mm(cross, w_o_cross, "nn", F32, "mm_o_cross", res=x1, pairs="a")
        hn2 = _rmsnorm_fwd(x1, 0, D_MODEL, mlp_norm_g[i], "rmsnorm_fwd")
        if i == 0:
            wts.update(zip([n for n, _ in first_mlp], _exchange_wait(coming_mlp, hn2, "gather_weights_wait_0")))
        act, act2 = _mm(hn2, wts["mlp_w_up"], "nn", BF16, "mm_mlp_up", epi="relu2", b_blk="cols")
        xc = _mm(act2, wts["mlp_w_down"], "nn", F32, "mm_mlp_down", res=x1, b_blk="rows")
        s.update(hn=hn, w_in=w_in, proj=proj, o=o, lse=lse, qoff=qoff, w_mem=w_mem, w_o_mix=w_o_mix,
                 w_o_cross=w_o_cross, kvmem=kvmem, cross=cross, x1=x1, hn2=hn2, act=act, act2=act2)
        saved.append(s)
        if i + 1 < DEPTH:
            got = _exchange_wait(coming, xc, "gather_weights_wait_%d" % (i + 1))
            weights.append(dict(zip([n for n, _ in layer_names(i + 1)], got)))

    dx, dx_b, dg_final, loss_part = _loss_head(xc, final_norm_g, tgt)
    loss = lax.psum(loss_part[0, 0], MESH_AXES)

    gains = {n: [None] * DEPTH for n in ("attn_norm_g", "mlp_norm_g")}
    for n in ("mla_q_norm_g", "mla_kv_norm_g", "swa_sinks"):
        gains[n] = [None] * 2
    leaving = {}
    token = None
    dmem_n = None
    for i in reversed(range(DEPTH)):
        j = i // 2
        s = saved[i]
        wts = weights[i]
        out = {}
        du = _mm(dx_b, wts["mlp_w_down"], "nt", BF16, "mm_mlp_down_dx", aux=s["act"], epi="mul2aux", b_blk="rows",
                 after=token)
        out["mlp_w_down"] = _mm(s["act2"], dx_b, "tn", BF16, "mm_mlp_down_dw", o_blk="rows")
        out["mlp_w_up"] = _mm(s["hn2"], du, "tn", BF16, "mm_mlp_up_dw", o_blk="cols")
        dx1, dx1_b, dg = _mm(du, wts["mlp_w_up"], "nt", F32, "mm_mlp_up_dx", b_blk="cols",
                             epi="normbwd", norm=(s["x1"], mlp_norm_g[i], dx))
        gains["mlp_norm_g"][i] = dg[0]

        do = _mm(dx1_b, s["w_o_mix"], "nt", BF16, "mm_o_mix_dx", pairs="o")
        dcross = _mm(dx1_b, s["w_o_cross"], "nt", BF16, "mm_o_cross_dx", pairs="o")
        dw_o = jnp.concatenate([_mm(s["o"], dx1_b, "tn", F32, "mm_o_mix_dw", pairs="a"),
                                _mm(s["cross"], dx1_b, "tn", F32, "mm_o_cross_dw", pairs="a")], axis=0)
        out["w_o"] = _split(dw_o, 1)
        dqc, dkm, dvm = _cross_attn_bwd(s["proj"], s["qoff"], s["kvmem"], dcross)
        dkvmem = jnp.concatenate([dkm, dvm], axis=1).astype(BF16)
        out["w_mem_kv"] = _split(_unpad_slots(_mm(mem_n, dkvmem, "tn", F32, "mm_mem_kv_dw"), 1), 1)
        dmem_n = _mm(dkvmem, s["w_mem"], "nt", F32, "mm_mem_kv_dx" if dmem_n is None else "mm_mem_kv_dx_acc",
                     res=dmem_n)
        leaving[(i, "main")], token = _exchange_start([out[n] for n, _ in layer_names(i)[-4:]], True,
                                                      "exchange_grads_main_start_%d" % i)

        if i % 2 == 0:
            dq, dk, dv = _mla_attn_bwd(s["q"], s["k"], s["v"], do, s["lse"], _mla_delta(s["o"], do), token)
            dqraw, dkv, dkr = _mla_rope_bwd(dq, dk, dv, tabs, consts)
            dcqn = _mm(dqraw, s["w_uq"], "nt", F32, "mm_mla_uq_dx")
            out["mla_w_uq"] = _split(_unpad_slots(_mm(s["cqn"], dqraw, "tn", F32, "mm_mla_uq_dw"), 1, MLA_QK), 2)
            dckvn = _mm(dkv, s["w_kv"], "nt", F32, "mm_mla_ukv_dx")
            out["mla_w_ukv"] = _split(_mm(s["ckvn"], dkv, "tn", F32, "mm_mla_ukv_dw"), 2)
            dcq, dg = _rmsnorm_bwd(s["proj"], 0, MLA_Q_RANK, mla_q_norm_g[j], dcqn, None, BF16, "rmsnorm_bwd_q")
            gains["mla_q_norm_g"][j] = dg[0]
            dckv, dg = _rmsnorm_bwd(s["proj"], 2, MLA_KV_RANK, mla_kv_norm_g[j], dckvn, None, BF16, "rmsnorm_bwd_kv")
            gains["mla_kv_norm_g"][j] = dg[0]
            dproj = jnp.concatenate([dcq, dkr.astype(BF16), dckv, dqc.astype(BF16)], axis=1)
            in_dx = "mm_mla_in_dx"
            out["mla_w_in"] = _split(_mla_in_unpad(_mm(s["hn"], dproj, "tn", F32, "mm_mla_in_dw")), 1)
        else:
            dq, dk, dv, dsink = _swa_attn_bwd(s["proj"], s["o"], do, s["lse"], pos_col, pos_row, slopes, swa_sinks[j],
                                              token)
            gains["swa_sinks"][j] = dsink[::8, 0]
            dproj = jnp.concatenate([dq, dk, dv, dqc], axis=1).astype(BF16)
            in_dx = "mm_swa_in_dx"
            out["swa_w_in"] = _split(_mm(s["hn"], dproj, "tn", F32, "mm_swa_in_dw", pairs="b"), 2)
        dx, dx_b, dg = _mm(dproj, s["w_in"], "nt", F32, in_dx, epi="normbwd", norm=(s["x_in"], attn_norm_g[i], dx1),
                           pairs="" if i % 2 == 0 else "a")
        gains["attn_norm_g"][i] = dg[0]

        leaving[(i, "mixer")], token = _exchange_start([out[n] for n, _ in layer_names(i)[:-4]], True,
                                                       "exchange_grads_mixer_start_%d" % i)

    _, dg_mem = _rmsnorm_bwd(mem0, 0, D_MODEL, mem_norm_g, dmem_n, None, BF16, "rmsnorm_bwd_mem")
    gains = {n: jnp.stack(g) for n, g in gains.items()}
    gains["mem_norm_g"] = dg_mem[0]
    gains["final_norm_g"] = dg_final[0]

    result = {}

    def adamw_of(names, received):
        for n in names:
            parts = [received[(n, l)] for l in range(given[n].shape[0])]
            for kind, r in enumerate(_adamw(parts, given[n], given["m_" + n], given["v_" + n], "adamw_" + n)):
                result[(kind, n)] = r

    received = {}
    for i in reversed(range(DEPTH)):
        got = _exchange_wait(leaving[(i, "main")], dx, "exchange_grads_main_wait_%d" % i)
        received.update(zip(layer_names(i)[-4:], got))
    adamw_of(("mlp_w_up", "mlp_w_down", "w_o", "w_mem_kv"), received)
    for i in reversed(range(DEPTH)):
        got = _exchange_wait(leaving[(i, "mixer")], result[(0, "w_mem_kv")], "exchange_grads_mixer_wait_%d" % i)
        received.update(zip(layer_names(i)[:-4], got))
    adamw_of(("mla_w_in", "mla_w_uq", "mla_w_ukv", "swa_w_in"), received)

    rep_shapes = [given[n].shape for n in REPLICATED]
    rep_parts = _all_gather([_pack([gains[n] for n in REPLICATED], SLOT, 8, F32)], "gather_gain_grads")[0]
    rep_packed = [_pack([given[p + n] for n in REPLICATED], SLOT, 8, F32)[None] for p in ("", "m_", "v_")]
    for kind, r in enumerate(_adamw([rep_parts], *rep_packed, "adamw_gains")):
        for n, part in zip(REPLICATED, _unpack(r[0], rep_shapes)):
            result[(kind, n)] = part

    outs = [loss, dx.reshape(1, seq, D_MODEL)]
    for kind in range(4):
        outs += [result[(kind, n)] for n in WEIGHT_ORDER]
    return tuple(outs)
```

```python
import functools

import jax
import jax.numpy as jnp
from jax import lax
from jax.experimental import pallas as pl
from jax.experimental.pallas import tpu as pltpu

F32 = jnp.float32
BF16 = jnp.bfloat16

D_MODEL = 1024
N_MEM = 256
DEPTH = 4
SLOT = 128
HEAD_DIM = 64
MLA_HEADS = 12
MLA_QK = 96
MLA_Q_RANK = 384
MLA_KV_RANK = 256
SWA_HEADS = 12
SWA_KV_HEADS = 4
SWA_GROUP = 3
MEM_HEADS = 4
WINDOW = 128
EPS = 1e-6
NEG = -1e30
ROPE_THETA = 10000.0
N_DEV = 8

ADAM_LR = 0.001
ADAM_B1 = 0.9
ADAM_B2 = 0.999
ADAM_EPS = 1e-08
ADAM_WD = 0.01
ADAM_STEP = 10

TM = 512
TQ_MLA = 1024
MLA_PACK = 2
SWA_PACK = 4
TQ_CROSS = 2048
MM_VMEM_BUDGET = 38 * 1024 * 1024
ADAM_ROWS = 128
VMEM_LIMIT = 56 * 1024 * 1024

MESH_AXES = ("x", "y", "c")

LOG2_E = 1.4426950408889634
MLA_SCALE = MLA_QK ** -0.5
MLA_Q_SCALE = MLA_SCALE * LOG2_E

MLA_QOFF = (MLA_Q_RANK + SLOT + MLA_KV_RANK) // SLOT
SWA_QOFF = SWA_HEADS + 2 * SWA_KV_HEADS

SHARDED = (
    ("mla_w_in", 1), ("mla_w_uq", 2), ("mla_w_ukv", 2), ("swa_w_in", 2),
    ("w_mem_kv", 1), ("w_o", 1), ("mlp_w_up", 2), ("mlp_w_down", 1),
)
REPLICATED = ("attn_norm_g", "mlp_norm_g", "mem_norm_g", "final_norm_g",
              "mla_q_norm_g", "mla_kv_norm_g", "swa_sinks")
WEIGHT_ORDER = ("attn_norm_g", "mlp_norm_g", "mem_norm_g", "final_norm_g", "mla_w_in",
                "mla_q_norm_g", "mla_kv_norm_g", "mla_w_uq", "mla_w_ukv", "swa_w_in",
                "swa_sinks", "w_mem_kv", "w_o", "mlp_w_up", "mlp_w_down")


def _cparams():
    return pltpu.CompilerParams(vmem_limit_bytes=VMEM_LIMIT)


_DIMS = {"nn": (((1,), (0,)), ((), ())), "nt": (((1,), (1,)), ((), ())), "tn": (((0,), (0,)), ((), ()))}


def _compact(x):
    pairs = [x[:, 2 * j * SLOT:(2 * j + 1) * SLOT] + pltpu.roll(x[:, (2 * j + 1) * SLOT:(2 * j + 2) * SLOT], HEAD_DIM, 1)
             for j in range(x.shape[1] // (2 * SLOT))]
    return pairs[0] if len(pairs) == 1 else jnp.concatenate(pairs, axis=1)


def _expand(x):
    low = lax.broadcasted_iota(jnp.int32, (x.shape[0], SLOT), 1) < HEAD_DIM
    slots = []
    for j in range(x.shape[1] // SLOT):
        pair = x[:, j * SLOT:(j + 1) * SLOT]
        slots += [jnp.where(low, pair, 0.0), pltpu.roll(jnp.where(low, 0.0, pair), HEAD_DIM, 1)]
    return jnp.concatenate(slots, axis=1)


def _mm_tiles(m, n, k, a_bytes, b_bytes, o_bytes, extra_bytes, tm_fixed, tn_fixed):
    best = None
    for tm in ([tm_fixed] if tm_fixed else [t for t in range(4096, 0, -SLOT) if m % t == 0] or [m]):
        for tn in ([tn_fixed] if tn_fixed else [t for t in range(1024, 0, -SLOT) if n % t == 0] or [n]):
            need = 2 * (tm * k * a_bytes + k * tn * b_bytes + tm * tn * (o_bytes + extra_bytes))
            need += tm * tn * 4
            if need <= MM_VMEM_BUDGET and (best is None or tm * tn > best[0] * best[1]):
                best = (tm, tn)
    assert best is not None, (m, n, k)
    return best


def _mm(a, b, mode, out_dtype, name, res=None, aux=None, epi=None, b_blk=None, o_blk=None, after=None, norm=None,
        pairs=""):
    if b_blk is not None:
        nb, br, bc = b.shape
        b_shape = (nb * br, bc) if b_blk == "rows" else (br, nb * bc)
    else:
        b_shape = b.shape
    assert not pairs or (b_blk is None and o_blk is None and not ("b" in pairs and mode == "nt"))
    a_shape = (a.shape[0], a.shape[1] // 2) if "a" in pairs else a.shape
    if "b" in pairs:
        b_shape = (b_shape[0], b_shape[1] // 2)
    if mode == "nn":
        (m, k), (k2, n) = a_shape, b_shape
    elif mode == "nt":
        (m, k), (n, k2) = a_shape, b_shape
    else:
        (k, m), (k2, n) = a_shape, b_shape
    assert k == k2, (a.shape, b_shape, mode)
    k_blocked = b_blk is not None and (b_blk == "rows") == (mode != "nt")
    tn_fixed = None
    if b_blk is not None and not k_blocked:
        tn_fixed = br if b_blk == "rows" else bc
    if o_blk == "cols":
        tn_fixed = n // N_DEV
    tm_fixed = m // N_DEV if o_blk == "rows" else None
    has_res, has_aux, has_norm = res is not None, aux is not None, epi == "normbwd"
    assert o_blk is None or not (has_res or has_aux or has_norm)
    n_out = 2 if epi == "relu2" else 1
    if has_norm:
        tn_fixed = n
        o_bytes, extra_bytes = 4 + 2, 4 + 4
    else:
        o_bytes = n_out * jnp.dtype(out_dtype).itemsize
        extra_bytes = (4 if has_res else 0) + (aux.dtype.itemsize if has_aux else 0)
    pa, pb, po = (2 if "a" in pairs else 1), (2 if "b" in pairs else 1), (2 if "o" in pairs else 1)
    tm, tn = _mm_tiles(m, n, k, a.dtype.itemsize * (3 if pa == 2 else 1), b.dtype.itemsize * (3 if pb == 2 else 1),
                       o_bytes * po, extra_bytes, tm_fixed, tn_fixed)
    dims = _DIMS[mode]
    if mode == "tn":
        a_spec = pl.BlockSpec((k, pa * tm), lambda i, j: (0, i))
    else:
        a_spec = pl.BlockSpec((tm, pa * k), lambda i, j: (i, 0))
    if b_blk is None:
        if mode == "nt":
            b_spec = pl.BlockSpec((tn, k), lambda i, j: (j, 0))
        else:
            b_spec = pl.BlockSpec((k, pb * tn), lambda i, j: (0, j))
    elif k_blocked and mode == "nt":
        b_spec = pl.BlockSpec((N_DEV, tn, bc), lambda i, j: (0, j, 0))
    elif k_blocked:
        b_spec = pl.BlockSpec((N_DEV, br, tn), lambda i, j: (0, 0, j))
    elif mode == "nt":
        b_spec = pl.BlockSpec((None, tn, k), lambda i, j: (j, 0, 0))
    else:
        b_spec = pl.BlockSpec((None, k, tn), lambda i, j: (j, 0, 0))
    if o_blk is None:
        o_spec = pl.BlockSpec((tm, po * tn), lambda i, j: (i, j))
        o_shape = (m, po * n)
    elif o_blk == "rows":
        o_spec = pl.BlockSpec((None, tm, tn), lambda i, j: (i, 0, j))
        o_shape = (N_DEV, tm, n)
    else:
        o_spec = pl.BlockSpec((None, tm, tn), lambda i, j: (j, i, 0))
        o_shape = (N_DEV, m, tn)

    def body(*refs):
        a_ref, b_ref = refs[0], refs[1]
        pos = 2
        res_ref = aux_ref = None
        if has_res:
            res_ref = refs[pos]
            pos += 1
        if has_aux:
            aux_ref = refs[pos]
            pos += 1
        if has_norm:
            x_ref, g_ref, dres_ref = refs[pos:pos + 3]
            pos += 3
        if after is not None:
            pos += 1
        outs = refs[pos:]
        if k_blocked and mode == "nt":
            r = None
            for d in range(N_DEV):
                part = lax.dot_general(a_ref[:, d * bc:(d + 1) * bc].astype(BF16), b_ref[d].astype(BF16), dims,
                                       preferred_element_type=F32)
                r = part if r is None else r + part
        else:
            bv = b_ref[...].reshape(k, tn) if k_blocked else b_ref[...]
            av = _compact(a_ref[...].astype(F32)) if pa == 2 else a_ref[...]
            bv = _compact(bv.astype(F32)) if pb == 2 else bv
            r = lax.dot_general(av.astype(BF16), bv.astype(BF16), dims, preferred_element_type=F32)
        if po == 2:
            r = _expand(r)
        if epi == "relu2":
            r = jnp.maximum(r, 0.0)
            outs[0][...] = r.astype(outs[0].dtype)
            outs[1][...] = (r * r).astype(outs[1].dtype)
        elif has_norm:
            xv = x_ref[...]
            rs = lax.rsqrt(jnp.mean(xv * xv, axis=1, keepdims=True) + EPS)
            xh = xv * rs
            dxh = r * g_ref[...]
            dx = rs * (dxh - xh * jnp.mean(dxh * xh, axis=1, keepdims=True)) + dres_ref[...]
            outs[0][...] = dx
            outs[1][...] = dx.astype(BF16)

            @pl.when(pl.program_id(0) == 0)
            def _():
                outs[2][...] = jnp.zeros_like(outs[2])

            outs[2][...] += jnp.sum(r * xh, axis=0, keepdims=True)
        else:
            if epi == "mul2aux":
                r = r * (2.0 * aux_ref[...].astype(F32))
            if has_res:
                r = r + res_ref[...]
            outs[0][...] = r.astype(outs[0].dtype)

    in_specs = [a_spec, b_spec]
    args = [a, b]
    if has_res:
        in_specs.append(o_spec)
        args.append(res)
    if has_aux:
        in_specs.append(o_spec)
        args.append(aux)
    vec_spec = pl.BlockSpec((1, n), lambda i, j: (0, 0))
    if has_norm:
        in_specs += [o_spec, vec_spec, o_spec]
        args += [norm[0], norm[1].reshape(1, n), norm[2]]
    if after is not None:
        in_specs.append(pl.BlockSpec(memory_space=pl.ANY))
        args.append(after)
    if has_norm:
        out_specs = [o_spec, o_spec, vec_spec]
        out_shape = [jax.ShapeDtypeStruct(o_shape, F32), jax.ShapeDtypeStruct(o_shape, BF16),
                     jax.ShapeDtypeStruct((1, n), F32)]
    else:
        out_specs = [o_spec] * n_out
        out_shape = [jax.ShapeDtypeStruct(o_shape, out_dtype)] * n_out
    out = pl.pallas_call(
        body, name=name, grid=(m // tm, n // tn),
        in_specs=in_specs, out_specs=out_specs, out_shape=out_shape, compiler_params=_cparams(),
    )(*args)
    return out if len(out) > 1 else out[0]


def _rmsnorm_fwd(xarr, colblk, width, g, name, after=None):
    rows = xarr.shape[0]
    tm = min(TM, rows)

    def body(x_ref, g_ref, *rest):
        y_ref = rest[-1]
        x = x_ref[...].astype(F32)
        r = lax.rsqrt(jnp.mean(x * x, axis=1, keepdims=True) + EPS)
        y_ref[...] = (x * r * g_ref[...]).astype(y_ref.dtype)

    in_specs = [pl.BlockSpec((tm, width), lambda i: (i, colblk)), pl.BlockSpec((1, width), lambda i: (0, 0))]
    args = [xarr, g.reshape(1, width)]
    if after is not None:
        in_specs.append(pl.BlockSpec(memory_space=pl.ANY))
        args.append(after)
    return pl.pallas_call(
        body, name=name, grid=(rows // tm,), in_specs=in_specs,
        out_specs=pl.BlockSpec((tm, width), lambda i: (i, 0)),
        out_shape=jax.ShapeDtypeStruct((rows, width), BF16), compiler_params=_cparams(),
    )(*args)


def _rmsnorm_bwd(xarr, colblk, width, g, dy, dres, out_dtype, name):
    rows = xarr.shape[0]
    tm = min(TM, rows)
    has_res = dres is not None

    def body(*refs):
        x_ref, g_ref, dy_ref = refs[0], refs[1], refs[2]
        dres_ref = refs[3] if has_res else None
        dx_ref, dg_ref = refs[-2], refs[-1]
        x = x_ref[...].astype(F32)
        dyv = dy_ref[...].astype(F32)
        r = lax.rsqrt(jnp.mean(x * x, axis=1, keepdims=True) + EPS)
        xh = x * r
        dxh = dyv * g_ref[...]
        dx = r * (dxh - xh * jnp.mean(dxh * xh, axis=1, keepdims=True))
        if has_res:
            dx = dx + dres_ref[...]
        dx_ref[...] = dx.astype(dx_ref.dtype)

        @pl.when(pl.program_id(0) == 0)
        def _():
            dg_ref[...] = jnp.zeros_like(dg_ref)

        dg_ref[...] += jnp.sum(dyv * xh, axis=0, keepdims=True)

    row_spec = pl.BlockSpec((tm, width), lambda i: (i, 0))
    vec_spec = pl.BlockSpec((1, width), lambda i: (0, 0))
    in_specs = [pl.BlockSpec((tm, width), lambda i: (i, colblk)), vec_spec, row_spec]
    args = [xarr, g.reshape(1, width), dy]
    if has_res:
        in_specs.append(row_spec)
        args.append(dres)
    return pl.pallas_call(
        body, name=name, grid=(rows // tm,), in_specs=in_specs, out_specs=[row_spec, vec_spec],
        out_shape=[jax.ShapeDtypeStruct((rows, width), out_dtype), jax.ShapeDtypeStruct((1, width), F32)],
        compiler_params=_cparams(),
    )(*args)


def _loss_head(x, g, tgt):
    rows, width = x.shape
    tm = min(TM, rows)

    def body(x_ref, g_ref, t_ref, dx_ref, dxb_ref, dg_ref, loss_ref):
        xv = x_ref[...]
        gv = g_ref[...]
        r = lax.rsqrt(jnp.mean(xv * xv, axis=1, keepdims=True) + EPS)
        xh = xv * r
        err = xh * gv - t_ref[...]
        part = 0.5 * jnp.sum(jnp.mean(err * err, axis=1, keepdims=True), axis=0, keepdims=True)
        dyv = err * (1.0 / width)
        dxh = dyv * gv
        dxv = r * (dxh - xh * jnp.mean(dxh * xh, axis=1, keepdims=True))
        dx_ref[...] = dxv
        dxb_ref[...] = dxv.astype(BF16)

        @pl.when(pl.program_id(0) == 0)
        def _():
            dg_ref[...] = jnp.zeros_like(dg_ref)
            loss_ref[...] = jnp.zeros_like(loss_ref)

        dg_ref[...] += jnp.sum(dyv * xh, axis=0, keepdims=True)
        loss_ref[...] += jnp.broadcast_to(part, loss_ref.shape)

    row_spec = pl.BlockSpec((tm, width), lambda i: (i, 0))
    vec_spec = pl.BlockSpec((1, width), lambda i: (0, 0))
    return pl.pallas_call(
        body, name="loss_head", grid=(rows // tm,), in_specs=[row_spec, vec_spec, row_spec],
        out_specs=[row_spec, row_spec, vec_spec, pl.BlockSpec((1, SLOT), lambda i: (0, 0))],
        out_shape=[jax.ShapeDtypeStruct((rows, width), F32), jax.ShapeDtypeStruct((rows, width), BF16),
                   jax.ShapeDtypeStruct((1, width), F32), jax.ShapeDtypeStruct((1, SLOT), F32)],
        compiler_params=_cparams(),
    )(x, g.reshape(1, width), tgt)


def _lane_consts():
    half = 16
    inv = ROPE_THETA ** (-(jnp.arange(half, dtype=F32) * 2.0) / 32)
    lane = jnp.arange(SLOT)
    first = (lane >= 64) & (lane < 80)
    second = (lane >= 80) & (lane < 96)
    inv_lane = jnp.where(first | second, inv[(lane - 64) % half], 0.0)
    rows = [inv_lane, (lane < 64).astype(F32), first.astype(F32), second.astype(F32)]
    rows += [jnp.zeros((SLOT,), F32)] * 4
    return jnp.stack(rows).astype(F32)


def _rope_tables(pos_col, consts):
    rows = pos_col.shape[0]
    tm = min(TM, rows)

    def body(p_ref, k_ref, c_ref, s1_ref, s2_ref):
        ang = p_ref[...] * k_ref[0:1, :]
        cos, sin = jnp.cos(ang), jnp.sin(ang)
        first, second = k_ref[2:3, :], k_ref[3:4, :]
        c_ref[...] = k_ref[1:2, :] + (first + second) * cos
        s1_ref[...] = -first * sin
        s2_ref[...] = second * sin

    spec = pl.BlockSpec((tm, SLOT), lambda i: (i, 0))
    shp = jax.ShapeDtypeStruct((rows, SLOT), F32)
    return pl.pallas_call(
        body, name="rope_tables", grid=(rows // tm,),
        in_specs=[pl.BlockSpec((tm, 1), lambda i: (i, 0)), pl.BlockSpec((8, SLOT), lambda i: (0, 0))],
        out_specs=[spec, spec, spec], out_shape=[shp, shp, shp], compiler_params=_cparams(),
    )(pos_col, consts)


def _rot(xv, c, s1, s2):
    return xv * c + pltpu.roll(xv, SLOT - 16, 1) * s1 + pltpu.roll(xv, 16, 1) * s2


def _rot_t(dy, c, s1, s2):
    return dy * c + pltpu.roll(dy * s1, 16, 1) + pltpu.roll(dy * s2, SLOT - 16, 1)


def _mla_rope_fwd(qraw, kvraw, proj, tabs):
    rows = qraw.shape[0]
    tm = min(256, rows)
    hw = MLA_HEADS * SLOT

    def body(q_ref, kv_ref, kr_ref, c_ref, s1_ref, s2_ref, qo, ko, vo):
        c, s1, s2 = c_ref[...], s1_ref[...], s2_ref[...]
        kr = _rot(kr_ref[...], c, s1, s2)
        low = lax.broadcasted_iota(jnp.int32, (tm, SLOT), 1) < HEAD_DIM
        for h in range(MLA_HEADS):
            sl = slice(h * SLOT, (h + 1) * SLOT)
            qo[:, sl] = (_rot(q_ref[:, sl], c, s1, s2) * MLA_Q_SCALE).astype(BF16)
            kvh = kv_ref[:, sl]
            ko[:, sl] = (jnp.where(low, kvh, 0.0) + kr).astype(BF16)
            vo[:, sl] = pltpu.roll(jnp.where(low, 0.0, kvh), HEAD_DIM, 1).astype(BF16)

    tab = pl.BlockSpec((tm, SLOT), lambda i: (i, 0))
    wide = pl.BlockSpec((tm, hw), lambda i: (i, 0))
    shp = jax.ShapeDtypeStruct((rows, hw), BF16)
    return pl.pallas_call(
        body, name="mla_rope_fwd", grid=(rows // tm,),
        in_specs=[wide, wide, pl.BlockSpec((tm, SLOT), lambda i: (i, 3)),
                  tab, tab, tab],
        out_specs=[wide, wide, wide], out_shape=[shp, shp, shp], compiler_params=_cparams(),
    )(qraw, kvraw, proj, *tabs)


def _mla_rope_bwd(dq, dk, dv, tabs, consts):
    rows = dq.shape[0]
    tm = min(256, rows)
    hw = MLA_HEADS * SLOT

    def body(dq_ref, dk_ref, dv_ref, c_ref, s1_ref, s2_ref, k_ref, dqo, dkvo, dkro):
        c, s1, s2 = c_ref[...], s1_ref[...], s2_ref[...]
        ksum = jnp.zeros((tm, SLOT), F32)
        low = lax.broadcasted_iota(jnp.int32, (tm, SLOT), 1) < HEAD_DIM
        for h in range(MLA_HEADS):
            sl = slice(h * SLOT, (h + 1) * SLOT)
            dqo[:, sl] = _rot_t(dq_ref[:, sl], c, s1, s2).astype(BF16)
            dkh = dk_ref[:, sl]
            ksum = ksum + dkh
            dvh = pltpu.roll(jnp.where(low, dv_ref[:, sl], 0.0), HEAD_DIM, 1)
            dkvo[:, sl] = (jnp.where(low, dkh, 0.0) + dvh).astype(BF16)
        dkro[...] = _rot_t(ksum, c, s1, s2) * (k_ref[2:3, :] + k_ref[3:4, :])

    tab = pl.BlockSpec((tm, SLOT), lambda i: (i, 0))
    wide = pl.BlockSpec((tm, hw), lambda i: (i, 0))
    return pl.pallas_call(
        body, name="mla_rope_bwd", grid=(rows // tm,),
        in_specs=[wide, wide, wide, tab, tab, tab, pl.BlockSpec((8, SLOT), lambda i: (0, 0))],
        out_specs=[wide, wide, tab],
        out_shape=[jax.ShapeDtypeStruct((rows, hw), BF16), jax.ShapeDtypeStruct((rows, hw), BF16),
                   jax.ShapeDtypeStruct((rows, SLOT), F32)],
        compiler_params=_cparams(),
    )(dq, dk, dv, *tabs, consts)


def _nt(a, b):
    return lax.dot_general(a, b, _DIMS["nt"], preferred_element_type=F32)


def _tn(a, b):
    return lax.dot_general(a, b, _DIMS["tn"], preferred_element_type=F32)


def _nn(a, b):
    return lax.dot_general(a, b, _DIMS["nn"], preferred_element_type=F32)


def _mla_attn_fwd(q, k, v):
    rows = q.shape[0]
    t = min(TQ_MLA, rows)
    nt = rows // t
    wide = MLA_PACK * SLOT

    def body(q_ref, k_ref, v_ref, o_ref, lse_ref, m_sc, l_sc, acc_sc):
        i, j = pl.program_id(1), pl.program_id(2)

        @pl.when(j == 0)
        def _():
            m_sc[...] = jnp.full_like(m_sc, NEG)
            l_sc[...] = jnp.zeros_like(l_sc)
            acc_sc[...] = jnp.zeros_like(acc_sc)

        def step(diagonal):
            for hh in range(MLA_PACK):
                sl = slice(hh * SLOT, (hh + 1) * SLOT)
                s = _nt(k_ref[:, sl], q_ref[:, sl])
                if diagonal:
                    key = lax.broadcasted_iota(jnp.int32, (t, t), 0)
                    s = jnp.where(key <= lax.broadcasted_iota(jnp.int32, (t, t), 1), s, NEG)
                m_prev = m_sc[hh]
                m_new = jnp.maximum(m_prev, jnp.max(s, axis=0, keepdims=True))
                p = jnp.exp2(s - m_new)
                alpha = jnp.exp2(m_prev - m_new)
                l_new = alpha * l_sc[hh] + jnp.sum(p, axis=0, keepdims=True)
                acc = alpha * acc_sc[hh] + _tn(v_ref[:, sl], p.astype(BF16))
                if diagonal:
                    o_ref[:, sl] = (acc / l_new).T.astype(o_ref.dtype)
                    lse_ref[hh:hh + 1, :] = m_new + jnp.log(l_new) * LOG2_E
                else:
                    m_sc[hh] = m_new
                    l_sc[hh] = l_new
                    acc_sc[hh] = acc

        @pl.when(j < i)
        def _():
            step(False)

        @pl.when(j == i)
        def _():
            lse_ref[...] = jnp.zeros_like(lse_ref)
            step(True)

    q_spec = pl.BlockSpec((t, wide), lambda h, i, j: (i, h))
    kv_spec = pl.BlockSpec((t, wide), lambda h, i, j: (jnp.minimum(j, i), h))
    return pl.pallas_call(
        body, name="mla_attn_fwd", grid=(MLA_HEADS // MLA_PACK, nt, nt),
        in_specs=[q_spec, kv_spec, kv_spec],
        out_specs=[q_spec, pl.BlockSpec((None, 8, t), lambda h, i, j: (h, 0, i))],
        out_shape=[jax.ShapeDtypeStruct(q.shape, BF16),
                   jax.ShapeDtypeStruct((MLA_HEADS // MLA_PACK, 8, rows), F32)],
        scratch_shapes=[pltpu.VMEM((MLA_PACK, 1, t), F32), pltpu.VMEM((MLA_PACK, 1, t), F32),
                        pltpu.VMEM((MLA_PACK, SLOT, t), F32)],
        compiler_params=_cparams(),
    )(q, k, v)


def _mla_delta(o, do):
    rows = o.shape[0]
    t = rows
    wide = MLA_PACK * SLOT

    def body(o_ref, do_ref, d_ref):
        d_ref[...] = jnp.zeros_like(d_ref)
        ones = jnp.ones((8, SLOT), BF16)
        for hh in range(MLA_PACK):
            sl = slice(hh * SLOT, (hh + 1) * SLOT)
            prod = do_ref[:, sl].astype(F32) * o_ref[:, sl].astype(F32)
            high = prod.astype(BF16)
            low = (prod - high.astype(F32)).astype(BF16)
            d_ref[hh:hh + 1, :] = (_nt(ones, high) + _nt(ones, low))[0:1, :]

    spec = pl.BlockSpec((t, wide), lambda h, i: (i, h))
    return pl.pallas_call(
        body, name="mla_delta", grid=(MLA_HEADS // MLA_PACK, rows // t), in_specs=[spec, spec],
        out_specs=pl.BlockSpec((None, 8, t), lambda h, i: (h, 0, i)),
        out_shape=jax.ShapeDtypeStruct((MLA_HEADS // MLA_PACK, 8, rows), F32), compiler_params=_cparams(),
    )(o, do)


def _mla_attn_bwd(q, k, v, do, lse, delta, after):
    rows = q.shape[0]
    t = min(TQ_MLA, rows)
    nt = rows // t
    wide = MLA_PACK * SLOT

    def body(q_ref, k_ref, v_ref, do_ref, lse_ref, delta_ref, after_ref, dq_ref, dk_ref, dv_ref, dk_sc, dv_sc):
        j, i = pl.program_id(1), pl.program_id(2)

        @pl.when((j == 0) & (i == 0))
        def _():
            dq_ref[...] = jnp.zeros_like(dq_ref)

        @pl.when(i == 0)
        def _():
            dk_sc[...] = jnp.zeros_like(dk_sc)
            dv_sc[...] = jnp.zeros_like(dv_sc)

        def chunk(hh, rows, keys, masked):
            sl = slice(hh * SLOT, (hh + 1) * SLOT)
            n_rows = rows.stop - rows.start
            qv, kv, dov = q_ref[rows, sl], k_ref[keys, sl], do_ref[rows, sl]
            s = _nt(kv, qv)
            if masked:
                shp = (keys.stop - keys.start, n_rows)
                s = jnp.where(keys.start + lax.broadcasted_iota(jnp.int32, shp, 0)
                              <= rows.start + lax.broadcasted_iota(jnp.int32, shp, 1), s, NEG)
            p = jnp.exp2(s - lse_ref[hh:hh + 1, rows])
            dp = _nt(v_ref[keys, sl], dov)
            ds = (p * (dp - delta_ref[hh:hh + 1, rows])).astype(BF16)
            dv_sc[keys, sl] += _nn(p.astype(BF16), dov)
            dk_sc[keys, sl] += _nn(ds, qv)
            r0 = pl.multiple_of(i * t + rows.start, n_rows)
            dq_ref[pl.ds(r0, n_rows), sl] += _tn(ds, kv) * MLA_SCALE

        @pl.when(i > j)
        def _():
            for hh in range(MLA_PACK):
                chunk(hh, slice(0, t), slice(0, t), False)

        @pl.when(i == j)
        def _():
            for hh in range(MLA_PACK):
                chunk(hh, slice(0, t), slice(0, t // 2), True)
                chunk(hh, slice(t // 2, t), slice(t // 2, t), True)

        @pl.when(i == nt - 1)
        def _():
            dk_ref[...] = dk_sc[...] * (1.0 / LOG2_E)
            dv_ref[...] = dv_sc[...]

    q_spec = pl.BlockSpec((t, wide), lambda h, j, i: (jnp.maximum(i, j), h))
    kv_spec = pl.BlockSpec((t, wide), lambda h, j, i: (j, h))
    row_spec = pl.BlockSpec((None, 8, t), lambda h, j, i: (h, 0, jnp.maximum(i, j)))
    head_spec = pl.BlockSpec((rows, wide), lambda h, j, i: (0, h))
    shp = jax.ShapeDtypeStruct(q.shape, F32)
    return pl.pallas_call(
        body, name="mla_attn_bwd", grid=(MLA_HEADS // MLA_PACK, nt, nt),
        in_specs=[q_spec, kv_spec, kv_spec, q_spec, row_spec, row_spec, pl.BlockSpec(memory_space=pl.ANY)],
        out_specs=[head_spec, kv_spec, kv_spec], out_shape=[shp, shp, shp],
        scratch_shapes=[pltpu.VMEM((t, wide), F32), pltpu.VMEM((t, wide), F32)],
        compiler_params=_cparams(),
    )(q, k, v, do, lse, delta, after)


def _swa_specs(t):
    def prev(i):
        return jnp.maximum(i - 1, 0)
    kw = SWA_PACK * SLOT
    k0, v0 = SWA_HEADS // SWA_PACK, (SWA_HEADS + SWA_KV_HEADS) // SWA_PACK
    q3 = pl.BlockSpec((t, SWA_PACK * SWA_GROUP * SLOT), lambda h, i: (i, h))
    kp = pl.BlockSpec((t, kw), lambda h, i: (prev(i), k0 + h))
    kc = pl.BlockSpec((t, kw), lambda h, i: (i, k0 + h))
    vp = pl.BlockSpec((t, kw), lambda h, i: (prev(i), v0 + h))
    vc = pl.BlockSpec((t, kw), lambda h, i: (i, v0 + h))
    pcol_c = pl.BlockSpec((t, 1), lambda h, i: (i, 0))
    pcol_p = pl.BlockSpec((t, 1), lambda h, i: (prev(i), 0))
    prow_q = pl.BlockSpec((1, t), lambda h, i: (0, i))
    return [q3, kp, kc, vp, vc, pcol_c, pcol_p, prow_q]


SWA_ROWS = -(-SWA_PACK * SWA_GROUP // 8) * 8


def _stack(ref, first):
    return jnp.concatenate([ref[:, (first + g) * SLOT:(first + g + 1) * SLOT] for g in range(SWA_GROUP)], axis=0)


def _swa_logits(q3, kp, kc, pk_c, pk_p, pq, slope_ref, kvh, i, t):
    key = lax.broadcasted_iota(jnp.int32, (t, t), 0)
    qry = lax.broadcasted_iota(jnp.int32, (t, t), 1)
    ok_c = key <= qry
    ok_p = (key - qry) > jnp.where(i > 0, 0, t)
    dist_p, dist_c = pq - pk_p, pq - pk_c
    s_p3 = _nt(kp, q3) * (HEAD_DIM ** -0.5)
    s_c3 = _nt(kc, q3) * (HEAD_DIM ** -0.5)
    out = []
    for g in range(SWA_GROUP):
        slope = slope_ref[kvh * SWA_GROUP + g]
        cols = slice(g * t, (g + 1) * t)
        out.append((jnp.where(ok_p, s_p3[:, cols] - slope * dist_p, NEG),
                    jnp.where(ok_c, s_c3[:, cols] - slope * dist_c, NEG)))
    return out


def _swa_attn_fwd(proj, pos_col, pos_row, slopes, sinks):
    rows = proj.shape[0]
    t = WINDOW
    hw = SWA_HEADS * SLOT

    def body(slope_ref, sink_ref, q_ref, kp_ref, kc_ref, vp_ref, vc_ref, pkc_ref, pkp_ref, pq_ref, o_ref, lse_ref):
        i = pl.program_id(1)
        lse_ref[...] = jnp.zeros_like(lse_ref)
        for kv in range(SWA_PACK):
            kvh = pl.program_id(0) * SWA_PACK + kv
            ksl = slice(kv * SLOT, (kv + 1) * SLOT)
            logits = _swa_logits(_stack(q_ref, kv * SWA_GROUP), kp_ref[:, ksl], kc_ref[:, ksl], pkc_ref[...],
                                 pkp_ref[...], pq_ref[...], slope_ref, kvh, i, t)
            e_p, e_c, norm = [], [], []
            for g, (s_p, s_c) in enumerate(logits):
                head = kv * SWA_GROUP + g
                sink = sink_ref[kvh * SWA_GROUP + g]
                m = jnp.maximum(jnp.maximum(jnp.max(s_p, axis=0, keepdims=True),
                                            jnp.max(s_c, axis=0, keepdims=True)), sink)
                ep, ec = jnp.exp(s_p - m), jnp.exp(s_c - m)
                l = jnp.sum(ep, axis=0, keepdims=True) + jnp.sum(ec, axis=0, keepdims=True) + jnp.exp(sink - m)
                e_p.append(ep.astype(BF16))
                e_c.append(ec.astype(BF16))
                norm.append(l)
                lse_ref[head:head + 1, :] = m + jnp.log(l)
            acc = (_tn(vp_ref[:, ksl], jnp.concatenate(e_p, axis=1))
                   + _tn(vc_ref[:, ksl], jnp.concatenate(e_c, axis=1)))
            for g in range(SWA_GROUP):
                sl = slice((kv * SWA_GROUP + g) * SLOT, (kv * SWA_GROUP + g + 1) * SLOT)
                o_ref[:, sl] = (acc[:, g * t:(g + 1) * t] / norm[g]).T.astype(o_ref.dtype)

    smem = pl.BlockSpec(memory_space=pltpu.SMEM)
    out_spec = pl.BlockSpec((t, SWA_PACK * SWA_GROUP * SLOT), lambda h, i: (i, h))
    return pl.pallas_call(
        body, name="swa_attn_fwd", grid=(SWA_KV_HEADS // SWA_PACK, rows // t),
        in_specs=[smem, smem] + _swa_specs(t),
        out_specs=[out_spec, pl.BlockSpec((None, SWA_ROWS, t), lambda h, i: (h, 0, i))],
        out_shape=[jax.ShapeDtypeStruct((rows, hw), BF16),
                   jax.ShapeDtypeStruct((SWA_KV_HEADS // SWA_PACK, SWA_ROWS, rows), F32)],
        compiler_params=_cparams(),
    )(slopes, sinks, proj, proj, proj, proj, proj, pos_col, pos_col, pos_row)


def _swa_attn_bwd(proj, o, do, lse, pos_col, pos_row, slopes, sinks, after):
    rows = proj.shape[0]
    t = WINDOW
    hw = SWA_HEADS * SLOT
    scale = HEAD_DIM ** -0.5

    def body(slope_ref, sink_ref, q_ref, kp_ref, kc_ref, vp_ref, vc_ref, pkc_ref, pkp_ref, pq_ref,
             o_ref, do_ref, lse_ref, after_ref, dq_ref, dk_ref, dv_ref, dsink_ref):
        i = pl.program_id(1)

        @pl.when(i == 0)
        def _():
            dk_ref[...] = jnp.zeros_like(dk_ref)
            dv_ref[...] = jnp.zeros_like(dv_ref)
            dsink_ref[...] = jnp.zeros_like(dsink_ref)

        r_c = pl.multiple_of(i * t, t)
        r_p = pl.multiple_of(jnp.maximum(i - 1, 0) * t, t)
        ones = jnp.ones((8, SLOT), BF16)
        for kv in range(SWA_PACK):
            kvh = pl.program_id(0) * SWA_PACK + kv
            ksl = slice(kv * SLOT, (kv + 1) * SLOT)
            q3, do3 = _stack(q_ref, kv * SWA_GROUP), _stack(do_ref, kv * SWA_GROUP)
            logits = _swa_logits(q3, kp_ref[:, ksl], kc_ref[:, ksl], pkc_ref[...], pkp_ref[...], pq_ref[...],
                                 slope_ref, kvh, i, t)
            dp_p3, dp_c3 = _nt(vp_ref[:, ksl], do3), _nt(vc_ref[:, ksl], do3)
            p_p, p_c, ds_p, ds_c = [], [], [], []
            for g, (s_p, s_c) in enumerate(logits):
                head = kv * SWA_GROUP + g
                sl = slice(head * SLOT, (head + 1) * SLOT)
                cols = slice(g * t, (g + 1) * t)
                lse_g = lse_ref[head:head + 1, :]
                pp, pc = jnp.exp(s_p - lse_g), jnp.exp(s_c - lse_g)
                prod = do_ref[:, sl].astype(F32) * o_ref[:, sl].astype(F32)
                high = prod.astype(BF16)
                low = (prod - high.astype(F32)).astype(BF16)
                delta = (_nt(ones, high) + _nt(ones, low))[0:1, :]
                p_p.append(pp.astype(BF16))
                p_c.append(pc.astype(BF16))
                ds_p.append((pp * (dp_p3[:, cols] - delta)).astype(BF16))
                ds_c.append((pc * (dp_c3[:, cols] - delta)).astype(BF16))
                sink = sink_ref[kvh * SWA_GROUP + g]
                dsink = -jnp.sum(jnp.exp(sink - lse_g) * delta, axis=1, keepdims=True)
                dsink_ref[head * 8:(head + 1) * 8, :] += jnp.broadcast_to(dsink, (8, SLOT))
            p_p3, p_c3 = jnp.concatenate(p_p, axis=1), jnp.concatenate(p_c, axis=1)
            ds_p3, ds_c3 = jnp.concatenate(ds_p, axis=1), jnp.concatenate(ds_c, axis=1)
            dq3 = (_tn(ds_p3, kp_ref[:, ksl]) + _tn(ds_c3, kc_ref[:, ksl])) * scale
            for g in range(SWA_GROUP):
                head = kv * SWA_GROUP + g
                dq_ref[:, head * SLOT:(head + 1) * SLOT] = dq3[g * t:(g + 1) * t]
            dk_ref[pl.ds(r_c, t), ksl] += _nn(ds_c3, q3) * scale
            dv_ref[pl.ds(r_c, t), ksl] += _nn(p_c3, do3)
            dk_ref[pl.ds(r_p, t), ksl] += _nn(ds_p3, q3) * scale
            dv_ref[pl.ds(r_p, t), ksl] += _nn(p_p3, do3)

    smem = pl.BlockSpec(memory_space=pltpu.SMEM)
    qlike = pl.BlockSpec((t, SWA_PACK * SWA_GROUP * SLOT), lambda h, i: (i, h))
    kv_out = pl.BlockSpec((rows, SWA_PACK * SLOT), lambda h, i: (0, h))
    return pl.pallas_call(
        body, name="swa_attn_bwd", grid=(SWA_KV_HEADS // SWA_PACK, rows // t),
        in_specs=[smem, smem] + _swa_specs(t) + [qlike, qlike, pl.BlockSpec((None, SWA_ROWS, t), lambda h, i: (h, 0, i)),
                                                 pl.BlockSpec(memory_space=pl.ANY)],
        out_specs=[qlike, kv_out, kv_out,
                   pl.BlockSpec((SWA_PACK * SWA_GROUP * 8, SLOT), lambda h, i: (h, 0))],
        out_shape=[jax.ShapeDtypeStruct((rows, hw), F32), jax.ShapeDtypeStruct((rows, SWA_KV_HEADS * SLOT), F32),
                   jax.ShapeDtypeStruct((rows, SWA_KV_HEADS * SLOT), F32),
                   jax.ShapeDtypeStruct((SWA_HEADS * 8, SLOT), F32)],
        compiler_params=_cparams(),
    )(slopes, sinks, proj, proj, proj, proj, proj, pos_col, pos_col, pos_row, o, do, lse, after)


def _cross_attn_fwd(proj, qoff, kvmem):
    rows = proj.shape[0]
    t = min(TQ_CROSS, rows)

    def body(q_ref, k_ref, v_ref, o_ref):
        s = _nt(k_ref[...], q_ref[...].astype(BF16)) * (HEAD_DIM ** -0.5)
        e = jnp.exp(s - jnp.max(s, axis=0, keepdims=True))
        p = e / jnp.sum(e, axis=0, keepdims=True)
        o_ref[...] = _tn(v_ref[...], p.astype(BF16)).T.astype(o_ref.dtype)

    return pl.pallas_call(
        body, name="cross_attn_fwd", grid=(rows // t, MEM_HEADS),
        in_specs=[pl.BlockSpec((t, SLOT), lambda i, h: (i, qoff + h)),
                  pl.BlockSpec((N_MEM, SLOT), lambda i, h: (0, h)),
                  pl.BlockSpec((N_MEM, SLOT), lambda i, h: (0, MEM_HEADS + h))],
        out_specs=pl.BlockSpec((t, SLOT), lambda i, h: (i, h)),
        out_shape=jax.ShapeDtypeStruct((rows, MEM_HEADS * SLOT), BF16), compiler_params=_cparams(),
    )(proj, kvmem, kvmem)


def _cross_attn_bwd(proj, qoff, kvmem, do):
    rows = proj.shape[0]
    t = min(TQ_CROSS, rows)
    scale = HEAD_DIM ** -0.5

    def body(q_ref, k_ref, v_ref, do_ref, dq_ref, dk_ref, dv_ref):
        @pl.when(pl.program_id(1) == 0)
        def _():
            dk_ref[...] = jnp.zeros_like(dk_ref)
            dv_ref[...] = jnp.zeros_like(dv_ref)

        qv, kv, dov = q_ref[...].astype(BF16), k_ref[...], do_ref[...]
        s = _nt(kv, qv) * scale
        e = jnp.exp(s - jnp.max(s, axis=0, keepdims=True))
        p = e / jnp.sum(e, axis=0, keepdims=True)
        dp = _nt(v_ref[...], dov)
        ds = (p * (dp - jnp.sum(p * dp, axis=0, keepdims=True))).astype(BF16)
        dq_ref[...] = _tn(ds, kv) * scale
        dk_ref[...] += _nn(ds, qv) * scale
        dv_ref[...] += _nn(p.astype(BF16), dov)

    mem_out = pl.BlockSpec((N_MEM, SLOT), lambda h, i: (0, h))
    return pl.pallas_call(
        body, name="cross_attn_bwd", grid=(MEM_HEADS, rows // t),
        in_specs=[pl.BlockSpec((t, SLOT), lambda h, i: (i, qoff + h)),
                  pl.BlockSpec((N_MEM, SLOT), lambda h, i: (0, h)),
                  pl.BlockSpec((N_MEM, SLOT), lambda h, i: (0, MEM_HEADS + h)),
                  pl.BlockSpec((t, SLOT), lambda h, i: (i, h))],
        out_specs=[pl.BlockSpec((t, SLOT), lambda h, i: (i, h)), mem_out, mem_out],
        out_shape=[jax.ShapeDtypeStruct((rows, MEM_HEADS * SLOT), F32),
                   jax.ShapeDtypeStruct((N_MEM, MEM_HEADS * SLOT), F32),
                   jax.ShapeDtypeStruct((N_MEM, MEM_HEADS * SLOT), F32)],
        compiler_params=_cparams(),
    )(proj, kvmem, kvmem, do)


def _place():
    return lax.axis_index("x"), lax.axis_index("y"), lax.axis_index("c")


def _flip(v, bit):
    return 1 - v if bit else v


def _all_gather(blocks, name):
    nb = len(blocks)

    def body(*refs):
        x_refs, out_refs = refs[:nb], refs[nb:2 * nb]
        send_sems, recv_sems, local_sems = refs[2 * nb:]
        x, y, c = _place()
        me, sibling = (x, y, c), (x, y, 1 - c)
        chips = [(1 - x, y), (x, 1 - y), (1 - x, 1 - y)]

        def copy(b, k, blk, to, from_input=False):
            slot = out_refs[b].at[4 * blk[0] + 2 * blk[1] + blk[2]]
            return pltpu.make_async_remote_copy(
                src_ref=x_refs[b] if from_input else slot, dst_ref=slot,
                send_sem=send_sems.at[b, k], recv_sem=recv_sems.at[b, k],
                device_id=to, device_id_type=pl.DeviceIdType.MESH)

        mine = [pltpu.make_async_copy(x_refs[b], out_refs[b].at[4 * x + 2 * y + c], local_sems.at[b])
                for b in range(nb)]
        for cp in mine:
            cp.start()
        first = []
        for b in range(nb):
            first.append(copy(b, 0, me, sibling, from_input=True))
            first += [copy(b, 1 + n, me, (*chip, c), from_input=True) for n, chip in enumerate(chips)]
        for cp in first:
            cp.start()
        passed = []
        for n, chip in enumerate(chips):
            for b in range(nb):
                copy(b, 1 + n, (*chip, c), me).wait_recv()
                passed.append(copy(b, 4 + n, (*chip, c), sibling))
                passed[-1].start()
        for b in range(nb):
            copy(b, 0, sibling, me).wait_recv()
            for n, chip in enumerate(chips):
                copy(b, 4 + n, (*chip, 1 - c), me).wait_recv()
        for cp in first + passed:
            cp.wait_send()
        for cp in mine:
            cp.wait()

    any_spec = pl.BlockSpec(memory_space=pl.ANY)
    return pl.pallas_call(
        body, name=name, in_specs=[any_spec] * nb, out_specs=[any_spec] * nb,
        out_shape=[jax.ShapeDtypeStruct((N_DEV,) + blk.shape, blk.dtype) for blk in blocks],
        scratch_shapes=[pltpu.SemaphoreType.DMA((nb, 7)), pltpu.SemaphoreType.DMA((nb, 7)),
                        pltpu.SemaphoreType.DMA((nb,))],
    )(*blocks)


def _peers(x, y, c):
    out = []
    for n in range(1, N_DEV):
        peer = (_flip(x, n & 4), _flip(y, n & 2), _flip(c, n & 1))
        out.append((n - 1, peer, 4 * peer[0] + 2 * peer[1] + peer[2]))
    return out


_HBM = pl.BlockSpec(memory_space=pltpu.HBM)
_SEM = pl.BlockSpec(memory_space=pltpu.SEMAPHORE)


def _exchange_start(srcs, scatter, name, after=None):
    ns = len(srcs)
    lands = [lax.empty(s.shape if scatter else (N_DEV,) + s.shape, s.dtype) for s in srcs]

    def body(*refs):
        src_refs, land_refs = refs[:ns], refs[ns:2 * ns]
        pos = 2 * ns + (1 if after is not None else 0)
        send_sems, recv_sems, token = refs[pos], refs[pos + 1], refs[-1]
        x, y, c = _place()
        my_idx = 4 * x + 2 * y + c
        for col, peer, peer_idx in _peers(x, y, c):
            for b in range(ns):
                pltpu.make_async_remote_copy(
                    src_ref=src_refs[b].at[peer_idx] if scatter else src_refs[b], dst_ref=land_refs[b].at[my_idx],
                    send_sem=send_sems.at[b * (N_DEV - 1) + col], recv_sem=recv_sems.at[b * (N_DEV - 1) + col],
                    device_id=peer, device_id_type=pl.DeviceIdType.MESH).start()
        token[...] = jnp.zeros_like(token)

    args = [pltpu.with_memory_space_constraint(a, pltpu.HBM) for a in list(srcs) + lands]
    in_specs = [_HBM] * (2 * ns)
    if after is not None:
        args.append(after)
        in_specs.append(pl.BlockSpec(memory_space=pl.ANY))
    out = pl.pallas_call(
        body, name=name, in_specs=in_specs,
        out_specs=[_SEM, _SEM] + [_HBM] * (2 * ns) + [pl.BlockSpec(memory_space=pltpu.VMEM)],
        out_shape=[pltpu.SemaphoreType.DMA((ns * (N_DEV - 1),)), pltpu.SemaphoreType.DMA((ns * (N_DEV - 1),))]
        + [pltpu.HBM(a.shape, a.dtype) for a in list(srcs) + lands] + [jax.ShapeDtypeStruct((8, SLOT), F32)],
        input_output_aliases={k: 2 + k for k in range(2 * ns)},
        compiler_params=pltpu.CompilerParams(has_side_effects=pltpu.SideEffectType.DATAFLOW_SIDE_EFFECTING),
    )(*args)
    return (out[0], out[1], out[2:2 + ns], out[2 + ns:2 + 2 * ns], scatter), out[-1]


def _exchange_wait(handle, after, name):
    send_sems, recv_sems, srcs, lands, scatter = handle
    ns = len(srcs)

    def body(*refs):
        src_refs, land_refs = refs[:ns], refs[ns:2 * ns]
        send_ref, recv_ref = refs[2 * ns], refs[2 * ns + 1]
        x, y, c = _place()
        for col, peer, peer_idx in _peers(x, y, c):
            for b in range(ns):
                copy = pltpu.make_async_remote_copy(
                    src_ref=src_refs[b].at[peer_idx] if scatter else src_refs[b], dst_ref=land_refs[b].at[peer_idx],
                    send_sem=send_ref.at[b * (N_DEV - 1) + col], recv_sem=recv_ref.at[b * (N_DEV - 1) + col],
                    device_id=peer, device_id_type=pl.DeviceIdType.MESH)
                copy.wait_send()
                copy.wait_recv()

    out = pl.pallas_call(
        body, name=name, in_specs=[_HBM] * (2 * ns) + [_SEM, _SEM, pl.BlockSpec(memory_space=pl.ANY)],
        out_specs=[_HBM] * (2 * ns),
        out_shape=[pltpu.HBM(a.shape, a.dtype) for a in list(srcs) + list(lands)],
        input_output_aliases={k: k for k in range(2 * ns)},
        compiler_params=pltpu.CompilerParams(has_side_effects=pltpu.SideEffectType.DATAFLOW_SIDE_EFFECTING),
    )(*srcs, *lands, send_sems, recv_sems, after)
    my_idx = 4 * lax.axis_index("x") + 2 * lax.axis_index("y") + lax.axis_index("c")
    landed = []
    for src, land in zip(out[:ns], out[ns:]):
        own = lax.dynamic_index_in_dim(src, my_idx, 0, keepdims=True) if scatter else src[None]
        landed.append(lax.dynamic_update_index_in_dim(land, own, my_idx, 0))
    return landed


def _adamw(parts, w, m, v, name):
    lyr, rows, cols = w.shape
    assert len(parts) == lyr
    tr = ADAM_ROWS if cols > 512 else 2 * ADAM_ROWS
    while rows % tr:
        tr //= 2
    tr = min(tr, rows)

    def body(*refs):
        p_refs = refs[:lyr]
        w_ref, m_ref, v_ref, g_out, d_out, m_out, v_out = refs[lyr:]
        for k in range(lyr):
            @pl.when(pl.program_id(0) == k)
            def _(p_ref=p_refs[k]):
                g = p_ref[0].astype(F32)
                for s in range(1, N_DEV):
                    g = g + p_ref[s].astype(F32)
                m2 = ADAM_B1 * m_ref[...] + (1.0 - ADAM_B1) * g
                v2 = ADAM_B2 * v_ref[...] + (1.0 - ADAM_B2) * (g * g)
                m_hat = m2 / (1.0 - ADAM_B1 ** ADAM_STEP)
                v_hat = v2 / (1.0 - ADAM_B2 ** ADAM_STEP)
                g_out[...] = g
                d_out[...] = -ADAM_LR * (m_hat / (jnp.sqrt(v_hat) + ADAM_EPS) + ADAM_WD * w_ref[...])
                m_out[...] = m2
                v_out[...] = v2

    def part_spec(k):
        return pl.BlockSpec((N_DEV, tr, cols), lambda l, i: (0, jnp.where(l == k, i, 0), 0))

    spec = pl.BlockSpec((None, tr, cols), lambda l, i: (l, i, 0))
    shp = jax.ShapeDtypeStruct((lyr, rows, cols), F32)
    return pl.pallas_call(
        body, name=name, grid=(lyr, rows // tr),
        in_specs=[part_spec(k) for k in range(lyr)] + [spec, spec, spec],
        out_specs=[spec] * 4, out_shape=[shp] * 4, compiler_params=_cparams(),
    )(*parts, w, m, v)


def _pack(arrays, lanes, row_mult, dtype):
    flat = jnp.concatenate([a.reshape(-1).astype(dtype) for a in arrays])
    unit = lanes * row_mult
    total = -(-flat.shape[0] // unit) * unit
    return jnp.pad(flat, (0, total - flat.shape[0])).reshape(total // lanes, lanes)


def _unpack(packed, shapes):
    flat = packed.reshape(-1)
    out, off = [], 0
    for shp in shapes:
        n = 1
        for d in shp:
            n *= d
        out.append(flat[off:off + n].reshape(shp))
        off += n
    return out


def _pad_slots(w, axis):
    axis = axis % w.ndim
    n = w.shape[axis] // HEAD_DIM
    shp = w.shape[:axis] + (n, HEAD_DIM) + w.shape[axis + 1:]
    pad = [(0, 0)] * (w.ndim + 1)
    pad[axis + 1] = (0, SLOT - HEAD_DIM)
    return jnp.pad(w.reshape(shp), pad).reshape(w.shape[:axis] + (n * SLOT,) + w.shape[axis + 1:])


def _unpad_slots(w, axis, keep=HEAD_DIM):
    axis = axis % w.ndim
    n = w.shape[axis] // SLOT
    shp = w.shape[:axis] + (n, SLOT) + w.shape[axis + 1:]
    idx = [slice(None)] * (w.ndim + 1)
    idx[axis + 1] = slice(0, keep)
    return w.reshape(shp)[tuple(idx)].reshape(w.shape[:axis] + (n * keep,) + w.shape[axis + 1:])


def _mla_in_pad(w):
    z = functools.partial(jnp.zeros, dtype=w.dtype)
    rows = w.shape[0]
    return jnp.concatenate([w[:, :384], z((rows, 64)), w[:, 640:672], z((rows, 32)), w[:, 384:640],
                            _pad_slots(w[:, 672:], 1)], axis=1)


def _mla_in_unpad(d):
    return jnp.concatenate([d[:, :384], d[:, 512:768], d[:, 448:480], _unpad_slots(d[:, 768:], 1)], axis=1)


def _mla_uq_pad(w):
    return jnp.pad(w.reshape(w.shape[0], MLA_HEADS, MLA_QK), ((0, 0), (0, 0), (0, SLOT - MLA_QK))).reshape(
        w.shape[0], MLA_HEADS * SLOT)


def _join(gathered, axis):
    nd, a, b = gathered.shape
    if axis == 1:
        return gathered.reshape(nd * a, b)
    return gathered.transpose(1, 0, 2).reshape(a, nd * b)


def _split(full, axis):
    r, c = full.shape
    if axis == 1:
        return full.reshape(N_DEV, r // N_DEV, c).astype(BF16)
    return full.reshape(r, N_DEV, c // N_DEV).transpose(1, 0, 2).astype(BF16)


def kernel(x, mem, positions, attn_norm_g, mlp_norm_g, mem_norm_g, final_norm_g, mla_w_in, mla_q_norm_g, mla_kv_norm_g, mla_w_uq, mla_w_ukv, swa_w_in, swa_sinks, w_mem_kv, w_o, mlp_w_up, mlp_w_down, loss_target, m_attn_norm_g, m_mlp_norm_g, m_mem_norm_g, m_final_norm_g, m_mla_w_in, m_mla_q_norm_g, m_mla_kv_norm_g, m_mla_w_uq, m_mla_w_ukv, m_swa_w_in, m_swa_sinks, m_w_mem_kv, m_w_o, m_mlp_w_up, m_mlp_w_down, v_attn_norm_g, v_mlp_norm_g, v_mem_norm_g, v_final_norm_g, v_mla_w_in, v_mla_q_norm_g, v_mla_kv_norm_g, v_mla_w_uq, v_mla_w_ukv, v_swa_w_in, v_swa_sinks, v_w_mem_kv, v_w_o, v_mlp_w_up, v_mlp_w_down):
    given = dict(locals())
    seq = x.shape[1]
    x0 = x.reshape(seq, D_MODEL)
    tgt = loss_target.reshape(seq, D_MODEL)
    mem0 = mem.reshape(N_MEM, D_MODEL)
    pos = positions.reshape(seq).astype(F32)
    pos_col, pos_row = pos.reshape(seq, 1), pos.reshape(1, seq)

    def layer_names(i):
        mixer = ("mla_w_in", "mla_w_uq", "mla_w_ukv") if i % 2 == 0 else ("swa_w_in",)
        return [(n, i // 2) for n in mixer] + [(n, i) for n in ("w_mem_kv", "w_o", "mlp_w_up", "mlp_w_down")]

    def local_weights(names):
        return [given[n][l].astype(BF16) for n, l in names]

    first_attn, first_mlp = layer_names(0)[:-2], layer_names(0)[-2:]
    weights = [dict(zip([n for n, _ in first_attn], _all_gather(local_weights(first_attn), "gather_weights_first")))]
    coming_mlp, first_token = _exchange_start(local_weights(first_mlp), False, "gather_weights_start_0",
                                              after=weights[0]["w_o"])

    consts = _lane_consts()
    tabs = _rope_tables(pos_col, consts)
    slopes = 2.0 ** (-8.0 * (jnp.arange(SWA_HEADS, dtype=F32) + 1.0) / SWA_HEADS)

    mem_n = _rmsnorm_fwd(mem0, 0, D_MODEL, mem_norm_g, "rmsnorm_fwd_mem")

    saved = []
    xc = x0
    for i in range(DEPTH):
        j = i // 2
        wts = weights[i]
        s = {"x_in": xc}
        token = None
        if i + 1 < DEPTH:
            coming, token = _exchange_start(local_weights(layer_names(i + 1)), False,
                                            "gather_weights_start_%d" % (i + 1),
                                            after=first_token if i == 0 else wts["w_o"])
        hn = _rmsnorm_fwd(xc, 0, D_MODEL, attn_norm_g[i], "rmsnorm_fwd", after=token)
        if i % 2 == 0:
            w_in = _mla_in_pad(_join(wts["mla_w_in"], 1))
            w_uq = _mla_uq_pad(_join(wts["mla_w_uq"], 2))
            w_kv = _join(wts["mla_w_ukv"], 2)
            proj = _mm(hn, w_in, "nn", F32, "mm_mla_in")
            cqn = _rmsnorm_fwd(proj, 0, MLA_Q_RANK, mla_q_norm_g[j], "rmsnorm_fwd_q")
            ckvn = _rmsnorm_fwd(proj, 2, MLA_KV_RANK, mla_kv_norm_g[j], "rmsnorm_fwd_kv")
            qraw = _mm(cqn, w_uq, "nn", F32, "mm_mla_uq")
            kvraw = _mm(ckvn, w_kv, "nn", F32, "mm_mla_ukv")
            q, k, v = _mla_rope_fwd(qraw, kvraw, proj, tabs)
            o, lse = _mla_attn_fwd(q, k, v)
            qoff = MLA_QOFF
            s.update(w_uq=w_uq, w_kv=w_kv, cqn=cqn, ckvn=ckvn, q=q, k=k, v=v)
        else:
            w_in = _join(wts["swa_w_in"], 2)
            proj = _mm(hn, w_in, "nn", BF16, "mm_swa_in", pairs="o")
            o, lse = _swa_attn_fwd(proj, pos_col, pos_row, slopes, swa_sinks[j])
            qoff = SWA_QOFF
        w_mem = _pad_slots(_join(wts["w_mem_kv"], 1), 1)
        w_out = _join(wts["w_o"], 1)
        w_o_mix, w_o_cross = w_out[:SWA_HEADS * HEAD_DIM], w_out[SWA_HEADS * HEAD_DIM:]
        kvmem = _mm(mem_n, w_mem, "nn", BF16, "mm_mem_kv")
        cross = _cross_attn_fwd(proj, qoff, kvmem)
        x1 = _mm(o, w_o_mix, "nn", F32, "mm_o_mix", res=xc, pairs="a")
        x1 = _mm(cross, w_o_cross, "nn", F32, "mm_o_cross", res=x1, pairs="a")
        hn2 = _rmsnorm_fwd(x1, 0, D_MODEL, mlp_norm_g[i], "rmsnorm_fwd")
        if i == 0:
            wts.update(zip([n for n, _ in first_mlp], _exchange_wait(coming_mlp, hn2, "gather_weights_wait_0")))
        act, act2 = _mm(hn2, wts["mlp_w_up"], "nn", BF16, "mm_mlp_up", epi="relu2", b_blk="cols")
        xc = _mm(act2, wts["mlp_w_down"], "nn", F32, "mm_mlp_down", res=x1, b_blk="rows")
        s.update(hn=hn, w_in=w_in, proj=proj, o=o, lse=lse, qoff=qoff, w_mem=w_mem, w_o_mix=w_o_mix,
                 w_o_cross=w_o_cross, kvmem=kvmem, cross=cross, x1=x1, hn2=hn2, act=act, act2=act2)
        saved.append(s)
        if i + 1 < DEPTH:
            got = _exchange_wait(coming, xc, "gather_weights_wait_%d" % (i + 1))
            weights.append(dict(zip([n for n, _ in layer_names(i + 1)], got)))

    dx, dx_b, dg_final, loss_part = _loss_head(xc, final_norm_g, tgt)
    loss = lax.psum(loss_part[0, 0], MESH_AXES)

    gains = {n: [None] * DEPTH for n in ("attn_norm_g", "mlp_norm_g")}
    for n in ("mla_q_norm_g", "mla_kv_norm_g", "swa_sinks"):
        gains[n] = [None] * 2
    leaving = {}
    token = None
    dmem_n = None
    for i in reversed(range(DEPTH)):
        j = i // 2
        s = saved[i]
        wts = weights[i]
        out = {}
        du = _mm(dx_b, wts["mlp_w_down"], "nt", BF16, "mm_mlp_down_dx", aux=s["act"], epi="mul2aux", b_blk="rows",
                 after=token)
        out["mlp_w_down"] = _mm(s["act2"], dx_b, "tn", BF16, "mm_mlp_down_dw", o_blk="rows")
        out["mlp_w_up"] = _mm(s["hn2"], du, "tn", BF16, "mm_mlp_up_dw", o_blk="cols")
        dx1, dx1_b, dg = _mm(du, wts["mlp_w_up"], "nt", F32, "mm_mlp_up_dx", b_blk="cols",
                             epi="normbwd", norm=(s["x1"], mlp_norm_g[i], dx))
        gains["mlp_norm_g"][i] = dg[0]

        do = _mm(dx1_b, s["w_o_mix"], "nt", BF16, "mm_o_mix_dx", pairs="o")
        dcross = _mm(dx1_b, s["w_o_cross"], "nt", BF16, "mm_o_cross_dx", pairs="o")
        dw_o = jnp.concatenate([_mm(s["o"], dx1_b, "tn", F32, "mm_o_mix_dw", pairs="a"),
                                _mm(s["cross"], dx1_b, "tn", F32, "mm_o_cross_dw", pairs="a")], axis=0)
        out["w_o"] = _split(dw_o, 1)
        dqc, dkm, dvm = _cross_attn_bwd(s["proj"], s["qoff"], s["kvmem"], dcross)
        dkvmem = jnp.concatenate([dkm, dvm], axis=1).astype(BF16)
        out["w_mem_kv"] = _split(_unpad_slots(_mm(mem_n, dkvmem, "tn", F32, "mm_mem_kv_dw"), 1), 1)
        dmem_n = _mm(dkvmem, s["w_mem"], "nt", F32, "mm_mem_kv_dx" if dmem_n is None else "mm_mem_kv_dx_acc",
                     res=dmem_n)
        leaving[(i, "main")], token = _exchange_start([out[n] for n, _ in layer_names(i)[-4:]], True,
                                                      "exchange_grads_main_start_%d" % i)

        if i % 2 == 0:
            dq, dk, dv = _mla_attn_bwd(s["q"], s["k"], s["v"], do, s["lse"], _mla_delta(s["o"], do), token)
            dqraw, dkv, dkr = _mla_rope_bwd(dq, dk, dv, tabs, consts)
            dcqn = _mm(dqraw, s["w_uq"], "nt", F32, "mm_mla_uq_dx")
            out["mla_w_uq"] = _split(_unpad_slots(_mm(s["cqn"], dqraw, "tn", F32, "mm_mla_uq_dw"), 1, MLA_QK), 2)
            dckvn = _mm(dkv, s["w_kv"], "nt", F32, "mm_mla_ukv_dx")
            out["mla_w_ukv"] = _split(_mm(s["ckvn"], dkv, "tn", F32, "mm_mla_ukv_dw"), 2)
            dcq, dg = _rmsnorm_bwd(s["proj"], 0, MLA_Q_RANK, mla_q_norm_g[j], dcqn, None, BF16, "rmsnorm_bwd_q")
            gains["mla_q_norm_g"][j] = dg[0]
            dckv, dg = _rmsnorm_bwd(s["proj"], 2, MLA_KV_RANK, mla_kv_norm_g[j], dckvn, None, BF16, "rmsnorm_bwd_kv")
            gains["mla_kv_norm_g"][j] = dg[0]
            dproj = jnp.concatenate([dcq, dkr.astype(BF16), dckv, dqc.astype(BF16)], axis=1)
            in_dx = "mm_mla_in_dx"
            out["mla_w_in"] = _split(_mla_in_unpad(_mm(s["hn"], dproj, "tn", F32, "mm_mla_in_dw")), 1)
        else:
            dq, dk, dv, dsink = _swa_attn_bwd(s["proj"], s["o"], do, s["lse"], pos_col, pos_row, slopes, swa_sinks[j],
                                              token)
            gains["swa_sinks"][j] = dsink[::8, 0]
            dproj = jnp.concatenate([dq, dk, dv, dqc], axis=1).astype(BF16)
            in_dx = "mm_swa_in_dx"
            out["swa_w_in"] = _split(_mm(s["hn"], dproj, "tn", F32, "mm_swa_in_dw", pairs="b"), 2)
        dx, dx_b, dg = _mm(dproj, s["w_in"], "nt", F32, in_dx, epi="normbwd", norm=(s["x_in"], attn_norm_g[i], dx1),
                           pairs="" if i % 2 == 0 else "a")
        gains["attn_norm_g"][i] = dg[0]

        leaving[(i, "mixer")], token = _exchange_start([out[n] for n, _ in layer_names(i)[:-4]], True,
                                                       "exchange_grads_mixer_start_%d" % i)

    _, dg_mem = _rmsnorm_bwd(mem0, 0, D_MODEL, mem_norm_g, dmem_n, None, BF16, "rmsnorm_bwd_mem")
    gains = {n: jnp.stack(g) for n, g in gains.items()}
    gains["mem_norm_g"] = dg_mem[0]
    gains["final_norm_g"] = dg_final[0]

    result = {}

    def adamw_of(names, received):
        for n in names:
            parts = [received[(n, l)] for l in range(given[n].shape[0])]
            for kind, r in enumerate(_adamw(parts, given[n], given["m_" + n], given["v_" + n], "adamw_" + n)):
                result[(kind, n)] = r

    received = {}
    for i in reversed(range(DEPTH)):
        got = _exchange_wait(leaving[(i, "main")], dx, "exchange_grads_main_wait_%d" % i)
        received.update(zip(layer_names(i)[-4:], got))
    adamw_of(("mlp_w_up", "mlp_w_down", "w_o", "w_mem_kv"), received)
    for i in reversed(range(DEPTH)):
        got = _exchange_wait(leaving[(i, "mixer")], result[(0, "w_mem_kv")], "exchange_grads_mixer_wait_%d" % i)
        received.update(zip(layer_names(i)[:-4], got))
    adamw_of(("mla_w_in", "mla_w_uq", "mla_w_ukv", "swa_w_in"), received)

    rep_shapes = [given[n].shape for n in REPLICATED]
    rep_parts = _all_gather([_pack([gains[n] for n in REPLICATED], SLOT, 8, F32)], "gather_gain_grads")[0]
    rep_packed = [_pack([given[p + n] for n in REPLICATED], SLOT, 8, F32)[None] for p in ("", "m_", "v_")]
    for kind, r in enumerate(_adamw([rep_parts], *rep_packed, "adamw_gains")):
        for n, part in zip(REPLICATED, _unpack(r[0], rep_shapes)):
            result[(kind, n)] = part

    outs = [loss, dx.reshape(1, seq, D_MODEL)]
    for kind in range(4):
        outs += [result[(kind, n)] for n in WEIGHT_ORDER]
    return tuple(outs)
```

```python
import functools

import jax
import jax.numpy as jnp
from jax import lax
from jax.experimental import pallas as pl
from jax.experimental.pallas import tpu as pltpu

F32 = jnp.float32
BF16 = jnp.bfloat16

D_MODEL = 1024
N_MEM = 256
DEPTH = 4
SLOT = 128
HEAD_DIM = 64
MLA_HEADS = 12
MLA_QK = 96
MLA_Q_RANK = 384
MLA_KV_RANK = 256
SWA_HEADS = 12
SWA_KV_HEADS = 4
SWA_GROUP = 3
MEM_HEADS = 4
WINDOW = 128
EPS = 1e-6
NEG = -1e30
ROPE_THETA = 10000.0
N_DEV = 8

ADAM_LR = 0.001
ADAM_B1 = 0.9
ADAM_B2 = 0.999
ADAM_EPS = 1e-08
ADAM_WD = 0.01
ADAM_STEP = 10

TM = 1024
TM_ROPE = 512
TQ_MLA = 1024
MLA_PACK = 4
SWA_PACK = 4
TQ_CROSS = 2048
MM_VMEM_BUDGET = 38 * 1024 * 1024
ADAM_ROWS = 128
VMEM_LIMIT = 56 * 1024 * 1024

MESH_AXES = ("x", "y", "c")

LOG2_E = 1.4426950408889634
MLA_SCALE = MLA_QK ** -0.5
MLA_Q_SCALE = MLA_SCALE * LOG2_E

MLA_QOFF = (MLA_Q_RANK + SLOT + MLA_KV_RANK) // SLOT
SWA_QOFF = SWA_HEADS + 2 * SWA_KV_HEADS

SHARDED = (
    ("mla_w_in", 1), ("mla_w_uq", 2), ("mla_w_ukv", 2), ("swa_w_in", 2),
    ("w_mem_kv", 1), ("w_o", 1), ("mlp_w_up", 2), ("mlp_w_down", 1),
)
REPLICATED = ("attn_norm_g", "mlp_norm_g", "mem_norm_g", "final_norm_g",
              "mla_q_norm_g", "mla_kv_norm_g", "swa_sinks")
WEIGHT_ORDER = ("attn_norm_g", "mlp_norm_g", "mem_norm_g", "final_norm_g", "mla_w_in",
                "mla_q_norm_g", "mla_kv_norm_g", "mla_w_uq", "mla_w_ukv", "swa_w_in",
                "swa_sinks", "w_mem_kv", "w_o", "mlp_w_up", "mlp_w_down")


def _cparams():
    return pltpu.CompilerParams(vmem_limit_bytes=VMEM_LIMIT)


_DIMS = {"nn": (((1,), (0,)), ((), ())), "nt": (((1,), (1,)), ((), ())), "tn": (((0,), (0,)), ((), ()))}


def _compact(x):
    pairs = [x[:, 2 * j * SLOT:(2 * j + 1) * SLOT] + pltpu.roll(x[:, (2 * j + 1) * SLOT:(2 * j + 2) * SLOT], HEAD_DIM, 1)
             for j in range(x.shape[1] // (2 * SLOT))]
    return pairs[0] if len(pairs) == 1 else jnp.concatenate(pairs, axis=1)


def _expand(x):
    low = lax.broadcasted_iota(jnp.int32, (x.shape[0], SLOT), 1) < HEAD_DIM
    slots = []
    for j in range(x.shape[1] // SLOT):
        pair = x[:, j * SLOT:(j + 1) * SLOT]
        slots += [jnp.where(low, pair, 0.0), pltpu.roll(jnp.where(low, 0.0, pair), HEAD_DIM, 1)]
    return jnp.concatenate(slots, axis=1)


def _mm_tiles(m, n, k, a_bytes, b_bytes, o_bytes, extra_bytes, tm_fixed, tn_fixed):
    best = None
    for tm in ([tm_fixed] if tm_fixed else [t for t in range(4096, 0, -SLOT) if m % t == 0] or [m]):
        for tn in ([tn_fixed] if tn_fixed else [t for t in range(1024, 0, -SLOT) if n % t == 0] or [n]):
            need = 2 * (tm * k * a_bytes + k * tn * b_bytes + tm * tn * (o_bytes + extra_bytes))
            need += tm * tn * 4
            if need <= MM_VMEM_BUDGET and (best is None or tm * tn > best[0] * best[1]):
                best = (tm, tn)
    assert best is not None, (m, n, k)
    return best


def _mm(a, b, mode, out_dtype, name, res=None, aux=None, epi=None, b_blk=None, o_blk=None, after=None, norm=None,
        pairs=""):
    if b_blk is not None:
        nb, br, bc = b.shape
        b_shape = (nb * br, bc) if b_blk == "rows" else (br, nb * bc)
    else:
        b_shape = b.shape
    assert not pairs or (b_blk is None and o_blk is None and not ("b" in pairs and mode == "nt"))
    a_shape = (a.shape[0], a.shape[1] // 2) if "a" in pairs else a.shape
    if "b" in pairs:
        b_shape = (b_shape[0], b_shape[1] // 2)
    if mode == "nn":
        (m, k), (k2, n) = a_shape, b_shape
    elif mode == "nt":
        (m, k), (n, k2) = a_shape, b_shape
    else:
        (k, m), (k2, n) = a_shape, b_shape
    assert k == k2, (a.shape, b_shape, mode)
    k_blocked = b_blk is not None and (b_blk == "rows") == (mode != "nt")
    tn_fixed = None
    if b_blk is not None and not k_blocked:
        tn_fixed = br if b_blk == "rows" else bc
    if o_blk == "cols":
        tn_fixed = n // N_DEV
    tm_fixed = m // N_DEV if o_blk == "rows" else None
    has_res, has_aux, has_norm = res is not None, aux is not None, epi == "normbwd"
    assert o_blk is None or not (has_res or has_aux or has_norm)
    n_out = 2 if epi == "relu2" else 1
    if has_norm:
        tn_fixed = n
        o_bytes, extra_bytes = 4 + 2, 4 + 4
    else:
        o_bytes = n_out * jnp.dtype(out_dtype).itemsize
        extra_bytes = (4 if has_res else 0) + (aux.dtype.itemsize if has_aux else 0)
    pa, pb, po = (2 if "a" in pairs else 1), (2 if "b" in pairs else 1), (2 if "o" in pairs else 1)
    tm, tn = _mm_tiles(m, n, k, a.dtype.itemsize * (3 if pa == 2 else 1), b.dtype.itemsize * (3 if pb == 2 else 1),
                       o_bytes * po, extra_bytes, tm_fixed, tn_fixed)
    dims = _DIMS[mode]
    if mode == "tn":
        a_spec = pl.BlockSpec((k, pa * tm), lambda i, j: (0, i))
    else:
        a_spec = pl.BlockSpec((tm, pa * k), lambda i, j: (i, 0))
    if b_blk is None:
        if mode == "nt":
            b_spec = pl.BlockSpec((tn, k), lambda i, j: (j, 0))
        else:
            b_spec = pl.BlockSpec((k, pb * tn), lambda i, j: (0, j))
    elif k_blocked and mode == "nt":
        b_spec = pl.BlockSpec((N_DEV, tn, bc), lambda i, j: (0, j, 0))
    elif k_blocked:
        b_spec = pl.BlockSpec((N_DEV, br, tn), lambda i, j: (0, 0, j))
    elif mode == "nt":
        b_spec = pl.BlockSpec((None, tn, k), lambda i, j: (j, 0, 0))
    else:
        b_spec = pl.BlockSpec((None, k, tn), lambda i, j: (j, 0, 0))
    if o_blk is None:
        o_spec = pl.BlockSpec((tm, po * tn), lambda i, j: (i, j))
        o_shape = (m, po * n)
    elif o_blk == "rows":
        o_spec = pl.BlockSpec((None, tm, tn), lambda i, j: (i, 0, j))
        o_shape = (N_DEV, tm, n)
    else:
        o_spec = pl.BlockSpec((None, tm, tn), lambda i, j: (j, i, 0))
        o_shape = (N_DEV, m, tn)

    def body(*refs):
        a_ref, b_ref = refs[0], refs[1]
        pos = 2
        res_ref = aux_ref = None
        if has_res:
            res_ref = refs[pos]
            pos += 1
        if has_aux:
            aux_ref = refs[pos]
            pos += 1
        if has_norm:
            x_ref, g_ref, dres_ref = refs[pos:pos + 3]
            pos += 3
        if after is not None:
            pos += 1
        outs = refs[pos:]
        if k_blocked and mode == "nt":
            r = None
            for d in range(N_DEV):
                part = lax.dot_general(a_ref[:, d * bc:(d + 1) * bc].astype(BF16), b_ref[d].astype(BF16), dims,
                                       preferred_element_type=F32)
                r = part if r is None else r + part
        else:
            bv = b_ref[...].reshape(k, tn) if k_blocked else b_ref[...]
            av = _compact(a_ref[...].astype(F32)) if pa == 2 else a_ref[...]
            bv = _compact(bv.astype(F32)) if pb == 2 else bv
            r = lax.dot_general(av.astype(BF16), bv.astype(BF16), dims, preferred_element_type=F32)
        if po == 2:
            r = _expand(r)
        if epi == "relu2":
            r = jnp.maximum(r, 0.0)
            outs[0][...] = r.astype(outs[0].dtype)
            outs[1][...] = (r * r).astype(outs[1].dtype)
        elif has_norm:
            xv = x_ref[...]
            rs = lax.rsqrt(jnp.mean(xv * xv, axis=1, keepdims=True) + EPS)
            xh = xv * rs
            dxh = r * g_ref[...]
            dx = rs * (dxh - xh * jnp.mean(dxh * xh, axis=1, keepdims=True)) + dres_ref[...]
            outs[0][...] = dx
            outs[1][...] = dx.astype(BF16)

            @pl.when(pl.program_id(0) == 0)
            def _():
                outs[2][...] = jnp.zeros_like(outs[2])

            outs[2][...] += jnp.sum(r * xh, axis=0, keepdims=True)
        else:
            if epi == "mul2aux":
                r = r * (2.0 * aux_ref[...].astype(F32))
            if has_res:
                r = r + res_ref[...]
            outs[0][...] = r.astype(outs[0].dtype)

    in_specs = [a_spec, b_spec]
    args = [a, b]
    if has_res:
        in_specs.append(o_spec)
        args.append(res)
    if has_aux:
        in_specs.append(o_spec)
        args.append(aux)
    vec_spec = pl.BlockSpec((1, n), lambda i, j: (0, 0))
    if has_norm:
        in_specs += [o_spec, vec_spec, o_spec]
        args += [norm[0], norm[1].reshape(1, n), norm[2]]
    if after is not None:
        in_specs.append(pl.BlockSpec(memory_space=pl.ANY))
        args.append(after)
    if has_norm:
        out_specs = [o_spec, o_spec, vec_spec]
        out_shape = [jax.ShapeDtypeStruct(o_shape, F32), jax.ShapeDtypeStruct(o_shape, BF16),
                     jax.ShapeDtypeStruct((1, n), F32)]
    else:
        out_specs = [o_spec] * n_out
        out_shape = [jax.ShapeDtypeStruct(o_shape, out_dtype)] * n_out
    out = pl.pallas_call(
        body, name=name, grid=(m // tm, n // tn),
        in_specs=in_specs, out_specs=out_specs, out_shape=out_shape, compiler_params=_cparams(),
    )(*args)
    return out if len(out) > 1 else out[0]


def _rmsnorm_fwd(xarr, colblk, width, g, name, after=None):
    rows = xarr.shape[0]
    tm = min(TM, rows)

    def body(x_ref, g_ref, *rest):
        y_ref = rest[-1]
        x = x_ref[...].astype(F32)
        r = lax.rsqrt(jnp.mean(x * x, axis=1, keepdims=True) + EPS)
        y_ref[...] = (x * r * g_ref[...]).astype(y_ref.dtype)

    in_specs = [pl.BlockSpec((tm, width), lambda i: (i, colblk)), pl.BlockSpec((1, width), lambda i: (0, 0))]
    args = [xarr, g.reshape(1, width)]
    if after is not None:
        in_specs.append(pl.BlockSpec(memory_space=pl.ANY))
        args.append(after)
    return pl.pallas_call(
        body, name=name, grid=(rows // tm,), in_specs=in_specs,
        out_specs=pl.BlockSpec((tm, width), lambda i: (i, 0)),
        out_shape=jax.ShapeDtypeStruct((rows, width), BF16), compiler_params=_cparams(),
    )(*args)


def _rmsnorm_bwd(xarr, colblk, width, g, dy, dres, out_dtype, name):
    rows = xarr.shape[0]
    tm = min(TM, rows)
    has_res = dres is not None

    def body(*refs):
        x_ref, g_ref, dy_ref = refs[0], refs[1], refs[2]
        dres_ref = refs[3] if has_res else None
        dx_ref, dg_ref = refs[-2], refs[-1]
        x = x_ref[...].astype(F32)
        dyv = dy_ref[...].astype(F32)
        r = lax.rsqrt(jnp.mean(x * x, axis=1, keepdims=True) + EPS)
        xh = x * r
        dxh = dyv * g_ref[...]
        dx = r * (dxh - xh * jnp.mean(dxh * xh, axis=1, keepdims=True))
        if has_res:
            dx = dx + dres_ref[...]
        dx_ref[...] = dx.astype(dx_ref.dtype)

        @pl.when(pl.program_id(0) == 0)
        def _():
            dg_ref[...] = jnp.zeros_like(dg_ref)

        dg_ref[...] += jnp.sum(dyv * xh, axis=0, keepdims=True)

    row_spec = pl.BlockSpec((tm, width), lambda i: (i, 0))
    vec_spec = pl.BlockSpec((1, width), lambda i: (0, 0))
    in_specs = [pl.BlockSpec((tm, width), lambda i: (i, colblk)), vec_spec, row_spec]
    args = [xarr, g.reshape(1, width), dy]
    if has_res:
        in_specs.append(row_spec)
        args.append(dres)
    return pl.pallas_call(
        body, name=name, grid=(rows // tm,), in_specs=in_specs, out_specs=[row_spec, vec_spec],
        out_shape=[jax.ShapeDtypeStruct((rows, width), out_dtype), jax.ShapeDtypeStruct((1, width), F32)],
        compiler_params=_cparams(),
    )(*args)


def _loss_head(x, g, tgt):
    rows, width = x.shape
    tm = min(TM, rows)

    def body(x_ref, g_ref, t_ref, dx_ref, dxb_ref, dg_ref, loss_ref):
        xv = x_ref[...]
        gv = g_ref[...]
        r = lax.rsqrt(jnp.mean(xv * xv, axis=1, keepdims=True) + EPS)
        xh = xv * r
        err = xh * gv - t_ref[...]
        part = 0.5 * jnp.sum(jnp.mean(err * err, axis=1, keepdims=True), axis=0, keepdims=True)
        dyv = err * (1.0 / width)
        dxh = dyv * gv
        dxv = r * (dxh - xh * jnp.mean(dxh * xh, axis=1, keepdims=True))
        dx_ref[...] = dxv
        dxb_ref[...] = dxv.astype(BF16)

        @pl.when(pl.program_id(0) == 0)
        def _():
            dg_ref[...] = jnp.zeros_like(dg_ref)
            loss_ref[...] = jnp.zeros_like(loss_ref)

        dg_ref[...] += jnp.sum(dyv * xh, axis=0, keepdims=True)
        loss_ref[...] += jnp.broadcast_to(part, loss_ref.shape)

    row_spec = pl.BlockSpec((tm, width), lambda i: (i, 0))
    vec_spec = pl.BlockSpec((1, width), lambda i: (0, 0))
    return pl.pallas_call(
        body, name="loss_head", grid=(rows // tm,), in_specs=[row_spec, vec_spec, row_spec],
        out_specs=[row_spec, row_spec, vec_spec, pl.BlockSpec((1, SLOT), lambda i: (0, 0))],
        out_shape=[jax.ShapeDtypeStruct((rows, width), F32), jax.ShapeDtypeStruct((rows, width), BF16),
                   jax.ShapeDtypeStruct((1, width), F32), jax.ShapeDtypeStruct((1, SLOT), F32)],
        compiler_params=_cparams(),
    )(x, g.reshape(1, width), tgt)


def _lane_consts():
    half = 16
    inv = ROPE_THETA ** (-(jnp.arange(half, dtype=F32) * 2.0) / 32)
    lane = jnp.arange(SLOT)
    first = (lane >= 64) & (lane < 80)
    second = (lane >= 80) & (lane < 96)
    inv_lane = jnp.where(first | second, inv[(lane - 64) % half], 0.0)
    rows = [inv_lane, (lane < 64).astype(F32), first.astype(F32), second.astype(F32)]
    rows += [jnp.zeros((SLOT,), F32)] * 4
    return jnp.stack(rows).astype(F32)


def _rope_tables(pos_col, consts):
    rows = pos_col.shape[0]
    tm = min(TM, rows)

    def body(p_ref, k_ref, c_ref, s1_ref, s2_ref):
        ang = p_ref[...] * k_ref[0:1, :]
        cos, sin = jnp.cos(ang), jnp.sin(ang)
        first, second = k_ref[2:3, :], k_ref[3:4, :]
        c_ref[...] = k_ref[1:2, :] + (first + second) * cos
        s1_ref[...] = -first * sin
        s2_ref[...] = second * sin

    spec = pl.BlockSpec((tm, SLOT), lambda i: (i, 0))
    shp = jax.ShapeDtypeStruct((rows, SLOT), F32)
    return pl.pallas_call(
        body, name="rope_tables", grid=(rows // tm,),
        in_specs=[pl.BlockSpec((tm, 1), lambda i: (i, 0)), pl.BlockSpec((8, SLOT), lambda i: (0, 0))],
        out_specs=[spec, spec, spec], out_shape=[shp, shp, shp], compiler_params=_cparams(),
    )(pos_col, consts)


def _rot(xv, c, s1, s2):
    return xv * c + pltpu.roll(xv, SLOT - 16, 1) * s1 + pltpu.roll(xv, 16, 1) * s2


def _rot_t(dy, c, s1, s2):
    return dy * c + pltpu.roll(dy * s1, 16, 1) + pltpu.roll(dy * s2, SLOT - 16, 1)


def _mla_rope_fwd(qraw, kvraw, proj, tabs):
    rows = qraw.shape[0]
    tm = min(TM_ROPE, rows)
    hw = MLA_HEADS * SLOT

    def body(q_ref, kv_ref, kr_ref, c_ref, s1_ref, s2_ref, qo, ko, vo):
        c, s1, s2 = c_ref[...], s1_ref[...], s2_ref[...]
        kr = _rot(kr_ref[...], c, s1, s2)
        low = lax.broadcasted_iota(jnp.int32, (tm, SLOT), 1) < HEAD_DIM
        for h in range(MLA_HEADS):
            sl = slice(h * SLOT, (h + 1) * SLOT)
            qo[:, sl] = (_rot(q_ref[:, sl], c, s1, s2) * MLA_Q_SCALE).astype(BF16)
            kvh = kv_ref[:, sl]
            ko[:, sl] = (jnp.where(low, kvh, 0.0) + kr).astype(BF16)
            vo[:, sl] = pltpu.roll(jnp.where(low, 0.0, kvh), HEAD_DIM, 1).astype(BF16)

    tab = pl.BlockSpec((tm, SLOT), lambda i: (i, 0))
    wide = pl.BlockSpec((tm, hw), lambda i: (i, 0))
    shp = jax.ShapeDtypeStruct((rows, hw), BF16)
    return pl.pallas_call(
        body, name="mla_rope_fwd", grid=(rows // tm,),
        in_specs=[wide, wide, pl.BlockSpec((tm, SLOT), lambda i: (i, 3)),
                  tab, tab, tab],
        out_specs=[wide, wide, wide], out_shape=[shp, shp, shp], compiler_params=_cparams(),
    )(qraw, kvraw, proj, *tabs)


def _mla_rope_bwd(dq, dk, dv, tabs, consts):
    rows = dq.shape[0]
    tm = min(TM_ROPE, rows)
    hw = MLA_HEADS * SLOT

    def body(dq_ref, dk_ref, dv_ref, c_ref, s1_ref, s2_ref, k_ref, dqo, dkvo, dkro):
        c, s1, s2 = c_ref[...], s1_ref[...], s2_ref[...]
        ksum = jnp.zeros((tm, SLOT), F32)
        low = lax.broadcasted_iota(jnp.int32, (tm, SLOT), 1) < HEAD_DIM
        for h in range(MLA_HEADS):
            sl = slice(h * SLOT, (h + 1) * SLOT)
            dqo[:, sl] = _rot_t(dq_ref[:, sl], c, s1, s2).astype(BF16)
            dkh = dk_ref[:, sl]
            ksum = ksum + dkh
            dvh = pltpu.roll(jnp.where(low, dv_ref[:, sl], 0.0), HEAD_DIM, 1)
            dkvo[:, sl] = (jnp.where(low, dkh, 0.0) + dvh).astype(BF16)
        dkro[...] = _rot_t(ksum, c, s1, s2) * (k_ref[2:3, :] + k_ref[3:4, :])

    tab = pl.BlockSpec((tm, SLOT), lambda i: (i, 0))
    wide = pl.BlockSpec((tm, hw), lambda i: (i, 0))
    return pl.pallas_call(
        body, name="mla_rope_bwd", grid=(rows // tm,),
        in_specs=[wide, wide, wide, tab, tab, tab, pl.BlockSpec((8, SLOT), lambda i: (0, 0))],
        out_specs=[wide, wide, tab],
        out_shape=[jax.ShapeDtypeStruct((rows, hw), BF16), jax.ShapeDtypeStruct((rows, hw), BF16),
                   jax.ShapeDtypeStruct((rows, SLOT), F32)],
        compiler_params=_cparams(),
    )(dq, dk, dv, *tabs, consts)


def _nt(a, b):
    return lax.dot_general(a, b, _DIMS["nt"], preferred_element_type=F32)


def _tn(a, b):
    return lax.dot_general(a, b, _DIMS["tn"], preferred_element_type=F32)


def _nn(a, b):
    return lax.dot_general(a, b, _DIMS["nn"], preferred_element_type=F32)


def _mla_attn_fwd(q, k, v):
    rows = q.shape[0]
    t = min(TQ_MLA, rows)
    nt = rows // t
    wide = MLA_PACK * SLOT

    def body(q_ref, k_ref, v_ref, o_ref, lse_ref, m_sc, l_sc, acc_sc):
        i, j = pl.program_id(1), pl.program_id(2)

        @pl.when(j == 0)
        def _():
            m_sc[...] = jnp.full_like(m_sc, NEG)
            l_sc[...] = jnp.zeros_like(l_sc)
            acc_sc[...] = jnp.zeros_like(acc_sc)

        def step(diagonal):
            for hh in range(MLA_PACK):
                sl = slice(hh * SLOT, (hh + 1) * SLOT)
                s = _nt(k_ref[:, sl], q_ref[:, sl])
                if diagonal:
                    key = lax.broadcasted_iota(jnp.int32, (t, t), 0)
                    s = jnp.where(key <= lax.broadcasted_iota(jnp.int32, (t, t), 1), s, NEG)
                m_prev = m_sc[hh]
                m_new = jnp.maximum(m_prev, jnp.max(s, axis=0, keepdims=True))
                p = jnp.exp2(s - m_new)
                alpha = jnp.exp2(m_prev - m_new)
                l_new = alpha * l_sc[hh] + jnp.sum(p, axis=0, keepdims=True)
                acc = alpha * acc_sc[hh] + _tn(v_ref[:, sl], p.astype(BF16))
                if diagonal:
                    o_ref[:, sl] = (acc / l_new).T.astype(o_ref.dtype)
                    lse_ref[hh:hh + 1, :] = m_new + jnp.log(l_new) * LOG2_E
                else:
                    m_sc[hh] = m_new
                    l_sc[hh] = l_new
                    acc_sc[hh] = acc

        @pl.when(j < i)
        def _():
            step(False)

        @pl.when(j == i)
        def _():
            lse_ref[...] = jnp.zeros_like(lse_ref)
            step(True)

    q_spec = pl.BlockSpec((t, wide), lambda h, i, j: (i, h))
    kv_spec = pl.BlockSpec((t, wide), lambda h, i, j: (jnp.minimum(j, i), h))
    return pl.pallas_call(
        body, name="mla_attn_fwd", grid=(MLA_HEADS // MLA_PACK, nt, nt),
        in_specs=[q_spec, kv_spec, kv_spec],
        out_specs=[q_spec, pl.BlockSpec((None, 8, t), lambda h, i, j: (h, 0, i))],
        out_shape=[jax.ShapeDtypeStruct(q.shape, BF16),
                   jax.ShapeDtypeStruct((MLA_HEADS // MLA_PACK, 8, rows), F32)],
        scratch_shapes=[pltpu.VMEM((MLA_PACK, 1, t), F32), pltpu.VMEM((MLA_PACK, 1, t), F32),
                        pltpu.VMEM((MLA_PACK, SLOT, t), F32)],
        compiler_params=_cparams(),
    )(q, k, v)


def _mla_delta(o, do):
    rows = o.shape[0]
    t = rows
    wide = MLA_PACK * SLOT

    def body(o_ref, do_ref, d_ref):
        d_ref[...] = jnp.zeros_like(d_ref)
        ones = jnp.ones((8, SLOT), BF16)
        for hh in range(MLA_PACK):
            sl = slice(hh * SLOT, (hh + 1) * SLOT)
            prod = do_ref[:, sl].astype(F32) * o_ref[:, sl].astype(F32)
            high = prod.astype(BF16)
            low = (prod - high.astype(F32)).astype(BF16)
            d_ref[hh:hh + 1, :] = (_nt(ones, high) + _nt(ones, low))[0:1, :]

    spec = pl.BlockSpec((t, wide), lambda h, i: (i, h))
    return pl.pallas_call(
        body, name="mla_delta", grid=(MLA_HEADS // MLA_PACK, rows // t), in_specs=[spec, spec],
        out_specs=pl.BlockSpec((None, 8, t), lambda h, i: (h, 0, i)),
        out_shape=jax.ShapeDtypeStruct((MLA_HEADS // MLA_PACK, 8, rows), F32), compiler_params=_cparams(),
    )(o, do)


def _mla_attn_bwd(q, k, v, do, lse, delta, after):
    rows = q.shape[0]
    t = min(TQ_MLA, rows)
    nt = rows // t
    wide = MLA_PACK * SLOT

    def body(q_ref, k_ref, v_ref, do_ref, lse_ref, delta_ref, after_ref, dq_ref, dk_ref, dv_ref, dk_sc, dv_sc):
        j, i = pl.program_id(1), pl.program_id(2)

        @pl.when((j == 0) & (i == 0))
        def _():
            dq_ref[...] = jnp.zeros_like(dq_ref)

        @pl.when(i == 0)
        def _():
            dk_sc[...] = jnp.zeros_like(dk_sc)
            dv_sc[...] = jnp.zeros_like(dv_sc)

        def chunk(hh, rows, keys, masked):
            sl = slice(hh * SLOT, (hh + 1) * SLOT)
            n_rows = rows.stop - rows.start
            qv, kv, dov = q_ref[rows, sl], k_ref[keys, sl], do_ref[rows, sl]
            s = _nt(kv, qv)
            if masked:
                shp = (keys.stop - keys.start, n_rows)
                s = jnp.where(keys.start + lax.broadcasted_iota(jnp.int32, shp, 0)
                              <= rows.start + lax.broadcasted_iota(jnp.int32, shp, 1), s, NEG)
            p = jnp.exp2(s - lse_ref[hh:hh + 1, rows])
            dp = _nt(v_ref[keys, sl], dov)
            ds = (p * (dp - delta_ref[hh:hh + 1, rows])).astype(BF16)
            dv_sc[keys, sl] += _nn(p.astype(BF16), dov)
            dk_sc[keys, sl] += _nn(ds, qv)
            r0 = pl.multiple_of(i * t + rows.start, n_rows)
            dq_ref[pl.ds(r0, n_rows), sl] += _tn(ds, kv) * MLA_SCALE

        @pl.when(i > j)
        def _():
            for hh in range(MLA_PACK):
                chunk(hh, slice(0, t), slice(0, t), False)

        @pl.when(i == j)
        def _():
            for hh in range(MLA_PACK):
                chunk(hh, slice(0, t), slice(0, t // 2), True)
                chunk(hh, slice(t // 2, t), slice(t // 2, t), True)

        @pl.when(i == nt - 1)
        def _():
            dk_ref[...] = dk_sc[...] * (1.0 / LOG2_E)
            dv_ref[...] = dv_sc[...]

    q_spec = pl.BlockSpec((t, wide), lambda h, j, i: (jnp.maximum(i, j), h))
    kv_spec = pl.BlockSpec((t, wide), lambda h, j, i: (j, h))
    row_spec = pl.BlockSpec((None, 8, t), lambda h, j, i: (h, 0, jnp.maximum(i, j)))
    head_spec = pl.BlockSpec((rows, wide), lambda h, j, i: (0, h))
    shp = jax.ShapeDtypeStruct(q.shape, F32)
    return pl.pallas_call(
        body, name="mla_attn_bwd", grid=(MLA_HEADS // MLA_PACK, nt, nt),
        in_specs=[q_spec, kv_spec, kv_spec, q_spec, row_spec, row_spec, pl.BlockSpec(memory_space=pl.ANY)],
        out_specs=[head_spec, kv_spec, kv_spec], out_shape=[shp, shp, shp],
        scratch_shapes=[pltpu.VMEM((t, wide), F32), pltpu.VMEM((t, wide), F32)],
        compiler_params=_cparams(),
    )(q, k, v, do, lse, delta, after)


def _swa_specs(t):
    def prev(i):
        return jnp.maximum(i - 1, 0)
    kw = SWA_PACK * SLOT
    k0, v0 = SWA_HEADS // SWA_PACK, (SWA_HEADS + SWA_KV_HEADS) // SWA_PACK
    q3 = pl.BlockSpec((t, SWA_PACK * SWA_GROUP * SLOT), lambda h, i: (i, h))
    kp = pl.BlockSpec((t, kw), lambda h, i: (prev(i), k0 + h))
    kc = pl.BlockSpec((t, kw), lambda h, i: (i, k0 + h))
    vp = pl.BlockSpec((t, kw), lambda h, i: (prev(i), v0 + h))
    vc = pl.BlockSpec((t, kw), lambda h, i: (i, v0 + h))
    pcol = pl.BlockSpec((t, 1), lambda h, i: (i, 0))
    prow_p = pl.BlockSpec((1, t), lambda h, i: (0, prev(i)))
    prow_c = pl.BlockSpec((1, t), lambda h, i: (0, i))
    return [q3, kp, kc, vp, vc, pcol, prow_p, prow_c]


def _stack(ref, first):
    return jnp.concatenate([ref[:, (first + g) * SLOT:(first + g + 1) * SLOT] for g in range(SWA_GROUP)], axis=0)


def _swa_logits(q3, kp, kc, pq, pkp, pkc, slope_ref, kvh, i, t):
    r = lax.broadcasted_iota(jnp.int32, (t, t), 0)
    c = lax.broadcasted_iota(jnp.int32, (t, t), 1)
    ok_c = c <= r
    ok_p = (c - r) > jnp.where(i > 0, 0, t)
    dist_p, dist_c = pq - pkp, pq - pkc
    s_p3 = _nt(q3, kp) * (HEAD_DIM ** -0.5)
    s_c3 = _nt(q3, kc) * (HEAD_DIM ** -0.5)
    out = []
    for g in range(SWA_GROUP):
        slope = slope_ref[kvh * SWA_GROUP + g]
        rows = slice(g * t, (g + 1) * t)
        out.append((jnp.where(ok_p, s_p3[rows] - slope * dist_p, NEG),
                    jnp.where(ok_c, s_c3[rows] - slope * dist_c, NEG)))
    return out


def _swa_attn_fwd(proj, pos_col, pos_row, slopes, sinks):
    rows = proj.shape[0]
    t = WINDOW
    hw = SWA_HEADS * SLOT

    def body(slope_ref, sink_ref, q_ref, kp_ref, kc_ref, vp_ref, vc_ref, pq_ref, pkp_ref, pkc_ref, o_ref, lse_ref):
        i = pl.program_id(1)
        for kv in range(SWA_PACK):
            kvh = pl.program_id(0) * SWA_PACK + kv
            ksl = slice(kv * SLOT, (kv + 1) * SLOT)
            logits = _swa_logits(_stack(q_ref, kv * SWA_GROUP), kp_ref[:, ksl], kc_ref[:, ksl], pq_ref[...],
                                 pkp_ref[...], pkc_ref[...], slope_ref, kvh, i, t)
            e_p, e_c, norm = [], [], []
            for g, (s_p, s_c) in enumerate(logits):
                sl = slice((kv * SWA_GROUP + g) * SLOT, (kv * SWA_GROUP + g + 1) * SLOT)
                sink = sink_ref[kvh * SWA_GROUP + g]
                m = jnp.maximum(jnp.maximum(jnp.max(s_p, axis=1, keepdims=True),
                                            jnp.max(s_c, axis=1, keepdims=True)), sink)
                ep, ec = jnp.exp(s_p - m), jnp.exp(s_c - m)
                l = jnp.sum(ep, axis=1, keepdims=True) + jnp.sum(ec, axis=1, keepdims=True) + jnp.exp(sink - m)
                e_p.append(ep.astype(BF16))
                e_c.append(ec.astype(BF16))
                norm.append(l)
                lse_ref[:, sl] = jnp.broadcast_to(m + jnp.log(l), (t, SLOT))
            acc = (_nn(jnp.concatenate(e_p, axis=0), vp_ref[:, ksl])
                   + _nn(jnp.concatenate(e_c, axis=0), vc_ref[:, ksl]))
            for g in range(SWA_GROUP):
                sl = slice((kv * SWA_GROUP + g) * SLOT, (kv * SWA_GROUP + g + 1) * SLOT)
                o_ref[:, sl] = (acc[g * t:(g + 1) * t] / norm[g]).astype(o_ref.dtype)

    smem = pl.BlockSpec(memory_space=pltpu.SMEM)
    out_spec = pl.BlockSpec((t, SWA_PACK * SWA_GROUP * SLOT), lambda h, i: (i, h))
    return pl.pallas_call(
        body, name="swa_attn_fwd", grid=(SWA_KV_HEADS // SWA_PACK, rows // t),
        in_specs=[smem, smem] + _swa_specs(t), out_specs=[out_spec, out_spec],
        out_shape=[jax.ShapeDtypeStruct((rows, hw), BF16), jax.ShapeDtypeStruct((rows, hw), F32)],
        compiler_params=_cparams(),
    )(slopes, sinks, proj, proj, proj, proj, proj, pos_col, pos_row, pos_row)


def _swa_attn_bwd(proj, o, do, lse, pos_col, pos_row, slopes, sinks, after):
    rows = proj.shape[0]
    t = WINDOW
    hw = SWA_HEADS * SLOT
    scale = HEAD_DIM ** -0.5

    def body(slope_ref, sink_ref, q_ref, kp_ref, kc_ref, vp_ref, vc_ref, pq_ref, pkp_ref, pkc_ref,
             o_ref, do_ref, lse_ref, after_ref, dq_ref, dk_ref, dv_ref, dsink_ref):
        i = pl.program_id(1)

        @pl.when(i == 0)
        def _():
            dk_ref[...] = jnp.zeros_like(dk_ref)
            dv_ref[...] = jnp.zeros_like(dv_ref)
            dsink_ref[...] = jnp.zeros_like(dsink_ref)

        r_c = pl.multiple_of(i * t, t)
        r_p = pl.multiple_of(jnp.maximum(i - 1, 0) * t, t)
        for kv in range(SWA_PACK):
            kvh = pl.program_id(0) * SWA_PACK + kv
            ksl = slice(kv * SLOT, (kv + 1) * SLOT)
            q3, do3 = _stack(q_ref, kv * SWA_GROUP), _stack(do_ref, kv * SWA_GROUP)
            logits = _swa_logits(q3, kp_ref[:, ksl], kc_ref[:, ksl], pq_ref[...], pkp_ref[...], pkc_ref[...],
                                 slope_ref, kvh, i, t)
            dp_p3, dp_c3 = _nt(do3, vp_ref[:, ksl]), _nt(do3, vc_ref[:, ksl])
            p_p, p_c, ds_p, ds_c = [], [], [], []
            for g, (s_p, s_c) in enumerate(logits):
                head = kv * SWA_GROUP + g
                sl = slice(head * SLOT, (head + 1) * SLOT)
                rws = slice(g * t, (g + 1) * t)
                lse_g = lse_ref[:, head * SLOT:head * SLOT + 1]
                pp, pc = jnp.exp(s_p - lse_g), jnp.exp(s_c - lse_g)
                delta = jnp.sum(do_ref[:, sl].astype(F32) * o_ref[:, sl].astype(F32), axis=1, keepdims=True)
                p_p.append(pp.astype(BF16))
                p_c.append(pc.astype(BF16))
                ds_p.append((pp * (dp_p3[rws] - delta)).astype(BF16))
                ds_c.append((pc * (dp_c3[rws] - delta)).astype(BF16))
                sink = sink_ref[kvh * SWA_GROUP + g]
                dsink = -jnp.sum(jnp.exp(sink - lse_g) * delta, axis=0, keepdims=True)
                dsink_ref[head * 8:(head + 1) * 8, :] += jnp.broadcast_to(dsink, (8, SLOT))
            p_p3, p_c3 = jnp.concatenate(p_p, axis=0), jnp.concatenate(p_c, axis=0)
            ds_p3, ds_c3 = jnp.concatenate(ds_p, axis=0), jnp.concatenate(ds_c, axis=0)
            dq3 = (_nn(ds_p3, kp_ref[:, ksl]) + _nn(ds_c3, kc_ref[:, ksl])) * scale
            for g in range(SWA_GROUP):
                head = kv * SWA_GROUP + g
                dq_ref[:, head * SLOT:(head + 1) * SLOT] = dq3[g * t:(g + 1) * t]
            dk_ref[pl.ds(r_c, t), ksl] += _tn(ds_c3, q3) * scale
            dv_ref[pl.ds(r_c, t), ksl] += _tn(p_c3, do3)
            dk_ref[pl.ds(r_p, t), ksl] += _tn(ds_p3, q3) * scale
            dv_ref[pl.ds(r_p, t), ksl] += _tn(p_p3, do3)

    smem = pl.BlockSpec(memory_space=pltpu.SMEM)
    qlike = pl.BlockSpec((t, SWA_PACK * SWA_GROUP * SLOT), lambda h, i: (i, h))
    kv_out = pl.BlockSpec((rows, SWA_PACK * SLOT), lambda h, i: (0, h))
    return pl.pallas_call(
        body, name="swa_attn_bwd", grid=(SWA_KV_HEADS // SWA_PACK, rows // t),
        in_specs=[smem, smem] + _swa_specs(t) + [qlike, qlike, qlike, pl.BlockSpec(memory_space=pl.ANY)],
        out_specs=[qlike, kv_out, kv_out,
                   pl.BlockSpec((SWA_PACK * SWA_GROUP * 8, SLOT), lambda h, i: (h, 0))],
        out_shape=[jax.ShapeDtypeStruct((rows, hw), F32), jax.ShapeDtypeStruct((rows, SWA_KV_HEADS * SLOT), F32),
                   jax.ShapeDtypeStruct((rows, SWA_KV_HEADS * SLOT), F32),
                   jax.ShapeDtypeStruct((SWA_HEADS * 8, SLOT), F32)],
        compiler_params=_cparams(),
    )(slopes, sinks, proj, proj, proj, proj, proj, pos_col, pos_row, pos_row, o, do, lse, after)


def _cross_attn_fwd(proj, qoff, kvmem):
    rows = proj.shape[0]
    t = min(TQ_CROSS, rows)

    def body(q_ref, k_ref, v_ref, o_ref):
        s = _nt(k_ref[...], q_ref[...].astype(BF16)) * (HEAD_DIM ** -0.5)
        e = jnp.exp(s - jnp.max(s, axis=0, keepdims=True))
        p = e / jnp.sum(e, axis=0, keepdims=True)
        o_ref[...] = _tn(v_ref[...], p.astype(BF16)).T.astype(o_ref.dtype)

    return pl.pallas_call(
        body, name="cross_attn_fwd", grid=(rows // t, MEM_HEADS),
        in_specs=[pl.BlockSpec((t, SLOT), lambda i, h: (i, qoff + h)),
                  pl.BlockSpec((N_MEM, SLOT), lambda i, h: (0, h)),
                  pl.BlockSpec((N_MEM, SLOT), lambda i, h: (0, MEM_HEADS + h))],
        out_specs=pl.BlockSpec((t, SLOT), lambda i, h: (i, h)),
        out_shape=jax.ShapeDtypeStruct((rows, MEM_HEADS * SLOT), BF16), compiler_params=_cparams(),
    )(proj, kvmem, kvmem)


def _cross_attn_bwd(proj, qoff, kvmem, do):
    rows = proj.shape[0]
    t = min(TQ_CROSS, rows)
    scale = HEAD_DIM ** -0.5

    def body(q_ref, k_ref, v_ref, do_ref, dq_ref, dk_ref, dv_ref):
        @pl.when(pl.program_id(1) == 0)
        def _():
            dk_ref[...] = jnp.zeros_like(dk_ref)
            dv_ref[...] = jnp.zeros_like(dv_ref)

        qv, kv, dov = q_ref[...].astype(BF16), k_ref[...], do_ref[...]
        s = _nt(kv, qv) * scale
        e = jnp.exp(s - jnp.max(s, axis=0, keepdims=True))
        p = e / jnp.sum(e, axis=0, keepdims=True)
        dp = _nt(v_ref[...], dov)
        ds = (p * (dp - jnp.sum(p * dp, axis=0, keepdims=True))).astype(BF16)
        dq_ref[...] = _tn(ds, kv) * scale
        dk_ref[...] += _nn(ds, qv) * scale
        dv_ref[...] += _nn(p.astype(BF16), dov)

    mem_out = pl.BlockSpec((N_MEM, SLOT), lambda h, i: (0, h))
    return pl.pallas_call(
        body, name="cross_attn_bwd", grid=(MEM_HEADS, rows // t),
        in_specs=[pl.BlockSpec((t, SLOT), lambda h, i: (i, qoff + h)),
                  pl.BlockSpec((N_MEM, SLOT), lambda h, i: (0, h)),
                  pl.BlockSpec((N_MEM, SLOT), lambda h, i: (0, MEM_HEADS + h)),
                  pl.BlockSpec((t, SLOT), lambda h, i: (i, h))],
        out_specs=[pl.BlockSpec((t, SLOT), lambda h, i: (i, h)), mem_out, mem_out],
        out_shape=[jax.ShapeDtypeStruct((rows, MEM_HEADS * SLOT), F32),
                   jax.ShapeDtypeStruct((N_MEM, MEM_HEADS * SLOT), F32),
                   jax.ShapeDtypeStruct((N_MEM, MEM_HEADS * SLOT), F32)],
        compiler_params=_cparams(),
    )(proj, kvmem, kvmem, do)


def _place():
    return lax.axis_index("x"), lax.axis_index("y"), lax.axis_index("c")


def _flip(v, bit):
    return 1 - v if bit else v


def _all_gather(blocks, name):
    nb = len(blocks)

    def body(*refs):
        x_refs, out_refs = refs[:nb], refs[nb:2 * nb]
        send_sems, recv_sems, local_sems = refs[2 * nb:]
        x, y, c = _place()
        me, sibling = (x, y, c), (x, y, 1 - c)
        chips = [(1 - x, y), (x, 1 - y), (1 - x, 1 - y)]

        def copy(b, k, blk, to, from_input=False):
            slot = out_refs[b].at[4 * blk[0] + 2 * blk[1] + blk[2]]
            return pltpu.make_async_remote_copy(
                src_ref=x_refs[b] if from_input else slot, dst_ref=slot,
                send_sem=send_sems.at[b, k], recv_sem=recv_sems.at[b, k],
                device_id=to, device_id_type=pl.DeviceIdType.MESH)

        mine = [pltpu.make_async_copy(x_refs[b], out_refs[b].at[4 * x + 2 * y + c], local_sems.at[b])
                for b in range(nb)]
        for cp in mine:
            cp.start()
        first = []
        for b in range(nb):
            first.append(copy(b, 0, me, sibling, from_input=True))
            first += [copy(b, 1 + n, me, (*chip, c), from_input=True) for n, chip in enumerate(chips)]
        for cp in first:
            cp.start()
        passed = []
        for n, chip in enumerate(chips):
            for b in range(nb):
                copy(b, 1 + n, (*chip, c), me).wait_recv()
                passed.append(copy(b, 4 + n, (*chip, c), sibling))
                passed[-1].start()
        for b in range(nb):
            copy(b, 0, sibling, me).wait_recv()
            for n, chip in enumerate(chips):
                copy(b, 4 + n, (*chip, 1 - c), me).wait_recv()
        for cp in first + passed:
            cp.wait_send()
        for cp in mine:
            cp.wait()

    any_spec = pl.BlockSpec(memory_space=pl.ANY)
    return pl.pallas_call(
        body, name=name, in_specs=[any_spec] * nb, out_specs=[any_spec] * nb,
        out_shape=[jax.ShapeDtypeStruct((N_DEV,) + blk.shape, blk.dtype) for blk in blocks],
        scratch_shapes=[pltpu.SemaphoreType.DMA((nb, 7)), pltpu.SemaphoreType.DMA((nb, 7)),
                        pltpu.SemaphoreType.DMA((nb,))],
    )(*blocks)


def _peers(x, y, c):
    out = []
    for n in range(1, N_DEV):
        peer = (_flip(x, n & 4), _flip(y, n & 2), _flip(c, n & 1))
        out.append((n - 1, peer, 4 * peer[0] + 2 * peer[1] + peer[2]))
    return out


_HBM = pl.BlockSpec(memory_space=pltpu.HBM)
_SEM = pl.BlockSpec(memory_space=pltpu.SEMAPHORE)


def _exchange_start(srcs, scatter, name, after=None):
    ns = len(srcs)
    lands = [lax.empty(s.shape if scatter else (N_DEV,) + s.shape, s.dtype) for s in srcs]

    def body(*refs):
        src_refs, land_refs = refs[:ns], refs[ns:2 * ns]
        pos = 2 * ns + (1 if after is not None else 0)
        send_sems, recv_sems, token = refs[pos], refs[pos + 1], refs[-1]
        x, y, c = _place()
        my_idx = 4 * x + 2 * y + c
        for col, peer, peer_idx in _peers(x, y, c):
            for b in range(ns):
                pltpu.make_async_remote_copy(
                    src_ref=src_refs[b].at[peer_idx] if scatter else src_refs[b], dst_ref=land_refs[b].at[my_idx],
                    send_sem=send_sems.at[b * (N_DEV - 1) + col], recv_sem=recv_sems.at[b * (N_DEV - 1) + col],
                    device_id=peer, device_id_type=pl.DeviceIdType.MESH).start()
        token[...] = jnp.zeros_like(token)

    args = [pltpu.with_memory_space_constraint(a, pltpu.HBM) for a in list(srcs) + lands]
    in_specs = [_HBM] * (2 * ns)
    if after is not None:
        args.append(after)
        in_specs.append(pl.BlockSpec(memory_space=pl.ANY))
    out = pl.pallas_call(
        body, name=name, in_specs=in_specs,
        out_specs=[_SEM, _SEM] + [_HBM] * (2 * ns) + [pl.BlockSpec(memory_space=pltpu.VMEM)],
        out_shape=[pltpu.SemaphoreType.DMA((ns * (N_DEV - 1),)), pltpu.SemaphoreType.DMA((ns * (N_DEV - 1),))]
        + [pltpu.HBM(a.shape, a.dtype) for a in list(srcs) + lands] + [jax.ShapeDtypeStruct((8, SLOT), F32)],
        input_output_aliases={k: 2 + k for k in range(2 * ns)},
        compiler_params=pltpu.CompilerParams(has_side_effects=pltpu.SideEffectType.DATAFLOW_SIDE_EFFECTING),
    )(*args)
    return (out[0], out[1], out[2:2 + ns], out[2 + ns:2 + 2 * ns], scatter), out[-1]


def _exchange_wait(handle, after, name):
    send_sems, recv_sems, srcs, lands, scatter = handle
    ns = len(srcs)

    def body(*refs):
        src_refs, land_refs = refs[:ns], refs[ns:2 * ns]
        send_ref, recv_ref = refs[2 * ns], refs[2 * ns + 1]
        x, y, c = _place()
        for col, peer, peer_idx in _peers(x, y, c):
            for b in range(ns):
                copy = pltpu.make_async_remote_copy(
                    src_ref=src_refs[b].at[peer_idx] if scatter else src_refs[b], dst_ref=land_refs[b].at[peer_idx],
                    send_sem=send_ref.at[b * (N_DEV - 1) + col], recv_sem=recv_ref.at[b * (N_DEV - 1) + col],
                    device_id=peer, device_id_type=pl.DeviceIdType.MESH)
                copy.wait_send()
                copy.wait_recv()

    out = pl.pallas_call(
        body, name=name, in_specs=[_HBM] * (2 * ns) + [_SEM, _SEM, pl.BlockSpec(memory_space=pl.ANY)],
        out_specs=[_HBM] * (2 * ns),
        out_shape=[pltpu.HBM(a.shape, a.dtype) for a in list(srcs) + list(lands)],
        input_output_aliases={k: k for k in range(2 * ns)},
        compiler_params=pltpu.CompilerParams(has_side_effects=pltpu.SideEffectType.DATAFLOW_SIDE_EFFECTING),
    )(*srcs, *lands, send_sems, recv_sems, after)
    my_idx = 4 * lax.axis_index("x") + 2 * lax.axis_index("y") + lax.axis_index("c")
    landed = []
    for src, land in zip(out[:ns], out[ns:]):
        own = lax.dynamic_index_in_dim(src, my_idx, 0, keepdims=True) if scatter else src[None]
        landed.append(lax.dynamic_update_index_in_dim(land, own, my_idx, 0))
    return landed


def _adamw(parts, w, m, v, name):
    lyr, rows, cols = w.shape
    assert len(parts) == lyr
    tr = ADAM_ROWS if cols > 512 else 2 * ADAM_ROWS
    while rows % tr:
        tr //= 2
    tr = min(tr, rows)

    def body(*refs):
        p_refs = refs[:lyr]
        w_ref, m_ref, v_ref, g_out, d_out, m_out, v_out = refs[lyr:]
        for k in range(lyr):
            @pl.when(pl.program_id(0) == k)
            def _(p_ref=p_refs[k]):
                g = p_ref[0].astype(F32)
                for s in range(1, N_DEV):
                    g = g + p_ref[s].astype(F32)
                m2 = ADAM_B1 * m_ref[...] + (1.0 - ADAM_B1) * g
                v2 = ADAM_B2 * v_ref[...] + (1.0 - ADAM_B2) * (g * g)
                m_hat = m2 / (1.0 - ADAM_B1 ** ADAM_STEP)
                v_hat = v2 / (1.0 - ADAM_B2 ** ADAM_STEP)
                g_out[...] = g
                d_out[...] = -ADAM_LR * (m_hat / (jnp.sqrt(v_hat) + ADAM_EPS) + ADAM_WD * w_ref[...])
                m_out[...] = m2
                v_out[...] = v2

    def part_spec(k):
        return pl.BlockSpec((N_DEV, tr, cols), lambda l, i: (0, jnp.where(l == k, i, 0), 0))

    spec = pl.BlockSpec((None, tr, cols), lambda l, i: (l, i, 0))
    shp = jax.ShapeDtypeStruct((lyr, rows, cols), F32)
    return pl.pallas_call(
        body, name=name, grid=(lyr, rows // tr),
        in_specs=[part_spec(k) for k in range(lyr)] + [spec, spec, spec],
        out_specs=[spec] * 4, out_shape=[shp] * 4, compiler_params=_cparams(),
    )(*parts, w, m, v)


def _pack(arrays, lanes, row_mult, dtype):
    flat = jnp.concatenate([a.reshape(-1).astype(dtype) for a in arrays])
    unit = lanes * row_mult
    total = -(-flat.shape[0] // unit) * unit
    return jnp.pad(flat, (0, total - flat.shape[0])).reshape(total // lanes, lanes)


def _unpack(packed, shapes):
    flat = packed.reshape(-1)
    out, off = [], 0
    for shp in shapes:
        n = 1
        for d in shp:
            n *= d
        out.append(flat[off:off + n].reshape(shp))
        off += n
    return out


def _pad_slots(w, axis):
    axis = axis % w.ndim
    n = w.shape[axis] // HEAD_DIM
    shp = w.shape[:axis] + (n, HEAD_DIM) + w.shape[axis + 1:]
    pad = [(0, 0)] * (w.ndim + 1)
    pad[axis + 1] = (0, SLOT - HEAD_DIM)
    return jnp.pad(w.reshape(shp), pad).reshape(w.shape[:axis] + (n * SLOT,) + w.shape[axis + 1:])


def _unpad_slots(w, axis, keep=HEAD_DIM):
    axis = axis % w.ndim
    n = w.shape[axis] // SLOT
    shp = w.shape[:axis] + (n, SLOT) + w.shape[axis + 1:]
    idx = [slice(None)] * (w.ndim + 1)
    idx[axis + 1] = slice(0, keep)
    return w.reshape(shp)[tuple(idx)].reshape(w.shape[:axis] + (n * keep,) + w.shape[axis + 1:])


def _mla_in_pad(w):
    z = functools.partial(jnp.zeros, dtype=w.dtype)
    rows = w.shape[0]
    return jnp.concatenate([w[:, :384], z((rows, 64)), w[:, 640:672], z((rows, 32)), w[:, 384:640],
                            _pad_slots(w[:, 672:], 1)], axis=1)


def _mla_in_unpad(d):
    return jnp.concatenate([d[:, :384], d[:, 512:768], d[:, 448:480], _unpad_slots(d[:, 768:], 1)], axis=1)


def _mla_uq_pad(w):
    return jnp.pad(w.reshape(w.shape[0], MLA_HEADS, MLA_QK), ((0, 0), (0, 0), (0, SLOT - MLA_QK))).reshape(
        w.shape[0], MLA_HEADS * SLOT)


def _join(gathered, axis):
    nd, a, b = gathered.shape
    if axis == 1:
        return gathered.reshape(nd * a, b)
    return gathered.transpose(1, 0, 2).reshape(a, nd * b)


def _split(full, axis):
    r, c = full.shape
    if axis == 1:
        return full.reshape(N_DEV, r // N_DEV, c).astype(BF16)
    return full.reshape(r, N_DEV, c // N_DEV).transpose(1, 0, 2).astype(BF16)


def kernel(x, mem, positions, attn_norm_g, mlp_norm_g, mem_norm_g, final_norm_g, mla_w_in, mla_q_norm_g, mla_kv_norm_g, mla_w_uq, mla_w_ukv, swa_w_in, swa_sinks, w_mem_kv, w_o, mlp_w_up, mlp_w_down, loss_target, m_attn_norm_g, m_mlp_norm_g, m_mem_norm_g, m_final_norm_g, m_mla_w_in, m_mla_q_norm_g, m_mla_kv_norm_g, m_mla_w_uq, m_mla_w_ukv, m_swa_w_in, m_swa_sinks, m_w_mem_kv, m_w_o, m_mlp_w_up, m_mlp_w_down, v_attn_norm_g, v_mlp_norm_g, v_mem_norm_g, v_final_norm_g, v_mla_w_in, v_mla_q_norm_g, v_mla_kv_norm_g, v_mla_w_uq, v_mla_w_ukv, v_swa_w_in, v_swa_sinks, v_w_mem_kv, v_w_o, v_mlp_w_up, v_mlp_w_down):
    given = dict(locals())
    seq = x.shape[1]
    x0 = x.reshape(seq, D_MODEL)
    tgt = loss_target.reshape(seq, D_MODEL)
    mem0 = mem.reshape(N_MEM, D_MODEL)
    pos = positions.reshape(seq).astype(F32)
    pos_col, pos_row = pos.reshape(seq, 1), pos.reshape(1, seq)

    def layer_names(i):
        mixer = ("mla_w_in", "mla_w_uq", "mla_w_ukv") if i % 2 == 0 else ("swa_w_in",)
        return [(n, i // 2) for n in mixer] + [(n, i) for n in ("w_mem_kv", "w_o", "mlp_w_up", "mlp_w_down")]

    def local_weights(names):
        return [given[n][l].astype(BF16) for n, l in names]

    first_attn, first_mlp = layer_names(0)[:-2], layer_names(0)[-2:]
    weights = [dict(zip([n for n, _ in first_attn], _all_gather(local_weights(first_attn), "gather_weights_first")))]
    coming_mlp, first_token = _exchange_start(local_weights(first_mlp), False, "gather_weights_start_0",
                                              after=weights[0]["w_o"])

    consts = _lane_consts()
    tabs = _rope_tables(pos_col, consts)
    slopes = 2.0 ** (-8.0 * (jnp.arange(SWA_HEADS, dtype=F32) + 1.0) / SWA_HEADS)

    mem_n = _rmsnorm_fwd(mem0, 0, D_MODEL, mem_norm_g, "rmsnorm_fwd_mem")

    saved = []
    xc = x0
    for i in range(DEPTH):
        j = i // 2
        wts = weights[i]
        s = {"x_in": xc}
        token = None
        if i + 1 < DEPTH:
            coming, token = _exchange_start(local_weights(layer_names(i + 1)), False,
                                            "gather_weights_start_%d" % (i + 1),
                                            after=first_token if i == 0 else wts["w_o"])
        hn = _rmsnorm_fwd(xc, 0, D_MODEL, attn_norm_g[i], "rmsnorm_fwd", after=token)
        if i % 2 == 0:
            w_in = _mla_in_pad(_join(wts["mla_w_in"], 1))
            w_uq = _mla_uq_pad(_join(wts["mla_w_uq"], 2))
            w_kv = _join(wts["mla_w_ukv"], 2)
            proj = _mm(hn, w_in, "nn", F32, "mm_mla_in")
            cqn = _rmsnorm_fwd(proj, 0, MLA_Q_RANK, mla_q_norm_g[j], "rmsnorm_fwd_q")
            ckvn = _rmsnorm_fwd(proj, 2, MLA_KV_RANK, mla_kv_norm_g[j], "rmsnorm_fwd_kv")
            qraw = _mm(cqn, w_uq, "nn", F32, "mm_mla_uq")
            kvraw = _mm(ckvn, w_kv, "nn", F32, "mm_mla_ukv")
            q, k, v = _mla_rope_fwd(qraw, kvraw, proj, tabs)
            o, lse = _mla_attn_fwd(q, k, v)
            qoff = MLA_QOFF
            s.update(w_uq=w_uq, w_kv=w_kv, cqn=cqn, ckvn=ckvn, q=q, k=k, v=v)
        else:
            w_in = _join(wts["swa_w_in"], 2)
            proj = _mm(hn, w_in, "nn", BF16, "mm_swa_in", pairs="o")
            o, lse = _swa_attn_fwd(proj, pos_col, pos_row, slopes, swa_sinks[j])
            qoff = SWA_QOFF
        w_mem = _pad_slots(_join(wts["w_mem_kv"], 1), 1)
        w_out = _join(wts["w_o"], 1)
        w_o_mix, w_o_cross = w_out[:SWA_HEADS * HEAD_DIM], w_out[SWA_HEADS * HEAD_DIM:]
        kvmem = _mm(mem_n, w_mem, "nn", BF16, "mm_mem_kv")
        cross = _cross_attn_fwd(proj, qoff, kvmem)
        x1 = _mm(o, w_o_mix, "nn", F32, "mm_o_mix", res=xc, pairs="a")
        x1 = _mm(cross, w_o_cross, "nn", F32, "mm_o_cross", res=x1, pairs="a")
        hn2 = _rmsnorm_fwd(x1, 0, D_MODEL, mlp_norm_g[i], "rmsnorm_fwd")
        if i == 0:
            wts.update(zip([n for n, _ in first_mlp], _exchange_wait(coming_mlp, hn2, "gather_weights_wait_0")))
        act, act2 = _mm(hn2, wts["mlp_w_up"], "nn", BF16, "mm_mlp_up", epi="relu2", b_blk="cols")
        xc = _mm(act2, wts["mlp_w_down"], "nn", F32, "mm_mlp_down", res=x1, b_blk="rows")
        s.update(hn=hn, w_in=w_in, proj=proj, o=o, lse=lse, qoff=qoff, w_mem=w_mem, w_o_mix=w_o_mix,
                 w_o_cross=w_o_cross, kvmem=kvmem, cross=cross, x1=x1, hn2=hn2, act=act, act2=act2)
        saved.append(s)
        if i + 1 < DEPTH:
            got = _exchange_wait(coming, xc, "gather_weights_wait_%d" % (i + 1))
            weights.append(dict(zip([n for n, _ in layer_names(i + 1)], got)))

    dx, dx_b, dg_final, loss_part = _loss_head(xc, final_norm_g, tgt)
    loss = lax.psum(loss_part[0, 0], MESH_AXES)

    gains = {n: [None] * DEPTH for n in ("attn_norm_g", "mlp_norm_g")}
    for n in ("mla_q_norm_g", "mla_kv_norm_g", "swa_sinks"):
        gains[n] = [None] * 2
    leaving = {}
    token = None
    dmem_n = None
    for i in reversed(range(DEPTH)):
        j = i // 2
        s = saved[i]
        wts = weights[i]
        out = {}
        du = _mm(dx_b, wts["mlp_w_down"], "nt", BF16, "mm_mlp_down_dx", aux=s["act"], epi="mul2aux", b_blk="rows",
                 after=token)
        out["mlp_w_down"] = _mm(s["act2"], dx_b, "tn", BF16, "mm_mlp_down_dw", o_blk="rows")
        out["mlp_w_up"] = _mm(s["hn2"], du, "tn", BF16, "mm_mlp_up_dw", o_blk="cols")
        dx1, dx1_b, dg = _mm(du, wts["mlp_w_up"], "nt", F32, "mm_mlp_up_dx", b_blk="cols",
                             epi="normbwd", norm=(s["x1"], mlp_norm_g[i], dx))
        gains["mlp_norm_g"][i] = dg[0]

        do = _mm(dx1_b, s["w_o_mix"], "nt", BF16, "mm_o_mix_dx", pairs="o")
        dcross = _mm(dx1_b, s["w_o_cross"], "nt", BF16, "mm_o_cross_dx", pairs="o")
        dw_o = jnp.concatenate([_mm(s["o"], dx1_b, "tn", F32, "mm_o_mix_dw", pairs="a"),
                                _mm(s["cross"], dx1_b, "tn", F32, "mm_o_cross_dw", pairs="a")], axis=0)
        out["w_o"] = _split(dw_o, 1)
        dqc, dkm, dvm = _cross_attn_bwd(s["proj"], s["qoff"], s["kvmem"], dcross)
        dkvmem = jnp.concatenate([dkm, dvm], axis=1).astype(BF16)
        out["w_mem_kv"] = _split(_unpad_slots(_mm(mem_n, dkvmem, "tn", F32, "mm_mem_kv_dw"), 1), 1)
        dmem_n = _mm(dkvmem, s["w_mem"], "nt", F32, "mm_mem_kv_dx" if dmem_n is None else "mm_mem_kv_dx_acc",
                     res=dmem_n)
        leaving[(i, "main")], token = _exchange_start([out[n] for n, _ in layer_names(i)[-4:]], True,
                                                      "exchange_grads_main_start_%d" % i)

        if i % 2 == 0:
            dq, dk, dv = _mla_attn_bwd(s["q"], s["k"], s["v"], do, s["lse"], _mla_delta(s["o"], do), token)
            dqraw, dkv, dkr = _mla_rope_bwd(dq, dk, dv, tabs, consts)
            dcqn = _mm(dqraw, s["w_uq"], "nt", F32, "mm_mla_uq_dx")
            out["mla_w_uq"] = _split(_unpad_slots(_mm(s["cqn"], dqraw, "tn", F32, "mm_mla_uq_dw"), 1, MLA_QK), 2)
            dckvn = _mm(dkv, s["w_kv"], "nt", F32, "mm_mla_ukv_dx")
            out["mla_w_ukv"] = _split(_mm(s["ckvn"], dkv, "tn", F32, "mm_mla_ukv_dw"), 2)
            dcq, dg = _rmsnorm_bwd(s["proj"], 0, MLA_Q_RANK, mla_q_norm_g[j], dcqn, None, BF16, "rmsnorm_bwd_q")
            gains["mla_q_norm_g"][j] = dg[0]
            dckv, dg = _rmsnorm_bwd(s["proj"], 2, MLA_KV_RANK, mla_kv_norm_g[j], dckvn, None, BF16, "rmsnorm_bwd_kv")
            gains["mla_kv_norm_g"][j] = dg[0]
            dproj = jnp.concatenate([dcq, dkr.astype(BF16), dckv, dqc.astype(BF16)], axis=1)
            in_dx = "mm_mla_in_dx"
            out["mla_w_in"] = _split(_mla_in_unpad(_mm(s["hn"], dproj, "tn", F32, "mm_mla_in_dw")), 1)
        else:
            dq, dk, dv, dsink = _swa_attn_bwd(s["proj"], s["o"], do, s["lse"], pos_col, pos_row, slopes, swa_sinks[j],
                                              token)
            gains["swa_sinks"][j] = dsink[::8, 0]
            dproj = jnp.concatenate([dq, dk, dv, dqc], axis=1).astype(BF16)
            in_dx = "mm_swa_in_dx"
            out["swa_w_in"] = _split(_mm(s["hn"], dproj, "tn", F32, "mm_swa_in_dw", pairs="b"), 2)
        dx, dx_b, dg = _mm(dproj, s["w_in"], "nt", F32, in_dx, epi="normbwd", norm=(s["x_in"], attn_norm_g[i], dx1),
                           pairs="" if i % 2 == 0 else "a")
        gains["attn_norm_g"][i] = dg[0]

        leaving[(i, "mixer")], token = _exchange_start([out[n] for n, _ in layer_names(i)[:-4]], True,
                                                       "exchange_grads_mixer_start_%d" % i)

    _, dg_mem = _rmsnorm_bwd(mem0, 0, D_MODEL, mem_norm_g, dmem_n, None, BF16, "rmsnorm_bwd_mem")
    gains = {n: jnp.stack(g) for n, g in gains.items()}
    gains["mem_norm_g"] = dg_mem[0]
    gains["final_norm_g"] = dg_final[0]

    result = {}

    def adamw_of(names, received):
        for n in names:
            parts = [received[(n, l)] for l in range(given[n].shape[0])]
            for kind, r in enumerate(_adamw(parts, given[n], given["m_" + n], given["v_" + n], "adamw_" + n)):
                result[(kind, n)] = r

    received = {}
    for i in reversed(range(DEPTH)):
        got = _exchange_wait(leaving[(i, "main")], dx, "exchange_grads_main_wait_%d" % i)
        received.update(zip(layer_names(i)[-4:], got))
    adamw_of(("mlp_w_up", "mlp_w_down", "w_o", "w_mem_kv"), received)
    for i in reversed(range(DEPTH)):
        got = _exchange_wait(leaving[(i, "mixer")], result[(0, "w_mem_kv")], "exchange_grads_mixer_wait_%d" % i)
        received.update(zip(layer_names(i)[:-4], got))
    adamw_of(("mla_w_in", "mla_w_uq", "mla_w_ukv", "swa_w_in"), received)

    rep_shapes = [given[n].shape for n in REPLICATED]
    rep_parts = _all_gather([_pack([gains[n] for n in REPLICATED], SLOT, 8, F32)], "gather_gain_grads")[0]
    rep_packed = [_pack([given[p + n] for n in REPLICATED], SLOT, 8, F32)[None] for p in ("", "m_", "v_")]
    for kind, r in enumerate(_adamw([rep_parts], *rep_packed, "adamw_gains")):
        for n, part in zip(REPLICATED, _unpack(r[0], rep_shapes)):
            result[(kind, n)] = part

    outs = [loss, dx.reshape(1, seq, D_MODEL)]
    for kind in range(4):
        outs += [result[(kind, n)] for n in WEIGHT_ORDER]
    return tuple(outs)
```

```python
import functools

import jax
import jax.numpy as jnp
from jax import lax
from jax.experimental import pallas as pl
from jax.experimental.pallas import tpu as pltpu

F32 = jnp.float32
BF16 = jnp.bfloat16

D_MODEL = 1024
N_MEM = 256
DEPTH = 4
SLOT = 128
HEAD_DIM = 64
MLA_HEADS = 12
MLA_QK = 96
MLA_Q_RANK = 384
MLA_KV_RANK = 256
SWA_HEADS = 12
SWA_KV_HEADS = 4
SWA_GROUP = 3
MEM_HEADS = 4
WINDOW = 128
EPS = 1e-6
NEG = -1e30
ROPE_THETA = 10000.0
N_DEV = 8

ADAM_LR = 0.001
ADAM_B1 = 0.9
ADAM_B2 = 0.999
ADAM_EPS = 1e-08
ADAM_WD = 0.01
ADAM_STEP = 10

TM = 1024
TM_ROPE = 512
TQ_MLA = 1024
MLA_PACK = 4
SWA_PACK = 4
TQ_CROSS = 2048
MM_VMEM_BUDGET = 38 * 1024 * 1024
ADAM_ROWS = 128
VMEM_LIMIT = 56 * 1024 * 1024

MESH_AXES = ("x", "y", "c")

LOG2_E = 1.4426950408889634
MLA_SCALE = MLA_QK ** -0.5
MLA_Q_SCALE = MLA_SCALE * LOG2_E

MLA_QOFF = (MLA_Q_RANK + SLOT + MLA_KV_RANK) // SLOT
SWA_QOFF = SWA_HEADS + 2 * SWA_KV_HEADS

SHARDED = (
    ("mla_w_in", 1), ("mla_w_uq", 2), ("mla_w_ukv", 2), ("swa_w_in", 2),
    ("w_mem_kv", 1), ("w_o", 1), ("mlp_w_up", 2), ("mlp_w_down", 1),
)
REPLICATED = ("attn_norm_g", "mlp_norm_g", "mem_norm_g", "final_norm_g",
              "mla_q_norm_g", "mla_kv_norm_g", "swa_sinks")
WEIGHT_ORDER = ("attn_norm_g", "mlp_norm_g", "mem_norm_g", "final_norm_g", "mla_w_in",
                "mla_q_norm_g", "mla_kv_norm_g", "mla_w_uq", "mla_w_ukv", "swa_w_in",
                "swa_sinks", "w_mem_kv", "w_o", "mlp_w_up", "mlp_w_down")


def _cparams():
    return pltpu.CompilerParams(vmem_limit_bytes=VMEM_LIMIT)


_DIMS = {"nn": (((1,), (0,)), ((), ())), "nt": (((1,), (1,)), ((), ())), "tn": (((0,), (0,)), ((), ()))}


def _compact(x):
    pairs = [x[:, 2 * j * SLOT:(2 * j + 1) * SLOT] + pltpu.roll(x[:, (2 * j + 1) * SLOT:(2 * j + 2) * SLOT], HEAD_DIM, 1)
             for j in range(x.shape[1] // (2 * SLOT))]
    return pairs[0] if len(pairs) == 1 else jnp.concatenate(pairs, axis=1)


def _expand(x):
    low = lax.broadcasted_iota(jnp.int32, (x.shape[0], SLOT), 1) < HEAD_DIM
    slots = []
    for j in range(x.shape[1] // SLOT):
        pair = x[:, j * SLOT:(j + 1) * SLOT]
        slots += [jnp.where(low, pair, 0.0), pltpu.roll(jnp.where(low, 0.0, pair), HEAD_DIM, 1)]
    return jnp.concatenate(slots, axis=1)


def _mm_tiles(m, n, k, a_bytes, b_bytes, o_bytes, extra_bytes, tm_fixed, tn_fixed):
    best = None
    for tm in ([tm_fixed] if tm_fixed else [t for t in range(4096, 0, -SLOT) if m % t == 0] or [m]):
        for tn in ([tn_fixed] if tn_fixed else [t for t in range(1024, 0, -SLOT) if n % t == 0] or [n]):
            need = 2 * (tm * k * a_bytes + k * tn * b_bytes + tm * tn * (o_bytes + extra_bytes))
            need += tm * tn * 4
            if need <= MM_VMEM_BUDGET and (best is None or tm * tn > best[0] * best[1]):
                best = (tm, tn)
    assert best is not None, (m, n, k)
    return best


def _mm(a, b, mode, out_dtype, name, res=None, aux=None, epi=None, b_blk=None, o_blk=None, after=None, norm=None,
        pairs="", second=None):
    if b_blk is not None:
        nb, br, bc = b.shape
        b_shape = (nb * br, bc) if b_blk == "rows" else (br, nb * bc)
    else:
        b_shape = b.shape
    assert not pairs or (b_blk is None and o_blk is None and not ("b" in pairs and mode == "nt"))
    a_shape = (a.shape[0], a.shape[1] // 2) if "a" in pairs else a.shape
    if "b" in pairs:
        b_shape = (b_shape[0], b_shape[1] // 2)
    if mode == "nn":
        (m, k), (k2, n) = a_shape, b_shape
    elif mode == "nt":
        (m, k), (n, k2) = a_shape, b_shape
    else:
        (k, m), (k2, n) = a_shape, b_shape
    assert k == k2, (a.shape, b_shape, mode)
    assert second is None or (mode == "nn" and b_blk is None and second[0].shape[0] == m and second[1].shape[1] == n)
    k_second = 0 if second is None else second[1].shape[0]
    k_blocked = b_blk is not None and (b_blk == "rows") == (mode != "nt")
    tn_fixed = None
    if b_blk is not None and not k_blocked:
        tn_fixed = br if b_blk == "rows" else bc
    if o_blk == "cols":
        tn_fixed = n // N_DEV
    tm_fixed = m // N_DEV if o_blk == "rows" else None
    has_res, has_aux, has_norm = res is not None, aux is not None, epi == "normbwd"
    assert o_blk is None or not (has_res or has_aux or has_norm)
    n_out = 2 if epi == "relu2" else 1
    if has_norm:
        tn_fixed = n
        o_bytes, extra_bytes = 4 + 2, 4 + 4
    else:
        o_bytes = n_out * jnp.dtype(out_dtype).itemsize
        extra_bytes = (4 if has_res else 0) + (aux.dtype.itemsize if has_aux else 0)
    pa, pb, po = (2 if "a" in pairs else 1), (2 if "b" in pairs else 1), (2 if "o" in pairs else 1)
    tm, tn = _mm_tiles(m, n, k + k_second, a.dtype.itemsize * (3 if pa == 2 else 1),
                       b.dtype.itemsize * (3 if pb == 2 else 1), o_bytes * po, extra_bytes, tm_fixed, tn_fixed)
    dims = _DIMS[mode]
    if mode == "tn":
        a_spec = pl.BlockSpec((k, pa * tm), lambda i, j: (0, i))
    else:
        a_spec = pl.BlockSpec((tm, pa * k), lambda i, j: (i, 0))
    if b_blk is None:
        if mode == "nt":
            b_spec = pl.BlockSpec((tn, k), lambda i, j: (j, 0))
        else:
            b_spec = pl.BlockSpec((k, pb * tn), lambda i, j: (0, j))
    elif k_blocked and mode == "nt":
        b_spec = pl.BlockSpec((N_DEV, tn, bc), lambda i, j: (0, j, 0))
    elif k_blocked:
        b_spec = pl.BlockSpec((N_DEV, br, tn), lambda i, j: (0, 0, j))
    elif mode == "nt":
        b_spec = pl.BlockSpec((None, tn, k), lambda i, j: (j, 0, 0))
    else:
        b_spec = pl.BlockSpec((None, k, tn), lambda i, j: (j, 0, 0))
    if o_blk is None:
        o_spec = pl.BlockSpec((tm, po * tn), lambda i, j: (i, j))
        o_shape = (m, po * n)
    elif o_blk == "rows":
        o_spec = pl.BlockSpec((None, tm, tn), lambda i, j: (i, 0, j))
        o_shape = (N_DEV, tm, n)
    else:
        o_spec = pl.BlockSpec((None, tm, tn), lambda i, j: (j, i, 0))
        o_shape = (N_DEV, m, tn)

    def body(*refs):
        a_ref, b_ref = refs[0], refs[1]
        pos = 2
        res_ref = aux_ref = None
        if has_res:
            res_ref = refs[pos]
            pos += 1
        if has_aux:
            aux_ref = refs[pos]
            pos += 1
        if has_norm:
            x_ref, g_ref, dres_ref = refs[pos:pos + 3]
            pos += 3
        if second is not None:
            a2_ref, b2_ref = refs[pos:pos + 2]
            pos += 2
        if after is not None:
            pos += 1
        outs = refs[pos:]
        if k_blocked and mode == "nt":
            r = None
            for d in range(N_DEV):
                part = lax.dot_general(a_ref[:, d * bc:(d + 1) * bc].astype(BF16), b_ref[d].astype(BF16), dims,
                                       preferred_element_type=F32)
                r = part if r is None else r + part
        else:
            bv = b_ref[...].reshape(k, tn) if k_blocked else b_ref[...]
            av = _compact(a_ref[...].astype(F32)) if pa == 2 else a_ref[...]
            bv = _compact(bv.astype(F32)) if pb == 2 else bv
            r = lax.dot_general(av.astype(BF16), bv.astype(BF16), dims, preferred_element_type=F32)
        if second is not None:
            av2 = _compact(a2_ref[...].astype(F32)) if pa == 2 else a2_ref[...]
            r = r + lax.dot_general(av2.astype(BF16), b2_ref[...].astype(BF16), dims, preferred_element_type=F32)
        if po == 2:
            r = _expand(r)
        if epi == "relu2":
            r = jnp.maximum(r, 0.0)
            outs[0][...] = r.astype(outs[0].dtype)
            outs[1][...] = (r * r).astype(outs[1].dtype)
        elif has_norm:
            xv = x_ref[...]
            rs = lax.rsqrt(jnp.mean(xv * xv, axis=1, keepdims=True) + EPS)
            xh = xv * rs
            dxh = r * g_ref[...]
            dx = rs * (dxh - xh * jnp.mean(dxh * xh, axis=1, keepdims=True)) + dres_ref[...]
            outs[0][...] = dx
            outs[1][...] = dx.astype(BF16)

            @pl.when(pl.program_id(0) == 0)
            def _():
                outs[2][...] = jnp.zeros_like(outs[2])

            outs[2][...] += jnp.sum(r * xh, axis=0, keepdims=True)
        else:
            if epi == "mul2aux":
                r = r * (2.0 * aux_ref[...].astype(F32))
            if has_res:
                r = r + res_ref[...]
            outs[0][...] = r.astype(outs[0].dtype)

    in_specs = [a_spec, b_spec]
    args = [a, b]
    if has_res:
        in_specs.append(o_spec)
        args.append(res)
    if has_aux:
        in_specs.append(o_spec)
        args.append(aux)
    vec_spec = pl.BlockSpec((1, n), lambda i, j: (0, 0))
    if has_norm:
        in_specs += [o_spec, vec_spec, o_spec]
        args += [norm[0], norm[1].reshape(1, n), norm[2]]
    if second is not None:
        in_specs += [pl.BlockSpec((tm, pa * k_second), lambda i, j: (i, 0)),
                     pl.BlockSpec((k_second, tn), lambda i, j: (0, j))]
        args += list(second)
    if after is not None:
        in_specs.append(pl.BlockSpec(memory_space=pl.ANY))
        args.append(after)
    if has_norm:
        out_specs = [o_spec, o_spec, vec_spec]
        out_shape = [jax.ShapeDtypeStruct(o_shape, F32), jax.ShapeDtypeStruct(o_shape, BF16),
                     jax.ShapeDtypeStruct((1, n), F32)]
    else:
        out_specs = [o_spec] * n_out
        out_shape = [jax.ShapeDtypeStruct(o_shape, out_dtype)] * n_out
    out = pl.pallas_call(
        body, name=name, grid=(m // tm, n // tn),
        in_specs=in_specs, out_specs=out_specs, out_shape=out_shape, compiler_params=_cparams(),
    )(*args)
    return out if len(out) > 1 else out[0]


def _rmsnorm_fwd(xarr, colblk, width, g, name, after=None):
    rows = xarr.shape[0]
    tm = min(TM, rows)

    def body(x_ref, g_ref, *rest):
        y_ref = rest[-1]
        x = x_ref[...].astype(F32)
        r = lax.rsqrt(jnp.mean(x * x, axis=1, keepdims=True) + EPS)
        y_ref[...] = (x * r * g_ref[...]).astype(y_ref.dtype)

    in_specs = [pl.BlockSpec((tm, width), lambda i: (i, colblk)), pl.BlockSpec((1, width), lambda i: (0, 0))]
    args = [xarr, g.reshape(1, width)]
    if after is not None:
        in_specs.append(pl.BlockSpec(memory_space=pl.ANY))
        args.append(after)
    return pl.pallas_call(
        body, name=name, grid=(rows // tm,), in_specs=in_specs,
        out_specs=pl.BlockSpec((tm, width), lambda i: (i, 0)),
        out_shape=jax.ShapeDtypeStruct((rows, width), BF16), compiler_params=_cparams(),
    )(*args)


def _rmsnorm_bwd(xarr, colblk, width, g, dy, dres, out_dtype, name):
    rows = xarr.shape[0]
    tm = min(TM, rows)
    has_res = dres is not None

    def body(*refs):
        x_ref, g_ref, dy_ref = refs[0], refs[1], refs[2]
        dres_ref = refs[3] if has_res else None
        dx_ref, dg_ref = refs[-2], refs[-1]
        x = x_ref[...].astype(F32)
        dyv = dy_ref[...].astype(F32)
        r = lax.rsqrt(jnp.mean(x * x, axis=1, keepdims=True) + EPS)
        xh = x * r
        dxh = dyv * g_ref[...]
        dx = r * (dxh - xh * jnp.mean(dxh * xh, axis=1, keepdims=True))
        if has_res:
            dx = dx + dres_ref[...]
        dx_ref[...] = dx.astype(dx_ref.dtype)

        @pl.when(pl.program_id(0) == 0)
        def _():
            dg_ref[...] = jnp.zeros_like(dg_ref)

        dg_ref[...] += jnp.sum(dyv * xh, axis=0, keepdims=True)

    row_spec = pl.BlockSpec((tm, width), lambda i: (i, 0))
    vec_spec = pl.BlockSpec((1, width), lambda i: (0, 0))
    in_specs = [pl.BlockSpec((tm, width), lambda i: (i, colblk)), vec_spec, row_spec]
    args = [xarr, g.reshape(1, width), dy]
    if has_res:
        in_specs.append(row_spec)
        args.append(dres)
    return pl.pallas_call(
        body, name=name, grid=(rows // tm,), in_specs=in_specs, out_specs=[row_spec, vec_spec],
        out_shape=[jax.ShapeDtypeStruct((rows, width), out_dtype), jax.ShapeDtypeStruct((1, width), F32)],
        compiler_params=_cparams(),
    )(*args)


def _loss_head(x, g, tgt):
    rows, width = x.shape
    tm = min(TM, rows)

    def body(x_ref, g_ref, t_ref, dx_ref, dxb_ref, dg_ref, loss_ref):
        xv = x_ref[...]
        gv = g_ref[...]
        r = lax.rsqrt(jnp.mean(xv * xv, axis=1, keepdims=True) + EPS)
        xh = xv * r
        err = xh * gv - t_ref[...]
        part = 0.5 * jnp.sum(jnp.mean(err * err, axis=1, keepdims=True), axis=0, keepdims=True)
        dyv = err * (1.0 / width)
        dxh = dyv * gv
        dxv = r * (dxh - xh * jnp.mean(dxh * xh, axis=1, keepdims=True))
        dx_ref[...] = dxv
        dxb_ref[...] = dxv.astype(BF16)

        @pl.when(pl.program_id(0) == 0)
        def _():
            dg_ref[...] = jnp.zeros_like(dg_ref)
            loss_ref[...] = jnp.zeros_like(loss_ref)

        dg_ref[...] += jnp.sum(dyv * xh, axis=0, keepdims=True)
        loss_ref[...] += jnp.broadcast_to(part, loss_ref.shape)

    row_spec = pl.BlockSpec((tm, width), lambda i: (i, 0))
    vec_spec = pl.BlockSpec((1, width), lambda i: (0, 0))
    return pl.pallas_call(
        body, name="loss_head", grid=(rows // tm,), in_specs=[row_spec, vec_spec, row_spec],
        out_specs=[row_spec, row_spec, vec_spec, pl.BlockSpec((1, SLOT), lambda i: (0, 0))],
        out_shape=[jax.ShapeDtypeStruct((rows, width), F32), jax.ShapeDtypeStruct((rows, width), BF16),
                   jax.ShapeDtypeStruct((1, width), F32), jax.ShapeDtypeStruct((1, SLOT), F32)],
        compiler_params=_cparams(),
    )(x, g.reshape(1, width), tgt)


def _lane_consts():
    half = 16
    inv = ROPE_THETA ** (-(jnp.arange(half, dtype=F32) * 2.0) / 32)
    lane = jnp.arange(SLOT)
    first = (lane >= 64) & (lane < 80)
    second = (lane >= 80) & (lane < 96)
    inv_lane = jnp.where(first | second, inv[(lane - 64) % half], 0.0)
    rows = [inv_lane, (lane < 64).astype(F32), first.astype(F32), second.astype(F32)]
    rows += [jnp.zeros((SLOT,), F32)] * 4
    return jnp.stack(rows).astype(F32)


def _rope_tables(pos_col, consts):
    rows = pos_col.shape[0]
    tm = min(TM, rows)

    def body(p_ref, k_ref, c_ref, s1_ref, s2_ref):
        ang = p_ref[...] * k_ref[0:1, :]
        cos, sin = jnp.cos(ang), jnp.sin(ang)
        first, second = k_ref[2:3, :], k_ref[3:4, :]
        c_ref[...] = k_ref[1:2, :] + (first + second) * cos
        s1_ref[...] = -first * sin
        s2_ref[...] = second * sin

    spec = pl.BlockSpec((tm, SLOT), lambda i: (i, 0))
    shp = jax.ShapeDtypeStruct((rows, SLOT), F32)
    return pl.pallas_call(
        body, name="rope_tables", grid=(rows // tm,),
        in_specs=[pl.BlockSpec((tm, 1), lambda i: (i, 0)), pl.BlockSpec((8, SLOT), lambda i: (0, 0))],
        out_specs=[spec, spec, spec], out_shape=[shp, shp, shp], compiler_params=_cparams(),
    )(pos_col, consts)


def _rot(xv, c, s1, s2):
    return xv * c + pltpu.roll(xv, SLOT - 16, 1) * s1 + pltpu.roll(xv, 16, 1) * s2


def _rot_t(dy, c, s1, s2):
    return dy * c + pltpu.roll(dy * s1, 16, 1) + pltpu.roll(dy * s2, SLOT - 16, 1)


def _mla_rope_fwd(qraw, kvraw, proj, tabs):
    rows = qraw.shape[0]
    tm = min(TM_ROPE, rows)
    hw = MLA_HEADS * SLOT

    def body(q_ref, kv_ref, kr_ref, c_ref, s1_ref, s2_ref, qo, ko, vo):
        c, s1, s2 = c_ref[...], s1_ref[...], s2_ref[...]
        kr = _rot(kr_ref[...], c, s1, s2)
        low = lax.broadcasted_iota(jnp.int32, (tm, SLOT), 1) < HEAD_DIM
        for h in range(MLA_HEADS):
            sl = slice(h * SLOT, (h + 1) * SLOT)
            qo[:, sl] = (_rot(q_ref[:, sl], c, s1, s2) * MLA_Q_SCALE).astype(BF16)
            kvh = kv_ref[:, sl]
            ko[:, sl] = (jnp.where(low, kvh, 0.0) + kr).astype(BF16)
            vo[:, sl] = pltpu.roll(jnp.where(low, 0.0, kvh), HEAD_DIM, 1).astype(BF16)

    tab = pl.BlockSpec((tm, SLOT), lambda i: (i, 0))
    wide = pl.BlockSpec((tm, hw), lambda i: (i, 0))
    shp = jax.ShapeDtypeStruct((rows, hw), BF16)
    return pl.pallas_call(
        body, name="mla_rope_fwd", grid=(rows // tm,),
        in_specs=[wide, wide, pl.BlockSpec((tm, SLOT), lambda i: (i, 3)),
                  tab, tab, tab],
        out_specs=[wide, wide, wide], out_shape=[shp, shp, shp], compiler_params=_cparams(),
    )(qraw, kvraw, proj, *tabs)


def _mla_rope_bwd(dq, dk, dv, tabs, consts):
    rows = dq.shape[0]
    tm = min(TM_ROPE, rows)
    hw = MLA_HEADS * SLOT

    def body(dq_ref, dk_ref, dv_ref, c_ref, s1_ref, s2_ref, k_ref, dqo, dkvo, dkro):
        c, s1, s2 = c_ref[...], s1_ref[...], s2_ref[...]
        ksum = jnp.zeros((tm, SLOT), F32)
        low = lax.broadcasted_iota(jnp.int32, (tm, SLOT), 1) < HEAD_DIM
        for h in range(MLA_HEADS):
            sl = slice(h * SLOT, (h + 1) * SLOT)
            dqo[:, sl] = _rot_t(dq_ref[:, sl], c, s1, s2).astype(BF16)
            dkh = dk_ref[:, sl]
            ksum = ksum + dkh
            dvh = pltpu.roll(jnp.where(low, dv_ref[:, sl], 0.0), HEAD_DIM, 1)
            dkvo[:, sl] = (jnp.where(low, dkh, 0.0) + dvh).astype(BF16)
        dkro[...] = _rot_t(ksum, c, s1, s2) * (k_ref[2:3, :] + k_ref[3:4, :])

    tab = pl.BlockSpec((tm, SLOT), lambda i: (i, 0))
    wide = pl.BlockSpec((tm, hw), lambda i: (i, 0))
    return pl.pallas_call(
        body, name="mla_rope_bwd", grid=(rows // tm,),
        in_specs=[wide, wide, wide, tab, tab, tab, pl.BlockSpec((8, SLOT), lambda i: (0, 0))],
        out_specs=[wide, wide, tab],
        out_shape=[jax.ShapeDtypeStruct((rows, hw), BF16), jax.ShapeDtypeStruct((rows, hw), BF16),
                   jax.ShapeDtypeStruct((rows, SLOT), F32)],
        compiler_params=_cparams(),
    )(dq, dk, dv, *tabs, consts)


def _nt(a, b):
    return lax.dot_general(a, b, _DIMS["nt"], preferred_element_type=F32)


def _tn(a, b):
    return lax.dot_general(a, b, _DIMS["tn"], preferred_element_type=F32)


def _nn(a, b):
    return lax.dot_general(a, b, _DIMS["nn"], preferred_element_type=F32)


def _mla_attn_fwd(q, k, v):
    rows = q.shape[0]
    t = min(TQ_MLA, rows)
    nt = rows // t
    wide = MLA_PACK * SLOT

    def body(q_ref, k_ref, v_ref, o_ref, lse_ref, m_sc, l_sc, acc_sc):
        i, j = pl.program_id(1), pl.program_id(2)

        @pl.when(j == 0)
        def _():
            m_sc[...] = jnp.full_like(m_sc, NEG)
            l_sc[...] = jnp.zeros_like(l_sc)
            acc_sc[...] = jnp.zeros_like(acc_sc)

        def step(diagonal):
            for hh in range(MLA_PACK):
                sl = slice(hh * SLOT, (hh + 1) * SLOT)
                s = _nt(k_ref[:, sl], q_ref[:, sl])
                if diagonal:
                    key = lax.broadcasted_iota(jnp.int32, (t, t), 0)
                    s = jnp.where(key <= lax.broadcasted_iota(jnp.int32, (t, t), 1), s, NEG)
                m_prev = m_sc[hh]
                m_new = jnp.maximum(m_prev, jnp.max(s, axis=0, keepdims=True))
                p = jnp.exp2(s - m_new)
                alpha = jnp.exp2(m_prev - m_new)
                l_new = alpha * l_sc[hh] + jnp.sum(p, axis=0, keepdims=True)
                acc = alpha * acc_sc[hh] + _tn(v_ref[:, sl], p.astype(BF16))
                if diagonal:
                    o_ref[:, sl] = (acc / l_new).T.astype(o_ref.dtype)
                    lse_ref[hh:hh + 1, :] = m_new + jnp.log(l_new) * LOG2_E
                else:
                    m_sc[hh] = m_new
                    l_sc[hh] = l_new
                    acc_sc[hh] = acc

        @pl.when(j < i)
        def _():
            step(False)

        @pl.when(j == i)
        def _():
            lse_ref[...] = jnp.zeros_like(lse_ref)
            step(True)

    q_spec = pl.BlockSpec((t, wide), lambda h, i, j: (i, h))
    kv_spec = pl.BlockSpec((t, wide), lambda h, i, j: (jnp.minimum(j, i), h))
    return pl.pallas_call(
        body, name="mla_attn_fwd", grid=(MLA_HEADS // MLA_PACK, nt, nt),
        in_specs=[q_spec, kv_spec, kv_spec],
        out_specs=[q_spec, pl.BlockSpec((None, 8, t), lambda h, i, j: (h, 0, i))],
        out_shape=[jax.ShapeDtypeStruct(q.shape, BF16),
                   jax.ShapeDtypeStruct((MLA_HEADS // MLA_PACK, 8, rows), F32)],
        scratch_shapes=[pltpu.VMEM((MLA_PACK, 1, t), F32), pltpu.VMEM((MLA_PACK, 1, t), F32),
                        pltpu.VMEM((MLA_PACK, SLOT, t), F32)],
        compiler_params=_cparams(),
    )(q, k, v)


def _mla_delta(o, do):
    rows = o.shape[0]
    t = rows
    wide = MLA_PACK * SLOT

    def body(o_ref, do_ref, d_ref):
        d_ref[...] = jnp.zeros_like(d_ref)
        ones = jnp.ones((8, SLOT), BF16)
        for hh in range(MLA_PACK):
            sl = slice(hh * SLOT, (hh + 1) * SLOT)
            prod = do_ref[:, sl].astype(F32) * o_ref[:, sl].astype(F32)
            high = prod.astype(BF16)
            low = (prod - high.astype(F32)).astype(BF16)
            d_ref[hh:hh + 1, :] = (_nt(ones, high) + _nt(ones, low))[0:1, :]

    spec = pl.BlockSpec((t, wide), lambda h, i: (i, h))
    return pl.pallas_call(
        body, name="mla_delta", grid=(MLA_HEADS // MLA_PACK, rows // t), in_specs=[spec, spec],
        out_specs=pl.BlockSpec((None, 8, t), lambda h, i: (h, 0, i)),
        out_shape=jax.ShapeDtypeStruct((MLA_HEADS // MLA_PACK, 8, rows), F32), compiler_params=_cparams(),
    )(o, do)


def _mla_attn_bwd(q, k, v, do, lse, delta, after):
    rows = q.shape[0]
    t = min(TQ_MLA, rows)
    nt = rows // t
    wide = MLA_PACK * SLOT

    def body(q_ref, k_ref, v_ref, do_ref, lse_ref, delta_ref, after_ref, dq_ref, dk_ref, dv_ref, dk_sc, dv_sc):
        j, i = pl.program_id(1), pl.program_id(2)

        @pl.when((j == 0) & (i == 0))
        def _():
            dq_ref[...] = jnp.zeros_like(dq_ref)

        @pl.when(i == 0)
        def _():
            dk_sc[...] = jnp.zeros_like(dk_sc)
            dv_sc[...] = jnp.zeros_like(dv_sc)

        def chunk(hh, rows, keys, masked):
            sl = slice(hh * SLOT, (hh + 1) * SLOT)
            n_rows = rows.stop - rows.start
            qv, kv, dov = q_ref[rows, sl], k_ref[keys, sl], do_ref[rows, sl]
            s = _nt(kv, qv)
            if masked:
                shp = (keys.stop - keys.start, n_rows)
                s = jnp.where(keys.start + lax.broadcasted_iota(jnp.int32, shp, 0)
                              <= rows.start + lax.broadcasted_iota(jnp.int32, shp, 1), s, NEG)
            p = jnp.exp2(s - lse_ref[hh:hh + 1, rows])
            dp = _nt(v_ref[keys, sl], dov)
            ds = (p * (dp - delta_ref[hh:hh + 1, rows])).astype(BF16)
            dv_sc[keys, sl] += _nn(p.astype(BF16), dov)
            dk_sc[keys, sl] += _nn(ds, qv)
            r0 = pl.multiple_of(i * t + rows.start, n_rows)
            dq_ref[pl.ds(r0, n_rows), sl] += _tn(ds, kv) * MLA_SCALE

        @pl.when(i > j)
        def _():
            for hh in range(MLA_PACK):
                chunk(hh, slice(0, t), slice(0, t), False)

        @pl.when(i == j)
        def _():
            for hh in range(MLA_PACK):
                chunk(hh, slice(0, t), slice(0, t // 2), True)
                chunk(hh, slice(t // 2, t), slice(t // 2, t), True)

        @pl.when(i == nt - 1)
        def _():
            dk_ref[...] = dk_sc[...] * (1.0 / LOG2_E)
            dv_ref[...] = dv_sc[...]

    q_spec = pl.BlockSpec((t, wide), lambda h, j, i: (jnp.maximum(i, j), h))
    kv_spec = pl.BlockSpec((t, wide), lambda h, j, i: (j, h))
    row_spec = pl.BlockSpec((None, 8, t), lambda h, j, i: (h, 0, jnp.maximum(i, j)))
    head_spec = pl.BlockSpec((rows, wide), lambda h, j, i: (0, h))
    shp = jax.ShapeDtypeStruct(q.shape, F32)
    return pl.pallas_call(
        body, name="mla_attn_bwd", grid=(MLA_HEADS // MLA_PACK, nt, nt),
        in_specs=[q_spec, kv_spec, kv_spec, q_spec, row_spec, row_spec, pl.BlockSpec(memory_space=pl.ANY)],
        out_specs=[head_spec, kv_spec, kv_spec], out_shape=[shp, shp, shp],
        scratch_shapes=[pltpu.VMEM((t, wide), F32), pltpu.VMEM((t, wide), F32)],
        compiler_params=_cparams(),
    )(q, k, v, do, lse, delta, after)


def _swa_specs(t):
    def prev(i):
        return jnp.maximum(i - 1, 0)
    kw = SWA_PACK * SLOT
    k0, v0 = SWA_HEADS // SWA_PACK, (SWA_HEADS + SWA_KV_HEADS) // SWA_PACK
    q3 = pl.BlockSpec((t, SWA_PACK * SWA_GROUP * SLOT), lambda h, i: (i, h))
    kp = pl.BlockSpec((t, kw), lambda h, i: (prev(i), k0 + h))
    kc = pl.BlockSpec((t, kw), lambda h, i: (i, k0 + h))
    vp = pl.BlockSpec((t, kw), lambda h, i: (prev(i), v0 + h))
    vc = pl.BlockSpec((t, kw), lambda h, i: (i, v0 + h))
    pcol = pl.BlockSpec((t, 1), lambda h, i: (i, 0))
    prow_p = pl.BlockSpec((1, t), lambda h, i: (0, prev(i)))
    prow_c = pl.BlockSpec((1, t), lambda h, i: (0, i))
    return [q3, kp, kc, vp, vc, pcol, prow_p, prow_c]


def _stack(ref, first):
    return jnp.concatenate([ref[:, (first + g) * SLOT:(first + g + 1) * SLOT] for g in range(SWA_GROUP)], axis=0)


def _swa_logits(q3, kp, kc, pq, pkp, pkc, slope_ref, kvh, i, t):
    r = lax.broadcasted_iota(jnp.int32, (t, t), 0)
    c = lax.broadcasted_iota(jnp.int32, (t, t), 1)
    ok_c = c <= r
    ok_p = (c - r) > jnp.where(i > 0, 0, t)
    dist_p, dist_c = pq - pkp, pq - pkc
    s_p3 = _nt(q3, kp) * (HEAD_DIM ** -0.5)
    s_c3 = _nt(q3, kc) * (HEAD_DIM ** -0.5)
    out = []
    for g in range(SWA_GROUP):
        slope = slope_ref[kvh * SWA_GROUP + g]
        rows = slice(g * t, (g + 1) * t)
        out.append((jnp.where(ok_p, s_p3[rows] - slope * dist_p, NEG),
                    jnp.where(ok_c, s_c3[rows] - slope * dist_c, NEG)))
    return out


def _swa_attn_fwd(proj, pos_col, pos_row, slopes, sinks):
    rows = proj.shape[0]
    t = WINDOW
    hw = SWA_HEADS * SLOT

    def body(slope_ref, sink_ref, q_ref, kp_ref, kc_ref, vp_ref, vc_ref, pq_ref, pkp_ref, pkc_ref, o_ref, lse_ref):
        i = pl.program_id(1)
        for kv in range(SWA_PACK):
            kvh = pl.program_id(0) * SWA_PACK + kv
            ksl = slice(kv * SLOT, (kv + 1) * SLOT)
            logits = _swa_logits(_stack(q_ref, kv * SWA_GROUP), kp_ref[:, ksl], kc_ref[:, ksl], pq_ref[...],
                                 pkp_ref[...], pkc_ref[...], slope_ref, kvh, i, t)
            e_p, e_c, norm = [], [], []
            for g, (s_p, s_c) in enumerate(logits):
                sl = slice((kv * SWA_GROUP + g) * SLOT, (kv * SWA_GROUP + g + 1) * SLOT)
                sink = sink_ref[kvh * SWA_GROUP + g]
                m = jnp.maximum(jnp.maximum(jnp.max(s_p, axis=1, keepdims=True),
                                            jnp.max(s_c, axis=1, keepdims=True)), sink)
                ep, ec = jnp.exp(s_p - m), jnp.exp(s_c - m)
                l = jnp.sum(ep, axis=1, keepdims=True) + jnp.sum(ec, axis=1, keepdims=True) + jnp.exp(sink - m)
                e_p.append(ep.astype(BF16))
                e_c.append(ec.astype(BF16))
                norm.append(l)
                lse_ref[:, sl] = jnp.broadcast_to(m + jnp.log(l), (t, SLOT))
            acc = (_nn(jnp.concatenate(e_p, axis=0), vp_ref[:, ksl])
                   + _nn(jnp.concatenate(e_c, axis=0), vc_ref[:, ksl]))
            for g in range(SWA_GROUP):
                sl = slice((kv * SWA_GROUP + g) * SLOT, (kv * SWA_GROUP + g + 1) * SLOT)
                o_ref[:, sl] = (acc[g * t:(g + 1) * t] / norm[g]).astype(o_ref.dtype)

    smem = pl.BlockSpec(memory_space=pltpu.SMEM)
    out_spec = pl.BlockSpec((t, SWA_PACK * SWA_GROUP * SLOT), lambda h, i: (i, h))
    return pl.pallas_call(
        body, name="swa_attn_fwd", grid=(SWA_KV_HEADS // SWA_PACK, rows // t),
        in_specs=[smem, smem] + _swa_specs(t), out_specs=[out_spec, out_spec],
        out_shape=[jax.ShapeDtypeStruct((rows, hw), BF16), jax.ShapeDtypeStruct((rows, hw), F32)],
        compiler_params=_cparams(),
    )(slopes, sinks, proj, proj, proj, proj, proj, pos_col, pos_row, pos_row)


def _swa_attn_bwd(proj, o, do, lse, pos_col, pos_row, slopes, sinks, after):
    rows = proj.shape[0]
    t = WINDOW
    hw = SWA_HEADS * SLOT
    scale = HEAD_DIM ** -0.5

    def body(slope_ref, sink_ref, q_ref, kp_ref, kc_ref, vp_ref, vc_ref, pq_ref, pkp_ref, pkc_ref,
             o_ref, do_ref, lse_ref, after_ref, dq_ref, dk_ref, dv_ref, dsink_ref):
        i = pl.program_id(1)

        @pl.when(i == 0)
        def _():
            dk_ref[...] = jnp.zeros_like(dk_ref)
            dv_ref[...] = jnp.zeros_like(dv_ref)
            dsink_ref[...] = jnp.zeros_like(dsink_ref)

        r_c = pl.multiple_of(i * t, t)
        r_p = pl.multiple_of(jnp.maximum(i - 1, 0) * t, t)
        for kv in range(SWA_PACK):
            kvh = pl.program_id(0) * SWA_PACK + kv
            ksl = slice(kv * SLOT, (kv + 1) * SLOT)
            q3, do3 = _stack(q_ref, kv * SWA_GROUP), _stack(do_ref, kv * SWA_GROUP)
            logits = _swa_logits(q3, kp_ref[:, ksl], kc_ref[:, ksl], pq_ref[...], pkp_ref[...], pkc_ref[...],
                                 slope_ref, kvh, i, t)
            dp_p3, dp_c3 = _nt(do3, vp_ref[:, ksl]), _nt(do3, vc_ref[:, ksl])
            p_p, p_c, ds_p, ds_c = [], [], [], []
            for g, (s_p, s_c) in enumerate(logits):
                head = kv * SWA_GROUP + g
                sl = slice(head * SLOT, (head + 1) * SLOT)
                rws = slice(g * t, (g + 1) * t)
                lse_g = lse_ref[:, head * SLOT:head * SLOT + 1]
                pp, pc = jnp.exp(s_p - lse_g), jnp.exp(s_c - lse_g)
                delta = jnp.sum(do_ref[:, sl].astype(F32) * o_ref[:, sl].astype(F32), axis=1, keepdims=True)
                p_p.append(pp.astype(BF16))
                p_c.append(pc.astype(BF16))
                ds_p.append((pp * (dp_p3[rws] - delta)).astype(BF16))
                ds_c.append((pc * (dp_c3[rws] - delta)).astype(BF16))
                sink = sink_ref[kvh * SWA_GROUP + g]
                dsink = -jnp.sum(jnp.exp(sink - lse_g) * delta, axis=0, keepdims=True)
                dsink_ref[head * 8:(head + 1) * 8, :] += jnp.broadcast_to(dsink, (8, SLOT))
            p_p3, p_c3 = jnp.concatenate(p_p, axis=0), jnp.concatenate(p_c, axis=0)
            ds_p3, ds_c3 = jnp.concatenate(ds_p, axis=0), jnp.concatenate(ds_c, axis=0)
            dq3 = (_nn(ds_p3, kp_ref[:, ksl]) + _nn(ds_c3, kc_ref[:, ksl])) * scale
            for g in range(SWA_GROUP):
                head = kv * SWA_GROUP + g
                dq_ref[:, head * SLOT:(head + 1) * SLOT] = dq3[g * t:(g + 1) * t]
            dk_ref[pl.ds(r_c, t), ksl] += _tn(ds_c3, q3) * scale
            dv_ref[pl.ds(r_c, t), ksl] += _tn(p_c3, do3)
            dk_ref[pl.ds(r_p, t), ksl] += _tn(ds_p3, q3) * scale
            dv_ref[pl.ds(r_p, t), ksl] += _tn(p_p3, do3)

    smem = pl.BlockSpec(memory_space=pltpu.SMEM)
    qlike = pl.BlockSpec((t, SWA_PACK * SWA_GROUP * SLOT), lambda h, i: (i, h))
    kv_out = pl.BlockSpec((rows, SWA_PACK * SLOT), lambda h, i: (0, h))
    return pl.pallas_call(
        body, name="swa_attn_bwd", grid=(SWA_KV_HEADS // SWA_PACK, rows // t),
        in_specs=[smem, smem] + _swa_specs(t) + [qlike, qlike, qlike, pl.BlockSpec(memory_space=pl.ANY)],
        out_specs=[qlike, kv_out, kv_out,
                   pl.BlockSpec((SWA_PACK * SWA_GROUP * 8, SLOT), lambda h, i: (h, 0))],
        out_shape=[jax.ShapeDtypeStruct((rows, hw), F32), jax.ShapeDtypeStruct((rows, SWA_KV_HEADS * SLOT), F32),
                   jax.ShapeDtypeStruct((rows, SWA_KV_HEADS * SLOT), F32),
                   jax.ShapeDtypeStruct((SWA_HEADS * 8, SLOT), F32)],
        compiler_params=_cparams(),
    )(slopes, sinks, proj, proj, proj, proj, proj, pos_col, pos_row, pos_row, o, do, lse, after)


def _cross_attn_fwd(proj, qoff, kvmem):
    rows = proj.shape[0]
    t = min(TQ_CROSS, rows)

    def body(q_ref, k_ref, v_ref, o_ref):
        s = _nt(k_ref[...], q_ref[...].astype(BF16)) * (HEAD_DIM ** -0.5)
        e = jnp.exp(s - jnp.max(s, axis=0, keepdims=True))
        p = e / jnp.sum(e, axis=0, keepdims=True)
        o_ref[...] = _tn(v_ref[...], p.astype(BF16)).T.astype(o_ref.dtype)

    return pl.pallas_call(
        body, name="cross_attn_fwd", grid=(rows // t, MEM_HEADS),
        in_specs=[pl.BlockSpec((t, SLOT), lambda i, h: (i, qoff + h)),
                  pl.BlockSpec((N_MEM, SLOT), lambda i, h: (0, h)),
                  pl.BlockSpec((N_MEM, SLOT), lambda i, h: (0, MEM_HEADS + h))],
        out_specs=pl.BlockSpec((t, SLOT), lambda i, h: (i, h)),
        out_shape=jax.ShapeDtypeStruct((rows, MEM_HEADS * SLOT), BF16), compiler_params=_cparams(),
    )(proj, kvmem, kvmem)


def _cross_attn_bwd(proj, qoff, kvmem, do, do_off):
    rows = proj.shape[0]
    t = min(TQ_CROSS, rows)
    scale = HEAD_DIM ** -0.5

    def body(q_ref, k_ref, v_ref, do_ref, dq_ref, dk_ref, dv_ref):
        @pl.when(pl.program_id(1) == 0)
        def _():
            dk_ref[...] = jnp.zeros_like(dk_ref)
            dv_ref[...] = jnp.zeros_like(dv_ref)

        qv, kv, dov = q_ref[...].astype(BF16), k_ref[...], do_ref[...]
        s = _nt(kv, qv) * scale
        e = jnp.exp(s - jnp.max(s, axis=0, keepdims=True))
        p = e / jnp.sum(e, axis=0, keepdims=True)
        dp = _nt(v_ref[...], dov)
        ds = (p * (dp - jnp.sum(p * dp, axis=0, keepdims=True))).astype(BF16)
        dq_ref[...] = _tn(ds, kv) * scale
        dk_ref[...] += _nn(ds, qv) * scale
        dv_ref[...] += _nn(p.astype(BF16), dov)

    mem_out = pl.BlockSpec((N_MEM, SLOT), lambda h, i: (0, h))
    return pl.pallas_call(
        body, name="cross_attn_bwd", grid=(MEM_HEADS, rows // t),
        in_specs=[pl.BlockSpec((t, SLOT), lambda h, i: (i, qoff + h)),
                  pl.BlockSpec((N_MEM, SLOT), lambda h, i: (0, h)),
                  pl.BlockSpec((N_MEM, SLOT), lambda h, i: (0, MEM_HEADS + h)),
                  pl.BlockSpec((t, SLOT), lambda h, i: (i, do_off + h))],
        out_specs=[pl.BlockSpec((t, SLOT), lambda h, i: (i, h)), mem_out, mem_out],
        out_shape=[jax.ShapeDtypeStruct((rows, MEM_HEADS * SLOT), F32),
                   jax.ShapeDtypeStruct((N_MEM, MEM_HEADS * SLOT), F32),
                   jax.ShapeDtypeStruct((N_MEM, MEM_HEADS * SLOT), F32)],
        compiler_params=_cparams(),
    )(proj, kvmem, kvmem, do)


def _place():
    return lax.axis_index("x"), lax.axis_index("y"), lax.axis_index("c")


def _flip(v, bit):
    return 1 - v if bit else v


def _all_gather(blocks, name):
    nb = len(blocks)

    def body(*refs):
        x_refs, out_refs = refs[:nb], refs[nb:2 * nb]
        send_sems, recv_sems, local_sems = refs[2 * nb:]
        x, y, c = _place()
        me, sibling = (x, y, c), (x, y, 1 - c)
        chips = [(1 - x, y), (x, 1 - y), (1 - x, 1 - y)]

        def copy(b, k, blk, to, from_input=False):
            slot = out_refs[b].at[4 * blk[0] + 2 * blk[1] + blk[2]]
            return pltpu.make_async_remote_copy(
                src_ref=x_refs[b] if from_input else slot, dst_ref=slot,
                send_sem=send_sems.at[b, k], recv_sem=recv_sems.at[b, k],
                device_id=to, device_id_type=pl.DeviceIdType.MESH)

        mine = [pltpu.make_async_copy(x_refs[b], out_refs[b].at[4 * x + 2 * y + c], local_sems.at[b])
                for b in range(nb)]
        for cp in mine:
            cp.start()
        first = []
        for b in range(nb):
            first.append(copy(b, 0, me, sibling, from_input=True))
            first += [copy(b, 1 + n, me, (*chip, c), from_input=True) for n, chip in enumerate(chips)]
        for cp in first:
            cp.start()
        passed = []
        for n, chip in enumerate(chips):
            for b in range(nb):
                copy(b, 1 + n, (*chip, c), me).wait_recv()
                passed.append(copy(b, 4 + n, (*chip, c), sibling))
                passed[-1].start()
        for b in range(nb):
            copy(b, 0, sibling, me).wait_recv()
            for n, chip in enumerate(chips):
                copy(b, 4 + n, (*chip, 1 - c), me).wait_recv()
        for cp in first + passed:
            cp.wait_send()
        for cp in mine:
            cp.wait()

    any_spec = pl.BlockSpec(memory_space=pl.ANY)
    return pl.pallas_call(
        body, name=name, in_specs=[any_spec] * nb, out_specs=[any_spec] * nb,
        out_shape=[jax.ShapeDtypeStruct((N_DEV,) + blk.shape, blk.dtype) for blk in blocks],
        scratch_shapes=[pltpu.SemaphoreType.DMA((nb, 7)), pltpu.SemaphoreType.DMA((nb, 7)),
                        pltpu.SemaphoreType.DMA((nb,))],
    )(*blocks)


def _peers(x, y, c):
    out = []
    for n in range(1, N_DEV):
        peer = (_flip(x, n & 4), _flip(y, n & 2), _flip(c, n & 1))
        out.append((n - 1, peer, 4 * peer[0] + 2 * peer[1] + peer[2]))
    return out


_HBM = pl.BlockSpec(memory_space=pltpu.HBM)
_SEM = pl.BlockSpec(memory_space=pltpu.SEMAPHORE)


def _exchange_start(srcs, scatter, name, after=None):
    ns = len(srcs)
    lands = [lax.empty(s.shape if scatter else (N_DEV,) + s.shape, s.dtype) for s in srcs]

    def body(*refs):
        src_refs, land_refs = refs[:ns], refs[ns:2 * ns]
        pos = 2 * ns + (1 if after is not None else 0)
        send_sems, recv_sems, token = refs[pos], refs[pos + 1], refs[-1]
        x, y, c = _place()
        my_idx = 4 * x + 2 * y + c
        for col, peer, peer_idx in _peers(x, y, c):
            for b in range(ns):
                pltpu.make_async_remote_copy(
                    src_ref=src_refs[b].at[peer_idx] if scatter else src_refs[b], dst_ref=land_refs[b].at[my_idx],
                    send_sem=send_sems.at[b * (N_DEV - 1) + col], recv_sem=recv_sems.at[b * (N_DEV - 1) + col],
                    device_id=peer, device_id_type=pl.DeviceIdType.MESH).start()
        token[...] = jnp.zeros_like(token)

    args = [pltpu.with_memory_space_constraint(a, pltpu.HBM) for a in list(srcs) + lands]
    in_specs = [_HBM] * (2 * ns)
    if after is not None:
        args.append(after)
        in_specs.append(pl.BlockSpec(memory_space=pl.ANY))
    out = pl.pallas_call(
        body, name=name, in_specs=in_specs,
        out_specs=[_SEM, _SEM] + [_HBM] * (2 * ns) + [pl.BlockSpec(memory_space=pltpu.VMEM)],
        out_shape=[pltpu.SemaphoreType.DMA((ns * (N_DEV - 1),)), pltpu.SemaphoreType.DMA((ns * (N_DEV - 1),))]
        + [pltpu.HBM(a.shape, a.dtype) for a in list(srcs) + lands] + [jax.ShapeDtypeStruct((8, SLOT), F32)],
        input_output_aliases={k: 2 + k for k in range(2 * ns)},
        compiler_params=pltpu.CompilerParams(has_side_effects=pltpu.SideEffectType.DATAFLOW_SIDE_EFFECTING),
    )(*args)
    return (out[0], out[1], out[2:2 + ns], out[2 + ns:2 + 2 * ns], scatter), out[-1]


def _exchange_wait(handle, after, name):
    send_sems, recv_sems, srcs, lands, scatter = handle
    ns = len(srcs)

    def body(*refs):
        src_refs, land_refs = refs[:ns], refs[ns:2 * ns]
        send_ref, recv_ref = refs[2 * ns], refs[2 * ns + 1]
        x, y, c = _place()
        for col, peer, peer_idx in _peers(x, y, c):
            for b in range(ns):
                copy = pltpu.make_async_remote_copy(
                    src_ref=src_refs[b].at[peer_idx] if scatter else src_refs[b], dst_ref=land_refs[b].at[peer_idx],
                    send_sem=send_ref.at[b * (N_DEV - 1) + col], recv_sem=recv_ref.at[b * (N_DEV - 1) + col],
                    device_id=peer, device_id_type=pl.DeviceIdType.MESH)
                copy.wait_send()
                copy.wait_recv()

    out = pl.pallas_call(
        body, name=name, in_specs=[_HBM] * (2 * ns) + [_SEM, _SEM, pl.BlockSpec(memory_space=pl.ANY)],
        out_specs=[_HBM] * (2 * ns),
        out_shape=[pltpu.HBM(a.shape, a.dtype) for a in list(srcs) + list(lands)],
        input_output_aliases={k: k for k in range(2 * ns)},
        compiler_params=pltpu.CompilerParams(has_side_effects=pltpu.SideEffectType.DATAFLOW_SIDE_EFFECTING),
    )(*srcs, *lands, send_sems, recv_sems, after)
    my_idx = 4 * lax.axis_index("x") + 2 * lax.axis_index("y") + lax.axis_index("c")
    landed = []
    for src, land in zip(out[:ns], out[ns:]):
        own = lax.dynamic_index_in_dim(src, my_idx, 0, keepdims=True) if scatter else src[None]
        landed.append(lax.dynamic_update_index_in_dim(land, own, my_idx, 0))
    return landed


def _adamw(parts, w, m, v, name):
    lyr, rows, cols = w.shape
    assert len(parts) == lyr
    tr = ADAM_ROWS if cols > 512 else 2 * ADAM_ROWS
    while rows % tr:
        tr //= 2
    tr = min(tr, rows)

    def body(*refs):
        p_refs = refs[:lyr]
        w_ref, m_ref, v_ref, g_out, d_out, m_out, v_out = refs[lyr:]
        for k in range(lyr):
            @pl.when(pl.program_id(0) == k)
            def _(p_ref=p_refs[k]):
                g = p_ref[0].astype(F32)
                for s in range(1, N_DEV):
                    g = g + p_ref[s].astype(F32)
                m2 = ADAM_B1 * m_ref[...] + (1.0 - ADAM_B1) * g
                v2 = ADAM_B2 * v_ref[...] + (1.0 - ADAM_B2) * (g * g)
                m_hat = m2 / (1.0 - ADAM_B1 ** ADAM_STEP)
                v_hat = v2 / (1.0 - ADAM_B2 ** ADAM_STEP)
                g_out[...] = g
                d_out[...] = -ADAM_LR * (m_hat / (jnp.sqrt(v_hat) + ADAM_EPS) + ADAM_WD * w_ref[...])
                m_out[...] = m2
                v_out[...] = v2

    def part_spec(k):
        return pl.BlockSpec((N_DEV, tr, cols), lambda l, i: (0, jnp.where(l == k, i, 0), 0))

    spec = pl.BlockSpec((None, tr, cols), lambda l, i: (l, i, 0))
    shp = jax.ShapeDtypeStruct((lyr, rows, cols), F32)
    return pl.pallas_call(
        body, name=name, grid=(lyr, rows // tr),
        in_specs=[part_spec(k) for k in range(lyr)] + [spec, spec, spec],
        out_specs=[spec] * 4, out_shape=[shp] * 4, compiler_params=_cparams(),
    )(*parts, w, m, v)


def _pack(arrays, lanes, row_mult, dtype):
    flat = jnp.concatenate([a.reshape(-1).astype(dtype) for a in arrays])
    unit = lanes * row_mult
    total = -(-flat.shape[0] // unit) * unit
    return jnp.pad(flat, (0, total - flat.shape[0])).reshape(total // lanes, lanes)


def _unpack(packed, shapes):
    flat = packed.reshape(-1)
    out, off = [], 0
    for shp in shapes:
        n = 1
        for d in shp:
            n *= d
        out.append(flat[off:off + n].reshape(shp))
        off += n
    return out


def _pad_slots(w, axis):
    axis = axis % w.ndim
    n = w.shape[axis] // HEAD_DIM
    shp = w.shape[:axis] + (n, HEAD_DIM) + w.shape[axis + 1:]
    pad = [(0, 0)] * (w.ndim + 1)
    pad[axis + 1] = (0, SLOT - HEAD_DIM)
    return jnp.pad(w.reshape(shp), pad).reshape(w.shape[:axis] + (n * SLOT,) + w.shape[axis + 1:])


def _unpad_slots(w, axis, keep=HEAD_DIM):
    axis = axis % w.ndim
    n = w.shape[axis] // SLOT
    shp = w.shape[:axis] + (n, SLOT) + w.shape[axis + 1:]
    idx = [slice(None)] * (w.ndim + 1)
    idx[axis + 1] = slice(0, keep)
    return w.reshape(shp)[tuple(idx)].reshape(w.shape[:axis] + (n * keep,) + w.shape[axis + 1:])


def _mla_in_pad(w):
    z = functools.partial(jnp.zeros, dtype=w.dtype)
    rows = w.shape[0]
    return jnp.concatenate([w[:, :384], z((rows, 64)), w[:, 640:672], z((rows, 32)), w[:, 384:640],
                            _pad_slots(w[:, 672:], 1)], axis=1)


def _mla_in_unpad(d):
    return jnp.concatenate([d[:, :384], d[:, 512:768], d[:, 448:480], _unpad_slots(d[:, 768:], 1)], axis=1)


def _mla_uq_pad(w):
    return jnp.pad(w.reshape(w.shape[0], MLA_HEADS, MLA_QK), ((0, 0), (0, 0), (0, SLOT - MLA_QK))).reshape(
        w.shape[0], MLA_HEADS * SLOT)


def _join(gathered, axis):
    nd, a, b = gathered.shape
    if axis == 1:
        return gathered.reshape(nd * a, b)
    return gathered.transpose(1, 0, 2).reshape(a, nd * b)


def _split(full, axis):
    r, c = full.shape
    if axis == 1:
        return full.reshape(N_DEV, r // N_DEV, c).astype(BF16)
    return full.reshape(r, N_DEV, c // N_DEV).transpose(1, 0, 2).astype(BF16)


def kernel(x, mem, positions, attn_norm_g, mlp_norm_g, mem_norm_g, final_norm_g, mla_w_in, mla_q_norm_g, mla_kv_norm_g, mla_w_uq, mla_w_ukv, swa_w_in, swa_sinks, w_mem_kv, w_o, mlp_w_up, mlp_w_down, loss_target, m_attn_norm_g, m_mlp_norm_g, m_mem_norm_g, m_final_norm_g, m_mla_w_in, m_mla_q_norm_g, m_mla_kv_norm_g, m_mla_w_uq, m_mla_w_ukv, m_swa_w_in, m_swa_sinks, m_w_mem_kv, m_w_o, m_mlp_w_up, m_mlp_w_down, v_attn_norm_g, v_mlp_norm_g, v_mem_norm_g, v_final_norm_g, v_mla_w_in, v_mla_q_norm_g, v_mla_kv_norm_g, v_mla_w_uq, v_mla_w_ukv, v_swa_w_in, v_swa_sinks, v_w_mem_kv, v_w_o, v_mlp_w_up, v_mlp_w_down):
    given = dict(locals())
    seq = x.shape[1]
    x0 = x.reshape(seq, D_MODEL)
    tgt = loss_target.reshape(seq, D_MODEL)
    mem0 = mem.reshape(N_MEM, D_MODEL)
    pos = positions.reshape(seq).astype(F32)
    pos_col, pos_row = pos.reshape(seq, 1), pos.reshape(1, seq)

    def layer_names(i):
        mixer = ("mla_w_in", "mla_w_uq", "mla_w_ukv") if i % 2 == 0 else ("swa_w_in",)
        return [(n, i // 2) for n in mixer] + [(n, i) for n in ("w_mem_kv", "w_o", "mlp_w_up", "mlp_w_down")]

    def local_weights(names):
        return [given[n][l].astype(BF16) for n, l in names]

    first_attn, first_mlp = layer_names(0)[:-2], layer_names(0)[-2:]
    weights = [dict(zip([n for n, _ in first_attn], _all_gather(local_weights(first_attn), "gather_weights_first")))]
    coming_mlp, first_token = _exchange_start(local_weights(first_mlp), False, "gather_weights_start_0",
                                              after=weights[0]["w_o"])

    consts = _lane_consts()
    tabs = _rope_tables(pos_col, consts)
    slopes = 2.0 ** (-8.0 * (jnp.arange(SWA_HEADS, dtype=F32) + 1.0) / SWA_HEADS)

    mem_n = _rmsnorm_fwd(mem0, 0, D_MODEL, mem_norm_g, "rmsnorm_fwd_mem")

    saved = []
    xc = x0
    for i in range(DEPTH):
        j = i // 2
        wts = weights[i]
        s = {"x_in": xc}
        token = None
        if i + 1 < DEPTH:
            coming, token = _exchange_start(local_weights(layer_names(i + 1)), False,
                                            "gather_weights_start_%d" % (i + 1),
                                            after=first_token if i == 0 else wts["w_o"])
        hn = _rmsnorm_fwd(xc, 0, D_MODEL, attn_norm_g[i], "rmsnorm_fwd", after=token)
        if i % 2 == 0:
            w_in = _mla_in_pad(_join(wts["mla_w_in"], 1))
            w_uq = _mla_uq_pad(_join(wts["mla_w_uq"], 2))
            w_kv = _join(wts["mla_w_ukv"], 2)
            proj = _mm(hn, w_in, "nn", F32, "mm_mla_in")
            cqn = _rmsnorm_fwd(proj, 0, MLA_Q_RANK, mla_q_norm_g[j], "rmsnorm_fwd_q")
            ckvn = _rmsnorm_fwd(proj, 2, MLA_KV_RANK, mla_kv_norm_g[j], "rmsnorm_fwd_kv")
            qraw = _mm(cqn, w_uq, "nn", F32, "mm_mla_uq")
            kvraw = _mm(ckvn, w_kv, "nn", F32, "mm_mla_ukv")
            q, k, v = _mla_rope_fwd(qraw, kvraw, proj, tabs)
            o, lse = _mla_attn_fwd(q, k, v)
            qoff = MLA_QOFF
            s.update(w_uq=w_uq, w_kv=w_kv, cqn=cqn, ckvn=ckvn, q=q, k=k, v=v)
        else:
            w_in = _join(wts["swa_w_in"], 2)
            proj = _mm(hn, w_in, "nn", BF16, "mm_swa_in", pairs="o")
            o, lse = _swa_attn_fwd(proj, pos_col, pos_row, slopes, swa_sinks[j])
            qoff = SWA_QOFF
        w_mem = _pad_slots(_join(wts["w_mem_kv"], 1), 1)
        w_out = _join(wts["w_o"], 1)
        w_o_mix, w_o_cross = w_out[:SWA_HEADS * HEAD_DIM], w_out[SWA_HEADS * HEAD_DIM:]
        kvmem = _mm(mem_n, w_mem, "nn", BF16, "mm_mem_kv")
        cross = _cross_attn_fwd(proj, qoff, kvmem)
        x1 = _mm(o, w_o_mix, "nn", F32, "mm_o", res=xc, pairs="a", second=(cross, w_o_cross))
        hn2 = _rmsnorm_fwd(x1, 0, D_MODEL, mlp_norm_g[i], "rmsnorm_fwd")
        if i == 0:
            wts.update(zip([n for n, _ in first_mlp], _exchange_wait(coming_mlp, hn2, "gather_weights_wait_0")))
        act, act2 = _mm(hn2, wts["mlp_w_up"], "nn", BF16, "mm_mlp_up", epi="relu2", b_blk="cols")
        xc = _mm(act2, wts["mlp_w_down"], "nn", F32, "mm_mlp_down", res=x1, b_blk="rows")
        s.update(hn=hn, w_in=w_in, proj=proj, o=o, lse=lse, qoff=qoff, w_mem=w_mem, w_out=w_out,
                 kvmem=kvmem, cross=cross, x1=x1, hn2=hn2, act=act, act2=act2)
        saved.append(s)
        if i + 1 < DEPTH:
            got = _exchange_wait(coming, xc, "gather_weights_wait_%d" % (i + 1))
            weights.append(dict(zip([n for n, _ in layer_names(i + 1)], got)))

    dx, dx_b, dg_final, loss_part = _loss_head(xc, final_norm_g, tgt)
    loss = lax.psum(loss_part[0, 0], MESH_AXES)

    gains = {n: [None] * DEPTH for n in ("attn_norm_g", "mlp_norm_g")}
    for n in ("mla_q_norm_g", "mla_kv_norm_g", "swa_sinks"):
        gains[n] = [None] * 2
    leaving = {}
    token = None
    dmem_n = None
    for i in reversed(range(DEPTH)):
        j = i // 2
        s = saved[i]
        wts = weights[i]
        out = {}
        du = _mm(dx_b, wts["mlp_w_down"], "nt", BF16, "mm_mlp_down_dx", aux=s["act"], epi="mul2aux", b_blk="rows",
                 after=token)
        out["mlp_w_down"] = _mm(s["act2"], dx_b, "tn", BF16, "mm_mlp_down_dw", o_blk="rows")
        out["mlp_w_up"] = _mm(s["hn2"], du, "tn", BF16, "mm_mlp_up_dw", o_blk="cols")
        dx1, dx1_b, dg = _mm(du, wts["mlp_w_up"], "nt", F32, "mm_mlp_up_dx", b_blk="cols",
                             epi="normbwd", norm=(s["x1"], mlp_norm_g[i], dx))
        gains["mlp_norm_g"][i] = dg[0]

        do = _mm(dx1_b, s["w_out"], "nt", BF16, "mm_o_dx", pairs="o")
        dw_o = jnp.concatenate([_mm(s["o"], dx1_b, "tn", F32, "mm_o_mix_dw", pairs="a"),
                                _mm(s["cross"], dx1_b, "tn", F32, "mm_o_cross_dw", pairs="a")], axis=0)
        out["w_o"] = _split(dw_o, 1)
        dqc, dkm, dvm = _cross_attn_bwd(s["proj"], s["qoff"], s["kvmem"], do, SWA_HEADS)
        dkvmem = jnp.concatenate([dkm, dvm], axis=1).astype(BF16)
        out["w_mem_kv"] = _split(_unpad_slots(_mm(mem_n, dkvmem, "tn", F32, "mm_mem_kv_dw"), 1), 1)
        dmem_n = _mm(dkvmem, s["w_mem"], "nt", F32, "mm_mem_kv_dx" if dmem_n is None else "mm_mem_kv_dx_acc",
                     res=dmem_n)
        leaving[(i, "main")], token = _exchange_start([out[n] for n, _ in layer_names(i)[-4:]], True,
                                                      "exchange_grads_main_start_%d" % i)

        if i % 2 == 0:
            dq, dk, dv = _mla_attn_bwd(s["q"], s["k"], s["v"], do, s["lse"], _mla_delta(s["o"], do), token)
            dqraw, dkv, dkr = _mla_rope_bwd(dq, dk, dv, tabs, consts)
            dcqn = _mm(dqraw, s["w_uq"], "nt", F32, "mm_mla_uq_dx")
            out["mla_w_uq"] = _split(_unpad_slots(_mm(s["cqn"], dqraw, "tn", F32, "mm_mla_uq_dw"), 1, MLA_QK), 2)
            dckvn = _mm(dkv, s["w_kv"], "nt", F32, "mm_mla_ukv_dx")
            out["mla_w_ukv"] = _split(_mm(s["ckvn"], dkv, "tn", F32, "mm_mla_ukv_dw"), 2)
            dcq, dg = _rmsnorm_bwd(s["proj"], 0, MLA_Q_RANK, mla_q_norm_g[j], dcqn, None, BF16, "rmsnorm_bwd_q")
            gains["mla_q_norm_g"][j] = dg[0]
            dckv, dg = _rmsnorm_bwd(s["proj"], 2, MLA_KV_RANK, mla_kv_norm_g[j], dckvn, None, BF16, "rmsnorm_bwd_kv")
            gains["mla_kv_norm_g"][j] = dg[0]
            dproj = jnp.concatenate([dcq, dkr.astype(BF16), dckv, dqc.astype(BF16)], axis=1)
            in_dx = "mm_mla_in_dx"
            out["mla_w_in"] = _split(_mla_in_unpad(_mm(s["hn"], dproj, "tn", F32, "mm_mla_in_dw")), 1)
        else:
            dq, dk, dv, dsink = _swa_attn_bwd(s["proj"], s["o"], do, s["lse"], pos_col, pos_row, slopes, swa_sinks[j],
                                              token)
            gains["swa_sinks"][j] = dsink[::8, 0]
            dproj = jnp.concatenate([dq, dk, dv, dqc], axis=1).astype(BF16)
            in_dx = "mm_swa_in_dx"
            out["swa_w_in"] = _split(_mm(s["hn"], dproj, "tn", F32, "mm_swa_in_dw", pairs="b"), 2)
        dx, dx_b, dg = _mm(dproj, s["w_in"], "nt", F32, in_dx, epi="normbwd", norm=(s["x_in"], attn_norm_g[i], dx1),
                           pairs="" if i % 2 == 0 else "a")
        gains["attn_norm_g"][i] = dg[0]

        leaving[(i, "mixer")], token = _exchange_start([out[n] for n, _ in layer_names(i)[:-4]], True,
                                                       "exchange_grads_mixer_start_%d" % i)

    _, dg_mem = _rmsnorm_bwd(mem0, 0, D_MODEL, mem_norm_g, dmem_n, None, BF16, "rmsnorm_bwd_mem")
    gains = {n: jnp.stack(g) for n, g in gains.items()}
    gains["mem_norm_g"] = dg_mem[0]
    gains["final_norm_g"] = dg_final[0]

    result = {}

    def adamw_of(names, received):
        for n in names:
            parts = [received[(n, l)] for l in range(given[n].shape[0])]
            for kind, r in enumerate(_adamw(parts, given[n], given["m_" + n], given["v_" + n], "adamw_" + n)):
                result[(kind, n)] = r

    received = {}
    for i in reversed(range(DEPTH)):
        got = _exchange_wait(leaving[(i, "main")], dx, "exchange_grads_main_wait_%d" % i)
        received.update(zip(layer_names(i)[-4:], got))
    adamw_of(("mlp_w_up", "mlp_w_down", "w_o", "w_mem_kv"), received)
    for i in reversed(range(DEPTH)):
        got = _exchange_wait(leaving[(i, "mixer")], result[(0, "w_mem_kv")], "exchange_grads_mixer_wait_%d" % i)
        received.update(zip(layer_names(i)[:-4], got))
    adamw_of(("mla_w_in", "mla_w_uq", "mla_w_ukv", "swa_w_in"), received)

    rep_shapes = [given[n].shape for n in REPLICATED]
    rep_parts = _all_gather([_pack([gains[n] for n in REPLICATED], SLOT, 8, F32)], "gather_gain_grads")[0]
    rep_packed = [_pack([given[p + n] for n in REPLICATED], SLOT, 8, F32)[None] for p in ("", "m_", "v_")]
    for kind, r in enumerate(_adamw([rep_parts], *rep_packed, "adamw_gains")):
        for n, part in zip(REPLICATED, _unpack(r[0], rep_shapes)):
            result[(kind, n)] = part

    outs = [loss, dx.reshape(1, seq, D_MODEL)]
    for kind in range(4):
        outs += [result[(kind, n)] for n in WEIGHT_ORDER]
    return tuple(outs)
```

```python
import functools

import jax
import jax.numpy as jnp
from jax import lax
from jax.experimental import pallas as pl
from jax.experimental.pallas import tpu as pltpu

F32 = jnp.float32
BF16 = jnp.bfloat16

D_MODEL = 1024
N_MEM = 256
DEPTH = 4
SLOT = 128
HEAD_DIM = 64
MLA_HEADS = 12
MLA_QK = 96
MLA_Q_RANK = 384
MLA_KV_RANK = 256
SWA_HEADS = 12
SWA_KV_HEADS = 4
SWA_GROUP = 3
MEM_HEADS = 4
WINDOW = 128
EPS = 1e-6
NEG = -1e30
ROPE_THETA = 10000.0
N_DEV = 8

ADAM_LR = 0.001
ADAM_B1 = 0.9
ADAM_B2 = 0.999
ADAM_EPS = 1e-08
ADAM_WD = 0.01
ADAM_STEP = 10

TM = 1024
TM_ROPE = 512
TQ_MLA = 1024
MLA_PACK = 4
SWA_PACK = 4
TQ_CROSS = 4096
MM_VMEM_BUDGET = 38 * 1024 * 1024
ADAM_ROWS = 256
VMEM_LIMIT = 56 * 1024 * 1024

MESH_AXES = ("x", "y", "c")

LOG2_E = 1.4426950408889634
MLA_SCALE = MLA_QK ** -0.5
MLA_Q_SCALE = MLA_SCALE * LOG2_E

MLA_QOFF = (MLA_Q_RANK + SLOT + MLA_KV_RANK) // SLOT
SWA_QOFF = SWA_HEADS + 2 * SWA_KV_HEADS

SHARDED = (
    ("mla_w_in", 1), ("mla_w_uq", 2), ("mla_w_ukv", 2), ("swa_w_in", 2),
    ("w_mem_kv", 1), ("w_o", 1), ("mlp_w_up", 2), ("mlp_w_down", 1),
)
REPLICATED = ("attn_norm_g", "mlp_norm_g", "mem_norm_g", "final_norm_g",
              "mla_q_norm_g", "mla_kv_norm_g", "swa_sinks")
WEIGHT_ORDER = ("attn_norm_g", "mlp_norm_g", "mem_norm_g", "final_norm_g", "mla_w_in",
                "mla_q_norm_g", "mla_kv_norm_g", "mla_w_uq", "mla_w_ukv", "swa_w_in",
                "swa_sinks", "w_mem_kv", "w_o", "mlp_w_up", "mlp_w_down")


def _cparams():
    return pltpu.CompilerParams(vmem_limit_bytes=VMEM_LIMIT)


_DIMS = {"nn": (((1,), (0,)), ((), ())), "nt": (((1,), (1,)), ((), ())), "tn": (((0,), (0,)), ((), ()))}


def _compact(x):
    pairs = [x[:, 2 * j * SLOT:(2 * j + 1) * SLOT] + pltpu.roll(x[:, (2 * j + 1) * SLOT:(2 * j + 2) * SLOT], HEAD_DIM, 1)
             for j in range(x.shape[1] // (2 * SLOT))]
    return pairs[0] if len(pairs) == 1 else jnp.concatenate(pairs, axis=1)


def _expand(x):
    low = lax.broadcasted_iota(jnp.int32, (x.shape[0], SLOT), 1) < HEAD_DIM
    slots = []
    for j in range(x.shape[1] // SLOT):
        pair = x[:, j * SLOT:(j + 1) * SLOT]
        slots += [jnp.where(low, pair, 0.0), pltpu.roll(jnp.where(low, 0.0, pair), HEAD_DIM, 1)]
    return jnp.concatenate(slots, axis=1)


def _mm_tiles(m, n, k, a_bytes, b_bytes, o_bytes, extra_bytes, tm_fixed, tn_fixed):
    best = None
    for tm in ([tm_fixed] if tm_fixed else [t for t in range(4096, 0, -SLOT) if m % t == 0] or [m]):
        for tn in ([tn_fixed] if tn_fixed else [t for t in range(1024, 0, -SLOT) if n % t == 0] or [n]):
            need = 2 * (tm * k * a_bytes + k * tn * b_bytes + tm * tn * (o_bytes + extra_bytes))
            need += tm * tn * 4
            if need <= MM_VMEM_BUDGET and (best is None or tm * tn > best[0] * best[1]):
                best = (tm, tn)
    assert best is not None, (m, n, k)
    return best


def _mm(a, b, mode, out_dtype, name, res=None, aux=None, epi=None, b_blk=None, o_blk=None, after=None, norm=None,
        pairs="", second=None):
    if b_blk is not None:
        nb, br, bc = b.shape
        b_shape = (nb * br, bc) if b_blk == "rows" else (br, nb * bc)
    else:
        b_shape = b.shape
    assert not pairs or (b_blk is None and o_blk is None and not ("b" in pairs and mode == "nt"))
    a_shape = (a.shape[0], a.shape[1] // 2) if "a" in pairs else a.shape
    if "b" in pairs:
        b_shape = (b_shape[0], b_shape[1] // 2)
    if mode == "nn":
        (m, k), (k2, n) = a_shape, b_shape
    elif mode == "nt":
        (m, k), (n, k2) = a_shape, b_shape
    else:
        (k, m), (k2, n) = a_shape, b_shape
    assert k == k2, (a.shape, b_shape, mode)
    assert second is None or (mode == "nn" and b_blk is None and second[0].shape[0] == m and second[1].shape[1] == n)
    k_second = 0 if second is None else second[1].shape[0]
    k_blocked = b_blk is not None and (b_blk == "rows") == (mode != "nt")
    tn_fixed = None
    if b_blk is not None and not k_blocked:
        tn_fixed = br if b_blk == "rows" else bc
    if o_blk == "cols":
        tn_fixed = n // N_DEV
    tm_fixed = m // N_DEV if o_blk == "rows" else None
    has_res, has_aux, has_norm = res is not None, aux is not None, epi == "normbwd"
    assert o_blk is None or not (has_res or has_aux or has_norm)
    n_out = 2 if epi == "relu2" else 1
    if has_norm:
        tn_fixed = n
        o_bytes, extra_bytes = 4 + 2, 4 + 4
    else:
        o_bytes = n_out * jnp.dtype(out_dtype).itemsize
        extra_bytes = (4 if has_res else 0) + (aux.dtype.itemsize if has_aux else 0)
    pa, pb, po = (2 if "a" in pairs else 1), (2 if "b" in pairs else 1), (2 if "o" in pairs else 1)
    tm, tn = _mm_tiles(m, n, k + k_second, a.dtype.itemsize * (3 if pa == 2 else 1),
                       b.dtype.itemsize * (3 if pb == 2 else 1), o_bytes * po, extra_bytes, tm_fixed, tn_fixed)
    dims = _DIMS[mode]
    if mode == "tn":
        a_spec = pl.BlockSpec((k, pa * tm), lambda i, j: (0, i))
    else:
        a_spec = pl.BlockSpec((tm, pa * k), lambda i, j: (i, 0))
    if b_blk is None:
        if mode == "nt":
            b_spec = pl.BlockSpec((tn, k), lambda i, j: (j, 0))
        else:
            b_spec = pl.BlockSpec((k, pb * tn), lambda i, j: (0, j))
    elif k_blocked and mode == "nt":
        b_spec = pl.BlockSpec((N_DEV, tn, bc), lambda i, j: (0, j, 0))
    elif k_blocked:
        b_spec = pl.BlockSpec((N_DEV, br, tn), lambda i, j: (0, 0, j))
    elif mode == "nt":
        b_spec = pl.BlockSpec((None, tn, k), lambda i, j: (j, 0, 0))
    else:
        b_spec = pl.BlockSpec((None, k, tn), lambda i, j: (j, 0, 0))
    if o_blk is None:
        o_spec = pl.BlockSpec((tm, po * tn), lambda i, j: (i, j))
        o_shape = (m, po * n)
    elif o_blk == "rows":
        o_spec = pl.BlockSpec((None, tm, tn), lambda i, j: (i, 0, j))
        o_shape = (N_DEV, tm, n)
    else:
        o_spec = pl.BlockSpec((None, tm, tn), lambda i, j: (j, i, 0))
        o_shape = (N_DEV, m, tn)

    def body(*refs):
        a_ref, b_ref = refs[0], refs[1]
        pos = 2
        res_ref = aux_ref = None
        if has_res:
            res_ref = refs[pos]
            pos += 1
        if has_aux:
            aux_ref = refs[pos]
            pos += 1
        if has_norm:
            x_ref, g_ref, dres_ref = refs[pos:pos + 3]
            pos += 3
        if second is not None:
            a2_ref, b2_ref = refs[pos:pos + 2]
            pos += 2
        if after is not None:
            pos += 1
        outs = refs[pos:]
        if k_blocked and mode == "nt":
            r = None
            for d in range(N_DEV):
                part = lax.dot_general(a_ref[:, d * bc:(d + 1) * bc].astype(BF16), b_ref[d].astype(BF16), dims,
                                       preferred_element_type=F32)
                r = part if r is None else r + part
        else:
            bv = b_ref[...].reshape(k, tn) if k_blocked else b_ref[...]
            av = _compact(a_ref[...].astype(F32)) if pa == 2 else a_ref[...]
            bv = _compact(bv.astype(F32)) if pb == 2 else bv
            r = lax.dot_general(av.astype(BF16), bv.astype(BF16), dims, preferred_element_type=F32)
        if second is not None:
            av2 = _compact(a2_ref[...].astype(F32)) if pa == 2 else a2_ref[...]
            r = r + lax.dot_general(av2.astype(BF16), b2_ref[...].astype(BF16), dims, preferred_element_type=F32)
        if po == 2:
            r = _expand(r)
        if epi == "relu2":
            r = jnp.maximum(r, 0.0)
            outs[0][...] = r.astype(outs[0].dtype)
            outs[1][...] = (r * r).astype(outs[1].dtype)
        elif has_norm:
            xv = x_ref[...]
            rs = lax.rsqrt(jnp.mean(xv * xv, axis=1, keepdims=True) + EPS)
            xh = xv * rs
            dxh = r * g_ref[...]
            dx = rs * (dxh - xh * jnp.mean(dxh * xh, axis=1, keepdims=True)) + dres_ref[...]
            outs[0][...] = dx
            outs[1][...] = dx.astype(BF16)

            @pl.when(pl.program_id(0) == 0)
            def _():
                outs[2][...] = jnp.zeros_like(outs[2])

            outs[2][...] += jnp.sum(r * xh, axis=0, keepdims=True)
        else:
            if epi == "mul2aux":
                r = r * (2.0 * aux_ref[...].astype(F32))
            if has_res:
                r = r + res_ref[...]
            outs[0][...] = r.astype(outs[0].dtype)

    in_specs = [a_spec, b_spec]
    args = [a, b]
    if has_res:
        in_specs.append(o_spec)
        args.append(res)
    if has_aux:
        in_specs.append(o_spec)
        args.append(aux)
    vec_spec = pl.BlockSpec((1, n), lambda i, j: (0, 0))
    if has_norm:
        in_specs += [o_spec, vec_spec, o_spec]
        args += [norm[0], norm[1].reshape(1, n), norm[2]]
    if second is not None:
        in_specs += [pl.BlockSpec((tm, pa * k_second), lambda i, j: (i, 0)),
                     pl.BlockSpec((k_second, tn), lambda i, j: (0, j))]
        args += list(second)
    if after is not None:
        in_specs.append(pl.BlockSpec(memory_space=pl.ANY))
        args.append(after)
    if has_norm:
        out_specs = [o_spec, o_spec, vec_spec]
        out_shape = [jax.ShapeDtypeStruct(o_shape, F32), jax.ShapeDtypeStruct(o_shape, BF16),
                     jax.ShapeDtypeStruct((1, n), F32)]
    else:
        out_specs = [o_spec] * n_out
        out_shape = [jax.ShapeDtypeStruct(o_shape, out_dtype)] * n_out
    out = pl.pallas_call(
        body, name=name, grid=(m // tm, n // tn),
        in_specs=in_specs, out_specs=out_specs, out_shape=out_shape, compiler_params=_cparams(),
    )(*args)
    return out if len(out) > 1 else out[0]


def _rmsnorm_fwd(xarr, colblk, width, g, name, after=None):
    rows = xarr.shape[0]
    tm = min(TM, rows)

    def body(x_ref, g_ref, *rest):
        y_ref = rest[-1]
        x = x_ref[...].astype(F32)
        r = lax.rsqrt(jnp.mean(x * x, axis=1, keepdims=True) + EPS)
        y_ref[...] = (x * r * g_ref[...]).astype(y_ref.dtype)

    in_specs = [pl.BlockSpec((tm, width), lambda i: (i, colblk)), pl.BlockSpec((1, width), lambda i: (0, 0))]
    args = [xarr, g.reshape(1, width)]
    if after is not None:
        in_specs.append(pl.BlockSpec(memory_space=pl.ANY))
        args.append(after)
    return pl.pallas_call(
        body, name=name, grid=(rows // tm,), in_specs=in_specs,
        out_specs=pl.BlockSpec((tm, width), lambda i: (i, 0)),
        out_shape=jax.ShapeDtypeStruct((rows, width), BF16), compiler_params=_cparams(),
    )(*args)


def _rmsnorm_bwd(xarr, colblk, width, g, dy, dres, out_dtype, name):
    rows = xarr.shape[0]
    tm = min(TM, rows)
    has_res = dres is not None

    def body(*refs):
        x_ref, g_ref, dy_ref = refs[0], refs[1], refs[2]
        dres_ref = refs[3] if has_res else None
        dx_ref, dg_ref = refs[-2], refs[-1]
        x = x_ref[...].astype(F32)
        dyv = dy_ref[...].astype(F32)
        r = lax.rsqrt(jnp.mean(x * x, axis=1, keepdims=True) + EPS)
        xh = x * r
        dxh = dyv * g_ref[...]
        dx = r * (dxh - xh * jnp.mean(dxh * xh, axis=1, keepdims=True))
        if has_res:
            dx = dx + dres_ref[...]
        dx_ref[...] = dx.astype(dx_ref.dtype)

        @pl.when(pl.program_id(0) == 0)
        def _():
            dg_ref[...] = jnp.zeros_like(dg_ref)

        dg_ref[...] += jnp.sum(dyv * xh, axis=0, keepdims=True)

    row_spec = pl.BlockSpec((tm, width), lambda i: (i, 0))
    vec_spec = pl.BlockSpec((1, width), lambda i: (0, 0))
    in_specs = [pl.BlockSpec((tm, width), lambda i: (i, colblk)), vec_spec, row_spec]
    args = [xarr, g.reshape(1, width), dy]
    if has_res:
        in_specs.append(row_spec)
        args.append(dres)
    return pl.pallas_call(
        body, name=name, grid=(rows // tm,), in_specs=in_specs, out_specs=[row_spec, vec_spec],
        out_shape=[jax.ShapeDtypeStruct((rows, width), out_dtype), jax.ShapeDtypeStruct((1, width), F32)],
        compiler_params=_cparams(),
    )(*args)


def _loss_head(x, g, tgt):
    rows, width = x.shape
    tm = min(TM, rows)

    def body(x_ref, g_ref, t_ref, dx_ref, dxb_ref, dg_ref, loss_ref):
        xv = x_ref[...]
        gv = g_ref[...]
        r = lax.rsqrt(jnp.mean(xv * xv, axis=1, keepdims=True) + EPS)
        xh = xv * r
        err = xh * gv - t_ref[...]
        part = 0.5 * jnp.sum(jnp.mean(err * err, axis=1, keepdims=True), axis=0, keepdims=True)
        dyv = err * (1.0 / width)
        dxh = dyv * gv
        dxv = r * (dxh - xh * jnp.mean(dxh * xh, axis=1, keepdims=True))
        dx_ref[...] = dxv
        dxb_ref[...] = dxv.astype(BF16)

        @pl.when(pl.program_id(0) == 0)
        def _():
            dg_ref[...] = jnp.zeros_like(dg_ref)
            loss_ref[...] = jnp.zeros_like(loss_ref)

        dg_ref[...] += jnp.sum(dyv * xh, axis=0, keepdims=True)
        loss_ref[...] += jnp.broadcast_to(part, loss_ref.shape)

    row_spec = pl.BlockSpec((tm, width), lambda i: (i, 0))
    vec_spec = pl.BlockSpec((1, width), lambda i: (0, 0))
    return pl.pallas_call(
        body, name="loss_head", grid=(rows // tm,), in_specs=[row_spec, vec_spec, row_spec],
        out_specs=[row_spec, row_spec, vec_spec, pl.BlockSpec((1, SLOT), lambda i: (0, 0))],
        out_shape=[jax.ShapeDtypeStruct((rows, width), F32), jax.ShapeDtypeStruct((rows, width), BF16),
                   jax.ShapeDtypeStruct((1, width), F32), jax.ShapeDtypeStruct((1, SLOT), F32)],
        compiler_params=_cparams(),
    )(x, g.reshape(1, width), tgt)


def _lane_consts():
    half = 16
    inv = ROPE_THETA ** (-(jnp.arange(half, dtype=F32) * 2.0) / 32)
    lane = jnp.arange(SLOT)
    first = (lane >= 64) & (lane < 80)
    second = (lane >= 80) & (lane < 96)
    inv_lane = jnp.where(first | second, inv[(lane - 64) % half], 0.0)
    rows = [inv_lane, (lane < 64).astype(F32), first.astype(F32), second.astype(F32)]
    rows += [jnp.zeros((SLOT,), F32)] * 4
    return jnp.stack(rows).astype(F32)


def _rope_tables(pos_col, consts):
    rows = pos_col.shape[0]
    tm = min(TM, rows)

    def body(p_ref, k_ref, c_ref, s1_ref, s2_ref):
        ang = p_ref[...] * k_ref[0:1, :]
        cos, sin = jnp.cos(ang), jnp.sin(ang)
        first, second = k_ref[2:3, :], k_ref[3:4, :]
        c_ref[...] = k_ref[1:2, :] + (first + second) * cos
        s1_ref[...] = -first * sin
        s2_ref[...] = second * sin

    spec = pl.BlockSpec((tm, SLOT), lambda i: (i, 0))
    shp = jax.ShapeDtypeStruct((rows, SLOT), F32)
    return pl.pallas_call(
        body, name="rope_tables", grid=(rows // tm,),
        in_specs=[pl.BlockSpec((tm, 1), lambda i: (i, 0)), pl.BlockSpec((8, SLOT), lambda i: (0, 0))],
        out_specs=[spec, spec, spec], out_shape=[shp, shp, shp], compiler_params=_cparams(),
    )(pos_col, consts)


def _rot(xv, c, s1, s2):
    return xv * c + pltpu.roll(xv, SLOT - 16, 1) * s1 + pltpu.roll(xv, 16, 1) * s2


def _rot_t(dy, c, s1, s2):
    return dy * c + pltpu.roll(dy * s1, 16, 1) + pltpu.roll(dy * s2, SLOT - 16, 1)


def _mla_rope_fwd(qraw, kvraw, proj, tabs):
    rows = qraw.shape[0]
    tm = min(TM_ROPE, rows)
    hw = MLA_HEADS * SLOT

    def body(q_ref, kv_ref, kr_ref, c_ref, s1_ref, s2_ref, qo, ko, vo):
        c, s1, s2 = c_ref[...], s1_ref[...], s2_ref[...]
        kr = _rot(kr_ref[...], c, s1, s2)
        low = lax.broadcasted_iota(jnp.int32, (tm, SLOT), 1) < HEAD_DIM
        for h in range(MLA_HEADS):
            sl = slice(h * SLOT, (h + 1) * SLOT)
            qo[:, sl] = (_rot(q_ref[:, sl], c, s1, s2) * MLA_Q_SCALE).astype(BF16)
            kvh = kv_ref[:, sl]
            ko[:, sl] = (jnp.where(low, kvh, 0.0) + kr).astype(BF16)
            vo[:, sl] = pltpu.roll(jnp.where(low, 0.0, kvh), HEAD_DIM, 1).astype(BF16)

    tab = pl.BlockSpec((tm, SLOT), lambda i: (i, 0))
    wide = pl.BlockSpec((tm, hw), lambda i: (i, 0))
    shp = jax.ShapeDtypeStruct((rows, hw), BF16)
    return pl.pallas_call(
        body, name="mla_rope_fwd", grid=(rows // tm,),
        in_specs=[wide, wide, pl.BlockSpec((tm, SLOT), lambda i: (i, 3)),
                  tab, tab, tab],
        out_specs=[wide, wide, wide], out_shape=[shp, shp, shp], compiler_params=_cparams(),
    )(qraw, kvraw, proj, *tabs)


def _mla_rope_bwd(dq, dk, dv, tabs, consts):
    rows = dq.shape[0]
    tm = min(TM_ROPE, rows)
    hw = MLA_HEADS * SLOT

    def body(dq_ref, dk_ref, dv_ref, c_ref, s1_ref, s2_ref, k_ref, dqo, dkvo, dkro):
        c, s1, s2 = c_ref[...], s1_ref[...], s2_ref[...]
        ksum = jnp.zeros((tm, SLOT), F32)
        low = lax.broadcasted_iota(jnp.int32, (tm, SLOT), 1) < HEAD_DIM
        for h in range(MLA_HEADS):
            sl = slice(h * SLOT, (h + 1) * SLOT)
            dqo[:, sl] = _rot_t(dq_ref[:, sl], c, s1, s2).astype(BF16)
            dkh = dk_ref[:, sl]
            ksum = ksum + dkh
            dvh = pltpu.roll(jnp.where(low, dv_ref[:, sl], 0.0), HEAD_DIM, 1)
            dkvo[:, sl] = (jnp.where(low, dkh, 0.0) + dvh).astype(BF16)
        dkro[...] = _rot_t(ksum, c, s1, s2) * (k_ref[2:3, :] + k_ref[3:4, :])

    tab = pl.BlockSpec((tm, SLOT), lambda i: (i, 0))
    wide = pl.BlockSpec((tm, hw), lambda i: (i, 0))
    return pl.pallas_call(
        body, name="mla_rope_bwd", grid=(rows // tm,),
        in_specs=[wide, wide, wide, tab, tab, tab, pl.BlockSpec((8, SLOT), lambda i: (0, 0))],
        out_specs=[wide, wide, tab],
        out_shape=[jax.ShapeDtypeStruct((rows, hw), BF16), jax.ShapeDtypeStruct((rows, hw), BF16),
                   jax.ShapeDtypeStruct((rows, SLOT), F32)],
        compiler_params=_cparams(),
    )(dq, dk, dv, *tabs, consts)


def _nt(a, b):
    return lax.dot_general(a, b, _DIMS["nt"], preferred_element_type=F32)


def _tn(a, b):
    return lax.dot_general(a, b, _DIMS["tn"], preferred_element_type=F32)


def _nn(a, b):
    return lax.dot_general(a, b, _DIMS["nn"], preferred_element_type=F32)


def _mla_attn_fwd(q, k, v):
    rows = q.shape[0]
    t = min(TQ_MLA, rows)
    nt = rows // t
    wide = MLA_PACK * SLOT

    def body(q_ref, k_ref, v_ref, o_ref, lse_ref, m_sc, l_sc, acc_sc):
        i, j = pl.program_id(1), pl.program_id(2)

        @pl.when(j == 0)
        def _():
            m_sc[...] = jnp.full_like(m_sc, NEG)
            l_sc[...] = jnp.zeros_like(l_sc)
            acc_sc[...] = jnp.zeros_like(acc_sc)

        def step(diagonal):
            for hh in range(MLA_PACK):
                sl = slice(hh * SLOT, (hh + 1) * SLOT)
                s = _nt(k_ref[:, sl], q_ref[:, sl])
                if diagonal:
                    key = lax.broadcasted_iota(jnp.int32, (t, t), 0)
                    s = jnp.where(key <= lax.broadcasted_iota(jnp.int32, (t, t), 1), s, NEG)
                m_prev = m_sc[hh]
                m_new = jnp.maximum(m_prev, jnp.max(s, axis=0, keepdims=True))
                p = jnp.exp2(s - m_new)
                alpha = jnp.exp2(m_prev - m_new)
                l_new = alpha * l_sc[hh] + jnp.sum(p, axis=0, keepdims=True)
                acc = alpha * acc_sc[hh] + _tn(v_ref[:, sl], p.astype(BF16))
                if diagonal:
                    o_ref[:, sl] = (acc / l_new).T.astype(o_ref.dtype)
                    lse_ref[hh:hh + 1, :] = m_new + jnp.log(l_new) * LOG2_E
                else:
                    m_sc[hh] = m_new
                    l_sc[hh] = l_new
                    acc_sc[hh] = acc

        @pl.when(j < i)
        def _():
            step(False)

        @pl.when(j == i)
        def _():
            lse_ref[...] = jnp.zeros_like(lse_ref)
            step(True)

    q_spec = pl.BlockSpec((t, wide), lambda h, i, j: (i, h))
    kv_spec = pl.BlockSpec((t, wide), lambda h, i, j: (jnp.minimum(j, i), h))
    return pl.pallas_call(
        body, name="mla_attn_fwd", grid=(MLA_HEADS // MLA_PACK, nt, nt),
        in_specs=[q_spec, kv_spec, kv_spec],
        out_specs=[q_spec, pl.BlockSpec((None, 8, t), lambda h, i, j: (h, 0, i))],
        out_shape=[jax.ShapeDtypeStruct(q.shape, BF16),
                   jax.ShapeDtypeStruct((MLA_HEADS // MLA_PACK, 8, rows), F32)],
        scratch_shapes=[pltpu.VMEM((MLA_PACK, 1, t), F32), pltpu.VMEM((MLA_PACK, 1, t), F32),
                        pltpu.VMEM((MLA_PACK, SLOT, t), F32)],
        compiler_params=_cparams(),
    )(q, k, v)


def _mla_delta(o, do):
    rows = o.shape[0]
    t = rows
    wide = MLA_PACK * SLOT

    def body(o_ref, do_ref, d_ref):
        d_ref[...] = jnp.zeros_like(d_ref)
        ones = jnp.ones((8, SLOT), BF16)
        for hh in range(MLA_PACK):
            sl = slice(hh * SLOT, (hh + 1) * SLOT)
            prod = do_ref[:, sl].astype(F32) * o_ref[:, sl].astype(F32)
            high = prod.astype(BF16)
            low = (prod - high.astype(F32)).astype(BF16)
            d_ref[hh:hh + 1, :] = (_nt(ones, high) + _nt(ones, low))[0:1, :]

    spec = pl.BlockSpec((t, wide), lambda h, i: (i, h))
    return pl.pallas_call(
        body, name="mla_delta", grid=(MLA_HEADS // MLA_PACK, rows // t), in_specs=[spec, spec],
        out_specs=pl.BlockSpec((None, 8, t), lambda h, i: (h, 0, i)),
        out_shape=jax.ShapeDtypeStruct((MLA_HEADS // MLA_PACK, 8, rows), F32), compiler_params=_cparams(),
    )(o, do)


def _mla_attn_bwd(q, k, v, do, lse, delta, after):
    rows = q.shape[0]
    t = min(TQ_MLA, rows)
    nt = rows // t
    wide = MLA_PACK * SLOT

    def body(q_ref, k_ref, v_ref, do_ref, lse_ref, delta_ref, after_ref, dq_ref, dk_ref, dv_ref, dk_sc, dv_sc):
        j, i = pl.program_id(1), pl.program_id(2)

        @pl.when((j == 0) & (i == 0))
        def _():
            dq_ref[...] = jnp.zeros_like(dq_ref)

        @pl.when(i == 0)
        def _():
            dk_sc[...] = jnp.zeros_like(dk_sc)
            dv_sc[...] = jnp.zeros_like(dv_sc)

        def chunk(hh, rows, keys, masked):
            sl = slice(hh * SLOT, (hh + 1) * SLOT)
            n_rows = rows.stop - rows.start
            qv, kv, dov = q_ref[rows, sl], k_ref[keys, sl], do_ref[rows, sl]
            s = _nt(kv, qv)
            if masked:
                shp = (keys.stop - keys.start, n_rows)
                s = jnp.where(keys.start + lax.broadcasted_iota(jnp.int32, shp, 0)
                              <= rows.start + lax.broadcasted_iota(jnp.int32, shp, 1), s, NEG)
            p = jnp.exp2(s - lse_ref[hh:hh + 1, rows])
            dp = _nt(v_ref[keys, sl], dov)
            ds = (p * (dp - delta_ref[hh:hh + 1, rows])).astype(BF16)
            dv_sc[keys, sl] += _nn(p.astype(BF16), dov)
            dk_sc[keys, sl] += _nn(ds, qv)
            r0 = pl.multiple_of(i * t + rows.start, n_rows)
            dq_ref[pl.ds(r0, n_rows), sl] += _tn(ds, kv) * MLA_SCALE

        @pl.when(i > j)
        def _():
            for hh in range(MLA_PACK):
                chunk(hh, slice(0, t), slice(0, t), False)

        @pl.when(i == j)
        def _():
            for hh in range(MLA_PACK):
                chunk(hh, slice(0, t), slice(0, t // 2), True)
                chunk(hh, slice(t // 2, t), slice(t // 2, t), True)

        @pl.when(i == nt - 1)
        def _():
            dk_ref[...] = dk_sc[...] * (1.0 / LOG2_E)
            dv_ref[...] = dv_sc[...]

    q_spec = pl.BlockSpec((t, wide), lambda h, j, i: (jnp.maximum(i, j), h))
    kv_spec = pl.BlockSpec((t, wide), lambda h, j, i: (j, h))
    row_spec = pl.BlockSpec((None, 8, t), lambda h, j, i: (h, 0, jnp.maximum(i, j)))
    head_spec = pl.BlockSpec((rows, wide), lambda h, j, i: (0, h))
    shp = jax.ShapeDtypeStruct(q.shape, F32)
    return pl.pallas_call(
        body, name="mla_attn_bwd", grid=(MLA_HEADS // MLA_PACK, nt, nt),
        in_specs=[q_spec, kv_spec, kv_spec, q_spec, row_spec, row_spec, pl.BlockSpec(memory_space=pl.ANY)],
        out_specs=[head_spec, kv_spec, kv_spec], out_shape=[shp, shp, shp],
        scratch_shapes=[pltpu.VMEM((t, wide), F32), pltpu.VMEM((t, wide), F32)],
        compiler_params=_cparams(),
    )(q, k, v, do, lse, delta, after)


def _swa_specs(t):
    def prev(i):
        return jnp.maximum(i - 1, 0)
    kw = SWA_PACK * SLOT
    k0, v0 = SWA_HEADS // SWA_PACK, (SWA_HEADS + SWA_KV_HEADS) // SWA_PACK
    q3 = pl.BlockSpec((t, SWA_PACK * SWA_GROUP * SLOT), lambda h, i: (i, h))
    kp = pl.BlockSpec((t, kw), lambda h, i: (prev(i), k0 + h))
    kc = pl.BlockSpec((t, kw), lambda h, i: (i, k0 + h))
    vp = pl.BlockSpec((t, kw), lambda h, i: (prev(i), v0 + h))
    vc = pl.BlockSpec((t, kw), lambda h, i: (i, v0 + h))
    pcol = pl.BlockSpec((t, 1), lambda h, i: (i, 0))
    prow_p = pl.BlockSpec((1, t), lambda h, i: (0, prev(i)))
    prow_c = pl.BlockSpec((1, t), lambda h, i: (0, i))
    return [q3, kp, kc, vp, vc, pcol, prow_p, prow_c]


def _stack(ref, first):
    return jnp.concatenate([ref[:, (first + g) * SLOT:(first + g + 1) * SLOT] for g in range(SWA_GROUP)], axis=0)


def _swa_logits(q3, kp, kc, pq, pkp, pkc, slope_ref, kvh, i, t):
    r = lax.broadcasted_iota(jnp.int32, (t, t), 0)
    c = lax.broadcasted_iota(jnp.int32, (t, t), 1)
    ok_c = c <= r
    ok_p = (c - r) > jnp.where(i > 0, 0, t)
    dist_p, dist_c = pq - pkp, pq - pkc
    s_p3 = _nt(q3, kp) * (HEAD_DIM ** -0.5)
    s_c3 = _nt(q3, kc) * (HEAD_DIM ** -0.5)
    out = []
    for g in range(SWA_GROUP):
        slope = slope_ref[kvh * SWA_GROUP + g]
        rows = slice(g * t, (g + 1) * t)
        out.append((jnp.where(ok_p, s_p3[rows] - slope * dist_p, NEG),
                    jnp.where(ok_c, s_c3[rows] - slope * dist_c, NEG)))
    return out


def _swa_attn_fwd(proj, pos_col, pos_row, slopes, sinks):
    rows = proj.shape[0]
    t = WINDOW
    hw = SWA_HEADS * SLOT

    def body(slope_ref, sink_ref, q_ref, kp_ref, kc_ref, vp_ref, vc_ref, pq_ref, pkp_ref, pkc_ref, o_ref, lse_ref):
        i = pl.program_id(1)
        for kv in range(SWA_PACK):
            kvh = pl.program_id(0) * SWA_PACK + kv
            ksl = slice(kv * SLOT, (kv + 1) * SLOT)
            logits = _swa_logits(_stack(q_ref, kv * SWA_GROUP), kp_ref[:, ksl], kc_ref[:, ksl], pq_ref[...],
                                 pkp_ref[...], pkc_ref[...], slope_ref, kvh, i, t)
            e_p, e_c, norm = [], [], []
            for g, (s_p, s_c) in enumerate(logits):
                sl = slice((kv * SWA_GROUP + g) * SLOT, (kv * SWA_GROUP + g + 1) * SLOT)
                sink = sink_ref[kvh * SWA_GROUP + g]
                m = jnp.maximum(jnp.maximum(jnp.max(s_p, axis=1, keepdims=True),
                                            jnp.max(s_c, axis=1, keepdims=True)), sink)
                ep, ec = jnp.exp(s_p - m), jnp.exp(s_c - m)
                l = jnp.sum(ep, axis=1, keepdims=True) + jnp.sum(ec, axis=1, keepdims=True) + jnp.exp(sink - m)
                e_p.append(ep.astype(BF16))
                e_c.append(ec.astype(BF16))
                norm.append(l)
                lse_ref[:, sl] = jnp.broadcast_to(m + jnp.log(l), (t, SLOT))
            acc = (_nn(jnp.concatenate(e_p, axis=0), vp_ref[:, ksl])
                   + _nn(jnp.concatenate(e_c, axis=0), vc_ref[:, ksl]))
            for g in range(SWA_GROUP):
                sl = slice((kv * SWA_GROUP + g) * SLOT, (kv * SWA_GROUP + g + 1) * SLOT)
                o_ref[:, sl] = (acc[g * t:(g + 1) * t] / norm[g]).astype(o_ref.dtype)

    smem = pl.BlockSpec(memory_space=pltpu.SMEM)
    out_spec = pl.BlockSpec((t, SWA_PACK * SWA_GROUP * SLOT), lambda h, i: (i, h))
    return pl.pallas_call(
        body, name="swa_attn_fwd", grid=(SWA_KV_HEADS // SWA_PACK, rows // t),
        in_specs=[smem, smem] + _swa_specs(t), out_specs=[out_spec, out_spec],
        out_shape=[jax.ShapeDtypeStruct((rows, hw), BF16), jax.ShapeDtypeStruct((rows, hw), F32)],
        compiler_params=_cparams(),
    )(slopes, sinks, proj, proj, proj, proj, proj, pos_col, pos_row, pos_row)


def _swa_attn_bwd(proj, o, do, lse, pos_col, pos_row, slopes, sinks, after):
    rows = proj.shape[0]
    t = WINDOW
    hw = SWA_HEADS * SLOT
    scale = HEAD_DIM ** -0.5

    def body(slope_ref, sink_ref, q_ref, kp_ref, kc_ref, vp_ref, vc_ref, pq_ref, pkp_ref, pkc_ref,
             o_ref, do_ref, lse_ref, after_ref, dq_ref, dk_ref, dv_ref, dsink_ref):
        i = pl.program_id(1)

        @pl.when(i == 0)
        def _():
            dk_ref[...] = jnp.zeros_like(dk_ref)
            dv_ref[...] = jnp.zeros_like(dv_ref)
            dsink_ref[...] = jnp.zeros_like(dsink_ref)

        r_c = pl.multiple_of(i * t, t)
        r_p = pl.multiple_of(jnp.maximum(i - 1, 0) * t, t)
        for kv in range(SWA_PACK):
            kvh = pl.program_id(0) * SWA_PACK + kv
            ksl = slice(kv * SLOT, (kv + 1) * SLOT)
            q3, do3 = _stack(q_ref, kv * SWA_GROUP), _stack(do_ref, kv * SWA_GROUP)
            logits = _swa_logits(q3, kp_ref[:, ksl], kc_ref[:, ksl], pq_ref[...], pkp_ref[...], pkc_ref[...],
                                 slope_ref, kvh, i, t)
            dp_p3, dp_c3 = _nt(do3, vp_ref[:, ksl]), _nt(do3, vc_ref[:, ksl])
            p_p, p_c, ds_p, ds_c = [], [], [], []
            for g, (s_p, s_c) in enumerate(logits):
                head = kv * SWA_GROUP + g
                sl = slice(head * SLOT, (head + 1) * SLOT)
                rws = slice(g * t, (g + 1) * t)
                lse_g = lse_ref[:, head * SLOT:head * SLOT + 1]
                pp, pc = jnp.exp(s_p - lse_g), jnp.exp(s_c - lse_g)
                delta = jnp.sum(do_ref[:, sl].astype(F32) * o_ref[:, sl].astype(F32), axis=1, keepdims=True)
                p_p.append(pp.astype(BF16))
                p_c.append(pc.astype(BF16))
                ds_p.append((pp * (dp_p3[rws] - delta)).astype(BF16))
                ds_c.append((pc * (dp_c3[rws] - delta)).astype(BF16))
                sink = sink_ref[kvh * SWA_GROUP + g]
                dsink = -jnp.sum(jnp.exp(sink - lse_g) * delta, axis=0, keepdims=True)
                dsink_ref[head * 8:(head + 1) * 8, :] += jnp.broadcast_to(dsink, (8, SLOT))
            p_p3, p_c3 = jnp.concatenate(p_p, axis=0), jnp.concatenate(p_c, axis=0)
            ds_p3, ds_c3 = jnp.concatenate(ds_p, axis=0), jnp.concatenate(ds_c, axis=0)
            dq3 = (_nn(ds_p3, kp_ref[:, ksl]) + _nn(ds_c3, kc_ref[:, ksl])) * scale
            for g in range(SWA_GROUP):
                head = kv * SWA_GROUP + g
                dq_ref[:, head * SLOT:(head + 1) * SLOT] = dq3[g * t:(g + 1) * t]
            dk_ref[pl.ds(r_c, t), ksl] += _tn(ds_c3, q3) * scale
            dv_ref[pl.ds(r_c, t), ksl] += _tn(p_c3, do3)
            dk_ref[pl.ds(r_p, t), ksl] += _tn(ds_p3, q3) * scale
            dv_ref[pl.ds(r_p, t), ksl] += _tn(p_p3, do3)

    smem = pl.BlockSpec(memory_space=pltpu.SMEM)
    qlike = pl.BlockSpec((t, SWA_PACK * SWA_GROUP * SLOT), lambda h, i: (i, h))
    kv_out = pl.BlockSpec((rows, SWA_PACK * SLOT), lambda h, i: (0, h))
    return pl.pallas_call(
        body, name="swa_attn_bwd", grid=(SWA_KV_HEADS // SWA_PACK, rows // t),
        in_specs=[smem, smem] + _swa_specs(t) + [qlike, qlike, qlike, pl.BlockSpec(memory_space=pl.ANY)],
        out_specs=[qlike, kv_out, kv_out,
                   pl.BlockSpec((SWA_PACK * SWA_GROUP * 8, SLOT), lambda h, i: (h, 0))],
        out_shape=[jax.ShapeDtypeStruct((rows, hw), F32), jax.ShapeDtypeStruct((rows, SWA_KV_HEADS * SLOT), F32),
                   jax.ShapeDtypeStruct((rows, SWA_KV_HEADS * SLOT), F32),
                   jax.ShapeDtypeStruct((SWA_HEADS * 8, SLOT), F32)],
        compiler_params=_cparams(),
    )(slopes, sinks, proj, proj, proj, proj, proj, pos_col, pos_row, pos_row, o, do, lse, after)


def _cross_attn_fwd(proj, qoff, kvmem):
    rows = proj.shape[0]
    t = min(TQ_CROSS, rows)

    def body(q_ref, k_ref, v_ref, o_ref):
        s = _nt(k_ref[...], q_ref[...].astype(BF16)) * (HEAD_DIM ** -0.5)
        e = jnp.exp(s - jnp.max(s, axis=0, keepdims=True))
        p = e / jnp.sum(e, axis=0, keepdims=True)
        o_ref[...] = _tn(v_ref[...], p.astype(BF16)).T.astype(o_ref.dtype)

    return pl.pallas_call(
        body, name="cross_attn_fwd", grid=(rows // t, MEM_HEADS),
        in_specs=[pl.BlockSpec((t, SLOT), lambda i, h: (i, qoff + h)),
                  pl.BlockSpec((N_MEM, SLOT), lambda i, h: (0, h)),
                  pl.BlockSpec((N_MEM, SLOT), lambda i, h: (0, MEM_HEADS + h))],
        out_specs=pl.BlockSpec((t, SLOT), lambda i, h: (i, h)),
        out_shape=jax.ShapeDtypeStruct((rows, MEM_HEADS * SLOT), BF16), compiler_params=_cparams(),
    )(proj, kvmem, kvmem)


def _cross_attn_bwd(proj, qoff, kvmem, do, do_off):
    rows = proj.shape[0]
    t = min(TQ_CROSS, rows)
    scale = HEAD_DIM ** -0.5

    def body(q_ref, k_ref, v_ref, do_ref, dq_ref, dk_ref, dv_ref):
        @pl.when(pl.program_id(1) == 0)
        def _():
            dk_ref[...] = jnp.zeros_like(dk_ref)
            dv_ref[...] = jnp.zeros_like(dv_ref)

        qv, kv, dov = q_ref[...].astype(BF16), k_ref[...], do_ref[...]
        s = _nt(kv, qv) * scale
        e = jnp.exp(s - jnp.max(s, axis=0, keepdims=True))
        p = e / jnp.sum(e, axis=0, keepdims=True)
        dp = _nt(v_ref[...], dov)
        ds = (p * (dp - jnp.sum(p * dp, axis=0, keepdims=True))).astype(BF16)
        dq_ref[...] = _tn(ds, kv) * scale
        dk_ref[...] += _nn(ds, qv) * scale
        dv_ref[...] += _nn(p.astype(BF16), dov)

    mem_out = pl.BlockSpec((N_MEM, SLOT), lambda h, i: (0, h))
    return pl.pallas_call(
        body, name="cross_attn_bwd", grid=(MEM_HEADS, rows // t),
        in_specs=[pl.BlockSpec((t, SLOT), lambda h, i: (i, qoff + h)),
                  pl.BlockSpec((N_MEM, SLOT), lambda h, i: (0, h)),
                  pl.BlockSpec((N_MEM, SLOT), lambda h, i: (0, MEM_HEADS + h)),
                  pl.BlockSpec((t, SLOT), lambda h, i: (i, do_off + h))],
        out_specs=[pl.BlockSpec((t, SLOT), lambda h, i: (i, h)), mem_out, mem_out],
        out_shape=[jax.ShapeDtypeStruct((rows, MEM_HEADS * SLOT), F32),
                   jax.ShapeDtypeStruct((N_MEM, MEM_HEADS * SLOT), F32),
                   jax.ShapeDtypeStruct((N_MEM, MEM_HEADS * SLOT), F32)],
        compiler_params=_cparams(),
    )(proj, kvmem, kvmem, do)


def _place():
    return lax.axis_index("x"), lax.axis_index("y"), lax.axis_index("c")


def _flip(v, bit):
    return 1 - v if bit else v


def _all_gather(blocks, name):
    nb = len(blocks)

    def body(*refs):
        x_refs, out_refs = refs[:nb], refs[nb:2 * nb]
        send_sems, recv_sems, local_sems = refs[2 * nb:]
        x, y, c = _place()
        me, sibling = (x, y, c), (x, y, 1 - c)
        chips = [(1 - x, y), (x, 1 - y), (1 - x, 1 - y)]

        def copy(b, k, blk, to, from_input=False):
            slot = out_refs[b].at[4 * blk[0] + 2 * blk[1] + blk[2]]
            return pltpu.make_async_remote_copy(
                src_ref=x_refs[b] if from_input else slot, dst_ref=slot,
                send_sem=send_sems.at[b, k], recv_sem=recv_sems.at[b, k],
                device_id=to, device_id_type=pl.DeviceIdType.MESH)

        mine = [pltpu.make_async_copy(x_refs[b], out_refs[b].at[4 * x + 2 * y + c], local_sems.at[b])
                for b in range(nb)]
        for cp in mine:
            cp.start()
        first = []
        for b in range(nb):
            first.append(copy(b, 0, me, sibling, from_input=True))
            first += [copy(b, 1 + n, me, (*chip, c), from_input=True) for n, chip in enumerate(chips)]
        for cp in first:
            cp.start()
        passed = []
        for n, chip in enumerate(chips):
            for b in range(nb):
                copy(b, 1 + n, (*chip, c), me).wait_recv()
                passed.append(copy(b, 4 + n, (*chip, c), sibling))
                passed[-1].start()
        for b in range(nb):
            copy(b, 0, sibling, me).wait_recv()
            for n, chip in enumerate(chips):
                copy(b, 4 + n, (*chip, 1 - c), me).wait_recv()
        for cp in first + passed:
            cp.wait_send()
        for cp in mine:
            cp.wait()

    any_spec = pl.BlockSpec(memory_space=pl.ANY)
    return pl.pallas_call(
        body, name=name, in_specs=[any_spec] * nb, out_specs=[any_spec] * nb,
        out_shape=[jax.ShapeDtypeStruct((N_DEV,) + blk.shape, blk.dtype) for blk in blocks],
        scratch_shapes=[pltpu.SemaphoreType.DMA((nb, 7)), pltpu.SemaphoreType.DMA((nb, 7)),
                        pltpu.SemaphoreType.DMA((nb,))],
    )(*blocks)


def _peers(x, y, c):
    out = []
    for n in range(1, N_DEV):
        peer = (_flip(x, n & 4), _flip(y, n & 2), _flip(c, n & 1))
        out.append((n - 1, peer, 4 * peer[0] + 2 * peer[1] + peer[2]))
    return out


_HBM = pl.BlockSpec(memory_space=pltpu.HBM)
_SEM = pl.BlockSpec(memory_space=pltpu.SEMAPHORE)


def _exchange_start(srcs, scatter, name, after=None):
    ns = len(srcs)
    lands = [lax.empty(s.shape if scatter else (N_DEV,) + s.shape, s.dtype) for s in srcs]

    def body(*refs):
        src_refs, land_refs = refs[:ns], refs[ns:2 * ns]
        pos = 2 * ns + (1 if after is not None else 0)
        send_sems, recv_sems, token = refs[pos], refs[pos + 1], refs[-1]
        x, y, c = _place()
        my_idx = 4 * x + 2 * y + c
        for col, peer, peer_idx in _peers(x, y, c):
            for b in range(ns):
                pltpu.make_async_remote_copy(
                    src_ref=src_refs[b].at[peer_idx] if scatter else src_refs[b], dst_ref=land_refs[b].at[my_idx],
                    send_sem=send_sems.at[b * (N_DEV - 1) + col], recv_sem=recv_sems.at[b * (N_DEV - 1) + col],
                    device_id=peer, device_id_type=pl.DeviceIdType.MESH).start()
        token[...] = jnp.zeros_like(token)

    args = [pltpu.with_memory_space_constraint(a, pltpu.HBM) for a in list(srcs) + lands]
    in_specs = [_HBM] * (2 * ns)
    if after is not None:
        args.append(after)
        in_specs.append(pl.BlockSpec(memory_space=pl.ANY))
    out = pl.pallas_call(
        body, name=name, in_specs=in_specs,
        out_specs=[_SEM, _SEM] + [_HBM] * (2 * ns) + [pl.BlockSpec(memory_space=pltpu.VMEM)],
        out_shape=[pltpu.SemaphoreType.DMA((ns * (N_DEV - 1),)), pltpu.SemaphoreType.DMA((ns * (N_DEV - 1),))]
        + [pltpu.HBM(a.shape, a.dtype) for a in list(srcs) + lands] + [jax.ShapeDtypeStruct((8, SLOT), F32)],
        input_output_aliases={k: 2 + k for k in range(2 * ns)},
        compiler_params=pltpu.CompilerParams(has_side_effects=pltpu.SideEffectType.DATAFLOW_SIDE_EFFECTING),
    )(*args)
    return (out[0], out[1], out[2:2 + ns], out[2 + ns:2 + 2 * ns], scatter), out[-1]


def _exchange_wait(handle, after, name):
    send_sems, recv_sems, srcs, lands, scatter = handle
    ns = len(srcs)

    def body(*refs):
        src_refs, land_refs = refs[:ns], refs[ns:2 * ns]
        send_ref, recv_ref = refs[2 * ns], refs[2 * ns + 1]
        x, y, c = _place()
        for col, peer, peer_idx in _peers(x, y, c):
            for b in range(ns):
                copy = pltpu.make_async_remote_copy(
                    src_ref=src_refs[b].at[peer_idx] if scatter else src_refs[b], dst_ref=land_refs[b].at[peer_idx],
                    send_sem=send_ref.at[b * (N_DEV - 1) + col], recv_sem=recv_ref.at[b * (N_DEV - 1) + col],
                    device_id=peer, device_id_type=pl.DeviceIdType.MESH)
                copy.wait_send()
                copy.wait_recv()

    out = pl.pallas_call(
        body, name=name, in_specs=[_HBM] * (2 * ns) + [_SEM, _SEM, pl.BlockSpec(memory_space=pl.ANY)],
        out_specs=[_HBM] * (2 * ns),
        out_shape=[pltpu.HBM(a.shape, a.dtype) for a in list(srcs) + list(lands)],
        input_output_aliases={k: k for k in range(2 * ns)},
        compiler_params=pltpu.CompilerParams(has_side_effects=pltpu.SideEffectType.DATAFLOW_SIDE_EFFECTING),
    )(*srcs, *lands, send_sems, recv_sems, after)
    my_idx = 4 * lax.axis_index("x") + 2 * lax.axis_index("y") + lax.axis_index("c")
    landed = []
    for src, land in zip(out[:ns], out[ns:]):
        own = lax.dynamic_index_in_dim(src, my_idx, 0, keepdims=True) if scatter else src[None]
        landed.append(lax.dynamic_update_index_in_dim(land, own, my_idx, 0))
    return landed


def _adamw(parts, w, m, v, name):
    lyr, rows, cols = w.shape
    assert len(parts) == lyr
    tr = ADAM_ROWS if cols > 512 else 2 * ADAM_ROWS
    while rows % tr:
        tr //= 2
    tr = min(tr, rows)

    def body(*refs):
        p_refs = refs[:lyr]
        w_ref, m_ref, v_ref, g_out, d_out, m_out, v_out = refs[lyr:]
        for k in range(lyr):
            @pl.when(pl.program_id(0) == k)
            def _(p_ref=p_refs[k]):
                g = p_ref[0].astype(F32)
                for s in range(1, N_DEV):
                    g = g + p_ref[s].astype(F32)
                m2 = ADAM_B1 * m_ref[...] + (1.0 - ADAM_B1) * g
                v2 = ADAM_B2 * v_ref[...] + (1.0 - ADAM_B2) * (g * g)
                m_hat = m2 / (1.0 - ADAM_B1 ** ADAM_STEP)
                v_hat = v2 / (1.0 - ADAM_B2 ** ADAM_STEP)
                g_out[...] = g
                d_out[...] = -ADAM_LR * (m_hat / (jnp.sqrt(v_hat) + ADAM_EPS) + ADAM_WD * w_ref[...])
                m_out[...] = m2
                v_out[...] = v2

    def part_spec(k):
        return pl.BlockSpec((N_DEV, tr, cols), lambda l, i: (0, jnp.where(l == k, i, 0), 0))

    spec = pl.BlockSpec((None, tr, cols), lambda l, i: (l, i, 0))
    shp = jax.ShapeDtypeStruct((lyr, rows, cols), F32)
    return pl.pallas_call(
        body, name=name, grid=(lyr, rows // tr),
        in_specs=[part_spec(k) for k in range(lyr)] + [spec, spec, spec],
        out_specs=[spec] * 4, out_shape=[shp] * 4, compiler_params=_cparams(),
    )(*parts, w, m, v)


def _pack(arrays, lanes, row_mult, dtype):
    flat = jnp.concatenate([a.reshape(-1).astype(dtype) for a in arrays])
    unit = lanes * row_mult
    total = -(-flat.shape[0] // unit) * unit
    return jnp.pad(flat, (0, total - flat.shape[0])).reshape(total // lanes, lanes)


def _unpack(packed, shapes):
    flat = packed.reshape(-1)
    out, off = [], 0
    for shp in shapes:
        n = 1
        for d in shp:
            n *= d
        out.append(flat[off:off + n].reshape(shp))
        off += n
    return out


def _pad_slots(w, axis):
    axis = axis % w.ndim
    n = w.shape[axis] // HEAD_DIM
    shp = w.shape[:axis] + (n, HEAD_DIM) + w.shape[axis + 1:]
    pad = [(0, 0)] * (w.ndim + 1)
    pad[axis + 1] = (0, SLOT - HEAD_DIM)
    return jnp.pad(w.reshape(shp), pad).reshape(w.shape[:axis] + (n * SLOT,) + w.shape[axis + 1:])


def _unpad_slots(w, axis, keep=HEAD_DIM):
    axis = axis % w.ndim
    n = w.shape[axis] // SLOT
    shp = w.shape[:axis] + (n, SLOT) + w.shape[axis + 1:]
    idx = [slice(None)] * (w.ndim + 1)
    idx[axis + 1] = slice(0, keep)
    return w.reshape(shp)[tuple(idx)].reshape(w.shape[:axis] + (n * keep,) + w.shape[axis + 1:])


def _mla_in_pad(w):
    z = functools.partial(jnp.zeros, dtype=w.dtype)
    rows = w.shape[0]
    return jnp.concatenate([w[:, :384], z((rows, 64)), w[:, 640:672], z((rows, 32)), w[:, 384:640],
                            _pad_slots(w[:, 672:], 1)], axis=1)


def _mla_in_unpad(d):
    return jnp.concatenate([d[:, :384], d[:, 512:768], d[:, 448:480], _unpad_slots(d[:, 768:], 1)], axis=1)


def _mla_uq_pad(w):
    return jnp.pad(w.reshape(w.shape[0], MLA_HEADS, MLA_QK), ((0, 0), (0, 0), (0, SLOT - MLA_QK))).reshape(
        w.shape[0], MLA_HEADS * SLOT)


def _join(gathered, axis):
    nd, a, b = gathered.shape
    if axis == 1:
        return gathered.reshape(nd * a, b)
    return gathered.transpose(1, 0, 2).reshape(a, nd * b)


def _split(full, axis):
    r, c = full.shape
    if axis == 1:
        return full.reshape(N_DEV, r // N_DEV, c).astype(BF16)
    return full.reshape(r, N_DEV, c // N_DEV).transpose(1, 0, 2).astype(BF16)


def kernel(x, mem, positions, attn_norm_g, mlp_norm_g, mem_norm_g, final_norm_g, mla_w_in, mla_q_norm_g, mla_kv_norm_g, mla_w_uq, mla_w_ukv, swa_w_in, swa_sinks, w_mem_kv, w_o, mlp_w_up, mlp_w_down, loss_target, m_attn_norm_g, m_mlp_norm_g, m_mem_norm_g, m_final_norm_g, m_mla_w_in, m_mla_q_norm_g, m_mla_kv_norm_g, m_mla_w_uq, m_mla_w_ukv, m_swa_w_in, m_swa_sinks, m_w_mem_kv, m_w_o, m_mlp_w_up, m_mlp_w_down, v_attn_norm_g, v_mlp_norm_g, v_mem_norm_g, v_final_norm_g, v_mla_w_in, v_mla_q_norm_g, v_mla_kv_norm_g, v_mla_w_uq, v_mla_w_ukv, v_swa_w_in, v_swa_sinks, v_w_mem_kv, v_w_o, v_mlp_w_up, v_mlp_w_down):
    given = dict(locals())
    seq = x.shape[1]
    x0 = x.reshape(seq, D_MODEL)
    tgt = loss_target.reshape(seq, D_MODEL)
    mem0 = mem.reshape(N_MEM, D_MODEL)
    pos = positions.reshape(seq).astype(F32)
    pos_col, pos_row = pos.reshape(seq, 1), pos.reshape(1, seq)

    def layer_names(i):
        mixer = ("mla_w_in", "mla_w_uq", "mla_w_ukv") if i % 2 == 0 else ("swa_w_in",)
        return [(n, i // 2) for n in mixer] + [(n, i) for n in ("w_mem_kv", "w_o", "mlp_w_up", "mlp_w_down")]

    def local_weights(names):
        return [given[n][l].astype(BF16) for n, l in names]

    first_attn, first_mlp = layer_names(0)[:-2], layer_names(0)[-2:]
    weights = [dict(zip([n for n, _ in first_attn], _all_gather(local_weights(first_attn), "gather_weights_first")))]
    coming_mlp, first_token = _exchange_start(local_weights(first_mlp), False, "gather_weights_start_0",
                                              after=weights[0]["w_o"])

    consts = _lane_consts()
    tabs = _rope_tables(pos_col, consts)
    slopes = 2.0 ** (-8.0 * (jnp.arange(SWA_HEADS, dtype=F32) + 1.0) / SWA_HEADS)

    mem_n = _rmsnorm_fwd(mem0, 0, D_MODEL, mem_norm_g, "rmsnorm_fwd_mem")

    saved = []
    xc = x0
    for i in range(DEPTH):
        j = i // 2
        wts = weights[i]
        s = {"x_in": xc}
        token = None
        if i + 1 < DEPTH:
            coming, token = _exchange_start(local_weights(layer_names(i + 1)), False,
                                            "gather_weights_start_%d" % (i + 1),
                                            after=first_token if i == 0 else wts["w_o"])
        hn = _rmsnorm_fwd(xc, 0, D_MODEL, attn_norm_g[i], "rmsnorm_fwd", after=token)
        if i % 2 == 0:
            w_in = _mla_in_pad(_join(wts["mla_w_in"], 1))
            w_uq = _mla_uq_pad(_join(wts["mla_w_uq"], 2))
            w_kv = _join(wts["mla_w_ukv"], 2)
            proj = _mm(hn, w_in, "nn", F32, "mm_mla_in")
            cqn = _rmsnorm_fwd(proj, 0, MLA_Q_RANK, mla_q_norm_g[j], "rmsnorm_fwd_q")
            ckvn = _rmsnorm_fwd(proj, 2, MLA_KV_RANK, mla_kv_norm_g[j], "rmsnorm_fwd_kv")
            qraw = _mm(cqn, w_uq, "nn", F32, "mm_mla_uq")
            kvraw = _mm(ckvn, w_kv, "nn", F32, "mm_mla_ukv")
            q, k, v = _mla_rope_fwd(qraw, kvraw, proj, tabs)
            o, lse = _mla_attn_fwd(q, k, v)
            qoff = MLA_QOFF
            s.update(w_uq=w_uq, w_kv=w_kv, cqn=cqn, ckvn=ckvn, q=q, k=k, v=v)
        else:
            w_in = _join(wts["swa_w_in"], 2)
            proj = _mm(hn, w_in, "nn", BF16, "mm_swa_in", pairs="o")
            o, lse = _swa_attn_fwd(proj, pos_col, pos_row, slopes, swa_sinks[j])
            qoff = SWA_QOFF
        w_mem = _pad_slots(_join(wts["w_mem_kv"], 1), 1)
        w_out = _join(wts["w_o"], 1)
        w_o_mix, w_o_cross = w_out[:SWA_HEADS * HEAD_DIM], w_out[SWA_HEADS * HEAD_DIM:]
        kvmem = _mm(mem_n, w_mem, "nn", BF16, "mm_mem_kv")
        cross = _cross_attn_fwd(proj, qoff, kvmem)
        x1 = _mm(o, w_o_mix, "nn", F32, "mm_o", res=xc, pairs="a", second=(cross, w_o_cross))
        hn2 = _rmsnorm_fwd(x1, 0, D_MODEL, mlp_norm_g[i], "rmsnorm_fwd")
        if i == 0:
            wts.update(zip([n for n, _ in first_mlp], _exchange_wait(coming_mlp, hn2, "gather_weights_wait_0")))
        act, act2 = _mm(hn2, wts["mlp_w_up"], "nn", BF16, "mm_mlp_up", epi="relu2", b_blk="cols")
        xc = _mm(act2, wts["mlp_w_down"], "nn", F32, "mm_mlp_down", res=x1, b_blk="rows")
        s.update(hn=hn, w_in=w_in, proj=proj, o=o, lse=lse, qoff=qoff, w_mem=w_mem, w_out=w_out,
                 kvmem=kvmem, cross=cross, x1=x1, hn2=hn2, act=act, act2=act2)
        saved.append(s)
        if i + 1 < DEPTH:
            got = _exchange_wait(coming, xc, "gather_weights_wait_%d" % (i + 1))
            weights.append(dict(zip([n for n, _ in layer_names(i + 1)], got)))

    dx, dx_b, dg_final, loss_part = _loss_head(xc, final_norm_g, tgt)
    loss = lax.psum(loss_part[0, 0], MESH_AXES)

    gains = {n: [None] * DEPTH for n in ("attn_norm_g", "mlp_norm_g")}
    for n in ("mla_q_norm_g", "mla_kv_norm_g", "swa_sinks"):
        gains[n] = [None] * 2
    leaving = {}
    token = None
    dmem_n = None
    for i in reversed(range(DEPTH)):
        j = i // 2
        s = saved[i]
        wts = weights[i]
        out = {}
        du = _mm(dx_b, wts["mlp_w_down"], "nt", BF16, "mm_mlp_down_dx", aux=s["act"], epi="mul2aux", b_blk="rows",
                 after=token)
        out["mlp_w_down"] = _mm(s["act2"], dx_b, "tn", BF16, "mm_mlp_down_dw", o_blk="rows")
        out["mlp_w_up"] = _mm(s["hn2"], du, "tn", BF16, "mm_mlp_up_dw", o_blk="cols")
        dx1, dx1_b, dg = _mm(du, wts["mlp_w_up"], "nt", F32, "mm_mlp_up_dx", b_blk="cols",
                             epi="normbwd", norm=(s["x1"], mlp_norm_g[i], dx))
        gains["mlp_norm_g"][i] = dg[0]

        do = _mm(dx1_b, s["w_out"], "nt", BF16, "mm_o_dx", pairs="o")
        dw_o = jnp.concatenate([_mm(s["o"], dx1_b, "tn", F32, "mm_o_mix_dw", pairs="a"),
                                _mm(s["cross"], dx1_b, "tn", F32, "mm_o_cross_dw", pairs="a")], axis=0)
        out["w_o"] = _split(dw_o, 1)
        dqc, dkm, dvm = _cross_attn_bwd(s["proj"], s["qoff"], s["kvmem"], do, SWA_HEADS)
        dkvmem = jnp.concatenate([dkm, dvm], axis=1).astype(BF16)
        out["w_mem_kv"] = _split(_unpad_slots(_mm(mem_n, dkvmem, "tn", F32, "mm_mem_kv_dw"), 1), 1)
        dmem_n = _mm(dkvmem, s["w_mem"], "nt", F32, "mm_mem_kv_dx" if dmem_n is None else "mm_mem_kv_dx_acc",
                     res=dmem_n)
        leaving[(i, "main")], token = _exchange_start([out[n] for n, _ in layer_names(i)[-4:]], True,
                                                      "exchange_grads_main_start_%d" % i)

        if i % 2 == 0:
            dq, dk, dv = _mla_attn_bwd(s["q"], s["k"], s["v"], do, s["lse"], _mla_delta(s["o"], do), token)
            dqraw, dkv, dkr = _mla_rope_bwd(dq, dk, dv, tabs, consts)
            dcqn = _mm(dqraw, s["w_uq"], "nt", F32, "mm_mla_uq_dx")
            out["mla_w_uq"] = _split(_unpad_slots(_mm(s["cqn"], dqraw, "tn", F32, "mm_mla_uq_dw"), 1, MLA_QK), 2)
            dckvn = _mm(dkv, s["w_kv"], "nt", F32, "mm_mla_ukv_dx")
            out["mla_w_ukv"] = _split(_mm(s["ckvn"], dkv, "tn", F32, "mm_mla_ukv_dw"), 2)
            dcq, dg = _rmsnorm_bwd(s["proj"], 0, MLA_Q_RANK, mla_q_norm_g[j], dcqn, None, BF16, "rmsnorm_bwd_q")
            gains["mla_q_norm_g"][j] = dg[0]
            dckv, dg = _rmsnorm_bwd(s["proj"], 2, MLA_KV_RANK, mla_kv_norm_g[j], dckvn, None, BF16, "rmsnorm_bwd_kv")
            gains["mla_kv_norm_g"][j] = dg[0]
            dproj = jnp.concatenate([dcq, dkr.astype(BF16), dckv, dqc.astype(BF16)], axis=1)
            in_dx = "mm_mla_in_dx"
            out["mla_w_in"] = _split(_mla_in_unpad(_mm(s["hn"], dproj, "tn", F32, "mm_mla_in_dw")), 1)
        else:
            dq, dk, dv, dsink = _swa_attn_bwd(s["proj"], s["o"], do, s["lse"], pos_col, pos_row, slopes, swa_sinks[j],
                                              token)
            gains["swa_sinks"][j] = dsink[::8, 0]
            dproj = jnp.concatenate([dq, dk, dv, dqc], axis=1).astype(BF16)
            in_dx = "mm_swa_in_dx"
            out["swa_w_in"] = _split(_mm(s["hn"], dproj, "tn", F32, "mm_swa_in_dw", pairs="b"), 2)
        dx, dx_b, dg = _mm(dproj, s["w_in"], "nt", F32, in_dx, epi="normbwd", norm=(s["x_in"], attn_norm_g[i], dx1),
                           pairs="" if i % 2 == 0 else "a")
        gains["attn_norm_g"][i] = dg[0]

        leaving[(i, "mixer")], token = _exchange_start([out[n] for n, _ in layer_names(i)[:-4]], True,
                                                       "exchange_grads_mixer_start_%d" % i)

    _, dg_mem = _rmsnorm_bwd(mem0, 0, D_MODEL, mem_norm_g, dmem_n, None, BF16, "rmsnorm_bwd_mem")
    gains = {n: jnp.stack(g) for n, g in gains.items()}
    gains["mem_norm_g"] = dg_mem[0]
    gains["final_norm_g"] = dg_final[0]

    result = {}

    def adamw_of(names, received):
        for n in names:
            parts = [received[(n, l)] for l in range(given[n].shape[0])]
            for kind, r in enumerate(_adamw(parts, given[n], given["m_" + n], given["v_" + n], "adamw_" + n)):
                result[(kind, n)] = r

    received = {}
    for i in reversed(range(DEPTH)):
        got = _exchange_wait(leaving[(i, "main")], dx, "exchange_grads_main_wait_%d" % i)
        received.update(zip(layer_names(i)[-4:], got))
    adamw_of(("mlp_w_up", "mlp_w_down", "w_o", "w_mem_kv"), received)
    for i in reversed(range(DEPTH)):
        got = _exchange_wait(leaving[(i, "mixer")], result[(0, "w_mem_kv")], "exchange_grads_mixer_wait_%d" % i)
        received.update(zip(layer_names(i)[:-4], got))
    adamw_of(("mla_w_in", "mla_w_uq", "mla_w_ukv", "swa_w_in"), received)

    rep_shapes = [given[n].shape for n in REPLICATED]
    rep_parts = _all_gather([_pack([gains[n] for n in REPLICATED], SLOT, 8, F32)], "gather_gain_grads")[0]
    rep_packed = [_pack([given[p + n] for n in REPLICATED], SLOT, 8, F32)[None] for p in ("", "m_", "v_")]
    for kind, r in enumerate(_adamw([rep_parts], *rep_packed, "adamw_gains")):
        for n, part in zip(REPLICATED, _unpack(r[0], rep_shapes)):
            result[(kind, n)] = part

    outs = [loss, dx.reshape(1, seq, D_MODEL)]
    for kind in range(4):
        outs += [result[(kind, n)] for n in WEIGHT_ORDER]
    return tuple(outs)
```

```python
import functools

import jax
import jax.numpy as jnp
from jax import lax
from jax.experimental import pallas as pl
from jax.experimental.pallas import tpu as pltpu

F32 = jnp.float32
BF16 = jnp.bfloat16

D_MODEL = 1024
N_MEM = 256
DEPTH = 4
SLOT = 128
HEAD_DIM = 64
MLA_HEADS = 12
MLA_QK = 96
MLA_Q_RANK = 384
MLA_KV_RANK = 256
SWA_HEADS = 12
SWA_KV_HEADS = 4
SWA_GROUP = 3
MEM_HEADS = 4
WINDOW = 128
EPS = 1e-6
NEG = -1e30
ROPE_THETA = 10000.0
N_DEV = 8

ADAM_LR = 0.001
ADAM_B1 = 0.9
ADAM_B2 = 0.999
ADAM_EPS = 1e-08
ADAM_WD = 0.01
ADAM_STEP = 10

TM = 1024
TM_ROPE = 512
TQ_MLA = 1024
MLA_PACK = 4
SWA_PACK = 4
TQ_CROSS = 4096
MM_VMEM_BUDGET = 44 * 1024 * 1024
ADAM_ROWS = 256
VMEM_LIMIT = 60 * 1024 * 1024

MESH_AXES = ("x", "y", "c")

LOG2_E = 1.4426950408889634
MLA_SCALE = MLA_QK ** -0.5
MLA_Q_SCALE = MLA_SCALE * LOG2_E

MLA_QOFF = (MLA_Q_RANK + SLOT + MLA_KV_RANK) // SLOT
SWA_QOFF = SWA_HEADS + 2 * SWA_KV_HEADS

SHARDED = (
    ("mla_w_in", 1), ("mla_w_uq", 2), ("mla_w_ukv", 2), ("swa_w_in", 2),
    ("w_mem_kv", 1), ("w_o", 1), ("mlp_w_up", 2), ("mlp_w_down", 1),
)
REPLICATED = ("attn_norm_g", "mlp_norm_g", "mem_norm_g", "final_norm_g",
              "mla_q_norm_g", "mla_kv_norm_g", "swa_sinks")
WEIGHT_ORDER = ("attn_norm_g", "mlp_norm_g", "mem_norm_g", "final_norm_g", "mla_w_in",
                "mla_q_norm_g", "mla_kv_norm_g", "mla_w_uq", "mla_w_ukv", "swa_w_in",
                "swa_sinks", "w_mem_kv", "w_o", "mlp_w_up", "mlp_w_down")


def _cparams():
    return pltpu.CompilerParams(vmem_limit_bytes=VMEM_LIMIT)


_DIMS = {"nn": (((1,), (0,)), ((), ())), "nt": (((1,), (1,)), ((), ())), "tn": (((0,), (0,)), ((), ()))}


def _compact(x):
    pairs = [x[:, 2 * j * SLOT:(2 * j + 1) * SLOT] + pltpu.roll(x[:, (2 * j + 1) * SLOT:(2 * j + 2) * SLOT], HEAD_DIM, 1)
             for j in range(x.shape[1] // (2 * SLOT))]
    return pairs[0] if len(pairs) == 1 else jnp.concatenate(pairs, axis=1)


def _expand(x):
    low = lax.broadcasted_iota(jnp.int32, (x.shape[0], SLOT), 1) < HEAD_DIM
    slots = []
    for j in range(x.shape[1] // SLOT):
        pair = x[:, j * SLOT:(j + 1) * SLOT]
        slots += [jnp.where(low, pair, 0.0), pltpu.roll(jnp.where(low, 0.0, pair), HEAD_DIM, 1)]
    return jnp.concatenate(slots, axis=1)


def _mm_tiles(m, n, k, a_bytes, b_bytes, o_bytes, extra_bytes, tm_fixed, tn_fixed):
    best = None
    for tm in ([tm_fixed] if tm_fixed else [t for t in range(4096, 0, -SLOT) if m % t == 0] or [m]):
        for tn in ([tn_fixed] if tn_fixed else [t for t in range(1024, 0, -SLOT) if n % t == 0] or [n]):
            need = 2 * (tm * k * a_bytes + k * tn * b_bytes + tm * tn * (o_bytes + extra_bytes))
            need += tm * tn * 4
            if need <= MM_VMEM_BUDGET and (best is None or tm * tn > best[0] * best[1]):
                best = (tm, tn)
    assert best is not None, (m, n, k)
    return best


def _mm(a, b, mode, out_dtype, name, res=None, aux=None, epi=None, b_blk=None, o_blk=None, after=None, norm=None,
        pairs="", second=None):
    if b_blk is not None:
        nb, br, bc = b.shape
        b_shape = (nb * br, bc) if b_blk == "rows" else (br, nb * bc)
    else:
        b_shape = b.shape
    assert not pairs or (b_blk is None and o_blk is None and not ("b" in pairs and mode == "nt"))
    a_shape = (a.shape[0], a.shape[1] // 2) if "a" in pairs else a.shape
    if "b" in pairs:
        b_shape = (b_shape[0], b_shape[1] // 2)
    if mode == "nn":
        (m, k), (k2, n) = a_shape, b_shape
    elif mode == "nt":
        (m, k), (n, k2) = a_shape, b_shape
    else:
        (k, m), (k2, n) = a_shape, b_shape
    assert k == k2, (a.shape, b_shape, mode)
    assert second is None or (mode == "nn" and b_blk is None and second[0].shape[0] == m and second[1].shape[1] == n)
    k_second = 0 if second is None else second[1].shape[0]
    k_blocked = b_blk is not None and (b_blk == "rows") == (mode != "nt")
    tn_fixed = None
    if b_blk is not None and not k_blocked:
        tn_fixed = br if b_blk == "rows" else bc
    if o_blk == "cols":
        tn_fixed = n // N_DEV
    tm_fixed = m // N_DEV if o_blk == "rows" else None
    has_res, has_aux, has_norm = res is not None, aux is not None, epi == "normbwd"
    assert o_blk is None or not (has_res or has_aux or has_norm)
    n_out = 2 if epi == "relu2" else 1
    if has_norm:
        tn_fixed = n
        o_bytes, extra_bytes = 4 + 2, 4 + 4
    else:
        o_bytes = n_out * jnp.dtype(out_dtype).itemsize
        extra_bytes = (4 if has_res else 0) + (aux.dtype.itemsize if has_aux else 0)
    pa, pb, po = (2 if "a" in pairs else 1), (2 if "b" in pairs else 1), (2 if "o" in pairs else 1)
    tm, tn = _mm_tiles(m, n, k + k_second, a.dtype.itemsize * (3 if pa == 2 else 1),
                       b.dtype.itemsize * (3 if pb == 2 else 1), o_bytes * po, extra_bytes, tm_fixed, tn_fixed)
    dims = _DIMS[mode]
    if mode == "tn":
        a_spec = pl.BlockSpec((k, pa * tm), lambda i, j: (0, i))
    else:
        a_spec = pl.BlockSpec((tm, pa * k), lambda i, j: (i, 0))
    if b_blk is None:
        if mode == "nt":
            b_spec = pl.BlockSpec((tn, k), lambda i, j: (j, 0))
        else:
            b_spec = pl.BlockSpec((k, pb * tn), lambda i, j: (0, j))
    elif k_blocked and mode == "nt":
        b_spec = pl.BlockSpec((N_DEV, tn, bc), lambda i, j: (0, j, 0))
    elif k_blocked:
        b_spec = pl.BlockSpec((N_DEV, br, tn), lambda i, j: (0, 0, j))
    elif mode == "nt":
        b_spec = pl.BlockSpec((None, tn, k), lambda i, j: (j, 0, 0))
    else:
        b_spec = pl.BlockSpec((None, k, tn), lambda i, j: (j, 0, 0))
    if o_blk is None:
        o_spec = pl.BlockSpec((tm, po * tn), lambda i, j: (i, j))
        o_shape = (m, po * n)
    elif o_blk == "rows":
        o_spec = pl.BlockSpec((None, tm, tn), lambda i, j: (i, 0, j))
        o_shape = (N_DEV, tm, n)
    else:
        o_spec = pl.BlockSpec((None, tm, tn), lambda i, j: (j, i, 0))
        o_shape = (N_DEV, m, tn)

    def body(*refs):
        a_ref, b_ref = refs[0], refs[1]
        pos = 2
        res_ref = aux_ref = None
        if has_res:
            res_ref = refs[pos]
            pos += 1
        if has_aux:
            aux_ref = refs[pos]
            pos += 1
        if has_norm:
            x_ref, g_ref, dres_ref = refs[pos:pos + 3]
            pos += 3
        if second is not None:
            a2_ref, b2_ref = refs[pos:pos + 2]
            pos += 2
        if after is not None:
            pos += 1
        outs = refs[pos:]
        if k_blocked and mode == "nt":
            r = None
            for d in range(N_DEV):
                part = lax.dot_general(a_ref[:, d * bc:(d + 1) * bc].astype(BF16), b_ref[d].astype(BF16), dims,
                                       preferred_element_type=F32)
                r = part if r is None else r + part
        else:
            bv = b_ref[...].reshape(k, tn) if k_blocked else b_ref[...]
            av = _compact(a_ref[...].astype(F32)) if pa == 2 else a_ref[...]
            bv = _compact(bv.astype(F32)) if pb == 2 else bv
            r = lax.dot_general(av.astype(BF16), bv.astype(BF16), dims, preferred_element_type=F32)
        if second is not None:
            av2 = _compact(a2_ref[...].astype(F32)) if pa == 2 else a2_ref[...]
            r = r + lax.dot_general(av2.astype(BF16), b2_ref[...].astype(BF16), dims, preferred_element_type=F32)
        if po == 2:
            r = _expand(r)
        if epi == "relu2":
            r = jnp.maximum(r, 0.0)
            outs[0][...] = r.astype(outs[0].dtype)
            outs[1][...] = (r * r).astype(outs[1].dtype)
        elif has_norm:
            xv = x_ref[...]
            rs = lax.rsqrt(jnp.mean(xv * xv, axis=1, keepdims=True) + EPS)
            xh = xv * rs
            dxh = r * g_ref[...]
            dx = rs * (dxh - xh * jnp.mean(dxh * xh, axis=1, keepdims=True)) + dres_ref[...]
            outs[0][...] = dx
            outs[1][...] = dx.astype(BF16)

            @pl.when(pl.program_id(0) == 0)
            def _():
                outs[2][...] = jnp.zeros_like(outs[2])

            outs[2][...] += jnp.sum(r * xh, axis=0, keepdims=True)
        else:
            if epi == "mul2aux":
                r = r * (2.0 * aux_ref[...].astype(F32))
            if has_res:
                r = r + res_ref[...]
            outs[0][...] = r.astype(outs[0].dtype)

    in_specs = [a_spec, b_spec]
    args = [a, b]
    if has_res:
        in_specs.append(o_spec)
        args.append(res)
    if has_aux:
        in_specs.append(o_spec)
        args.append(aux)
    vec_spec = pl.BlockSpec((1, n), lambda i, j: (0, 0))
    if has_norm:
        in_specs += [o_spec, vec_spec, o_spec]
        args += [norm[0], norm[1].reshape(1, n), norm[2]]
    if second is not None:
        in_specs += [pl.BlockSpec((tm, pa * k_second), lambda i, j: (i, 0)),
                     pl.BlockSpec((k_second, tn), lambda i, j: (0, j))]
        args += list(second)
    if after is not None:
        in_specs.append(pl.BlockSpec(memory_space=pl.ANY))
        args.append(after)
    if has_norm:
        out_specs = [o_spec, o_spec, vec_spec]
        out_shape = [jax.ShapeDtypeStruct(o_shape, F32), jax.ShapeDtypeStruct(o_shape, BF16),
                     jax.ShapeDtypeStruct((1, n), F32)]
    else:
        out_specs = [o_spec] * n_out
        out_shape = [jax.ShapeDtypeStruct(o_shape, out_dtype)] * n_out
    out = pl.pallas_call(
        body, name=name, grid=(m // tm, n // tn),
        in_specs=in_specs, out_specs=out_specs, out_shape=out_shape, compiler_params=_cparams(),
    )(*args)
    return out if len(out) > 1 else out[0]


def _rmsnorm_fwd(xarr, colblk, width, g, name, after=None):
    rows = xarr.shape[0]
    tm = min(TM, rows)

    def body(x_ref, g_ref, *rest):
        y_ref = rest[-1]
        x = x_ref[...].astype(F32)
        r = lax.rsqrt(jnp.mean(x * x, axis=1, keepdims=True) + EPS)
        y_ref[...] = (x * r * g_ref[...]).astype(y_ref.dtype)

    in_specs = [pl.BlockSpec((tm, width), lambda i: (i, colblk)), pl.BlockSpec((1, width), lambda i: (0, 0))]
    args = [xarr, g.reshape(1, width)]
    if after is not None:
        in_specs.append(pl.BlockSpec(memory_space=pl.ANY))
        args.append(after)
    return pl.pallas_call(
        body, name=name, grid=(rows // tm,), in_specs=in_specs,
        out_specs=pl.BlockSpec((tm, width), lambda i: (i, 0)),
        out_shape=jax.ShapeDtypeStruct((rows, width), BF16), compiler_params=_cparams(),
    )(*args)


def _rmsnorm_bwd(xarr, colblk, width, g, dy, dres, out_dtype, name):
    rows = xarr.shape[0]
    tm = min(TM, rows)
    has_res = dres is not None

    def body(*refs):
        x_ref, g_ref, dy_ref = refs[0], refs[1], refs[2]
        dres_ref = refs[3] if has_res else None
        dx_ref, dg_ref = refs[-2], refs[-1]
        x = x_ref[...].astype(F32)
        dyv = dy_ref[...].astype(F32)
        r = lax.rsqrt(jnp.mean(x * x, axis=1, keepdims=True) + EPS)
        xh = x * r
        dxh = dyv * g_ref[...]
        dx = r * (dxh - xh * jnp.mean(dxh * xh, axis=1, keepdims=True))
        if has_res:
            dx = dx + dres_ref[...]
        dx_ref[...] = dx.astype(dx_ref.dtype)

        @pl.when(pl.program_id(0) == 0)
        def _():
            dg_ref[...] = jnp.zeros_like(dg_ref)

        dg_ref[...] += jnp.sum(dyv * xh, axis=0, keepdims=True)

    row_spec = pl.BlockSpec((tm, width), lambda i: (i, 0))
    vec_spec = pl.BlockSpec((1, width), lambda i: (0, 0))
    in_specs = [pl.BlockSpec((tm, width), lambda i: (i, colblk)), vec_spec, row_spec]
    args = [xarr, g.reshape(1, width), dy]
    if has_res:
        in_specs.append(row_spec)
        args.append(dres)
    return pl.pallas_call(
        body, name=name, grid=(rows // tm,), in_specs=in_specs, out_specs=[row_spec, vec_spec],
        out_shape=[jax.ShapeDtypeStruct((rows, width), out_dtype), jax.ShapeDtypeStruct((1, width), F32)],
        compiler_params=_cparams(),
    )(*args)


def _loss_head(x, g, tgt):
    rows, width = x.shape
    tm = min(TM, rows)

    def body(x_ref, g_ref, t_ref, dx_ref, dxb_ref, dg_ref, loss_ref):
        xv = x_ref[...]
        gv = g_ref[...]
        r = lax.rsqrt(jnp.mean(xv * xv, axis=1, keepdims=True) + EPS)
        xh = xv * r
        err = xh * gv - t_ref[...]
        part = 0.5 * jnp.sum(jnp.mean(err * err, axis=1, keepdims=True), axis=0, keepdims=True)
        dyv = err * (1.0 / width)
        dxh = dyv * gv
        dxv = r * (dxh - xh * jnp.mean(dxh * xh, axis=1, keepdims=True))
        dx_ref[...] = dxv
        dxb_ref[...] = dxv.astype(BF16)

        @pl.when(pl.program_id(0) == 0)
        def _():
            dg_ref[...] = jnp.zeros_like(dg_ref)
            loss_ref[...] = jnp.zeros_like(loss_ref)

        dg_ref[...] += jnp.sum(dyv * xh, axis=0, keepdims=True)
        loss_ref[...] += jnp.broadcast_to(part, loss_ref.shape)

    row_spec = pl.BlockSpec((tm, width), lambda i: (i, 0))
    vec_spec = pl.BlockSpec((1, width), lambda i: (0, 0))
    return pl.pallas_call(
        body, name="loss_head", grid=(rows // tm,), in_specs=[row_spec, vec_spec, row_spec],
        out_specs=[row_spec, row_spec, vec_spec, pl.BlockSpec((1, SLOT), lambda i: (0, 0))],
        out_shape=[jax.ShapeDtypeStruct((rows, width), F32), jax.ShapeDtypeStruct((rows, width), BF16),
                   jax.ShapeDtypeStruct((1, width), F32), jax.ShapeDtypeStruct((1, SLOT), F32)],
        compiler_params=_cparams(),
    )(x, g.reshape(1, width), tgt)


def _lane_consts():
    half = 16
    inv = ROPE_THETA ** (-(jnp.arange(half, dtype=F32) * 2.0) / 32)
    lane = jnp.arange(SLOT)
    first = (lane >= 64) & (lane < 80)
    second = (lane >= 80) & (lane < 96)
    inv_lane = jnp.where(first | second, inv[(lane - 64) % half], 0.0)
    rows = [inv_lane, (lane < 64).astype(F32), first.astype(F32), second.astype(F32)]
    rows += [jnp.zeros((SLOT,), F32)] * 4
    return jnp.stack(rows).astype(F32)


def _rope_tables(pos_col, consts):
    rows = pos_col.shape[0]
    tm = min(TM, rows)

    def body(p_ref, k_ref, c_ref, s1_ref, s2_ref):
        ang = p_ref[...] * k_ref[0:1, :]
        cos, sin = jnp.cos(ang), jnp.sin(ang)
        first, second = k_ref[2:3, :], k_ref[3:4, :]
        c_ref[...] = k_ref[1:2, :] + (first + second) * cos
        s1_ref[...] = -first * sin
        s2_ref[...] = second * sin

    spec = pl.BlockSpec((tm, SLOT), lambda i: (i, 0))
    shp = jax.ShapeDtypeStruct((rows, SLOT), F32)
    return pl.pallas_call(
        body, name="rope_tables", grid=(rows // tm,),
        in_specs=[pl.BlockSpec((tm, 1), lambda i: (i, 0)), pl.BlockSpec((8, SLOT), lambda i: (0, 0))],
        out_specs=[spec, spec, spec], out_shape=[shp, shp, shp], compiler_params=_cparams(),
    )(pos_col, consts)


def _rot(xv, c, s1, s2):
    return xv * c + pltpu.roll(xv, SLOT - 16, 1) * s1 + pltpu.roll(xv, 16, 1) * s2


def _rot_t(dy, c, s1, s2):
    return dy * c + pltpu.roll(dy * s1, 16, 1) + pltpu.roll(dy * s2, SLOT - 16, 1)


def _mla_rope_fwd(qraw, kvraw, proj, tabs):
    rows = qraw.shape[0]
    tm = min(TM_ROPE, rows)
    hw = MLA_HEADS * SLOT

    def body(q_ref, kv_ref, kr_ref, c_ref, s1_ref, s2_ref, qo, ko, vo):
        c, s1, s2 = c_ref[...], s1_ref[...], s2_ref[...]
        kr = _rot(kr_ref[...], c, s1, s2)
        low = lax.broadcasted_iota(jnp.int32, (tm, SLOT), 1) < HEAD_DIM
        for h in range(MLA_HEADS):
            sl = slice(h * SLOT, (h + 1) * SLOT)
            qo[:, sl] = (_rot(q_ref[:, sl], c, s1, s2) * MLA_Q_SCALE).astype(BF16)
            kvh = kv_ref[:, sl]
            ko[:, sl] = (jnp.where(low, kvh, 0.0) + kr).astype(BF16)
            vo[:, sl] = pltpu.roll(jnp.where(low, 0.0, kvh), HEAD_DIM, 1).astype(BF16)

    tab = pl.BlockSpec((tm, SLOT), lambda i: (i, 0))
    wide = pl.BlockSpec((tm, hw), lambda i: (i, 0))
    shp = jax.ShapeDtypeStruct((rows, hw), BF16)
    return pl.pallas_call(
        body, name="mla_rope_fwd", grid=(rows // tm,),
        in_specs=[wide, wide, pl.BlockSpec((tm, SLOT), lambda i: (i, 3)),
                  tab, tab, tab],
        out_specs=[wide, wide, wide], out_shape=[shp, shp, shp], compiler_params=_cparams(),
    )(qraw, kvraw, proj, *tabs)


def _mla_rope_bwd(dq, dk, dv, tabs, consts):
    rows = dq.shape[0]
    tm = min(TM_ROPE, rows)
    hw = MLA_HEADS * SLOT

    def body(dq_ref, dk_ref, dv_ref, c_ref, s1_ref, s2_ref, k_ref, dqo, dkvo, dkro):
        c, s1, s2 = c_ref[...], s1_ref[...], s2_ref[...]
        ksum = jnp.zeros((tm, SLOT), F32)
        low = lax.broadcasted_iota(jnp.int32, (tm, SLOT), 1) < HEAD_DIM
        for h in range(MLA_HEADS):
            sl = slice(h * SLOT, (h + 1) * SLOT)
            dqo[:, sl] = _rot_t(dq_ref[:, sl], c, s1, s2).astype(BF16)
            dkh = dk_ref[:, sl]
            ksum = ksum + dkh
            dvh = pltpu.roll(jnp.where(low, dv_ref[:, sl], 0.0), HEAD_DIM, 1)
            dkvo[:, sl] = (jnp.where(low, dkh, 0.0) + dvh).astype(BF16)
        dkro[...] = _rot_t(ksum, c, s1, s2) * (k_ref[2:3, :] + k_ref[3:4, :])

    tab = pl.BlockSpec((tm, SLOT), lambda i: (i, 0))
    wide = pl.BlockSpec((tm, hw), lambda i: (i, 0))
    return pl.pallas_call(
        body, name="mla_rope_bwd", grid=(rows // tm,),
        in_specs=[wide, wide, wide, tab, tab, tab, pl.BlockSpec((8, SLOT), lambda i: (0, 0))],
        out_specs=[wide, wide, tab],
        out_shape=[jax.ShapeDtypeStruct((rows, hw), BF16), jax.ShapeDtypeStruct((rows, hw), BF16),
                   jax.ShapeDtypeStruct((rows, SLOT), F32)],
        compiler_params=_cparams(),
    )(dq, dk, dv, *tabs, consts)


def _nt(a, b):
    return lax.dot_general(a, b, _DIMS["nt"], preferred_element_type=F32)


def _tn(a, b):
    return lax.dot_general(a, b, _DIMS["tn"], preferred_element_type=F32)


def _nn(a, b):
    return lax.dot_general(a, b, _DIMS["nn"], preferred_element_type=F32)


def _mla_attn_fwd(q, k, v):
    rows = q.shape[0]
    t = min(TQ_MLA, rows)
    nt = rows // t
    wide = MLA_PACK * SLOT

    def body(q_ref, k_ref, v_ref, o_ref, lse_ref, m_sc, l_sc, acc_sc):
        i, j = pl.program_id(1), pl.program_id(2)

        @pl.when(j == 0)
        def _():
            m_sc[...] = jnp.full_like(m_sc, NEG)
            l_sc[...] = jnp.zeros_like(l_sc)
            acc_sc[...] = jnp.zeros_like(acc_sc)

        def step(diagonal):
            for hh in range(MLA_PACK):
                sl = slice(hh * SLOT, (hh + 1) * SLOT)
                s = _nt(k_ref[:, sl], q_ref[:, sl])
                if diagonal:
                    key = lax.broadcasted_iota(jnp.int32, (t, t), 0)
                    s = jnp.where(key <= lax.broadcasted_iota(jnp.int32, (t, t), 1), s, NEG)
                m_prev = m_sc[hh]
                m_new = jnp.maximum(m_prev, jnp.max(s, axis=0, keepdims=True))
                p = jnp.exp2(s - m_new)
                alpha = jnp.exp2(m_prev - m_new)
                l_new = alpha * l_sc[hh] + jnp.sum(p, axis=0, keepdims=True)
                acc = alpha * acc_sc[hh] + _tn(v_ref[:, sl], p.astype(BF16))
                if diagonal:
                    o_ref[:, sl] = (acc / l_new).T.astype(o_ref.dtype)
                    lse_ref[hh:hh + 1, :] = m_new + jnp.log(l_new) * LOG2_E
                else:
                    m_sc[hh] = m_new
                    l_sc[hh] = l_new
                    acc_sc[hh] = acc

        @pl.when(j < i)
        def _():
            step(False)

        @pl.when(j == i)
        def _():
            lse_ref[...] = jnp.zeros_like(lse_ref)
            step(True)

    q_spec = pl.BlockSpec((t, wide), lambda h, i, j: (i, h))
    kv_spec = pl.BlockSpec((t, wide), lambda h, i, j: (jnp.minimum(j, i), h))
    return pl.pallas_call(
        body, name="mla_attn_fwd", grid=(MLA_HEADS // MLA_PACK, nt, nt),
        in_specs=[q_spec, kv_spec, kv_spec],
        out_specs=[q_spec, pl.BlockSpec((None, 8, t), lambda h, i, j: (h, 0, i))],
        out_shape=[jax.ShapeDtypeStruct(q.shape, BF16),
                   jax.ShapeDtypeStruct((MLA_HEADS // MLA_PACK, 8, rows), F32)],
        scratch_shapes=[pltpu.VMEM((MLA_PACK, 1, t), F32), pltpu.VMEM((MLA_PACK, 1, t), F32),
                        pltpu.VMEM((MLA_PACK, SLOT, t), F32)],
        compiler_params=_cparams(),
    )(q, k, v)


def _mla_delta(o, do):
    rows = o.shape[0]
    t = rows
    wide = MLA_PACK * SLOT

    def body(o_ref, do_ref, d_ref):
        d_ref[...] = jnp.zeros_like(d_ref)
        ones = jnp.ones((8, SLOT), BF16)
        for hh in range(MLA_PACK):
            sl = slice(hh * SLOT, (hh + 1) * SLOT)
            prod = do_ref[:, sl].astype(F32) * o_ref[:, sl].astype(F32)
            high = prod.astype(BF16)
            low = (prod - high.astype(F32)).astype(BF16)
            d_ref[hh:hh + 1, :] = (_nt(ones, high) + _nt(ones, low))[0:1, :]

    spec = pl.BlockSpec((t, wide), lambda h, i: (i, h))
    return pl.pallas_call(
        body, name="mla_delta", grid=(MLA_HEADS // MLA_PACK, rows // t), in_specs=[spec, spec],
        out_specs=pl.BlockSpec((None, 8, t), lambda h, i: (h, 0, i)),
        out_shape=jax.ShapeDtypeStruct((MLA_HEADS // MLA_PACK, 8, rows), F32), compiler_params=_cparams(),
    )(o, do)


def _mla_attn_bwd(q, k, v, do, lse, delta, after):
    rows = q.shape[0]
    t = min(TQ_MLA, rows)
    nt = rows // t
    wide = MLA_PACK * SLOT

    def body(q_ref, k_ref, v_ref, do_ref, lse_ref, delta_ref, after_ref, dq_ref, dk_ref, dv_ref, dk_sc, dv_sc):
        j, i = pl.program_id(1), pl.program_id(2)

        @pl.when((j == 0) & (i == 0))
        def _():
            dq_ref[...] = jnp.zeros_like(dq_ref)

        @pl.when(i == 0)
        def _():
            dk_sc[...] = jnp.zeros_like(dk_sc)
            dv_sc[...] = jnp.zeros_like(dv_sc)

        def chunk(hh, rows, keys, masked):
            sl = slice(hh * SLOT, (hh + 1) * SLOT)
            n_rows = rows.stop - rows.start
            qv, kv, dov = q_ref[rows, sl], k_ref[keys, sl], do_ref[rows, sl]
            s = _nt(kv, qv)
            if masked:
                shp = (keys.stop - keys.start, n_rows)
                s = jnp.where(keys.start + lax.broadcasted_iota(jnp.int32, shp, 0)
                              <= rows.start + lax.broadcasted_iota(jnp.int32, shp, 1), s, NEG)
            p = jnp.exp2(s - lse_ref[hh:hh + 1, rows])
            dp = _nt(v_ref[keys, sl], dov)
            ds = (p * (dp - delta_ref[hh:hh + 1, rows])).astype(BF16)
            dv_sc[keys, sl] += _nn(p.astype(BF16), dov)
            dk_sc[keys, sl] += _nn(ds, qv)
            r0 = pl.multiple_of(i * t + rows.start, n_rows)
            dq_ref[pl.ds(r0, n_rows), sl] += _tn(ds, kv) * MLA_SCALE

        @pl.when(i > j)
        def _():
            for hh in range(MLA_PACK):
                chunk(hh, slice(0, t), slice(0, t), False)

        @pl.when(i == j)
        def _():
            for hh in range(MLA_PACK):
                chunk(hh, slice(0, t), slice(0, t // 2), True)
                chunk(hh, slice(t // 2, t), slice(t // 2, t), True)

        @pl.when(i == nt - 1)
        def _():
            dk_ref[...] = dk_sc[...] * (1.0 / LOG2_E)
            dv_ref[...] = dv_sc[...]

    q_spec = pl.BlockSpec((t, wide), lambda h, j, i: (jnp.maximum(i, j), h))
    kv_spec = pl.BlockSpec((t, wide), lambda h, j, i: (j, h))
    row_spec = pl.BlockSpec((None, 8, t), lambda h, j, i: (h, 0, jnp.maximum(i, j)))
    head_spec = pl.BlockSpec((rows, wide), lambda h, j, i: (0, h))
    shp = jax.ShapeDtypeStruct(q.shape, F32)
    return pl.pallas_call(
        body, name="mla_attn_bwd", grid=(MLA_HEADS // MLA_PACK, nt, nt),
        in_specs=[q_spec, kv_spec, kv_spec, q_spec, row_spec, row_spec, pl.BlockSpec(memory_space=pl.ANY)],
        out_specs=[head_spec, kv_spec, kv_spec], out_shape=[shp, shp, shp],
        scratch_shapes=[pltpu.VMEM((t, wide), F32), pltpu.VMEM((t, wide), F32)],
        compiler_params=_cparams(),
    )(q, k, v, do, lse, delta, after)


def _swa_specs(t):
    def prev(i):
        return jnp.maximum(i - 1, 0)
    kw = SWA_PACK * SLOT
    k0, v0 = SWA_HEADS // SWA_PACK, (SWA_HEADS + SWA_KV_HEADS) // SWA_PACK
    q3 = pl.BlockSpec((t, SWA_PACK * SWA_GROUP * SLOT), lambda h, i: (i, h))
    kp = pl.BlockSpec((t, kw), lambda h, i: (prev(i), k0 + h))
    kc = pl.BlockSpec((t, kw), lambda h, i: (i, k0 + h))
    vp = pl.BlockSpec((t, kw), lambda h, i: (prev(i), v0 + h))
    vc = pl.BlockSpec((t, kw), lambda h, i: (i, v0 + h))
    pcol = pl.BlockSpec((t, 1), lambda h, i: (i, 0))
    prow_p = pl.BlockSpec((1, t), lambda h, i: (0, prev(i)))
    prow_c = pl.BlockSpec((1, t), lambda h, i: (0, i))
    return [q3, kp, kc, vp, vc, pcol, prow_p, prow_c]


def _stack(ref, first):
    return jnp.concatenate([ref[:, (first + g) * SLOT:(first + g + 1) * SLOT] for g in range(SWA_GROUP)], axis=0)


def _swa_logits(q3, kp, kc, pq, pkp, pkc, slope_ref, kvh, i, t):
    r = lax.broadcasted_iota(jnp.int32, (t, t), 0)
    c = lax.broadcasted_iota(jnp.int32, (t, t), 1)
    ok_c = c <= r
    ok_p = (c - r) > jnp.where(i > 0, 0, t)
    dist_p, dist_c = pq - pkp, pq - pkc
    s_p3 = _nt(q3, kp) * (HEAD_DIM ** -0.5)
    s_c3 = _nt(q3, kc) * (HEAD_DIM ** -0.5)
    out = []
    for g in range(SWA_GROUP):
        slope = slope_ref[kvh * SWA_GROUP + g]
        rows = slice(g * t, (g + 1) * t)
        out.append((jnp.where(ok_p, s_p3[rows] - slope * dist_p, NEG),
                    jnp.where(ok_c, s_c3[rows] - slope * dist_c, NEG)))
    return out


def _swa_attn_fwd(proj, pos_col, pos_row, slopes, sinks):
    rows = proj.shape[0]
    t = WINDOW
    hw = SWA_HEADS * SLOT

    def body(slope_ref, sink_ref, q_ref, kp_ref, kc_ref, vp_ref, vc_ref, pq_ref, pkp_ref, pkc_ref, o_ref, lse_ref):
        i = pl.program_id(1)
        for kv in range(SWA_PACK):
            kvh = pl.program_id(0) * SWA_PACK + kv
            ksl = slice(kv * SLOT, (kv + 1) * SLOT)
            logits = _swa_logits(_stack(q_ref, kv * SWA_GROUP), kp_ref[:, ksl], kc_ref[:, ksl], pq_ref[...],
                                 pkp_ref[...], pkc_ref[...], slope_ref, kvh, i, t)
            e_p, e_c, norm = [], [], []
            for g, (s_p, s_c) in enumerate(logits):
                sl = slice((kv * SWA_GROUP + g) * SLOT, (kv * SWA_GROUP + g + 1) * SLOT)
                sink = sink_ref[kvh * SWA_GROUP + g]
                m = jnp.maximum(jnp.maximum(jnp.max(s_p, axis=1, keepdims=True),
                                            jnp.max(s_c, axis=1, keepdims=True)), sink)
                ep, ec = jnp.exp(s_p - m), jnp.exp(s_c - m)
                l = jnp.sum(ep, axis=1, keepdims=True) + jnp.sum(ec, axis=1, keepdims=True) + jnp.exp(sink - m)
                e_p.append(ep.astype(BF16))
                e_c.append(ec.astype(BF16))
                norm.append(l)
                lse_ref[:, sl] = jnp.broadcast_to(m + jnp.log(l), (t, SLOT))
            acc = (_nn(jnp.concatenate(e_p, axis=0), vp_ref[:, ksl])
                   + _nn(jnp.concatenate(e_c, axis=0), vc_ref[:, ksl]))
            for g in range(SWA_GROUP):
                sl = slice((kv * SWA_GROUP + g) * SLOT, (kv * SWA_GROUP + g + 1) * SLOT)
                o_ref[:, sl] = (acc[g * t:(g + 1) * t] / norm[g]).astype(o_ref.dtype)

    smem = pl.BlockSpec(memory_space=pltpu.SMEM)
    out_spec = pl.BlockSpec((t, SWA_PACK * SWA_GROUP * SLOT), lambda h, i: (i, h))
    return pl.pallas_call(
        body, name="swa_attn_fwd", grid=(SWA_KV_HEADS // SWA_PACK, rows // t),
        in_specs=[smem, smem] + _swa_specs(t), out_specs=[out_spec, out_spec],
        out_shape=[jax.ShapeDtypeStruct((rows, hw), BF16), jax.ShapeDtypeStruct((rows, hw), F32)],
        compiler_params=_cparams(),
    )(slopes, sinks, proj, proj, proj, proj, proj, pos_col, pos_row, pos_row)


def _swa_attn_bwd(proj, o, do, lse, pos_col, pos_row, slopes, sinks, after):
    rows = proj.shape[0]
    t = WINDOW
    hw = SWA_HEADS * SLOT
    scale = HEAD_DIM ** -0.5

    def body(slope_ref, sink_ref, q_ref, kp_ref, kc_ref, vp_ref, vc_ref, pq_ref, pkp_ref, pkc_ref,
             o_ref, do_ref, lse_ref, after_ref, dq_ref, dk_ref, dv_ref, dsink_ref):
        i = pl.program_id(1)

        @pl.when(i == 0)
        def _():
            dk_ref[...] = jnp.zeros_like(dk_ref)
            dv_ref[...] = jnp.zeros_like(dv_ref)
            dsink_ref[...] = jnp.zeros_like(dsink_ref)

        r_c = pl.multiple_of(i * t, t)
        r_p = pl.multiple_of(jnp.maximum(i - 1, 0) * t, t)
        for kv in range(SWA_PACK):
            kvh = pl.program_id(0) * SWA_PACK + kv
            ksl = slice(kv * SLOT, (kv + 1) * SLOT)
            q3, do3 = _stack(q_ref, kv * SWA_GROUP), _stack(do_ref, kv * SWA_GROUP)
            logits = _swa_logits(q3, kp_ref[:, ksl], kc_ref[:, ksl], pq_ref[...], pkp_ref[...], pkc_ref[...],
                                 slope_ref, kvh, i, t)
            dp_p3, dp_c3 = _nt(do3, vp_ref[:, ksl]), _nt(do3, vc_ref[:, ksl])
            p_p, p_c, ds_p, ds_c = [], [], [], []
            for g, (s_p, s_c) in enumerate(logits):
                head = kv * SWA_GROUP + g
                sl = slice(head * SLOT, (head + 1) * SLOT)
                rws = slice(g * t, (g + 1) * t)
                lse_g = lse_ref[:, head * SLOT:head * SLOT + 1]
                pp, pc = jnp.exp(s_p - lse_g), jnp.exp(s_c - lse_g)
                delta = jnp.sum(do_ref[:, sl].astype(F32) * o_ref[:, sl].astype(F32), axis=1, keepdims=True)
                p_p.append(pp.astype(BF16))
                p_c.append(pc.astype(BF16))
                ds_p.append((pp * (dp_p3[rws] - delta)).astype(BF16))
                ds_c.append((pc * (dp_c3[rws] - delta)).astype(BF16))
                sink = sink_ref[kvh * SWA_GROUP + g]
                dsink = -jnp.sum(jnp.exp(sink - lse_g) * delta, axis=0, keepdims=True)
                dsink_ref[head * 8:(head + 1) * 8, :] += jnp.broadcast_to(dsink, (8, SLOT))
            p_p3, p_c3 = jnp.concatenate(p_p, axis=0), jnp.concatenate(p_c, axis=0)
            ds_p3, ds_c3 = jnp.concatenate(ds_p, axis=0), jnp.concatenate(ds_c, axis=0)
            dq3 = (_nn(ds_p3, kp_ref[:, ksl]) + _nn(ds_c3, kc_ref[:, ksl])) * scale
            for g in range(SWA_GROUP):
                head = kv * SWA_GROUP + g
                dq_ref[:, head * SLOT:(head + 1) * SLOT] = dq3[g * t:(g + 1) * t]
            dk_ref[pl.ds(r_c, t), ksl] += _tn(ds_c3, q3) * scale
            dv_ref[pl.ds(r_c, t), ksl] += _tn(p_c3, do3)
            dk_ref[pl.ds(r_p, t), ksl] += _tn(ds_p3, q3) * scale
            dv_ref[pl.ds(r_p, t), ksl] += _tn(p_p3, do3)

    smem = pl.BlockSpec(memory_space=pltpu.SMEM)
    qlike = pl.BlockSpec((t, SWA_PACK * SWA_GROUP * SLOT), lambda h, i: (i, h))
    kv_out = pl.BlockSpec((rows, SWA_PACK * SLOT), lambda h, i: (0, h))
    return pl.pallas_call(
        body, name="swa_attn_bwd", grid=(SWA_KV_HEADS // SWA_PACK, rows // t),
        in_specs=[smem, smem] + _swa_specs(t) + [qlike, qlike, qlike, pl.BlockSpec(memory_space=pl.ANY)],
        out_specs=[qlike, kv_out, kv_out,
                   pl.BlockSpec((SWA_PACK * SWA_GROUP * 8, SLOT), lambda h, i: (h, 0))],
        out_shape=[jax.ShapeDtypeStruct((rows, hw), F32), jax.ShapeDtypeStruct((rows, SWA_KV_HEADS * SLOT), F32),
                   jax.ShapeDtypeStruct((rows, SWA_KV_HEADS * SLOT), F32),
                   jax.ShapeDtypeStruct((SWA_HEADS * 8, SLOT), F32)],
        compiler_params=_cparams(),
    )(slopes, sinks, proj, proj, proj, proj, proj, pos_col, pos_row, pos_row, o, do, lse, after)


def _cross_attn_fwd(proj, qoff, kvmem):
    rows = proj.shape[0]
    t = min(TQ_CROSS, rows)

    def body(q_ref, k_ref, v_ref, o_ref):
        s = _nt(k_ref[...], q_ref[...].astype(BF16)) * (HEAD_DIM ** -0.5)
        e = jnp.exp(s - jnp.max(s, axis=0, keepdims=True))
        p = e / jnp.sum(e, axis=0, keepdims=True)
        o_ref[...] = _tn(v_ref[...], p.astype(BF16)).T.astype(o_ref.dtype)

    return pl.pallas_call(
        body, name="cross_attn_fwd", grid=(rows // t, MEM_HEADS),
        in_specs=[pl.BlockSpec((t, SLOT), lambda i, h: (i, qoff + h)),
                  pl.BlockSpec((N_MEM, SLOT), lambda i, h: (0, h)),
                  pl.BlockSpec((N_MEM, SLOT), lambda i, h: (0, MEM_HEADS + h))],
        out_specs=pl.BlockSpec((t, SLOT), lambda i, h: (i, h)),
        out_shape=jax.ShapeDtypeStruct((rows, MEM_HEADS * SLOT), BF16), compiler_params=_cparams(),
    )(proj, kvmem, kvmem)


def _cross_attn_bwd(proj, qoff, kvmem, do, do_off):
    rows = proj.shape[0]
    t = min(TQ_CROSS, rows)
    scale = HEAD_DIM ** -0.5

    def body(q_ref, k_ref, v_ref, do_ref, dq_ref, dk_ref, dv_ref):
        @pl.when(pl.program_id(1) == 0)
        def _():
            dk_ref[...] = jnp.zeros_like(dk_ref)
            dv_ref[...] = jnp.zeros_like(dv_ref)

        qv, kv, dov = q_ref[...].astype(BF16), k_ref[...], do_ref[...]
        s = _nt(kv, qv) * scale
        e = jnp.exp(s - jnp.max(s, axis=0, keepdims=True))
        p = e / jnp.sum(e, axis=0, keepdims=True)
        dp = _nt(v_ref[...], dov)
        ds = (p * (dp - jnp.sum(p * dp, axis=0, keepdims=True))).astype(BF16)
        dq_ref[...] = _tn(ds, kv) * scale
        dk_ref[...] += _nn(ds, qv) * scale
        dv_ref[...] += _nn(p.astype(BF16), dov)

    mem_out = pl.BlockSpec((N_MEM, SLOT), lambda h, i: (0, h))
    return pl.pallas_call(
        body, name="cross_attn_bwd", grid=(MEM_HEADS, rows // t),
        in_specs=[pl.BlockSpec((t, SLOT), lambda h, i: (i, qoff + h)),
                  pl.BlockSpec((N_MEM, SLOT), lambda h, i: (0, h)),
                  pl.BlockSpec((N_MEM, SLOT), lambda h, i: (0, MEM_HEADS + h)),
                  pl.BlockSpec((t, SLOT), lambda h, i: (i, do_off + h))],
        out_specs=[pl.BlockSpec((t, SLOT), lambda h, i: (i, h)), mem_out, mem_out],
        out_shape=[jax.ShapeDtypeStruct((rows, MEM_HEADS * SLOT), F32),
                   jax.ShapeDtypeStruct((N_MEM, MEM_HEADS * SLOT), F32),
                   jax.ShapeDtypeStruct((N_MEM, MEM_HEADS * SLOT), F32)],
        compiler_params=_cparams(),
    )(proj, kvmem, kvmem, do)


def _place():
    return lax.axis_index("x"), lax.axis_index("y"), lax.axis_index("c")


def _flip(v, bit):
    return 1 - v if bit else v


def _all_gather(blocks, name):
    nb = len(blocks)

    def body(*refs):
        x_refs, out_refs = refs[:nb], refs[nb:2 * nb]
        send_sems, recv_sems, local_sems = refs[2 * nb:]
        x, y, c = _place()
        me, sibling = (x, y, c), (x, y, 1 - c)
        chips = [(1 - x, y), (x, 1 - y), (1 - x, 1 - y)]

        def copy(b, k, blk, to, from_input=False):
            slot = out_refs[b].at[4 * blk[0] + 2 * blk[1] + blk[2]]
            return pltpu.make_async_remote_copy(
                src_ref=x_refs[b] if from_input else slot, dst_ref=slot,
                send_sem=send_sems.at[b, k], recv_sem=recv_sems.at[b, k],
                device_id=to, device_id_type=pl.DeviceIdType.MESH)

        mine = [pltpu.make_async_copy(x_refs[b], out_refs[b].at[4 * x + 2 * y + c], local_sems.at[b])
                for b in range(nb)]
        for cp in mine:
            cp.start()
        first = []
        for b in range(nb):
            first.append(copy(b, 0, me, sibling, from_input=True))
            first += [copy(b, 1 + n, me, (*chip, c), from_input=True) for n, chip in enumerate(chips)]
        for cp in first:
            cp.start()
        passed = []
        for n, chip in enumerate(chips):
            for b in range(nb):
                copy(b, 1 + n, (*chip, c), me).wait_recv()
                passed.append(copy(b, 4 + n, (*chip, c), sibling))
                passed[-1].start()
        for b in range(nb):
            copy(b, 0, sibling, me).wait_recv()
            for n, chip in enumerate(chips):
                copy(b, 4 + n, (*chip, 1 - c), me).wait_recv()
        for cp in first + passed:
            cp.wait_send()
        for cp in mine:
            cp.wait()

    any_spec = pl.BlockSpec(memory_space=pl.ANY)
    return pl.pallas_call(
        body, name=name, in_specs=[any_spec] * nb, out_specs=[any_spec] * nb,
        out_shape=[jax.ShapeDtypeStruct((N_DEV,) + blk.shape, blk.dtype) for blk in blocks],
        scratch_shapes=[pltpu.SemaphoreType.DMA((nb, 7)), pltpu.SemaphoreType.DMA((nb, 7)),
                        pltpu.SemaphoreType.DMA((nb,))],
    )(*blocks)


def _peers(x, y, c):
    out = []
    for n in range(1, N_DEV):
        peer = (_flip(x, n & 4), _flip(y, n & 2), _flip(c, n & 1))
        out.append((n - 1, peer, 4 * peer[0] + 2 * peer[1] + peer[2]))
    return out


_HBM = pl.BlockSpec(memory_space=pltpu.HBM)
_SEM = pl.BlockSpec(memory_space=pltpu.SEMAPHORE)


def _exchange_start(srcs, scatter, name, after=None):
    ns = len(srcs)
    lands = [lax.empty(s.shape if scatter else (N_DEV,) + s.shape, s.dtype) for s in srcs]

    def body(*refs):
        src_refs, land_refs = refs[:ns], refs[ns:2 * ns]
        pos = 2 * ns + (1 if after is not None else 0)
        send_sems, recv_sems, token = refs[pos], refs[pos + 1], refs[-1]
        x, y, c = _place()
        my_idx = 4 * x + 2 * y + c
        for col, peer, peer_idx in _peers(x, y, c):
            for b in range(ns):
                pltpu.make_async_remote_copy(
                    src_ref=src_refs[b].at[peer_idx] if scatter else src_refs[b], dst_ref=land_refs[b].at[my_idx],
                    send_sem=send_sems.at[b * (N_DEV - 1) + col], recv_sem=recv_sems.at[b * (N_DEV - 1) + col],
                    device_id=peer, device_id_type=pl.DeviceIdType.MESH).start()
        token[...] = jnp.zeros_like(token)

    args = [pltpu.with_memory_space_constraint(a, pltpu.HBM) for a in list(srcs) + lands]
    in_specs = [_HBM] * (2 * ns)
    if after is not None:
        args.append(after)
        in_specs.append(pl.BlockSpec(memory_space=pl.ANY))
    out = pl.pallas_call(
        body, name=name, in_specs=in_specs,
        out_specs=[_SEM, _SEM] + [_HBM] * (2 * ns) + [pl.BlockSpec(memory_space=pltpu.VMEM)],
        out_shape=[pltpu.SemaphoreType.DMA((ns * (N_DEV - 1),)), pltpu.SemaphoreType.DMA((ns * (N_DEV - 1),))]
        + [pltpu.HBM(a.shape, a.dtype) for a in list(srcs) + lands] + [jax.ShapeDtypeStruct((8, SLOT), F32)],
        input_output_aliases={k: 2 + k for k in range(2 * ns)},
        compiler_params=pltpu.CompilerParams(has_side_effects=pltpu.SideEffectType.DATAFLOW_SIDE_EFFECTING),
    )(*args)
    return (out[0], out[1], out[2:2 + ns], out[2 + ns:2 + 2 * ns], scatter), out[-1]


def _exchange_wait(handle, after, name):
    send_sems, recv_sems, srcs, lands, scatter = handle
    ns = len(srcs)

    def body(*refs):
        src_refs, land_refs = refs[:ns], refs[ns:2 * ns]
        send_ref, recv_ref = refs[2 * ns], refs[2 * ns + 1]
        x, y, c = _place()
        for col, peer, peer_idx in _peers(x, y, c):
            for b in range(ns):
                copy = pltpu.make_async_remote_copy(
                    src_ref=src_refs[b].at[peer_idx] if scatter else src_refs[b], dst_ref=land_refs[b].at[peer_idx],
                    send_sem=send_ref.at[b * (N_DEV - 1) + col], recv_sem=recv_ref.at[b * (N_DEV - 1) + col],
                    device_id=peer, device_id_type=pl.DeviceIdType.MESH)
                copy.wait_send()
                copy.wait_recv()

    out = pl.pallas_call(
        body, name=name, in_specs=[_HBM] * (2 * ns) + [_SEM, _SEM, pl.BlockSpec(memory_space=pl.ANY)],
        out_specs=[_HBM] * (2 * ns),
        out_shape=[pltpu.HBM(a.shape, a.dtype) for a in list(srcs) + list(lands)],
        input_output_aliases={k: k for k in range(2 * ns)},
        compiler_params=pltpu.CompilerParams(has_side_effects=pltpu.SideEffectType.DATAFLOW_SIDE_EFFECTING),
    )(*srcs, *lands, send_sems, recv_sems, after)
    my_idx = 4 * lax.axis_index("x") + 2 * lax.axis_index("y") + lax.axis_index("c")
    landed = []
    for src, land in zip(out[:ns], out[ns:]):
        own = lax.dynamic_index_in_dim(src, my_idx, 0, keepdims=True) if scatter else src[None]
        landed.append(lax.dynamic_update_index_in_dim(land, own, my_idx, 0))
    return landed


def _adamw(parts, w, m, v, name):
    lyr, rows, cols = w.shape
    assert len(parts) == lyr
    tr = ADAM_ROWS if cols > 512 else 2 * ADAM_ROWS
    while rows % tr:
        tr //= 2
    tr = min(tr, rows)

    def body(*refs):
        p_refs = refs[:lyr]
        w_ref, m_ref, v_ref, g_out, d_out, m_out, v_out = refs[lyr:]
        for k in range(lyr):
            @pl.when(pl.program_id(0) == k)
            def _(p_ref=p_refs[k]):
                g = p_ref[0].astype(F32)
                for s in range(1, N_DEV):
                    g = g + p_ref[s].astype(F32)
                m2 = ADAM_B1 * m_ref[...] + (1.0 - ADAM_B1) * g
                v2 = ADAM_B2 * v_ref[...] + (1.0 - ADAM_B2) * (g * g)
                m_hat = m2 / (1.0 - ADAM_B1 ** ADAM_STEP)
                v_hat = v2 / (1.0 - ADAM_B2 ** ADAM_STEP)
                g_out[...] = g
                d_out[...] = -ADAM_LR * (m_hat / (jnp.sqrt(v_hat) + ADAM_EPS) + ADAM_WD * w_ref[...])
                m_out[...] = m2
                v_out[...] = v2

    def part_spec(k):
        return pl.BlockSpec((N_DEV, tr, cols), lambda l, i: (0, jnp.where(l == k, i, 0), 0))

    spec = pl.BlockSpec((None, tr, cols), lambda l, i: (l, i, 0))
    shp = jax.ShapeDtypeStruct((lyr, rows, cols), F32)
    return pl.pallas_call(
        body, name=name, grid=(lyr, rows // tr),
        in_specs=[part_spec(k) for k in range(lyr)] + [spec, spec, spec],
        out_specs=[spec] * 4, out_shape=[shp] * 4, compiler_params=_cparams(),
    )(*parts, w, m, v)


def _pack(arrays, lanes, row_mult, dtype):
    flat = jnp.concatenate([a.reshape(-1).astype(dtype) for a in arrays])
    unit = lanes * row_mult
    total = -(-flat.shape[0] // unit) * unit
    return jnp.pad(flat, (0, total - flat.shape[0])).reshape(total // lanes, lanes)


def _unpack(packed, shapes):
    flat = packed.reshape(-1)
    out, off = [], 0
    for shp in shapes:
        n = 1
        for d in shp:
            n *= d
        out.append(flat[off:off + n].reshape(shp))
        off += n
    return out


def _pad_slots(w, axis):
    axis = axis % w.ndim
    n = w.shape[axis] // HEAD_DIM
    shp = w.shape[:axis] + (n, HEAD_DIM) + w.shape[axis + 1:]
    pad = [(0, 0)] * (w.ndim + 1)
    pad[axis + 1] = (0, SLOT - HEAD_DIM)
    return jnp.pad(w.reshape(shp), pad).reshape(w.shape[:axis] + (n * SLOT,) + w.shape[axis + 1:])


def _unpad_slots(w, axis, keep=HEAD_DIM):
    axis = axis % w.ndim
    n = w.shape[axis] // SLOT
    shp = w.shape[:axis] + (n, SLOT) + w.shape[axis + 1:]
    idx = [slice(None)] * (w.ndim + 1)
    idx[axis + 1] = slice(0, keep)
    return w.reshape(shp)[tuple(idx)].reshape(w.shape[:axis] + (n * keep,) + w.shape[axis + 1:])


def _mla_in_pad(w):
    z = functools.partial(jnp.zeros, dtype=w.dtype)
    rows = w.shape[0]
    return jnp.concatenate([w[:, :384], z((rows, 64)), w[:, 640:672], z((rows, 32)), w[:, 384:640],
                            _pad_slots(w[:, 672:], 1)], axis=1)


def _mla_in_unpad(d):
    return jnp.concatenate([d[:, :384], d[:, 512:768], d[:, 448:480], _unpad_slots(d[:, 768:], 1)], axis=1)


def _mla_uq_pad(w):
    return jnp.pad(w.reshape(w.shape[0], MLA_HEADS, MLA_QK), ((0, 0), (0, 0), (0, SLOT - MLA_QK))).reshape(
        w.shape[0], MLA_HEADS * SLOT)


def _join(gathered, axis):
    nd, a, b = gathered.shape
    if axis == 1:
        return gathered.reshape(nd * a, b)
    return gathered.transpose(1, 0, 2).reshape(a, nd * b)


def _split(full, axis):
    r, c = full.shape
    if axis == 1:
        return full.reshape(N_DEV, r // N_DEV, c).astype(BF16)
    return full.reshape(r, N_DEV, c // N_DEV).transpose(1, 0, 2).astype(BF16)


def kernel(x, mem, positions, attn_norm_g, mlp_norm_g, mem_norm_g, final_norm_g, mla_w_in, mla_q_norm_g, mla_kv_norm_g, mla_w_uq, mla_w_ukv, swa_w_in, swa_sinks, w_mem_kv, w_o, mlp_w_up, mlp_w_down, loss_target, m_attn_norm_g, m_mlp_norm_g, m_mem_norm_g, m_final_norm_g, m_mla_w_in, m_mla_q_norm_g, m_mla_kv_norm_g, m_mla_w_uq, m_mla_w_ukv, m_swa_w_in, m_swa_sinks, m_w_mem_kv, m_w_o, m_mlp_w_up, m_mlp_w_down, v_attn_norm_g, v_mlp_norm_g, v_mem_norm_g, v_final_norm_g, v_mla_w_in, v_mla_q_norm_g, v_mla_kv_norm_g, v_mla_w_uq, v_mla_w_ukv, v_swa_w_in, v_swa_sinks, v_w_mem_kv, v_w_o, v_mlp_w_up, v_mlp_w_down):
    given = dict(locals())
    seq = x.shape[1]
    x0 = x.reshape(seq, D_MODEL)
    tgt = loss_target.reshape(seq, D_MODEL)
    mem0 = mem.reshape(N_MEM, D_MODEL)
    pos = positions.reshape(seq).astype(F32)
    pos_col, pos_row = pos.reshape(seq, 1), pos.reshape(1, seq)

    def layer_names(i):
        mixer = ("mla_w_in", "mla_w_uq", "mla_w_ukv") if i % 2 == 0 else ("swa_w_in",)
        return [(n, i // 2) for n in mixer] + [(n, i) for n in ("w_mem_kv", "w_o", "mlp_w_up", "mlp_w_down")]

    def local_weights(names):
        return [given[n][l].astype(BF16) for n, l in names]

    first_attn, first_mlp = layer_names(0)[:-2], layer_names(0)[-2:]
    weights = [dict(zip([n for n, _ in first_attn], _all_gather(local_weights(first_attn), "gather_weights_first")))]
    coming_mlp, first_token = _exchange_start(local_weights(first_mlp), False, "gather_weights_start_0",
                                              after=weights[0]["w_o"])

    consts = _lane_consts()
    tabs = _rope_tables(pos_col, consts)
    slopes = 2.0 ** (-8.0 * (jnp.arange(SWA_HEADS, dtype=F32) + 1.0) / SWA_HEADS)

    mem_n = _rmsnorm_fwd(mem0, 0, D_MODEL, mem_norm_g, "rmsnorm_fwd_mem")

    saved = []
    xc = x0
    for i in range(DEPTH):
        j = i // 2
        wts = weights[i]
        s = {"x_in": xc}
        token = None
        if i + 1 < DEPTH:
            coming, token = _exchange_start(local_weights(layer_names(i + 1)), False,
                                            "gather_weights_start_%d" % (i + 1),
                                            after=first_token if i == 0 else wts["w_o"])
        hn = _rmsnorm_fwd(xc, 0, D_MODEL, attn_norm_g[i], "rmsnorm_fwd", after=token)
        if i % 2 == 0:
            w_in = _mla_in_pad(_join(wts["mla_w_in"], 1))
            w_uq = _mla_uq_pad(_join(wts["mla_w_uq"], 2))
            w_kv = _join(wts["mla_w_ukv"], 2)
            proj = _mm(hn, w_in, "nn", F32, "mm_mla_in")
            cqn = _rmsnorm_fwd(proj, 0, MLA_Q_RANK, mla_q_norm_g[j], "rmsnorm_fwd_q")
            ckvn = _rmsnorm_fwd(proj, 2, MLA_KV_RANK, mla_kv_norm_g[j], "rmsnorm_fwd_kv")
            qraw = _mm(cqn, w_uq, "nn", F32, "mm_mla_uq")
            kvraw = _mm(ckvn, w_kv, "nn", F32, "mm_mla_ukv")
            q, k, v = _mla_rope_fwd(qraw, kvraw, proj, tabs)
            o, lse = _mla_attn_fwd(q, k, v)
            qoff = MLA_QOFF
            s.update(w_uq=w_uq, w_kv=w_kv, cqn=cqn, ckvn=ckvn, q=q, k=k, v=v)
        else:
            w_in = _join(wts["swa_w_in"], 2)
            proj = _mm(hn, w_in, "nn", BF16, "mm_swa_in", pairs="o")
            o, lse = _swa_attn_fwd(proj, pos_col, pos_row, slopes, swa_sinks[j])
            qoff = SWA_QOFF
        w_mem = _pad_slots(_join(wts["w_mem_kv"], 1), 1)
        w_out = _join(wts["w_o"], 1)
        w_o_mix, w_o_cross = w_out[:SWA_HEADS * HEAD_DIM], w_out[SWA_HEADS * HEAD_DIM:]
        kvmem = _mm(mem_n, w_mem, "nn", BF16, "mm_mem_kv")
        cross = _cross_attn_fwd(proj, qoff, kvmem)
        x1 = _mm(o, w_o_mix, "nn", F32, "mm_o", res=xc, pairs="a", second=(cross, w_o_cross))
        hn2 = _rmsnorm_fwd(x1, 0, D_MODEL, mlp_norm_g[i], "rmsnorm_fwd")
        if i == 0:
            wts.update(zip([n for n, _ in first_mlp], _exchange_wait(coming_mlp, hn2, "gather_weights_wait_0")))
        act, act2 = _mm(hn2, wts["mlp_w_up"], "nn", BF16, "mm_mlp_up", epi="relu2", b_blk="cols")
        xc = _mm(act2, wts["mlp_w_down"], "nn", F32, "mm_mlp_down", res=x1, b_blk="rows")
        s.update(hn=hn, w_in=w_in, proj=proj, o=o, lse=lse, qoff=qoff, w_mem=w_mem, w_out=w_out,
                 kvmem=kvmem, cross=cross, x1=x1, hn2=hn2, act=act, act2=act2)
        saved.append(s)
        if i + 1 < DEPTH:
            got = _exchange_wait(coming, xc, "gather_weights_wait_%d" % (i + 1))
            weights.append(dict(zip([n for n, _ in layer_names(i + 1)], got)))

    dx, dx_b, dg_final, loss_part = _loss_head(xc, final_norm_g, tgt)
    loss = lax.psum(loss_part[0, 0], MESH_AXES)

    gains = {n: [None] * DEPTH for n in ("attn_norm_g", "mlp_norm_g")}
    for n in ("mla_q_norm_g", "mla_kv_norm_g", "swa_sinks"):
        gains[n] = [None] * 2
    leaving = {}
    token = None
    dmem_n = None
    for i in reversed(range(DEPTH)):
        j = i // 2
        s = saved[i]
        wts = weights[i]
        out = {}
        du = _mm(dx_b, wts["mlp_w_down"], "nt", BF16, "mm_mlp_down_dx", aux=s["act"], epi="mul2aux", b_blk="rows",
                 after=token)
        out["mlp_w_down"] = _mm(s["act2"], dx_b, "tn", BF16, "mm_mlp_down_dw", o_blk="rows")
        out["mlp_w_up"] = _mm(s["hn2"], du, "tn", BF16, "mm_mlp_up_dw", o_blk="cols")
        dx1, dx1_b, dg = _mm(du, wts["mlp_w_up"], "nt", F32, "mm_mlp_up_dx", b_blk="cols",
                             epi="normbwd", norm=(s["x1"], mlp_norm_g[i], dx))
        gains["mlp_norm_g"][i] = dg[0]

        do = _mm(dx1_b, s["w_out"], "nt", BF16, "mm_o_dx", pairs="o")
        dw_o = jnp.concatenate([_mm(s["o"], dx1_b, "tn", F32, "mm_o_mix_dw", pairs="a"),
                                _mm(s["cross"], dx1_b, "tn", F32, "mm_o_cross_dw", pairs="a")], axis=0)
        out["w_o"] = _split(dw_o, 1)
        dqc, dkm, dvm = _cross_attn_bwd(s["proj"], s["qoff"], s["kvmem"], do, SWA_HEADS)
        dkvmem = jnp.concatenate([dkm, dvm], axis=1).astype(BF16)
        out["w_mem_kv"] = _split(_unpad_slots(_mm(mem_n, dkvmem, "tn", F32, "mm_mem_kv_dw"), 1), 1)
        dmem_n = _mm(dkvmem, s["w_mem"], "nt", F32, "mm_mem_kv_dx" if dmem_n is None else "mm_mem_kv_dx_acc",
                     res=dmem_n)
        leaving[(i, "main")], token = _exchange_start([out[n] for n, _ in layer_names(i)[-4:]], True,
                                                      "exchange_grads_main_start_%d" % i)

        if i % 2 == 0:
            dq, dk, dv = _mla_attn_bwd(s["q"], s["k"], s["v"], do, s["lse"], _mla_delta(s["o"], do), token)
            dqraw, dkv, dkr = _mla_rope_bwd(dq, dk, dv, tabs, consts)
            dcqn = _mm(dqraw, s["w_uq"], "nt", F32, "mm_mla_uq_dx")
            out["mla_w_uq"] = _split(_unpad_slots(_mm(s["cqn"], dqraw, "tn", F32, "mm_mla_uq_dw"), 1, MLA_QK), 2)
            dckvn = _mm(dkv, s["w_kv"], "nt", F32, "mm_mla_ukv_dx")
            out["mla_w_ukv"] = _split(_mm(s["ckvn"], dkv, "tn", F32, "mm_mla_ukv_dw"), 2)
            dcq, dg = _rmsnorm_bwd(s["proj"], 0, MLA_Q_RANK, mla_q_norm_g[j], dcqn, None, BF16, "rmsnorm_bwd_q")
            gains["mla_q_norm_g"][j] = dg[0]
            dckv, dg = _rmsnorm_bwd(s["proj"], 2, MLA_KV_RANK, mla_kv_norm_g[j], dckvn, None, BF16, "rmsnorm_bwd_kv")
            gains["mla_kv_norm_g"][j] = dg[0]
            dproj = jnp.concatenate([dcq, dkr.astype(BF16), dckv, dqc.astype(BF16)], axis=1)
            in_dx = "mm_mla_in_dx"
            out["mla_w_in"] = _split(_mla_in_unpad(_mm(s["hn"], dproj, "tn", F32, "mm_mla_in_dw")), 1)
        else:
            dq, dk, dv, dsink = _swa_attn_bwd(s["proj"], s["o"], do, s["lse"], pos_col, pos_row, slopes, swa_sinks[j],
                                              token)
            gains["swa_sinks"][j] = dsink[::8, 0]
            dproj = jnp.concatenate([dq, dk, dv, dqc], axis=1).astype(BF16)
            in_dx = "mm_swa_in_dx"
            out["swa_w_in"] = _split(_mm(s["hn"], dproj, "tn", F32, "mm_swa_in_dw", pairs="b"), 2)
        dx, dx_b, dg = _mm(dproj, s["w_in"], "nt", F32, in_dx, epi="normbwd", norm=(s["x_in"], attn_norm_g[i], dx1),
                           pairs="" if i % 2 == 0 else "a")
        gains["attn_norm_g"][i] = dg[0]

        leaving[(i, "mixer")], token = _exchange_start([out[n] for n, _ in layer_names(i)[:-4]], True,
                                                       "exchange_grads_mixer_start_%d" % i)

    _, dg_mem = _rmsnorm_bwd(mem0, 0, D_MODEL, mem_norm_g, dmem_n, None, BF16, "rmsnorm_bwd_mem")
    gains = {n: jnp.stack(g) for n, g in gains.items()}
    gains["mem_norm_g"] = dg_mem[0]
    gains["final_norm_g"] = dg_final[0]

    result = {}

    def adamw_of(names, received):
        for n in names:
            parts = [received[(n, l)] for l in range(given[n].shape[0])]
            for kind, r in enumerate(_adamw(parts, given[n], given["m_" + n], given["v_" + n], "adamw_" + n)):
                result[(kind, n)] = r

    received = {}
    for i in reversed(range(DEPTH)):
        got = _exchange_wait(leaving[(i, "main")], dx, "exchange_grads_main_wait_%d" % i)
        received.update(zip(layer_names(i)[-4:], got))
    adamw_of(("mlp_w_up", "mlp_w_down", "w_o", "w_mem_kv"), received)
    for i in reversed(range(DEPTH)):
        got = _exchange_wait(leaving[(i, "mixer")], result[(0, "w_mem_kv")], "exchange_grads_mixer_wait_%d" % i)
        received.update(zip(layer_names(i)[:-4], got))
    adamw_of(("mla_w_in", "mla_w_uq", "mla_w_ukv", "swa_w_in"), received)

    rep_shapes = [given[n].shape for n in REPLICATED]
    rep_parts = _all_gather([_pack([gains[n] for n in REPLICATED], SLOT, 8, F32)], "gather_gain_grads")[0]
    rep_packed = [_pack([given[p + n] for n in REPLICATED], SLOT, 8, F32)[None] for p in ("", "m_", "v_")]
    for kind, r in enumerate(_adamw([rep_parts], *rep_packed, "adamw_gains")):
        for n, part in zip(REPLICATED, _unpack(r[0], rep_shapes)):
            result[(kind, n)] = part

    outs = [loss, dx.reshape(1, seq, D_MODEL)]
    for kind in range(4):
        outs += [result[(kind, n)] for n in WEIGHT_ORDER]
    return tuple(outs)
```

```python
import functools

import jax
import jax.numpy as jnp
from jax import lax
from jax.experimental import pallas as pl
from jax.experimental.pallas import tpu as pltpu

F32 = jnp.float32
BF16 = jnp.bfloat16

D_MODEL = 1024
N_MEM = 256
DEPTH = 4
SLOT = 128
HEAD_DIM = 64
MLA_HEADS = 12
MLA_QK = 96
MLA_Q_RANK = 384
MLA_KV_RANK = 256
SWA_HEADS = 12
SWA_KV_HEADS = 4
SWA_GROUP = 3
MEM_HEADS = 4
WINDOW = 128
EPS = 1e-6
NEG = -1e30
ROPE_THETA = 10000.0
N_DEV = 8

ADAM_LR = 0.001
ADAM_B1 = 0.9
ADAM_B2 = 0.999
ADAM_EPS = 1e-08
ADAM_WD = 0.01
ADAM_STEP = 10

TM = 1024
TM_ROPE = 512
TQ_MLA = 1024
MLA_PACK = 4
SWA_PACK = 4
TQ_CROSS = 4096
MM_VMEM_BUDGET = 38 * 1024 * 1024
ADAM_ROWS = 256
VMEM_LIMIT = 56 * 1024 * 1024

MESH_AXES = ("x", "y", "c")

LOG2_E = 1.4426950408889634
MLA_SCALE = MLA_QK ** -0.5
MLA_Q_SCALE = MLA_SCALE * LOG2_E

MLA_QOFF = (MLA_Q_RANK + SLOT + MLA_KV_RANK) // SLOT
SWA_QOFF = SWA_HEADS + 2 * SWA_KV_HEADS

SHARDED = (
    ("mla_w_in", 1), ("mla_w_uq", 2), ("mla_w_ukv", 2), ("swa_w_in", 2),
    ("w_mem_kv", 1), ("w_o", 1), ("mlp_w_up", 2), ("mlp_w_down", 1),
)
REPLICATED = ("attn_norm_g", "mlp_norm_g", "mem_norm_g", "final_norm_g",
              "mla_q_norm_g", "mla_kv_norm_g", "swa_sinks")
WEIGHT_ORDER = ("attn_norm_g", "mlp_norm_g", "mem_norm_g", "final_norm_g", "mla_w_in",
                "mla_q_norm_g", "mla_kv_norm_g", "mla_w_uq", "mla_w_ukv", "swa_w_in",
                "swa_sinks", "w_mem_kv", "w_o", "mlp_w_up", "mlp_w_down")


def _cparams():
    return pltpu.CompilerParams(vmem_limit_bytes=VMEM_LIMIT)


_DIMS = {"nn": (((1,), (0,)), ((), ())), "nt": (((1,), (1,)), ((), ())), "tn": (((0,), (0,)), ((), ()))}


def _compact(x):
    pairs = [x[:, 2 * j * SLOT:(2 * j + 1) * SLOT] + pltpu.roll(x[:, (2 * j + 1) * SLOT:(2 * j + 2) * SLOT], HEAD_DIM, 1)
             for j in range(x.shape[1] // (2 * SLOT))]
    return pairs[0] if len(pairs) == 1 else jnp.concatenate(pairs, axis=1)


def _expand(x):
    low = lax.broadcasted_iota(jnp.int32, (x.shape[0], SLOT), 1) < HEAD_DIM
    slots = []
    for j in range(x.shape[1] // SLOT):
        pair = x[:, j * SLOT:(j + 1) * SLOT]
        slots += [jnp.where(low, pair, 0.0), pltpu.roll(jnp.where(low, 0.0, pair), HEAD_DIM, 1)]
    return jnp.concatenate(slots, axis=1)


def _mm_tiles(m, n, k, a_bytes, b_bytes, o_bytes, extra_bytes, tm_fixed, tn_fixed):
    best = None
    for tm in ([tm_fixed] if tm_fixed else [t for t in range(4096, 0, -SLOT) if m % t == 0] or [m]):
        for tn in ([tn_fixed] if tn_fixed else [t for t in range(1024, 0, -SLOT) if n % t == 0] or [n]):
            need = 2 * (tm * k * a_bytes + k * tn * b_bytes + tm * tn * (o_bytes + extra_bytes))
            need += tm * tn * 4
            if need <= MM_VMEM_BUDGET and (best is None or tm * tn > best[0] * best[1]):
                best = (tm, tn)
    assert best is not None, (m, n, k)
    return best


def _mm(a, b, mode, out_dtype, name, res=None, aux=None, epi=None, b_blk=None, o_blk=None, after=None, norm=None,
        pairs="", second=None):
    if b_blk is not None:
        nb, br, bc = b.shape
        b_shape = (nb * br, bc) if b_blk == "rows" else (br, nb * bc)
    else:
        b_shape = b.shape
    assert not pairs or (b_blk is None and o_blk is None and not ("b" in pairs and mode == "nt"))
    a_shape = (a.shape[0], a.shape[1] // 2) if "a" in pairs else a.shape
    if "b" in pairs:
        b_shape = (b_shape[0], b_shape[1] // 2)
    if mode == "nn":
        (m, k), (k2, n) = a_shape, b_shape
    elif mode == "nt":
        (m, k), (n, k2) = a_shape, b_shape
    else:
        (k, m), (k2, n) = a_shape, b_shape
    assert k == k2, (a.shape, b_shape, mode)
    assert second is None or (mode == "nn" and b_blk is None and second[0].shape[0] == m and second[1].shape[1] == n)
    k_second = 0 if second is None else second[1].shape[0]
    k_blocked = b_blk is not None and (b_blk == "rows") == (mode != "nt")
    tn_fixed = None
    if b_blk is not None and not k_blocked:
        tn_fixed = br if b_blk == "rows" else bc
    if o_blk == "cols":
        tn_fixed = n // N_DEV
    tm_fixed = m // N_DEV if o_blk == "rows" else None
    has_res, has_aux, has_norm, has_normf = res is not None, aux is not None, epi == "normbwd", epi == "normfwd"
    assert o_blk is None or not (has_res or has_aux or has_norm or has_normf)
    n_out = 2 if epi == "relu2" else 1
    if has_norm:
        tn_fixed = n
        o_bytes, extra_bytes = 4 + 2, 4 + 4
    elif has_normf:
        tn_fixed = n
        o_bytes, extra_bytes = 4 + 2, (4 if has_res else 0)
    else:
        o_bytes = n_out * jnp.dtype(out_dtype).itemsize
        extra_bytes = (4 if has_res else 0) + (aux.dtype.itemsize if has_aux else 0)
    pa, pb, po = (2 if "a" in pairs else 1), (2 if "b" in pairs else 1), (2 if "o" in pairs else 1)
    tm, tn = _mm_tiles(m, n, k + k_second, a.dtype.itemsize * (3 if pa == 2 else 1),
                       b.dtype.itemsize * (3 if pb == 2 else 1), o_bytes * po, extra_bytes, tm_fixed, tn_fixed)
    dims = _DIMS[mode]
    if mode == "tn":
        a_spec = pl.BlockSpec((k, pa * tm), lambda i, j: (0, i))
    else:
        a_spec = pl.BlockSpec((tm, pa * k), lambda i, j: (i, 0))
    if b_blk is None:
        if mode == "nt":
            b_spec = pl.BlockSpec((tn, k), lambda i, j: (j, 0))
        else:
            b_spec = pl.BlockSpec((k, pb * tn), lambda i, j: (0, j))
    elif k_blocked and mode == "nt":
        b_spec = pl.BlockSpec((N_DEV, tn, bc), lambda i, j: (0, j, 0))
    elif k_blocked:
        b_spec = pl.BlockSpec((N_DEV, br, tn), lambda i, j: (0, 0, j))
    elif mode == "nt":
        b_spec = pl.BlockSpec((None, tn, k), lambda i, j: (j, 0, 0))
    else:
        b_spec = pl.BlockSpec((None, k, tn), lambda i, j: (j, 0, 0))
    if o_blk is None:
        o_spec = pl.BlockSpec((tm, po * tn), lambda i, j: (i, j))
        o_shape = (m, po * n)
    elif o_blk == "rows":
        o_spec = pl.BlockSpec((None, tm, tn), lambda i, j: (i, 0, j))
        o_shape = (N_DEV, tm, n)
    else:
        o_spec = pl.BlockSpec((None, tm, tn), lambda i, j: (j, i, 0))
        o_shape = (N_DEV, m, tn)

    def body(*refs):
        a_ref, b_ref = refs[0], refs[1]
        pos = 2
        res_ref = aux_ref = None
        if has_res:
            res_ref = refs[pos]
            pos += 1
        if has_aux:
            aux_ref = refs[pos]
            pos += 1
        if has_norm:
            x_ref, g_ref, dres_ref = refs[pos:pos + 3]
            pos += 3
        if has_normf:
            g_ref = refs[pos]
            pos += 1
        if second is not None:
            a2_ref, b2_ref = refs[pos:pos + 2]
            pos += 2
        if after is not None:
            pos += 1
        outs = refs[pos:]
        if k_blocked and mode == "nt":
            r = None
            for d in range(N_DEV):
                part = lax.dot_general(a_ref[:, d * bc:(d + 1) * bc].astype(BF16), b_ref[d].astype(BF16), dims,
                                       preferred_element_type=F32)
                r = part if r is None else r + part
        else:
            bv = b_ref[...].reshape(k, tn) if k_blocked else b_ref[...]
            av = _compact(a_ref[...].astype(F32)) if pa == 2 else a_ref[...]
            bv = _compact(bv.astype(F32)) if pb == 2 else bv
            r = lax.dot_general(av.astype(BF16), bv.astype(BF16), dims, preferred_element_type=F32)
        if second is not None:
            av2 = _compact(a2_ref[...].astype(F32)) if pa == 2 else a2_ref[...]
            r = r + lax.dot_general(av2.astype(BF16), b2_ref[...].astype(BF16), dims, preferred_element_type=F32)
        if po == 2:
            r = _expand(r)
        if epi == "relu2":
            r = jnp.maximum(r, 0.0)
            outs[0][...] = r.astype(outs[0].dtype)
            outs[1][...] = (r * r).astype(outs[1].dtype)
        elif has_norm:
            xv = x_ref[...]
            rs = lax.rsqrt(jnp.mean(xv * xv, axis=1, keepdims=True) + EPS)
            xh = xv * rs
            dxh = r * g_ref[...]
            dx = rs * (dxh - xh * jnp.mean(dxh * xh, axis=1, keepdims=True)) + dres_ref[...]
            outs[0][...] = dx
            outs[1][...] = dx.astype(BF16)

            @pl.when(pl.program_id(0) == 0)
            def _():
                outs[2][...] = jnp.zeros_like(outs[2])

            outs[2][...] += jnp.sum(r * xh, axis=0, keepdims=True)
        else:
            if epi == "mul2aux":
                r = r * (2.0 * aux_ref[...].astype(F32))
            if has_res:
                r = r + res_ref[...]
            outs[0][...] = r.astype(outs[0].dtype)
            if has_normf:
                rs = lax.rsqrt(jnp.mean(r * r, axis=1, keepdims=True) + EPS)
                outs[1][...] = (r * rs * g_ref[...]).astype(BF16)

    in_specs = [a_spec, b_spec]
    args = [a, b]
    if has_res:
        in_specs.append(o_spec)
        args.append(res)
    if has_aux:
        in_specs.append(o_spec)
        args.append(aux)
    vec_spec = pl.BlockSpec((1, n), lambda i, j: (0, 0))
    if has_norm:
        in_specs += [o_spec, vec_spec, o_spec]
        args += [norm[0], norm[1].reshape(1, n), norm[2]]
    if has_normf:
        in_specs.append(vec_spec)
        args.append(norm.reshape(1, n))
    if second is not None:
        in_specs += [pl.BlockSpec((tm, pa * k_second), lambda i, j: (i, 0)),
                     pl.BlockSpec((k_second, tn), lambda i, j: (0, j))]
        args += list(second)
    if after is not None:
        in_specs.append(pl.BlockSpec(memory_space=pl.ANY))
        args.append(after)
    if has_norm:
        out_specs = [o_spec, o_spec, vec_spec]
        out_shape = [jax.ShapeDtypeStruct(o_shape, F32), jax.ShapeDtypeStruct(o_shape, BF16),
                     jax.ShapeDtypeStruct((1, n), F32)]
    elif has_normf:
        out_specs = [o_spec, o_spec]
        out_shape = [jax.ShapeDtypeStruct(o_shape, out_dtype), jax.ShapeDtypeStruct(o_shape, BF16)]
    else:
        out_specs = [o_spec] * n_out
        out_shape = [jax.ShapeDtypeStruct(o_shape, out_dtype)] * n_out
    out = pl.pallas_call(
        body, name=name, grid=(m // tm, n // tn),
        in_specs=in_specs, out_specs=out_specs, out_shape=out_shape, compiler_params=_cparams(),
    )(*args)
    return out if len(out) > 1 else out[0]


def _rmsnorm_fwd(xarr, colblk, width, g, name, after=None):
    rows = xarr.shape[0]
    tm = min(TM, rows)

    def body(x_ref, g_ref, *rest):
        y_ref = rest[-1]
        x = x_ref[...].astype(F32)
        r = lax.rsqrt(jnp.mean(x * x, axis=1, keepdims=True) + EPS)
        y_ref[...] = (x * r * g_ref[...]).astype(y_ref.dtype)

    in_specs = [pl.BlockSpec((tm, width), lambda i: (i, colblk)), pl.BlockSpec((1, width), lambda i: (0, 0))]
    args = [xarr, g.reshape(1, width)]
    if after is not None:
        in_specs.append(pl.BlockSpec(memory_space=pl.ANY))
        args.append(after)
    return pl.pallas_call(
        body, name=name, grid=(rows // tm,), in_specs=in_specs,
        out_specs=pl.BlockSpec((tm, width), lambda i: (i, 0)),
        out_shape=jax.ShapeDtypeStruct((rows, width), BF16), compiler_params=_cparams(),
    )(*args)


def _rmsnorm_bwd(xarr, colblk, width, g, dy, dres, out_dtype, name):
    rows = xarr.shape[0]
    tm = min(TM, rows)
    has_res = dres is not None

    def body(*refs):
        x_ref, g_ref, dy_ref = refs[0], refs[1], refs[2]
        dres_ref = refs[3] if has_res else None
        dx_ref, dg_ref = refs[-2], refs[-1]
        x = x_ref[...].astype(F32)
        dyv = dy_ref[...].astype(F32)
        r = lax.rsqrt(jnp.mean(x * x, axis=1, keepdims=True) + EPS)
        xh = x * r
        dxh = dyv * g_ref[...]
        dx = r * (dxh - xh * jnp.mean(dxh * xh, axis=1, keepdims=True))
        if has_res:
            dx = dx + dres_ref[...]
        dx_ref[...] = dx.astype(dx_ref.dtype)

        @pl.when(pl.program_id(0) == 0)
        def _():
            dg_ref[...] = jnp.zeros_like(dg_ref)

        dg_ref[...] += jnp.sum(dyv * xh, axis=0, keepdims=True)

    row_spec = pl.BlockSpec((tm, width), lambda i: (i, 0))
    vec_spec = pl.BlockSpec((1, width), lambda i: (0, 0))
    in_specs = [pl.BlockSpec((tm, width), lambda i: (i, colblk)), vec_spec, row_spec]
    args = [xarr, g.reshape(1, width), dy]
    if has_res:
        in_specs.append(row_spec)
        args.append(dres)
    return pl.pallas_call(
        body, name=name, grid=(rows // tm,), in_specs=in_specs, out_specs=[row_spec, vec_spec],
        out_shape=[jax.ShapeDtypeStruct((rows, width), out_dtype), jax.ShapeDtypeStruct((1, width), F32)],
        compiler_params=_cparams(),
    )(*args)


def _loss_head(x, g, tgt):
    rows, width = x.shape
    tm = min(TM, rows)

    def body(x_ref, g_ref, t_ref, dx_ref, dxb_ref, dg_ref, loss_ref):
        xv = x_ref[...]
        gv = g_ref[...]
        r = lax.rsqrt(jnp.mean(xv * xv, axis=1, keepdims=True) + EPS)
        xh = xv * r
        err = xh * gv - t_ref[...]
        part = 0.5 * jnp.sum(jnp.mean(err * err, axis=1, keepdims=True), axis=0, keepdims=True)
        dyv = err * (1.0 / width)
        dxh = dyv * gv
        dxv = r * (dxh - xh * jnp.mean(dxh * xh, axis=1, keepdims=True))
        dx_ref[...] = dxv
        dxb_ref[...] = dxv.astype(BF16)

        @pl.when(pl.program_id(0) == 0)
        def _():
            dg_ref[...] = jnp.zeros_like(dg_ref)
            loss_ref[...] = jnp.zeros_like(loss_ref)

        dg_ref[...] += jnp.sum(dyv * xh, axis=0, keepdims=True)
        loss_ref[...] += jnp.broadcast_to(part, loss_ref.shape)

    row_spec = pl.BlockSpec((tm, width), lambda i: (i, 0))
    vec_spec = pl.BlockSpec((1, width), lambda i: (0, 0))
    return pl.pallas_call(
        body, name="loss_head", grid=(rows // tm,), in_specs=[row_spec, vec_spec, row_spec],
        out_specs=[row_spec, row_spec, vec_spec, pl.BlockSpec((1, SLOT), lambda i: (0, 0))],
        out_shape=[jax.ShapeDtypeStruct((rows, width), F32), jax.ShapeDtypeStruct((rows, width), BF16),
                   jax.ShapeDtypeStruct((1, width), F32), jax.ShapeDtypeStruct((1, SLOT), F32)],
        compiler_params=_cparams(),
    )(x, g.reshape(1, width), tgt)


def _lane_consts():
    half = 16
    inv = ROPE_THETA ** (-(jnp.arange(half, dtype=F32) * 2.0) / 32)
    lane = jnp.arange(SLOT)
    first = (lane >= 64) & (lane < 80)
    second = (lane >= 80) & (lane < 96)
    inv_lane = jnp.where(first | second, inv[(lane - 64) % half], 0.0)
    rows = [inv_lane, (lane < 64).astype(F32), first.astype(F32), second.astype(F32)]
    rows += [jnp.zeros((SLOT,), F32)] * 4
    return jnp.stack(rows).astype(F32)


def _rope_tables(pos_col, consts):
    rows = pos_col.shape[0]
    tm = min(TM, rows)

    def body(p_ref, k_ref, c_ref, s1_ref, s2_ref):
        ang = p_ref[...] * k_ref[0:1, :]
        cos, sin = jnp.cos(ang), jnp.sin(ang)
        first, second = k_ref[2:3, :], k_ref[3:4, :]
        c_ref[...] = k_ref[1:2, :] + (first + second) * cos
        s1_ref[...] = -first * sin
        s2_ref[...] = second * sin

    spec = pl.BlockSpec((tm, SLOT), lambda i: (i, 0))
    shp = jax.ShapeDtypeStruct((rows, SLOT), F32)
    return pl.pallas_call(
        body, name="rope_tables", grid=(rows // tm,),
        in_specs=[pl.BlockSpec((tm, 1), lambda i: (i, 0)), pl.BlockSpec((8, SLOT), lambda i: (0, 0))],
        out_specs=[spec, spec, spec], out_shape=[shp, shp, shp], compiler_params=_cparams(),
    )(pos_col, consts)


def _rot(xv, c, s1, s2):
    return xv * c + pltpu.roll(xv, SLOT - 16, 1) * s1 + pltpu.roll(xv, 16, 1) * s2


def _rot_t(dy, c, s1, s2):
    return dy * c + pltpu.roll(dy * s1, 16, 1) + pltpu.roll(dy * s2, SLOT - 16, 1)


def _mla_rope_fwd(qraw, kvraw, proj, tabs):
    rows = qraw.shape[0]
    tm = min(TM_ROPE, rows)
    hw = MLA_HEADS * SLOT

    def body(q_ref, kv_ref, kr_ref, c_ref, s1_ref, s2_ref, qo, ko, vo):
        c, s1, s2 = c_ref[...], s1_ref[...], s2_ref[...]
        kr = _rot(kr_ref[...], c, s1, s2)
        low = lax.broadcasted_iota(jnp.int32, (tm, SLOT), 1) < HEAD_DIM
        for h in range(MLA_HEADS):
            sl = slice(h * SLOT, (h + 1) * SLOT)
            qo[:, sl] = (_rot(q_ref[:, sl], c, s1, s2) * MLA_Q_SCALE).astype(BF16)
            kvh = kv_ref[:, sl]
            ko[:, sl] = (jnp.where(low, kvh, 0.0) + kr).astype(BF16)
            vo[:, sl] = pltpu.roll(jnp.where(low, 0.0, kvh), HEAD_DIM, 1).astype(BF16)

    tab = pl.BlockSpec((tm, SLOT), lambda i: (i, 0))
    wide = pl.BlockSpec((tm, hw), lambda i: (i, 0))
    shp = jax.ShapeDtypeStruct((rows, hw), BF16)
    return pl.pallas_call(
        body, name="mla_rope_fwd", grid=(rows // tm,),
        in_specs=[wide, wide, pl.BlockSpec((tm, SLOT), lambda i: (i, 3)),
                  tab, tab, tab],
        out_specs=[wide, wide, wide], out_shape=[shp, shp, shp], compiler_params=_cparams(),
    )(qraw, kvraw, proj, *tabs)


def _mla_rope_bwd(dq, dk, dv, tabs, consts):
    rows = dq.shape[0]
    tm = min(TM_ROPE, rows)
    hw = MLA_HEADS * SLOT

    def body(dq_ref, dk_ref, dv_ref, c_ref, s1_ref, s2_ref, k_ref, dqo, dkvo, dkro):
        c, s1, s2 = c_ref[...], s1_ref[...], s2_ref[...]
        ksum = jnp.zeros((tm, SLOT), F32)
        low = lax.broadcasted_iota(jnp.int32, (tm, SLOT), 1) < HEAD_DIM
        for h in range(MLA_HEADS):
            sl = slice(h * SLOT, (h + 1) * SLOT)
            dqo[:, sl] = _rot_t(dq_ref[:, sl], c, s1, s2).astype(BF16)
            dkh = dk_ref[:, sl]
            ksum = ksum + dkh
            dvh = pltpu.roll(jnp.where(low, dv_ref[:, sl], 0.0), HEAD_DIM, 1)
            dkvo[:, sl] = (jnp.where(low, dkh, 0.0) + dvh).astype(BF16)
        dkro[...] = _rot_t(ksum, c, s1, s2) * (k_ref[2:3, :] + k_ref[3:4, :])

    tab = pl.BlockSpec((tm, SLOT), lambda i: (i, 0))
    wide = pl.BlockSpec((tm, hw), lambda i: (i, 0))
    return pl.pallas_call(
        body, name="mla_rope_bwd", grid=(rows // tm,),
        in_specs=[wide, wide, wide, tab, tab, tab, pl.BlockSpec((8, SLOT), lambda i: (0, 0))],
        out_specs=[wide, wide, tab],
        out_shape=[jax.ShapeDtypeStruct((rows, hw), BF16), jax.ShapeDtypeStruct((rows, hw), BF16),
                   jax.ShapeDtypeStruct((rows, SLOT), F32)],
        compiler_params=_cparams(),
    )(dq, dk, dv, *tabs, consts)


def _nt(a, b):
    return lax.dot_general(a, b, _DIMS["nt"], preferred_element_type=F32)


def _tn(a, b):
    return lax.dot_general(a, b, _DIMS["tn"], preferred_element_type=F32)


def _nn(a, b):
    return lax.dot_general(a, b, _DIMS["nn"], preferred_element_type=F32)


def _mla_attn_fwd(q, k, v):
    rows = q.shape[0]
    t = min(TQ_MLA, rows)
    nt = rows // t
    wide = MLA_PACK * SLOT

    def body(q_ref, k_ref, v_ref, o_ref, lse_ref, m_sc, l_sc, acc_sc):
        i, j = pl.program_id(1), pl.program_id(2)

        @pl.when(j == 0)
        def _():
            m_sc[...] = jnp.full_like(m_sc, NEG)
            l_sc[...] = jnp.zeros_like(l_sc)
            acc_sc[...] = jnp.zeros_like(acc_sc)

        def step(diagonal):
            for hh in range(MLA_PACK):
                sl = slice(hh * SLOT, (hh + 1) * SLOT)
                s = _nt(k_ref[:, sl], q_ref[:, sl])
                if diagonal:
                    key = lax.broadcasted_iota(jnp.int32, (t, t), 0)
                    s = jnp.where(key <= lax.broadcasted_iota(jnp.int32, (t, t), 1), s, NEG)
                m_prev = m_sc[hh]
                m_new = jnp.maximum(m_prev, jnp.max(s, axis=0, keepdims=True))
                p = jnp.exp2(s - m_new)
                alpha = jnp.exp2(m_prev - m_new)
                l_new = alpha * l_sc[hh] + jnp.sum(p, axis=0, keepdims=True)
                acc = alpha * acc_sc[hh] + _tn(v_ref[:, sl], p.astype(BF16))
                if diagonal:
                    o_ref[:, sl] = (acc / l_new).T.astype(o_ref.dtype)
                    lse_ref[hh:hh + 1, :] = m_new + jnp.log(l_new) * LOG2_E
                else:
                    m_sc[hh] = m_new
                    l_sc[hh] = l_new
                    acc_sc[hh] = acc

        @pl.when(j < i)
        def _():
            step(False)

        @pl.when(j == i)
        def _():
            lse_ref[...] = jnp.zeros_like(lse_ref)
            step(True)

    q_spec = pl.BlockSpec((t, wide), lambda h, i, j: (i, h))
    kv_spec = pl.BlockSpec((t, wide), lambda h, i, j: (jnp.minimum(j, i), h))
    return pl.pallas_call(
        body, name="mla_attn_fwd", grid=(MLA_HEADS // MLA_PACK, nt, nt),
        in_specs=[q_spec, kv_spec, kv_spec],
        out_specs=[q_spec, pl.BlockSpec((None, 8, t), lambda h, i, j: (h, 0, i))],
        out_shape=[jax.ShapeDtypeStruct(q.shape, BF16),
                   jax.ShapeDtypeStruct((MLA_HEADS // MLA_PACK, 8, rows), F32)],
        scratch_shapes=[pltpu.VMEM((MLA_PACK, 1, t), F32), pltpu.VMEM((MLA_PACK, 1, t), F32),
                        pltpu.VMEM((MLA_PACK, SLOT, t), F32)],
        compiler_params=_cparams(),
    )(q, k, v)


def _mla_delta(o, do):
    rows = o.shape[0]
    t = rows
    wide = MLA_PACK * SLOT

    def body(o_ref, do_ref, d_ref):
        d_ref[...] = jnp.zeros_like(d_ref)
        ones = jnp.ones((8, SLOT), BF16)
        for hh in range(MLA_PACK):
            sl = slice(hh * SLOT, (hh + 1) * SLOT)
            prod = do_ref[:, sl].astype(F32) * o_ref[:, sl].astype(F32)
            high = prod.astype(BF16)
            low = (prod - high.astype(F32)).astype(BF16)
            d_ref[hh:hh + 1, :] = (_nt(ones, high) + _nt(ones, low))[0:1, :]

    spec = pl.BlockSpec((t, wide), lambda h, i: (i, h))
    return pl.pallas_call(
        body, name="mla_delta", grid=(MLA_HEADS // MLA_PACK, rows // t), in_specs=[spec, spec],
        out_specs=pl.BlockSpec((None, 8, t), lambda h, i: (h, 0, i)),
        out_shape=jax.ShapeDtypeStruct((MLA_HEADS // MLA_PACK, 8, rows), F32), compiler_params=_cparams(),
    )(o, do)


def _mla_attn_bwd(q, k, v, do, lse, delta, after):
    rows = q.shape[0]
    t = min(TQ_MLA, rows)
    nt = rows // t
    wide = MLA_PACK * SLOT

    def body(q_ref, k_ref, v_ref, do_ref, lse_ref, delta_ref, after_ref, dq_ref, dk_ref, dv_ref, dk_sc, dv_sc):
        j, i = pl.program_id(1), pl.program_id(2)

        @pl.when((j == 0) & (i == 0))
        def _():
            dq_ref[...] = jnp.zeros_like(dq_ref)

        @pl.when(i == 0)
        def _():
            dk_sc[...] = jnp.zeros_like(dk_sc)
            dv_sc[...] = jnp.zeros_like(dv_sc)

        def chunk(hh, rows, keys, masked):
            sl = slice(hh * SLOT, (hh + 1) * SLOT)
            n_rows = rows.stop - rows.start
            qv, kv, dov = q_ref[rows, sl], k_ref[keys, sl], do_ref[rows, sl]
            s = _nt(kv, qv)
            if masked:
                shp = (keys.stop - keys.start, n_rows)
                s = jnp.where(keys.start + lax.broadcasted_iota(jnp.int32, shp, 0)
                              <= rows.start + lax.broadcasted_iota(jnp.int32, shp, 1), s, NEG)
            p = jnp.exp2(s - lse_ref[hh:hh + 1, rows])
            dp = _nt(v_ref[keys, sl], dov)
            ds = (p * (dp - delta_ref[hh:hh + 1, rows])).astype(BF16)
            dv_sc[keys, sl] += _nn(p.astype(BF16), dov)
            dk_sc[keys, sl] += _nn(ds, qv)
            r0 = pl.multiple_of(i * t + rows.start, n_rows)
            dq_ref[pl.ds(r0, n_rows), sl] += _tn(ds, kv) * MLA_SCALE

        @pl.when(i > j)
        def _():
            for hh in range(MLA_PACK):
                chunk(hh, slice(0, t), slice(0, t), False)

        @pl.when(i == j)
        def _():
            for hh in range(MLA_PACK):
                chunk(hh, slice(0, t), slice(0, t // 2), True)
                chunk(hh, slice(t // 2, t), slice(t // 2, t), True)

        @pl.when(i == nt - 1)
        def _():
            dk_ref[...] = dk_sc[...] * (1.0 / LOG2_E)
            dv_ref[...] = dv_sc[...]

    q_spec = pl.BlockSpec((t, wide), lambda h, j, i: (jnp.maximum(i, j), h))
    kv_spec = pl.BlockSpec((t, wide), lambda h, j, i: (j, h))
    row_spec = pl.BlockSpec((None, 8, t), lambda h, j, i: (h, 0, jnp.maximum(i, j)))
    head_spec = pl.BlockSpec((rows, wide), lambda h, j, i: (0, h))
    shp = jax.ShapeDtypeStruct(q.shape, F32)
    return pl.pallas_call(
        body, name="mla_attn_bwd", grid=(MLA_HEADS // MLA_PACK, nt, nt),
        in_specs=[q_spec, kv_spec, kv_spec, q_spec, row_spec, row_spec, pl.BlockSpec(memory_space=pl.ANY)],
        out_specs=[head_spec, kv_spec, kv_spec], out_shape=[shp, shp, shp],
        scratch_shapes=[pltpu.VMEM((t, wide), F32), pltpu.VMEM((t, wide), F32)],
        compiler_params=_cparams(),
    )(q, k, v, do, lse, delta, after)


def _swa_specs(t):
    def prev(i):
        return jnp.maximum(i - 1, 0)
    kw = SWA_PACK * SLOT
    k0, v0 = SWA_HEADS // SWA_PACK, (SWA_HEADS + SWA_KV_HEADS) // SWA_PACK
    q3 = pl.BlockSpec((t, SWA_PACK * SWA_GROUP * SLOT), lambda h, i: (i, h))
    kp = pl.BlockSpec((t, kw), lambda h, i: (prev(i), k0 + h))
    kc = pl.BlockSpec((t, kw), lambda h, i: (i, k0 + h))
    vp = pl.BlockSpec((t, kw), lambda h, i: (prev(i), v0 + h))
    vc = pl.BlockSpec((t, kw), lambda h, i: (i, v0 + h))
    pcol = pl.BlockSpec((t, 1), lambda h, i: (i, 0))
    prow_p = pl.BlockSpec((1, t), lambda h, i: (0, prev(i)))
    prow_c = pl.BlockSpec((1, t), lambda h, i: (0, i))
    return [q3, kp, kc, vp, vc, pcol, prow_p, prow_c]


def _stack(ref, first):
    return jnp.concatenate([ref[:, (first + g) * SLOT:(first + g + 1) * SLOT] for g in range(SWA_GROUP)], axis=0)


def _swa_logits(q3, kp, kc, pq, pkp, pkc, slope_ref, kvh, i, t):
    r = lax.broadcasted_iota(jnp.int32, (t, t), 0)
    c = lax.broadcasted_iota(jnp.int32, (t, t), 1)
    ok_c = c <= r
    ok_p = (c - r) > jnp.where(i > 0, 0, t)
    dist_p, dist_c = pq - pkp, pq - pkc
    s_p3 = _nt(q3, kp) * (HEAD_DIM ** -0.5)
    s_c3 = _nt(q3, kc) * (HEAD_DIM ** -0.5)
    out = []
    for g in range(SWA_GROUP):
        slope = slope_ref[kvh * SWA_GROUP + g]
        rows = slice(g * t, (g + 1) * t)
        out.append((jnp.where(ok_p, s_p3[rows] - slope * dist_p, NEG),
                    jnp.where(ok_c, s_c3[rows] - slope * dist_c, NEG)))
    return out


def _swa_attn_fwd(proj, pos_col, pos_row, slopes, sinks):
    rows = proj.shape[0]
    t = WINDOW
    hw = SWA_HEADS * SLOT

    def body(slope_ref, sink_ref, q_ref, kp_ref, kc_ref, vp_ref, vc_ref, pq_ref, pkp_ref, pkc_ref, o_ref, lse_ref):
        i = pl.program_id(1)
        for kv in range(SWA_PACK):
            kvh = pl.program_id(0) * SWA_PACK + kv
            ksl = slice(kv * SLOT, (kv + 1) * SLOT)
            logits = _swa_logits(_stack(q_ref, kv * SWA_GROUP), kp_ref[:, ksl], kc_ref[:, ksl], pq_ref[...],
                                 pkp_ref[...], pkc_ref[...], slope_ref, kvh, i, t)
            e_p, e_c, norm = [], [], []
            for g, (s_p, s_c) in enumerate(logits):
                sl = slice((kv * SWA_GROUP + g) * SLOT, (kv * SWA_GROUP + g + 1) * SLOT)
                sink = sink_ref[kvh * SWA_GROUP + g]
                m = jnp.maximum(jnp.maximum(jnp.max(s_p, axis=1, keepdims=True),
                                            jnp.max(s_c, axis=1, keepdims=True)), sink)
                ep, ec = jnp.exp(s_p - m), jnp.exp(s_c - m)
                l = jnp.sum(ep, axis=1, keepdims=True) + jnp.sum(ec, axis=1, keepdims=True) + jnp.exp(sink - m)
                e_p.append(ep.astype(BF16))
                e_c.append(ec.astype(BF16))
                norm.append(l)
                lse_ref[:, sl] = jnp.broadcast_to(m + jnp.log(l), (t, SLOT))
            acc = (_nn(jnp.concatenate(e_p, axis=0), vp_ref[:, ksl])
                   + _nn(jnp.concatenate(e_c, axis=0), vc_ref[:, ksl]))
            for g in range(SWA_GROUP):
                sl = slice((kv * SWA_GROUP + g) * SLOT, (kv * SWA_GROUP + g + 1) * SLOT)
                o_ref[:, sl] = (acc[g * t:(g + 1) * t] / norm[g]).astype(o_ref.dtype)

    smem = pl.BlockSpec(memory_space=pltpu.SMEM)
    out_spec = pl.BlockSpec((t, SWA_PACK * SWA_GROUP * SLOT), lambda h, i: (i, h))
    return pl.pallas_call(
        body, name="swa_attn_fwd", grid=(SWA_KV_HEADS // SWA_PACK, rows // t),
        in_specs=[smem, smem] + _swa_specs(t), out_specs=[out_spec, out_spec],
        out_shape=[jax.ShapeDtypeStruct((rows, hw), BF16), jax.ShapeDtypeStruct((rows, hw), F32)],
        compiler_params=_cparams(),
    )(slopes, sinks, proj, proj, proj, proj, proj, pos_col, pos_row, pos_row)


def _swa_attn_bwd(proj, o, do, lse, pos_col, pos_row, slopes, sinks, after):
    rows = proj.shape[0]
    t = WINDOW
    hw = SWA_HEADS * SLOT
    scale = HEAD_DIM ** -0.5

    def body(slope_ref, sink_ref, q_ref, kp_ref, kc_ref, vp_ref, vc_ref, pq_ref, pkp_ref, pkc_ref,
             o_ref, do_ref, lse_ref, after_ref, dq_ref, dk_ref, dv_ref, dsink_ref):
        i = pl.program_id(1)

        @pl.when(i == 0)
        def _():
            dk_ref[...] = jnp.zeros_like(dk_ref)
            dv_ref[...] = jnp.zeros_like(dv_ref)
            dsink_ref[...] = jnp.zeros_like(dsink_ref)

        r_c = pl.multiple_of(i * t, t)
        r_p = pl.multiple_of(jnp.maximum(i - 1, 0) * t, t)
        for kv in range(SWA_PACK):
            kvh = pl.program_id(0) * SWA_PACK + kv
            ksl = slice(kv * SLOT, (kv + 1) * SLOT)
            q3, do3 = _stack(q_ref, kv * SWA_GROUP), _stack(do_ref, kv * SWA_GROUP)
            logits = _swa_logits(q3, kp_ref[:, ksl], kc_ref[:, ksl], pq_ref[...], pkp_ref[...], pkc_ref[...],
                                 slope_ref, kvh, i, t)
            dp_p3, dp_c3 = _nt(do3, vp_ref[:, ksl]), _nt(do3, vc_ref[:, ksl])
            p_p, p_c, ds_p, ds_c = [], [], [], []
            for g, (s_p, s_c) in enumerate(logits):
                head = kv * SWA_GROUP + g
                sl = slice(head * SLOT, (head + 1) * SLOT)
                rws = slice(g * t, (g + 1) * t)
                lse_g = lse_ref[:, head * SLOT:head * SLOT + 1]
                pp, pc = jnp.exp(s_p - lse_g), jnp.exp(s_c - lse_g)
                delta = jnp.sum(do_ref[:, sl].astype(F32) * o_ref[:, sl].astype(F32), axis=1, keepdims=True)
                p_p.append(pp.astype(BF16))
                p_c.append(pc.astype(BF16))
                ds_p.append((pp * (dp_p3[rws] - delta)).astype(BF16))
                ds_c.append((pc * (dp_c3[rws] - delta)).astype(BF16))
                sink = sink_ref[kvh * SWA_GROUP + g]
                dsink = -jnp.sum(jnp.exp(sink - lse_g) * delta, axis=0, keepdims=True)
                dsink_ref[head * 8:(head + 1) * 8, :] += jnp.broadcast_to(dsink, (8, SLOT))
            p_p3, p_c3 = jnp.concatenate(p_p, axis=0), jnp.concatenate(p_c, axis=0)
            ds_p3, ds_c3 = jnp.concatenate(ds_p, axis=0), jnp.concatenate(ds_c, axis=0)
            dq3 = (_nn(ds_p3, kp_ref[:, ksl]) + _nn(ds_c3, kc_ref[:, ksl])) * scale
            for g in range(SWA_GROUP):
                head = kv * SWA_GROUP + g
                dq_ref[:, head * SLOT:(head + 1) * SLOT] = dq3[g * t:(g + 1) * t]
            dk_ref[pl.ds(r_c, t), ksl] += _tn(ds_c3, q3) * scale
            dv_ref[pl.ds(r_c, t), ksl] += _tn(p_c3, do3)
            dk_ref[pl.ds(r_p, t), ksl] += _tn(ds_p3, q3) * scale
            dv_ref[pl.ds(r_p, t), ksl] += _tn(p_p3, do3)

    smem = pl.BlockSpec(memory_space=pltpu.SMEM)
    qlike = pl.BlockSpec((t, SWA_PACK * SWA_GROUP * SLOT), lambda h, i: (i, h))
    kv_out = pl.BlockSpec((rows, SWA_PACK * SLOT), lambda h, i: (0, h))
    return pl.pallas_call(
        body, name="swa_attn_bwd", grid=(SWA_KV_HEADS // SWA_PACK, rows // t),
        in_specs=[smem, smem] + _swa_specs(t) + [qlike, qlike, qlike, pl.BlockSpec(memory_space=pl.ANY)],
        out_specs=[qlike, kv_out, kv_out,
                   pl.BlockSpec((SWA_PACK * SWA_GROUP * 8, SLOT), lambda h, i: (h, 0))],
        out_shape=[jax.ShapeDtypeStruct((rows, hw), F32), jax.ShapeDtypeStruct((rows, SWA_KV_HEADS * SLOT), F32),
                   jax.ShapeDtypeStruct((rows, SWA_KV_HEADS * SLOT), F32),
                   jax.ShapeDtypeStruct((SWA_HEADS * 8, SLOT), F32)],
        compiler_params=_cparams(),
    )(slopes, sinks, proj, proj, proj, proj, proj, pos_col, pos_row, pos_row, o, do, lse, after)


def _cross_attn_fwd(proj, qoff, kvmem):
    rows = proj.shape[0]
    t = min(TQ_CROSS, rows)

    def body(q_ref, k_ref, v_ref, o_ref):
        s = _nt(k_ref[...], q_ref[...].astype(BF16)) * (HEAD_DIM ** -0.5)
        e = jnp.exp(s - jnp.max(s, axis=0, keepdims=True))
        p = e / jnp.sum(e, axis=0, keepdims=True)
        o_ref[...] = _tn(v_ref[...], p.astype(BF16)).T.astype(o_ref.dtype)

    return pl.pallas_call(
        body, name="cross_attn_fwd", grid=(rows // t, MEM_HEADS),
        in_specs=[pl.BlockSpec((t, SLOT), lambda i, h: (i, qoff + h)),
                  pl.BlockSpec((N_MEM, SLOT), lambda i, h: (0, h)),
                  pl.BlockSpec((N_MEM, SLOT), lambda i, h: (0, MEM_HEADS + h))],
        out_specs=pl.BlockSpec((t, SLOT), lambda i, h: (i, h)),
        out_shape=jax.ShapeDtypeStruct((rows, MEM_HEADS * SLOT), BF16), compiler_params=_cparams(),
    )(proj, kvmem, kvmem)


def _cross_attn_bwd(proj, qoff, kvmem, do, do_off):
    rows = proj.shape[0]
    t = min(TQ_CROSS, rows)
    scale = HEAD_DIM ** -0.5

    def body(q_ref, k_ref, v_ref, do_ref, dq_ref, dk_ref, dv_ref):
        @pl.when(pl.program_id(1) == 0)
        def _():
            dk_ref[...] = jnp.zeros_like(dk_ref)
            dv_ref[...] = jnp.zeros_like(dv_ref)

        qv, kv, dov = q_ref[...].astype(BF16), k_ref[...], do_ref[...]
        s = _nt(kv, qv) * scale
        e = jnp.exp(s - jnp.max(s, axis=0, keepdims=True))
        p = e / jnp.sum(e, axis=0, keepdims=True)
        dp = _nt(v_ref[...], dov)
        ds = (p * (dp - jnp.sum(p * dp, axis=0, keepdims=True))).astype(BF16)
        dq_ref[...] = _tn(ds, kv) * scale
        dk_ref[...] += _nn(ds, qv) * scale
        dv_ref[...] += _nn(p.astype(BF16), dov)

    mem_out = pl.BlockSpec((N_MEM, SLOT), lambda h, i: (0, h))
    return pl.pallas_call(
        body, name="cross_attn_bwd", grid=(MEM_HEADS, rows // t),
        in_specs=[pl.BlockSpec((t, SLOT), lambda h, i: (i, qoff + h)),
                  pl.BlockSpec((N_MEM, SLOT), lambda h, i: (0, h)),
                  pl.BlockSpec((N_MEM, SLOT), lambda h, i: (0, MEM_HEADS + h)),
                  pl.BlockSpec((t, SLOT), lambda h, i: (i, do_off + h))],
        out_specs=[pl.BlockSpec((t, SLOT), lambda h, i: (i, h)), mem_out, mem_out],
        out_shape=[jax.ShapeDtypeStruct((rows, MEM_HEADS * SLOT), F32),
                   jax.ShapeDtypeStruct((N_MEM, MEM_HEADS * SLOT), F32),
                   jax.ShapeDtypeStruct((N_MEM, MEM_HEADS * SLOT), F32)],
        compiler_params=_cparams(),
    )(proj, kvmem, kvmem, do)


def _place():
    return lax.axis_index("x"), lax.axis_index("y"), lax.axis_index("c")


def _flip(v, bit):
    return 1 - v if bit else v


def _all_gather(blocks, name):
    nb = len(blocks)

    def body(*refs):
        x_refs, out_refs = refs[:nb], refs[nb:2 * nb]
        send_sems, recv_sems, local_sems = refs[2 * nb:]
        x, y, c = _place()
        me, sibling = (x, y, c), (x, y, 1 - c)
        chips = [(1 - x, y), (x, 1 - y), (1 - x, 1 - y)]

        def copy(b, k, blk, to, from_input=False):
            slot = out_refs[b].at[4 * blk[0] + 2 * blk[1] + blk[2]]
            return pltpu.make_async_remote_copy(
                src_ref=x_refs[b] if from_input else slot, dst_ref=slot,
                send_sem=send_sems.at[b, k], recv_sem=recv_sems.at[b, k],
                device_id=to, device_id_type=pl.DeviceIdType.MESH)

        mine = [pltpu.make_async_copy(x_refs[b], out_refs[b].at[4 * x + 2 * y + c], local_sems.at[b])
                for b in range(nb)]
        for cp in mine:
            cp.start()
        first = []
        for b in range(nb):
            first.append(copy(b, 0, me, sibling, from_input=True))
            first += [copy(b, 1 + n, me, (*chip, c), from_input=True) for n, chip in enumerate(chips)]
        for cp in first:
            cp.start()
        passed = []
        for n, chip in enumerate(chips):
            for b in range(nb):
                copy(b, 1 + n, (*chip, c), me).wait_recv()
                passed.append(copy(b, 4 + n, (*chip, c), sibling))
                passed[-1].start()
        for b in range(nb):
            copy(b, 0, sibling, me).wait_recv()
            for n, chip in enumerate(chips):
                copy(b, 4 + n, (*chip, 1 - c), me).wait_recv()
        for cp in first + passed:
            cp.wait_send()
        for cp in mine:
            cp.wait()

    any_spec = pl.BlockSpec(memory_space=pl.ANY)
    return pl.pallas_call(
        body, name=name, in_specs=[any_spec] * nb, out_specs=[any_spec] * nb,
        out_shape=[jax.ShapeDtypeStruct((N_DEV,) + blk.shape, blk.dtype) for blk in blocks],
        scratch_shapes=[pltpu.SemaphoreType.DMA((nb, 7)), pltpu.SemaphoreType.DMA((nb, 7)),
                        pltpu.SemaphoreType.DMA((nb,))],
    )(*blocks)


def _peers(x, y, c):
    out = []
    for n in range(1, N_DEV):
        peer = (_flip(x, n & 4), _flip(y, n & 2), _flip(c, n & 1))
        out.append((n - 1, peer, 4 * peer[0] + 2 * peer[1] + peer[2]))
    return out


_HBM = pl.BlockSpec(memory_space=pltpu.HBM)
_SEM = pl.BlockSpec(memory_space=pltpu.SEMAPHORE)


def _exchange_start(srcs, scatter, name, after=None):
    ns = len(srcs)
    lands = [lax.empty(s.shape if scatter else (N_DEV,) + s.shape, s.dtype) for s in srcs]

    def body(*refs):
        src_refs, land_refs = refs[:ns], refs[ns:2 * ns]
        pos = 2 * ns + (1 if after is not None else 0)
        send_sems, recv_sems, token = refs[pos], refs[pos + 1], refs[-1]
        x, y, c = _place()
        my_idx = 4 * x + 2 * y + c
        for col, peer, peer_idx in _peers(x, y, c):
            for b in range(ns):
                pltpu.make_async_remote_copy(
                    src_ref=src_refs[b].at[peer_idx] if scatter else src_refs[b], dst_ref=land_refs[b].at[my_idx],
                    send_sem=send_sems.at[b * (N_DEV - 1) + col], recv_sem=recv_sems.at[b * (N_DEV - 1) + col],
                    device_id=peer, device_id_type=pl.DeviceIdType.MESH).start()
        token[...] = jnp.zeros_like(token)

    args = [pltpu.with_memory_space_constraint(a, pltpu.HBM) for a in list(srcs) + lands]
    in_specs = [_HBM] * (2 * ns)
    if after is not None:
        args.append(after)
        in_specs.append(pl.BlockSpec(memory_space=pl.ANY))
    out = pl.pallas_call(
        body, name=name, in_specs=in_specs,
        out_specs=[_SEM, _SEM] + [_HBM] * (2 * ns) + [pl.BlockSpec(memory_space=pltpu.VMEM)],
        out_shape=[pltpu.SemaphoreType.DMA((ns * (N_DEV - 1),)), pltpu.SemaphoreType.DMA((ns * (N_DEV - 1),))]
        + [pltpu.HBM(a.shape, a.dtype) for a in list(srcs) + lands] + [jax.ShapeDtypeStruct((8, SLOT), F32)],
        input_output_aliases={k: 2 + k for k in range(2 * ns)},
        compiler_params=pltpu.CompilerParams(has_side_effects=pltpu.SideEffectType.DATAFLOW_SIDE_EFFECTING),
    )(*args)
    return (out[0], out[1], out[2:2 + ns], out[2 + ns:2 + 2 * ns], scatter), out[-1]


def _exchange_wait(handle, after, name):
    send_sems, recv_sems, srcs, lands, scatter = handle
    ns = len(srcs)

    def body(*refs):
        src_refs, land_refs = refs[:ns], refs[ns:2 * ns]
        send_ref, recv_ref = refs[2 * ns], refs[2 * ns + 1]
        x, y, c = _place()
        for col, peer, peer_idx in _peers(x, y, c):
            for b in range(ns):
                copy = pltpu.make_async_remote_copy(
                    src_ref=src_refs[b].at[peer_idx] if scatter else src_refs[b], dst_ref=land_refs[b].at[peer_idx],
                    send_sem=send_ref.at[b * (N_DEV - 1) + col], recv_sem=recv_ref.at[b * (N_DEV - 1) + col],
                    device_id=peer, device_id_type=pl.DeviceIdType.MESH)
                copy.wait_send()
                copy.wait_recv()

    out = pl.pallas_call(
        body, name=name, in_specs=[_HBM] * (2 * ns) + [_SEM, _SEM, pl.BlockSpec(memory_space=pl.ANY)],
        out_specs=[_HBM] * (2 * ns),
        out_shape=[pltpu.HBM(a.shape, a.dtype) for a in list(srcs) + list(lands)],
        input_output_aliases={k: k for k in range(2 * ns)},
        compiler_params=pltpu.CompilerParams(has_side_effects=pltpu.SideEffectType.DATAFLOW_SIDE_EFFECTING),
    )(*srcs, *lands, send_sems, recv_sems, after)
    my_idx = 4 * lax.axis_index("x") + 2 * lax.axis_index("y") + lax.axis_index("c")
    landed = []
    for src, land in zip(out[:ns], out[ns:]):
        own = lax.dynamic_index_in_dim(src, my_idx, 0, keepdims=True) if scatter else src[None]
        landed.append(lax.dynamic_update_index_in_dim(land, own, my_idx, 0))
    return landed


def _adamw(parts, w, m, v, name):
    lyr, rows, cols = w.shape
    assert len(parts) == lyr
    tr = ADAM_ROWS if cols > 512 else 2 * ADAM_ROWS
    while rows % tr:
        tr //= 2
    tr = min(tr, rows)

    def body(*refs):
        p_refs = refs[:lyr]
        w_ref, m_ref, v_ref, g_out, d_out, m_out, v_out = refs[lyr:]
        for k in range(lyr):
            @pl.when(pl.program_id(0) == k)
            def _(p_ref=p_refs[k]):
                g = p_ref[0].astype(F32)
                for s in range(1, N_DEV):
                    g = g + p_ref[s].astype(F32)
                m2 = ADAM_B1 * m_ref[...] + (1.0 - ADAM_B1) * g
                v2 = ADAM_B2 * v_ref[...] + (1.0 - ADAM_B2) * (g * g)
                m_hat = m2 / (1.0 - ADAM_B1 ** ADAM_STEP)
                v_hat = v2 / (1.0 - ADAM_B2 ** ADAM_STEP)
                g_out[...] = g
                d_out[...] = -ADAM_LR * (m_hat / (jnp.sqrt(v_hat) + ADAM_EPS) + ADAM_WD * w_ref[...])
                m_out[...] = m2
                v_out[...] = v2

    def part_spec(k):
        return pl.BlockSpec((N_DEV, tr, cols), lambda l, i: (0, jnp.where(l == k, i, 0), 0))

    spec = pl.BlockSpec((None, tr, cols), lambda l, i: (l, i, 0))
    shp = jax.ShapeDtypeStruct((lyr, rows, cols), F32)
    return pl.pallas_call(
        body, name=name, grid=(lyr, rows // tr),
        in_specs=[part_spec(k) for k in range(lyr)] + [spec, spec, spec],
        out_specs=[spec] * 4, out_shape=[shp] * 4, compiler_params=_cparams(),
    )(*parts, w, m, v)


def _pack(arrays, lanes, row_mult, dtype):
    flat = jnp.concatenate([a.reshape(-1).astype(dtype) for a in arrays])
    unit = lanes * row_mult
    total = -(-flat.shape[0] // unit) * unit
    return jnp.pad(flat, (0, total - flat.shape[0])).reshape(total // lanes, lanes)


def _unpack(packed, shapes):
    flat = packed.reshape(-1)
    out, off = [], 0
    for shp in shapes:
        n = 1
        for d in shp:
            n *= d
        out.append(flat[off:off + n].reshape(shp))
        off += n
    return out


def _pad_slots(w, axis):
    axis = axis % w.ndim
    n = w.shape[axis] // HEAD_DIM
    shp = w.shape[:axis] + (n, HEAD_DIM) + w.shape[axis + 1:]
    pad = [(0, 0)] * (w.ndim + 1)
    pad[axis + 1] = (0, SLOT - HEAD_DIM)
    return jnp.pad(w.reshape(shp), pad).reshape(w.shape[:axis] + (n * SLOT,) + w.shape[axis + 1:])


def _unpad_slots(w, axis, keep=HEAD_DIM):
    axis = axis % w.ndim
    n = w.shape[axis] // SLOT
    shp = w.shape[:axis] + (n, SLOT) + w.shape[axis + 1:]
    idx = [slice(None)] * (w.ndim + 1)
    idx[axis + 1] = slice(0, keep)
    return w.reshape(shp)[tuple(idx)].reshape(w.shape[:axis] + (n * keep,) + w.shape[axis + 1:])


def _mla_in_pad(w):
    z = functools.partial(jnp.zeros, dtype=w.dtype)
    rows = w.shape[0]
    return jnp.concatenate([w[:, :384], z((rows, 64)), w[:, 640:672], z((rows, 32)), w[:, 384:640],
                            _pad_slots(w[:, 672:], 1)], axis=1)


def _mla_in_unpad(d):
    return jnp.concatenate([d[:, :384], d[:, 512:768], d[:, 448:480], _unpad_slots(d[:, 768:], 1)], axis=1)


def _mla_uq_pad(w):
    return jnp.pad(w.reshape(w.shape[0], MLA_HEADS, MLA_QK), ((0, 0), (0, 0), (0, SLOT - MLA_QK))).reshape(
        w.shape[0], MLA_HEADS * SLOT)


def _join(gathered, axis):
    nd, a, b = gathered.shape
    if axis == 1:
        return gathered.reshape(nd * a, b)
    return gathered.transpose(1, 0, 2).reshape(a, nd * b)


def _split(full, axis):
    r, c = full.shape
    if axis == 1:
        return full.reshape(N_DEV, r // N_DEV, c).astype(BF16)
    return full.reshape(r, N_DEV, c // N_DEV).transpose(1, 0, 2).astype(BF16)


def kernel(x, mem, positions, attn_norm_g, mlp_norm_g, mem_norm_g, final_norm_g, mla_w_in, mla_q_norm_g, mla_kv_norm_g, mla_w_uq, mla_w_ukv, swa_w_in, swa_sinks, w_mem_kv, w_o, mlp_w_up, mlp_w_down, loss_target, m_attn_norm_g, m_mlp_norm_g, m_mem_norm_g, m_final_norm_g, m_mla_w_in, m_mla_q_norm_g, m_mla_kv_norm_g, m_mla_w_uq, m_mla_w_ukv, m_swa_w_in, m_swa_sinks, m_w_mem_kv, m_w_o, m_mlp_w_up, m_mlp_w_down, v_attn_norm_g, v_mlp_norm_g, v_mem_norm_g, v_final_norm_g, v_mla_w_in, v_mla_q_norm_g, v_mla_kv_norm_g, v_mla_w_uq, v_mla_w_ukv, v_swa_w_in, v_swa_sinks, v_w_mem_kv, v_w_o, v_mlp_w_up, v_mlp_w_down):
    given = dict(locals())
    seq = x.shape[1]
    x0 = x.reshape(seq, D_MODEL)
    tgt = loss_target.reshape(seq, D_MODEL)
    mem0 = mem.reshape(N_MEM, D_MODEL)
    pos = positions.reshape(seq).astype(F32)
    pos_col, pos_row = pos.reshape(seq, 1), pos.reshape(1, seq)

    def layer_names(i):
        mixer = ("mla_w_in", "mla_w_uq", "mla_w_ukv") if i % 2 == 0 else ("swa_w_in",)
        return [(n, i // 2) for n in mixer] + [(n, i) for n in ("w_mem_kv", "w_o", "mlp_w_up", "mlp_w_down")]

    def local_weights(names):
        return [given[n][l].astype(BF16) for n, l in names]

    first_attn, first_mlp = layer_names(0)[:-2], layer_names(0)[-2:]
    weights = [dict(zip([n for n, _ in first_attn], _all_gather(local_weights(first_attn), "gather_weights_first")))]
    coming_mlp, first_token = _exchange_start(local_weights(first_mlp), False, "gather_weights_start_0",
                                              after=weights[0]["w_o"])

    consts = _lane_consts()
    tabs = _rope_tables(pos_col, consts)
    slopes = 2.0 ** (-8.0 * (jnp.arange(SWA_HEADS, dtype=F32) + 1.0) / SWA_HEADS)

    mem_n = _rmsnorm_fwd(mem0, 0, D_MODEL, mem_norm_g, "rmsnorm_fwd_mem")

    saved = []
    xc = x0
    for i in range(DEPTH):
        j = i // 2
        wts = weights[i]
        s = {"x_in": xc}
        token = None
        if i + 1 < DEPTH:
            coming, token = _exchange_start(local_weights(layer_names(i + 1)), False,
                                            "gather_weights_start_%d" % (i + 1),
                                            after=first_token if i == 0 else wts["w_o"])
        if i == 0:
            hn = _rmsnorm_fwd(xc, 0, D_MODEL, attn_norm_g[i], "rmsnorm_fwd")
        if i % 2 == 0:
            w_in = _mla_in_pad(_join(wts["mla_w_in"], 1))
            w_uq = _mla_uq_pad(_join(wts["mla_w_uq"], 2))
            w_kv = _join(wts["mla_w_ukv"], 2)
            proj = _mm(hn, w_in, "nn", F32, "mm_mla_in", after=token)
            cqn = _rmsnorm_fwd(proj, 0, MLA_Q_RANK, mla_q_norm_g[j], "rmsnorm_fwd_q")
            ckvn = _rmsnorm_fwd(proj, 2, MLA_KV_RANK, mla_kv_norm_g[j], "rmsnorm_fwd_kv")
            qraw = _mm(cqn, w_uq, "nn", F32, "mm_mla_uq")
            kvraw = _mm(ckvn, w_kv, "nn", F32, "mm_mla_ukv")
            q, k, v = _mla_rope_fwd(qraw, kvraw, proj, tabs)
            o, lse = _mla_attn_fwd(q, k, v)
            qoff = MLA_QOFF
            s.update(w_uq=w_uq, w_kv=w_kv, cqn=cqn, ckvn=ckvn, q=q, k=k, v=v)
        else:
            w_in = _join(wts["swa_w_in"], 2)
            proj = _mm(hn, w_in, "nn", BF16, "mm_swa_in", pairs="o", after=token)
            o, lse = _swa_attn_fwd(proj, pos_col, pos_row, slopes, swa_sinks[j])
            qoff = SWA_QOFF
        w_mem = _pad_slots(_join(wts["w_mem_kv"], 1), 1)
        w_out = _join(wts["w_o"], 1)
        w_o_mix, w_o_cross = w_out[:SWA_HEADS * HEAD_DIM], w_out[SWA_HEADS * HEAD_DIM:]
        kvmem = _mm(mem_n, w_mem, "nn", BF16, "mm_mem_kv")
        cross = _cross_attn_fwd(proj, qoff, kvmem)
        x1, hn2 = _mm(o, w_o_mix, "nn", F32, "mm_o", res=xc, pairs="a", second=(cross, w_o_cross),
                      epi="normfwd", norm=mlp_norm_g[i])
        if i == 0:
            wts.update(zip([n for n, _ in first_mlp], _exchange_wait(coming_mlp, hn2, "gather_weights_wait_0")))
        act, act2 = _mm(hn2, wts["mlp_w_up"], "nn", BF16, "mm_mlp_up", epi="relu2", b_blk="cols")
        if i + 1 < DEPTH:
            xc, hn_next = _mm(act2, wts["mlp_w_down"], "nn", F32, "mm_mlp_down", res=x1, b_blk="rows",
                              epi="normfwd", norm=attn_norm_g[i + 1])
        else:
            xc = _mm(act2, wts["mlp_w_down"], "nn", F32, "mm_mlp_down", res=x1, b_blk="rows")
        s.update(hn=hn, w_in=w_in, proj=proj, o=o, lse=lse, qoff=qoff, w_mem=w_mem, w_out=w_out,
                 kvmem=kvmem, cross=cross, x1=x1, hn2=hn2, act=act, act2=act2)
        saved.append(s)
        if i + 1 < DEPTH:
            hn = hn_next
            got = _exchange_wait(coming, xc, "gather_weights_wait_%d" % (i + 1))
            weights.append(dict(zip([n for n, _ in layer_names(i + 1)], got)))

    dx, dx_b, dg_final, loss_part = _loss_head(xc, final_norm_g, tgt)
    loss = lax.psum(loss_part[0, 0], MESH_AXES)

    gains = {n: [None] * DEPTH for n in ("attn_norm_g", "mlp_norm_g")}
    for n in ("mla_q_norm_g", "mla_kv_norm_g", "swa_sinks"):
        gains[n] = [None] * 2
    leaving = {}
    token = None
    dmem_n = None
    for i in reversed(range(DEPTH)):
        j = i // 2
        s = saved[i]
        wts = weights[i]
        out = {}
        du = _mm(dx_b, wts["mlp_w_down"], "nt", BF16, "mm_mlp_down_dx", aux=s["act"], epi="mul2aux", b_blk="rows",
                 after=token)
        out["mlp_w_down"] = _mm(s["act2"], dx_b, "tn", BF16, "mm_mlp_down_dw", o_blk="rows")
        out["mlp_w_up"] = _mm(s["hn2"], du, "tn", BF16, "mm_mlp_up_dw", o_blk="cols")
        dx1, dx1_b, dg = _mm(du, wts["mlp_w_up"], "nt", F32, "mm_mlp_up_dx", b_blk="cols",
                             epi="normbwd", norm=(s["x1"], mlp_norm_g[i], dx))
        gains["mlp_norm_g"][i] = dg[0]

        do = _mm(dx1_b, s["w_out"], "nt", BF16, "mm_o_dx", pairs="o")
        dw_o = jnp.concatenate([_mm(s["o"], dx1_b, "tn", F32, "mm_o_mix_dw", pairs="a"),
                                _mm(s["cross"], dx1_b, "tn", F32, "mm_o_cross_dw", pairs="a")], axis=0)
        out["w_o"] = _split(dw_o, 1)
        dqc, dkm, dvm = _cross_attn_bwd(s["proj"], s["qoff"], s["kvmem"], do, SWA_HEADS)
        dkvmem = jnp.concatenate([dkm, dvm], axis=1).astype(BF16)
        out["w_mem_kv"] = _split(_unpad_slots(_mm(mem_n, dkvmem, "tn", F32, "mm_mem_kv_dw"), 1), 1)
        dmem_n = _mm(dkvmem, s["w_mem"], "nt", F32, "mm_mem_kv_dx" if dmem_n is None else "mm_mem_kv_dx_acc",
                     res=dmem_n)
        leaving[(i, "main")], token = _exchange_start([out[n] for n, _ in layer_names(i)[-4:]], True,
                                                      "exchange_grads_main_start_%d" % i)

        if i % 2 == 0:
            dq, dk, dv = _mla_attn_bwd(s["q"], s["k"], s["v"], do, s["lse"], _mla_delta(s["o"], do), token)
            dqraw, dkv, dkr = _mla_rope_bwd(dq, dk, dv, tabs, consts)
            dcqn = _mm(dqraw, s["w_uq"], "nt", F32, "mm_mla_uq_dx")
            out["mla_w_uq"] = _split(_unpad_slots(_mm(s["cqn"], dqraw, "tn", F32, "mm_mla_uq_dw"), 1, MLA_QK), 2)
            dckvn = _mm(dkv, s["w_kv"], "nt", F32, "mm_mla_ukv_dx")
            out["mla_w_ukv"] = _split(_mm(s["ckvn"], dkv, "tn", F32, "mm_mla_ukv_dw"), 2)
            dcq, dg = _rmsnorm_bwd(s["proj"], 0, MLA_Q_RANK, mla_q_norm_g[j], dcqn, None, BF16, "rmsnorm_bwd_q")
            gains["mla_q_norm_g"][j] = dg[0]
            dckv, dg = _rmsnorm_bwd(s["proj"], 2, MLA_KV_RANK, mla_kv_norm_g[j], dckvn, None, BF16, "rmsnorm_bwd_kv")
            gains["mla_kv_norm_g"][j] = dg[0]
            dproj = jnp.concatenate([dcq, dkr.astype(BF16), dckv, dqc.astype(BF16)], axis=1)
            in_dx = "mm_mla_in_dx"
            out["mla_w_in"] = _split(_mla_in_unpad(_mm(s["hn"], dproj, "tn", F32, "mm_mla_in_dw")), 1)
        else:
            dq, dk, dv, dsink = _swa_attn_bwd(s["proj"], s["o"], do, s["lse"], pos_col, pos_row, slopes, swa_sinks[j],
                                              token)
            gains["swa_sinks"][j] = dsink[::8, 0]
            dproj = jnp.concatenate([dq, dk, dv, dqc], axis=1).astype(BF16)
            in_dx = "mm_swa_in_dx"
            out["swa_w_in"] = _split(_mm(s["hn"], dproj, "tn", F32, "mm_swa_in_dw", pairs="b"), 2)
        dx, dx_b, dg = _mm(dproj, s["w_in"], "nt", F32, in_dx, epi="normbwd", norm=(s["x_in"], attn_norm_g[i], dx1),
                           pairs="" if i % 2 == 0 else "a")
        gains["attn_norm_g"][i] = dg[0]

        leaving[(i, "mixer")], token = _exchange_start([out[n] for n, _ in layer_names(i)[:-4]], True,
                                                       "exchange_grads_mixer_start_%d" % i)

    _, dg_mem = _rmsnorm_bwd(mem0, 0, D_MODEL, mem_norm_g, dmem_n, None, BF16, "rmsnorm_bwd_mem")
    gains = {n: jnp.stack(g) for n, g in gains.items()}
    gains["mem_norm_g"] = dg_mem[0]
    gains["final_norm_g"] = dg_final[0]

    result = {}

    def adamw_of(names, received):
        for n in names:
            parts = [received[(n, l)] for l in range(given[n].shape[0])]
            for kind, r in enumerate(_adamw(parts, given[n], given["m_" + n], given["v_" + n], "adamw_" + n)):
                result[(kind, n)] = r

    received = {}
    for i in reversed(range(DEPTH)):
        got = _exchange_wait(leaving[(i, "main")], dx, "exchange_grads_main_wait_%d" % i)
        received.update(zip(layer_names(i)[-4:], got))
    adamw_of(("mlp_w_up", "mlp_w_down", "w_o", "w_mem_kv"), received)
    for i in reversed(range(DEPTH)):
        got = _exchange_wait(leaving[(i, "mixer")], result[(0, "w_mem_kv")], "exchange_grads_mixer_wait_%d" % i)
        received.update(zip(layer_names(i)[:-4], got))
    adamw_of(("mla_w_in", "mla_w_uq", "mla_w_ukv", "swa_w_in"), received)

    rep_shapes = [given[n].shape for n in REPLICATED]
    rep_parts = _all_gather([_pack([gains[n] for n in REPLICATED], SLOT, 8, F32)], "gather_gain_grads")[0]
    rep_packed = [_pack([given[p + n] for n in REPLICATED], SLOT, 8, F32)[None] for p in ("", "m_", "v_")]
    for kind, r in enumerate(_adamw([rep_parts], *rep_packed, "adamw_gains")):
        for n, part in zip(REPLICATED, _unpack(r[0], rep_shapes)):
            result[(kind, n)] = part

    outs = [loss, dx.reshape(1, seq, D_MODEL)]
    for kind in range(4):
        outs += [result[(kind, n)] for n in WEIGHT_ORDER]
    return tuple(outs)
```

```python
import functools

import jax
import jax.numpy as jnp
from jax import lax
from jax.experimental import pallas as pl
from jax.experimental.pallas import tpu as pltpu

F32 = jnp.float32
BF16 = jnp.bfloat16

D_MODEL = 1024
N_MEM = 256
DEPTH = 4
SLOT = 128
HEAD_DIM = 64
MLA_HEADS = 12
MLA_QK = 96
MLA_Q_RANK = 384
MLA_KV_RANK = 256
SWA_HEADS = 12
SWA_KV_HEADS = 4
SWA_GROUP = 3
MEM_HEADS = 4
WINDOW = 128
EPS = 1e-6
NEG = -1e30
ROPE_THETA = 10000.0
N_DEV = 8

ADAM_LR = 0.001
ADAM_B1 = 0.9
ADAM_B2 = 0.999
ADAM_EPS = 1e-08
ADAM_WD = 0.01
ADAM_STEP = 10

TM = 1024
TM_ROPE = 512
TQ_MLA = 1024
MLA_PACK = 4
SWA_PACK = 4
TQ_CROSS = 4096
MM_VMEM_BUDGET = 38 * 1024 * 1024
ADAM_ROWS = 256
VMEM_LIMIT = 56 * 1024 * 1024

MESH_AXES = ("x", "y", "c")

LOG2_E = 1.4426950408889634
MLA_SCALE = MLA_QK ** -0.5
MLA_Q_SCALE = MLA_SCALE * LOG2_E

MLA_QOFF = (MLA_Q_RANK + SLOT + MLA_KV_RANK) // SLOT
SWA_QOFF = SWA_HEADS + 2 * SWA_KV_HEADS

SHARDED = (
    ("mla_w_in", 1), ("mla_w_uq", 2), ("mla_w_ukv", 2), ("swa_w_in", 2),
    ("w_mem_kv", 1), ("w_o", 1), ("mlp_w_up", 2), ("mlp_w_down", 1),
)
REPLICATED = ("attn_norm_g", "mlp_norm_g", "mem_norm_g", "final_norm_g",
              "mla_q_norm_g", "mla_kv_norm_g", "swa_sinks")
WEIGHT_ORDER = ("attn_norm_g", "mlp_norm_g", "mem_norm_g", "final_norm_g", "mla_w_in",
                "mla_q_norm_g", "mla_kv_norm_g", "mla_w_uq", "mla_w_ukv", "swa_w_in",
                "swa_sinks", "w_mem_kv", "w_o", "mlp_w_up", "mlp_w_down")


def _cparams():
    return pltpu.CompilerParams(vmem_limit_bytes=VMEM_LIMIT)


_DIMS = {"nn": (((1,), (0,)), ((), ())), "nt": (((1,), (1,)), ((), ())), "tn": (((0,), (0,)), ((), ()))}


def _compact(x):
    pairs = [x[:, 2 * j * SLOT:(2 * j + 1) * SLOT] + pltpu.roll(x[:, (2 * j + 1) * SLOT:(2 * j + 2) * SLOT], HEAD_DIM, 1)
             for j in range(x.shape[1] // (2 * SLOT))]
    return pairs[0] if len(pairs) == 1 else jnp.concatenate(pairs, axis=1)


def _expand(x):
    low = lax.broadcasted_iota(jnp.int32, (x.shape[0], SLOT), 1) < HEAD_DIM
    slots = []
    for j in range(x.shape[1] // SLOT):
        pair = x[:, j * SLOT:(j + 1) * SLOT]
        slots += [jnp.where(low, pair, 0.0), pltpu.roll(jnp.where(low, 0.0, pair), HEAD_DIM, 1)]
    return jnp.concatenate(slots, axis=1)


def _mm_tiles(m, n, k, a_bytes, b_bytes, o_bytes, extra_bytes, tm_fixed, tn_fixed):
    best = None
    for tm in ([tm_fixed] if tm_fixed else [t for t in range(4096, 0, -SLOT) if m % t == 0] or [m]):
        for tn in ([tn_fixed] if tn_fixed else [t for t in range(1024, 0, -SLOT) if n % t == 0] or [n]):
            need = 2 * (tm * k * a_bytes + k * tn * b_bytes + tm * tn * (o_bytes + extra_bytes))
            need += tm * tn * 4
            if need <= MM_VMEM_BUDGET and (best is None or tm * tn > best[0] * best[1]):
                best = (tm, tn)
    assert best is not None, (m, n, k)
    return best


def _mm(a, b, mode, out_dtype, name, res=None, aux=None, epi=None, b_blk=None, o_blk=None, after=None, norm=None,
        pairs="", second=None):
    if b_blk is not None:
        nb, br, bc = b.shape
        b_shape = (nb * br, bc) if b_blk == "rows" else (br, nb * bc)
    else:
        b_shape = b.shape
    assert not pairs or (b_blk is None and o_blk is None and not ("b" in pairs and mode == "nt"))
    a_shape = (a.shape[0], a.shape[1] // 2) if "a" in pairs else a.shape
    if "b" in pairs:
        b_shape = (b_shape[0], b_shape[1] // 2)
    if mode == "nn":
        (m, k), (k2, n) = a_shape, b_shape
    elif mode == "nt":
        (m, k), (n, k2) = a_shape, b_shape
    else:
        (k, m), (k2, n) = a_shape, b_shape
    assert k == k2, (a.shape, b_shape, mode)
    assert second is None or (mode == "nn" and b_blk is None and second[0].shape[0] == m and second[1].shape[1] == n)
    k_second = 0 if second is None else second[1].shape[0]
    k_blocked = b_blk is not None and (b_blk == "rows") == (mode != "nt")
    tn_fixed = None
    if b_blk is not None and not k_blocked:
        tn_fixed = br if b_blk == "rows" else bc
    if o_blk == "cols":
        tn_fixed = n // N_DEV
    tm_fixed = m // N_DEV if o_blk == "rows" else None
    has_res, has_aux, has_norm, has_normf = res is not None, aux is not None, epi == "normbwd", epi == "normfwd"
    assert o_blk is None or not (has_res or has_aux or has_norm or has_normf)
    n_out = 2 if epi == "relu2" else 1
    if has_norm:
        tn_fixed = n
        o_bytes, extra_bytes = 4 + 2, 4 + 4
    elif has_normf:
        tn_fixed = n
        o_bytes, extra_bytes = 4 + 2, (4 if has_res else 0)
    else:
        o_bytes = n_out * jnp.dtype(out_dtype).itemsize
        extra_bytes = (4 if has_res else 0) + (aux.dtype.itemsize if has_aux else 0)
    pa, pb, po = (2 if "a" in pairs else 1), (2 if "b" in pairs else 1), (2 if "o" in pairs else 1)
    tm, tn = _mm_tiles(m, n, k + k_second, a.dtype.itemsize * (3 if pa == 2 else 1),
                       b.dtype.itemsize * (3 if pb == 2 else 1), o_bytes * po, extra_bytes, tm_fixed, tn_fixed)
    dims = _DIMS[mode]
    if mode == "tn":
        a_spec = pl.BlockSpec((k, pa * tm), lambda i, j: (0, i))
    else:
        a_spec = pl.BlockSpec((tm, pa * k), lambda i, j: (i, 0))
    if b_blk is None:
        if mode == "nt":
            b_spec = pl.BlockSpec((tn, k), lambda i, j: (j, 0))
        else:
            b_spec = pl.BlockSpec((k, pb * tn), lambda i, j: (0, j))
    elif k_blocked and mode == "nt":
        b_spec = pl.BlockSpec((N_DEV, tn, bc), lambda i, j: (0, j, 0))
    elif k_blocked:
        b_spec = pl.BlockSpec((N_DEV, br, tn), lambda i, j: (0, 0, j))
    elif mode == "nt":
        b_spec = pl.BlockSpec((None, tn, k), lambda i, j: (j, 0, 0))
    else:
        b_spec = pl.BlockSpec((None, k, tn), lambda i, j: (j, 0, 0))
    if o_blk is None:
        o_spec = pl.BlockSpec((tm, po * tn), lambda i, j: (i, j))
        o_shape = (m, po * n)
    elif o_blk == "rows":
        o_spec = pl.BlockSpec((None, tm, tn), lambda i, j: (i, 0, j))
        o_shape = (N_DEV, tm, n)
    else:
        o_spec = pl.BlockSpec((None, tm, tn), lambda i, j: (j, i, 0))
        o_shape = (N_DEV, m, tn)

    def body(*refs):
        a_ref, b_ref = refs[0], refs[1]
        pos = 2
        res_ref = aux_ref = None
        if has_res:
            res_ref = refs[pos]
            pos += 1
        if has_aux:
            aux_ref = refs[pos]
            pos += 1
        if has_norm:
            x_ref, g_ref, dres_ref = refs[pos:pos + 3]
            pos += 3
        if has_normf:
            g_ref = refs[pos]
            pos += 1
        if second is not None:
            a2_ref, b2_ref = refs[pos:pos + 2]
            pos += 2
        if after is not None:
            pos += 1
        outs = refs[pos:]
        if k_blocked and mode == "nt":
            r = None
            for d in range(N_DEV):
                part = lax.dot_general(a_ref[:, d * bc:(d + 1) * bc].astype(BF16), b_ref[d].astype(BF16), dims,
                                       preferred_element_type=F32)
                r = part if r is None else r + part
        else:
            bv = b_ref[...].reshape(k, tn) if k_blocked else b_ref[...]
            av = _compact(a_ref[...].astype(F32)) if pa == 2 else a_ref[...]
            bv = _compact(bv.astype(F32)) if pb == 2 else bv
            r = lax.dot_general(av.astype(BF16), bv.astype(BF16), dims, preferred_element_type=F32)
        if second is not None:
            av2 = _compact(a2_ref[...].astype(F32)) if pa == 2 else a2_ref[...]
            r = r + lax.dot_general(av2.astype(BF16), b2_ref[...].astype(BF16), dims, preferred_element_type=F32)
        if po == 2:
            r = _expand(r)
        if epi == "relu2":
            r = jnp.maximum(r, 0.0)
            outs[0][...] = r.astype(outs[0].dtype)
            outs[1][...] = (r * r).astype(outs[1].dtype)
        elif has_norm:
            xv = x_ref[...]
            rs = lax.rsqrt(jnp.mean(xv * xv, axis=1, keepdims=True) + EPS)
            xh = xv * rs
            dxh = r * g_ref[...]
            dx = rs * (dxh - xh * jnp.mean(dxh * xh, axis=1, keepdims=True)) + dres_ref[...]
            outs[0][...] = dx
            outs[1][...] = dx.astype(BF16)

            @pl.when(pl.program_id(0) == 0)
            def _():
                outs[2][...] = jnp.zeros_like(outs[2])

            outs[2][...] += jnp.sum(r * xh, axis=0, keepdims=True)
        else:
            if epi == "mul2aux":
                r = r * (2.0 * aux_ref[...].astype(F32))
            if has_res:
                r = r + res_ref[...]
            outs[0][...] = r.astype(outs[0].dtype)
            if has_normf:
                rs = lax.rsqrt(jnp.mean(r * r, axis=1, keepdims=True) + EPS)
                outs[1][...] = (r * rs * g_ref[...]).astype(BF16)

    in_specs = [a_spec, b_spec]
    args = [a, b]
    if has_res:
        in_specs.append(o_spec)
        args.append(res)
    if has_aux:
        in_specs.append(o_spec)
        args.append(aux)
    vec_spec = pl.BlockSpec((1, n), lambda i, j: (0, 0))
    if has_norm:
        in_specs += [o_spec, vec_spec, o_spec]
        args += [norm[0], norm[1].reshape(1, n), norm[2]]
    if has_normf:
        in_specs.append(vec_spec)
        args.append(norm.reshape(1, n))
    if second is not None:
        in_specs += [pl.BlockSpec((tm, pa * k_second), lambda i, j: (i, 0)),
                     pl.BlockSpec((k_second, tn), lambda i, j: (0, j))]
        args += list(second)
    if after is not None:
        in_specs.append(pl.BlockSpec(memory_space=pl.ANY))
        args.append(after)
    if has_norm:
        out_specs = [o_spec, o_spec, vec_spec]
        out_shape = [jax.ShapeDtypeStruct(o_shape, F32), jax.ShapeDtypeStruct(o_shape, BF16),
                     jax.ShapeDtypeStruct((1, n), F32)]
    elif has_normf:
        out_specs = [o_spec, o_spec]
        out_shape = [jax.ShapeDtypeStruct(o_shape, out_dtype), jax.ShapeDtypeStruct(o_shape, BF16)]
    else:
        out_specs = [o_spec] * n_out
        out_shape = [jax.ShapeDtypeStruct(o_shape, out_dtype)] * n_out
    out = pl.pallas_call(
        body, name=name, grid=(m // tm, n // tn),
        in_specs=in_specs, out_specs=out_specs, out_shape=out_shape, compiler_params=_cparams(),
    )(*args)
    return out if len(out) > 1 else out[0]


def _rmsnorm_fwd(xarr, colblk, width, g, name, after=None):
    rows = xarr.shape[0]
    tm = min(TM, rows)

    def body(x_ref, g_ref, *rest):
        y_ref = rest[-1]
        x = x_ref[...].astype(F32)
        r = lax.rsqrt(jnp.mean(x * x, axis=1, keepdims=True) + EPS)
        y_ref[...] = (x * r * g_ref[...]).astype(y_ref.dtype)

    in_specs = [pl.BlockSpec((tm, width), lambda i: (i, colblk)), pl.BlockSpec((1, width), lambda i: (0, 0))]
    args = [xarr, g.reshape(1, width)]
    if after is not None:
        in_specs.append(pl.BlockSpec(memory_space=pl.ANY))
        args.append(after)
    return pl.pallas_call(
        body, name=name, grid=(rows // tm,), in_specs=in_specs,
        out_specs=pl.BlockSpec((tm, width), lambda i: (i, 0)),
        out_shape=jax.ShapeDtypeStruct((rows, width), BF16), compiler_params=_cparams(),
    )(*args)


def _rmsnorm_bwd(xarr, colblk, width, g, dy, dres, out_dtype, name):
    rows = xarr.shape[0]
    tm = min(TM, rows)
    has_res = dres is not None

    def body(*refs):
        x_ref, g_ref, dy_ref = refs[0], refs[1], refs[2]
        dres_ref = refs[3] if has_res else None
        dx_ref, dg_ref = refs[-2], refs[-1]
        x = x_ref[...].astype(F32)
        dyv = dy_ref[...].astype(F32)
        r = lax.rsqrt(jnp.mean(x * x, axis=1, keepdims=True) + EPS)
        xh = x * r
        dxh = dyv * g_ref[...]
        dx = r * (dxh - xh * jnp.mean(dxh * xh, axis=1, keepdims=True))
        if has_res:
            dx = dx + dres_ref[...]
        dx_ref[...] = dx.astype(dx_ref.dtype)

        @pl.when(pl.program_id(0) == 0)
        def _():
            dg_ref[...] = jnp.zeros_like(dg_ref)

        dg_ref[...] += jnp.sum(dyv * xh, axis=0, keepdims=True)

    row_spec = pl.BlockSpec((tm, width), lambda i: (i, 0))
    vec_spec = pl.BlockSpec((1, width), lambda i: (0, 0))
    in_specs = [pl.BlockSpec((tm, width), lambda i: (i, colblk)), vec_spec, row_spec]
    args = [xarr, g.reshape(1, width), dy]
    if has_res:
        in_specs.append(row_spec)
        args.append(dres)
    return pl.pallas_call(
        body, name=name, grid=(rows // tm,), in_specs=in_specs, out_specs=[row_spec, vec_spec],
        out_shape=[jax.ShapeDtypeStruct((rows, width), out_dtype), jax.ShapeDtypeStruct((1, width), F32)],
        compiler_params=_cparams(),
    )(*args)


def _loss_head(x, g, tgt):
    rows, width = x.shape
    tm = min(TM, rows)

    def body(x_ref, g_ref, t_ref, dx_ref, dxb_ref, dg_ref, loss_ref):
        xv = x_ref[...]
        gv = g_ref[...]
        r = lax.rsqrt(jnp.mean(xv * xv, axis=1, keepdims=True) + EPS)
        xh = xv * r
        err = xh * gv - t_ref[...]
        part = 0.5 * jnp.sum(jnp.mean(err * err, axis=1, keepdims=True), axis=0, keepdims=True)
        dyv = err * (1.0 / width)
        dxh = dyv * gv
        dxv = r * (dxh - xh * jnp.mean(dxh * xh, axis=1, keepdims=True))
        dx_ref[...] = dxv
        dxb_ref[...] = dxv.astype(BF16)

        @pl.when(pl.program_id(0) == 0)
        def _():
            dg_ref[...] = jnp.zeros_like(dg_ref)
            loss_ref[...] = jnp.zeros_like(loss_ref)

        dg_ref[...] += jnp.sum(dyv * xh, axis=0, keepdims=True)
        loss_ref[...] += jnp.broadcast_to(part, loss_ref.shape)

    row_spec = pl.BlockSpec((tm, width), lambda i: (i, 0))
    vec_spec = pl.BlockSpec((1, width), lambda i: (0, 0))
    return pl.pallas_call(
        body, name="loss_head", grid=(rows // tm,), in_specs=[row_spec, vec_spec, row_spec],
        out_specs=[row_spec, row_spec, vec_spec, pl.BlockSpec((1, SLOT), lambda i: (0, 0))],
        out_shape=[jax.ShapeDtypeStruct((rows, width), F32), jax.ShapeDtypeStruct((rows, width), BF16),
                   jax.ShapeDtypeStruct((1, width), F32), jax.ShapeDtypeStruct((1, SLOT), F32)],
        compiler_params=_cparams(),
    )(x, g.reshape(1, width), tgt)


def _lane_consts():
    half = 16
    inv = ROPE_THETA ** (-(jnp.arange(half, dtype=F32) * 2.0) / 32)
    lane = jnp.arange(SLOT)
    first = (lane >= 64) & (lane < 80)
    second = (lane >= 80) & (lane < 96)
    inv_lane = jnp.where(first | second, inv[(lane - 64) % half], 0.0)
    rows = [inv_lane, (lane < 64).astype(F32), first.astype(F32), second.astype(F32)]
    rows += [jnp.zeros((SLOT,), F32)] * 4
    return jnp.stack(rows).astype(F32)


def _rope_tables(pos_col, consts):
    rows = pos_col.shape[0]
    tm = min(TM, rows)

    def body(p_ref, k_ref, c_ref, s1_ref, s2_ref):
        ang = p_ref[...] * k_ref[0:1, :]
        cos, sin = jnp.cos(ang), jnp.sin(ang)
        first, second = k_ref[2:3, :], k_ref[3:4, :]
        c_ref[...] = k_ref[1:2, :] + (first + second) * cos
        s1_ref[...] = -first * sin
        s2_ref[...] = second * sin

    spec = pl.BlockSpec((tm, SLOT), lambda i: (i, 0))
    shp = jax.ShapeDtypeStruct((rows, SLOT), F32)
    return pl.pallas_call(
        body, name="rope_tables", grid=(rows // tm,),
        in_specs=[pl.BlockSpec((tm, 1), lambda i: (i, 0)), pl.BlockSpec((8, SLOT), lambda i: (0, 0))],
        out_specs=[spec, spec, spec], out_shape=[shp, shp, shp], compiler_params=_cparams(),
    )(pos_col, consts)


def _rot(xv, c, s1, s2):
    return xv * c + pltpu.roll(xv, SLOT - 16, 1) * s1 + pltpu.roll(xv, 16, 1) * s2


def _rot_t(dy, c, s1, s2):
    return dy * c + pltpu.roll(dy * s1, 16, 1) + pltpu.roll(dy * s2, SLOT - 16, 1)


def _mla_rope_fwd(qraw, kvraw, proj, tabs):
    rows = qraw.shape[0]
    tm = min(TM_ROPE, rows)
    hw = MLA_HEADS * SLOT

    def body(q_ref, kv_ref, kr_ref, c_ref, s1_ref, s2_ref, qo, ko, vo):
        c, s1, s2 = c_ref[...], s1_ref[...], s2_ref[...]
        kr = _rot(kr_ref[...], c, s1, s2)
        low = lax.broadcasted_iota(jnp.int32, (tm, SLOT), 1) < HEAD_DIM
        for h in range(MLA_HEADS):
            sl = slice(h * SLOT, (h + 1) * SLOT)
            qo[:, sl] = (_rot(q_ref[:, sl], c, s1, s2) * MLA_Q_SCALE).astype(BF16)
            kvh = kv_ref[:, sl]
            ko[:, sl] = (jnp.where(low, kvh, 0.0) + kr).astype(BF16)
            vo[:, sl] = pltpu.roll(jnp.where(low, 0.0, kvh), HEAD_DIM, 1).astype(BF16)

    tab = pl.BlockSpec((tm, SLOT), lambda i: (i, 0))
    wide = pl.BlockSpec((tm, hw), lambda i: (i, 0))
    shp = jax.ShapeDtypeStruct((rows, hw), BF16)
    return pl.pallas_call(
        body, name="mla_rope_fwd", grid=(rows // tm,),
        in_specs=[wide, wide, pl.BlockSpec((tm, SLOT), lambda i: (i, 3)),
                  tab, tab, tab],
        out_specs=[wide, wide, wide], out_shape=[shp, shp, shp], compiler_params=_cparams(),
    )(qraw, kvraw, proj, *tabs)


def _mla_rope_bwd(dq, dk, dv, tabs, consts):
    rows = dq.shape[0]
    tm = min(TM_ROPE, rows)
    hw = MLA_HEADS * SLOT

    def body(dq_ref, dk_ref, dv_ref, c_ref, s1_ref, s2_ref, k_ref, dqo, dkvo, dkro):
        c, s1, s2 = c_ref[...], s1_ref[...], s2_ref[...]
        ksum = jnp.zeros((tm, SLOT), F32)
        low = lax.broadcasted_iota(jnp.int32, (tm, SLOT), 1) < HEAD_DIM
        for h in range(MLA_HEADS):
            sl = slice(h * SLOT, (h + 1) * SLOT)
            dqo[:, sl] = _rot_t(dq_ref[:, sl], c, s1, s2).astype(BF16)
            dkh = dk_ref[:, sl]
            ksum = ksum + dkh
            dvh = pltpu.roll(jnp.where(low, dv_ref[:, sl], 0.0), HEAD_DIM, 1)
            dkvo[:, sl] = (jnp.where(low, dkh, 0.0) + dvh).astype(BF16)
        dkro[...] = _rot_t(ksum, c, s1, s2) * (k_ref[2:3, :] + k_ref[3:4, :])

    tab = pl.BlockSpec((tm, SLOT), lambda i: (i, 0))
    wide = pl.BlockSpec((tm, hw), lambda i: (i, 0))
    return pl.pallas_call(
        body, name="mla_rope_bwd", grid=(rows // tm,),
        in_specs=[wide, wide, wide, tab, tab, tab, pl.BlockSpec((8, SLOT), lambda i: (0, 0))],
        out_specs=[wide, wide, tab],
        out_shape=[jax.ShapeDtypeStruct((rows, hw), BF16), jax.ShapeDtypeStruct((rows, hw), BF16),
                   jax.ShapeDtypeStruct((rows, SLOT), F32)],
        compiler_params=_cparams(),
    )(dq, dk, dv, *tabs, consts)


def _nt(a, b):
    return lax.dot_general(a, b, _DIMS["nt"], preferred_element_type=F32)


def _tn(a, b):
    return lax.dot_general(a, b, _DIMS["tn"], preferred_element_type=F32)


def _nn(a, b):
    return lax.dot_general(a, b, _DIMS["nn"], preferred_element_type=F32)


def _mla_attn_fwd(q, k, v):
    rows = q.shape[0]
    t = min(TQ_MLA, rows)
    nt = rows // t
    wide = MLA_PACK * SLOT

    def body(q_ref, k_ref, v_ref, o_ref, lse_ref, m_sc, l_sc, acc_sc):
        i, j = pl.program_id(1), pl.program_id(2)

        @pl.when(j == 0)
        def _():
            m_sc[...] = jnp.full_like(m_sc, NEG)
            l_sc[...] = jnp.zeros_like(l_sc)
            acc_sc[...] = jnp.zeros_like(acc_sc)

        def step(diagonal):
            for hh in range(MLA_PACK):
                sl = slice(hh * SLOT, (hh + 1) * SLOT)
                s = _nt(k_ref[:, sl], q_ref[:, sl])
                if diagonal:
                    key = lax.broadcasted_iota(jnp.int32, (t, t), 0)
                    s = jnp.where(key <= lax.broadcasted_iota(jnp.int32, (t, t), 1), s, NEG)
                m_prev = m_sc[hh]
                m_new = jnp.maximum(m_prev, jnp.max(s, axis=0, keepdims=True))
                p = jnp.exp2(s - m_new)
                alpha = jnp.exp2(m_prev - m_new)
                l_new = alpha * l_sc[hh] + jnp.sum(p, axis=0, keepdims=True)
                acc = alpha * acc_sc[hh] + _tn(v_ref[:, sl], p.astype(BF16))
                if diagonal:
                    o_ref[:, sl] = (acc / l_new).T.astype(o_ref.dtype)
                    lse_ref[hh:hh + 1, :] = m_new + jnp.log(l_new) * LOG2_E
                else:
                    m_sc[hh] = m_new
                    l_sc[hh] = l_new
                    acc_sc[hh] = acc

        @pl.when(j < i)
        def _():
            step(False)

        @pl.when(j == i)
        def _():
            lse_ref[...] = jnp.zeros_like(lse_ref)
            step(True)

    q_spec = pl.BlockSpec((t, wide), lambda h, i, j: (i, h))
    kv_spec = pl.BlockSpec((t, wide), lambda h, i, j: (jnp.minimum(j, i), h))
    return pl.pallas_call(
        body, name="mla_attn_fwd", grid=(MLA_HEADS // MLA_PACK, nt, nt),
        in_specs=[q_spec, kv_spec, kv_spec],
        out_specs=[q_spec, pl.BlockSpec((None, 8, t), lambda h, i, j: (h, 0, i))],
        out_shape=[jax.ShapeDtypeStruct(q.shape, BF16),
                   jax.ShapeDtypeStruct((MLA_HEADS // MLA_PACK, 8, rows), F32)],
        scratch_shapes=[pltpu.VMEM((MLA_PACK, 1, t), F32), pltpu.VMEM((MLA_PACK, 1, t), F32),
                        pltpu.VMEM((MLA_PACK, SLOT, t), F32)],
        compiler_params=_cparams(),
    )(q, k, v)


def _mla_delta(o, do):
    rows = o.shape[0]
    t = rows
    wide = MLA_PACK * SLOT

    def body(o_ref, do_ref, d_ref):
        d_ref[...] = jnp.zeros_like(d_ref)
        ones = jnp.ones((8, SLOT), BF16)
        for hh in range(MLA_PACK):
            sl = slice(hh * SLOT, (hh + 1) * SLOT)
            prod = do_ref[:, sl].astype(F32) * o_ref[:, sl].astype(F32)
            high = prod.astype(BF16)
            low = (prod - high.astype(F32)).astype(BF16)
            d_ref[hh:hh + 1, :] = (_nt(ones, high) + _nt(ones, low))[0:1, :]

    spec = pl.BlockSpec((t, wide), lambda h, i: (i, h))
    return pl.pallas_call(
        body, name="mla_delta", grid=(MLA_HEADS // MLA_PACK, rows // t), in_specs=[spec, spec],
        out_specs=pl.BlockSpec((None, 8, t), lambda h, i: (h, 0, i)),
        out_shape=jax.ShapeDtypeStruct((MLA_HEADS // MLA_PACK, 8, rows), F32), compiler_params=_cparams(),
    )(o, do)


def _mla_attn_bwd(q, k, v, do, lse, delta, after):
    rows = q.shape[0]
    t = min(TQ_MLA, rows)
    nt = rows // t
    wide = MLA_PACK * SLOT

    def body(q_ref, k_ref, v_ref, do_ref, lse_ref, delta_ref, after_ref, dq_ref, dk_ref, dv_ref, dk_sc, dv_sc):
        j, i = pl.program_id(1), pl.program_id(2)

        @pl.when((j == 0) & (i == 0))
        def _():
            dq_ref[...] = jnp.zeros_like(dq_ref)

        @pl.when(i == 0)
        def _():
            dk_sc[...] = jnp.zeros_like(dk_sc)
            dv_sc[...] = jnp.zeros_like(dv_sc)

        def chunk(hh, rows, keys, masked):
            sl = slice(hh * SLOT, (hh + 1) * SLOT)
            n_rows = rows.stop - rows.start
            qv, kv, dov = q_ref[rows, sl], k_ref[keys, sl], do_ref[rows, sl]
            s = _nt(kv, qv)
            if masked:
                shp = (keys.stop - keys.start, n_rows)
                s = jnp.where(keys.start + lax.broadcasted_iota(jnp.int32, shp, 0)
                              <= rows.start + lax.broadcasted_iota(jnp.int32, shp, 1), s, NEG)
            p = jnp.exp2(s - lse_ref[hh:hh + 1, rows])
            dp = _nt(v_ref[keys, sl], dov)
            ds = (p * (dp - delta_ref[hh:hh + 1, rows])).astype(BF16)
            dv_sc[keys, sl] += _nn(p.astype(BF16), dov)
            dk_sc[keys, sl] += _nn(ds, qv)
            r0 = pl.multiple_of(i * t + rows.start, n_rows)
            dq_ref[pl.ds(r0, n_rows), sl] += _tn(ds, kv) * MLA_SCALE

        @pl.when(i > j)
        def _():
            for hh in range(MLA_PACK):
                chunk(hh, slice(0, t), slice(0, t), False)

        @pl.when(i == j)
        def _():
            for hh in range(MLA_PACK):
                chunk(hh, slice(0, t), slice(0, t // 2), True)
                chunk(hh, slice(t // 2, t), slice(t // 2, t), True)

        @pl.when(i == nt - 1)
        def _():
            dk_ref[...] = dk_sc[...] * (1.0 / LOG2_E)
            dv_ref[...] = dv_sc[...]

    q_spec = pl.BlockSpec((t, wide), lambda h, j, i: (jnp.maximum(i, j), h))
    kv_spec = pl.BlockSpec((t, wide), lambda h, j, i: (j, h))
    row_spec = pl.BlockSpec((None, 8, t), lambda h, j, i: (h, 0, jnp.maximum(i, j)))
    head_spec = pl.BlockSpec((rows, wide), lambda h, j, i: (0, h))
    shp = jax.ShapeDtypeStruct(q.shape, F32)
    return pl.pallas_call(
        body, name="mla_attn_bwd", grid=(MLA_HEADS // MLA_PACK, nt, nt),
        in_specs=[q_spec, kv_spec, kv_spec, q_spec, row_spec, row_spec, pl.BlockSpec(memory_space=pl.ANY)],
        out_specs=[head_spec, kv_spec, kv_spec], out_shape=[shp, shp, shp],
        scratch_shapes=[pltpu.VMEM((t, wide), F32), pltpu.VMEM((t, wide), F32)],
        compiler_params=_cparams(),
    )(q, k, v, do, lse, delta, after)


def _swa_specs(t):
    def prev(i):
        return jnp.maximum(i - 1, 0)
    kw = SWA_PACK * SLOT
    k0, v0 = SWA_HEADS // SWA_PACK, (SWA_HEADS + SWA_KV_HEADS) // SWA_PACK
    q3 = pl.BlockSpec((t, SWA_PACK * SWA_GROUP * SLOT), lambda h, i: (i, h))
    kp = pl.BlockSpec((t, kw), lambda h, i: (prev(i), k0 + h))
    kc = pl.BlockSpec((t, kw), lambda h, i: (i, k0 + h))
    vp = pl.BlockSpec((t, kw), lambda h, i: (prev(i), v0 + h))
    vc = pl.BlockSpec((t, kw), lambda h, i: (i, v0 + h))
    pcol = pl.BlockSpec((t, 1), lambda h, i: (i, 0))
    prow_p = pl.BlockSpec((1, t), lambda h, i: (0, prev(i)))
    prow_c = pl.BlockSpec((1, t), lambda h, i: (0, i))
    return [q3, kp, kc, vp, vc, pcol, prow_p, prow_c]


def _stack(ref, first):
    return jnp.concatenate([ref[:, (first + g) * SLOT:(first + g + 1) * SLOT] for g in range(SWA_GROUP)], axis=0)


def _swa_logits(q3, kp, kc, pq, pkp, pkc, slope_ref, kvh, i, t):
    r = lax.broadcasted_iota(jnp.int32, (t, t), 0)
    c = lax.broadcasted_iota(jnp.int32, (t, t), 1)
    ok_c = c <= r
    ok_p = (c - r) > jnp.where(i > 0, 0, t)
    dist_p, dist_c = pq - pkp, pq - pkc
    s_p3 = _nt(q3, kp) * (HEAD_DIM ** -0.5)
    s_c3 = _nt(q3, kc) * (HEAD_DIM ** -0.5)
    out = []
    for g in range(SWA_GROUP):
        slope = slope_ref[kvh * SWA_GROUP + g]
        rows = slice(g * t, (g + 1) * t)
        out.append((jnp.where(ok_p, s_p3[rows] - slope * dist_p, NEG),
                    jnp.where(ok_c, s_c3[rows] - slope * dist_c, NEG)))
    return out


def _swa_attn_fwd(proj, pos_col, pos_row, slopes, sinks):
    rows = proj.shape[0]
    t = WINDOW
    hw = SWA_HEADS * SLOT

    def body(slope_ref, sink_ref, q_ref, kp_ref, kc_ref, vp_ref, vc_ref, pq_ref, pkp_ref, pkc_ref, o_ref, lse_ref):
        i = pl.program_id(1)
        for kv in range(SWA_PACK):
            kvh = pl.program_id(0) * SWA_PACK + kv
            ksl = slice(kv * SLOT, (kv + 1) * SLOT)
            logits = _swa_logits(_stack(q_ref, kv * SWA_GROUP), kp_ref[:, ksl], kc_ref[:, ksl], pq_ref[...],
                                 pkp_ref[...], pkc_ref[...], slope_ref, kvh, i, t)
            e_p, e_c, norm = [], [], []
            for g, (s_p, s_c) in enumerate(logits):
                sl = slice((kv * SWA_GROUP + g) * SLOT, (kv * SWA_GROUP + g + 1) * SLOT)
                sink = sink_ref[kvh * SWA_GROUP + g]
                m = jnp.maximum(jnp.maximum(jnp.max(s_p, axis=1, keepdims=True),
                                            jnp.max(s_c, axis=1, keepdims=True)), sink)
                ep, ec = jnp.exp(s_p - m), jnp.exp(s_c - m)
                l = jnp.sum(ep, axis=1, keepdims=True) + jnp.sum(ec, axis=1, keepdims=True) + jnp.exp(sink - m)
                e_p.append(ep.astype(BF16))
                e_c.append(ec.astype(BF16))
                norm.append(l)
                lse_ref[:, sl] = jnp.broadcast_to(m + jnp.log(l), (t, SLOT))
            acc = (_nn(jnp.concatenate(e_p, axis=0), vp_ref[:, ksl])
                   + _nn(jnp.concatenate(e_c, axis=0), vc_ref[:, ksl]))
            for g in range(SWA_GROUP):
                sl = slice((kv * SWA_GROUP + g) * SLOT, (kv * SWA_GROUP + g + 1) * SLOT)
                o_ref[:, sl] = (acc[g * t:(g + 1) * t] / norm[g]).astype(o_ref.dtype)

    smem = pl.BlockSpec(memory_space=pltpu.SMEM)
    out_spec = pl.BlockSpec((t, SWA_PACK * SWA_GROUP * SLOT), lambda h, i: (i, h))
    return pl.pallas_call(
        body, name="swa_attn_fwd", grid=(SWA_KV_HEADS // SWA_PACK, rows // t),
        in_specs=[smem, smem] + _swa_specs(t), out_specs=[out_spec, out_spec],
        out_shape=[jax.ShapeDtypeStruct((rows, hw), BF16), jax.ShapeDtypeStruct((rows, hw), F32)],
        compiler_params=_cparams(),
    )(slopes, sinks, proj, proj, proj, proj, proj, pos_col, pos_row, pos_row)


def _swa_attn_bwd(proj, o, do, lse, pos_col, pos_row, slopes, sinks, after):
    rows = proj.shape[0]
    t = WINDOW
    hw = SWA_HEADS * SLOT
    scale = HEAD_DIM ** -0.5

    def body(slope_ref, sink_ref, q_ref, kp_ref, kc_ref, vp_ref, vc_ref, pq_ref, pkp_ref, pkc_ref,
             o_ref, do_ref, lse_ref, after_ref, dq_ref, dk_out, dv_out, dsink_ref, dk_ref, dv_ref):
        i = pl.program_id(1)

        @pl.when(i == 0)
        def _():
            dk_ref[...] = jnp.zeros_like(dk_ref)
            dv_ref[...] = jnp.zeros_like(dv_ref)
            dsink_ref[...] = jnp.zeros_like(dsink_ref)

        r_c = pl.multiple_of(i * t, t)
        r_p = pl.multiple_of(jnp.maximum(i - 1, 0) * t, t)
        for kv in range(SWA_PACK):
            kvh = pl.program_id(0) * SWA_PACK + kv
            ksl = slice(kv * SLOT, (kv + 1) * SLOT)
            q3, do3 = _stack(q_ref, kv * SWA_GROUP), _stack(do_ref, kv * SWA_GROUP)
            logits = _swa_logits(q3, kp_ref[:, ksl], kc_ref[:, ksl], pq_ref[...], pkp_ref[...], pkc_ref[...],
                                 slope_ref, kvh, i, t)
            dp_p3, dp_c3 = _nt(do3, vp_ref[:, ksl]), _nt(do3, vc_ref[:, ksl])
            p_p, p_c, ds_p, ds_c = [], [], [], []
            for g, (s_p, s_c) in enumerate(logits):
                head = kv * SWA_GROUP + g
                sl = slice(head * SLOT, (head + 1) * SLOT)
                rws = slice(g * t, (g + 1) * t)
                lse_g = lse_ref[:, head * SLOT:head * SLOT + 1]
                pp, pc = jnp.exp(s_p - lse_g), jnp.exp(s_c - lse_g)
                delta = jnp.sum(do_ref[:, sl].astype(F32) * o_ref[:, sl].astype(F32), axis=1, keepdims=True)
                p_p.append(pp.astype(BF16))
                p_c.append(pc.astype(BF16))
                ds_p.append((pp * (dp_p3[rws] - delta)).astype(BF16))
                ds_c.append((pc * (dp_c3[rws] - delta)).astype(BF16))
                sink = sink_ref[kvh * SWA_GROUP + g]
                dsink = -jnp.sum(jnp.exp(sink - lse_g) * delta, axis=0, keepdims=True)
                dsink_ref[head * 8:(head + 1) * 8, :] += jnp.broadcast_to(dsink, (8, SLOT))
            p_p3, p_c3 = jnp.concatenate(p_p, axis=0), jnp.concatenate(p_c, axis=0)
            ds_p3, ds_c3 = jnp.concatenate(ds_p, axis=0), jnp.concatenate(ds_c, axis=0)
            dq3 = (_nn(ds_p3, kp_ref[:, ksl]) + _nn(ds_c3, kc_ref[:, ksl])) * scale
            for g in range(SWA_GROUP):
                head = kv * SWA_GROUP + g
                dq_ref[:, head * SLOT:(head + 1) * SLOT] = dq3[g * t:(g + 1) * t].astype(dq_ref.dtype)
            dk_ref[pl.ds(r_c, t), ksl] += _tn(ds_c3, q3) * scale
            dv_ref[pl.ds(r_c, t), ksl] += _tn(p_c3, do3)
            dk_ref[pl.ds(r_p, t), ksl] += _tn(ds_p3, q3) * scale
            dv_ref[pl.ds(r_p, t), ksl] += _tn(p_p3, do3)

        @pl.when(i == pl.num_programs(1) - 1)
        def _():
            dk_out[...] = dk_ref[...].astype(dk_out.dtype)
            dv_out[...] = dv_ref[...].astype(dv_out.dtype)

    smem = pl.BlockSpec(memory_space=pltpu.SMEM)
    qlike = pl.BlockSpec((t, SWA_PACK * SWA_GROUP * SLOT), lambda h, i: (i, h))
    kv_out = pl.BlockSpec((rows, SWA_PACK * SLOT), lambda h, i: (0, h))
    return pl.pallas_call(
        body, name="swa_attn_bwd", grid=(SWA_KV_HEADS // SWA_PACK, rows // t),
        in_specs=[smem, smem] + _swa_specs(t) + [qlike, qlike, qlike, pl.BlockSpec(memory_space=pl.ANY)],
        out_specs=[qlike, kv_out, kv_out,
                   pl.BlockSpec((SWA_PACK * SWA_GROUP * 8, SLOT), lambda h, i: (h, 0))],
        out_shape=[jax.ShapeDtypeStruct((rows, hw), BF16), jax.ShapeDtypeStruct((rows, SWA_KV_HEADS * SLOT), BF16),
                   jax.ShapeDtypeStruct((rows, SWA_KV_HEADS * SLOT), BF16),
                   jax.ShapeDtypeStruct((SWA_HEADS * 8, SLOT), F32)],
        scratch_shapes=[pltpu.VMEM((rows, SWA_PACK * SLOT), F32), pltpu.VMEM((rows, SWA_PACK * SLOT), F32)],
        compiler_params=_cparams(),
    )(slopes, sinks, proj, proj, proj, proj, proj, pos_col, pos_row, pos_row, o, do, lse, after)


def _cross_attn_fwd(proj, qoff, kvmem):
    rows = proj.shape[0]
    t = min(TQ_CROSS, rows)

    def body(q_ref, k_ref, v_ref, o_ref):
        s = _nt(k_ref[...], q_ref[...].astype(BF16)) * (HEAD_DIM ** -0.5)
        e = jnp.exp(s - jnp.max(s, axis=0, keepdims=True))
        p = e / jnp.sum(e, axis=0, keepdims=True)
        o_ref[...] = _tn(v_ref[...], p.astype(BF16)).T.astype(o_ref.dtype)

    return pl.pallas_call(
        body, name="cross_attn_fwd", grid=(rows // t, MEM_HEADS),
        in_specs=[pl.BlockSpec((t, SLOT), lambda i, h: (i, qoff + h)),
                  pl.BlockSpec((N_MEM, SLOT), lambda i, h: (0, h)),
                  pl.BlockSpec((N_MEM, SLOT), lambda i, h: (0, MEM_HEADS + h))],
        out_specs=pl.BlockSpec((t, SLOT), lambda i, h: (i, h)),
        out_shape=jax.ShapeDtypeStruct((rows, MEM_HEADS * SLOT), BF16), compiler_params=_cparams(),
    )(proj, kvmem, kvmem)


def _cross_attn_bwd(proj, qoff, kvmem, do, do_off):
    rows = proj.shape[0]
    t = min(TQ_CROSS, rows)
    scale = HEAD_DIM ** -0.5

    def body(q_ref, k_ref, v_ref, do_ref, dq_ref, dk_ref, dv_ref):
        @pl.when(pl.program_id(1) == 0)
        def _():
            dk_ref[...] = jnp.zeros_like(dk_ref)
            dv_ref[...] = jnp.zeros_like(dv_ref)

        qv, kv, dov = q_ref[...].astype(BF16), k_ref[...], do_ref[...]
        s = _nt(kv, qv) * scale
        e = jnp.exp(s - jnp.max(s, axis=0, keepdims=True))
        p = e / jnp.sum(e, axis=0, keepdims=True)
        dp = _nt(v_ref[...], dov)
        ds = (p * (dp - jnp.sum(p * dp, axis=0, keepdims=True))).astype(BF16)
        dq_ref[...] = (_tn(ds, kv) * scale).astype(dq_ref.dtype)
        dk_ref[...] += _nn(ds, qv) * scale
        dv_ref[...] += _nn(p.astype(BF16), dov)

    mem_out = pl.BlockSpec((N_MEM, SLOT), lambda h, i: (0, h))
    return pl.pallas_call(
        body, name="cross_attn_bwd", grid=(MEM_HEADS, rows // t),
        in_specs=[pl.BlockSpec((t, SLOT), lambda h, i: (i, qoff + h)),
                  pl.BlockSpec((N_MEM, SLOT), lambda h, i: (0, h)),
                  pl.BlockSpec((N_MEM, SLOT), lambda h, i: (0, MEM_HEADS + h)),
                  pl.BlockSpec((t, SLOT), lambda h, i: (i, do_off + h))],
        out_specs=[pl.BlockSpec((t, SLOT), lambda h, i: (i, h)), mem_out, mem_out],
        out_shape=[jax.ShapeDtypeStruct((rows, MEM_HEADS * SLOT), BF16),
                   jax.ShapeDtypeStruct((N_MEM, MEM_HEADS * SLOT), F32),
                   jax.ShapeDtypeStruct((N_MEM, MEM_HEADS * SLOT), F32)],
        compiler_params=_cparams(),
    )(proj, kvmem, kvmem, do)


def _place():
    return lax.axis_index("x"), lax.axis_index("y"), lax.axis_index("c")


def _flip(v, bit):
    return 1 - v if bit else v


def _all_gather(blocks, name):
    nb = len(blocks)

    def body(*refs):
        x_refs, out_refs = refs[:nb], refs[nb:2 * nb]
        send_sems, recv_sems, local_sems = refs[2 * nb:]
        x, y, c = _place()
        me, sibling = (x, y, c), (x, y, 1 - c)
        chips = [(1 - x, y), (x, 1 - y), (1 - x, 1 - y)]

        def copy(b, k, blk, to, from_input=False):
            slot = out_refs[b].at[4 * blk[0] + 2 * blk[1] + blk[2]]
            return pltpu.make_async_remote_copy(
                src_ref=x_refs[b] if from_input else slot, dst_ref=slot,
                send_sem=send_sems.at[b, k], recv_sem=recv_sems.at[b, k],
                device_id=to, device_id_type=pl.DeviceIdType.MESH)

        mine = [pltpu.make_async_copy(x_refs[b], out_refs[b].at[4 * x + 2 * y + c], local_sems.at[b])
                for b in range(nb)]
        for cp in mine:
            cp.start()
        first = []
        for b in range(nb):
            first.append(copy(b, 0, me, sibling, from_input=True))
            first += [copy(b, 1 + n, me, (*chip, c), from_input=True) for n, chip in enumerate(chips)]
        for cp in first:
            cp.start()
        passed = []
        for n, chip in enumerate(chips):
            for b in range(nb):
                copy(b, 1 + n, (*chip, c), me).wait_recv()
                passed.append(copy(b, 4 + n, (*chip, c), sibling))
                passed[-1].start()
        for b in range(nb):
            copy(b, 0, sibling, me).wait_recv()
            for n, chip in enumerate(chips):
                copy(b, 4 + n, (*chip, 1 - c), me).wait_recv()
        for cp in first + passed:
            cp.wait_send()
        for cp in mine:
            cp.wait()

    any_spec = pl.BlockSpec(memory_space=pl.ANY)
    return pl.pallas_call(
        body, name=name, in_specs=[any_spec] * nb, out_specs=[any_spec] * nb,
        out_shape=[jax.ShapeDtypeStruct((N_DEV,) + blk.shape, blk.dtype) for blk in blocks],
        scratch_shapes=[pltpu.SemaphoreType.DMA((nb, 7)), pltpu.SemaphoreType.DMA((nb, 7)),
                        pltpu.SemaphoreType.DMA((nb,))],
    )(*blocks)


def _peers(x, y, c):
    out = []
    for n in range(1, N_DEV):
        peer = (_flip(x, n & 4), _flip(y, n & 2), _flip(c, n & 1))
        out.append((n - 1, peer, 4 * peer[0] + 2 * peer[1] + peer[2]))
    return out


_HBM = pl.BlockSpec(memory_space=pltpu.HBM)
_SEM = pl.BlockSpec(memory_space=pltpu.SEMAPHORE)


def _exchange_start(srcs, scatter, name, after=None):
    ns = len(srcs)
    lands = [lax.empty(s.shape if scatter else (N_DEV,) + s.shape, s.dtype) for s in srcs]

    def body(*refs):
        src_refs, land_refs = refs[:ns], refs[ns:2 * ns]
        pos = 2 * ns + (1 if after is not None else 0)
        send_sems, recv_sems, token = refs[pos], refs[pos + 1], refs[-1]
        x, y, c = _place()
        my_idx = 4 * x + 2 * y + c
        for col, peer, peer_idx in _peers(x, y, c):
            for b in range(ns):
                pltpu.make_async_remote_copy(
                    src_ref=src_refs[b].at[peer_idx] if scatter else src_refs[b], dst_ref=land_refs[b].at[my_idx],
                    send_sem=send_sems.at[b * (N_DEV - 1) + col], recv_sem=recv_sems.at[b * (N_DEV - 1) + col],
                    device_id=peer, device_id_type=pl.DeviceIdType.MESH).start()
        token[...] = jnp.zeros_like(token)

    args = [pltpu.with_memory_space_constraint(a, pltpu.HBM) for a in list(srcs) + lands]
    in_specs = [_HBM] * (2 * ns)
    if after is not None:
        args.append(after)
        in_specs.append(pl.BlockSpec(memory_space=pl.ANY))
    out = pl.pallas_call(
        body, name=name, in_specs=in_specs,
        out_specs=[_SEM, _SEM] + [_HBM] * (2 * ns) + [pl.BlockSpec(memory_space=pltpu.VMEM)],
        out_shape=[pltpu.SemaphoreType.DMA((ns * (N_DEV - 1),)), pltpu.SemaphoreType.DMA((ns * (N_DEV - 1),))]
        + [pltpu.HBM(a.shape, a.dtype) for a in list(srcs) + lands] + [jax.ShapeDtypeStruct((8, SLOT), F32)],
        input_output_aliases={k: 2 + k for k in range(2 * ns)},
        compiler_params=pltpu.CompilerParams(has_side_effects=pltpu.SideEffectType.DATAFLOW_SIDE_EFFECTING),
    )(*args)
    return (out[0], out[1], out[2:2 + ns], out[2 + ns:2 + 2 * ns], scatter), out[-1]


def _exchange_wait(handle, after, name):
    send_sems, recv_sems, srcs, lands, scatter = handle
    ns = len(srcs)

    def body(*refs):
        src_refs, land_refs = refs[:ns], refs[ns:2 * ns]
        send_ref, recv_ref = refs[2 * ns], refs[2 * ns + 1]
        x, y, c = _place()
        for col, peer, peer_idx in _peers(x, y, c):
            for b in range(ns):
                copy = pltpu.make_async_remote_copy(
                    src_ref=src_refs[b].at[peer_idx] if scatter else src_refs[b], dst_ref=land_refs[b].at[peer_idx],
                    send_sem=send_ref.at[b * (N_DEV - 1) + col], recv_sem=recv_ref.at[b * (N_DEV - 1) + col],
                    device_id=peer, device_id_type=pl.DeviceIdType.MESH)
                copy.wait_send()
                copy.wait_recv()

    out = pl.pallas_call(
        body, name=name, in_specs=[_HBM] * (2 * ns) + [_SEM, _SEM, pl.BlockSpec(memory_space=pl.ANY)],
        out_specs=[_HBM] * (2 * ns),
        out_shape=[pltpu.HBM(a.shape, a.dtype) for a in list(srcs) + list(lands)],
        input_output_aliases={k: k for k in range(2 * ns)},
        compiler_params=pltpu.CompilerParams(has_side_effects=pltpu.SideEffectType.DATAFLOW_SIDE_EFFECTING),
    )(*srcs, *lands, send_sems, recv_sems, after)
    my_idx = 4 * lax.axis_index("x") + 2 * lax.axis_index("y") + lax.axis_index("c")
    landed = []
    for src, land in zip(out[:ns], out[ns:]):
        own = lax.dynamic_index_in_dim(src, my_idx, 0, keepdims=True) if scatter else src[None]
        landed.append(lax.dynamic_update_index_in_dim(land, own, my_idx, 0))
    return landed


def _adamw(parts, w, m, v, name):
    lyr, rows, cols = w.shape
    assert len(parts) == lyr
    tr = ADAM_ROWS if cols > 512 else 2 * ADAM_ROWS
    while rows % tr:
        tr //= 2
    tr = min(tr, rows)

    def body(*refs):
        p_refs = refs[:lyr]
        w_ref, m_ref, v_ref, g_out, d_out, m_out, v_out = refs[lyr:]
        for k in range(lyr):
            @pl.when(pl.program_id(0) == k)
            def _(p_ref=p_refs[k]):
                g = p_ref[0].astype(F32)
                for s in range(1, N_DEV):
                    g = g + p_ref[s].astype(F32)
                m2 = ADAM_B1 * m_ref[...] + (1.0 - ADAM_B1) * g
                v2 = ADAM_B2 * v_ref[...] + (1.0 - ADAM_B2) * (g * g)
                m_hat = m2 / (1.0 - ADAM_B1 ** ADAM_STEP)
                v_hat = v2 / (1.0 - ADAM_B2 ** ADAM_STEP)
                g_out[...] = g
                d_out[...] = -ADAM_LR * (m_hat / (jnp.sqrt(v_hat) + ADAM_EPS) + ADAM_WD * w_ref[...])
                m_out[...] = m2
                v_out[...] = v2

    def part_spec(k):
        return pl.BlockSpec((N_DEV, tr, cols), lambda l, i: (0, jnp.where(l == k, i, 0), 0))

    spec = pl.BlockSpec((None, tr, cols), lambda l, i: (l, i, 0))
    shp = jax.ShapeDtypeStruct((lyr, rows, cols), F32)
    return pl.pallas_call(
        body, name=name, grid=(lyr, rows // tr),
        in_specs=[part_spec(k) for k in range(lyr)] + [spec, spec, spec],
        out_specs=[spec] * 4, out_shape=[shp] * 4, compiler_params=_cparams(),
    )(*parts, w, m, v)


def _pack(arrays, lanes, row_mult, dtype):
    flat = jnp.concatenate([a.reshape(-1).astype(dtype) for a in arrays])
    unit = lanes * row_mult
    total = -(-flat.shape[0] // unit) * unit
    return jnp.pad(flat, (0, total - flat.shape[0])).reshape(total // lanes, lanes)


def _unpack(packed, shapes):
    flat = packed.reshape(-1)
    out, off = [], 0
    for shp in shapes:
        n = 1
        for d in shp:
            n *= d
        out.append(flat[off:off + n].reshape(shp))
        off += n
    return out


def _pad_slots(w, axis):
    axis = axis % w.ndim
    n = w.shape[axis] // HEAD_DIM
    shp = w.shape[:axis] + (n, HEAD_DIM) + w.shape[axis + 1:]
    pad = [(0, 0)] * (w.ndim + 1)
    pad[axis + 1] = (0, SLOT - HEAD_DIM)
    return jnp.pad(w.reshape(shp), pad).reshape(w.shape[:axis] + (n * SLOT,) + w.shape[axis + 1:])


def _unpad_slots(w, axis, keep=HEAD_DIM):
    axis = axis % w.ndim
    n = w.shape[axis] // SLOT
    shp = w.shape[:axis] + (n, SLOT) + w.shape[axis + 1:]
    idx = [slice(None)] * (w.ndim + 1)
    idx[axis + 1] = slice(0, keep)
    return w.reshape(shp)[tuple(idx)].reshape(w.shape[:axis] + (n * keep,) + w.shape[axis + 1:])


def _mla_in_pad(w):
    z = functools.partial(jnp.zeros, dtype=w.dtype)
    rows = w.shape[0]
    return jnp.concatenate([w[:, :384], z((rows, 64)), w[:, 640:672], z((rows, 32)), w[:, 384:640],
                            _pad_slots(w[:, 672:], 1)], axis=1)


def _mla_in_unpad(d):
    return jnp.concatenate([d[:, :384], d[:, 512:768], d[:, 448:480], _unpad_slots(d[:, 768:], 1)], axis=1)


def _mla_uq_pad(w):
    return jnp.pad(w.reshape(w.shape[0], MLA_HEADS, MLA_QK), ((0, 0), (0, 0), (0, SLOT - MLA_QK))).reshape(
        w.shape[0], MLA_HEADS * SLOT)


def _join(gathered, axis):
    nd, a, b = gathered.shape
    if axis == 1:
        return gathered.reshape(nd * a, b)
    return gathered.transpose(1, 0, 2).reshape(a, nd * b)


def _split(full, axis):
    r, c = full.shape
    if axis == 1:
        return full.reshape(N_DEV, r // N_DEV, c).astype(BF16)
    return full.reshape(r, N_DEV, c // N_DEV).transpose(1, 0, 2).astype(BF16)


def kernel(x, mem, positions, attn_norm_g, mlp_norm_g, mem_norm_g, final_norm_g, mla_w_in, mla_q_norm_g, mla_kv_norm_g, mla_w_uq, mla_w_ukv, swa_w_in, swa_sinks, w_mem_kv, w_o, mlp_w_up, mlp_w_down, loss_target, m_attn_norm_g, m_mlp_norm_g, m_mem_norm_g, m_final_norm_g, m_mla_w_in, m_mla_q_norm_g, m_mla_kv_norm_g, m_mla_w_uq, m_mla_w_ukv, m_swa_w_in, m_swa_sinks, m_w_mem_kv, m_w_o, m_mlp_w_up, m_mlp_w_down, v_attn_norm_g, v_mlp_norm_g, v_mem_norm_g, v_final_norm_g, v_mla_w_in, v_mla_q_norm_g, v_mla_kv_norm_g, v_mla_w_uq, v_mla_w_ukv, v_swa_w_in, v_swa_sinks, v_w_mem_kv, v_w_o, v_mlp_w_up, v_mlp_w_down):
    given = dict(locals())
    seq = x.shape[1]
    x0 = x.reshape(seq, D_MODEL)
    tgt = loss_target.reshape(seq, D_MODEL)
    mem0 = mem.reshape(N_MEM, D_MODEL)
    pos = positions.reshape(seq).astype(F32)
    pos_col, pos_row = pos.reshape(seq, 1), pos.reshape(1, seq)

    def layer_names(i):
        mixer = ("mla_w_in", "mla_w_uq", "mla_w_ukv") if i % 2 == 0 else ("swa_w_in",)
        return [(n, i // 2) for n in mixer] + [(n, i) for n in ("w_mem_kv", "w_o", "mlp_w_up", "mlp_w_down")]

    def local_weights(names):
        return [given[n][l].astype(BF16) for n, l in names]

    first_attn, first_mlp = layer_names(0)[:-2], layer_names(0)[-2:]
    weights = [dict(zip([n for n, _ in first_attn], _all_gather(local_weights(first_attn), "gather_weights_first")))]
    coming_mlp, first_token = _exchange_start(local_weights(first_mlp), False, "gather_weights_start_0",
                                              after=weights[0]["w_o"])

    consts = _lane_consts()
    tabs = _rope_tables(pos_col, consts)
    slopes = 2.0 ** (-8.0 * (jnp.arange(SWA_HEADS, dtype=F32) + 1.0) / SWA_HEADS)

    mem_n = _rmsnorm_fwd(mem0, 0, D_MODEL, mem_norm_g, "rmsnorm_fwd_mem")

    saved = []
    xc = x0
    for i in range(DEPTH):
        j = i // 2
        wts = weights[i]
        s = {"x_in": xc}
        token = None
        if i + 1 < DEPTH:
            coming, token = _exchange_start(local_weights(layer_names(i + 1)), False,
                                            "gather_weights_start_%d" % (i + 1),
                                            after=first_token if i == 0 else wts["w_o"])
        if i == 0:
            hn = _rmsnorm_fwd(xc, 0, D_MODEL, attn_norm_g[i], "rmsnorm_fwd")
        if i % 2 == 0:
            w_in = _mla_in_pad(_join(wts["mla_w_in"], 1))
            w_uq = _mla_uq_pad(_join(wts["mla_w_uq"], 2))
            w_kv = _join(wts["mla_w_ukv"], 2)
            proj = _mm(hn, w_in, "nn", F32, "mm_mla_in", after=token)
            cqn = _rmsnorm_fwd(proj, 0, MLA_Q_RANK, mla_q_norm_g[j], "rmsnorm_fwd_q")
            ckvn = _rmsnorm_fwd(proj, 2, MLA_KV_RANK, mla_kv_norm_g[j], "rmsnorm_fwd_kv")
            qraw = _mm(cqn, w_uq, "nn", F32, "mm_mla_uq")
            kvraw = _mm(ckvn, w_kv, "nn", F32, "mm_mla_ukv")
            q, k, v = _mla_rope_fwd(qraw, kvraw, proj, tabs)
            o, lse = _mla_attn_fwd(q, k, v)
            qoff = MLA_QOFF
            s.update(w_uq=w_uq, w_kv=w_kv, cqn=cqn, ckvn=ckvn, q=q, k=k, v=v)
        else:
            w_in = _join(wts["swa_w_in"], 2)
            proj = _mm(hn, w_in, "nn", BF16, "mm_swa_in", pairs="o", after=token)
            o, lse = _swa_attn_fwd(proj, pos_col, pos_row, slopes, swa_sinks[j])
            qoff = SWA_QOFF
        w_mem = _pad_slots(_join(wts["w_mem_kv"], 1), 1)
        w_out = _join(wts["w_o"], 1)
        w_o_mix, w_o_cross = w_out[:SWA_HEADS * HEAD_DIM], w_out[SWA_HEADS * HEAD_DIM:]
        kvmem = _mm(mem_n, w_mem, "nn", BF16, "mm_mem_kv")
        cross = _cross_attn_fwd(proj, qoff, kvmem)
        x1, hn2 = _mm(o, w_o_mix, "nn", F32, "mm_o", res=xc, pairs="a", second=(cross, w_o_cross),
                      epi="normfwd", norm=mlp_norm_g[i])
        if i == 0:
            wts.update(zip([n for n, _ in first_mlp], _exchange_wait(coming_mlp, hn2, "gather_weights_wait_0")))
        act, act2 = _mm(hn2, wts["mlp_w_up"], "nn", BF16, "mm_mlp_up", epi="relu2", b_blk="cols")
        if i + 1 < DEPTH:
            xc, hn_next = _mm(act2, wts["mlp_w_down"], "nn", F32, "mm_mlp_down", res=x1, b_blk="rows",
                              epi="normfwd", norm=attn_norm_g[i + 1])
        else:
            xc = _mm(act2, wts["mlp_w_down"], "nn", F32, "mm_mlp_down", res=x1, b_blk="rows")
        s.update(hn=hn, w_in=w_in, proj=proj, o=o, lse=lse, qoff=qoff, w_mem=w_mem, w_out=w_out,
                 kvmem=kvmem, cross=cross, x1=x1, hn2=hn2, act=act, act2=act2)
        saved.append(s)
        if i + 1 < DEPTH:
            hn = hn_next
            got = _exchange_wait(coming, xc, "gather_weights_wait_%d" % (i + 1))
            weights.append(dict(zip([n for n, _ in layer_names(i + 1)], got)))

    dx, dx_b, dg_final, loss_part = _loss_head(xc, final_norm_g, tgt)
    loss = lax.psum(loss_part[0, 0], MESH_AXES)

    gains = {n: [None] * DEPTH for n in ("attn_norm_g", "mlp_norm_g")}
    for n in ("mla_q_norm_g", "mla_kv_norm_g", "swa_sinks"):
        gains[n] = [None] * 2
    leaving = {}
    token = None
    dmem_n = None
    for i in reversed(range(DEPTH)):
        j = i // 2
        s = saved[i]
        wts = weights[i]
        out = {}
        du = _mm(dx_b, wts["mlp_w_down"], "nt", BF16, "mm_mlp_down_dx", aux=s["act"], epi="mul2aux", b_blk="rows",
                 after=token)
        out["mlp_w_down"] = _mm(s["act2"], dx_b, "tn", BF16, "mm_mlp_down_dw", o_blk="rows")
        out["mlp_w_up"] = _mm(s["hn2"], du, "tn", BF16, "mm_mlp_up_dw", o_blk="cols")
        dx1, dx1_b, dg = _mm(du, wts["mlp_w_up"], "nt", F32, "mm_mlp_up_dx", b_blk="cols",
                             epi="normbwd", norm=(s["x1"], mlp_norm_g[i], dx))
        gains["mlp_norm_g"][i] = dg[0]

        do = _mm(dx1_b, s["w_out"], "nt", BF16, "mm_o_dx", pairs="o")
        dw_o = jnp.concatenate([_mm(s["o"], dx1_b, "tn", F32, "mm_o_mix_dw", pairs="a"),
                                _mm(s["cross"], dx1_b, "tn", F32, "mm_o_cross_dw", pairs="a")], axis=0)
        out["w_o"] = _split(dw_o, 1)
        dqc, dkm, dvm = _cross_attn_bwd(s["proj"], s["qoff"], s["kvmem"], do, SWA_HEADS)
        dkvmem = jnp.concatenate([dkm, dvm], axis=1).astype(BF16)
        out["w_mem_kv"] = _split(_unpad_slots(_mm(mem_n, dkvmem, "tn", F32, "mm_mem_kv_dw"), 1), 1)
        dmem_n = _mm(dkvmem, s["w_mem"], "nt", F32, "mm_mem_kv_dx" if dmem_n is None else "mm_mem_kv_dx_acc",
                     res=dmem_n)
        leaving[(i, "main")], token = _exchange_start([out[n] for n, _ in layer_names(i)[-4:]], True,
                                                      "exchange_grads_main_start_%d" % i)

        if i % 2 == 0:
            dq, dk, dv = _mla_attn_bwd(s["q"], s["k"], s["v"], do, s["lse"], _mla_delta(s["o"], do), token)
            dqraw, dkv, dkr = _mla_rope_bwd(dq, dk, dv, tabs, consts)
            dcqn = _mm(dqraw, s["w_uq"], "nt", F32, "mm_mla_uq_dx")
            out["mla_w_uq"] = _split(_unpad_slots(_mm(s["cqn"], dqraw, "tn", F32, "mm_mla_uq_dw"), 1, MLA_QK), 2)
            dckvn = _mm(dkv, s["w_kv"], "nt", F32, "mm_mla_ukv_dx")
            out["mla_w_ukv"] = _split(_mm(s["ckvn"], dkv, "tn", F32, "mm_mla_ukv_dw"), 2)
            dcq, dg = _rmsnorm_bwd(s["proj"], 0, MLA_Q_RANK, mla_q_norm_g[j], dcqn, None, BF16, "rmsnorm_bwd_q")
            gains["mla_q_norm_g"][j] = dg[0]
            dckv, dg = _rmsnorm_bwd(s["proj"], 2, MLA_KV_RANK, mla_kv_norm_g[j], dckvn, None, BF16, "rmsnorm_bwd_kv")
            gains["mla_kv_norm_g"][j] = dg[0]
            dproj = jnp.concatenate([dcq, dkr.astype(BF16), dckv, dqc.astype(BF16)], axis=1)
            in_dx = "mm_mla_in_dx"
            out["mla_w_in"] = _split(_mla_in_unpad(_mm(s["hn"], dproj, "tn", F32, "mm_mla_in_dw")), 1)
        else:
            dq, dk, dv, dsink = _swa_attn_bwd(s["proj"], s["o"], do, s["lse"], pos_col, pos_row, slopes, swa_sinks[j],
                                              token)
            gains["swa_sinks"][j] = dsink[::8, 0]
            dproj = jnp.concatenate([dq, dk, dv, dqc], axis=1).astype(BF16)
            in_dx = "mm_swa_in_dx"
            out["swa_w_in"] = _split(_mm(s["hn"], dproj, "tn", F32, "mm_swa_in_dw", pairs="b"), 2)
        dx, dx_b, dg = _mm(dproj, s["w_in"], "nt", F32, in_dx, epi="normbwd", norm=(s["x_in"], attn_norm_g[i], dx1),
                           pairs="" if i % 2 == 0 else "a")
        gains["attn_norm_g"][i] = dg[0]

        leaving[(i, "mixer")], token = _exchange_start([out[n] for n, _ in layer_names(i)[:-4]], True,
                                                       "exchange_grads_mixer_start_%d" % i)

    _, dg_mem = _rmsnorm_bwd(mem0, 0, D_MODEL, mem_norm_g, dmem_n, None, BF16, "rmsnorm_bwd_mem")
    gains = {n: jnp.stack(g) for n, g in gains.items()}
    gains["mem_norm_g"] = dg_mem[0]
    gains["final_norm_g"] = dg_final[0]

    result = {}

    def adamw_of(names, received):
        for n in names:
            parts = [received[(n, l)] for l in range(given[n].shape[0])]
            for kind, r in enumerate(_adamw(parts, given[n], given["m_" + n], given["v_" + n], "adamw_" + n)):
                result[(kind, n)] = r

    received = {}
    for i in reversed(range(DEPTH)):
        got = _exchange_wait(leaving[(i, "main")], dx, "exchange_grads_main_wait_%d" % i)
        received.update(zip(layer_names(i)[-4:], got))
    adamw_of(("mlp_w_up", "mlp_w_down", "w_o", "w_mem_kv"), received)
    for i in reversed(range(DEPTH)):
        got = _exchange_wait(leaving[(i, "mixer")], result[(0, "w_mem_kv")], "exchange_grads_mixer_wait_%d" % i)
        received.update(zip(layer_names(i)[:-4], got))
    adamw_of(("mla_w_in", "mla_w_uq", "mla_w_ukv", "swa_w_in"), received)

    rep_shapes = [given[n].shape for n in REPLICATED]
    rep_parts = _all_gather([_pack([gains[n] for n in REPLICATED], SLOT, 8, F32)], "gather_gain_grads")[0]
    rep_packed = [_pack([given[p + n] for n in REPLICATED], SLOT, 8, F32)[None] for p in ("", "m_", "v_")]
    for kind, r in enumerate(_adamw([rep_parts], *rep_packed, "adamw_gains")):
        for n, part in zip(REPLICATED, _unpack(r[0], rep_shapes)):
            result[(kind, n)] = part

    outs = [loss, dx.reshape(1, seq, D_MODEL)]
    for kind in range(4):
        outs += [result[(kind, n)] for n in WEIGHT_ORDER]
    return tuple(outs)
```

```python
import functools

import jax
import jax.numpy as jnp
from jax import lax
from jax.experimental import pallas as pl
from jax.experimental.pallas import tpu as pltpu

F32 = jnp.float32
BF16 = jnp.bfloat16

D_MODEL = 1024
N_MEM = 256
DEPTH = 4
SLOT = 128
HEAD_DIM = 64
MLA_HEADS = 12
MLA_QK = 96
MLA_Q_RANK = 384
MLA_KV_RANK = 256
SWA_HEADS = 12
SWA_KV_HEADS = 4
SWA_GROUP = 3
MEM_HEADS = 4
WINDOW = 128
EPS = 1e-6
NEG = -1e30
ROPE_THETA = 10000.0
N_DEV = 8

ADAM_LR = 0.001
ADAM_B1 = 0.9
ADAM_B2 = 0.999
ADAM_EPS = 1e-08
ADAM_WD = 0.01
ADAM_STEP = 10

TM = 1024
TM_ROPE = 512
TQ_MLA = 1024
MLA_PACK = 4
SWA_PACK = 4
TQ_CROSS = 4096
MM_VMEM_BUDGET = 38 * 1024 * 1024
ADAM_ROWS = 256
VMEM_LIMIT = 56 * 1024 * 1024

MESH_AXES = ("x", "y", "c")

LOG2_E = 1.4426950408889634
MLA_SCALE = MLA_QK ** -0.5
MLA_Q_SCALE = MLA_SCALE * LOG2_E

MLA_QOFF = (MLA_Q_RANK + SLOT + MLA_KV_RANK) // SLOT
SWA_QOFF = SWA_HEADS + 2 * SWA_KV_HEADS

SHARDED = (
    ("mla_w_in", 1), ("mla_w_uq", 2), ("mla_w_ukv", 2), ("swa_w_in", 2),
    ("w_mem_kv", 1), ("w_o", 1), ("mlp_w_up", 2), ("mlp_w_down", 1),
)
REPLICATED = ("attn_norm_g", "mlp_norm_g", "mem_norm_g", "final_norm_g",
              "mla_q_norm_g", "mla_kv_norm_g", "swa_sinks")
WEIGHT_ORDER = ("attn_norm_g", "mlp_norm_g", "mem_norm_g", "final_norm_g", "mla_w_in",
                "mla_q_norm_g", "mla_kv_norm_g", "mla_w_uq", "mla_w_ukv", "swa_w_in",
                "swa_sinks", "w_mem_kv", "w_o", "mlp_w_up", "mlp_w_down")


def _cparams():
    return pltpu.CompilerParams(vmem_limit_bytes=VMEM_LIMIT)


_DIMS = {"nn": (((1,), (0,)), ((), ())), "nt": (((1,), (1,)), ((), ())), "tn": (((0,), (0,)), ((), ()))}


def _compact(x):
    pairs = [x[:, 2 * j * SLOT:(2 * j + 1) * SLOT] + pltpu.roll(x[:, (2 * j + 1) * SLOT:(2 * j + 2) * SLOT], HEAD_DIM, 1)
             for j in range(x.shape[1] // (2 * SLOT))]
    return pairs[0] if len(pairs) == 1 else jnp.concatenate(pairs, axis=1)


def _expand(x):
    low = lax.broadcasted_iota(jnp.int32, (x.shape[0], SLOT), 1) < HEAD_DIM
    slots = []
    for j in range(x.shape[1] // SLOT):
        pair = x[:, j * SLOT:(j + 1) * SLOT]
        slots += [jnp.where(low, pair, 0.0), pltpu.roll(jnp.where(low, 0.0, pair), HEAD_DIM, 1)]
    return jnp.concatenate(slots, axis=1)


def _mm_tiles(m, n, k, a_bytes, b_bytes, o_bytes, extra_bytes, tm_fixed, tn_fixed):
    best = None
    for tm in ([tm_fixed] if tm_fixed else [t for t in range(4096, 0, -SLOT) if m % t == 0] or [m]):
        for tn in ([tn_fixed] if tn_fixed else [t for t in range(1024, 0, -SLOT) if n % t == 0] or [n]):
            need = 2 * (tm * k * a_bytes + k * tn * b_bytes + tm * tn * (o_bytes + extra_bytes))
            need += tm * tn * 4
            if need <= MM_VMEM_BUDGET and (best is None or tm * tn > best[0] * best[1]):
                best = (tm, tn)
    assert best is not None, (m, n, k)
    return best


def _mm(a, b, mode, out_dtype, name, res=None, aux=None, epi=None, b_blk=None, o_blk=None, after=None, norm=None,
        pairs="", second=None):
    if b_blk is not None:
        nb, br, bc = b.shape
        b_shape = (nb * br, bc) if b_blk == "rows" else (br, nb * bc)
    else:
        b_shape = b.shape
    assert not pairs or (b_blk is None and o_blk is None and not ("b" in pairs and mode == "nt"))
    a_shape = (a.shape[0], a.shape[1] // 2) if "a" in pairs else a.shape
    if "b" in pairs:
        b_shape = (b_shape[0], b_shape[1] // 2)
    if mode == "nn":
        (m, k), (k2, n) = a_shape, b_shape
    elif mode == "nt":
        (m, k), (n, k2) = a_shape, b_shape
    else:
        (k, m), (k2, n) = a_shape, b_shape
    assert k == k2, (a.shape, b_shape, mode)
    assert second is None or (mode == "nn" and b_blk is None and second[0].shape[0] == m and second[1].shape[1] == n)
    k_second = 0 if second is None else second[1].shape[0]
    k_blocked = b_blk is not None and (b_blk == "rows") == (mode != "nt")
    tn_fixed = None
    if b_blk is not None and not k_blocked:
        tn_fixed = br if b_blk == "rows" else bc
    if o_blk == "cols":
        tn_fixed = n // N_DEV
    tm_fixed = m // N_DEV if o_blk == "rows" else None
    has_res, has_aux, has_norm, has_normf = res is not None, aux is not None, epi == "normbwd", epi == "normfwd"
    assert o_blk is None or not (has_res or has_aux or has_norm or has_normf)
    n_out = 2 if epi == "relu2" else 1
    if has_norm:
        tn_fixed = n
        o_bytes, extra_bytes = 4 + 2, 4 + 4
    elif has_normf:
        tn_fixed = n
        o_bytes, extra_bytes = 4 + 2, (4 if has_res else 0)
    else:
        o_bytes = n_out * jnp.dtype(out_dtype).itemsize
        extra_bytes = (4 if has_res else 0) + (aux.dtype.itemsize if has_aux else 0)
    pa, pb, po = (2 if "a" in pairs else 1), (2 if "b" in pairs else 1), (2 if "o" in pairs else 1)
    tm, tn = _mm_tiles(m, n, k + k_second, a.dtype.itemsize * (3 if pa == 2 else 1),
                       b.dtype.itemsize * (3 if pb == 2 else 1), o_bytes * po, extra_bytes, tm_fixed, tn_fixed)
    dims = _DIMS[mode]
    if mode == "tn":
        a_spec = pl.BlockSpec((k, pa * tm), lambda i, j: (0, i))
    else:
        a_spec = pl.BlockSpec((tm, pa * k), lambda i, j: (i, 0))
    if b_blk is None:
        if mode == "nt":
            b_spec = pl.BlockSpec((tn, k), lambda i, j: (j, 0))
        else:
            b_spec = pl.BlockSpec((k, pb * tn), lambda i, j: (0, j))
    elif k_blocked and mode == "nt":
        b_spec = pl.BlockSpec((N_DEV, tn, bc), lambda i, j: (0, j, 0))
    elif k_blocked:
        b_spec = pl.BlockSpec((N_DEV, br, tn), lambda i, j: (0, 0, j))
    elif mode == "nt":
        b_spec = pl.BlockSpec((None, tn, k), lambda i, j: (j, 0, 0))
    else:
        b_spec = pl.BlockSpec((None, k, tn), lambda i, j: (j, 0, 0))
    if o_blk is None:
        o_spec = pl.BlockSpec((tm, po * tn), lambda i, j: (i, j))
        o_shape = (m, po * n)
    elif o_blk == "rows":
        o_spec = pl.BlockSpec((None, tm, tn), lambda i, j: (i, 0, j))
        o_shape = (N_DEV, tm, n)
    else:
        o_spec = pl.BlockSpec((None, tm, tn), lambda i, j: (j, i, 0))
        o_shape = (N_DEV, m, tn)

    def body(*refs):
        a_ref, b_ref = refs[0], refs[1]
        pos = 2
        res_ref = aux_ref = None
        if has_res:
            res_ref = refs[pos]
            pos += 1
        if has_aux:
            aux_ref = refs[pos]
            pos += 1
        if has_norm:
            x_ref, g_ref, dres_ref = refs[pos:pos + 3]
            pos += 3
        if has_normf:
            g_ref = refs[pos]
            pos += 1
        if second is not None:
            a2_ref, b2_ref = refs[pos:pos + 2]
            pos += 2
        if after is not None:
            pos += 1
        outs = refs[pos:]
        if k_blocked and mode == "nt":
            r = None
            for d in range(N_DEV):
                part = lax.dot_general(a_ref[:, d * bc:(d + 1) * bc].astype(BF16), b_ref[d].astype(BF16), dims,
                                       preferred_element_type=F32)
                r = part if r is None else r + part
        else:
            bv = b_ref[...].reshape(k, tn) if k_blocked else b_ref[...]
            av = _compact(a_ref[...].astype(F32)) if pa == 2 else a_ref[...]
            bv = _compact(bv.astype(F32)) if pb == 2 else bv
            r = lax.dot_general(av.astype(BF16), bv.astype(BF16), dims, preferred_element_type=F32)
        if second is not None:
            av2 = _compact(a2_ref[...].astype(F32)) if pa == 2 else a2_ref[...]
            r = r + lax.dot_general(av2.astype(BF16), b2_ref[...].astype(BF16), dims, preferred_element_type=F32)
        if po == 2:
            r = _expand(r)
        if epi == "relu2":
            r = jnp.maximum(r, 0.0)
            outs[0][...] = r.astype(outs[0].dtype)
            outs[1][...] = (r * r).astype(outs[1].dtype)
        elif has_norm:
            xv = x_ref[...]
            rs = lax.rsqrt(jnp.mean(xv * xv, axis=1, keepdims=True) + EPS)
            xh = xv * rs
            dxh = r * g_ref[...]
            dx = rs * (dxh - xh * jnp.mean(dxh * xh, axis=1, keepdims=True)) + dres_ref[...]
            outs[0][...] = dx
            outs[1][...] = dx.astype(BF16)

            @pl.when(pl.program_id(0) == 0)
            def _():
                outs[2][...] = jnp.zeros_like(outs[2])

            outs[2][...] += jnp.sum(r * xh, axis=0, keepdims=True)
        else:
            if epi == "mul2aux":
                r = r * (2.0 * aux_ref[...].astype(F32))
            if has_res:
                r = r + res_ref[...]
            outs[0][...] = r.astype(outs[0].dtype)
            if has_normf:
                rs = lax.rsqrt(jnp.mean(r * r, axis=1, keepdims=True) + EPS)
                outs[1][...] = (r * rs * g_ref[...]).astype(BF16)

    in_specs = [a_spec, b_spec]
    args = [a, b]
    if has_res:
        in_specs.append(o_spec)
        args.append(res)
    if has_aux:
        in_specs.append(o_spec)
        args.append(aux)
    vec_spec = pl.BlockSpec((1, n), lambda i, j: (0, 0))
    if has_norm:
        in_specs += [o_spec, vec_spec, o_spec]
        args += [norm[0], norm[1].reshape(1, n), norm[2]]
    if has_normf:
        in_specs.append(vec_spec)
        args.append(norm.reshape(1, n))
    if second is not None:
        in_specs += [pl.BlockSpec((tm, pa * k_second), lambda i, j: (i, 0)),
                     pl.BlockSpec((k_second, tn), lambda i, j: (0, j))]
        args += list(second)
    if after is not None:
        in_specs.append(pl.BlockSpec(memory_space=pl.ANY))
        args.append(after)
    if has_norm:
        out_specs = [o_spec, o_spec, vec_spec]
        out_shape = [jax.ShapeDtypeStruct(o_shape, F32), jax.ShapeDtypeStruct(o_shape, BF16),
                     jax.ShapeDtypeStruct((1, n), F32)]
    elif has_normf:
        out_specs = [o_spec, o_spec]
        out_shape = [jax.ShapeDtypeStruct(o_shape, out_dtype), jax.ShapeDtypeStruct(o_shape, BF16)]
    else:
        out_specs = [o_spec] * n_out
        out_shape = [jax.ShapeDtypeStruct(o_shape, out_dtype)] * n_out
    out = pl.pallas_call(
        body, name=name, grid=(m // tm, n // tn),
        in_specs=in_specs, out_specs=out_specs, out_shape=out_shape, compiler_params=_cparams(),
    )(*args)
    return out if len(out) > 1 else out[0]


def _rmsnorm_fwd(xarr, colblk, width, g, name, after=None):
    rows = xarr.shape[0]
    tm = min(TM, rows)

    def body(x_ref, g_ref, *rest):
        y_ref = rest[-1]
        x = x_ref[...].astype(F32)
        r = lax.rsqrt(jnp.mean(x * x, axis=1, keepdims=True) + EPS)
        y_ref[...] = (x * r * g_ref[...]).astype(y_ref.dtype)

    in_specs = [pl.BlockSpec((tm, width), lambda i: (i, colblk)), pl.BlockSpec((1, width), lambda i: (0, 0))]
    args = [xarr, g.reshape(1, width)]
    if after is not None:
        in_specs.append(pl.BlockSpec(memory_space=pl.ANY))
        args.append(after)
    return pl.pallas_call(
        body, name=name, grid=(rows // tm,), in_specs=in_specs,
        out_specs=pl.BlockSpec((tm, width), lambda i: (i, 0)),
        out_shape=jax.ShapeDtypeStruct((rows, width), BF16), compiler_params=_cparams(),
    )(*args)


def _rmsnorm_bwd(xarr, colblk, width, g, dy, dres, out_dtype, name):
    rows = xarr.shape[0]
    tm = min(TM, rows)
    has_res = dres is not None

    def body(*refs):
        x_ref, g_ref, dy_ref = refs[0], refs[1], refs[2]
        dres_ref = refs[3] if has_res else None
        dx_ref, dg_ref = refs[-2], refs[-1]
        x = x_ref[...].astype(F32)
        dyv = dy_ref[...].astype(F32)
        r = lax.rsqrt(jnp.mean(x * x, axis=1, keepdims=True) + EPS)
        xh = x * r
        dxh = dyv * g_ref[...]
        dx = r * (dxh - xh * jnp.mean(dxh * xh, axis=1, keepdims=True))
        if has_res:
            dx = dx + dres_ref[...]
        dx_ref[...] = dx.astype(dx_ref.dtype)

        @pl.when(pl.program_id(0) == 0)
        def _():
            dg_ref[...] = jnp.zeros_like(dg_ref)

        dg_ref[...] += jnp.sum(dyv * xh, axis=0, keepdims=True)

    row_spec = pl.BlockSpec((tm, width), lambda i: (i, 0))
    vec_spec = pl.BlockSpec((1, width), lambda i: (0, 0))
    in_specs = [pl.BlockSpec((tm, width), lambda i: (i, colblk)), vec_spec, row_spec]
    args = [xarr, g.reshape(1, width), dy]
    if has_res:
        in_specs.append(row_spec)
        args.append(dres)
    return pl.pallas_call(
        body, name=name, grid=(rows // tm,), in_specs=in_specs, out_specs=[row_spec, vec_spec],
        out_shape=[jax.ShapeDtypeStruct((rows, width), out_dtype), jax.ShapeDtypeStruct((1, width), F32)],
        compiler_params=_cparams(),
    )(*args)


def _loss_head(x, g, tgt):
    rows, width = x.shape
    tm = min(TM, rows)

    def body(x_ref, g_ref, t_ref, dx_ref, dxb_ref, dg_ref, loss_ref):
        xv = x_ref[...]
        gv = g_ref[...]
        r = lax.rsqrt(jnp.mean(xv * xv, axis=1, keepdims=True) + EPS)
        xh = xv * r
        err = xh * gv - t_ref[...]
        part = 0.5 * jnp.sum(jnp.mean(err * err, axis=1, keepdims=True), axis=0, keepdims=True)
        dyv = err * (1.0 / width)
        dxh = dyv * gv
        dxv = r * (dxh - xh * jnp.mean(dxh * xh, axis=1, keepdims=True))
        dx_ref[...] = dxv
        dxb_ref[...] = dxv.astype(BF16)

        @pl.when(pl.program_id(0) == 0)
        def _():
            dg_ref[...] = jnp.zeros_like(dg_ref)
            loss_ref[...] = jnp.zeros_like(loss_ref)

        dg_ref[...] += jnp.sum(dyv * xh, axis=0, keepdims=True)
        loss_ref[...] += jnp.broadcast_to(part, loss_ref.shape)

    row_spec = pl.BlockSpec((tm, width), lambda i: (i, 0))
    vec_spec = pl.BlockSpec((1, width), lambda i: (0, 0))
    return pl.pallas_call(
        body, name="loss_head", grid=(rows // tm,), in_specs=[row_spec, vec_spec, row_spec],
        out_specs=[row_spec, row_spec, vec_spec, pl.BlockSpec((1, SLOT), lambda i: (0, 0))],
        out_shape=[jax.ShapeDtypeStruct((rows, width), F32), jax.ShapeDtypeStruct((rows, width), BF16),
                   jax.ShapeDtypeStruct((1, width), F32), jax.ShapeDtypeStruct((1, SLOT), F32)],
        compiler_params=_cparams(),
    )(x, g.reshape(1, width), tgt)


def _lane_consts():
    half = 16
    inv = ROPE_THETA ** (-(jnp.arange(half, dtype=F32) * 2.0) / 32)
    lane = jnp.arange(SLOT)
    first = (lane >= 64) & (lane < 80)
    second = (lane >= 80) & (lane < 96)
    inv_lane = jnp.where(first | second, inv[(lane - 64) % half], 0.0)
    rows = [inv_lane, (lane < 64).astype(F32), first.astype(F32), second.astype(F32)]
    rows += [jnp.zeros((SLOT,), F32)] * 4
    return jnp.stack(rows).astype(F32)


def _rope_tables(pos_col, consts):
    rows = pos_col.shape[0]
    tm = min(TM, rows)

    def body(p_ref, k_ref, c_ref, s1_ref, s2_ref):
        ang = p_ref[...] * k_ref[0:1, :]
        cos, sin = jnp.cos(ang), jnp.sin(ang)
        first, second = k_ref[2:3, :], k_ref[3:4, :]
        c_ref[...] = k_ref[1:2, :] + (first + second) * cos
        s1_ref[...] = -first * sin
        s2_ref[...] = second * sin

    spec = pl.BlockSpec((tm, SLOT), lambda i: (i, 0))
    shp = jax.ShapeDtypeStruct((rows, SLOT), F32)
    return pl.pallas_call(
        body, name="rope_tables", grid=(rows // tm,),
        in_specs=[pl.BlockSpec((tm, 1), lambda i: (i, 0)), pl.BlockSpec((8, SLOT), lambda i: (0, 0))],
        out_specs=[spec, spec, spec], out_shape=[shp, shp, shp], compiler_params=_cparams(),
    )(pos_col, consts)


def _rot(xv, c, s1, s2):
    return xv * c + pltpu.roll(xv, SLOT - 16, 1) * s1 + pltpu.roll(xv, 16, 1) * s2


def _rot_t(dy, c, s1, s2):
    return dy * c + pltpu.roll(dy * s1, 16, 1) + pltpu.roll(dy * s2, SLOT - 16, 1)


def _mla_rope_fwd(qraw, kvraw, proj, tabs):
    rows = qraw.shape[0]
    tm = min(TM_ROPE, rows)
    hw = MLA_HEADS * SLOT

    def body(q_ref, kv_ref, kr_ref, c_ref, s1_ref, s2_ref, qo, ko, vo):
        c, s1, s2 = c_ref[...], s1_ref[...], s2_ref[...]
        kr = _rot(kr_ref[...], c, s1, s2)
        low = lax.broadcasted_iota(jnp.int32, (tm, SLOT), 1) < HEAD_DIM
        for h in range(MLA_HEADS):
            sl = slice(h * SLOT, (h + 1) * SLOT)
            qo[:, sl] = (_rot(q_ref[:, sl], c, s1, s2) * MLA_Q_SCALE).astype(BF16)
            kvh = kv_ref[:, sl]
            ko[:, sl] = (jnp.where(low, kvh, 0.0) + kr).astype(BF16)
            vo[:, sl] = pltpu.roll(jnp.where(low, 0.0, kvh), HEAD_DIM, 1).astype(BF16)

    tab = pl.BlockSpec((tm, SLOT), lambda i: (i, 0))
    wide = pl.BlockSpec((tm, hw), lambda i: (i, 0))
    shp = jax.ShapeDtypeStruct((rows, hw), BF16)
    return pl.pallas_call(
        body, name="mla_rope_fwd", grid=(rows // tm,),
        in_specs=[wide, wide, pl.BlockSpec((tm, SLOT), lambda i: (i, 3)),
                  tab, tab, tab],
        out_specs=[wide, wide, wide], out_shape=[shp, shp, shp], compiler_params=_cparams(),
    )(qraw, kvraw, proj, *tabs)


def _mla_rope_bwd(dq, dk, dv, tabs, consts):
    rows = dq.shape[0]
    tm = min(TM_ROPE, rows)
    hw = MLA_HEADS * SLOT

    def body(dq_ref, dk_ref, dv_ref, c_ref, s1_ref, s2_ref, k_ref, dqo, dkvo, dkro):
        c, s1, s2 = c_ref[...], s1_ref[...], s2_ref[...]
        ksum = jnp.zeros((tm, SLOT), F32)
        low = lax.broadcasted_iota(jnp.int32, (tm, SLOT), 1) < HEAD_DIM
        for h in range(MLA_HEADS):
            sl = slice(h * SLOT, (h + 1) * SLOT)
            dqo[:, sl] = _rot_t(dq_ref[:, sl], c, s1, s2).astype(BF16)
            dkh = dk_ref[:, sl]
            ksum = ksum + dkh
            dvh = pltpu.roll(jnp.where(low, dv_ref[:, sl].astype(F32), 0.0), HEAD_DIM, 1)
            dkvo[:, sl] = (jnp.where(low, dkh, 0.0) + dvh).astype(BF16)
        dkro[...] = _rot_t(ksum, c, s1, s2) * (k_ref[2:3, :] + k_ref[3:4, :])

    tab = pl.BlockSpec((tm, SLOT), lambda i: (i, 0))
    wide = pl.BlockSpec((tm, hw), lambda i: (i, 0))
    return pl.pallas_call(
        body, name="mla_rope_bwd", grid=(rows // tm,),
        in_specs=[wide, wide, wide, tab, tab, tab, pl.BlockSpec((8, SLOT), lambda i: (0, 0))],
        out_specs=[wide, wide, tab],
        out_shape=[jax.ShapeDtypeStruct((rows, hw), BF16), jax.ShapeDtypeStruct((rows, hw), BF16),
                   jax.ShapeDtypeStruct((rows, SLOT), F32)],
        compiler_params=_cparams(),
    )(dq, dk, dv, *tabs, consts)


def _nt(a, b):
    return lax.dot_general(a, b, _DIMS["nt"], preferred_element_type=F32)


def _tn(a, b):
    return lax.dot_general(a, b, _DIMS["tn"], preferred_element_type=F32)


def _nn(a, b):
    return lax.dot_general(a, b, _DIMS["nn"], preferred_element_type=F32)


def _mla_attn_fwd(q, k, v):
    rows = q.shape[0]
    t = min(TQ_MLA, rows)
    nt = rows // t
    wide = MLA_PACK * SLOT

    def body(q_ref, k_ref, v_ref, o_ref, lse_ref, m_sc, l_sc, acc_sc):
        i, j = pl.program_id(1), pl.program_id(2)

        @pl.when(j == 0)
        def _():
            m_sc[...] = jnp.full_like(m_sc, NEG)
            l_sc[...] = jnp.zeros_like(l_sc)
            acc_sc[...] = jnp.zeros_like(acc_sc)

        def step(diagonal):
            for hh in range(MLA_PACK):
                sl = slice(hh * SLOT, (hh + 1) * SLOT)
                s = _nt(k_ref[:, sl], q_ref[:, sl])
                if diagonal:
                    key = lax.broadcasted_iota(jnp.int32, (t, t), 0)
                    s = jnp.where(key <= lax.broadcasted_iota(jnp.int32, (t, t), 1), s, NEG)
                m_prev = m_sc[hh]
                m_new = jnp.maximum(m_prev, jnp.max(s, axis=0, keepdims=True))
                p = jnp.exp2(s - m_new)
                alpha = jnp.exp2(m_prev - m_new)
                l_new = alpha * l_sc[hh] + jnp.sum(p, axis=0, keepdims=True)
                acc = alpha * acc_sc[hh] + _tn(v_ref[:, sl], p.astype(BF16))
                if diagonal:
                    o_ref[:, sl] = (acc / l_new).T.astype(o_ref.dtype)
                    lse_ref[hh:hh + 1, :] = m_new + jnp.log(l_new) * LOG2_E
                else:
                    m_sc[hh] = m_new
                    l_sc[hh] = l_new
                    acc_sc[hh] = acc

        @pl.when(j < i)
        def _():
            step(False)

        @pl.when(j == i)
        def _():
            lse_ref[...] = jnp.zeros_like(lse_ref)
            step(True)

    q_spec = pl.BlockSpec((t, wide), lambda h, i, j: (i, h))
    kv_spec = pl.BlockSpec((t, wide), lambda h, i, j: (jnp.minimum(j, i), h))
    return pl.pallas_call(
        body, name="mla_attn_fwd", grid=(MLA_HEADS // MLA_PACK, nt, nt),
        in_specs=[q_spec, kv_spec, kv_spec],
        out_specs=[q_spec, pl.BlockSpec((None, 8, t), lambda h, i, j: (h, 0, i))],
        out_shape=[jax.ShapeDtypeStruct(q.shape, BF16),
                   jax.ShapeDtypeStruct((MLA_HEADS // MLA_PACK, 8, rows), F32)],
        scratch_shapes=[pltpu.VMEM((MLA_PACK, 1, t), F32), pltpu.VMEM((MLA_PACK, 1, t), F32),
                        pltpu.VMEM((MLA_PACK, SLOT, t), F32)],
        compiler_params=_cparams(),
    )(q, k, v)


def _mla_delta(o, do):
    rows = o.shape[0]
    t = rows
    wide = MLA_PACK * SLOT

    def body(o_ref, do_ref, d_ref):
        d_ref[...] = jnp.zeros_like(d_ref)
        ones = jnp.ones((8, SLOT), BF16)
        for hh in range(MLA_PACK):
            sl = slice(hh * SLOT, (hh + 1) * SLOT)
            prod = do_ref[:, sl].astype(F32) * o_ref[:, sl].astype(F32)
            high = prod.astype(BF16)
            low = (prod - high.astype(F32)).astype(BF16)
            d_ref[hh:hh + 1, :] = (_nt(ones, high) + _nt(ones, low))[0:1, :]

    spec = pl.BlockSpec((t, wide), lambda h, i: (i, h))
    return pl.pallas_call(
        body, name="mla_delta", grid=(MLA_HEADS // MLA_PACK, rows // t), in_specs=[spec, spec],
        out_specs=pl.BlockSpec((None, 8, t), lambda h, i: (h, 0, i)),
        out_shape=jax.ShapeDtypeStruct((MLA_HEADS // MLA_PACK, 8, rows), F32), compiler_params=_cparams(),
    )(o, do)


def _mla_attn_bwd(q, k, v, do, lse, delta, after):
    rows = q.shape[0]
    t = min(TQ_MLA, rows)
    nt = rows // t
    wide = MLA_PACK * SLOT

    def body(q_ref, k_ref, v_ref, do_ref, lse_ref, delta_ref, after_ref, dq_ref, dk_ref, dv_ref, dk_sc, dv_sc):
        j, i = pl.program_id(1), pl.program_id(2)

        @pl.when((j == 0) & (i == 0))
        def _():
            dq_ref[...] = jnp.zeros_like(dq_ref)

        @pl.when(i == 0)
        def _():
            dk_sc[...] = jnp.zeros_like(dk_sc)
            dv_sc[...] = jnp.zeros_like(dv_sc)

        def chunk(hh, rows, keys, masked):
            sl = slice(hh * SLOT, (hh + 1) * SLOT)
            n_rows = rows.stop - rows.start
            qv, kv, dov = q_ref[rows, sl], k_ref[keys, sl], do_ref[rows, sl]
            s = _nt(kv, qv)
            if masked:
                shp = (keys.stop - keys.start, n_rows)
                s = jnp.where(keys.start + lax.broadcasted_iota(jnp.int32, shp, 0)
                              <= rows.start + lax.broadcasted_iota(jnp.int32, shp, 1), s, NEG)
            p = jnp.exp2(s - lse_ref[hh:hh + 1, rows])
            dp = _nt(v_ref[keys, sl], dov)
            ds = (p * (dp - delta_ref[hh:hh + 1, rows])).astype(BF16)
            dv_sc[keys, sl] += _nn(p.astype(BF16), dov)
            dk_sc[keys, sl] += _nn(ds, qv)
            r0 = pl.multiple_of(i * t + rows.start, n_rows)
            dq_ref[pl.ds(r0, n_rows), sl] += _tn(ds, kv) * MLA_SCALE

        @pl.when(i > j)
        def _():
            for hh in range(MLA_PACK):
                chunk(hh, slice(0, t), slice(0, t), False)

        @pl.when(i == j)
        def _():
            for hh in range(MLA_PACK):
                chunk(hh, slice(0, t), slice(0, t // 2), True)
                chunk(hh, slice(t // 2, t), slice(t // 2, t), True)

        @pl.when(i == nt - 1)
        def _():
            dk_ref[...] = dk_sc[...] * (1.0 / LOG2_E)
            dv_ref[...] = dv_sc[...].astype(dv_ref.dtype)

    q_spec = pl.BlockSpec((t, wide), lambda h, j, i: (jnp.maximum(i, j), h))
    kv_spec = pl.BlockSpec((t, wide), lambda h, j, i: (j, h))
    row_spec = pl.BlockSpec((None, 8, t), lambda h, j, i: (h, 0, jnp.maximum(i, j)))
    head_spec = pl.BlockSpec((rows, wide), lambda h, j, i: (0, h))
    shp = jax.ShapeDtypeStruct(q.shape, F32)
    return pl.pallas_call(
        body, name="mla_attn_bwd", grid=(MLA_HEADS // MLA_PACK, nt, nt),
        in_specs=[q_spec, kv_spec, kv_spec, q_spec, row_spec, row_spec, pl.BlockSpec(memory_space=pl.ANY)],
        out_specs=[head_spec, kv_spec, kv_spec], out_shape=[shp, shp, jax.ShapeDtypeStruct(q.shape, BF16)],
        scratch_shapes=[pltpu.VMEM((t, wide), F32), pltpu.VMEM((t, wide), F32)],
        compiler_params=_cparams(),
    )(q, k, v, do, lse, delta, after)


def _swa_specs(t):
    def prev(i):
        return jnp.maximum(i - 1, 0)
    kw = SWA_PACK * SLOT
    k0, v0 = SWA_HEADS // SWA_PACK, (SWA_HEADS + SWA_KV_HEADS) // SWA_PACK
    q3 = pl.BlockSpec((t, SWA_PACK * SWA_GROUP * SLOT), lambda h, i: (i, h))
    kp = pl.BlockSpec((t, kw), lambda h, i: (prev(i), k0 + h))
    kc = pl.BlockSpec((t, kw), lambda h, i: (i, k0 + h))
    vp = pl.BlockSpec((t, kw), lambda h, i: (prev(i), v0 + h))
    vc = pl.BlockSpec((t, kw), lambda h, i: (i, v0 + h))
    pcol = pl.BlockSpec((t, 1), lambda h, i: (i, 0))
    prow_p = pl.BlockSpec((1, t), lambda h, i: (0, prev(i)))
    prow_c = pl.BlockSpec((1, t), lambda h, i: (0, i))
    return [q3, kp, kc, vp, vc, pcol, prow_p, prow_c]


def _stack(ref, first):
    return jnp.concatenate([ref[:, (first + g) * SLOT:(first + g + 1) * SLOT] for g in range(SWA_GROUP)], axis=0)


def _swa_logits(q3, kp, kc, pq, pkp, pkc, slope_ref, kvh, i, t):
    r = lax.broadcasted_iota(jnp.int32, (t, t), 0)
    c = lax.broadcasted_iota(jnp.int32, (t, t), 1)
    ok_c = c <= r
    ok_p = (c - r) > jnp.where(i > 0, 0, t)
    dist_p, dist_c = pq - pkp, pq - pkc
    s_p3 = _nt(q3, kp) * (HEAD_DIM ** -0.5)
    s_c3 = _nt(q3, kc) * (HEAD_DIM ** -0.5)
    out = []
    for g in range(SWA_GROUP):
        slope = slope_ref[kvh * SWA_GROUP + g]
        rows = slice(g * t, (g + 1) * t)
        out.append((jnp.where(ok_p, s_p3[rows] - slope * dist_p, NEG),
                    jnp.where(ok_c, s_c3[rows] - slope * dist_c, NEG)))
    return out


def _swa_attn_fwd(proj, pos_col, pos_row, slopes, sinks):
    rows = proj.shape[0]
    t = WINDOW
    hw = SWA_HEADS * SLOT

    def body(slope_ref, sink_ref, q_ref, kp_ref, kc_ref, vp_ref, vc_ref, pq_ref, pkp_ref, pkc_ref, o_ref, lse_ref):
        i = pl.program_id(1)
        for kv in range(SWA_PACK):
            kvh = pl.program_id(0) * SWA_PACK + kv
            ksl = slice(kv * SLOT, (kv + 1) * SLOT)
            logits = _swa_logits(_stack(q_ref, kv * SWA_GROUP), kp_ref[:, ksl], kc_ref[:, ksl], pq_ref[...],
                                 pkp_ref[...], pkc_ref[...], slope_ref, kvh, i, t)
            e_p, e_c, norm = [], [], []
            for g, (s_p, s_c) in enumerate(logits):
                sl = slice((kv * SWA_GROUP + g) * SLOT, (kv * SWA_GROUP + g + 1) * SLOT)
                sink = sink_ref[kvh * SWA_GROUP + g]
                m = jnp.maximum(jnp.maximum(jnp.max(s_p, axis=1, keepdims=True),
                                            jnp.max(s_c, axis=1, keepdims=True)), sink)
                ep, ec = jnp.exp(s_p - m), jnp.exp(s_c - m)
                l = jnp.sum(ep, axis=1, keepdims=True) + jnp.sum(ec, axis=1, keepdims=True) + jnp.exp(sink - m)
                e_p.append(ep.astype(BF16))
                e_c.append(ec.astype(BF16))
                norm.append(l)
                lse_ref[:, sl] = jnp.broadcast_to(m + jnp.log(l), (t, SLOT))
            acc = (_nn(jnp.concatenate(e_p, axis=0), vp_ref[:, ksl])
                   + _nn(jnp.concatenate(e_c, axis=0), vc_ref[:, ksl]))
            for g in range(SWA_GROUP):
                sl = slice((kv * SWA_GROUP + g) * SLOT, (kv * SWA_GROUP + g + 1) * SLOT)
                o_ref[:, sl] = (acc[g * t:(g + 1) * t] / norm[g]).astype(o_ref.dtype)

    smem = pl.BlockSpec(memory_space=pltpu.SMEM)
    out_spec = pl.BlockSpec((t, SWA_PACK * SWA_GROUP * SLOT), lambda h, i: (i, h))
    return pl.pallas_call(
        body, name="swa_attn_fwd", grid=(SWA_KV_HEADS // SWA_PACK, rows // t),
        in_specs=[smem, smem] + _swa_specs(t), out_specs=[out_spec, out_spec],
        out_shape=[jax.ShapeDtypeStruct((rows, hw), BF16), jax.ShapeDtypeStruct((rows, hw), F32)],
        compiler_params=_cparams(),
    )(slopes, sinks, proj, proj, proj, proj, proj, pos_col, pos_row, pos_row)


def _swa_attn_bwd(proj, o, do, lse, pos_col, pos_row, slopes, sinks, after):
    rows = proj.shape[0]
    t = WINDOW
    hw = SWA_HEADS * SLOT
    scale = HEAD_DIM ** -0.5

    def body(slope_ref, sink_ref, q_ref, kp_ref, kc_ref, vp_ref, vc_ref, pq_ref, pkp_ref, pkc_ref,
             o_ref, do_ref, lse_ref, after_ref, dq_ref, dk_out, dv_out, dsink_ref, dk_ref, dv_ref):
        i = pl.program_id(1)

        @pl.when(i == 0)
        def _():
            dk_ref[...] = jnp.zeros_like(dk_ref)
            dv_ref[...] = jnp.zeros_like(dv_ref)
            dsink_ref[...] = jnp.zeros_like(dsink_ref)

        r_c = pl.multiple_of(i * t, t)
        r_p = pl.multiple_of(jnp.maximum(i - 1, 0) * t, t)
        for kv in range(SWA_PACK):
            kvh = pl.program_id(0) * SWA_PACK + kv
            ksl = slice(kv * SLOT, (kv + 1) * SLOT)
            q3, do3 = _stack(q_ref, kv * SWA_GROUP), _stack(do_ref, kv * SWA_GROUP)
            logits = _swa_logits(q3, kp_ref[:, ksl], kc_ref[:, ksl], pq_ref[...], pkp_ref[...], pkc_ref[...],
                                 slope_ref, kvh, i, t)
            dp_p3, dp_c3 = _nt(do3, vp_ref[:, ksl]), _nt(do3, vc_ref[:, ksl])
            p_p, p_c, ds_p, ds_c = [], [], [], []
            for g, (s_p, s_c) in enumerate(logits):
                head = kv * SWA_GROUP + g
                sl = slice(head * SLOT, (head + 1) * SLOT)
                rws = slice(g * t, (g + 1) * t)
                lse_g = lse_ref[:, head * SLOT:head * SLOT + 1]
                pp, pc = jnp.exp(s_p - lse_g), jnp.exp(s_c - lse_g)
                delta = jnp.sum(do_ref[:, sl].astype(F32) * o_ref[:, sl].astype(F32), axis=1, keepdims=True)
                p_p.append(pp.astype(BF16))
                p_c.append(pc.astype(BF16))
                ds_p.append((pp * (dp_p3[rws] - delta)).astype(BF16))
                ds_c.append((pc * (dp_c3[rws] - delta)).astype(BF16))
                sink = sink_ref[kvh * SWA_GROUP + g]
                dsink = -jnp.sum(jnp.exp(sink - lse_g) * delta, axis=0, keepdims=True)
                dsink_ref[head * 8:(head + 1) * 8, :] += jnp.broadcast_to(dsink, (8, SLOT))
            p_p3, p_c3 = jnp.concatenate(p_p, axis=0), jnp.concatenate(p_c, axis=0)
            ds_p3, ds_c3 = jnp.concatenate(ds_p, axis=0), jnp.concatenate(ds_c, axis=0)
            dq3 = (_nn(ds_p3, kp_ref[:, ksl]) + _nn(ds_c3, kc_ref[:, ksl])) * scale
            for g in range(SWA_GROUP):
                head = kv * SWA_GROUP + g
                dq_ref[:, head * SLOT:(head + 1) * SLOT] = dq3[g * t:(g + 1) * t].astype(dq_ref.dtype)
            dk_ref[pl.ds(r_c, t), ksl] += _tn(ds_c3, q3) * scale
            dv_ref[pl.ds(r_c, t), ksl] += _tn(p_c3, do3)
            dk_ref[pl.ds(r_p, t), ksl] += _tn(ds_p3, q3) * scale
            dv_ref[pl.ds(r_p, t), ksl] += _tn(p_p3, do3)

        @pl.when(i == pl.num_programs(1) - 1)
        def _():
            dk_out[...] = dk_ref[...].astype(dk_out.dtype)
            dv_out[...] = dv_ref[...].astype(dv_out.dtype)

    smem = pl.BlockSpec(memory_space=pltpu.SMEM)
    qlike = pl.BlockSpec((t, SWA_PACK * SWA_GROUP * SLOT), lambda h, i: (i, h))
    kv_out = pl.BlockSpec((rows, SWA_PACK * SLOT), lambda h, i: (0, h))
    return pl.pallas_call(
        body, name="swa_attn_bwd", grid=(SWA_KV_HEADS // SWA_PACK, rows // t),
        in_specs=[smem, smem] + _swa_specs(t) + [qlike, qlike, qlike, pl.BlockSpec(memory_space=pl.ANY)],
        out_specs=[qlike, kv_out, kv_out,
                   pl.BlockSpec((SWA_PACK * SWA_GROUP * 8, SLOT), lambda h, i: (h, 0))],
        out_shape=[jax.ShapeDtypeStruct((rows, hw), BF16), jax.ShapeDtypeStruct((rows, SWA_KV_HEADS * SLOT), BF16),
                   jax.ShapeDtypeStruct((rows, SWA_KV_HEADS * SLOT), BF16),
                   jax.ShapeDtypeStruct((SWA_HEADS * 8, SLOT), F32)],
        scratch_shapes=[pltpu.VMEM((rows, SWA_PACK * SLOT), F32), pltpu.VMEM((rows, SWA_PACK * SLOT), F32)],
        compiler_params=_cparams(),
    )(slopes, sinks, proj, proj, proj, proj, proj, pos_col, pos_row, pos_row, o, do, lse, after)


def _cross_attn_fwd(proj, qoff, kvmem):
    rows = proj.shape[0]
    t = min(TQ_CROSS, rows)

    def body(q_ref, k_ref, v_ref, o_ref):
        s = _nt(k_ref[...], q_ref[...].astype(BF16)) * (HEAD_DIM ** -0.5)
        e = jnp.exp(s - jnp.max(s, axis=0, keepdims=True))
        p = e / jnp.sum(e, axis=0, keepdims=True)
        o_ref[...] = _tn(v_ref[...], p.astype(BF16)).T.astype(o_ref.dtype)

    return pl.pallas_call(
        body, name="cross_attn_fwd", grid=(rows // t, MEM_HEADS),
        in_specs=[pl.BlockSpec((t, SLOT), lambda i, h: (i, qoff + h)),
                  pl.BlockSpec((N_MEM, SLOT), lambda i, h: (0, h)),
                  pl.BlockSpec((N_MEM, SLOT), lambda i, h: (0, MEM_HEADS + h))],
        out_specs=pl.BlockSpec((t, SLOT), lambda i, h: (i, h)),
        out_shape=jax.ShapeDtypeStruct((rows, MEM_HEADS * SLOT), BF16), compiler_params=_cparams(),
    )(proj, kvmem, kvmem)


def _cross_attn_bwd(proj, qoff, kvmem, do, do_off):
    rows = proj.shape[0]
    t = min(TQ_CROSS, rows)
    scale = HEAD_DIM ** -0.5

    def body(q_ref, k_ref, v_ref, do_ref, dq_ref, dk_ref, dv_ref):
        @pl.when(pl.program_id(1) == 0)
        def _():
            dk_ref[...] = jnp.zeros_like(dk_ref)
            dv_ref[...] = jnp.zeros_like(dv_ref)

        qv, kv, dov = q_ref[...].astype(BF16), k_ref[...], do_ref[...]
        s = _nt(kv, qv) * scale
        e = jnp.exp(s - jnp.max(s, axis=0, keepdims=True))
        p = e / jnp.sum(e, axis=0, keepdims=True)
        dp = _nt(v_ref[...], dov)
        ds = (p * (dp - jnp.sum(p * dp, axis=0, keepdims=True))).astype(BF16)
        dq_ref[...] = (_tn(ds, kv) * scale).astype(dq_ref.dtype)
        dk_ref[...] += _nn(ds, qv) * scale
        dv_ref[...] += _nn(p.astype(BF16), dov)

    mem_out = pl.BlockSpec((N_MEM, SLOT), lambda h, i: (0, h))
    return pl.pallas_call(
        body, name="cross_attn_bwd", grid=(MEM_HEADS, rows // t),
        in_specs=[pl.BlockSpec((t, SLOT), lambda h, i: (i, qoff + h)),
                  pl.BlockSpec((N_MEM, SLOT), lambda h, i: (0, h)),
                  pl.BlockSpec((N_MEM, SLOT), lambda h, i: (0, MEM_HEADS + h)),
                  pl.BlockSpec((t, SLOT), lambda h, i: (i, do_off + h))],
        out_specs=[pl.BlockSpec((t, SLOT), lambda h, i: (i, h)), mem_out, mem_out],
        out_shape=[jax.ShapeDtypeStruct((rows, MEM_HEADS * SLOT), BF16),
                   jax.ShapeDtypeStruct((N_MEM, MEM_HEADS * SLOT), F32),
                   jax.ShapeDtypeStruct((N_MEM, MEM_HEADS * SLOT), F32)],
        compiler_params=_cparams(),
    )(proj, kvmem, kvmem, do)


def _place():
    return lax.axis_index("x"), lax.axis_index("y"), lax.axis_index("c")


def _flip(v, bit):
    return 1 - v if bit else v


def _all_gather(blocks, name):
    nb = len(blocks)

    def body(*refs):
        x_refs, out_refs = refs[:nb], refs[nb:2 * nb]
        send_sems, recv_sems, local_sems = refs[2 * nb:]
        x, y, c = _place()
        me, sibling = (x, y, c), (x, y, 1 - c)
        chips = [(1 - x, y), (x, 1 - y), (1 - x, 1 - y)]

        def copy(b, k, blk, to, from_input=False):
            slot = out_refs[b].at[4 * blk[0] + 2 * blk[1] + blk[2]]
            return pltpu.make_async_remote_copy(
                src_ref=x_refs[b] if from_input else slot, dst_ref=slot,
                send_sem=send_sems.at[b, k], recv_sem=recv_sems.at[b, k],
                device_id=to, device_id_type=pl.DeviceIdType.MESH)

        mine = [pltpu.make_async_copy(x_refs[b], out_refs[b].at[4 * x + 2 * y + c], local_sems.at[b])
                for b in range(nb)]
        for cp in mine:
            cp.start()
        first = []
        for b in range(nb):
            first.append(copy(b, 0, me, sibling, from_input=True))
            first += [copy(b, 1 + n, me, (*chip, c), from_input=True) for n, chip in enumerate(chips)]
        for cp in first:
            cp.start()
        passed = []
        for n, chip in enumerate(chips):
            for b in range(nb):
                copy(b, 1 + n, (*chip, c), me).wait_recv()
                passed.append(copy(b, 4 + n, (*chip, c), sibling))
                passed[-1].start()
        for b in range(nb):
            copy(b, 0, sibling, me).wait_recv()
            for n, chip in enumerate(chips):
                copy(b, 4 + n, (*chip, 1 - c), me).wait_recv()
        for cp in first + passed:
            cp.wait_send()
        for cp in mine:
            cp.wait()

    any_spec = pl.BlockSpec(memory_space=pl.ANY)
    return pl.pallas_call(
        body, name=name, in_specs=[any_spec] * nb, out_specs=[any_spec] * nb,
        out_shape=[jax.ShapeDtypeStruct((N_DEV,) + blk.shape, blk.dtype) for blk in blocks],
        scratch_shapes=[pltpu.SemaphoreType.DMA((nb, 7)), pltpu.SemaphoreType.DMA((nb, 7)),
                        pltpu.SemaphoreType.DMA((nb,))],
    )(*blocks)


def _peers(x, y, c):
    out = []
    for n in range(1, N_DEV):
        peer = (_flip(x, n & 4), _flip(y, n & 2), _flip(c, n & 1))
        out.append((n - 1, peer, 4 * peer[0] + 2 * peer[1] + peer[2]))
    return out


_HBM = pl.BlockSpec(memory_space=pltpu.HBM)
_SEM = pl.BlockSpec(memory_space=pltpu.SEMAPHORE)


def _exchange_start(srcs, scatter, name, after=None):
    ns = len(srcs)
    lands = [lax.empty(s.shape if scatter else (N_DEV,) + s.shape, s.dtype) for s in srcs]

    def body(*refs):
        src_refs, land_refs = refs[:ns], refs[ns:2 * ns]
        pos = 2 * ns + (1 if after is not None else 0)
        send_sems, recv_sems, token = refs[pos], refs[pos + 1], refs[-1]
        x, y, c = _place()
        my_idx = 4 * x + 2 * y + c
        for col, peer, peer_idx in _peers(x, y, c):
            for b in range(ns):
                pltpu.make_async_remote_copy(
                    src_ref=src_refs[b].at[peer_idx] if scatter else src_refs[b], dst_ref=land_refs[b].at[my_idx],
                    send_sem=send_sems.at[b * (N_DEV - 1) + col], recv_sem=recv_sems.at[b * (N_DEV - 1) + col],
                    device_id=peer, device_id_type=pl.DeviceIdType.MESH).start()
        token[...] = jnp.zeros_like(token)

    args = [pltpu.with_memory_space_constraint(a, pltpu.HBM) for a in list(srcs) + lands]
    in_specs = [_HBM] * (2 * ns)
    if after is not None:
        args.append(after)
        in_specs.append(pl.BlockSpec(memory_space=pl.ANY))
    out = pl.pallas_call(
        body, name=name, in_specs=in_specs,
        out_specs=[_SEM, _SEM] + [_HBM] * (2 * ns) + [pl.BlockSpec(memory_space=pltpu.VMEM)],
        out_shape=[pltpu.SemaphoreType.DMA((ns * (N_DEV - 1),)), pltpu.SemaphoreType.DMA((ns * (N_DEV - 1),))]
        + [pltpu.HBM(a.shape, a.dtype) for a in list(srcs) + lands] + [jax.ShapeDtypeStruct((8, SLOT), F32)],
        input_output_aliases={k: 2 + k for k in range(2 * ns)},
        compiler_params=pltpu.CompilerParams(has_side_effects=pltpu.SideEffectType.DATAFLOW_SIDE_EFFECTING),
    )(*args)
    return (out[0], out[1], out[2:2 + ns], out[2 + ns:2 + 2 * ns], scatter), out[-1]


def _exchange_wait(handle, after, name):
    send_sems, recv_sems, srcs, lands, scatter = handle
    ns = len(srcs)

    def body(*refs):
        src_refs, land_refs = refs[:ns], refs[ns:2 * ns]
        send_ref, recv_ref = refs[2 * ns], refs[2 * ns + 1]
        x, y, c = _place()
        for col, peer, peer_idx in _peers(x, y, c):
            for b in range(ns):
                copy = pltpu.make_async_remote_copy(
                    src_ref=src_refs[b].at[peer_idx] if scatter else src_refs[b], dst_ref=land_refs[b].at[peer_idx],
                    send_sem=send_ref.at[b * (N_DEV - 1) + col], recv_sem=recv_ref.at[b * (N_DEV - 1) + col],
                    device_id=peer, device_id_type=pl.DeviceIdType.MESH)
                copy.wait_send()
                copy.wait_recv()

    out = pl.pallas_call(
        body, name=name, in_specs=[_HBM] * (2 * ns) + [_SEM, _SEM, pl.BlockSpec(memory_space=pl.ANY)],
        out_specs=[_HBM] * (2 * ns),
        out_shape=[pltpu.HBM(a.shape, a.dtype) for a in list(srcs) + list(lands)],
        input_output_aliases={k: k for k in range(2 * ns)},
        compiler_params=pltpu.CompilerParams(has_side_effects=pltpu.SideEffectType.DATAFLOW_SIDE_EFFECTING),
    )(*srcs, *lands, send_sems, recv_sems, after)
    my_idx = 4 * lax.axis_index("x") + 2 * lax.axis_index("y") + lax.axis_index("c")
    landed = []
    for src, land in zip(out[:ns], out[ns:]):
        own = lax.dynamic_index_in_dim(src, my_idx, 0, keepdims=True) if scatter else src[None]
        landed.append(lax.dynamic_update_index_in_dim(land, own, my_idx, 0))
    return landed


def _adamw(parts, w, m, v, name):
    lyr, rows, cols = w.shape
    assert len(parts) == lyr
    tr = ADAM_ROWS if cols > 512 else 2 * ADAM_ROWS
    while rows % tr:
        tr //= 2
    tr = min(tr, rows)

    def body(*refs):
        p_refs = refs[:lyr]
        w_ref, m_ref, v_ref, g_out, d_out, m_out, v_out = refs[lyr:]
        for k in range(lyr):
            @pl.when(pl.program_id(0) == k)
            def _(p_ref=p_refs[k]):
                g = p_ref[0].astype(F32)
                for s in range(1, N_DEV):
                    g = g + p_ref[s].astype(F32)
                m2 = ADAM_B1 * m_ref[...] + (1.0 - ADAM_B1) * g
                v2 = ADAM_B2 * v_ref[...] + (1.0 - ADAM_B2) * (g * g)
                m_hat = m2 / (1.0 - ADAM_B1 ** ADAM_STEP)
                v_hat = v2 / (1.0 - ADAM_B2 ** ADAM_STEP)
                g_out[...] = g
                d_out[...] = -ADAM_LR * (m_hat / (jnp.sqrt(v_hat) + ADAM_EPS) + ADAM_WD * w_ref[...])
                m_out[...] = m2
                v_out[...] = v2

    def part_spec(k):
        return pl.BlockSpec((N_DEV, tr, cols), lambda l, i: (0, jnp.where(l == k, i, 0), 0))

    spec = pl.BlockSpec((None, tr, cols), lambda l, i: (l, i, 0))
    shp = jax.ShapeDtypeStruct((lyr, rows, cols), F32)
    return pl.pallas_call(
        body, name=name, grid=(lyr, rows // tr),
        in_specs=[part_spec(k) for k in range(lyr)] + [spec, spec, spec],
        out_specs=[spec] * 4, out_shape=[shp] * 4, compiler_params=_cparams(),
    )(*parts, w, m, v)


def _pack(arrays, lanes, row_mult, dtype):
    flat = jnp.concatenate([a.reshape(-1).astype(dtype) for a in arrays])
    unit = lanes * row_mult
    total = -(-flat.shape[0] // unit) * unit
    return jnp.pad(flat, (0, total - flat.shape[0])).reshape(total // lanes, lanes)


def _unpack(packed, shapes):
    flat = packed.reshape(-1)
    out, off = [], 0
    for shp in shapes:
        n = 1
        for d in shp:
            n *= d
        out.append(flat[off:off + n].reshape(shp))
        off += n
    return out


def _pad_slots(w, axis):
    axis = axis % w.ndim
    n = w.shape[axis] // HEAD_DIM
    shp = w.shape[:axis] + (n, HEAD_DIM) + w.shape[axis + 1:]
    pad = [(0, 0)] * (w.ndim + 1)
    pad[axis + 1] = (0, SLOT - HEAD_DIM)
    return jnp.pad(w.reshape(shp), pad).reshape(w.shape[:axis] + (n * SLOT,) + w.shape[axis + 1:])


def _unpad_slots(w, axis, keep=HEAD_DIM):
    axis = axis % w.ndim
    n = w.shape[axis] // SLOT
    shp = w.shape[:axis] + (n, SLOT) + w.shape[axis + 1:]
    idx = [slice(None)] * (w.ndim + 1)
    idx[axis + 1] = slice(0, keep)
    return w.reshape(shp)[tuple(idx)].reshape(w.shape[:axis] + (n * keep,) + w.shape[axis + 1:])


def _mla_in_pad(w):
    z = functools.partial(jnp.zeros, dtype=w.dtype)
    rows = w.shape[0]
    return jnp.concatenate([w[:, :384], z((rows, 64)), w[:, 640:672], z((rows, 32)), w[:, 384:640],
                            _pad_slots(w[:, 672:], 1)], axis=1)


def _mla_in_unpad(d):
    return jnp.concatenate([d[:, :384], d[:, 512:768], d[:, 448:480], _unpad_slots(d[:, 768:], 1)], axis=1)


def _mla_uq_pad(w):
    return jnp.pad(w.reshape(w.shape[0], MLA_HEADS, MLA_QK), ((0, 0), (0, 0), (0, SLOT - MLA_QK))).reshape(
        w.shape[0], MLA_HEADS * SLOT)


def _join(gathered, axis):
    nd, a, b = gathered.shape
    if axis == 1:
        return gathered.reshape(nd * a, b)
    return gathered.transpose(1, 0, 2).reshape(a, nd * b)


def _split(full, axis):
    r, c = full.shape
    if axis == 1:
        return full.reshape(N_DEV, r // N_DEV, c).astype(BF16)
    return full.reshape(r, N_DEV, c // N_DEV).transpose(1, 0, 2).astype(BF16)


def kernel(x, mem, positions, attn_norm_g, mlp_norm_g, mem_norm_g, final_norm_g, mla_w_in, mla_q_norm_g, mla_kv_norm_g, mla_w_uq, mla_w_ukv, swa_w_in, swa_sinks, w_mem_kv, w_o, mlp_w_up, mlp_w_down, loss_target, m_attn_norm_g, m_mlp_norm_g, m_mem_norm_g, m_final_norm_g, m_mla_w_in, m_mla_q_norm_g, m_mla_kv_norm_g, m_mla_w_uq, m_mla_w_ukv, m_swa_w_in, m_swa_sinks, m_w_mem_kv, m_w_o, m_mlp_w_up, m_mlp_w_down, v_attn_norm_g, v_mlp_norm_g, v_mem_norm_g, v_final_norm_g, v_mla_w_in, v_mla_q_norm_g, v_mla_kv_norm_g, v_mla_w_uq, v_mla_w_ukv, v_swa_w_in, v_swa_sinks, v_w_mem_kv, v_w_o, v_mlp_w_up, v_mlp_w_down):
    given = dict(locals())
    seq = x.shape[1]
    x0 = x.reshape(seq, D_MODEL)
    tgt = loss_target.reshape(seq, D_MODEL)
    mem0 = mem.reshape(N_MEM, D_MODEL)
    pos = positions.reshape(seq).astype(F32)
    pos_col, pos_row = pos.reshape(seq, 1), pos.reshape(1, seq)

    def layer_names(i):
        mixer = ("mla_w_in", "mla_w_uq", "mla_w_ukv") if i % 2 == 0 else ("swa_w_in",)
        return [(n, i // 2) for n in mixer] + [(n, i) for n in ("w_mem_kv", "w_o", "mlp_w_up", "mlp_w_down")]

    def local_weights(names):
        return [given[n][l].astype(BF16) for n, l in names]

    first_attn, first_mlp = layer_names(0)[:-2], layer_names(0)[-2:]
    weights = [dict(zip([n for n, _ in first_attn], _all_gather(local_weights(first_attn), "gather_weights_first")))]
    coming_mlp, first_token = _exchange_start(local_weights(first_mlp), False, "gather_weights_start_0",
                                              after=weights[0]["w_o"])

    consts = _lane_consts()
    tabs = _rope_tables(pos_col, consts)
    slopes = 2.0 ** (-8.0 * (jnp.arange(SWA_HEADS, dtype=F32) + 1.0) / SWA_HEADS)

    mem_n = _rmsnorm_fwd(mem0, 0, D_MODEL, mem_norm_g, "rmsnorm_fwd_mem")

    saved = []
    xc = x0
    for i in range(DEPTH):
        j = i // 2
        wts = weights[i]
        s = {"x_in": xc}
        token = None
        if i + 1 < DEPTH:
            coming, token = _exchange_start(local_weights(layer_names(i + 1)), False,
                                            "gather_weights_start_%d" % (i + 1),
                                            after=first_token if i == 0 else wts["w_o"])
        if i == 0:
            hn = _rmsnorm_fwd(xc, 0, D_MODEL, attn_norm_g[i], "rmsnorm_fwd")
        if i % 2 == 0:
            w_in = _mla_in_pad(_join(wts["mla_w_in"], 1))
            w_uq = _mla_uq_pad(_join(wts["mla_w_uq"], 2))
            w_kv = _join(wts["mla_w_ukv"], 2)
            proj = _mm(hn, w_in, "nn", F32, "mm_mla_in", after=token)
            cqn = _rmsnorm_fwd(proj, 0, MLA_Q_RANK, mla_q_norm_g[j], "rmsnorm_fwd_q")
            ckvn = _rmsnorm_fwd(proj, 2, MLA_KV_RANK, mla_kv_norm_g[j], "rmsnorm_fwd_kv")
            qraw = _mm(cqn, w_uq, "nn", F32, "mm_mla_uq")
            kvraw = _mm(ckvn, w_kv, "nn", F32, "mm_mla_ukv")
            q, k, v = _mla_rope_fwd(qraw, kvraw, proj, tabs)
            o, lse = _mla_attn_fwd(q, k, v)
            qoff = MLA_QOFF
            s.update(w_uq=w_uq, w_kv=w_kv, cqn=cqn, ckvn=ckvn, q=q, k=k, v=v)
        else:
            w_in = _join(wts["swa_w_in"], 2)
            proj = _mm(hn, w_in, "nn", BF16, "mm_swa_in", pairs="o", after=token)
            o, lse = _swa_attn_fwd(proj, pos_col, pos_row, slopes, swa_sinks[j])
            qoff = SWA_QOFF
        w_mem = _pad_slots(_join(wts["w_mem_kv"], 1), 1)
        w_out = _join(wts["w_o"], 1)
        w_o_mix, w_o_cross = w_out[:SWA_HEADS * HEAD_DIM], w_out[SWA_HEADS * HEAD_DIM:]
        kvmem = _mm(mem_n, w_mem, "nn", BF16, "mm_mem_kv")
        cross = _cross_attn_fwd(proj, qoff, kvmem)
        x1, hn2 = _mm(o, w_o_mix, "nn", F32, "mm_o", res=xc, pairs="a", second=(cross, w_o_cross),
                      epi="normfwd", norm=mlp_norm_g[i])
        if i == 0:
            wts.update(zip([n for n, _ in first_mlp], _exchange_wait(coming_mlp, hn2, "gather_weights_wait_0")))
        act, act2 = _mm(hn2, wts["mlp_w_up"], "nn", BF16, "mm_mlp_up", epi="relu2", b_blk="cols")
        if i + 1 < DEPTH:
            xc, hn_next = _mm(act2, wts["mlp_w_down"], "nn", F32, "mm_mlp_down", res=x1, b_blk="rows",
                              epi="normfwd", norm=attn_norm_g[i + 1])
        else:
            xc = _mm(act2, wts["mlp_w_down"], "nn", F32, "mm_mlp_down", res=x1, b_blk="rows")
        s.update(hn=hn, w_in=w_in, proj=proj, o=o, lse=lse, qoff=qoff, w_mem=w_mem, w_out=w_out,
                 kvmem=kvmem, cross=cross, x1=x1, hn2=hn2, act=act, act2=act2)
        saved.append(s)
        if i + 1 < DEPTH:
            hn = hn_next
            got = _exchange_wait(coming, xc, "gather_weights_wait_%d" % (i + 1))
            weights.append(dict(zip([n for n, _ in layer_names(i + 1)], got)))

    dx, dx_b, dg_final, loss_part = _loss_head(xc, final_norm_g, tgt)
    loss = lax.psum(loss_part[0, 0], MESH_AXES)

    gains = {n: [None] * DEPTH for n in ("attn_norm_g", "mlp_norm_g")}
    for n in ("mla_q_norm_g", "mla_kv_norm_g", "swa_sinks"):
        gains[n] = [None] * 2
    leaving = {}
    token = None
    dmem_n = None
    for i in reversed(range(DEPTH)):
        j = i // 2
        s = saved[i]
        wts = weights[i]
        out = {}
        du = _mm(dx_b, wts["mlp_w_down"], "nt", BF16, "mm_mlp_down_dx", aux=s["act"], epi="mul2aux", b_blk="rows",
                 after=token)
        out["mlp_w_down"] = _mm(s["act2"], dx_b, "tn", BF16, "mm_mlp_down_dw", o_blk="rows")
        out["mlp_w_up"] = _mm(s["hn2"], du, "tn", BF16, "mm_mlp_up_dw", o_blk="cols")
        dx1, dx1_b, dg = _mm(du, wts["mlp_w_up"], "nt", F32, "mm_mlp_up_dx", b_blk="cols",
                             epi="normbwd", norm=(s["x1"], mlp_norm_g[i], dx))
        gains["mlp_norm_g"][i] = dg[0]

        do = _mm(dx1_b, s["w_out"], "nt", BF16, "mm_o_dx", pairs="o")
        dw_o = jnp.concatenate([_mm(s["o"], dx1_b, "tn", F32, "mm_o_mix_dw", pairs="a"),
                                _mm(s["cross"], dx1_b, "tn", F32, "mm_o_cross_dw", pairs="a")], axis=0)
        out["w_o"] = _split(dw_o, 1)
        dqc, dkm, dvm = _cross_attn_bwd(s["proj"], s["qoff"], s["kvmem"], do, SWA_HEADS)
        dkvmem = jnp.concatenate([dkm, dvm], axis=1).astype(BF16)
        out["w_mem_kv"] = _split(_unpad_slots(_mm(mem_n, dkvmem, "tn", F32, "mm_mem_kv_dw"), 1), 1)
        dmem_n = _mm(dkvmem, s["w_mem"], "nt", F32, "mm_mem_kv_dx" if dmem_n is None else "mm_mem_kv_dx_acc",
                     res=dmem_n)
        leaving[(i, "main")], token = _exchange_start([out[n] for n, _ in layer_names(i)[-4:]], True,
                                                      "exchange_grads_main_start_%d" % i)

        if i % 2 == 0:
            dq, dk, dv = _mla_attn_bwd(s["q"], s["k"], s["v"], do, s["lse"], _mla_delta(s["o"], do), token)
            dqraw, dkv, dkr = _mla_rope_bwd(dq, dk, dv, tabs, consts)
            dcqn = _mm(dqraw, s["w_uq"], "nt", F32, "mm_mla_uq_dx")
            out["mla_w_uq"] = _split(_unpad_slots(_mm(s["cqn"], dqraw, "tn", F32, "mm_mla_uq_dw"), 1, MLA_QK), 2)
            dckvn = _mm(dkv, s["w_kv"], "nt", F32, "mm_mla_ukv_dx")
            out["mla_w_ukv"] = _split(_mm(s["ckvn"], dkv, "tn", F32, "mm_mla_ukv_dw"), 2)
            dcq, dg = _rmsnorm_bwd(s["proj"], 0, MLA_Q_RANK, mla_q_norm_g[j], dcqn, None, BF16, "rmsnorm_bwd_q")
            gains["mla_q_norm_g"][j] = dg[0]
            dckv, dg = _rmsnorm_bwd(s["proj"], 2, MLA_KV_RANK, mla_kv_norm_g[j], dckvn, None, BF16, "rmsnorm_bwd_kv")
            gains["mla_kv_norm_g"][j] = dg[0]
            dproj = jnp.concatenate([dcq, dkr.astype(BF16), dckv, dqc.astype(BF16)], axis=1)
            in_dx = "mm_mla_in_dx"
            out["mla_w_in"] = _split(_mla_in_unpad(_mm(s["hn"], dproj, "tn", F32, "mm_mla_in_dw")), 1)
        else:
            dq, dk, dv, dsink = _swa_attn_bwd(s["proj"], s["o"], do, s["lse"], pos_col, pos_row, slopes, swa_sinks[j],
                                              token)
            gains["swa_sinks"][j] = dsink[::8, 0]
            dproj = jnp.concatenate([dq, dk, dv, dqc], axis=1).astype(BF16)
            in_dx = "mm_swa_in_dx"
            out["swa_w_in"] = _split(_mm(s["hn"], dproj, "tn", F32, "mm_swa_in_dw", pairs="b"), 2)
        dx, dx_b, dg = _mm(dproj, s["w_in"], "nt", F32, in_dx, epi="normbwd", norm=(s["x_in"], attn_norm_g[i], dx1),
                           pairs="" if i % 2 == 0 else "a")
        gains["attn_norm_g"][i] = dg[0]

        leaving[(i, "mixer")], token = _exchange_start([out[n] for n, _ in layer_names(i)[:-4]], True,
                                                       "exchange_grads_mixer_start_%d" % i)

    _, dg_mem = _rmsnorm_bwd(mem0, 0, D_MODEL, mem_norm_g, dmem_n, None, BF16, "rmsnorm_bwd_mem")
    gains = {n: jnp.stack(g) for n, g in gains.items()}
    gains["mem_norm_g"] = dg_mem[0]
    gains["final_norm_g"] = dg_final[0]

    result = {}

    def adamw_of(names, received):
        for n in names:
            parts = [received[(n, l)] for l in range(given[n].shape[0])]
            for kind, r in enumerate(_adamw(parts, given[n], given["m_" + n], given["v_" + n], "adamw_" + n)):
                result[(kind, n)] = r

    received = {}
    for i in reversed(range(DEPTH)):
        got = _exchange_wait(leaving[(i, "main")], dx, "exchange_grads_main_wait_%d" % i)
        received.update(zip(layer_names(i)[-4:], got))
    adamw_of(("mlp_w_up", "mlp_w_down", "w_o", "w_mem_kv"), received)
    for i in reversed(range(DEPTH)):
        got = _exchange_wait(leaving[(i, "mixer")], result[(0, "w_mem_kv")], "exchange_grads_mixer_wait_%d" % i)
        received.update(zip(layer_names(i)[:-4], got))
    adamw_of(("mla_w_in", "mla_w_uq", "mla_w_ukv", "swa_w_in"), received)

    rep_shapes = [given[n].shape for n in REPLICATED]
    rep_parts = _all_gather([_pack([gains[n] for n in REPLICATED], SLOT, 8, F32)], "gather_gain_grads")[0]
    rep_packed = [_pack([given[p + n] for n in REPLICATED], SLOT, 8, F32)[None] for p in ("", "m_", "v_")]
    for kind, r in enumerate(_adamw([rep_parts], *rep_packed, "adamw_gains")):
        for n, part in zip(REPLICATED, _unpack(r[0], rep_shapes)):
            result[(kind, n)] = part

    outs = [loss, dx.reshape(1, seq, D_MODEL)]
    for kind in range(4):
        outs += [result[(kind, n)] for n in WEIGHT_ORDER]
    return tuple(outs)
```

```python
import functools

import jax
import jax.numpy as jnp
from jax import lax
from jax.experimental import pallas as pl
from jax.experimental.pallas import tpu as pltpu

F32 = jnp.float32
BF16 = jnp.bfloat16

D_MODEL = 1024
N_MEM = 256
DEPTH = 4
SLOT = 128
HEAD_DIM = 64
MLA_HEADS = 12
MLA_QK = 96
MLA_Q_RANK = 384
MLA_KV_RANK = 256
SWA_HEADS = 12
SWA_KV_HEADS = 4
SWA_GROUP = 3
MEM_HEADS = 4
WINDOW = 128
EPS = 1e-6
NEG = -1e30
ROPE_THETA = 10000.0
N_DEV = 8

ADAM_LR = 0.001
ADAM_B1 = 0.9
ADAM_B2 = 0.999
ADAM_EPS = 1e-08
ADAM_WD = 0.01
ADAM_STEP = 10

TM = 1024
TM_ROPE = 512
TQ_MLA = 1024
MLA_PACK = 4
SWA_PACK = 4
TQ_CROSS = 4096
MM_VMEM_BUDGET = 38 * 1024 * 1024
ADAM_ROWS = 256
VMEM_LIMIT = 56 * 1024 * 1024

MESH_AXES = ("x", "y", "c")

LOG2_E = 1.4426950408889634
MLA_SCALE = MLA_QK ** -0.5
MLA_Q_SCALE = MLA_SCALE * LOG2_E

MLA_QOFF = (MLA_Q_RANK + SLOT + MLA_KV_RANK) // SLOT
SWA_QOFF = SWA_HEADS + 2 * SWA_KV_HEADS

SHARDED = (
    ("mla_w_in", 1), ("mla_w_uq", 2), ("mla_w_ukv", 2), ("swa_w_in", 2),
    ("w_mem_kv", 1), ("w_o", 1), ("mlp_w_up", 2), ("mlp_w_down", 1),
)
REPLICATED = ("attn_norm_g", "mlp_norm_g", "mem_norm_g", "final_norm_g",
              "mla_q_norm_g", "mla_kv_norm_g", "swa_sinks")
WEIGHT_ORDER = ("attn_norm_g", "mlp_norm_g", "mem_norm_g", "final_norm_g", "mla_w_in",
                "mla_q_norm_g", "mla_kv_norm_g", "mla_w_uq", "mla_w_ukv", "swa_w_in",
                "swa_sinks", "w_mem_kv", "w_o", "mlp_w_up", "mlp_w_down")


def _cparams():
    return pltpu.CompilerParams(vmem_limit_bytes=VMEM_LIMIT)


_DIMS = {"nn": (((1,), (0,)), ((), ())), "nt": (((1,), (1,)), ((), ())), "tn": (((0,), (0,)), ((), ()))}


def _compact(x):
    pairs = [x[:, 2 * j * SLOT:(2 * j + 1) * SLOT] + pltpu.roll(x[:, (2 * j + 1) * SLOT:(2 * j + 2) * SLOT], HEAD_DIM, 1)
             for j in range(x.shape[1] // (2 * SLOT))]
    return pairs[0] if len(pairs) == 1 else jnp.concatenate(pairs, axis=1)


def _expand(x):
    low = lax.broadcasted_iota(jnp.int32, (x.shape[0], SLOT), 1) < HEAD_DIM
    slots = []
    for j in range(x.shape[1] // SLOT):
        pair = x[:, j * SLOT:(j + 1) * SLOT]
        slots += [jnp.where(low, pair, 0.0), pltpu.roll(jnp.where(low, 0.0, pair), HEAD_DIM, 1)]
    return jnp.concatenate(slots, axis=1)


def _mm_tiles(m, n, k, a_bytes, b_bytes, o_bytes, extra_bytes, tm_fixed, tn_fixed):
    best = None
    for tm in ([tm_fixed] if tm_fixed else [t for t in range(4096, 0, -SLOT) if m % t == 0] or [m]):
        for tn in ([tn_fixed] if tn_fixed else [t for t in range(1024, 0, -SLOT) if n % t == 0] or [n]):
            need = 2 * (tm * k * a_bytes + k * tn * b_bytes + tm * tn * (o_bytes + extra_bytes))
            need += tm * tn * 4
            if need <= MM_VMEM_BUDGET and (best is None or tm * tn > best[0] * best[1]):
                best = (tm, tn)
    assert best is not None, (m, n, k)
    return best


def _mm(a, b, mode, out_dtype, name, res=None, aux=None, epi=None, b_blk=None, o_blk=None, after=None, norm=None,
        pairs="", second=None):
    if b_blk is not None:
        nb, br, bc = b.shape
        b_shape = (nb * br, bc) if b_blk == "rows" else (br, nb * bc)
    else:
        b_shape = b.shape
    assert not pairs or (b_blk is None and o_blk is None and not ("b" in pairs and mode == "nt"))
    a_shape = (a.shape[0], a.shape[1] // 2) if "a" in pairs else a.shape
    if "b" in pairs:
        b_shape = (b_shape[0], b_shape[1] // 2)
    if mode == "nn":
        (m, k), (k2, n) = a_shape, b_shape
    elif mode == "nt":
        (m, k), (n, k2) = a_shape, b_shape
    else:
        (k, m), (k2, n) = a_shape, b_shape
    assert k == k2, (a.shape, b_shape, mode)
    assert second is None or (mode == "nn" and b_blk is None and second[0].shape[0] == m and second[1].shape[1] == n)
    k_second = 0 if second is None else second[1].shape[0]
    k_blocked = b_blk is not None and (b_blk == "rows") == (mode != "nt")
    tn_fixed = None
    if b_blk is not None and not k_blocked:
        tn_fixed = br if b_blk == "rows" else bc
    if o_blk == "cols":
        tn_fixed = n // N_DEV
    tm_fixed = m // N_DEV if o_blk == "rows" else None
    has_res, has_aux, has_norm, has_normf = res is not None, aux is not None, epi == "normbwd", epi == "normfwd"
    assert o_blk is None or not (has_res or has_aux or has_norm or has_normf)
    n_out = 2 if epi == "relu2" else 1
    if has_norm:
        tn_fixed = n
        o_bytes, extra_bytes = 4 + 2, 4 + 4
    elif has_normf:
        tn_fixed = n
        o_bytes, extra_bytes = 4 + 2, (4 if has_res else 0)
    else:
        o_bytes = n_out * jnp.dtype(out_dtype).itemsize
        extra_bytes = (4 if has_res else 0) + (aux.dtype.itemsize if has_aux else 0)
    pa, pb, po = (2 if "a" in pairs else 1), (2 if "b" in pairs else 1), (2 if "o" in pairs else 1)
    tm, tn = _mm_tiles(m, n, k + k_second, a.dtype.itemsize * (3 if pa == 2 else 1),
                       b.dtype.itemsize * (3 if pb == 2 else 1), o_bytes * po, extra_bytes, tm_fixed, tn_fixed)
    dims = _DIMS[mode]
    if mode == "tn":
        a_spec = pl.BlockSpec((k, pa * tm), lambda i, j: (0, i))
    else:
        a_spec = pl.BlockSpec((tm, pa * k), lambda i, j: (i, 0))
    if b_blk is None:
        if mode == "nt":
            b_spec = pl.BlockSpec((tn, k), lambda i, j: (j, 0))
        else:
            b_spec = pl.BlockSpec((k, pb * tn), lambda i, j: (0, j))
    elif k_blocked and mode == "nt":
        b_spec = pl.BlockSpec((N_DEV, tn, bc), lambda i, j: (0, j, 0))
    elif k_blocked:
        b_spec = pl.BlockSpec((N_DEV, br, tn), lambda i, j: (0, 0, j))
    elif mode == "nt":
        b_spec = pl.BlockSpec((None, tn, k), lambda i, j: (j, 0, 0))
    else:
        b_spec = pl.BlockSpec((None, k, tn), lambda i, j: (j, 0, 0))
    if o_blk is None:
        o_spec = pl.BlockSpec((tm, po * tn), lambda i, j: (i, j))
        o_shape = (m, po * n)
    elif o_blk == "rows":
        o_spec = pl.BlockSpec((None, tm, tn), lambda i, j: (i, 0, j))
        o_shape = (N_DEV, tm, n)
    else:
        o_spec = pl.BlockSpec((None, tm, tn), lambda i, j: (j, i, 0))
        o_shape = (N_DEV, m, tn)

    def body(*refs):
        a_ref, b_ref = refs[0], refs[1]
        pos = 2
        res_ref = aux_ref = None
        if has_res:
            res_ref = refs[pos]
            pos += 1
        if has_aux:
            aux_ref = refs[pos]
            pos += 1
        if has_norm:
            x_ref, g_ref, dres_ref = refs[pos:pos + 3]
            pos += 3
        if has_normf:
            g_ref = refs[pos]
            pos += 1
        if second is not None:
            a2_ref, b2_ref = refs[pos:pos + 2]
            pos += 2
        if after is not None:
            pos += 1
        outs = refs[pos:]
        if k_blocked and mode == "nt":
            r = None
            for d in range(N_DEV):
                part = lax.dot_general(a_ref[:, d * bc:(d + 1) * bc].astype(BF16), b_ref[d].astype(BF16), dims,
                                       preferred_element_type=F32)
                r = part if r is None else r + part
        else:
            bv = b_ref[...].reshape(k, tn) if k_blocked else b_ref[...]
            av = _compact(a_ref[...].astype(F32)) if pa == 2 else a_ref[...]
            bv = _compact(bv.astype(F32)) if pb == 2 else bv
            r = lax.dot_general(av.astype(BF16), bv.astype(BF16), dims, preferred_element_type=F32)
        if second is not None:
            av2 = _compact(a2_ref[...].astype(F32)) if pa == 2 else a2_ref[...]
            r = r + lax.dot_general(av2.astype(BF16), b2_ref[...].astype(BF16), dims, preferred_element_type=F32)
        if po == 2:
            r = _expand(r)
        if epi == "relu2":
            r = jnp.maximum(r, 0.0)
            outs[0][...] = r.astype(outs[0].dtype)
            outs[1][...] = (r * r).astype(outs[1].dtype)
        elif has_norm:
            xv = x_ref[...]
            rs = lax.rsqrt(jnp.mean(xv * xv, axis=1, keepdims=True) + EPS)
            xh = xv * rs
            dxh = r * g_ref[...]
            dx = rs * (dxh - xh * jnp.mean(dxh * xh, axis=1, keepdims=True)) + dres_ref[...]
            outs[0][...] = dx
            outs[1][...] = dx.astype(BF16)

            @pl.when(pl.program_id(0) == 0)
            def _():
                outs[2][...] = jnp.zeros_like(outs[2])

            outs[2][...] += jnp.sum(r * xh, axis=0, keepdims=True)
        else:
            if epi == "mul2aux":
                r = r * (2.0 * aux_ref[...].astype(F32))
            if has_res:
                r = r + res_ref[...]
            outs[0][...] = r.astype(outs[0].dtype)
            if has_normf:
                rs = lax.rsqrt(jnp.mean(r * r, axis=1, keepdims=True) + EPS)
                outs[1][...] = (r * rs * g_ref[...]).astype(BF16)

    in_specs = [a_spec, b_spec]
    args = [a, b]
    if has_res:
        in_specs.append(o_spec)
        args.append(res)
    if has_aux:
        in_specs.append(o_spec)
        args.append(aux)
    vec_spec = pl.BlockSpec((1, n), lambda i, j: (0, 0))
    if has_norm:
        in_specs += [o_spec, vec_spec, o_spec]
        args += [norm[0], norm[1].reshape(1, n), norm[2]]
    if has_normf:
        in_specs.append(vec_spec)
        args.append(norm.reshape(1, n))
    if second is not None:
        in_specs += [pl.BlockSpec((tm, pa * k_second), lambda i, j: (i, 0)),
                     pl.BlockSpec((k_second, tn), lambda i, j: (0, j))]
        args += list(second)
    if after is not None:
        in_specs.append(pl.BlockSpec(memory_space=pl.ANY))
        args.append(after)
    if has_norm:
        out_specs = [o_spec, o_spec, vec_spec]
        out_shape = [jax.ShapeDtypeStruct(o_shape, F32), jax.ShapeDtypeStruct(o_shape, BF16),
                     jax.ShapeDtypeStruct((1, n), F32)]
    elif has_normf:
        out_specs = [o_spec, o_spec]
        out_shape = [jax.ShapeDtypeStruct(o_shape, out_dtype), jax.ShapeDtypeStruct(o_shape, BF16)]
    else:
        out_specs = [o_spec] * n_out
        out_shape = [jax.ShapeDtypeStruct(o_shape, out_dtype)] * n_out
    out = pl.pallas_call(
        body, name=name, grid=(m // tm, n // tn),
        in_specs=in_specs, out_specs=out_specs, out_shape=out_shape, compiler_params=_cparams(),
    )(*args)
    return out if len(out) > 1 else out[0]


def _rmsnorm_fwd(xarr, colblk, width, g, name, after=None):
    rows = xarr.shape[0]
    tm = min(TM, rows)

    def body(x_ref, g_ref, *rest):
        y_ref = rest[-1]
        x = x_ref[...].astype(F32)
        r = lax.rsqrt(jnp.mean(x * x, axis=1, keepdims=True) + EPS)
        y_ref[...] = (x * r * g_ref[...]).astype(y_ref.dtype)

    in_specs = [pl.BlockSpec((tm, width), lambda i: (i, colblk)), pl.BlockSpec((1, width), lambda i: (0, 0))]
    args = [xarr, g.reshape(1, width)]
    if after is not None:
        in_specs.append(pl.BlockSpec(memory_space=pl.ANY))
        args.append(after)
    return pl.pallas_call(
        body, name=name, grid=(rows // tm,), in_specs=in_specs,
        out_specs=pl.BlockSpec((tm, width), lambda i: (i, 0)),
        out_shape=jax.ShapeDtypeStruct((rows, width), BF16), compiler_params=_cparams(),
    )(*args)


def _rmsnorm_bwd(xarr, colblk, width, g, dy, dres, out_dtype, name):
    rows = xarr.shape[0]
    tm = min(TM, rows)
    has_res = dres is not None

    def body(*refs):
        x_ref, g_ref, dy_ref = refs[0], refs[1], refs[2]
        dres_ref = refs[3] if has_res else None
        dx_ref, dg_ref = refs[-2], refs[-1]
        x = x_ref[...].astype(F32)
        dyv = dy_ref[...].astype(F32)
        r = lax.rsqrt(jnp.mean(x * x, axis=1, keepdims=True) + EPS)
        xh = x * r
        dxh = dyv * g_ref[...]
        dx = r * (dxh - xh * jnp.mean(dxh * xh, axis=1, keepdims=True))
        if has_res:
            dx = dx + dres_ref[...]
        dx_ref[...] = dx.astype(dx_ref.dtype)

        @pl.when(pl.program_id(0) == 0)
        def _():
            dg_ref[...] = jnp.zeros_like(dg_ref)

        dg_ref[...] += jnp.sum(dyv * xh, axis=0, keepdims=True)

    row_spec = pl.BlockSpec((tm, width), lambda i: (i, 0))
    vec_spec = pl.BlockSpec((1, width), lambda i: (0, 0))
    in_specs = [pl.BlockSpec((tm, width), lambda i: (i, colblk)), vec_spec, row_spec]
    args = [xarr, g.reshape(1, width), dy]
    if has_res:
        in_specs.append(row_spec)
        args.append(dres)
    return pl.pallas_call(
        body, name=name, grid=(rows // tm,), in_specs=in_specs, out_specs=[row_spec, vec_spec],
        out_shape=[jax.ShapeDtypeStruct((rows, width), out_dtype), jax.ShapeDtypeStruct((1, width), F32)],
        compiler_params=_cparams(),
    )(*args)


def _loss_head(x, g, tgt):
    rows, width = x.shape
    tm = min(TM, rows)

    def body(x_ref, g_ref, t_ref, dx_ref, dxb_ref, dg_ref, loss_ref):
        xv = x_ref[...]
        gv = g_ref[...]
        r = lax.rsqrt(jnp.mean(xv * xv, axis=1, keepdims=True) + EPS)
        xh = xv * r
        err = xh * gv - t_ref[...]
        part = 0.5 * jnp.sum(jnp.mean(err * err, axis=1, keepdims=True), axis=0, keepdims=True)
        dyv = err * (1.0 / width)
        dxh = dyv * gv
        dxv = r * (dxh - xh * jnp.mean(dxh * xh, axis=1, keepdims=True))
        dx_ref[...] = dxv
        dxb_ref[...] = dxv.astype(BF16)

        @pl.when(pl.program_id(0) == 0)
        def _():
            dg_ref[...] = jnp.zeros_like(dg_ref)
            loss_ref[...] = jnp.zeros_like(loss_ref)

        dg_ref[...] += jnp.sum(dyv * xh, axis=0, keepdims=True)
        loss_ref[...] += jnp.broadcast_to(part, loss_ref.shape)

    row_spec = pl.BlockSpec((tm, width), lambda i: (i, 0))
    vec_spec = pl.BlockSpec((1, width), lambda i: (0, 0))
    return pl.pallas_call(
        body, name="loss_head", grid=(rows // tm,), in_specs=[row_spec, vec_spec, row_spec],
        out_specs=[row_spec, row_spec, vec_spec, pl.BlockSpec((1, SLOT), lambda i: (0, 0))],
        out_shape=[jax.ShapeDtypeStruct((rows, width), F32), jax.ShapeDtypeStruct((rows, width), BF16),
                   jax.ShapeDtypeStruct((1, width), F32), jax.ShapeDtypeStruct((1, SLOT), F32)],
        compiler_params=_cparams(),
    )(x, g.reshape(1, width), tgt)


def _lane_consts():
    half = 16
    inv = ROPE_THETA ** (-(jnp.arange(half, dtype=F32) * 2.0) / 32)
    lane = jnp.arange(SLOT)
    first = (lane >= 64) & (lane < 80)
    second = (lane >= 80) & (lane < 96)
    inv_lane = jnp.where(first | second, inv[(lane - 64) % half], 0.0)
    rows = [inv_lane, (lane < 64).astype(F32), first.astype(F32), second.astype(F32)]
    rows += [jnp.zeros((SLOT,), F32)] * 4
    return jnp.stack(rows).astype(F32)


def _rope_tables(pos_col, consts):
    rows = pos_col.shape[0]
    tm = min(TM, rows)

    def body(p_ref, k_ref, c_ref, s1_ref, s2_ref):
        ang = p_ref[...] * k_ref[0:1, :]
        cos, sin = jnp.cos(ang), jnp.sin(ang)
        first, second = k_ref[2:3, :], k_ref[3:4, :]
        c_ref[...] = k_ref[1:2, :] + (first + second) * cos
        s1_ref[...] = -first * sin
        s2_ref[...] = second * sin

    spec = pl.BlockSpec((tm, SLOT), lambda i: (i, 0))
    shp = jax.ShapeDtypeStruct((rows, SLOT), F32)
    return pl.pallas_call(
        body, name="rope_tables", grid=(rows // tm,),
        in_specs=[pl.BlockSpec((tm, 1), lambda i: (i, 0)), pl.BlockSpec((8, SLOT), lambda i: (0, 0))],
        out_specs=[spec, spec, spec], out_shape=[shp, shp, shp], compiler_params=_cparams(),
    )(pos_col, consts)


def _rot(xv, c, s1, s2):
    return xv * c + pltpu.roll(xv, SLOT - 16, 1) * s1 + pltpu.roll(xv, 16, 1) * s2


def _rot_t(dy, c, s1, s2):
    return dy * c + pltpu.roll(dy * s1, 16, 1) + pltpu.roll(dy * s2, SLOT - 16, 1)


def _mla_rope_fwd(qraw, kvraw, proj, tabs):
    rows = qraw.shape[0]
    tm = min(TM_ROPE, rows)
    hw = MLA_HEADS * SLOT

    def body(q_ref, kv_ref, kr_ref, c_ref, s1_ref, s2_ref, qo, ko, vo):
        c, s1, s2 = c_ref[...], s1_ref[...], s2_ref[...]
        kr = _rot(kr_ref[...], c, s1, s2)
        low = lax.broadcasted_iota(jnp.int32, (tm, SLOT), 1) < HEAD_DIM
        for h in range(MLA_HEADS):
            sl = slice(h * SLOT, (h + 1) * SLOT)
            qo[:, sl] = (_rot(q_ref[:, sl], c, s1, s2) * MLA_Q_SCALE).astype(BF16)
            kvh = kv_ref[:, sl]
            ko[:, sl] = (jnp.where(low, kvh, 0.0) + kr).astype(BF16)
            vo[:, sl] = pltpu.roll(jnp.where(low, 0.0, kvh), HEAD_DIM, 1).astype(BF16)

    tab = pl.BlockSpec((tm, SLOT), lambda i: (i, 0))
    wide = pl.BlockSpec((tm, hw), lambda i: (i, 0))
    shp = jax.ShapeDtypeStruct((rows, hw), BF16)
    return pl.pallas_call(
        body, name="mla_rope_fwd", grid=(rows // tm,),
        in_specs=[wide, wide, pl.BlockSpec((tm, SLOT), lambda i: (i, 3)),
                  tab, tab, tab],
        out_specs=[wide, wide, wide], out_shape=[shp, shp, shp], compiler_params=_cparams(),
    )(qraw, kvraw, proj, *tabs)


def _mla_rope_bwd(dq, dk, dv, tabs, consts):
    rows = dq.shape[0]
    tm = min(TM_ROPE, rows)
    hw = MLA_HEADS * SLOT

    def body(dq_ref, dk_ref, dv_ref, c_ref, s1_ref, s2_ref, k_ref, dqo, dkvo, dkro):
        c, s1, s2 = c_ref[...], s1_ref[...], s2_ref[...]
        ksum = jnp.zeros((tm, SLOT), F32)
        low = lax.broadcasted_iota(jnp.int32, (tm, SLOT), 1) < HEAD_DIM
        for h in range(MLA_HEADS):
            sl = slice(h * SLOT, (h + 1) * SLOT)
            dqo[:, sl] = _rot_t(dq_ref[:, sl], c, s1, s2).astype(BF16)
            dkh = dk_ref[:, sl]
            ksum = ksum + dkh
            dvh = pltpu.roll(jnp.where(low, dv_ref[:, sl].astype(F32), 0.0), HEAD_DIM, 1)
            dkvo[:, sl] = (jnp.where(low, dkh, 0.0) + dvh).astype(BF16)
        dkro[...] = _rot_t(ksum, c, s1, s2) * (k_ref[2:3, :] + k_ref[3:4, :])

    tab = pl.BlockSpec((tm, SLOT), lambda i: (i, 0))
    wide = pl.BlockSpec((tm, hw), lambda i: (i, 0))
    return pl.pallas_call(
        body, name="mla_rope_bwd", grid=(rows // tm,),
        in_specs=[wide, wide, wide, tab, tab, tab, pl.BlockSpec((8, SLOT), lambda i: (0, 0))],
        out_specs=[wide, wide, tab],
        out_shape=[jax.ShapeDtypeStruct((rows, hw), BF16), jax.ShapeDtypeStruct((rows, hw), BF16),
                   jax.ShapeDtypeStruct((rows, SLOT), F32)],
        compiler_params=_cparams(),
    )(dq, dk, dv, *tabs, consts)


def _nt(a, b):
    return lax.dot_general(a, b, _DIMS["nt"], preferred_element_type=F32)


def _tn(a, b):
    return lax.dot_general(a, b, _DIMS["tn"], preferred_element_type=F32)


def _nn(a, b):
    return lax.dot_general(a, b, _DIMS["nn"], preferred_element_type=F32)


def _mla_attn_fwd(q, k, v):
    rows = q.shape[0]
    t = min(TQ_MLA, rows)
    nt = rows // t
    wide = MLA_PACK * SLOT

    def body(q_ref, k_ref, v_ref, o_ref, lse_ref, m_sc, l_sc, acc_sc):
        i, j = pl.program_id(1), pl.program_id(2)

        @pl.when(j == 0)
        def _():
            m_sc[...] = jnp.full_like(m_sc, NEG)
            l_sc[...] = jnp.zeros_like(l_sc)
            acc_sc[...] = jnp.zeros_like(acc_sc)

        def step(diagonal):
            for hh in range(MLA_PACK):
                sl = slice(hh * SLOT, (hh + 1) * SLOT)
                s = _nt(k_ref[:, sl], q_ref[:, sl])
                if diagonal:
                    key = lax.broadcasted_iota(jnp.int32, (t, t), 0)
                    s = jnp.where(key <= lax.broadcasted_iota(jnp.int32, (t, t), 1), s, NEG)
                m_prev = m_sc[hh]
                m_new = jnp.maximum(m_prev, jnp.max(s, axis=0, keepdims=True))
                p = jnp.exp2(s - m_new)
                alpha = jnp.exp2(m_prev - m_new)
                l_new = alpha * l_sc[hh] + jnp.sum(p, axis=0, keepdims=True)
                acc = alpha * acc_sc[hh] + _tn(v_ref[:, sl], p.astype(BF16))
                if diagonal:
                    o_ref[:, sl] = (acc / l_new).T.astype(o_ref.dtype)
                    lse_ref[hh:hh + 1, :] = m_new + jnp.log(l_new) * LOG2_E
                else:
                    m_sc[hh] = m_new
                    l_sc[hh] = l_new
                    acc_sc[hh] = acc

        @pl.when(j < i)
        def _():
            step(False)

        @pl.when(j == i)
        def _():
            lse_ref[...] = jnp.zeros_like(lse_ref)
            step(True)

    q_spec = pl.BlockSpec((t, wide), lambda h, i, j: (i, h))
    kv_spec = pl.BlockSpec((t, wide), lambda h, i, j: (jnp.minimum(j, i), h))
    return pl.pallas_call(
        body, name="mla_attn_fwd", grid=(MLA_HEADS // MLA_PACK, nt, nt),
        in_specs=[q_spec, kv_spec, kv_spec],
        out_specs=[q_spec, pl.BlockSpec((None, 8, t), lambda h, i, j: (h, 0, i))],
        out_shape=[jax.ShapeDtypeStruct(q.shape, BF16),
                   jax.ShapeDtypeStruct((MLA_HEADS // MLA_PACK, 8, rows), F32)],
        scratch_shapes=[pltpu.VMEM((MLA_PACK, 1, t), F32), pltpu.VMEM((MLA_PACK, 1, t), F32),
                        pltpu.VMEM((MLA_PACK, SLOT, t), F32)],
        compiler_params=_cparams(),
    )(q, k, v)


def _mla_delta(o, do):
    rows = o.shape[0]
    t = rows
    wide = MLA_PACK * SLOT

    def body(o_ref, do_ref, d_ref):
        d_ref[...] = jnp.zeros_like(d_ref)
        ones = jnp.ones((8, SLOT), BF16)
        for hh in range(MLA_PACK):
            sl = slice(hh * SLOT, (hh + 1) * SLOT)
            prod = do_ref[:, sl].astype(F32) * o_ref[:, sl].astype(F32)
            high = prod.astype(BF16)
            low = (prod - high.astype(F32)).astype(BF16)
            d_ref[hh:hh + 1, :] = (_nt(ones, high) + _nt(ones, low))[0:1, :]

    spec = pl.BlockSpec((t, wide), lambda h, i: (i, h))
    return pl.pallas_call(
        body, name="mla_delta", grid=(MLA_HEADS // MLA_PACK, rows // t), in_specs=[spec, spec],
        out_specs=pl.BlockSpec((None, 8, t), lambda h, i: (h, 0, i)),
        out_shape=jax.ShapeDtypeStruct((MLA_HEADS // MLA_PACK, 8, rows), F32), compiler_params=_cparams(),
    )(o, do)


def _mla_attn_bwd(q, k, v, do, lse, delta, after):
    rows = q.shape[0]
    t = min(TQ_MLA, rows)
    nt = rows // t
    wide = MLA_PACK * SLOT

    def body(q_ref, k_ref, v_ref, do_ref, lse_ref, delta_ref, after_ref, dq_ref, dk_ref, dv_ref, dk_sc, dv_sc):
        j, i = pl.program_id(1), pl.program_id(2)

        @pl.when((j == 0) & (i == 0))
        def _():
            dq_ref[...] = jnp.zeros_like(dq_ref)

        @pl.when(i == 0)
        def _():
            dk_sc[...] = jnp.zeros_like(dk_sc)
            dv_sc[...] = jnp.zeros_like(dv_sc)

        def chunk(hh, rows, keys, masked):
            sl = slice(hh * SLOT, (hh + 1) * SLOT)
            n_rows = rows.stop - rows.start
            qv, kv, dov = q_ref[rows, sl], k_ref[keys, sl], do_ref[rows, sl]
            s = _nt(kv, qv)
            if masked:
                shp = (keys.stop - keys.start, n_rows)
                s = jnp.where(keys.start + lax.broadcasted_iota(jnp.int32, shp, 0)
                              <= rows.start + lax.broadcasted_iota(jnp.int32, shp, 1), s, NEG)
            p = jnp.exp2(s - lse_ref[hh:hh + 1, rows])
            dp = _nt(v_ref[keys, sl], dov)
            ds = (p * (dp - delta_ref[hh:hh + 1, rows])).astype(BF16)
            dv_sc[keys, sl] += _nn(p.astype(BF16), dov)
            dk_sc[keys, sl] += _nn(ds, qv)
            r0 = pl.multiple_of(i * t + rows.start, n_rows)
            dq_ref[pl.ds(r0, n_rows), sl] += _tn(ds, kv) * MLA_SCALE

        @pl.when(i > j)
        def _():
            for hh in range(MLA_PACK):
                chunk(hh, slice(0, t), slice(0, t), False)

        @pl.when(i == j)
        def _():
            for hh in range(MLA_PACK):
                chunk(hh, slice(0, t), slice(0, t // 2), True)
                chunk(hh, slice(t // 2, t), slice(t // 2, t), True)

        @pl.when(i == nt - 1)
        def _():
            dk_ref[...] = dk_sc[...] * (1.0 / LOG2_E)
            dv_ref[...] = dv_sc[...].astype(dv_ref.dtype)

    q_spec = pl.BlockSpec((t, wide), lambda h, j, i: (jnp.maximum(i, j), h))
    kv_spec = pl.BlockSpec((t, wide), lambda h, j, i: (j, h))
    row_spec = pl.BlockSpec((None, 8, t), lambda h, j, i: (h, 0, jnp.maximum(i, j)))
    head_spec = pl.BlockSpec((rows, wide), lambda h, j, i: (0, h))
    shp = jax.ShapeDtypeStruct(q.shape, F32)
    return pl.pallas_call(
        body, name="mla_attn_bwd", grid=(MLA_HEADS // MLA_PACK, nt, nt),
        in_specs=[q_spec, kv_spec, kv_spec, q_spec, row_spec, row_spec, pl.BlockSpec(memory_space=pl.ANY)],
        out_specs=[head_spec, kv_spec, kv_spec], out_shape=[shp, shp, jax.ShapeDtypeStruct(q.shape, BF16)],
        scratch_shapes=[pltpu.VMEM((t, wide), F32), pltpu.VMEM((t, wide), F32)],
        compiler_params=_cparams(),
    )(q, k, v, do, lse, delta, after)


def _swa_specs(t):
    def prev(i):
        return jnp.maximum(i - 1, 0)
    kw = SWA_PACK * SLOT
    k0, v0 = SWA_HEADS // SWA_PACK, (SWA_HEADS + SWA_KV_HEADS) // SWA_PACK
    q3 = pl.BlockSpec((t, SWA_PACK * SWA_GROUP * SLOT), lambda h, i: (i, h))
    kp = pl.BlockSpec((t, kw), lambda h, i: (prev(i), k0 + h))
    kc = pl.BlockSpec((t, kw), lambda h, i: (i, k0 + h))
    vp = pl.BlockSpec((t, kw), lambda h, i: (prev(i), v0 + h))
    vc = pl.BlockSpec((t, kw), lambda h, i: (i, v0 + h))
    pcol = pl.BlockSpec((t, 1), lambda h, i: (i, 0))
    prow_p = pl.BlockSpec((1, t), lambda h, i: (0, prev(i)))
    prow_c = pl.BlockSpec((1, t), lambda h, i: (0, i))
    return [q3, kp, kc, vp, vc, pcol, prow_p, prow_c]


def _stack(ref, first):
    return jnp.concatenate([ref[:, (first + g) * SLOT:(first + g + 1) * SLOT] for g in range(SWA_GROUP)], axis=0)


def _swa_logits(q3, kp, kc, pq, pkp, pkc, slope_ref, kvh, i, t):
    r = lax.broadcasted_iota(jnp.int32, (t, t), 0)
    c = lax.broadcasted_iota(jnp.int32, (t, t), 1)
    ok_c = c <= r
    ok_p = (c - r) > jnp.where(i > 0, 0, t)
    dist_p, dist_c = pq - pkp, pq - pkc
    s_p3 = _nt(q3, kp) * (HEAD_DIM ** -0.5)
    s_c3 = _nt(q3, kc) * (HEAD_DIM ** -0.5)
    out = []
    for g in range(SWA_GROUP):
        slope = slope_ref[kvh * SWA_GROUP + g]
        rows = slice(g * t, (g + 1) * t)
        out.append((jnp.where(ok_p, s_p3[rows] - slope * dist_p, NEG),
                    jnp.where(ok_c, s_c3[rows] - slope * dist_c, NEG)))
    return out


def _swa_attn_fwd(proj, pos_col, pos_row, slopes, sinks):
    rows = proj.shape[0]
    t = WINDOW
    hw = SWA_HEADS * SLOT

    def body(slope_ref, sink_ref, q_ref, kp_ref, kc_ref, vp_ref, vc_ref, pq_ref, pkp_ref, pkc_ref, o_ref, lse_ref):
        i = pl.program_id(1)
        lane = lax.broadcasted_iota(jnp.int32, (t, SLOT), 1)
        lse_all = jnp.zeros((t, SLOT), F32)
        for kv in range(SWA_PACK):
            kvh = pl.program_id(0) * SWA_PACK + kv
            ksl = slice(kv * SLOT, (kv + 1) * SLOT)
            logits = _swa_logits(_stack(q_ref, kv * SWA_GROUP), kp_ref[:, ksl], kc_ref[:, ksl], pq_ref[...],
                                 pkp_ref[...], pkc_ref[...], slope_ref, kvh, i, t)
            e_p, e_c, norm = [], [], []
            for g, (s_p, s_c) in enumerate(logits):
                sl = slice((kv * SWA_GROUP + g) * SLOT, (kv * SWA_GROUP + g + 1) * SLOT)
                sink = sink_ref[kvh * SWA_GROUP + g]
                m = jnp.maximum(jnp.maximum(jnp.max(s_p, axis=1, keepdims=True),
                                            jnp.max(s_c, axis=1, keepdims=True)), sink)
                ep, ec = jnp.exp(s_p - m), jnp.exp(s_c - m)
                l = jnp.sum(ep, axis=1, keepdims=True) + jnp.sum(ec, axis=1, keepdims=True) + jnp.exp(sink - m)
                e_p.append(ep.astype(BF16))
                e_c.append(ec.astype(BF16))
                norm.append(l)
                lse_all = jnp.where(lane == kv * SWA_GROUP + g, m + jnp.log(l), lse_all)
            acc = (_nn(jnp.concatenate(e_p, axis=0), vp_ref[:, ksl])
                   + _nn(jnp.concatenate(e_c, axis=0), vc_ref[:, ksl]))
            for g in range(SWA_GROUP):
                sl = slice((kv * SWA_GROUP + g) * SLOT, (kv * SWA_GROUP + g + 1) * SLOT)
                o_ref[:, sl] = (acc[g * t:(g + 1) * t] / norm[g]).astype(o_ref.dtype)
        lse_ref[...] = lse_all

    smem = pl.BlockSpec(memory_space=pltpu.SMEM)
    out_spec = pl.BlockSpec((t, SWA_PACK * SWA_GROUP * SLOT), lambda h, i: (i, h))
    return pl.pallas_call(
        body, name="swa_attn_fwd", grid=(SWA_KV_HEADS // SWA_PACK, rows // t),
        in_specs=[smem, smem] + _swa_specs(t),
        out_specs=[out_spec, pl.BlockSpec((t, SLOT), lambda h, i: (i, h))],
        out_shape=[jax.ShapeDtypeStruct((rows, hw), BF16),
                   jax.ShapeDtypeStruct((rows, SWA_KV_HEADS // SWA_PACK * SLOT), F32)],
        compiler_params=_cparams(),
    )(slopes, sinks, proj, proj, proj, proj, proj, pos_col, pos_row, pos_row)


def _swa_attn_bwd(proj, o, do, lse, pos_col, pos_row, slopes, sinks, after):
    rows = proj.shape[0]
    t = WINDOW
    hw = SWA_HEADS * SLOT
    scale = HEAD_DIM ** -0.5

    def body(slope_ref, sink_ref, q_ref, kp_ref, kc_ref, vp_ref, vc_ref, pq_ref, pkp_ref, pkc_ref,
             o_ref, do_ref, lse_ref, after_ref, dq_ref, dk_out, dv_out, dsink_ref, dk_ref, dv_ref):
        i = pl.program_id(1)

        @pl.when(i == 0)
        def _():
            dk_ref[...] = jnp.zeros_like(dk_ref)
            dv_ref[...] = jnp.zeros_like(dv_ref)
            dsink_ref[...] = jnp.zeros_like(dsink_ref)

        r_c = pl.multiple_of(i * t, t)
        r_p = pl.multiple_of(jnp.maximum(i - 1, 0) * t, t)
        for kv in range(SWA_PACK):
            kvh = pl.program_id(0) * SWA_PACK + kv
            ksl = slice(kv * SLOT, (kv + 1) * SLOT)
            q3, do3 = _stack(q_ref, kv * SWA_GROUP), _stack(do_ref, kv * SWA_GROUP)
            logits = _swa_logits(q3, kp_ref[:, ksl], kc_ref[:, ksl], pq_ref[...], pkp_ref[...], pkc_ref[...],
                                 slope_ref, kvh, i, t)
            dp_p3, dp_c3 = _nt(do3, vp_ref[:, ksl]), _nt(do3, vc_ref[:, ksl])
            p_p, p_c, ds_p, ds_c = [], [], [], []
            for g, (s_p, s_c) in enumerate(logits):
                head = kv * SWA_GROUP + g
                sl = slice(head * SLOT, (head + 1) * SLOT)
                rws = slice(g * t, (g + 1) * t)
                lse_g = lse_ref[:, head:head + 1]
                pp, pc = jnp.exp(s_p - lse_g), jnp.exp(s_c - lse_g)
                delta = jnp.sum(do_ref[:, sl].astype(F32) * o_ref[:, sl].astype(F32), axis=1, keepdims=True)
                p_p.append(pp.astype(BF16))
                p_c.append(pc.astype(BF16))
                ds_p.append((pp * (dp_p3[rws] - delta)).astype(BF16))
                ds_c.append((pc * (dp_c3[rws] - delta)).astype(BF16))
                sink = sink_ref[kvh * SWA_GROUP + g]
                dsink = -jnp.sum(jnp.exp(sink - lse_g) * delta, axis=0, keepdims=True)
                dsink_ref[head * 8:(head + 1) * 8, :] += jnp.broadcast_to(dsink, (8, SLOT))
            p_p3, p_c3 = jnp.concatenate(p_p, axis=0), jnp.concatenate(p_c, axis=0)
            ds_p3, ds_c3 = jnp.concatenate(ds_p, axis=0), jnp.concatenate(ds_c, axis=0)
            dq3 = (_nn(ds_p3, kp_ref[:, ksl]) + _nn(ds_c3, kc_ref[:, ksl])) * scale
            for g in range(SWA_GROUP):
                head = kv * SWA_GROUP + g
                dq_ref[:, head * SLOT:(head + 1) * SLOT] = dq3[g * t:(g + 1) * t].astype(dq_ref.dtype)
            dk_ref[pl.ds(r_c, t), ksl] += _tn(ds_c3, q3) * scale
            dv_ref[pl.ds(r_c, t), ksl] += _tn(p_c3, do3)
            dk_ref[pl.ds(r_p, t), ksl] += _tn(ds_p3, q3) * scale
            dv_ref[pl.ds(r_p, t), ksl] += _tn(p_p3, do3)

        @pl.when(i == pl.num_programs(1) - 1)
        def _():
            dk_out[...] = dk_ref[...].astype(dk_out.dtype)
            dv_out[...] = dv_ref[...].astype(dv_out.dtype)

    smem = pl.BlockSpec(memory_space=pltpu.SMEM)
    qlike = pl.BlockSpec((t, SWA_PACK * SWA_GROUP * SLOT), lambda h, i: (i, h))
    kv_out = pl.BlockSpec((rows, SWA_PACK * SLOT), lambda h, i: (0, h))
    return pl.pallas_call(
        body, name="swa_attn_bwd", grid=(SWA_KV_HEADS // SWA_PACK, rows // t),
        in_specs=[smem, smem] + _swa_specs(t) + [qlike, qlike, pl.BlockSpec((t, SLOT), lambda h, i: (i, h)),
                                                 pl.BlockSpec(memory_space=pl.ANY)],
        out_specs=[qlike, kv_out, kv_out,
                   pl.BlockSpec((SWA_PACK * SWA_GROUP * 8, SLOT), lambda h, i: (h, 0))],
        out_shape=[jax.ShapeDtypeStruct((rows, hw), BF16), jax.ShapeDtypeStruct((rows, SWA_KV_HEADS * SLOT), BF16),
                   jax.ShapeDtypeStruct((rows, SWA_KV_HEADS * SLOT), BF16),
                   jax.ShapeDtypeStruct((SWA_HEADS * 8, SLOT), F32)],
        scratch_shapes=[pltpu.VMEM((rows, SWA_PACK * SLOT), F32), pltpu.VMEM((rows, SWA_PACK * SLOT), F32)],
        compiler_params=_cparams(),
    )(slopes, sinks, proj, proj, proj, proj, proj, pos_col, pos_row, pos_row, o, do, lse, after)


def _cross_attn_fwd(proj, qoff, kvmem):
    rows = proj.shape[0]
    t = min(TQ_CROSS, rows)

    def body(q_ref, k_ref, v_ref, o_ref):
        s = _nt(k_ref[...], q_ref[...].astype(BF16)) * (HEAD_DIM ** -0.5)
        e = jnp.exp(s - jnp.max(s, axis=0, keepdims=True))
        p = e / jnp.sum(e, axis=0, keepdims=True)
        o_ref[...] = _tn(v_ref[...], p.astype(BF16)).T.astype(o_ref.dtype)

    return pl.pallas_call(
        body, name="cross_attn_fwd", grid=(rows // t, MEM_HEADS),
        in_specs=[pl.BlockSpec((t, SLOT), lambda i, h: (i, qoff + h)),
                  pl.BlockSpec((N_MEM, SLOT), lambda i, h: (0, h)),
                  pl.BlockSpec((N_MEM, SLOT), lambda i, h: (0, MEM_HEADS + h))],
        out_specs=pl.BlockSpec((t, SLOT), lambda i, h: (i, h)),
        out_shape=jax.ShapeDtypeStruct((rows, MEM_HEADS * SLOT), BF16), compiler_params=_cparams(),
    )(proj, kvmem, kvmem)


def _cross_attn_bwd(proj, qoff, kvmem, do, do_off):
    rows = proj.shape[0]
    t = min(TQ_CROSS, rows)
    scale = HEAD_DIM ** -0.5

    def body(q_ref, k_ref, v_ref, do_ref, dq_ref, dk_ref, dv_ref):
        @pl.when(pl.program_id(1) == 0)
        def _():
            dk_ref[...] = jnp.zeros_like(dk_ref)
            dv_ref[...] = jnp.zeros_like(dv_ref)

        qv, kv, dov = q_ref[...].astype(BF16), k_ref[...], do_ref[...]
        s = _nt(kv, qv) * scale
        e = jnp.exp(s - jnp.max(s, axis=0, keepdims=True))
        p = e / jnp.sum(e, axis=0, keepdims=True)
        dp = _nt(v_ref[...], dov)
        ds = (p * (dp - jnp.sum(p * dp, axis=0, keepdims=True))).astype(BF16)
        dq_ref[...] = (_tn(ds, kv) * scale).astype(dq_ref.dtype)
        dk_ref[...] += _nn(ds, qv) * scale
        dv_ref[...] += _nn(p.astype(BF16), dov)

    mem_out = pl.BlockSpec((N_MEM, SLOT), lambda h, i: (0, h))
    return pl.pallas_call(
        body, name="cross_attn_bwd", grid=(MEM_HEADS, rows // t),
        in_specs=[pl.BlockSpec((t, SLOT), lambda h, i: (i, qoff + h)),
                  pl.BlockSpec((N_MEM, SLOT), lambda h, i: (0, h)),
                  pl.BlockSpec((N_MEM, SLOT), lambda h, i: (0, MEM_HEADS + h)),
                  pl.BlockSpec((t, SLOT), lambda h, i: (i, do_off + h))],
        out_specs=[pl.BlockSpec((t, SLOT), lambda h, i: (i, h)), mem_out, mem_out],
        out_shape=[jax.ShapeDtypeStruct((rows, MEM_HEADS * SLOT), BF16),
                   jax.ShapeDtypeStruct((N_MEM, MEM_HEADS * SLOT), F32),
                   jax.ShapeDtypeStruct((N_MEM, MEM_HEADS * SLOT), F32)],
        compiler_params=_cparams(),
    )(proj, kvmem, kvmem, do)


def _place():
    return lax.axis_index("x"), lax.axis_index("y"), lax.axis_index("c")


def _flip(v, bit):
    return 1 - v if bit else v


def _all_gather(blocks, name):
    nb = len(blocks)

    def body(*refs):
        x_refs, out_refs = refs[:nb], refs[nb:2 * nb]
        send_sems, recv_sems, local_sems = refs[2 * nb:]
        x, y, c = _place()
        me, sibling = (x, y, c), (x, y, 1 - c)
        chips = [(1 - x, y), (x, 1 - y), (1 - x, 1 - y)]

        def copy(b, k, blk, to, from_input=False):
            slot = out_refs[b].at[4 * blk[0] + 2 * blk[1] + blk[2]]
            return pltpu.make_async_remote_copy(
                src_ref=x_refs[b] if from_input else slot, dst_ref=slot,
                send_sem=send_sems.at[b, k], recv_sem=recv_sems.at[b, k],
                device_id=to, device_id_type=pl.DeviceIdType.MESH)

        mine = [pltpu.make_async_copy(x_refs[b], out_refs[b].at[4 * x + 2 * y + c], local_sems.at[b])
                for b in range(nb)]
        for cp in mine:
            cp.start()
        first = []
        for b in range(nb):
            first.append(copy(b, 0, me, sibling, from_input=True))
            first += [copy(b, 1 + n, me, (*chip, c), from_input=True) for n, chip in enumerate(chips)]
        for cp in first:
            cp.start()
        passed = []
        for n, chip in enumerate(chips):
            for b in range(nb):
                copy(b, 1 + n, (*chip, c), me).wait_recv()
                passed.append(copy(b, 4 + n, (*chip, c), sibling))
                passed[-1].start()
        for b in range(nb):
            copy(b, 0, sibling, me).wait_recv()
            for n, chip in enumerate(chips):
                copy(b, 4 + n, (*chip, 1 - c), me).wait_recv()
        for cp in first + passed:
            cp.wait_send()
        for cp in mine:
            cp.wait()

    any_spec = pl.BlockSpec(memory_space=pl.ANY)
    return pl.pallas_call(
        body, name=name, in_specs=[any_spec] * nb, out_specs=[any_spec] * nb,
        out_shape=[jax.ShapeDtypeStruct((N_DEV,) + blk.shape, blk.dtype) for blk in blocks],
        scratch_shapes=[pltpu.SemaphoreType.DMA((nb, 7)), pltpu.SemaphoreType.DMA((nb, 7)),
                        pltpu.SemaphoreType.DMA((nb,))],
    )(*blocks)


def _peers(x, y, c):
    out = []
    for n in range(1, N_DEV):
        peer = (_flip(x, n & 4), _flip(y, n & 2), _flip(c, n & 1))
        out.append((n - 1, peer, 4 * peer[0] + 2 * peer[1] + peer[2]))
    return out


_HBM = pl.BlockSpec(memory_space=pltpu.HBM)
_SEM = pl.BlockSpec(memory_space=pltpu.SEMAPHORE)


def _exchange_start(srcs, scatter, name, after=None):
    ns = len(srcs)
    lands = [lax.empty(s.shape if scatter else (N_DEV,) + s.shape, s.dtype) for s in srcs]

    def body(*refs):
        src_refs, land_refs = refs[:ns], refs[ns:2 * ns]
        pos = 2 * ns + (1 if after is not None else 0)
        send_sems, recv_sems, token = refs[pos], refs[pos + 1], refs[-1]
        x, y, c = _place()
        my_idx = 4 * x + 2 * y + c
        for col, peer, peer_idx in _peers(x, y, c):
            for b in range(ns):
                pltpu.make_async_remote_copy(
                    src_ref=src_refs[b].at[peer_idx] if scatter else src_refs[b], dst_ref=land_refs[b].at[my_idx],
                    send_sem=send_sems.at[b * (N_DEV - 1) + col], recv_sem=recv_sems.at[b * (N_DEV - 1) + col],
                    device_id=peer, device_id_type=pl.DeviceIdType.MESH).start()
        token[...] = jnp.zeros_like(token)

    args = [pltpu.with_memory_space_constraint(a, pltpu.HBM) for a in list(srcs) + lands]
    in_specs = [_HBM] * (2 * ns)
    if after is not None:
        args.append(after)
        in_specs.append(pl.BlockSpec(memory_space=pl.ANY))
    out = pl.pallas_call(
        body, name=name, in_specs=in_specs,
        out_specs=[_SEM, _SEM] + [_HBM] * (2 * ns) + [pl.BlockSpec(memory_space=pltpu.VMEM)],
        out_shape=[pltpu.SemaphoreType.DMA((ns * (N_DEV - 1),)), pltpu.SemaphoreType.DMA((ns * (N_DEV - 1),))]
        + [pltpu.HBM(a.shape, a.dtype) for a in list(srcs) + lands] + [jax.ShapeDtypeStruct((8, SLOT), F32)],
        input_output_aliases={k: 2 + k for k in range(2 * ns)},
        compiler_params=pltpu.CompilerParams(has_side_effects=pltpu.SideEffectType.DATAFLOW_SIDE_EFFECTING),
    )(*args)
    return (out[0], out[1], out[2:2 + ns], out[2 + ns:2 + 2 * ns], scatter), out[-1]


def _exchange_wait(handle, after, name):
    send_sems, recv_sems, srcs, lands, scatter = handle
    ns = len(srcs)

    def body(*refs):
        src_refs, land_refs = refs[:ns], refs[ns:2 * ns]
        send_ref, recv_ref = refs[2 * ns], refs[2 * ns + 1]
        x, y, c = _place()
        for col, peer, peer_idx in _peers(x, y, c):
            for b in range(ns):
                copy = pltpu.make_async_remote_copy(
                    src_ref=src_refs[b].at[peer_idx] if scatter else src_refs[b], dst_ref=land_refs[b].at[peer_idx],
                    send_sem=send_ref.at[b * (N_DEV - 1) + col], recv_sem=recv_ref.at[b * (N_DEV - 1) + col],
                    device_id=peer, device_id_type=pl.DeviceIdType.MESH)
                copy.wait_send()
                copy.wait_recv()

    out = pl.pallas_call(
        body, name=name, in_specs=[_HBM] * (2 * ns) + [_SEM, _SEM, pl.BlockSpec(memory_space=pl.ANY)],
        out_specs=[_HBM] * (2 * ns),
        out_shape=[pltpu.HBM(a.shape, a.dtype) for a in list(srcs) + list(lands)],
        input_output_aliases={k: k for k in range(2 * ns)},
        compiler_params=pltpu.CompilerParams(has_side_effects=pltpu.SideEffectType.DATAFLOW_SIDE_EFFECTING),
    )(*srcs, *lands, send_sems, recv_sems, after)
    my_idx = 4 * lax.axis_index("x") + 2 * lax.axis_index("y") + lax.axis_index("c")
    landed = []
    for src, land in zip(out[:ns], out[ns:]):
        own = lax.dynamic_index_in_dim(src, my_idx, 0, keepdims=True) if scatter else src[None]
        landed.append(lax.dynamic_update_index_in_dim(land, own, my_idx, 0))
    return landed


def _adamw(parts, w, m, v, name):
    lyr, rows, cols = w.shape
    assert len(parts) == lyr
    tr = ADAM_ROWS if cols > 512 else 2 * ADAM_ROWS
    while rows % tr:
        tr //= 2
    tr = min(tr, rows)

    def body(*refs):
        p_refs = refs[:lyr]
        w_ref, m_ref, v_ref, g_out, d_out, m_out, v_out = refs[lyr:]
        for k in range(lyr):
            @pl.when(pl.program_id(0) == k)
            def _(p_ref=p_refs[k]):
                g = p_ref[0].astype(F32)
                for s in range(1, N_DEV):
                    g = g + p_ref[s].astype(F32)
                m2 = ADAM_B1 * m_ref[...] + (1.0 - ADAM_B1) * g
                v2 = ADAM_B2 * v_ref[...] + (1.0 - ADAM_B2) * (g * g)
                m_hat = m2 / (1.0 - ADAM_B1 ** ADAM_STEP)
                v_hat = v2 / (1.0 - ADAM_B2 ** ADAM_STEP)
                g_out[...] = g
                d_out[...] = -ADAM_LR * (m_hat / (jnp.sqrt(v_hat) + ADAM_EPS) + ADAM_WD * w_ref[...])
                m_out[...] = m2
                v_out[...] = v2

    def part_spec(k):
        return pl.BlockSpec((N_DEV, tr, cols), lambda l, i: (0, jnp.where(l == k, i, 0), 0))

    spec = pl.BlockSpec((None, tr, cols), lambda l, i: (l, i, 0))
    shp = jax.ShapeDtypeStruct((lyr, rows, cols), F32)
    return pl.pallas_call(
        body, name=name, grid=(lyr, rows // tr),
        in_specs=[part_spec(k) for k in range(lyr)] + [spec, spec, spec],
        out_specs=[spec] * 4, out_shape=[shp] * 4, compiler_params=_cparams(),
    )(*parts, w, m, v)


def _pack(arrays, lanes, row_mult, dtype):
    flat = jnp.concatenate([a.reshape(-1).astype(dtype) for a in arrays])
    unit = lanes * row_mult
    total = -(-flat.shape[0] // unit) * unit
    return jnp.pad(flat, (0, total - flat.shape[0])).reshape(total // lanes, lanes)


def _unpack(packed, shapes):
    flat = packed.reshape(-1)
    out, off = [], 0
    for shp in shapes:
        n = 1
        for d in shp:
            n *= d
        out.append(flat[off:off + n].reshape(shp))
        off += n
    return out


def _pad_slots(w, axis):
    axis = axis % w.ndim
    n = w.shape[axis] // HEAD_DIM
    shp = w.shape[:axis] + (n, HEAD_DIM) + w.shape[axis + 1:]
    pad = [(0, 0)] * (w.ndim + 1)
    pad[axis + 1] = (0, SLOT - HEAD_DIM)
    return jnp.pad(w.reshape(shp), pad).reshape(w.shape[:axis] + (n * SLOT,) + w.shape[axis + 1:])


def _unpad_slots(w, axis, keep=HEAD_DIM):
    axis = axis % w.ndim
    n = w.shape[axis] // SLOT
    shp = w.shape[:axis] + (n, SLOT) + w.shape[axis + 1:]
    idx = [slice(None)] * (w.ndim + 1)
    idx[axis + 1] = slice(0, keep)
    return w.reshape(shp)[tuple(idx)].reshape(w.shape[:axis] + (n * keep,) + w.shape[axis + 1:])


def _mla_in_pad(w):
    z = functools.partial(jnp.zeros, dtype=w.dtype)
    rows = w.shape[0]
    return jnp.concatenate([w[:, :384], z((rows, 64)), w[:, 640:672], z((rows, 32)), w[:, 384:640],
                            _pad_slots(w[:, 672:], 1)], axis=1)


def _mla_in_unpad(d):
    return jnp.concatenate([d[:, :384], d[:, 512:768], d[:, 448:480], _unpad_slots(d[:, 768:], 1)], axis=1)


def _mla_uq_pad(w):
    return jnp.pad(w.reshape(w.shape[0], MLA_HEADS, MLA_QK), ((0, 0), (0, 0), (0, SLOT - MLA_QK))).reshape(
        w.shape[0], MLA_HEADS * SLOT)


def _join(gathered, axis):
    nd, a, b = gathered.shape
    if axis == 1:
        return gathered.reshape(nd * a, b)
    return gathered.transpose(1, 0, 2).reshape(a, nd * b)


def _split(full, axis):
    r, c = full.shape
    if axis == 1:
        return full.reshape(N_DEV, r // N_DEV, c).astype(BF16)
    return full.reshape(r, N_DEV, c // N_DEV).transpose(1, 0, 2).astype(BF16)


def kernel(x, mem, positions, attn_norm_g, mlp_norm_g, mem_norm_g, final_norm_g, mla_w_in, mla_q_norm_g, mla_kv_norm_g, mla_w_uq, mla_w_ukv, swa_w_in, swa_sinks, w_mem_kv, w_o, mlp_w_up, mlp_w_down, loss_target, m_attn_norm_g, m_mlp_norm_g, m_mem_norm_g, m_final_norm_g, m_mla_w_in, m_mla_q_norm_g, m_mla_kv_norm_g, m_mla_w_uq, m_mla_w_ukv, m_swa_w_in, m_swa_sinks, m_w_mem_kv, m_w_o, m_mlp_w_up, m_mlp_w_down, v_attn_norm_g, v_mlp_norm_g, v_mem_norm_g, v_final_norm_g, v_mla_w_in, v_mla_q_norm_g, v_mla_kv_norm_g, v_mla_w_uq, v_mla_w_ukv, v_swa_w_in, v_swa_sinks, v_w_mem_kv, v_w_o, v_mlp_w_up, v_mlp_w_down):
    given = dict(locals())
    seq = x.shape[1]
    x0 = x.reshape(seq, D_MODEL)
    tgt = loss_target.reshape(seq, D_MODEL)
    mem0 = mem.reshape(N_MEM, D_MODEL)
    pos = positions.reshape(seq).astype(F32)
    pos_col, pos_row = pos.reshape(seq, 1), pos.reshape(1, seq)

    def layer_names(i):
        mixer = ("mla_w_in", "mla_w_uq", "mla_w_ukv") if i % 2 == 0 else ("swa_w_in",)
        return [(n, i // 2) for n in mixer] + [(n, i) for n in ("w_mem_kv", "w_o", "mlp_w_up", "mlp_w_down")]

    def local_weights(names):
        return [given[n][l].astype(BF16) for n, l in names]

    first_attn, first_mlp = layer_names(0)[:-2], layer_names(0)[-2:]
    weights = [dict(zip([n for n, _ in first_attn], _all_gather(local_weights(first_attn), "gather_weights_first")))]
    coming_mlp, first_token = _exchange_start(local_weights(first_mlp), False, "gather_weights_start_0",
                                              after=weights[0]["w_o"])

    consts = _lane_consts()
    tabs = _rope_tables(pos_col, consts)
    slopes = 2.0 ** (-8.0 * (jnp.arange(SWA_HEADS, dtype=F32) + 1.0) / SWA_HEADS)

    mem_n = _rmsnorm_fwd(mem0, 0, D_MODEL, mem_norm_g, "rmsnorm_fwd_mem")

    saved = []
    xc = x0
    for i in range(DEPTH):
        j = i // 2
        wts = weights[i]
        s = {"x_in": xc}
        token = None
        if i + 1 < DEPTH:
            coming, token = _exchange_start(local_weights(layer_names(i + 1)), False,
                                            "gather_weights_start_%d" % (i + 1),
                                            after=first_token if i == 0 else wts["w_o"])
        if i == 0:
            hn = _rmsnorm_fwd(xc, 0, D_MODEL, attn_norm_g[i], "rmsnorm_fwd")
        if i % 2 == 0:
            w_in = _mla_in_pad(_join(wts["mla_w_in"], 1))
            w_uq = _mla_uq_pad(_join(wts["mla_w_uq"], 2))
            w_kv = _join(wts["mla_w_ukv"], 2)
            proj = _mm(hn, w_in, "nn", F32, "mm_mla_in", after=token)
            cqn = _rmsnorm_fwd(proj, 0, MLA_Q_RANK, mla_q_norm_g[j], "rmsnorm_fwd_q")
            ckvn = _rmsnorm_fwd(proj, 2, MLA_KV_RANK, mla_kv_norm_g[j], "rmsnorm_fwd_kv")
            qraw = _mm(cqn, w_uq, "nn", F32, "mm_mla_uq")
            kvraw = _mm(ckvn, w_kv, "nn", F32, "mm_mla_ukv")
            q, k, v = _mla_rope_fwd(qraw, kvraw, proj, tabs)
            o, lse = _mla_attn_fwd(q, k, v)
            qoff = MLA_QOFF
            s.update(w_uq=w_uq, w_kv=w_kv, cqn=cqn, ckvn=ckvn, q=q, k=k, v=v)
        else:
            w_in = _join(wts["swa_w_in"], 2)
            proj = _mm(hn, w_in, "nn", BF16, "mm_swa_in", pairs="o", after=token)
            o, lse = _swa_attn_fwd(proj, pos_col, pos_row, slopes, swa_sinks[j])
            qoff = SWA_QOFF
        w_mem = _pad_slots(_join(wts["w_mem_kv"], 1), 1)
        w_out = _join(wts["w_o"], 1)
        w_o_mix, w_o_cross = w_out[:SWA_HEADS * HEAD_DIM], w_out[SWA_HEADS * HEAD_DIM:]
        kvmem = _mm(mem_n, w_mem, "nn", BF16, "mm_mem_kv")
        cross = _cross_attn_fwd(proj, qoff, kvmem)
        x1, hn2 = _mm(o, w_o_mix, "nn", F32, "mm_o", res=xc, pairs="a", second=(cross, w_o_cross),
                      epi="normfwd", norm=mlp_norm_g[i])
        if i == 0:
            wts.update(zip([n for n, _ in first_mlp], _exchange_wait(coming_mlp, hn2, "gather_weights_wait_0")))
        act, act2 = _mm(hn2, wts["mlp_w_up"], "nn", BF16, "mm_mlp_up", epi="relu2", b_blk="cols")
        if i + 1 < DEPTH:
            xc, hn_next = _mm(act2, wts["mlp_w_down"], "nn", F32, "mm_mlp_down", res=x1, b_blk="rows",
                              epi="normfwd", norm=attn_norm_g[i + 1])
        else:
            xc = _mm(act2, wts["mlp_w_down"], "nn", F32, "mm_mlp_down", res=x1, b_blk="rows")
        s.update(hn=hn, w_in=w_in, proj=proj, o=o, lse=lse, qoff=qoff, w_mem=w_mem, w_out=w_out,
                 kvmem=kvmem, cross=cross, x1=x1, hn2=hn2, act=act, act2=act2)
        saved.append(s)
        if i + 1 < DEPTH:
            hn = hn_next
            got = _exchange_wait(coming, xc, "gather_weights_wait_%d" % (i + 1))
            weights.append(dict(zip([n for n, _ in layer_names(i + 1)], got)))

    dx, dx_b, dg_final, loss_part = _loss_head(xc, final_norm_g, tgt)
    loss = lax.psum(loss_part[0, 0], MESH_AXES)

    gains = {n: [None] * DEPTH for n in ("attn_norm_g", "mlp_norm_g")}
    for n in ("mla_q_norm_g", "mla_kv_norm_g", "swa_sinks"):
        gains[n] = [None] * 2
    leaving = {}
    token = None
    dmem_n = None
    for i in reversed(range(DEPTH)):
        j = i // 2
        s = saved[i]
        wts = weights[i]
        out = {}
        du = _mm(dx_b, wts["mlp_w_down"], "nt", BF16, "mm_mlp_down_dx", aux=s["act"], epi="mul2aux", b_blk="rows",
                 after=token)
        out["mlp_w_down"] = _mm(s["act2"], dx_b, "tn", BF16, "mm_mlp_down_dw", o_blk="rows")
        out["mlp_w_up"] = _mm(s["hn2"], du, "tn", BF16, "mm_mlp_up_dw", o_blk="cols")
        dx1, dx1_b, dg = _mm(du, wts["mlp_w_up"], "nt", F32, "mm_mlp_up_dx", b_blk="cols",
                             epi="normbwd", norm=(s["x1"], mlp_norm_g[i], dx))
        gains["mlp_norm_g"][i] = dg[0]

        do = _mm(dx1_b, s["w_out"], "nt", BF16, "mm_o_dx", pairs="o")
        dw_o = jnp.concatenate([_mm(s["o"], dx1_b, "tn", F32, "mm_o_mix_dw", pairs="a"),
                                _mm(s["cross"], dx1_b, "tn", F32, "mm_o_cross_dw", pairs="a")], axis=0)
        out["w_o"] = _split(dw_o, 1)
        dqc, dkm, dvm = _cross_attn_bwd(s["proj"], s["qoff"], s["kvmem"], do, SWA_HEADS)
        dkvmem = jnp.concatenate([dkm, dvm], axis=1).astype(BF16)
        out["w_mem_kv"] = _split(_unpad_slots(_mm(mem_n, dkvmem, "tn", F32, "mm_mem_kv_dw"), 1), 1)
        dmem_n = _mm(dkvmem, s["w_mem"], "nt", F32, "mm_mem_kv_dx" if dmem_n is None else "mm_mem_kv_dx_acc",
                     res=dmem_n)
        leaving[(i, "main")], token = _exchange_start([out[n] for n, _ in layer_names(i)[-4:]], True,
                                                      "exchange_grads_main_start_%d" % i)

        if i % 2 == 0:
            dq, dk, dv = _mla_attn_bwd(s["q"], s["k"], s["v"], do, s["lse"], _mla_delta(s["o"], do), token)
            dqraw, dkv, dkr = _mla_rope_bwd(dq, dk, dv, tabs, consts)
            dcqn = _mm(dqraw, s["w_uq"], "nt", F32, "mm_mla_uq_dx")
            out["mla_w_uq"] = _split(_unpad_slots(_mm(s["cqn"], dqraw, "tn", F32, "mm_mla_uq_dw"), 1, MLA_QK), 2)
            dckvn = _mm(dkv, s["w_kv"], "nt", F32, "mm_mla_ukv_dx")
            out["mla_w_ukv"] = _split(_mm(s["ckvn"], dkv, "tn", F32, "mm_mla_ukv_dw"), 2)
            dcq, dg = _rmsnorm_bwd(s["proj"], 0, MLA_Q_RANK, mla_q_norm_g[j], dcqn, None, BF16, "rmsnorm_bwd_q")
            gains["mla_q_norm_g"][j] = dg[0]
            dckv, dg = _rmsnorm_bwd(s["proj"], 2, MLA_KV_RANK, mla_kv_norm_g[j], dckvn, None, BF16, "rmsnorm_bwd_kv")
            gains["mla_kv_norm_g"][j] = dg[0]
            dproj = jnp.concatenate([dcq, dkr.astype(BF16), dckv, dqc.astype(BF16)], axis=1)
            in_dx = "mm_mla_in_dx"
            out["mla_w_in"] = _split(_mla_in_unpad(_mm(s["hn"], dproj, "tn", F32, "mm_mla_in_dw")), 1)
        else:
            dq, dk, dv, dsink = _swa_attn_bwd(s["proj"], s["o"], do, s["lse"], pos_col, pos_row, slopes, swa_sinks[j],
                                              token)
            gains["swa_sinks"][j] = dsink[::8, 0]
            dproj = jnp.concatenate([dq, dk, dv, dqc], axis=1).astype(BF16)
            in_dx = "mm_swa_in_dx"
            out["swa_w_in"] = _split(_mm(s["hn"], dproj, "tn", F32, "mm_swa_in_dw", pairs="b"), 2)
        dx, dx_b, dg = _mm(dproj, s["w_in"], "nt", F32, in_dx, epi="normbwd", norm=(s["x_in"], attn_norm_g[i], dx1),
                           pairs="" if i % 2 == 0 else "a")
        gains["attn_norm_g"][i] = dg[0]

        leaving[(i, "mixer")], token = _exchange_start([out[n] for n, _ in layer_names(i)[:-4]], True,
                                                       "exchange_grads_mixer_start_%d" % i)

    _, dg_mem = _rmsnorm_bwd(mem0, 0, D_MODEL, mem_norm_g, dmem_n, None, BF16, "rmsnorm_bwd_mem")
    gains = {n: jnp.stack(g) for n, g in gains.items()}
    gains["mem_norm_g"] = dg_mem[0]
    gains["final_norm_g"] = dg_final[0]

    result = {}

    def adamw_of(names, received):
        for n in names:
            parts = [received[(n, l)] for l in range(given[n].shape[0])]
            for kind, r in enumerate(_adamw(parts, given[n], given["m_" + n], given["v_" + n], "adamw_" + n)):
                result[(kind, n)] = r

    received = {}
    for i in reversed(range(DEPTH)):
        got = _exchange_wait(leaving[(i, "main")], dx, "exchange_grads_main_wait_%d" % i)
        received.update(zip(layer_names(i)[-4:], got))
    adamw_of(("mlp_w_up", "mlp_w_down", "w_o", "w_mem_kv"), received)
    for i in reversed(range(DEPTH)):
        got = _exchange_wait(leaving[(i, "mixer")], result[(0, "w_mem_kv")], "exchange_grads_mixer_wait_%d" % i)
        received.update(zip(layer_names(i)[:-4], got))
    adamw_of(("mla_w_in", "mla_w_uq", "mla_w_ukv", "swa_w_in"), received)

    rep_shapes = [given[n].shape for n in REPLICATED]
    rep_parts = _all_gather([_pack([gains[n] for n in REPLICATED], SLOT, 8, F32)], "gather_gain_grads")[0]
    rep_packed = [_pack([given[p + n] for n in REPLICATED], SLOT, 8, F32)[None] for p in ("", "m_", "v_")]
    for kind, r in enumerate(_adamw([rep_parts], *rep_packed, "adamw_gains")):
        for n, part in zip(REPLICATED, _unpack(r[0], rep_shapes)):
            result[(kind, n)] = part

    outs = [loss, dx.reshape(1, seq, D_MODEL)]
    for kind in range(4):
        outs += [result[(kind, n)] for n in WEIGHT_ORDER]
    return tuple(outs)
```
